```python
import jax, jax.numpy as jnp
from jax import lax
import numpy as np

D_MODEL = 1024
BATCH = 8
SEQ = 2048
DEPTH = 2

CHUNK = 64
HEAD_DIM = 64
N_HEADS_SB = 4
N_HEADS_CH = 8
N_HEADS_FOX = 4
W_SB = N_HEADS_SB * HEAD_DIM
W_CH = N_HEADS_CH * HEAD_DIM
W_FOX = N_HEADS_FOX * HEAD_DIM
LEFT_CHUNKS = 8
BAND = (LEFT_CHUNKS + 1) * CHUNK
MAX_REL = 128
N_REL = 2 * MAX_REL + 1
Q_BLOCK = 128
N_BRANCH = 3
D_FF = ((8 * D_MODEL // 3 + 127) // 128) * 128
QKV_WIDTH = 3 * (W_SB + W_CH + W_FOX)
FORGET_OFFSET = QKV_WIDTH
IN_WIDTH = QKV_WIDTH + N_HEADS_FOX + N_BRANCH * D_MODEL
SPLIT_SIZES = (W_SB, W_SB, W_SB, W_CH, W_CH, W_CH, W_FOX, W_FOX, W_FOX, N_HEADS_FOX, D_MODEL, D_MODEL, D_MODEL)
SPLIT_POINTS = tuple(int(v) for v in np.cumsum(SPLIT_SIZES)[:-1])
FORGET_BIAS_INIT = 4.0
RMS_EPS = 1e-6
NEG = -1e30

kernel_name = "hybrid_stickbreak_chunkrel_fox_macaron"


def rmsnorm(x, g):
    xf = x.astype(jnp.float32)
    y = xf * lax.rsqrt(jnp.mean(xf * xf, axis=-1, keepdims=True) + RMS_EPS)
    return (y * g.astype(jnp.float32)).astype(x.dtype)


def swiglu(h, w_in, w_out):
    gate, up = jnp.split(h @ w_in, 2, axis=-1)
    return (jax.nn.silu(gate) * up) @ w_out


def split_heads(t, n_heads):
    b, s, _ = t.shape
    return t.reshape(b, s, n_heads, HEAD_DIM).transpose(0, 2, 1, 3)


def merge_heads(t):
    b, h, s, d = t.shape
    return t.transpose(0, 2, 1, 3).reshape(b, s, h * d)


def stick_breaking_attention(q, k, v):
    T = q.shape[2]
    scale = HEAD_DIM ** -0.5
    outs = []
    for start in range(0, T, Q_BLOCK):
        end = start + Q_BLOCK
        z = jnp.einsum('bhqd,bhkd->bhqk', q[:, :, start:end], k[:, :, :end]).astype(jnp.float32) * scale
        strict = jnp.arange(end)[None, :] < jnp.arange(start, end)[:, None]
        log_beta = jax.nn.log_sigmoid(z)
        log_fail = jnp.where(strict, jax.nn.log_sigmoid(-z), 0.0)
        between = lax.cumsum(log_fail, axis=3, reverse=True) - log_fail
        w = jnp.where(strict, jnp.exp(log_beta + between), 0.0)
        outs.append(jnp.einsum('bhqk,bhkd->bhqd', w.astype(v.dtype), v[:, :, :end]))
    return jnp.concatenate(outs, axis=2)


def forgetting_attention(q, k, v, log_f):
    T = q.shape[2]
    scale = HEAD_DIM ** -0.5
    F = jnp.cumsum(log_f, axis=-1)
    outs = []
    for start in range(0, T, Q_BLOCK):
        end = start + Q_BLOCK
        z = jnp.einsum('bhqd,bhkd->bhqk', q[:, :, start:end], k[:, :, :end]).astype(jnp.float32) * scale
        z = z + F[:, :, start:end, None] - F[:, :, None, :end]
        causal = jnp.arange(end)[None, :] <= jnp.arange(start, end)[:, None]
        p = jax.nn.softmax(jnp.where(causal, z, NEG), axis=-1)
        outs.append(jnp.einsum('bhqk,bhkd->bhqd', p.astype(v.dtype), v[:, :, :end]))
    return jnp.concatenate(outs, axis=2)


def chunked_relpos_attention(q, k, v, rel_table):
    B, H, T, Dh = q.shape
    nc = T // CHUNK
    scale = Dh ** -0.5
    qc = q.reshape(B, H, nc, CHUNK, Dh)
    pad = ((0, 0), (0, 0), (LEFT_CHUNKS * CHUNK, 0), (0, 0))
    kp = jnp.pad(k, pad).reshape(B, H, nc + LEFT_CHUNKS, CHUNK, Dh)
    vp = jnp.pad(v, pad).reshape(B, H, nc + LEFT_CHUNKS, CHUNK, Dh)
    band_idx = jnp.arange(nc)[:, None] + jnp.arange(LEFT_CHUNKS + 1)[None, :]
    k_band = kp[:, :, band_idx].reshape(B, H, nc, BAND, Dh)
    v_band = vp[:, :, band_idx].reshape(B, H, nc, BAND, Dh)
    z = jnp.einsum('bhcqd,bhckd->bhcqk', qc, k_band).astype(jnp.float32) * scale
    rel = (jnp.arange(CHUNK)[:, None] + LEFT_CHUNKS * CHUNK) - jnp.arange(BAND)[None, :]
    rel = jnp.clip(rel, -MAX_REL, MAX_REL) + MAX_REL
    bias = rel_table[rel].astype(jnp.float32).transpose(2, 0, 1)
    z = z + bias[None, :, None]
    key_abs = (jnp.arange(nc)[:, None] - LEFT_CHUNKS) * CHUNK + jnp.arange(BAND)[None, :]
    valid = key_abs >= 0
    p = jax.nn.softmax(jnp.where(valid[None, None, :, None, :], z, NEG), axis=-1)
    o = jnp.einsum('bhcqk,bhckd->bhcqd', p.astype(v.dtype), v_band)
    return o.reshape(B, H, T, Dh)


def hybrid_layer(x, g_ffn1, w_ffn1_in, w_ffn1_out, g_mix, w_in, b_in, rel_bias,
                 w_br_sb, w_br_ch, w_br_fox, w_out, g_ffn2, w_ffn2_in, w_ffn2_out):
    x = x + 0.5 * swiglu(rmsnorm(x, g_ffn1), w_ffn1_in, w_ffn1_out)
    h = rmsnorm(x, g_mix)
    proj = h @ w_in + b_in
    (q_a, k_a, v_a, q_b, k_b, v_b, q_c, k_c, v_c,
     f_logit, g_a, g_b, g_c) = jnp.split(proj, list(SPLIT_POINTS), axis=-1)
    o_a = stick_breaking_attention(split_heads(q_a, N_HEADS_SB), split_heads(k_a, N_HEADS_SB),
                                   split_heads(v_a, N_HEADS_SB))
    o_b = chunked_relpos_attention(split_heads(q_b, N_HEADS_CH), split_heads(k_b, N_HEADS_CH),
                                   split_heads(v_b, N_HEADS_CH), rel_bias)
    log_f = jax.nn.log_sigmoid(f_logit.astype(jnp.float32)).transpose(0, 2, 1)
    o_c = forgetting_attention(split_heads(q_c, N_HEADS_FOX), split_heads(k_c, N_HEADS_FOX),
                               split_heads(v_c, N_HEADS_FOX), log_f)
    merged = (jax.nn.sigmoid(g_a) * (merge_heads(o_a) @ w_br_sb)
              + jax.nn.sigmoid(g_b) * (merge_heads(o_b) @ w_br_ch)
              + jax.nn.sigmoid(g_c) * (merge_heads(o_c) @ w_br_fox))
    x = x + merged @ w_out
    x = x + 0.5 * swiglu(rmsnorm(x, g_ffn2), w_ffn2_in, w_ffn2_out)
    return x


def _fwd_setup_inputs(seed: int = 0) -> dict:
    key = jax.random.key(seed)
    ks = jax.random.split(key, 18)

    def dense(k, shape, fan_in):
        return jax.random.normal(k, shape, jnp.float32) * fan_in ** -0.5

    def gain(k, shape):
        return 1.0 + 0.05 * jax.random.normal(k, shape, jnp.float32)

    b_in = 0.02 * jax.random.normal(ks[6], (DEPTH, IN_WIDTH), jnp.float32)
    b_in = b_in.at[:, FORGET_OFFSET:FORGET_OFFSET + N_HEADS_FOX].add(FORGET_BIAS_INIT)
    return {
        "x": jax.random.normal(ks[0], (BATCH, SEQ, D_MODEL), jnp.float32),
        "g_ffn1": gain(ks[1], (DEPTH, D_MODEL)),
        "w_ffn1_in": dense(ks[2], (DEPTH, D_MODEL, 2 * D_FF), D_MODEL),
        "w_ffn1_out": dense(ks[3], (DEPTH, D_FF, D_MODEL), D_FF),
        "g_mix": gain(ks[4], (DEPTH, D_MODEL)),
        "w_in": dense(ks[5], (DEPTH, D_MODEL, IN_WIDTH), D_MODEL),
        "b_in": b_in,
        "rel_bias": 0.1 * jax.random.normal(ks[7], (DEPTH, N_REL, N_HEADS_CH), jnp.float32),
        "w_br_sb": dense(ks[8], (DEPTH, W_SB, D_MODEL), W_SB),
        "w_br_ch": dense(ks[9], (DEPTH, W_CH, D_MODEL), W_CH),
        "w_br_fox": dense(ks[10], (DEPTH, W_FOX, D_MODEL), W_FOX),
        "w_out": dense(ks[11], (DEPTH, D_MODEL, D_MODEL), D_MODEL),
        "g_ffn2": gain(ks[12], (DEPTH, D_MODEL)),
        "w_ffn2_in": dense(ks[13], (DEPTH, D_MODEL, 2 * D_FF), D_MODEL),
        "w_ffn2_out": dense(ks[14], (DEPTH, D_FF, D_MODEL), D_FF),
        "g_final": gain(ks[15], (D_MODEL,)),
    }


def _fwd_reference(x, g_ffn1, w_ffn1_in, w_ffn1_out, g_mix, w_in, b_in, rel_bias,
              w_br_sb, w_br_ch, w_br_fox, w_out, g_ffn2, w_ffn2_in, w_ffn2_out, g_final):
    for layer in range(DEPTH):
        x = hybrid_layer(x, g_ffn1[layer], w_ffn1_in[layer], w_ffn1_out[layer], g_mix[layer],
                         w_in[layer], b_in[layer], rel_bias[layer], w_br_sb[layer], w_br_ch[layer],
                         w_br_fox[layer], w_out[layer], g_ffn2[layer], w_ffn2_in[layer],
                         w_ffn2_out[layer])
    return rmsnorm(x, g_final)


import jax as _jax
import jax.numpy as _jnp

TWIN_FORMAT = 'train_step'
FWD_PARAMS = ['x', 'g_ffn1', 'w_ffn1_in', 'w_ffn1_out', 'g_mix', 'w_in', 'b_in', 'rel_bias', 'w_br_sb', 'w_br_ch', 'w_br_fox', 'w_out', 'g_ffn2', 'w_ffn2_in', 'w_ffn2_out', 'g_final']
TWIN_WEIGHTS = ['g_ffn1', 'w_ffn1_in', 'w_ffn1_out', 'g_mix', 'w_in', 'b_in', 'rel_bias', 'w_br_sb', 'w_br_ch', 'w_br_fox', 'w_out', 'g_ffn2', 'w_ffn2_in', 'w_ffn2_out', 'g_final']
TWIN_DIFF_INPUT = 'x'
TWIN_INPUTS = ['x', 'g_ffn1', 'w_ffn1_in', 'w_ffn1_out', 'g_mix', 'w_in', 'b_in', 'rel_bias', 'w_br_sb', 'w_br_ch', 'w_br_fox', 'w_out', 'g_ffn2', 'w_ffn2_in', 'w_ffn2_out', 'g_final', 'loss_target', 'm_g_ffn1', 'm_w_ffn1_in', 'm_w_ffn1_out', 'm_g_mix', 'm_w_in', 'm_b_in', 'm_rel_bias', 'm_w_br_sb', 'm_w_br_ch', 'm_w_br_fox', 'm_w_out', 'm_g_ffn2', 'm_w_ffn2_in', 'm_w_ffn2_out', 'm_g_final', 'v_g_ffn1', 'v_w_ffn1_in', 'v_w_ffn1_out', 'v_g_mix', 'v_w_in', 'v_b_in', 'v_rel_bias', 'v_w_br_sb', 'v_w_br_ch', 'v_w_br_fox', 'v_w_out', 'v_g_ffn2', 'v_w_ffn2_in', 'v_w_ffn2_out', 'v_g_final']
TWIN_OUTPUTS = ['loss', 'grad_x', 'grad_g_ffn1', 'grad_w_ffn1_in', 'grad_w_ffn1_out', 'grad_g_mix', 'grad_w_in', 'grad_b_in', 'grad_rel_bias', 'grad_w_br_sb', 'grad_w_br_ch', 'grad_w_br_fox', 'grad_w_out', 'grad_g_ffn2', 'grad_w_ffn2_in', 'grad_w_ffn2_out', 'grad_g_final', 'delta_g_ffn1', 'delta_w_ffn1_in', 'delta_w_ffn1_out', 'delta_g_mix', 'delta_w_in', 'delta_b_in', 'delta_rel_bias', 'delta_w_br_sb', 'delta_w_br_ch', 'delta_w_br_fox', 'delta_w_out', 'delta_g_ffn2', 'delta_w_ffn2_in', 'delta_w_ffn2_out', 'delta_g_final', 'new_m_g_ffn1', 'new_m_w_ffn1_in', 'new_m_w_ffn1_out', 'new_m_g_mix', 'new_m_w_in', 'new_m_b_in', 'new_m_rel_bias', 'new_m_w_br_sb', 'new_m_w_br_ch', 'new_m_w_br_fox', 'new_m_w_out', 'new_m_g_ffn2', 'new_m_w_ffn2_in', 'new_m_w_ffn2_out', 'new_m_g_final', 'new_v_g_ffn1', 'new_v_w_ffn1_in', 'new_v_w_ffn1_out', 'new_v_g_mix', 'new_v_w_in', 'new_v_b_in', 'new_v_rel_bias', 'new_v_w_br_sb', 'new_v_w_br_ch', 'new_v_w_br_fox', 'new_v_w_out', 'new_v_g_ffn2', 'new_v_w_ffn2_in', 'new_v_w_ffn2_out', 'new_v_g_final']
TWIN_LEAF_KINDS = {'loss': 'loss', 'grad_x': 'grad_x', 'grad_g_ffn1': 'grad_w', 'grad_w_ffn1_in': 'grad_w', 'grad_w_ffn1_out': 'grad_w', 'grad_g_mix': 'grad_w', 'grad_w_in': 'grad_w', 'grad_b_in': 'grad_w', 'grad_rel_bias': 'grad_w', 'grad_w_br_sb': 'grad_w', 'grad_w_br_ch': 'grad_w', 'grad_w_br_fox': 'grad_w', 'grad_w_out': 'grad_w', 'grad_g_ffn2': 'grad_w', 'grad_w_ffn2_in': 'grad_w', 'grad_w_ffn2_out': 'grad_w', 'grad_g_final': 'grad_w', 'delta_g_ffn1': 'delta_w', 'delta_w_ffn1_in': 'delta_w', 'delta_w_ffn1_out': 'delta_w', 'delta_g_mix': 'delta_w', 'delta_w_in': 'delta_w', 'delta_b_in': 'delta_w', 'delta_rel_bias': 'delta_w', 'delta_w_br_sb': 'delta_w', 'delta_w_br_ch': 'delta_w', 'delta_w_br_fox': 'delta_w', 'delta_w_out': 'delta_w', 'delta_g_ffn2': 'delta_w', 'delta_w_ffn2_in': 'delta_w', 'delta_w_ffn2_out': 'delta_w', 'delta_g_final': 'delta_w', 'new_m_g_ffn1': 'new_m', 'new_m_w_ffn1_in': 'new_m', 'new_m_w_ffn1_out': 'new_m', 'new_m_g_mix': 'new_m', 'new_m_w_in': 'new_m', 'new_m_b_in': 'new_m', 'new_m_rel_bias': 'new_m', 'new_m_w_br_sb': 'new_m', 'new_m_w_br_ch': 'new_m', 'new_m_w_br_fox': 'new_m', 'new_m_w_out': 'new_m', 'new_m_g_ffn2': 'new_m', 'new_m_w_ffn2_in': 'new_m', 'new_m_w_ffn2_out': 'new_m', 'new_m_g_final': 'new_m', 'new_v_g_ffn1': 'new_v', 'new_v_w_ffn1_in': 'new_v', 'new_v_w_ffn1_out': 'new_v', 'new_v_g_mix': 'new_v', 'new_v_w_in': 'new_v', 'new_v_b_in': 'new_v', 'new_v_rel_bias': 'new_v', 'new_v_w_br_sb': 'new_v', 'new_v_w_br_ch': 'new_v', 'new_v_w_br_fox': 'new_v', 'new_v_w_out': 'new_v', 'new_v_g_ffn2': 'new_v', 'new_v_w_ffn2_in': 'new_v', 'new_v_w_ffn2_out': 'new_v', 'new_v_g_final': 'new_v'}


def _forward(args):
    return _fwd_reference(*[args[k] for k in FWD_PARAMS])


def _output_shape():
    out = _jax.eval_shape(lambda: _forward(_fwd_setup_inputs(0)))
    return out.shape, out.dtype

N_MICROBATCH = 1
ADAM_LR = 0.001
ADAM_B1 = 0.9
ADAM_B2 = 0.999
ADAM_EPS = 1e-08
ADAM_WD = 0.01
ADAM_STEP = 10
PER_EXAMPLE_BATCH_AXIS = {'x': 0, 'loss_target': 0}
SHARED_INPUTS = []
_WEIGHT_DTYPES = {'g_ffn1': _jnp.float32, 'w_ffn1_in': _jnp.float32, 'w_ffn1_out': _jnp.float32, 'g_mix': _jnp.float32, 'w_in': _jnp.float32, 'b_in': _jnp.float32, 'rel_bias': _jnp.float32, 'w_br_sb': _jnp.float32, 'w_br_ch': _jnp.float32, 'w_br_fox': _jnp.float32, 'w_out': _jnp.float32, 'g_ffn2': _jnp.float32, 'w_ffn2_in': _jnp.float32, 'w_ffn2_out': _jnp.float32, 'g_final': _jnp.float32}
MOMENT_SCALE = {'g_ffn1': 5.642383e-02, 'w_ffn1_in': 2.363370e-02, 'w_ffn1_out': 3.852799e-02, 'g_mix': 5.943317e-02, 'w_in': 2.439351e-02, 'b_in': 5.078516e-02, 'rel_bias': 6.199917e-03, 'w_br_sb': 4.168760e-02, 'w_br_ch': 1.015904e-02, 'w_br_fox': 1.653454e-02, 'w_out': 4.547053e-02, 'g_ffn2': 4.840139e-02, 'w_ffn2_in': 2.024611e-02, 'w_ffn2_out': 3.313750e-02, 'g_final': 1.599555e+01}


def _to_microbatches(a, axis):
    t = _jnp.moveaxis(a, axis, 0)
    t = t.reshape((N_MICROBATCH, t.shape[0] // N_MICROBATCH) + t.shape[1:])
    return _jnp.moveaxis(t, 1, axis + 1)


def setup_inputs(seed: int = 0) -> dict:
    inp = _fwd_setup_inputs(seed)
    key = _jax.random.fold_in(_jax.random.key(seed), 7919)
    shape, _ = _output_shape()
    out = dict(inp)
    out["loss_target"] = _jax.random.normal(_jax.random.fold_in(key, 0), shape, _jnp.float32)
    for i, name in enumerate(TWIN_WEIGHTS):
        w = inp[name].astype(_jnp.float32)
        if MOMENT_SCALE is None:
            s = _jnp.sqrt(_jnp.mean(_jnp.square(w)) + 1e-30)
        else:
            s = MOMENT_SCALE[name]
        km, kv = _jax.random.split(_jax.random.fold_in(key, i + 1))
        out[name] = w
        out["m_" + name] = s * _jax.random.normal(km, w.shape, _jnp.float32)
        out["v_" + name] = (s * s) * _jax.random.uniform(kv, w.shape, _jnp.float32, 0.5, 1.5)
    if N_MICROBATCH > 1:
        for name, axis in PER_EXAMPLE_BATCH_AXIS.items():
            out[name] = _to_microbatches(out[name], axis)
    return {'x': out['x'], 'g_ffn1': out['g_ffn1'], 'w_ffn1_in': out['w_ffn1_in'], 'w_ffn1_out': out['w_ffn1_out'], 'g_mix': out['g_mix'], 'w_in': out['w_in'], 'b_in': out['b_in'], 'rel_bias': out['rel_bias'], 'w_br_sb': out['w_br_sb'], 'w_br_ch': out['w_br_ch'], 'w_br_fox': out['w_br_fox'], 'w_out': out['w_out'], 'g_ffn2': out['g_ffn2'], 'w_ffn2_in': out['w_ffn2_in'], 'w_ffn2_out': out['w_ffn2_out'], 'g_final': out['g_final'], 'loss_target': out['loss_target'], 'm_g_ffn1': out['m_g_ffn1'], 'm_w_ffn1_in': out['m_w_ffn1_in'], 'm_w_ffn1_out': out['m_w_ffn1_out'], 'm_g_mix': out['m_g_mix'], 'm_w_in': out['m_w_in'], 'm_b_in': out['m_b_in'], 'm_rel_bias': out['m_rel_bias'], 'm_w_br_sb': out['m_w_br_sb'], 'm_w_br_ch': out['m_w_br_ch'], 'm_w_br_fox': out['m_w_br_fox'], 'm_w_out': out['m_w_out'], 'm_g_ffn2': out['m_g_ffn2'], 'm_w_ffn2_in': out['m_w_ffn2_in'], 'm_w_ffn2_out': out['m_w_ffn2_out'], 'm_g_final': out['m_g_final'], 'v_g_ffn1': out['v_g_ffn1'], 'v_w_ffn1_in': out['v_w_ffn1_in'], 'v_w_ffn1_out': out['v_w_ffn1_out'], 'v_g_mix': out['v_g_mix'], 'v_w_in': out['v_w_in'], 'v_b_in': out['v_b_in'], 'v_rel_bias': out['v_rel_bias'], 'v_w_br_sb': out['v_w_br_sb'], 'v_w_br_ch': out['v_w_br_ch'], 'v_w_br_fox': out['v_w_br_fox'], 'v_w_out': out['v_w_out'], 'v_g_ffn2': out['v_g_ffn2'], 'v_w_ffn2_in': out['v_w_ffn2_in'], 'v_w_ffn2_out': out['v_w_ffn2_out'], 'v_g_final': out['v_g_final']}


def _loss(weights, diff, rest, loss_target):
    with _jax.named_scope("forward"):
        args = {**rest, TWIN_DIFF_INPUT: diff, **{k: w.astype(_WEIGHT_DTYPES[k]) for k, w in weights.items()}}
        y = _forward(args)
    with _jax.named_scope("loss_head"):
        err = _jnp.square(y.astype(_jnp.float32) - loss_target)
        return 0.5 * _jnp.sum(_jnp.mean(err, axis=-1)) if err.ndim else 0.5 * err


def _adamw(w, g, m, v):
    m = ADAM_B1 * m + (1.0 - ADAM_B1) * g
    v = ADAM_B2 * v + (1.0 - ADAM_B2) * _jnp.square(g)
    m_hat = m / (1.0 - ADAM_B1 ** ADAM_STEP)
    v_hat = v / (1.0 - ADAM_B2 ** ADAM_STEP)
    delta = -ADAM_LR * (m_hat / (_jnp.sqrt(v_hat) + ADAM_EPS) + ADAM_WD * w)
    return delta, m, v


def reference(x, g_ffn1, w_ffn1_in, w_ffn1_out, g_mix, w_in, b_in, rel_bias, w_br_sb, w_br_ch, w_br_fox, w_out, g_ffn2, w_ffn2_in, w_ffn2_out, g_final, loss_target, m_g_ffn1, m_w_ffn1_in, m_w_ffn1_out, m_g_mix, m_w_in, m_b_in, m_rel_bias, m_w_br_sb, m_w_br_ch, m_w_br_fox, m_w_out, m_g_ffn2, m_w_ffn2_in, m_w_ffn2_out, m_g_final, v_g_ffn1, v_w_ffn1_in, v_w_ffn1_out, v_g_mix, v_w_in, v_b_in, v_rel_bias, v_w_br_sb, v_w_br_ch, v_w_br_fox, v_w_out, v_g_ffn2, v_w_ffn2_in, v_w_ffn2_out, v_g_final):
    given = dict(x=x, g_ffn1=g_ffn1, w_ffn1_in=w_ffn1_in, w_ffn1_out=w_ffn1_out, g_mix=g_mix, w_in=w_in, b_in=b_in, rel_bias=rel_bias, w_br_sb=w_br_sb, w_br_ch=w_br_ch, w_br_fox=w_br_fox, w_out=w_out, g_ffn2=g_ffn2, w_ffn2_in=w_ffn2_in, w_ffn2_out=w_ffn2_out, g_final=g_final, loss_target=loss_target, m_g_ffn1=m_g_ffn1, m_w_ffn1_in=m_w_ffn1_in, m_w_ffn1_out=m_w_ffn1_out, m_g_mix=m_g_mix, m_w_in=m_w_in, m_b_in=m_b_in, m_rel_bias=m_rel_bias, m_w_br_sb=m_w_br_sb, m_w_br_ch=m_w_br_ch, m_w_br_fox=m_w_br_fox, m_w_out=m_w_out, m_g_ffn2=m_g_ffn2, m_w_ffn2_in=m_w_ffn2_in, m_w_ffn2_out=m_w_ffn2_out, m_g_final=m_g_final, v_g_ffn1=v_g_ffn1, v_w_ffn1_in=v_w_ffn1_in, v_w_ffn1_out=v_w_ffn1_out, v_g_mix=v_g_mix, v_w_in=v_w_in, v_b_in=v_b_in, v_rel_bias=v_rel_bias, v_w_br_sb=v_w_br_sb, v_w_br_ch=v_w_br_ch, v_w_br_fox=v_w_br_fox, v_w_out=v_w_out, v_g_ffn2=v_g_ffn2, v_w_ffn2_in=v_w_ffn2_in, v_w_ffn2_out=v_w_ffn2_out, v_g_final=v_g_final)
    weights = {n: given[n] for n in TWIN_WEIGHTS}
    shared = {n: given[n] for n in SHARED_INPUTS}
    per_example = {n: given[n] for n in ['x']}
    grad_fn = _jax.value_and_grad(_loss, argnums=(0, 1))

    def one_microbatch(ex, loss_target):
        ex = dict(ex)
        diff = ex.pop(TWIN_DIFF_INPUT)
        return grad_fn(weights, diff, {**shared, **ex}, loss_target)

    if N_MICROBATCH == 1:
        loss, (grad_w, grad_x) = one_microbatch(per_example, given["loss_target"])
    else:
        def body(carry, xs):
            loss_sum, grad_sum = carry
            l_k, (gw_k, gx_k) = one_microbatch(xs[0], xs[1])
            with _jax.named_scope("update"):
                return (loss_sum + l_k, _jax.tree.map(_jnp.add, grad_sum, gw_k)), gx_k

        init = (_jnp.zeros((), _jnp.float32), _jax.tree.map(_jnp.zeros_like, weights))
        (loss, grad_w), grad_x = _jax.lax.scan(body, init, (per_example, given["loss_target"]))
    with _jax.named_scope("update"):
        delta_w, new_m, new_v = {}, {}, {}
        for n in TWIN_WEIGHTS:
            delta_w[n], new_m[n], new_v[n] = _adamw(weights[n], grad_w[n], given["m_" + n], given["v_" + n])
    return (loss, grad_x, *[grad_w[n] for n in TWIN_WEIGHTS], *[delta_w[n] for n in TWIN_WEIGHTS],
            *[new_m[n] for n in TWIN_WEIGHTS], *[new_v[n] for n in TWIN_WEIGHTS])
```

```python
import functools

import numpy as np
import jax
import jax.numpy as jnp
from jax import lax
from jax.experimental import pallas as pl
from jax.experimental.pallas import tpu as pltpu

F32 = jnp.float32
BF16 = jnp.bfloat16

N_DEV = 8
D_MODEL = 1024
DEPTH = 2
HEAD_DIM = 64
W_SB, W_CH, W_FOX = 256, 512, 256
QKV_WIDTH = 3 * (W_SB + W_CH + W_FOX)
N_HEADS_FOX = 4
N_HEADS_CH = 8
D_FF = 2816
FF_BLK = 2 * D_FF // N_DEV
CHUNK = 64
LEFT_CHUNKS = 8
MAX_REL = 128
N_REL = 2 * MAX_REL + 1
REL_PAD = 384
QB = 128
CH_WIN = 5
CH_KEYS = CH_WIN * QB
RMS_EPS = 1e-6
NEG = -1e30
SCALE = HEAD_DIM ** -0.5
LANES = 128
VMEM_LIMIT = 56 * 1024 * 1024

ADAM_LR, ADAM_B1, ADAM_B2, ADAM_EPS, ADAM_WD, ADAM_STEP = 0.001, 0.9, 0.999, 1e-08, 0.01, 10

SMALL_ROWS = 192

MESH = pl.DeviceIdType.MESH
ANY = pl.BlockSpec(memory_space=pl.ANY)
HIGHEST = lax.Precision.HIGHEST

NN = (((1,), (0,)), ((), ()))
NT = (((1,), (1,)), ((), ()))
TN = (((0,), (0,)), ((), ()))


def _cparams(n_grid):
    return pltpu.CompilerParams(dimension_semantics=("arbitrary",) * n_grid,
                                vmem_limit_bytes=VMEM_LIMIT)


def _sds(shape, dtype):
    return jax.ShapeDtypeStruct(tuple(shape), dtype)


def _my_index():
    return 4 * lax.axis_index("x") + 2 * lax.axis_index("y") + lax.axis_index("c")


def _peer(mask):
    x, y, c = lax.axis_index("x"), lax.axis_index("y"), lax.axis_index("c")
    px = x ^ ((mask >> 2) & 1)
    py = y ^ ((mask >> 1) & 1)
    pc = c ^ (mask & 1)
    return (px, py, pc), 4 * px + 2 * py + pc


def all_gather(shard, name):
    s, r, c = shard.shape

    def body(in_ref, out_ref, send_sems, recv_sems, local_sem):
        me = _my_index()
        mine = pltpu.make_async_copy(in_ref, out_ref.at[:, me], local_sem)
        mine.start()
        sends = []
        for mask in range(1, N_DEV):
            peer, _ = _peer(mask)
            cp = pltpu.make_async_remote_copy(
                src_ref=in_ref, dst_ref=out_ref.at[:, me],
                send_sem=send_sems.at[mask - 1], recv_sem=recv_sems.at[mask - 1],
                device_id=peer, device_id_type=MESH)
            cp.start()
            sends.append(cp)
        for mask in range(1, N_DEV):
            peer, pidx = _peer(mask)
            pltpu.make_async_remote_copy(
                src_ref=in_ref, dst_ref=out_ref.at[:, pidx],
                send_sem=send_sems.at[mask - 1], recv_sem=recv_sems.at[mask - 1],
                device_id=peer, device_id_type=MESH).wait_recv()
        for cp in sends:
            cp.wait_send()
        mine.wait()

    return pl.pallas_call(
        body, name=name,
        out_shape=_sds((s, N_DEV, r, c), shard.dtype),
        in_specs=[ANY], out_specs=ANY,
        scratch_shapes=[pltpu.SemaphoreType.DMA((N_DEV - 1,)),
                        pltpu.SemaphoreType.DMA((N_DEV - 1,)),
                        pltpu.SemaphoreType.DMA],
    )(shard)


def all_to_all(parts, name):
    s, _, r, c = parts.shape

    def body(in_ref, out_ref, send_sems, recv_sems, local_sem):
        me = _my_index()
        mine = pltpu.make_async_copy(in_ref.at[:, me], out_ref.at[me], local_sem)
        mine.start()
        sends = []
        for mask in range(1, N_DEV):
            peer, pidx = _peer(mask)
            cp = pltpu.make_async_remote_copy(
                src_ref=in_ref.at[:, pidx], dst_ref=out_ref.at[me],
                send_sem=send_sems.at[mask - 1], recv_sem=recv_sems.at[mask - 1],
                device_id=peer, device_id_type=MESH)
            cp.start()
            sends.append(cp)
        for mask in range(1, N_DEV):
            peer, pidx = _peer(mask)
            pltpu.make_async_remote_copy(
                src_ref=in_ref.at[:, me], dst_ref=out_ref.at[pidx],
                send_sem=send_sems.at[mask - 1], recv_sem=recv_sems.at[mask - 1],
                device_id=peer, device_id_type=MESH).wait_recv()
        for cp in sends:
            cp.wait_send()
        mine.wait()

    return pl.pallas_call(
        body, name=name,
        out_shape=_sds((N_DEV, s, r, c), parts.dtype),
        in_specs=[ANY], out_specs=ANY,
        scratch_shapes=[pltpu.SemaphoreType.DMA((N_DEV - 1,)),
                        pltpu.SemaphoreType.DMA((N_DEV - 1,)),
                        pltpu.SemaphoreType.DMA],
    )(parts)


def all_reduce_small(packed, name):
    rows = packed.shape[0]

    def body(in_ref, out_ref, slots, send_sems, recv_sems):
        me = _my_index()
        sends = []
        for mask in range(1, N_DEV):
            peer, _ = _peer(mask)
            cp = pltpu.make_async_remote_copy(
                src_ref=in_ref, dst_ref=slots.at[me],
                send_sem=send_sems.at[mask - 1], recv_sem=recv_sems.at[mask - 1],
                device_id=peer, device_id_type=MESH)
            cp.start()
            sends.append(cp)
        slots[me] = in_ref[...]
        for mask in range(1, N_DEV):
            peer, pidx = _peer(mask)
            pltpu.make_async_remote_copy(
                src_ref=in_ref, dst_ref=slots.at[pidx],
                send_sem=send_sems.at[mask - 1], recv_sem=recv_sems.at[mask - 1],
                device_id=peer, device_id_type=MESH).wait_recv()
        for cp in sends:
            cp.wait_send()
        total = slots[0]
        for p in range(1, N_DEV):
            total = total + slots[p]
        out_ref[...] = total

    return pl.pallas_call(
        body, name=name,
        out_shape=_sds((rows, LANES), F32),
        in_specs=[pl.BlockSpec(memory_space=pltpu.VMEM)],
        out_specs=pl.BlockSpec(memory_space=pltpu.VMEM),
        scratch_shapes=[pltpu.VMEM((N_DEV, rows, LANES), F32),
                        pltpu.SemaphoreType.DMA((N_DEV - 1,)),
                        pltpu.SemaphoreType.DMA((N_DEV - 1,))],
    )(packed)


def matmul(dims, a, b, out_sds, grid, a_spec, b_spec, o_spec, acc_shape, *, name, alpha=1.0,
           bias=None, bias_spec=None, res=None, res_spec=None, colsum_sds=None, colsum_spec=None):
    nk = grid[2]
    has_bias, has_res, has_cs = bias is not None, res is not None, colsum_sds is not None
    if has_cs:
        assert grid[0] == 1 and dims == TN

    def body(*refs):
        a_ref, b_ref = refs[0], refs[1]
        pos = 2
        bias_ref = res_ref = cs_ref = None
        if has_bias:
            bias_ref = refs[pos]; pos += 1
        if has_res:
            res_ref = refs[pos]; pos += 1
        o_ref = refs[pos]; pos += 1
        if has_cs:
            cs_ref = refs[pos]; pos += 1
        k = pl.program_id(2)
        bval = b_ref[...]
        part = lax.dot_general(a_ref[...].astype(BF16), bval.astype(BF16), dims,
                               preferred_element_type=F32)

        def finish(total):
            r = total * alpha if alpha != 1.0 else total
            if has_bias:
                r = r + bias_ref[...]
            if has_res:
                r = r + res_ref[...].astype(F32)
            o_ref[...] = r.astype(o_ref.dtype)

        if has_cs:
            csum = jnp.sum(bval.astype(F32), axis=0, keepdims=True)

            @pl.when(k == 0)
            def _():
                cs_ref[...] = csum

            @pl.when(k > 0)
            def _():
                cs_ref[...] += csum

        if nk == 1:
            finish(part)
        else:
            acc_ref = refs[pos]

            @pl.when(k == 0)
            def _():
                acc_ref[...] = part

            @pl.when(k > 0)
            def _():
                acc_ref[...] += part

            @pl.when(k == nk - 1)
            def _():
                finish(acc_ref[...])

    in_specs, args = [a_spec, b_spec], [a, b]
    if has_bias:
        in_specs.append(bias_spec); args.append(bias)
    if has_res:
        in_specs.append(res_spec); args.append(res)
    out_shape, out_specs = [out_sds], [o_spec]
    if has_cs:
        out_shape.append(colsum_sds); out_specs.append(colsum_spec)
    scratch = [] if nk == 1 else [pltpu.VMEM(acc_shape, F32)]
    outs = pl.pallas_call(
        body, name=name, grid=grid, in_specs=in_specs, out_specs=out_specs, out_shape=out_shape,
        scratch_shapes=scratch, compiler_params=_cparams(3))(*args)
    return outs if has_cs else outs[0]


def _sigmoid(z):
    return 1.0 / (1.0 + jnp.exp(-z))


def _log_sigmoid(z):
    return jnp.minimum(z, 0.0) - jnp.log(1.0 + jnp.exp(-jnp.abs(z)))


def rmsnorm_fwd(x, gain, tm, name):
    t, d = x.shape

    def body(x_ref, g_ref, o_ref):
        xf = x_ref[...]
        r = lax.rsqrt(jnp.mean(xf * xf, axis=-1, keepdims=True) + RMS_EPS)
        o_ref[...] = (xf * r * g_ref[...]).astype(o_ref.dtype)

    return pl.pallas_call(
        body, name=name, grid=(t // tm,),
        in_specs=[pl.BlockSpec((tm, d), lambda i: (i, 0)), pl.BlockSpec((1, d), lambda i: (0, 0))],
        out_specs=pl.BlockSpec((tm, d), lambda i: (i, 0)),
        out_shape=_sds((t, d), BF16), compiler_params=_cparams(1))(x, gain)


def rmsnorm_bwd(x, gain, dh, dres, tm, name):
    t, d = x.shape

    def body(x_ref, g_ref, dh_ref, dres_ref, dx_ref, dg_ref):
        i = pl.program_id(0)
        xf = x_ref[...]
        r = lax.rsqrt(jnp.mean(xf * xf, axis=-1, keepdims=True) + RMS_EPS)
        xhat = xf * r
        dh_v = dh_ref[...]
        dxhat = dh_v * g_ref[...]
        dx = r * (dxhat - xhat * jnp.mean(dxhat * xhat, axis=-1, keepdims=True))
        dx_ref[...] = dres_ref[...] + dx
        dg = jnp.sum(dh_v * xhat, axis=0, keepdims=True)

        @pl.when(i == 0)
        def _():
            dg_ref[...] = dg

        @pl.when(i > 0)
        def _():
            dg_ref[...] += dg

    row = pl.BlockSpec((tm, d), lambda i: (i, 0))
    vec = pl.BlockSpec((1, d), lambda i: (0, 0))
    return pl.pallas_call(
        body, name=name, grid=(t // tm,), in_specs=[row, vec, row, row], out_specs=[row, vec],
        out_shape=[_sds((t, d), F32), _sds((1, d), F32)], compiler_params=_cparams(1))(x, gain, dh, dres)


def loss_head(x, gain, target, tm, name):
    t, d = x.shape

    def body(x_ref, g_ref, tgt_ref, dx_ref, dg_ref, loss_ref):
        i = pl.program_id(0)
        xf = x_ref[...]
        g = g_ref[...]
        r = lax.rsqrt(jnp.mean(xf * xf, axis=-1, keepdims=True) + RMS_EPS)
        xhat = xf * r
        err = xhat * g - tgt_ref[...]
        part = 0.5 * jnp.sum(jnp.mean(err * err, axis=-1, keepdims=True))
        dy = err * (1.0 / d)
        dxhat = dy * g
        dx_ref[...] = r * (dxhat - xhat * jnp.mean(dxhat * xhat, axis=-1, keepdims=True))
        dg = jnp.sum(dy * xhat, axis=0, keepdims=True)
        lpart = jnp.full((8, LANES), part, F32)

        @pl.when(i == 0)
        def _():
            dg_ref[...] = dg
            loss_ref[...] = lpart

        @pl.when(i > 0)
        def _():
            dg_ref[...] += dg
            loss_ref[...] += lpart

    row = pl.BlockSpec((tm, d), lambda i: (i, 0))
    vec = pl.BlockSpec((1, d), lambda i: (0, 0))
    return pl.pallas_call(
        body, name=name, grid=(t // tm,), in_specs=[row, vec, row],
        out_specs=[row, vec, pl.BlockSpec((8, LANES), lambda i: (0, 0))],
        out_shape=[_sds((t, d), F32), _sds((1, d), F32), _sds((8, LANES), F32)],
        compiler_params=_cparams(1))(x, gain, target)


def swiglu_fwd(gu, tm, name):
    _, t, w = gu.shape

    def body(g_ref, u_ref, o_ref):
        g = g_ref[...].astype(F32)
        o_ref[...] = (g * _sigmoid(g) * u_ref[...].astype(F32)).astype(o_ref.dtype)

    return pl.pallas_call(
        body, name=name, grid=(4, t // tm),
        in_specs=[pl.BlockSpec((None, tm, w), lambda j, i: (j, i, 0)),
                  pl.BlockSpec((None, tm, w), lambda j, i: (j + 4, i, 0))],
        out_specs=pl.BlockSpec((None, tm, w), lambda j, i: (j, i, 0)),
        out_shape=_sds((4, t, w), BF16), compiler_params=_cparams(2))(gu, gu)


def swiglu_bwd(gu, dact, tm, name):
    _, t, w = gu.shape

    def body(g_ref, u_ref, da_ref, o_ref):
        j = pl.program_id(0)
        g = g_ref[...].astype(F32)
        da = da_ref[...].astype(F32)
        s = _sigmoid(g)

        @pl.when(j < 4)
        def _():
            o_ref[...] = (da * u_ref[...].astype(F32) * (s * (1.0 + g * (1.0 - s)))).astype(o_ref.dtype)

        @pl.when(j >= 4)
        def _():
            o_ref[...] = (da * g * s).astype(o_ref.dtype)

    return pl.pallas_call(
        body, name=name, grid=(8, t // tm),
        in_specs=[pl.BlockSpec((None, tm, w), lambda j, i: (j % 4, i, 0)),
                  pl.BlockSpec((None, tm, w), lambda j, i: (j % 4 + 4, i, 0)),
                  pl.BlockSpec((None, tm, w), lambda j, i: (j % 4, i, 0))],
        out_specs=pl.BlockSpec((None, tm, w), lambda j, i: (j, i, 0)),
        out_shape=_sds((8, t, w), BF16), compiler_params=_cparams(2))(gu, gu, dact)


def merge_fwd(gates, ya, yb, yc, tm, name):
    t, d = ya.shape

    def body(ga_ref, gb_ref, gc_ref, ya_ref, yb_ref, yc_ref, o_ref):
        m = (_sigmoid(ga_ref[...]) * ya_ref[...] + _sigmoid(gb_ref[...]) * yb_ref[...]
             + _sigmoid(gc_ref[...]) * yc_ref[...])
        o_ref[...] = m.astype(o_ref.dtype)

    row = pl.BlockSpec((tm, d), lambda i: (i, 0))
    gspecs = [pl.BlockSpec((tm, d), functools.partial(lambda i, a: (i, a), a=a)) for a in range(3)]
    return pl.pallas_call(
        body, name=name, grid=(t // tm,), in_specs=gspecs + [row, row, row], out_specs=row,
        out_shape=_sds((t, d), BF16), compiler_params=_cparams(1))(gates, gates, gates, ya, yb, yc)


def merge_bwd(dm, gates, ya, yb, yc, tm, name):
    t, d = ya.shape

    def body(dm_ref, g_ref, y_ref, dg_ref, dy_ref):
        dmv = dm_ref[...]
        s = _sigmoid(g_ref[...])
        dy_ref[...] = (dmv * s).astype(dy_ref.dtype)
        dg_ref[...] = (dmv * y_ref[...] * s * (1.0 - s)).astype(dg_ref.dtype)

    outs = []
    dgs = []
    for a, y in enumerate((ya, yb, yc)):
        row = pl.BlockSpec((tm, d), lambda i: (i, 0))
        gspec = pl.BlockSpec((tm, d), functools.partial(lambda i, a: (i, a), a=a))
        dg, dy = pl.pallas_call(
            functools.partial(body), name=f"{name}_{a}", grid=(t // tm,),
            in_specs=[row, gspec, row], out_specs=[row, row],
            out_shape=[_sds((t, d), BF16), _sds((t, d), BF16)],
            compiler_params=_cparams(1))(dm, gates, y)
        dgs.append(dg)
        outs.append(dy)
    return dgs, outs


def _iota2(shape, dim):
    return lax.broadcasted_iota(jnp.int32, shape, dim)


def forget_cumsum(f, name):
    t = f.shape[0]
    nq = t // QB

    def body(f_ref, fcol_ref, frow_ref, carry):
        j = pl.program_id(0)

        @pl.when(j == 0)
        def _():
            carry[...] = jnp.zeros_like(carry)

        logf = _log_sigmoid(f_ref[...])
        tri = (_iota2((QB, QB), 1) <= _iota2((QB, QB), 0)).astype(F32)
        blk = jnp.dot(tri, logf, precision=HIGHEST, preferred_element_type=F32) + carry[...]
        carry[...] += jnp.sum(logf, axis=0, keepdims=True)
        fcol_ref[...] = blk
        frow_ref[...] = blk.T[0:8, :]

    return pl.pallas_call(
        body, name=name, grid=(nq,),
        in_specs=[pl.BlockSpec((QB, LANES), lambda j: (j, 0))],
        out_specs=[pl.BlockSpec((QB, LANES), lambda j: (j, 0)),
                   pl.BlockSpec((None, 8, QB), lambda j: (j, 0, 0))],
        out_shape=[_sds((t, LANES), F32), _sds((nq, 8, QB), F32)],
        scratch_shapes=[pltpu.VMEM((1, LANES), F32)], compiler_params=_cparams(1))(f)


def forget_cumsum_bwd(dfrow, f, name):
    t = f.shape[0]
    nq = t // QB

    def body(dfr_ref, f_ref, df_ref, carry):
        jj = pl.program_id(0)

        @pl.when(jj == 0)
        def _():
            carry[...] = jnp.zeros_like(carry)

        padded = jnp.concatenate([dfr_ref[...], jnp.zeros((QB - 8, QB), F32)], axis=0)
        dfcol = padded.T
        tri = (_iota2((QB, QB), 1) >= _iota2((QB, QB), 0)).astype(F32)
        dlogf = jnp.dot(tri, dfcol, precision=HIGHEST, preferred_element_type=F32) + carry[...]
        carry[...] += jnp.sum(dfcol, axis=0, keepdims=True)
        df_ref[...] = dlogf * _sigmoid(-f_ref[...])

    return pl.pallas_call(
        body, name=name, grid=(nq,),
        in_specs=[pl.BlockSpec((None, 8, QB), lambda jj: (nq - 1 - jj, 0, 0)),
                  pl.BlockSpec((QB, LANES), lambda jj: (nq - 1 - jj, 0))],
        out_specs=pl.BlockSpec((QB, LANES), lambda jj: (nq - 1 - jj, 0)),
        out_shape=_sds((t, LANES), F32),
        scratch_shapes=[pltpu.VMEM((1, LANES), F32)], compiler_params=_cparams(1))(dfrow, f)


REL_NB = 4096


def _rel_index():
    qi = np.arange(QB)[:, None]
    kj = np.arange(CH_KEYS)[None, :]
    rel = np.clip(qi + (CH_WIN - 1) * QB - kj, -MAX_REL, MAX_REL) + MAX_REL
    return rel.reshape(1, QB * CH_KEYS).astype(np.int32)


def rel_bias_build(tab_t, name):
    n = QB * CH_KEYS
    idx = jnp.asarray(_rel_index())

    def body(tab_ref, idx_ref, o_ref):
        onehot = (_iota2((REL_PAD, REL_NB), 0) == idx_ref[...]).astype(F32)
        o_ref[...] = jnp.dot(tab_ref[...], onehot, precision=HIGHEST, preferred_element_type=F32)

    return pl.pallas_call(
        body, name=name, grid=(n // REL_NB,),
        in_specs=[pl.BlockSpec((8, REL_PAD), lambda i: (0, 0)),
                  pl.BlockSpec((1, REL_NB), lambda i: (0, i))],
        out_specs=pl.BlockSpec((8, REL_NB), lambda i: (0, i)),
        out_shape=_sds((8, n), F32), compiler_params=_cparams(1))(tab_t, idx)


def rel_bias_scatter(dbias, name):
    n = QB * CH_KEYS
    idx = jnp.asarray(_rel_index())

    def body(db_ref, idx_ref, o_ref):
        i = pl.program_id(0)
        onehot = (_iota2((REL_PAD, REL_NB), 0) == idx_ref[...]).astype(F32)
        part = lax.dot_general(db_ref[...], onehot, NT, precision=HIGHEST, preferred_element_type=F32)

        @pl.when(i == 0)
        def _():
            o_ref[...] = part

        @pl.when(i > 0)
        def _():
            o_ref[...] += part

    return pl.pallas_call(
        body, name=name, grid=(n // REL_NB,),
        in_specs=[pl.BlockSpec((8, REL_NB), lambda i: (0, i)),
                  pl.BlockSpec((1, REL_NB), lambda i: (0, i))],
        out_specs=pl.BlockSpec((8, REL_PAD), lambda i: (0, 0)),
        out_shape=_sds((8, REL_PAD), F32), compiler_params=_cparams(1))(dbias, idx)


def _hl(h):
    return slice(h * HEAD_DIM, (h + 1) * HEAD_DIM)


def _split_dot(x, tri_bf16):
    hi = x.astype(BF16)
    lo = (x - hi.astype(F32)).astype(BF16)
    return (jnp.dot(hi, tri_bf16, preferred_element_type=F32)
            + jnp.dot(lo, tri_bf16, preferred_element_type=F32))


def _rows(j):
    return pl.ds(pl.multiple_of(j * QB, QB), QB)


def _qkv_specs(t, col0, n_pairs):
    q_spec = pl.BlockSpec((QB, LANES), lambda hp, i: (i, col0 + hp))
    k_spec = pl.BlockSpec((t, LANES), lambda hp, i: (0, col0 + n_pairs + hp))
    v_spec = pl.BlockSpec((t, LANES), lambda hp, i: (0, col0 + 2 * n_pairs + hp))
    return q_spec, k_spec, v_spec


def sb_fwd(qkv, name):
    t = qkv.shape[0]
    nq = t // QB

    def body(q_ref, k_ref, v_ref, o_ref):
        i = pl.program_id(1)
        tri_after = (_iota2((QB, QB), 0) > _iota2((QB, QB), 1)).astype(BF16)
        t_idx = i * QB + _iota2((QB, QB), 0)
        for h in range(2):
            q = q_ref[:, _hl(h)]

            def step(jj, carry):
                tail, acc = carry
                j = i - jj
                k = k_ref[_rows(j), _hl(h)]
                v = v_ref[_rows(j), _hl(h)]
                z = lax.dot_general(q, k, NT, preferred_element_type=F32) * SCALE
                strict = (j * QB + _iota2((QB, QB), 1)) < t_idx
                lf = jnp.where(strict, _log_sigmoid(-z), 0.0)
                between = _split_dot(lf, tri_after) + tail
                w = jnp.where(strict, jnp.exp(_log_sigmoid(z) + between), 0.0)
                acc = acc + jnp.dot(w.astype(BF16), v, preferred_element_type=F32)
                return tail + jnp.sum(lf, axis=1, keepdims=True), acc

            _, acc = lax.fori_loop(0, i + 1, step,
                                   (jnp.zeros((QB, 1), F32), jnp.zeros((QB, HEAD_DIM), F32)))
            o_ref[:, _hl(h)] = acc.astype(o_ref.dtype)

    q_spec, k_spec, v_spec = _qkv_specs(t, 0, 2)
    return pl.pallas_call(
        body, name=name, grid=(2, nq), in_specs=[q_spec, k_spec, v_spec],
        out_specs=pl.BlockSpec((QB, LANES), lambda hp, i: (i, hp)),
        out_shape=_sds((t, W_SB), BF16), compiler_params=_cparams(2))(qkv, qkv, qkv)


def sb_bwd(qkv, do, name):
    t = qkv.shape[0]
    nq = t // QB

    def body(q_ref, k_ref, v_ref, do_ref, dq_ref, dk_ref, dv_ref, w_scr):
        i = pl.program_id(1)

        @pl.when(i == 0)
        def _():
            dk_ref[...] = jnp.zeros_like(dk_ref)
            dv_ref[...] = jnp.zeros_like(dv_ref)

        tri_after = (_iota2((QB, QB), 0) > _iota2((QB, QB), 1)).astype(BF16)
        tri_before = (_iota2((QB, QB), 0) < _iota2((QB, QB), 1)).astype(BF16)
        t_idx = i * QB + _iota2((QB, QB), 0)
        for h in range(2):
            q = q_ref[:, _hl(h)]
            dov = do_ref[:, _hl(h)]

            def weights(jj, tail):
                j = i - jj
                k = k_ref[_rows(j), _hl(h)]
                z = lax.dot_general(q, k, NT, preferred_element_type=F32) * SCALE
                strict = (j * QB + _iota2((QB, QB), 1)) < t_idx
                lf = jnp.where(strict, _log_sigmoid(-z), 0.0)
                between = _split_dot(lf, tri_after) + tail
                w_scr[j] = jnp.where(strict, jnp.exp(_log_sigmoid(z) + between), 0.0)
                return tail + jnp.sum(lf, axis=1, keepdims=True)

            lax.fori_loop(0, i + 1, weights, jnp.zeros((QB, 1), F32))

            def grads(j, carry):
                head, dq = carry
                k = k_ref[_rows(j), _hl(h)]
                v = v_ref[_rows(j), _hl(h)]
                w = w_scr[j]
                z = lax.dot_general(q, k, NT, preferred_element_type=F32) * SCALE
                beta = _sigmoid(z)
                strict = (j * QB + _iota2((QB, QB), 1)) < t_idx
                e = lax.dot_general(dov, v, NT, preferred_element_type=F32) * w
                before = _split_dot(e, tri_before) + head
                dz = jnp.where(strict, e * (1.0 - beta) - before * beta, 0.0) * SCALE
                dzb = dz.astype(BF16)
                dq = dq + jnp.dot(dzb, k, preferred_element_type=F32)
                dk_ref[_rows(j), _hl(h)] += lax.dot_general(dzb, q, TN, preferred_element_type=F32)
                dv_ref[_rows(j), _hl(h)] += lax.dot_general(w.astype(BF16), dov, TN,
                                                            preferred_element_type=F32)
                return head + jnp.sum(e, axis=1, keepdims=True), dq

            _, dq = lax.fori_loop(0, i + 1, grads,
                                  (jnp.zeros((QB, 1), F32), jnp.zeros((QB, HEAD_DIM), F32)))
            dq_ref[:, _hl(h)] = dq.astype(dq_ref.dtype)

    q_spec, k_spec, v_spec = _qkv_specs(t, 0, 2)
    blk = pl.BlockSpec((QB, LANES), lambda hp, i: (i, hp))
    full = pl.BlockSpec((t, LANES), lambda hp, i: (0, hp))
    return pl.pallas_call(
        body, name=name, grid=(2, nq), in_specs=[q_spec, k_spec, v_spec, blk],
        out_specs=[blk, full, full],
        out_shape=[_sds((t, W_SB), BF16), _sds((t, W_SB), F32), _sds((t, W_SB), F32)],
        scratch_shapes=[pltpu.VMEM((nq, QB, QB), F32)], compiler_params=_cparams(2))(qkv, qkv, qkv, do)


def fox_fwd(qkv, fcol, frow, name):
    t = qkv.shape[0]
    nq = t // QB

    def body(q_ref, k_ref, v_ref, fc_ref, fr_ref, o_ref, lse_ref):
        hp = pl.program_id(0)
        i = pl.program_id(1)
        t_idx = i * QB + _iota2((QB, QB), 0)
        lane = _iota2((QB, LANES), 1)
        sub = _iota2((8, QB), 0)
        for h in range(2):
            hd = hp * 2 + h
            q = q_ref[:, _hl(h)]
            f_q = jnp.sum(jnp.where(lane == hd, fc_ref[...], 0.0), axis=1, keepdims=True)

            def step(j, carry):
                m, l, acc = carry
                k = k_ref[_rows(j), _hl(h)]
                v = v_ref[_rows(j), _hl(h)]
                f_k = jnp.sum(jnp.where(sub == hd, fr_ref[j], 0.0), axis=0, keepdims=True)
                z = lax.dot_general(q, k, NT, preferred_element_type=F32) * SCALE + f_q - f_k
                causal = (j * QB + _iota2((QB, QB), 1)) <= t_idx
                z = jnp.where(causal, z, NEG)
                m_new = jnp.maximum(m, jnp.max(z, axis=1, keepdims=True))
                p = jnp.exp(z - m_new)
                corr = jnp.exp(m - m_new)
                l = l * corr + jnp.sum(p, axis=1, keepdims=True)
                acc = acc * corr + jnp.dot(p.astype(BF16), v, preferred_element_type=F32)
                return m_new, l, acc

            m, l, acc = lax.fori_loop(0, i + 1, step,
                                      (jnp.full((QB, 1), NEG, F32), jnp.zeros((QB, 1), F32),
                                       jnp.zeros((QB, HEAD_DIM), F32)))
            o_ref[:, _hl(h)] = (acc / l).astype(o_ref.dtype)
            lse_ref[:, _hl(h)] = jnp.broadcast_to(m + jnp.log(l), (QB, HEAD_DIM))

    q_spec, k_spec, v_spec = _qkv_specs(t, 18, 2)
    blk = pl.BlockSpec((QB, LANES), lambda hp, i: (i, hp))
    return pl.pallas_call(
        body, name=name, grid=(2, nq),
        in_specs=[q_spec, k_spec, v_spec, pl.BlockSpec((QB, LANES), lambda hp, i: (i, 0)),
                  pl.BlockSpec((nq, 8, QB), lambda hp, i: (0, 0, 0))],
        out_specs=[blk, blk],
        out_shape=[_sds((t, W_FOX), BF16), _sds((t, W_FOX), F32)],
        compiler_params=_cparams(2))(qkv, qkv, qkv, fcol, frow)


def fox_bwd(qkv, fcol, frow, o, lse, do, name):
    t = qkv.shape[0]
    nq = t // QB

    def body(q_ref, k_ref, v_ref, fc_ref, fr_ref, o_ref, lse_ref, do_ref,
             dq_ref, dk_ref, dv_ref, dfr_ref):
        hp = pl.program_id(0)
        i = pl.program_id(1)

        @pl.when(i == 0)
        def _():
            dk_ref[...] = jnp.zeros_like(dk_ref)
            dv_ref[...] = jnp.zeros_like(dv_ref)

        @pl.when((i == 0) & (hp == 0))
        def _():
            dfr_ref[...] = jnp.zeros_like(dfr_ref)

        t_idx = i * QB + _iota2((QB, QB), 0)
        lane = _iota2((QB, LANES), 1)
        sub = _iota2((8, QB), 0)
        for h in range(2):
            hd = hp * 2 + h
            q = q_ref[:, _hl(h)]
            dov = do_ref[:, _hl(h)]
            f_q = jnp.sum(jnp.where(lane == hd, fc_ref[...], 0.0), axis=1, keepdims=True)
            lse_q = lse_ref[:, h * HEAD_DIM:h * HEAD_DIM + 1]
            delta = jnp.sum(dov.astype(F32) * o_ref[:, _hl(h)].astype(F32), axis=1, keepdims=True)

            def step(j, dq):
                k = k_ref[_rows(j), _hl(h)]
                v = v_ref[_rows(j), _hl(h)]
                f_k = jnp.sum(jnp.where(sub == hd, fr_ref[j], 0.0), axis=0, keepdims=True)
                z = lax.dot_general(q, k, NT, preferred_element_type=F32) * SCALE + f_q - f_k
                causal = (j * QB + _iota2((QB, QB), 1)) <= t_idx
                p = jnp.where(causal, jnp.exp(z - lse_q), 0.0)
                dp = lax.dot_general(dov, v, NT, preferred_element_type=F32)
                ds = p * (dp - delta)
                dsb = (ds * SCALE).astype(BF16)
                dq = dq + jnp.dot(dsb, k, preferred_element_type=F32)
                dk_ref[_rows(j), _hl(h)] += lax.dot_general(dsb, q, TN, preferred_element_type=F32)
                dv_ref[_rows(j), _hl(h)] += lax.dot_general(p.astype(BF16), dov, TN,
                                                            preferred_element_type=F32)
                colsum = jnp.sum(ds, axis=0, keepdims=True)
                dfr_ref[j] += jnp.where(sub == hd, -colsum, 0.0)
                return dq

            dq = lax.fori_loop(0, i + 1, step, jnp.zeros((QB, HEAD_DIM), F32))
            dq_ref[:, _hl(h)] = dq.astype(dq_ref.dtype)

    q_spec, k_spec, v_spec = _qkv_specs(t, 18, 2)
    blk = pl.BlockSpec((QB, LANES), lambda hp, i: (i, hp))
    full = pl.BlockSpec((t, LANES), lambda hp, i: (0, hp))
    frs = pl.BlockSpec((nq, 8, QB), lambda hp, i: (0, 0, 0))
    return pl.pallas_call(
        body, name=name, grid=(2, nq),
        in_specs=[q_spec, k_spec, v_spec, pl.BlockSpec((QB, LANES), lambda hp, i: (i, 0)), frs,
                  blk, blk, blk],
        out_specs=[blk, full, full, frs],
        out_shape=[_sds((t, W_FOX), BF16), _sds((t, W_FOX), F32), _sds((t, W_FOX), F32),
                   _sds((nq, 8, QB), F32)],
        compiler_params=_cparams(2))(qkv, qkv, qkv, fcol, frow, o, lse, do)


def _chunk_valid(i):
    qi = _iota2((QB, CH_KEYS), 0)
    kj = _iota2((QB, CH_KEYS), 1)
    dchunk = (qi >> 6) + LEFT_CHUNKS - (kj >> 6)
    return (dchunk >= 0) & (dchunk <= LEFT_CHUNKS) & ((i - (CH_WIN - 1)) * QB + kj >= 0)


def _chunk_scores(q, k_ref, h, i, bias, valid):
    zs = []
    for b in range(CH_WIN):
        kb = jnp.maximum(i - (CH_WIN - 1) + b, 0)
        k = k_ref[_rows(kb), _hl(h)]
        zs.append(lax.dot_general(q, k, NT, preferred_element_type=F32) * SCALE)
    z = jnp.where(valid, jnp.concatenate(zs, axis=1) + bias, NEG)
    z = z - jnp.max(z, axis=1, keepdims=True)
    p = jnp.exp(z)
    return p / jnp.sum(p, axis=1, keepdims=True)


def chunk_fwd(qkv, bias, name):
    t = qkv.shape[0]
    nq = t // QB

    def body(q_ref, k_ref, v_ref, b_ref, o_ref):
        i = pl.program_id(1)
        valid = _chunk_valid(i)
        for h in range(2):
            q = q_ref[:, _hl(h)]
            p = _chunk_scores(q, k_ref, h, i, b_ref[h], valid)
            acc = jnp.zeros((QB, HEAD_DIM), F32)
            for b in range(CH_WIN):
                kb = jnp.maximum(i - (CH_WIN - 1) + b, 0)
                v = v_ref[_rows(kb), _hl(h)]
                acc = acc + jnp.dot(p[:, b * QB:(b + 1) * QB].astype(BF16), v,
                                    preferred_element_type=F32)
            o_ref[:, _hl(h)] = acc.astype(o_ref.dtype)

    q_spec, k_spec, v_spec = _qkv_specs(t, 6, 4)
    return pl.pallas_call(
        body, name=name, grid=(4, nq),
        in_specs=[q_spec, k_spec, v_spec, pl.BlockSpec((2, QB, CH_KEYS), lambda hp, i: (hp, 0, 0))],
        out_specs=pl.BlockSpec((QB, LANES), lambda hp, i: (i, hp)),
        out_shape=_sds((t, W_CH), BF16), compiler_params=_cparams(2))(qkv, qkv, qkv, bias)


def chunk_bwd(qkv, bias, do, name):
    t = qkv.shape[0]
    nq = t // QB

    def body(q_ref, k_ref, v_ref, b_ref, do_ref, dq_ref, dk_ref, dv_ref, db_ref):
        i = pl.program_id(1)

        @pl.when(i == 0)
        def _():
            dk_ref[...] = jnp.zeros_like(dk_ref)
            dv_ref[...] = jnp.zeros_like(dv_ref)
            db_ref[...] = jnp.zeros_like(db_ref)

        valid = _chunk_valid(i)
        for h in range(2):
            q = q_ref[:, _hl(h)]
            dov = do_ref[:, _hl(h)]
            p = _chunk_scores(q, k_ref, h, i, b_ref[h], valid)
            dps = []
            for b in range(CH_WIN):
                kb = jnp.maximum(i - (CH_WIN - 1) + b, 0)
                dps.append(lax.dot_general(dov, v_ref[_rows(kb), _hl(h)], NT,
                                           preferred_element_type=F32))
            dp = jnp.concatenate(dps, axis=1)
            ds = p * (dp - jnp.sum(p * dp, axis=1, keepdims=True))
            db_ref[h] += ds
            dq = jnp.zeros((QB, HEAD_DIM), F32)
            for b in range(CH_WIN):
                kb = jnp.maximum(i - (CH_WIN - 1) + b, 0)
                dsb = (ds[:, b * QB:(b + 1) * QB] * SCALE).astype(BF16)
                pb = p[:, b * QB:(b + 1) * QB].astype(BF16)
                dq = dq + jnp.dot(dsb, k_ref[_rows(kb), _hl(h)], preferred_element_type=F32)
                dk_ref[_rows(kb), _hl(h)] += lax.dot_general(dsb, q, TN, preferred_element_type=F32)
                dv_ref[_rows(kb), _hl(h)] += lax.dot_general(pb, dov, TN, preferred_element_type=F32)
            dq_ref[:, _hl(h)] = dq.astype(dq_ref.dtype)

    q_spec, k_spec, v_spec = _qkv_specs(t, 6, 4)
    blk = pl.BlockSpec((QB, LANES), lambda hp, i: (i, hp))
    full = pl.BlockSpec((t, LANES), lambda hp, i: (0, hp))
    bspec = pl.BlockSpec((2, QB, CH_KEYS), lambda hp, i: (hp, 0, 0))
    return pl.pallas_call(
        body, name=name, grid=(4, nq), in_specs=[q_spec, k_spec, v_spec, bspec, blk],
        out_specs=[blk, full, full, bspec],
        out_shape=[_sds((t, W_CH), BF16), _sds((t, W_CH), F32), _sds((t, W_CH), F32),
                   _sds((N_HEADS_CH, QB, CH_KEYS), F32)],
        compiler_params=_cparams(2))(qkv, qkv, qkv, bias, do)


def _sum_parts(p_ref):
    total = p_ref[0].astype(F32)
    for p in range(1, p_ref.shape[0]):
        total = total + p_ref[p].astype(F32)
    return total


def sum_parts(parts, grid, p_spec, o_spec, out_sds, name):
    def body(p_ref, o_ref):
        o_ref[...] = _sum_parts(p_ref)

    return pl.pallas_call(body, name=name, grid=grid, in_specs=[p_spec], out_specs=o_spec,
                          out_shape=out_sds, compiler_params=_cparams(len(grid)))(parts)


def adamw(parts, w, m, v, grid, p_spec, w_spec, name):
    c1 = 1.0 / (1.0 - ADAM_B1 ** ADAM_STEP)
    c2 = 1.0 / (1.0 - ADAM_B2 ** ADAM_STEP)

    def body(p_ref, w_ref, m_ref, v_ref, g_out, d_out, m_out, v_out):
        g = _sum_parts(p_ref)
        m_new = ADAM_B1 * m_ref[...] + (1.0 - ADAM_B1) * g
        v_new = ADAM_B2 * v_ref[...] + (1.0 - ADAM_B2) * (g * g)
        m_hat = m_new * c1
        v_hat = v_new * c2
        g_out[...] = g
        d_out[...] = -ADAM_LR * (m_hat / (jnp.sqrt(v_hat) + ADAM_EPS) + ADAM_WD * w_ref[...])
        m_out[...] = m_new
        v_out[...] = v_new

    out = _sds(w.shape, F32)
    return pl.pallas_call(
        body, name=name, grid=grid, in_specs=[p_spec, w_spec, w_spec, w_spec],
        out_specs=[w_spec] * 4, out_shape=[out] * 4,
        compiler_params=_cparams(len(grid)))(parts, w, m, v)


def _ffn_fwd(x, gain, wa, wb, s, tm, tag):
    t = x.shape[0]
    hn = rmsnorm_fwd(x, gain, tm, f"rms_{tag}")
    gu = matmul(NN, hn, wa, _sds((8, t, FF_BLK), BF16), (t // tm, 8, 1),
                pl.BlockSpec((tm, D_MODEL), lambda i, j, k: (i, 0)),
                pl.BlockSpec((None, None, D_MODEL, FF_BLK), lambda i, j, k: (s, j, 0, 0)),
                pl.BlockSpec((None, tm, FF_BLK), lambda i, j, k: (j, i, 0)), None, name=f"ffn_in_{tag}")
    act = swiglu_fwd(gu, tm, f"swiglu_{tag}")
    row = pl.BlockSpec((tm, D_MODEL), lambda i, j, k: (i, 0))
    y = matmul(NN, act, wb, _sds((t, D_MODEL), F32), (t // tm, 1, 4),
               pl.BlockSpec((None, tm, FF_BLK), lambda i, j, k: (k, i, 0)),
               pl.BlockSpec((None, None, FF_BLK, D_MODEL), lambda i, j, k: (s, k, 0, 0)),
               row, (tm, D_MODEL), name=f"ffn_out_{tag}", alpha=0.5, res=x, res_spec=row)
    return y, (hn, gu, act)


def _ffn_bwd(dy, x, gain, saved, wa, wb, s, tm, tag):
    t = x.shape[0]
    hn, gu, act = saved
    nt = t // tm
    row = pl.BlockSpec((tm, D_MODEL), lambda i, j, k: (i, 0))
    dact = matmul(NT, dy, wb, _sds((4, t, FF_BLK), BF16), (nt, 4, 1), row,
                  pl.BlockSpec((None, None, FF_BLK, D_MODEL), lambda i, j, k: (s, j, 0, 0)),
                  pl.BlockSpec((None, tm, FF_BLK), lambda i, j, k: (j, i, 0)), None,
                  name=f"ffn_dact_{tag}", alpha=0.5)
    dgu = swiglu_bwd(gu, dact, tm, f"swiglu_bwd_{tag}")
    dwb = matmul(TN, act, dy, _sds((4, FF_BLK, D_MODEL), BF16), (4, 1, nt),
                 pl.BlockSpec((None, tm, FF_BLK), lambda i, j, k: (i, k, 0)),
                 pl.BlockSpec((tm, D_MODEL), lambda i, j, k: (k, 0)),
                 pl.BlockSpec((None, FF_BLK, D_MODEL), lambda i, j, k: (i, 0, 0)),
                 (FF_BLK, D_MODEL), name=f"ffn_dwout_{tag}", alpha=0.5)
    dwa = matmul(TN, hn, dgu, _sds((8, D_MODEL, FF_BLK), BF16), (1, 8, nt),
                 pl.BlockSpec((tm, D_MODEL), lambda i, j, k: (k, 0)),
                 pl.BlockSpec((None, tm, FF_BLK), lambda i, j, k: (j, k, 0)),
                 pl.BlockSpec((None, D_MODEL, FF_BLK), lambda i, j, k: (j, 0, 0)),
                 (D_MODEL, FF_BLK), name=f"ffn_dwin_{tag}")
    dhn = matmul(NT, dgu, wa, _sds((t, D_MODEL), F32), (nt, 1, 8),
                 pl.BlockSpec((None, tm, FF_BLK), lambda i, j, k: (k, i, 0)),
                 pl.BlockSpec((None, None, D_MODEL, FF_BLK), lambda i, j, k: (s, k, 0, 0)),
                 row, (tm, D_MODEL), name=f"ffn_dh_{tag}")
    dx, dgain = rmsnorm_bwd(x, gain, dhn, dy, tm, f"rms_bwd_{tag}")
    return dx, dgain, dwa, dwb


BR_ROWS = ((0, 1), (1, 2), (3, 1))


def _mixer_fwd(x, gain, wqkv, wf, wgate, wbr, wout, bq, bf, bg, bias, layer, tm, tag):
    t = x.shape[0]
    nt = t // tm
    hm = rmsnorm_fwd(x, gain, tm, f"rms_{tag}")
    a_full = pl.BlockSpec((tm, D_MODEL), lambda i, j, k: (i, 0))
    wide_out = pl.BlockSpec((tm, D_MODEL), lambda i, j, k: (i, j))
    wide_b = pl.BlockSpec((1, D_MODEL), lambda i, j, k: (0, j))
    qkv = matmul(NN, hm, wqkv, _sds((t, QKV_WIDTH), BF16), (nt, 3, 1), a_full,
                 pl.BlockSpec((None, D_MODEL, D_MODEL), lambda i, j, k: (layer, 0, j)), wide_out, None,
                 name=f"proj_qkv_{tag}", bias=bq, bias_spec=wide_b)
    gates = matmul(NN, hm, wgate, _sds((t, 3 * D_MODEL), F32), (nt, 3, 1), a_full,
                   pl.BlockSpec((None, D_MODEL, D_MODEL), lambda i, j, k: (DEPTH + layer, 0, j)), wide_out,
                   None, name=f"proj_gate_{tag}", bias=bg, bias_spec=wide_b)
    f = matmul(NN, hm, wf, _sds((t, LANES), F32), (nt, 1, 1), a_full,
               pl.BlockSpec((None, D_MODEL, LANES), lambda i, j, k: (layer, 0, 0)),
               pl.BlockSpec((tm, LANES), lambda i, j, k: (i, 0)), None,
               name=f"proj_f_{tag}", bias=bf, bias_spec=pl.BlockSpec((1, LANES), lambda i, j, k: (0, 0)))
    fcol, frow = forget_cumsum(f, f"fcum_{tag}")
    o_sb = sb_fwd(qkv, f"sb_fwd_{tag}")
    o_ch = chunk_fwd(qkv, bias, f"chunk_fwd_{tag}")
    o_fox, lse = fox_fwd(qkv, fcol, frow, f"fox_fwd_{tag}")
    ys = []
    for a, (o, (r0, nr)) in enumerate(zip((o_sb, o_ch, o_fox), BR_ROWS)):
        ys.append(matmul(
            NN, o, wbr, _sds((t, D_MODEL), F32), (nt, 1, nr),
            pl.BlockSpec((tm, 256), lambda i, j, k: (i, k)),
            pl.BlockSpec((None, 256, D_MODEL), functools.partial(lambda i, j, k, r0: (layer, r0 + k, 0), r0=r0)),
            a_full, (tm, D_MODEL), name=f"branch{a}_{tag}"))
    merged = merge_fwd(gates, ys[0], ys[1], ys[2], tm, f"merge_{tag}")
    x_new = matmul(NN, merged, wout, _sds((t, D_MODEL), F32), (nt, 1, 1), a_full,
                   pl.BlockSpec((None, D_MODEL, D_MODEL), lambda i, j, k: (layer, 0, 0)), a_full, None,
                   name=f"wout_{tag}", res=x, res_spec=a_full)
    saved = (hm, qkv, gates, f, fcol, frow, o_sb, o_ch, o_fox, lse, ys, merged)
    return x_new, saved


def _mixer_bwd(dy, x, gain, saved, wqkv, wf, wgate, wbr, wout, bias, layer, tm, tag):
    t = x.shape[0]
    nt = t // tm
    hm, qkv, gates, f, fcol, frow, o_sb, o_ch, o_fox, lse, ys, merged = saved
    a_full = pl.BlockSpec((tm, D_MODEL), lambda i, j, k: (i, 0))
    red_row = pl.BlockSpec((tm, D_MODEL), lambda i, j, k: (k, 0))
    sq = pl.BlockSpec((D_MODEL, D_MODEL), lambda i, j, k: (0, 0))
    dmerged = matmul(NT, dy, wout, _sds((t, D_MODEL), F32), (nt, 1, 1), a_full,
                     pl.BlockSpec((None, D_MODEL, D_MODEL), lambda i, j, k: (layer, 0, 0)), a_full, None,
                     name=f"dmerged_{tag}")
    dwout = matmul(TN, merged, dy, _sds((D_MODEL, D_MODEL), BF16), (1, 1, nt), red_row, red_row, sq,
                   (D_MODEL, D_MODEL), name=f"dwout_{tag}")
    dgs, dys = merge_bwd(dmerged, gates, ys[0], ys[1], ys[2], tm, f"merge_bwd_{tag}")
    dos, dwbrs = [], []
    for a, (o, (r0, nr)) in enumerate(zip((o_sb, o_ch, o_fox), BR_ROWS)):
        dos.append(matmul(
            NT, dys[a], wbr, _sds((t, nr * 256), BF16), (nt, nr, 1), a_full,
            pl.BlockSpec((None, 256, D_MODEL), functools.partial(lambda i, j, k, r0: (layer, r0 + j, 0), r0=r0)),
            pl.BlockSpec((tm, 256), lambda i, j, k: (i, j)), None, name=f"dbranch{a}_{tag}"))
        dwbrs.append(matmul(
            TN, o, dys[a], _sds((nr * 256, D_MODEL), BF16), (nr, 1, nt),
            pl.BlockSpec((tm, 256), lambda i, j, k: (k, i)), red_row,
            pl.BlockSpec((256, D_MODEL), lambda i, j, k: (i, 0)), (256, D_MODEL), name=f"dwbr{a}_{tag}"))
    dq_a, dk_a, dv_a = sb_bwd(qkv, dos[0], f"sb_bwd_{tag}")
    dq_b, dk_b, dv_b, dbias = chunk_bwd(qkv, bias, dos[1], f"chunk_bwd_{tag}")
    dq_c, dk_c, dv_c, dfrow = fox_bwd(qkv, fcol, frow, o_fox, lse, dos[2], f"fox_bwd_{tag}")
    df = forget_cumsum_bwd(dfrow, f, f"fcum_bwd_{tag}")
    dqkv = jnp.concatenate([p.astype(BF16) for p in
                            (dq_a, dk_a, dv_a, dq_b, dk_b, dv_b, dq_c, dk_c, dv_c)], axis=1)
    dgates = jnp.concatenate(dgs, axis=1)
    dtab = rel_bias_scatter(dbias.reshape(N_HEADS_CH, QB * CH_KEYS), f"rel_scatter_{tag}")

    wide_b = pl.BlockSpec((tm, D_MODEL), lambda i, j, k: (k, j))
    wide_o = pl.BlockSpec((D_MODEL, D_MODEL), lambda i, j, k: (0, j))
    wide_cs = pl.BlockSpec((1, D_MODEL), lambda i, j, k: (0, j))
    dwqkv, dbq = matmul(TN, hm, dqkv, _sds((D_MODEL, QKV_WIDTH), BF16), (1, 3, nt), red_row, wide_b,
                        wide_o, (D_MODEL, D_MODEL), name=f"dwqkv_{tag}",
                        colsum_sds=_sds((1, QKV_WIDTH), F32), colsum_spec=wide_cs)
    dwgate, dbg = matmul(TN, hm, dgates, _sds((D_MODEL, 3 * D_MODEL), BF16), (1, 3, nt), red_row,
                         wide_b, wide_o, (D_MODEL, D_MODEL), name=f"dwgate_{tag}",
                         colsum_sds=_sds((1, 3 * D_MODEL), F32), colsum_spec=wide_cs)
    dwf, dbf = matmul(TN, hm, df, _sds((D_MODEL, LANES), BF16), (1, 1, nt), red_row,
                      pl.BlockSpec((tm, LANES), lambda i, j, k: (k, 0)),
                      pl.BlockSpec((D_MODEL, LANES), lambda i, j, k: (0, 0)), (D_MODEL, LANES),
                      name=f"dwf_{tag}", colsum_sds=_sds((1, LANES), F32),
                      colsum_spec=pl.BlockSpec((1, LANES), lambda i, j, k: (0, 0)))
    wide_a = pl.BlockSpec((tm, D_MODEL), lambda i, j, k: (i, k))
    dhm = matmul(NT, dqkv, wqkv, _sds((t, D_MODEL), F32), (nt, 1, 3), wide_a,
                 pl.BlockSpec((None, D_MODEL, D_MODEL), lambda i, j, k: (layer, 0, k)), a_full,
                 (tm, D_MODEL), name=f"dhm_qkv_{tag}")
    dhm = matmul(NT, dgates, wgate, _sds((t, D_MODEL), F32), (nt, 1, 3), wide_a,
                 pl.BlockSpec((None, D_MODEL, D_MODEL), lambda i, j, k: (DEPTH + layer, 0, k)), a_full,
                 (tm, D_MODEL), name=f"dhm_gate_{tag}", res=dhm, res_spec=a_full)
    dhm = matmul(NT, df, wf, _sds((t, D_MODEL), F32), (nt, 1, 1),
                 pl.BlockSpec((tm, LANES), lambda i, j, k: (i, 0)),
                 pl.BlockSpec((None, D_MODEL, LANES), lambda i, j, k: (layer, 0, 0)), a_full, None,
                 name=f"dhm_f_{tag}", res=dhm, res_spec=a_full)
    dx, dgain = rmsnorm_bwd(x, gain, dhm, dy, tm, f"rms_bwd_{tag}")
    dwbr = jnp.concatenate(dwbrs, axis=0)
    grads = dict(dwqkv=dwqkv, dwgate=dwgate, dwf=dwf, dwbr=dwbr, dwout=dwout,
                 dbq=dbq, dbg=dbg, dbf=dbf, dtab=dtab, dgain=dgain)
    return dx, grads


def _pack_small(pieces):
    flat = jnp.concatenate([p.reshape(-1).astype(F32) for p in pieces])
    flat = jnp.pad(flat, (0, SMALL_ROWS * LANES - flat.shape[0]))
    return flat.reshape(SMALL_ROWS, LANES)


def _unpack_small(packed, shapes):
    flat = packed.reshape(-1)
    out, pos = [], 0
    for shp in shapes:
        n = int(np.prod(shp))
        out.append(flat[pos:pos + n].reshape(shp))
        pos += n
    return out


def kernel(x, g_ffn1, w_ffn1_in, w_ffn1_out, g_mix, w_in, b_in, rel_bias, w_br_sb, w_br_ch, w_br_fox, w_out, g_ffn2, w_ffn2_in, w_ffn2_out, g_final, loss_target, m_g_ffn1, m_w_ffn1_in, m_w_ffn1_out, m_g_mix, m_w_in, m_b_in, m_rel_bias, m_w_br_sb, m_w_br_ch, m_w_br_fox, m_w_out, m_g_ffn2, m_w_ffn2_in, m_w_ffn2_out, m_g_final, v_g_ffn1, v_w_ffn1_in, v_w_ffn1_out, v_g_mix, v_w_in, v_b_in, v_rel_bias, v_w_br_sb, v_w_br_ch, v_w_br_fox, v_w_out, v_g_ffn2, v_w_ffn2_in, v_w_ffn2_out, v_g_final):
    t = x.shape[1]
    tm = min(512, t)
    xs = x[0]
    target = loss_target[0]
    f_lo, f_hi = QKV_WIDTH, QKV_WIDTH + N_HEADS_FOX

    sa = jnp.concatenate([w_ffn1_in, w_ffn2_in], axis=0).astype(BF16)
    sb = jnp.concatenate([w_ffn1_out, w_ffn2_out], axis=0).astype(BF16)
    sc = jnp.concatenate([w_in[:, :, :QKV_WIDTH], w_in[:, :, f_hi:]], axis=0).astype(BF16)
    sf = jnp.pad(w_in[:, :, f_lo:f_hi], ((0, 0), (0, 0), (0, LANES - N_HEADS_FOX))).astype(BF16)
    so = w_out.astype(BF16)
    sbr = jnp.concatenate([w_br_sb, w_br_ch, w_br_fox], axis=1).astype(BF16)

    wa = all_gather(sa, "gather_ffn_in")
    wb = all_gather(sb, "gather_ffn_out").reshape(4, 4, FF_BLK, D_MODEL)
    wc = all_gather(sc, "gather_w_in").reshape(4, D_MODEL, QKV_WIDTH)
    wqkv = wgate = wc
    wf = all_gather(sf, "gather_w_f").reshape(DEPTH, D_MODEL, LANES)
    wout = all_gather(so, "gather_w_out").reshape(DEPTH, D_MODEL, D_MODEL)
    wbr = all_gather(sbr, "gather_w_br")
    wbr = wbr.transpose(0, 2, 1, 3).reshape(DEPTH, D_MODEL, D_MODEL)

    bq = b_in[:, None, :QKV_WIDTH]
    bf = jnp.pad(b_in[:, f_lo:f_hi], ((0, 0), (0, LANES - N_HEADS_FOX)))[:, None, :]
    bg = b_in[:, None, f_hi:]
    tab_t = jnp.pad(rel_bias.transpose(0, 2, 1), ((0, 0), (0, 0), (0, REL_PAD - N_REL)))

    h = xs
    saved = []
    for l in range(DEPTH):
        bias = rel_bias_build(tab_t[l], f"rel_build_l{l}").reshape(N_HEADS_CH, QB, CH_KEYS)
        x0 = h
        x1, s1 = _ffn_fwd(x0, g_ffn1[l:l + 1], wa, wb, l, tm, f"ffn1_l{l}")
        x2, sm = _mixer_fwd(x1, g_mix[l:l + 1], wqkv, wf, wgate, wbr, wout, bq[l], bf[l], bg[l], bias,
                            l, tm, f"mix_l{l}")
        x3, s2 = _ffn_fwd(x2, g_ffn2[l:l + 1], wa, wb, DEPTH + l, tm, f"ffn2_l{l}")
        saved.append((x0, x1, x2, s1, sm, s2, bias))
        h = x3

    dx, dg_final, loss_blk = loss_head(h, g_final[None, :], target, tm, "loss_head")

    g_wa = [None] * 4
    g_wb = [None] * 4
    g_mix_l = [None] * DEPTH
    dgains = {}
    for l in reversed(range(DEPTH)):
        x0, x1, x2, s1, sm, s2, bias = saved[l]
        dx, dgains[("ffn2", l)], g_wa[DEPTH + l], g_wb[DEPTH + l] = _ffn_bwd(
            dx, x2, g_ffn2[l:l + 1], s2, wa, wb, DEPTH + l, tm, f"ffn2_l{l}")
        dx, g_mix_l[l] = _mixer_bwd(dx, x1, g_mix[l:l + 1], sm, wqkv, wf, wgate, wbr, wout, bias, l, tm,
                                    f"mix_l{l}")
        dx, dgains[("ffn1", l)], g_wa[l], g_wb[l] = _ffn_bwd(
            dx, x0, g_ffn1[l:l + 1], s1, wa, wb, l, tm, f"ffn1_l{l}")

    ga = all_to_all(jnp.stack(g_wa), "scatter_ffn_in")
    gb = all_to_all(jnp.stack(g_wb).reshape(4, N_DEV, D_FF // N_DEV, D_MODEL), "scatter_ffn_out")
    gc = all_to_all(jnp.stack([g_mix_l[0]["dwqkv"], g_mix_l[1]["dwqkv"],
                               g_mix_l[0]["dwgate"], g_mix_l[1]["dwgate"]]
                              ).reshape(4, N_DEV, LANES, QKV_WIDTH), "scatter_w_in")
    gf = all_to_all(jnp.stack([g_mix_l[l]["dwf"] for l in range(DEPTH)]
                              ).reshape(DEPTH, N_DEV, LANES, LANES), "scatter_w_f")
    go = all_to_all(jnp.stack([g_mix_l[l]["dwout"] for l in range(DEPTH)]
                              ).reshape(DEPTH, N_DEV, LANES, D_MODEL), "scatter_w_out")
    gbr = all_to_all(jnp.stack([g_mix_l[l]["dwbr"] for l in range(DEPTH)]
                               ).reshape(DEPTH, D_MODEL, N_DEV, LANES).transpose(0, 2, 1, 3),
                     "scatter_w_br")

    small_shapes = []
    small_pieces = []
    small_w, small_m, small_v = [], [], []

    def add_small(piece, w, m, v):
        small_shapes.append(w.shape)
        small_pieces.append(piece)
        small_w.append(w); small_m.append(m); small_v.append(v)

    dg1 = jnp.concatenate([dgains[("ffn1", l)] for l in range(DEPTH)], axis=0)
    dgm = jnp.concatenate([g_mix_l[l]["dgain"] for l in range(DEPTH)], axis=0)
    dg2 = jnp.concatenate([dgains[("ffn2", l)] for l in range(DEPTH)], axis=0)
    db = jnp.stack([jnp.concatenate([g_mix_l[l]["dbq"][0], g_mix_l[l]["dbf"][0, :N_HEADS_FOX],
                                     g_mix_l[l]["dbg"][0]]) for l in range(DEPTH)])
    drel = jnp.stack([g_mix_l[l]["dtab"][:, :N_REL].T for l in range(DEPTH)])
    add_small(dg1, g_ffn1, m_g_ffn1, v_g_ffn1)
    add_small(dgm, g_mix, m_g_mix, v_g_mix)
    add_small(db, b_in, m_b_in, v_b_in)
    add_small(drel, rel_bias, m_rel_bias, v_rel_bias)
    add_small(dg2, g_ffn2, m_g_ffn2, v_g_ffn2)
    add_small(dg_final[0], g_final, m_g_final, v_g_final)
    loss_piece = loss_blk[0, 0:1]
    small_sum = all_reduce_small(_pack_small(small_pieces + [loss_piece]), "allreduce_small")
    n_small = sum(int(np.prod(s)) for s in small_shapes)
    loss = small_sum.reshape(-1)[n_small]

    sm_spec = pl.BlockSpec((SMALL_ROWS, LANES), lambda i: (0, 0))
    sm_out = adamw(small_sum[None], _pack_small(small_w), _pack_small(small_m), _pack_small(small_v),
                   (1,), pl.BlockSpec((1, SMALL_ROWS, LANES), lambda i: (0, 0, 0)), sm_spec, "adamw_small")
    sm_g, sm_d, sm_m, sm_v = [_unpack_small(o, small_shapes) for o in sm_out]

    def upd(parts, s0, w, m, v, tr, name, rb0=0):
        _, r, c = w.shape
        nr = r // tr
        return adamw(parts, w, m, v, (DEPTH, nr),
                     pl.BlockSpec((N_DEV, None, tr, c), lambda l, i: (0, s0 + l, rb0 + i, 0)),
                     pl.BlockSpec((None, tr, c), lambda l, i: (l, i, 0)), name)

    r_ffn1_in = upd(ga, 0, w_ffn1_in, m_w_ffn1_in, v_w_ffn1_in, 256, "adamw_ffn1_in")
    r_ffn2_in = upd(ga, DEPTH, w_ffn2_in, m_w_ffn2_in, v_w_ffn2_in, 256, "adamw_ffn2_in")
    out_rows = D_FF // N_DEV // 2
    r_ffn1_out = upd(gb, 0, w_ffn1_out, m_w_ffn1_out, v_w_ffn1_out, out_rows, "adamw_ffn1_out")
    r_ffn2_out = upd(gb, DEPTH, w_ffn2_out, m_w_ffn2_out, v_w_ffn2_out, out_rows, "adamw_ffn2_out")
    r_out = upd(go, 0, w_out, m_w_out, v_w_out, LANES, "adamw_w_out")
    r_br_sb = upd(gbr, 0, w_br_sb, m_w_br_sb, v_w_br_sb, 256, "adamw_br_sb", rb0=0)
    r_br_ch = upd(gbr, 0, w_br_ch, m_w_br_ch, v_w_br_ch, 256, "adamw_br_ch", rb0=1)
    r_br_fox = upd(gbr, 0, w_br_fox, m_w_br_fox, v_w_br_fox, 256, "adamw_br_fox", rb0=3)

    gc_sum = sum_parts(gc, (4,), pl.BlockSpec((N_DEV, None, LANES, QKV_WIDTH), lambda s: (0, s, 0, 0)),
                       pl.BlockSpec((None, LANES, QKV_WIDTH), lambda s: (s, 0, 0)),
                       _sds((4, LANES, QKV_WIDTH), F32), "sum_w_in")
    gf_sum = sum_parts(gf, (DEPTH,), pl.BlockSpec((N_DEV, None, LANES, LANES), lambda s: (0, s, 0, 0)),
                       pl.BlockSpec((None, LANES, LANES), lambda s: (s, 0, 0)),
                       _sds((DEPTH, LANES, LANES), F32), "sum_w_f")
    g_w_in = jnp.concatenate([gc_sum[:DEPTH], gf_sum[:, :, :N_HEADS_FOX], gc_sum[DEPTH:]], axis=2)
    win_rows = 32
    win_spec = pl.BlockSpec((None, win_rows, w_in.shape[2]), lambda l, i: (l, i, 0))
    r_in = adamw(g_w_in[None], w_in, m_w_in, v_w_in, (DEPTH, LANES // win_rows),
                 pl.BlockSpec((1, None, win_rows, w_in.shape[2]), lambda l, i: (0, l, i, 0)), win_spec,
                 "adamw_w_in")

    def per_kind(k):
        small = (sm_g, sm_d, sm_m, sm_v)[k]
        return [small[0], r_ffn1_in[k], r_ffn1_out[k], small[1], r_in[k], small[2], small[3],
                r_br_sb[k], r_br_ch[k], r_br_fox[k], r_out[k], small[4], r_ffn2_in[k], r_ffn2_out[k],
                small[5]]

    return (loss, dx[None], *per_kind(0), *per_kind(1), *per_kind(2), *per_kind(3))
```

```python
import functools

import numpy as np
import jax
import jax.numpy as jnp
from jax import lax
from jax.experimental import pallas as pl
from jax.experimental.pallas import tpu as pltpu

F32 = jnp.float32
BF16 = jnp.bfloat16

N_DEV = 8
D_MODEL = 1024
DEPTH = 2
HEAD_DIM = 64
W_SB, W_CH, W_FOX = 256, 512, 256
QKV_WIDTH = 3 * (W_SB + W_CH + W_FOX)
N_HEADS_FOX = 4
N_HEADS_CH = 8
D_FF = 2816
FF_BLK = 2 * D_FF // N_DEV
CHUNK = 64
LEFT_CHUNKS = 8
MAX_REL = 128
N_REL = 2 * MAX_REL + 1
REL_PAD = 384
QB = 128
KB = 512
KSUB = KB // QB
CH_WIN = 5
CH_KEYS = CH_WIN * QB
RMS_EPS = 1e-6
NEG = -1e30
SCALE = HEAD_DIM ** -0.5
LANES = 128
VMEM_LIMIT = 56 * 1024 * 1024

ADAM_LR, ADAM_B1, ADAM_B2, ADAM_EPS, ADAM_WD, ADAM_STEP = 0.001, 0.9, 0.999, 1e-08, 0.01, 10

SMALL_ROWS = 192

MESH = pl.DeviceIdType.MESH
ANY = pl.BlockSpec(memory_space=pl.ANY)
HIGHEST = lax.Precision.HIGHEST

NN = (((1,), (0,)), ((), ()))
NT = (((1,), (1,)), ((), ()))
TN = (((0,), (0,)), ((), ()))


def _cparams(n_grid):
    return pltpu.CompilerParams(dimension_semantics=("arbitrary",) * n_grid,
                                vmem_limit_bytes=VMEM_LIMIT)


def _sds(shape, dtype):
    return jax.ShapeDtypeStruct(tuple(shape), dtype)


def _my_index():
    return 4 * lax.axis_index("x") + 2 * lax.axis_index("y") + lax.axis_index("c")


def _peer(mask):
    x, y, c = lax.axis_index("x"), lax.axis_index("y"), lax.axis_index("c")
    px = x ^ ((mask >> 2) & 1)
    py = y ^ ((mask >> 1) & 1)
    pc = c ^ (mask & 1)
    return (px, py, pc), 4 * px + 2 * py + pc


def all_gather(shard, name):
    s, r, c = shard.shape

    def body(in_ref, out_ref, send_sems, recv_sems, local_sem):
        me = _my_index()
        mine = pltpu.make_async_copy(in_ref, out_ref.at[:, me], local_sem)
        mine.start()
        sends = []
        for mask in range(1, N_DEV):
            peer, _ = _peer(mask)
            cp = pltpu.make_async_remote_copy(
                src_ref=in_ref, dst_ref=out_ref.at[:, me],
                send_sem=send_sems.at[mask - 1], recv_sem=recv_sems.at[mask - 1],
                device_id=peer, device_id_type=MESH)
            cp.start()
            sends.append(cp)
        for mask in range(1, N_DEV):
            peer, pidx = _peer(mask)
            pltpu.make_async_remote_copy(
                src_ref=in_ref, dst_ref=out_ref.at[:, pidx],
                send_sem=send_sems.at[mask - 1], recv_sem=recv_sems.at[mask - 1],
                device_id=peer, device_id_type=MESH).wait_recv()
        for cp in sends:
            cp.wait_send()
        mine.wait()

    return pl.pallas_call(
        body, name=name,
        out_shape=_sds((s, N_DEV, r, c), shard.dtype),
        in_specs=[ANY], out_specs=ANY,
        scratch_shapes=[pltpu.SemaphoreType.DMA((N_DEV - 1,)),
                        pltpu.SemaphoreType.DMA((N_DEV - 1,)),
                        pltpu.SemaphoreType.DMA],
    )(shard)


def all_to_all(parts, name):
    s, _, r, c = parts.shape

    def body(in_ref, out_ref, send_sems, recv_sems, local_sem):
        me = _my_index()
        mine = pltpu.make_async_copy(in_ref.at[:, me], out_ref.at[me], local_sem)
        mine.start()
        sends = []
        for mask in range(1, N_DEV):
            peer, pidx = _peer(mask)
            cp = pltpu.make_async_remote_copy(
                src_ref=in_ref.at[:, pidx], dst_ref=out_ref.at[me],
                send_sem=send_sems.at[mask - 1], recv_sem=recv_sems.at[mask - 1],
                device_id=peer, device_id_type=MESH)
            cp.start()
            sends.append(cp)
        for mask in range(1, N_DEV):
            peer, pidx = _peer(mask)
            pltpu.make_async_remote_copy(
                src_ref=in_ref.at[:, me], dst_ref=out_ref.at[pidx],
                send_sem=send_sems.at[mask - 1], recv_sem=recv_sems.at[mask - 1],
                device_id=peer, device_id_type=MESH).wait_recv()
        for cp in sends:
            cp.wait_send()
        mine.wait()

    return pl.pallas_call(
        body, name=name,
        out_shape=_sds((N_DEV, s, r, c), parts.dtype),
        in_specs=[ANY], out_specs=ANY,
        scratch_shapes=[pltpu.SemaphoreType.DMA((N_DEV - 1,)),
                        pltpu.SemaphoreType.DMA((N_DEV - 1,)),
                        pltpu.SemaphoreType.DMA],
    )(parts)


def all_reduce_small(packed, name):
    rows = packed.shape[0]

    def body(in_ref, out_ref, slots, send_sems, recv_sems):
        me = _my_index()
        sends = []
        for mask in range(1, N_DEV):
            peer, _ = _peer(mask)
            cp = pltpu.make_async_remote_copy(
                src_ref=in_ref, dst_ref=slots.at[me],
                send_sem=send_sems.at[mask - 1], recv_sem=recv_sems.at[mask - 1],
                device_id=peer, device_id_type=MESH)
            cp.start()
            sends.append(cp)
        slots[me] = in_ref[...]
        for mask in range(1, N_DEV):
            peer, pidx = _peer(mask)
            pltpu.make_async_remote_copy(
                src_ref=in_ref, dst_ref=slots.at[pidx],
                send_sem=send_sems.at[mask - 1], recv_sem=recv_sems.at[mask - 1],
                device_id=peer, device_id_type=MESH).wait_recv()
        for cp in sends:
            cp.wait_send()
        total = slots[0]
        for p in range(1, N_DEV):
            total = total + slots[p]
        out_ref[...] = total

    return pl.pallas_call(
        body, name=name,
        out_shape=_sds((rows, LANES), F32),
        in_specs=[pl.BlockSpec(memory_space=pltpu.VMEM)],
        out_specs=pl.BlockSpec(memory_space=pltpu.VMEM),
        scratch_shapes=[pltpu.VMEM((N_DEV, rows, LANES), F32),
                        pltpu.SemaphoreType.DMA((N_DEV - 1,)),
                        pltpu.SemaphoreType.DMA((N_DEV - 1,))],
    )(packed)


def matmul(dims, a, b, out_sds, grid, a_spec, b_spec, o_spec, acc_shape, *, name, alpha=1.0,
           bias=None, bias_spec=None, res=None, res_spec=None, colsum_sds=None, colsum_spec=None):
    nk = grid[2]
    has_bias, has_res, has_cs = bias is not None, res is not None, colsum_sds is not None
    if has_cs:
        assert grid[0] == 1 and dims == TN

    def body(*refs):
        a_ref, b_ref = refs[0], refs[1]
        pos = 2
        bias_ref = res_ref = cs_ref = None
        if has_bias:
            bias_ref = refs[pos]; pos += 1
        if has_res:
            res_ref = refs[pos]; pos += 1
        o_ref = refs[pos]; pos += 1
        if has_cs:
            cs_ref = refs[pos]; pos += 1
        k = pl.program_id(2)
        bval = b_ref[...]
        part = lax.dot_general(a_ref[...].astype(BF16), bval.astype(BF16), dims,
                               preferred_element_type=F32)

        def finish(total):
            r = total * alpha if alpha != 1.0 else total
            if has_bias:
                r = r + bias_ref[...]
            if has_res:
                r = r + res_ref[...].astype(F32)
            o_ref[...] = r.astype(o_ref.dtype)

        if has_cs:
            csum = jnp.sum(bval.astype(F32), axis=0, keepdims=True)

            @pl.when(k == 0)
            def _():
                cs_ref[...] = csum

            @pl.when(k > 0)
            def _():
                cs_ref[...] += csum

        if nk == 1:
            finish(part)
        else:
            acc_ref = refs[pos]

            @pl.when(k == 0)
            def _():
                acc_ref[...] = part

            @pl.when(k > 0)
            def _():
                acc_ref[...] += part

            @pl.when(k == nk - 1)
            def _():
                finish(acc_ref[...])

    in_specs, args = [a_spec, b_spec], [a, b]
    if has_bias:
        in_specs.append(bias_spec); args.append(bias)
    if has_res:
        in_specs.append(res_spec); args.append(res)
    out_shape, out_specs = [out_sds], [o_spec]
    if has_cs:
        out_shape.append(colsum_sds); out_specs.append(colsum_spec)
    scratch = [] if nk == 1 else [pltpu.VMEM(acc_shape, F32)]
    outs = pl.pallas_call(
        body, name=name, grid=grid, in_specs=in_specs, out_specs=out_specs, out_shape=out_shape,
        scratch_shapes=scratch, compiler_params=_cparams(3))(*args)
    return outs if has_cs else outs[0]


def _sigmoid(z):
    return 1.0 / (1.0 + jnp.exp(-z))


def _log_sigmoid(z):
    return jnp.minimum(z, 0.0) - jnp.log(1.0 + jnp.exp(-jnp.abs(z)))


def rmsnorm_fwd(x, gain, tm, name):
    t, d = x.shape

    def body(x_ref, g_ref, o_ref):
        xf = x_ref[...]
        r = lax.rsqrt(jnp.mean(xf * xf, axis=-1, keepdims=True) + RMS_EPS)
        o_ref[...] = (xf * r * g_ref[...]).astype(o_ref.dtype)

    return pl.pallas_call(
        body, name=name, grid=(t // tm,),
        in_specs=[pl.BlockSpec((tm, d), lambda i: (i, 0)), pl.BlockSpec((1, d), lambda i: (0, 0))],
        out_specs=pl.BlockSpec((tm, d), lambda i: (i, 0)),
        out_shape=_sds((t, d), BF16), compiler_params=_cparams(1))(x, gain)


def rmsnorm_bwd(x, gain, dh, dres, tm, name):
    t, d = x.shape

    def body(x_ref, g_ref, dh_ref, dres_ref, dx_ref, dg_ref):
        i = pl.program_id(0)
        xf = x_ref[...]
        r = lax.rsqrt(jnp.mean(xf * xf, axis=-1, keepdims=True) + RMS_EPS)
        xhat = xf * r
        dh_v = dh_ref[...]
        dxhat = dh_v * g_ref[...]
        dx = r * (dxhat - xhat * jnp.mean(dxhat * xhat, axis=-1, keepdims=True))
        dx_ref[...] = dres_ref[...] + dx
        dg = jnp.sum(dh_v * xhat, axis=0, keepdims=True)

        @pl.when(i == 0)
        def _():
            dg_ref[...] = dg

        @pl.when(i > 0)
        def _():
            dg_ref[...] += dg

    row = pl.BlockSpec((tm, d), lambda i: (i, 0))
    vec = pl.BlockSpec((1, d), lambda i: (0, 0))
    return pl.pallas_call(
        body, name=name, grid=(t // tm,), in_specs=[row, vec, row, row], out_specs=[row, vec],
        out_shape=[_sds((t, d), F32), _sds((1, d), F32)], compiler_params=_cparams(1))(x, gain, dh, dres)


def loss_head(x, gain, target, tm, name):
    t, d = x.shape

    def body(x_ref, g_ref, tgt_ref, dx_ref, dg_ref, loss_ref):
        i = pl.program_id(0)
        xf = x_ref[...]
        g = g_ref[...]
        r = lax.rsqrt(jnp.mean(xf * xf, axis=-1, keepdims=True) + RMS_EPS)
        xhat = xf * r
        err = xhat * g - tgt_ref[...]
        part = 0.5 * jnp.sum(jnp.mean(err * err, axis=-1, keepdims=True))
        dy = err * (1.0 / d)
        dxhat = dy * g
        dx_ref[...] = r * (dxhat - xhat * jnp.mean(dxhat * xhat, axis=-1, keepdims=True))
        dg = jnp.sum(dy * xhat, axis=0, keepdims=True)
        lpart = jnp.full((8, LANES), part, F32)

        @pl.when(i == 0)
        def _():
            dg_ref[...] = dg
            loss_ref[...] = lpart

        @pl.when(i > 0)
        def _():
            dg_ref[...] += dg
            loss_ref[...] += lpart

    row = pl.BlockSpec((tm, d), lambda i: (i, 0))
    vec = pl.BlockSpec((1, d), lambda i: (0, 0))
    return pl.pallas_call(
        body, name=name, grid=(t // tm,), in_specs=[row, vec, row],
        out_specs=[row, vec, pl.BlockSpec((8, LANES), lambda i: (0, 0))],
        out_shape=[_sds((t, d), F32), _sds((1, d), F32), _sds((8, LANES), F32)],
        compiler_params=_cparams(1))(x, gain, target)


def swiglu_fwd(gu, tm, name):
    _, t, w = gu.shape

    def body(g_ref, u_ref, o_ref):
        g = g_ref[...].astype(F32)
        o_ref[...] = (g * _sigmoid(g) * u_ref[...].astype(F32)).astype(o_ref.dtype)

    return pl.pallas_call(
        body, name=name, grid=(4, t // tm),
        in_specs=[pl.BlockSpec((None, tm, w), lambda j, i: (j, i, 0)),
                  pl.BlockSpec((None, tm, w), lambda j, i: (j + 4, i, 0))],
        out_specs=pl.BlockSpec((None, tm, w), lambda j, i: (j, i, 0)),
        out_shape=_sds((4, t, w), BF16), compiler_params=_cparams(2))(gu, gu)


def swiglu_bwd(gu, dact, tm, name):
    _, t, w = gu.shape

    def body(g_ref, u_ref, da_ref, o_ref):
        j = pl.program_id(0)
        g = g_ref[...].astype(F32)
        da = da_ref[...].astype(F32)
        s = _sigmoid(g)

        @pl.when(j < 4)
        def _():
            o_ref[...] = (da * u_ref[...].astype(F32) * (s * (1.0 + g * (1.0 - s)))).astype(o_ref.dtype)

        @pl.when(j >= 4)
        def _():
            o_ref[...] = (da * g * s).astype(o_ref.dtype)

    return pl.pallas_call(
        body, name=name, grid=(8, t // tm),
        in_specs=[pl.BlockSpec((None, tm, w), lambda j, i: (j % 4, i, 0)),
                  pl.BlockSpec((None, tm, w), lambda j, i: (j % 4 + 4, i, 0)),
                  pl.BlockSpec((None, tm, w), lambda j, i: (j % 4, i, 0))],
        out_specs=pl.BlockSpec((None, tm, w), lambda j, i: (j, i, 0)),
        out_shape=_sds((8, t, w), BF16), compiler_params=_cparams(2))(gu, gu, dact)


def merge_fwd(gates, ya, yb, yc, tm, name):
    t, d = ya.shape

    def body(ga_ref, gb_ref, gc_ref, ya_ref, yb_ref, yc_ref, o_ref):
        m = (_sigmoid(ga_ref[...]) * ya_ref[...] + _sigmoid(gb_ref[...]) * yb_ref[...]
             + _sigmoid(gc_ref[...]) * yc_ref[...])
        o_ref[...] = m.astype(o_ref.dtype)

    row = pl.BlockSpec((tm, d), lambda i: (i, 0))
    gspecs = [pl.BlockSpec((tm, d), functools.partial(lambda i, a: (i, a), a=a)) for a in range(3)]
    return pl.pallas_call(
        body, name=name, grid=(t // tm,), in_specs=gspecs + [row, row, row], out_specs=row,
        out_shape=_sds((t, d), BF16), compiler_params=_cparams(1))(gates, gates, gates, ya, yb, yc)


def merge_bwd(dm, gates, ya, yb, yc, tm, name):
    t, d = ya.shape

    def body(dm_ref, g_ref, y_ref, dg_ref, dy_ref):
        dmv = dm_ref[...]
        s = _sigmoid(g_ref[...])
        dy_ref[...] = (dmv * s).astype(dy_ref.dtype)
        dg_ref[...] = (dmv * y_ref[...] * s * (1.0 - s)).astype(dg_ref.dtype)

    outs = []
    dgs = []
    for a, y in enumerate((ya, yb, yc)):
        row = pl.BlockSpec((tm, d), lambda i: (i, 0))
        gspec = pl.BlockSpec((tm, d), functools.partial(lambda i, a: (i, a), a=a))
        dg, dy = pl.pallas_call(
            functools.partial(body), name=f"{name}_{a}", grid=(t // tm,),
            in_specs=[row, gspec, row], out_specs=[row, row],
            out_shape=[_sds((t, d), BF16), _sds((t, d), BF16)],
            compiler_params=_cparams(1))(dm, gates, y)
        dgs.append(dg)
        outs.append(dy)
    return dgs, outs


def _iota2(shape, dim):
    return lax.broadcasted_iota(jnp.int32, shape, dim)


def forget_cumsum(f, name):
    t = f.shape[0]
    nq = t // QB

    def body(f_ref, fcol_ref, frow_ref, carry):
        j = pl.program_id(0)

        @pl.when(j == 0)
        def _():
            carry[...] = jnp.zeros_like(carry)

        logf = _log_sigmoid(f_ref[...])
        tri = (_iota2((QB, QB), 1) <= _iota2((QB, QB), 0)).astype(F32)
        blk = jnp.dot(tri, logf, precision=HIGHEST, preferred_element_type=F32) + carry[...]
        carry[...] += jnp.sum(logf, axis=0, keepdims=True)
        fcol_ref[...] = blk
        frow_ref[...] = blk.T[0:8, :]

    return pl.pallas_call(
        body, name=name, grid=(nq,),
        in_specs=[pl.BlockSpec((QB, LANES), lambda j: (j, 0))],
        out_specs=[pl.BlockSpec((QB, LANES), lambda j: (j, 0)),
                   pl.BlockSpec((None, 8, QB), lambda j: (j, 0, 0))],
        out_shape=[_sds((t, LANES), F32), _sds((nq, 8, QB), F32)],
        scratch_shapes=[pltpu.VMEM((1, LANES), F32)], compiler_params=_cparams(1))(f)


def forget_cumsum_bwd(dfrow, f, name):
    t = f.shape[0]
    nq = t // QB

    def body(dfr_ref, f_ref, df_ref, carry):
        jj = pl.program_id(0)

        @pl.when(jj == 0)
        def _():
            carry[...] = jnp.zeros_like(carry)

        padded = jnp.concatenate([dfr_ref[...], jnp.zeros((QB - 8, QB), F32)], axis=0)
        dfcol = padded.T
        tri = (_iota2((QB, QB), 1) >= _iota2((QB, QB), 0)).astype(F32)
        dlogf = jnp.dot(tri, dfcol, precision=HIGHEST, preferred_element_type=F32) + carry[...]
        carry[...] += jnp.sum(dfcol, axis=0, keepdims=True)
        df_ref[...] = dlogf * _sigmoid(-f_ref[...])

    return pl.pallas_call(
        body, name=name, grid=(nq,),
        in_specs=[pl.BlockSpec((None, 8, QB), lambda jj: (nq - 1 - jj, 0, 0)),
                  pl.BlockSpec((QB, LANES), lambda jj: (nq - 1 - jj, 0))],
        out_specs=pl.BlockSpec((QB, LANES), lambda jj: (nq - 1 - jj, 0)),
        out_shape=_sds((t, LANES), F32),
        scratch_shapes=[pltpu.VMEM((1, LANES), F32)], compiler_params=_cparams(1))(dfrow, f)


REL_NB = 4096


def _rel_index():
    qi = np.arange(QB)[:, None]
    kj = np.arange(CH_KEYS)[None, :]
    rel = np.clip(qi + (CH_WIN - 1) * QB - kj, -MAX_REL, MAX_REL) + MAX_REL
    return rel.reshape(1, QB * CH_KEYS).astype(np.int32)


def rel_bias_build(tab_t, name):
    n = QB * CH_KEYS
    idx = jnp.asarray(_rel_index())

    def body(tab_ref, idx_ref, o_ref):
        onehot = (_iota2((REL_PAD, REL_NB), 0) == idx_ref[...]).astype(F32)
        o_ref[...] = jnp.dot(tab_ref[...], onehot, precision=HIGHEST, preferred_element_type=F32)

    return pl.pallas_call(
        body, name=name, grid=(n // REL_NB,),
        in_specs=[pl.BlockSpec((8, REL_PAD), lambda i: (0, 0)),
                  pl.BlockSpec((1, REL_NB), lambda i: (0, i))],
        out_specs=pl.BlockSpec((8, REL_NB), lambda i: (0, i)),
        out_shape=_sds((8, n), F32), compiler_params=_cparams(1))(tab_t, idx)


def rel_bias_scatter(dbias, name):
    n = QB * CH_KEYS
    idx = jnp.asarray(_rel_index())

    def body(db_ref, idx_ref, o_ref):
        i = pl.program_id(0)
        onehot = (_iota2((REL_PAD, REL_NB), 0) == idx_ref[...]).astype(F32)
        part = lax.dot_general(db_ref[...], onehot, NT, precision=HIGHEST, preferred_element_type=F32)

        @pl.when(i == 0)
        def _():
            o_ref[...] = part

        @pl.when(i > 0)
        def _():
            o_ref[...] += part

    return pl.pallas_call(
        body, name=name, grid=(n // REL_NB,),
        in_specs=[pl.BlockSpec((8, REL_NB), lambda i: (0, i)),
                  pl.BlockSpec((1, REL_NB), lambda i: (0, i))],
        out_specs=pl.BlockSpec((8, REL_PAD), lambda i: (0, 0)),
        out_shape=_sds((8, REL_PAD), F32), compiler_params=_cparams(1))(dbias, idx)


def _hl(h):
    return slice(h * HEAD_DIM, (h + 1) * HEAD_DIM)


def _split_dot(x, tri_bf16):
    hi = x.astype(BF16)
    lo = (x - hi.astype(F32)).astype(BF16)
    return (jnp.dot(hi, tri_bf16, preferred_element_type=F32)
            + jnp.dot(lo, tri_bf16, preferred_element_type=F32))


def _rows(j):
    return pl.ds(pl.multiple_of(j * QB, QB), QB)


def _krows(g):
    return pl.ds(pl.multiple_of(g * KB, KB), KB)


def _log_sigmoid_pair(z):
    sp = jnp.log(1.0 + jnp.exp(-jnp.abs(z)))
    return jnp.minimum(z, 0.0) - sp, -jnp.maximum(z, 0.0) - sp


def _qkv_specs(t, col0, n_pairs):
    q_spec = pl.BlockSpec((QB, LANES), lambda hp, i: (i, col0 + hp))
    k_spec = pl.BlockSpec((t, LANES), lambda hp, i: (0, col0 + n_pairs + hp))
    v_spec = pl.BlockSpec((t, LANES), lambda hp, i: (0, col0 + 2 * n_pairs + hp))
    return q_spec, k_spec, v_spec


def sb_fwd(qkv, name):
    t = qkv.shape[0]
    nq = t // QB

    def body(q_ref, k_ref, v_ref, o_ref):
        i = pl.program_id(1)
        groups = i // KSUB + 1
        tri_after = (_iota2((KB, KB), 0) > _iota2((KB, KB), 1)).astype(BF16)
        t_idx = i * QB + _iota2((QB, KB), 0)
        qs = [q_ref[:, _hl(h)] for h in range(2)]

        def step(gg, carry):
            g = groups - 1 - gg
            strict = (g * KB + _iota2((QB, KB), 1)) < t_idx
            out = []
            for h in range(2):
                tail, acc = carry[2 * h], carry[2 * h + 1]
                k = k_ref[_krows(g), _hl(h)]
                v = v_ref[_krows(g), _hl(h)]
                z = lax.dot_general(qs[h], k, NT, preferred_element_type=F32) * SCALE
                lb, lf = _log_sigmoid_pair(z)
                lf = jnp.where(strict, lf, 0.0)
                between = _split_dot(lf, tri_after) + tail
                w = jnp.where(strict, jnp.exp(lb + between), 0.0)
                acc = acc + jnp.dot(w.astype(BF16), v, preferred_element_type=F32)
                out += [tail + jnp.sum(lf, axis=1, keepdims=True), acc]
            return tuple(out)

        init = (jnp.zeros((QB, 1), F32), jnp.zeros((QB, HEAD_DIM), F32)) * 2
        res = lax.fori_loop(0, groups, step, init)
        for h in range(2):
            o_ref[:, _hl(h)] = res[2 * h + 1].astype(o_ref.dtype)

    q_spec, k_spec, v_spec = _qkv_specs(t, 0, 2)
    return pl.pallas_call(
        body, name=name, grid=(2, nq), in_specs=[q_spec, k_spec, v_spec],
        out_specs=pl.BlockSpec((QB, LANES), lambda hp, i: (i, hp)),
        out_shape=_sds((t, W_SB), BF16), compiler_params=_cparams(2))(qkv, qkv, qkv)


def sb_bwd(qkv, do, name):
    t = qkv.shape[0]
    nq = t // QB

    def body(q_ref, k_ref, v_ref, do_ref, dq_ref, dk_ref, dv_ref, w_scr):
        i = pl.program_id(1)

        @pl.when(i == 0)
        def _():
            dk_ref[...] = jnp.zeros_like(dk_ref)
            dv_ref[...] = jnp.zeros_like(dv_ref)

        groups = i // KSUB + 1
        tri_after = (_iota2((KB, KB), 0) > _iota2((KB, KB), 1)).astype(BF16)
        tri_before = (_iota2((KB, KB), 0) < _iota2((KB, KB), 1)).astype(BF16)
        t_idx = i * QB + _iota2((QB, KB), 0)
        qs = [q_ref[:, _hl(h)] for h in range(2)]
        dos = [do_ref[:, _hl(h)] for h in range(2)]

        def weights(gg, tails):
            g = groups - 1 - gg
            strict = (g * KB + _iota2((QB, KB), 1)) < t_idx
            out = []
            for h in range(2):
                k = k_ref[_krows(g), _hl(h)]
                z = lax.dot_general(qs[h], k, NT, preferred_element_type=F32) * SCALE
                lb, lf = _log_sigmoid_pair(z)
                lf = jnp.where(strict, lf, 0.0)
                between = _split_dot(lf, tri_after) + tails[h]
                w_scr[h, g] = jnp.where(strict, jnp.exp(lb + between), 0.0)
                out.append(tails[h] + jnp.sum(lf, axis=1, keepdims=True))
            return tuple(out)

        lax.fori_loop(0, groups, weights, (jnp.zeros((QB, 1), F32),) * 2)

        def grads(g, carry):
            strict = (g * KB + _iota2((QB, KB), 1)) < t_idx
            out = []
            for h in range(2):
                head, dq = carry[2 * h], carry[2 * h + 1]
                k = k_ref[_krows(g), _hl(h)]
                v = v_ref[_krows(g), _hl(h)]
                w = w_scr[h, g]
                z = lax.dot_general(qs[h], k, NT, preferred_element_type=F32) * SCALE
                beta = _sigmoid(z)
                e = lax.dot_general(dos[h], v, NT, preferred_element_type=F32) * w
                before = _split_dot(e, tri_before) + head
                dz = jnp.where(strict, e * (1.0 - beta) - before * beta, 0.0) * SCALE
                dzb = dz.astype(BF16)
                dq = dq + jnp.dot(dzb, k, preferred_element_type=F32)
                dk_ref[_krows(g), _hl(h)] += lax.dot_general(dzb, qs[h], TN, preferred_element_type=F32)
                dv_ref[_krows(g), _hl(h)] += lax.dot_general(w.astype(BF16), dos[h], TN,
                                                             preferred_element_type=F32)
                out += [head + jnp.sum(e, axis=1, keepdims=True), dq]
            return tuple(out)

        init = (jnp.zeros((QB, 1), F32), jnp.zeros((QB, HEAD_DIM), F32)) * 2
        res = lax.fori_loop(0, groups, grads, init)
        for h in range(2):
            dq_ref[:, _hl(h)] = res[2 * h + 1].astype(dq_ref.dtype)

    q_spec, k_spec, v_spec = _qkv_specs(t, 0, 2)
    blk = pl.BlockSpec((QB, LANES), lambda hp, i: (i, hp))
    full = pl.BlockSpec((t, LANES), lambda hp, i: (0, hp))
    return pl.pallas_call(
        body, name=name, grid=(2, nq), in_specs=[q_spec, k_spec, v_spec, blk],
        out_specs=[blk, full, full],
        out_shape=[_sds((t, W_SB), BF16), _sds((t, W_SB), F32), _sds((t, W_SB), F32)],
        scratch_shapes=[pltpu.VMEM((2, t // KB, QB, KB), F32)],
        compiler_params=_cparams(2))(qkv, qkv, qkv, do)


def fox_fwd(qkv, fcol, frow, name):
    t = qkv.shape[0]
    nq = t // QB

    def body(q_ref, k_ref, v_ref, fc_ref, fr_ref, o_ref, lse_ref):
        hp = pl.program_id(0)
        i = pl.program_id(1)
        groups = i // KSUB + 1
        t_idx = i * QB + _iota2((QB, KB), 0)
        lane = _iota2((QB, LANES), 1)
        sub = _iota2((8, KB), 0)
        qs = [q_ref[:, _hl(h)] for h in range(2)]
        f_qs = [jnp.sum(jnp.where(lane == hp * 2 + h, fc_ref[...], 0.0), axis=1, keepdims=True)
                for h in range(2)]

        def step(g, carry):
            causal = (g * KB + _iota2((QB, KB), 1)) <= t_idx
            fr = fr_ref[g]
            out = []
            for h in range(2):
                m, l, acc = carry[3 * h:3 * h + 3]
                k = k_ref[_krows(g), _hl(h)]
                v = v_ref[_krows(g), _hl(h)]
                f_k = jnp.sum(jnp.where(sub == hp * 2 + h, fr, 0.0), axis=0, keepdims=True)
                z = lax.dot_general(qs[h], k, NT, preferred_element_type=F32) * SCALE + f_qs[h] - f_k
                z = jnp.where(causal, z, NEG)
                m_new = jnp.maximum(m, jnp.max(z, axis=1, keepdims=True))
                p = jnp.exp(z - m_new)
                corr = jnp.exp(m - m_new)
                l = l * corr + jnp.sum(p, axis=1, keepdims=True)
                acc = acc * corr + jnp.dot(p.astype(BF16), v, preferred_element_type=F32)
                out += [m_new, l, acc]
            return tuple(out)

        init = (jnp.full((QB, 1), NEG, F32), jnp.zeros((QB, 1), F32), jnp.zeros((QB, HEAD_DIM), F32)) * 2
        res = lax.fori_loop(0, groups, step, init)
        for h in range(2):
            m, l, acc = res[3 * h:3 * h + 3]
            o_ref[:, _hl(h)] = (acc / l).astype(o_ref.dtype)
            lse_ref[:, _hl(h)] = jnp.broadcast_to(m + jnp.log(l), (QB, HEAD_DIM))

    q_spec, k_spec, v_spec = _qkv_specs(t, 18, 2)
    blk = pl.BlockSpec((QB, LANES), lambda hp, i: (i, hp))
    return pl.pallas_call(
        body, name=name, grid=(2, nq),
        in_specs=[q_spec, k_spec, v_spec, pl.BlockSpec((QB, LANES), lambda hp, i: (i, 0)),
                  pl.BlockSpec((t // KB, 8, KB), lambda hp, i: (0, 0, 0))],
        out_specs=[blk, blk],
        out_shape=[_sds((t, W_FOX), BF16), _sds((t, W_FOX), F32)],
        compiler_params=_cparams(2))(qkv, qkv, qkv, fcol, frow)


def fox_bwd(qkv, fcol, frow, o, lse, do, name):
    t = qkv.shape[0]
    nq = t // QB

    def body(q_ref, k_ref, v_ref, fc_ref, fr_ref, o_ref, lse_ref, do_ref,
             dq_ref, dk_ref, dv_ref, dfr_ref):
        hp = pl.program_id(0)
        i = pl.program_id(1)

        @pl.when(i == 0)
        def _():
            dk_ref[...] = jnp.zeros_like(dk_ref)
            dv_ref[...] = jnp.zeros_like(dv_ref)

        @pl.when((i == 0) & (hp == 0))
        def _():
            dfr_ref[...] = jnp.zeros_like(dfr_ref)

        groups = i // KSUB + 1
        t_idx = i * QB + _iota2((QB, KB), 0)
        lane = _iota2((QB, LANES), 1)
        sub = _iota2((8, KB), 0)
        qs = [q_ref[:, _hl(h)] for h in range(2)]
        dos = [do_ref[:, _hl(h)] for h in range(2)]
        f_qs = [jnp.sum(jnp.where(lane == hp * 2 + h, fc_ref[...], 0.0), axis=1, keepdims=True)
                for h in range(2)]
        lse_qs = [lse_ref[:, h * HEAD_DIM:h * HEAD_DIM + 1] for h in range(2)]
        deltas = [jnp.sum(dos[h].astype(F32) * o_ref[:, _hl(h)].astype(F32), axis=1, keepdims=True)
                  for h in range(2)]

        def step(g, dqs):
            causal = (g * KB + _iota2((QB, KB), 1)) <= t_idx
            fr = fr_ref[g]
            out = []
            dfr = jnp.zeros((8, KB), F32)
            for h in range(2):
                k = k_ref[_krows(g), _hl(h)]
                v = v_ref[_krows(g), _hl(h)]
                f_k = jnp.sum(jnp.where(sub == hp * 2 + h, fr, 0.0), axis=0, keepdims=True)
                z = lax.dot_general(qs[h], k, NT, preferred_element_type=F32) * SCALE + f_qs[h] - f_k
                p = jnp.where(causal, jnp.exp(z - lse_qs[h]), 0.0)
                dp = lax.dot_general(dos[h], v, NT, preferred_element_type=F32)
                ds = p * (dp - deltas[h])
                dsb = (ds * SCALE).astype(BF16)
                out.append(dqs[h] + jnp.dot(dsb, k, preferred_element_type=F32))
                dk_ref[_krows(g), _hl(h)] += lax.dot_general(dsb, qs[h], TN, preferred_element_type=F32)
                dv_ref[_krows(g), _hl(h)] += lax.dot_general(p.astype(BF16), dos[h], TN,
                                                             preferred_element_type=F32)
                colsum = jnp.sum(ds, axis=0, keepdims=True)
                dfr = dfr + jnp.where(sub == hp * 2 + h, -colsum, 0.0)
            dfr_ref[g] += dfr
            return tuple(out)

        res = lax.fori_loop(0, groups, step, (jnp.zeros((QB, HEAD_DIM), F32),) * 2)
        for h in range(2):
            dq_ref[:, _hl(h)] = res[h].astype(dq_ref.dtype)

    q_spec, k_spec, v_spec = _qkv_specs(t, 18, 2)
    blk = pl.BlockSpec((QB, LANES), lambda hp, i: (i, hp))
    full = pl.BlockSpec((t, LANES), lambda hp, i: (0, hp))
    frs = pl.BlockSpec((t // KB, 8, KB), lambda hp, i: (0, 0, 0))
    return pl.pallas_call(
        body, name=name, grid=(2, nq),
        in_specs=[q_spec, k_spec, v_spec, pl.BlockSpec((QB, LANES), lambda hp, i: (i, 0)), frs,
                  blk, blk, blk],
        out_specs=[blk, full, full, frs],
        out_shape=[_sds((t, W_FOX), BF16), _sds((t, W_FOX), F32), _sds((t, W_FOX), F32),
                   _sds((t // KB, 8, KB), F32)],
        compiler_params=_cparams(2))(qkv, qkv, qkv, fcol, frow, o, lse, do)


def _frow_to_groups(frow):
    n = frow.shape[0] // KSUB
    return frow.reshape(n, KSUB, 8, QB).transpose(0, 2, 1, 3).reshape(n, 8, KB)


def _frow_from_groups(frow):
    n = frow.shape[0]
    return frow.reshape(n, 8, KSUB, QB).transpose(0, 2, 1, 3).reshape(n * KSUB, 8, QB)


def _chunk_valid(i):
    qi = _iota2((QB, CH_KEYS), 0)
    kj = _iota2((QB, CH_KEYS), 1)
    dchunk = (qi >> 6) + LEFT_CHUNKS - (kj >> 6)
    return (dchunk >= 0) & (dchunk <= LEFT_CHUNKS) & ((i - (CH_WIN - 1)) * QB + kj >= 0)


def _chunk_scores(q, k_ref, h, i, bias, valid):
    zs = []
    for b in range(CH_WIN):
        kb = jnp.maximum(i - (CH_WIN - 1) + b, 0)
        k = k_ref[_rows(kb), _hl(h)]
        zs.append(lax.dot_general(q, k, NT, preferred_element_type=F32) * SCALE)
    z = jnp.where(valid, jnp.concatenate(zs, axis=1) + bias, NEG)
    z = z - jnp.max(z, axis=1, keepdims=True)
    p = jnp.exp(z)
    return p / jnp.sum(p, axis=1, keepdims=True)


def chunk_fwd(qkv, bias, name):
    t = qkv.shape[0]
    nq = t // QB

    def body(q_ref, k_ref, v_ref, b_ref, o_ref):
        i = pl.program_id(1)
        valid = _chunk_valid(i)
        for h in range(2):
            q = q_ref[:, _hl(h)]
            p = _chunk_scores(q, k_ref, h, i, b_ref[h], valid)
            acc = jnp.zeros((QB, HEAD_DIM), F32)
            for b in range(CH_WIN):
                kb = jnp.maximum(i - (CH_WIN - 1) + b, 0)
                v = v_ref[_rows(kb), _hl(h)]
                acc = acc + jnp.dot(p[:, b * QB:(b + 1) * QB].astype(BF16), v,
                                    preferred_element_type=F32)
            o_ref[:, _hl(h)] = acc.astype(o_ref.dtype)

    q_spec, k_spec, v_spec = _qkv_specs(t, 6, 4)
    return pl.pallas_call(
        body, name=name, grid=(4, nq),
        in_specs=[q_spec, k_spec, v_spec, pl.BlockSpec((2, QB, CH_KEYS), lambda hp, i: (hp, 0, 0))],
        out_specs=pl.BlockSpec((QB, LANES), lambda hp, i: (i, hp)),
        out_shape=_sds((t, W_CH), BF16), compiler_params=_cparams(2))(qkv, qkv, qkv, bias)


def chunk_bwd(qkv, bias, do, name):
    t = qkv.shape[0]
    nq = t // QB

    def body(q_ref, k_ref, v_ref, b_ref, do_ref, dq_ref, dk_ref, dv_ref, db_ref):
        i = pl.program_id(1)

        @pl.when(i == 0)
        def _():
            dk_ref[...] = jnp.zeros_like(dk_ref)
            dv_ref[...] = jnp.zeros_like(dv_ref)
            db_ref[...] = jnp.zeros_like(db_ref)

        valid = _chunk_valid(i)
        for h in range(2):
            q = q_ref[:, _hl(h)]
            dov = do_ref[:, _hl(h)]
            p = _chunk_scores(q, k_ref, h, i, b_ref[h], valid)
            dps = []
            for b in range(CH_WIN):
                kb = jnp.maximum(i - (CH_WIN - 1) + b, 0)
                dps.append(lax.dot_general(dov, v_ref[_rows(kb), _hl(h)], NT,
                                           preferred_element_type=F32))
            dp = jnp.concatenate(dps, axis=1)
            ds = p * (dp - jnp.sum(p * dp, axis=1, keepdims=True))
            db_ref[h] += ds
            dq = jnp.zeros((QB, HEAD_DIM), F32)
            for b in range(CH_WIN):
                kb = jnp.maximum(i - (CH_WIN - 1) + b, 0)
                dsb = (ds[:, b * QB:(b + 1) * QB] * SCALE).astype(BF16)
                pb = p[:, b * QB:(b + 1) * QB].astype(BF16)
                dq = dq + jnp.dot(dsb, k_ref[_rows(kb), _hl(h)], preferred_element_type=F32)
                dk_ref[_rows(kb), _hl(h)] += lax.dot_general(dsb, q, TN, preferred_element_type=F32)
                dv_ref[_rows(kb), _hl(h)] += lax.dot_general(pb, dov, TN, preferred_element_type=F32)
            dq_ref[:, _hl(h)] = dq.astype(dq_ref.dtype)

    q_spec, k_spec, v_spec = _qkv_specs(t, 6, 4)
    blk = pl.BlockSpec((QB, LANES), lambda hp, i: (i, hp))
    full = pl.BlockSpec((t, LANES), lambda hp, i: (0, hp))
    bspec = pl.BlockSpec((2, QB, CH_KEYS), lambda hp, i: (hp, 0, 0))
    return pl.pallas_call(
        body, name=name, grid=(4, nq), in_specs=[q_spec, k_spec, v_spec, bspec, blk],
        out_specs=[blk, full, full, bspec],
        out_shape=[_sds((t, W_CH), BF16), _sds((t, W_CH), F32), _sds((t, W_CH), F32),
                   _sds((N_HEADS_CH, QB, CH_KEYS), F32)],
        compiler_params=_cparams(2))(qkv, qkv, qkv, bias, do)


def _sum_parts(p_ref):
    total = p_ref[0].astype(F32)
    for p in range(1, p_ref.shape[0]):
        total = total + p_ref[p].astype(F32)
    return total


def sum_parts(parts, grid, p_spec, o_spec, out_sds, name):
    def body(p_ref, o_ref):
        o_ref[...] = _sum_parts(p_ref)

    return pl.pallas_call(body, name=name, grid=grid, in_specs=[p_spec], out_specs=o_spec,
                          out_shape=out_sds, compiler_params=_cparams(len(grid)))(parts)


def adamw(parts, w, m, v, grid, p_spec, w_spec, name):
    c1 = 1.0 / (1.0 - ADAM_B1 ** ADAM_STEP)
    c2 = 1.0 / (1.0 - ADAM_B2 ** ADAM_STEP)

    def body(p_ref, w_ref, m_ref, v_ref, g_out, d_out, m_out, v_out):
        g = _sum_parts(p_ref)
        m_new = ADAM_B1 * m_ref[...] + (1.0 - ADAM_B1) * g
        v_new = ADAM_B2 * v_ref[...] + (1.0 - ADAM_B2) * (g * g)
        m_hat = m_new * c1
        v_hat = v_new * c2
        g_out[...] = g
        d_out[...] = -ADAM_LR * (m_hat / (jnp.sqrt(v_hat) + ADAM_EPS) + ADAM_WD * w_ref[...])
        m_out[...] = m_new
        v_out[...] = v_new

    out = _sds(w.shape, F32)
    return pl.pallas_call(
        body, name=name, grid=grid, in_specs=[p_spec, w_spec, w_spec, w_spec],
        out_specs=[w_spec] * 4, out_shape=[out] * 4,
        compiler_params=_cparams(len(grid)))(parts, w, m, v)


def _ffn_fwd(x, gain, wa, wb, s, tm, tag):
    t = x.shape[0]
    hn = rmsnorm_fwd(x, gain, tm, f"rms_{tag}")
    gu = matmul(NN, hn, wa, _sds((8, t, FF_BLK), BF16), (t // tm, 8, 1),
                pl.BlockSpec((tm, D_MODEL), lambda i, j, k: (i, 0)),
                pl.BlockSpec((None, None, D_MODEL, FF_BLK), lambda i, j, k: (s, j, 0, 0)),
                pl.BlockSpec((None, tm, FF_BLK), lambda i, j, k: (j, i, 0)), None, name=f"ffn_in_{tag}")
    act = swiglu_fwd(gu, tm, f"swiglu_{tag}")
    row = pl.BlockSpec((tm, D_MODEL), lambda i, j, k: (i, 0))
    y = matmul(NN, act, wb, _sds((t, D_MODEL), F32), (t // tm, 1, 4),
               pl.BlockSpec((None, tm, FF_BLK), lambda i, j, k: (k, i, 0)),
               pl.BlockSpec((None, None, FF_BLK, D_MODEL), lambda i, j, k: (s, k, 0, 0)),
               row, (tm, D_MODEL), name=f"ffn_out_{tag}", alpha=0.5, res=x, res_spec=row)
    return y, (hn, gu, act)


def _ffn_bwd(dy, x, gain, saved, wa, wb, s, tm, tag):
    t = x.shape[0]
    hn, gu, act = saved
    nt = t // tm
    row = pl.BlockSpec((tm, D_MODEL), lambda i, j, k: (i, 0))
    dact = matmul(NT, dy, wb, _sds((4, t, FF_BLK), BF16), (nt, 4, 1), row,
                  pl.BlockSpec((None, None, FF_BLK, D_MODEL), lambda i, j, k: (s, j, 0, 0)),
                  pl.BlockSpec((None, tm, FF_BLK), lambda i, j, k: (j, i, 0)), None,
                  name=f"ffn_dact_{tag}", alpha=0.5)
    dgu = swiglu_bwd(gu, dact, tm, f"swiglu_bwd_{tag}")
    dwb = matmul(TN, act, dy, _sds((4, FF_BLK, D_MODEL), BF16), (4, 1, nt),
                 pl.BlockSpec((None, tm, FF_BLK), lambda i, j, k: (i, k, 0)),
                 pl.BlockSpec((tm, D_MODEL), lambda i, j, k: (k, 0)),
                 pl.BlockSpec((None, FF_BLK, D_MODEL), lambda i, j, k: (i, 0, 0)),
                 (FF_BLK, D_MODEL), name=f"ffn_dwout_{tag}", alpha=0.5)
    dwa = matmul(TN, hn, dgu, _sds((8, D_MODEL, FF_BLK), BF16), (1, 8, nt),
                 pl.BlockSpec((tm, D_MODEL), lambda i, j, k: (k, 0)),
                 pl.BlockSpec((None, tm, FF_BLK), lambda i, j, k: (j, k, 0)),
                 pl.BlockSpec((None, D_MODEL, FF_BLK), lambda i, j, k: (j, 0, 0)),
                 (D_MODEL, FF_BLK), name=f"ffn_dwin_{tag}")
    dhn = matmul(NT, dgu, wa, _sds((t, D_MODEL), F32), (nt, 1, 8),
                 pl.BlockSpec((None, tm, FF_BLK), lambda i, j, k: (k, i, 0)),
                 pl.BlockSpec((None, None, D_MODEL, FF_BLK), lambda i, j, k: (s, k, 0, 0)),
                 row, (tm, D_MODEL), name=f"ffn_dh_{tag}")
    dx, dgain = rmsnorm_bwd(x, gain, dhn, dy, tm, f"rms_bwd_{tag}")
    return dx, dgain, dwa, dwb


BR_ROWS = ((0, 1), (1, 2), (3, 1))


def _mixer_fwd(x, gain, wqkv, wf, wgate, wbr, wout, bq, bf, bg, bias, layer, tm, tag):
    t = x.shape[0]
    nt = t // tm
    hm = rmsnorm_fwd(x, gain, tm, f"rms_{tag}")
    a_full = pl.BlockSpec((tm, D_MODEL), lambda i, j, k: (i, 0))
    wide_out = pl.BlockSpec((tm, D_MODEL), lambda i, j, k: (i, j))
    wide_b = pl.BlockSpec((1, D_MODEL), lambda i, j, k: (0, j))
    qkv = matmul(NN, hm, wqkv, _sds((t, QKV_WIDTH), BF16), (nt, 3, 1), a_full,
                 pl.BlockSpec((None, D_MODEL, D_MODEL), lambda i, j, k: (layer, 0, j)), wide_out, None,
                 name=f"proj_qkv_{tag}", bias=bq, bias_spec=wide_b)
    gates = matmul(NN, hm, wgate, _sds((t, 3 * D_MODEL), F32), (nt, 3, 1), a_full,
                   pl.BlockSpec((None, D_MODEL, D_MODEL), lambda i, j, k: (DEPTH + layer, 0, j)), wide_out,
                   None, name=f"proj_gate_{tag}", bias=bg, bias_spec=wide_b)
    f = matmul(NN, hm, wf, _sds((t, LANES), F32), (nt, 1, 1), a_full,
               pl.BlockSpec((None, D_MODEL, LANES), lambda i, j, k: (layer, 0, 0)),
               pl.BlockSpec((tm, LANES), lambda i, j, k: (i, 0)), None,
               name=f"proj_f_{tag}", bias=bf, bias_spec=pl.BlockSpec((1, LANES), lambda i, j, k: (0, 0)))
    fcol, frow = forget_cumsum(f, f"fcum_{tag}")
    frow = _frow_to_groups(frow)
    o_sb = sb_fwd(qkv, f"sb_fwd_{tag}")
    o_ch = chunk_fwd(qkv, bias, f"chunk_fwd_{tag}")
    o_fox, lse = fox_fwd(qkv, fcol, frow, f"fox_fwd_{tag}")
    ys = []
    for a, (o, (r0, nr)) in enumerate(zip((o_sb, o_ch, o_fox), BR_ROWS)):
        ys.append(matmul(
            NN, o, wbr, _sds((t, D_MODEL), F32), (nt, 1, nr),
            pl.BlockSpec((tm, 256), lambda i, j, k: (i, k)),
            pl.BlockSpec((None, 256, D_MODEL), functools.partial(lambda i, j, k, r0: (layer, r0 + k, 0), r0=r0)),
            a_full, (tm, D_MODEL), name=f"branch{a}_{tag}"))
    merged = merge_fwd(gates, ys[0], ys[1], ys[2], tm, f"merge_{tag}")
    x_new = matmul(NN, merged, wout, _sds((t, D_MODEL), F32), (nt, 1, 1), a_full,
                   pl.BlockSpec((None, D_MODEL, D_MODEL), lambda i, j, k: (layer, 0, 0)), a_full, None,
                   name=f"wout_{tag}", res=x, res_spec=a_full)
    saved = (hm, qkv, gates, f, fcol, frow, o_sb, o_ch, o_fox, lse, ys, merged)
    return x_new, saved


def _mixer_bwd(dy, x, gain, saved, wqkv, wf, wgate, wbr, wout, bias, layer, tm, tag):
    t = x.shape[0]
    nt = t // tm
    hm, qkv, gates, f, fcol, frow, o_sb, o_ch, o_fox, lse, ys, merged = saved
    a_full = pl.BlockSpec((tm, D_MODEL), lambda i, j, k: (i, 0))
    red_row = pl.BlockSpec((tm, D_MODEL), lambda i, j, k: (k, 0))
    sq = pl.BlockSpec((D_MODEL, D_MODEL), lambda i, j, k: (0, 0))
    dmerged = matmul(NT, dy, wout, _sds((t, D_MODEL), F32), (nt, 1, 1), a_full,
                     pl.BlockSpec((None, D_MODEL, D_MODEL), lambda i, j, k: (layer, 0, 0)), a_full, None,
                     name=f"dmerged_{tag}")
    dwout = matmul(TN, merged, dy, _sds((D_MODEL, D_MODEL), BF16), (1, 1, nt), red_row, red_row, sq,
                   (D_MODEL, D_MODEL), name=f"dwout_{tag}")
    dgs, dys = merge_bwd(dmerged, gates, ys[0], ys[1], ys[2], tm, f"merge_bwd_{tag}")
    dos, dwbrs = [], []
    for a, (o, (r0, nr)) in enumerate(zip((o_sb, o_ch, o_fox), BR_ROWS)):
        dos.append(matmul(
            NT, dys[a], wbr, _sds((t, nr * 256), BF16), (nt, nr, 1), a_full,
            pl.BlockSpec((None, 256, D_MODEL), functools.partial(lambda i, j, k, r0: (layer, r0 + j, 0), r0=r0)),
            pl.BlockSpec((tm, 256), lambda i, j, k: (i, j)), None, name=f"dbranch{a}_{tag}"))
        dwbrs.append(matmul(
            TN, o, dys[a], _sds((nr * 256, D_MODEL), BF16), (nr, 1, nt),
            pl.BlockSpec((tm, 256), lambda i, j, k: (k, i)), red_row,
            pl.BlockSpec((256, D_MODEL), lambda i, j, k: (i, 0)), (256, D_MODEL), name=f"dwbr{a}_{tag}"))
    dq_a, dk_a, dv_a = sb_bwd(qkv, dos[0], f"sb_bwd_{tag}")
    dq_b, dk_b, dv_b, dbias = chunk_bwd(qkv, bias, dos[1], f"chunk_bwd_{tag}")
    dq_c, dk_c, dv_c, dfrow = fox_bwd(qkv, fcol, frow, o_fox, lse, dos[2], f"fox_bwd_{tag}")
    df = forget_cumsum_bwd(_frow_from_groups(dfrow), f, f"fcum_bwd_{tag}")
    dqkv = jnp.concatenate([p.astype(BF16) for p in
                            (dq_a, dk_a, dv_a, dq_b, dk_b, dv_b, dq_c, dk_c, dv_c)], axis=1)
    dgates = jnp.concatenate(dgs, axis=1)
    dtab = rel_bias_scatter(dbias.reshape(N_HEADS_CH, QB * CH_KEYS), f"rel_scatter_{tag}")

    wide_b = pl.BlockSpec((tm, D_MODEL), lambda i, j, k: (k, j))
    wide_o = pl.BlockSpec((D_MODEL, D_MODEL), lambda i, j, k: (0, j))
    wide_cs = pl.BlockSpec((1, D_MODEL), lambda i, j, k: (0, j))
    dwqkv, dbq = matmul(TN, hm, dqkv, _sds((D_MODEL, QKV_WIDTH), BF16), (1, 3, nt), red_row, wide_b,
                        wide_o, (D_MODEL, D_MODEL), name=f"dwqkv_{tag}",
                        colsum_sds=_sds((1, QKV_WIDTH), F32), colsum_spec=wide_cs)
    dwgate, dbg = matmul(TN, hm, dgates, _sds((D_MODEL, 3 * D_MODEL), BF16), (1, 3, nt), red_row,
                         wide_b, wide_o, (D_MODEL, D_MODEL), name=f"dwgate_{tag}",
                         colsum_sds=_sds((1, 3 * D_MODEL), F32), colsum_spec=wide_cs)
    dwf, dbf = matmul(TN, hm, df, _sds((D_MODEL, LANES), BF16), (1, 1, nt), red_row,
                      pl.BlockSpec((tm, LANES), lambda i, j, k: (k, 0)),
                      pl.BlockSpec((D_MODEL, LANES), lambda i, j, k: (0, 0)), (D_MODEL, LANES),
                      name=f"dwf_{tag}", colsum_sds=_sds((1, LANES), F32),
                      colsum_spec=pl.BlockSpec((1, LANES), lambda i, j, k: (0, 0)))
    wide_a = pl.BlockSpec((tm, D_MODEL), lambda i, j, k: (i, k))
    dhm = matmul(NT, dqkv, wqkv, _sds((t, D_MODEL), F32), (nt, 1, 3), wide_a,
                 pl.BlockSpec((None, D_MODEL, D_MODEL), lambda i, j, k: (layer, 0, k)), a_full,
                 (tm, D_MODEL), name=f"dhm_qkv_{tag}")
    dhm = matmul(NT, dgates, wgate, _sds((t, D_MODEL), F32), (nt, 1, 3), wide_a,
                 pl.BlockSpec((None, D_MODEL, D_MODEL), lambda i, j, k: (DEPTH + layer, 0, k)), a_full,
                 (tm, D_MODEL), name=f"dhm_gate_{tag}", res=dhm, res_spec=a_full)
    dhm = matmul(NT, df, wf, _sds((t, D_MODEL), F32), (nt, 1, 1),
                 pl.BlockSpec((tm, LANES), lambda i, j, k: (i, 0)),
                 pl.BlockSpec((None, D_MODEL, LANES), lambda i, j, k: (layer, 0, 0)), a_full, None,
                 name=f"dhm_f_{tag}", res=dhm, res_spec=a_full)
    dx, dgain = rmsnorm_bwd(x, gain, dhm, dy, tm, f"rms_bwd_{tag}")
    dwbr = jnp.concatenate(dwbrs, axis=0)
    grads = dict(dwqkv=dwqkv, dwgate=dwgate, dwf=dwf, dwbr=dwbr, dwout=dwout,
                 dbq=dbq, dbg=dbg, dbf=dbf, dtab=dtab, dgain=dgain)
    return dx, grads


def _pack_small(pieces):
    flat = jnp.concatenate([p.reshape(-1).astype(F32) for p in pieces])
    flat = jnp.pad(flat, (0, SMALL_ROWS * LANES - flat.shape[0]))
    return flat.reshape(SMALL_ROWS, LANES)


def _unpack_small(packed, shapes):
    flat = packed.reshape(-1)
    out, pos = [], 0
    for shp in shapes:
        n = int(np.prod(shp))
        out.append(flat[pos:pos + n].reshape(shp))
        pos += n
    return out


def kernel(x, g_ffn1, w_ffn1_in, w_ffn1_out, g_mix, w_in, b_in, rel_bias, w_br_sb, w_br_ch, w_br_fox, w_out, g_ffn2, w_ffn2_in, w_ffn2_out, g_final, loss_target, m_g_ffn1, m_w_ffn1_in, m_w_ffn1_out, m_g_mix, m_w_in, m_b_in, m_rel_bias, m_w_br_sb, m_w_br_ch, m_w_br_fox, m_w_out, m_g_ffn2, m_w_ffn2_in, m_w_ffn2_out, m_g_final, v_g_ffn1, v_w_ffn1_in, v_w_ffn1_out, v_g_mix, v_w_in, v_b_in, v_rel_bias, v_w_br_sb, v_w_br_ch, v_w_br_fox, v_w_out, v_g_ffn2, v_w_ffn2_in, v_w_ffn2_out, v_g_final):
    t = x.shape[1]
    tm = min(512, t)
    xs = x[0]
    target = loss_target[0]
    f_lo, f_hi = QKV_WIDTH, QKV_WIDTH + N_HEADS_FOX

    sa = jnp.concatenate([w_ffn1_in, w_ffn2_in], axis=0).astype(BF16)
    sb = jnp.concatenate([w_ffn1_out, w_ffn2_out], axis=0).astype(BF16)
    sc = jnp.concatenate([w_in[:, :, :QKV_WIDTH], w_in[:, :, f_hi:]], axis=0).astype(BF16)
    sf = jnp.pad(w_in[:, :, f_lo:f_hi], ((0, 0), (0, 0), (0, LANES - N_HEADS_FOX))).astype(BF16)
    so = w_out.astype(BF16)
    sbr = jnp.concatenate([w_br_sb, w_br_ch, w_br_fox], axis=1).astype(BF16)

    wa = all_gather(sa, "gather_ffn_in")
    wb = all_gather(sb, "gather_ffn_out").reshape(4, 4, FF_BLK, D_MODEL)
    wc = all_gather(sc, "gather_w_in").reshape(4, D_MODEL, QKV_WIDTH)
    wqkv = wgate = wc
    wf = all_gather(sf, "gather_w_f").reshape(DEPTH, D_MODEL, LANES)
    wout = all_gather(so, "gather_w_out").reshape(DEPTH, D_MODEL, D_MODEL)
    wbr = all_gather(sbr, "gather_w_br")
    wbr = wbr.transpose(0, 2, 1, 3).reshape(DEPTH, D_MODEL, D_MODEL)

    bq = b_in[:, None, :QKV_WIDTH]
    bf = jnp.pad(b_in[:, f_lo:f_hi], ((0, 0), (0, LANES - N_HEADS_FOX)))[:, None, :]
    bg = b_in[:, None, f_hi:]
    tab_t = jnp.pad(rel_bias.transpose(0, 2, 1), ((0, 0), (0, 0), (0, REL_PAD - N_REL)))

    h = xs
    saved = []
    for l in range(DEPTH):
        bias = rel_bias_build(tab_t[l], f"rel_build_l{l}").reshape(N_HEADS_CH, QB, CH_KEYS)
        x0 = h
        x1, s1 = _ffn_fwd(x0, g_ffn1[l:l + 1], wa, wb, l, tm, f"ffn1_l{l}")
        x2, sm = _mixer_fwd(x1, g_mix[l:l + 1], wqkv, wf, wgate, wbr, wout, bq[l], bf[l], bg[l], bias,
                            l, tm, f"mix_l{l}")
        x3, s2 = _ffn_fwd(x2, g_ffn2[l:l + 1], wa, wb, DEPTH + l, tm, f"ffn2_l{l}")
        saved.append((x0, x1, x2, s1, sm, s2, bias))
        h = x3

    dx, dg_final, loss_blk = loss_head(h, g_final[None, :], target, tm, "loss_head")

    g_wa = [None] * 4
    g_wb = [None] * 4
    g_mix_l = [None] * DEPTH
    dgains = {}
    for l in reversed(range(DEPTH)):
        x0, x1, x2, s1, sm, s2, bias = saved[l]
        dx, dgains[("ffn2", l)], g_wa[DEPTH + l], g_wb[DEPTH + l] = _ffn_bwd(
            dx, x2, g_ffn2[l:l + 1], s2, wa, wb, DEPTH + l, tm, f"ffn2_l{l}")
        dx, g_mix_l[l] = _mixer_bwd(dx, x1, g_mix[l:l + 1], sm, wqkv, wf, wgate, wbr, wout, bias, l, tm,
                                    f"mix_l{l}")
        dx, dgains[("ffn1", l)], g_wa[l], g_wb[l] = _ffn_bwd(
            dx, x0, g_ffn1[l:l + 1], s1, wa, wb, l, tm, f"ffn1_l{l}")

    ga = all_to_all(jnp.stack(g_wa), "scatter_ffn_in")
    gb = all_to_all(jnp.stack(g_wb).reshape(4, N_DEV, D_FF // N_DEV, D_MODEL), "scatter_ffn_out")
    gc = all_to_all(jnp.stack([g_mix_l[0]["dwqkv"], g_mix_l[1]["dwqkv"],
                               g_mix_l[0]["dwgate"], g_mix_l[1]["dwgate"]]
                              ).reshape(4, N_DEV, LANES, QKV_WIDTH), "scatter_w_in")
    gf = all_to_all(jnp.stack([g_mix_l[l]["dwf"] for l in range(DEPTH)]
                              ).reshape(DEPTH, N_DEV, LANES, LANES), "scatter_w_f")
    go = all_to_all(jnp.stack([g_mix_l[l]["dwout"] for l in range(DEPTH)]
                              ).reshape(DEPTH, N_DEV, LANES, D_MODEL), "scatter_w_out")
    gbr = all_to_all(jnp.stack([g_mix_l[l]["dwbr"] for l in range(DEPTH)]
                               ).reshape(DEPTH, D_MODEL, N_DEV, LANES).transpose(0, 2, 1, 3),
                     "scatter_w_br")

    small_shapes = []
    small_pieces = []
    small_w, small_m, small_v = [], [], []

    def add_small(piece, w, m, v):
        small_shapes.append(w.shape)
        small_pieces.append(piece)
        small_w.append(w); small_m.append(m); small_v.append(v)

    dg1 = jnp.concatenate([dgains[("ffn1", l)] for l in range(DEPTH)], axis=0)
    dgm = jnp.concatenate([g_mix_l[l]["dgain"] for l in range(DEPTH)], axis=0)
    dg2 = jnp.concatenate([dgains[("ffn2", l)] for l in range(DEPTH)], axis=0)
    db = jnp.stack([jnp.concatenate([g_mix_l[l]["dbq"][0], g_mix_l[l]["dbf"][0, :N_HEADS_FOX],
                                     g_mix_l[l]["dbg"][0]]) for l in range(DEPTH)])
    drel = jnp.stack([g_mix_l[l]["dtab"][:, :N_REL].T for l in range(DEPTH)])
    add_small(dg1, g_ffn1, m_g_ffn1, v_g_ffn1)
    add_small(dgm, g_mix, m_g_mix, v_g_mix)
    add_small(db, b_in, m_b_in, v_b_in)
    add_small(drel, rel_bias, m_rel_bias, v_rel_bias)
    add_small(dg2, g_ffn2, m_g_ffn2, v_g_ffn2)
    add_small(dg_final[0], g_final, m_g_final, v_g_final)
    loss_piece = loss_blk[0, 0:1]
    small_sum = all_reduce_small(_pack_small(small_pieces + [loss_piece]), "allreduce_small")
    n_small = sum(int(np.prod(s)) for s in small_shapes)
    loss = small_sum.reshape(-1)[n_small]

    sm_spec = pl.BlockSpec((SMALL_ROWS, LANES), lambda i: (0, 0))
    sm_out = adamw(small_sum[None], _pack_small(small_w), _pack_small(small_m), _pack_small(small_v),
                   (1,), pl.BlockSpec((1, SMALL_ROWS, LANES), lambda i: (0, 0, 0)), sm_spec, "adamw_small")
    sm_g, sm_d, sm_m, sm_v = [_unpack_small(o, small_shapes) for o in sm_out]

    def upd(parts, s0, w, m, v, tr, name, rb0=0):
        _, r, c = w.shape
        nr = r // tr
        return adamw(parts, w, m, v, (DEPTH, nr),
                     pl.BlockSpec((N_DEV, None, tr, c), lambda l, i: (0, s0 + l, rb0 + i, 0)),
                     pl.BlockSpec((None, tr, c), lambda l, i: (l, i, 0)), name)

    r_ffn1_in = upd(ga, 0, w_ffn1_in, m_w_ffn1_in, v_w_ffn1_in, 256, "adamw_ffn1_in")
    r_ffn2_in = upd(ga, DEPTH, w_ffn2_in, m_w_ffn2_in, v_w_ffn2_in, 256, "adamw_ffn2_in")
    out_rows = D_FF // N_DEV // 2
    r_ffn1_out = upd(gb, 0, w_ffn1_out, m_w_ffn1_out, v_w_ffn1_out, out_rows, "adamw_ffn1_out")
    r_ffn2_out = upd(gb, DEPTH, w_ffn2_out, m_w_ffn2_out, v_w_ffn2_out, out_rows, "adamw_ffn2_out")
    r_out = upd(go, 0, w_out, m_w_out, v_w_out, LANES, "adamw_w_out")
    r_br_sb = upd(gbr, 0, w_br_sb, m_w_br_sb, v_w_br_sb, 256, "adamw_br_sb", rb0=0)
    r_br_ch = upd(gbr, 0, w_br_ch, m_w_br_ch, v_w_br_ch, 256, "adamw_br_ch", rb0=1)
    r_br_fox = upd(gbr, 0, w_br_fox, m_w_br_fox, v_w_br_fox, 256, "adamw_br_fox", rb0=3)

    gc_sum = sum_parts(gc, (4,), pl.BlockSpec((N_DEV, None, LANES, QKV_WIDTH), lambda s: (0, s, 0, 0)),
                       pl.BlockSpec((None, LANES, QKV_WIDTH), lambda s: (s, 0, 0)),
                       _sds((4, LANES, QKV_WIDTH), F32), "sum_w_in")
    gf_sum = sum_parts(gf, (DEPTH,), pl.BlockSpec((N_DEV, None, LANES, LANES), lambda s: (0, s, 0, 0)),
                       pl.BlockSpec((None, LANES, LANES), lambda s: (s, 0, 0)),
                       _sds((DEPTH, LANES, LANES), F32), "sum_w_f")
    g_w_in = jnp.concatenate([gc_sum[:DEPTH], gf_sum[:, :, :N_HEADS_FOX], gc_sum[DEPTH:]], axis=2)
    win_rows = 32
    win_spec = pl.BlockSpec((None, win_rows, w_in.shape[2]), lambda l, i: (l, i, 0))
    r_in = adamw(g_w_in[None], w_in, m_w_in, v_w_in, (DEPTH, LANES // win_rows),
                 pl.BlockSpec((1, None, win_rows, w_in.shape[2]), lambda l, i: (0, l, i, 0)), win_spec,
                 "adamw_w_in")

    def per_kind(k):
        small = (sm_g, sm_d, sm_m, sm_v)[k]
        return [small[0], r_ffn1_in[k], r_ffn1_out[k], small[1], r_in[k], small[2], small[3],
                r_br_sb[k], r_br_ch[k], r_br_fox[k], r_out[k], small[4], r_ffn2_in[k], r_ffn2_out[k],
                small[5]]

    return (loss, dx[None], *per_kind(0), *per_kind(1), *per_kind(2), *per_kind(3))
```

```python
import functools

import numpy as np
import jax
import jax.numpy as jnp
from jax import lax
from jax.experimental import pallas as pl
from jax.experimental.pallas import tpu as pltpu

F32 = jnp.float32
BF16 = jnp.bfloat16

N_DEV = 8
D_MODEL = 1024
DEPTH = 2
HEAD_DIM = 64
W_SB, W_CH, W_FOX = 256, 512, 256
QKV_WIDTH = 3 * (W_SB + W_CH + W_FOX)
N_HEADS_FOX = 4
N_HEADS_CH = 8
D_FF = 2816
FF_BLK = 2 * D_FF // N_DEV
CHUNK = 64
LEFT_CHUNKS = 8
MAX_REL = 128
N_REL = 2 * MAX_REL + 1
REL_PAD = 384
QB = 128
KB = 512
KSUB = KB // QB
CH_WIN = 5
CH_KEYS = CH_WIN * QB
RMS_EPS = 1e-6
NEG = -1e30
SCALE = HEAD_DIM ** -0.5
LANES = 128
VMEM_LIMIT = 56 * 1024 * 1024

ADAM_LR, ADAM_B1, ADAM_B2, ADAM_EPS, ADAM_WD, ADAM_STEP = 0.001, 0.9, 0.999, 1e-08, 0.01, 10

SMALL_ROWS = 192

MESH = pl.DeviceIdType.MESH
ANY = pl.BlockSpec(memory_space=pl.ANY)
HIGHEST = lax.Precision.HIGHEST

NN = (((1,), (0,)), ((), ()))
NT = (((1,), (1,)), ((), ()))
TN = (((0,), (0,)), ((), ()))


def _cparams(n_grid):
    return pltpu.CompilerParams(dimension_semantics=("arbitrary",) * n_grid,
                                vmem_limit_bytes=VMEM_LIMIT)


def _sds(shape, dtype):
    return jax.ShapeDtypeStruct(tuple(shape), dtype)


def _my_index():
    return 4 * lax.axis_index("x") + 2 * lax.axis_index("y") + lax.axis_index("c")


def _peer(mask):
    x, y, c = lax.axis_index("x"), lax.axis_index("y"), lax.axis_index("c")
    px = x ^ ((mask >> 2) & 1)
    py = y ^ ((mask >> 1) & 1)
    pc = c ^ (mask & 1)
    return (px, py, pc), 4 * px + 2 * py + pc


def all_gather(shard, name):
    s, r, c = shard.shape

    def body(in_ref, out_ref, send_sems, recv_sems, local_sem):
        me = _my_index()
        mine = pltpu.make_async_copy(in_ref, out_ref.at[:, me], local_sem)
        mine.start()
        sends = []
        for mask in range(1, N_DEV):
            peer, _ = _peer(mask)
            cp = pltpu.make_async_remote_copy(
                src_ref=in_ref, dst_ref=out_ref.at[:, me],
                send_sem=send_sems.at[mask - 1], recv_sem=recv_sems.at[mask - 1],
                device_id=peer, device_id_type=MESH)
            cp.start()
            sends.append(cp)
        for mask in range(1, N_DEV):
            peer, pidx = _peer(mask)
            pltpu.make_async_remote_copy(
                src_ref=in_ref, dst_ref=out_ref.at[:, pidx],
                send_sem=send_sems.at[mask - 1], recv_sem=recv_sems.at[mask - 1],
                device_id=peer, device_id_type=MESH).wait_recv()
        for cp in sends:
            cp.wait_send()
        mine.wait()

    return pl.pallas_call(
        body, name=name,
        out_shape=_sds((s, N_DEV, r, c), shard.dtype),
        in_specs=[ANY], out_specs=ANY,
        scratch_shapes=[pltpu.SemaphoreType.DMA((N_DEV - 1,)),
                        pltpu.SemaphoreType.DMA((N_DEV - 1,)),
                        pltpu.SemaphoreType.DMA],
    )(shard)


def all_to_all(parts, name):
    s, _, r, c = parts.shape

    def body(in_ref, out_ref, send_sems, recv_sems, local_sem):
        me = _my_index()
        mine = pltpu.make_async_copy(in_ref.at[:, me], out_ref.at[me], local_sem)
        mine.start()
        sends = []
        for mask in range(1, N_DEV):
            peer, pidx = _peer(mask)
            cp = pltpu.make_async_remote_copy(
                src_ref=in_ref.at[:, pidx], dst_ref=out_ref.at[me],
                send_sem=send_sems.at[mask - 1], recv_sem=recv_sems.at[mask - 1],
                device_id=peer, device_id_type=MESH)
            cp.start()
            sends.append(cp)
        for mask in range(1, N_DEV):
            peer, pidx = _peer(mask)
            pltpu.make_async_remote_copy(
                src_ref=in_ref.at[:, me], dst_ref=out_ref.at[pidx],
                send_sem=send_sems.at[mask - 1], recv_sem=recv_sems.at[mask - 1],
                device_id=peer, device_id_type=MESH).wait_recv()
        for cp in sends:
            cp.wait_send()
        mine.wait()

    return pl.pallas_call(
        body, name=name,
        out_shape=_sds((N_DEV, s, r, c), parts.dtype),
        in_specs=[ANY], out_specs=ANY,
        scratch_shapes=[pltpu.SemaphoreType.DMA((N_DEV - 1,)),
                        pltpu.SemaphoreType.DMA((N_DEV - 1,)),
                        pltpu.SemaphoreType.DMA],
    )(parts)


HBM_SPEC = pl.BlockSpec(memory_space=pltpu.HBM)
SEM_SPEC = pl.BlockSpec(memory_space=pltpu.SEMAPHORE)
EFFECT = pltpu.SideEffectType.DATAFLOW_SIDE_EFFECTING


def _exchange_refs(mode, in_ref, land_ref, me, pidx):
    if mode == "gather":
        return in_ref, land_ref.at[:, me], land_ref.at[:, pidx]
    return in_ref.at[:, pidx], land_ref.at[me], land_ref.at[pidx]


def _landing(mode, a, me):
    if mode == "gather":
        s, r, c = a.shape
        return lax.dynamic_update_slice(lax.empty((s, N_DEV, r, c), a.dtype), a[:, None], (0, me, 0, 0))
    s, _, r, c = a.shape
    own = lax.dynamic_index_in_dim(a, me, axis=1, keepdims=False)
    return lax.dynamic_update_slice(lax.empty((N_DEV, s, r, c), a.dtype), own[None], (me, 0, 0, 0))


def exchange_start(mode, arrays, name):
    n = len(arrays)
    me = _my_index()
    lands0 = [_landing(mode, a, me) for a in arrays]

    def body(*refs):
        in_refs, land_refs = refs[:n], refs[n:2 * n]
        send_sems, recv_sems, token = refs[2 * n], refs[2 * n + 1], refs[-1]
        mine = _my_index()
        for k in range(n):
            for mask in range(1, N_DEV):
                peer, pidx = _peer(mask)
                src, dst, _ = _exchange_refs(mode, in_refs[k], land_refs[k], mine, pidx)
                sem = k * (N_DEV - 1) + mask - 1
                pltpu.make_async_remote_copy(
                    src_ref=src, dst_ref=dst, send_sem=send_sems.at[sem], recv_sem=recv_sems.at[sem],
                    device_id=peer, device_id_type=MESH).start()
        token[...] = jnp.zeros_like(token)

    nsem = n * (N_DEV - 1)
    outs = pl.pallas_call(
        body, name=name,
        out_shape=(pltpu.SemaphoreType.DMA((nsem,)), pltpu.SemaphoreType.DMA((nsem,)),
                   *[pltpu.HBM(a.shape, a.dtype) for a in arrays],
                   *[pltpu.HBM(l.shape, l.dtype) for l in lands0], _sds((8, LANES), F32)),
        in_specs=[HBM_SPEC] * (2 * n),
        out_specs=(SEM_SPEC, SEM_SPEC, *[HBM_SPEC] * (2 * n), pl.BlockSpec(memory_space=pltpu.VMEM)),
        input_output_aliases={k: 2 + k for k in range(2 * n)},
        compiler_params=pltpu.CompilerParams(has_side_effects=EFFECT),
    )(*[pltpu.with_memory_space_constraint(a, pltpu.HBM) for a in arrays],
      *[pltpu.with_memory_space_constraint(l, pltpu.HBM) for l in lands0])
    return dict(mode=mode, n=n, send=outs[0], recv=outs[1], ins=outs[2:2 + n],
                lands=outs[2 + n:2 + 2 * n], token=outs[-1])


def exchange_wait(handle, after, name):
    n, mode = handle["n"], handle["mode"]

    def body(*refs):
        in_refs, land_refs = refs[:n], refs[n:2 * n]
        send_sems, recv_sems = refs[2 * n], refs[2 * n + 1]
        mine = _my_index()
        for k in range(n):
            for mask in range(1, N_DEV):
                peer, pidx = _peer(mask)
                src, _, here = _exchange_refs(mode, in_refs[k], land_refs[k], mine, pidx)
                sem = k * (N_DEV - 1) + mask - 1
                cp = pltpu.make_async_remote_copy(
                    src_ref=src, dst_ref=here, send_sem=send_sems.at[sem], recv_sem=recv_sems.at[sem],
                    device_id=peer, device_id_type=MESH)
                cp.wait_send()
                cp.wait_recv()

    thru = (*handle["ins"], *handle["lands"])
    outs = pl.pallas_call(
        body, name=name,
        out_shape=tuple(pltpu.HBM(a.shape, a.dtype) for a in thru),
        in_specs=[HBM_SPEC] * (2 * n) + [SEM_SPEC, SEM_SPEC, ANY],
        out_specs=tuple([HBM_SPEC] * (2 * n)),
        input_output_aliases={k: k for k in range(2 * n)},
        compiler_params=pltpu.CompilerParams(has_side_effects=EFFECT),
    )(*thru, handle["send"], handle["recv"], after)
    return list(outs[n:])


def all_reduce_small(packed, name):
    rows = packed.shape[0]

    def body(in_ref, out_ref, slots, send_sems, recv_sems):
        me = _my_index()
        sends = []
        for mask in range(1, N_DEV):
            peer, _ = _peer(mask)
            cp = pltpu.make_async_remote_copy(
                src_ref=in_ref, dst_ref=slots.at[me],
                send_sem=send_sems.at[mask - 1], recv_sem=recv_sems.at[mask - 1],
                device_id=peer, device_id_type=MESH)
            cp.start()
            sends.append(cp)
        slots[me] = in_ref[...]
        for mask in range(1, N_DEV):
            peer, pidx = _peer(mask)
            pltpu.make_async_remote_copy(
                src_ref=in_ref, dst_ref=slots.at[pidx],
                send_sem=send_sems.at[mask - 1], recv_sem=recv_sems.at[mask - 1],
                device_id=peer, device_id_type=MESH).wait_recv()
        for cp in sends:
            cp.wait_send()
        total = slots[0]
        for p in range(1, N_DEV):
            total = total + slots[p]
        out_ref[...] = total

    return pl.pallas_call(
        body, name=name,
        out_shape=_sds((rows, LANES), F32),
        in_specs=[pl.BlockSpec(memory_space=pltpu.VMEM)],
        out_specs=pl.BlockSpec(memory_space=pltpu.VMEM),
        scratch_shapes=[pltpu.VMEM((N_DEV, rows, LANES), F32),
                        pltpu.SemaphoreType.DMA((N_DEV - 1,)),
                        pltpu.SemaphoreType.DMA((N_DEV - 1,))],
    )(packed)


def matmul(dims, a, b, out_sds, grid, a_spec, b_spec, o_spec, acc_shape, *, name, alpha=1.0,
           bias=None, bias_spec=None, res=None, res_spec=None, colsum_sds=None, colsum_spec=None,
           deps=()):
    nk = grid[2]
    has_bias, has_res, has_cs = bias is not None, res is not None, colsum_sds is not None
    if has_cs:
        assert grid[0] == 1 and dims == TN

    def body(*refs):
        a_ref, b_ref = refs[0], refs[1]
        pos = 2
        bias_ref = res_ref = cs_ref = None
        if has_bias:
            bias_ref = refs[pos]; pos += 1
        if has_res:
            res_ref = refs[pos]; pos += 1
        pos += len(deps)
        o_ref = refs[pos]; pos += 1
        if has_cs:
            cs_ref = refs[pos]; pos += 1
        k = pl.program_id(2)
        bval = b_ref[...]
        part = lax.dot_general(a_ref[...].astype(BF16), bval.astype(BF16), dims,
                               preferred_element_type=F32)

        def finish(total):
            r = total * alpha if alpha != 1.0 else total
            if has_bias:
                r = r + bias_ref[...]
            if has_res:
                r = r + res_ref[...].astype(F32)
            o_ref[...] = r.astype(o_ref.dtype)

        if has_cs:
            csum = jnp.sum(bval.astype(F32), axis=0, keepdims=True)

            @pl.when(k == 0)
            def _():
                cs_ref[...] = csum

            @pl.when(k > 0)
            def _():
                cs_ref[...] += csum

        if nk == 1:
            finish(part)
        else:
            acc_ref = refs[pos]

            @pl.when(k == 0)
            def _():
                acc_ref[...] = part

            @pl.when(k > 0)
            def _():
                acc_ref[...] += part

            @pl.when(k == nk - 1)
            def _():
                finish(acc_ref[...])

    in_specs, args = [a_spec, b_spec], [a, b]
    if has_bias:
        in_specs.append(bias_spec); args.append(bias)
    if has_res:
        in_specs.append(res_spec); args.append(res)
    in_specs += [ANY] * len(deps)
    args += list(deps)
    out_shape, out_specs = [out_sds], [o_spec]
    if has_cs:
        out_shape.append(colsum_sds); out_specs.append(colsum_spec)
    scratch = [] if nk == 1 else [pltpu.VMEM(acc_shape, F32)]
    outs = pl.pallas_call(
        body, name=name, grid=grid, in_specs=in_specs, out_specs=out_specs, out_shape=out_shape,
        scratch_shapes=scratch, compiler_params=_cparams(3))(*args)
    return outs if has_cs else outs[0]


def _sigmoid(z):
    return 1.0 / (1.0 + jnp.exp(-z))


def _log_sigmoid(z):
    return jnp.minimum(z, 0.0) - jnp.log(1.0 + jnp.exp(-jnp.abs(z)))


def rmsnorm_fwd(x, gain, tm, name, deps=()):
    t, d = x.shape

    def body(x_ref, g_ref, *rest):
        o_ref = rest[-1]
        xf = x_ref[...]
        r = lax.rsqrt(jnp.mean(xf * xf, axis=-1, keepdims=True) + RMS_EPS)
        o_ref[...] = (xf * r * g_ref[...]).astype(o_ref.dtype)

    return pl.pallas_call(
        body, name=name, grid=(t // tm,),
        in_specs=[pl.BlockSpec((tm, d), lambda i: (i, 0)), pl.BlockSpec((1, d), lambda i: (0, 0))]
        + [ANY] * len(deps),
        out_specs=pl.BlockSpec((tm, d), lambda i: (i, 0)),
        out_shape=_sds((t, d), BF16), compiler_params=_cparams(1))(x, gain, *deps)


def rmsnorm_bwd(x, gain, dh, dres, tm, name):
    t, d = x.shape

    def body(x_ref, g_ref, dh_ref, dres_ref, dx_ref, dg_ref):
        i = pl.program_id(0)
        xf = x_ref[...]
        r = lax.rsqrt(jnp.mean(xf * xf, axis=-1, keepdims=True) + RMS_EPS)
        xhat = xf * r
        dh_v = dh_ref[...]
        dxhat = dh_v * g_ref[...]
        dx = r * (dxhat - xhat * jnp.mean(dxhat * xhat, axis=-1, keepdims=True))
        dx_ref[...] = dres_ref[...] + dx
        dg = jnp.sum(dh_v * xhat, axis=0, keepdims=True)

        @pl.when(i == 0)
        def _():
            dg_ref[...] = dg

        @pl.when(i > 0)
        def _():
            dg_ref[...] += dg

    row = pl.BlockSpec((tm, d), lambda i: (i, 0))
    vec = pl.BlockSpec((1, d), lambda i: (0, 0))
    return pl.pallas_call(
        body, name=name, grid=(t // tm,), in_specs=[row, vec, row, row], out_specs=[row, vec],
        out_shape=[_sds((t, d), F32), _sds((1, d), F32)], compiler_params=_cparams(1))(x, gain, dh, dres)


def loss_head(x, gain, target, tm, name):
    t, d = x.shape

    def body(x_ref, g_ref, tgt_ref, dx_ref, dg_ref, loss_ref):
        i = pl.program_id(0)
        xf = x_ref[...]
        g = g_ref[...]
        r = lax.rsqrt(jnp.mean(xf * xf, axis=-1, keepdims=True) + RMS_EPS)
        xhat = xf * r
        err = xhat * g - tgt_ref[...]
        part = 0.5 * jnp.sum(jnp.mean(err * err, axis=-1, keepdims=True))
        dy = err * (1.0 / d)
        dxhat = dy * g
        dx_ref[...] = r * (dxhat - xhat * jnp.mean(dxhat * xhat, axis=-1, keepdims=True))
        dg = jnp.sum(dy * xhat, axis=0, keepdims=True)
        lpart = jnp.full((8, LANES), part, F32)

        @pl.when(i == 0)
        def _():
            dg_ref[...] = dg
            loss_ref[...] = lpart

        @pl.when(i > 0)
        def _():
            dg_ref[...] += dg
            loss_ref[...] += lpart

    row = pl.BlockSpec((tm, d), lambda i: (i, 0))
    vec = pl.BlockSpec((1, d), lambda i: (0, 0))
    return pl.pallas_call(
        body, name=name, grid=(t // tm,), in_specs=[row, vec, row],
        out_specs=[row, vec, pl.BlockSpec((8, LANES), lambda i: (0, 0))],
        out_shape=[_sds((t, d), F32), _sds((1, d), F32), _sds((8, LANES), F32)],
        compiler_params=_cparams(1))(x, gain, target)


def swiglu_fwd(gu, tm, name):
    _, t, w = gu.shape

    def body(g_ref, u_ref, o_ref):
        g = g_ref[...].astype(F32)
        o_ref[...] = (g * _sigmoid(g) * u_ref[...].astype(F32)).astype(o_ref.dtype)

    return pl.pallas_call(
        body, name=name, grid=(4, t // tm),
        in_specs=[pl.BlockSpec((None, tm, w), lambda j, i: (j, i, 0)),
                  pl.BlockSpec((None, tm, w), lambda j, i: (j + 4, i, 0))],
        out_specs=pl.BlockSpec((None, tm, w), lambda j, i: (j, i, 0)),
        out_shape=_sds((4, t, w), BF16), compiler_params=_cparams(2))(gu, gu)


def swiglu_bwd(gu, dact, tm, name):
    _, t, w = gu.shape

    def body(g_ref, u_ref, da_ref, o_ref):
        j = pl.program_id(0)
        g = g_ref[...].astype(F32)
        da = da_ref[...].astype(F32)
        s = _sigmoid(g)

        @pl.when(j < 4)
        def _():
            o_ref[...] = (da * u_ref[...].astype(F32) * (s * (1.0 + g * (1.0 - s)))).astype(o_ref.dtype)

        @pl.when(j >= 4)
        def _():
            o_ref[...] = (da * g * s).astype(o_ref.dtype)

    return pl.pallas_call(
        body, name=name, grid=(8, t // tm),
        in_specs=[pl.BlockSpec((None, tm, w), lambda j, i: (j % 4, i, 0)),
                  pl.BlockSpec((None, tm, w), lambda j, i: (j % 4 + 4, i, 0)),
                  pl.BlockSpec((None, tm, w), lambda j, i: (j % 4, i, 0))],
        out_specs=pl.BlockSpec((None, tm, w), lambda j, i: (j, i, 0)),
        out_shape=_sds((8, t, w), BF16), compiler_params=_cparams(2))(gu, gu, dact)


def merge_fwd(gates, ya, yb, yc, tm, name):
    t, d = ya.shape

    def body(ga_ref, gb_ref, gc_ref, ya_ref, yb_ref, yc_ref, o_ref):
        m = (_sigmoid(ga_ref[...]) * ya_ref[...] + _sigmoid(gb_ref[...]) * yb_ref[...]
             + _sigmoid(gc_ref[...]) * yc_ref[...])
        o_ref[...] = m.astype(o_ref.dtype)

    row = pl.BlockSpec((tm, d), lambda i: (i, 0))
    gspecs = [pl.BlockSpec((tm, d), functools.partial(lambda i, a: (i, a), a=a)) for a in range(3)]
    return pl.pallas_call(
        body, name=name, grid=(t // tm,), in_specs=gspecs + [row, row, row], out_specs=row,
        out_shape=_sds((t, d), BF16), compiler_params=_cparams(1))(gates, gates, gates, ya, yb, yc)


def merge_bwd(dm, gates, ya, yb, yc, tm, name):
    t, d = ya.shape

    def body(dm_ref, g_ref, y_ref, dg_ref, dy_ref):
        dmv = dm_ref[...]
        s = _sigmoid(g_ref[...])
        dy_ref[...] = (dmv * s).astype(dy_ref.dtype)
        dg_ref[...] = (dmv * y_ref[...] * s * (1.0 - s)).astype(dg_ref.dtype)

    outs = []
    dgs = []
    for a, y in enumerate((ya, yb, yc)):
        row = pl.BlockSpec((tm, d), lambda i: (i, 0))
        gspec = pl.BlockSpec((tm, d), functools.partial(lambda i, a: (i, a), a=a))
        dg, dy = pl.pallas_call(
            functools.partial(body), name=f"{name}_{a}", grid=(t // tm,),
            in_specs=[row, gspec, row], out_specs=[row, row],
            out_shape=[_sds((t, d), BF16), _sds((t, d), BF16)],
            compiler_params=_cparams(1))(dm, gates, y)
        dgs.append(dg)
        outs.append(dy)
    return dgs, outs


def _iota2(shape, dim):
    return lax.broadcasted_iota(jnp.int32, shape, dim)


def forget_cumsum(f, name):
    t = f.shape[0]
    nq = t // QB

    def body(f_ref, fcol_ref, frow_ref, carry):
        j = pl.program_id(0)

        @pl.when(j == 0)
        def _():
            carry[...] = jnp.zeros_like(carry)

        logf = _log_sigmoid(f_ref[...])
        tri = (_iota2((QB, QB), 1) <= _iota2((QB, QB), 0)).astype(F32)
        blk = jnp.dot(tri, logf, precision=HIGHEST, preferred_element_type=F32) + carry[...]
        carry[...] += jnp.sum(logf, axis=0, keepdims=True)
        fcol_ref[...] = blk
        frow_ref[...] = blk.T[0:8, :]

    return pl.pallas_call(
        body, name=name, grid=(nq,),
        in_specs=[pl.BlockSpec((QB, LANES), lambda j: (j, 0))],
        out_specs=[pl.BlockSpec((QB, LANES), lambda j: (j, 0)),
                   pl.BlockSpec((None, 8, QB), lambda j: (j, 0, 0))],
        out_shape=[_sds((t, LANES), F32), _sds((nq, 8, QB), F32)],
        scratch_shapes=[pltpu.VMEM((1, LANES), F32)], compiler_params=_cparams(1))(f)


def forget_cumsum_bwd(dfrow, f, name):
    t = f.shape[0]
    nq = t // QB

    def body(dfr_ref, f_ref, df_ref, carry):
        jj = pl.program_id(0)

        @pl.when(jj == 0)
        def _():
            carry[...] = jnp.zeros_like(carry)

        padded = jnp.concatenate([dfr_ref[...], jnp.zeros((QB - 8, QB), F32)], axis=0)
        dfcol = padded.T
        tri = (_iota2((QB, QB), 1) >= _iota2((QB, QB), 0)).astype(F32)
        dlogf = jnp.dot(tri, dfcol, precision=HIGHEST, preferred_element_type=F32) + carry[...]
        carry[...] += jnp.sum(dfcol, axis=0, keepdims=True)
        df_ref[...] = dlogf * _sigmoid(-f_ref[...])

    return pl.pallas_call(
        body, name=name, grid=(nq,),
        in_specs=[pl.BlockSpec((None, 8, QB), lambda jj: (nq - 1 - jj, 0, 0)),
                  pl.BlockSpec((QB, LANES), lambda jj: (nq - 1 - jj, 0))],
        out_specs=pl.BlockSpec((QB, LANES), lambda jj: (nq - 1 - jj, 0)),
        out_shape=_sds((t, LANES), F32),
        scratch_shapes=[pltpu.VMEM((1, LANES), F32)], compiler_params=_cparams(1))(dfrow, f)


REL_NB = 4096


def _rel_index():
    qi = np.arange(QB)[:, None]
    kj = np.arange(CH_KEYS)[None, :]
    rel = np.clip(qi + (CH_WIN - 1) * QB - kj, -MAX_REL, MAX_REL) + MAX_REL
    return rel.reshape(1, QB * CH_KEYS).astype(np.int32)


def rel_bias_build(tab_t, name):
    n = QB * CH_KEYS
    idx = jnp.asarray(_rel_index())

    def body(tab_ref, idx_ref, o_ref):
        onehot = (_iota2((REL_PAD, REL_NB), 0) == idx_ref[...]).astype(F32)
        o_ref[...] = jnp.dot(tab_ref[...], onehot, precision=HIGHEST, preferred_element_type=F32)

    return pl.pallas_call(
        body, name=name, grid=(n // REL_NB,),
        in_specs=[pl.BlockSpec((8, REL_PAD), lambda i: (0, 0)),
                  pl.BlockSpec((1, REL_NB), lambda i: (0, i))],
        out_specs=pl.BlockSpec((8, REL_NB), lambda i: (0, i)),
        out_shape=_sds((8, n), F32), compiler_params=_cparams(1))(tab_t, idx)


def rel_bias_scatter(dbias, name):
    n = QB * CH_KEYS
    idx = jnp.asarray(_rel_index())

    def body(db_ref, idx_ref, o_ref):
        i = pl.program_id(0)
        onehot = (_iota2((REL_PAD, REL_NB), 0) == idx_ref[...]).astype(F32)
        part = lax.dot_general(db_ref[...], onehot, NT, precision=HIGHEST, preferred_element_type=F32)

        @pl.when(i == 0)
        def _():
            o_ref[...] = part

        @pl.when(i > 0)
        def _():
            o_ref[...] += part

    return pl.pallas_call(
        body, name=name, grid=(n // REL_NB,),
        in_specs=[pl.BlockSpec((8, REL_NB), lambda i: (0, i)),
                  pl.BlockSpec((1, REL_NB), lambda i: (0, i))],
        out_specs=pl.BlockSpec((8, REL_PAD), lambda i: (0, 0)),
        out_shape=_sds((8, REL_PAD), F32), compiler_params=_cparams(1))(dbias, idx)


def _hl(h):
    return slice(h * HEAD_DIM, (h + 1) * HEAD_DIM)


def _split_dot(x, tri_bf16):
    hi = x.astype(BF16)
    lo = (x - hi.astype(F32)).astype(BF16)
    return (jnp.dot(hi, tri_bf16, preferred_element_type=F32)
            + jnp.dot(lo, tri_bf16, preferred_element_type=F32))


def _rows(j):
    return pl.ds(pl.multiple_of(j * QB, QB), QB)


def _krows(g):
    return pl.ds(pl.multiple_of(g * KB, KB), KB)


def _log_sigmoid_pair(z):
    sp = jnp.log(1.0 + jnp.exp(-jnp.abs(z)))
    return jnp.minimum(z, 0.0) - sp, -jnp.maximum(z, 0.0) - sp


def _qkv_specs(t, col0, n_pairs):
    q_spec = pl.BlockSpec((QB, LANES), lambda hp, i: (i, col0 + hp))
    k_spec = pl.BlockSpec((t, LANES), lambda hp, i: (0, col0 + n_pairs + hp))
    v_spec = pl.BlockSpec((t, LANES), lambda hp, i: (0, col0 + 2 * n_pairs + hp))
    return q_spec, k_spec, v_spec


def sb_fwd(qkv, name):
    t = qkv.shape[0]
    nq = t // QB

    def body(q_ref, k_ref, v_ref, o_ref):
        i = pl.program_id(1)
        groups = i // KSUB + 1
        tri_after = (_iota2((KB, KB), 0) > _iota2((KB, KB), 1)).astype(BF16)
        t_idx = i * QB + _iota2((QB, KB), 0)
        qs = [q_ref[:, _hl(h)] for h in range(2)]

        def step(gg, carry):
            g = groups - 1 - gg
            strict = (g * KB + _iota2((QB, KB), 1)) < t_idx
            out = []
            for h in range(2):
                tail, acc = carry[2 * h], carry[2 * h + 1]
                k = k_ref[_krows(g), _hl(h)]
                v = v_ref[_krows(g), _hl(h)]
                z = lax.dot_general(qs[h], k, NT, preferred_element_type=F32) * SCALE
                lb, lf = _log_sigmoid_pair(z)
                lf = jnp.where(strict, lf, 0.0)
                between = _split_dot(lf, tri_after) + tail
                w = jnp.where(strict, jnp.exp(lb + between), 0.0)
                acc = acc + jnp.dot(w.astype(BF16), v, preferred_element_type=F32)
                out += [tail + jnp.sum(lf, axis=1, keepdims=True), acc]
            return tuple(out)

        init = (jnp.zeros((QB, 1), F32), jnp.zeros((QB, HEAD_DIM), F32)) * 2
        res = lax.fori_loop(0, groups, step, init)
        for h in range(2):
            o_ref[:, _hl(h)] = res[2 * h + 1].astype(o_ref.dtype)

    q_spec, k_spec, v_spec = _qkv_specs(t, 0, 2)
    return pl.pallas_call(
        body, name=name, grid=(2, nq), in_specs=[q_spec, k_spec, v_spec],
        out_specs=pl.BlockSpec((QB, LANES), lambda hp, i: (i, hp)),
        out_shape=_sds((t, W_SB), BF16), compiler_params=_cparams(2))(qkv, qkv, qkv)


def sb_bwd(qkv, do, name):
    t = qkv.shape[0]
    nq = t // QB

    def body(q_ref, k_ref, v_ref, do_ref, dq_ref, dk_ref, dv_ref, w_scr):
        i = pl.program_id(1)

        @pl.when(i == 0)
        def _():
            dk_ref[...] = jnp.zeros_like(dk_ref)
            dv_ref[...] = jnp.zeros_like(dv_ref)

        groups = i // KSUB + 1
        tri_after = (_iota2((KB, KB), 0) > _iota2((KB, KB), 1)).astype(BF16)
        tri_before = (_iota2((KB, KB), 0) < _iota2((KB, KB), 1)).astype(BF16)
        t_idx = i * QB + _iota2((QB, KB), 0)
        qs = [q_ref[:, _hl(h)] for h in range(2)]
        dos = [do_ref[:, _hl(h)] for h in range(2)]

        def weights(gg, tails):
            g = groups - 1 - gg
            strict = (g * KB + _iota2((QB, KB), 1)) < t_idx
            out = []
            for h in range(2):
                k = k_ref[_krows(g), _hl(h)]
                z = lax.dot_general(qs[h], k, NT, preferred_element_type=F32) * SCALE
                lb, lf = _log_sigmoid_pair(z)
                lf = jnp.where(strict, lf, 0.0)
                between = _split_dot(lf, tri_after) + tails[h]
                w_scr[h, g] = jnp.where(strict, jnp.exp(lb + between), 0.0)
                out.append(tails[h] + jnp.sum(lf, axis=1, keepdims=True))
            return tuple(out)

        lax.fori_loop(0, groups, weights, (jnp.zeros((QB, 1), F32),) * 2)

        def grads(g, carry):
            strict = (g * KB + _iota2((QB, KB), 1)) < t_idx
            out = []
            for h in range(2):
                head, dq = carry[2 * h], carry[2 * h + 1]
                k = k_ref[_krows(g), _hl(h)]
                v = v_ref[_krows(g), _hl(h)]
                w = w_scr[h, g]
                z = lax.dot_general(qs[h], k, NT, preferred_element_type=F32) * SCALE
                beta = _sigmoid(z)
                e = lax.dot_general(dos[h], v, NT, preferred_element_type=F32) * w
                before = _split_dot(e, tri_before) + head
                dz = jnp.where(strict, e * (1.0 - beta) - before * beta, 0.0) * SCALE
                dzb = dz.astype(BF16)
                dq = dq + jnp.dot(dzb, k, preferred_element_type=F32)
                dk_ref[_krows(g), _hl(h)] += lax.dot_general(dzb, qs[h], TN, preferred_element_type=F32)
                dv_ref[_krows(g), _hl(h)] += lax.dot_general(w.astype(BF16), dos[h], TN,
                                                             preferred_element_type=F32)
                out += [head + jnp.sum(e, axis=1, keepdims=True), dq]
            return tuple(out)

        init = (jnp.zeros((QB, 1), F32), jnp.zeros((QB, HEAD_DIM), F32)) * 2
        res = lax.fori_loop(0, groups, grads, init)
        for h in range(2):
            dq_ref[:, _hl(h)] = res[2 * h + 1].astype(dq_ref.dtype)

    q_spec, k_spec, v_spec = _qkv_specs(t, 0, 2)
    blk = pl.BlockSpec((QB, LANES), lambda hp, i: (i, hp))
    full = pl.BlockSpec((t, LANES), lambda hp, i: (0, hp))
    return pl.pallas_call(
        body, name=name, grid=(2, nq), in_specs=[q_spec, k_spec, v_spec, blk],
        out_specs=[blk, full, full],
        out_shape=[_sds((t, W_SB), BF16), _sds((t, W_SB), F32), _sds((t, W_SB), F32)],
        scratch_shapes=[pltpu.VMEM((2, t // KB, QB, KB), F32)],
        compiler_params=_cparams(2))(qkv, qkv, qkv, do)


def fox_fwd(qkv, fcol, frow, name):
    t = qkv.shape[0]
    nq = t // QB

    def body(q_ref, k_ref, v_ref, fc_ref, fr_ref, o_ref, lse_ref):
        hp = pl.program_id(0)
        i = pl.program_id(1)
        groups = i // KSUB + 1
        t_idx = i * QB + _iota2((QB, KB), 0)
        lane = _iota2((QB, LANES), 1)
        sub = _iota2((8, KB), 0)
        qs = [q_ref[:, _hl(h)] for h in range(2)]
        f_qs = [jnp.sum(jnp.where(lane == hp * 2 + h, fc_ref[...], 0.0), axis=1, keepdims=True)
                for h in range(2)]

        def step(g, carry):
            causal = (g * KB + _iota2((QB, KB), 1)) <= t_idx
            fr = fr_ref[g]
            out = []
            for h in range(2):
                m, l, acc = carry[3 * h:3 * h + 3]
                k = k_ref[_krows(g), _hl(h)]
                v = v_ref[_krows(g), _hl(h)]
                f_k = jnp.sum(jnp.where(sub == hp * 2 + h, fr, 0.0), axis=0, keepdims=True)
                z = lax.dot_general(qs[h], k, NT, preferred_element_type=F32) * SCALE + f_qs[h] - f_k
                z = jnp.where(causal, z, NEG)
                m_new = jnp.maximum(m, jnp.max(z, axis=1, keepdims=True))
                p = jnp.exp(z - m_new)
                corr = jnp.exp(m - m_new)
                l = l * corr + jnp.sum(p, axis=1, keepdims=True)
                acc = acc * corr + jnp.dot(p.astype(BF16), v, preferred_element_type=F32)
                out += [m_new, l, acc]
            return tuple(out)

        init = (jnp.full((QB, 1), NEG, F32), jnp.zeros((QB, 1), F32), jnp.zeros((QB, HEAD_DIM), F32)) * 2
        res = lax.fori_loop(0, groups, step, init)
        for h in range(2):
            m, l, acc = res[3 * h:3 * h + 3]
            o_ref[:, _hl(h)] = (acc / l).astype(o_ref.dtype)
            lse_ref[:, _hl(h)] = jnp.broadcast_to(m + jnp.log(l), (QB, HEAD_DIM))

    q_spec, k_spec, v_spec = _qkv_specs(t, 18, 2)
    blk = pl.BlockSpec((QB, LANES), lambda hp, i: (i, hp))
    return pl.pallas_call(
        body, name=name, grid=(2, nq),
        in_specs=[q_spec, k_spec, v_spec, pl.BlockSpec((QB, LANES), lambda hp, i: (i, 0)),
                  pl.BlockSpec((t // KB, 8, KB), lambda hp, i: (0, 0, 0))],
        out_specs=[blk, blk],
        out_shape=[_sds((t, W_FOX), BF16), _sds((t, W_FOX), F32)],
        compiler_params=_cparams(2))(qkv, qkv, qkv, fcol, frow)


def fox_bwd(qkv, fcol, frow, o, lse, do, name):
    t = qkv.shape[0]
    nq = t // QB

    def body(q_ref, k_ref, v_ref, fc_ref, fr_ref, o_ref, lse_ref, do_ref,
             dq_ref, dk_ref, dv_ref, dfr_ref):
        hp = pl.program_id(0)
        i = pl.program_id(1)

        @pl.when(i == 0)
        def _():
            dk_ref[...] = jnp.zeros_like(dk_ref)
            dv_ref[...] = jnp.zeros_like(dv_ref)

        @pl.when((i == 0) & (hp == 0))
        def _():
            dfr_ref[...] = jnp.zeros_like(dfr_ref)

        groups = i // KSUB + 1
        t_idx = i * QB + _iota2((QB, KB), 0)
        lane = _iota2((QB, LANES), 1)
        sub = _iota2((8, KB), 0)
        qs = [q_ref[:, _hl(h)] for h in range(2)]
        dos = [do_ref[:, _hl(h)] for h in range(2)]
        f_qs = [jnp.sum(jnp.where(lane == hp * 2 + h, fc_ref[...], 0.0), axis=1, keepdims=True)
                for h in range(2)]
        lse_qs = [lse_ref[:, h * HEAD_DIM:h * HEAD_DIM + 1] for h in range(2)]
        deltas = [jnp.sum(dos[h].astype(F32) * o_ref[:, _hl(h)].astype(F32), axis=1, keepdims=True)
                  for h in range(2)]

        def step(g, dqs):
            causal = (g * KB + _iota2((QB, KB), 1)) <= t_idx
            fr = fr_ref[g]
            out = []
            dfr = jnp.zeros((8, KB), F32)
            for h in range(2):
                k = k_ref[_krows(g), _hl(h)]
                v = v_ref[_krows(g), _hl(h)]
                f_k = jnp.sum(jnp.where(sub == hp * 2 + h, fr, 0.0), axis=0, keepdims=True)
                z = lax.dot_general(qs[h], k, NT, preferred_element_type=F32) * SCALE + f_qs[h] - f_k
                p = jnp.where(causal, jnp.exp(z - lse_qs[h]), 0.0)
                dp = lax.dot_general(dos[h], v, NT, preferred_element_type=F32)
                ds = p * (dp - deltas[h])
                dsb = (ds * SCALE).astype(BF16)
                out.append(dqs[h] + jnp.dot(dsb, k, preferred_element_type=F32))
                dk_ref[_krows(g), _hl(h)] += lax.dot_general(dsb, qs[h], TN, preferred_element_type=F32)
                dv_ref[_krows(g), _hl(h)] += lax.dot_general(p.astype(BF16), dos[h], TN,
                                                             preferred_element_type=F32)
                colsum = jnp.sum(ds, axis=0, keepdims=True)
                dfr = dfr + jnp.where(sub == hp * 2 + h, -colsum, 0.0)
            dfr_ref[g] += dfr
            return tuple(out)

        res = lax.fori_loop(0, groups, step, (jnp.zeros((QB, HEAD_DIM), F32),) * 2)
        for h in range(2):
            dq_ref[:, _hl(h)] = res[h].astype(dq_ref.dtype)

    q_spec, k_spec, v_spec = _qkv_specs(t, 18, 2)
    blk = pl.BlockSpec((QB, LANES), lambda hp, i: (i, hp))
    full = pl.BlockSpec((t, LANES), lambda hp, i: (0, hp))
    frs = pl.BlockSpec((t // KB, 8, KB), lambda hp, i: (0, 0, 0))
    return pl.pallas_call(
        body, name=name, grid=(2, nq),
        in_specs=[q_spec, k_spec, v_spec, pl.BlockSpec((QB, LANES), lambda hp, i: (i, 0)), frs,
                  blk, blk, blk],
        out_specs=[blk, full, full, frs],
        out_shape=[_sds((t, W_FOX), BF16), _sds((t, W_FOX), F32), _sds((t, W_FOX), F32),
                   _sds((t // KB, 8, KB), F32)],
        compiler_params=_cparams(2))(qkv, qkv, qkv, fcol, frow, o, lse, do)


def _frow_to_groups(frow):
    n = frow.shape[0] // KSUB
    return frow.reshape(n, KSUB, 8, QB).transpose(0, 2, 1, 3).reshape(n, 8, KB)


def _frow_from_groups(frow):
    n = frow.shape[0]
    return frow.reshape(n, 8, KSUB, QB).transpose(0, 2, 1, 3).reshape(n * KSUB, 8, QB)


def _chunk_valid(i):
    qi = _iota2((QB, CH_KEYS), 0)
    kj = _iota2((QB, CH_KEYS), 1)
    dchunk = (qi >> 6) + LEFT_CHUNKS - (kj >> 6)
    return (dchunk >= 0) & (dchunk <= LEFT_CHUNKS) & ((i - (CH_WIN - 1)) * QB + kj >= 0)


def _chunk_scores(q, k_ref, h, i, bias, valid):
    zs = []
    for b in range(CH_WIN):
        kb = jnp.maximum(i - (CH_WIN - 1) + b, 0)
        k = k_ref[_rows(kb), _hl(h)]
        zs.append(lax.dot_general(q, k, NT, preferred_element_type=F32) * SCALE)
    z = jnp.where(valid, jnp.concatenate(zs, axis=1) + bias, NEG)
    z = z - jnp.max(z, axis=1, keepdims=True)
    p = jnp.exp(z)
    return p / jnp.sum(p, axis=1, keepdims=True)


def chunk_fwd(qkv, bias, name):
    t = qkv.shape[0]
    nq = t // QB

    def body(q_ref, k_ref, v_ref, b_ref, o_ref):
        i = pl.program_id(1)
        valid = _chunk_valid(i)
        for h in range(2):
            q = q_ref[:, _hl(h)]
            p = _chunk_scores(q, k_ref, h, i, b_ref[h], valid)
            acc = jnp.zeros((QB, HEAD_DIM), F32)
            for b in range(CH_WIN):
                kb = jnp.maximum(i - (CH_WIN - 1) + b, 0)
                v = v_ref[_rows(kb), _hl(h)]
                acc = acc + jnp.dot(p[:, b * QB:(b + 1) * QB].astype(BF16), v,
                                    preferred_element_type=F32)
            o_ref[:, _hl(h)] = acc.astype(o_ref.dtype)

    q_spec, k_spec, v_spec = _qkv_specs(t, 6, 4)
    return pl.pallas_call(
        body, name=name, grid=(4, nq),
        in_specs=[q_spec, k_spec, v_spec, pl.BlockSpec((2, QB, CH_KEYS), lambda hp, i: (hp, 0, 0))],
        out_specs=pl.BlockSpec((QB, LANES), lambda hp, i: (i, hp)),
        out_shape=_sds((t, W_CH), BF16), compiler_params=_cparams(2))(qkv, qkv, qkv, bias)


def chunk_bwd(qkv, bias, do, name):
    t = qkv.shape[0]
    nq = t // QB

    def body(q_ref, k_ref, v_ref, b_ref, do_ref, dq_ref, dk_ref, dv_ref, db_ref):
        i = pl.program_id(1)

        @pl.when(i == 0)
        def _():
            dk_ref[...] = jnp.zeros_like(dk_ref)
            dv_ref[...] = jnp.zeros_like(dv_ref)
            db_ref[...] = jnp.zeros_like(db_ref)

        valid = _chunk_valid(i)
        for h in range(2):
            q = q_ref[:, _hl(h)]
            dov = do_ref[:, _hl(h)]
            p = _chunk_scores(q, k_ref, h, i, b_ref[h], valid)
            dps = []
            for b in range(CH_WIN):
                kb = jnp.maximum(i - (CH_WIN - 1) + b, 0)
                dps.append(lax.dot_general(dov, v_ref[_rows(kb), _hl(h)], NT,
                                           preferred_element_type=F32))
            dp = jnp.concatenate(dps, axis=1)
            ds = p * (dp - jnp.sum(p * dp, axis=1, keepdims=True))
            db_ref[h] += ds
            dq = jnp.zeros((QB, HEAD_DIM), F32)
            for b in range(CH_WIN):
                kb = jnp.maximum(i - (CH_WIN - 1) + b, 0)
                dsb = (ds[:, b * QB:(b + 1) * QB] * SCALE).astype(BF16)
                pb = p[:, b * QB:(b + 1) * QB].astype(BF16)
                dq = dq + jnp.dot(dsb, k_ref[_rows(kb), _hl(h)], preferred_element_type=F32)
                dk_ref[_rows(kb), _hl(h)] += lax.dot_general(dsb, q, TN, preferred_element_type=F32)
                dv_ref[_rows(kb), _hl(h)] += lax.dot_general(pb, dov, TN, preferred_element_type=F32)
            dq_ref[:, _hl(h)] = dq.astype(dq_ref.dtype)

    q_spec, k_spec, v_spec = _qkv_specs(t, 6, 4)
    blk = pl.BlockSpec((QB, LANES), lambda hp, i: (i, hp))
    full = pl.BlockSpec((t, LANES), lambda hp, i: (0, hp))
    bspec = pl.BlockSpec((2, QB, CH_KEYS), lambda hp, i: (hp, 0, 0))
    return pl.pallas_call(
        body, name=name, grid=(4, nq), in_specs=[q_spec, k_spec, v_spec, bspec, blk],
        out_specs=[blk, full, full, bspec],
        out_shape=[_sds((t, W_CH), BF16), _sds((t, W_CH), F32), _sds((t, W_CH), F32),
                   _sds((N_HEADS_CH, QB, CH_KEYS), F32)],
        compiler_params=_cparams(2))(qkv, qkv, qkv, bias, do)


def _sum_parts(p_ref):
    total = p_ref[0].astype(F32)
    for p in range(1, p_ref.shape[0]):
        total = total + p_ref[p].astype(F32)
    return total


def sum_parts(parts, grid, p_spec, o_spec, out_sds, name):
    def body(p_ref, o_ref):
        o_ref[...] = _sum_parts(p_ref)

    return pl.pallas_call(body, name=name, grid=grid, in_specs=[p_spec], out_specs=o_spec,
                          out_shape=out_sds, compiler_params=_cparams(len(grid)))(parts)


def adamw(parts, w, m, v, grid, p_specs, w_spec, name):
    c1 = 1.0 / (1.0 - ADAM_B1 ** ADAM_STEP)
    c2 = 1.0 / (1.0 - ADAM_B2 ** ADAM_STEP)
    n = len(parts)

    def body(*refs):
        w_ref, m_ref, v_ref, g_out, d_out, m_out, v_out = refs[n:]
        g = _sum_parts(refs[0])
        for q in range(1, n):
            g = jnp.where(pl.program_id(0) == q, _sum_parts(refs[q]), g)
        m_new = ADAM_B1 * m_ref[...] + (1.0 - ADAM_B1) * g
        v_new = ADAM_B2 * v_ref[...] + (1.0 - ADAM_B2) * (g * g)
        m_hat = m_new * c1
        v_hat = v_new * c2
        g_out[...] = g
        d_out[...] = -ADAM_LR * (m_hat / (jnp.sqrt(v_hat) + ADAM_EPS) + ADAM_WD * w_ref[...])
        m_out[...] = m_new
        v_out[...] = v_new

    out = _sds(w.shape, F32)
    return pl.pallas_call(
        body, name=name, grid=grid, in_specs=[*p_specs, w_spec, w_spec, w_spec],
        out_specs=[w_spec] * 4, out_shape=[out] * 4,
        compiler_params=_cparams(len(grid)))(*parts, w, m, v)


def _ffn_fwd(x, gain, wa, wb, s, tm, tag, deps=()):
    t = x.shape[0]
    hn = rmsnorm_fwd(x, gain, tm, f"rms_{tag}", deps)
    gu = matmul(NN, hn, wa, _sds((8, t, FF_BLK), BF16), (t // tm, 8, 1),
                pl.BlockSpec((tm, D_MODEL), lambda i, j, k: (i, 0)),
                pl.BlockSpec((None, None, D_MODEL, FF_BLK), lambda i, j, k: (s, j, 0, 0)),
                pl.BlockSpec((None, tm, FF_BLK), lambda i, j, k: (j, i, 0)), None, name=f"ffn_in_{tag}")
    act = swiglu_fwd(gu, tm, f"swiglu_{tag}")
    row = pl.BlockSpec((tm, D_MODEL), lambda i, j, k: (i, 0))
    y = matmul(NN, act, wb, _sds((t, D_MODEL), F32), (t // tm, 1, 4),
               pl.BlockSpec((None, tm, FF_BLK), lambda i, j, k: (k, i, 0)),
               pl.BlockSpec((None, None, FF_BLK, D_MODEL), lambda i, j, k: (s, k, 0, 0)),
               row, (tm, D_MODEL), name=f"ffn_out_{tag}", alpha=0.5, res=x, res_spec=row)
    return y, (hn, gu, act)


def _ffn_bwd(dy, x, gain, saved, wa, wb, s, tm, tag, deps=()):
    t = x.shape[0]
    hn, gu, act = saved
    nt = t // tm
    row = pl.BlockSpec((tm, D_MODEL), lambda i, j, k: (i, 0))
    dact = matmul(NT, dy, wb, _sds((4, t, FF_BLK), BF16), (nt, 4, 1), row,
                  pl.BlockSpec((None, None, FF_BLK, D_MODEL), lambda i, j, k: (s, j, 0, 0)),
                  pl.BlockSpec((None, tm, FF_BLK), lambda i, j, k: (j, i, 0)), None,
                  name=f"ffn_dact_{tag}", alpha=0.5, deps=deps)
    dgu = swiglu_bwd(gu, dact, tm, f"swiglu_bwd_{tag}")
    dwb = matmul(TN, act, dy, _sds((4, FF_BLK, D_MODEL), BF16), (4, 1, nt),
                 pl.BlockSpec((None, tm, FF_BLK), lambda i, j, k: (i, k, 0)),
                 pl.BlockSpec((tm, D_MODEL), lambda i, j, k: (k, 0)),
                 pl.BlockSpec((None, FF_BLK, D_MODEL), lambda i, j, k: (i, 0, 0)),
                 (FF_BLK, D_MODEL), name=f"ffn_dwout_{tag}", alpha=0.5)
    dwa = matmul(TN, hn, dgu, _sds((8, D_MODEL, FF_BLK), BF16), (1, 8, nt),
                 pl.BlockSpec((tm, D_MODEL), lambda i, j, k: (k, 0)),
                 pl.BlockSpec((None, tm, FF_BLK), lambda i, j, k: (j, k, 0)),
                 pl.BlockSpec((None, D_MODEL, FF_BLK), lambda i, j, k: (j, 0, 0)),
                 (D_MODEL, FF_BLK), name=f"ffn_dwin_{tag}")
    dhn = matmul(NT, dgu, wa, _sds((t, D_MODEL), F32), (nt, 1, 8),
                 pl.BlockSpec((None, tm, FF_BLK), lambda i, j, k: (k, i, 0)),
                 pl.BlockSpec((None, None, D_MODEL, FF_BLK), lambda i, j, k: (s, k, 0, 0)),
                 row, (tm, D_MODEL), name=f"ffn_dh_{tag}")
    dx, dgain = rmsnorm_bwd(x, gain, dhn, dy, tm, f"rms_bwd_{tag}")
    return dx, dgain, dwa, dwb


BR_ROWS = ((0, 1), (1, 2), (3, 1))


def _mixer_fwd(x, gain, wqkv, wf, wgate, wbr, wout, bq, bf, bg, bias, layer, tm, tag):
    t = x.shape[0]
    nt = t // tm
    hm = rmsnorm_fwd(x, gain, tm, f"rms_{tag}")
    a_full = pl.BlockSpec((tm, D_MODEL), lambda i, j, k: (i, 0))
    wide_out = pl.BlockSpec((tm, D_MODEL), lambda i, j, k: (i, j))
    wide_b = pl.BlockSpec((1, D_MODEL), lambda i, j, k: (0, j))
    qkv = matmul(NN, hm, wqkv, _sds((t, QKV_WIDTH), BF16), (nt, 3, 1), a_full,
                 pl.BlockSpec((None, D_MODEL, D_MODEL), lambda i, j, k: (layer, 0, j)), wide_out, None,
                 name=f"proj_qkv_{tag}", bias=bq, bias_spec=wide_b)
    gates = matmul(NN, hm, wgate, _sds((t, 3 * D_MODEL), F32), (nt, 3, 1), a_full,
                   pl.BlockSpec((None, D_MODEL, D_MODEL), lambda i, j, k: (layer + 1, 0,j)), wide_out,
                   None, name=f"proj_gate_{tag}", bias=bg, bias_spec=wide_b)
    f = matmul(NN, hm, wf, _sds((t, LANES), F32), (nt, 1, 1), a_full,
               pl.BlockSpec((None, D_MODEL, LANES), lambda i, j, k: (layer, 0, 0)),
               pl.BlockSpec((tm, LANES), lambda i, j, k: (i, 0)), None,
               name=f"proj_f_{tag}", bias=bf, bias_spec=pl.BlockSpec((1, LANES), lambda i, j, k: (0, 0)))
    fcol, frow = forget_cumsum(f, f"fcum_{tag}")
    frow = _frow_to_groups(frow)
    o_sb = sb_fwd(qkv, f"sb_fwd_{tag}")
    o_ch = chunk_fwd(qkv, bias, f"chunk_fwd_{tag}")
    o_fox, lse = fox_fwd(qkv, fcol, frow, f"fox_fwd_{tag}")
    ys = []
    for a, (o, (r0, nr)) in enumerate(zip((o_sb, o_ch, o_fox), BR_ROWS)):
        ys.append(matmul(
            NN, o, wbr, _sds((t, D_MODEL), F32), (nt, 1, nr),
            pl.BlockSpec((tm, 256), lambda i, j, k: (i, k)),
            pl.BlockSpec((None, 256, D_MODEL), functools.partial(lambda i, j, k, r0: (layer, r0 + k, 0), r0=r0)),
            a_full, (tm, D_MODEL), name=f"branch{a}_{tag}"))
    merged = merge_fwd(gates, ys[0], ys[1], ys[2], tm, f"merge_{tag}")
    x_new = matmul(NN, merged, wout, _sds((t, D_MODEL), F32), (nt, 1, 1), a_full,
                   pl.BlockSpec((None, D_MODEL, D_MODEL), lambda i, j, k: (layer, 0, 0)), a_full, None,
                   name=f"wout_{tag}", res=x, res_spec=a_full)
    saved = (hm, qkv, gates, f, fcol, frow, o_sb, o_ch, o_fox, lse, ys, merged)
    return x_new, saved


def _mixer_bwd(dy, x, gain, saved, wqkv, wf, wgate, wbr, wout, bias, layer, tm, tag, deps=()):
    t = x.shape[0]
    nt = t // tm
    hm, qkv, gates, f, fcol, frow, o_sb, o_ch, o_fox, lse, ys, merged = saved
    a_full = pl.BlockSpec((tm, D_MODEL), lambda i, j, k: (i, 0))
    red_row = pl.BlockSpec((tm, D_MODEL), lambda i, j, k: (k, 0))
    sq = pl.BlockSpec((D_MODEL, D_MODEL), lambda i, j, k: (0, 0))
    dmerged = matmul(NT, dy, wout, _sds((t, D_MODEL), F32), (nt, 1, 1), a_full,
                     pl.BlockSpec((None, D_MODEL, D_MODEL), lambda i, j, k: (layer, 0, 0)), a_full, None,
                     name=f"dmerged_{tag}", deps=deps)
    dwout = matmul(TN, merged, dy, _sds((D_MODEL, D_MODEL), BF16), (1, 1, nt), red_row, red_row, sq,
                   (D_MODEL, D_MODEL), name=f"dwout_{tag}")
    dgs, dys = merge_bwd(dmerged, gates, ys[0], ys[1], ys[2], tm, f"merge_bwd_{tag}")
    dos, dwbrs = [], []
    for a, (o, (r0, nr)) in enumerate(zip((o_sb, o_ch, o_fox), BR_ROWS)):
        dos.append(matmul(
            NT, dys[a], wbr, _sds((t, nr * 256), BF16), (nt, nr, 1), a_full,
            pl.BlockSpec((None, 256, D_MODEL), functools.partial(lambda i, j, k, r0: (layer, r0 + j, 0), r0=r0)),
            pl.BlockSpec((tm, 256), lambda i, j, k: (i, j)), None, name=f"dbranch{a}_{tag}"))
        dwbrs.append(matmul(
            TN, o, dys[a], _sds((nr * 256, D_MODEL), BF16), (nr, 1, nt),
            pl.BlockSpec((tm, 256), lambda i, j, k: (k, i)), red_row,
            pl.BlockSpec((256, D_MODEL), lambda i, j, k: (i, 0)), (256, D_MODEL), name=f"dwbr{a}_{tag}"))
    dq_a, dk_a, dv_a = sb_bwd(qkv, dos[0], f"sb_bwd_{tag}")
    dq_b, dk_b, dv_b, dbias = chunk_bwd(qkv, bias, dos[1], f"chunk_bwd_{tag}")
    dq_c, dk_c, dv_c, dfrow = fox_bwd(qkv, fcol, frow, o_fox, lse, dos[2], f"fox_bwd_{tag}")
    df = forget_cumsum_bwd(_frow_from_groups(dfrow), f, f"fcum_bwd_{tag}")
    dqkv = jnp.concatenate([p.astype(BF16) for p in
                            (dq_a, dk_a, dv_a, dq_b, dk_b, dv_b, dq_c, dk_c, dv_c)], axis=1)
    dgates = jnp.concatenate(dgs, axis=1)
    dtab = rel_bias_scatter(dbias.reshape(N_HEADS_CH, QB * CH_KEYS), f"rel_scatter_{tag}")

    wide_b = pl.BlockSpec((tm, D_MODEL), lambda i, j, k: (k, j))
    wide_o = pl.BlockSpec((D_MODEL, D_MODEL), lambda i, j, k: (0, j))
    wide_cs = pl.BlockSpec((1, D_MODEL), lambda i, j, k: (0, j))
    dwqkv, dbq = matmul(TN, hm, dqkv, _sds((D_MODEL, QKV_WIDTH), BF16), (1, 3, nt), red_row, wide_b,
                        wide_o, (D_MODEL, D_MODEL), name=f"dwqkv_{tag}",
                        colsum_sds=_sds((1, QKV_WIDTH), F32), colsum_spec=wide_cs)
    dwgate, dbg = matmul(TN, hm, dgates, _sds((D_MODEL, 3 * D_MODEL), BF16), (1, 3, nt), red_row,
                         wide_b, wide_o, (D_MODEL, D_MODEL), name=f"dwgate_{tag}",
                         colsum_sds=_sds((1, 3 * D_MODEL), F32), colsum_spec=wide_cs)
    dwf, dbf = matmul(TN, hm, df, _sds((D_MODEL, LANES), BF16), (1, 1, nt), red_row,
                      pl.BlockSpec((tm, LANES), lambda i, j, k: (k, 0)),
                      pl.BlockSpec((D_MODEL, LANES), lambda i, j, k: (0, 0)), (D_MODEL, LANES),
                      name=f"dwf_{tag}", colsum_sds=_sds((1, LANES), F32),
                      colsum_spec=pl.BlockSpec((1, LANES), lambda i, j, k: (0, 0)))
    wide_a = pl.BlockSpec((tm, D_MODEL), lambda i, j, k: (i, k))
    dhm = matmul(NT, dqkv, wqkv, _sds((t, D_MODEL), F32), (nt, 1, 3), wide_a,
                 pl.BlockSpec((None, D_MODEL, D_MODEL), lambda i, j, k: (layer, 0, k)), a_full,
                 (tm, D_MODEL), name=f"dhm_qkv_{tag}")
    dhm = matmul(NT, dgates, wgate, _sds((t, D_MODEL), F32), (nt, 1, 3), wide_a,
                 pl.BlockSpec((None, D_MODEL, D_MODEL), lambda i, j, k: (layer + 1, 0,k)), a_full,
                 (tm, D_MODEL), name=f"dhm_gate_{tag}", res=dhm, res_spec=a_full)
    dhm = matmul(NT, df, wf, _sds((t, D_MODEL), F32), (nt, 1, 1),
                 pl.BlockSpec((tm, LANES), lambda i, j, k: (i, 0)),
                 pl.BlockSpec((None, D_MODEL, LANES), lambda i, j, k: (layer, 0, 0)), a_full, None,
                 name=f"dhm_f_{tag}", res=dhm, res_spec=a_full)
    dx, dgain = rmsnorm_bwd(x, gain, dhm, dy, tm, f"rms_bwd_{tag}")
    dwbr = jnp.concatenate(dwbrs, axis=0)
    grads = dict(dwqkv=dwqkv, dwgate=dwgate, dwf=dwf, dwbr=dwbr, dwout=dwout,
                 dbq=dbq, dbg=dbg, dbf=dbf, dtab=dtab, dgain=dgain)
    return dx, grads


def _pack_small(pieces):
    flat = jnp.concatenate([p.reshape(-1).astype(F32) for p in pieces])
    flat = jnp.pad(flat, (0, SMALL_ROWS * LANES - flat.shape[0]))
    return flat.reshape(SMALL_ROWS, LANES)


def _unpack_small(packed, shapes):
    flat = packed.reshape(-1)
    out, pos = [], 0
    for shp in shapes:
        n = int(np.prod(shp))
        out.append(flat[pos:pos + n].reshape(shp))
        pos += n
    return out


def kernel(x, g_ffn1, w_ffn1_in, w_ffn1_out, g_mix, w_in, b_in, rel_bias, w_br_sb, w_br_ch, w_br_fox, w_out, g_ffn2, w_ffn2_in, w_ffn2_out, g_final, loss_target, m_g_ffn1, m_w_ffn1_in, m_w_ffn1_out, m_g_mix, m_w_in, m_b_in, m_rel_bias, m_w_br_sb, m_w_br_ch, m_w_br_fox, m_w_out, m_g_ffn2, m_w_ffn2_in, m_w_ffn2_out, m_g_final, v_g_ffn1, v_w_ffn1_in, v_w_ffn1_out, v_g_mix, v_w_in, v_b_in, v_rel_bias, v_w_br_sb, v_w_br_ch, v_w_br_fox, v_w_out, v_g_ffn2, v_w_ffn2_in, v_w_ffn2_out, v_g_final):
    t = x.shape[1]
    tm = min(512, t)
    xs = x[0]
    target = loss_target[0]
    f_lo, f_hi = QKV_WIDTH, QKV_WIDTH + N_HEADS_FOX

    def ffn_shards(w_in_, w_out_, l):
        return [w_in_[l:l + 1].astype(BF16), w_out_[l:l + 1].astype(BF16)]

    def mixer_shards(l):
        wl = w_in[l]
        return [jnp.stack([wl[:, :QKV_WIDTH], wl[:, f_hi:]]).astype(BF16),
                jnp.pad(wl[:, f_lo:f_hi], ((0, 0), (0, LANES - N_HEADS_FOX)))[None].astype(BF16),
                w_out[l:l + 1].astype(BF16),
                jnp.concatenate([w_br_sb[l], w_br_ch[l], w_br_fox[l]], axis=0)[None].astype(BF16)]

    gathers = []
    for l in range(DEPTH):
        gathers.append(exchange_start("gather", ffn_shards(w_ffn1_in, w_ffn1_out, l), f"gather_ffn1_l{l}"))
        gathers.append(exchange_start("gather", mixer_shards(l), f"gather_mix_l{l}"))
        gathers.append(exchange_start("gather", ffn_shards(w_ffn2_in, w_ffn2_out, l), f"gather_ffn2_l{l}"))
    gather_tokens = [g["token"] for g in gathers]

    def ffn_weights(handle, after, name):
        wa_, wb_ = exchange_wait(handle, after, name)
        return wa_, wb_.reshape(1, 4, FF_BLK, D_MODEL)

    def mixer_weights(handle, after, name):
        wc_, wf_, wout_, wbr_ = exchange_wait(handle, after, name)
        return (wc_.reshape(2, D_MODEL, QKV_WIDTH), wf_.reshape(1, D_MODEL, LANES),
                wout_.reshape(1, D_MODEL, D_MODEL),
                wbr_.transpose(0, 2, 1, 3).reshape(1, D_MODEL, D_MODEL))

    bq = b_in[:, None, :QKV_WIDTH]
    bf = jnp.pad(b_in[:, f_lo:f_hi], ((0, 0), (0, LANES - N_HEADS_FOX)))[:, None, :]
    bg = b_in[:, None, f_hi:]
    tab_t = jnp.pad(rel_bias.transpose(0, 2, 1), ((0, 0), (0, 0), (0, REL_PAD - N_REL)))

    h = xs
    saved = []
    weights = []
    for l in range(DEPTH):
        bias = rel_bias_build(tab_t[l], f"rel_build_l{l}").reshape(N_HEADS_CH, QB, CH_KEYS)
        x0 = h
        w1 = ffn_weights(gathers[3 * l], x0, f"gathered_ffn1_l{l}")
        x1, s1 = _ffn_fwd(x0, g_ffn1[l:l + 1], *w1, 0, tm, f"ffn1_l{l}",
                          deps=gather_tokens if l == 0 else ())
        wc, wf, wout, wbr = mixer_weights(gathers[3 * l + 1], x1, f"gathered_mix_l{l}")
        x2, sm = _mixer_fwd(x1, g_mix[l:l + 1], wc, wf, wc, wbr, wout, bq[l], bf[l], bg[l], bias,
                            0, tm, f"mix_l{l}")
        w2 = ffn_weights(gathers[3 * l + 2], x2, f"gathered_ffn2_l{l}")
        x3, s2 = _ffn_fwd(x2, g_ffn2[l:l + 1], *w2, 0, tm, f"ffn2_l{l}")
        saved.append((x0, x1, x2, s1, sm, s2, bias))
        weights.append((w1, (wc, wf, wout, wbr), w2))
        h = x3

    dx, dg_final, loss_blk = loss_head(h, g_final[None, :], target, tm, "loss_head")

    g_mix_l = [None] * DEPTH
    dgains = {}
    scatters = {}
    token = ()
    for l in reversed(range(DEPTH)):
        x0, x1, x2, s1, sm, s2, bias = saved[l]
        w1, (wc, wf, wout, wbr), w2 = weights[l]
        dx, dgains[("ffn2", l)], dwa, dwb = _ffn_bwd(
            dx, x2, g_ffn2[l:l + 1], s2, *w2, 0, tm, f"ffn2_l{l}", deps=token)
        scatters[("ffn2", l)] = exchange_start(
            "scatter", [dwa[None], dwb.reshape(1, N_DEV, D_FF // N_DEV, D_MODEL)], f"scatter_ffn2_l{l}")
        token = (scatters[("ffn2", l)]["token"],)
        dx, g_mix_l[l] = _mixer_bwd(dx, x1, g_mix[l:l + 1], sm, wc, wf, wc, wbr, wout, bias, 0, tm,
                                    f"mix_l{l}", deps=token)
        gm = g_mix_l[l]
        scatters[("mix", l)] = exchange_start(
            "scatter",
            [gm["dwqkv"].reshape(1, N_DEV, LANES, QKV_WIDTH), gm["dwgate"].reshape(1, N_DEV, LANES, QKV_WIDTH),
             gm["dwf"].reshape(1, N_DEV, LANES, LANES), gm["dwout"].reshape(1, N_DEV, LANES, D_MODEL),
             gm["dwbr"].reshape(1, D_MODEL, N_DEV, LANES).transpose(0, 2, 1, 3)], f"scatter_mix_l{l}")
        token = (scatters[("mix", l)]["token"],)
        dx, dgains[("ffn1", l)], dwa, dwb = _ffn_bwd(
            dx, x0, g_ffn1[l:l + 1], s1, *w1, 0, tm, f"ffn1_l{l}", deps=token)
        scatters[("ffn1", l)] = exchange_start(
            "scatter", [dwa[None], dwb.reshape(1, N_DEV, D_FF // N_DEV, D_MODEL)], f"scatter_ffn1_l{l}")
        token = (scatters[("ffn1", l)]["token"],)

    small_shapes = []
    small_pieces = []
    small_w, small_m, small_v = [], [], []

    def add_small(piece, w, m, v):
        small_shapes.append(w.shape)
        small_pieces.append(piece)
        small_w.append(w); small_m.append(m); small_v.append(v)

    dg1 = jnp.concatenate([dgains[("ffn1", l)] for l in range(DEPTH)], axis=0)
    dgm = jnp.concatenate([g_mix_l[l]["dgain"] for l in range(DEPTH)], axis=0)
    dg2 = jnp.concatenate([dgains[("ffn2", l)] for l in range(DEPTH)], axis=0)
    db = jnp.stack([jnp.concatenate([g_mix_l[l]["dbq"][0], g_mix_l[l]["dbf"][0, :N_HEADS_FOX],
                                     g_mix_l[l]["dbg"][0]]) for l in range(DEPTH)])
    drel = jnp.stack([g_mix_l[l]["dtab"][:, :N_REL].T for l in range(DEPTH)])
    add_small(dg1, g_ffn1, m_g_ffn1, v_g_ffn1)
    add_small(dgm, g_mix, m_g_mix, v_g_mix)
    add_small(db, b_in, m_b_in, v_b_in)
    add_small(drel, rel_bias, m_rel_bias, v_rel_bias)
    add_small(dg2, g_ffn2, m_g_ffn2, v_g_ffn2)
    add_small(dg_final[0], g_final, m_g_final, v_g_final)
    loss_piece = loss_blk[0, 0:1]
    small_sum = all_reduce_small(_pack_small(small_pieces + [loss_piece]), "allreduce_small")
    n_small = sum(int(np.prod(s)) for s in small_shapes)
    loss = small_sum.reshape(-1)[n_small]

    sm_spec = pl.BlockSpec((SMALL_ROWS, LANES), lambda i: (0, 0))
    sm_out = adamw([small_sum[None]], _pack_small(small_w), _pack_small(small_m), _pack_small(small_v),
                   (1,), [pl.BlockSpec((1, SMALL_ROWS, LANES), lambda i: (0, 0, 0))], sm_spec, "adamw_small")
    sm_g, sm_d, sm_m, sm_v = [_unpack_small(o, small_shapes) for o in sm_out]

    recv = {}
    for l in reversed(range(DEPTH)):
        for grp in ("ffn2", "mix", "ffn1"):
            recv[(grp, l)] = exchange_wait(scatters[(grp, l)], dx, f"scattered_{grp}_l{l}")

    def upd(parts, w, m, v, tr, name, rb0=0):
        _, r, c = w.shape
        nr = r // tr

        def p_spec(layer):
            pinned = (nr - 1) if layer == 0 else 0
            return pl.BlockSpec((N_DEV, None, tr, c),
                                lambda l, i: (0, 0, rb0 + jnp.where(l == layer, i, pinned), 0))

        return adamw(parts, w, m, v, (DEPTH, nr), [p_spec(0), p_spec(1)],
                     pl.BlockSpec((None, tr, c), lambda l, i: (l, i, 0)), name)

    def both(grp, k):
        return [recv[(grp, l)][k] for l in range(DEPTH)]

    out_rows = D_FF // N_DEV // 2
    r_ffn1_in = upd(both("ffn1", 0), w_ffn1_in, m_w_ffn1_in, v_w_ffn1_in, 256, "adamw_ffn1_in")
    r_ffn2_in = upd(both("ffn2", 0), w_ffn2_in, m_w_ffn2_in, v_w_ffn2_in, 256, "adamw_ffn2_in")
    r_ffn1_out = upd(both("ffn1", 1), w_ffn1_out, m_w_ffn1_out, v_w_ffn1_out, out_rows, "adamw_ffn1_out")
    r_ffn2_out = upd(both("ffn2", 1), w_ffn2_out, m_w_ffn2_out, v_w_ffn2_out, out_rows, "adamw_ffn2_out")
    r_out = upd(both("mix", 3), w_out, m_w_out, v_w_out, LANES, "adamw_w_out")
    r_br_sb = upd(both("mix", 4), w_br_sb, m_w_br_sb, v_w_br_sb, 256, "adamw_br_sb", rb0=0)
    r_br_ch = upd(both("mix", 4), w_br_ch, m_w_br_ch, v_w_br_ch, 256, "adamw_br_ch", rb0=1)
    r_br_fox = upd(both("mix", 4), w_br_fox, m_w_br_fox, v_w_br_fox, 256, "adamw_br_fox", rb0=3)

    def summed(parts, name):
        _, _, r, c = parts.shape
        return sum_parts(parts, (1,), pl.BlockSpec((N_DEV, None, r, c), lambda s: (0, 0, 0, 0)),
                         pl.BlockSpec((r, c), lambda s: (0, 0)), _sds((r, c), F32), name)

    g_w_in = jnp.stack([
        jnp.concatenate([summed(recv[("mix", l)][0], f"sum_wqkv_l{l}"),
                         summed(recv[("mix", l)][2], f"sum_wf_l{l}")[:, :N_HEADS_FOX],
                         summed(recv[("mix", l)][1], f"sum_wgate_l{l}")], axis=1) for l in range(DEPTH)])
    win_rows = 32
    win_spec = pl.BlockSpec((None, win_rows, w_in.shape[2]), lambda l, i: (l, i, 0))
    r_in = adamw([g_w_in[None]], w_in, m_w_in, v_w_in, (DEPTH, LANES // win_rows),
                 [pl.BlockSpec((1, None, win_rows, w_in.shape[2]), lambda l, i: (0, l, i, 0))], win_spec,
                 "adamw_w_in")

    def per_kind(k):
        small = (sm_g, sm_d, sm_m, sm_v)[k]
        return [small[0], r_ffn1_in[k], r_ffn1_out[k], small[1], r_in[k], small[2], small[3],
                r_br_sb[k], r_br_ch[k], r_br_fox[k], r_out[k], small[4], r_ffn2_in[k], r_ffn2_out[k],
                small[5]]

    return (loss, dx[None], *per_kind(0), *per_kind(1), *per_kind(2), *per_kind(3))
```

```python
import functools

import numpy as np
import jax
import jax.numpy as jnp
from jax import lax
from jax.experimental import pallas as pl
from jax.experimental.pallas import tpu as pltpu

F32 = jnp.float32
BF16 = jnp.bfloat16

N_DEV = 8
D_MODEL = 1024
DEPTH = 2
HEAD_DIM = 64
W_SB, W_CH, W_FOX = 256, 512, 256
QKV_WIDTH = 3 * (W_SB + W_CH + W_FOX)
N_HEADS_FOX = 4
N_HEADS_CH = 8
D_FF = 2816
FF_BLK = 2 * D_FF // N_DEV
CHUNK = 64
LEFT_CHUNKS = 8
MAX_REL = 128
N_REL = 2 * MAX_REL + 1
REL_PAD = 384
QB = 128
KB = 512
KSUB = KB // QB
CH_WIN = 5
CH_KEYS = CH_WIN * QB
RMS_EPS = 1e-6
NEG = -1e30
SCALE = HEAD_DIM ** -0.5
LANES = 128
VMEM_LIMIT = 56 * 1024 * 1024

ADAM_LR, ADAM_B1, ADAM_B2, ADAM_EPS, ADAM_WD, ADAM_STEP = 0.001, 0.9, 0.999, 1e-08, 0.01, 10

SMALL_ROWS = 192

MESH = pl.DeviceIdType.MESH
ANY = pl.BlockSpec(memory_space=pl.ANY)
HIGHEST = lax.Precision.HIGHEST

NN = (((1,), (0,)), ((), ()))
NT = (((1,), (1,)), ((), ()))
TN = (((0,), (0,)), ((), ()))


def _cparams(n_grid):
    return pltpu.CompilerParams(dimension_semantics=("arbitrary",) * n_grid,
                                vmem_limit_bytes=VMEM_LIMIT)


def _sds(shape, dtype):
    return jax.ShapeDtypeStruct(tuple(shape), dtype)


def _my_index():
    return 4 * lax.axis_index("x") + 2 * lax.axis_index("y") + lax.axis_index("c")


def _peer(mask):
    x, y, c = lax.axis_index("x"), lax.axis_index("y"), lax.axis_index("c")
    px = x ^ ((mask >> 2) & 1)
    py = y ^ ((mask >> 1) & 1)
    pc = c ^ (mask & 1)
    return (px, py, pc), 4 * px + 2 * py + pc


def all_gather(shard, name):
    s, r, c = shard.shape

    def body(in_ref, out_ref, send_sems, recv_sems, local_sem):
        me = _my_index()
        mine = pltpu.make_async_copy(in_ref, out_ref.at[:, me], local_sem)
        mine.start()
        sends = []
        for mask in range(1, N_DEV):
            peer, _ = _peer(mask)
            cp = pltpu.make_async_remote_copy(
                src_ref=in_ref, dst_ref=out_ref.at[:, me],
                send_sem=send_sems.at[mask - 1], recv_sem=recv_sems.at[mask - 1],
                device_id=peer, device_id_type=MESH)
            cp.start()
            sends.append(cp)
        for mask in range(1, N_DEV):
            peer, pidx = _peer(mask)
            pltpu.make_async_remote_copy(
                src_ref=in_ref, dst_ref=out_ref.at[:, pidx],
                send_sem=send_sems.at[mask - 1], recv_sem=recv_sems.at[mask - 1],
                device_id=peer, device_id_type=MESH).wait_recv()
        for cp in sends:
            cp.wait_send()
        mine.wait()

    return pl.pallas_call(
        body, name=name,
        out_shape=_sds((s, N_DEV, r, c), shard.dtype),
        in_specs=[ANY], out_specs=ANY,
        scratch_shapes=[pltpu.SemaphoreType.DMA((N_DEV - 1,)),
                        pltpu.SemaphoreType.DMA((N_DEV - 1,)),
                        pltpu.SemaphoreType.DMA],
    )(shard)


def all_to_all(parts, name):
    s, _, r, c = parts.shape

    def body(in_ref, out_ref, send_sems, recv_sems, local_sem):
        me = _my_index()
        mine = pltpu.make_async_copy(in_ref.at[:, me], out_ref.at[me], local_sem)
        mine.start()
        sends = []
        for mask in range(1, N_DEV):
            peer, pidx = _peer(mask)
            cp = pltpu.make_async_remote_copy(
                src_ref=in_ref.at[:, pidx], dst_ref=out_ref.at[me],
                send_sem=send_sems.at[mask - 1], recv_sem=recv_sems.at[mask - 1],
                device_id=peer, device_id_type=MESH)
            cp.start()
            sends.append(cp)
        for mask in range(1, N_DEV):
            peer, pidx = _peer(mask)
            pltpu.make_async_remote_copy(
                src_ref=in_ref.at[:, me], dst_ref=out_ref.at[pidx],
                send_sem=send_sems.at[mask - 1], recv_sem=recv_sems.at[mask - 1],
                device_id=peer, device_id_type=MESH).wait_recv()
        for cp in sends:
            cp.wait_send()
        mine.wait()

    return pl.pallas_call(
        body, name=name,
        out_shape=_sds((N_DEV, s, r, c), parts.dtype),
        in_specs=[ANY], out_specs=ANY,
        scratch_shapes=[pltpu.SemaphoreType.DMA((N_DEV - 1,)),
                        pltpu.SemaphoreType.DMA((N_DEV - 1,)),
                        pltpu.SemaphoreType.DMA],
    )(parts)


HBM_SPEC = pl.BlockSpec(memory_space=pltpu.HBM)
SEM_SPEC = pl.BlockSpec(memory_space=pltpu.SEMAPHORE)
EFFECT = pltpu.SideEffectType.DATAFLOW_SIDE_EFFECTING


def _exchange_refs(mode, in_ref, land_ref, me, pidx):
    if mode == "gather":
        return in_ref, land_ref.at[:, me], land_ref.at[:, pidx]
    return in_ref.at[:, pidx], land_ref.at[me], land_ref.at[pidx]


def _landing(mode, a, me):
    if mode == "gather":
        s, r, c = a.shape
        return lax.dynamic_update_slice(lax.empty((s, N_DEV, r, c), a.dtype), a[:, None], (0, me, 0, 0))
    s, _, r, c = a.shape
    own = lax.dynamic_index_in_dim(a, me, axis=1, keepdims=False)
    return lax.dynamic_update_slice(lax.empty((N_DEV, s, r, c), a.dtype), own[None], (me, 0, 0, 0))


def exchange_start(mode, arrays, name):
    n = len(arrays)
    me = _my_index()
    lands0 = [_landing(mode, a, me) for a in arrays]

    def body(*refs):
        in_refs, land_refs = refs[:n], refs[n:2 * n]
        send_sems, recv_sems, token = refs[2 * n], refs[2 * n + 1], refs[-1]
        mine = _my_index()
        for k in range(n):
            for mask in range(1, N_DEV):
                peer, pidx = _peer(mask)
                src, dst, _ = _exchange_refs(mode, in_refs[k], land_refs[k], mine, pidx)
                sem = k * (N_DEV - 1) + mask - 1
                pltpu.make_async_remote_copy(
                    src_ref=src, dst_ref=dst, send_sem=send_sems.at[sem], recv_sem=recv_sems.at[sem],
                    device_id=peer, device_id_type=MESH).start()
        token[...] = jnp.zeros_like(token)

    nsem = n * (N_DEV - 1)
    outs = pl.pallas_call(
        body, name=name,
        out_shape=(pltpu.SemaphoreType.DMA((nsem,)), pltpu.SemaphoreType.DMA((nsem,)),
                   *[pltpu.HBM(a.shape, a.dtype) for a in arrays],
                   *[pltpu.HBM(l.shape, l.dtype) for l in lands0], _sds((8, LANES), F32)),
        in_specs=[HBM_SPEC] * (2 * n),
        out_specs=(SEM_SPEC, SEM_SPEC, *[HBM_SPEC] * (2 * n), pl.BlockSpec(memory_space=pltpu.VMEM)),
        input_output_aliases={k: 2 + k for k in range(2 * n)},
        compiler_params=pltpu.CompilerParams(has_side_effects=EFFECT),
    )(*[pltpu.with_memory_space_constraint(a, pltpu.HBM) for a in arrays],
      *[pltpu.with_memory_space_constraint(l, pltpu.HBM) for l in lands0])
    return dict(mode=mode, n=n, send=outs[0], recv=outs[1], ins=outs[2:2 + n],
                lands=outs[2 + n:2 + 2 * n], token=outs[-1])


def exchange_wait(handle, after, name):
    n, mode = handle["n"], handle["mode"]

    def body(*refs):
        in_refs, land_refs = refs[:n], refs[n:2 * n]
        send_sems, recv_sems = refs[2 * n], refs[2 * n + 1]
        mine = _my_index()
        for k in range(n):
            for mask in range(1, N_DEV):
                peer, pidx = _peer(mask)
                src, _, here = _exchange_refs(mode, in_refs[k], land_refs[k], mine, pidx)
                sem = k * (N_DEV - 1) + mask - 1
                cp = pltpu.make_async_remote_copy(
                    src_ref=src, dst_ref=here, send_sem=send_sems.at[sem], recv_sem=recv_sems.at[sem],
                    device_id=peer, device_id_type=MESH)
                cp.wait_send()
                cp.wait_recv()

    thru = (*handle["ins"], *handle["lands"])
    outs = pl.pallas_call(
        body, name=name,
        out_shape=tuple(pltpu.HBM(a.shape, a.dtype) for a in thru),
        in_specs=[HBM_SPEC] * (2 * n) + [SEM_SPEC, SEM_SPEC, ANY],
        out_specs=tuple([HBM_SPEC] * (2 * n)),
        input_output_aliases={k: k for k in range(2 * n)},
        compiler_params=pltpu.CompilerParams(has_side_effects=EFFECT),
    )(*thru, handle["send"], handle["recv"], after)
    return list(outs[n:])


def all_reduce_small(packed, name):
    rows = packed.shape[0]

    def body(in_ref, out_ref, slots, send_sems, recv_sems):
        me = _my_index()
        sends = []
        for mask in range(1, N_DEV):
            peer, _ = _peer(mask)
            cp = pltpu.make_async_remote_copy(
                src_ref=in_ref, dst_ref=slots.at[me],
                send_sem=send_sems.at[mask - 1], recv_sem=recv_sems.at[mask - 1],
                device_id=peer, device_id_type=MESH)
            cp.start()
            sends.append(cp)
        slots[me] = in_ref[...]
        for mask in range(1, N_DEV):
            peer, pidx = _peer(mask)
            pltpu.make_async_remote_copy(
                src_ref=in_ref, dst_ref=slots.at[pidx],
                send_sem=send_sems.at[mask - 1], recv_sem=recv_sems.at[mask - 1],
                device_id=peer, device_id_type=MESH).wait_recv()
        for cp in sends:
            cp.wait_send()
        total = slots[0]
        for p in range(1, N_DEV):
            total = total + slots[p]
        out_ref[...] = total

    return pl.pallas_call(
        body, name=name,
        out_shape=_sds((rows, LANES), F32),
        in_specs=[pl.BlockSpec(memory_space=pltpu.VMEM)],
        out_specs=pl.BlockSpec(memory_space=pltpu.VMEM),
        scratch_shapes=[pltpu.VMEM((N_DEV, rows, LANES), F32),
                        pltpu.SemaphoreType.DMA((N_DEV - 1,)),
                        pltpu.SemaphoreType.DMA((N_DEV - 1,))],
    )(packed)


def matmul(dims, a, b, out_sds, grid, a_spec, b_spec, o_spec, acc_shape, *, name, alpha=1.0,
           bias=None, bias_spec=None, res=None, res_spec=None, colsum_sds=None, colsum_spec=None,
           deps=()):
    nk = grid[2]
    has_bias, has_res, has_cs = bias is not None, res is not None, colsum_sds is not None
    if has_cs:
        assert grid[0] == 1 and dims == TN

    def body(*refs):
        a_ref, b_ref = refs[0], refs[1]
        pos = 2
        bias_ref = res_ref = cs_ref = None
        if has_bias:
            bias_ref = refs[pos]; pos += 1
        if has_res:
            res_ref = refs[pos]; pos += 1
        pos += len(deps)
        o_ref = refs[pos]; pos += 1
        if has_cs:
            cs_ref = refs[pos]; pos += 1
        k = pl.program_id(2)
        bval = b_ref[...]
        part = lax.dot_general(a_ref[...].astype(BF16), bval.astype(BF16), dims,
                               preferred_element_type=F32)

        def finish(total):
            r = total * alpha if alpha != 1.0 else total
            if has_bias:
                r = r + bias_ref[...]
            if has_res:
                r = r + res_ref[...].astype(F32)
            o_ref[...] = r.astype(o_ref.dtype)

        if has_cs:
            csum = jnp.sum(bval.astype(F32), axis=0, keepdims=True)

            @pl.when(k == 0)
            def _():
                cs_ref[...] = csum

            @pl.when(k > 0)
            def _():
                cs_ref[...] += csum

        if nk == 1:
            finish(part)
        else:
            acc_ref = refs[pos]

            @pl.when(k == 0)
            def _():
                acc_ref[...] = part

            @pl.when(k > 0)
            def _():
                acc_ref[...] += part

            @pl.when(k == nk - 1)
            def _():
                finish(acc_ref[...])

    in_specs, args = [a_spec, b_spec], [a, b]
    if has_bias:
        in_specs.append(bias_spec); args.append(bias)
    if has_res:
        in_specs.append(res_spec); args.append(res)
    in_specs += [ANY] * len(deps)
    args += list(deps)
    out_shape, out_specs = [out_sds], [o_spec]
    if has_cs:
        out_shape.append(colsum_sds); out_specs.append(colsum_spec)
    scratch = [] if nk == 1 else [pltpu.VMEM(acc_shape, F32)]
    outs = pl.pallas_call(
        body, name=name, grid=grid, in_specs=in_specs, out_specs=out_specs, out_shape=out_shape,
        scratch_shapes=scratch, compiler_params=_cparams(3))(*args)
    return outs if has_cs else outs[0]


def _sigmoid(z):
    return 1.0 / (1.0 + jnp.exp(-z))


def _log_sigmoid(z):
    return jnp.minimum(z, 0.0) - jnp.log(1.0 + jnp.exp(-jnp.abs(z)))


def rmsnorm_fwd(x, gain, tm, name, deps=()):
    t, d = x.shape

    def body(x_ref, g_ref, *rest):
        o_ref = rest[-1]
        xf = x_ref[...]
        r = lax.rsqrt(jnp.mean(xf * xf, axis=-1, keepdims=True) + RMS_EPS)
        o_ref[...] = (xf * r * g_ref[...]).astype(o_ref.dtype)

    return pl.pallas_call(
        body, name=name, grid=(t // tm,),
        in_specs=[pl.BlockSpec((tm, d), lambda i: (i, 0)), pl.BlockSpec((1, d), lambda i: (0, 0))]
        + [ANY] * len(deps),
        out_specs=pl.BlockSpec((tm, d), lambda i: (i, 0)),
        out_shape=_sds((t, d), BF16), compiler_params=_cparams(1))(x, gain, *deps)


def rmsnorm_bwd(x, gain, dh, dres, tm, name):
    t, d = x.shape

    def body(x_ref, g_ref, dh_ref, dres_ref, dx_ref, dg_ref):
        i = pl.program_id(0)
        xf = x_ref[...]
        r = lax.rsqrt(jnp.mean(xf * xf, axis=-1, keepdims=True) + RMS_EPS)
        xhat = xf * r
        dh_v = dh_ref[...]
        dxhat = dh_v * g_ref[...]
        dx = r * (dxhat - xhat * jnp.mean(dxhat * xhat, axis=-1, keepdims=True))
        dx_ref[...] = dres_ref[...] + dx
        dg = jnp.sum(dh_v * xhat, axis=0, keepdims=True)

        @pl.when(i == 0)
        def _():
            dg_ref[...] = dg

        @pl.when(i > 0)
        def _():
            dg_ref[...] += dg

    row = pl.BlockSpec((tm, d), lambda i: (i, 0))
    vec = pl.BlockSpec((1, d), lambda i: (0, 0))
    return pl.pallas_call(
        body, name=name, grid=(t // tm,), in_specs=[row, vec, row, row], out_specs=[row, vec],
        out_shape=[_sds((t, d), F32), _sds((1, d), F32)], compiler_params=_cparams(1))(x, gain, dh, dres)


def loss_head(x, gain, target, tm, name):
    t, d = x.shape

    def body(x_ref, g_ref, tgt_ref, dx_ref, dg_ref, loss_ref):
        i = pl.program_id(0)
        xf = x_ref[...]
        g = g_ref[...]
        r = lax.rsqrt(jnp.mean(xf * xf, axis=-1, keepdims=True) + RMS_EPS)
        xhat = xf * r
        err = xhat * g - tgt_ref[...]
        part = 0.5 * jnp.sum(jnp.mean(err * err, axis=-1, keepdims=True))
        dy = err * (1.0 / d)
        dxhat = dy * g
        dx_ref[...] = r * (dxhat - xhat * jnp.mean(dxhat * xhat, axis=-1, keepdims=True))
        dg = jnp.sum(dy * xhat, axis=0, keepdims=True)
        lpart = jnp.full((8, LANES), part, F32)

        @pl.when(i == 0)
        def _():
            dg_ref[...] = dg
            loss_ref[...] = lpart

        @pl.when(i > 0)
        def _():
            dg_ref[...] += dg
            loss_ref[...] += lpart

    row = pl.BlockSpec((tm, d), lambda i: (i, 0))
    vec = pl.BlockSpec((1, d), lambda i: (0, 0))
    return pl.pallas_call(
        body, name=name, grid=(t // tm,), in_specs=[row, vec, row],
        out_specs=[row, vec, pl.BlockSpec((8, LANES), lambda i: (0, 0))],
        out_shape=[_sds((t, d), F32), _sds((1, d), F32), _sds((8, LANES), F32)],
        compiler_params=_cparams(1))(x, gain, target)


def ffn_in_swiglu(hn, wa, s, tm, name):
    t = hn.shape[0]

    def body(h_ref, wg_ref, wu_ref, gu_ref, act_ref):
        h = h_ref[...]
        g = jnp.dot(h, wg_ref[...], preferred_element_type=F32)
        u = jnp.dot(h, wu_ref[...], preferred_element_type=F32)
        gu_ref[0] = g.astype(gu_ref.dtype)
        gu_ref[1] = u.astype(gu_ref.dtype)
        act_ref[...] = (g * _sigmoid(g) * u).astype(act_ref.dtype)

    return pl.pallas_call(
        body, name=name, grid=(t // tm, 4),
        in_specs=[pl.BlockSpec((tm, D_MODEL), lambda i, j: (i, 0)),
                  pl.BlockSpec((None, None, D_MODEL, FF_BLK), lambda i, j: (s, j, 0, 0)),
                  pl.BlockSpec((None, None, D_MODEL, FF_BLK), lambda i, j: (s, j + 4, 0, 0))],
        out_specs=[pl.BlockSpec((None, 2, tm, FF_BLK), lambda i, j: (j, 0, i, 0)),
                   pl.BlockSpec((None, tm, FF_BLK), lambda i, j: (j, i, 0))],
        out_shape=[_sds((4, 2, t, FF_BLK), BF16), _sds((4, t, FF_BLK), BF16)],
        compiler_params=_cparams(2))(hn, wa, wa)


def ffn_dact_swiglu(dy, wb, gu, s, tm, name):
    t = dy.shape[0]

    def body(dy_ref, w_ref, gu_ref, o_ref):
        da = 0.5 * lax.dot_general(dy_ref[...].astype(BF16), w_ref[...], NT, preferred_element_type=F32)
        g = gu_ref[0].astype(F32)
        u = gu_ref[1].astype(F32)
        sg = _sigmoid(g)
        o_ref[0] = (da * u * (sg * (1.0 + g * (1.0 - sg)))).astype(o_ref.dtype)
        o_ref[1] = (da * g * sg).astype(o_ref.dtype)

    blk = pl.BlockSpec((None, 2, tm, FF_BLK), lambda i, j: (j, 0, i, 0))
    return pl.pallas_call(
        body, name=name, grid=(t // tm, 4),
        in_specs=[pl.BlockSpec((tm, D_MODEL), lambda i, j: (i, 0)),
                  pl.BlockSpec((None, None, FF_BLK, D_MODEL), lambda i, j: (s, j, 0, 0)), blk],
        out_specs=blk, out_shape=_sds((4, 2, t, FF_BLK), BF16),
        compiler_params=_cparams(2))(dy, wb, gu)


def merge_fwd(gates, ya, yb, yc, tm, name):
    t, d = ya.shape

    def body(ga_ref, gb_ref, gc_ref, ya_ref, yb_ref, yc_ref, o_ref):
        m = (_sigmoid(ga_ref[...]) * ya_ref[...] + _sigmoid(gb_ref[...]) * yb_ref[...]
             + _sigmoid(gc_ref[...]) * yc_ref[...])
        o_ref[...] = m.astype(o_ref.dtype)

    row = pl.BlockSpec((tm, d), lambda i: (i, 0))
    gspecs = [pl.BlockSpec((tm, d), functools.partial(lambda i, a: (i, a), a=a)) for a in range(3)]
    return pl.pallas_call(
        body, name=name, grid=(t // tm,), in_specs=gspecs + [row, row, row], out_specs=row,
        out_shape=_sds((t, d), BF16), compiler_params=_cparams(1))(gates, gates, gates, ya, yb, yc)


def merge_bwd(dm, gates, ya, yb, yc, tm, name):
    t, d = ya.shape

    def body(dm_ref, g_ref, y_ref, dg_ref, dy_ref):
        dmv = dm_ref[...]
        s = _sigmoid(g_ref[...])
        dy_ref[...] = (dmv * s).astype(dy_ref.dtype)
        dg_ref[...] = (dmv * y_ref[...] * s * (1.0 - s)).astype(dg_ref.dtype)

    outs = []
    dgs = []
    for a, y in enumerate((ya, yb, yc)):
        row = pl.BlockSpec((tm, d), lambda i: (i, 0))
        gspec = pl.BlockSpec((tm, d), functools.partial(lambda i, a: (i, a), a=a))
        dg, dy = pl.pallas_call(
            functools.partial(body), name=f"{name}_{a}", grid=(t // tm,),
            in_specs=[row, gspec, row], out_specs=[row, row],
            out_shape=[_sds((t, d), BF16), _sds((t, d), BF16)],
            compiler_params=_cparams(1))(dm, gates, y)
        dgs.append(dg)
        outs.append(dy)
    return dgs, outs


def _iota2(shape, dim):
    return lax.broadcasted_iota(jnp.int32, shape, dim)


def forget_cumsum(f, name):
    t = f.shape[0]
    nq = t // QB

    def body(f_ref, fcol_ref, frow_ref, carry):
        j = pl.program_id(0)

        @pl.when(j == 0)
        def _():
            carry[...] = jnp.zeros_like(carry)

        logf = _log_sigmoid(f_ref[...])
        tri = (_iota2((QB, QB), 1) <= _iota2((QB, QB), 0)).astype(F32)
        blk = jnp.dot(tri, logf, precision=HIGHEST, preferred_element_type=F32) + carry[...]
        carry[...] += jnp.sum(logf, axis=0, keepdims=True)
        fcol_ref[...] = blk
        frow_ref[...] = blk.T[0:8, :]

    return pl.pallas_call(
        body, name=name, grid=(nq,),
        in_specs=[pl.BlockSpec((QB, LANES), lambda j: (j, 0))],
        out_specs=[pl.BlockSpec((QB, LANES), lambda j: (j, 0)),
                   pl.BlockSpec((None, 8, QB), lambda j: (j, 0, 0))],
        out_shape=[_sds((t, LANES), F32), _sds((nq, 8, QB), F32)],
        scratch_shapes=[pltpu.VMEM((1, LANES), F32)], compiler_params=_cparams(1))(f)


def forget_cumsum_bwd(dfrow, f, name):
    t = f.shape[0]
    nq = t // QB

    def body(dfr_ref, f_ref, df_ref, carry):
        jj = pl.program_id(0)

        @pl.when(jj == 0)
        def _():
            carry[...] = jnp.zeros_like(carry)

        padded = jnp.concatenate([dfr_ref[...], jnp.zeros((QB - 8, QB), F32)], axis=0)
        dfcol = padded.T
        tri = (_iota2((QB, QB), 1) >= _iota2((QB, QB), 0)).astype(F32)
        dlogf = jnp.dot(tri, dfcol, precision=HIGHEST, preferred_element_type=F32) + carry[...]
        carry[...] += jnp.sum(dfcol, axis=0, keepdims=True)
        df_ref[...] = dlogf * _sigmoid(-f_ref[...])

    return pl.pallas_call(
        body, name=name, grid=(nq,),
        in_specs=[pl.BlockSpec((None, 8, QB), lambda jj: (nq - 1 - jj, 0, 0)),
                  pl.BlockSpec((QB, LANES), lambda jj: (nq - 1 - jj, 0))],
        out_specs=pl.BlockSpec((QB, LANES), lambda jj: (nq - 1 - jj, 0)),
        out_shape=_sds((t, LANES), F32),
        scratch_shapes=[pltpu.VMEM((1, LANES), F32)], compiler_params=_cparams(1))(dfrow, f)


REL_DIAG = 768
REL_SHIFT = REL_DIAG - (QB - 1)


def _diag_onehot():
    u = _iota2((REL_PAD, REL_DIAG), 1)
    rel = jnp.clip(CH_KEYS - 1 - u, -MAX_REL, MAX_REL) + MAX_REL
    return (_iota2((REL_PAD, REL_DIAG), 0) == rel).astype(F32)


def rel_bias_build(tab_t, name):
    def body(tab_ref, o_ref):
        diag = jnp.dot(tab_ref[...], _diag_onehot(), precision=HIGHEST, preferred_element_type=F32)
        for h in range(N_HEADS_CH):
            rows = jnp.broadcast_to(diag[h:h + 1, :], (QB, REL_DIAG))
            o_ref[h] = pltpu.roll(rows, REL_SHIFT, 1, stride=1, stride_axis=0)[:, :CH_KEYS]

    return pl.pallas_call(
        body, name=name, out_shape=_sds((N_HEADS_CH, QB, CH_KEYS), F32),
        in_specs=[pl.BlockSpec(memory_space=pltpu.VMEM)], out_specs=pl.BlockSpec(memory_space=pltpu.VMEM),
    )(tab_t)


def rel_bias_scatter(dbias, name):
    def body(db_ref, o_ref, ddiag):
        flip = (_iota2((QB, QB), 0) + _iota2((QB, QB), 1) == QB - 1).astype(F32)
        for h in range(N_HEADS_CH):
            padded = jnp.concatenate([db_ref[h], jnp.zeros((QB, REL_DIAG - CH_KEYS), F32)], axis=1)
            flipped = jnp.dot(flip, padded, precision=HIGHEST, preferred_element_type=F32)
            unrolled = pltpu.roll(flipped, 0, 1, stride=1, stride_axis=0)
            ddiag[h:h + 1, :] = jnp.sum(unrolled, axis=0, keepdims=True)
        o_ref[...] = lax.dot_general(ddiag[...], _diag_onehot(), NT, precision=HIGHEST,
                                     preferred_element_type=F32)

    return pl.pallas_call(
        body, name=name, out_shape=_sds((N_HEADS_CH, REL_PAD), F32),
        in_specs=[pl.BlockSpec(memory_space=pltpu.VMEM)], out_specs=pl.BlockSpec(memory_space=pltpu.VMEM),
        scratch_shapes=[pltpu.VMEM((N_HEADS_CH, REL_DIAG), F32)],
    )(dbias)


def _hl(h):
    return slice(h * HEAD_DIM, (h + 1) * HEAD_DIM)


def _split_dot(x, tri_bf16):
    hi = x.astype(BF16)
    lo = (x - hi.astype(F32)).astype(BF16)
    return (jnp.dot(hi, tri_bf16, preferred_element_type=F32)
            + jnp.dot(lo, tri_bf16, preferred_element_type=F32))


def _rows(j):
    return pl.ds(pl.multiple_of(j * QB, QB), QB)


def _krows(g):
    return pl.ds(pl.multiple_of(g * KB, KB), KB)


def _log_sigmoid_pair(z):
    sp = jnp.log(1.0 + jnp.exp(-jnp.abs(z)))
    return jnp.minimum(z, 0.0) - sp, -jnp.maximum(z, 0.0) - sp


def _qkv_specs(t, col0, n_pairs):
    q_spec = pl.BlockSpec((QB, LANES), lambda hp, i: (i, col0 + hp))
    k_spec = pl.BlockSpec((t, LANES), lambda hp, i: (0, col0 + n_pairs + hp))
    v_spec = pl.BlockSpec((t, LANES), lambda hp, i: (0, col0 + 2 * n_pairs + hp))
    return q_spec, k_spec, v_spec


def sb_fwd(qkv, name):
    t = qkv.shape[0]
    nq = t // QB

    def body(q_ref, k_ref, v_ref, o_ref):
        i = pl.program_id(1)
        groups = i // KSUB + 1
        tri_after = (_iota2((KB, KB), 0) > _iota2((KB, KB), 1)).astype(BF16)
        t_idx = i * QB + _iota2((QB, KB), 0)
        qs = [q_ref[:, _hl(h)] for h in range(2)]

        def step(gg, carry):
            g = groups - 1 - gg
            strict = (g * KB + _iota2((QB, KB), 1)) < t_idx
            out = []
            for h in range(2):
                tail, acc = carry[2 * h], carry[2 * h + 1]
                k = k_ref[_krows(g), _hl(h)]
                v = v_ref[_krows(g), _hl(h)]
                z = lax.dot_general(qs[h], k, NT, preferred_element_type=F32) * SCALE
                lb, lf = _log_sigmoid_pair(z)
                lf = jnp.where(strict, lf, 0.0)
                between = _split_dot(lf, tri_after) + tail
                w = jnp.where(strict, jnp.exp(lb + between), 0.0)
                acc = acc + jnp.dot(w.astype(BF16), v, preferred_element_type=F32)
                out += [tail + jnp.sum(lf, axis=1, keepdims=True), acc]
            return tuple(out)

        init = (jnp.zeros((QB, 1), F32), jnp.zeros((QB, HEAD_DIM), F32)) * 2
        res = lax.fori_loop(0, groups, step, init)
        for h in range(2):
            o_ref[:, _hl(h)] = res[2 * h + 1].astype(o_ref.dtype)

    q_spec, k_spec, v_spec = _qkv_specs(t, 0, 2)
    return pl.pallas_call(
        body, name=name, grid=(2, nq), in_specs=[q_spec, k_spec, v_spec],
        out_specs=pl.BlockSpec((QB, LANES), lambda hp, i: (i, hp)),
        out_shape=_sds((t, W_SB), BF16), compiler_params=_cparams(2))(qkv, qkv, qkv)


def sb_bwd(qkv, do, name):
    t = qkv.shape[0]
    nq = t // QB

    def body(q_ref, k_ref, v_ref, do_ref, dq_ref, dk_ref, dv_ref, w_scr):
        i = pl.program_id(1)

        @pl.when(i == 0)
        def _():
            dk_ref[...] = jnp.zeros_like(dk_ref)
            dv_ref[...] = jnp.zeros_like(dv_ref)

        groups = i // KSUB + 1
        tri_after = (_iota2((KB, KB), 0) > _iota2((KB, KB), 1)).astype(BF16)
        tri_before = (_iota2((KB, KB), 0) < _iota2((KB, KB), 1)).astype(BF16)
        t_idx = i * QB + _iota2((QB, KB), 0)
        qs = [q_ref[:, _hl(h)] for h in range(2)]
        dos = [do_ref[:, _hl(h)] for h in range(2)]

        def weights(gg, tails):
            g = groups - 1 - gg
            strict = (g * KB + _iota2((QB, KB), 1)) < t_idx
            out = []
            for h in range(2):
                k = k_ref[_krows(g), _hl(h)]
                z = lax.dot_general(qs[h], k, NT, preferred_element_type=F32) * SCALE
                lb, lf = _log_sigmoid_pair(z)
                lf = jnp.where(strict, lf, 0.0)
                between = _split_dot(lf, tri_after) + tails[h]
                w_scr[h, g] = jnp.where(strict, jnp.exp(lb + between), 0.0)
                out.append(tails[h] + jnp.sum(lf, axis=1, keepdims=True))
            return tuple(out)

        lax.fori_loop(0, groups, weights, (jnp.zeros((QB, 1), F32),) * 2)

        def grads(g, carry):
            strict = (g * KB + _iota2((QB, KB), 1)) < t_idx
            out = []
            for h in range(2):
                head, dq = carry[2 * h], carry[2 * h + 1]
                k = k_ref[_krows(g), _hl(h)]
                v = v_ref[_krows(g), _hl(h)]
                w = w_scr[h, g]
                z = lax.dot_general(qs[h], k, NT, preferred_element_type=F32) * SCALE
                beta = _sigmoid(z)
                e = lax.dot_general(dos[h], v, NT, preferred_element_type=F32) * w
                before = _split_dot(e, tri_before) + head
                dz = jnp.where(strict, e * (1.0 - beta) - before * beta, 0.0) * SCALE
                dzb = dz.astype(BF16)
                dq = dq + jnp.dot(dzb, k, preferred_element_type=F32)
                dk_ref[_krows(g), _hl(h)] += lax.dot_general(dzb, qs[h], TN, preferred_element_type=F32)
                dv_ref[_krows(g), _hl(h)] += lax.dot_general(w.astype(BF16), dos[h], TN,
                                                             preferred_element_type=F32)
                out += [head + jnp.sum(e, axis=1, keepdims=True), dq]
            return tuple(out)

        init = (jnp.zeros((QB, 1), F32), jnp.zeros((QB, HEAD_DIM), F32)) * 2
        res = lax.fori_loop(0, groups, grads, init)
        for h in range(2):
            dq_ref[:, _hl(h)] = res[2 * h + 1].astype(dq_ref.dtype)

    q_spec, k_spec, v_spec = _qkv_specs(t, 0, 2)
    blk = pl.BlockSpec((QB, LANES), lambda hp, i: (i, hp))
    full = pl.BlockSpec((t, LANES), lambda hp, i: (0, hp))
    return pl.pallas_call(
        body, name=name, grid=(2, nq), in_specs=[q_spec, k_spec, v_spec, blk],
        out_specs=[blk, full, full],
        out_shape=[_sds((t, W_SB), BF16), _sds((t, W_SB), F32), _sds((t, W_SB), F32)],
        scratch_shapes=[pltpu.VMEM((2, t // KB, QB, KB), F32)],
        compiler_params=_cparams(2))(qkv, qkv, qkv, do)


def fox_fwd(qkv, fcol, frow, name):
    t = qkv.shape[0]
    nq = t // QB

    def body(q_ref, k_ref, v_ref, fc_ref, fr_ref, o_ref, lse_ref):
        hp = pl.program_id(0)
        i = pl.program_id(1)
        groups = i // KSUB + 1
        t_idx = i * QB + _iota2((QB, KB), 0)
        lane = _iota2((QB, LANES), 1)
        sub = _iota2((8, KB), 0)
        qs = [q_ref[:, _hl(h)] for h in range(2)]
        f_qs = [jnp.sum(jnp.where(lane == hp * 2 + h, fc_ref[...], 0.0), axis=1, keepdims=True)
                for h in range(2)]

        def step(g, carry):
            causal = (g * KB + _iota2((QB, KB), 1)) <= t_idx
            fr = fr_ref[g]
            out = []
            for h in range(2):
                m, l, acc = carry[3 * h:3 * h + 3]
                k = k_ref[_krows(g), _hl(h)]
                v = v_ref[_krows(g), _hl(h)]
                f_k = jnp.sum(jnp.where(sub == hp * 2 + h, fr, 0.0), axis=0, keepdims=True)
                z = lax.dot_general(qs[h], k, NT, preferred_element_type=F32) * SCALE + f_qs[h] - f_k
                z = jnp.where(causal, z, NEG)
                m_new = jnp.maximum(m, jnp.max(z, axis=1, keepdims=True))
                p = jnp.exp(z - m_new)
                corr = jnp.exp(m - m_new)
                l = l * corr + jnp.sum(p, axis=1, keepdims=True)
                acc = acc * corr + jnp.dot(p.astype(BF16), v, preferred_element_type=F32)
                out += [m_new, l, acc]
            return tuple(out)

        init = (jnp.full((QB, 1), NEG, F32), jnp.zeros((QB, 1), F32), jnp.zeros((QB, HEAD_DIM), F32)) * 2
        res = lax.fori_loop(0, groups, step, init)
        for h in range(2):
            m, l, acc = res[3 * h:3 * h + 3]
            o_ref[:, _hl(h)] = (acc / l).astype(o_ref.dtype)
            lse_ref[:, _hl(h)] = jnp.broadcast_to(m + jnp.log(l), (QB, HEAD_DIM))

    q_spec, k_spec, v_spec = _qkv_specs(t, 18, 2)
    blk = pl.BlockSpec((QB, LANES), lambda hp, i: (i, hp))
    return pl.pallas_call(
        body, name=name, grid=(2, nq),
        in_specs=[q_spec, k_spec, v_spec, pl.BlockSpec((QB, LANES), lambda hp, i: (i, 0)),
                  pl.BlockSpec((t // KB, 8, KB), lambda hp, i: (0, 0, 0))],
        out_specs=[blk, blk],
        out_shape=[_sds((t, W_FOX), BF16), _sds((t, W_FOX), F32)],
        compiler_params=_cparams(2))(qkv, qkv, qkv, fcol, frow)


def fox_bwd(qkv, fcol, frow, o, lse, do, name):
    t = qkv.shape[0]
    nq = t // QB

    def body(q_ref, k_ref, v_ref, fc_ref, fr_ref, o_ref, lse_ref, do_ref,
             dq_ref, dk_ref, dv_ref, dfr_ref):
        hp = pl.program_id(0)
        i = pl.program_id(1)

        @pl.when(i == 0)
        def _():
            dk_ref[...] = jnp.zeros_like(dk_ref)
            dv_ref[...] = jnp.zeros_like(dv_ref)

        @pl.when((i == 0) & (hp == 0))
        def _():
            dfr_ref[...] = jnp.zeros_like(dfr_ref)

        groups = i // KSUB + 1
        t_idx = i * QB + _iota2((QB, KB), 0)
        lane = _iota2((QB, LANES), 1)
        sub = _iota2((8, KB), 0)
        qs = [q_ref[:, _hl(h)] for h in range(2)]
        dos = [do_ref[:, _hl(h)] for h in range(2)]
        f_qs = [jnp.sum(jnp.where(lane == hp * 2 + h, fc_ref[...], 0.0), axis=1, keepdims=True)
                for h in range(2)]
        lse_qs = [lse_ref[:, h * HEAD_DIM:h * HEAD_DIM + 1] for h in range(2)]
        deltas = [jnp.sum(dos[h].astype(F32) * o_ref[:, _hl(h)].astype(F32), axis=1, keepdims=True)
                  for h in range(2)]

        def step(g, dqs):
            causal = (g * KB + _iota2((QB, KB), 1)) <= t_idx
            fr = fr_ref[g]
            out = []
            dfr = jnp.zeros((8, KB), F32)
            for h in range(2):
                k = k_ref[_krows(g), _hl(h)]
                v = v_ref[_krows(g), _hl(h)]
                f_k = jnp.sum(jnp.where(sub == hp * 2 + h, fr, 0.0), axis=0, keepdims=True)
                z = lax.dot_general(qs[h], k, NT, preferred_element_type=F32) * SCALE + f_qs[h] - f_k
                p = jnp.where(causal, jnp.exp(z - lse_qs[h]), 0.0)
                dp = lax.dot_general(dos[h], v, NT, preferred_element_type=F32)
                ds = p * (dp - deltas[h])
                dsb = (ds * SCALE).astype(BF16)
                out.append(dqs[h] + jnp.dot(dsb, k, preferred_element_type=F32))
                dk_ref[_krows(g), _hl(h)] += lax.dot_general(dsb, qs[h], TN, preferred_element_type=F32)
                dv_ref[_krows(g), _hl(h)] += lax.dot_general(p.astype(BF16), dos[h], TN,
                                                             preferred_element_type=F32)
                colsum = jnp.sum(ds, axis=0, keepdims=True)
                dfr = dfr + jnp.where(sub == hp * 2 + h, -colsum, 0.0)
            dfr_ref[g] += dfr
            return tuple(out)

        res = lax.fori_loop(0, groups, step, (jnp.zeros((QB, HEAD_DIM), F32),) * 2)
        for h in range(2):
            dq_ref[:, _hl(h)] = res[h].astype(dq_ref.dtype)

    q_spec, k_spec, v_spec = _qkv_specs(t, 18, 2)
    blk = pl.BlockSpec((QB, LANES), lambda hp, i: (i, hp))
    full = pl.BlockSpec((t, LANES), lambda hp, i: (0, hp))
    frs = pl.BlockSpec((t // KB, 8, KB), lambda hp, i: (0, 0, 0))
    return pl.pallas_call(
        body, name=name, grid=(2, nq),
        in_specs=[q_spec, k_spec, v_spec, pl.BlockSpec((QB, LANES), lambda hp, i: (i, 0)), frs,
                  blk, blk, blk],
        out_specs=[blk, full, full, frs],
        out_shape=[_sds((t, W_FOX), BF16), _sds((t, W_FOX), F32), _sds((t, W_FOX), F32),
                   _sds((t // KB, 8, KB), F32)],
        compiler_params=_cparams(2))(qkv, qkv, qkv, fcol, frow, o, lse, do)


def _frow_to_groups(frow):
    n = frow.shape[0] // KSUB
    return frow.reshape(n, KSUB, 8, QB).transpose(0, 2, 1, 3).reshape(n, 8, KB)


def _frow_from_groups(frow):
    n = frow.shape[0]
    return frow.reshape(n, 8, KSUB, QB).transpose(0, 2, 1, 3).reshape(n * KSUB, 8, QB)


def _chunk_valid(i):
    qi = _iota2((QB, CH_KEYS), 0)
    kj = _iota2((QB, CH_KEYS), 1)
    dchunk = (qi >> 6) + LEFT_CHUNKS - (kj >> 6)
    return (dchunk >= 0) & (dchunk <= LEFT_CHUNKS) & ((i - (CH_WIN - 1)) * QB + kj >= 0)


CH_PAD = (CH_WIN - 1) * QB


def _window(i):
    return pl.ds(pl.multiple_of(i * QB, QB), CH_KEYS)


def _chunk_probs(q, kw, bias, valid):
    z = lax.dot_general(q, kw, NT, preferred_element_type=F32) * SCALE + bias
    z = jnp.where(valid, z, NEG)
    z = z - jnp.max(z, axis=1, keepdims=True)
    p = jnp.exp(z)
    return p / jnp.sum(p, axis=1, keepdims=True)


def _chunk_specs(t):
    q_spec = pl.BlockSpec((QB, LANES), lambda hp, i: (i, 6 + hp))
    kv_spec = pl.BlockSpec((t + CH_PAD, LANES), lambda hp, i: (0, hp))
    return q_spec, kv_spec


def chunk_fwd(qkv, kp, vp, bias, name):
    t = qkv.shape[0]
    nq = t // QB

    def body(q_ref, k_ref, v_ref, b_ref, o_ref):
        i = pl.program_id(1)
        valid = _chunk_valid(i)
        for h in range(2):
            p = _chunk_probs(q_ref[:, _hl(h)], k_ref[_window(i), _hl(h)], b_ref[h], valid)
            o_ref[:, _hl(h)] = jnp.dot(p.astype(BF16), v_ref[_window(i), _hl(h)],
                                       preferred_element_type=F32).astype(o_ref.dtype)

    q_spec, kv_spec = _chunk_specs(t)
    return pl.pallas_call(
        body, name=name, grid=(4, nq),
        in_specs=[q_spec, kv_spec, kv_spec, pl.BlockSpec((2, QB, CH_KEYS), lambda hp, i: (hp, 0, 0))],
        out_specs=pl.BlockSpec((QB, LANES), lambda hp, i: (i, hp)),
        out_shape=_sds((t, W_CH), BF16), compiler_params=_cparams(2))(qkv, kp, vp, bias)


def chunk_bwd(qkv, kp, vp, bias, do, name):
    t = qkv.shape[0]
    nq = t // QB

    def body(q_ref, k_ref, v_ref, b_ref, do_ref, dq_ref, dk_ref, dv_ref, db_ref):
        i = pl.program_id(1)

        @pl.when(i == 0)
        def _():
            dk_ref[...] = jnp.zeros_like(dk_ref)
            dv_ref[...] = jnp.zeros_like(dv_ref)
            db_ref[...] = jnp.zeros_like(db_ref)

        valid = _chunk_valid(i)
        for h in range(2):
            q = q_ref[:, _hl(h)]
            dov = do_ref[:, _hl(h)]
            kw = k_ref[_window(i), _hl(h)]
            p = _chunk_probs(q, kw, b_ref[h], valid)
            dp = lax.dot_general(dov, v_ref[_window(i), _hl(h)], NT, preferred_element_type=F32)
            ds = p * (dp - jnp.sum(p * dp, axis=1, keepdims=True))
            db_ref[h] += ds
            dsb = (ds * SCALE).astype(BF16)
            dq_ref[:, _hl(h)] = jnp.dot(dsb, kw, preferred_element_type=F32).astype(dq_ref.dtype)
            dk_ref[_window(i), _hl(h)] += lax.dot_general(dsb, q, TN, preferred_element_type=F32)
            dv_ref[_window(i), _hl(h)] += lax.dot_general(p.astype(BF16), dov, TN,
                                                          preferred_element_type=F32)

    q_spec, kv_spec = _chunk_specs(t)
    blk = pl.BlockSpec((QB, LANES), lambda hp, i: (i, hp))
    bspec = pl.BlockSpec((2, QB, CH_KEYS), lambda hp, i: (hp, 0, 0))
    return pl.pallas_call(
        body, name=name, grid=(4, nq), in_specs=[q_spec, kv_spec, kv_spec, bspec, blk],
        out_specs=[blk, kv_spec, kv_spec, bspec],
        out_shape=[_sds((t, W_CH), BF16), _sds((t + CH_PAD, W_CH), F32), _sds((t + CH_PAD, W_CH), F32),
                   _sds((N_HEADS_CH, QB, CH_KEYS), F32)],
        compiler_params=_cparams(2))(qkv, kp, vp, bias, do)


def _sum_parts(p_ref):
    total = p_ref[0].astype(F32)
    for p in range(1, p_ref.shape[0]):
        total = total + p_ref[p].astype(F32)
    return total


def sum_parts(parts, grid, p_spec, o_spec, out_sds, name):
    def body(p_ref, o_ref):
        o_ref[...] = _sum_parts(p_ref)

    return pl.pallas_call(body, name=name, grid=grid, in_specs=[p_spec], out_specs=o_spec,
                          out_shape=out_sds, compiler_params=_cparams(len(grid)))(parts)


def adamw(parts, w, m, v, grid, p_specs, w_spec, name):
    c1 = 1.0 / (1.0 - ADAM_B1 ** ADAM_STEP)
    c2 = 1.0 / (1.0 - ADAM_B2 ** ADAM_STEP)
    n = len(parts)

    def body(*refs):
        w_ref, m_ref, v_ref, g_out, d_out, m_out, v_out = refs[n:]
        g = _sum_parts(refs[0])
        for q in range(1, n):
            g = jnp.where(pl.program_id(0) == q, _sum_parts(refs[q]), g)
        m_new = ADAM_B1 * m_ref[...] + (1.0 - ADAM_B1) * g
        v_new = ADAM_B2 * v_ref[...] + (1.0 - ADAM_B2) * (g * g)
        m_hat = m_new * c1
        v_hat = v_new * c2
        g_out[...] = g
        d_out[...] = -ADAM_LR * (m_hat / (jnp.sqrt(v_hat) + ADAM_EPS) + ADAM_WD * w_ref[...])
        m_out[...] = m_new
        v_out[...] = v_new

    out = _sds(w.shape, F32)
    return pl.pallas_call(
        body, name=name, grid=grid, in_specs=[*p_specs, w_spec, w_spec, w_spec],
        out_specs=[w_spec] * 4, out_shape=[out] * 4,
        compiler_params=_cparams(len(grid)))(*parts, w, m, v)


def _ffn_fwd(x, gain, wa, wb, s, tm, tag, deps=()):
    t = x.shape[0]
    hn = rmsnorm_fwd(x, gain, tm, f"rms_{tag}", deps)
    gu, act = ffn_in_swiglu(hn, wa, s, tm, f"ffn_in_{tag}")
    row = pl.BlockSpec((tm, D_MODEL), lambda i, j, k: (i, 0))
    y = matmul(NN, act, wb, _sds((t, D_MODEL), F32), (t // tm, 1, 4),
               pl.BlockSpec((None, tm, FF_BLK), lambda i, j, k: (k, i, 0)),
               pl.BlockSpec((None, None, FF_BLK, D_MODEL), lambda i, j, k: (s, k, 0, 0)),
               row, (tm, D_MODEL), name=f"ffn_out_{tag}", alpha=0.5, res=x, res_spec=row)
    return y, (hn, gu, act)


def _ffn_bwd(dy, x, gain, saved, wa, wb, s, tm, tag, on_grads):
    t = x.shape[0]
    hn, gu, act = saved
    nt = t // tm
    row = pl.BlockSpec((tm, D_MODEL), lambda i, j, k: (i, 0))
    dgu = ffn_dact_swiglu(dy, wb, gu, s, tm, f"ffn_dact_{tag}")
    dwb = matmul(TN, act, dy, _sds((4, FF_BLK, D_MODEL), BF16), (4, 1, nt),
                 pl.BlockSpec((None, tm, FF_BLK), lambda i, j, k: (i, k, 0)),
                 pl.BlockSpec((tm, D_MODEL), lambda i, j, k: (k, 0)),
                 pl.BlockSpec((None, FF_BLK, D_MODEL), lambda i, j, k: (i, 0, 0)),
                 (FF_BLK, D_MODEL), name=f"ffn_dwout_{tag}", alpha=0.5)
    dwa = matmul(TN, hn, dgu, _sds((8, D_MODEL, FF_BLK), BF16), (1, 8, nt),
                 pl.BlockSpec((tm, D_MODEL), lambda i, j, k: (k, 0)),
                 pl.BlockSpec((None, None, tm, FF_BLK), lambda i, j, k: (j % 4, j // 4, k, 0)),
                 pl.BlockSpec((None, D_MODEL, FF_BLK), lambda i, j, k: (j, 0, 0)),
                 (D_MODEL, FF_BLK), name=f"ffn_dwin_{tag}")
    deps = on_grads(dwa, dwb)
    dhn = matmul(NT, dgu, wa, _sds((t, D_MODEL), F32), (nt, 1, 8),
                 pl.BlockSpec((None, None, tm, FF_BLK), lambda i, j, k: (k % 4, k // 4, i, 0)),
                 pl.BlockSpec((None, None, D_MODEL, FF_BLK), lambda i, j, k: (s, k, 0, 0)),
                 row, (tm, D_MODEL), name=f"ffn_dh_{tag}", deps=deps)
    dx, dgain = rmsnorm_bwd(x, gain, dhn, dy, tm, f"rms_bwd_{tag}")
    return dx, dgain


BR_ROWS = ((0, 1), (1, 2), (3, 1))


def _mixer_fwd(x, gain, wqkv, wf, wgate, wbr, wout, bq, bf, bg, bias, layer, tm, tag):
    t = x.shape[0]
    nt = t // tm
    hm = rmsnorm_fwd(x, gain, tm, f"rms_{tag}")
    a_full = pl.BlockSpec((tm, D_MODEL), lambda i, j, k: (i, 0))
    wide_out = pl.BlockSpec((tm, D_MODEL), lambda i, j, k: (i, j))
    wide_b = pl.BlockSpec((1, D_MODEL), lambda i, j, k: (0, j))
    qkv = matmul(NN, hm, wqkv, _sds((t, QKV_WIDTH), BF16), (nt, 3, 1), a_full,
                 pl.BlockSpec((None, D_MODEL, D_MODEL), lambda i, j, k: (layer, 0, j)), wide_out, None,
                 name=f"proj_qkv_{tag}", bias=bq, bias_spec=wide_b)
    gates = matmul(NN, hm, wgate, _sds((t, 3 * D_MODEL), F32), (nt, 3, 1), a_full,
                   pl.BlockSpec((None, D_MODEL, D_MODEL), lambda i, j, k: (layer + 1, 0,j)), wide_out,
                   None, name=f"proj_gate_{tag}", bias=bg, bias_spec=wide_b)
    f = matmul(NN, hm, wf, _sds((t, LANES), F32), (nt, 1, 1), a_full,
               pl.BlockSpec((None, D_MODEL, LANES), lambda i, j, k: (layer, 0, 0)),
               pl.BlockSpec((tm, LANES), lambda i, j, k: (i, 0)), None,
               name=f"proj_f_{tag}", bias=bf, bias_spec=pl.BlockSpec((1, LANES), lambda i, j, k: (0, 0)))
    fcol, frow = forget_cumsum(f, f"fcum_{tag}")
    frow = _frow_to_groups(frow)
    o_sb = sb_fwd(qkv, f"sb_fwd_{tag}")
    kp = jnp.pad(qkv[:, 10 * LANES:14 * LANES], ((CH_PAD, 0), (0, 0)))
    vp = jnp.pad(qkv[:, 14 * LANES:18 * LANES], ((CH_PAD, 0), (0, 0)))
    o_ch = chunk_fwd(qkv, kp, vp, bias, f"chunk_fwd_{tag}")
    o_fox, lse = fox_fwd(qkv, fcol, frow, f"fox_fwd_{tag}")
    ys = []
    for a, (o, (r0, nr)) in enumerate(zip((o_sb, o_ch, o_fox), BR_ROWS)):
        ys.append(matmul(
            NN, o, wbr, _sds((t, D_MODEL), F32), (nt, 1, nr),
            pl.BlockSpec((tm, 256), lambda i, j, k: (i, k)),
            pl.BlockSpec((None, 256, D_MODEL), functools.partial(lambda i, j, k, r0: (layer, r0 + k, 0), r0=r0)),
            a_full, (tm, D_MODEL), name=f"branch{a}_{tag}"))
    merged = merge_fwd(gates, ys[0], ys[1], ys[2], tm, f"merge_{tag}")
    x_new = matmul(NN, merged, wout, _sds((t, D_MODEL), F32), (nt, 1, 1), a_full,
                   pl.BlockSpec((None, D_MODEL, D_MODEL), lambda i, j, k: (layer, 0, 0)), a_full, None,
                   name=f"wout_{tag}", res=x, res_spec=a_full)
    saved = (hm, qkv, gates, f, fcol, frow, o_sb, o_ch, o_fox, lse, ys, merged, kp, vp)
    return x_new, saved


def _mixer_bwd(dy, x, gain, saved, wqkv, wf, wgate, wbr, wout, bias, layer, tm, tag, on_grads):
    t = x.shape[0]
    nt = t // tm
    hm, qkv, gates, f, fcol, frow, o_sb, o_ch, o_fox, lse, ys, merged, kp, vp = saved
    a_full = pl.BlockSpec((tm, D_MODEL), lambda i, j, k: (i, 0))
    red_row = pl.BlockSpec((tm, D_MODEL), lambda i, j, k: (k, 0))
    sq = pl.BlockSpec((D_MODEL, D_MODEL), lambda i, j, k: (0, 0))
    dmerged = matmul(NT, dy, wout, _sds((t, D_MODEL), F32), (nt, 1, 1), a_full,
                     pl.BlockSpec((None, D_MODEL, D_MODEL), lambda i, j, k: (layer, 0, 0)), a_full, None,
                     name=f"dmerged_{tag}")
    dwout = matmul(TN, merged, dy, _sds((D_MODEL, D_MODEL), BF16), (1, 1, nt), red_row, red_row, sq,
                   (D_MODEL, D_MODEL), name=f"dwout_{tag}")
    dgs, dys = merge_bwd(dmerged, gates, ys[0], ys[1], ys[2], tm, f"merge_bwd_{tag}")
    dos, dwbrs = [], []
    for a, (o, (r0, nr)) in enumerate(zip((o_sb, o_ch, o_fox), BR_ROWS)):
        dos.append(matmul(
            NT, dys[a], wbr, _sds((t, nr * 256), BF16), (nt, nr, 1), a_full,
            pl.BlockSpec((None, 256, D_MODEL), functools.partial(lambda i, j, k, r0: (layer, r0 + j, 0), r0=r0)),
            pl.BlockSpec((tm, 256), lambda i, j, k: (i, j)), None, name=f"dbranch{a}_{tag}"))
        dwbrs.append(matmul(
            TN, o, dys[a], _sds((nr * 256, D_MODEL), BF16), (nr, 1, nt),
            pl.BlockSpec((tm, 256), lambda i, j, k: (k, i)), red_row,
            pl.BlockSpec((256, D_MODEL), lambda i, j, k: (i, 0)), (256, D_MODEL), name=f"dwbr{a}_{tag}"))
    dq_a, dk_a, dv_a = sb_bwd(qkv, dos[0], f"sb_bwd_{tag}")
    dq_b, dk_b, dv_b, dbias = chunk_bwd(qkv, kp, vp, bias, dos[1], f"chunk_bwd_{tag}")
    dk_b, dv_b = dk_b[CH_PAD:], dv_b[CH_PAD:]
    dq_c, dk_c, dv_c, dfrow = fox_bwd(qkv, fcol, frow, o_fox, lse, dos[2], f"fox_bwd_{tag}")
    df = forget_cumsum_bwd(_frow_from_groups(dfrow), f, f"fcum_bwd_{tag}")
    dqkv = jnp.concatenate([p.astype(BF16) for p in
                            (dq_a, dk_a, dv_a, dq_b, dk_b, dv_b, dq_c, dk_c, dv_c)], axis=1)
    dgates = jnp.concatenate(dgs, axis=1)
    dtab = rel_bias_scatter(dbias, f"rel_scatter_{tag}")

    wide_b = pl.BlockSpec((tm, D_MODEL), lambda i, j, k: (k, j))
    wide_o = pl.BlockSpec((D_MODEL, D_MODEL), lambda i, j, k: (0, j))
    wide_cs = pl.BlockSpec((1, D_MODEL), lambda i, j, k: (0, j))
    dwqkv, dbq = matmul(TN, hm, dqkv, _sds((D_MODEL, QKV_WIDTH), BF16), (1, 3, nt), red_row, wide_b,
                        wide_o, (D_MODEL, D_MODEL), name=f"dwqkv_{tag}",
                        colsum_sds=_sds((1, QKV_WIDTH), F32), colsum_spec=wide_cs)
    dwgate, dbg = matmul(TN, hm, dgates, _sds((D_MODEL, 3 * D_MODEL), BF16), (1, 3, nt), red_row,
                         wide_b, wide_o, (D_MODEL, D_MODEL), name=f"dwgate_{tag}",
                         colsum_sds=_sds((1, 3 * D_MODEL), F32), colsum_spec=wide_cs)
    dwf, dbf = matmul(TN, hm, df, _sds((D_MODEL, LANES), BF16), (1, 1, nt), red_row,
                      pl.BlockSpec((tm, LANES), lambda i, j, k: (k, 0)),
                      pl.BlockSpec((D_MODEL, LANES), lambda i, j, k: (0, 0)), (D_MODEL, LANES),
                      name=f"dwf_{tag}", colsum_sds=_sds((1, LANES), F32),
                      colsum_spec=pl.BlockSpec((1, LANES), lambda i, j, k: (0, 0)))
    dwbr = jnp.concatenate(dwbrs, axis=0)
    deps = on_grads(dict(dwqkv=dwqkv, dwgate=dwgate, dwf=dwf, dwbr=dwbr, dwout=dwout))
    wide_a = pl.BlockSpec((tm, D_MODEL), lambda i, j, k: (i, k))
    dhm = matmul(NT, dqkv, wqkv, _sds((t, D_MODEL), F32), (nt, 1, 3), wide_a,
                 pl.BlockSpec((None, D_MODEL, D_MODEL), lambda i, j, k: (layer, 0, k)), a_full,
                 (tm, D_MODEL), name=f"dhm_qkv_{tag}", deps=deps)
    dhm = matmul(NT, dgates, wgate, _sds((t, D_MODEL), F32), (nt, 1, 3), wide_a,
                 pl.BlockSpec((None, D_MODEL, D_MODEL), lambda i, j, k: (layer + 1, 0,k)), a_full,
                 (tm, D_MODEL), name=f"dhm_gate_{tag}", res=dhm, res_spec=a_full)
    dhm = matmul(NT, df, wf, _sds((t, D_MODEL), F32), (nt, 1, 1),
                 pl.BlockSpec((tm, LANES), lambda i, j, k: (i, 0)),
                 pl.BlockSpec((None, D_MODEL, LANES), lambda i, j, k: (layer, 0, 0)), a_full, None,
                 name=f"dhm_f_{tag}", res=dhm, res_spec=a_full)
    dx, dgain = rmsnorm_bwd(x, gain, dhm, dy, tm, f"rms_bwd_{tag}")
    return dx, dict(dbq=dbq, dbg=dbg, dbf=dbf, dtab=dtab, dgain=dgain)


def _pack_small(pieces):
    flat = jnp.concatenate([p.reshape(-1).astype(F32) for p in pieces])
    flat = jnp.pad(flat, (0, SMALL_ROWS * LANES - flat.shape[0]))
    return flat.reshape(SMALL_ROWS, LANES)


def _unpack_small(packed, shapes):
    flat = packed.reshape(-1)
    out, pos = [], 0
    for shp in shapes:
        n = int(np.prod(shp))
        out.append(flat[pos:pos + n].reshape(shp))
        pos += n
    return out


def kernel(x, g_ffn1, w_ffn1_in, w_ffn1_out, g_mix, w_in, b_in, rel_bias, w_br_sb, w_br_ch, w_br_fox, w_out, g_ffn2, w_ffn2_in, w_ffn2_out, g_final, loss_target, m_g_ffn1, m_w_ffn1_in, m_w_ffn1_out, m_g_mix, m_w_in, m_b_in, m_rel_bias, m_w_br_sb, m_w_br_ch, m_w_br_fox, m_w_out, m_g_ffn2, m_w_ffn2_in, m_w_ffn2_out, m_g_final, v_g_ffn1, v_w_ffn1_in, v_w_ffn1_out, v_g_mix, v_w_in, v_b_in, v_rel_bias, v_w_br_sb, v_w_br_ch, v_w_br_fox, v_w_out, v_g_ffn2, v_w_ffn2_in, v_w_ffn2_out, v_g_final):
    t = x.shape[1]
    tm = min(512, t)
    xs = x[0]
    target = loss_target[0]
    f_lo, f_hi = QKV_WIDTH, QKV_WIDTH + N_HEADS_FOX

    def ffn_shards(w_in_, w_out_, l):
        return [w_in_[l:l + 1].astype(BF16), w_out_[l:l + 1].astype(BF16)]

    def mixer_shards(l):
        wl = w_in[l]
        return [jnp.stack([wl[:, :QKV_WIDTH], wl[:, f_hi:]]).astype(BF16),
                jnp.pad(wl[:, f_lo:f_hi], ((0, 0), (0, LANES - N_HEADS_FOX)))[None].astype(BF16),
                w_out[l:l + 1].astype(BF16),
                jnp.concatenate([w_br_sb[l], w_br_ch[l], w_br_fox[l]], axis=0)[None].astype(BF16)]

    gathers = []
    for l in range(DEPTH):
        gathers.append(exchange_start("gather", ffn_shards(w_ffn1_in, w_ffn1_out, l), f"gather_ffn1_l{l}"))
        gathers.append(exchange_start("gather", mixer_shards(l), f"gather_mix_l{l}"))
        gathers.append(exchange_start("gather", ffn_shards(w_ffn2_in, w_ffn2_out, l), f"gather_ffn2_l{l}"))
    gather_tokens = [g["token"] for g in gathers]

    def ffn_weights(handle, after, name):
        wa_, wb_ = exchange_wait(handle, after, name)
        return wa_, wb_.reshape(1, 4, FF_BLK, D_MODEL)

    def mixer_weights(handle, after, name):
        wc_, wf_, wout_, wbr_ = exchange_wait(handle, after, name)
        return (wc_.reshape(2, D_MODEL, QKV_WIDTH), wf_.reshape(1, D_MODEL, LANES),
                wout_.reshape(1, D_MODEL, D_MODEL),
                wbr_.transpose(0, 2, 1, 3).reshape(1, D_MODEL, D_MODEL))

    bq = b_in[:, None, :QKV_WIDTH]
    bf = jnp.pad(b_in[:, f_lo:f_hi], ((0, 0), (0, LANES - N_HEADS_FOX)))[:, None, :]
    bg = b_in[:, None, f_hi:]
    tab_t = jnp.pad(rel_bias.transpose(0, 2, 1), ((0, 0), (0, 0), (0, REL_PAD - N_REL)))

    h = xs
    saved = []
    weights = []
    for l in range(DEPTH):
        bias = rel_bias_build(tab_t[l], f"rel_build_l{l}").reshape(N_HEADS_CH, QB, CH_KEYS)
        x0 = h
        w1 = ffn_weights(gathers[3 * l], x0, f"gathered_ffn1_l{l}")
        x1, s1 = _ffn_fwd(x0, g_ffn1[l:l + 1], *w1, 0, tm, f"ffn1_l{l}",
                          deps=gather_tokens if l == 0 else ())
        wc, wf, wout, wbr = mixer_weights(gathers[3 * l + 1], x1, f"gathered_mix_l{l}")
        x2, sm = _mixer_fwd(x1, g_mix[l:l + 1], wc, wf, wc, wbr, wout, bq[l], bf[l], bg[l], bias,
                            0, tm, f"mix_l{l}")
        w2 = ffn_weights(gathers[3 * l + 2], x2, f"gathered_ffn2_l{l}")
        x3, s2 = _ffn_fwd(x2, g_ffn2[l:l + 1], *w2, 0, tm, f"ffn2_l{l}")
        saved.append((x0, x1, x2, s1, sm, s2, bias))
        weights.append((w1, (wc, wf, wout, wbr), w2))
        h = x3

    dx, dg_final, loss_blk = loss_head(h, g_final[None, :], target, tm, "loss_head")

    g_mix_l = [None] * DEPTH
    dgains = {}
    scatters = {}

    def scatter_ffn(key):
        def on_grads(dwa, dwb):
            scatters[key] = exchange_start(
                "scatter", [dwa[None], dwb.reshape(1, N_DEV, D_FF // N_DEV, D_MODEL)],
                f"scatter_{key[0]}_l{key[1]}")
            return (scatters[key]["token"],)
        return on_grads

    def scatter_mixer(key):
        def on_grads(gm):
            scatters[key] = exchange_start(
                "scatter",
                [gm["dwqkv"].reshape(1, N_DEV, LANES, QKV_WIDTH), gm["dwgate"].reshape(1, N_DEV, LANES, QKV_WIDTH),
                 gm["dwf"].reshape(1, N_DEV, LANES, LANES), gm["dwout"].reshape(1, N_DEV, LANES, D_MODEL),
                 gm["dwbr"].reshape(1, D_MODEL, N_DEV, LANES).transpose(0, 2, 1, 3)],
                f"scatter_{key[0]}_l{key[1]}")
            return (scatters[key]["token"],)
        return on_grads

    for l in reversed(range(DEPTH)):
        x0, x1, x2, s1, sm, s2, bias = saved[l]
        w1, (wc, wf, wout, wbr), w2 = weights[l]
        dx, dgains[("ffn2", l)] = _ffn_bwd(dx, x2, g_ffn2[l:l + 1], s2, *w2, 0, tm, f"ffn2_l{l}",
                                           scatter_ffn(("ffn2", l)))
        dx, g_mix_l[l] = _mixer_bwd(dx, x1, g_mix[l:l + 1], sm, wc, wf, wc, wbr, wout, bias, 0, tm,
                                    f"mix_l{l}", scatter_mixer(("mix", l)))
        dx, dgains[("ffn1", l)] = _ffn_bwd(dx, x0, g_ffn1[l:l + 1], s1, *w1, 0, tm, f"ffn1_l{l}",
                                           scatter_ffn(("ffn1", l)))

    small_shapes = []
    small_pieces = []
    small_w, small_m, small_v = [], [], []

    def add_small(piece, w, m, v):
        small_shapes.append(w.shape)
        small_pieces.append(piece)
        small_w.append(w); small_m.append(m); small_v.append(v)

    dg1 = jnp.concatenate([dgains[("ffn1", l)] for l in range(DEPTH)], axis=0)
    dgm = jnp.concatenate([g_mix_l[l]["dgain"] for l in range(DEPTH)], axis=0)
    dg2 = jnp.concatenate([dgains[("ffn2", l)] for l in range(DEPTH)], axis=0)
    db = jnp.stack([jnp.concatenate([g_mix_l[l]["dbq"][0], g_mix_l[l]["dbf"][0, :N_HEADS_FOX],
                                     g_mix_l[l]["dbg"][0]]) for l in range(DEPTH)])
    drel = jnp.stack([g_mix_l[l]["dtab"][:, :N_REL].T for l in range(DEPTH)])
    add_small(dg1, g_ffn1, m_g_ffn1, v_g_ffn1)
    add_small(dgm, g_mix, m_g_mix, v_g_mix)
    add_small(db, b_in, m_b_in, v_b_in)
    add_small(drel, rel_bias, m_rel_bias, v_rel_bias)
    add_small(dg2, g_ffn2, m_g_ffn2, v_g_ffn2)
    add_small(dg_final[0], g_final, m_g_final, v_g_final)
    loss_piece = loss_blk[0, 0:1]
    small_sum = all_reduce_small(_pack_small(small_pieces + [loss_piece]), "allreduce_small")
    n_small = sum(int(np.prod(s)) for s in small_shapes)
    loss = small_sum.reshape(-1)[n_small]

    sm_spec = pl.BlockSpec((SMALL_ROWS, LANES), lambda i: (0, 0))
    sm_out = adamw([small_sum[None]], _pack_small(small_w), _pack_small(small_m), _pack_small(small_v),
                   (1,), [pl.BlockSpec((1, SMALL_ROWS, LANES), lambda i: (0, 0, 0))], sm_spec, "adamw_small")
    sm_g, sm_d, sm_m, sm_v = [_unpack_small(o, small_shapes) for o in sm_out]

    recv = {}
    for l in reversed(range(DEPTH)):
        for grp in ("ffn2", "mix", "ffn1"):
            recv[(grp, l)] = exchange_wait(scatters[(grp, l)], dx, f"scattered_{grp}_l{l}")

    def upd(parts, w, m, v, tr, name, rb0=0):
        _, r, c = w.shape
        nr = r // tr

        def p_spec(layer):
            pinned = (nr - 1) if layer == 0 else 0
            return pl.BlockSpec((N_DEV, None, tr, c),
                                lambda l, i: (0, 0, rb0 + jnp.where(l == layer, i, pinned), 0))

        return adamw(parts, w, m, v, (DEPTH, nr), [p_spec(0), p_spec(1)],
                     pl.BlockSpec((None, tr, c), lambda l, i: (l, i, 0)), name)

    def both(grp, k):
        return [recv[(grp, l)][k] for l in range(DEPTH)]

    out_rows = D_FF // N_DEV // 2
    r_ffn1_in = upd(both("ffn1", 0), w_ffn1_in, m_w_ffn1_in, v_w_ffn1_in, 256, "adamw_ffn1_in")
    r_ffn2_in = upd(both("ffn2", 0), w_ffn2_in, m_w_ffn2_in, v_w_ffn2_in, 256, "adamw_ffn2_in")
    r_ffn1_out = upd(both("ffn1", 1), w_ffn1_out, m_w_ffn1_out, v_w_ffn1_out, out_rows, "adamw_ffn1_out")
    r_ffn2_out = upd(both("ffn2", 1), w_ffn2_out, m_w_ffn2_out, v_w_ffn2_out, out_rows, "adamw_ffn2_out")
    r_out = upd(both("mix", 3), w_out, m_w_out, v_w_out, LANES, "adamw_w_out")
    r_br_sb = upd(both("mix", 4), w_br_sb, m_w_br_sb, v_w_br_sb, 256, "adamw_br_sb", rb0=0)
    r_br_ch = upd(both("mix", 4), w_br_ch, m_w_br_ch, v_w_br_ch, 256, "adamw_br_ch", rb0=1)
    r_br_fox = upd(both("mix", 4), w_br_fox, m_w_br_fox, v_w_br_fox, 256, "adamw_br_fox", rb0=3)

    def summed(parts, name):
        _, _, r, c = parts.shape
        return sum_parts(parts, (1,), pl.BlockSpec((N_DEV, None, r, c), lambda s: (0, 0, 0, 0)),
                         pl.BlockSpec((r, c), lambda s: (0, 0)), _sds((r, c), F32), name)

    g_w_in = jnp.stack([
        jnp.concatenate([summed(recv[("mix", l)][0], f"sum_wqkv_l{l}"),
                         summed(recv[("mix", l)][2], f"sum_wf_l{l}")[:, :N_HEADS_FOX],
                         summed(recv[("mix", l)][1], f"sum_wgate_l{l}")], axis=1) for l in range(DEPTH)])
    win_rows = 32
    win_spec = pl.BlockSpec((None, win_rows, w_in.shape[2]), lambda l, i: (l, i, 0))
    r_in = adamw([g_w_in[None]], w_in, m_w_in, v_w_in, (DEPTH, LANES // win_rows),
                 [pl.BlockSpec((1, None, win_rows, w_in.shape[2]), lambda l, i: (0, l, i, 0))], win_spec,
                 "adamw_w_in")

    def per_kind(k):
        small = (sm_g, sm_d, sm_m, sm_v)[k]
        return [small[0], r_ffn1_in[k], r_ffn1_out[k], small[1], r_in[k], small[2], small[3],
                r_br_sb[k], r_br_ch[k], r_br_fox[k], r_out[k], small[4], r_ffn2_in[k], r_ffn2_out[k],
                small[5]]

    return (loss, dx[None], *per_kind(0), *per_kind(1), *per_kind(2), *per_kind(3))
```

```python
import functools

import numpy as np
import jax
import jax.numpy as jnp
from jax import lax
from jax.experimental import pallas as pl
from jax.experimental.pallas import tpu as pltpu

F32 = jnp.float32
BF16 = jnp.bfloat16

N_DEV = 8
D_MODEL = 1024
DEPTH = 2
HEAD_DIM = 64
W_SB, W_CH, W_FOX = 256, 512, 256
QKV_WIDTH = 3 * (W_SB + W_CH + W_FOX)
N_HEADS_FOX = 4
N_HEADS_CH = 8
D_FF = 2816
FF_BLK = 2 * D_FF // N_DEV
CHUNK = 64
LEFT_CHUNKS = 8
MAX_REL = 128
N_REL = 2 * MAX_REL + 1
REL_PAD = 384
QB = 128
KB = 512
KSUB = KB // QB
CH_WIN = 5
CH_KEYS = CH_WIN * QB
RMS_EPS = 1e-6
NEG = -1e30
SCALE = HEAD_DIM ** -0.5
LANES = 128
VMEM_LIMIT = 56 * 1024 * 1024

ADAM_LR, ADAM_B1, ADAM_B2, ADAM_EPS, ADAM_WD, ADAM_STEP = 0.001, 0.9, 0.999, 1e-08, 0.01, 10

SMALL_ROWS = 192

MESH = pl.DeviceIdType.MESH
ANY = pl.BlockSpec(memory_space=pl.ANY)
HIGHEST = lax.Precision.HIGHEST

NN = (((1,), (0,)), ((), ()))
NT = (((1,), (1,)), ((), ()))
TN = (((0,), (0,)), ((), ()))


def _cparams(n_grid):
    return pltpu.CompilerParams(dimension_semantics=("arbitrary",) * n_grid,
                                vmem_limit_bytes=VMEM_LIMIT)


def _sds(shape, dtype):
    return jax.ShapeDtypeStruct(tuple(shape), dtype)


def _my_index():
    return 4 * lax.axis_index("x") + 2 * lax.axis_index("y") + lax.axis_index("c")


def _peer(mask):
    x, y, c = lax.axis_index("x"), lax.axis_index("y"), lax.axis_index("c")
    px = x ^ ((mask >> 2) & 1)
    py = y ^ ((mask >> 1) & 1)
    pc = c ^ (mask & 1)
    return (px, py, pc), 4 * px + 2 * py + pc


def all_gather(shard, name):
    s, r, c = shard.shape

    def body(in_ref, out_ref, send_sems, recv_sems, local_sem):
        me = _my_index()
        mine = pltpu.make_async_copy(in_ref, out_ref.at[:, me], local_sem)
        mine.start()
        sends = []
        for mask in range(1, N_DEV):
            peer, _ = _peer(mask)
            cp = pltpu.make_async_remote_copy(
                src_ref=in_ref, dst_ref=out_ref.at[:, me],
                send_sem=send_sems.at[mask - 1], recv_sem=recv_sems.at[mask - 1],
                device_id=peer, device_id_type=MESH)
            cp.start()
            sends.append(cp)
        for mask in range(1, N_DEV):
            peer, pidx = _peer(mask)
            pltpu.make_async_remote_copy(
                src_ref=in_ref, dst_ref=out_ref.at[:, pidx],
                send_sem=send_sems.at[mask - 1], recv_sem=recv_sems.at[mask - 1],
                device_id=peer, device_id_type=MESH).wait_recv()
        for cp in sends:
            cp.wait_send()
        mine.wait()

    return pl.pallas_call(
        body, name=name,
        out_shape=_sds((s, N_DEV, r, c), shard.dtype),
        in_specs=[ANY], out_specs=ANY,
        scratch_shapes=[pltpu.SemaphoreType.DMA((N_DEV - 1,)),
                        pltpu.SemaphoreType.DMA((N_DEV - 1,)),
                        pltpu.SemaphoreType.DMA],
    )(shard)


def all_to_all(parts, name):
    s, _, r, c = parts.shape

    def body(in_ref, out_ref, send_sems, recv_sems, local_sem):
        me = _my_index()
        mine = pltpu.make_async_copy(in_ref.at[:, me], out_ref.at[me], local_sem)
        mine.start()
        sends = []
        for mask in range(1, N_DEV):
            peer, pidx = _peer(mask)
            cp = pltpu.make_async_remote_copy(
                src_ref=in_ref.at[:, pidx], dst_ref=out_ref.at[me],
                send_sem=send_sems.at[mask - 1], recv_sem=recv_sems.at[mask - 1],
                device_id=peer, device_id_type=MESH)
            cp.start()
            sends.append(cp)
        for mask in range(1, N_DEV):
            peer, pidx = _peer(mask)
            pltpu.make_async_remote_copy(
                src_ref=in_ref.at[:, me], dst_ref=out_ref.at[pidx],
                send_sem=send_sems.at[mask - 1], recv_sem=recv_sems.at[mask - 1],
                device_id=peer, device_id_type=MESH).wait_recv()
        for cp in sends:
            cp.wait_send()
        mine.wait()

    return pl.pallas_call(
        body, name=name,
        out_shape=_sds((N_DEV, s, r, c), parts.dtype),
        in_specs=[ANY], out_specs=ANY,
        scratch_shapes=[pltpu.SemaphoreType.DMA((N_DEV - 1,)),
                        pltpu.SemaphoreType.DMA((N_DEV - 1,)),
                        pltpu.SemaphoreType.DMA],
    )(parts)


HBM_SPEC = pl.BlockSpec(memory_space=pltpu.HBM)
SEM_SPEC = pl.BlockSpec(memory_space=pltpu.SEMAPHORE)
EFFECT = pltpu.SideEffectType.DATAFLOW_SIDE_EFFECTING


def _exchange_refs(mode, in_ref, land_ref, me, pidx):
    if mode == "gather":
        return in_ref, land_ref.at[:, me], land_ref.at[:, pidx]
    return in_ref.at[:, pidx], land_ref.at[me], land_ref.at[pidx]


def _landing(mode, a, me):
    if mode == "gather":
        s, r, c = a.shape
        return lax.dynamic_update_slice(lax.empty((s, N_DEV, r, c), a.dtype), a[:, None], (0, me, 0, 0))
    s, _, r, c = a.shape
    own = lax.dynamic_index_in_dim(a, me, axis=1, keepdims=False)
    return lax.dynamic_update_slice(lax.empty((N_DEV, s, r, c), a.dtype), own[None], (me, 0, 0, 0))


def exchange_start(mode, arrays, name):
    n = len(arrays)
    me = _my_index()
    lands0 = [_landing(mode, a, me) for a in arrays]

    def body(*refs):
        in_refs, land_refs = refs[:n], refs[n:2 * n]
        send_sems, recv_sems, token = refs[2 * n], refs[2 * n + 1], refs[-1]
        mine = _my_index()
        for k in range(n):
            for mask in range(1, N_DEV):
                peer, pidx = _peer(mask)
                src, dst, _ = _exchange_refs(mode, in_refs[k], land_refs[k], mine, pidx)
                sem = k * (N_DEV - 1) + mask - 1
                pltpu.make_async_remote_copy(
                    src_ref=src, dst_ref=dst, send_sem=send_sems.at[sem], recv_sem=recv_sems.at[sem],
                    device_id=peer, device_id_type=MESH).start()
        token[...] = jnp.zeros_like(token)

    nsem = n * (N_DEV - 1)
    outs = pl.pallas_call(
        body, name=name,
        out_shape=(pltpu.SemaphoreType.DMA((nsem,)), pltpu.SemaphoreType.DMA((nsem,)),
                   *[pltpu.HBM(a.shape, a.dtype) for a in arrays],
                   *[pltpu.HBM(l.shape, l.dtype) for l in lands0], _sds((8, LANES), F32)),
        in_specs=[HBM_SPEC] * (2 * n),
        out_specs=(SEM_SPEC, SEM_SPEC, *[HBM_SPEC] * (2 * n), pl.BlockSpec(memory_space=pltpu.VMEM)),
        input_output_aliases={k: 2 + k for k in range(2 * n)},
        compiler_params=pltpu.CompilerParams(has_side_effects=EFFECT),
    )(*[pltpu.with_memory_space_constraint(a, pltpu.HBM) for a in arrays],
      *[pltpu.with_memory_space_constraint(l, pltpu.HBM) for l in lands0])
    return dict(mode=mode, n=n, send=outs[0], recv=outs[1], ins=outs[2:2 + n],
                lands=outs[2 + n:2 + 2 * n], token=outs[-1])


def exchange_wait(handle, after, name):
    n, mode = handle["n"], handle["mode"]

    def body(*refs):
        in_refs, land_refs = refs[:n], refs[n:2 * n]
        send_sems, recv_sems = refs[2 * n], refs[2 * n + 1]
        mine = _my_index()
        for k in range(n):
            for mask in range(1, N_DEV):
                peer, pidx = _peer(mask)
                src, _, here = _exchange_refs(mode, in_refs[k], land_refs[k], mine, pidx)
                sem = k * (N_DEV - 1) + mask - 1
                cp = pltpu.make_async_remote_copy(
                    src_ref=src, dst_ref=here, send_sem=send_sems.at[sem], recv_sem=recv_sems.at[sem],
                    device_id=peer, device_id_type=MESH)
                cp.wait_send()
                cp.wait_recv()

    thru = (*handle["ins"], *handle["lands"])
    outs = pl.pallas_call(
        body, name=name,
        out_shape=tuple(pltpu.HBM(a.shape, a.dtype) for a in thru),
        in_specs=[HBM_SPEC] * (2 * n) + [SEM_SPEC, SEM_SPEC, ANY],
        out_specs=tuple([HBM_SPEC] * (2 * n)),
        input_output_aliases={k: k for k in range(2 * n)},
        compiler_params=pltpu.CompilerParams(has_side_effects=EFFECT),
    )(*thru, handle["send"], handle["recv"], after)
    return list(outs[n:])


def all_reduce_small(packed, name, deps=()):
    rows = packed.shape[0]
    nd = len(deps)

    def body(in_ref, *rest):
        out_ref, slots, send_sems, recv_sems = rest[nd:]
        me = _my_index()
        sends = []
        for mask in range(1, N_DEV):
            peer, _ = _peer(mask)
            cp = pltpu.make_async_remote_copy(
                src_ref=in_ref, dst_ref=slots.at[me],
                send_sem=send_sems.at[mask - 1], recv_sem=recv_sems.at[mask - 1],
                device_id=peer, device_id_type=MESH)
            cp.start()
            sends.append(cp)
        slots[me] = in_ref[...]
        for mask in range(1, N_DEV):
            peer, pidx = _peer(mask)
            pltpu.make_async_remote_copy(
                src_ref=in_ref, dst_ref=slots.at[pidx],
                send_sem=send_sems.at[mask - 1], recv_sem=recv_sems.at[mask - 1],
                device_id=peer, device_id_type=MESH).wait_recv()
        for cp in sends:
            cp.wait_send()
        total = slots[0]
        for p in range(1, N_DEV):
            total = total + slots[p]
        out_ref[...] = total

    return pl.pallas_call(
        body, name=name,
        out_shape=_sds((rows, LANES), F32),
        in_specs=[pl.BlockSpec(memory_space=pltpu.VMEM)] + [ANY] * nd,
        out_specs=pl.BlockSpec(memory_space=pltpu.VMEM),
        scratch_shapes=[pltpu.VMEM((N_DEV, rows, LANES), F32),
                        pltpu.SemaphoreType.DMA((N_DEV - 1,)),
                        pltpu.SemaphoreType.DMA((N_DEV - 1,))],
    )(packed, *deps)


def matmul(dims, a, b, out_sds, grid, a_spec, b_spec, o_spec, acc_shape, *, name, alpha=1.0,
           bias=None, bias_spec=None, res=None, res_spec=None, colsum_sds=None, colsum_spec=None,
           deps=()):
    nk = grid[2]
    has_bias, has_res, has_cs = bias is not None, res is not None, colsum_sds is not None
    if has_cs:
        assert grid[0] == 1 and dims == TN

    def body(*refs):
        a_ref, b_ref = refs[0], refs[1]
        pos = 2
        bias_ref = res_ref = cs_ref = None
        if has_bias:
            bias_ref = refs[pos]; pos += 1
        if has_res:
            res_ref = refs[pos]; pos += 1
        pos += len(deps)
        o_ref = refs[pos]; pos += 1
        if has_cs:
            cs_ref = refs[pos]; pos += 1
        k = pl.program_id(2)
        bval = b_ref[...]
        part = lax.dot_general(a_ref[...].astype(BF16), bval.astype(BF16), dims,
                               preferred_element_type=F32)

        def finish(total):
            r = total * alpha if alpha != 1.0 else total
            if has_bias:
                r = r + bias_ref[...]
            if has_res:
                r = r + res_ref[...].astype(F32)
            o_ref[...] = r.astype(o_ref.dtype)

        if has_cs:
            csum = jnp.sum(bval.astype(F32), axis=0, keepdims=True)

            @pl.when(k == 0)
            def _():
                cs_ref[...] = csum

            @pl.when(k > 0)
            def _():
                cs_ref[...] += csum

        if nk == 1:
            finish(part)
        else:
            acc_ref = refs[pos]

            @pl.when(k == 0)
            def _():
                acc_ref[...] = part

            @pl.when(k > 0)
            def _():
                acc_ref[...] += part

            @pl.when(k == nk - 1)
            def _():
                finish(acc_ref[...])

    in_specs, args = [a_spec, b_spec], [a, b]
    if has_bias:
        in_specs.append(bias_spec); args.append(bias)
    if has_res:
        in_specs.append(res_spec); args.append(res)
    in_specs += [ANY] * len(deps)
    args += list(deps)
    out_shape, out_specs = [out_sds], [o_spec]
    if has_cs:
        out_shape.append(colsum_sds); out_specs.append(colsum_spec)
    scratch = [] if nk == 1 else [pltpu.VMEM(acc_shape, F32)]
    outs = pl.pallas_call(
        body, name=name, grid=grid, in_specs=in_specs, out_specs=out_specs, out_shape=out_shape,
        scratch_shapes=scratch, compiler_params=_cparams(3))(*args)
    return outs if has_cs else outs[0]


def _sigmoid(z):
    return 1.0 / (1.0 + jnp.exp(-z))


def _log_sigmoid(z):
    return jnp.minimum(z, 0.0) - jnp.log(1.0 + jnp.exp(-jnp.abs(z)))


def rmsnorm_fwd(x, gain, tm, name, deps=()):
    t, d = x.shape

    def body(x_ref, g_ref, *rest):
        o_ref = rest[-1]
        xf = x_ref[...]
        r = lax.rsqrt(jnp.mean(xf * xf, axis=-1, keepdims=True) + RMS_EPS)
        o_ref[...] = (xf * r * g_ref[...]).astype(o_ref.dtype)

    return pl.pallas_call(
        body, name=name, grid=(t // tm,),
        in_specs=[pl.BlockSpec((tm, d), lambda i: (i, 0)), pl.BlockSpec((1, d), lambda i: (0, 0))]
        + [ANY] * len(deps),
        out_specs=pl.BlockSpec((tm, d), lambda i: (i, 0)),
        out_shape=_sds((t, d), BF16), compiler_params=_cparams(1))(x, gain, *deps)


def rmsnorm_bwd(x, gain, dh, dres, tm, name):
    t, d = x.shape

    def body(x_ref, g_ref, dh_ref, dres_ref, dx_ref, dg_ref):
        i = pl.program_id(0)
        xf = x_ref[...]
        r = lax.rsqrt(jnp.mean(xf * xf, axis=-1, keepdims=True) + RMS_EPS)
        xhat = xf * r
        dh_v = dh_ref[...]
        dxhat = dh_v * g_ref[...]
        dx = r * (dxhat - xhat * jnp.mean(dxhat * xhat, axis=-1, keepdims=True))
        dx_ref[...] = dres_ref[...] + dx
        dg = jnp.sum(dh_v * xhat, axis=0, keepdims=True)

        @pl.when(i == 0)
        def _():
            dg_ref[...] = dg

        @pl.when(i > 0)
        def _():
            dg_ref[...] += dg

    row = pl.BlockSpec((tm, d), lambda i: (i, 0))
    vec = pl.BlockSpec((1, d), lambda i: (0, 0))
    return pl.pallas_call(
        body, name=name, grid=(t // tm,), in_specs=[row, vec, row, row], out_specs=[row, vec],
        out_shape=[_sds((t, d), F32), _sds((1, d), F32)], compiler_params=_cparams(1))(x, gain, dh, dres)


def loss_head(x, gain, target, tm, name):
    t, d = x.shape

    def body(x_ref, g_ref, tgt_ref, dx_ref, dg_ref, loss_ref):
        i = pl.program_id(0)
        xf = x_ref[...]
        g = g_ref[...]
        r = lax.rsqrt(jnp.mean(xf * xf, axis=-1, keepdims=True) + RMS_EPS)
        xhat = xf * r
        err = xhat * g - tgt_ref[...]
        part = 0.5 * jnp.sum(jnp.mean(err * err, axis=-1, keepdims=True))
        dy = err * (1.0 / d)
        dxhat = dy * g
        dx_ref[...] = r * (dxhat - xhat * jnp.mean(dxhat * xhat, axis=-1, keepdims=True))
        dg = jnp.sum(dy * xhat, axis=0, keepdims=True)
        lpart = jnp.full((8, LANES), part, F32)

        @pl.when(i == 0)
        def _():
            dg_ref[...] = dg
            loss_ref[...] = lpart

        @pl.when(i > 0)
        def _():
            dg_ref[...] += dg
            loss_ref[...] += lpart

    row = pl.BlockSpec((tm, d), lambda i: (i, 0))
    vec = pl.BlockSpec((1, d), lambda i: (0, 0))
    return pl.pallas_call(
        body, name=name, grid=(t // tm,), in_specs=[row, vec, row],
        out_specs=[row, vec, pl.BlockSpec((8, LANES), lambda i: (0, 0))],
        out_shape=[_sds((t, d), F32), _sds((1, d), F32), _sds((8, LANES), F32)],
        compiler_params=_cparams(1))(x, gain, target)


def ffn_in_swiglu(hn, wa, s, tm, name):
    t = hn.shape[0]

    def body(h_ref, wg_ref, wu_ref, gu_ref, act_ref):
        h = h_ref[...]
        g = jnp.dot(h, wg_ref[...], preferred_element_type=F32)
        u = jnp.dot(h, wu_ref[...], preferred_element_type=F32)
        gu_ref[0] = g.astype(gu_ref.dtype)
        gu_ref[1] = u.astype(gu_ref.dtype)
        act_ref[...] = (g * _sigmoid(g) * u).astype(act_ref.dtype)

    return pl.pallas_call(
        body, name=name, grid=(t // tm, 4),
        in_specs=[pl.BlockSpec((tm, D_MODEL), lambda i, j: (i, 0)),
                  pl.BlockSpec((None, None, D_MODEL, FF_BLK), lambda i, j: (s, j, 0, 0)),
                  pl.BlockSpec((None, None, D_MODEL, FF_BLK), lambda i, j: (s, j + 4, 0, 0))],
        out_specs=[pl.BlockSpec((None, 2, tm, FF_BLK), lambda i, j: (j, 0, i, 0)),
                   pl.BlockSpec((None, tm, FF_BLK), lambda i, j: (j, i, 0))],
        out_shape=[_sds((4, 2, t, FF_BLK), BF16), _sds((4, t, FF_BLK), BF16)],
        compiler_params=_cparams(2))(hn, wa, wa)


def ffn_dact_swiglu(dy, wb, gu, s, tm, name):
    t = dy.shape[0]

    def body(dy_ref, w_ref, gu_ref, o_ref):
        da = 0.5 * lax.dot_general(dy_ref[...].astype(BF16), w_ref[...], NT, preferred_element_type=F32)
        g = gu_ref[0].astype(F32)
        u = gu_ref[1].astype(F32)
        sg = _sigmoid(g)
        o_ref[0] = (da * u * (sg * (1.0 + g * (1.0 - sg)))).astype(o_ref.dtype)
        o_ref[1] = (da * g * sg).astype(o_ref.dtype)

    blk = pl.BlockSpec((None, 2, tm, FF_BLK), lambda i, j: (j, 0, i, 0))
    return pl.pallas_call(
        body, name=name, grid=(t // tm, 4),
        in_specs=[pl.BlockSpec((tm, D_MODEL), lambda i, j: (i, 0)),
                  pl.BlockSpec((None, None, FF_BLK, D_MODEL), lambda i, j: (s, j, 0, 0)), blk],
        out_specs=blk, out_shape=_sds((4, 2, t, FF_BLK), BF16),
        compiler_params=_cparams(2))(dy, wb, gu)


def ffn_out_residual(act, wb, x, s, tm, name):
    t = x.shape[0]

    def body(a_ref, w_ref, x_ref, o_ref):
        acc = jnp.dot(a_ref[0], w_ref[0], preferred_element_type=F32)
        for k in range(1, 4):
            acc = acc + jnp.dot(a_ref[k], w_ref[k], preferred_element_type=F32)
        o_ref[...] = x_ref[...] + 0.5 * acc

    row = pl.BlockSpec((tm, D_MODEL), lambda i: (i, 0))
    return pl.pallas_call(
        body, name=name, grid=(t // tm,),
        in_specs=[pl.BlockSpec((4, tm, FF_BLK), lambda i: (0, i, 0)),
                  pl.BlockSpec((None, 4, FF_BLK, D_MODEL), lambda i: (s, 0, 0, 0)), row],
        out_specs=row, out_shape=_sds((t, D_MODEL), F32), compiler_params=_cparams(1))(act, wb, x)


def ffn_dh(dgu, wa, s, tm, name, deps):
    t = dgu.shape[2]

    def body(g_ref, w_ref, *rest):
        o_ref = rest[-1]
        acc = lax.dot_general(g_ref[0, 0], w_ref[0], NT, preferred_element_type=F32)
        for p in range(1, N_DEV):
            acc = acc + lax.dot_general(g_ref[p % 4, p // 4], w_ref[p], NT, preferred_element_type=F32)
        o_ref[...] = acc

    return pl.pallas_call(
        body, name=name, grid=(t // tm,),
        in_specs=[pl.BlockSpec((4, 2, tm, FF_BLK), lambda i: (0, 0, i, 0)),
                  pl.BlockSpec((None, N_DEV, D_MODEL, FF_BLK), lambda i: (s, 0, 0, 0))] + [ANY] * len(deps),
        out_specs=pl.BlockSpec((tm, D_MODEL), lambda i: (i, 0)),
        out_shape=_sds((t, D_MODEL), F32), compiler_params=_cparams(1))(dgu, wa, *deps)


def merge_fwd(gates, ya, yb, yc, tm, name):
    t, d = ya.shape

    def body(ga_ref, gb_ref, gc_ref, ya_ref, yb_ref, yc_ref, o_ref):
        m = (_sigmoid(ga_ref[...]) * ya_ref[...] + _sigmoid(gb_ref[...]) * yb_ref[...]
             + _sigmoid(gc_ref[...]) * yc_ref[...])
        o_ref[...] = m.astype(o_ref.dtype)

    row = pl.BlockSpec((tm, d), lambda i: (i, 0))
    gspecs = [pl.BlockSpec((tm, d), functools.partial(lambda i, a: (i, a), a=a)) for a in range(3)]
    return pl.pallas_call(
        body, name=name, grid=(t // tm,), in_specs=gspecs + [row, row, row], out_specs=row,
        out_shape=_sds((t, d), BF16), compiler_params=_cparams(1))(gates, gates, gates, ya, yb, yc)


def merge_bwd(dm, gates, ya, yb, yc, tm, name):
    t, d = ya.shape

    def body(dm_ref, g_ref, y_ref, dg_ref, dy_ref):
        dmv = dm_ref[...]
        s = _sigmoid(g_ref[...])
        dy_ref[...] = (dmv * s).astype(dy_ref.dtype)
        dg_ref[...] = (dmv * y_ref[...] * s * (1.0 - s)).astype(dg_ref.dtype)

    outs = []
    dgs = []
    for a, y in enumerate((ya, yb, yc)):
        row = pl.BlockSpec((tm, d), lambda i: (i, 0))
        gspec = pl.BlockSpec((tm, d), functools.partial(lambda i, a: (i, a), a=a))
        dg, dy = pl.pallas_call(
            functools.partial(body), name=f"{name}_{a}", grid=(t // tm,),
            in_specs=[row, gspec, row], out_specs=[row, row],
            out_shape=[_sds((t, d), BF16), _sds((t, d), BF16)],
            compiler_params=_cparams(1))(dm, gates, y)
        dgs.append(dg)
        outs.append(dy)
    return dgs, outs


def _iota2(shape, dim):
    return lax.broadcasted_iota(jnp.int32, shape, dim)


def forget_cumsum(f, name):
    t = f.shape[0]
    nq = t // QB

    def body(f_ref, fcol_ref, frow_ref, carry):
        j = pl.program_id(0)

        @pl.when(j == 0)
        def _():
            carry[...] = jnp.zeros_like(carry)

        logf = _log_sigmoid(f_ref[...])
        tri = (_iota2((QB, QB), 1) <= _iota2((QB, QB), 0)).astype(F32)
        blk = jnp.dot(tri, logf, precision=HIGHEST, preferred_element_type=F32) + carry[...]
        carry[...] += jnp.sum(logf, axis=0, keepdims=True)
        fcol_ref[...] = blk
        frow_ref[...] = blk.T[0:8, :]

    return pl.pallas_call(
        body, name=name, grid=(nq,),
        in_specs=[pl.BlockSpec((QB, LANES), lambda j: (j, 0))],
        out_specs=[pl.BlockSpec((QB, LANES), lambda j: (j, 0)),
                   pl.BlockSpec((None, 8, QB), lambda j: (j, 0, 0))],
        out_shape=[_sds((t, LANES), F32), _sds((nq, 8, QB), F32)],
        scratch_shapes=[pltpu.VMEM((1, LANES), F32)], compiler_params=_cparams(1))(f)


def forget_cumsum_bwd(dfrow, f, name):
    t = f.shape[0]
    nq = t // QB

    def body(dfr_ref, f_ref, df_ref, carry):
        jj = pl.program_id(0)

        @pl.when(jj == 0)
        def _():
            carry[...] = jnp.zeros_like(carry)

        padded = jnp.concatenate([dfr_ref[...], jnp.zeros((QB - 8, QB), F32)], axis=0)
        dfcol = padded.T
        tri = (_iota2((QB, QB), 1) >= _iota2((QB, QB), 0)).astype(F32)
        dlogf = jnp.dot(tri, dfcol, precision=HIGHEST, preferred_element_type=F32) + carry[...]
        carry[...] += jnp.sum(dfcol, axis=0, keepdims=True)
        df_ref[...] = dlogf * _sigmoid(-f_ref[...])

    return pl.pallas_call(
        body, name=name, grid=(nq,),
        in_specs=[pl.BlockSpec((None, 8, QB), lambda jj: (nq - 1 - jj, 0, 0)),
                  pl.BlockSpec((QB, LANES), lambda jj: (nq - 1 - jj, 0))],
        out_specs=pl.BlockSpec((QB, LANES), lambda jj: (nq - 1 - jj, 0)),
        out_shape=_sds((t, LANES), F32),
        scratch_shapes=[pltpu.VMEM((1, LANES), F32)], compiler_params=_cparams(1))(dfrow, f)


REL_DIAG = 768
REL_SHIFT = REL_DIAG - (QB - 1)


def _diag_onehot():
    u = _iota2((REL_PAD, REL_DIAG), 1)
    rel = jnp.clip(CH_KEYS - 1 - u, -MAX_REL, MAX_REL) + MAX_REL
    return (_iota2((REL_PAD, REL_DIAG), 0) == rel).astype(F32)


def rel_bias_build(tab_t, name):
    def body(tab_ref, o_ref):
        diag = jnp.dot(tab_ref[...], _diag_onehot(), precision=HIGHEST, preferred_element_type=F32)
        for h in range(N_HEADS_CH):
            rows = jnp.broadcast_to(diag[h:h + 1, :], (QB, REL_DIAG))
            o_ref[h] = pltpu.roll(rows, REL_SHIFT, 1, stride=1, stride_axis=0)[:, :CH_KEYS]

    return pl.pallas_call(
        body, name=name, out_shape=_sds((N_HEADS_CH, QB, CH_KEYS), F32),
        in_specs=[pl.BlockSpec(memory_space=pltpu.VMEM)], out_specs=pl.BlockSpec(memory_space=pltpu.VMEM),
    )(tab_t)


def rel_bias_scatter(dbias, name):
    def body(db_ref, o_ref, ddiag):
        flip = (_iota2((QB, QB), 0) + _iota2((QB, QB), 1) == QB - 1).astype(F32)
        for h in range(N_HEADS_CH):
            padded = jnp.concatenate([db_ref[h], jnp.zeros((QB, REL_DIAG - CH_KEYS), F32)], axis=1)
            flipped = jnp.dot(flip, padded, precision=HIGHEST, preferred_element_type=F32)
            unrolled = pltpu.roll(flipped, 0, 1, stride=1, stride_axis=0)
            ddiag[h:h + 1, :] = jnp.sum(unrolled, axis=0, keepdims=True)
        o_ref[...] = lax.dot_general(ddiag[...], _diag_onehot(), NT, precision=HIGHEST,
                                     preferred_element_type=F32)

    return pl.pallas_call(
        body, name=name, out_shape=_sds((N_HEADS_CH, REL_PAD), F32),
        in_specs=[pl.BlockSpec(memory_space=pltpu.VMEM)], out_specs=pl.BlockSpec(memory_space=pltpu.VMEM),
        scratch_shapes=[pltpu.VMEM((N_HEADS_CH, REL_DIAG), F32)],
    )(dbias)


def _hl(h):
    return slice(h * HEAD_DIM, (h + 1) * HEAD_DIM)


def _split_dot(x, tri_bf16):
    hi = x.astype(BF16)
    lo = (x - hi.astype(F32)).astype(BF16)
    return (jnp.dot(hi, tri_bf16, preferred_element_type=F32)
            + jnp.dot(lo, tri_bf16, preferred_element_type=F32))


def _rows(j):
    return pl.ds(pl.multiple_of(j * QB, QB), QB)


def _krows(g):
    return pl.ds(pl.multiple_of(g * KB, KB), KB)


def _log_sigmoid_pair(z):
    sp = jnp.log(1.0 + jnp.exp(-jnp.abs(z)))
    return jnp.minimum(z, 0.0) - sp, -jnp.maximum(z, 0.0) - sp


def _qkv_specs(t, col0, n_pairs):
    q_spec = pl.BlockSpec((QB, LANES), lambda hp, i: (i, col0 + hp))
    k_spec = pl.BlockSpec((t, LANES), lambda hp, i: (0, col0 + n_pairs + hp))
    v_spec = pl.BlockSpec((t, LANES), lambda hp, i: (0, col0 + 2 * n_pairs + hp))
    return q_spec, k_spec, v_spec


def sb_fwd(qkv, name):
    t = qkv.shape[0]
    nq = t // QB

    def body(q_ref, k_ref, v_ref, o_ref):
        i = pl.program_id(1)
        groups = i // KSUB + 1
        tri_after = (_iota2((KB, KB), 0) > _iota2((KB, KB), 1)).astype(BF16)
        t_idx = i * QB + _iota2((QB, KB), 0)
        qs = [q_ref[:, _hl(h)] for h in range(2)]

        def step(gg, carry):
            g = groups - 1 - gg
            strict = (g * KB + _iota2((QB, KB), 1)) < t_idx
            out = []
            for h in range(2):
                tail, acc = carry[2 * h], carry[2 * h + 1]
                k = k_ref[_krows(g), _hl(h)]
                v = v_ref[_krows(g), _hl(h)]
                z = lax.dot_general(qs[h], k, NT, preferred_element_type=F32) * SCALE
                lb, lf = _log_sigmoid_pair(z)
                lf = jnp.where(strict, lf, 0.0)
                between = _split_dot(lf, tri_after) + tail
                w = jnp.where(strict, jnp.exp(lb + between), 0.0)
                acc = acc + jnp.dot(w.astype(BF16), v, preferred_element_type=F32)
                out += [tail + jnp.sum(lf, axis=1, keepdims=True), acc]
            return tuple(out)

        init = (jnp.zeros((QB, 1), F32), jnp.zeros((QB, HEAD_DIM), F32)) * 2
        res = lax.fori_loop(0, groups, step, init)
        for h in range(2):
            o_ref[:, _hl(h)] = res[2 * h + 1].astype(o_ref.dtype)

    q_spec, k_spec, v_spec = _qkv_specs(t, 0, 2)
    return pl.pallas_call(
        body, name=name, grid=(2, nq), in_specs=[q_spec, k_spec, v_spec],
        out_specs=pl.BlockSpec((QB, LANES), lambda hp, i: (i, hp)),
        out_shape=_sds((t, W_SB), BF16), compiler_params=_cparams(2))(qkv, qkv, qkv)


def sb_bwd(qkv, do, name):
    t = qkv.shape[0]
    nq = t // QB

    def body(q_ref, k_ref, v_ref, do_ref, dq_ref, dk_ref, dv_ref, w_scr):
        i = pl.program_id(1)

        @pl.when(i == 0)
        def _():
            dk_ref[...] = jnp.zeros_like(dk_ref)
            dv_ref[...] = jnp.zeros_like(dv_ref)

        groups = i // KSUB + 1
        tri_after = (_iota2((KB, KB), 0) > _iota2((KB, KB), 1)).astype(BF16)
        tri_before = (_iota2((KB, KB), 0) < _iota2((KB, KB), 1)).astype(BF16)
        t_idx = i * QB + _iota2((QB, KB), 0)
        qs = [q_ref[:, _hl(h)] for h in range(2)]
        dos = [do_ref[:, _hl(h)] for h in range(2)]

        def weights(gg, tails):
            g = groups - 1 - gg
            strict = (g * KB + _iota2((QB, KB), 1)) < t_idx
            out = []
            for h in range(2):
                k = k_ref[_krows(g), _hl(h)]
                z = lax.dot_general(qs[h], k, NT, preferred_element_type=F32) * SCALE
                lb, lf = _log_sigmoid_pair(z)
                lf = jnp.where(strict, lf, 0.0)
                between = _split_dot(lf, tri_after) + tails[h]
                w_scr[h, g] = jnp.where(strict, jnp.exp(lb + between), 0.0)
                out.append(tails[h] + jnp.sum(lf, axis=1, keepdims=True))
            return tuple(out)

        lax.fori_loop(0, groups, weights, (jnp.zeros((QB, 1), F32),) * 2)

        def grads(g, carry):
            strict = (g * KB + _iota2((QB, KB), 1)) < t_idx
            out = []
            for h in range(2):
                head, dq = carry[2 * h], carry[2 * h + 1]
                k = k_ref[_krows(g), _hl(h)]
                v = v_ref[_krows(g), _hl(h)]
                w = w_scr[h, g]
                z = lax.dot_general(qs[h], k, NT, preferred_element_type=F32) * SCALE
                beta = _sigmoid(z)
                e = lax.dot_general(dos[h], v, NT, preferred_element_type=F32) * w
                before = _split_dot(e, tri_before) + head
                dz = jnp.where(strict, e * (1.0 - beta) - before * beta, 0.0) * SCALE
                dzb = dz.astype(BF16)
                dq = dq + jnp.dot(dzb, k, preferred_element_type=F32)
                dk_ref[_krows(g), _hl(h)] += lax.dot_general(dzb, qs[h], TN, preferred_element_type=F32)
                dv_ref[_krows(g), _hl(h)] += lax.dot_general(w.astype(BF16), dos[h], TN,
                                                             preferred_element_type=F32)
                out += [head + jnp.sum(e, axis=1, keepdims=True), dq]
            return tuple(out)

        init = (jnp.zeros((QB, 1), F32), jnp.zeros((QB, HEAD_DIM), F32)) * 2
        res = lax.fori_loop(0, groups, grads, init)
        for h in range(2):
            dq_ref[:, _hl(h)] = res[2 * h + 1].astype(dq_ref.dtype)

    q_spec, k_spec, v_spec = _qkv_specs(t, 0, 2)
    blk = pl.BlockSpec((QB, LANES), lambda hp, i: (i, hp))
    full = pl.BlockSpec((t, LANES), lambda hp, i: (0, hp))
    return pl.pallas_call(
        body, name=name, grid=(2, nq), in_specs=[q_spec, k_spec, v_spec, blk],
        out_specs=[blk, full, full],
        out_shape=[_sds((t, W_SB), BF16), _sds((t, W_SB), F32), _sds((t, W_SB), F32)],
        scratch_shapes=[pltpu.VMEM((2, t // KB, QB, KB), F32)],
        compiler_params=_cparams(2))(qkv, qkv, qkv, do)


def fox_fwd(qkv, fcol, frow, name):
    t = qkv.shape[0]
    nq = t // QB

    def body(q_ref, k_ref, v_ref, fc_ref, fr_ref, o_ref, lse_ref):
        hp = pl.program_id(0)
        i = pl.program_id(1)
        groups = i // KSUB + 1
        t_idx = i * QB + _iota2((QB, KB), 0)
        lane = _iota2((QB, LANES), 1)
        sub = _iota2((8, KB), 0)
        qs = [q_ref[:, _hl(h)] for h in range(2)]
        f_qs = [jnp.sum(jnp.where(lane == hp * 2 + h, fc_ref[...], 0.0), axis=1, keepdims=True)
                for h in range(2)]

        def step(g, carry):
            causal = (g * KB + _iota2((QB, KB), 1)) <= t_idx
            fr = fr_ref[g]
            out = []
            for h in range(2):
                m, l, acc = carry[3 * h:3 * h + 3]
                k = k_ref[_krows(g), _hl(h)]
                v = v_ref[_krows(g), _hl(h)]
                f_k = jnp.sum(jnp.where(sub == hp * 2 + h, fr, 0.0), axis=0, keepdims=True)
                z = lax.dot_general(qs[h], k, NT, preferred_element_type=F32) * SCALE + f_qs[h] - f_k
                z = jnp.where(causal, z, NEG)
                m_new = jnp.maximum(m, jnp.max(z, axis=1, keepdims=True))
                p = jnp.exp(z - m_new)
                corr = jnp.exp(m - m_new)
                l = l * corr + jnp.sum(p, axis=1, keepdims=True)
                acc = acc * corr + jnp.dot(p.astype(BF16), v, preferred_element_type=F32)
                out += [m_new, l, acc]
            return tuple(out)

        init = (jnp.full((QB, 1), NEG, F32), jnp.zeros((QB, 1), F32), jnp.zeros((QB, HEAD_DIM), F32)) * 2
        res = lax.fori_loop(0, groups, step, init)
        for h in range(2):
            m, l, acc = res[3 * h:3 * h + 3]
            o_ref[:, _hl(h)] = (acc / l).astype(o_ref.dtype)
            lse_ref[:, _hl(h)] = jnp.broadcast_to(m + jnp.log(l), (QB, HEAD_DIM))

    q_spec, k_spec, v_spec = _qkv_specs(t, 18, 2)
    blk = pl.BlockSpec((QB, LANES), lambda hp, i: (i, hp))
    return pl.pallas_call(
        body, name=name, grid=(2, nq),
        in_specs=[q_spec, k_spec, v_spec, pl.BlockSpec((QB, LANES), lambda hp, i: (i, 0)),
                  pl.BlockSpec((t // KB, 8, KB), lambda hp, i: (0, 0, 0))],
        out_specs=[blk, blk],
        out_shape=[_sds((t, W_FOX), BF16), _sds((t, W_FOX), F32)],
        compiler_params=_cparams(2))(qkv, qkv, qkv, fcol, frow)


def fox_bwd(qkv, fcol, frow, o, lse, do, name):
    t = qkv.shape[0]
    nq = t // QB

    def body(q_ref, k_ref, v_ref, fc_ref, fr_ref, o_ref, lse_ref, do_ref,
             dq_ref, dk_ref, dv_ref, dfr_ref):
        hp = pl.program_id(0)
        i = pl.program_id(1)

        @pl.when(i == 0)
        def _():
            dk_ref[...] = jnp.zeros_like(dk_ref)
            dv_ref[...] = jnp.zeros_like(dv_ref)

        @pl.when((i == 0) & (hp == 0))
        def _():
            dfr_ref[...] = jnp.zeros_like(dfr_ref)

        groups = i // KSUB + 1
        t_idx = i * QB + _iota2((QB, KB), 0)
        lane = _iota2((QB, LANES), 1)
        sub = _iota2((8, KB), 0)
        qs = [q_ref[:, _hl(h)] for h in range(2)]
        dos = [do_ref[:, _hl(h)] for h in range(2)]
        f_qs = [jnp.sum(jnp.where(lane == hp * 2 + h, fc_ref[...], 0.0), axis=1, keepdims=True)
                for h in range(2)]
        lse_qs = [lse_ref[:, h * HEAD_DIM:h * HEAD_DIM + 1] for h in range(2)]
        deltas = [jnp.sum(dos[h].astype(F32) * o_ref[:, _hl(h)].astype(F32), axis=1, keepdims=True)
                  for h in range(2)]

        def step(g, dqs):
            causal = (g * KB + _iota2((QB, KB), 1)) <= t_idx
            fr = fr_ref[g]
            out = []
            dfr = jnp.zeros((8, KB), F32)
            for h in range(2):
                k = k_ref[_krows(g), _hl(h)]
                v = v_ref[_krows(g), _hl(h)]
                f_k = jnp.sum(jnp.where(sub == hp * 2 + h, fr, 0.0), axis=0, keepdims=True)
                z = lax.dot_general(qs[h], k, NT, preferred_element_type=F32) * SCALE + f_qs[h] - f_k
                p = jnp.where(causal, jnp.exp(z - lse_qs[h]), 0.0)
                dp = lax.dot_general(dos[h], v, NT, preferred_element_type=F32)
                ds = p * (dp - deltas[h])
                dsb = (ds * SCALE).astype(BF16)
                out.append(dqs[h] + jnp.dot(dsb, k, preferred_element_type=F32))
                dk_ref[_krows(g), _hl(h)] += lax.dot_general(dsb, qs[h], TN, preferred_element_type=F32)
                dv_ref[_krows(g), _hl(h)] += lax.dot_general(p.astype(BF16), dos[h], TN,
                                                             preferred_element_type=F32)
                colsum = jnp.sum(ds, axis=0, keepdims=True)
                dfr = dfr + jnp.where(sub == hp * 2 + h, -colsum, 0.0)
            dfr_ref[g] += dfr
            return tuple(out)

        res = lax.fori_loop(0, groups, step, (jnp.zeros((QB, HEAD_DIM), F32),) * 2)
        for h in range(2):
            dq_ref[:, _hl(h)] = res[h].astype(dq_ref.dtype)

    q_spec, k_spec, v_spec = _qkv_specs(t, 18, 2)
    blk = pl.BlockSpec((QB, LANES), lambda hp, i: (i, hp))
    full = pl.BlockSpec((t, LANES), lambda hp, i: (0, hp))
    frs = pl.BlockSpec((t // KB, 8, KB), lambda hp, i: (0, 0, 0))
    return pl.pallas_call(
        body, name=name, grid=(2, nq),
        in_specs=[q_spec, k_spec, v_spec, pl.BlockSpec((QB, LANES), lambda hp, i: (i, 0)), frs,
                  blk, blk, blk],
        out_specs=[blk, full, full, frs],
        out_shape=[_sds((t, W_FOX), BF16), _sds((t, W_FOX), F32), _sds((t, W_FOX), F32),
                   _sds((t // KB, 8, KB), F32)],
        compiler_params=_cparams(2))(qkv, qkv, qkv, fcol, frow, o, lse, do)


def _frow_to_groups(frow):
    n = frow.shape[0] // KSUB
    return frow.reshape(n, KSUB, 8, QB).transpose(0, 2, 1, 3).reshape(n, 8, KB)


def _frow_from_groups(frow):
    n = frow.shape[0]
    return frow.reshape(n, 8, KSUB, QB).transpose(0, 2, 1, 3).reshape(n * KSUB, 8, QB)


def _chunk_valid(i):
    qi = _iota2((QB, CH_KEYS), 0)
    kj = _iota2((QB, CH_KEYS), 1)
    dchunk = (qi >> 6) + LEFT_CHUNKS - (kj >> 6)
    return (dchunk >= 0) & (dchunk <= LEFT_CHUNKS) & ((i - (CH_WIN - 1)) * QB + kj >= 0)


CH_PAD = (CH_WIN - 1) * QB


def _window(i):
    return pl.ds(pl.multiple_of(i * QB, QB), CH_KEYS)


def _chunk_probs(q, kw, bias, valid):
    z = lax.dot_general(q, kw, NT, preferred_element_type=F32) * SCALE + bias
    z = jnp.where(valid, z, NEG)
    z = z - jnp.max(z, axis=1, keepdims=True)
    p = jnp.exp(z)
    return p / jnp.sum(p, axis=1, keepdims=True)


def _chunk_specs(t):
    q_spec = pl.BlockSpec((QB, LANES), lambda hp, i: (i, 6 + hp))
    kv_spec = pl.BlockSpec((t + CH_PAD, LANES), lambda hp, i: (0, hp))
    return q_spec, kv_spec


def chunk_fwd(qkv, kp, vp, bias, name):
    t = qkv.shape[0]
    nq = t // QB

    def body(q_ref, k_ref, v_ref, b_ref, o_ref):
        i = pl.program_id(1)
        valid = _chunk_valid(i)
        for h in range(2):
            p = _chunk_probs(q_ref[:, _hl(h)], k_ref[_window(i), _hl(h)], b_ref[h], valid)
            o_ref[:, _hl(h)] = jnp.dot(p.astype(BF16), v_ref[_window(i), _hl(h)],
                                       preferred_element_type=F32).astype(o_ref.dtype)

    q_spec, kv_spec = _chunk_specs(t)
    return pl.pallas_call(
        body, name=name, grid=(4, nq),
        in_specs=[q_spec, kv_spec, kv_spec, pl.BlockSpec((2, QB, CH_KEYS), lambda hp, i: (hp, 0, 0))],
        out_specs=pl.BlockSpec((QB, LANES), lambda hp, i: (i, hp)),
        out_shape=_sds((t, W_CH), BF16), compiler_params=_cparams(2))(qkv, kp, vp, bias)


def chunk_bwd(qkv, kp, vp, bias, do, name):
    t = qkv.shape[0]
    nq = t // QB

    def body(q_ref, k_ref, v_ref, b_ref, do_ref, dq_ref, dk_ref, dv_ref, db_ref):
        i = pl.program_id(1)

        @pl.when(i == 0)
        def _():
            dk_ref[...] = jnp.zeros_like(dk_ref)
            dv_ref[...] = jnp.zeros_like(dv_ref)
            db_ref[...] = jnp.zeros_like(db_ref)

        valid = _chunk_valid(i)
        for h in range(2):
            q = q_ref[:, _hl(h)]
            dov = do_ref[:, _hl(h)]
            kw = k_ref[_window(i), _hl(h)]
            p = _chunk_probs(q, kw, b_ref[h], valid)
            dp = lax.dot_general(dov, v_ref[_window(i), _hl(h)], NT, preferred_element_type=F32)
            ds = p * (dp - jnp.sum(p * dp, axis=1, keepdims=True))
            db_ref[h] += ds
            dsb = (ds * SCALE).astype(BF16)
            dq_ref[:, _hl(h)] = jnp.dot(dsb, kw, preferred_element_type=F32).astype(dq_ref.dtype)
            dk_ref[_window(i), _hl(h)] += lax.dot_general(dsb, q, TN, preferred_element_type=F32)
            dv_ref[_window(i), _hl(h)] += lax.dot_general(p.astype(BF16), dov, TN,
                                                          preferred_element_type=F32)

    q_spec, kv_spec = _chunk_specs(t)
    blk = pl.BlockSpec((QB, LANES), lambda hp, i: (i, hp))
    bspec = pl.BlockSpec((2, QB, CH_KEYS), lambda hp, i: (hp, 0, 0))
    return pl.pallas_call(
        body, name=name, grid=(4, nq), in_specs=[q_spec, kv_spec, kv_spec, bspec, blk],
        out_specs=[blk, kv_spec, kv_spec, bspec],
        out_shape=[_sds((t, W_CH), BF16), _sds((t + CH_PAD, W_CH), F32), _sds((t + CH_PAD, W_CH), F32),
                   _sds((N_HEADS_CH, QB, CH_KEYS), F32)],
        compiler_params=_cparams(2))(qkv, kp, vp, bias, do)


def _sum_parts(p_ref):
    total = p_ref[0].astype(F32)
    for p in range(1, p_ref.shape[0]):
        total = total + p_ref[p].astype(F32)
    return total


def sum_parts(parts, grid, p_spec, o_spec, out_sds, name):
    def body(p_ref, o_ref):
        o_ref[...] = _sum_parts(p_ref)

    return pl.pallas_call(body, name=name, grid=grid, in_specs=[p_spec], out_specs=o_spec,
                          out_shape=out_sds, compiler_params=_cparams(len(grid)))(parts)


def adamw(parts, w, m, v, grid, p_specs, w_spec, name):
    c1 = 1.0 / (1.0 - ADAM_B1 ** ADAM_STEP)
    c2 = 1.0 / (1.0 - ADAM_B2 ** ADAM_STEP)
    n = len(parts)

    def body(*refs):
        w_ref, m_ref, v_ref, g_out, d_out, m_out, v_out = refs[n:]
        g = _sum_parts(refs[0])
        for q in range(1, n):
            g = jnp.where(pl.program_id(0) == q, _sum_parts(refs[q]), g)
        m_new = ADAM_B1 * m_ref[...] + (1.0 - ADAM_B1) * g
        v_new = ADAM_B2 * v_ref[...] + (1.0 - ADAM_B2) * (g * g)
        m_hat = m_new * c1
        v_hat = v_new * c2
        g_out[...] = g
        d_out[...] = -ADAM_LR * (m_hat / (jnp.sqrt(v_hat) + ADAM_EPS) + ADAM_WD * w_ref[...])
        m_out[...] = m_new
        v_out[...] = v_new

    out = _sds(w.shape, F32)
    return pl.pallas_call(
        body, name=name, grid=grid, in_specs=[*p_specs, w_spec, w_spec, w_spec],
        out_specs=[w_spec] * 4, out_shape=[out] * 4,
        compiler_params=_cparams(len(grid)))(*parts, w, m, v)


def _ffn_fwd(x, gain, wa, wb, s, tm, tag, deps=()):
    t = x.shape[0]
    hn = rmsnorm_fwd(x, gain, tm, f"rms_{tag}", deps)
    gu, act = ffn_in_swiglu(hn, wa, s, min(2 * tm, t), f"ffn_in_{tag}")
    y = ffn_out_residual(act, wb, x, s, min(2 * tm, t), f"ffn_out_{tag}")
    return y, (hn, gu, act)


def _ffn_bwd(dy, x, gain, saved, wa, wb, s, tm, tag, on_grads):
    t = x.shape[0]
    hn, gu, act = saved
    dgu = ffn_dact_swiglu(dy, wb, gu, s, min(2 * tm, t), f"ffn_dact_{tag}")
    dwb = matmul(TN, act, dy, _sds((4, FF_BLK, D_MODEL), BF16), (4, 1, 1),
                 pl.BlockSpec((None, t, FF_BLK), lambda i, j, k: (i, 0, 0)),
                 pl.BlockSpec((t, D_MODEL), lambda i, j, k: (0, 0)),
                 pl.BlockSpec((None, FF_BLK, D_MODEL), lambda i, j, k: (i, 0, 0)),
                 None, name=f"ffn_dwout_{tag}", alpha=0.5)
    dwa = matmul(TN, hn, dgu, _sds((8, D_MODEL, FF_BLK), BF16), (1, 8, 1),
                 pl.BlockSpec((t, D_MODEL), lambda i, j, k: (0, 0)),
                 pl.BlockSpec((None, None, t, FF_BLK), lambda i, j, k: (j % 4, j // 4, 0, 0)),
                 pl.BlockSpec((None, D_MODEL, FF_BLK), lambda i, j, k: (j, 0, 0)),
                 None, name=f"ffn_dwin_{tag}")
    deps = on_grads(dwa, dwb)
    dhn = ffn_dh(dgu, wa, s, tm, f"ffn_dh_{tag}", deps)
    dx, dgain = rmsnorm_bwd(x, gain, dhn, dy, tm, f"rms_bwd_{tag}")
    return dx, dgain


BR_ROWS = ((0, 1), (1, 2), (3, 1))


def _mixer_fwd(x, gain, wqkv, wf, wgate, wbr, wout, bq, bf, bg, bias, layer, tm, tag):
    t = x.shape[0]
    nt = t // tm
    hm = rmsnorm_fwd(x, gain, tm, f"rms_{tag}")
    a_full = pl.BlockSpec((tm, D_MODEL), lambda i, j, k: (i, 0))
    wide_out = pl.BlockSpec((tm, D_MODEL), lambda i, j, k: (i, j))
    wide_b = pl.BlockSpec((1, D_MODEL), lambda i, j, k: (0, j))
    qkv = matmul(NN, hm, wqkv, _sds((t, QKV_WIDTH), BF16), (nt, 3, 1), a_full,
                 pl.BlockSpec((None, D_MODEL, D_MODEL), lambda i, j, k: (layer, 0, j)), wide_out, None,
                 name=f"proj_qkv_{tag}", bias=bq, bias_spec=wide_b)
    gates = matmul(NN, hm, wgate, _sds((t, 3 * D_MODEL), F32), (nt, 3, 1), a_full,
                   pl.BlockSpec((None, D_MODEL, D_MODEL), lambda i, j, k: (layer + 1, 0,j)), wide_out,
                   None, name=f"proj_gate_{tag}", bias=bg, bias_spec=wide_b)
    f = matmul(NN, hm, wf, _sds((t, LANES), F32), (nt, 1, 1), a_full,
               pl.BlockSpec((None, D_MODEL, LANES), lambda i, j, k: (layer, 0, 0)),
               pl.BlockSpec((tm, LANES), lambda i, j, k: (i, 0)), None,
               name=f"proj_f_{tag}", bias=bf, bias_spec=pl.BlockSpec((1, LANES), lambda i, j, k: (0, 0)))
    fcol, frow = forget_cumsum(f, f"fcum_{tag}")
    frow = _frow_to_groups(frow)
    o_sb = sb_fwd(qkv, f"sb_fwd_{tag}")
    kp = jnp.pad(qkv[:, 10 * LANES:14 * LANES], ((CH_PAD, 0), (0, 0)))
    vp = jnp.pad(qkv[:, 14 * LANES:18 * LANES], ((CH_PAD, 0), (0, 0)))
    o_ch = chunk_fwd(qkv, kp, vp, bias, f"chunk_fwd_{tag}")
    o_fox, lse = fox_fwd(qkv, fcol, frow, f"fox_fwd_{tag}")
    ys = []
    for a, (o, (r0, nr)) in enumerate(zip((o_sb, o_ch, o_fox), BR_ROWS)):
        ys.append(matmul(
            NN, o, wbr, _sds((t, D_MODEL), F32), (nt, 1, nr),
            pl.BlockSpec((tm, 256), lambda i, j, k: (i, k)),
            pl.BlockSpec((None, 256, D_MODEL), functools.partial(lambda i, j, k, r0: (layer, r0 + k, 0), r0=r0)),
            a_full, (tm, D_MODEL), name=f"branch{a}_{tag}"))
    merged = merge_fwd(gates, ys[0], ys[1], ys[2], tm, f"merge_{tag}")
    x_new = matmul(NN, merged, wout, _sds((t, D_MODEL), F32), (nt, 1, 1), a_full,
                   pl.BlockSpec((None, D_MODEL, D_MODEL), lambda i, j, k: (layer, 0, 0)), a_full, None,
                   name=f"wout_{tag}", res=x, res_spec=a_full)
    saved = (hm, qkv, gates, f, fcol, frow, o_sb, o_ch, o_fox, lse, ys, merged, kp, vp)
    return x_new, saved


def _mixer_bwd(dy, x, gain, saved, wqkv, wf, wgate, wbr, wout, bias, layer, tm, tag, on_grads):
    t = x.shape[0]
    nt = t // tm
    hm, qkv, gates, f, fcol, frow, o_sb, o_ch, o_fox, lse, ys, merged, kp, vp = saved
    a_full = pl.BlockSpec((tm, D_MODEL), lambda i, j, k: (i, 0))
    red_row = pl.BlockSpec((tm, D_MODEL), lambda i, j, k: (k, 0))
    sq = pl.BlockSpec((D_MODEL, D_MODEL), lambda i, j, k: (0, 0))
    dmerged = matmul(NT, dy, wout, _sds((t, D_MODEL), F32), (nt, 1, 1), a_full,
                     pl.BlockSpec((None, D_MODEL, D_MODEL), lambda i, j, k: (layer, 0, 0)), a_full, None,
                     name=f"dmerged_{tag}")
    all_t = pl.BlockSpec((t, D_MODEL), lambda i, j, k: (0, 0))
    dwout = matmul(TN, merged, dy, _sds((D_MODEL, D_MODEL), BF16), (1, 1, 1), all_t, all_t, sq,
                   None, name=f"dwout_{tag}")
    dgs, dys = merge_bwd(dmerged, gates, ys[0], ys[1], ys[2], tm, f"merge_bwd_{tag}")
    dos, dwbrs = [], []
    for a, (o, (r0, nr)) in enumerate(zip((o_sb, o_ch, o_fox), BR_ROWS)):
        dos.append(matmul(
            NT, dys[a], wbr, _sds((t, nr * 256), BF16), (nt, nr, 1), a_full,
            pl.BlockSpec((None, 256, D_MODEL), functools.partial(lambda i, j, k, r0: (layer, r0 + j, 0), r0=r0)),
            pl.BlockSpec((tm, 256), lambda i, j, k: (i, j)), None, name=f"dbranch{a}_{tag}"))
        dwbrs.append(matmul(
            TN, o, dys[a], _sds((nr * 256, D_MODEL), BF16), (nr, 1, 1),
            pl.BlockSpec((t, 256), lambda i, j, k: (0, i)), all_t,
            pl.BlockSpec((256, D_MODEL), lambda i, j, k: (i, 0)), None, name=f"dwbr{a}_{tag}"))
    dq_a, dk_a, dv_a = sb_bwd(qkv, dos[0], f"sb_bwd_{tag}")
    dq_b, dk_b, dv_b, dbias = chunk_bwd(qkv, kp, vp, bias, dos[1], f"chunk_bwd_{tag}")
    dk_b, dv_b = dk_b[CH_PAD:], dv_b[CH_PAD:]
    dq_c, dk_c, dv_c, dfrow = fox_bwd(qkv, fcol, frow, o_fox, lse, dos[2], f"fox_bwd_{tag}")
    df = forget_cumsum_bwd(_frow_from_groups(dfrow), f, f"fcum_bwd_{tag}")
    dqkv = jnp.concatenate([p.astype(BF16) for p in
                            (dq_a, dk_a, dv_a, dq_b, dk_b, dv_b, dq_c, dk_c, dv_c)], axis=1)
    dgates = jnp.concatenate(dgs, axis=1)
    dtab = rel_bias_scatter(dbias, f"rel_scatter_{tag}")

    all_rows = pl.BlockSpec((t, D_MODEL), lambda i, j, k: (0, 0))
    wide_b = pl.BlockSpec((t, D_MODEL), lambda i, j, k: (0, j))
    wide_o = pl.BlockSpec((D_MODEL, D_MODEL), lambda i, j, k: (0, j))
    wide_cs = pl.BlockSpec((1, D_MODEL), lambda i, j, k: (0, j))
    dwqkv, dbq = matmul(TN, hm, dqkv, _sds((D_MODEL, QKV_WIDTH), BF16), (1, 3, 1), all_rows, wide_b,
                        wide_o, None, name=f"dwqkv_{tag}",
                        colsum_sds=_sds((1, QKV_WIDTH), F32), colsum_spec=wide_cs)
    dwgate, dbg = matmul(TN, hm, dgates, _sds((D_MODEL, 3 * D_MODEL), BF16), (1, 3, 1), all_rows,
                         wide_b, wide_o, None, name=f"dwgate_{tag}",
                         colsum_sds=_sds((1, 3 * D_MODEL), F32), colsum_spec=wide_cs)
    dwf, dbf = matmul(TN, hm, df, _sds((D_MODEL, LANES), BF16), (1, 1, 1), all_rows,
                      pl.BlockSpec((t, LANES), lambda i, j, k: (0, 0)),
                      pl.BlockSpec((D_MODEL, LANES), lambda i, j, k: (0, 0)), None,
                      name=f"dwf_{tag}", colsum_sds=_sds((1, LANES), F32),
                      colsum_spec=pl.BlockSpec((1, LANES), lambda i, j, k: (0, 0)))
    dwbr = jnp.concatenate(dwbrs, axis=0)
    deps = on_grads(dict(dwqkv=dwqkv, dwgate=dwgate, dwf=dwf, dwbr=dwbr, dwout=dwout))
    wide_a = pl.BlockSpec((tm, QKV_WIDTH), lambda i, j, k: (i, 0))
    dhm = matmul(NT, dqkv, wqkv, _sds((t, D_MODEL), F32), (nt, 1, 1), wide_a,
                 pl.BlockSpec((None, D_MODEL, QKV_WIDTH), lambda i, j, k: (layer, 0, 0)), a_full,
                 None, name=f"dhm_qkv_{tag}", deps=deps)
    dhm = matmul(NT, dgates, wgate, _sds((t, D_MODEL), F32), (nt, 1, 1), wide_a,
                 pl.BlockSpec((None, D_MODEL, QKV_WIDTH), lambda i, j, k: (layer + 1, 0, 0)), a_full,
                 None, name=f"dhm_gate_{tag}", res=dhm, res_spec=a_full)
    dhm = matmul(NT, df, wf, _sds((t, D_MODEL), F32), (nt, 1, 1),
                 pl.BlockSpec((tm, LANES), lambda i, j, k: (i, 0)),
                 pl.BlockSpec((None, D_MODEL, LANES), lambda i, j, k: (layer, 0, 0)), a_full, None,
                 name=f"dhm_f_{tag}", res=dhm, res_spec=a_full)
    dx, dgain = rmsnorm_bwd(x, gain, dhm, dy, tm, f"rms_bwd_{tag}")
    return dx, dict(dbq=dbq, dbg=dbg, dbf=dbf, dtab=dtab, dgain=dgain)


def _pack_small(pieces):
    flat = jnp.concatenate([p.reshape(-1).astype(F32) for p in pieces])
    flat = jnp.pad(flat, (0, SMALL_ROWS * LANES - flat.shape[0]))
    return flat.reshape(SMALL_ROWS, LANES)


def _unpack_small(packed, shapes):
    flat = packed.reshape(-1)
    out, pos = [], 0
    for shp in shapes:
        n = int(np.prod(shp))
        out.append(flat[pos:pos + n].reshape(shp))
        pos += n
    return out


def kernel(x, g_ffn1, w_ffn1_in, w_ffn1_out, g_mix, w_in, b_in, rel_bias, w_br_sb, w_br_ch, w_br_fox, w_out, g_ffn2, w_ffn2_in, w_ffn2_out, g_final, loss_target, m_g_ffn1, m_w_ffn1_in, m_w_ffn1_out, m_g_mix, m_w_in, m_b_in, m_rel_bias, m_w_br_sb, m_w_br_ch, m_w_br_fox, m_w_out, m_g_ffn2, m_w_ffn2_in, m_w_ffn2_out, m_g_final, v_g_ffn1, v_w_ffn1_in, v_w_ffn1_out, v_g_mix, v_w_in, v_b_in, v_rel_bias, v_w_br_sb, v_w_br_ch, v_w_br_fox, v_w_out, v_g_ffn2, v_w_ffn2_in, v_w_ffn2_out, v_g_final):
    t = x.shape[1]
    tm = min(512, t)
    xs = x[0]
    target = loss_target[0]
    f_lo, f_hi = QKV_WIDTH, QKV_WIDTH + N_HEADS_FOX

    def ffn_shards(w_in_, w_out_, l):
        return [w_in_[l:l + 1].astype(BF16), w_out_[l:l + 1].astype(BF16)]

    def mixer_shards(l):
        wl = w_in[l]
        return [jnp.stack([wl[:, :QKV_WIDTH], wl[:, f_hi:]]).astype(BF16),
                jnp.pad(wl[:, f_lo:f_hi], ((0, 0), (0, LANES - N_HEADS_FOX)))[None].astype(BF16),
                w_out[l:l + 1].astype(BF16),
                jnp.concatenate([w_br_sb[l], w_br_ch[l], w_br_fox[l]], axis=0)[None].astype(BF16)]

    gathers = []
    for l in range(DEPTH):
        gathers.append(exchange_start("gather", ffn_shards(w_ffn1_in, w_ffn1_out, l), f"gather_ffn1_l{l}"))
        gathers.append(exchange_start("gather", mixer_shards(l), f"gather_mix_l{l}"))
        gathers.append(exchange_start("gather", ffn_shards(w_ffn2_in, w_ffn2_out, l), f"gather_ffn2_l{l}"))
    gather_tokens = [g["token"] for g in gathers]

    def ffn_weights(handle, after, name):
        wa_, wb_ = exchange_wait(handle, after, name)
        return wa_, wb_.reshape(1, 4, FF_BLK, D_MODEL)

    def mixer_weights(handle, after, name):
        wc_, wf_, wout_, wbr_ = exchange_wait(handle, after, name)
        return (wc_.reshape(2, D_MODEL, QKV_WIDTH), wf_.reshape(1, D_MODEL, LANES),
                wout_.reshape(1, D_MODEL, D_MODEL),
                wbr_.transpose(0, 2, 1, 3).reshape(1, D_MODEL, D_MODEL))

    bq = b_in[:, None, :QKV_WIDTH]
    bf = jnp.pad(b_in[:, f_lo:f_hi], ((0, 0), (0, LANES - N_HEADS_FOX)))[:, None, :]
    bg = b_in[:, None, f_hi:]
    tab_t = jnp.pad(rel_bias.transpose(0, 2, 1), ((0, 0), (0, 0), (0, REL_PAD - N_REL)))

    h = xs
    saved = []
    weights = []
    for l in range(DEPTH):
        bias = rel_bias_build(tab_t[l], f"rel_build_l{l}").reshape(N_HEADS_CH, QB, CH_KEYS)
        x0 = h
        w1 = ffn_weights(gathers[3 * l], x0, f"gathered_ffn1_l{l}")
        x1, s1 = _ffn_fwd(x0, g_ffn1[l:l + 1], *w1, 0, tm, f"ffn1_l{l}",
                          deps=gather_tokens if l == 0 else ())
        wc, wf, wout, wbr = mixer_weights(gathers[3 * l + 1], x1, f"gathered_mix_l{l}")
        x2, sm = _mixer_fwd(x1, g_mix[l:l + 1], wc, wf, wc, wbr, wout, bq[l], bf[l], bg[l], bias,
                            0, tm, f"mix_l{l}")
        w2 = ffn_weights(gathers[3 * l + 2], x2, f"gathered_ffn2_l{l}")
        x3, s2 = _ffn_fwd(x2, g_ffn2[l:l + 1], *w2, 0, tm, f"ffn2_l{l}")
        saved.append((x0, x1, x2, s1, sm, s2, bias))
        weights.append((w1, (wc, wf, wout, wbr), w2))
        h = x3

    dx, dg_final, loss_blk = loss_head(h, g_final[None, :], target, tm, "loss_head")

    g_mix_l = [None] * DEPTH
    dgains = {}
    scatters = {}

    def scatter_ffn(key):
        def on_grads(dwa, dwb):
            scatters[key] = exchange_start(
                "scatter", [dwa[None], dwb.reshape(1, N_DEV, D_FF // N_DEV, D_MODEL)],
                f"scatter_{key[0]}_l{key[1]}")
            return (scatters[key]["token"],)
        return on_grads

    def scatter_mixer(key):
        def on_grads(gm):
            scatters[key] = exchange_start(
                "scatter",
                [gm["dwqkv"].reshape(1, N_DEV, LANES, QKV_WIDTH), gm["dwgate"].reshape(1, N_DEV, LANES, QKV_WIDTH),
                 gm["dwf"].reshape(1, N_DEV, LANES, LANES), gm["dwout"].reshape(1, N_DEV, LANES, D_MODEL),
                 gm["dwbr"].reshape(1, D_MODEL, N_DEV, LANES).transpose(0, 2, 1, 3)],
                f"scatter_{key[0]}_l{key[1]}")
            return (scatters[key]["token"],)
        return on_grads

    for l in reversed(range(DEPTH)):
        x0, x1, x2, s1, sm, s2, bias = saved[l]
        w1, (wc, wf, wout, wbr), w2 = weights[l]
        dx, dgains[("ffn2", l)] = _ffn_bwd(dx, x2, g_ffn2[l:l + 1], s2, *w2, 0, tm, f"ffn2_l{l}",
                                           scatter_ffn(("ffn2", l)))
        dx, g_mix_l[l] = _mixer_bwd(dx, x1, g_mix[l:l + 1], sm, wc, wf, wc, wbr, wout, bias, 0, tm,
                                    f"mix_l{l}", scatter_mixer(("mix", l)))
        dx, dgains[("ffn1", l)] = _ffn_bwd(dx, x0, g_ffn1[l:l + 1], s1, *w1, 0, tm, f"ffn1_l{l}",
                                           scatter_ffn(("ffn1", l)))

    small_shapes = []
    small_pieces = []
    small_w, small_m, small_v = [], [], []

    def add_small(piece, w, m, v):
        small_shapes.append(w.shape)
        small_pieces.append(piece)
        small_w.append(w); small_m.append(m); small_v.append(v)

    dg1 = jnp.concatenate([dgains[("ffn1", l)] for l in range(DEPTH)], axis=0)
    dgm = jnp.concatenate([g_mix_l[l]["dgain"] for l in range(DEPTH)], axis=0)
    dg2 = jnp.concatenate([dgains[("ffn2", l)] for l in range(DEPTH)], axis=0)
    db = jnp.stack([jnp.concatenate([g_mix_l[l]["dbq"][0], g_mix_l[l]["dbf"][0, :N_HEADS_FOX],
                                     g_mix_l[l]["dbg"][0]]) for l in range(DEPTH)])
    drel = jnp.stack([g_mix_l[l]["dtab"][:, :N_REL].T for l in range(DEPTH)])
    add_small(dg1, g_ffn1, m_g_ffn1, v_g_ffn1)
    add_small(dgm, g_mix, m_g_mix, v_g_mix)
    add_small(db, b_in, m_b_in, v_b_in)
    add_small(drel, rel_bias, m_rel_bias, v_rel_bias)
    add_small(dg2, g_ffn2, m_g_ffn2, v_g_ffn2)
    add_small(dg_final[0], g_final, m_g_final, v_g_final)
    loss_piece = loss_blk[0, 0:1]
    small_packed = _pack_small(small_pieces + [loss_piece])

    recv = {}
    last = ("ffn1", 0)
    for l in reversed(range(DEPTH)):
        for grp in ("ffn2", "mix", "ffn1"):
            if (grp, l) != last:
                recv[(grp, l)] = exchange_wait(scatters[(grp, l)], dx, f"scattered_{grp}_l{l}")

    def upd(parts, w, m, v, tr, name, rb0=0):
        _, r, c = w.shape
        nr = r // tr

        def p_spec(layer):
            pinned = (nr - 1) if layer == 0 else 0
            return pl.BlockSpec((N_DEV, None, tr, c),
                                lambda l, i: (0, 0, rb0 + jnp.where(l == layer, i, pinned), 0))

        return adamw(parts, w, m, v, (DEPTH, nr), [p_spec(0), p_spec(1)],
                     pl.BlockSpec((None, tr, c), lambda l, i: (l, i, 0)), name)

    def both(grp, k):
        return [recv[(grp, l)][k] for l in range(DEPTH)]

    out_rows = D_FF // N_DEV // 2
    r_ffn2_in = upd(both("ffn2", 0), w_ffn2_in, m_w_ffn2_in, v_w_ffn2_in, 256, "adamw_ffn2_in")
    r_ffn2_out = upd(both("ffn2", 1), w_ffn2_out, m_w_ffn2_out, v_w_ffn2_out, out_rows, "adamw_ffn2_out")
    r_out = upd(both("mix", 3), w_out, m_w_out, v_w_out, LANES, "adamw_w_out")
    r_br_sb = upd(both("mix", 4), w_br_sb, m_w_br_sb, v_w_br_sb, 256, "adamw_br_sb", rb0=0)
    r_br_ch = upd(both("mix", 4), w_br_ch, m_w_br_ch, v_w_br_ch, 256, "adamw_br_ch", rb0=1)
    r_br_fox = upd(both("mix", 4), w_br_fox, m_w_br_fox, v_w_br_fox, 256, "adamw_br_fox", rb0=3)

    def summed(parts, name):
        _, _, r, c = parts.shape
        return sum_parts(parts, (1,), pl.BlockSpec((N_DEV, None, r, c), lambda s: (0, 0, 0, 0)),
                         pl.BlockSpec((r, c), lambda s: (0, 0)), _sds((r, c), F32), name)

    g_w_in = jnp.stack([
        jnp.concatenate([summed(recv[("mix", l)][0], f"sum_wqkv_l{l}"),
                         summed(recv[("mix", l)][2], f"sum_wf_l{l}")[:, :N_HEADS_FOX],
                         summed(recv[("mix", l)][1], f"sum_wgate_l{l}")], axis=1) for l in range(DEPTH)])
    win_rows = 32
    win_spec = pl.BlockSpec((None, win_rows, w_in.shape[2]), lambda l, i: (l, i, 0))
    r_in = adamw([g_w_in[None]], w_in, m_w_in, v_w_in, (DEPTH, LANES // win_rows),
                 [pl.BlockSpec((1, None, win_rows, w_in.shape[2]), lambda l, i: (0, l, i, 0))], win_spec,
                 "adamw_w_in")

    recv[last] = exchange_wait(scatters[last], r_in[1], "scattered_ffn1_l0")
    r_ffn1_in = upd(both("ffn1", 0), w_ffn1_in, m_w_ffn1_in, v_w_ffn1_in, 256, "adamw_ffn1_in")
    r_ffn1_out = upd(both("ffn1", 1), w_ffn1_out, m_w_ffn1_out, v_w_ffn1_out, out_rows, "adamw_ffn1_out")

    small_sum = all_reduce_small(small_packed, "allreduce_small", deps=(r_ffn1_out[1],))
    n_small = sum(int(np.prod(s)) for s in small_shapes)
    loss = small_sum.reshape(-1)[n_small]
    sm_spec = pl.BlockSpec((SMALL_ROWS, LANES), lambda i: (0, 0))
    sm_out = adamw([small_sum[None]], _pack_small(small_w), _pack_small(small_m), _pack_small(small_v),
                   (1,), [pl.BlockSpec((1, SMALL_ROWS, LANES), lambda i: (0, 0, 0))], sm_spec, "adamw_small")
    sm_g, sm_d, sm_m, sm_v = [_unpack_small(o, small_shapes) for o in sm_out]

    def per_kind(k):
        small = (sm_g, sm_d, sm_m, sm_v)[k]
        return [small[0], r_ffn1_in[k], r_ffn1_out[k], small[1], r_in[k], small[2], small[3],
                r_br_sb[k], r_br_ch[k], r_br_fox[k], r_out[k], small[4], r_ffn2_in[k], r_ffn2_out[k],
                small[5]]

    return (loss, dx[None], *per_kind(0), *per_kind(1), *per_kind(2), *per_kind(3))
```

```python
import functools

import numpy as np
import jax
import jax.numpy as jnp
from jax import lax
from jax.experimental import pallas as pl
from jax.experimental.pallas import tpu as pltpu

F32 = jnp.float32
BF16 = jnp.bfloat16

N_DEV = 8
D_MODEL = 1024
DEPTH = 2
HEAD_DIM = 64
W_SB, W_CH, W_FOX = 256, 512, 256
QKV_WIDTH = 3 * (W_SB + W_CH + W_FOX)
N_HEADS_FOX = 4
N_HEADS_CH = 8
D_FF = 2816
FF_BLK = 2 * D_FF // N_DEV
CHUNK = 64
LEFT_CHUNKS = 8
MAX_REL = 128
N_REL = 2 * MAX_REL + 1
REL_PAD = 384
QB = 128
KB = 512
KSUB = KB // QB
CH_WIN = 5
CH_KEYS = CH_WIN * QB
RMS_EPS = 1e-6
NEG = -1e30
SCALE = HEAD_DIM ** -0.5
LANES = 128
VMEM_LIMIT = 56 * 1024 * 1024

ADAM_LR, ADAM_B1, ADAM_B2, ADAM_EPS, ADAM_WD, ADAM_STEP = 0.001, 0.9, 0.999, 1e-08, 0.01, 10

SMALL_ROWS = 192

MESH = pl.DeviceIdType.MESH
ANY = pl.BlockSpec(memory_space=pl.ANY)
HIGHEST = lax.Precision.HIGHEST

NN = (((1,), (0,)), ((), ()))
NT = (((1,), (1,)), ((), ()))
TN = (((0,), (0,)), ((), ()))


def _cparams(n_grid):
    return pltpu.CompilerParams(dimension_semantics=("arbitrary",) * n_grid,
                                vmem_limit_bytes=VMEM_LIMIT)


def _sds(shape, dtype):
    return jax.ShapeDtypeStruct(tuple(shape), dtype)


def _my_index():
    return 4 * lax.axis_index("x") + 2 * lax.axis_index("y") + lax.axis_index("c")


def _peer(mask):
    x, y, c = lax.axis_index("x"), lax.axis_index("y"), lax.axis_index("c")
    px = x ^ ((mask >> 2) & 1)
    py = y ^ ((mask >> 1) & 1)
    pc = c ^ (mask & 1)
    return (px, py, pc), 4 * px + 2 * py + pc


def all_gather(shard, name):
    s, r, c = shard.shape

    def body(in_ref, out_ref, send_sems, recv_sems, local_sem):
        me = _my_index()
        mine = pltpu.make_async_copy(in_ref, out_ref.at[:, me], local_sem)
        mine.start()
        sends = []
        for mask in range(1, N_DEV):
            peer, _ = _peer(mask)
            cp = pltpu.make_async_remote_copy(
                src_ref=in_ref, dst_ref=out_ref.at[:, me],
                send_sem=send_sems.at[mask - 1], recv_sem=recv_sems.at[mask - 1],
                device_id=peer, device_id_type=MESH)
            cp.start()
            sends.append(cp)
        for mask in range(1, N_DEV):
            peer, pidx = _peer(mask)
            pltpu.make_async_remote_copy(
                src_ref=in_ref, dst_ref=out_ref.at[:, pidx],
                send_sem=send_sems.at[mask - 1], recv_sem=recv_sems.at[mask - 1],
                device_id=peer, device_id_type=MESH).wait_recv()
        for cp in sends:
            cp.wait_send()
        mine.wait()

    return pl.pallas_call(
        body, name=name,
        out_shape=_sds((s, N_DEV, r, c), shard.dtype),
        in_specs=[ANY], out_specs=ANY,
        scratch_shapes=[pltpu.SemaphoreType.DMA((N_DEV - 1,)),
                        pltpu.SemaphoreType.DMA((N_DEV - 1,)),
                        pltpu.SemaphoreType.DMA],
    )(shard)


def all_to_all(parts, name):
    s, _, r, c = parts.shape

    def body(in_ref, out_ref, send_sems, recv_sems, local_sem):
        me = _my_index()
        mine = pltpu.make_async_copy(in_ref.at[:, me], out_ref.at[me], local_sem)
        mine.start()
        sends = []
        for mask in range(1, N_DEV):
            peer, pidx = _peer(mask)
            cp = pltpu.make_async_remote_copy(
                src_ref=in_ref.at[:, pidx], dst_ref=out_ref.at[me],
                send_sem=send_sems.at[mask - 1], recv_sem=recv_sems.at[mask - 1],
                device_id=peer, device_id_type=MESH)
            cp.start()
            sends.append(cp)
        for mask in range(1, N_DEV):
            peer, pidx = _peer(mask)
            pltpu.make_async_remote_copy(
                src_ref=in_ref.at[:, me], dst_ref=out_ref.at[pidx],
                send_sem=send_sems.at[mask - 1], recv_sem=recv_sems.at[mask - 1],
                device_id=peer, device_id_type=MESH).wait_recv()
        for cp in sends:
            cp.wait_send()
        mine.wait()

    return pl.pallas_call(
        body, name=name,
        out_shape=_sds((N_DEV, s, r, c), parts.dtype),
        in_specs=[ANY], out_specs=ANY,
        scratch_shapes=[pltpu.SemaphoreType.DMA((N_DEV - 1,)),
                        pltpu.SemaphoreType.DMA((N_DEV - 1,)),
                        pltpu.SemaphoreType.DMA],
    )(parts)


HBM_SPEC = pl.BlockSpec(memory_space=pltpu.HBM)
SEM_SPEC = pl.BlockSpec(memory_space=pltpu.SEMAPHORE)
EFFECT = pltpu.SideEffectType.DATAFLOW_SIDE_EFFECTING


def _exchange_refs(mode, in_ref, land_ref, me, pidx):
    if mode == "gather":
        return in_ref, land_ref.at[:, me], land_ref.at[:, pidx]
    return in_ref.at[:, pidx], land_ref.at[me], land_ref.at[pidx]


def _landing_shape(mode, a):
    if mode == "gather":
        s, r, c = a.shape
        return (s, N_DEV, r, c)
    s, _, r, c = a.shape
    return (N_DEV, s, r, c)


def _own_copy(mode, in_ref, land_ref, me, sem):
    if mode == "gather":
        return pltpu.make_async_copy(in_ref, land_ref.at[:, me], sem)
    return pltpu.make_async_copy(in_ref.at[:, me], land_ref.at[me], sem)


def exchange_start(mode, arrays, name):
    n = len(arrays)
    lands0 = [lax.empty(_landing_shape(mode, a), a.dtype) for a in arrays]

    def body(*refs):
        in_refs, land_refs = refs[:n], refs[n:2 * n]
        send_sems, recv_sems, own_sems, token = refs[2 * n], refs[2 * n + 1], refs[2 * n + 2], refs[-1]
        mine = _my_index()
        for k in range(n):
            _own_copy(mode, in_refs[k], land_refs[k], mine, own_sems.at[k]).start()
            for mask in range(1, N_DEV):
                peer, pidx = _peer(mask)
                src, dst, _ = _exchange_refs(mode, in_refs[k], land_refs[k], mine, pidx)
                sem = k * (N_DEV - 1) + mask - 1
                pltpu.make_async_remote_copy(
                    src_ref=src, dst_ref=dst, send_sem=send_sems.at[sem], recv_sem=recv_sems.at[sem],
                    device_id=peer, device_id_type=MESH).start()
        token[...] = jnp.zeros_like(token)

    nsem = n * (N_DEV - 1)
    outs = pl.pallas_call(
        body, name=name,
        out_shape=(pltpu.SemaphoreType.DMA((nsem,)), pltpu.SemaphoreType.DMA((nsem,)),
                   pltpu.SemaphoreType.DMA((n,)),
                   *[pltpu.HBM(a.shape, a.dtype) for a in arrays],
                   *[pltpu.HBM(l.shape, l.dtype) for l in lands0], _sds((8, LANES), F32)),
        in_specs=[HBM_SPEC] * (2 * n),
        out_specs=(SEM_SPEC, SEM_SPEC, SEM_SPEC, *[HBM_SPEC] * (2 * n),
                   pl.BlockSpec(memory_space=pltpu.VMEM)),
        input_output_aliases={k: 3 + k for k in range(2 * n)},
        compiler_params=pltpu.CompilerParams(has_side_effects=EFFECT),
    )(*[pltpu.with_memory_space_constraint(a, pltpu.HBM) for a in arrays],
      *[pltpu.with_memory_space_constraint(l, pltpu.HBM) for l in lands0])
    return dict(mode=mode, n=n, send=outs[0], recv=outs[1], own=outs[2], ins=outs[3:3 + n],
                lands=outs[3 + n:3 + 2 * n], token=outs[-1])


def exchange_wait(handle, after, name):
    n, mode = handle["n"], handle["mode"]

    def body(*refs):
        in_refs, land_refs = refs[:n], refs[n:2 * n]
        send_sems, recv_sems, own_sems = refs[2 * n], refs[2 * n + 1], refs[2 * n + 2]
        mine = _my_index()
        for k in range(n):
            _own_copy(mode, in_refs[k], land_refs[k], mine, own_sems.at[k]).wait()
            for mask in range(1, N_DEV):
                peer, pidx = _peer(mask)
                src, _, here = _exchange_refs(mode, in_refs[k], land_refs[k], mine, pidx)
                sem = k * (N_DEV - 1) + mask - 1
                cp = pltpu.make_async_remote_copy(
                    src_ref=src, dst_ref=here, send_sem=send_sems.at[sem], recv_sem=recv_sems.at[sem],
                    device_id=peer, device_id_type=MESH)
                cp.wait_send()
                cp.wait_recv()

    thru = (*handle["ins"], *handle["lands"])
    outs = pl.pallas_call(
        body, name=name,
        out_shape=tuple(pltpu.HBM(a.shape, a.dtype) for a in thru),
        in_specs=[HBM_SPEC] * (2 * n) + [SEM_SPEC, SEM_SPEC, SEM_SPEC, ANY],
        out_specs=tuple([HBM_SPEC] * (2 * n)),
        input_output_aliases={k: k for k in range(2 * n)},
        compiler_params=pltpu.CompilerParams(has_side_effects=EFFECT),
    )(*thru, handle["send"], handle["recv"], handle["own"], after)
    return list(outs[n:])


def all_reduce_small(packed, name, deps=()):
    rows = packed.shape[0]
    nd = len(deps)

    def body(in_ref, *rest):
        out_ref, slots, send_sems, recv_sems = rest[nd:]
        me = _my_index()
        sends = []
        for mask in range(1, N_DEV):
            peer, _ = _peer(mask)
            cp = pltpu.make_async_remote_copy(
                src_ref=in_ref, dst_ref=slots.at[me],
                send_sem=send_sems.at[mask - 1], recv_sem=recv_sems.at[mask - 1],
                device_id=peer, device_id_type=MESH)
            cp.start()
            sends.append(cp)
        slots[me] = in_ref[...]
        for mask in range(1, N_DEV):
            peer, pidx = _peer(mask)
            pltpu.make_async_remote_copy(
                src_ref=in_ref, dst_ref=slots.at[pidx],
                send_sem=send_sems.at[mask - 1], recv_sem=recv_sems.at[mask - 1],
                device_id=peer, device_id_type=MESH).wait_recv()
        for cp in sends:
            cp.wait_send()
        total = slots[0]
        for p in range(1, N_DEV):
            total = total + slots[p]
        out_ref[...] = total

    return pl.pallas_call(
        body, name=name,
        out_shape=_sds((rows, LANES), F32),
        in_specs=[pl.BlockSpec(memory_space=pltpu.VMEM)] + [ANY] * nd,
        out_specs=pl.BlockSpec(memory_space=pltpu.VMEM),
        scratch_shapes=[pltpu.VMEM((N_DEV, rows, LANES), F32),
                        pltpu.SemaphoreType.DMA((N_DEV - 1,)),
                        pltpu.SemaphoreType.DMA((N_DEV - 1,))],
    )(packed, *deps)


def matmul(dims, a, b, out_sds, grid, a_spec, b_spec, o_spec, acc_shape, *, name, alpha=1.0,
           bias=None, bias_spec=None, res=None, res_spec=None, colsum_sds=None, colsum_spec=None,
           deps=()):
    nk = grid[2]
    has_bias, has_res, has_cs = bias is not None, res is not None, colsum_sds is not None
    if has_cs:
        assert grid[0] == 1 and dims == TN

    def body(*refs):
        a_ref, b_ref = refs[0], refs[1]
        pos = 2
        bias_ref = res_ref = cs_ref = None
        if has_bias:
            bias_ref = refs[pos]; pos += 1
        if has_res:
            res_ref = refs[pos]; pos += 1
        pos += len(deps)
        o_ref = refs[pos]; pos += 1
        if has_cs:
            cs_ref = refs[pos]; pos += 1
        k = pl.program_id(2)
        bval = b_ref[...]
        part = lax.dot_general(a_ref[...].astype(BF16), bval.astype(BF16), dims,
                               preferred_element_type=F32)

        def finish(total):
            r = total * alpha if alpha != 1.0 else total
            if has_bias:
                r = r + bias_ref[...]
            if has_res:
                r = r + res_ref[...].astype(F32)
            o_ref[...] = r.astype(o_ref.dtype)

        if has_cs:
            csum = jnp.sum(bval.astype(F32), axis=0, keepdims=True)

            @pl.when(k == 0)
            def _():
                cs_ref[...] = csum

            @pl.when(k > 0)
            def _():
                cs_ref[...] += csum

        if nk == 1:
            finish(part)
        else:
            acc_ref = refs[pos]

            @pl.when(k == 0)
            def _():
                acc_ref[...] = part

            @pl.when(k > 0)
            def _():
                acc_ref[...] += part

            @pl.when(k == nk - 1)
            def _():
                finish(acc_ref[...])

    in_specs, args = [a_spec, b_spec], [a, b]
    if has_bias:
        in_specs.append(bias_spec); args.append(bias)
    if has_res:
        in_specs.append(res_spec); args.append(res)
    in_specs += [ANY] * len(deps)
    args += list(deps)
    out_shape, out_specs = [out_sds], [o_spec]
    if has_cs:
        out_shape.append(colsum_sds); out_specs.append(colsum_spec)
    scratch = [] if nk == 1 else [pltpu.VMEM(acc_shape, F32)]
    outs = pl.pallas_call(
        body, name=name, grid=grid, in_specs=in_specs, out_specs=out_specs, out_shape=out_shape,
        scratch_shapes=scratch, compiler_params=_cparams(3))(*args)
    return outs if has_cs else outs[0]


def _sigmoid(z):
    return 1.0 / (1.0 + jnp.exp(-z))


def _log_sigmoid(z):
    return jnp.minimum(z, 0.0) - jnp.log(1.0 + jnp.exp(-jnp.abs(z)))


def rmsnorm_fwd(x, gain, tm, name, deps=()):
    t, d = x.shape

    def body(x_ref, g_ref, *rest):
        o_ref = rest[-1]
        xf = x_ref[...]
        r = lax.rsqrt(jnp.mean(xf * xf, axis=-1, keepdims=True) + RMS_EPS)
        o_ref[...] = (xf * r * g_ref[...]).astype(o_ref.dtype)

    return pl.pallas_call(
        body, name=name, grid=(t // tm,),
        in_specs=[pl.BlockSpec((tm, d), lambda i: (i, 0)), pl.BlockSpec((1, d), lambda i: (0, 0))]
        + [ANY] * len(deps),
        out_specs=pl.BlockSpec((tm, d), lambda i: (i, 0)),
        out_shape=_sds((t, d), BF16), compiler_params=_cparams(1))(x, gain, *deps)


def rmsnorm_bwd(x, gain, dh, dres, tm, name):
    t, d = x.shape

    def body(x_ref, g_ref, dh_ref, dres_ref, dx_ref, dg_ref):
        i = pl.program_id(0)
        xf = x_ref[...]
        r = lax.rsqrt(jnp.mean(xf * xf, axis=-1, keepdims=True) + RMS_EPS)
        xhat = xf * r
        dh_v = dh_ref[...]
        dxhat = dh_v * g_ref[...]
        dx = r * (dxhat - xhat * jnp.mean(dxhat * xhat, axis=-1, keepdims=True))
        dx_ref[...] = dres_ref[...] + dx
        dg = jnp.sum(dh_v * xhat, axis=0, keepdims=True)

        @pl.when(i == 0)
        def _():
            dg_ref[...] = dg

        @pl.when(i > 0)
        def _():
            dg_ref[...] += dg

    row = pl.BlockSpec((tm, d), lambda i: (i, 0))
    vec = pl.BlockSpec((1, d), lambda i: (0, 0))
    return pl.pallas_call(
        body, name=name, grid=(t // tm,), in_specs=[row, vec, row, row], out_specs=[row, vec],
        out_shape=[_sds((t, d), F32), _sds((1, d), F32)], compiler_params=_cparams(1))(x, gain, dh, dres)


def loss_head(x, gain, target, tm, name):
    t, d = x.shape

    def body(x_ref, g_ref, tgt_ref, dx_ref, dg_ref, loss_ref):
        i = pl.program_id(0)
        xf = x_ref[...]
        g = g_ref[...]
        r = lax.rsqrt(jnp.mean(xf * xf, axis=-1, keepdims=True) + RMS_EPS)
        xhat = xf * r
        err = xhat * g - tgt_ref[...]
        part = 0.5 * jnp.sum(jnp.mean(err * err, axis=-1, keepdims=True))
        dy = err * (1.0 / d)
        dxhat = dy * g
        dx_ref[...] = r * (dxhat - xhat * jnp.mean(dxhat * xhat, axis=-1, keepdims=True))
        dg = jnp.sum(dy * xhat, axis=0, keepdims=True)
        lpart = jnp.full((8, LANES), part, F32)

        @pl.when(i == 0)
        def _():
            dg_ref[...] = dg
            loss_ref[...] = lpart

        @pl.when(i > 0)
        def _():
            dg_ref[...] += dg
            loss_ref[...] += lpart

    row = pl.BlockSpec((tm, d), lambda i: (i, 0))
    vec = pl.BlockSpec((1, d), lambda i: (0, 0))
    return pl.pallas_call(
        body, name=name, grid=(t // tm,), in_specs=[row, vec, row],
        out_specs=[row, vec, pl.BlockSpec((8, LANES), lambda i: (0, 0))],
        out_shape=[_sds((t, d), F32), _sds((1, d), F32), _sds((8, LANES), F32)],
        compiler_params=_cparams(1))(x, gain, target)


def ffn_in_swiglu(hn, wa, s, tm, name):
    t = hn.shape[0]

    def body(h_ref, wg_ref, wu_ref, gu_ref, act_ref):
        h = h_ref[...]
        g = jnp.dot(h, wg_ref[...], preferred_element_type=F32)
        u = jnp.dot(h, wu_ref[...], preferred_element_type=F32)
        gu_ref[0] = g.astype(gu_ref.dtype)
        gu_ref[1] = u.astype(gu_ref.dtype)
        act_ref[...] = (g * _sigmoid(g) * u).astype(act_ref.dtype)

    return pl.pallas_call(
        body, name=name, grid=(t // tm, 4),
        in_specs=[pl.BlockSpec((tm, D_MODEL), lambda i, j: (i, 0)),
                  pl.BlockSpec((None, None, D_MODEL, FF_BLK), lambda i, j: (s, j, 0, 0)),
                  pl.BlockSpec((None, None, D_MODEL, FF_BLK), lambda i, j: (s, j + 4, 0, 0))],
        out_specs=[pl.BlockSpec((None, 2, tm, FF_BLK), lambda i, j: (j, 0, i, 0)),
                   pl.BlockSpec((None, tm, FF_BLK), lambda i, j: (j, i, 0))],
        out_shape=[_sds((4, 2, t, FF_BLK), BF16), _sds((4, t, FF_BLK), BF16)],
        compiler_params=_cparams(2))(hn, wa, wa)


def ffn_dact_swiglu(dy, wb, gu, s, tm, name):
    t = dy.shape[0]

    def body(dy_ref, w_ref, gu_ref, o_ref):
        da = 0.5 * lax.dot_general(dy_ref[...].astype(BF16), w_ref[...], NT, preferred_element_type=F32)
        g = gu_ref[0].astype(F32)
        u = gu_ref[1].astype(F32)
        sg = _sigmoid(g)
        o_ref[0] = (da * u * (sg * (1.0 + g * (1.0 - sg)))).astype(o_ref.dtype)
        o_ref[1] = (da * g * sg).astype(o_ref.dtype)

    blk = pl.BlockSpec((None, 2, tm, FF_BLK), lambda i, j: (j, 0, i, 0))
    return pl.pallas_call(
        body, name=name, grid=(t // tm, 4),
        in_specs=[pl.BlockSpec((tm, D_MODEL), lambda i, j: (i, 0)),
                  pl.BlockSpec((None, None, FF_BLK, D_MODEL), lambda i, j: (s, j, 0, 0)), blk],
        out_specs=blk, out_shape=_sds((4, 2, t, FF_BLK), BF16),
        compiler_params=_cparams(2))(dy, wb, gu)


def ffn_out_residual(act, wb, x, s, tm, name):
    t = x.shape[0]

    def body(a_ref, w_ref, x_ref, o_ref):
        acc = jnp.dot(a_ref[0], w_ref[0], preferred_element_type=F32)
        for k in range(1, 4):
            acc = acc + jnp.dot(a_ref[k], w_ref[k], preferred_element_type=F32)
        o_ref[...] = x_ref[...] + 0.5 * acc

    row = pl.BlockSpec((tm, D_MODEL), lambda i: (i, 0))
    return pl.pallas_call(
        body, name=name, grid=(t // tm,),
        in_specs=[pl.BlockSpec((4, tm, FF_BLK), lambda i: (0, i, 0)),
                  pl.BlockSpec((None, 4, FF_BLK, D_MODEL), lambda i: (s, 0, 0, 0)), row],
        out_specs=row, out_shape=_sds((t, D_MODEL), F32), compiler_params=_cparams(1))(act, wb, x)


def ffn_dh(dgu, wa, s, tm, name, deps):
    t = dgu.shape[2]

    def body(g_ref, w_ref, *rest):
        o_ref = rest[-1]
        acc = lax.dot_general(g_ref[0, 0], w_ref[0], NT, preferred_element_type=F32)
        for p in range(1, N_DEV):
            acc = acc + lax.dot_general(g_ref[p % 4, p // 4], w_ref[p], NT, preferred_element_type=F32)
        o_ref[...] = acc

    return pl.pallas_call(
        body, name=name, grid=(t // tm,),
        in_specs=[pl.BlockSpec((4, 2, tm, FF_BLK), lambda i: (0, 0, i, 0)),
                  pl.BlockSpec((None, N_DEV, D_MODEL, FF_BLK), lambda i: (s, 0, 0, 0))] + [ANY] * len(deps),
        out_specs=pl.BlockSpec((tm, D_MODEL), lambda i: (i, 0)),
        out_shape=_sds((t, D_MODEL), F32), compiler_params=_cparams(1))(dgu, wa, *deps)


def merge_fwd(gates, ya, yb, yc, tm, name):
    t, d = ya.shape

    def body(ga_ref, gb_ref, gc_ref, ya_ref, yb_ref, yc_ref, o_ref):
        m = (_sigmoid(ga_ref[...]) * ya_ref[...] + _sigmoid(gb_ref[...]) * yb_ref[...]
             + _sigmoid(gc_ref[...]) * yc_ref[...])
        o_ref[...] = m.astype(o_ref.dtype)

    row = pl.BlockSpec((tm, d), lambda i: (i, 0))
    gspecs = [pl.BlockSpec((tm, d), functools.partial(lambda i, a: (i, a), a=a)) for a in range(3)]
    return pl.pallas_call(
        body, name=name, grid=(t // tm,), in_specs=gspecs + [row, row, row], out_specs=row,
        out_shape=_sds((t, d), BF16), compiler_params=_cparams(1))(gates, gates, gates, ya, yb, yc)


def merge_bwd(dm, gates, ya, yb, yc, tm, name):
    t, d = ya.shape

    def body(dm_ref, g_ref, y_ref, dg_ref, dy_ref):
        dmv = dm_ref[...]
        s = _sigmoid(g_ref[...])
        dy_ref[...] = (dmv * s).astype(dy_ref.dtype)
        dg_ref[...] = (dmv * y_ref[...] * s * (1.0 - s)).astype(dg_ref.dtype)

    outs = []
    dgs = []
    for a, y in enumerate((ya, yb, yc)):
        row = pl.BlockSpec((tm, d), lambda i: (i, 0))
        gspec = pl.BlockSpec((tm, d), functools.partial(lambda i, a: (i, a), a=a))
        dg, dy = pl.pallas_call(
            functools.partial(body), name=f"{name}_{a}", grid=(t // tm,),
            in_specs=[row, gspec, row], out_specs=[row, row],
            out_shape=[_sds((t, d), BF16), _sds((t, d), BF16)],
            compiler_params=_cparams(1))(dm, gates, y)
        dgs.append(dg)
        outs.append(dy)
    return dgs, outs


def _iota2(shape, dim):
    return lax.broadcasted_iota(jnp.int32, shape, dim)


def forget_cumsum(f, name):
    t = f.shape[0]
    nq = t // QB

    def body(f_ref, fcol_ref, frow_ref, carry):
        j = pl.program_id(0)

        @pl.when(j == 0)
        def _():
            carry[...] = jnp.zeros_like(carry)

        logf = _log_sigmoid(f_ref[...])
        tri = (_iota2((QB, QB), 1) <= _iota2((QB, QB), 0)).astype(F32)
        blk = jnp.dot(tri, logf, precision=HIGHEST, preferred_element_type=F32) + carry[...]
        carry[...] += jnp.sum(logf, axis=0, keepdims=True)
        fcol_ref[...] = blk
        frow_ref[...] = blk.T[0:8, :]

    return pl.pallas_call(
        body, name=name, grid=(nq,),
        in_specs=[pl.BlockSpec((QB, LANES), lambda j: (j, 0))],
        out_specs=[pl.BlockSpec((QB, LANES), lambda j: (j, 0)),
                   pl.BlockSpec((None, 8, QB), lambda j: (j, 0, 0))],
        out_shape=[_sds((t, LANES), F32), _sds((nq, 8, QB), F32)],
        scratch_shapes=[pltpu.VMEM((1, LANES), F32)], compiler_params=_cparams(1))(f)


def forget_cumsum_bwd(dfrow, f, name):
    t = f.shape[0]
    nq = t // QB

    def body(dfr_ref, f_ref, df_ref, carry):
        jj = pl.program_id(0)

        @pl.when(jj == 0)
        def _():
            carry[...] = jnp.zeros_like(carry)

        padded = jnp.concatenate([dfr_ref[...], jnp.zeros((QB - 8, QB), F32)], axis=0)
        dfcol = padded.T
        tri = (_iota2((QB, QB), 1) >= _iota2((QB, QB), 0)).astype(F32)
        dlogf = jnp.dot(tri, dfcol, precision=HIGHEST, preferred_element_type=F32) + carry[...]
        carry[...] += jnp.sum(dfcol, axis=0, keepdims=True)
        df_ref[...] = dlogf * _sigmoid(-f_ref[...])

    return pl.pallas_call(
        body, name=name, grid=(nq,),
        in_specs=[pl.BlockSpec((None, 8, QB), lambda jj: (nq - 1 - jj, 0, 0)),
                  pl.BlockSpec((QB, LANES), lambda jj: (nq - 1 - jj, 0))],
        out_specs=pl.BlockSpec((QB, LANES), lambda jj: (nq - 1 - jj, 0)),
        out_shape=_sds((t, LANES), F32),
        scratch_shapes=[pltpu.VMEM((1, LANES), F32)], compiler_params=_cparams(1))(dfrow, f)


REL_DIAG = 768
REL_SHIFT = REL_DIAG - (QB - 1)


def _diag_onehot():
    u = _iota2((REL_PAD, REL_DIAG), 1)
    rel = jnp.clip(CH_KEYS - 1 - u, -MAX_REL, MAX_REL) + MAX_REL
    return (_iota2((REL_PAD, REL_DIAG), 0) == rel).astype(F32)


def rel_bias_build(tab_t, name):
    def body(tab_ref, o_ref):
        diag = jnp.dot(tab_ref[...], _diag_onehot(), precision=HIGHEST, preferred_element_type=F32)
        for h in range(N_HEADS_CH):
            rows = jnp.broadcast_to(diag[h:h + 1, :], (QB, REL_DIAG))
            o_ref[h] = pltpu.roll(rows, REL_SHIFT, 1, stride=1, stride_axis=0)[:, :CH_KEYS]

    return pl.pallas_call(
        body, name=name, out_shape=_sds((N_HEADS_CH, QB, CH_KEYS), F32),
        in_specs=[pl.BlockSpec(memory_space=pltpu.VMEM)], out_specs=pl.BlockSpec(memory_space=pltpu.VMEM),
    )(tab_t)


def rel_bias_scatter(dbias, name):
    def body(db_ref, o_ref, ddiag):
        flip = (_iota2((QB, QB), 0) + _iota2((QB, QB), 1) == QB - 1).astype(F32)
        for h in range(N_HEADS_CH):
            padded = jnp.concatenate([db_ref[h], jnp.zeros((QB, REL_DIAG - CH_KEYS), F32)], axis=1)
            flipped = jnp.dot(flip, padded, precision=HIGHEST, preferred_element_type=F32)
            unrolled = pltpu.roll(flipped, 0, 1, stride=1, stride_axis=0)
            ddiag[h:h + 1, :] = jnp.sum(unrolled, axis=0, keepdims=True)
        o_ref[...] = lax.dot_general(ddiag[...], _diag_onehot(), NT, precision=HIGHEST,
                                     preferred_element_type=F32)

    return pl.pallas_call(
        body, name=name, out_shape=_sds((N_HEADS_CH, REL_PAD), F32),
        in_specs=[pl.BlockSpec(memory_space=pltpu.VMEM)], out_specs=pl.BlockSpec(memory_space=pltpu.VMEM),
        scratch_shapes=[pltpu.VMEM((N_HEADS_CH, REL_DIAG), F32)],
    )(dbias)


def _hl(h):
    return slice(h * HEAD_DIM, (h + 1) * HEAD_DIM)


def _split_dot(x, tri_bf16):
    hi = x.astype(BF16)
    lo = (x - hi.astype(F32)).astype(BF16)
    return (jnp.dot(hi, tri_bf16, preferred_element_type=F32)
            + jnp.dot(lo, tri_bf16, preferred_element_type=F32))


def _rows(j):
    return pl.ds(pl.multiple_of(j * QB, QB), QB)


def _krows(g):
    return pl.ds(pl.multiple_of(g * KB, KB), KB)


def _log_sigmoid_pair(z):
    sp = jnp.log(1.0 + jnp.exp(-jnp.abs(z)))
    return jnp.minimum(z, 0.0) - sp, -jnp.maximum(z, 0.0) - sp


def _qkv_specs(t, col0, n_pairs):
    q_spec = pl.BlockSpec((QB, LANES), lambda hp, i: (i, col0 + hp))
    k_spec = pl.BlockSpec((t, LANES), lambda hp, i: (0, col0 + n_pairs + hp))
    v_spec = pl.BlockSpec((t, LANES), lambda hp, i: (0, col0 + 2 * n_pairs + hp))
    return q_spec, k_spec, v_spec


def sb_fwd(qkv, name):
    t = qkv.shape[0]
    nq = t // QB

    def body(q_ref, k_ref, v_ref, o_ref):
        i = pl.program_id(1)
        groups = i // KSUB + 1
        tri_after = (_iota2((KB, KB), 0) > _iota2((KB, KB), 1)).astype(BF16)
        t_idx = i * QB + _iota2((QB, KB), 0)
        qs = [q_ref[:, _hl(h)] for h in range(2)]

        def step(gg, carry):
            g = groups - 1 - gg
            strict = (g * KB + _iota2((QB, KB), 1)) < t_idx
            out = []
            for h in range(2):
                tail, acc = carry[2 * h], carry[2 * h + 1]
                k = k_ref[_krows(g), _hl(h)]
                v = v_ref[_krows(g), _hl(h)]
                z = lax.dot_general(qs[h], k, NT, preferred_element_type=F32) * SCALE
                lb, lf = _log_sigmoid_pair(z)
                lf = jnp.where(strict, lf, 0.0)
                between = _split_dot(lf, tri_after) + tail
                w = jnp.where(strict, jnp.exp(lb + between), 0.0)
                acc = acc + jnp.dot(w.astype(BF16), v, preferred_element_type=F32)
                out += [tail + jnp.sum(lf, axis=1, keepdims=True), acc]
            return tuple(out)

        init = (jnp.zeros((QB, 1), F32), jnp.zeros((QB, HEAD_DIM), F32)) * 2
        res = lax.fori_loop(0, groups, step, init)
        for h in range(2):
            o_ref[:, _hl(h)] = res[2 * h + 1].astype(o_ref.dtype)

    q_spec, k_spec, v_spec = _qkv_specs(t, 0, 2)
    return pl.pallas_call(
        body, name=name, grid=(2, nq), in_specs=[q_spec, k_spec, v_spec],
        out_specs=pl.BlockSpec((QB, LANES), lambda hp, i: (i, hp)),
        out_shape=_sds((t, W_SB), BF16), compiler_params=_cparams(2))(qkv, qkv, qkv)


def sb_bwd(qkv, do, name):
    t = qkv.shape[0]
    nq = t // QB

    def body(q_ref, k_ref, v_ref, do_ref, dq_ref, dk_ref, dv_ref, w_scr):
        i = pl.program_id(1)

        @pl.when(i == 0)
        def _():
            dk_ref[...] = jnp.zeros_like(dk_ref)
            dv_ref[...] = jnp.zeros_like(dv_ref)

        groups = i // KSUB + 1
        tri_after = (_iota2((KB, KB), 0) > _iota2((KB, KB), 1)).astype(BF16)
        tri_before = (_iota2((KB, KB), 0) < _iota2((KB, KB), 1)).astype(BF16)
        t_idx = i * QB + _iota2((QB, KB), 0)
        qs = [q_ref[:, _hl(h)] for h in range(2)]
        dos = [do_ref[:, _hl(h)] for h in range(2)]

        def weights(gg, tails):
            g = groups - 1 - gg
            strict = (g * KB + _iota2((QB, KB), 1)) < t_idx
            out = []
            for h in range(2):
                k = k_ref[_krows(g), _hl(h)]
                z = lax.dot_general(qs[h], k, NT, preferred_element_type=F32) * SCALE
                lb, lf = _log_sigmoid_pair(z)
                lf = jnp.where(strict, lf, 0.0)
                between = _split_dot(lf, tri_after) + tails[h]
                w_scr[h, g] = jnp.where(strict, jnp.exp(lb + between), 0.0)
                out.append(tails[h] + jnp.sum(lf, axis=1, keepdims=True))
            return tuple(out)

        lax.fori_loop(0, groups, weights, (jnp.zeros((QB, 1), F32),) * 2)

        def grads(g, carry):
            strict = (g * KB + _iota2((QB, KB), 1)) < t_idx
            out = []
            for h in range(2):
                head, dq = carry[2 * h], carry[2 * h + 1]
                k = k_ref[_krows(g), _hl(h)]
                v = v_ref[_krows(g), _hl(h)]
                w = w_scr[h, g]
                z = lax.dot_general(qs[h], k, NT, preferred_element_type=F32) * SCALE
                beta = _sigmoid(z)
                e = lax.dot_general(dos[h], v, NT, preferred_element_type=F32) * w
                before = _split_dot(e, tri_before) + head
                dz = jnp.where(strict, e * (1.0 - beta) - before * beta, 0.0) * SCALE
                dzb = dz.astype(BF16)
                dq = dq + jnp.dot(dzb, k, preferred_element_type=F32)
                dk_ref[_krows(g), _hl(h)] += lax.dot_general(dzb, qs[h], TN, preferred_element_type=F32)
                dv_ref[_krows(g), _hl(h)] += lax.dot_general(w.astype(BF16), dos[h], TN,
                                                             preferred_element_type=F32)
                out += [head + jnp.sum(e, axis=1, keepdims=True), dq]
            return tuple(out)

        init = (jnp.zeros((QB, 1), F32), jnp.zeros((QB, HEAD_DIM), F32)) * 2
        res = lax.fori_loop(0, groups, grads, init)
        for h in range(2):
            dq_ref[:, _hl(h)] = res[2 * h + 1].astype(dq_ref.dtype)

    q_spec, k_spec, v_spec = _qkv_specs(t, 0, 2)
    blk = pl.BlockSpec((QB, LANES), lambda hp, i: (i, hp))
    full = pl.BlockSpec((t, LANES), lambda hp, i: (0, hp))
    return pl.pallas_call(
        body, name=name, grid=(2, nq), in_specs=[q_spec, k_spec, v_spec, blk],
        out_specs=[blk, full, full],
        out_shape=[_sds((t, W_SB), BF16), _sds((t, W_SB), F32), _sds((t, W_SB), F32)],
        scratch_shapes=[pltpu.VMEM((2, t // KB, QB, KB), F32)],
        compiler_params=_cparams(2))(qkv, qkv, qkv, do)


def fox_fwd(qkv, fcol, frow, name):
    t = qkv.shape[0]
    nq = t // QB

    def body(q_ref, k_ref, v_ref, fc_ref, fr_ref, o_ref, lse_ref):
        hp = pl.program_id(0)
        i = pl.program_id(1)
        groups = i // KSUB + 1
        t_idx = i * QB + _iota2((QB, KB), 0)
        lane = _iota2((QB, LANES), 1)
        sub = _iota2((8, KB), 0)
        qs = [q_ref[:, _hl(h)] for h in range(2)]
        f_qs = [jnp.sum(jnp.where(lane == hp * 2 + h, fc_ref[...], 0.0), axis=1, keepdims=True)
                for h in range(2)]

        def step(g, carry):
            causal = (g * KB + _iota2((QB, KB), 1)) <= t_idx
            fr = fr_ref[g]
            out = []
            for h in range(2):
                m, l, acc = carry[3 * h:3 * h + 3]
                k = k_ref[_krows(g), _hl(h)]
                v = v_ref[_krows(g), _hl(h)]
                f_k = jnp.sum(jnp.where(sub == hp * 2 + h, fr, 0.0), axis=0, keepdims=True)
                z = lax.dot_general(qs[h], k, NT, preferred_element_type=F32) * SCALE + f_qs[h] - f_k
                z = jnp.where(causal, z, NEG)
                m_new = jnp.maximum(m, jnp.max(z, axis=1, keepdims=True))
                p = jnp.exp(z - m_new)
                corr = jnp.exp(m - m_new)
                l = l * corr + jnp.sum(p, axis=1, keepdims=True)
                acc = acc * corr + jnp.dot(p.astype(BF16), v, preferred_element_type=F32)
                out += [m_new, l, acc]
            return tuple(out)

        init = (jnp.full((QB, 1), NEG, F32), jnp.zeros((QB, 1), F32), jnp.zeros((QB, HEAD_DIM), F32)) * 2
        res = lax.fori_loop(0, groups, step, init)
        for h in range(2):
            m, l, acc = res[3 * h:3 * h + 3]
            o_ref[:, _hl(h)] = (acc / l).astype(o_ref.dtype)
            lse_ref[:, _hl(h)] = jnp.broadcast_to(m + jnp.log(l), (QB, HEAD_DIM))

    q_spec, k_spec, v_spec = _qkv_specs(t, 18, 2)
    blk = pl.BlockSpec((QB, LANES), lambda hp, i: (i, hp))
    return pl.pallas_call(
        body, name=name, grid=(2, nq),
        in_specs=[q_spec, k_spec, v_spec, pl.BlockSpec((QB, LANES), lambda hp, i: (i, 0)),
                  pl.BlockSpec((t // KB, 8, KB), lambda hp, i: (0, 0, 0))],
        out_specs=[blk, blk],
        out_shape=[_sds((t, W_FOX), BF16), _sds((t, W_FOX), F32)],
        compiler_params=_cparams(2))(qkv, qkv, qkv, fcol, frow)


def fox_bwd(qkv, fcol, frow, o, lse, do, name):
    t = qkv.shape[0]
    nq = t // QB

    def body(q_ref, k_ref, v_ref, fc_ref, fr_ref, o_ref, lse_ref, do_ref,
             dq_ref, dk_ref, dv_ref, dfr_ref):
        hp = pl.program_id(0)
        i = pl.program_id(1)

        @pl.when(i == 0)
        def _():
            dk_ref[...] = jnp.zeros_like(dk_ref)
            dv_ref[...] = jnp.zeros_like(dv_ref)

        @pl.when((i == 0) & (hp == 0))
        def _():
            dfr_ref[...] = jnp.zeros_like(dfr_ref)

        groups = i // KSUB + 1
        t_idx = i * QB + _iota2((QB, KB), 0)
        lane = _iota2((QB, LANES), 1)
        sub = _iota2((8, KB), 0)
        qs = [q_ref[:, _hl(h)] for h in range(2)]
        dos = [do_ref[:, _hl(h)] for h in range(2)]
        f_qs = [jnp.sum(jnp.where(lane == hp * 2 + h, fc_ref[...], 0.0), axis=1, keepdims=True)
                for h in range(2)]
        lse_qs = [lse_ref[:, h * HEAD_DIM:h * HEAD_DIM + 1] for h in range(2)]
        deltas = [jnp.sum(dos[h].astype(F32) * o_ref[:, _hl(h)].astype(F32), axis=1, keepdims=True)
                  for h in range(2)]

        def step(g, dqs):
            causal = (g * KB + _iota2((QB, KB), 1)) <= t_idx
            fr = fr_ref[g]
            out = []
            dfr = jnp.zeros((8, KB), F32)
            for h in range(2):
                k = k_ref[_krows(g), _hl(h)]
                v = v_ref[_krows(g), _hl(h)]
                f_k = jnp.sum(jnp.where(sub == hp * 2 + h, fr, 0.0), axis=0, keepdims=True)
                z = lax.dot_general(qs[h], k, NT, preferred_element_type=F32) * SCALE + f_qs[h] - f_k
                p = jnp.where(causal, jnp.exp(z - lse_qs[h]), 0.0)
                dp = lax.dot_general(dos[h], v, NT, preferred_element_type=F32)
                ds = p * (dp - deltas[h])
                dsb = (ds * SCALE).astype(BF16)
                out.append(dqs[h] + jnp.dot(dsb, k, preferred_element_type=F32))
                dk_ref[_krows(g), _hl(h)] += lax.dot_general(dsb, qs[h], TN, preferred_element_type=F32)
                dv_ref[_krows(g), _hl(h)] += lax.dot_general(p.astype(BF16), dos[h], TN,
                                                             preferred_element_type=F32)
                colsum = jnp.sum(ds, axis=0, keepdims=True)
                dfr = dfr + jnp.where(sub == hp * 2 + h, -colsum, 0.0)
            dfr_ref[g] += dfr
            return tuple(out)

        res = lax.fori_loop(0, groups, step, (jnp.zeros((QB, HEAD_DIM), F32),) * 2)
        for h in range(2):
            dq_ref[:, _hl(h)] = res[h].astype(dq_ref.dtype)

    q_spec, k_spec, v_spec = _qkv_specs(t, 18, 2)
    blk = pl.BlockSpec((QB, LANES), lambda hp, i: (i, hp))
    full = pl.BlockSpec((t, LANES), lambda hp, i: (0, hp))
    frs = pl.BlockSpec((t // KB, 8, KB), lambda hp, i: (0, 0, 0))
    return pl.pallas_call(
        body, name=name, grid=(2, nq),
        in_specs=[q_spec, k_spec, v_spec, pl.BlockSpec((QB, LANES), lambda hp, i: (i, 0)), frs,
                  blk, blk, blk],
        out_specs=[blk, full, full, frs],
        out_shape=[_sds((t, W_FOX), BF16), _sds((t, W_FOX), F32), _sds((t, W_FOX), F32),
                   _sds((t // KB, 8, KB), F32)],
        compiler_params=_cparams(2))(qkv, qkv, qkv, fcol, frow, o, lse, do)


def _frow_to_groups(frow):
    n = frow.shape[0] // KSUB
    return frow.reshape(n, KSUB, 8, QB).transpose(0, 2, 1, 3).reshape(n, 8, KB)


def _frow_from_groups(frow):
    n = frow.shape[0]
    return frow.reshape(n, 8, KSUB, QB).transpose(0, 2, 1, 3).reshape(n * KSUB, 8, QB)


def _chunk_valid(i):
    qi = _iota2((QB, CH_KEYS), 0)
    kj = _iota2((QB, CH_KEYS), 1)
    dchunk = (qi >> 6) + LEFT_CHUNKS - (kj >> 6)
    return (dchunk >= 0) & (dchunk <= LEFT_CHUNKS) & ((i - (CH_WIN - 1)) * QB + kj >= 0)


CH_PAD = (CH_WIN - 1) * QB


def _window(i):
    return pl.ds(pl.multiple_of(i * QB, QB), CH_KEYS)


def _chunk_probs(q, kw, bias, valid):
    z = lax.dot_general(q, kw, NT, preferred_element_type=F32) * SCALE + bias
    z = jnp.where(valid, z, NEG)
    z = z - jnp.max(z, axis=1, keepdims=True)
    p = jnp.exp(z)
    return p / jnp.sum(p, axis=1, keepdims=True)


def _chunk_specs(t):
    q_spec = pl.BlockSpec((QB, LANES), lambda hp, i: (i, 6 + hp))
    kv_spec = pl.BlockSpec((t + CH_PAD, LANES), lambda hp, i: (0, hp))
    return q_spec, kv_spec


def chunk_fwd(qkv, kp, vp, bias, name):
    t = qkv.shape[0]
    nq = t // QB

    def body(q_ref, k_ref, v_ref, b_ref, o_ref):
        i = pl.program_id(1)
        valid = _chunk_valid(i)
        for h in range(2):
            p = _chunk_probs(q_ref[:, _hl(h)], k_ref[_window(i), _hl(h)], b_ref[h], valid)
            o_ref[:, _hl(h)] = jnp.dot(p.astype(BF16), v_ref[_window(i), _hl(h)],
                                       preferred_element_type=F32).astype(o_ref.dtype)

    q_spec, kv_spec = _chunk_specs(t)
    return pl.pallas_call(
        body, name=name, grid=(4, nq),
        in_specs=[q_spec, kv_spec, kv_spec, pl.BlockSpec((2, QB, CH_KEYS), lambda hp, i: (hp, 0, 0))],
        out_specs=pl.BlockSpec((QB, LANES), lambda hp, i: (i, hp)),
        out_shape=_sds((t, W_CH), BF16), compiler_params=_cparams(2))(qkv, kp, vp, bias)


def chunk_bwd(qkv, kp, vp, bias, do, name):
    t = qkv.shape[0]
    nq = t // QB

    def body(q_ref, k_ref, v_ref, b_ref, do_ref, dq_ref, dk_ref, dv_ref, db_ref):
        i = pl.program_id(1)

        @pl.when(i == 0)
        def _():
            dk_ref[...] = jnp.zeros_like(dk_ref)
            dv_ref[...] = jnp.zeros_like(dv_ref)
            db_ref[...] = jnp.zeros_like(db_ref)

        valid = _chunk_valid(i)
        for h in range(2):
            q = q_ref[:, _hl(h)]
            dov = do_ref[:, _hl(h)]
            kw = k_ref[_window(i), _hl(h)]
            p = _chunk_probs(q, kw, b_ref[h], valid)
            dp = lax.dot_general(dov, v_ref[_window(i), _hl(h)], NT, preferred_element_type=F32)
            ds = p * (dp - jnp.sum(p * dp, axis=1, keepdims=True))
            db_ref[h] += ds
            dsb = (ds * SCALE).astype(BF16)
            dq_ref[:, _hl(h)] = jnp.dot(dsb, kw, preferred_element_type=F32).astype(dq_ref.dtype)
            dk_ref[_window(i), _hl(h)] += lax.dot_general(dsb, q, TN, preferred_element_type=F32)
            dv_ref[_window(i), _hl(h)] += lax.dot_general(p.astype(BF16), dov, TN,
                                                          preferred_element_type=F32)

    q_spec, kv_spec = _chunk_specs(t)
    blk = pl.BlockSpec((QB, LANES), lambda hp, i: (i, hp))
    bspec = pl.BlockSpec((2, QB, CH_KEYS), lambda hp, i: (hp, 0, 0))
    return pl.pallas_call(
        body, name=name, grid=(4, nq), in_specs=[q_spec, kv_spec, kv_spec, bspec, blk],
        out_specs=[blk, kv_spec, kv_spec, bspec],
        out_shape=[_sds((t, W_CH), BF16), _sds((t + CH_PAD, W_CH), F32), _sds((t + CH_PAD, W_CH), F32),
                   _sds((N_HEADS_CH, QB, CH_KEYS), F32)],
        compiler_params=_cparams(2))(qkv, kp, vp, bias, do)


def _sum_parts(p_ref):
    total = p_ref[0].astype(F32)
    for p in range(1, p_ref.shape[0]):
        total = total + p_ref[p].astype(F32)
    return total


def sum_parts(parts, grid, p_spec, o_spec, out_sds, name):
    def body(p_ref, o_ref):
        o_ref[...] = _sum_parts(p_ref)

    return pl.pallas_call(body, name=name, grid=grid, in_specs=[p_spec], out_specs=o_spec,
                          out_shape=out_sds, compiler_params=_cparams(len(grid)))(parts)


def adamw(parts, w, m, v, grid, p_specs, w_spec, name):
    c1 = 1.0 / (1.0 - ADAM_B1 ** ADAM_STEP)
    c2 = 1.0 / (1.0 - ADAM_B2 ** ADAM_STEP)
    n = len(parts)

    def body(*refs):
        w_ref, m_ref, v_ref, g_out, d_out, m_out, v_out = refs[n:]
        g = _sum_parts(refs[0])
        for q in range(1, n):
            g = jnp.where(pl.program_id(0) == q, _sum_parts(refs[q]), g)
        m_new = ADAM_B1 * m_ref[...] + (1.0 - ADAM_B1) * g
        v_new = ADAM_B2 * v_ref[...] + (1.0 - ADAM_B2) * (g * g)
        m_hat = m_new * c1
        v_hat = v_new * c2
        g_out[...] = g
        d_out[...] = -ADAM_LR * (m_hat / (jnp.sqrt(v_hat) + ADAM_EPS) + ADAM_WD * w_ref[...])
        m_out[...] = m_new
        v_out[...] = v_new

    out = _sds(w.shape, F32)
    return pl.pallas_call(
        body, name=name, grid=grid, in_specs=[*p_specs, w_spec, w_spec, w_spec],
        out_specs=[w_spec] * 4, out_shape=[out] * 4,
        compiler_params=_cparams(len(grid)))(*parts, w, m, v)


def _ffn_fwd(x, gain, wa, wb_after, s, tm, tag, deps=()):
    t = x.shape[0]
    hn = rmsnorm_fwd(x, gain, tm, f"rms_{tag}", deps)
    gu, act = ffn_in_swiglu(hn, wa, s, min(2 * tm, t), f"ffn_in_{tag}")
    wb = wb_after(act)
    y = ffn_out_residual(act, wb, x, s, min(2 * tm, t), f"ffn_out_{tag}")
    return y, (hn, gu, act), wb


def _ffn_bwd(dy, x, gain, saved, wa, wb, s, tm, tag, on_grads):
    t = x.shape[0]
    hn, gu, act = saved
    dgu = ffn_dact_swiglu(dy, wb, gu, s, min(2 * tm, t), f"ffn_dact_{tag}")
    dwb = matmul(TN, act, dy, _sds((4, FF_BLK, D_MODEL), BF16), (4, 1, 1),
                 pl.BlockSpec((None, t, FF_BLK), lambda i, j, k: (i, 0, 0)),
                 pl.BlockSpec((t, D_MODEL), lambda i, j, k: (0, 0)),
                 pl.BlockSpec((None, FF_BLK, D_MODEL), lambda i, j, k: (i, 0, 0)),
                 None, name=f"ffn_dwout_{tag}", alpha=0.5)
    dwa = matmul(TN, dgu, hn, _sds((8, FF_BLK, D_MODEL), BF16), (1, 8, 1),
                 pl.BlockSpec((None, None, t, FF_BLK), lambda i, j, k: (j % 4, j // 4, 0, 0)),
                 pl.BlockSpec((t, D_MODEL), lambda i, j, k: (0, 0)),
                 pl.BlockSpec((None, FF_BLK, D_MODEL), lambda i, j, k: (j, 0, 0)),
                 None, name=f"ffn_dwin_{tag}")
    deps = on_grads(dwa, dwb)
    dhn = ffn_dh(dgu, wa, s, tm, f"ffn_dh_{tag}", deps)
    dx, dgain = rmsnorm_bwd(x, gain, dhn, dy, tm, f"rms_bwd_{tag}")
    return dx, dgain


BR_ROWS = ((0, 1), (1, 2), (3, 1))


def _mixer_fwd(x, gain, wqkv, wf, wgate, late_after, bq, bf, bg, bias, layer, tm, tag):
    t = x.shape[0]
    nt = t // tm
    hm = rmsnorm_fwd(x, gain, tm, f"rms_{tag}")
    a_full = pl.BlockSpec((tm, D_MODEL), lambda i, j, k: (i, 0))
    wide_out = pl.BlockSpec((tm, D_MODEL), lambda i, j, k: (i, j))
    wide_b = pl.BlockSpec((1, D_MODEL), lambda i, j, k: (0, j))
    qkv = matmul(NN, hm, wqkv, _sds((t, QKV_WIDTH), BF16), (nt, 3, 1), a_full,
                 pl.BlockSpec((None, D_MODEL, D_MODEL), lambda i, j, k: (layer, 0, j)), wide_out, None,
                 name=f"proj_qkv_{tag}", bias=bq, bias_spec=wide_b)
    gates = matmul(NN, hm, wgate, _sds((t, 3 * D_MODEL), F32), (nt, 3, 1), a_full,
                   pl.BlockSpec((None, D_MODEL, D_MODEL), lambda i, j, k: (layer + 1, 0,j)), wide_out,
                   None, name=f"proj_gate_{tag}", bias=bg, bias_spec=wide_b)
    f = matmul(NN, hm, wf, _sds((t, LANES), F32), (nt, 1, 1), a_full,
               pl.BlockSpec((None, D_MODEL, LANES), lambda i, j, k: (layer, 0, 0)),
               pl.BlockSpec((tm, LANES), lambda i, j, k: (i, 0)), None,
               name=f"proj_f_{tag}", bias=bf, bias_spec=pl.BlockSpec((1, LANES), lambda i, j, k: (0, 0)))
    fcol, frow = forget_cumsum(f, f"fcum_{tag}")
    frow = _frow_to_groups(frow)
    o_sb = sb_fwd(qkv, f"sb_fwd_{tag}")
    kp = jnp.pad(qkv[:, 10 * LANES:14 * LANES], ((CH_PAD, 0), (0, 0)))
    vp = jnp.pad(qkv[:, 14 * LANES:18 * LANES], ((CH_PAD, 0), (0, 0)))
    o_ch = chunk_fwd(qkv, kp, vp, bias, f"chunk_fwd_{tag}")
    o_fox, lse = fox_fwd(qkv, fcol, frow, f"fox_fwd_{tag}")
    wbr, wout = late_after(o_fox)
    ys = []
    for a, (o, (r0, nr)) in enumerate(zip((o_sb, o_ch, o_fox), BR_ROWS)):
        ys.append(matmul(
            NN, o, wbr, _sds((t, D_MODEL), F32), (nt, 1, nr),
            pl.BlockSpec((tm, 256), lambda i, j, k: (i, k)),
            pl.BlockSpec((None, 256, D_MODEL), functools.partial(lambda i, j, k, r0: (layer, r0 + k, 0), r0=r0)),
            a_full, (tm, D_MODEL), name=f"branch{a}_{tag}"))
    merged = merge_fwd(gates, ys[0], ys[1], ys[2], tm, f"merge_{tag}")
    x_new = matmul(NN, merged, wout, _sds((t, D_MODEL), F32), (nt, 1, 1), a_full,
                   pl.BlockSpec((None, D_MODEL, D_MODEL), lambda i, j, k: (layer, 0, 0)), a_full, None,
                   name=f"wout_{tag}", res=x, res_spec=a_full)
    saved = (hm, qkv, gates, f, fcol, frow, o_sb, o_ch, o_fox, lse, ys, merged, kp, vp)
    return x_new, saved, wbr, wout


def _mixer_bwd(dy, x, gain, saved, wqkv, wf, wgate, wbr, wout, bias, layer, tm, tag, on_grads):
    t = x.shape[0]
    nt = t // tm
    hm, qkv, gates, f, fcol, frow, o_sb, o_ch, o_fox, lse, ys, merged, kp, vp = saved
    a_full = pl.BlockSpec((tm, D_MODEL), lambda i, j, k: (i, 0))
    red_row = pl.BlockSpec((tm, D_MODEL), lambda i, j, k: (k, 0))
    sq = pl.BlockSpec((D_MODEL, D_MODEL), lambda i, j, k: (0, 0))
    dmerged = matmul(NT, dy, wout, _sds((t, D_MODEL), F32), (nt, 1, 1), a_full,
                     pl.BlockSpec((None, D_MODEL, D_MODEL), lambda i, j, k: (layer, 0, 0)), a_full, None,
                     name=f"dmerged_{tag}")
    all_t = pl.BlockSpec((t, D_MODEL), lambda i, j, k: (0, 0))
    dwout = matmul(TN, merged, dy, _sds((D_MODEL, D_MODEL), BF16), (1, 1, 1), all_t, all_t, sq,
                   None, name=f"dwout_{tag}")
    dgs, dys = merge_bwd(dmerged, gates, ys[0], ys[1], ys[2], tm, f"merge_bwd_{tag}")
    dos, dwbrs = [], []
    for a, (o, (r0, nr)) in enumerate(zip((o_sb, o_ch, o_fox), BR_ROWS)):
        dos.append(matmul(
            NT, dys[a], wbr, _sds((t, nr * 256), BF16), (nt, nr, 1), a_full,
            pl.BlockSpec((None, 256, D_MODEL), functools.partial(lambda i, j, k, r0: (layer, r0 + j, 0), r0=r0)),
            pl.BlockSpec((tm, 256), lambda i, j, k: (i, j)), None, name=f"dbranch{a}_{tag}"))
        dwbrs.append(matmul(
            TN, o, dys[a], _sds((nr * 256, D_MODEL), BF16), (nr, 1, 1),
            pl.BlockSpec((t, 256), lambda i, j, k: (0, i)), all_t,
            pl.BlockSpec((256, D_MODEL), lambda i, j, k: (i, 0)), None, name=f"dwbr{a}_{tag}"))
    dq_a, dk_a, dv_a = sb_bwd(qkv, dos[0], f"sb_bwd_{tag}")
    dq_b, dk_b, dv_b, dbias = chunk_bwd(qkv, kp, vp, bias, dos[1], f"chunk_bwd_{tag}")
    dk_b, dv_b = dk_b[CH_PAD:], dv_b[CH_PAD:]
    dq_c, dk_c, dv_c, dfrow = fox_bwd(qkv, fcol, frow, o_fox, lse, dos[2], f"fox_bwd_{tag}")
    df = forget_cumsum_bwd(_frow_from_groups(dfrow), f, f"fcum_bwd_{tag}")
    dqkv = jnp.concatenate([p.astype(BF16) for p in
                            (dq_a, dk_a, dv_a, dq_b, dk_b, dv_b, dq_c, dk_c, dv_c)], axis=1)
    dgates = jnp.concatenate(dgs, axis=1)
    dtab = rel_bias_scatter(dbias, f"rel_scatter_{tag}")

    all_rows = pl.BlockSpec((t, D_MODEL), lambda i, j, k: (0, 0))
    wide_b = pl.BlockSpec((t, D_MODEL), lambda i, j, k: (0, j))
    wide_o = pl.BlockSpec((D_MODEL, D_MODEL), lambda i, j, k: (0, j))
    wide_cs = pl.BlockSpec((1, D_MODEL), lambda i, j, k: (0, j))
    dwqkv, dbq = matmul(TN, hm, dqkv, _sds((D_MODEL, QKV_WIDTH), BF16), (1, 3, 1), all_rows, wide_b,
                        wide_o, None, name=f"dwqkv_{tag}",
                        colsum_sds=_sds((1, QKV_WIDTH), F32), colsum_spec=wide_cs)
    dwgate, dbg = matmul(TN, hm, dgates, _sds((D_MODEL, 3 * D_MODEL), BF16), (1, 3, 1), all_rows,
                         wide_b, wide_o, None, name=f"dwgate_{tag}",
                         colsum_sds=_sds((1, 3 * D_MODEL), F32), colsum_spec=wide_cs)
    dwf, dbf = matmul(TN, hm, df, _sds((D_MODEL, LANES), BF16), (1, 1, 1), all_rows,
                      pl.BlockSpec((t, LANES), lambda i, j, k: (0, 0)),
                      pl.BlockSpec((D_MODEL, LANES), lambda i, j, k: (0, 0)), None,
                      name=f"dwf_{tag}", colsum_sds=_sds((1, LANES), F32),
                      colsum_spec=pl.BlockSpec((1, LANES), lambda i, j, k: (0, 0)))
    dwbr = jnp.concatenate(dwbrs, axis=0)
    deps = on_grads(dict(dwqkv=dwqkv, dwgate=dwgate, dwf=dwf, dwbr=dwbr, dwout=dwout))
    wide_a = pl.BlockSpec((tm, QKV_WIDTH), lambda i, j, k: (i, 0))
    dhm = matmul(NT, dqkv, wqkv, _sds((t, D_MODEL), F32), (nt, 1, 1), wide_a,
                 pl.BlockSpec((None, D_MODEL, QKV_WIDTH), lambda i, j, k: (layer, 0, 0)), a_full,
                 None, name=f"dhm_qkv_{tag}", deps=deps)
    dhm = matmul(NT, dgates, wgate, _sds((t, D_MODEL), F32), (nt, 1, 1), wide_a,
                 pl.BlockSpec((None, D_MODEL, QKV_WIDTH), lambda i, j, k: (layer + 1, 0, 0)), a_full,
                 None, name=f"dhm_gate_{tag}", res=dhm, res_spec=a_full)
    dhm = matmul(NT, df, wf, _sds((t, D_MODEL), F32), (nt, 1, 1),
                 pl.BlockSpec((tm, LANES), lambda i, j, k: (i, 0)),
                 pl.BlockSpec((None, D_MODEL, LANES), lambda i, j, k: (layer, 0, 0)), a_full, None,
                 name=f"dhm_f_{tag}", res=dhm, res_spec=a_full)
    dx, dgain = rmsnorm_bwd(x, gain, dhm, dy, tm, f"rms_bwd_{tag}")
    return dx, dict(dbq=dbq, dbg=dbg, dbf=dbf, dtab=dtab, dgain=dgain)


def _pack_small(pieces):
    flat = jnp.concatenate([p.reshape(-1).astype(F32) for p in pieces])
    flat = jnp.pad(flat, (0, SMALL_ROWS * LANES - flat.shape[0]))
    return flat.reshape(SMALL_ROWS, LANES)


def _unpack_small(packed, shapes):
    flat = packed.reshape(-1)
    out, pos = [], 0
    for shp in shapes:
        n = int(np.prod(shp))
        out.append(flat[pos:pos + n].reshape(shp))
        pos += n
    return out


def kernel(x, g_ffn1, w_ffn1_in, w_ffn1_out, g_mix, w_in, b_in, rel_bias, w_br_sb, w_br_ch, w_br_fox, w_out, g_ffn2, w_ffn2_in, w_ffn2_out, g_final, loss_target, m_g_ffn1, m_w_ffn1_in, m_w_ffn1_out, m_g_mix, m_w_in, m_b_in, m_rel_bias, m_w_br_sb, m_w_br_ch, m_w_br_fox, m_w_out, m_g_ffn2, m_w_ffn2_in, m_w_ffn2_out, m_g_final, v_g_ffn1, v_w_ffn1_in, v_w_ffn1_out, v_g_mix, v_w_in, v_b_in, v_rel_bias, v_w_br_sb, v_w_br_ch, v_w_br_fox, v_w_out, v_g_ffn2, v_w_ffn2_in, v_w_ffn2_out, v_g_final):
    t = x.shape[1]
    tm = min(512, t)
    xs = x[0]
    target = loss_target[0]
    f_lo, f_hi = QKV_WIDTH, QKV_WIDTH + N_HEADS_FOX

    def ffn_shards(w_in_, w_out_, l):
        return [w_in_[l:l + 1].astype(BF16), w_out_[l:l + 1].astype(BF16)]

    def mixer_shards(l):
        wl = w_in[l]
        return [jnp.stack([wl[:, :QKV_WIDTH], wl[:, f_hi:]]).astype(BF16),
                jnp.pad(wl[:, f_lo:f_hi], ((0, 0), (0, LANES - N_HEADS_FOX)))[None].astype(BF16),
                w_out[l:l + 1].astype(BF16),
                jnp.concatenate([w_br_sb[l], w_br_ch[l], w_br_fox[l]], axis=0)[None].astype(BF16)]

    gathers = {}
    for l in range(DEPTH):
        for grp, shards in (("ffn1", ffn_shards(w_ffn1_in, w_ffn1_out, l)), ("mix", mixer_shards(l)),
                            ("ffn2", ffn_shards(w_ffn2_in, w_ffn2_out, l))):
            cut = len(shards) // 2
            if l == 0 and grp != "ffn2":
                gathers[(grp, l)] = (exchange_start("gather", shards[:cut], f"gather_{grp}_l{l}_a"),
                                     exchange_start("gather", shards[cut:], f"gather_{grp}_l{l}_b"))
            else:
                gathers[(grp, l)] = (exchange_start("gather", shards, f"gather_{grp}_l{l}"),)
    gather_tokens = [h["token"] for hs in gathers.values() for h in hs]

    def gathered(key, after):
        hs = gathers[key]
        cut = hs[0]["n"]
        first = exchange_wait(hs[0], after, f"gathered_{key[0]}_l{key[1]}")
        if len(hs) == 1:
            return first[:cut // 2], lambda later: first[cut // 2:]
        return first, lambda later: exchange_wait(hs[1], later, f"gathered_{key[0]}_l{key[1]}_b")

    def ffn_weights(key, after):
        (wa_,), rest = gathered(key, after)
        return wa_, lambda later: rest(later)[0].reshape(1, 4, FF_BLK, D_MODEL)

    def mixer_weights(key, after):
        (wc_, wf_), rest = gathered(key, after)

        def late(later):
            wout_, wbr_ = rest(later)
            return (wbr_.transpose(0, 2, 1, 3).reshape(1, D_MODEL, D_MODEL), wout_.reshape(1, D_MODEL, D_MODEL))

        return wc_.reshape(2, D_MODEL, QKV_WIDTH), wf_.reshape(1, D_MODEL, LANES), late

    bq = b_in[:, None, :QKV_WIDTH]
    bf = jnp.pad(b_in[:, f_lo:f_hi], ((0, 0), (0, LANES - N_HEADS_FOX)))[:, None, :]
    bg = b_in[:, None, f_hi:]
    tab_t = jnp.pad(rel_bias.transpose(0, 2, 1), ((0, 0), (0, 0), (0, REL_PAD - N_REL)))

    h = xs
    saved = []
    weights = []
    for l in range(DEPTH):
        bias = rel_bias_build(tab_t[l], f"rel_build_l{l}").reshape(N_HEADS_CH, QB, CH_KEYS)
        x0 = h
        wa1, wb1_after = ffn_weights(("ffn1", l), x0)
        x1, s1, wb1 = _ffn_fwd(x0, g_ffn1[l:l + 1], wa1, wb1_after, 0, tm, f"ffn1_l{l}",
                               deps=gather_tokens if l == 0 else ())
        wc, wf, late_after = mixer_weights(("mix", l), x1)
        x2, sm, wbr, wout = _mixer_fwd(x1, g_mix[l:l + 1], wc, wf, wc, late_after, bq[l], bf[l], bg[l],
                                       bias, 0, tm, f"mix_l{l}")
        wa2, wb2_after = ffn_weights(("ffn2", l), x2)
        x3, s2, wb2 = _ffn_fwd(x2, g_ffn2[l:l + 1], wa2, wb2_after, 0, tm, f"ffn2_l{l}")
        saved.append((x0, x1, x2, s1, sm, s2, bias))
        weights.append(((wa1, wb1), (wc, wf, wout, wbr), (wa2, wb2)))
        h = x3

    dx, dg_final, loss_blk = loss_head(h, g_final[None, :], target, tm, "loss_head")

    g_mix_l = [None] * DEPTH
    dgains = {}
    scatters = {}

    def scatter_ffn(key):
        def on_grads(dwa, dwb):
            scatters[key] = exchange_start(
                "scatter", [dwa[None], dwb.reshape(1, N_DEV, D_FF // N_DEV, D_MODEL)],
                f"scatter_{key[0]}_l{key[1]}")
            return (scatters[key]["token"],)
        return on_grads

    def scatter_mixer(key):
        def on_grads(gm):
            scatters[key] = exchange_start(
                "scatter",
                [gm["dwqkv"].reshape(1, N_DEV, LANES, QKV_WIDTH), gm["dwgate"].reshape(1, N_DEV, LANES, QKV_WIDTH),
                 gm["dwf"].reshape(1, N_DEV, LANES, LANES), gm["dwout"].reshape(1, N_DEV, LANES, D_MODEL),
                 gm["dwbr"].reshape(1, D_MODEL, N_DEV, LANES).transpose(0, 2, 1, 3)],
                f"scatter_{key[0]}_l{key[1]}")
            return (scatters[key]["token"],)
        return on_grads

    for l in reversed(range(DEPTH)):
        x0, x1, x2, s1, sm, s2, bias = saved[l]
        w1, (wc, wf, wout, wbr), w2 = weights[l]
        dx, dgains[("ffn2", l)] = _ffn_bwd(dx, x2, g_ffn2[l:l + 1], s2, *w2, 0, tm, f"ffn2_l{l}",
                                           scatter_ffn(("ffn2", l)))
        dx, g_mix_l[l] = _mixer_bwd(dx, x1, g_mix[l:l + 1], sm, wc, wf, wc, wbr, wout, bias, 0, tm,
                                    f"mix_l{l}", scatter_mixer(("mix", l)))
        dx, dgains[("ffn1", l)] = _ffn_bwd(dx, x0, g_ffn1[l:l + 1], s1, *w1, 0, tm, f"ffn1_l{l}",
                                           scatter_ffn(("ffn1", l)))

    small_shapes = []
    small_pieces = []
    small_w, small_m, small_v = [], [], []

    def add_small(piece, w, m, v):
        small_shapes.append(w.shape)
        small_pieces.append(piece)
        small_w.append(w); small_m.append(m); small_v.append(v)

    dg1 = jnp.concatenate([dgains[("ffn1", l)] for l in range(DEPTH)], axis=0)
    dgm = jnp.concatenate([g_mix_l[l]["dgain"] for l in range(DEPTH)], axis=0)
    dg2 = jnp.concatenate([dgains[("ffn2", l)] for l in range(DEPTH)], axis=0)
    db = jnp.stack([jnp.concatenate([g_mix_l[l]["dbq"][0], g_mix_l[l]["dbf"][0, :N_HEADS_FOX],
                                     g_mix_l[l]["dbg"][0]]) for l in range(DEPTH)])
    drel = jnp.stack([g_mix_l[l]["dtab"][:, :N_REL].T for l in range(DEPTH)])
    add_small(dg1, g_ffn1, m_g_ffn1, v_g_ffn1)
    add_small(dgm, g_mix, m_g_mix, v_g_mix)
    add_small(db, b_in, m_b_in, v_b_in)
    add_small(drel, rel_bias, m_rel_bias, v_rel_bias)
    add_small(dg2, g_ffn2, m_g_ffn2, v_g_ffn2)
    add_small(dg_final[0], g_final, m_g_final, v_g_final)
    loss_piece = loss_blk[0, 0:1]
    small_packed = _pack_small(small_pieces + [loss_piece])

    recv = {}
    last = ("ffn1", 0)
    for l in reversed(range(DEPTH)):
        for grp in ("ffn2", "mix", "ffn1"):
            if (grp, l) != last:
                recv[(grp, l)] = exchange_wait(scatters[(grp, l)], dx, f"scattered_{grp}_l{l}")

    def upd(parts, w, m, v, tr, name, rb0=0):
        _, r, c = w.shape
        nr = r // tr

        def p_spec(layer):
            pinned = (nr - 1) if layer == 0 else 0
            return pl.BlockSpec((N_DEV, None, tr, c),
                                lambda l, i: (0, 0, rb0 + jnp.where(l == layer, i, pinned), 0))

        return adamw(parts, w, m, v, (DEPTH, nr), [p_spec(0), p_spec(1)],
                     pl.BlockSpec((None, tr, c), lambda l, i: (l, i, 0)), name)

    def both(grp, k):
        return [recv[(grp, l)][k] for l in range(DEPTH)]

    out_rows = D_FF // N_DEV // 2
    def upd_transposed(parts, w, m, v, tr, name):
        tp = lambda a: jnp.transpose(a, (0, 2, 1))
        return [tp(o) for o in upd(parts, tp(w), tp(m), tp(v), tr, name)]

    in_rows = FF_BLK // 4
    r_ffn2_in = upd_transposed(both("ffn2", 0), w_ffn2_in, m_w_ffn2_in, v_w_ffn2_in, in_rows, "adamw_ffn2_in")
    r_ffn2_out = upd(both("ffn2", 1), w_ffn2_out, m_w_ffn2_out, v_w_ffn2_out, out_rows, "adamw_ffn2_out")
    r_out = upd(both("mix", 3), w_out, m_w_out, v_w_out, LANES, "adamw_w_out")
    r_br_sb = upd(both("mix", 4), w_br_sb, m_w_br_sb, v_w_br_sb, 256, "adamw_br_sb", rb0=0)
    r_br_ch = upd(both("mix", 4), w_br_ch, m_w_br_ch, v_w_br_ch, 256, "adamw_br_ch", rb0=1)
    r_br_fox = upd(both("mix", 4), w_br_fox, m_w_br_fox, v_w_br_fox, 256, "adamw_br_fox", rb0=3)

    def summed(parts, name):
        _, _, r, c = parts.shape
        return sum_parts(parts, (1,), pl.BlockSpec((N_DEV, None, r, c), lambda s: (0, 0, 0, 0)),
                         pl.BlockSpec((r, c), lambda s: (0, 0)), _sds((r, c), F32), name)

    g_w_in = jnp.stack([
        jnp.concatenate([summed(recv[("mix", l)][0], f"sum_wqkv_l{l}"),
                         summed(recv[("mix", l)][2], f"sum_wf_l{l}")[:, :N_HEADS_FOX],
                         summed(recv[("mix", l)][1], f"sum_wgate_l{l}")], axis=1) for l in range(DEPTH)])
    to_cols = lambda a: jnp.transpose(a, (2, 0, 1))
    n_cols = w_in.shape[2]
    col_blk = n_cols // 4
    win_spec = pl.BlockSpec((col_blk, DEPTH, LANES), lambda i: (i, 0, 0))
    r_in = adamw([to_cols(g_w_in)[None]], to_cols(w_in), to_cols(m_w_in), to_cols(v_w_in), (4,),
                 [pl.BlockSpec((1, col_blk, DEPTH, LANES), lambda i: (0, i, 0, 0))], win_spec, "adamw_w_in")
    r_in = [jnp.transpose(o, (1, 2, 0)) for o in r_in]

    recv[last] = exchange_wait(scatters[last], r_in[1], "scattered_ffn1_l0")
    r_ffn1_in = upd_transposed(both("ffn1", 0), w_ffn1_in, m_w_ffn1_in, v_w_ffn1_in, in_rows, "adamw_ffn1_in")
    r_ffn1_out = upd(both("ffn1", 1), w_ffn1_out, m_w_ffn1_out, v_w_ffn1_out, out_rows, "adamw_ffn1_out")

    small_sum = all_reduce_small(small_packed, "allreduce_small", deps=(r_ffn1_out[1],))
    n_small = sum(int(np.prod(s)) for s in small_shapes)
    loss = small_sum.reshape(-1)[n_small]
    sm_spec = pl.BlockSpec((SMALL_ROWS, LANES), lambda i: (0, 0))
    sm_out = adamw([small_sum[None]], _pack_small(small_w), _pack_small(small_m), _pack_small(small_v),
                   (1,), [pl.BlockSpec((1, SMALL_ROWS, LANES), lambda i: (0, 0, 0))], sm_spec, "adamw_small")
    sm_g, sm_d, sm_m, sm_v = [_unpack_small(o, small_shapes) for o in sm_out]

    def per_kind(k):
        small = (sm_g, sm_d, sm_m, sm_v)[k]
        return [small[0], r_ffn1_in[k], r_ffn1_out[k], small[1], r_in[k], small[2], small[3],
                r_br_sb[k], r_br_ch[k], r_br_fox[k], r_out[k], small[4], r_ffn2_in[k], r_ffn2_out[k],
                small[5]]

    return (loss, dx[None], *per_kind(0), *per_kind(1), *per_kind(2), *per_kind(3))
```

```python
import functools

import numpy as np
import jax
import jax.numpy as jnp
from jax import lax
from jax.experimental import pallas as pl
from jax.experimental.pallas import tpu as pltpu

F32 = jnp.float32
BF16 = jnp.bfloat16

N_DEV = 8
D_MODEL = 1024
DEPTH = 2
HEAD_DIM = 64
W_SB, W_CH, W_FOX = 256, 512, 256
QKV_WIDTH = 3 * (W_SB + W_CH + W_FOX)
N_HEADS_FOX = 4
N_HEADS_CH = 8
D_FF = 2816
FF_BLK = 2 * D_FF // N_DEV
CHUNK = 64
LEFT_CHUNKS = 8
MAX_REL = 128
N_REL = 2 * MAX_REL + 1
REL_PAD = 384
QB = 128
KB = 512
KSUB = KB // QB
CH_WIN = 5
CH_KEYS = CH_WIN * QB
RMS_EPS = 1e-6
NEG = -1e30
SCALE = HEAD_DIM ** -0.5
LANES = 128
VMEM_LIMIT = 56 * 1024 * 1024

ADAM_LR, ADAM_B1, ADAM_B2, ADAM_EPS, ADAM_WD, ADAM_STEP = 0.001, 0.9, 0.999, 1e-08, 0.01, 10

SMALL_ROWS = 192

MESH = pl.DeviceIdType.MESH
ANY = pl.BlockSpec(memory_space=pl.ANY)
HIGHEST = lax.Precision.HIGHEST

NN = (((1,), (0,)), ((), ()))
NT = (((1,), (1,)), ((), ()))
TN = (((0,), (0,)), ((), ()))


def _cparams(n_grid):
    return pltpu.CompilerParams(dimension_semantics=("arbitrary",) * n_grid,
                                vmem_limit_bytes=VMEM_LIMIT)


def _sds(shape, dtype):
    return jax.ShapeDtypeStruct(tuple(shape), dtype)


def _my_index():
    return 4 * lax.axis_index("x") + 2 * lax.axis_index("y") + lax.axis_index("c")


def _peer(mask):
    x, y, c = lax.axis_index("x"), lax.axis_index("y"), lax.axis_index("c")
    px = x ^ ((mask >> 2) & 1)
    py = y ^ ((mask >> 1) & 1)
    pc = c ^ (mask & 1)
    return (px, py, pc), 4 * px + 2 * py + pc


def all_gather(shard, name):
    s, r, c = shard.shape

    def body(in_ref, out_ref, send_sems, recv_sems, local_sem):
        me = _my_index()
        mine = pltpu.make_async_copy(in_ref, out_ref.at[:, me], local_sem)
        mine.start()
        sends = []
        for mask in range(1, N_DEV):
            peer, _ = _peer(mask)
            cp = pltpu.make_async_remote_copy(
                src_ref=in_ref, dst_ref=out_ref.at[:, me],
                send_sem=send_sems.at[mask - 1], recv_sem=recv_sems.at[mask - 1],
                device_id=peer, device_id_type=MESH)
            cp.start()
            sends.append(cp)
        for mask in range(1, N_DEV):
            peer, pidx = _peer(mask)
            pltpu.make_async_remote_copy(
                src_ref=in_ref, dst_ref=out_ref.at[:, pidx],
                send_sem=send_sems.at[mask - 1], recv_sem=recv_sems.at[mask - 1],
                device_id=peer, device_id_type=MESH).wait_recv()
        for cp in sends:
            cp.wait_send()
        mine.wait()

    return pl.pallas_call(
        body, name=name,
        out_shape=_sds((s, N_DEV, r, c), shard.dtype),
        in_specs=[ANY], out_specs=ANY,
        scratch_shapes=[pltpu.SemaphoreType.DMA((N_DEV - 1,)),
                        pltpu.SemaphoreType.DMA((N_DEV - 1,)),
                        pltpu.SemaphoreType.DMA],
    )(shard)


def all_to_all(parts, name):
    s, _, r, c = parts.shape

    def body(in_ref, out_ref, send_sems, recv_sems, local_sem):
        me = _my_index()
        mine = pltpu.make_async_copy(in_ref.at[:, me], out_ref.at[me], local_sem)
        mine.start()
        sends = []
        for mask in range(1, N_DEV):
            peer, pidx = _peer(mask)
            cp = pltpu.make_async_remote_copy(
                src_ref=in_ref.at[:, pidx], dst_ref=out_ref.at[me],
                send_sem=send_sems.at[mask - 1], recv_sem=recv_sems.at[mask - 1],
                device_id=peer, device_id_type=MESH)
            cp.start()
            sends.append(cp)
        for mask in range(1, N_DEV):
            peer, pidx = _peer(mask)
            pltpu.make_async_remote_copy(
                src_ref=in_ref.at[:, me], dst_ref=out_ref.at[pidx],
                send_sem=send_sems.at[mask - 1], recv_sem=recv_sems.at[mask - 1],
                device_id=peer, device_id_type=MESH).wait_recv()
        for cp in sends:
            cp.wait_send()
        mine.wait()

    return pl.pallas_call(
        body, name=name,
        out_shape=_sds((N_DEV, s, r, c), parts.dtype),
        in_specs=[ANY], out_specs=ANY,
        scratch_shapes=[pltpu.SemaphoreType.DMA((N_DEV - 1,)),
                        pltpu.SemaphoreType.DMA((N_DEV - 1,)),
                        pltpu.SemaphoreType.DMA],
    )(parts)


HBM_SPEC = pl.BlockSpec(memory_space=pltpu.HBM)
SEM_SPEC = pl.BlockSpec(memory_space=pltpu.SEMAPHORE)
EFFECT = pltpu.SideEffectType.DATAFLOW_SIDE_EFFECTING


def _exchange_refs(mode, in_ref, land_ref, me, pidx):
    if mode == "gather":
        return in_ref, land_ref.at[:, me], land_ref.at[:, pidx]
    return in_ref.at[:, pidx], land_ref.at[me], land_ref.at[pidx]


def _landing_shape(mode, a):
    if mode == "gather":
        s, r, c = a.shape
        return (s, N_DEV, r, c)
    s, _, r, c = a.shape
    return (N_DEV, s, r, c)


def _own_copy(mode, in_ref, land_ref, me, sem):
    if mode == "gather":
        return pltpu.make_async_copy(in_ref, land_ref.at[:, me], sem)
    return pltpu.make_async_copy(in_ref.at[:, me], land_ref.at[me], sem)


def exchange_start(mode, arrays, name, deps=()):
    n = len(arrays)
    lands0 = [lax.empty(_landing_shape(mode, a), a.dtype) for a in arrays]

    def body(*refs):
        in_refs, land_refs = refs[:n], refs[n:2 * n]
        outs_at = 2 * n + len(deps)
        send_sems, recv_sems, own_sems, token = refs[outs_at], refs[outs_at + 1], refs[outs_at + 2], refs[-1]
        mine = _my_index()
        for k in range(n):
            _own_copy(mode, in_refs[k], land_refs[k], mine, own_sems.at[k]).start()
            for mask in range(1, N_DEV):
                peer, pidx = _peer(mask)
                src, dst, _ = _exchange_refs(mode, in_refs[k], land_refs[k], mine, pidx)
                sem = k * (N_DEV - 1) + mask - 1
                pltpu.make_async_remote_copy(
                    src_ref=src, dst_ref=dst, send_sem=send_sems.at[sem], recv_sem=recv_sems.at[sem],
                    device_id=peer, device_id_type=MESH).start()
        token[...] = jnp.zeros_like(token)

    nsem = n * (N_DEV - 1)
    outs = pl.pallas_call(
        body, name=name,
        out_shape=(pltpu.SemaphoreType.DMA((nsem,)), pltpu.SemaphoreType.DMA((nsem,)),
                   pltpu.SemaphoreType.DMA((n,)),
                   *[pltpu.HBM(a.shape, a.dtype) for a in arrays],
                   *[pltpu.HBM(l.shape, l.dtype) for l in lands0], _sds((8, LANES), F32)),
        in_specs=[HBM_SPEC] * (2 * n) + [ANY] * len(deps),
        out_specs=(SEM_SPEC, SEM_SPEC, SEM_SPEC, *[HBM_SPEC] * (2 * n),
                   pl.BlockSpec(memory_space=pltpu.VMEM)),
        input_output_aliases={k: 3 + k for k in range(2 * n)},
        compiler_params=pltpu.CompilerParams(has_side_effects=EFFECT),
    )(*[pltpu.with_memory_space_constraint(a, pltpu.HBM) for a in arrays],
      *[pltpu.with_memory_space_constraint(l, pltpu.HBM) for l in lands0], *deps)
    return dict(mode=mode, n=n, send=outs[0], recv=outs[1], own=outs[2], ins=outs[3:3 + n],
                lands=outs[3 + n:3 + 2 * n], token=outs[-1])


def exchange_wait(handle, after, name):
    n, mode = handle["n"], handle["mode"]

    def body(*refs):
        in_refs, land_refs = refs[:n], refs[n:2 * n]
        send_sems, recv_sems, own_sems = refs[2 * n], refs[2 * n + 1], refs[2 * n + 2]
        mine = _my_index()
        for k in range(n):
            _own_copy(mode, in_refs[k], land_refs[k], mine, own_sems.at[k]).wait()
            for mask in range(1, N_DEV):
                peer, pidx = _peer(mask)
                src, _, here = _exchange_refs(mode, in_refs[k], land_refs[k], mine, pidx)
                sem = k * (N_DEV - 1) + mask - 1
                cp = pltpu.make_async_remote_copy(
                    src_ref=src, dst_ref=here, send_sem=send_sems.at[sem], recv_sem=recv_sems.at[sem],
                    device_id=peer, device_id_type=MESH)
                cp.wait_send()
                cp.wait_recv()

    thru = (*handle["ins"], *handle["lands"])
    outs = pl.pallas_call(
        body, name=name,
        out_shape=tuple(pltpu.HBM(a.shape, a.dtype) for a in thru),
        in_specs=[HBM_SPEC] * (2 * n) + [SEM_SPEC, SEM_SPEC, SEM_SPEC, ANY],
        out_specs=tuple([HBM_SPEC] * (2 * n)),
        input_output_aliases={k: k for k in range(2 * n)},
        compiler_params=pltpu.CompilerParams(has_side_effects=EFFECT),
    )(*thru, handle["send"], handle["recv"], handle["own"], after)
    return list(outs[n:])


def all_reduce_small(packed, name, deps=()):
    rows = packed.shape[0]
    nd = len(deps)

    def body(in_ref, *rest):
        out_ref, slots, send_sems, recv_sems = rest[nd:]
        me = _my_index()
        sends = []
        for mask in range(1, N_DEV):
            peer, _ = _peer(mask)
            cp = pltpu.make_async_remote_copy(
                src_ref=in_ref, dst_ref=slots.at[me],
                send_sem=send_sems.at[mask - 1], recv_sem=recv_sems.at[mask - 1],
                device_id=peer, device_id_type=MESH)
            cp.start()
            sends.append(cp)
        slots[me] = in_ref[...]
        for mask in range(1, N_DEV):
            peer, pidx = _peer(mask)
            pltpu.make_async_remote_copy(
                src_ref=in_ref, dst_ref=slots.at[pidx],
                send_sem=send_sems.at[mask - 1], recv_sem=recv_sems.at[mask - 1],
                device_id=peer, device_id_type=MESH).wait_recv()
        for cp in sends:
            cp.wait_send()
        total = slots[0]
        for p in range(1, N_DEV):
            total = total + slots[p]
        out_ref[...] = total

    return pl.pallas_call(
        body, name=name,
        out_shape=_sds((rows, LANES), F32),
        in_specs=[pl.BlockSpec(memory_space=pltpu.VMEM)] + [ANY] * nd,
        out_specs=pl.BlockSpec(memory_space=pltpu.VMEM),
        scratch_shapes=[pltpu.VMEM((N_DEV, rows, LANES), F32),
                        pltpu.SemaphoreType.DMA((N_DEV - 1,)),
                        pltpu.SemaphoreType.DMA((N_DEV - 1,))],
    )(packed, *deps)


def matmul(dims, a, b, out_sds, grid, a_spec, b_spec, o_spec, acc_shape, *, name, alpha=1.0,
           bias=None, bias_spec=None, res=None, res_spec=None, colsum_sds=None, colsum_spec=None,
           deps=()):
    nk = grid[2]
    has_bias, has_res, has_cs = bias is not None, res is not None, colsum_sds is not None
    if has_cs:
        assert grid[0] == 1 and dims == TN

    def body(*refs):
        a_ref, b_ref = refs[0], refs[1]
        pos = 2
        bias_ref = res_ref = cs_ref = None
        if has_bias:
            bias_ref = refs[pos]; pos += 1
        if has_res:
            res_ref = refs[pos]; pos += 1
        pos += len(deps)
        o_ref = refs[pos]; pos += 1
        if has_cs:
            cs_ref = refs[pos]; pos += 1
        k = pl.program_id(2)
        bval = b_ref[...]
        part = lax.dot_general(a_ref[...].astype(BF16), bval.astype(BF16), dims,
                               preferred_element_type=F32)

        def finish(total):
            r = total * alpha if alpha != 1.0 else total
            if has_bias:
                r = r + bias_ref[...]
            if has_res:
                r = r + res_ref[...].astype(F32)
            o_ref[...] = r.astype(o_ref.dtype)

        if has_cs:
            csum = jnp.sum(bval.astype(F32), axis=0, keepdims=True)

            @pl.when(k == 0)
            def _():
                cs_ref[...] = csum

            @pl.when(k > 0)
            def _():
                cs_ref[...] += csum

        if nk == 1:
            finish(part)
        else:
            acc_ref = refs[pos]

            @pl.when(k == 0)
            def _():
                acc_ref[...] = part

            @pl.when(k > 0)
            def _():
                acc_ref[...] += part

            @pl.when(k == nk - 1)
            def _():
                finish(acc_ref[...])

    in_specs, args = [a_spec, b_spec], [a, b]
    if has_bias:
        in_specs.append(bias_spec); args.append(bias)
    if has_res:
        in_specs.append(res_spec); args.append(res)
    in_specs += [ANY] * len(deps)
    args += list(deps)
    out_shape, out_specs = [out_sds], [o_spec]
    if has_cs:
        out_shape.append(colsum_sds); out_specs.append(colsum_spec)
    scratch = [] if nk == 1 else [pltpu.VMEM(acc_shape, F32)]
    outs = pl.pallas_call(
        body, name=name, grid=grid, in_specs=in_specs, out_specs=out_specs, out_shape=out_shape,
        scratch_shapes=scratch, compiler_params=_cparams(3))(*args)
    return outs if has_cs else outs[0]


def _sigmoid(z):
    return 1.0 / (1.0 + jnp.exp(-z))


def _log_sigmoid(z):
    return jnp.minimum(z, 0.0) - jnp.log(1.0 + jnp.exp(-jnp.abs(z)))


def rmsnorm_fwd(x, gain, tm, name, deps=()):
    t, d = x.shape

    def body(x_ref, g_ref, *rest):
        o_ref = rest[-1]
        xf = x_ref[...]
        r = lax.rsqrt(jnp.mean(xf * xf, axis=-1, keepdims=True) + RMS_EPS)
        o_ref[...] = (xf * r * g_ref[...]).astype(o_ref.dtype)

    return pl.pallas_call(
        body, name=name, grid=(t // tm,),
        in_specs=[pl.BlockSpec((tm, d), lambda i: (i, 0)), pl.BlockSpec((1, d), lambda i: (0, 0))]
        + [ANY] * len(deps),
        out_specs=pl.BlockSpec((tm, d), lambda i: (i, 0)),
        out_shape=_sds((t, d), BF16), compiler_params=_cparams(1))(x, gain, *deps)


def rmsnorm_bwd(x, gain, dh, dres, tm, name):
    t, d = x.shape

    def body(x_ref, g_ref, dh_ref, dres_ref, dx_ref, dg_ref):
        i = pl.program_id(0)
        xf = x_ref[...]
        r = lax.rsqrt(jnp.mean(xf * xf, axis=-1, keepdims=True) + RMS_EPS)
        xhat = xf * r
        dh_v = dh_ref[...]
        dxhat = dh_v * g_ref[...]
        dx = r * (dxhat - xhat * jnp.mean(dxhat * xhat, axis=-1, keepdims=True))
        dx_ref[...] = dres_ref[...] + dx
        dg = jnp.sum(dh_v * xhat, axis=0, keepdims=True)

        @pl.when(i == 0)
        def _():
            dg_ref[...] = dg

        @pl.when(i > 0)
        def _():
            dg_ref[...] += dg

    row = pl.BlockSpec((tm, d), lambda i: (i, 0))
    vec = pl.BlockSpec((1, d), lambda i: (0, 0))
    return pl.pallas_call(
        body, name=name, grid=(t // tm,), in_specs=[row, vec, row, row], out_specs=[row, vec],
        out_shape=[_sds((t, d), F32), _sds((1, d), F32)], compiler_params=_cparams(1))(x, gain, dh, dres)


def loss_head(x, gain, target, tm, name):
    t, d = x.shape

    def body(x_ref, g_ref, tgt_ref, dx_ref, dg_ref, loss_ref):
        i = pl.program_id(0)
        xf = x_ref[...]
        g = g_ref[...]
        r = lax.rsqrt(jnp.mean(xf * xf, axis=-1, keepdims=True) + RMS_EPS)
        xhat = xf * r
        err = xhat * g - tgt_ref[...]
        part = 0.5 * jnp.sum(jnp.mean(err * err, axis=-1, keepdims=True))
        dy = err * (1.0 / d)
        dxhat = dy * g
        dx_ref[...] = r * (dxhat - xhat * jnp.mean(dxhat * xhat, axis=-1, keepdims=True))
        dg = jnp.sum(dy * xhat, axis=0, keepdims=True)
        lpart = jnp.full((8, LANES), part, F32)

        @pl.when(i == 0)
        def _():
            dg_ref[...] = dg
            loss_ref[...] = lpart

        @pl.when(i > 0)
        def _():
            dg_ref[...] += dg
            loss_ref[...] += lpart

    row = pl.BlockSpec((tm, d), lambda i: (i, 0))
    vec = pl.BlockSpec((1, d), lambda i: (0, 0))
    return pl.pallas_call(
        body, name=name, grid=(t // tm,), in_specs=[row, vec, row],
        out_specs=[row, vec, pl.BlockSpec((8, LANES), lambda i: (0, 0))],
        out_shape=[_sds((t, d), F32), _sds((1, d), F32), _sds((8, LANES), F32)],
        compiler_params=_cparams(1))(x, gain, target)


def ffn_in_swiglu(hn, wa, s, tm, name):
    t = hn.shape[0]

    def body(h_ref, wg_ref, wu_ref, gu_ref, act_ref):
        h = h_ref[...]
        g = jnp.dot(h, wg_ref[...], preferred_element_type=F32)
        u = jnp.dot(h, wu_ref[...], preferred_element_type=F32)
        gu_ref[0] = g.astype(gu_ref.dtype)
        gu_ref[1] = u.astype(gu_ref.dtype)
        act_ref[...] = (g * _sigmoid(g) * u).astype(act_ref.dtype)

    return pl.pallas_call(
        body, name=name, grid=(t // tm, 4),
        in_specs=[pl.BlockSpec((tm, D_MODEL), lambda i, j: (i, 0)),
                  pl.BlockSpec((None, None, D_MODEL, FF_BLK), lambda i, j: (s, j, 0, 0)),
                  pl.BlockSpec((None, None, D_MODEL, FF_BLK), lambda i, j: (s, j + 4, 0, 0))],
        out_specs=[pl.BlockSpec((None, 2, tm, FF_BLK), lambda i, j: (j, 0, i, 0)),
                   pl.BlockSpec((None, tm, FF_BLK), lambda i, j: (j, i, 0))],
        out_shape=[_sds((4, 2, t, FF_BLK), BF16), _sds((4, t, FF_BLK), BF16)],
        compiler_params=_cparams(2))(hn, wa, wa)


def ffn_dact_swiglu(dy, wb, gu, s, tm, name):
    t = dy.shape[0]

    def body(dy_ref, w_ref, gu_ref, o_ref):
        da = 0.5 * lax.dot_general(dy_ref[...].astype(BF16), w_ref[...], NT, preferred_element_type=F32)
        g = gu_ref[0].astype(F32)
        u = gu_ref[1].astype(F32)
        sg = _sigmoid(g)
        o_ref[0] = (da * u * (sg * (1.0 + g * (1.0 - sg)))).astype(o_ref.dtype)
        o_ref[1] = (da * g * sg).astype(o_ref.dtype)

    blk = pl.BlockSpec((None, 2, tm, FF_BLK), lambda i, j: (j, 0, i, 0))
    return pl.pallas_call(
        body, name=name, grid=(t // tm, 4),
        in_specs=[pl.BlockSpec((tm, D_MODEL), lambda i, j: (i, 0)),
                  pl.BlockSpec((None, None, FF_BLK, D_MODEL), lambda i, j: (s, j, 0, 0)), blk],
        out_specs=blk, out_shape=_sds((4, 2, t, FF_BLK), BF16),
        compiler_params=_cparams(2))(dy, wb, gu)


def ffn_out_residual(act, wb, x, s, tm, name):
    t = x.shape[0]

    def body(a_ref, w_ref, x_ref, o_ref):
        acc = jnp.dot(a_ref[0], w_ref[0], preferred_element_type=F32)
        for k in range(1, 4):
            acc = acc + jnp.dot(a_ref[k], w_ref[k], preferred_element_type=F32)
        o_ref[...] = x_ref[...] + 0.5 * acc

    row = pl.BlockSpec((tm, D_MODEL), lambda i: (i, 0))
    return pl.pallas_call(
        body, name=name, grid=(t // tm,),
        in_specs=[pl.BlockSpec((4, tm, FF_BLK), lambda i: (0, i, 0)),
                  pl.BlockSpec((None, 4, FF_BLK, D_MODEL), lambda i: (s, 0, 0, 0)), row],
        out_specs=row, out_shape=_sds((t, D_MODEL), F32), compiler_params=_cparams(1))(act, wb, x)


def ffn_dh(dgu, wa, s, tm, name, deps):
    t = dgu.shape[2]

    def body(g_ref, w_ref, *rest):
        o_ref = rest[-1]
        acc = lax.dot_general(g_ref[0, 0], w_ref[0], NT, preferred_element_type=F32)
        for p in range(1, N_DEV):
            acc = acc + lax.dot_general(g_ref[p % 4, p // 4], w_ref[p], NT, preferred_element_type=F32)
        o_ref[...] = acc

    return pl.pallas_call(
        body, name=name, grid=(t // tm,),
        in_specs=[pl.BlockSpec((4, 2, tm, FF_BLK), lambda i: (0, 0, i, 0)),
                  pl.BlockSpec((None, N_DEV, D_MODEL, FF_BLK), lambda i: (s, 0, 0, 0))] + [ANY] * len(deps),
        out_specs=pl.BlockSpec((tm, D_MODEL), lambda i: (i, 0)),
        out_shape=_sds((t, D_MODEL), F32), compiler_params=_cparams(1))(dgu, wa, *deps)


def merge_fwd(gates, ya, yb, yc, tm, name):
    t, d = ya.shape

    def body(ga_ref, gb_ref, gc_ref, ya_ref, yb_ref, yc_ref, o_ref):
        m = (_sigmoid(ga_ref[...]) * ya_ref[...] + _sigmoid(gb_ref[...]) * yb_ref[...]
             + _sigmoid(gc_ref[...]) * yc_ref[...])
        o_ref[...] = m.astype(o_ref.dtype)

    row = pl.BlockSpec((tm, d), lambda i: (i, 0))
    gspecs = [pl.BlockSpec((tm, d), functools.partial(lambda i, a: (i, a), a=a)) for a in range(3)]
    return pl.pallas_call(
        body, name=name, grid=(t // tm,), in_specs=gspecs + [row, row, row], out_specs=row,
        out_shape=_sds((t, d), BF16), compiler_params=_cparams(1))(gates, gates, gates, ya, yb, yc)


def merge_bwd(dm, gates, ya, yb, yc, tm, name):
    t, d = ya.shape

    def body(dm_ref, g_ref, y_ref, dg_ref, dy_ref):
        dmv = dm_ref[...]
        s = _sigmoid(g_ref[...])
        dy_ref[...] = (dmv * s).astype(dy_ref.dtype)
        dg_ref[...] = (dmv * y_ref[...] * s * (1.0 - s)).astype(dg_ref.dtype)

    outs = []
    dgs = []
    for a, y in enumerate((ya, yb, yc)):
        row = pl.BlockSpec((tm, d), lambda i: (i, 0))
        gspec = pl.BlockSpec((tm, d), functools.partial(lambda i, a: (i, a), a=a))
        dg, dy = pl.pallas_call(
            functools.partial(body), name=f"{name}_{a}", grid=(t // tm,),
            in_specs=[row, gspec, row], out_specs=[row, row],
            out_shape=[_sds((t, d), BF16), _sds((t, d), BF16)],
            compiler_params=_cparams(1))(dm, gates, y)
        dgs.append(dg)
        outs.append(dy)
    return dgs, outs


def _iota2(shape, dim):
    return lax.broadcasted_iota(jnp.int32, shape, dim)


def forget_cumsum(f, name):
    t = f.shape[0]
    nq = t // QB

    def body(f_ref, fcol_ref, frow_ref, carry):
        j = pl.program_id(0)

        @pl.when(j == 0)
        def _():
            carry[...] = jnp.zeros_like(carry)

        logf = _log_sigmoid(f_ref[...])
        tri = (_iota2((QB, QB), 1) <= _iota2((QB, QB), 0)).astype(F32)
        blk = jnp.dot(tri, logf, precision=HIGHEST, preferred_element_type=F32) + carry[...]
        carry[...] += jnp.sum(logf, axis=0, keepdims=True)
        fcol_ref[...] = blk
        frow_ref[...] = blk.T[0:8, :]

    return pl.pallas_call(
        body, name=name, grid=(nq,),
        in_specs=[pl.BlockSpec((QB, LANES), lambda j: (j, 0))],
        out_specs=[pl.BlockSpec((QB, LANES), lambda j: (j, 0)),
                   pl.BlockSpec((None, 8, QB), lambda j: (j, 0, 0))],
        out_shape=[_sds((t, LANES), F32), _sds((nq, 8, QB), F32)],
        scratch_shapes=[pltpu.VMEM((1, LANES), F32)], compiler_params=_cparams(1))(f)


def forget_cumsum_bwd(dfrow, f, name):
    t = f.shape[0]
    nq = t // QB

    def body(dfr_ref, f_ref, df_ref, carry):
        jj = pl.program_id(0)

        @pl.when(jj == 0)
        def _():
            carry[...] = jnp.zeros_like(carry)

        padded = jnp.concatenate([dfr_ref[...], jnp.zeros((QB - 8, QB), F32)], axis=0)
        dfcol = padded.T
        tri = (_iota2((QB, QB), 1) >= _iota2((QB, QB), 0)).astype(F32)
        dlogf = jnp.dot(tri, dfcol, precision=HIGHEST, preferred_element_type=F32) + carry[...]
        carry[...] += jnp.sum(dfcol, axis=0, keepdims=True)
        df_ref[...] = dlogf * _sigmoid(-f_ref[...])

    return pl.pallas_call(
        body, name=name, grid=(nq,),
        in_specs=[pl.BlockSpec((None, 8, QB), lambda jj: (nq - 1 - jj, 0, 0)),
                  pl.BlockSpec((QB, LANES), lambda jj: (nq - 1 - jj, 0))],
        out_specs=pl.BlockSpec((QB, LANES), lambda jj: (nq - 1 - jj, 0)),
        out_shape=_sds((t, LANES), F32),
        scratch_shapes=[pltpu.VMEM((1, LANES), F32)], compiler_params=_cparams(1))(dfrow, f)


REL_DIAG = 768
REL_SHIFT = REL_DIAG - (QB - 1)


def _diag_onehot():
    u = _iota2((REL_PAD, REL_DIAG), 1)
    rel = jnp.clip(CH_KEYS - 1 - u, -MAX_REL, MAX_REL) + MAX_REL
    return (_iota2((REL_PAD, REL_DIAG), 0) == rel).astype(F32)


def rel_bias_build(tab_t, name):
    def body(tab_ref, o_ref):
        diag = jnp.dot(tab_ref[...], _diag_onehot(), precision=HIGHEST, preferred_element_type=F32)
        for h in range(N_HEADS_CH):
            rows = jnp.broadcast_to(diag[h:h + 1, :], (QB, REL_DIAG))
            o_ref[h] = pltpu.roll(rows, REL_SHIFT, 1, stride=1, stride_axis=0)[:, :CH_KEYS]

    return pl.pallas_call(
        body, name=name, out_shape=_sds((N_HEADS_CH, QB, CH_KEYS), F32),
        in_specs=[pl.BlockSpec(memory_space=pltpu.VMEM)], out_specs=pl.BlockSpec(memory_space=pltpu.VMEM),
    )(tab_t)


def rel_bias_scatter(dbias, name):
    def body(db_ref, o_ref, ddiag):
        flip = (_iota2((QB, QB), 0) + _iota2((QB, QB), 1) == QB - 1).astype(F32)
        for h in range(N_HEADS_CH):
            padded = jnp.concatenate([db_ref[h], jnp.zeros((QB, REL_DIAG - CH_KEYS), F32)], axis=1)
            flipped = jnp.dot(flip, padded, precision=HIGHEST, preferred_element_type=F32)
            unrolled = pltpu.roll(flipped, 0, 1, stride=1, stride_axis=0)
            ddiag[h:h + 1, :] = jnp.sum(unrolled, axis=0, keepdims=True)
        o_ref[...] = lax.dot_general(ddiag[...], _diag_onehot(), NT, precision=HIGHEST,
                                     preferred_element_type=F32)

    return pl.pallas_call(
        body, name=name, out_shape=_sds((N_HEADS_CH, REL_PAD), F32),
        in_specs=[pl.BlockSpec(memory_space=pltpu.VMEM)], out_specs=pl.BlockSpec(memory_space=pltpu.VMEM),
        scratch_shapes=[pltpu.VMEM((N_HEADS_CH, REL_DIAG), F32)],
    )(dbias)


def _hl(h):
    return slice(h * HEAD_DIM, (h + 1) * HEAD_DIM)


def _split_dot(x, tri_bf16):
    hi = x.astype(BF16)
    lo = (x - hi.astype(F32)).astype(BF16)
    return (jnp.dot(hi, tri_bf16, preferred_element_type=F32)
            + jnp.dot(lo, tri_bf16, preferred_element_type=F32))


def _rows(j):
    return pl.ds(pl.multiple_of(j * QB, QB), QB)


def _krows(g):
    return pl.ds(pl.multiple_of(g * KB, KB), KB)


def _log_sigmoid_pair(z):
    sp = jnp.log(1.0 + jnp.exp(-jnp.abs(z)))
    return jnp.minimum(z, 0.0) - sp, -jnp.maximum(z, 0.0) - sp


def _qkv_specs(t, col0, n_pairs):
    q_spec = pl.BlockSpec((QB, LANES), lambda hp, i: (i, col0 + hp))
    k_spec = pl.BlockSpec((t, LANES), lambda hp, i: (0, col0 + n_pairs + hp))
    v_spec = pl.BlockSpec((t, LANES), lambda hp, i: (0, col0 + 2 * n_pairs + hp))
    return q_spec, k_spec, v_spec


def _keys_major(xt):
    pairs, groups, _, _ = xt.shape
    return xt.transpose(1, 3, 0, 2).reshape(groups * KB, pairs * LANES)


def sb_fwd(qkv, name):
    t = qkv.shape[0]
    nq = t // QB

    def body(q_ref, k_ref, v_ref, o_ref, w_ref):
        i = pl.program_id(1)
        groups = i // KSUB + 1
        tri_after = (_iota2((KB, KB), 0) > _iota2((KB, KB), 1)).astype(BF16)
        t_idx = i * QB + _iota2((QB, KB), 0)
        qs = [q_ref[:, _hl(h)] for h in range(2)]

        def step(gg, carry):
            g = groups - 1 - gg
            strict = (g * KB + _iota2((QB, KB), 1)) < t_idx
            out = []
            for h in range(2):
                tail, acc = carry[2 * h], carry[2 * h + 1]
                k = k_ref[_krows(g), _hl(h)]
                v = v_ref[_krows(g), _hl(h)]
                z = lax.dot_general(qs[h], k, NT, preferred_element_type=F32) * SCALE
                lb, lf = _log_sigmoid_pair(z)
                lf = jnp.where(strict, lf, 0.0)
                between = _split_dot(lf, tri_after) + tail
                w = jnp.where(strict, jnp.exp(lb + between), 0.0).astype(BF16)
                w_ref[h, g] = w
                acc = acc + jnp.dot(w, v, preferred_element_type=F32)
                out += [tail + jnp.sum(lf, axis=1, keepdims=True), acc]
            return tuple(out)

        init = (jnp.zeros((QB, 1), F32), jnp.zeros((QB, HEAD_DIM), F32)) * 2
        res = lax.fori_loop(0, groups, step, init)
        for h in range(2):
            o_ref[:, _hl(h)] = res[2 * h + 1].astype(o_ref.dtype)

    q_spec, k_spec, v_spec = _qkv_specs(t, 0, 2)
    return pl.pallas_call(
        body, name=name, grid=(2, nq), in_specs=[q_spec, k_spec, v_spec],
        out_specs=[pl.BlockSpec((QB, LANES), lambda hp, i: (i, hp)),
                   pl.BlockSpec((2, None, t // KB, QB, KB), lambda hp, i: (hp, i, 0, 0, 0))],
        out_shape=[_sds((t, W_SB), BF16), _sds((4, nq, t // KB, QB, KB), BF16)],
        compiler_params=_cparams(2))(qkv, qkv, qkv)


def _hs(h):
    return slice(h * HEAD_DIM, (h + 1) * HEAD_DIM)


def sb_bwd(qkv, qkv_t, w, do, do_t, name):
    t = qkv.shape[0]
    nq = t // QB

    def body(q_ref, k_ref, v_ref, do_ref, qt_ref, dot_ref, w_ref, dq_ref, dkt_ref, dvt_ref):
        i = pl.program_id(1)

        @pl.when(i == 0)
        def _():
            dkt_ref[...] = jnp.zeros_like(dkt_ref)
            dvt_ref[...] = jnp.zeros_like(dvt_ref)

        groups = i // KSUB + 1
        tri_before = (_iota2((KB, KB), 0) < _iota2((KB, KB), 1)).astype(BF16)
        t_idx = i * QB + _iota2((QB, KB), 0)
        qs = [q_ref[:, _hl(h)] for h in range(2)]
        dos = [do_ref[:, _hl(h)] for h in range(2)]
        qts = [qt_ref[_hs(h), :] for h in range(2)]
        dots = [dot_ref[_hs(h), :] for h in range(2)]

        def grads(g, carry):
            strict = (g * KB + _iota2((QB, KB), 1)) < t_idx
            out = []
            for h in range(2):
                head, dq = carry[2 * h], carry[2 * h + 1]
                k = k_ref[_krows(g), _hl(h)]
                v = v_ref[_krows(g), _hl(h)]
                wb = w_ref[h, g]
                z = lax.dot_general(qs[h], k, NT, preferred_element_type=F32) * SCALE
                beta = _sigmoid(z)
                e = lax.dot_general(dos[h], v, NT, preferred_element_type=F32) * wb.astype(F32)
                before = _split_dot(e, tri_before) + head
                dz = jnp.where(strict, e * (1.0 - beta) - before * beta, 0.0) * SCALE
                dzb = dz.astype(BF16)
                dq = dq + jnp.dot(dzb, k, preferred_element_type=F32)
                dkt_ref[g, _hs(h), :] += jnp.dot(qts[h], dzb, preferred_element_type=F32)
                dvt_ref[g, _hs(h), :] += jnp.dot(dots[h], wb, preferred_element_type=F32)
                out += [head + jnp.sum(e, axis=1, keepdims=True), dq]
            return tuple(out)

        init = (jnp.zeros((QB, 1), F32), jnp.zeros((QB, HEAD_DIM), F32)) * 2
        res = lax.fori_loop(0, groups, grads, init)
        for h in range(2):
            dq_ref[:, _hl(h)] = res[2 * h + 1].astype(dq_ref.dtype)

    q_spec, k_spec, v_spec = _qkv_specs(t, 0, 2)
    blk = pl.BlockSpec((QB, LANES), lambda hp, i: (i, hp))
    blk_t = pl.BlockSpec((LANES, QB), lambda hp, i: (hp, i))
    acc_t = pl.BlockSpec((None, t // KB, LANES, KB), lambda hp, i: (hp, 0, 0, 0))
    acc_sds = _sds((2, t // KB, LANES, KB), F32)
    return pl.pallas_call(
        body, name=name, grid=(2, nq),
        in_specs=[q_spec, k_spec, v_spec, blk, blk_t, blk_t,
                  pl.BlockSpec((2, None, t // KB, QB, KB), lambda hp, i: (hp, i, 0, 0, 0))],
        out_specs=[blk, acc_t, acc_t],
        out_shape=[_sds((t, W_SB), BF16), acc_sds, acc_sds],
        compiler_params=_cparams(2))(qkv, qkv, qkv, do, qkv_t, do_t, w)


def fox_fwd(qkv, fcol, frow, name):
    t = qkv.shape[0]
    nq = t // QB

    def body(q_ref, k_ref, v_ref, fc_ref, fr_ref, o_ref, lse_ref):
        hp = pl.program_id(0)
        i = pl.program_id(1)
        groups = i // KSUB + 1
        t_idx = i * QB + _iota2((QB, KB), 0)
        lane = _iota2((QB, LANES), 1)
        sub = _iota2((8, KB), 0)
        qs = [q_ref[:, _hl(h)] for h in range(2)]
        f_qs = [jnp.sum(jnp.where(lane == hp * 2 + h, fc_ref[...], 0.0), axis=1, keepdims=True)
                for h in range(2)]

        def step(g, carry):
            causal = (g * KB + _iota2((QB, KB), 1)) <= t_idx
            fr = fr_ref[g]
            out = []
            for h in range(2):
                m, l, acc = carry[3 * h:3 * h + 3]
                k = k_ref[_krows(g), _hl(h)]
                v = v_ref[_krows(g), _hl(h)]
                f_k = jnp.sum(jnp.where(sub == hp * 2 + h, fr, 0.0), axis=0, keepdims=True)
                z = lax.dot_general(qs[h], k, NT, preferred_element_type=F32) * SCALE + f_qs[h] - f_k
                z = jnp.where(causal, z, NEG)
                m_new = jnp.maximum(m, jnp.max(z, axis=1, keepdims=True))
                p = jnp.exp(z - m_new)
                corr = jnp.exp(m - m_new)
                l = l * corr + jnp.sum(p, axis=1, keepdims=True)
                acc = acc * corr + jnp.dot(p.astype(BF16), v, preferred_element_type=F32)
                out += [m_new, l, acc]
            return tuple(out)

        init = (jnp.full((QB, 1), NEG, F32), jnp.zeros((QB, 1), F32), jnp.zeros((QB, HEAD_DIM), F32)) * 2
        res = lax.fori_loop(0, groups, step, init)
        for h in range(2):
            m, l, acc = res[3 * h:3 * h + 3]
            o_ref[:, _hl(h)] = (acc / l).astype(o_ref.dtype)
            lse_ref[:, _hl(h)] = jnp.broadcast_to(m + jnp.log(l), (QB, HEAD_DIM))

    q_spec, k_spec, v_spec = _qkv_specs(t, 18, 2)
    blk = pl.BlockSpec((QB, LANES), lambda hp, i: (i, hp))
    return pl.pallas_call(
        body, name=name, grid=(2, nq),
        in_specs=[q_spec, k_spec, v_spec, pl.BlockSpec((QB, LANES), lambda hp, i: (i, 0)),
                  pl.BlockSpec((t // KB, 8, KB), lambda hp, i: (0, 0, 0))],
        out_specs=[blk, blk],
        out_shape=[_sds((t, W_FOX), BF16), _sds((t, W_FOX), F32)],
        compiler_params=_cparams(2))(qkv, qkv, qkv, fcol, frow)


def fox_bwd(qkv, qkv_t, fcol, frow, o, lse, do, do_t, name):
    t = qkv.shape[0]
    nq = t // QB

    def body(q_ref, k_ref, v_ref, fc_ref, fr_ref, o_ref, lse_ref, do_ref, qt_ref, dot_ref,
             dq_ref, dk_ref, dv_ref, dfr_ref):
        hp = pl.program_id(0)
        i = pl.program_id(1)
        qts = [qt_ref[_hs(h), :] for h in range(2)]
        dots = [dot_ref[_hs(h), :] for h in range(2)]

        @pl.when(i == 0)
        def _():
            dk_ref[...] = jnp.zeros_like(dk_ref)
            dv_ref[...] = jnp.zeros_like(dv_ref)

        @pl.when((i == 0) & (hp == 0))
        def _():
            dfr_ref[...] = jnp.zeros_like(dfr_ref)

        groups = i // KSUB + 1
        t_idx = i * QB + _iota2((QB, KB), 0)
        lane = _iota2((QB, LANES), 1)
        sub = _iota2((8, KB), 0)
        qs = [q_ref[:, _hl(h)] for h in range(2)]
        dos = [do_ref[:, _hl(h)] for h in range(2)]
        f_qs = [jnp.sum(jnp.where(lane == hp * 2 + h, fc_ref[...], 0.0), axis=1, keepdims=True)
                for h in range(2)]
        lse_qs = [lse_ref[:, h * HEAD_DIM:h * HEAD_DIM + 1] for h in range(2)]
        deltas = [jnp.sum(dos[h].astype(F32) * o_ref[:, _hl(h)].astype(F32), axis=1, keepdims=True)
                  for h in range(2)]

        def step(g, dqs):
            causal = (g * KB + _iota2((QB, KB), 1)) <= t_idx
            fr = fr_ref[g]
            out = []
            dfr = jnp.zeros((8, KB), F32)
            for h in range(2):
                k = k_ref[_krows(g), _hl(h)]
                v = v_ref[_krows(g), _hl(h)]
                f_k = jnp.sum(jnp.where(sub == hp * 2 + h, fr, 0.0), axis=0, keepdims=True)
                z = lax.dot_general(qs[h], k, NT, preferred_element_type=F32) * SCALE + f_qs[h] - f_k
                p = jnp.where(causal, jnp.exp(z - lse_qs[h]), 0.0)
                dp = lax.dot_general(dos[h], v, NT, preferred_element_type=F32)
                ds = p * (dp - deltas[h])
                dsb = (ds * SCALE).astype(BF16)
                out.append(dqs[h] + jnp.dot(dsb, k, preferred_element_type=F32))
                dk_ref[g, _hs(h), :] += jnp.dot(qts[h], dsb, preferred_element_type=F32)
                dv_ref[g, _hs(h), :] += jnp.dot(dots[h], p.astype(BF16), preferred_element_type=F32)
                colsum = jnp.sum(ds, axis=0, keepdims=True)
                dfr = dfr + jnp.where(sub == hp * 2 + h, -colsum, 0.0)
            dfr_ref[g] += dfr
            return tuple(out)

        res = lax.fori_loop(0, groups, step, (jnp.zeros((QB, HEAD_DIM), F32),) * 2)
        for h in range(2):
            dq_ref[:, _hl(h)] = res[h].astype(dq_ref.dtype)

    q_spec, k_spec, v_spec = _qkv_specs(t, 18, 2)
    blk = pl.BlockSpec((QB, LANES), lambda hp, i: (i, hp))
    frs = pl.BlockSpec((t // KB, 8, KB), lambda hp, i: (0, 0, 0))
    acc_t = pl.BlockSpec((None, t // KB, LANES, KB), lambda hp, i: (hp, 0, 0, 0))
    acc_sds = _sds((2, t // KB, LANES, KB), F32)
    return pl.pallas_call(
        body, name=name, grid=(2, nq),
        in_specs=[q_spec, k_spec, v_spec, pl.BlockSpec((QB, LANES), lambda hp, i: (i, 0)), frs,
                  blk, blk, blk, pl.BlockSpec((LANES, QB), lambda hp, i: (18 + hp, i)),
                  pl.BlockSpec((LANES, QB), lambda hp, i: (hp, i))],
        out_specs=[blk, acc_t, acc_t, frs],
        out_shape=[_sds((t, W_FOX), BF16), acc_sds, acc_sds, _sds((t // KB, 8, KB), F32)],
        compiler_params=_cparams(2))(qkv, qkv, qkv, fcol, frow, o, lse, do, qkv_t, do_t)


def _frow_to_groups(frow):
    n = frow.shape[0] // KSUB
    return frow.reshape(n, KSUB, 8, QB).transpose(0, 2, 1, 3).reshape(n, 8, KB)


def _frow_from_groups(frow):
    n = frow.shape[0]
    return frow.reshape(n, 8, KSUB, QB).transpose(0, 2, 1, 3).reshape(n * KSUB, 8, QB)


def _chunk_valid(i):
    qi = _iota2((QB, CH_KEYS), 0)
    kj = _iota2((QB, CH_KEYS), 1)
    dchunk = (qi >> 6) + LEFT_CHUNKS - (kj >> 6)
    return (dchunk >= 0) & (dchunk <= LEFT_CHUNKS) & ((i - (CH_WIN - 1)) * QB + kj >= 0)


CH_PAD = (CH_WIN - 1) * QB


def _window(i):
    return pl.ds(pl.multiple_of(i * QB, QB), CH_KEYS)


def _chunk_probs(q, kw, bias, valid):
    z = lax.dot_general(q, kw, NT, preferred_element_type=F32) * SCALE + bias
    z = jnp.where(valid, z, NEG)
    z = z - jnp.max(z, axis=1, keepdims=True)
    p = jnp.exp(z)
    return p / jnp.sum(p, axis=1, keepdims=True)


def _chunk_specs(t):
    q_spec = pl.BlockSpec((QB, LANES), lambda hp, i: (i, 6 + hp))
    kv_spec = pl.BlockSpec((t + CH_PAD, LANES), lambda hp, i: (0, hp))
    return q_spec, kv_spec


def chunk_fwd(qkv, kp, vp, bias, name):
    t = qkv.shape[0]
    nq = t // QB

    def body(q_ref, k_ref, v_ref, b_ref, o_ref):
        i = pl.program_id(1)
        valid = _chunk_valid(i)
        for h in range(2):
            p = _chunk_probs(q_ref[:, _hl(h)], k_ref[_window(i), _hl(h)], b_ref[h], valid)
            o_ref[:, _hl(h)] = jnp.dot(p.astype(BF16), v_ref[_window(i), _hl(h)],
                                       preferred_element_type=F32).astype(o_ref.dtype)

    q_spec, kv_spec = _chunk_specs(t)
    return pl.pallas_call(
        body, name=name, grid=(4, nq),
        in_specs=[q_spec, kv_spec, kv_spec, pl.BlockSpec((2, QB, CH_KEYS), lambda hp, i: (hp, 0, 0))],
        out_specs=pl.BlockSpec((QB, LANES), lambda hp, i: (i, hp)),
        out_shape=_sds((t, W_CH), BF16), compiler_params=_cparams(2))(qkv, kp, vp, bias)


def chunk_bwd(qkv, qkv_t, kp, vp, bias, do, do_t, name):
    t = qkv.shape[0]
    nq = t // QB

    def body(q_ref, k_ref, v_ref, b_ref, do_ref, qt_ref, dot_ref, dq_ref, dk_ref, dv_ref, db_ref):
        i = pl.program_id(1)

        @pl.when(i == 0)
        def _():
            dk_ref[...] = jnp.zeros_like(dk_ref)
            dv_ref[...] = jnp.zeros_like(dv_ref)
            db_ref[...] = jnp.zeros_like(db_ref)

        valid = _chunk_valid(i)
        for h in range(2):
            q = q_ref[:, _hl(h)]
            dov = do_ref[:, _hl(h)]
            kw = k_ref[_window(i), _hl(h)]
            p = _chunk_probs(q, kw, b_ref[h], valid)
            dp = lax.dot_general(dov, v_ref[_window(i), _hl(h)], NT, preferred_element_type=F32)
            ds = p * (dp - jnp.sum(p * dp, axis=1, keepdims=True))
            db_ref[h] += ds
            dsb = (ds * SCALE).astype(BF16)
            dq_ref[:, _hl(h)] = jnp.dot(dsb, kw, preferred_element_type=F32).astype(dq_ref.dtype)
            dkt = jnp.dot(qt_ref[_hs(h), :], dsb, preferred_element_type=F32)
            dvt = jnp.dot(dot_ref[_hs(h), :], p.astype(BF16), preferred_element_type=F32)
            for b in range(CH_WIN):
                dk_ref[i + b, _hs(h), :] += dkt[:, b * QB:(b + 1) * QB]
                dv_ref[i + b, _hs(h), :] += dvt[:, b * QB:(b + 1) * QB]

    q_spec, kv_spec = _chunk_specs(t)
    blk = pl.BlockSpec((QB, LANES), lambda hp, i: (i, hp))
    bspec = pl.BlockSpec((2, QB, CH_KEYS), lambda hp, i: (hp, 0, 0))
    nblk = nq + CH_WIN - 1
    acc_t = pl.BlockSpec((None, nblk, LANES, QB), lambda hp, i: (hp, 0, 0, 0))
    acc_sds = _sds((4, nblk, LANES, QB), F32)
    return pl.pallas_call(
        body, name=name, grid=(4, nq),
        in_specs=[q_spec, kv_spec, kv_spec, bspec, blk,
                  pl.BlockSpec((LANES, QB), lambda hp, i: (6 + hp, i)),
                  pl.BlockSpec((LANES, QB), lambda hp, i: (hp, i))],
        out_specs=[blk, acc_t, acc_t, bspec],
        out_shape=[_sds((t, W_CH), BF16), acc_sds, acc_sds, _sds((N_HEADS_CH, QB, CH_KEYS), F32)],
        compiler_params=_cparams(2))(qkv, kp, vp, bias, do, qkv_t, do_t)


def _sum_parts(p_ref):
    total = p_ref[0].astype(F32)
    for p in range(1, p_ref.shape[0]):
        total = total + p_ref[p].astype(F32)
    return total


def sum_parts(parts, grid, p_spec, o_spec, out_sds, name):
    def body(p_ref, o_ref):
        o_ref[...] = _sum_parts(p_ref)

    return pl.pallas_call(body, name=name, grid=grid, in_specs=[p_spec], out_specs=o_spec,
                          out_shape=out_sds, compiler_params=_cparams(len(grid)))(parts)


def adamw(parts, w, m, v, grid, p_specs, w_spec, name):
    c1 = 1.0 / (1.0 - ADAM_B1 ** ADAM_STEP)
    c2 = 1.0 / (1.0 - ADAM_B2 ** ADAM_STEP)
    n = len(parts)

    def body(*refs):
        w_ref, m_ref, v_ref, g_out, d_out, m_out, v_out = refs[n:]
        g = _sum_parts(refs[0])
        for q in range(1, n):
            g = jnp.where(pl.program_id(0) == q, _sum_parts(refs[q]), g)
        m_new = ADAM_B1 * m_ref[...] + (1.0 - ADAM_B1) * g
        v_new = ADAM_B2 * v_ref[...] + (1.0 - ADAM_B2) * (g * g)
        m_hat = m_new * c1
        v_hat = v_new * c2
        g_out[...] = g
        d_out[...] = -ADAM_LR * (m_hat / (jnp.sqrt(v_hat) + ADAM_EPS) + ADAM_WD * w_ref[...])
        m_out[...] = m_new
        v_out[...] = v_new

    out = _sds(w.shape, F32)
    return pl.pallas_call(
        body, name=name, grid=grid, in_specs=[*p_specs, w_spec, w_spec, w_spec],
        out_specs=[w_spec] * 4, out_shape=[out] * 4,
        compiler_params=_cparams(len(grid)))(*parts, w, m, v)


def _ffn_fwd(x, gain, wa, wb_after, s, tm, tag, deps=()):
    t = x.shape[0]
    hn = rmsnorm_fwd(x, gain, tm, f"rms_{tag}", deps)
    gu, act = ffn_in_swiglu(hn, wa, s, min(2 * tm, t), f"ffn_in_{tag}")
    wb = wb_after(act)
    y = ffn_out_residual(act, wb, x, s, min(2 * tm, t), f"ffn_out_{tag}")
    return y, (hn, gu, act), wb


def _ffn_bwd(dy, x, gain, saved, wa, wb, s, tm, tag, on_grads):
    t = x.shape[0]
    hn, gu, act = saved
    dgu = ffn_dact_swiglu(dy, wb, gu, s, min(2 * tm, t), f"ffn_dact_{tag}")
    dwb = matmul(TN, act, dy, _sds((4, FF_BLK, D_MODEL), BF16), (4, 1, 1),
                 pl.BlockSpec((None, t, FF_BLK), lambda i, j, k: (i, 0, 0)),
                 pl.BlockSpec((t, D_MODEL), lambda i, j, k: (0, 0)),
                 pl.BlockSpec((None, FF_BLK, D_MODEL), lambda i, j, k: (i, 0, 0)),
                 None, name=f"ffn_dwout_{tag}", alpha=0.5)
    dwa = matmul(TN, dgu, hn, _sds((8, FF_BLK, D_MODEL), BF16), (1, 8, 1),
                 pl.BlockSpec((None, None, t, FF_BLK), lambda i, j, k: (j % 4, j // 4, 0, 0)),
                 pl.BlockSpec((t, D_MODEL), lambda i, j, k: (0, 0)),
                 pl.BlockSpec((None, FF_BLK, D_MODEL), lambda i, j, k: (j, 0, 0)),
                 None, name=f"ffn_dwin_{tag}")
    deps = on_grads(dwa, dwb)
    dhn = ffn_dh(dgu, wa, s, tm, f"ffn_dh_{tag}", deps)
    dx, dgain = rmsnorm_bwd(x, gain, dhn, dy, tm, f"rms_bwd_{tag}")
    return dx, dgain


BR_ROWS = ((0, 1), (1, 2), (3, 1))


def _mixer_fwd(x, gain, wqkv, wf, wgate, late_after, bq, bf, bg, bias, layer, tm, tag):
    t = x.shape[0]
    nt = t // tm
    hm = rmsnorm_fwd(x, gain, tm, f"rms_{tag}")
    a_full = pl.BlockSpec((tm, D_MODEL), lambda i, j, k: (i, 0))
    wide_out = pl.BlockSpec((tm, D_MODEL), lambda i, j, k: (i, j))
    wide_b = pl.BlockSpec((1, D_MODEL), lambda i, j, k: (0, j))
    qkv = matmul(NN, hm, wqkv, _sds((t, QKV_WIDTH), BF16), (nt, 3, 1), a_full,
                 pl.BlockSpec((None, D_MODEL, D_MODEL), lambda i, j, k: (layer, 0, j)), wide_out, None,
                 name=f"proj_qkv_{tag}", bias=bq, bias_spec=wide_b)
    gates = matmul(NN, hm, wgate, _sds((t, 3 * D_MODEL), F32), (nt, 3, 1), a_full,
                   pl.BlockSpec((None, D_MODEL, D_MODEL), lambda i, j, k: (layer + 1, 0,j)), wide_out,
                   None, name=f"proj_gate_{tag}", bias=bg, bias_spec=wide_b)
    f = matmul(NN, hm, wf, _sds((t, LANES), F32), (nt, 1, 1), a_full,
               pl.BlockSpec((None, D_MODEL, LANES), lambda i, j, k: (layer, 0, 0)),
               pl.BlockSpec((tm, LANES), lambda i, j, k: (i, 0)), None,
               name=f"proj_f_{tag}", bias=bf, bias_spec=pl.BlockSpec((1, LANES), lambda i, j, k: (0, 0)))
    fcol, frow = forget_cumsum(f, f"fcum_{tag}")
    frow = _frow_to_groups(frow)
    o_sb, w_sb = sb_fwd(qkv, f"sb_fwd_{tag}")
    kp = jnp.pad(qkv[:, 10 * LANES:14 * LANES], ((CH_PAD, 0), (0, 0)))
    vp = jnp.pad(qkv[:, 14 * LANES:18 * LANES], ((CH_PAD, 0), (0, 0)))
    o_ch = chunk_fwd(qkv, kp, vp, bias, f"chunk_fwd_{tag}")
    o_fox, lse = fox_fwd(qkv, fcol, frow, f"fox_fwd_{tag}")
    wbr, wout = late_after(o_fox)
    ys = []
    for a, (o, (r0, nr)) in enumerate(zip((o_sb, o_ch, o_fox), BR_ROWS)):
        ys.append(matmul(
            NN, o, wbr, _sds((t, D_MODEL), F32), (nt, 1, nr),
            pl.BlockSpec((tm, 256), lambda i, j, k: (i, k)),
            pl.BlockSpec((None, 256, D_MODEL), functools.partial(lambda i, j, k, r0: (layer, r0 + k, 0), r0=r0)),
            a_full, (tm, D_MODEL), name=f"branch{a}_{tag}"))
    merged = merge_fwd(gates, ys[0], ys[1], ys[2], tm, f"merge_{tag}")
    x_new = matmul(NN, merged, wout, _sds((t, D_MODEL), F32), (nt, 1, 1), a_full,
                   pl.BlockSpec((None, D_MODEL, D_MODEL), lambda i, j, k: (layer, 0, 0)), a_full, None,
                   name=f"wout_{tag}", res=x, res_spec=a_full)
    saved = (hm, qkv, gates, f, fcol, frow, o_sb, o_ch, o_fox, lse, ys, merged, kp, vp, w_sb)
    return x_new, saved, wbr, wout


def _mixer_bwd(dy, x, gain, saved, wqkv, wf, wgate, wbr, wout, bias, layer, tm, tag, on_grads):
    t = x.shape[0]
    nt = t // tm
    hm, qkv, gates, f, fcol, frow, o_sb, o_ch, o_fox, lse, ys, merged, kp, vp, w_sb = saved
    a_full = pl.BlockSpec((tm, D_MODEL), lambda i, j, k: (i, 0))
    red_row = pl.BlockSpec((tm, D_MODEL), lambda i, j, k: (k, 0))
    sq = pl.BlockSpec((D_MODEL, D_MODEL), lambda i, j, k: (0, 0))
    dmerged = matmul(NT, dy, wout, _sds((t, D_MODEL), F32), (nt, 1, 1), a_full,
                     pl.BlockSpec((None, D_MODEL, D_MODEL), lambda i, j, k: (layer, 0, 0)), a_full, None,
                     name=f"dmerged_{tag}")
    all_t = pl.BlockSpec((t, D_MODEL), lambda i, j, k: (0, 0))
    dwout = matmul(TN, merged, dy, _sds((D_MODEL, D_MODEL), BF16), (1, 1, 1), all_t, all_t, sq,
                   None, name=f"dwout_{tag}")
    dgs, dys = merge_bwd(dmerged, gates, ys[0], ys[1], ys[2], tm, f"merge_bwd_{tag}")
    dos, dwbrs = [], []
    for a, (o, (r0, nr)) in enumerate(zip((o_sb, o_ch, o_fox), BR_ROWS)):
        dos.append(matmul(
            NT, dys[a], wbr, _sds((t, nr * 256), BF16), (nt, nr, 1), a_full,
            pl.BlockSpec((None, 256, D_MODEL), functools.partial(lambda i, j, k, r0: (layer, r0 + j, 0), r0=r0)),
            pl.BlockSpec((tm, 256), lambda i, j, k: (i, j)), None, name=f"dbranch{a}_{tag}"))
        dwbrs.append(matmul(
            TN, o, dys[a], _sds((nr * 256, D_MODEL), BF16), (nr, 1, 1),
            pl.BlockSpec((t, 256), lambda i, j, k: (0, i)), all_t,
            pl.BlockSpec((256, D_MODEL), lambda i, j, k: (i, 0)), None, name=f"dwbr{a}_{tag}"))
    qkv_t = qkv.T
    dq_a, dk_a, dv_a = sb_bwd(qkv, qkv_t, w_sb, dos[0], dos[0].T, f"sb_bwd_{tag}")
    dk_a, dv_a = _keys_major(dk_a), _keys_major(dv_a)
    dq_b, dk_b, dv_b, dbias = chunk_bwd(qkv, qkv_t, kp, vp, bias, dos[1], dos[1].T, f"chunk_bwd_{tag}")
    dk_b, dv_b = [x[:, CH_WIN - 1:].transpose(1, 3, 0, 2).reshape(t, W_CH) for x in (dk_b, dv_b)]
    dq_c, dk_c, dv_c, dfrow = fox_bwd(qkv, qkv_t, fcol, frow, o_fox, lse, dos[2], dos[2].T, f"fox_bwd_{tag}")
    dk_c, dv_c = _keys_major(dk_c), _keys_major(dv_c)
    df = forget_cumsum_bwd(_frow_from_groups(dfrow), f, f"fcum_bwd_{tag}")
    dqkv = jnp.concatenate([p.astype(BF16) for p in
                            (dq_a, dk_a, dv_a, dq_b, dk_b, dv_b, dq_c, dk_c, dv_c)], axis=1)
    dgates = jnp.concatenate(dgs, axis=1)
    dtab = rel_bias_scatter(dbias, f"rel_scatter_{tag}")

    all_rows = pl.BlockSpec((t, D_MODEL), lambda i, j, k: (0, 0))
    wide_b = pl.BlockSpec((t, D_MODEL), lambda i, j, k: (0, j))
    wide_o = pl.BlockSpec((D_MODEL, D_MODEL), lambda i, j, k: (0, j))
    wide_cs = pl.BlockSpec((1, D_MODEL), lambda i, j, k: (0, j))
    dwqkv, dbq = matmul(TN, hm, dqkv, _sds((D_MODEL, QKV_WIDTH), BF16), (1, 3, 1), all_rows, wide_b,
                        wide_o, None, name=f"dwqkv_{tag}",
                        colsum_sds=_sds((1, QKV_WIDTH), F32), colsum_spec=wide_cs)
    dwgate, dbg = matmul(TN, hm, dgates, _sds((D_MODEL, 3 * D_MODEL), BF16), (1, 3, 1), all_rows,
                         wide_b, wide_o, None, name=f"dwgate_{tag}",
                         colsum_sds=_sds((1, 3 * D_MODEL), F32), colsum_spec=wide_cs)
    dwf, dbf = matmul(TN, hm, df, _sds((D_MODEL, LANES), BF16), (1, 1, 1), all_rows,
                      pl.BlockSpec((t, LANES), lambda i, j, k: (0, 0)),
                      pl.BlockSpec((D_MODEL, LANES), lambda i, j, k: (0, 0)), None,
                      name=f"dwf_{tag}", colsum_sds=_sds((1, LANES), F32),
                      colsum_spec=pl.BlockSpec((1, LANES), lambda i, j, k: (0, 0)))
    dwbr = jnp.concatenate(dwbrs, axis=0)
    deps = on_grads(dict(dwqkv=dwqkv, dwgate=dwgate, dwf=dwf, dwbr=dwbr, dwout=dwout))
    wide_a = pl.BlockSpec((tm, QKV_WIDTH), lambda i, j, k: (i, 0))
    dhm = matmul(NT, dqkv, wqkv, _sds((t, D_MODEL), F32), (nt, 1, 1), wide_a,
                 pl.BlockSpec((None, D_MODEL, QKV_WIDTH), lambda i, j, k: (layer, 0, 0)), a_full,
                 None, name=f"dhm_qkv_{tag}", deps=deps)
    dhm = matmul(NT, dgates, wgate, _sds((t, D_MODEL), F32), (nt, 1, 1), wide_a,
                 pl.BlockSpec((None, D_MODEL, QKV_WIDTH), lambda i, j, k: (layer + 1, 0, 0)), a_full,
                 None, name=f"dhm_gate_{tag}", res=dhm, res_spec=a_full)
    dhm = matmul(NT, df, wf, _sds((t, D_MODEL), F32), (nt, 1, 1),
                 pl.BlockSpec((tm, LANES), lambda i, j, k: (i, 0)),
                 pl.BlockSpec((None, D_MODEL, LANES), lambda i, j, k: (layer, 0, 0)), a_full, None,
                 name=f"dhm_f_{tag}", res=dhm, res_spec=a_full)
    dx, dgain = rmsnorm_bwd(x, gain, dhm, dy, tm, f"rms_bwd_{tag}")
    return dx, dict(dbq=dbq, dbg=dbg, dbf=dbf, dtab=dtab, dgain=dgain)


def _pack_small(pieces):
    flat = jnp.concatenate([p.reshape(-1).astype(F32) for p in pieces])
    flat = jnp.pad(flat, (0, SMALL_ROWS * LANES - flat.shape[0]))
    return flat.reshape(SMALL_ROWS, LANES)


def _unpack_small(packed, shapes):
    flat = packed.reshape(-1)
    out, pos = [], 0
    for shp in shapes:
        n = int(np.prod(shp))
        out.append(flat[pos:pos + n].reshape(shp))
        pos += n
    return out


def kernel(x, g_ffn1, w_ffn1_in, w_ffn1_out, g_mix, w_in, b_in, rel_bias, w_br_sb, w_br_ch, w_br_fox, w_out, g_ffn2, w_ffn2_in, w_ffn2_out, g_final, loss_target, m_g_ffn1, m_w_ffn1_in, m_w_ffn1_out, m_g_mix, m_w_in, m_b_in, m_rel_bias, m_w_br_sb, m_w_br_ch, m_w_br_fox, m_w_out, m_g_ffn2, m_w_ffn2_in, m_w_ffn2_out, m_g_final, v_g_ffn1, v_w_ffn1_in, v_w_ffn1_out, v_g_mix, v_w_in, v_b_in, v_rel_bias, v_w_br_sb, v_w_br_ch, v_w_br_fox, v_w_out, v_g_ffn2, v_w_ffn2_in, v_w_ffn2_out, v_g_final):
    t = x.shape[1]
    tm = min(512, t)
    xs = x[0]
    target = loss_target[0]
    f_lo, f_hi = QKV_WIDTH, QKV_WIDTH + N_HEADS_FOX

    def ffn_shards(w_in_, w_out_, l):
        return [w_in_[l:l + 1].astype(BF16), w_out_[l:l + 1].astype(BF16)]

    def mixer_shards(l):
        wl = w_in[l]
        return [jnp.stack([wl[:, :QKV_WIDTH], wl[:, f_hi:]]).astype(BF16),
                jnp.pad(wl[:, f_lo:f_hi], ((0, 0), (0, LANES - N_HEADS_FOX)))[None].astype(BF16),
                w_out[l:l + 1].astype(BF16),
                jnp.concatenate([w_br_sb[l], w_br_ch[l], w_br_fox[l]], axis=0)[None].astype(BF16)]

    gathers = {}
    gather_tokens = []

    def start_gather(shards, name):
        handle = exchange_start("gather", shards, name, deps=gather_tokens[-1:])
        gather_tokens.append(handle["token"])
        return handle

    for l in range(DEPTH):
        for grp, shards in (("ffn1", ffn_shards(w_ffn1_in, w_ffn1_out, l)), ("mix", mixer_shards(l)),
                            ("ffn2", ffn_shards(w_ffn2_in, w_ffn2_out, l))):
            cut = len(shards) // 2
            if l == 0 and grp != "ffn2":
                gathers[(grp, l)] = (start_gather(shards[:cut], f"gather_{grp}_l{l}_a"),
                                     start_gather(shards[cut:], f"gather_{grp}_l{l}_b"))
            else:
                gathers[(grp, l)] = (start_gather(shards, f"gather_{grp}_l{l}"),)

    def gathered(key, after):
        hs = gathers[key]
        cut = hs[0]["n"]
        first = exchange_wait(hs[0], after, f"gathered_{key[0]}_l{key[1]}")
        if len(hs) == 1:
            return first[:cut // 2], lambda later: first[cut // 2:]
        return first, lambda later: exchange_wait(hs[1], later, f"gathered_{key[0]}_l{key[1]}_b")

    def ffn_weights(key, after):
        (wa_,), rest = gathered(key, after)
        return wa_, lambda later: rest(later)[0].reshape(1, 4, FF_BLK, D_MODEL)

    def mixer_weights(key, after):
        (wc_, wf_), rest = gathered(key, after)

        def late(later):
            wout_, wbr_ = rest(later)
            return (wbr_.transpose(0, 2, 1, 3).reshape(1, D_MODEL, D_MODEL), wout_.reshape(1, D_MODEL, D_MODEL))

        return wc_.reshape(2, D_MODEL, QKV_WIDTH), wf_.reshape(1, D_MODEL, LANES), late

    bq = b_in[:, None, :QKV_WIDTH]
    bf = jnp.pad(b_in[:, f_lo:f_hi], ((0, 0), (0, LANES - N_HEADS_FOX)))[:, None, :]
    bg = b_in[:, None, f_hi:]
    tab_t = jnp.pad(rel_bias.transpose(0, 2, 1), ((0, 0), (0, 0), (0, REL_PAD - N_REL)))

    h = xs
    saved = []
    weights = []
    for l in range(DEPTH):
        bias = rel_bias_build(tab_t[l], f"rel_build_l{l}").reshape(N_HEADS_CH, QB, CH_KEYS)
        x0 = h
        wa1, wb1_after = ffn_weights(("ffn1", l), x0)
        x1, s1, wb1 = _ffn_fwd(x0, g_ffn1[l:l + 1], wa1, wb1_after, 0, tm, f"ffn1_l{l}",
                               deps=gather_tokens if l == 0 else ())
        wc, wf, late_after = mixer_weights(("mix", l), x1)
        x2, sm, wbr, wout = _mixer_fwd(x1, g_mix[l:l + 1], wc, wf, wc, late_after, bq[l], bf[l], bg[l],
                                       bias, 0, tm, f"mix_l{l}")
        wa2, wb2_after = ffn_weights(("ffn2", l), x2)
        x3, s2, wb2 = _ffn_fwd(x2, g_ffn2[l:l + 1], wa2, wb2_after, 0, tm, f"ffn2_l{l}")
        saved.append((x0, x1, x2, s1, sm, s2, bias))
        weights.append(((wa1, wb1), (wc, wf, wout, wbr), (wa2, wb2)))
        h = x3

    dx, dg_final, loss_blk = loss_head(h, g_final[None, :], target, tm, "loss_head")

    g_mix_l = [None] * DEPTH
    dgains = {}
    scatters = {}

    def scatter_ffn(key):
        def on_grads(dwa, dwb):
            scatters[key] = exchange_start(
                "scatter", [dwa[None], dwb.reshape(1, N_DEV, D_FF // N_DEV, D_MODEL)],
                f"scatter_{key[0]}_l{key[1]}")
            return (scatters[key]["token"],)
        return on_grads

    def scatter_mixer(key):
        def on_grads(gm):
            scatters[key] = exchange_start(
                "scatter",
                [gm["dwqkv"].reshape(1, N_DEV, LANES, QKV_WIDTH), gm["dwgate"].reshape(1, N_DEV, LANES, QKV_WIDTH),
                 gm["dwf"].reshape(1, N_DEV, LANES, LANES), gm["dwout"].reshape(1, N_DEV, LANES, D_MODEL),
                 gm["dwbr"].reshape(1, D_MODEL, N_DEV, LANES).transpose(0, 2, 1, 3)],
                f"scatter_{key[0]}_l{key[1]}")
            return (scatters[key]["token"],)
        return on_grads

    for l in reversed(range(DEPTH)):
        x0, x1, x2, s1, sm, s2, bias = saved[l]
        w1, (wc, wf, wout, wbr), w2 = weights[l]
        dx, dgains[("ffn2", l)] = _ffn_bwd(dx, x2, g_ffn2[l:l + 1], s2, *w2, 0, tm, f"ffn2_l{l}",
                                           scatter_ffn(("ffn2", l)))
        dx, g_mix_l[l] = _mixer_bwd(dx, x1, g_mix[l:l + 1], sm, wc, wf, wc, wbr, wout, bias, 0, tm,
                                    f"mix_l{l}", scatter_mixer(("mix", l)))
        dx, dgains[("ffn1", l)] = _ffn_bwd(dx, x0, g_ffn1[l:l + 1], s1, *w1, 0, tm, f"ffn1_l{l}",
                                           scatter_ffn(("ffn1", l)))

    small_shapes = []
    small_pieces = []
    small_w, small_m, small_v = [], [], []

    def add_small(piece, w, m, v):
        small_shapes.append(w.shape)
        small_pieces.append(piece)
        small_w.append(w); small_m.append(m); small_v.append(v)

    dg1 = jnp.concatenate([dgains[("ffn1", l)] for l in range(DEPTH)], axis=0)
    dgm = jnp.concatenate([g_mix_l[l]["dgain"] for l in range(DEPTH)], axis=0)
    dg2 = jnp.concatenate([dgains[("ffn2", l)] for l in range(DEPTH)], axis=0)
    db = jnp.stack([jnp.concatenate([g_mix_l[l]["dbq"][0], g_mix_l[l]["dbf"][0, :N_HEADS_FOX],
                                     g_mix_l[l]["dbg"][0]]) for l in range(DEPTH)])
    drel = jnp.stack([g_mix_l[l]["dtab"][:, :N_REL].T for l in range(DEPTH)])
    add_small(dg1, g_ffn1, m_g_ffn1, v_g_ffn1)
    add_small(dgm, g_mix, m_g_mix, v_g_mix)
    add_small(db, b_in, m_b_in, v_b_in)
    add_small(drel, rel_bias, m_rel_bias, v_rel_bias)
    add_small(dg2, g_ffn2, m_g_ffn2, v_g_ffn2)
    add_small(dg_final[0], g_final, m_g_final, v_g_final)
    loss_piece = loss_blk[0, 0:1]
    small_packed = _pack_small(small_pieces + [loss_piece])

    recv = {}
    last = ("ffn1", 0)
    for l in reversed(range(DEPTH)):
        for grp in ("ffn2", "mix", "ffn1"):
            if (grp, l) != last:
                recv[(grp, l)] = exchange_wait(scatters[(grp, l)], dx, f"scattered_{grp}_l{l}")

    def upd(parts, w, m, v, tr, name, rb0=0):
        _, r, c = w.shape
        nr = r // tr

        def p_spec(layer):
            pinned = (nr - 1) if layer == 0 else 0
            return pl.BlockSpec((N_DEV, None, tr, c),
                                lambda l, i: (0, 0, rb0 + jnp.where(l == layer, i, pinned), 0))

        return adamw(parts, w, m, v, (DEPTH, nr), [p_spec(0), p_spec(1)],
                     pl.BlockSpec((None, tr, c), lambda l, i: (l, i, 0)), name)

    def both(grp, k):
        return [recv[(grp, l)][k] for l in range(DEPTH)]

    out_rows = D_FF // N_DEV // 2
    def upd_transposed(parts, w, m, v, tr, name):
        tp = lambda a: jnp.transpose(a, (0, 2, 1))
        return [tp(o) for o in upd(parts, tp(w), tp(m), tp(v), tr, name)]

    in_rows = FF_BLK // 4
    r_ffn2_in = upd_transposed(both("ffn2", 0), w_ffn2_in, m_w_ffn2_in, v_w_ffn2_in, in_rows, "adamw_ffn2_in")
    r_ffn2_out = upd(both("ffn2", 1), w_ffn2_out, m_w_ffn2_out, v_w_ffn2_out, out_rows, "adamw_ffn2_out")
    r_out = upd(both("mix", 3), w_out, m_w_out, v_w_out, LANES, "adamw_w_out")
    r_br_sb = upd(both("mix", 4), w_br_sb, m_w_br_sb, v_w_br_sb, 256, "adamw_br_sb", rb0=0)
    r_br_ch = upd(both("mix", 4), w_br_ch, m_w_br_ch, v_w_br_ch, 256, "adamw_br_ch", rb0=1)
    r_br_fox = upd(both("mix", 4), w_br_fox, m_w_br_fox, v_w_br_fox, 256, "adamw_br_fox", rb0=3)

    def summed(parts, name):
        _, _, r, c = parts.shape
        return sum_parts(parts, (1,), pl.BlockSpec((N_DEV, None, r, c), lambda s: (0, 0, 0, 0)),
                         pl.BlockSpec((r, c), lambda s: (0, 0)), _sds((r, c), F32), name)

    g_w_in = jnp.stack([
        jnp.concatenate([summed(recv[("mix", l)][0], f"sum_wqkv_l{l}"),
                         summed(recv[("mix", l)][2], f"sum_wf_l{l}")[:, :N_HEADS_FOX],
                         summed(recv[("mix", l)][1], f"sum_wgate_l{l}")], axis=1) for l in range(DEPTH)])
    to_cols = lambda a: jnp.transpose(a, (2, 0, 1))
    n_cols = w_in.shape[2]
    col_blk = n_cols // 4
    win_spec = pl.BlockSpec((col_blk, DEPTH, LANES), lambda i: (i, 0, 0))
    r_in = adamw([to_cols(g_w_in)[None]], to_cols(w_in), to_cols(m_w_in), to_cols(v_w_in), (4,),
                 [pl.BlockSpec((1, col_blk, DEPTH, LANES), lambda i: (0, i, 0, 0))], win_spec, "adamw_w_in")
    r_in = [jnp.transpose(o, (1, 2, 0)) for o in r_in]

    recv[last] = exchange_wait(scatters[last], r_in[1], "scattered_ffn1_l0")
    r_ffn1_in = upd_transposed(both("ffn1", 0), w_ffn1_in, m_w_ffn1_in, v_w_ffn1_in, in_rows, "adamw_ffn1_in")
    r_ffn1_out = upd(both("ffn1", 1), w_ffn1_out, m_w_ffn1_out, v_w_ffn1_out, out_rows, "adamw_ffn1_out")

    small_sum = all_reduce_small(small_packed, "allreduce_small", deps=(r_ffn1_out[1],))
    n_small = sum(int(np.prod(s)) for s in small_shapes)
    loss = small_sum.reshape(-1)[n_small]
    sm_spec = pl.BlockSpec((SMALL_ROWS, LANES), lambda i: (0, 0))
    sm_out = adamw([small_sum[None]], _pack_small(small_w), _pack_small(small_m), _pack_small(small_v),
                   (1,), [pl.BlockSpec((1, SMALL_ROWS, LANES), lambda i: (0, 0, 0))], sm_spec, "adamw_small")
    sm_g, sm_d, sm_m, sm_v = [_unpack_small(o, small_shapes) for o in sm_out]

    def per_kind(k):
        small = (sm_g, sm_d, sm_m, sm_v)[k]
        return [small[0], r_ffn1_in[k], r_ffn1_out[k], small[1], r_in[k], small[2], small[3],
                r_br_sb[k], r_br_ch[k], r_br_fox[k], r_out[k], small[4], r_ffn2_in[k], r_ffn2_out[k],
                small[5]]

    return (loss, dx[None], *per_kind(0), *per_kind(1), *per_kind(2), *per_kind(3))
```

```python
import functools

import numpy as np
import jax
import jax.numpy as jnp
from jax import lax
from jax.experimental import pallas as pl
from jax.experimental.pallas import tpu as pltpu

F32 = jnp.float32
BF16 = jnp.bfloat16

N_DEV = 8
D_MODEL = 1024
DEPTH = 2
HEAD_DIM = 64
W_SB, W_CH, W_FOX = 256, 512, 256
QKV_WIDTH = 3 * (W_SB + W_CH + W_FOX)
N_HEADS_FOX = 4
N_HEADS_CH = 8
D_FF = 2816
FF_BLK = 2 * D_FF // N_DEV
CHUNK = 64
LEFT_CHUNKS = 8
MAX_REL = 128
N_REL = 2 * MAX_REL + 1
REL_PAD = 384
QB = 128
KB = 512
KSUB = KB // QB
CH_WIN = 5
CH_KEYS = CH_WIN * QB
RMS_EPS = 1e-6
NEG = -1e30
SCALE = HEAD_DIM ** -0.5
LANES = 128
VMEM_LIMIT = 56 * 1024 * 1024

ADAM_LR, ADAM_B1, ADAM_B2, ADAM_EPS, ADAM_WD, ADAM_STEP = 0.001, 0.9, 0.999, 1e-08, 0.01, 10

SMALL_ROWS = 192

MESH = pl.DeviceIdType.MESH
ANY = pl.BlockSpec(memory_space=pl.ANY)
HIGHEST = lax.Precision.HIGHEST

NN = (((1,), (0,)), ((), ()))
NT = (((1,), (1,)), ((), ()))
TN = (((0,), (0,)), ((), ()))


def _cparams(n_grid):
    return pltpu.CompilerParams(dimension_semantics=("arbitrary",) * n_grid,
                                vmem_limit_bytes=VMEM_LIMIT)


def _sds(shape, dtype):
    return jax.ShapeDtypeStruct(tuple(shape), dtype)


def _my_index():
    return 4 * lax.axis_index("x") + 2 * lax.axis_index("y") + lax.axis_index("c")


def _peer(mask):
    x, y, c = lax.axis_index("x"), lax.axis_index("y"), lax.axis_index("c")
    px = x ^ ((mask >> 2) & 1)
    py = y ^ ((mask >> 1) & 1)
    pc = c ^ (mask & 1)
    return (px, py, pc), 4 * px + 2 * py + pc


def all_gather(shard, name):
    s, r, c = shard.shape

    def body(in_ref, out_ref, send_sems, recv_sems, local_sem):
        me = _my_index()
        mine = pltpu.make_async_copy(in_ref, out_ref.at[:, me], local_sem)
        mine.start()
        sends = []
        for mask in range(1, N_DEV):
            peer, _ = _peer(mask)
            cp = pltpu.make_async_remote_copy(
                src_ref=in_ref, dst_ref=out_ref.at[:, me],
                send_sem=send_sems.at[mask - 1], recv_sem=recv_sems.at[mask - 1],
                device_id=peer, device_id_type=MESH)
            cp.start()
            sends.append(cp)
        for mask in range(1, N_DEV):
            peer, pidx = _peer(mask)
            pltpu.make_async_remote_copy(
                src_ref=in_ref, dst_ref=out_ref.at[:, pidx],
                send_sem=send_sems.at[mask - 1], recv_sem=recv_sems.at[mask - 1],
                device_id=peer, device_id_type=MESH).wait_recv()
        for cp in sends:
            cp.wait_send()
        mine.wait()

    return pl.pallas_call(
        body, name=name,
        out_shape=_sds((s, N_DEV, r, c), shard.dtype),
        in_specs=[ANY], out_specs=ANY,
        scratch_shapes=[pltpu.SemaphoreType.DMA((N_DEV - 1,)),
                        pltpu.SemaphoreType.DMA((N_DEV - 1,)),
                        pltpu.SemaphoreType.DMA],
    )(shard)


def all_to_all(parts, name):
    s, _, r, c = parts.shape

    def body(in_ref, out_ref, send_sems, recv_sems, local_sem):
        me = _my_index()
        mine = pltpu.make_async_copy(in_ref.at[:, me], out_ref.at[me], local_sem)
        mine.start()
        sends = []
        for mask in range(1, N_DEV):
            peer, pidx = _peer(mask)
            cp = pltpu.make_async_remote_copy(
                src_ref=in_ref.at[:, pidx], dst_ref=out_ref.at[me],
                send_sem=send_sems.at[mask - 1], recv_sem=recv_sems.at[mask - 1],
                device_id=peer, device_id_type=MESH)
            cp.start()
            sends.append(cp)
        for mask in range(1, N_DEV):
            peer, pidx = _peer(mask)
            pltpu.make_async_remote_copy(
                src_ref=in_ref.at[:, me], dst_ref=out_ref.at[pidx],
                send_sem=send_sems.at[mask - 1], recv_sem=recv_sems.at[mask - 1],
                device_id=peer, device_id_type=MESH).wait_recv()
        for cp in sends:
            cp.wait_send()
        mine.wait()

    return pl.pallas_call(
        body, name=name,
        out_shape=_sds((N_DEV, s, r, c), parts.dtype),
        in_specs=[ANY], out_specs=ANY,
        scratch_shapes=[pltpu.SemaphoreType.DMA((N_DEV - 1,)),
                        pltpu.SemaphoreType.DMA((N_DEV - 1,)),
                        pltpu.SemaphoreType.DMA],
    )(parts)


HBM_SPEC = pl.BlockSpec(memory_space=pltpu.HBM)
SEM_SPEC = pl.BlockSpec(memory_space=pltpu.SEMAPHORE)
EFFECT = pltpu.SideEffectType.DATAFLOW_SIDE_EFFECTING


def _exchange_refs(mode, in_ref, land_ref, me, pidx):
    if mode == "gather":
        return in_ref, land_ref.at[:, me], land_ref.at[:, pidx]
    return in_ref.at[:, pidx], land_ref.at[me], land_ref.at[pidx]


def _landing_shape(mode, a):
    if mode == "gather":
        s, r, c = a.shape
        return (s, N_DEV, r, c)
    s, _, r, c = a.shape
    return (N_DEV, s, r, c)


def _own_copy(mode, in_ref, land_ref, me, sem):
    if mode == "gather":
        return pltpu.make_async_copy(in_ref, land_ref.at[:, me], sem)
    return pltpu.make_async_copy(in_ref.at[:, me], land_ref.at[me], sem)


def exchange_start(mode, arrays, name, deps=()):
    n = len(arrays)
    lands0 = [lax.empty(_landing_shape(mode, a), a.dtype) for a in arrays]

    def body(*refs):
        in_refs, land_refs = refs[:n], refs[n:2 * n]
        outs_at = 2 * n + len(deps)
        send_sems, recv_sems, own_sems, token = refs[outs_at], refs[outs_at + 1], refs[outs_at + 2], refs[-1]
        mine = _my_index()
        for k in range(n):
            _own_copy(mode, in_refs[k], land_refs[k], mine, own_sems.at[k]).start()
            for mask in range(1, N_DEV):
                peer, pidx = _peer(mask)
                src, dst, _ = _exchange_refs(mode, in_refs[k], land_refs[k], mine, pidx)
                sem = k * (N_DEV - 1) + mask - 1
                pltpu.make_async_remote_copy(
                    src_ref=src, dst_ref=dst, send_sem=send_sems.at[sem], recv_sem=recv_sems.at[sem],
                    device_id=peer, device_id_type=MESH).start()
        token[...] = jnp.zeros_like(token)

    nsem = n * (N_DEV - 1)
    outs = pl.pallas_call(
        body, name=name,
        out_shape=(pltpu.SemaphoreType.DMA((nsem,)), pltpu.SemaphoreType.DMA((nsem,)),
                   pltpu.SemaphoreType.DMA((n,)),
                   *[pltpu.HBM(a.shape, a.dtype) for a in arrays],
                   *[pltpu.HBM(l.shape, l.dtype) for l in lands0], _sds((8, LANES), F32)),
        in_specs=[HBM_SPEC] * (2 * n) + [ANY] * len(deps),
        out_specs=(SEM_SPEC, SEM_SPEC, SEM_SPEC, *[HBM_SPEC] * (2 * n),
                   pl.BlockSpec(memory_space=pltpu.VMEM)),
        input_output_aliases={k: 3 + k for k in range(2 * n)},
        compiler_params=pltpu.CompilerParams(has_side_effects=EFFECT),
    )(*[pltpu.with_memory_space_constraint(a, pltpu.HBM) for a in arrays],
      *[pltpu.with_memory_space_constraint(l, pltpu.HBM) for l in lands0], *deps)
    return dict(mode=mode, n=n, send=outs[0], recv=outs[1], own=outs[2], ins=outs[3:3 + n],
                lands=outs[3 + n:3 + 2 * n], token=outs[-1])


def exchange_wait(handle, after, name):
    n, mode = handle["n"], handle["mode"]

    def body(*refs):
        in_refs, land_refs = refs[:n], refs[n:2 * n]
        send_sems, recv_sems, own_sems = refs[2 * n], refs[2 * n + 1], refs[2 * n + 2]
        mine = _my_index()
        for k in range(n):
            _own_copy(mode, in_refs[k], land_refs[k], mine, own_sems.at[k]).wait()
            for mask in range(1, N_DEV):
                peer, pidx = _peer(mask)
                src, _, here = _exchange_refs(mode, in_refs[k], land_refs[k], mine, pidx)
                sem = k * (N_DEV - 1) + mask - 1
                cp = pltpu.make_async_remote_copy(
                    src_ref=src, dst_ref=here, send_sem=send_sems.at[sem], recv_sem=recv_sems.at[sem],
                    device_id=peer, device_id_type=MESH)
                cp.wait_send()
                cp.wait_recv()

    thru = (*handle["ins"], *handle["lands"])
    outs = pl.pallas_call(
        body, name=name,
        out_shape=tuple(pltpu.HBM(a.shape, a.dtype) for a in thru),
        in_specs=[HBM_SPEC] * (2 * n) + [SEM_SPEC, SEM_SPEC, SEM_SPEC, ANY],
        out_specs=tuple([HBM_SPEC] * (2 * n)),
        input_output_aliases={k: k for k in range(2 * n)},
        compiler_params=pltpu.CompilerParams(has_side_effects=EFFECT),
    )(*thru, handle["send"], handle["recv"], handle["own"], after)
    return list(outs[n:])


FAR_MASKS = (2, 4, 6)
PHASE1_MASKS = (1,) + FAR_MASKS


def gather_start(arrays, name, deps=()):
    n = len(arrays)
    n1 = len(PHASE1_MASKS)
    lands0 = [lax.empty(_landing_shape("gather", a), a.dtype) for a in arrays]

    def body(*refs):
        in_refs, land_refs = refs[:n], refs[n:2 * n]
        outs_at = 2 * n + len(deps)
        send_sems, recv_sems, own_sems, token = refs[outs_at], refs[outs_at + 1], refs[outs_at + 2], refs[-1]
        mine = _my_index()
        for k in range(n):
            _own_copy("gather", in_refs[k], land_refs[k], mine, own_sems.at[k]).start()
            for j, mask in enumerate(PHASE1_MASKS):
                peer, _ = _peer(mask)
                pltpu.make_async_remote_copy(
                    src_ref=in_refs[k], dst_ref=land_refs[k].at[:, mine],
                    send_sem=send_sems.at[k * n1 + j], recv_sem=recv_sems.at[k * n1 + j],
                    device_id=peer, device_id_type=MESH).start()
        token[...] = jnp.zeros_like(token)

    outs = pl.pallas_call(
        body, name=name,
        out_shape=(pltpu.SemaphoreType.DMA((n * n1,)), pltpu.SemaphoreType.DMA((n * n1,)),
                   pltpu.SemaphoreType.DMA((n,)),
                   *[pltpu.HBM(a.shape, a.dtype) for a in arrays],
                   *[pltpu.HBM(l.shape, l.dtype) for l in lands0], _sds((8, LANES), F32)),
        in_specs=[HBM_SPEC] * (2 * n) + [ANY] * len(deps),
        out_specs=(SEM_SPEC, SEM_SPEC, SEM_SPEC, *[HBM_SPEC] * (2 * n),
                   pl.BlockSpec(memory_space=pltpu.VMEM)),
        input_output_aliases={k: 3 + k for k in range(2 * n)},
        compiler_params=pltpu.CompilerParams(has_side_effects=EFFECT),
    )(*[pltpu.with_memory_space_constraint(a, pltpu.HBM) for a in arrays],
      *[pltpu.with_memory_space_constraint(l, pltpu.HBM) for l in lands0], *deps)
    return dict(n=n, send=outs[0], recv=outs[1], own=outs[2], ins=outs[3:3 + n],
                lands=outs[3 + n:3 + 2 * n], token=outs[-1], name=name)


def gather_relay(handle, after):
    n = handle["n"]
    n1, n2 = len(PHASE1_MASKS), len(FAR_MASKS)

    def body(*refs):
        in_refs, land_refs = refs[:n], refs[n:2 * n]
        send1, recv1 = refs[2 * n], refs[2 * n + 1]
        send2, recv2 = refs[2 * n + 3], refs[2 * n + 4]
        sibling, _ = _peer(1)
        for k in range(n):
            for j, mask in enumerate(FAR_MASKS):
                peer, pidx = _peer(mask)
                landed = land_refs[k].at[:, pidx]
                pltpu.make_async_remote_copy(
                    src_ref=in_refs[k], dst_ref=landed, send_sem=send1.at[k * n1 + 1 + j],
                    recv_sem=recv1.at[k * n1 + 1 + j], device_id=peer, device_id_type=MESH).wait_recv()
                pltpu.make_async_remote_copy(
                    src_ref=landed, dst_ref=landed, send_sem=send2.at[k * n2 + j],
                    recv_sem=recv2.at[k * n2 + j], device_id=sibling, device_id_type=MESH).start()

    thru = (*handle["ins"], *handle["lands"])
    outs = pl.pallas_call(
        body, name=handle["name"] + "_relay",
        out_shape=(pltpu.SemaphoreType.DMA((n * n2,)), pltpu.SemaphoreType.DMA((n * n2,)),
                   *[pltpu.HBM(a.shape, a.dtype) for a in thru]),
        in_specs=[HBM_SPEC] * (2 * n) + [SEM_SPEC, SEM_SPEC, ANY],
        out_specs=(SEM_SPEC, SEM_SPEC, *[HBM_SPEC] * (2 * n)),
        input_output_aliases={k: 2 + k for k in range(2 * n)},
        compiler_params=pltpu.CompilerParams(has_side_effects=EFFECT),
    )(*thru, handle["send"], handle["recv"], after)
    handle.update(send2=outs[0], recv2=outs[1], ins=outs[2:2 + n], lands=outs[2 + n:2 + 2 * n])


def gather_finish(handle, after):
    n = handle["n"]
    n1, n2 = len(PHASE1_MASKS), len(FAR_MASKS)

    def body(*refs):
        in_refs, land_refs = refs[:n], refs[n:2 * n]
        send1, recv1, own_sems, send2, recv2 = refs[2 * n:2 * n + 5]
        mine = _my_index()
        sibling, sib_idx = _peer(1)
        for k in range(n):
            _own_copy("gather", in_refs[k], land_refs[k], mine, own_sems.at[k]).wait()
            for j, mask in enumerate(PHASE1_MASKS):
                peer, pidx = _peer(mask)
                cp = pltpu.make_async_remote_copy(
                    src_ref=in_refs[k], dst_ref=land_refs[k].at[:, pidx], send_sem=send1.at[k * n1 + j],
                    recv_sem=recv1.at[k * n1 + j], device_id=peer, device_id_type=MESH)
                cp.wait_send()
                if mask == 1:
                    cp.wait_recv()
            for j, mask in enumerate(FAR_MASKS):
                _, pidx = _peer(mask)
                _, far_of_sibling = _peer(mask ^ 1)
                cp = pltpu.make_async_remote_copy(
                    src_ref=land_refs[k].at[:, pidx], dst_ref=land_refs[k].at[:, far_of_sibling],
                    send_sem=send2.at[k * n2 + j], recv_sem=recv2.at[k * n2 + j],
                    device_id=sibling, device_id_type=MESH)
                cp.wait_send()
                cp.wait_recv()

    thru = (*handle["ins"], *handle["lands"])
    outs = pl.pallas_call(
        body, name=handle["name"] + "_finish",
        out_shape=tuple(pltpu.HBM(a.shape, a.dtype) for a in thru),
        in_specs=[HBM_SPEC] * (2 * n) + [SEM_SPEC] * 5 + [ANY],
        out_specs=tuple([HBM_SPEC] * (2 * n)),
        input_output_aliases={k: k for k in range(2 * n)},
        compiler_params=pltpu.CompilerParams(has_side_effects=EFFECT),
    )(*thru, handle["send"], handle["recv"], handle["own"], handle["send2"], handle["recv2"], after)
    return list(outs[n:])


def all_reduce_small(packed, name, deps=()):
    rows = packed.shape[0]
    nd = len(deps)

    def body(in_ref, *rest):
        out_ref, slots, send_sems, recv_sems = rest[nd:]
        me = _my_index()
        sends = []
        for mask in range(1, N_DEV):
            peer, _ = _peer(mask)
            cp = pltpu.make_async_remote_copy(
                src_ref=in_ref, dst_ref=slots.at[me],
                send_sem=send_sems.at[mask - 1], recv_sem=recv_sems.at[mask - 1],
                device_id=peer, device_id_type=MESH)
            cp.start()
            sends.append(cp)
        slots[me] = in_ref[...]
        for mask in range(1, N_DEV):
            peer, pidx = _peer(mask)
            pltpu.make_async_remote_copy(
                src_ref=in_ref, dst_ref=slots.at[pidx],
                send_sem=send_sems.at[mask - 1], recv_sem=recv_sems.at[mask - 1],
                device_id=peer, device_id_type=MESH).wait_recv()
        for cp in sends:
            cp.wait_send()
        total = slots[0]
        for p in range(1, N_DEV):
            total = total + slots[p]
        out_ref[...] = total

    return pl.pallas_call(
        body, name=name,
        out_shape=_sds((rows, LANES), F32),
        in_specs=[pl.BlockSpec(memory_space=pltpu.VMEM)] + [ANY] * nd,
        out_specs=pl.BlockSpec(memory_space=pltpu.VMEM),
        scratch_shapes=[pltpu.VMEM((N_DEV, rows, LANES), F32),
                        pltpu.SemaphoreType.DMA((N_DEV - 1,)),
                        pltpu.SemaphoreType.DMA((N_DEV - 1,))],
    )(packed, *deps)


def matmul(dims, a, b, out_sds, grid, a_spec, b_spec, o_spec, acc_shape, *, name, alpha=1.0,
           bias=None, bias_spec=None, res=None, res_spec=None, colsum_sds=None, colsum_spec=None,
           deps=()):
    nk = grid[2]
    has_bias, has_res, has_cs = bias is not None, res is not None, colsum_sds is not None
    if has_cs:
        assert grid[0] == 1 and dims == TN

    def body(*refs):
        a_ref, b_ref = refs[0], refs[1]
        pos = 2
        bias_ref = res_ref = cs_ref = None
        if has_bias:
            bias_ref = refs[pos]; pos += 1
        if has_res:
            res_ref = refs[pos]; pos += 1
        pos += len(deps)
        o_ref = refs[pos]; pos += 1
        if has_cs:
            cs_ref = refs[pos]; pos += 1
        k = pl.program_id(2)
        bval = b_ref[...]
        part = lax.dot_general(a_ref[...].astype(BF16), bval.astype(BF16), dims,
                               preferred_element_type=F32)

        def finish(total):
            r = total * alpha if alpha != 1.0 else total
            if has_bias:
                r = r + bias_ref[...]
            if has_res:
                r = r + res_ref[...].astype(F32)
            o_ref[...] = r.astype(o_ref.dtype)

        if has_cs:
            csum = jnp.sum(bval.astype(F32), axis=0, keepdims=True)

            @pl.when(k == 0)
            def _():
                cs_ref[...] = csum

            @pl.when(k > 0)
            def _():
                cs_ref[...] += csum

        if nk == 1:
            finish(part)
        else:
            acc_ref = refs[pos]

            @pl.when(k == 0)
            def _():
                acc_ref[...] = part

            @pl.when(k > 0)
            def _():
                acc_ref[...] += part

            @pl.when(k == nk - 1)
            def _():
                finish(acc_ref[...])

    in_specs, args = [a_spec, b_spec], [a, b]
    if has_bias:
        in_specs.append(bias_spec); args.append(bias)
    if has_res:
        in_specs.append(res_spec); args.append(res)
    in_specs += [ANY] * len(deps)
    args += list(deps)
    out_shape, out_specs = [out_sds], [o_spec]
    if has_cs:
        out_shape.append(colsum_sds); out_specs.append(colsum_spec)
    scratch = [] if nk == 1 else [pltpu.VMEM(acc_shape, F32)]
    outs = pl.pallas_call(
        body, name=name, grid=grid, in_specs=in_specs, out_specs=out_specs, out_shape=out_shape,
        scratch_shapes=scratch, compiler_params=_cparams(3))(*args)
    return outs if has_cs else outs[0]


def _sigmoid(z):
    return 1.0 / (1.0 + jnp.exp(-z))


def _log_sigmoid(z):
    return jnp.minimum(z, 0.0) - jnp.log(1.0 + jnp.exp(-jnp.abs(z)))


def rmsnorm_fwd(x, gain, tm, name, deps=()):
    t, d = x.shape

    def body(x_ref, g_ref, *rest):
        o_ref = rest[-1]
        xf = x_ref[...]
        r = lax.rsqrt(jnp.mean(xf * xf, axis=-1, keepdims=True) + RMS_EPS)
        o_ref[...] = (xf * r * g_ref[...]).astype(o_ref.dtype)

    return pl.pallas_call(
        body, name=name, grid=(t // tm,),
        in_specs=[pl.BlockSpec((tm, d), lambda i: (i, 0)), pl.BlockSpec((1, d), lambda i: (0, 0))]
        + [ANY] * len(deps),
        out_specs=pl.BlockSpec((tm, d), lambda i: (i, 0)),
        out_shape=_sds((t, d), BF16), compiler_params=_cparams(1))(x, gain, *deps)


def rmsnorm_bwd(x, gain, dh, dres, tm, name):
    t, d = x.shape

    def body(x_ref, g_ref, dh_ref, dres_ref, dx_ref, dg_ref):
        i = pl.program_id(0)
        xf = x_ref[...]
        r = lax.rsqrt(jnp.mean(xf * xf, axis=-1, keepdims=True) + RMS_EPS)
        xhat = xf * r
        dh_v = dh_ref[...]
        dxhat = dh_v * g_ref[...]
        dx = r * (dxhat - xhat * jnp.mean(dxhat * xhat, axis=-1, keepdims=True))
        dx_ref[...] = dres_ref[...] + dx
        dg = jnp.sum(dh_v * xhat, axis=0, keepdims=True)

        @pl.when(i == 0)
        def _():
            dg_ref[...] = dg

        @pl.when(i > 0)
        def _():
            dg_ref[...] += dg

    row = pl.BlockSpec((tm, d), lambda i: (i, 0))
    vec = pl.BlockSpec((1, d), lambda i: (0, 0))
    return pl.pallas_call(
        body, name=name, grid=(t // tm,), in_specs=[row, vec, row, row], out_specs=[row, vec],
        out_shape=[_sds((t, d), F32), _sds((1, d), F32)], compiler_params=_cparams(1))(x, gain, dh, dres)


def loss_head(x, gain, target, tm, name):
    t, d = x.shape

    def body(x_ref, g_ref, tgt_ref, dx_ref, dg_ref, loss_ref):
        i = pl.program_id(0)
        xf = x_ref[...]
        g = g_ref[...]
        r = lax.rsqrt(jnp.mean(xf * xf, axis=-1, keepdims=True) + RMS_EPS)
        xhat = xf * r
        err = xhat * g - tgt_ref[...]
        part = 0.5 * jnp.sum(jnp.mean(err * err, axis=-1, keepdims=True))
        dy = err * (1.0 / d)
        dxhat = dy * g
        dx_ref[...] = r * (dxhat - xhat * jnp.mean(dxhat * xhat, axis=-1, keepdims=True))
        dg = jnp.sum(dy * xhat, axis=0, keepdims=True)
        lpart = jnp.full((8, LANES), part, F32)

        @pl.when(i == 0)
        def _():
            dg_ref[...] = dg
            loss_ref[...] = lpart

        @pl.when(i > 0)
        def _():
            dg_ref[...] += dg
            loss_ref[...] += lpart

    row = pl.BlockSpec((tm, d), lambda i: (i, 0))
    vec = pl.BlockSpec((1, d), lambda i: (0, 0))
    return pl.pallas_call(
        body, name=name, grid=(t // tm,), in_specs=[row, vec, row],
        out_specs=[row, vec, pl.BlockSpec((8, LANES), lambda i: (0, 0))],
        out_shape=[_sds((t, d), F32), _sds((1, d), F32), _sds((8, LANES), F32)],
        compiler_params=_cparams(1))(x, gain, target)


def ffn_in_swiglu(hn, wa, s, tm, name):
    t = hn.shape[0]

    def body(h_ref, wg_ref, wu_ref, gu_ref, act_ref):
        h = h_ref[...]
        g = jnp.dot(h, wg_ref[...], preferred_element_type=F32)
        u = jnp.dot(h, wu_ref[...], preferred_element_type=F32)
        gu_ref[0] = g.astype(gu_ref.dtype)
        gu_ref[1] = u.astype(gu_ref.dtype)
        act_ref[...] = (g * _sigmoid(g) * u).astype(act_ref.dtype)

    return pl.pallas_call(
        body, name=name, grid=(t // tm, 4),
        in_specs=[pl.BlockSpec((tm, D_MODEL), lambda i, j: (i, 0)),
                  pl.BlockSpec((None, None, D_MODEL, FF_BLK), lambda i, j: (s, j, 0, 0)),
                  pl.BlockSpec((None, None, D_MODEL, FF_BLK), lambda i, j: (s, j + 4, 0, 0))],
        out_specs=[pl.BlockSpec((None, 2, tm, FF_BLK), lambda i, j: (j, 0, i, 0)),
                   pl.BlockSpec((None, tm, FF_BLK), lambda i, j: (j, i, 0))],
        out_shape=[_sds((4, 2, t, FF_BLK), BF16), _sds((4, t, FF_BLK), BF16)],
        compiler_params=_cparams(2))(hn, wa, wa)


def ffn_dact_swiglu(dy, wb, gu, s, tm, name):
    t = dy.shape[0]

    def body(dy_ref, w_ref, gu_ref, o_ref):
        da = 0.5 * lax.dot_general(dy_ref[...].astype(BF16), w_ref[...], NT, preferred_element_type=F32)
        g = gu_ref[0].astype(F32)
        u = gu_ref[1].astype(F32)
        sg = _sigmoid(g)
        o_ref[0] = (da * u * (sg * (1.0 + g * (1.0 - sg)))).astype(o_ref.dtype)
        o_ref[1] = (da * g * sg).astype(o_ref.dtype)

    blk = pl.BlockSpec((None, 2, tm, FF_BLK), lambda i, j: (j, 0, i, 0))
    return pl.pallas_call(
        body, name=name, grid=(t // tm, 4),
        in_specs=[pl.BlockSpec((tm, D_MODEL), lambda i, j: (i, 0)),
                  pl.BlockSpec((None, None, FF_BLK, D_MODEL), lambda i, j: (s, j, 0, 0)), blk],
        out_specs=blk, out_shape=_sds((4, 2, t, FF_BLK), BF16),
        compiler_params=_cparams(2))(dy, wb, gu)


def ffn_out_residual(act, wb, x, s, tm, name):
    t = x.shape[0]

    def body(a_ref, w_ref, x_ref, o_ref):
        acc = jnp.dot(a_ref[0], w_ref[0], preferred_element_type=F32)
        for k in range(1, 4):
            acc = acc + jnp.dot(a_ref[k], w_ref[k], preferred_element_type=F32)
        o_ref[...] = x_ref[...] + 0.5 * acc

    row = pl.BlockSpec((tm, D_MODEL), lambda i: (i, 0))
    return pl.pallas_call(
        body, name=name, grid=(t // tm,),
        in_specs=[pl.BlockSpec((4, tm, FF_BLK), lambda i: (0, i, 0)),
                  pl.BlockSpec((None, 4, FF_BLK, D_MODEL), lambda i: (s, 0, 0, 0)), row],
        out_specs=row, out_shape=_sds((t, D_MODEL), F32), compiler_params=_cparams(1))(act, wb, x)


def ffn_dh(dgu, wa, s, tm, name, deps):
    t = dgu.shape[2]

    def body(g_ref, w_ref, *rest):
        o_ref = rest[-1]
        acc = lax.dot_general(g_ref[0, 0], w_ref[0], NT, preferred_element_type=F32)
        for p in range(1, N_DEV):
            acc = acc + lax.dot_general(g_ref[p % 4, p // 4], w_ref[p], NT, preferred_element_type=F32)
        o_ref[...] = acc

    return pl.pallas_call(
        body, name=name, grid=(t // tm,),
        in_specs=[pl.BlockSpec((4, 2, tm, FF_BLK), lambda i: (0, 0, i, 0)),
                  pl.BlockSpec((None, N_DEV, D_MODEL, FF_BLK), lambda i: (s, 0, 0, 0))] + [ANY] * len(deps),
        out_specs=pl.BlockSpec((tm, D_MODEL), lambda i: (i, 0)),
        out_shape=_sds((t, D_MODEL), F32), compiler_params=_cparams(1))(dgu, wa, *deps)


def merge_fwd(gates, ya, yb, yc, tm, name):
    t, d = ya.shape

    def body(ga_ref, gb_ref, gc_ref, ya_ref, yb_ref, yc_ref, o_ref):
        m = (_sigmoid(ga_ref[...]) * ya_ref[...] + _sigmoid(gb_ref[...]) * yb_ref[...]
             + _sigmoid(gc_ref[...]) * yc_ref[...])
        o_ref[...] = m.astype(o_ref.dtype)

    row = pl.BlockSpec((tm, d), lambda i: (i, 0))
    gspecs = [pl.BlockSpec((tm, d), functools.partial(lambda i, a: (i, a), a=a)) for a in range(3)]
    return pl.pallas_call(
        body, name=name, grid=(t // tm,), in_specs=gspecs + [row, row, row], out_specs=row,
        out_shape=_sds((t, d), BF16), compiler_params=_cparams(1))(gates, gates, gates, ya, yb, yc)


def merge_bwd(dm, gates, ya, yb, yc, tm, name):
    t, d = ya.shape

    def body(dm_ref, g_ref, y_ref, dg_ref, dy_ref):
        dmv = dm_ref[...]
        s = _sigmoid(g_ref[...])
        dy_ref[...] = (dmv * s).astype(dy_ref.dtype)
        dg_ref[...] = (dmv * y_ref[...] * s * (1.0 - s)).astype(dg_ref.dtype)

    outs = []
    dgs = []
    for a, y in enumerate((ya, yb, yc)):
        row = pl.BlockSpec((tm, d), lambda i: (i, 0))
        gspec = pl.BlockSpec((tm, d), functools.partial(lambda i, a: (i, a), a=a))
        dg, dy = pl.pallas_call(
            functools.partial(body), name=f"{name}_{a}", grid=(t // tm,),
            in_specs=[row, gspec, row], out_specs=[row, row],
            out_shape=[_sds((t, d), BF16), _sds((t, d), BF16)],
            compiler_params=_cparams(1))(dm, gates, y)
        dgs.append(dg)
        outs.append(dy)
    return dgs, outs


def _iota2(shape, dim):
    return lax.broadcasted_iota(jnp.int32, shape, dim)


def forget_cumsum(f, name):
    t = f.shape[0]
    nq = t // QB

    def body(f_ref, fcol_ref, frow_ref, carry):
        j = pl.program_id(0)

        @pl.when(j == 0)
        def _():
            carry[...] = jnp.zeros_like(carry)

        logf = _log_sigmoid(f_ref[...])
        tri = (_iota2((QB, QB), 1) <= _iota2((QB, QB), 0)).astype(F32)
        blk = jnp.dot(tri, logf, precision=HIGHEST, preferred_element_type=F32) + carry[...]
        carry[...] += jnp.sum(logf, axis=0, keepdims=True)
        fcol_ref[...] = blk
        frow_ref[...] = blk.T[0:8, :]

    return pl.pallas_call(
        body, name=name, grid=(nq,),
        in_specs=[pl.BlockSpec((QB, LANES), lambda j: (j, 0))],
        out_specs=[pl.BlockSpec((QB, LANES), lambda j: (j, 0)),
                   pl.BlockSpec((None, 8, QB), lambda j: (j, 0, 0))],
        out_shape=[_sds((t, LANES), F32), _sds((nq, 8, QB), F32)],
        scratch_shapes=[pltpu.VMEM((1, LANES), F32)], compiler_params=_cparams(1))(f)


def forget_cumsum_bwd(dfrow, f, name):
    t = f.shape[0]
    nq = t // QB

    def body(dfr_ref, f_ref, df_ref, carry):
        jj = pl.program_id(0)

        @pl.when(jj == 0)
        def _():
            carry[...] = jnp.zeros_like(carry)

        padded = jnp.concatenate([dfr_ref[...], jnp.zeros((QB - 8, QB), F32)], axis=0)
        dfcol = padded.T
        tri = (_iota2((QB, QB), 1) >= _iota2((QB, QB), 0)).astype(F32)
        dlogf = jnp.dot(tri, dfcol, precision=HIGHEST, preferred_element_type=F32) + carry[...]
        carry[...] += jnp.sum(dfcol, axis=0, keepdims=True)
        df_ref[...] = dlogf * _sigmoid(-f_ref[...])

    return pl.pallas_call(
        body, name=name, grid=(nq,),
        in_specs=[pl.BlockSpec((None, 8, QB), lambda jj: (nq - 1 - jj, 0, 0)),
                  pl.BlockSpec((QB, LANES), lambda jj: (nq - 1 - jj, 0))],
        out_specs=pl.BlockSpec((QB, LANES), lambda jj: (nq - 1 - jj, 0)),
        out_shape=_sds((t, LANES), F32),
        scratch_shapes=[pltpu.VMEM((1, LANES), F32)], compiler_params=_cparams(1))(dfrow, f)


REL_DIAG = 768
REL_SHIFT = REL_DIAG - (QB - 1)


def _diag_onehot():
    u = _iota2((REL_PAD, REL_DIAG), 1)
    rel = jnp.clip(CH_KEYS - 1 - u, -MAX_REL, MAX_REL) + MAX_REL
    return (_iota2((REL_PAD, REL_DIAG), 0) == rel).astype(F32)


def rel_bias_build(tab_t, name):
    def body(tab_ref, o_ref):
        diag = jnp.dot(tab_ref[...], _diag_onehot(), precision=HIGHEST, preferred_element_type=F32)
        for h in range(N_HEADS_CH):
            rows = jnp.broadcast_to(diag[h:h + 1, :], (QB, REL_DIAG))
            o_ref[h] = pltpu.roll(rows, REL_SHIFT, 1, stride=1, stride_axis=0)[:, :CH_KEYS]

    return pl.pallas_call(
        body, name=name, out_shape=_sds((N_HEADS_CH, QB, CH_KEYS), F32),
        in_specs=[pl.BlockSpec(memory_space=pltpu.VMEM)], out_specs=pl.BlockSpec(memory_space=pltpu.VMEM),
    )(tab_t)


def rel_bias_scatter(dbias, name):
    def body(db_ref, o_ref, ddiag):
        flip = (_iota2((QB, QB), 0) + _iota2((QB, QB), 1) == QB - 1).astype(F32)
        for h in range(N_HEADS_CH):
            padded = jnp.concatenate([db_ref[h], jnp.zeros((QB, REL_DIAG - CH_KEYS), F32)], axis=1)
            flipped = jnp.dot(flip, padded, precision=HIGHEST, preferred_element_type=F32)
            unrolled = pltpu.roll(flipped, 0, 1, stride=1, stride_axis=0)
            ddiag[h:h + 1, :] = jnp.sum(unrolled, axis=0, keepdims=True)
        o_ref[...] = lax.dot_general(ddiag[...], _diag_onehot(), NT, precision=HIGHEST,
                                     preferred_element_type=F32)

    return pl.pallas_call(
        body, name=name, out_shape=_sds((N_HEADS_CH, REL_PAD), F32),
        in_specs=[pl.BlockSpec(memory_space=pltpu.VMEM)], out_specs=pl.BlockSpec(memory_space=pltpu.VMEM),
        scratch_shapes=[pltpu.VMEM((N_HEADS_CH, REL_DIAG), F32)],
    )(dbias)


def _hl(h):
    return slice(h * HEAD_DIM, (h + 1) * HEAD_DIM)


def _split_dot(x, tri_bf16):
    hi = x.astype(BF16)
    lo = (x - hi.astype(F32)).astype(BF16)
    return (jnp.dot(hi, tri_bf16, preferred_element_type=F32)
            + jnp.dot(lo, tri_bf16, preferred_element_type=F32))


def _rows(j):
    return pl.ds(pl.multiple_of(j * QB, QB), QB)


def _krows(g):
    return pl.ds(pl.multiple_of(g * KB, KB), KB)


def _log_sigmoid_pair(z):
    sp = jnp.log(1.0 + jnp.exp(-jnp.abs(z)))
    return jnp.minimum(z, 0.0) - sp, -jnp.maximum(z, 0.0) - sp


def _qkv_specs(t, col0, n_pairs):
    q_spec = pl.BlockSpec((QB, LANES), lambda hp, i: (i, col0 + hp))
    k_spec = pl.BlockSpec((t, LANES), lambda hp, i: (0, col0 + n_pairs + hp))
    v_spec = pl.BlockSpec((t, LANES), lambda hp, i: (0, col0 + 2 * n_pairs + hp))
    return q_spec, k_spec, v_spec


def _keys_major(xt):
    pairs, groups, _, _ = xt.shape
    return xt.transpose(1, 3, 0, 2).reshape(groups * KB, pairs * LANES)


def sb_fwd(qkv, name):
    t = qkv.shape[0]
    nq = t // QB

    def body(q_ref, k_ref, v_ref, o_ref, w_ref):
        i = pl.program_id(1)
        groups = i // KSUB + 1
        tri_after = (_iota2((KB, KB), 0) > _iota2((KB, KB), 1)).astype(BF16)
        t_idx = i * QB + _iota2((QB, KB), 0)
        qs = [q_ref[:, _hl(h)] for h in range(2)]

        def step(gg, carry):
            g = groups - 1 - gg
            strict = (g * KB + _iota2((QB, KB), 1)) < t_idx
            out = []
            for h in range(2):
                tail, acc = carry[2 * h], carry[2 * h + 1]
                k = k_ref[_krows(g), _hl(h)]
                v = v_ref[_krows(g), _hl(h)]
                z = lax.dot_general(qs[h], k, NT, preferred_element_type=F32) * SCALE
                lb, lf = _log_sigmoid_pair(z)
                lf = jnp.where(strict, lf, 0.0)
                between = _split_dot(lf, tri_after) + tail
                w = jnp.where(strict, jnp.exp(lb + between), 0.0).astype(BF16)
                w_ref[h, g] = w
                acc = acc + jnp.dot(w, v, preferred_element_type=F32)
                out += [tail + jnp.sum(lf, axis=1, keepdims=True), acc]
            return tuple(out)

        init = (jnp.zeros((QB, 1), F32), jnp.zeros((QB, HEAD_DIM), F32)) * 2
        res = lax.fori_loop(0, groups, step, init)
        for h in range(2):
            o_ref[:, _hl(h)] = res[2 * h + 1].astype(o_ref.dtype)

    q_spec, k_spec, v_spec = _qkv_specs(t, 0, 2)
    return pl.pallas_call(
        body, name=name, grid=(2, nq), in_specs=[q_spec, k_spec, v_spec],
        out_specs=[pl.BlockSpec((QB, LANES), lambda hp, i: (i, hp)),
                   pl.BlockSpec((2, None, t // KB, QB, KB), lambda hp, i: (hp, i, 0, 0, 0))],
        out_shape=[_sds((t, W_SB), BF16), _sds((4, nq, t // KB, QB, KB), BF16)],
        compiler_params=_cparams(2))(qkv, qkv, qkv)


def _hs(h):
    return slice(h * HEAD_DIM, (h + 1) * HEAD_DIM)


def sb_bwd(qkv, qkv_t, w, do, do_t, name):
    t = qkv.shape[0]
    nq = t // QB

    def body(q_ref, k_ref, v_ref, do_ref, qt_ref, dot_ref, w_ref, dq_ref, dkt_ref, dvt_ref):
        i = pl.program_id(1)

        @pl.when(i == 0)
        def _():
            dkt_ref[...] = jnp.zeros_like(dkt_ref)
            dvt_ref[...] = jnp.zeros_like(dvt_ref)

        groups = i // KSUB + 1
        tri_before = (_iota2((KB, KB), 0) < _iota2((KB, KB), 1)).astype(BF16)
        t_idx = i * QB + _iota2((QB, KB), 0)
        qs = [q_ref[:, _hl(h)] for h in range(2)]
        dos = [do_ref[:, _hl(h)] for h in range(2)]
        qts = [qt_ref[_hs(h), :] for h in range(2)]
        dots = [dot_ref[_hs(h), :] for h in range(2)]

        def grads(g, carry):
            strict = (g * KB + _iota2((QB, KB), 1)) < t_idx
            out = []
            for h in range(2):
                head, dq = carry[2 * h], carry[2 * h + 1]
                k = k_ref[_krows(g), _hl(h)]
                v = v_ref[_krows(g), _hl(h)]
                wb = w_ref[h, g]
                z = lax.dot_general(qs[h], k, NT, preferred_element_type=F32) * SCALE
                beta = _sigmoid(z)
                e = lax.dot_general(dos[h], v, NT, preferred_element_type=F32) * wb.astype(F32)
                before = _split_dot(e, tri_before) + head
                dz = jnp.where(strict, e * (1.0 - beta) - before * beta, 0.0) * SCALE
                dzb = dz.astype(BF16)
                dq = dq + jnp.dot(dzb, k, preferred_element_type=F32)
                dkt_ref[g, _hs(h), :] += jnp.dot(qts[h], dzb, preferred_element_type=F32)
                dvt_ref[g, _hs(h), :] += jnp.dot(dots[h], wb, preferred_element_type=F32)
                out += [head + jnp.sum(e, axis=1, keepdims=True), dq]
            return tuple(out)

        init = (jnp.zeros((QB, 1), F32), jnp.zeros((QB, HEAD_DIM), F32)) * 2
        res = lax.fori_loop(0, groups, grads, init)
        for h in range(2):
            dq_ref[:, _hl(h)] = res[2 * h + 1].astype(dq_ref.dtype)

    q_spec, k_spec, v_spec = _qkv_specs(t, 0, 2)
    blk = pl.BlockSpec((QB, LANES), lambda hp, i: (i, hp))
    blk_t = pl.BlockSpec((LANES, QB), lambda hp, i: (hp, i))
    acc_t = pl.BlockSpec((None, t // KB, LANES, KB), lambda hp, i: (hp, 0, 0, 0))
    acc_sds = _sds((2, t // KB, LANES, KB), F32)
    return pl.pallas_call(
        body, name=name, grid=(2, nq),
        in_specs=[q_spec, k_spec, v_spec, blk, blk_t, blk_t,
                  pl.BlockSpec((2, None, t // KB, QB, KB), lambda hp, i: (hp, i, 0, 0, 0))],
        out_specs=[blk, acc_t, acc_t],
        out_shape=[_sds((t, W_SB), BF16), acc_sds, acc_sds],
        compiler_params=_cparams(2))(qkv, qkv, qkv, do, qkv_t, do_t, w)


def fox_fwd(qkv, fcol, frow, name):
    t = qkv.shape[0]
    nq = t // QB

    def body(q_ref, k_ref, v_ref, fc_ref, fr_ref, o_ref, lse_ref):
        hp = pl.program_id(0)
        i = pl.program_id(1)
        groups = i // KSUB + 1
        t_idx = i * QB + _iota2((QB, KB), 0)
        lane = _iota2((QB, LANES), 1)
        sub = _iota2((8, KB), 0)
        qs = [q_ref[:, _hl(h)] for h in range(2)]
        f_qs = [jnp.sum(jnp.where(lane == hp * 2 + h, fc_ref[...], 0.0), axis=1, keepdims=True)
                for h in range(2)]

        def step(g, carry):
            causal = (g * KB + _iota2((QB, KB), 1)) <= t_idx
            fr = fr_ref[g]
            out = []
            for h in range(2):
                m, l, acc = carry[3 * h:3 * h + 3]
                k = k_ref[_krows(g), _hl(h)]
                v = v_ref[_krows(g), _hl(h)]
                f_k = jnp.sum(jnp.where(sub == hp * 2 + h, fr, 0.0), axis=0, keepdims=True)
                z = lax.dot_general(qs[h], k, NT, preferred_element_type=F32) * SCALE + f_qs[h] - f_k
                z = jnp.where(causal, z, NEG)
                m_new = jnp.maximum(m, jnp.max(z, axis=1, keepdims=True))
                p = jnp.exp(z - m_new)
                corr = jnp.exp(m - m_new)
                l = l * corr + jnp.sum(p, axis=1, keepdims=True)
                acc = acc * corr + jnp.dot(p.astype(BF16), v, preferred_element_type=F32)
                out += [m_new, l, acc]
            return tuple(out)

        init = (jnp.full((QB, 1), NEG, F32), jnp.zeros((QB, 1), F32), jnp.zeros((QB, HEAD_DIM), F32)) * 2
        res = lax.fori_loop(0, groups, step, init)
        for h in range(2):
            m, l, acc = res[3 * h:3 * h + 3]
            o_ref[:, _hl(h)] = (acc / l).astype(o_ref.dtype)
            lse_ref[:, _hl(h)] = jnp.broadcast_to(m + jnp.log(l), (QB, HEAD_DIM))

    q_spec, k_spec, v_spec = _qkv_specs(t, 18, 2)
    blk = pl.BlockSpec((QB, LANES), lambda hp, i: (i, hp))
    return pl.pallas_call(
        body, name=name, grid=(2, nq),
        in_specs=[q_spec, k_spec, v_spec, pl.BlockSpec((QB, LANES), lambda hp, i: (i, 0)),
                  pl.BlockSpec((t // KB, 8, KB), lambda hp, i: (0, 0, 0))],
        out_specs=[blk, blk],
        out_shape=[_sds((t, W_FOX), BF16), _sds((t, W_FOX), F32)],
        compiler_params=_cparams(2))(qkv, qkv, qkv, fcol, frow)


def fox_bwd(qkv, qkv_t, fcol, frow, o, lse, do, do_t, name):
    t = qkv.shape[0]
    nq = t // QB

    def body(q_ref, k_ref, v_ref, fc_ref, fr_ref, o_ref, lse_ref, do_ref, qt_ref, dot_ref,
             dq_ref, dk_ref, dv_ref, dfr_ref):
        hp = pl.program_id(0)
        i = pl.program_id(1)
        qts = [qt_ref[_hs(h), :] for h in range(2)]
        dots = [dot_ref[_hs(h), :] for h in range(2)]

        @pl.when(i == 0)
        def _():
            dk_ref[...] = jnp.zeros_like(dk_ref)
            dv_ref[...] = jnp.zeros_like(dv_ref)

        @pl.when((i == 0) & (hp == 0))
        def _():
            dfr_ref[...] = jnp.zeros_like(dfr_ref)

        groups = i // KSUB + 1
        t_idx = i * QB + _iota2((QB, KB), 0)
        lane = _iota2((QB, LANES), 1)
        sub = _iota2((8, KB), 0)
        qs = [q_ref[:, _hl(h)] for h in range(2)]
        dos = [do_ref[:, _hl(h)] for h in range(2)]
        f_qs = [jnp.sum(jnp.where(lane == hp * 2 + h, fc_ref[...], 0.0), axis=1, keepdims=True)
                for h in range(2)]
        lse_qs = [lse_ref[:, h * HEAD_DIM:h * HEAD_DIM + 1] for h in range(2)]
        deltas = [jnp.sum(dos[h].astype(F32) * o_ref[:, _hl(h)].astype(F32), axis=1, keepdims=True)
                  for h in range(2)]

        def step(g, dqs):
            causal = (g * KB + _iota2((QB, KB), 1)) <= t_idx
            fr = fr_ref[g]
            out = []
            dfr = jnp.zeros((8, KB), F32)
            for h in range(2):
                k = k_ref[_krows(g), _hl(h)]
                v = v_ref[_krows(g), _hl(h)]
                f_k = jnp.sum(jnp.where(sub == hp * 2 + h, fr, 0.0), axis=0, keepdims=True)
                z = lax.dot_general(qs[h], k, NT, preferred_element_type=F32) * SCALE + f_qs[h] - f_k
                p = jnp.where(causal, jnp.exp(z - lse_qs[h]), 0.0)
                dp = lax.dot_general(dos[h], v, NT, preferred_element_type=F32)
                ds = p * (dp - deltas[h])
                dsb = (ds * SCALE).astype(BF16)
                out.append(dqs[h] + jnp.dot(dsb, k, preferred_element_type=F32))
                dk_ref[g, _hs(h), :] += jnp.dot(qts[h], dsb, preferred_element_type=F32)
                dv_ref[g, _hs(h), :] += jnp.dot(dots[h], p.astype(BF16), preferred_element_type=F32)
                colsum = jnp.sum(ds, axis=0, keepdims=True)
                dfr = dfr + jnp.where(sub == hp * 2 + h, -colsum, 0.0)
            dfr_ref[g] += dfr
            return tuple(out)

        res = lax.fori_loop(0, groups, step, (jnp.zeros((QB, HEAD_DIM), F32),) * 2)
        for h in range(2):
            dq_ref[:, _hl(h)] = res[h].astype(dq_ref.dtype)

    q_spec, k_spec, v_spec = _qkv_specs(t, 18, 2)
    blk = pl.BlockSpec((QB, LANES), lambda hp, i: (i, hp))
    frs = pl.BlockSpec((t // KB, 8, KB), lambda hp, i: (0, 0, 0))
    acc_t = pl.BlockSpec((None, t // KB, LANES, KB), lambda hp, i: (hp, 0, 0, 0))
    acc_sds = _sds((2, t // KB, LANES, KB), F32)
    return pl.pallas_call(
        body, name=name, grid=(2, nq),
        in_specs=[q_spec, k_spec, v_spec, pl.BlockSpec((QB, LANES), lambda hp, i: (i, 0)), frs,
                  blk, blk, blk, pl.BlockSpec((LANES, QB), lambda hp, i: (18 + hp, i)),
                  pl.BlockSpec((LANES, QB), lambda hp, i: (hp, i))],
        out_specs=[blk, acc_t, acc_t, frs],
        out_shape=[_sds((t, W_FOX), BF16), acc_sds, acc_sds, _sds((t // KB, 8, KB), F32)],
        compiler_params=_cparams(2))(qkv, qkv, qkv, fcol, frow, o, lse, do, qkv_t, do_t)


def _frow_to_groups(frow):
    n = frow.shape[0] // KSUB
    return frow.reshape(n, KSUB, 8, QB).transpose(0, 2, 1, 3).reshape(n, 8, KB)


def _frow_from_groups(frow):
    n = frow.shape[0]
    return frow.reshape(n, 8, KSUB, QB).transpose(0, 2, 1, 3).reshape(n * KSUB, 8, QB)


def _chunk_valid(i):
    qi = _iota2((QB, CH_KEYS), 0)
    kj = _iota2((QB, CH_KEYS), 1)
    dchunk = (qi >> 6) + LEFT_CHUNKS - (kj >> 6)
    return (dchunk >= 0) & (dchunk <= LEFT_CHUNKS) & ((i - (CH_WIN - 1)) * QB + kj >= 0)


CH_PAD = (CH_WIN - 1) * QB


def _window(i):
    return pl.ds(pl.multiple_of(i * QB, QB), CH_KEYS)


def _chunk_probs(q, kw, bias, valid):
    z = lax.dot_general(q, kw, NT, preferred_element_type=F32) * SCALE + bias
    z = jnp.where(valid, z, NEG)
    z = z - jnp.max(z, axis=1, keepdims=True)
    p = jnp.exp(z)
    return p / jnp.sum(p, axis=1, keepdims=True)


def _chunk_specs(t):
    q_spec = pl.BlockSpec((QB, LANES), lambda hp, i: (i, 6 + hp))
    kv_spec = pl.BlockSpec((t + CH_PAD, LANES), lambda hp, i: (0, hp))
    return q_spec, kv_spec


def chunk_fwd(qkv, kp, vp, bias, name):
    t = qkv.shape[0]
    nq = t // QB

    def body(q_ref, k_ref, v_ref, b_ref, o_ref):
        i = pl.program_id(1)
        valid = _chunk_valid(i)
        for h in range(2):
            p = _chunk_probs(q_ref[:, _hl(h)], k_ref[_window(i), _hl(h)], b_ref[h], valid)
            o_ref[:, _hl(h)] = jnp.dot(p.astype(BF16), v_ref[_window(i), _hl(h)],
                                       preferred_element_type=F32).astype(o_ref.dtype)

    q_spec, kv_spec = _chunk_specs(t)
    return pl.pallas_call(
        body, name=name, grid=(4, nq),
        in_specs=[q_spec, kv_spec, kv_spec, pl.BlockSpec((2, QB, CH_KEYS), lambda hp, i: (hp, 0, 0))],
        out_specs=pl.BlockSpec((QB, LANES), lambda hp, i: (i, hp)),
        out_shape=_sds((t, W_CH), BF16), compiler_params=_cparams(2))(qkv, kp, vp, bias)


def chunk_bwd(qkv, qkv_t, kp, vp, bias, do, do_t, name):
    t = qkv.shape[0]
    nq = t // QB

    def body(q_ref, k_ref, v_ref, b_ref, do_ref, qt_ref, dot_ref, dq_ref, dk_ref, dv_ref, db_ref):
        i = pl.program_id(1)

        @pl.when(i == 0)
        def _():
            dk_ref[...] = jnp.zeros_like(dk_ref)
            dv_ref[...] = jnp.zeros_like(dv_ref)
            db_ref[...] = jnp.zeros_like(db_ref)

        valid = _chunk_valid(i)
        for h in range(2):
            q = q_ref[:, _hl(h)]
            dov = do_ref[:, _hl(h)]
            kw = k_ref[_window(i), _hl(h)]
            p = _chunk_probs(q, kw, b_ref[h], valid)
            dp = lax.dot_general(dov, v_ref[_window(i), _hl(h)], NT, preferred_element_type=F32)
            ds = p * (dp - jnp.sum(p * dp, axis=1, keepdims=True))
            db_ref[h] += ds
            dsb = (ds * SCALE).astype(BF16)
            dq_ref[:, _hl(h)] = jnp.dot(dsb, kw, preferred_element_type=F32).astype(dq_ref.dtype)
            dkt = jnp.dot(qt_ref[_hs(h), :], dsb, preferred_element_type=F32)
            dvt = jnp.dot(dot_ref[_hs(h), :], p.astype(BF16), preferred_element_type=F32)
            for b in range(CH_WIN):
                dk_ref[i + b, _hs(h), :] += dkt[:, b * QB:(b + 1) * QB]
                dv_ref[i + b, _hs(h), :] += dvt[:, b * QB:(b + 1) * QB]

    q_spec, kv_spec = _chunk_specs(t)
    blk = pl.BlockSpec((QB, LANES), lambda hp, i: (i, hp))
    bspec = pl.BlockSpec((2, QB, CH_KEYS), lambda hp, i: (hp, 0, 0))
    nblk = nq + CH_WIN - 1
    acc_t = pl.BlockSpec((None, nblk, LANES, QB), lambda hp, i: (hp, 0, 0, 0))
    acc_sds = _sds((4, nblk, LANES, QB), F32)
    return pl.pallas_call(
        body, name=name, grid=(4, nq),
        in_specs=[q_spec, kv_spec, kv_spec, bspec, blk,
                  pl.BlockSpec((LANES, QB), lambda hp, i: (6 + hp, i)),
                  pl.BlockSpec((LANES, QB), lambda hp, i: (hp, i))],
        out_specs=[blk, acc_t, acc_t, bspec],
        out_shape=[_sds((t, W_CH), BF16), acc_sds, acc_sds, _sds((N_HEADS_CH, QB, CH_KEYS), F32)],
        compiler_params=_cparams(2))(qkv, kp, vp, bias, do, qkv_t, do_t)


def _sum_parts(p_ref):
    total = p_ref[0].astype(F32)
    for p in range(1, p_ref.shape[0]):
        total = total + p_ref[p].astype(F32)
    return total


def sum_parts(parts, grid, p_spec, o_spec, out_sds, name):
    def body(p_ref, o_ref):
        o_ref[...] = _sum_parts(p_ref)

    return pl.pallas_call(body, name=name, grid=grid, in_specs=[p_spec], out_specs=o_spec,
                          out_shape=out_sds, compiler_params=_cparams(len(grid)))(parts)


def adamw(parts, w, m, v, grid, p_specs, w_spec, name):
    c1 = 1.0 / (1.0 - ADAM_B1 ** ADAM_STEP)
    c2 = 1.0 / (1.0 - ADAM_B2 ** ADAM_STEP)
    n = len(parts)

    def body(*refs):
        w_ref, m_ref, v_ref, g_out, d_out, m_out, v_out = refs[n:]
        g = _sum_parts(refs[0])
        for q in range(1, n):
            g = jnp.where(pl.program_id(0) == q, _sum_parts(refs[q]), g)
        m_new = ADAM_B1 * m_ref[...] + (1.0 - ADAM_B1) * g
        v_new = ADAM_B2 * v_ref[...] + (1.0 - ADAM_B2) * (g * g)
        m_hat = m_new * c1
        v_hat = v_new * c2
        g_out[...] = g
        d_out[...] = -ADAM_LR * (m_hat / (jnp.sqrt(v_hat) + ADAM_EPS) + ADAM_WD * w_ref[...])
        m_out[...] = m_new
        v_out[...] = v_new

    out = _sds(w.shape, F32)
    return pl.pallas_call(
        body, name=name, grid=grid, in_specs=[*p_specs, w_spec, w_spec, w_spec],
        out_specs=[w_spec] * 4, out_shape=[out] * 4,
        compiler_params=_cparams(len(grid)))(*parts, w, m, v)


def _ffn_fwd(x, gain, wa, wb_after, s, tm, tag, on_event, deps=()):
    t = x.shape[0]
    hn = rmsnorm_fwd(x, gain, tm, f"rms_{tag}", deps)
    gu, act = ffn_in_swiglu(hn, wa, s, min(2 * tm, t), f"ffn_in_{tag}")
    on_event("act", act)
    wb = wb_after(act)
    y = ffn_out_residual(act, wb, x, s, min(2 * tm, t), f"ffn_out_{tag}")
    return y, (hn, gu, act), wb


def _ffn_bwd(dy, x, gain, saved, wa, wb, s, tm, tag, on_grads):
    t = x.shape[0]
    hn, gu, act = saved
    dgu = ffn_dact_swiglu(dy, wb, gu, s, min(2 * tm, t), f"ffn_dact_{tag}")
    dwb = matmul(TN, act, dy, _sds((4, FF_BLK, D_MODEL), BF16), (4, 1, 1),
                 pl.BlockSpec((None, t, FF_BLK), lambda i, j, k: (i, 0, 0)),
                 pl.BlockSpec((t, D_MODEL), lambda i, j, k: (0, 0)),
                 pl.BlockSpec((None, FF_BLK, D_MODEL), lambda i, j, k: (i, 0, 0)),
                 None, name=f"ffn_dwout_{tag}", alpha=0.5)
    dwa = matmul(TN, dgu, hn, _sds((8, FF_BLK, D_MODEL), BF16), (1, 8, 1),
                 pl.BlockSpec((None, None, t, FF_BLK), lambda i, j, k: (j % 4, j // 4, 0, 0)),
                 pl.BlockSpec((t, D_MODEL), lambda i, j, k: (0, 0)),
                 pl.BlockSpec((None, FF_BLK, D_MODEL), lambda i, j, k: (j, 0, 0)),
                 None, name=f"ffn_dwin_{tag}")
    deps = on_grads(dwa, dwb)
    dhn = ffn_dh(dgu, wa, s, tm, f"ffn_dh_{tag}", deps)
    dx, dgain = rmsnorm_bwd(x, gain, dhn, dy, tm, f"rms_bwd_{tag}")
    return dx, dgain


BR_ROWS = ((0, 1), (1, 2), (3, 1))


def _mixer_fwd(x, gain, wqkv, wf, wgate, late_after, bq, bf, bg, bias, layer, tm, tag, on_event):
    t = x.shape[0]
    nt = t // tm
    hm = rmsnorm_fwd(x, gain, tm, f"rms_{tag}")
    a_full = pl.BlockSpec((tm, D_MODEL), lambda i, j, k: (i, 0))
    wide_out = pl.BlockSpec((tm, D_MODEL), lambda i, j, k: (i, j))
    wide_b = pl.BlockSpec((1, D_MODEL), lambda i, j, k: (0, j))
    qkv = matmul(NN, hm, wqkv, _sds((t, QKV_WIDTH), BF16), (nt, 3, 1), a_full,
                 pl.BlockSpec((None, D_MODEL, D_MODEL), lambda i, j, k: (layer, 0, j)), wide_out, None,
                 name=f"proj_qkv_{tag}", bias=bq, bias_spec=wide_b)
    gates = matmul(NN, hm, wgate, _sds((t, 3 * D_MODEL), F32), (nt, 3, 1), a_full,
                   pl.BlockSpec((None, D_MODEL, D_MODEL), lambda i, j, k: (layer + 1, 0,j)), wide_out,
                   None, name=f"proj_gate_{tag}", bias=bg, bias_spec=wide_b)
    f = matmul(NN, hm, wf, _sds((t, LANES), F32), (nt, 1, 1), a_full,
               pl.BlockSpec((None, D_MODEL, LANES), lambda i, j, k: (layer, 0, 0)),
               pl.BlockSpec((tm, LANES), lambda i, j, k: (i, 0)), None,
               name=f"proj_f_{tag}", bias=bf, bias_spec=pl.BlockSpec((1, LANES), lambda i, j, k: (0, 0)))
    fcol, frow = forget_cumsum(f, f"fcum_{tag}")
    frow = _frow_to_groups(frow)
    on_event("qkv", qkv)
    o_sb, w_sb = sb_fwd(qkv, f"sb_fwd_{tag}")
    on_event("o_sb", o_sb)
    kp = jnp.pad(qkv[:, 10 * LANES:14 * LANES], ((CH_PAD, 0), (0, 0)))
    vp = jnp.pad(qkv[:, 14 * LANES:18 * LANES], ((CH_PAD, 0), (0, 0)))
    o_ch = chunk_fwd(qkv, kp, vp, bias, f"chunk_fwd_{tag}")
    o_fox, lse = fox_fwd(qkv, fcol, frow, f"fox_fwd_{tag}")
    wbr, wout = late_after(o_fox)
    ys = []
    for a, (o, (r0, nr)) in enumerate(zip((o_sb, o_ch, o_fox), BR_ROWS)):
        ys.append(matmul(
            NN, o, wbr, _sds((t, D_MODEL), F32), (nt, 1, nr),
            pl.BlockSpec((tm, 256), lambda i, j, k: (i, k)),
            pl.BlockSpec((None, 256, D_MODEL), functools.partial(lambda i, j, k, r0: (layer, r0 + k, 0), r0=r0)),
            a_full, (tm, D_MODEL), name=f"branch{a}_{tag}"))
    merged = merge_fwd(gates, ys[0], ys[1], ys[2], tm, f"merge_{tag}")
    x_new = matmul(NN, merged, wout, _sds((t, D_MODEL), F32), (nt, 1, 1), a_full,
                   pl.BlockSpec((None, D_MODEL, D_MODEL), lambda i, j, k: (layer, 0, 0)), a_full, None,
                   name=f"wout_{tag}", res=x, res_spec=a_full)
    saved = (hm, qkv, gates, f, fcol, frow, o_sb, o_ch, o_fox, lse, ys, merged, kp, vp, w_sb)
    return x_new, saved, wbr, wout


def _mixer_bwd(dy, x, gain, saved, wqkv, wf, wgate, wbr, wout, bias, layer, tm, tag, on_grads):
    t = x.shape[0]
    nt = t // tm
    hm, qkv, gates, f, fcol, frow, o_sb, o_ch, o_fox, lse, ys, merged, kp, vp, w_sb = saved
    a_full = pl.BlockSpec((tm, D_MODEL), lambda i, j, k: (i, 0))
    red_row = pl.BlockSpec((tm, D_MODEL), lambda i, j, k: (k, 0))
    sq = pl.BlockSpec((D_MODEL, D_MODEL), lambda i, j, k: (0, 0))
    dmerged = matmul(NT, dy, wout, _sds((t, D_MODEL), F32), (nt, 1, 1), a_full,
                     pl.BlockSpec((None, D_MODEL, D_MODEL), lambda i, j, k: (layer, 0, 0)), a_full, None,
                     name=f"dmerged_{tag}")
    all_t = pl.BlockSpec((t, D_MODEL), lambda i, j, k: (0, 0))
    dwout = matmul(TN, merged, dy, _sds((D_MODEL, D_MODEL), BF16), (1, 1, 1), all_t, all_t, sq,
                   None, name=f"dwout_{tag}")
    dgs, dys = merge_bwd(dmerged, gates, ys[0], ys[1], ys[2], tm, f"merge_bwd_{tag}")
    dos, dwbrs = [], []
    for a, (o, (r0, nr)) in enumerate(zip((o_sb, o_ch, o_fox), BR_ROWS)):
        dos.append(matmul(
            NT, dys[a], wbr, _sds((t, nr * 256), BF16), (nt, nr, 1), a_full,
            pl.BlockSpec((None, 256, D_MODEL), functools.partial(lambda i, j, k, r0: (layer, r0 + j, 0), r0=r0)),
            pl.BlockSpec((tm, 256), lambda i, j, k: (i, j)), None, name=f"dbranch{a}_{tag}"))
        dwbrs.append(matmul(
            TN, o, dys[a], _sds((nr * 256, D_MODEL), BF16), (nr, 1, 1),
            pl.BlockSpec((t, 256), lambda i, j, k: (0, i)), all_t,
            pl.BlockSpec((256, D_MODEL), lambda i, j, k: (i, 0)), None, name=f"dwbr{a}_{tag}"))
    qkv_t = qkv.T
    dq_a, dk_a, dv_a = sb_bwd(qkv, qkv_t, w_sb, dos[0], dos[0].T, f"sb_bwd_{tag}")
    dk_a, dv_a = _keys_major(dk_a), _keys_major(dv_a)
    dq_b, dk_b, dv_b, dbias = chunk_bwd(qkv, qkv_t, kp, vp, bias, dos[1], dos[1].T, f"chunk_bwd_{tag}")
    dk_b, dv_b = [x[:, CH_WIN - 1:].transpose(1, 3, 0, 2).reshape(t, W_CH) for x in (dk_b, dv_b)]
    dq_c, dk_c, dv_c, dfrow = fox_bwd(qkv, qkv_t, fcol, frow, o_fox, lse, dos[2], dos[2].T, f"fox_bwd_{tag}")
    dk_c, dv_c = _keys_major(dk_c), _keys_major(dv_c)
    df = forget_cumsum_bwd(_frow_from_groups(dfrow), f, f"fcum_bwd_{tag}")
    dqkv = jnp.concatenate([p.astype(BF16) for p in
                            (dq_a, dk_a, dv_a, dq_b, dk_b, dv_b, dq_c, dk_c, dv_c)], axis=1)
    dgates = jnp.concatenate(dgs, axis=1)
    dtab = rel_bias_scatter(dbias, f"rel_scatter_{tag}")

    all_rows = pl.BlockSpec((t, D_MODEL), lambda i, j, k: (0, 0))
    wide_b = pl.BlockSpec((t, D_MODEL), lambda i, j, k: (0, j))
    wide_o = pl.BlockSpec((D_MODEL, D_MODEL), lambda i, j, k: (0, j))
    wide_cs = pl.BlockSpec((1, D_MODEL), lambda i, j, k: (0, j))
    dwqkv, dbq = matmul(TN, hm, dqkv, _sds((D_MODEL, QKV_WIDTH), BF16), (1, 3, 1), all_rows, wide_b,
                        wide_o, None, name=f"dwqkv_{tag}",
                        colsum_sds=_sds((1, QKV_WIDTH), F32), colsum_spec=wide_cs)
    dwgate, dbg = matmul(TN, hm, dgates, _sds((D_MODEL, 3 * D_MODEL), BF16), (1, 3, 1), all_rows,
                         wide_b, wide_o, None, name=f"dwgate_{tag}",
                         colsum_sds=_sds((1, 3 * D_MODEL), F32), colsum_spec=wide_cs)
    dwf, dbf = matmul(TN, hm, df, _sds((D_MODEL, LANES), BF16), (1, 1, 1), all_rows,
                      pl.BlockSpec((t, LANES), lambda i, j, k: (0, 0)),
                      pl.BlockSpec((D_MODEL, LANES), lambda i, j, k: (0, 0)), None,
                      name=f"dwf_{tag}", colsum_sds=_sds((1, LANES), F32),
                      colsum_spec=pl.BlockSpec((1, LANES), lambda i, j, k: (0, 0)))
    dwbr = jnp.concatenate(dwbrs, axis=0)
    deps = on_grads(dict(dwqkv=dwqkv, dwgate=dwgate, dwf=dwf, dwbr=dwbr, dwout=dwout))
    wide_a = pl.BlockSpec((tm, QKV_WIDTH), lambda i, j, k: (i, 0))
    dhm = matmul(NT, dqkv, wqkv, _sds((t, D_MODEL), F32), (nt, 1, 1), wide_a,
                 pl.BlockSpec((None, D_MODEL, QKV_WIDTH), lambda i, j, k: (layer, 0, 0)), a_full,
                 None, name=f"dhm_qkv_{tag}", deps=deps)
    dhm = matmul(NT, dgates, wgate, _sds((t, D_MODEL), F32), (nt, 1, 1), wide_a,
                 pl.BlockSpec((None, D_MODEL, QKV_WIDTH), lambda i, j, k: (layer + 1, 0, 0)), a_full,
                 None, name=f"dhm_gate_{tag}", res=dhm, res_spec=a_full)
    dhm = matmul(NT, df, wf, _sds((t, D_MODEL), F32), (nt, 1, 1),
                 pl.BlockSpec((tm, LANES), lambda i, j, k: (i, 0)),
                 pl.BlockSpec((None, D_MODEL, LANES), lambda i, j, k: (layer, 0, 0)), a_full, None,
                 name=f"dhm_f_{tag}", res=dhm, res_spec=a_full)
    dx, dgain = rmsnorm_bwd(x, gain, dhm, dy, tm, f"rms_bwd_{tag}")
    return dx, dict(dbq=dbq, dbg=dbg, dbf=dbf, dtab=dtab, dgain=dgain)


def _pack_small(pieces):
    flat = jnp.concatenate([p.reshape(-1).astype(F32) for p in pieces])
    flat = jnp.pad(flat, (0, SMALL_ROWS * LANES - flat.shape[0]))
    return flat.reshape(SMALL_ROWS, LANES)


def _unpack_small(packed, shapes):
    flat = packed.reshape(-1)
    out, pos = [], 0
    for shp in shapes:
        n = int(np.prod(shp))
        out.append(flat[pos:pos + n].reshape(shp))
        pos += n
    return out


def kernel(x, g_ffn1, w_ffn1_in, w_ffn1_out, g_mix, w_in, b_in, rel_bias, w_br_sb, w_br_ch, w_br_fox, w_out, g_ffn2, w_ffn2_in, w_ffn2_out, g_final, loss_target, m_g_ffn1, m_w_ffn1_in, m_w_ffn1_out, m_g_mix, m_w_in, m_b_in, m_rel_bias, m_w_br_sb, m_w_br_ch, m_w_br_fox, m_w_out, m_g_ffn2, m_w_ffn2_in, m_w_ffn2_out, m_g_final, v_g_ffn1, v_w_ffn1_in, v_w_ffn1_out, v_g_mix, v_w_in, v_b_in, v_rel_bias, v_w_br_sb, v_w_br_ch, v_w_br_fox, v_w_out, v_g_ffn2, v_w_ffn2_in, v_w_ffn2_out, v_g_final):
    t = x.shape[1]
    tm = min(512, t)
    xs = x[0]
    target = loss_target[0]
    f_lo, f_hi = QKV_WIDTH, QKV_WIDTH + N_HEADS_FOX

    def ffn_shards(w_in_, w_out_, l):
        return [w_in_[l:l + 1].astype(BF16), w_out_[l:l + 1].astype(BF16)]

    def mixer_shards(l):
        wl = w_in[l]
        return [jnp.stack([wl[:, :QKV_WIDTH], wl[:, f_hi:]]).astype(BF16),
                jnp.pad(wl[:, f_lo:f_hi], ((0, 0), (0, LANES - N_HEADS_FOX)))[None].astype(BF16),
                w_out[l:l + 1].astype(BF16),
                jnp.concatenate([w_br_sb[l], w_br_ch[l], w_br_fox[l]], axis=0)[None].astype(BF16)]

    gathers = {}
    gather_tokens = []

    def start_gather(shards, name):
        handle = gather_start(shards, name, deps=gather_tokens[-1:])
        gather_tokens.append(handle["token"])
        return handle

    def relay(handle, after):
        if "send2" not in handle:
            gather_relay(handle, after)

    relay_on = {("mix", 0, "qkv"): ("mix", 0, 1), ("mix", 0, "o_sb"): ("ffn2", 0, 0),
                ("ffn2", 0, "act"): ("ffn1", 1, 0), ("ffn1", 1, "act"): ("mix", 1, 0),
                ("mix", 1, "qkv"): ("ffn2", 1, 0)}

    def on_event(grp, l):
        def fire(event, array):
            target = relay_on.get((grp, l, event))
            if target is not None:
                relay(gathers[target[:2]][target[2]], array)
        return fire

    for l in range(DEPTH):
        for grp, shards in (("ffn1", ffn_shards(w_ffn1_in, w_ffn1_out, l)), ("mix", mixer_shards(l)),
                            ("ffn2", ffn_shards(w_ffn2_in, w_ffn2_out, l))):
            cut = len(shards) // 2
            if l == 0 and grp != "ffn2":
                gathers[(grp, l)] = (start_gather(shards[:cut], f"gather_{grp}_l{l}_a"),
                                     start_gather(shards[cut:], f"gather_{grp}_l{l}_b"))
            else:
                gathers[(grp, l)] = (start_gather(shards, f"gather_{grp}_l{l}"),)

    def gathered(key, after):
        hs = gathers[key]
        cut = hs[0]["n"]
        relay(hs[0], after)
        first = gather_finish(hs[0], after)
        if len(hs) == 1:
            return first[:cut // 2], lambda later: first[cut // 2:]

        def second(later):
            relay(hs[1], later)
            return gather_finish(hs[1], later)

        return first, second

    def ffn_weights(key, after):
        (wa_,), rest = gathered(key, after)
        return wa_, lambda later: rest(later)[0].reshape(1, 4, FF_BLK, D_MODEL)

    def mixer_weights(key, after):
        (wc_, wf_), rest = gathered(key, after)

        def late(later):
            wout_, wbr_ = rest(later)
            return (wbr_.transpose(0, 2, 1, 3).reshape(1, D_MODEL, D_MODEL), wout_.reshape(1, D_MODEL, D_MODEL))

        return wc_.reshape(2, D_MODEL, QKV_WIDTH), wf_.reshape(1, D_MODEL, LANES), late

    bq = b_in[:, None, :QKV_WIDTH]
    bf = jnp.pad(b_in[:, f_lo:f_hi], ((0, 0), (0, LANES - N_HEADS_FOX)))[:, None, :]
    bg = b_in[:, None, f_hi:]
    tab_t = jnp.pad(rel_bias.transpose(0, 2, 1), ((0, 0), (0, 0), (0, REL_PAD - N_REL)))

    h = xs
    saved = []
    weights = []
    for l in range(DEPTH):
        bias = rel_bias_build(tab_t[l], f"rel_build_l{l}").reshape(N_HEADS_CH, QB, CH_KEYS)
        x0 = h
        wa1, wb1_after = ffn_weights(("ffn1", l), x0)
        x1, s1, wb1 = _ffn_fwd(x0, g_ffn1[l:l + 1], wa1, wb1_after, 0, tm, f"ffn1_l{l}", on_event("ffn1", l),
                               deps=gather_tokens if l == 0 else ())
        wc, wf, late_after = mixer_weights(("mix", l), x1)
        x2, sm, wbr, wout = _mixer_fwd(x1, g_mix[l:l + 1], wc, wf, wc, late_after, bq[l], bf[l], bg[l],
                                       bias, 0, tm, f"mix_l{l}", on_event("mix", l))
        wa2, wb2_after = ffn_weights(("ffn2", l), x2)
        x3, s2, wb2 = _ffn_fwd(x2, g_ffn2[l:l + 1], wa2, wb2_after, 0, tm, f"ffn2_l{l}", on_event("ffn2", l))
        saved.append((x0, x1, x2, s1, sm, s2, bias))
        weights.append(((wa1, wb1), (wc, wf, wout, wbr), (wa2, wb2)))
        h = x3

    dx, dg_final, loss_blk = loss_head(h, g_final[None, :], target, tm, "loss_head")

    g_mix_l = [None] * DEPTH
    dgains = {}
    scatters = {}

    def scatter_ffn(key):
        def on_grads(dwa, dwb):
            scatters[key] = exchange_start(
                "scatter", [dwa[None], dwb.reshape(1, N_DEV, D_FF // N_DEV, D_MODEL)],
                f"scatter_{key[0]}_l{key[1]}")
            return (scatters[key]["token"],)
        return on_grads

    def scatter_mixer(key):
        def on_grads(gm):
            scatters[key] = exchange_start(
                "scatter",
                [gm["dwqkv"].reshape(1, N_DEV, LANES, QKV_WIDTH), gm["dwgate"].reshape(1, N_DEV, LANES, QKV_WIDTH),
                 gm["dwf"].reshape(1, N_DEV, LANES, LANES), gm["dwout"].reshape(1, N_DEV, LANES, D_MODEL),
                 gm["dwbr"].reshape(1, D_MODEL, N_DEV, LANES).transpose(0, 2, 1, 3)],
                f"scatter_{key[0]}_l{key[1]}")
            return (scatters[key]["token"],)
        return on_grads

    for l in reversed(range(DEPTH)):
        x0, x1, x2, s1, sm, s2, bias = saved[l]
        w1, (wc, wf, wout, wbr), w2 = weights[l]
        dx, dgains[("ffn2", l)] = _ffn_bwd(dx, x2, g_ffn2[l:l + 1], s2, *w2, 0, tm, f"ffn2_l{l}",
                                           scatter_ffn(("ffn2", l)))
        dx, g_mix_l[l] = _mixer_bwd(dx, x1, g_mix[l:l + 1], sm, wc, wf, wc, wbr, wout, bias, 0, tm,
                                    f"mix_l{l}", scatter_mixer(("mix", l)))
        dx, dgains[("ffn1", l)] = _ffn_bwd(dx, x0, g_ffn1[l:l + 1], s1, *w1, 0, tm, f"ffn1_l{l}",
                                           scatter_ffn(("ffn1", l)))

    small_shapes = []
    small_pieces = []
    small_w, small_m, small_v = [], [], []

    def add_small(piece, w, m, v):
        small_shapes.append(w.shape)
        small_pieces.append(piece)
        small_w.append(w); small_m.append(m); small_v.append(v)

    dg1 = jnp.concatenate([dgains[("ffn1", l)] for l in range(DEPTH)], axis=0)
    dgm = jnp.concatenate([g_mix_l[l]["dgain"] for l in range(DEPTH)], axis=0)
    dg2 = jnp.concatenate([dgains[("ffn2", l)] for l in range(DEPTH)], axis=0)
    db = jnp.stack([jnp.concatenate([g_mix_l[l]["dbq"][0], g_mix_l[l]["dbf"][0, :N_HEADS_FOX],
                                     g_mix_l[l]["dbg"][0]]) for l in range(DEPTH)])
    drel = jnp.stack([g_mix_l[l]["dtab"][:, :N_REL].T for l in range(DEPTH)])
    add_small(dg1, g_ffn1, m_g_ffn1, v_g_ffn1)
    add_small(dgm, g_mix, m_g_mix, v_g_mix)
    add_small(db, b_in, m_b_in, v_b_in)
    add_small(drel, rel_bias, m_rel_bias, v_rel_bias)
    add_small(dg2, g_ffn2, m_g_ffn2, v_g_ffn2)
    add_small(dg_final[0], g_final, m_g_final, v_g_final)
    loss_piece = loss_blk[0, 0:1]
    small_packed = _pack_small(small_pieces + [loss_piece])

    recv = {}
    last = ("ffn1", 0)
    for l in reversed(range(DEPTH)):
        for grp in ("ffn2", "mix", "ffn1"):
            if (grp, l) != last:
                recv[(grp, l)] = exchange_wait(scatters[(grp, l)], dx, f"scattered_{grp}_l{l}")

    def upd(parts, w, m, v, tr, name, rb0=0):
        _, r, c = w.shape
        nr = r // tr

        def p_spec(layer):
            pinned = (nr - 1) if layer == 0 else 0
            return pl.BlockSpec((N_DEV, None, tr, c),
                                lambda l, i: (0, 0, rb0 + jnp.where(l == layer, i, pinned), 0))

        return adamw(parts, w, m, v, (DEPTH, nr), [p_spec(0), p_spec(1)],
                     pl.BlockSpec((None, tr, c), lambda l, i: (l, i, 0)), name)

    def both(grp, k):
        return [recv[(grp, l)][k] for l in range(DEPTH)]

    out_rows = D_FF // N_DEV // 2
    def upd_transposed(parts, w, m, v, tr, name):
        tp = lambda a: jnp.transpose(a, (0, 2, 1))
        return [tp(o) for o in upd(parts, tp(w), tp(m), tp(v), tr, name)]

    in_rows = FF_BLK // 4
    r_ffn2_in = upd_transposed(both("ffn2", 0), w_ffn2_in, m_w_ffn2_in, v_w_ffn2_in, in_rows, "adamw_ffn2_in")
    r_ffn2_out = upd(both("ffn2", 1), w_ffn2_out, m_w_ffn2_out, v_w_ffn2_out, out_rows, "adamw_ffn2_out")
    r_out = upd(both("mix", 3), w_out, m_w_out, v_w_out, LANES, "adamw_w_out")
    r_br_sb = upd(both("mix", 4), w_br_sb, m_w_br_sb, v_w_br_sb, 256, "adamw_br_sb", rb0=0)
    r_br_ch = upd(both("mix", 4), w_br_ch, m_w_br_ch, v_w_br_ch, 256, "adamw_br_ch", rb0=1)
    r_br_fox = upd(both("mix", 4), w_br_fox, m_w_br_fox, v_w_br_fox, 256, "adamw_br_fox", rb0=3)

    def summed(parts, name):
        _, _, r, c = parts.shape
        return sum_parts(parts, (1,), pl.BlockSpec((N_DEV, None, r, c), lambda s: (0, 0, 0, 0)),
                         pl.BlockSpec((r, c), lambda s: (0, 0)), _sds((r, c), F32), name)

    g_w_in = jnp.stack([
        jnp.concatenate([summed(recv[("mix", l)][0], f"sum_wqkv_l{l}"),
                         summed(recv[("mix", l)][2], f"sum_wf_l{l}")[:, :N_HEADS_FOX],
                         summed(recv[("mix", l)][1], f"sum_wgate_l{l}")], axis=1) for l in range(DEPTH)])
    to_cols = lambda a: jnp.transpose(a, (2, 0, 1))
    n_cols = w_in.shape[2]
    col_blk = n_cols // 4
    win_spec = pl.BlockSpec((col_blk, DEPTH, LANES), lambda i: (i, 0, 0))
    r_in = adamw([to_cols(g_w_in)[None]], to_cols(w_in), to_cols(m_w_in), to_cols(v_w_in), (4,),
                 [pl.BlockSpec((1, col_blk, DEPTH, LANES), lambda i: (0, i, 0, 0))], win_spec, "adamw_w_in")
    r_in = [jnp.transpose(o, (1, 2, 0)) for o in r_in]

    recv[last] = exchange_wait(scatters[last], r_in[1], "scattered_ffn1_l0")
    r_ffn1_in = upd_transposed(both("ffn1", 0), w_ffn1_in, m_w_ffn1_in, v_w_ffn1_in, in_rows, "adamw_ffn1_in")
    r_ffn1_out = upd(both("ffn1", 1), w_ffn1_out, m_w_ffn1_out, v_w_ffn1_out, out_rows, "adamw_ffn1_out")

    small_sum = all_reduce_small(small_packed, "allreduce_small", deps=(r_ffn1_out[1],))
    n_small = sum(int(np.prod(s)) for s in small_shapes)
    loss = small_sum.reshape(-1)[n_small]
    sm_spec = pl.BlockSpec((SMALL_ROWS, LANES), lambda i: (0, 0))
    sm_out = adamw([small_sum[None]], _pack_small(small_w), _pack_small(small_m), _pack_small(small_v),
                   (1,), [pl.BlockSpec((1, SMALL_ROWS, LANES), lambda i: (0, 0, 0))], sm_spec, "adamw_small")
    sm_g, sm_d, sm_m, sm_v = [_unpack_small(o, small_shapes) for o in sm_out]

    def per_kind(k):
        small = (sm_g, sm_d, sm_m, sm_v)[k]
        return [small[0], r_ffn1_in[k], r_ffn1_out[k], small[1], r_in[k], small[2], small[3],
                r_br_sb[k], r_br_ch[k], r_br_fox[k], r_out[k], small[4], r_ffn2_in[k], r_ffn2_out[k],
                small[5]]

    return (loss, dx[None], *per_kind(0), *per_kind(1), *per_kind(2), *per_kind(3))
```

```python
import functools

import numpy as np
import jax
import jax.numpy as jnp
from jax import lax
from jax.experimental import pallas as pl
from jax.experimental.pallas import tpu as pltpu

F32 = jnp.float32
BF16 = jnp.bfloat16

N_DEV = 8
D_MODEL = 1024
DEPTH = 2
HEAD_DIM = 64
W_SB, W_CH, W_FOX = 256, 512, 256
QKV_WIDTH = 3 * (W_SB + W_CH + W_FOX)
N_HEADS_FOX = 4
N_HEADS_CH = 8
D_FF = 2816
FF_BLK = 2 * D_FF // N_DEV
CHUNK = 64
LEFT_CHUNKS = 8
MAX_REL = 128
N_REL = 2 * MAX_REL + 1
REL_PAD = 384
QB = 128
KB = 512
KSUB = KB // QB
CH_WIN = 5
CH_KEYS = CH_WIN * QB
RMS_EPS = 1e-6
NEG = -1e30
SCALE = HEAD_DIM ** -0.5
LANES = 128
VMEM_LIMIT = 56 * 1024 * 1024

ADAM_LR, ADAM_B1, ADAM_B2, ADAM_EPS, ADAM_WD, ADAM_STEP = 0.001, 0.9, 0.999, 1e-08, 0.01, 10

SMALL_ROWS = 192

MESH = pl.DeviceIdType.MESH
ANY = pl.BlockSpec(memory_space=pl.ANY)
HIGHEST = lax.Precision.HIGHEST

NN = (((1,), (0,)), ((), ()))
NT = (((1,), (1,)), ((), ()))
TN = (((0,), (0,)), ((), ()))


def _cparams(n_grid):
    return pltpu.CompilerParams(dimension_semantics=("arbitrary",) * n_grid,
                                vmem_limit_bytes=VMEM_LIMIT)


def _sds(shape, dtype):
    return jax.ShapeDtypeStruct(tuple(shape), dtype)


def _my_index():
    return 4 * lax.axis_index("x") + 2 * lax.axis_index("y") + lax.axis_index("c")


def _peer(mask):
    x, y, c = lax.axis_index("x"), lax.axis_index("y"), lax.axis_index("c")
    px = x ^ ((mask >> 2) & 1)
    py = y ^ ((mask >> 1) & 1)
    pc = c ^ (mask & 1)
    return (px, py, pc), 4 * px + 2 * py + pc


def all_gather(shard, name):
    s, r, c = shard.shape

    def body(in_ref, out_ref, send_sems, recv_sems, local_sem):
        me = _my_index()
        mine = pltpu.make_async_copy(in_ref, out_ref.at[:, me], local_sem)
        mine.start()
        sends = []
        for mask in range(1, N_DEV):
            peer, _ = _peer(mask)
            cp = pltpu.make_async_remote_copy(
                src_ref=in_ref, dst_ref=out_ref.at[:, me],
                send_sem=send_sems.at[mask - 1], recv_sem=recv_sems.at[mask - 1],
                device_id=peer, device_id_type=MESH)
            cp.start()
            sends.append(cp)
        for mask in range(1, N_DEV):
            peer, pidx = _peer(mask)
            pltpu.make_async_remote_copy(
                src_ref=in_ref, dst_ref=out_ref.at[:, pidx],
                send_sem=send_sems.at[mask - 1], recv_sem=recv_sems.at[mask - 1],
                device_id=peer, device_id_type=MESH).wait_recv()
        for cp in sends:
            cp.wait_send()
        mine.wait()

    return pl.pallas_call(
        body, name=name,
        out_shape=_sds((s, N_DEV, r, c), shard.dtype),
        in_specs=[ANY], out_specs=ANY,
        scratch_shapes=[pltpu.SemaphoreType.DMA((N_DEV - 1,)),
                        pltpu.SemaphoreType.DMA((N_DEV - 1,)),
                        pltpu.SemaphoreType.DMA],
    )(shard)


def all_to_all(parts, name):
    s, _, r, c = parts.shape

    def body(in_ref, out_ref, send_sems, recv_sems, local_sem):
        me = _my_index()
        mine = pltpu.make_async_copy(in_ref.at[:, me], out_ref.at[me], local_sem)
        mine.start()
        sends = []
        for mask in range(1, N_DEV):
            peer, pidx = _peer(mask)
            cp = pltpu.make_async_remote_copy(
                src_ref=in_ref.at[:, pidx], dst_ref=out_ref.at[me],
                send_sem=send_sems.at[mask - 1], recv_sem=recv_sems.at[mask - 1],
                device_id=peer, device_id_type=MESH)
            cp.start()
            sends.append(cp)
        for mask in range(1, N_DEV):
            peer, pidx = _peer(mask)
            pltpu.make_async_remote_copy(
                src_ref=in_ref.at[:, me], dst_ref=out_ref.at[pidx],
                send_sem=send_sems.at[mask - 1], recv_sem=recv_sems.at[mask - 1],
                device_id=peer, device_id_type=MESH).wait_recv()
        for cp in sends:
            cp.wait_send()
        mine.wait()

    return pl.pallas_call(
        body, name=name,
        out_shape=_sds((N_DEV, s, r, c), parts.dtype),
        in_specs=[ANY], out_specs=ANY,
        scratch_shapes=[pltpu.SemaphoreType.DMA((N_DEV - 1,)),
                        pltpu.SemaphoreType.DMA((N_DEV - 1,)),
                        pltpu.SemaphoreType.DMA],
    )(parts)


HBM_SPEC = pl.BlockSpec(memory_space=pltpu.HBM)
SEM_SPEC = pl.BlockSpec(memory_space=pltpu.SEMAPHORE)
EFFECT = pltpu.SideEffectType.DATAFLOW_SIDE_EFFECTING


def _exchange_refs(mode, in_ref, land_ref, me, pidx):
    if mode == "gather":
        return in_ref, land_ref.at[:, me], land_ref.at[:, pidx]
    return in_ref.at[:, pidx], land_ref.at[me], land_ref.at[pidx]


def _landing_shape(mode, a):
    if mode == "gather":
        s, r, c = a.shape
        return (s, N_DEV, r, c)
    s, _, r, c = a.shape
    return (N_DEV, s, r, c)


def _own_copy(mode, in_ref, land_ref, me, sem):
    if mode == "gather":
        return pltpu.make_async_copy(in_ref, land_ref.at[:, me], sem)
    return pltpu.make_async_copy(in_ref.at[:, me], land_ref.at[me], sem)


def exchange_start(mode, arrays, name, deps=()):
    n = len(arrays)
    lands0 = [lax.empty(_landing_shape(mode, a), a.dtype) for a in arrays]

    def body(*refs):
        in_refs, land_refs = refs[:n], refs[n:2 * n]
        outs_at = 2 * n + len(deps)
        send_sems, recv_sems, own_sems, token = refs[outs_at], refs[outs_at + 1], refs[outs_at + 2], refs[-1]
        mine = _my_index()
        for k in range(n):
            _own_copy(mode, in_refs[k], land_refs[k], mine, own_sems.at[k]).start()
            for mask in range(1, N_DEV):
                peer, pidx = _peer(mask)
                src, dst, _ = _exchange_refs(mode, in_refs[k], land_refs[k], mine, pidx)
                sem = k * (N_DEV - 1) + mask - 1
                pltpu.make_async_remote_copy(
                    src_ref=src, dst_ref=dst, send_sem=send_sems.at[sem], recv_sem=recv_sems.at[sem],
                    device_id=peer, device_id_type=MESH).start()
        token[...] = jnp.zeros_like(token)

    nsem = n * (N_DEV - 1)
    outs = pl.pallas_call(
        body, name=name,
        out_shape=(pltpu.SemaphoreType.DMA((nsem,)), pltpu.SemaphoreType.DMA((nsem,)),
                   pltpu.SemaphoreType.DMA((n,)),
                   *[pltpu.HBM(a.shape, a.dtype) for a in arrays],
                   *[pltpu.HBM(l.shape, l.dtype) for l in lands0], _sds((8, LANES), F32)),
        in_specs=[HBM_SPEC] * (2 * n) + [ANY] * len(deps),
        out_specs=(SEM_SPEC, SEM_SPEC, SEM_SPEC, *[HBM_SPEC] * (2 * n),
                   pl.BlockSpec(memory_space=pltpu.VMEM)),
        input_output_aliases={k: 3 + k for k in range(2 * n)},
        compiler_params=pltpu.CompilerParams(has_side_effects=EFFECT),
    )(*[pltpu.with_memory_space_constraint(a, pltpu.HBM) for a in arrays],
      *[pltpu.with_memory_space_constraint(l, pltpu.HBM) for l in lands0], *deps)
    return dict(mode=mode, n=n, send=outs[0], recv=outs[1], own=outs[2], ins=outs[3:3 + n],
                lands=outs[3 + n:3 + 2 * n], token=outs[-1])


def exchange_wait(handle, after, name):
    n, mode = handle["n"], handle["mode"]

    def body(*refs):
        in_refs, land_refs = refs[:n], refs[n:2 * n]
        send_sems, recv_sems, own_sems = refs[2 * n], refs[2 * n + 1], refs[2 * n + 2]
        mine = _my_index()
        for k in range(n):
            _own_copy(mode, in_refs[k], land_refs[k], mine, own_sems.at[k]).wait()
            for mask in range(1, N_DEV):
                peer, pidx = _peer(mask)
                src, _, here = _exchange_refs(mode, in_refs[k], land_refs[k], mine, pidx)
                sem = k * (N_DEV - 1) + mask - 1
                cp = pltpu.make_async_remote_copy(
                    src_ref=src, dst_ref=here, send_sem=send_sems.at[sem], recv_sem=recv_sems.at[sem],
                    device_id=peer, device_id_type=MESH)
                cp.wait_send()
                cp.wait_recv()

    thru = (*handle["ins"], *handle["lands"])
    outs = pl.pallas_call(
        body, name=name,
        out_shape=tuple(pltpu.HBM(a.shape, a.dtype) for a in thru),
        in_specs=[HBM_SPEC] * (2 * n) + [SEM_SPEC, SEM_SPEC, SEM_SPEC, ANY],
        out_specs=tuple([HBM_SPEC] * (2 * n)),
        input_output_aliases={k: k for k in range(2 * n)},
        compiler_params=pltpu.CompilerParams(has_side_effects=EFFECT),
    )(*thru, handle["send"], handle["recv"], handle["own"], after)
    return list(outs[n:])


FAR_MASKS = (2, 4, 6)
PHASE1_MASKS = (1,) + FAR_MASKS


def gather_start(arrays, name, deps=()):
    n = len(arrays)
    n1 = len(PHASE1_MASKS)
    lands0 = [lax.empty(_landing_shape("gather", a), a.dtype) for a in arrays]

    def body(*refs):
        in_refs, land_refs = refs[:n], refs[n:2 * n]
        outs_at = 2 * n + len(deps)
        send_sems, recv_sems, own_sems, token = refs[outs_at], refs[outs_at + 1], refs[outs_at + 2], refs[-1]
        mine = _my_index()
        for k in range(n):
            _own_copy("gather", in_refs[k], land_refs[k], mine, own_sems.at[k]).start()
            for j, mask in enumerate(PHASE1_MASKS):
                peer, _ = _peer(mask)
                pltpu.make_async_remote_copy(
                    src_ref=in_refs[k], dst_ref=land_refs[k].at[:, mine],
                    send_sem=send_sems.at[k * n1 + j], recv_sem=recv_sems.at[k * n1 + j],
                    device_id=peer, device_id_type=MESH).start()
        token[...] = jnp.zeros_like(token)

    outs = pl.pallas_call(
        body, name=name,
        out_shape=(pltpu.SemaphoreType.DMA((n * n1,)), pltpu.SemaphoreType.DMA((n * n1,)),
                   pltpu.SemaphoreType.DMA((n,)),
                   *[pltpu.HBM(a.shape, a.dtype) for a in arrays],
                   *[pltpu.HBM(l.shape, l.dtype) for l in lands0], _sds((8, LANES), F32)),
        in_specs=[HBM_SPEC] * (2 * n) + [ANY] * len(deps),
        out_specs=(SEM_SPEC, SEM_SPEC, SEM_SPEC, *[HBM_SPEC] * (2 * n),
                   pl.BlockSpec(memory_space=pltpu.VMEM)),
        input_output_aliases={k: 3 + k for k in range(2 * n)},
        compiler_params=pltpu.CompilerParams(has_side_effects=EFFECT),
    )(*[pltpu.with_memory_space_constraint(a, pltpu.HBM) for a in arrays],
      *[pltpu.with_memory_space_constraint(l, pltpu.HBM) for l in lands0], *deps)
    return dict(n=n, send=outs[0], recv=outs[1], own=outs[2], ins=outs[3:3 + n],
                lands=outs[3 + n:3 + 2 * n], token=outs[-1], name=name)


def gather_relay(handle, after):
    n = handle["n"]
    n1, n2 = len(PHASE1_MASKS), len(FAR_MASKS)

    def body(*refs):
        in_refs, land_refs = refs[:n], refs[n:2 * n]
        send1, recv1 = refs[2 * n], refs[2 * n + 1]
        send2, recv2 = refs[2 * n + 3], refs[2 * n + 4]
        sibling, _ = _peer(1)
        for k in range(n):
            for j, mask in enumerate(FAR_MASKS):
                peer, pidx = _peer(mask)
                landed = land_refs[k].at[:, pidx]
                pltpu.make_async_remote_copy(
                    src_ref=in_refs[k], dst_ref=landed, send_sem=send1.at[k * n1 + 1 + j],
                    recv_sem=recv1.at[k * n1 + 1 + j], device_id=peer, device_id_type=MESH).wait_recv()
                pltpu.make_async_remote_copy(
                    src_ref=landed, dst_ref=landed, send_sem=send2.at[k * n2 + j],
                    recv_sem=recv2.at[k * n2 + j], device_id=sibling, device_id_type=MESH).start()

    thru = (*handle["ins"], *handle["lands"])
    outs = pl.pallas_call(
        body, name=handle["name"] + "_relay",
        out_shape=(pltpu.SemaphoreType.DMA((n * n2,)), pltpu.SemaphoreType.DMA((n * n2,)),
                   *[pltpu.HBM(a.shape, a.dtype) for a in thru]),
        in_specs=[HBM_SPEC] * (2 * n) + [SEM_SPEC, SEM_SPEC, ANY],
        out_specs=(SEM_SPEC, SEM_SPEC, *[HBM_SPEC] * (2 * n)),
        input_output_aliases={k: 2 + k for k in range(2 * n)},
        compiler_params=pltpu.CompilerParams(has_side_effects=EFFECT),
    )(*thru, handle["send"], handle["recv"], after)
    handle.update(send2=outs[0], recv2=outs[1], ins=outs[2:2 + n], lands=outs[2 + n:2 + 2 * n])


def gather_finish(handle, after):
    n = handle["n"]
    n1, n2 = len(PHASE1_MASKS), len(FAR_MASKS)

    def body(*refs):
        in_refs, land_refs = refs[:n], refs[n:2 * n]
        send1, recv1, own_sems, send2, recv2 = refs[2 * n:2 * n + 5]
        mine = _my_index()
        sibling, sib_idx = _peer(1)
        for k in range(n):
            _own_copy("gather", in_refs[k], land_refs[k], mine, own_sems.at[k]).wait()
            for j, mask in enumerate(PHASE1_MASKS):
                peer, pidx = _peer(mask)
                cp = pltpu.make_async_remote_copy(
                    src_ref=in_refs[k], dst_ref=land_refs[k].at[:, pidx], send_sem=send1.at[k * n1 + j],
                    recv_sem=recv1.at[k * n1 + j], device_id=peer, device_id_type=MESH)
                cp.wait_send()
                if mask == 1:
                    cp.wait_recv()
            for j, mask in enumerate(FAR_MASKS):
                _, pidx = _peer(mask)
                _, far_of_sibling = _peer(mask ^ 1)
                cp = pltpu.make_async_remote_copy(
                    src_ref=land_refs[k].at[:, pidx], dst_ref=land_refs[k].at[:, far_of_sibling],
                    send_sem=send2.at[k * n2 + j], recv_sem=recv2.at[k * n2 + j],
                    device_id=sibling, device_id_type=MESH)
                cp.wait_send()
                cp.wait_recv()

    thru = (*handle["ins"], *handle["lands"])
    outs = pl.pallas_call(
        body, name=handle["name"] + "_finish",
        out_shape=tuple(pltpu.HBM(a.shape, a.dtype) for a in thru),
        in_specs=[HBM_SPEC] * (2 * n) + [SEM_SPEC] * 5 + [ANY],
        out_specs=tuple([HBM_SPEC] * (2 * n)),
        input_output_aliases={k: k for k in range(2 * n)},
        compiler_params=pltpu.CompilerParams(has_side_effects=EFFECT),
    )(*thru, handle["send"], handle["recv"], handle["own"], handle["send2"], handle["recv2"], after)
    return list(outs[n:])


def all_reduce_small(packed, name, deps=()):
    rows = packed.shape[0]
    nd = len(deps)

    def body(in_ref, *rest):
        out_ref, slots, send_sems, recv_sems = rest[nd:]
        me = _my_index()
        sends = []
        for mask in range(1, N_DEV):
            peer, _ = _peer(mask)
            cp = pltpu.make_async_remote_copy(
                src_ref=in_ref, dst_ref=slots.at[me],
                send_sem=send_sems.at[mask - 1], recv_sem=recv_sems.at[mask - 1],
                device_id=peer, device_id_type=MESH)
            cp.start()
            sends.append(cp)
        slots[me] = in_ref[...]
        for mask in range(1, N_DEV):
            peer, pidx = _peer(mask)
            pltpu.make_async_remote_copy(
                src_ref=in_ref, dst_ref=slots.at[pidx],
                send_sem=send_sems.at[mask - 1], recv_sem=recv_sems.at[mask - 1],
                device_id=peer, device_id_type=MESH).wait_recv()
        for cp in sends:
            cp.wait_send()
        total = slots[0]
        for p in range(1, N_DEV):
            total = total + slots[p]
        out_ref[...] = total

    return pl.pallas_call(
        body, name=name,
        out_shape=_sds((rows, LANES), F32),
        in_specs=[pl.BlockSpec(memory_space=pltpu.VMEM)] + [ANY] * nd,
        out_specs=pl.BlockSpec(memory_space=pltpu.VMEM),
        scratch_shapes=[pltpu.VMEM((N_DEV, rows, LANES), F32),
                        pltpu.SemaphoreType.DMA((N_DEV - 1,)),
                        pltpu.SemaphoreType.DMA((N_DEV - 1,))],
    )(packed, *deps)


def matmul(dims, a, b, out_sds, grid, a_spec, b_spec, o_spec, acc_shape, *, name, alpha=1.0,
           bias=None, bias_spec=None, scale=None, scale_spec=None, res=None, res_spec=None,
           colsum_sds=None, colsum_spec=None, out_t_sds=None, out_t_spec=None, deps=()):
    nk = grid[2]
    has_bias, has_scale, has_res = bias is not None, scale is not None, res is not None
    has_cs, has_t = colsum_sds is not None, out_t_sds is not None
    if has_cs:
        assert grid[0] == 1 and dims == TN

    def body(*refs):
        a_ref, b_ref = refs[0], refs[1]
        pos = 2
        bias_ref = scale_ref = res_ref = cs_ref = ot_ref = None
        if has_bias:
            bias_ref = refs[pos]; pos += 1
        if has_scale:
            scale_ref = refs[pos]; pos += 1
        if has_res:
            res_ref = refs[pos]; pos += 1
        pos += len(deps)
        o_ref = refs[pos]; pos += 1
        if has_cs:
            cs_ref = refs[pos]; pos += 1
        if has_t:
            ot_ref = refs[pos]; pos += 1
        k = pl.program_id(2)
        bval = b_ref[...]
        part = lax.dot_general(a_ref[...].astype(BF16), bval.astype(BF16), dims,
                               preferred_element_type=F32)

        def finish(total):
            r = total * alpha if alpha != 1.0 else total
            if has_bias:
                r = r + bias_ref[...]
            if has_scale:
                r = r * scale_ref[...]
            if has_res:
                r = r + res_ref[...].astype(F32)
            o_ref[...] = r.astype(o_ref.dtype)
            if has_t:
                ot_ref[...] = r.T.astype(ot_ref.dtype)

        if has_cs:
            csum = jnp.sum(bval.astype(F32), axis=0, keepdims=True)

            @pl.when(k == 0)
            def _():
                cs_ref[...] = csum

            @pl.when(k > 0)
            def _():
                cs_ref[...] += csum

        if nk == 1:
            finish(part)
        else:
            acc_ref = refs[pos]

            @pl.when(k == 0)
            def _():
                acc_ref[...] = part

            @pl.when(k > 0)
            def _():
                acc_ref[...] += part

            @pl.when(k == nk - 1)
            def _():
                finish(acc_ref[...])

    in_specs, args = [a_spec, b_spec], [a, b]
    if has_bias:
        in_specs.append(bias_spec); args.append(bias)
    if has_scale:
        in_specs.append(scale_spec); args.append(scale)
    if has_res:
        in_specs.append(res_spec); args.append(res)
    in_specs += [ANY] * len(deps)
    args += list(deps)
    out_shape, out_specs = [out_sds], [o_spec]
    if has_cs:
        out_shape.append(colsum_sds); out_specs.append(colsum_spec)
    if has_t:
        out_shape.append(out_t_sds); out_specs.append(out_t_spec)
    scratch = [] if nk == 1 else [pltpu.VMEM(acc_shape, F32)]
    outs = pl.pallas_call(
        body, name=name, grid=grid, in_specs=in_specs, out_specs=out_specs, out_shape=out_shape,
        scratch_shapes=scratch, compiler_params=_cparams(3))(*args)
    return outs if (has_cs or has_t) else outs[0]


def _sigmoid(z):
    return 1.0 / (1.0 + jnp.exp(-z))


def _log_sigmoid(z):
    return jnp.minimum(z, 0.0) - jnp.log(1.0 + jnp.exp(-jnp.abs(z)))


def rmsnorm_fwd(x, gain, tm, name, deps=()):
    t, d = x.shape

    def body(x_ref, g_ref, *rest):
        o_ref = rest[-1]
        xf = x_ref[...]
        r = lax.rsqrt(jnp.mean(xf * xf, axis=-1, keepdims=True) + RMS_EPS)
        o_ref[...] = (xf * r * g_ref[...]).astype(o_ref.dtype)

    return pl.pallas_call(
        body, name=name, grid=(t // tm,),
        in_specs=[pl.BlockSpec((tm, d), lambda i: (i, 0)), pl.BlockSpec((1, d), lambda i: (0, 0))]
        + [ANY] * len(deps),
        out_specs=pl.BlockSpec((tm, d), lambda i: (i, 0)),
        out_shape=_sds((t, d), BF16), compiler_params=_cparams(1))(x, gain, *deps)


def rmsnorm_bwd(x, gain, dh, dres, tm, name):
    t, d = x.shape

    def body(x_ref, g_ref, dh_ref, dres_ref, dx_ref, dg_ref):
        i = pl.program_id(0)
        xf = x_ref[...]
        r = lax.rsqrt(jnp.mean(xf * xf, axis=-1, keepdims=True) + RMS_EPS)
        xhat = xf * r
        dh_v = dh_ref[...]
        dxhat = dh_v * g_ref[...]
        dx = r * (dxhat - xhat * jnp.mean(dxhat * xhat, axis=-1, keepdims=True))
        dx_ref[...] = dres_ref[...] + dx
        dg = jnp.sum(dh_v * xhat, axis=0, keepdims=True)

        @pl.when(i == 0)
        def _():
            dg_ref[...] = dg

        @pl.when(i > 0)
        def _():
            dg_ref[...] += dg

    row = pl.BlockSpec((tm, d), lambda i: (i, 0))
    vec = pl.BlockSpec((1, d), lambda i: (0, 0))
    return pl.pallas_call(
        body, name=name, grid=(t // tm,), in_specs=[row, vec, row, row], out_specs=[row, vec],
        out_shape=[_sds((t, d), F32), _sds((1, d), F32)], compiler_params=_cparams(1))(x, gain, dh, dres)


def loss_head(x, gain, target, tm, name):
    t, d = x.shape

    def body(x_ref, g_ref, tgt_ref, dx_ref, dg_ref, loss_ref):
        i = pl.program_id(0)
        xf = x_ref[...]
        g = g_ref[...]
        r = lax.rsqrt(jnp.mean(xf * xf, axis=-1, keepdims=True) + RMS_EPS)
        xhat = xf * r
        err = xhat * g - tgt_ref[...]
        part = 0.5 * jnp.sum(jnp.mean(err * err, axis=-1, keepdims=True))
        dy = err * (1.0 / d)
        dxhat = dy * g
        dx_ref[...] = r * (dxhat - xhat * jnp.mean(dxhat * xhat, axis=-1, keepdims=True))
        dg = jnp.sum(dy * xhat, axis=0, keepdims=True)
        lpart = jnp.full((8, LANES), part, F32)

        @pl.when(i == 0)
        def _():
            dg_ref[...] = dg
            loss_ref[...] = lpart

        @pl.when(i > 0)
        def _():
            dg_ref[...] += dg
            loss_ref[...] += lpart

    row = pl.BlockSpec((tm, d), lambda i: (i, 0))
    vec = pl.BlockSpec((1, d), lambda i: (0, 0))
    return pl.pallas_call(
        body, name=name, grid=(t // tm,), in_specs=[row, vec, row],
        out_specs=[row, vec, pl.BlockSpec((8, LANES), lambda i: (0, 0))],
        out_shape=[_sds((t, d), F32), _sds((1, d), F32), _sds((8, LANES), F32)],
        compiler_params=_cparams(1))(x, gain, target)


def ffn_in_swiglu(hn, wa, s, tm, name):
    t = hn.shape[0]

    def body(h_ref, wg_ref, wu_ref, gu_ref, act_ref):
        h = h_ref[...]
        g = jnp.dot(h, wg_ref[...], preferred_element_type=F32)
        u = jnp.dot(h, wu_ref[...], preferred_element_type=F32)
        gu_ref[0] = g.astype(gu_ref.dtype)
        gu_ref[1] = u.astype(gu_ref.dtype)
        act_ref[...] = (g * _sigmoid(g) * u).astype(act_ref.dtype)

    return pl.pallas_call(
        body, name=name, grid=(t // tm, 4),
        in_specs=[pl.BlockSpec((tm, D_MODEL), lambda i, j: (i, 0)),
                  pl.BlockSpec((None, None, D_MODEL, FF_BLK), lambda i, j: (s, j, 0, 0)),
                  pl.BlockSpec((None, None, D_MODEL, FF_BLK), lambda i, j: (s, j + 4, 0, 0))],
        out_specs=[pl.BlockSpec((None, 2, tm, FF_BLK), lambda i, j: (j, 0, i, 0)),
                   pl.BlockSpec((None, tm, FF_BLK), lambda i, j: (j, i, 0))],
        out_shape=[_sds((4, 2, t, FF_BLK), BF16), _sds((4, t, FF_BLK), BF16)],
        compiler_params=_cparams(2))(hn, wa, wa)


def ffn_dact_swiglu(dy, wb, gu, s, tm, name):
    t = dy.shape[0]

    def body(dy_ref, w_ref, gu_ref, o_ref):
        da = 0.5 * lax.dot_general(dy_ref[...].astype(BF16), w_ref[...], NT, preferred_element_type=F32)
        g = gu_ref[0].astype(F32)
        u = gu_ref[1].astype(F32)
        sg = _sigmoid(g)
        o_ref[0] = (da * u * (sg * (1.0 + g * (1.0 - sg)))).astype(o_ref.dtype)
        o_ref[1] = (da * g * sg).astype(o_ref.dtype)

    blk = pl.BlockSpec((None, 2, tm, FF_BLK), lambda i, j: (j, 0, i, 0))
    return pl.pallas_call(
        body, name=name, grid=(t // tm, 4),
        in_specs=[pl.BlockSpec((tm, D_MODEL), lambda i, j: (i, 0)),
                  pl.BlockSpec((None, None, FF_BLK, D_MODEL), lambda i, j: (s, j, 0, 0)), blk],
        out_specs=blk, out_shape=_sds((4, 2, t, FF_BLK), BF16),
        compiler_params=_cparams(2))(dy, wb, gu)


def ffn_out_residual(act, wb, x, s, tm, name):
    t = x.shape[0]

    def body(a_ref, w_ref, x_ref, o_ref):
        acc = jnp.dot(a_ref[0], w_ref[0], preferred_element_type=F32)
        for k in range(1, 4):
            acc = acc + jnp.dot(a_ref[k], w_ref[k], preferred_element_type=F32)
        o_ref[...] = x_ref[...] + 0.5 * acc

    row = pl.BlockSpec((tm, D_MODEL), lambda i: (i, 0))
    return pl.pallas_call(
        body, name=name, grid=(t // tm,),
        in_specs=[pl.BlockSpec((4, tm, FF_BLK), lambda i: (0, i, 0)),
                  pl.BlockSpec((None, 4, FF_BLK, D_MODEL), lambda i: (s, 0, 0, 0)), row],
        out_specs=row, out_shape=_sds((t, D_MODEL), F32), compiler_params=_cparams(1))(act, wb, x)


def ffn_dh(dgu, wa, s, tm, name, deps):
    t = dgu.shape[2]

    def body(g_ref, w_ref, *rest):
        o_ref = rest[-1]
        acc = lax.dot_general(g_ref[0, 0], w_ref[0], NT, preferred_element_type=F32)
        for p in range(1, N_DEV):
            acc = acc + lax.dot_general(g_ref[p % 4, p // 4], w_ref[p], NT, preferred_element_type=F32)
        o_ref[...] = acc

    return pl.pallas_call(
        body, name=name, grid=(t // tm,),
        in_specs=[pl.BlockSpec((4, 2, tm, FF_BLK), lambda i: (0, 0, i, 0)),
                  pl.BlockSpec((None, N_DEV, D_MODEL, FF_BLK), lambda i: (s, 0, 0, 0))] + [ANY] * len(deps),
        out_specs=pl.BlockSpec((tm, D_MODEL), lambda i: (i, 0)),
        out_shape=_sds((t, D_MODEL), F32), compiler_params=_cparams(1))(dgu, wa, *deps)


def merge_fwd(gates, ya, yb, yc, tm, name):
    t, d = ya.shape

    def body(ga_ref, gb_ref, gc_ref, ya_ref, yb_ref, yc_ref, o_ref):
        m = (_sigmoid(ga_ref[...]) * ya_ref[...] + _sigmoid(gb_ref[...]) * yb_ref[...]
             + _sigmoid(gc_ref[...]) * yc_ref[...])
        o_ref[...] = m.astype(o_ref.dtype)

    row = pl.BlockSpec((tm, d), lambda i: (i, 0))
    gspecs = [pl.BlockSpec((tm, d), functools.partial(lambda i, a: (i, a), a=a)) for a in range(3)]
    return pl.pallas_call(
        body, name=name, grid=(t // tm,), in_specs=gspecs + [row, row, row], out_specs=row,
        out_shape=_sds((t, d), BF16), compiler_params=_cparams(1))(gates, gates, gates, ya, yb, yc)


def merge_bwd(dm, gates, ya, yb, yc, tm, name):
    t, d = ya.shape

    def body(dm_ref, g_ref, y_ref, dg_ref, dy_ref):
        dmv = dm_ref[...]
        s = _sigmoid(g_ref[...])
        dy_ref[...] = (dmv * s).astype(dy_ref.dtype)
        dg_ref[...] = (dmv * y_ref[...] * s * (1.0 - s)).astype(dg_ref.dtype)

    outs = []
    dgs = []
    for a, y in enumerate((ya, yb, yc)):
        row = pl.BlockSpec((tm, d), lambda i: (i, 0))
        gspec = pl.BlockSpec((tm, d), functools.partial(lambda i, a: (i, a), a=a))
        dg, dy = pl.pallas_call(
            functools.partial(body), name=f"{name}_{a}", grid=(t // tm,),
            in_specs=[row, gspec, row], out_specs=[row, row],
            out_shape=[_sds((t, d), BF16), _sds((t, d), BF16)],
            compiler_params=_cparams(1))(dm, gates, y)
        dgs.append(dg)
        outs.append(dy)
    return dgs, outs


def _iota2(shape, dim):
    return lax.broadcasted_iota(jnp.int32, shape, dim)


def forget_cumsum(f, name):
    t = f.shape[0]
    nq = t // QB

    def body(f_ref, fcol_ref, frow_ref, carry):
        j = pl.program_id(0)

        @pl.when(j == 0)
        def _():
            carry[...] = jnp.zeros_like(carry)

        logf = _log_sigmoid(f_ref[...])
        tri = (_iota2((QB, QB), 1) <= _iota2((QB, QB), 0)).astype(F32)
        blk = jnp.dot(tri, logf, precision=HIGHEST, preferred_element_type=F32) + carry[...]
        carry[...] += jnp.sum(logf, axis=0, keepdims=True)
        fcol_ref[...] = blk
        frow_ref[...] = blk.T[0:8, :]

    return pl.pallas_call(
        body, name=name, grid=(nq,),
        in_specs=[pl.BlockSpec((QB, LANES), lambda j: (j, 0))],
        out_specs=[pl.BlockSpec((QB, LANES), lambda j: (j, 0)),
                   pl.BlockSpec((None, 8, QB), lambda j: (j, 0, 0))],
        out_shape=[_sds((t, LANES), F32), _sds((nq, 8, QB), F32)],
        scratch_shapes=[pltpu.VMEM((1, LANES), F32)], compiler_params=_cparams(1))(f)


def forget_cumsum_bwd(dfrow, f, name):
    t = f.shape[0]
    nq = t // QB

    def body(dfr_ref, f_ref, df_ref, carry):
        jj = pl.program_id(0)

        @pl.when(jj == 0)
        def _():
            carry[...] = jnp.zeros_like(carry)

        padded = jnp.concatenate([dfr_ref[...], jnp.zeros((QB - 8, QB), F32)], axis=0)
        dfcol = padded.T
        tri = (_iota2((QB, QB), 1) >= _iota2((QB, QB), 0)).astype(F32)
        dlogf = jnp.dot(tri, dfcol, precision=HIGHEST, preferred_element_type=F32) + carry[...]
        carry[...] += jnp.sum(dfcol, axis=0, keepdims=True)
        df_ref[...] = dlogf * _sigmoid(-f_ref[...])

    return pl.pallas_call(
        body, name=name, grid=(nq,),
        in_specs=[pl.BlockSpec((None, 8, QB), lambda jj: (nq - 1 - jj, 0, 0)),
                  pl.BlockSpec((QB, LANES), lambda jj: (nq - 1 - jj, 0))],
        out_specs=pl.BlockSpec((QB, LANES), lambda jj: (nq - 1 - jj, 0)),
        out_shape=_sds((t, LANES), F32),
        scratch_shapes=[pltpu.VMEM((1, LANES), F32)], compiler_params=_cparams(1))(dfrow, f)


REL_DIAG = 768
REL_SHIFT = REL_DIAG - (QB - 1)


def _diag_onehot():
    u = _iota2((REL_PAD, REL_DIAG), 1)
    rel = jnp.clip(CH_KEYS - 1 - u, -MAX_REL, MAX_REL) + MAX_REL
    return (_iota2((REL_PAD, REL_DIAG), 0) == rel).astype(F32)


def rel_bias_build(tab_t, name):
    def body(tab_ref, o_ref):
        diag = jnp.dot(tab_ref[...], _diag_onehot(), precision=HIGHEST, preferred_element_type=F32)
        for h in range(N_HEADS_CH):
            rows = jnp.broadcast_to(diag[h:h + 1, :], (QB, REL_DIAG))
            o_ref[h] = pltpu.roll(rows, REL_SHIFT, 1, stride=1, stride_axis=0)[:, :CH_KEYS]

    return pl.pallas_call(
        body, name=name, out_shape=_sds((N_HEADS_CH, QB, CH_KEYS), F32),
        in_specs=[pl.BlockSpec(memory_space=pltpu.VMEM)], out_specs=pl.BlockSpec(memory_space=pltpu.VMEM),
    )(tab_t)


def rel_bias_scatter(dbias, name):
    def body(db_ref, o_ref, ddiag):
        flip = (_iota2((QB, QB), 0) + _iota2((QB, QB), 1) == QB - 1).astype(F32)
        for h in range(N_HEADS_CH):
            padded = jnp.concatenate([db_ref[h], jnp.zeros((QB, REL_DIAG - CH_KEYS), F32)], axis=1)
            flipped = jnp.dot(flip, padded, precision=HIGHEST, preferred_element_type=F32)
            unrolled = pltpu.roll(flipped, 0, 1, stride=1, stride_axis=0)
            ddiag[h:h + 1, :] = jnp.sum(unrolled, axis=0, keepdims=True)
        o_ref[...] = lax.dot_general(ddiag[...], _diag_onehot(), NT, precision=HIGHEST,
                                     preferred_element_type=F32)

    return pl.pallas_call(
        body, name=name, out_shape=_sds((N_HEADS_CH, REL_PAD), F32),
        in_specs=[pl.BlockSpec(memory_space=pltpu.VMEM)], out_specs=pl.BlockSpec(memory_space=pltpu.VMEM),
        scratch_shapes=[pltpu.VMEM((N_HEADS_CH, REL_DIAG), F32)],
    )(dbias)


def _hl(h):
    return slice(h * HEAD_DIM, (h + 1) * HEAD_DIM)


def _split_dot(x, tri_bf16):
    hi = x.astype(BF16)
    lo = (x - hi.astype(F32)).astype(BF16)
    return (jnp.dot(hi, tri_bf16, preferred_element_type=F32)
            + jnp.dot(lo, tri_bf16, preferred_element_type=F32))


def _rows(j):
    return pl.ds(pl.multiple_of(j * QB, QB), QB)


def _krows(g):
    return pl.ds(pl.multiple_of(g * KB, KB), KB)


def _log_sigmoid_pair(z):
    sp = jnp.log(1.0 + jnp.exp(-jnp.abs(z)))
    return jnp.minimum(z, 0.0) - sp, -jnp.maximum(z, 0.0) - sp


def _qkv_specs(t, col0, n_pairs):
    q_spec = pl.BlockSpec((QB, LANES), lambda hp, i: (i, col0 + hp))
    k_spec = pl.BlockSpec((t, LANES), lambda hp, i: (0, col0 + n_pairs + hp))
    v_spec = pl.BlockSpec((t, LANES), lambda hp, i: (0, col0 + 2 * n_pairs + hp))
    return q_spec, k_spec, v_spec


def _keys_major(xt):
    pairs, groups, _, _ = xt.shape
    return xt.transpose(1, 3, 0, 2).reshape(groups * KB, pairs * LANES)


def sb_fwd(qkv, name):
    t = qkv.shape[0]
    nq = t // QB

    def body(q_ref, k_ref, v_ref, o_ref, w_ref):
        i = pl.program_id(1)
        groups = i // KSUB + 1
        tri_after = (_iota2((KB, KB), 0) > _iota2((KB, KB), 1)).astype(BF16)
        t_idx = i * QB + _iota2((QB, KB), 0)
        qs = [q_ref[:, _hl(h)] for h in range(2)]

        def step(g, carry, masked):
            strict = (g * KB + _iota2((QB, KB), 1)) < t_idx
            out = []
            for h in range(2):
                tail, acc = carry[2 * h], carry[2 * h + 1]
                k = k_ref[_krows(g), _hl(h)]
                v = v_ref[_krows(g), _hl(h)]
                z = lax.dot_general(qs[h], k, NT, preferred_element_type=F32)
                lb, lf = _log_sigmoid_pair(z)
                if masked:
                    lf = jnp.where(strict, lf, 0.0)
                between = _split_dot(lf, tri_after) + tail
                w = jnp.exp(lb + between)
                if masked:
                    w = jnp.where(strict, w, 0.0)
                w = w.astype(BF16)
                w_ref[h, g] = w
                acc = acc + jnp.dot(w, v, preferred_element_type=F32)
                out += [tail + jnp.sum(lf, axis=1, keepdims=True), acc]
            return tuple(out)

        init = (jnp.zeros((QB, 1), F32), jnp.zeros((QB, HEAD_DIM), F32)) * 2
        res = step(groups - 1, init, True)
        res = lax.fori_loop(0, groups - 1, lambda gg, c: step(groups - 2 - gg, c, False), res)
        for h in range(2):
            o_ref[:, _hl(h)] = res[2 * h + 1].astype(o_ref.dtype)

    q_spec, k_spec, v_spec = _qkv_specs(t, 0, 2)
    return pl.pallas_call(
        body, name=name, grid=(2, nq), in_specs=[q_spec, k_spec, v_spec],
        out_specs=[pl.BlockSpec((QB, LANES), lambda hp, i: (i, hp)),
                   pl.BlockSpec((2, None, t // KB, QB, KB), lambda hp, i: (hp, i, 0, 0, 0))],
        out_shape=[_sds((t, W_SB), BF16), _sds((4, nq, t // KB, QB, KB), BF16)],
        compiler_params=_cparams(2))(qkv, qkv, qkv)


def _hs(h):
    return slice(h * HEAD_DIM, (h + 1) * HEAD_DIM)


def sb_bwd(qkv, qkv_t, w, do, do_t, name):
    t = qkv.shape[0]
    nq = t // QB

    def body(q_ref, k_ref, v_ref, do_ref, qt_ref, dot_ref, w_ref, dq_ref, dkt_ref, dvt_ref):
        i = pl.program_id(1)

        @pl.when(i == 0)
        def _():
            dkt_ref[...] = jnp.zeros_like(dkt_ref)
            dvt_ref[...] = jnp.zeros_like(dvt_ref)

        groups = i // KSUB + 1
        tri_before = (_iota2((KB, KB), 0) < _iota2((KB, KB), 1)).astype(BF16)
        t_idx = i * QB + _iota2((QB, KB), 0)
        qs = [q_ref[:, _hl(h)] for h in range(2)]
        dos = [do_ref[:, _hl(h)] for h in range(2)]
        qts = [qt_ref[_hs(h), :] for h in range(2)]
        dots = [dot_ref[_hs(h), :] for h in range(2)]

        def grads(g, carry, masked):
            strict = (g * KB + _iota2((QB, KB), 1)) < t_idx
            out = []
            for h in range(2):
                head, dq = carry[2 * h], carry[2 * h + 1]
                k = k_ref[_krows(g), _hl(h)]
                v = v_ref[_krows(g), _hl(h)]
                wb = w_ref[h, g]
                z = lax.dot_general(qs[h], k, NT, preferred_element_type=F32)
                beta = _sigmoid(z)
                e = lax.dot_general(dos[h], v, NT, preferred_element_type=F32) * wb.astype(F32)
                before = _split_dot(e, tri_before) + head
                dz = e * (1.0 - beta) - before * beta
                if masked:
                    dz = jnp.where(strict, dz, 0.0)
                dzb = dz.astype(BF16)
                dq = dq + jnp.dot(dzb, k, preferred_element_type=F32)
                dkt_ref[g, _hs(h), :] += jnp.dot(qts[h], dzb, preferred_element_type=F32)
                dvt_ref[g, _hs(h), :] += jnp.dot(dots[h], wb, preferred_element_type=F32)
                out += [head + jnp.sum(e, axis=1, keepdims=True), dq]
            return tuple(out)

        init = (jnp.zeros((QB, 1), F32), jnp.zeros((QB, HEAD_DIM), F32)) * 2
        res = lax.fori_loop(0, groups - 1, lambda g, c: grads(g, c, False), init)
        res = grads(groups - 1, res, True)
        for h in range(2):
            dq_ref[:, _hl(h)] = (res[2 * h + 1] * SCALE).astype(dq_ref.dtype)

    q_spec, k_spec, v_spec = _qkv_specs(t, 0, 2)
    blk = pl.BlockSpec((QB, LANES), lambda hp, i: (i, hp))
    blk_t = pl.BlockSpec((LANES, QB), lambda hp, i: (hp, i))
    acc_t = pl.BlockSpec((None, t // KB, LANES, KB), lambda hp, i: (hp, 0, 0, 0))
    acc_sds = _sds((2, t // KB, LANES, KB), F32)
    return pl.pallas_call(
        body, name=name, grid=(2, nq),
        in_specs=[q_spec, k_spec, v_spec, blk, blk_t, blk_t,
                  pl.BlockSpec((2, None, t // KB, QB, KB), lambda hp, i: (hp, i, 0, 0, 0))],
        out_specs=[blk, acc_t, acc_t],
        out_shape=[_sds((t, W_SB), BF16), acc_sds, acc_sds],
        compiler_params=_cparams(2))(qkv, qkv, qkv, do, qkv_t, do_t, w)


def fox_fwd(qkv, fcol, frow, name):
    t = qkv.shape[0]
    nq = t // QB

    def body(q_ref, k_ref, v_ref, fc_ref, fr_ref, o_ref, lse_ref):
        hp = pl.program_id(0)
        i = pl.program_id(1)
        groups = i // KSUB + 1
        t_idx = i * QB + _iota2((QB, KB), 0)
        lane = _iota2((QB, LANES), 1)
        sub = _iota2((8, KB), 0)
        qs = [q_ref[:, _hl(h)] for h in range(2)]
        f_qs = [jnp.sum(jnp.where(lane == hp * 2 + h, fc_ref[...], 0.0), axis=1, keepdims=True)
                for h in range(2)]

        def step(g, carry, masked):
            causal = (g * KB + _iota2((QB, KB), 1)) <= t_idx
            fr = fr_ref[g]
            out = []
            for h in range(2):
                m, l, acc = carry[3 * h:3 * h + 3]
                k = k_ref[_krows(g), _hl(h)]
                v = v_ref[_krows(g), _hl(h)]
                f_k = jnp.sum(jnp.where(sub == hp * 2 + h, fr, 0.0), axis=0, keepdims=True)
                z = lax.dot_general(qs[h], k, NT, preferred_element_type=F32) + f_qs[h] - f_k
                if masked:
                    z = jnp.where(causal, z, NEG)
                m_new = jnp.maximum(m, jnp.max(z, axis=1, keepdims=True))
                p = jnp.exp(z - m_new)
                corr = jnp.exp(m - m_new)
                l = l * corr + jnp.sum(p, axis=1, keepdims=True)
                acc = acc * corr + jnp.dot(p.astype(BF16), v, preferred_element_type=F32)
                out += [m_new, l, acc]
            return tuple(out)

        init = (jnp.full((QB, 1), NEG, F32), jnp.zeros((QB, 1), F32), jnp.zeros((QB, HEAD_DIM), F32)) * 2
        res = lax.fori_loop(0, groups - 1, lambda g, c: step(g, c, False), init)
        res = step(groups - 1, res, True)
        for h in range(2):
            m, l, acc = res[3 * h:3 * h + 3]
            o_ref[:, _hl(h)] = (acc / l).astype(o_ref.dtype)
            lse_ref[:, _hl(h)] = jnp.broadcast_to(m + jnp.log(l), (QB, HEAD_DIM))

    q_spec, k_spec, v_spec = _qkv_specs(t, 18, 2)
    blk = pl.BlockSpec((QB, LANES), lambda hp, i: (i, hp))
    return pl.pallas_call(
        body, name=name, grid=(2, nq),
        in_specs=[q_spec, k_spec, v_spec, pl.BlockSpec((QB, LANES), lambda hp, i: (i, 0)),
                  pl.BlockSpec((t // KB, 8, KB), lambda hp, i: (0, 0, 0))],
        out_specs=[blk, blk],
        out_shape=[_sds((t, W_FOX), BF16), _sds((t, W_FOX), F32)],
        compiler_params=_cparams(2))(qkv, qkv, qkv, fcol, frow)


def fox_bwd(qkv, qkv_t, fcol, frow, o, lse, do, do_t, name):
    t = qkv.shape[0]
    nq = t // QB

    def body(q_ref, k_ref, v_ref, fc_ref, fr_ref, o_ref, lse_ref, do_ref, qt_ref, dot_ref,
             dq_ref, dk_ref, dv_ref, dfr_ref):
        hp = pl.program_id(0)
        i = pl.program_id(1)
        qts = [qt_ref[_hs(h), :] for h in range(2)]
        dots = [dot_ref[_hs(h), :] for h in range(2)]

        @pl.when(i == 0)
        def _():
            dk_ref[...] = jnp.zeros_like(dk_ref)
            dv_ref[...] = jnp.zeros_like(dv_ref)

        @pl.when((i == 0) & (hp == 0))
        def _():
            dfr_ref[...] = jnp.zeros_like(dfr_ref)

        groups = i // KSUB + 1
        t_idx = i * QB + _iota2((QB, KB), 0)
        lane = _iota2((QB, LANES), 1)
        sub = _iota2((8, KB), 0)
        qs = [q_ref[:, _hl(h)] for h in range(2)]
        dos = [do_ref[:, _hl(h)] for h in range(2)]
        f_qs = [jnp.sum(jnp.where(lane == hp * 2 + h, fc_ref[...], 0.0), axis=1, keepdims=True)
                for h in range(2)]
        lse_qs = [lse_ref[:, h * HEAD_DIM:h * HEAD_DIM + 1] for h in range(2)]
        deltas = [jnp.sum(dos[h].astype(F32) * o_ref[:, _hl(h)].astype(F32), axis=1, keepdims=True)
                  for h in range(2)]

        def step(g, dqs, masked):
            causal = (g * KB + _iota2((QB, KB), 1)) <= t_idx
            fr = fr_ref[g]
            out = []
            dfr = jnp.zeros((8, KB), F32)
            for h in range(2):
                k = k_ref[_krows(g), _hl(h)]
                v = v_ref[_krows(g), _hl(h)]
                f_k = jnp.sum(jnp.where(sub == hp * 2 + h, fr, 0.0), axis=0, keepdims=True)
                z = lax.dot_general(qs[h], k, NT, preferred_element_type=F32) + f_qs[h] - f_k
                p = jnp.exp(z - lse_qs[h])
                if masked:
                    p = jnp.where(causal, p, 0.0)
                dp = lax.dot_general(dos[h], v, NT, preferred_element_type=F32)
                ds = p * (dp - deltas[h])
                dsb = ds.astype(BF16)
                out.append(dqs[h] + jnp.dot(dsb, k, preferred_element_type=F32))
                dk_ref[g, _hs(h), :] += jnp.dot(qts[h], dsb, preferred_element_type=F32)
                dv_ref[g, _hs(h), :] += jnp.dot(dots[h], p.astype(BF16), preferred_element_type=F32)
                colsum = jnp.sum(ds, axis=0, keepdims=True)
                dfr = dfr + jnp.where(sub == hp * 2 + h, -colsum, 0.0)
            dfr_ref[g] += dfr
            return tuple(out)

        res = lax.fori_loop(0, groups - 1, lambda g, c: step(g, c, False),
                            (jnp.zeros((QB, HEAD_DIM), F32),) * 2)
        res = step(groups - 1, res, True)
        for h in range(2):
            dq_ref[:, _hl(h)] = (res[h] * SCALE).astype(dq_ref.dtype)

    q_spec, k_spec, v_spec = _qkv_specs(t, 18, 2)
    blk = pl.BlockSpec((QB, LANES), lambda hp, i: (i, hp))
    frs = pl.BlockSpec((t // KB, 8, KB), lambda hp, i: (0, 0, 0))
    acc_t = pl.BlockSpec((None, t // KB, LANES, KB), lambda hp, i: (hp, 0, 0, 0))
    acc_sds = _sds((2, t // KB, LANES, KB), F32)
    return pl.pallas_call(
        body, name=name, grid=(2, nq),
        in_specs=[q_spec, k_spec, v_spec, pl.BlockSpec((QB, LANES), lambda hp, i: (i, 0)), frs,
                  blk, blk, blk, pl.BlockSpec((LANES, QB), lambda hp, i: (18 + hp, i)),
                  pl.BlockSpec((LANES, QB), lambda hp, i: (hp, i))],
        out_specs=[blk, acc_t, acc_t, frs],
        out_shape=[_sds((t, W_FOX), BF16), acc_sds, acc_sds, _sds((t // KB, 8, KB), F32)],
        compiler_params=_cparams(2))(qkv, qkv, qkv, fcol, frow, o, lse, do, qkv_t, do_t)


def _frow_to_groups(frow):
    n = frow.shape[0] // KSUB
    return frow.reshape(n, KSUB, 8, QB).transpose(0, 2, 1, 3).reshape(n, 8, KB)


def _frow_from_groups(frow):
    n = frow.shape[0]
    return frow.reshape(n, 8, KSUB, QB).transpose(0, 2, 1, 3).reshape(n * KSUB, 8, QB)


def _chunk_valid(i):
    qi = _iota2((QB, CH_KEYS), 0)
    kj = _iota2((QB, CH_KEYS), 1)
    dchunk = (qi >> 6) + LEFT_CHUNKS - (kj >> 6)
    return (dchunk >= 0) & (dchunk <= LEFT_CHUNKS) & ((i - (CH_WIN - 1)) * QB + kj >= 0)


CH_PAD = (CH_WIN - 1) * QB


def _window(i):
    return pl.ds(pl.multiple_of(i * QB, QB), CH_KEYS)


def _chunk_probs(q, kw, bias, valid):
    z = lax.dot_general(q, kw, NT, preferred_element_type=F32) + bias
    z = jnp.where(valid, z, NEG)
    z = z - jnp.max(z, axis=1, keepdims=True)
    p = jnp.exp(z)
    return p / jnp.sum(p, axis=1, keepdims=True)


def _chunk_specs(t):
    q_spec = pl.BlockSpec((QB, LANES), lambda hp, i: (i, 6 + hp))
    kv_spec = pl.BlockSpec((t + CH_PAD, LANES), lambda hp, i: (0, hp))
    return q_spec, kv_spec


def chunk_fwd(qkv, kp, vp, bias, name):
    t = qkv.shape[0]
    nq = t // QB

    def body(q_ref, k_ref, v_ref, b_ref, o_ref):
        i = pl.program_id(1)
        valid = _chunk_valid(i)
        for h in range(2):
            p = _chunk_probs(q_ref[:, _hl(h)], k_ref[_window(i), _hl(h)], b_ref[h], valid)
            o_ref[:, _hl(h)] = jnp.dot(p.astype(BF16), v_ref[_window(i), _hl(h)],
                                       preferred_element_type=F32).astype(o_ref.dtype)

    q_spec, kv_spec = _chunk_specs(t)
    return pl.pallas_call(
        body, name=name, grid=(4, nq),
        in_specs=[q_spec, kv_spec, kv_spec, pl.BlockSpec((2, QB, CH_KEYS), lambda hp, i: (hp, 0, 0))],
        out_specs=pl.BlockSpec((QB, LANES), lambda hp, i: (i, hp)),
        out_shape=_sds((t, W_CH), BF16), compiler_params=_cparams(2))(qkv, kp, vp, bias)


def chunk_bwd(qkv, qkv_t, kp, vp, bias, do, do_t, name):
    t = qkv.shape[0]
    nq = t // QB

    def body(q_ref, k_ref, v_ref, b_ref, do_ref, qt_ref, dot_ref, dq_ref, dk_ref, dv_ref, db_ref):
        i = pl.program_id(1)

        @pl.when(i == 0)
        def _():
            dk_ref[...] = jnp.zeros_like(dk_ref)
            dv_ref[...] = jnp.zeros_like(dv_ref)
            db_ref[...] = jnp.zeros_like(db_ref)

        valid = _chunk_valid(i)
        for h in range(2):
            q = q_ref[:, _hl(h)]
            dov = do_ref[:, _hl(h)]
            kw = k_ref[_window(i), _hl(h)]
            p = _chunk_probs(q, kw, b_ref[h], valid)
            dp = lax.dot_general(dov, v_ref[_window(i), _hl(h)], NT, preferred_element_type=F32)
            ds = p * (dp - jnp.sum(p * dp, axis=1, keepdims=True))
            db_ref[h] += ds
            dsb = ds.astype(BF16)
            dq_ref[:, _hl(h)] = (jnp.dot(dsb, kw, preferred_element_type=F32) * SCALE).astype(dq_ref.dtype)
            dkt = jnp.dot(qt_ref[_hs(h), :], dsb, preferred_element_type=F32)
            dvt = jnp.dot(dot_ref[_hs(h), :], p.astype(BF16), preferred_element_type=F32)
            for b in range(CH_WIN):
                dk_ref[i + b, _hs(h), :] += dkt[:, b * QB:(b + 1) * QB]
                dv_ref[i + b, _hs(h), :] += dvt[:, b * QB:(b + 1) * QB]

    q_spec, kv_spec = _chunk_specs(t)
    blk = pl.BlockSpec((QB, LANES), lambda hp, i: (i, hp))
    bspec = pl.BlockSpec((2, QB, CH_KEYS), lambda hp, i: (hp, 0, 0))
    nblk = nq + CH_WIN - 1
    acc_t = pl.BlockSpec((None, nblk, LANES, QB), lambda hp, i: (hp, 0, 0, 0))
    acc_sds = _sds((4, nblk, LANES, QB), F32)
    return pl.pallas_call(
        body, name=name, grid=(4, nq),
        in_specs=[q_spec, kv_spec, kv_spec, bspec, blk,
                  pl.BlockSpec((LANES, QB), lambda hp, i: (6 + hp, i)),
                  pl.BlockSpec((LANES, QB), lambda hp, i: (hp, i))],
        out_specs=[blk, acc_t, acc_t, bspec],
        out_shape=[_sds((t, W_CH), BF16), acc_sds, acc_sds, _sds((N_HEADS_CH, QB, CH_KEYS), F32)],
        compiler_params=_cparams(2))(qkv, kp, vp, bias, do, qkv_t, do_t)


def _sum_parts(p_ref):
    total = p_ref[0].astype(F32)
    for p in range(1, p_ref.shape[0]):
        total = total + p_ref[p].astype(F32)
    return total


def sum_parts(parts, grid, p_spec, o_spec, out_sds, name):
    def body(p_ref, o_ref):
        o_ref[...] = _sum_parts(p_ref)

    return pl.pallas_call(body, name=name, grid=grid, in_specs=[p_spec], out_specs=o_spec,
                          out_shape=out_sds, compiler_params=_cparams(len(grid)))(parts)


def adamw(parts, w, m, v, grid, p_specs, w_spec, name):
    c1 = 1.0 / (1.0 - ADAM_B1 ** ADAM_STEP)
    c2 = 1.0 / (1.0 - ADAM_B2 ** ADAM_STEP)
    n = len(parts)

    def body(*refs):
        w_ref, m_ref, v_ref, g_out, d_out, m_out, v_out = refs[n:]
        g = _sum_parts(refs[0])
        for q in range(1, n):
            g = jnp.where(pl.program_id(0) == q, _sum_parts(refs[q]), g)
        m_new = ADAM_B1 * m_ref[...] + (1.0 - ADAM_B1) * g
        v_new = ADAM_B2 * v_ref[...] + (1.0 - ADAM_B2) * (g * g)
        m_hat = m_new * c1
        v_hat = v_new * c2
        g_out[...] = g
        d_out[...] = -ADAM_LR * (m_hat / (jnp.sqrt(v_hat) + ADAM_EPS) + ADAM_WD * w_ref[...])
        m_out[...] = m_new
        v_out[...] = v_new

    out = _sds(w.shape, F32)
    return pl.pallas_call(
        body, name=name, grid=grid, in_specs=[*p_specs, w_spec, w_spec, w_spec],
        out_specs=[w_spec] * 4, out_shape=[out] * 4,
        compiler_params=_cparams(len(grid)))(*parts, w, m, v)


def _ffn_fwd(x, gain, wa, wb_after, s, tm, tag, on_event, deps=()):
    t = x.shape[0]
    hn = rmsnorm_fwd(x, gain, tm, f"rms_{tag}", deps)
    gu, act = ffn_in_swiglu(hn, wa, s, min(2 * tm, t), f"ffn_in_{tag}")
    on_event("act", act)
    wb = wb_after(act)
    y = ffn_out_residual(act, wb, x, s, min(2 * tm, t), f"ffn_out_{tag}")
    return y, (hn, gu, act), wb


def _ffn_bwd(dy, x, gain, saved, wa, wb, s, tm, tag, on_grads):
    t = x.shape[0]
    hn, gu, act = saved
    dgu = ffn_dact_swiglu(dy, wb, gu, s, min(2 * tm, t), f"ffn_dact_{tag}")
    dwb = matmul(TN, act, dy, _sds((4, FF_BLK, D_MODEL), BF16), (4, 1, 1),
                 pl.BlockSpec((None, t, FF_BLK), lambda i, j, k: (i, 0, 0)),
                 pl.BlockSpec((t, D_MODEL), lambda i, j, k: (0, 0)),
                 pl.BlockSpec((None, FF_BLK, D_MODEL), lambda i, j, k: (i, 0, 0)),
                 None, name=f"ffn_dwout_{tag}", alpha=0.5)
    dwa = matmul(TN, dgu, hn, _sds((8, FF_BLK, D_MODEL), BF16), (1, 8, 1),
                 pl.BlockSpec((None, None, t, FF_BLK), lambda i, j, k: (j % 4, j // 4, 0, 0)),
                 pl.BlockSpec((t, D_MODEL), lambda i, j, k: (0, 0)),
                 pl.BlockSpec((None, FF_BLK, D_MODEL), lambda i, j, k: (j, 0, 0)),
                 None, name=f"ffn_dwin_{tag}")
    deps = on_grads(dwa, dwb)
    dhn = ffn_dh(dgu, wa, s, tm, f"ffn_dh_{tag}", deps)
    dx, dgain = rmsnorm_bwd(x, gain, dhn, dy, tm, f"rms_bwd_{tag}")
    return dx, dgain


BR_ROWS = ((0, 1), (1, 2), (3, 1))

_Q_COLUMN_SCALE = np.ones((1, QKV_WIDTH), np.float32)
for _lo, _width in ((0, W_SB), (3 * W_SB, W_CH), (3 * (W_SB + W_CH), W_FOX)):
    _Q_COLUMN_SCALE[0, _lo:_lo + _width] = SCALE


def _mixer_fwd(x, gain, wqkv, wf, wgate, late_after, bq, bf, bg, bias, layer, tm, tag, on_event):
    t = x.shape[0]
    nt = t // tm
    hm = rmsnorm_fwd(x, gain, tm, f"rms_{tag}")
    a_full = pl.BlockSpec((tm, D_MODEL), lambda i, j, k: (i, 0))
    wide_out = pl.BlockSpec((tm, D_MODEL), lambda i, j, k: (i, j))
    wide_b = pl.BlockSpec((1, D_MODEL), lambda i, j, k: (0, j))
    qkv, qkv_t = matmul(NN, hm, wqkv, _sds((t, QKV_WIDTH), BF16), (nt, 3, 1), a_full,
                        pl.BlockSpec((None, D_MODEL, D_MODEL), lambda i, j, k: (layer, 0, j)), wide_out, None,
                        name=f"proj_qkv_{tag}", bias=bq, bias_spec=wide_b,
                        scale=jnp.asarray(_Q_COLUMN_SCALE), scale_spec=wide_b,
                        out_t_sds=_sds((QKV_WIDTH, t), BF16),
                        out_t_spec=pl.BlockSpec((D_MODEL, tm), lambda i, j, k: (j, i)))
    gates = matmul(NN, hm, wgate, _sds((t, 3 * D_MODEL), F32), (nt, 3, 1), a_full,
                   pl.BlockSpec((None, D_MODEL, D_MODEL), lambda i, j, k: (layer + 1, 0,j)), wide_out,
                   None, name=f"proj_gate_{tag}", bias=bg, bias_spec=wide_b)
    f = matmul(NN, hm, wf, _sds((t, LANES), F32), (nt, 1, 1), a_full,
               pl.BlockSpec((None, D_MODEL, LANES), lambda i, j, k: (layer, 0, 0)),
               pl.BlockSpec((tm, LANES), lambda i, j, k: (i, 0)), None,
               name=f"proj_f_{tag}", bias=bf, bias_spec=pl.BlockSpec((1, LANES), lambda i, j, k: (0, 0)))
    fcol, frow = forget_cumsum(f, f"fcum_{tag}")
    frow = _frow_to_groups(frow)
    on_event("qkv", qkv)
    o_sb, w_sb = sb_fwd(qkv, f"sb_fwd_{tag}")
    on_event("o_sb", o_sb)
    kp = jnp.pad(qkv[:, 10 * LANES:14 * LANES], ((CH_PAD, 0), (0, 0)))
    vp = jnp.pad(qkv[:, 14 * LANES:18 * LANES], ((CH_PAD, 0), (0, 0)))
    o_ch = chunk_fwd(qkv, kp, vp, bias, f"chunk_fwd_{tag}")
    o_fox, lse = fox_fwd(qkv, fcol, frow, f"fox_fwd_{tag}")
    wbr, wout = late_after(o_fox)
    ys = []
    for a, (o, (r0, nr)) in enumerate(zip((o_sb, o_ch, o_fox), BR_ROWS)):
        ys.append(matmul(
            NN, o, wbr, _sds((t, D_MODEL), F32), (nt, 1, nr),
            pl.BlockSpec((tm, 256), lambda i, j, k: (i, k)),
            pl.BlockSpec((None, 256, D_MODEL), functools.partial(lambda i, j, k, r0: (layer, r0 + k, 0), r0=r0)),
            a_full, (tm, D_MODEL), name=f"branch{a}_{tag}"))
    merged = merge_fwd(gates, ys[0], ys[1], ys[2], tm, f"merge_{tag}")
    x_new = matmul(NN, merged, wout, _sds((t, D_MODEL), F32), (nt, 1, 1), a_full,
                   pl.BlockSpec((None, D_MODEL, D_MODEL), lambda i, j, k: (layer, 0, 0)), a_full, None,
                   name=f"wout_{tag}", res=x, res_spec=a_full)
    saved = (hm, qkv, gates, f, fcol, frow, o_sb, o_ch, o_fox, lse, ys, merged, kp, vp, w_sb, qkv_t)
    return x_new, saved, wbr, wout


def _mixer_bwd(dy, x, gain, saved, wqkv, wf, wgate, wbr, wout, bias, layer, tm, tag, on_grads):
    t = x.shape[0]
    nt = t // tm
    hm, qkv, gates, f, fcol, frow, o_sb, o_ch, o_fox, lse, ys, merged, kp, vp, w_sb, qkv_t = saved
    a_full = pl.BlockSpec((tm, D_MODEL), lambda i, j, k: (i, 0))
    red_row = pl.BlockSpec((tm, D_MODEL), lambda i, j, k: (k, 0))
    sq = pl.BlockSpec((D_MODEL, D_MODEL), lambda i, j, k: (0, 0))
    dmerged = matmul(NT, dy, wout, _sds((t, D_MODEL), F32), (nt, 1, 1), a_full,
                     pl.BlockSpec((None, D_MODEL, D_MODEL), lambda i, j, k: (layer, 0, 0)), a_full, None,
                     name=f"dmerged_{tag}")
    all_t = pl.BlockSpec((t, D_MODEL), lambda i, j, k: (0, 0))
    dwout = matmul(TN, merged, dy, _sds((D_MODEL, D_MODEL), BF16), (1, 1, 1), all_t, all_t, sq,
                   None, name=f"dwout_{tag}")
    dgs, dys = merge_bwd(dmerged, gates, ys[0], ys[1], ys[2], tm, f"merge_bwd_{tag}")
    dos, dos_t, dwbrs = [], [], []
    for a, (o, (r0, nr)) in enumerate(zip((o_sb, o_ch, o_fox), BR_ROWS)):
        do, do_t = matmul(
            NT, dys[a], wbr, _sds((t, nr * 256), BF16), (nt, nr, 1), a_full,
            pl.BlockSpec((None, 256, D_MODEL), functools.partial(lambda i, j, k, r0: (layer, r0 + j, 0), r0=r0)),
            pl.BlockSpec((tm, 256), lambda i, j, k: (i, j)), None, name=f"dbranch{a}_{tag}",
            out_t_sds=_sds((nr * 256, t), BF16), out_t_spec=pl.BlockSpec((256, tm), lambda i, j, k: (j, i)))
        dos.append(do)
        dos_t.append(do_t)
        dwbrs.append(matmul(
            TN, o, dys[a], _sds((nr * 256, D_MODEL), BF16), (nr, 1, 1),
            pl.BlockSpec((t, 256), lambda i, j, k: (0, i)), all_t,
            pl.BlockSpec((256, D_MODEL), lambda i, j, k: (i, 0)), None, name=f"dwbr{a}_{tag}"))
    dq_a, dk_a, dv_a = sb_bwd(qkv, qkv_t, w_sb, dos[0], dos_t[0], f"sb_bwd_{tag}")
    dk_a, dv_a = _keys_major(dk_a), _keys_major(dv_a)
    dq_b, dk_b, dv_b, dbias = chunk_bwd(qkv, qkv_t, kp, vp, bias, dos[1], dos_t[1], f"chunk_bwd_{tag}")
    dk_b, dv_b = [x[:, CH_WIN - 1:].transpose(1, 3, 0, 2).reshape(t, W_CH) for x in (dk_b, dv_b)]
    dq_c, dk_c, dv_c, dfrow = fox_bwd(qkv, qkv_t, fcol, frow, o_fox, lse, dos[2], dos_t[2], f"fox_bwd_{tag}")
    dk_c, dv_c = _keys_major(dk_c), _keys_major(dv_c)
    df = forget_cumsum_bwd(_frow_from_groups(dfrow), f, f"fcum_bwd_{tag}")
    dqkv = jnp.concatenate([p.astype(BF16) for p in
                            (dq_a, dk_a, dv_a, dq_b, dk_b, dv_b, dq_c, dk_c, dv_c)], axis=1)
    dgates = jnp.concatenate(dgs, axis=1)
    dtab = rel_bias_scatter(dbias, f"rel_scatter_{tag}")

    all_rows = pl.BlockSpec((t, D_MODEL), lambda i, j, k: (0, 0))
    wide_b = pl.BlockSpec((t, D_MODEL), lambda i, j, k: (0, j))
    wide_o = pl.BlockSpec((D_MODEL, D_MODEL), lambda i, j, k: (0, j))
    wide_cs = pl.BlockSpec((1, D_MODEL), lambda i, j, k: (0, j))
    dwqkv, dbq = matmul(TN, hm, dqkv, _sds((D_MODEL, QKV_WIDTH), BF16), (1, 3, 1), all_rows, wide_b,
                        wide_o, None, name=f"dwqkv_{tag}",
                        colsum_sds=_sds((1, QKV_WIDTH), F32), colsum_spec=wide_cs)
    dwgate, dbg = matmul(TN, hm, dgates, _sds((D_MODEL, 3 * D_MODEL), BF16), (1, 3, 1), all_rows,
                         wide_b, wide_o, None, name=f"dwgate_{tag}",
                         colsum_sds=_sds((1, 3 * D_MODEL), F32), colsum_spec=wide_cs)
    dwf, dbf = matmul(TN, hm, df, _sds((D_MODEL, LANES), BF16), (1, 1, 1), all_rows,
                      pl.BlockSpec((t, LANES), lambda i, j, k: (0, 0)),
                      pl.BlockSpec((D_MODEL, LANES), lambda i, j, k: (0, 0)), None,
                      name=f"dwf_{tag}", colsum_sds=_sds((1, LANES), F32),
                      colsum_spec=pl.BlockSpec((1, LANES), lambda i, j, k: (0, 0)))
    dwbr = jnp.concatenate(dwbrs, axis=0)
    deps = on_grads(dict(dwqkv=dwqkv, dwgate=dwgate, dwf=dwf, dwbr=dwbr, dwout=dwout))
    wide_a = pl.BlockSpec((tm, QKV_WIDTH), lambda i, j, k: (i, 0))
    dhm = matmul(NT, dqkv, wqkv, _sds((t, D_MODEL), F32), (nt, 1, 1), wide_a,
                 pl.BlockSpec((None, D_MODEL, QKV_WIDTH), lambda i, j, k: (layer, 0, 0)), a_full,
                 None, name=f"dhm_qkv_{tag}", deps=deps)
    dhm = matmul(NT, dgates, wgate, _sds((t, D_MODEL), F32), (nt, 1, 1), wide_a,
                 pl.BlockSpec((None, D_MODEL, QKV_WIDTH), lambda i, j, k: (layer + 1, 0, 0)), a_full,
                 None, name=f"dhm_gate_{tag}", res=dhm, res_spec=a_full)
    dhm = matmul(NT, df, wf, _sds((t, D_MODEL), F32), (nt, 1, 1),
                 pl.BlockSpec((tm, LANES), lambda i, j, k: (i, 0)),
                 pl.BlockSpec((None, D_MODEL, LANES), lambda i, j, k: (layer, 0, 0)), a_full, None,
                 name=f"dhm_f_{tag}", res=dhm, res_spec=a_full)
    dx, dgain = rmsnorm_bwd(x, gain, dhm, dy, tm, f"rms_bwd_{tag}")
    return dx, dict(dbq=dbq, dbg=dbg, dbf=dbf, dtab=dtab, dgain=dgain)


def _pack_small(pieces):
    flat = jnp.concatenate([p.reshape(-1).astype(F32) for p in pieces])
    flat = jnp.pad(flat, (0, SMALL_ROWS * LANES - flat.shape[0]))
    return flat.reshape(SMALL_ROWS, LANES)


def _unpack_small(packed, shapes):
    flat = packed.reshape(-1)
    out, pos = [], 0
    for shp in shapes:
        n = int(np.prod(shp))
        out.append(flat[pos:pos + n].reshape(shp))
        pos += n
    return out


def kernel(x, g_ffn1, w_ffn1_in, w_ffn1_out, g_mix, w_in, b_in, rel_bias, w_br_sb, w_br_ch, w_br_fox, w_out, g_ffn2, w_ffn2_in, w_ffn2_out, g_final, loss_target, m_g_ffn1, m_w_ffn1_in, m_w_ffn1_out, m_g_mix, m_w_in, m_b_in, m_rel_bias, m_w_br_sb, m_w_br_ch, m_w_br_fox, m_w_out, m_g_ffn2, m_w_ffn2_in, m_w_ffn2_out, m_g_final, v_g_ffn1, v_w_ffn1_in, v_w_ffn1_out, v_g_mix, v_w_in, v_b_in, v_rel_bias, v_w_br_sb, v_w_br_ch, v_w_br_fox, v_w_out, v_g_ffn2, v_w_ffn2_in, v_w_ffn2_out, v_g_final):
    t = x.shape[1]
    tm = min(512, t)
    xs = x[0]
    target = loss_target[0]
    f_lo, f_hi = QKV_WIDTH, QKV_WIDTH + N_HEADS_FOX

    def ffn_shards(w_in_, w_out_, l):
        return [w_in_[l:l + 1].astype(BF16), w_out_[l:l + 1].astype(BF16)]

    def mixer_shards(l):
        wl = w_in[l]
        return [jnp.stack([wl[:, :QKV_WIDTH], wl[:, f_hi:]]).astype(BF16),
                jnp.pad(wl[:, f_lo:f_hi], ((0, 0), (0, LANES - N_HEADS_FOX)))[None].astype(BF16),
                w_out[l:l + 1].astype(BF16),
                jnp.concatenate([w_br_sb[l], w_br_ch[l], w_br_fox[l]], axis=0)[None].astype(BF16)]

    gathers = {}
    gather_tokens = []

    def start_gather(shards, name):
        handle = gather_start(shards, name, deps=gather_tokens[-1:])
        gather_tokens.append(handle["token"])
        return handle

    def relay(handle, after):
        if "send2" not in handle:
            gather_relay(handle, after)

    relay_on = {("mix", 0, "qkv"): ("mix", 0, 1), ("mix", 0, "o_sb"): ("ffn2", 0, 0),
                ("ffn2", 0, "act"): ("ffn1", 1, 0), ("ffn1", 1, "act"): ("mix", 1, 0),
                ("mix", 1, "qkv"): ("ffn2", 1, 0)}

    def on_event(grp, l):
        def fire(event, array):
            target = relay_on.get((grp, l, event))
            if target is not None:
                relay(gathers[target[:2]][target[2]], array)
        return fire

    for l in range(DEPTH):
        for grp, shards in (("ffn1", ffn_shards(w_ffn1_in, w_ffn1_out, l)), ("mix", mixer_shards(l)),
                            ("ffn2", ffn_shards(w_ffn2_in, w_ffn2_out, l))):
            cut = len(shards) // 2
            if l == 0 and grp != "ffn2":
                gathers[(grp, l)] = (start_gather(shards[:cut], f"gather_{grp}_l{l}_a"),
                                     start_gather(shards[cut:], f"gather_{grp}_l{l}_b"))
            else:
                gathers[(grp, l)] = (start_gather(shards, f"gather_{grp}_l{l}"),)

    def gathered(key, after):
        hs = gathers[key]
        cut = hs[0]["n"]
        relay(hs[0], after)
        first = gather_finish(hs[0], after)
        if len(hs) == 1:
            return first[:cut // 2], lambda later: first[cut // 2:]

        def second(later):
            relay(hs[1], later)
            return gather_finish(hs[1], later)

        return first, second

    def ffn_weights(key, after):
        (wa_,), rest = gathered(key, after)
        return wa_, lambda later: rest(later)[0].reshape(1, 4, FF_BLK, D_MODEL)

    def mixer_weights(key, after):
        (wc_, wf_), rest = gathered(key, after)

        def late(later):
            wout_, wbr_ = rest(later)
            return (wbr_.transpose(0, 2, 1, 3).reshape(1, D_MODEL, D_MODEL), wout_.reshape(1, D_MODEL, D_MODEL))

        return wc_.reshape(2, D_MODEL, QKV_WIDTH), wf_.reshape(1, D_MODEL, LANES), late

    bq = b_in[:, None, :QKV_WIDTH]
    bf = jnp.pad(b_in[:, f_lo:f_hi], ((0, 0), (0, LANES - N_HEADS_FOX)))[:, None, :]
    bg = b_in[:, None, f_hi:]
    tab_t = jnp.pad(rel_bias.transpose(0, 2, 1), ((0, 0), (0, 0), (0, REL_PAD - N_REL)))

    h = xs
    saved = []
    weights = []
    for l in range(DEPTH):
        bias = rel_bias_build(tab_t[l], f"rel_build_l{l}").reshape(N_HEADS_CH, QB, CH_KEYS)
        x0 = h
        wa1, wb1_after = ffn_weights(("ffn1", l), x0)
        x1, s1, wb1 = _ffn_fwd(x0, g_ffn1[l:l + 1], wa1, wb1_after, 0, tm, f"ffn1_l{l}", on_event("ffn1", l),
                               deps=gather_tokens if l == 0 else ())
        wc, wf, late_after = mixer_weights(("mix", l), x1)
        x2, sm, wbr, wout = _mixer_fwd(x1, g_mix[l:l + 1], wc, wf, wc, late_after, bq[l], bf[l], bg[l],
                                       bias, 0, tm, f"mix_l{l}", on_event("mix", l))
        wa2, wb2_after = ffn_weights(("ffn2", l), x2)
        x3, s2, wb2 = _ffn_fwd(x2, g_ffn2[l:l + 1], wa2, wb2_after, 0, tm, f"ffn2_l{l}", on_event("ffn2", l))
        saved.append((x0, x1, x2, s1, sm, s2, bias))
        weights.append(((wa1, wb1), (wc, wf, wout, wbr), (wa2, wb2)))
        h = x3

    dx, dg_final, loss_blk = loss_head(h, g_final[None, :], target, tm, "loss_head")

    g_mix_l = [None] * DEPTH
    dgains = {}
    scatters = {}

    def scatter_ffn(key):
        def on_grads(dwa, dwb):
            scatters[key] = exchange_start(
                "scatter", [dwa[None], dwb.reshape(1, N_DEV, D_FF // N_DEV, D_MODEL)],
                f"scatter_{key[0]}_l{key[1]}")
            return (scatters[key]["token"],)
        return on_grads

    def scatter_mixer(key):
        def on_grads(gm):
            scatters[key] = exchange_start(
                "scatter",
                [gm["dwqkv"].reshape(1, N_DEV, LANES, QKV_WIDTH), gm["dwgate"].reshape(1, N_DEV, LANES, QKV_WIDTH),
                 gm["dwf"].reshape(1, N_DEV, LANES, LANES), gm["dwout"].reshape(1, N_DEV, LANES, D_MODEL),
                 gm["dwbr"].reshape(1, D_MODEL, N_DEV, LANES).transpose(0, 2, 1, 3)],
                f"scatter_{key[0]}_l{key[1]}")
            return (scatters[key]["token"],)
        return on_grads

    for l in reversed(range(DEPTH)):
        x0, x1, x2, s1, sm, s2, bias = saved[l]
        w1, (wc, wf, wout, wbr), w2 = weights[l]
        dx, dgains[("ffn2", l)] = _ffn_bwd(dx, x2, g_ffn2[l:l + 1], s2, *w2, 0, tm, f"ffn2_l{l}",
                                           scatter_ffn(("ffn2", l)))
        dx, g_mix_l[l] = _mixer_bwd(dx, x1, g_mix[l:l + 1], sm, wc, wf, wc, wbr, wout, bias, 0, tm,
                                    f"mix_l{l}", scatter_mixer(("mix", l)))
        dx, dgains[("ffn1", l)] = _ffn_bwd(dx, x0, g_ffn1[l:l + 1], s1, *w1, 0, tm, f"ffn1_l{l}",
                                           scatter_ffn(("ffn1", l)))

    small_shapes = []
    small_pieces = []
    small_w, small_m, small_v = [], [], []

    def add_small(piece, w, m, v):
        small_shapes.append(w.shape)
        small_pieces.append(piece)
        small_w.append(w); small_m.append(m); small_v.append(v)

    dg1 = jnp.concatenate([dgains[("ffn1", l)] for l in range(DEPTH)], axis=0)
    dgm = jnp.concatenate([g_mix_l[l]["dgain"] for l in range(DEPTH)], axis=0)
    dg2 = jnp.concatenate([dgains[("ffn2", l)] for l in range(DEPTH)], axis=0)
    db = jnp.stack([jnp.concatenate([g_mix_l[l]["dbq"][0], g_mix_l[l]["dbf"][0, :N_HEADS_FOX],
                                     g_mix_l[l]["dbg"][0]]) for l in range(DEPTH)])
    drel = jnp.stack([g_mix_l[l]["dtab"][:, :N_REL].T for l in range(DEPTH)])
    add_small(dg1, g_ffn1, m_g_ffn1, v_g_ffn1)
    add_small(dgm, g_mix, m_g_mix, v_g_mix)
    add_small(db, b_in, m_b_in, v_b_in)
    add_small(drel, rel_bias, m_rel_bias, v_rel_bias)
    add_small(dg2, g_ffn2, m_g_ffn2, v_g_ffn2)
    add_small(dg_final[0], g_final, m_g_final, v_g_final)
    loss_piece = loss_blk[0, 0:1]
    small_packed = _pack_small(small_pieces + [loss_piece])

    recv = {}
    last = ("ffn1", 0)
    for l in reversed(range(DEPTH)):
        for grp in ("ffn2", "mix", "ffn1"):
            if (grp, l) != last:
                recv[(grp, l)] = exchange_wait(scatters[(grp, l)], dx, f"scattered_{grp}_l{l}")

    def upd(parts, w, m, v, tr, name, rb0=0):
        _, r, c = w.shape
        nr = r // tr

        def p_spec(layer):
            pinned = (nr - 1) if layer == 0 else 0
            return pl.BlockSpec((N_DEV, None, tr, c),
                                lambda l, i: (0, 0, rb0 + jnp.where(l == layer, i, pinned), 0))

        return adamw(parts, w, m, v, (DEPTH, nr), [p_spec(0), p_spec(1)],
                     pl.BlockSpec((None, tr, c), lambda l, i: (l, i, 0)), name)

    def both(grp, k):
        return [recv[(grp, l)][k] for l in range(DEPTH)]

    out_rows = D_FF // N_DEV // 2
    def upd_transposed(parts, w, m, v, tr, name):
        tp = lambda a: jnp.transpose(a, (0, 2, 1))
        return [tp(o) for o in upd(parts, tp(w), tp(m), tp(v), tr, name)]

    in_rows = FF_BLK // 4
    r_ffn2_in = upd_transposed(both("ffn2", 0), w_ffn2_in, m_w_ffn2_in, v_w_ffn2_in, in_rows, "adamw_ffn2_in")
    r_ffn2_out = upd(both("ffn2", 1), w_ffn2_out, m_w_ffn2_out, v_w_ffn2_out, out_rows, "adamw_ffn2_out")
    r_out = upd(both("mix", 3), w_out, m_w_out, v_w_out, LANES, "adamw_w_out")
    r_br_sb = upd(both("mix", 4), w_br_sb, m_w_br_sb, v_w_br_sb, 256, "adamw_br_sb", rb0=0)
    r_br_ch = upd(both("mix", 4), w_br_ch, m_w_br_ch, v_w_br_ch, 256, "adamw_br_ch", rb0=1)
    r_br_fox = upd(both("mix", 4), w_br_fox, m_w_br_fox, v_w_br_fox, 256, "adamw_br_fox", rb0=3)

    def summed(parts, name):
        _, _, r, c = parts.shape
        return sum_parts(parts, (1,), pl.BlockSpec((N_DEV, None, r, c), lambda s: (0, 0, 0, 0)),
                         pl.BlockSpec((r, c), lambda s: (0, 0)), _sds((r, c), F32), name)

    g_w_in = jnp.stack([
        jnp.concatenate([summed(recv[("mix", l)][0], f"sum_wqkv_l{l}"),
                         summed(recv[("mix", l)][2], f"sum_wf_l{l}")[:, :N_HEADS_FOX],
                         summed(recv[("mix", l)][1], f"sum_wgate_l{l}")], axis=1) for l in range(DEPTH)])
    to_cols = lambda a: jnp.transpose(a, (2, 0, 1))
    n_cols = w_in.shape[2]
    col_blk = n_cols // 4
    win_spec = pl.BlockSpec((col_blk, DEPTH, LANES), lambda i: (i, 0, 0))
    r_in = adamw([to_cols(g_w_in)[None]], to_cols(w_in), to_cols(m_w_in), to_cols(v_w_in), (4,),
                 [pl.BlockSpec((1, col_blk, DEPTH, LANES), lambda i: (0, i, 0, 0))], win_spec, "adamw_w_in")
    r_in = [jnp.transpose(o, (1, 2, 0)) for o in r_in]

    recv[last] = exchange_wait(scatters[last], r_in[1], "scattered_ffn1_l0")
    r_ffn1_in = upd_transposed(both("ffn1", 0), w_ffn1_in, m_w_ffn1_in, v_w_ffn1_in, in_rows, "adamw_ffn1_in")
    r_ffn1_out = upd(both("ffn1", 1), w_ffn1_out, m_w_ffn1_out, v_w_ffn1_out, out_rows, "adamw_ffn1_out")

    small_sum = all_reduce_small(small_packed, "allreduce_small", deps=(r_ffn1_out[1],))
    n_small = sum(int(np.prod(s)) for s in small_shapes)
    loss = small_sum.reshape(-1)[n_small]
    sm_spec = pl.BlockSpec((SMALL_ROWS, LANES), lambda i: (0, 0))
    sm_out = adamw([small_sum[None]], _pack_small(small_w), _pack_small(small_m), _pack_small(small_v),
                   (1,), [pl.BlockSpec((1, SMALL_ROWS, LANES), lambda i: (0, 0, 0))], sm_spec, "adamw_small")
    sm_g, sm_d, sm_m, sm_v = [_unpack_small(o, small_shapes) for o in sm_out]

    def per_kind(k):
        small = (sm_g, sm_d, sm_m, sm_v)[k]
        return [small[0], r_ffn1_in[k], r_ffn1_out[k], small[1], r_in[k], small[2], small[3],
                r_br_sb[k], r_br_ch[k], r_br_fox[k], r_out[k], small[4], r_ffn2_in[k], r_ffn2_out[k],
                small[5]]

    return (loss, dx[None], *per_kind(0), *per_kind(1), *per_kind(2), *per_kind(3))
```

```python
import functools

import numpy as np
import jax
import jax.numpy as jnp
from jax import lax
from jax.experimental import pallas as pl
from jax.experimental.pallas import tpu as pltpu

F32 = jnp.float32
BF16 = jnp.bfloat16

N_DEV = 8
D_MODEL = 1024
DEPTH = 2
HEAD_DIM = 64
W_SB, W_CH, W_FOX = 256, 512, 256
QKV_WIDTH = 3 * (W_SB + W_CH + W_FOX)
N_HEADS_FOX = 4
N_HEADS_CH = 8
D_FF = 2816
FF_BLK = 2 * D_FF // N_DEV
CHUNK = 64
LEFT_CHUNKS = 8
MAX_REL = 128
N_REL = 2 * MAX_REL + 1
REL_PAD = 384
QB = 128
KB = 512
KSUB = KB // QB
CH_WIN = 5
CH_KEYS = CH_WIN * QB
RMS_EPS = 1e-6
NEG = -1e30
SCALE = HEAD_DIM ** -0.5
LANES = 128
VMEM_LIMIT = 56 * 1024 * 1024

ADAM_LR, ADAM_B1, ADAM_B2, ADAM_EPS, ADAM_WD, ADAM_STEP = 0.001, 0.9, 0.999, 1e-08, 0.01, 10

SMALL_ROWS = 192

MESH = pl.DeviceIdType.MESH
ANY = pl.BlockSpec(memory_space=pl.ANY)
HIGHEST = lax.Precision.HIGHEST

NN = (((1,), (0,)), ((), ()))
NT = (((1,), (1,)), ((), ()))
TN = (((0,), (0,)), ((), ()))


def _cparams(n_grid):
    return pltpu.CompilerParams(dimension_semantics=("arbitrary",) * n_grid,
                                vmem_limit_bytes=VMEM_LIMIT)


def _sds(shape, dtype):
    return jax.ShapeDtypeStruct(tuple(shape), dtype)


def _my_index():
    return 4 * lax.axis_index("x") + 2 * lax.axis_index("y") + lax.axis_index("c")


def _peer(mask):
    x, y, c = lax.axis_index("x"), lax.axis_index("y"), lax.axis_index("c")
    px = x ^ ((mask >> 2) & 1)
    py = y ^ ((mask >> 1) & 1)
    pc = c ^ (mask & 1)
    return (px, py, pc), 4 * px + 2 * py + pc


def all_gather(shard, name):
    s, r, c = shard.shape

    def body(in_ref, out_ref, send_sems, recv_sems, local_sem):
        me = _my_index()
        mine = pltpu.make_async_copy(in_ref, out_ref.at[:, me], local_sem)
        mine.start()
        sends = []
        for mask in range(1, N_DEV):
            peer, _ = _peer(mask)
            cp = pltpu.make_async_remote_copy(
                src_ref=in_ref, dst_ref=out_ref.at[:, me],
                send_sem=send_sems.at[mask - 1], recv_sem=recv_sems.at[mask - 1],
                device_id=peer, device_id_type=MESH)
            cp.start()
            sends.append(cp)
        for mask in range(1, N_DEV):
            peer, pidx = _peer(mask)
            pltpu.make_async_remote_copy(
                src_ref=in_ref, dst_ref=out_ref.at[:, pidx],
                send_sem=send_sems.at[mask - 1], recv_sem=recv_sems.at[mask - 1],
                device_id=peer, device_id_type=MESH).wait_recv()
        for cp in sends:
            cp.wait_send()
        mine.wait()

    return pl.pallas_call(
        body, name=name,
        out_shape=_sds((s, N_DEV, r, c), shard.dtype),
        in_specs=[ANY], out_specs=ANY,
        scratch_shapes=[pltpu.SemaphoreType.DMA((N_DEV - 1,)),
                        pltpu.SemaphoreType.DMA((N_DEV - 1,)),
                        pltpu.SemaphoreType.DMA],
    )(shard)


def all_to_all(parts, name):
    s, _, r, c = parts.shape

    def body(in_ref, out_ref, send_sems, recv_sems, local_sem):
        me = _my_index()
        mine = pltpu.make_async_copy(in_ref.at[:, me], out_ref.at[me], local_sem)
        mine.start()
        sends = []
        for mask in range(1, N_DEV):
            peer, pidx = _peer(mask)
            cp = pltpu.make_async_remote_copy(
                src_ref=in_ref.at[:, pidx], dst_ref=out_ref.at[me],
                send_sem=send_sems.at[mask - 1], recv_sem=recv_sems.at[mask - 1],
                device_id=peer, device_id_type=MESH)
            cp.start()
            sends.append(cp)
        for mask in range(1, N_DEV):
            peer, pidx = _peer(mask)
            pltpu.make_async_remote_copy(
                src_ref=in_ref.at[:, me], dst_ref=out_ref.at[pidx],
                send_sem=send_sems.at[mask - 1], recv_sem=recv_sems.at[mask - 1],
                device_id=peer, device_id_type=MESH).wait_recv()
        for cp in sends:
            cp.wait_send()
        mine.wait()

    return pl.pallas_call(
        body, name=name,
        out_shape=_sds((N_DEV, s, r, c), parts.dtype),
        in_specs=[ANY], out_specs=ANY,
        scratch_shapes=[pltpu.SemaphoreType.DMA((N_DEV - 1,)),
                        pltpu.SemaphoreType.DMA((N_DEV - 1,)),
                        pltpu.SemaphoreType.DMA],
    )(parts)


HBM_SPEC = pl.BlockSpec(memory_space=pltpu.HBM)
SEM_SPEC = pl.BlockSpec(memory_space=pltpu.SEMAPHORE)
EFFECT = pltpu.SideEffectType.DATAFLOW_SIDE_EFFECTING


def _exchange_refs(mode, in_ref, land_ref, me, pidx):
    if mode == "gather":
        return in_ref, land_ref.at[:, me], land_ref.at[:, pidx]
    return in_ref.at[:, pidx], land_ref.at[me], land_ref.at[pidx]


def _landing_shape(mode, a):
    if mode == "gather":
        s, r, c = a.shape
        return (s, N_DEV, r, c)
    s, _, r, c = a.shape
    return (N_DEV, s, r, c)


def _own_copy(mode, in_ref, land_ref, me, sem):
    if mode == "gather":
        return pltpu.make_async_copy(in_ref, land_ref.at[:, me], sem)
    return pltpu.make_async_copy(in_ref.at[:, me], land_ref.at[me], sem)


def exchange_start(mode, arrays, name, deps=()):
    n = len(arrays)
    lands0 = [lax.empty(_landing_shape(mode, a), a.dtype) for a in arrays]

    def body(*refs):
        in_refs, land_refs = refs[:n], refs[n:2 * n]
        outs_at = 2 * n + len(deps)
        send_sems, recv_sems, own_sems, token = refs[outs_at], refs[outs_at + 1], refs[outs_at + 2], refs[-1]
        mine = _my_index()
        for k in range(n):
            _own_copy(mode, in_refs[k], land_refs[k], mine, own_sems.at[k]).start()
            for mask in range(1, N_DEV):
                peer, pidx = _peer(mask)
                src, dst, _ = _exchange_refs(mode, in_refs[k], land_refs[k], mine, pidx)
                sem = k * (N_DEV - 1) + mask - 1
                pltpu.make_async_remote_copy(
                    src_ref=src, dst_ref=dst, send_sem=send_sems.at[sem], recv_sem=recv_sems.at[sem],
                    device_id=peer, device_id_type=MESH).start()
        token[...] = jnp.zeros_like(token)

    nsem = n * (N_DEV - 1)
    outs = pl.pallas_call(
        body, name=name,
        out_shape=(pltpu.SemaphoreType.DMA((nsem,)), pltpu.SemaphoreType.DMA((nsem,)),
                   pltpu.SemaphoreType.DMA((n,)),
                   *[pltpu.HBM(a.shape, a.dtype) for a in arrays],
                   *[pltpu.HBM(l.shape, l.dtype) for l in lands0], _sds((8, LANES), F32)),
        in_specs=[HBM_SPEC] * (2 * n) + [ANY] * len(deps),
        out_specs=(SEM_SPEC, SEM_SPEC, SEM_SPEC, *[HBM_SPEC] * (2 * n),
                   pl.BlockSpec(memory_space=pltpu.VMEM)),
        input_output_aliases={k: 3 + k for k in range(2 * n)},
        compiler_params=pltpu.CompilerParams(has_side_effects=EFFECT),
    )(*[pltpu.with_memory_space_constraint(a, pltpu.HBM) for a in arrays],
      *[pltpu.with_memory_space_constraint(l, pltpu.HBM) for l in lands0], *deps)
    return dict(mode=mode, n=n, send=outs[0], recv=outs[1], own=outs[2], ins=outs[3:3 + n],
                lands=outs[3 + n:3 + 2 * n], token=outs[-1])


def exchange_wait(handle, after, name):
    n, mode = handle["n"], handle["mode"]

    def body(*refs):
        in_refs, land_refs = refs[:n], refs[n:2 * n]
        send_sems, recv_sems, own_sems = refs[2 * n], refs[2 * n + 1], refs[2 * n + 2]
        mine = _my_index()
        for k in range(n):
            _own_copy(mode, in_refs[k], land_refs[k], mine, own_sems.at[k]).wait()
            for mask in range(1, N_DEV):
                peer, pidx = _peer(mask)
                src, _, here = _exchange_refs(mode, in_refs[k], land_refs[k], mine, pidx)
                sem = k * (N_DEV - 1) + mask - 1
                cp = pltpu.make_async_remote_copy(
                    src_ref=src, dst_ref=here, send_sem=send_sems.at[sem], recv_sem=recv_sems.at[sem],
                    device_id=peer, device_id_type=MESH)
                cp.wait_send()
                cp.wait_recv()

    thru = (*handle["ins"], *handle["lands"])
    outs = pl.pallas_call(
        body, name=name,
        out_shape=tuple(pltpu.HBM(a.shape, a.dtype) for a in thru),
        in_specs=[HBM_SPEC] * (2 * n) + [SEM_SPEC, SEM_SPEC, SEM_SPEC, ANY],
        out_specs=tuple([HBM_SPEC] * (2 * n)),
        input_output_aliases={k: k for k in range(2 * n)},
        compiler_params=pltpu.CompilerParams(has_side_effects=EFFECT),
    )(*thru, handle["send"], handle["recv"], handle["own"], after)
    return list(outs[n:])


FAR_MASKS = (2, 4, 6)
PHASE1_MASKS = (1,) + FAR_MASKS


def gather_start(arrays, name, deps=()):
    n = len(arrays)
    n1 = len(PHASE1_MASKS)
    lands0 = [lax.empty(_landing_shape("gather", a), a.dtype) for a in arrays]

    def body(*refs):
        in_refs, land_refs = refs[:n], refs[n:2 * n]
        outs_at = 2 * n + len(deps)
        send_sems, recv_sems, own_sems, token = refs[outs_at], refs[outs_at + 1], refs[outs_at + 2], refs[-1]
        mine = _my_index()
        for k in range(n):
            _own_copy("gather", in_refs[k], land_refs[k], mine, own_sems.at[k]).start()
            for j, mask in enumerate(PHASE1_MASKS):
                peer, _ = _peer(mask)
                pltpu.make_async_remote_copy(
                    src_ref=in_refs[k], dst_ref=land_refs[k].at[:, mine],
                    send_sem=send_sems.at[k * n1 + j], recv_sem=recv_sems.at[k * n1 + j],
                    device_id=peer, device_id_type=MESH).start()
        token[...] = jnp.zeros_like(token)

    outs = pl.pallas_call(
        body, name=name,
        out_shape=(pltpu.SemaphoreType.DMA((n * n1,)), pltpu.SemaphoreType.DMA((n * n1,)),
                   pltpu.SemaphoreType.DMA((n,)),
                   *[pltpu.HBM(a.shape, a.dtype) for a in arrays],
                   *[pltpu.HBM(l.shape, l.dtype) for l in lands0], _sds((8, LANES), F32)),
        in_specs=[HBM_SPEC] * (2 * n) + [ANY] * len(deps),
        out_specs=(SEM_SPEC, SEM_SPEC, SEM_SPEC, *[HBM_SPEC] * (2 * n),
                   pl.BlockSpec(memory_space=pltpu.VMEM)),
        input_output_aliases={k: 3 + k for k in range(2 * n)},
        compiler_params=pltpu.CompilerParams(has_side_effects=EFFECT),
    )(*[pltpu.with_memory_space_constraint(a, pltpu.HBM) for a in arrays],
      *[pltpu.with_memory_space_constraint(l, pltpu.HBM) for l in lands0], *deps)
    return dict(n=n, send=outs[0], recv=outs[1], own=outs[2], ins=outs[3:3 + n],
                lands=outs[3 + n:3 + 2 * n], token=outs[-1], name=name)


def gather_relay(handle, after):
    n = handle["n"]
    n1, n2 = len(PHASE1_MASKS), len(FAR_MASKS)

    def body(*refs):
        in_refs, land_refs = refs[:n], refs[n:2 * n]
        send1, recv1 = refs[2 * n], refs[2 * n + 1]
        send2, recv2 = refs[2 * n + 3], refs[2 * n + 4]
        sibling, _ = _peer(1)
        for k in range(n):
            for j, mask in enumerate(FAR_MASKS):
                peer, pidx = _peer(mask)
                landed = land_refs[k].at[:, pidx]
                pltpu.make_async_remote_copy(
                    src_ref=in_refs[k], dst_ref=landed, send_sem=send1.at[k * n1 + 1 + j],
                    recv_sem=recv1.at[k * n1 + 1 + j], device_id=peer, device_id_type=MESH).wait_recv()
                pltpu.make_async_remote_copy(
                    src_ref=landed, dst_ref=landed, send_sem=send2.at[k * n2 + j],
                    recv_sem=recv2.at[k * n2 + j], device_id=sibling, device_id_type=MESH).start()

    thru = (*handle["ins"], *handle["lands"])
    outs = pl.pallas_call(
        body, name=handle["name"] + "_relay",
        out_shape=(pltpu.SemaphoreType.DMA((n * n2,)), pltpu.SemaphoreType.DMA((n * n2,)),
                   *[pltpu.HBM(a.shape, a.dtype) for a in thru]),
        in_specs=[HBM_SPEC] * (2 * n) + [SEM_SPEC, SEM_SPEC, ANY],
        out_specs=(SEM_SPEC, SEM_SPEC, *[HBM_SPEC] * (2 * n)),
        input_output_aliases={k: 2 + k for k in range(2 * n)},
        compiler_params=pltpu.CompilerParams(has_side_effects=EFFECT),
    )(*thru, handle["send"], handle["recv"], after)
    handle.update(send2=outs[0], recv2=outs[1], ins=outs[2:2 + n], lands=outs[2 + n:2 + 2 * n])


def gather_finish(handle, after):
    n = handle["n"]
    n1, n2 = len(PHASE1_MASKS), len(FAR_MASKS)

    def body(*refs):
        in_refs, land_refs = refs[:n], refs[n:2 * n]
        send1, recv1, own_sems, send2, recv2 = refs[2 * n:2 * n + 5]
        mine = _my_index()
        sibling, sib_idx = _peer(1)
        for k in range(n):
            _own_copy("gather", in_refs[k], land_refs[k], mine, own_sems.at[k]).wait()
            for j, mask in enumerate(PHASE1_MASKS):
                peer, pidx = _peer(mask)
                cp = pltpu.make_async_remote_copy(
                    src_ref=in_refs[k], dst_ref=land_refs[k].at[:, pidx], send_sem=send1.at[k * n1 + j],
                    recv_sem=recv1.at[k * n1 + j], device_id=peer, device_id_type=MESH)
                cp.wait_send()
                if mask == 1:
                    cp.wait_recv()
            for j, mask in enumerate(FAR_MASKS):
                _, pidx = _peer(mask)
                _, far_of_sibling = _peer(mask ^ 1)
                cp = pltpu.make_async_remote_copy(
                    src_ref=land_refs[k].at[:, pidx], dst_ref=land_refs[k].at[:, far_of_sibling],
                    send_sem=send2.at[k * n2 + j], recv_sem=recv2.at[k * n2 + j],
                    device_id=sibling, device_id_type=MESH)
                cp.wait_send()
                cp.wait_recv()

    thru = (*handle["ins"], *handle["lands"])
    outs = pl.pallas_call(
        body, name=handle["name"] + "_finish",
        out_shape=tuple(pltpu.HBM(a.shape, a.dtype) for a in thru),
        in_specs=[HBM_SPEC] * (2 * n) + [SEM_SPEC] * 5 + [ANY],
        out_specs=tuple([HBM_SPEC] * (2 * n)),
        input_output_aliases={k: k for k in range(2 * n)},
        compiler_params=pltpu.CompilerParams(has_side_effects=EFFECT),
    )(*thru, handle["send"], handle["recv"], handle["own"], handle["send2"], handle["recv2"], after)
    return list(outs[n:])


def all_reduce_small(packed, name, deps=()):
    rows = packed.shape[0]
    nd = len(deps)

    def body(in_ref, *rest):
        out_ref, slots, send_sems, recv_sems = rest[nd:]
        me = _my_index()
        sends = []
        for mask in range(1, N_DEV):
            peer, _ = _peer(mask)
            cp = pltpu.make_async_remote_copy(
                src_ref=in_ref, dst_ref=slots.at[me],
                send_sem=send_sems.at[mask - 1], recv_sem=recv_sems.at[mask - 1],
                device_id=peer, device_id_type=MESH)
            cp.start()
            sends.append(cp)
        slots[me] = in_ref[...]
        for mask in range(1, N_DEV):
            peer, pidx = _peer(mask)
            pltpu.make_async_remote_copy(
                src_ref=in_ref, dst_ref=slots.at[pidx],
                send_sem=send_sems.at[mask - 1], recv_sem=recv_sems.at[mask - 1],
                device_id=peer, device_id_type=MESH).wait_recv()
        for cp in sends:
            cp.wait_send()
        total = slots[0]
        for p in range(1, N_DEV):
            total = total + slots[p]
        out_ref[...] = total

    return pl.pallas_call(
        body, name=name,
        out_shape=_sds((rows, LANES), F32),
        in_specs=[pl.BlockSpec(memory_space=pltpu.VMEM)] + [ANY] * nd,
        out_specs=pl.BlockSpec(memory_space=pltpu.VMEM),
        scratch_shapes=[pltpu.VMEM((N_DEV, rows, LANES), F32),
                        pltpu.SemaphoreType.DMA((N_DEV - 1,)),
                        pltpu.SemaphoreType.DMA((N_DEV - 1,))],
    )(packed, *deps)


def matmul(dims, a, b, out_sds, grid, a_spec, b_spec, o_spec, acc_shape, *, name, alpha=1.0,
           bias=None, bias_spec=None, scale=None, scale_spec=None, res=None, res_spec=None,
           colsum_sds=None, colsum_spec=None, out_t_sds=None, out_t_spec=None, deps=()):
    nk = grid[2]
    has_bias, has_scale, has_res = bias is not None, scale is not None, res is not None
    has_cs, has_t = colsum_sds is not None, out_t_sds is not None
    if has_cs:
        assert grid[0] == 1 and dims == TN

    def body(*refs):
        a_ref, b_ref = refs[0], refs[1]
        pos = 2
        bias_ref = scale_ref = res_ref = cs_ref = ot_ref = None
        if has_bias:
            bias_ref = refs[pos]; pos += 1
        if has_scale:
            scale_ref = refs[pos]; pos += 1
        if has_res:
            res_ref = refs[pos]; pos += 1
        pos += len(deps)
        o_ref = refs[pos]; pos += 1
        if has_cs:
            cs_ref = refs[pos]; pos += 1
        if has_t:
            ot_ref = refs[pos]; pos += 1
        k = pl.program_id(2)
        bval = b_ref[...]
        part = lax.dot_general(a_ref[...].astype(BF16), bval.astype(BF16), dims,
                               preferred_element_type=F32)

        def finish(total):
            r = total * alpha if alpha != 1.0 else total
            if has_bias:
                r = r + bias_ref[...]
            if has_scale:
                r = r * scale_ref[...]
            if has_res:
                r = r + res_ref[...].astype(F32)
            o_ref[...] = r.astype(o_ref.dtype)
            if has_t:
                ot_ref[...] = r.T.astype(ot_ref.dtype)

        if has_cs:
            csum = jnp.sum(bval.astype(F32), axis=0, keepdims=True)

            @pl.when(k == 0)
            def _():
                cs_ref[...] = csum

            @pl.when(k > 0)
            def _():
                cs_ref[...] += csum

        if nk == 1:
            finish(part)
        else:
            acc_ref = refs[pos]

            @pl.when(k == 0)
            def _():
                acc_ref[...] = part

            @pl.when(k > 0)
            def _():
                acc_ref[...] += part

            @pl.when(k == nk - 1)
            def _():
                finish(acc_ref[...])

    in_specs, args = [a_spec, b_spec], [a, b]
    if has_bias:
        in_specs.append(bias_spec); args.append(bias)
    if has_scale:
        in_specs.append(scale_spec); args.append(scale)
    if has_res:
        in_specs.append(res_spec); args.append(res)
    in_specs += [ANY] * len(deps)
    args += list(deps)
    out_shape, out_specs = [out_sds], [o_spec]
    if has_cs:
        out_shape.append(colsum_sds); out_specs.append(colsum_spec)
    if has_t:
        out_shape.append(out_t_sds); out_specs.append(out_t_spec)
    scratch = [] if nk == 1 else [pltpu.VMEM(acc_shape, F32)]
    outs = pl.pallas_call(
        body, name=name, grid=grid, in_specs=in_specs, out_specs=out_specs, out_shape=out_shape,
        scratch_shapes=scratch, compiler_params=_cparams(3))(*args)
    return outs if (has_cs or has_t) else outs[0]


def _sigmoid(z):
    return 1.0 / (1.0 + jnp.exp(-z))


def _log_sigmoid(z):
    return jnp.minimum(z, 0.0) - jnp.log(1.0 + jnp.exp(-jnp.abs(z)))


def rmsnorm_fwd(x, gain, tm, name, deps=()):
    t, d = x.shape

    def body(x_ref, g_ref, *rest):
        o_ref = rest[-1]
        xf = x_ref[...]
        r = lax.rsqrt(jnp.mean(xf * xf, axis=-1, keepdims=True) + RMS_EPS)
        o_ref[...] = (xf * r * g_ref[...]).astype(o_ref.dtype)

    return pl.pallas_call(
        body, name=name, grid=(t // tm,),
        in_specs=[pl.BlockSpec((tm, d), lambda i: (i, 0)), pl.BlockSpec((1, d), lambda i: (0, 0))]
        + [ANY] * len(deps),
        out_specs=pl.BlockSpec((tm, d), lambda i: (i, 0)),
        out_shape=_sds((t, d), BF16), compiler_params=_cparams(1))(x, gain, *deps)


def rmsnorm_bwd(x, gain, dh, dres, tm, name):
    t, d = x.shape

    def body(x_ref, g_ref, dh_ref, dres_ref, dx_ref, dg_ref):
        i = pl.program_id(0)
        xf = x_ref[...]
        r = lax.rsqrt(jnp.mean(xf * xf, axis=-1, keepdims=True) + RMS_EPS)
        xhat = xf * r
        dh_v = dh_ref[...]
        dxhat = dh_v * g_ref[...]
        dx = r * (dxhat - xhat * jnp.mean(dxhat * xhat, axis=-1, keepdims=True))
        dx_ref[...] = dres_ref[...] + dx
        dg = jnp.sum(dh_v * xhat, axis=0, keepdims=True)

        @pl.when(i == 0)
        def _():
            dg_ref[...] = dg

        @pl.when(i > 0)
        def _():
            dg_ref[...] += dg

    row = pl.BlockSpec((tm, d), lambda i: (i, 0))
    vec = pl.BlockSpec((1, d), lambda i: (0, 0))
    return pl.pallas_call(
        body, name=name, grid=(t // tm,), in_specs=[row, vec, row, row], out_specs=[row, vec],
        out_shape=[_sds((t, d), F32), _sds((1, d), F32)], compiler_params=_cparams(1))(x, gain, dh, dres)


def loss_head(x, gain, target, tm, name):
    t, d = x.shape

    def body(x_ref, g_ref, tgt_ref, dx_ref, dg_ref, loss_ref):
        i = pl.program_id(0)
        xf = x_ref[...]
        g = g_ref[...]
        r = lax.rsqrt(jnp.mean(xf * xf, axis=-1, keepdims=True) + RMS_EPS)
        xhat = xf * r
        err = xhat * g - tgt_ref[...]
        part = 0.5 * jnp.sum(jnp.mean(err * err, axis=-1, keepdims=True))
        dy = err * (1.0 / d)
        dxhat = dy * g
        dx_ref[...] = r * (dxhat - xhat * jnp.mean(dxhat * xhat, axis=-1, keepdims=True))
        dg = jnp.sum(dy * xhat, axis=0, keepdims=True)
        lpart = jnp.full((8, LANES), part, F32)

        @pl.when(i == 0)
        def _():
            dg_ref[...] = dg
            loss_ref[...] = lpart

        @pl.when(i > 0)
        def _():
            dg_ref[...] += dg
            loss_ref[...] += lpart

    row = pl.BlockSpec((tm, d), lambda i: (i, 0))
    vec = pl.BlockSpec((1, d), lambda i: (0, 0))
    return pl.pallas_call(
        body, name=name, grid=(t // tm,), in_specs=[row, vec, row],
        out_specs=[row, vec, pl.BlockSpec((8, LANES), lambda i: (0, 0))],
        out_shape=[_sds((t, d), F32), _sds((1, d), F32), _sds((8, LANES), F32)],
        compiler_params=_cparams(1))(x, gain, target)


def ffn_in_swiglu(hn, wa, s, tm, name):
    t = hn.shape[0]

    def body(h_ref, wg_ref, wu_ref, gu_ref, act_ref):
        h = h_ref[...]
        g = jnp.dot(h, wg_ref[...], preferred_element_type=F32)
        u = jnp.dot(h, wu_ref[...], preferred_element_type=F32)
        gu_ref[0] = g.astype(gu_ref.dtype)
        gu_ref[1] = u.astype(gu_ref.dtype)
        act_ref[...] = (g * _sigmoid(g) * u).astype(act_ref.dtype)

    return pl.pallas_call(
        body, name=name, grid=(t // tm, 4),
        in_specs=[pl.BlockSpec((tm, D_MODEL), lambda i, j: (i, 0)),
                  pl.BlockSpec((None, None, D_MODEL, FF_BLK), lambda i, j: (s, j, 0, 0)),
                  pl.BlockSpec((None, None, D_MODEL, FF_BLK), lambda i, j: (s, j + 4, 0, 0))],
        out_specs=[pl.BlockSpec((None, 2, tm, FF_BLK), lambda i, j: (j, 0, i, 0)),
                   pl.BlockSpec((None, tm, FF_BLK), lambda i, j: (j, i, 0))],
        out_shape=[_sds((4, 2, t, FF_BLK), BF16), _sds((4, t, FF_BLK), BF16)],
        compiler_params=_cparams(2))(hn, wa, wa)


def ffn_dact_swiglu(dy, wb, gu, s, tm, name):
    t = dy.shape[0]

    def body(dy_ref, w_ref, gu_ref, o_ref):
        da = 0.5 * lax.dot_general(dy_ref[...].astype(BF16), w_ref[...], NT, preferred_element_type=F32)
        g = gu_ref[0].astype(F32)
        u = gu_ref[1].astype(F32)
        sg = _sigmoid(g)
        o_ref[0] = (da * u * (sg * (1.0 + g * (1.0 - sg)))).astype(o_ref.dtype)
        o_ref[1] = (da * g * sg).astype(o_ref.dtype)

    blk = pl.BlockSpec((None, 2, tm, FF_BLK), lambda i, j: (j, 0, i, 0))
    return pl.pallas_call(
        body, name=name, grid=(t // tm, 4),
        in_specs=[pl.BlockSpec((tm, D_MODEL), lambda i, j: (i, 0)),
                  pl.BlockSpec((None, None, FF_BLK, D_MODEL), lambda i, j: (s, j, 0, 0)), blk],
        out_specs=blk, out_shape=_sds((4, 2, t, FF_BLK), BF16),
        compiler_params=_cparams(2))(dy, wb, gu)


def ffn_out_residual(act, wb, x, s, tm, name):
    t = x.shape[0]

    def body(a_ref, w_ref, x_ref, o_ref):
        acc = jnp.dot(a_ref[0], w_ref[0], preferred_element_type=F32)
        for k in range(1, 4):
            acc = acc + jnp.dot(a_ref[k], w_ref[k], preferred_element_type=F32)
        o_ref[...] = x_ref[...] + 0.5 * acc

    row = pl.BlockSpec((tm, D_MODEL), lambda i: (i, 0))
    return pl.pallas_call(
        body, name=name, grid=(t // tm,),
        in_specs=[pl.BlockSpec((4, tm, FF_BLK), lambda i: (0, i, 0)),
                  pl.BlockSpec((None, 4, FF_BLK, D_MODEL), lambda i: (s, 0, 0, 0)), row],
        out_specs=row, out_shape=_sds((t, D_MODEL), F32), compiler_params=_cparams(1))(act, wb, x)


def ffn_dh(dgu, wa, s, tm, name, deps):
    t = dgu.shape[2]

    def body(g_ref, w_ref, *rest):
        o_ref = rest[-1]
        acc = lax.dot_general(g_ref[0, 0], w_ref[0], NT, preferred_element_type=F32)
        for p in range(1, N_DEV):
            acc = acc + lax.dot_general(g_ref[p % 4, p // 4], w_ref[p], NT, preferred_element_type=F32)
        o_ref[...] = acc

    return pl.pallas_call(
        body, name=name, grid=(t // tm,),
        in_specs=[pl.BlockSpec((4, 2, tm, FF_BLK), lambda i: (0, 0, i, 0)),
                  pl.BlockSpec((None, N_DEV, D_MODEL, FF_BLK), lambda i: (s, 0, 0, 0))] + [ANY] * len(deps),
        out_specs=pl.BlockSpec((tm, D_MODEL), lambda i: (i, 0)),
        out_shape=_sds((t, D_MODEL), F32), compiler_params=_cparams(1))(dgu, wa, *deps)


def merge_fwd(gates, ya, yb, yc, tm, name):
    t, d = ya.shape

    def body(ga_ref, gb_ref, gc_ref, ya_ref, yb_ref, yc_ref, o_ref):
        m = (_sigmoid(ga_ref[...]) * ya_ref[...] + _sigmoid(gb_ref[...]) * yb_ref[...]
             + _sigmoid(gc_ref[...]) * yc_ref[...])
        o_ref[...] = m.astype(o_ref.dtype)

    row = pl.BlockSpec((tm, d), lambda i: (i, 0))
    gspecs = [pl.BlockSpec((tm, d), functools.partial(lambda i, a: (i, a), a=a)) for a in range(3)]
    return pl.pallas_call(
        body, name=name, grid=(t // tm,), in_specs=gspecs + [row, row, row], out_specs=row,
        out_shape=_sds((t, d), BF16), compiler_params=_cparams(1))(gates, gates, gates, ya, yb, yc)


def merge_bwd(dm, gates, ya, yb, yc, tm, name):
    t, d = ya.shape

    def body(dm_ref, g_ref, y_ref, dg_ref, dy_ref):
        dmv = dm_ref[...]
        s = _sigmoid(g_ref[...])
        dy_ref[...] = (dmv * s).astype(dy_ref.dtype)
        dg_ref[...] = (dmv * y_ref[...] * s * (1.0 - s)).astype(dg_ref.dtype)

    outs = []
    dgs = []
    for a, y in enumerate((ya, yb, yc)):
        row = pl.BlockSpec((tm, d), lambda i: (i, 0))
        gspec = pl.BlockSpec((tm, d), functools.partial(lambda i, a: (i, a), a=a))
        dg, dy = pl.pallas_call(
            functools.partial(body), name=f"{name}_{a}", grid=(t // tm,),
            in_specs=[row, gspec, row], out_specs=[row, row],
            out_shape=[_sds((t, d), BF16), _sds((t, d), BF16)],
            compiler_params=_cparams(1))(dm, gates, y)
        dgs.append(dg)
        outs.append(dy)
    return dgs, outs


def _iota2(shape, dim):
    return lax.broadcasted_iota(jnp.int32, shape, dim)


def forget_cumsum(f, name):
    t = f.shape[0]
    nq = t // QB

    def body(f_ref, fcol_ref, frow_ref, carry):
        j = pl.program_id(0)

        @pl.when(j == 0)
        def _():
            carry[...] = jnp.zeros_like(carry)

        logf = _log_sigmoid(f_ref[...])
        tri = (_iota2((QB, QB), 1) <= _iota2((QB, QB), 0)).astype(F32)
        blk = jnp.dot(tri, logf, precision=HIGHEST, preferred_element_type=F32) + carry[...]
        carry[...] += jnp.sum(logf, axis=0, keepdims=True)
        fcol_ref[...] = blk
        frow_ref[...] = blk.T[0:8, :]

    return pl.pallas_call(
        body, name=name, grid=(nq,),
        in_specs=[pl.BlockSpec((QB, LANES), lambda j: (j, 0))],
        out_specs=[pl.BlockSpec((QB, LANES), lambda j: (j, 0)),
                   pl.BlockSpec((None, 8, QB), lambda j: (j, 0, 0))],
        out_shape=[_sds((t, LANES), F32), _sds((nq, 8, QB), F32)],
        scratch_shapes=[pltpu.VMEM((1, LANES), F32)], compiler_params=_cparams(1))(f)


def forget_cumsum_bwd(dfrow, f, name):
    t = f.shape[0]
    nq = t // QB

    def body(dfr_ref, f_ref, df_ref, carry):
        jj = pl.program_id(0)

        @pl.when(jj == 0)
        def _():
            carry[...] = jnp.zeros_like(carry)

        padded = jnp.concatenate([dfr_ref[...], jnp.zeros((QB - 8, QB), F32)], axis=0)
        dfcol = padded.T
        tri = (_iota2((QB, QB), 1) >= _iota2((QB, QB), 0)).astype(F32)
        dlogf = jnp.dot(tri, dfcol, precision=HIGHEST, preferred_element_type=F32) + carry[...]
        carry[...] += jnp.sum(dfcol, axis=0, keepdims=True)
        df_ref[...] = dlogf * _sigmoid(-f_ref[...])

    return pl.pallas_call(
        body, name=name, grid=(nq,),
        in_specs=[pl.BlockSpec((None, 8, QB), lambda jj: (nq - 1 - jj, 0, 0)),
                  pl.BlockSpec((QB, LANES), lambda jj: (nq - 1 - jj, 0))],
        out_specs=pl.BlockSpec((QB, LANES), lambda jj: (nq - 1 - jj, 0)),
        out_shape=_sds((t, LANES), F32),
        scratch_shapes=[pltpu.VMEM((1, LANES), F32)], compiler_params=_cparams(1))(dfrow, f)


REL_DIAG = 768
REL_SHIFT = REL_DIAG - (QB - 1)


def _diag_onehot():
    u = _iota2((REL_PAD, REL_DIAG), 1)
    rel = jnp.clip(CH_KEYS - 1 - u, -MAX_REL, MAX_REL) + MAX_REL
    return (_iota2((REL_PAD, REL_DIAG), 0) == rel).astype(F32)


def rel_bias_build(tab_t, name):
    def body(tab_ref, o_ref):
        diag = jnp.dot(tab_ref[...], _diag_onehot(), precision=HIGHEST, preferred_element_type=F32)
        band = _chunk_band()
        for h in range(N_HEADS_CH):
            rows = jnp.broadcast_to(diag[h:h + 1, :], (QB, REL_DIAG))
            o_ref[h] = pltpu.roll(rows, REL_SHIFT, 1, stride=1, stride_axis=0)[:, :CH_KEYS] + band

    return pl.pallas_call(
        body, name=name, out_shape=_sds((N_HEADS_CH, QB, CH_KEYS), F32),
        in_specs=[pl.BlockSpec(memory_space=pltpu.VMEM)], out_specs=pl.BlockSpec(memory_space=pltpu.VMEM),
    )(tab_t)


def rel_bias_scatter(dbias, name):
    def body(db_ref, o_ref, ddiag):
        flip = (_iota2((QB, QB), 0) + _iota2((QB, QB), 1) == QB - 1).astype(F32)
        for h in range(N_HEADS_CH):
            padded = jnp.concatenate([db_ref[h], jnp.zeros((QB, REL_DIAG - CH_KEYS), F32)], axis=1)
            flipped = jnp.dot(flip, padded, precision=HIGHEST, preferred_element_type=F32)
            unrolled = pltpu.roll(flipped, 0, 1, stride=1, stride_axis=0)
            ddiag[h:h + 1, :] = jnp.sum(unrolled, axis=0, keepdims=True)
        o_ref[...] = lax.dot_general(ddiag[...], _diag_onehot(), NT, precision=HIGHEST,
                                     preferred_element_type=F32)

    return pl.pallas_call(
        body, name=name, out_shape=_sds((N_HEADS_CH, REL_PAD), F32),
        in_specs=[pl.BlockSpec(memory_space=pltpu.VMEM)], out_specs=pl.BlockSpec(memory_space=pltpu.VMEM),
        scratch_shapes=[pltpu.VMEM((N_HEADS_CH, REL_DIAG), F32)],
    )(dbias)


def _hl(h):
    return slice(h * HEAD_DIM, (h + 1) * HEAD_DIM)


def _split_dot(x, tri_bf16):
    hi = x.astype(BF16)
    lo = (x - hi.astype(F32)).astype(BF16)
    return (jnp.dot(hi, tri_bf16, preferred_element_type=F32)
            + jnp.dot(lo, tri_bf16, preferred_element_type=F32))


def _rows(j):
    return pl.ds(pl.multiple_of(j * QB, QB), QB)


def _krows(g):
    return pl.ds(pl.multiple_of(g * KB, KB), KB)


def _log_sigmoid_pair(z):
    sp = jnp.log(1.0 + jnp.exp(-jnp.abs(z)))
    return jnp.minimum(z, 0.0) - sp, -jnp.maximum(z, 0.0) - sp


def _qkv_specs(t, col0, n_pairs):
    q_spec = pl.BlockSpec((QB, LANES), lambda hp, i: (i, col0 + hp))
    k_spec = pl.BlockSpec((t, LANES), lambda hp, i: (0, col0 + n_pairs + hp))
    v_spec = pl.BlockSpec((t, LANES), lambda hp, i: (0, col0 + 2 * n_pairs + hp))
    return q_spec, k_spec, v_spec


def _keys_major(xt):
    pairs, groups, _, _ = xt.shape
    return xt.transpose(1, 3, 0, 2).reshape(groups * KB, pairs * LANES)


def sb_fwd(qkv, name):
    t = qkv.shape[0]
    nq = t // QB

    def body(q_ref, k_ref, v_ref, o_ref, w_ref):
        i = pl.program_id(1)
        groups = i // KSUB + 1
        tri_after = (_iota2((KB, KB), 0) > _iota2((KB, KB), 1)).astype(BF16)
        t_idx = i * QB + _iota2((QB, KB), 0)
        qs = [q_ref[:, _hl(h)] for h in range(2)]

        def step(g, carry, masked):
            strict = (g * KB + _iota2((QB, KB), 1)) < t_idx
            out = []
            for h in range(2):
                tail, acc = carry[2 * h], carry[2 * h + 1]
                k = k_ref[_krows(g), _hl(h)]
                v = v_ref[_krows(g), _hl(h)]
                z = lax.dot_general(qs[h], k, NT, preferred_element_type=F32)
                lb, lf = _log_sigmoid_pair(z)
                if masked:
                    lf = jnp.where(strict, lf, 0.0)
                between = _split_dot(lf, tri_after) + tail
                w = jnp.exp(lb + between)
                if masked:
                    w = jnp.where(strict, w, 0.0)
                w = w.astype(BF16)
                w_ref[h, g] = w
                acc = acc + jnp.dot(w, v, preferred_element_type=F32)
                out += [tail + jnp.sum(lf, axis=1, keepdims=True), acc]
            return tuple(out)

        init = (jnp.zeros((QB, 1), F32), jnp.zeros((QB, HEAD_DIM), F32)) * 2
        res = step(groups - 1, init, True)
        res = lax.fori_loop(0, groups - 1, lambda gg, c: step(groups - 2 - gg, c, False), res)
        for h in range(2):
            o_ref[:, _hl(h)] = res[2 * h + 1].astype(o_ref.dtype)

    q_spec, k_spec, v_spec = _qkv_specs(t, 0, 2)
    return pl.pallas_call(
        body, name=name, grid=(2, nq), in_specs=[q_spec, k_spec, v_spec],
        out_specs=[pl.BlockSpec((QB, LANES), lambda hp, i: (i, hp)),
                   pl.BlockSpec((2, None, t // KB, QB, KB), lambda hp, i: (hp, i, 0, 0, 0))],
        out_shape=[_sds((t, W_SB), BF16), _sds((4, nq, t // KB, QB, KB), BF16)],
        compiler_params=_cparams(2))(qkv, qkv, qkv)


def _hs(h):
    return slice(h * HEAD_DIM, (h + 1) * HEAD_DIM)


def sb_bwd(qkv, qkv_t, w, do, do_t, name):
    t = qkv.shape[0]
    nq = t // QB

    def body(q_ref, k_ref, v_ref, do_ref, qt_ref, dot_ref, w_ref, dq_ref, dkt_ref, dvt_ref):
        i = pl.program_id(1)

        @pl.when(i == 0)
        def _():
            dkt_ref[...] = jnp.zeros_like(dkt_ref)
            dvt_ref[...] = jnp.zeros_like(dvt_ref)

        groups = i // KSUB + 1
        tri_before = (_iota2((KB, KB), 0) < _iota2((KB, KB), 1)).astype(BF16)
        t_idx = i * QB + _iota2((QB, KB), 0)
        qs = [q_ref[:, _hl(h)] for h in range(2)]
        dos = [do_ref[:, _hl(h)] for h in range(2)]
        qts = [qt_ref[_hs(h), :] for h in range(2)]
        dots = [dot_ref[_hs(h), :] for h in range(2)]

        def grads(g, carry, masked):
            strict = (g * KB + _iota2((QB, KB), 1)) < t_idx
            out = []
            for h in range(2):
                head, dq = carry[2 * h], carry[2 * h + 1]
                k = k_ref[_krows(g), _hl(h)]
                v = v_ref[_krows(g), _hl(h)]
                wb = w_ref[h, g]
                z = lax.dot_general(qs[h], k, NT, preferred_element_type=F32)
                beta = _sigmoid(z)
                e = lax.dot_general(dos[h], v, NT, preferred_element_type=F32) * wb.astype(F32)
                before = _split_dot(e, tri_before) + head
                dz = e * (1.0 - beta) - before * beta
                if masked:
                    dz = jnp.where(strict, dz, 0.0)
                dzb = dz.astype(BF16)
                dq = dq + jnp.dot(dzb, k, preferred_element_type=F32)
                dkt_ref[g, _hs(h), :] += jnp.dot(qts[h], dzb, preferred_element_type=F32)
                dvt_ref[g, _hs(h), :] += jnp.dot(dots[h], wb, preferred_element_type=F32)
                out += [head + jnp.sum(e, axis=1, keepdims=True), dq]
            return tuple(out)

        init = (jnp.zeros((QB, 1), F32), jnp.zeros((QB, HEAD_DIM), F32)) * 2
        res = lax.fori_loop(0, groups - 1, lambda g, c: grads(g, c, False), init)
        res = grads(groups - 1, res, True)
        for h in range(2):
            dq_ref[:, _hl(h)] = (res[2 * h + 1] * SCALE).astype(dq_ref.dtype)

    q_spec, k_spec, v_spec = _qkv_specs(t, 0, 2)
    blk = pl.BlockSpec((QB, LANES), lambda hp, i: (i, hp))
    blk_t = pl.BlockSpec((LANES, QB), lambda hp, i: (hp, i))
    acc_t = pl.BlockSpec((None, t // KB, LANES, KB), lambda hp, i: (hp, 0, 0, 0))
    acc_sds = _sds((2, t // KB, LANES, KB), F32)
    return pl.pallas_call(
        body, name=name, grid=(2, nq),
        in_specs=[q_spec, k_spec, v_spec, blk, blk_t, blk_t,
                  pl.BlockSpec((2, None, t // KB, QB, KB), lambda hp, i: (hp, i, 0, 0, 0))],
        out_specs=[blk, acc_t, acc_t],
        out_shape=[_sds((t, W_SB), BF16), acc_sds, acc_sds],
        compiler_params=_cparams(2))(qkv, qkv, qkv, do, qkv_t, do_t, w)


def fox_fwd(qkv, fcol, frow, name):
    t = qkv.shape[0]
    nq = t // QB

    def body(q_ref, k_ref, v_ref, fc_ref, fr_ref, o_ref, lse_ref):
        hp = pl.program_id(0)
        i = pl.program_id(1)
        groups = i // KSUB + 1
        t_idx = i * QB + _iota2((QB, KB), 0)
        lane = _iota2((QB, LANES), 1)
        sub = _iota2((8, KB), 0)
        qs = [q_ref[:, _hl(h)] for h in range(2)]
        f_qs = [jnp.sum(jnp.where(lane == hp * 2 + h, fc_ref[...], 0.0), axis=1, keepdims=True)
                for h in range(2)]

        def step(g, carry, masked):
            causal = (g * KB + _iota2((QB, KB), 1)) <= t_idx
            fr = fr_ref[g]
            out = []
            for h in range(2):
                m, l, acc = carry[3 * h:3 * h + 3]
                k = k_ref[_krows(g), _hl(h)]
                v = v_ref[_krows(g), _hl(h)]
                f_k = jnp.sum(jnp.where(sub == hp * 2 + h, fr, 0.0), axis=0, keepdims=True)
                z = lax.dot_general(qs[h], k, NT, preferred_element_type=F32) + f_qs[h] - f_k
                if masked:
                    z = jnp.where(causal, z, NEG)
                m_new = jnp.maximum(m, jnp.max(z, axis=1, keepdims=True))
                p = jnp.exp(z - m_new)
                corr = jnp.exp(m - m_new)
                l = l * corr + jnp.sum(p, axis=1, keepdims=True)
                acc = acc * corr + jnp.dot(p.astype(BF16), v, preferred_element_type=F32)
                out += [m_new, l, acc]
            return tuple(out)

        init = (jnp.full((QB, 1), NEG, F32), jnp.zeros((QB, 1), F32), jnp.zeros((QB, HEAD_DIM), F32)) * 2
        res = lax.fori_loop(0, groups - 1, lambda g, c: step(g, c, False), init)
        res = step(groups - 1, res, True)
        for h in range(2):
            m, l, acc = res[3 * h:3 * h + 3]
            o_ref[:, _hl(h)] = (acc / l).astype(o_ref.dtype)
            lse_ref[:, _hl(h)] = jnp.broadcast_to(m + jnp.log(l), (QB, HEAD_DIM))

    q_spec, k_spec, v_spec = _qkv_specs(t, 18, 2)
    blk = pl.BlockSpec((QB, LANES), lambda hp, i: (i, hp))
    return pl.pallas_call(
        body, name=name, grid=(2, nq),
        in_specs=[q_spec, k_spec, v_spec, pl.BlockSpec((QB, LANES), lambda hp, i: (i, 0)),
                  pl.BlockSpec((t // KB, 8, KB), lambda hp, i: (0, 0, 0))],
        out_specs=[blk, blk],
        out_shape=[_sds((t, W_FOX), BF16), _sds((t, W_FOX), F32)],
        compiler_params=_cparams(2))(qkv, qkv, qkv, fcol, frow)


def fox_bwd(qkv, qkv_t, fcol, frow, o, lse, do, do_t, name):
    t = qkv.shape[0]
    nq = t // QB

    def body(q_ref, k_ref, v_ref, fc_ref, fr_ref, o_ref, lse_ref, do_ref, qt_ref, dot_ref,
             dq_ref, dk_ref, dv_ref, dfr_ref):
        hp = pl.program_id(0)
        i = pl.program_id(1)
        qts = [qt_ref[_hs(h), :] for h in range(2)]
        dots = [dot_ref[_hs(h), :] for h in range(2)]

        @pl.when(i == 0)
        def _():
            dk_ref[...] = jnp.zeros_like(dk_ref)
            dv_ref[...] = jnp.zeros_like(dv_ref)

        @pl.when((i == 0) & (hp == 0))
        def _():
            dfr_ref[...] = jnp.zeros_like(dfr_ref)

        groups = i // KSUB + 1
        t_idx = i * QB + _iota2((QB, KB), 0)
        lane = _iota2((QB, LANES), 1)
        sub = _iota2((8, KB), 0)
        qs = [q_ref[:, _hl(h)] for h in range(2)]
        dos = [do_ref[:, _hl(h)] for h in range(2)]
        f_qs = [jnp.sum(jnp.where(lane == hp * 2 + h, fc_ref[...], 0.0), axis=1, keepdims=True)
                for h in range(2)]
        lse_qs = [lse_ref[:, h * HEAD_DIM:h * HEAD_DIM + 1] for h in range(2)]
        deltas = [jnp.sum(dos[h].astype(F32) * o_ref[:, _hl(h)].astype(F32), axis=1, keepdims=True)
                  for h in range(2)]

        def step(g, dqs, masked):
            causal = (g * KB + _iota2((QB, KB), 1)) <= t_idx
            fr = fr_ref[g]
            out = []
            dfr = jnp.zeros((8, KB), F32)
            for h in range(2):
                k = k_ref[_krows(g), _hl(h)]
                v = v_ref[_krows(g), _hl(h)]
                f_k = jnp.sum(jnp.where(sub == hp * 2 + h, fr, 0.0), axis=0, keepdims=True)
                z = lax.dot_general(qs[h], k, NT, preferred_element_type=F32) + f_qs[h] - f_k
                p = jnp.exp(z - lse_qs[h])
                if masked:
                    p = jnp.where(causal, p, 0.0)
                dp = lax.dot_general(dos[h], v, NT, preferred_element_type=F32)
                ds = p * (dp - deltas[h])
                dsb = ds.astype(BF16)
                out.append(dqs[h] + jnp.dot(dsb, k, preferred_element_type=F32))
                dk_ref[g, _hs(h), :] += jnp.dot(qts[h], dsb, preferred_element_type=F32)
                dv_ref[g, _hs(h), :] += jnp.dot(dots[h], p.astype(BF16), preferred_element_type=F32)
                colsum = jnp.sum(ds, axis=0, keepdims=True)
                dfr = dfr + jnp.where(sub == hp * 2 + h, -colsum, 0.0)
            dfr_ref[g] += dfr
            return tuple(out)

        res = lax.fori_loop(0, groups - 1, lambda g, c: step(g, c, False),
                            (jnp.zeros((QB, HEAD_DIM), F32),) * 2)
        res = step(groups - 1, res, True)
        for h in range(2):
            dq_ref[:, _hl(h)] = (res[h] * SCALE).astype(dq_ref.dtype)

    q_spec, k_spec, v_spec = _qkv_specs(t, 18, 2)
    blk = pl.BlockSpec((QB, LANES), lambda hp, i: (i, hp))
    frs = pl.BlockSpec((t // KB, 8, KB), lambda hp, i: (0, 0, 0))
    acc_t = pl.BlockSpec((None, t // KB, LANES, KB), lambda hp, i: (hp, 0, 0, 0))
    acc_sds = _sds((2, t // KB, LANES, KB), F32)
    return pl.pallas_call(
        body, name=name, grid=(2, nq),
        in_specs=[q_spec, k_spec, v_spec, pl.BlockSpec((QB, LANES), lambda hp, i: (i, 0)), frs,
                  blk, blk, blk, pl.BlockSpec((LANES, QB), lambda hp, i: (18 + hp, i)),
                  pl.BlockSpec((LANES, QB), lambda hp, i: (hp, i))],
        out_specs=[blk, acc_t, acc_t, frs],
        out_shape=[_sds((t, W_FOX), BF16), acc_sds, acc_sds, _sds((t // KB, 8, KB), F32)],
        compiler_params=_cparams(2))(qkv, qkv, qkv, fcol, frow, o, lse, do, qkv_t, do_t)


def _frow_to_groups(frow):
    n = frow.shape[0] // KSUB
    return frow.reshape(n, KSUB, 8, QB).transpose(0, 2, 1, 3).reshape(n, 8, KB)


def _frow_from_groups(frow):
    n = frow.shape[0]
    return frow.reshape(n, 8, KSUB, QB).transpose(0, 2, 1, 3).reshape(n * KSUB, 8, QB)


def _chunk_band():
    qi = _iota2((QB, CH_KEYS), 0)
    kj = _iota2((QB, CH_KEYS), 1)
    dchunk = (qi >> 6) + LEFT_CHUNKS - (kj >> 6)
    return jnp.where((dchunk >= 0) & (dchunk <= LEFT_CHUNKS), 0.0, NEG)


def _chunk_pad_row(i):
    kj = _iota2((1, CH_KEYS), 1)
    return jnp.where((i - (CH_WIN - 1)) * QB + kj >= 0, 0.0, NEG)


CH_PAD = (CH_WIN - 1) * QB
CH_STEP_HEADS = 4
CH_COLS = CH_STEP_HEADS * HEAD_DIM


def _window(i):
    return pl.ds(pl.multiple_of(i * QB, QB), CH_KEYS)


def _chunk_weights(q, kw, bias, pad_row):
    z = lax.dot_general(q, kw, NT, preferred_element_type=F32) + bias + pad_row
    e = jnp.exp(z - jnp.max(z, axis=1, keepdims=True))
    return e, 1.0 / jnp.sum(e, axis=1, keepdims=True)


def _chunk_specs(t):
    q_spec = pl.BlockSpec((QB, CH_COLS), lambda hp, i: (i, 3 * W_SB // CH_COLS + hp))
    kv_spec = pl.BlockSpec((t + CH_PAD, CH_COLS), lambda hp, i: (0, hp))
    return q_spec, kv_spec


def chunk_fwd(qkv, kp, vp, bias, name):
    t = qkv.shape[0]
    nq = t // QB

    def body(q_ref, k_ref, v_ref, b_ref, o_ref):
        i = pl.program_id(1)
        pad_row = _chunk_pad_row(i)
        for h in range(CH_STEP_HEADS):
            e, inv = _chunk_weights(q_ref[:, _hl(h)], k_ref[_window(i), _hl(h)], b_ref[h], pad_row)
            o = jnp.dot(e.astype(BF16), v_ref[_window(i), _hl(h)], preferred_element_type=F32)
            o_ref[:, _hl(h)] = (o * inv).astype(o_ref.dtype)

    q_spec, kv_spec = _chunk_specs(t)
    return pl.pallas_call(
        body, name=name, grid=(W_CH // CH_COLS, nq),
        in_specs=[q_spec, kv_spec, kv_spec,
                  pl.BlockSpec((CH_STEP_HEADS, QB, CH_KEYS), lambda hp, i: (hp, 0, 0))],
        out_specs=pl.BlockSpec((QB, CH_COLS), lambda hp, i: (i, hp)),
        out_shape=_sds((t, W_CH), BF16), compiler_params=_cparams(2))(qkv, kp, vp, bias)


def chunk_bwd(qkv, qkv_t, kp, vp, bias, do, do_t, name):
    t = qkv.shape[0]
    nq = t // QB

    def body(q_ref, k_ref, v_ref, b_ref, do_ref, qt_ref, dot_ref, dq_ref, dk_ref, dv_ref, db_ref):
        i = pl.program_id(1)

        @pl.when(i == 0)
        def _():
            dk_ref[...] = jnp.zeros_like(dk_ref)
            dv_ref[...] = jnp.zeros_like(dv_ref)
            db_ref[...] = jnp.zeros_like(db_ref)

        pad_row = _chunk_pad_row(i)
        for h in range(CH_STEP_HEADS):
            q = q_ref[:, _hl(h)]
            dov = do_ref[:, _hl(h)]
            kw = k_ref[_window(i), _hl(h)]
            e, inv = _chunk_weights(q, kw, b_ref[h], pad_row)
            p = e * inv
            dp = lax.dot_general(dov, v_ref[_window(i), _hl(h)], NT, preferred_element_type=F32)
            ds = p * (dp - jnp.sum(p * dp, axis=1, keepdims=True))
            db_ref[h] += ds
            dsb = ds.astype(BF16)
            dq_ref[:, _hl(h)] = (jnp.dot(dsb, kw, preferred_element_type=F32) * SCALE).astype(dq_ref.dtype)
            dkt = jnp.dot(qt_ref[_hs(h), :], dsb, preferred_element_type=F32)
            dvt = jnp.dot(dot_ref[_hs(h), :], p.astype(BF16), preferred_element_type=F32)
            for b in range(CH_WIN):
                dk_ref[i + b, _hs(h), :] += dkt[:, b * QB:(b + 1) * QB]
                dv_ref[i + b, _hs(h), :] += dvt[:, b * QB:(b + 1) * QB]

    q_spec, kv_spec = _chunk_specs(t)
    blk = pl.BlockSpec((QB, CH_COLS), lambda hp, i: (i, hp))
    bspec = pl.BlockSpec((CH_STEP_HEADS, QB, CH_KEYS), lambda hp, i: (hp, 0, 0))
    nblk = nq + CH_WIN - 1
    acc_t = pl.BlockSpec((None, nblk, CH_COLS, QB), lambda hp, i: (hp, 0, 0, 0))
    acc_sds = _sds((W_CH // CH_COLS, nblk, CH_COLS, QB), F32)
    return pl.pallas_call(
        body, name=name, grid=(W_CH // CH_COLS, nq),
        in_specs=[q_spec, kv_spec, kv_spec, bspec, blk,
                  pl.BlockSpec((CH_COLS, QB), lambda hp, i: (3 * W_SB // CH_COLS + hp, i)),
                  pl.BlockSpec((CH_COLS, QB), lambda hp, i: (hp, i))],
        out_specs=[blk, acc_t, acc_t, bspec],
        out_shape=[_sds((t, W_CH), BF16), acc_sds, acc_sds, _sds((N_HEADS_CH, QB, CH_KEYS), F32)],
        compiler_params=_cparams(2))(qkv, kp, vp, bias, do, qkv_t, do_t)


def _sum_parts(p_ref):
    total = p_ref[0].astype(F32)
    for p in range(1, p_ref.shape[0]):
        total = total + p_ref[p].astype(F32)
    return total


def sum_parts(parts, grid, p_spec, o_spec, out_sds, name):
    def body(p_ref, o_ref):
        o_ref[...] = _sum_parts(p_ref)

    return pl.pallas_call(body, name=name, grid=grid, in_specs=[p_spec], out_specs=o_spec,
                          out_shape=out_sds, compiler_params=_cparams(len(grid)))(parts)


def adamw(parts, w, m, v, grid, p_specs, w_spec, name):
    c1 = 1.0 / (1.0 - ADAM_B1 ** ADAM_STEP)
    c2 = 1.0 / (1.0 - ADAM_B2 ** ADAM_STEP)
    n = len(parts)

    def body(*refs):
        w_ref, m_ref, v_ref, g_out, d_out, m_out, v_out = refs[n:]
        g = _sum_parts(refs[0])
        for q in range(1, n):
            g = jnp.where(pl.program_id(0) == q, _sum_parts(refs[q]), g)
        m_new = ADAM_B1 * m_ref[...] + (1.0 - ADAM_B1) * g
        v_new = ADAM_B2 * v_ref[...] + (1.0 - ADAM_B2) * (g * g)
        m_hat = m_new * c1
        v_hat = v_new * c2
        g_out[...] = g
        d_out[...] = -ADAM_LR * (m_hat / (jnp.sqrt(v_hat) + ADAM_EPS) + ADAM_WD * w_ref[...])
        m_out[...] = m_new
        v_out[...] = v_new

    out = _sds(w.shape, F32)
    return pl.pallas_call(
        body, name=name, grid=grid, in_specs=[*p_specs, w_spec, w_spec, w_spec],
        out_specs=[w_spec] * 4, out_shape=[out] * 4,
        compiler_params=_cparams(len(grid)))(*parts, w, m, v)


def _ffn_fwd(x, gain, wa, wb_after, s, tm, tag, on_event, deps=()):
    t = x.shape[0]
    hn = rmsnorm_fwd(x, gain, tm, f"rms_{tag}", deps)
    gu, act = ffn_in_swiglu(hn, wa, s, min(2 * tm, t), f"ffn_in_{tag}")
    on_event("act", act)
    wb = wb_after(act)
    y = ffn_out_residual(act, wb, x, s, min(2 * tm, t), f"ffn_out_{tag}")
    return y, (hn, gu, act), wb


def _ffn_bwd(dy, x, gain, saved, wa, wb, s, tm, tag, on_grads):
    t = x.shape[0]
    hn, gu, act = saved
    dgu = ffn_dact_swiglu(dy, wb, gu, s, min(2 * tm, t), f"ffn_dact_{tag}")
    dwb = matmul(TN, act, dy, _sds((4, FF_BLK, D_MODEL), BF16), (4, 1, 1),
                 pl.BlockSpec((None, t, FF_BLK), lambda i, j, k: (i, 0, 0)),
                 pl.BlockSpec((t, D_MODEL), lambda i, j, k: (0, 0)),
                 pl.BlockSpec((None, FF_BLK, D_MODEL), lambda i, j, k: (i, 0, 0)),
                 None, name=f"ffn_dwout_{tag}", alpha=0.5)
    dwa = matmul(TN, dgu, hn, _sds((8, FF_BLK, D_MODEL), BF16), (1, 8, 1),
                 pl.BlockSpec((None, None, t, FF_BLK), lambda i, j, k: (j % 4, j // 4, 0, 0)),
                 pl.BlockSpec((t, D_MODEL), lambda i, j, k: (0, 0)),
                 pl.BlockSpec((None, FF_BLK, D_MODEL), lambda i, j, k: (j, 0, 0)),
                 None, name=f"ffn_dwin_{tag}")
    deps = on_grads(dwa, dwb)
    dhn = ffn_dh(dgu, wa, s, tm, f"ffn_dh_{tag}", deps)
    dx, dgain = rmsnorm_bwd(x, gain, dhn, dy, tm, f"rms_bwd_{tag}")
    return dx, dgain


BR_ROWS = ((0, 1), (1, 2), (3, 1))

_Q_COLUMN_SCALE = np.ones((1, QKV_WIDTH), np.float32)
for _lo, _width in ((0, W_SB), (3 * W_SB, W_CH), (3 * (W_SB + W_CH), W_FOX)):
    _Q_COLUMN_SCALE[0, _lo:_lo + _width] = SCALE


def _mixer_fwd(x, gain, wqkv, wf, wgate, late_after, bq, bf, bg, bias, layer, tm, tag, on_event):
    t = x.shape[0]
    nt = t // tm
    hm = rmsnorm_fwd(x, gain, tm, f"rms_{tag}")
    a_full = pl.BlockSpec((tm, D_MODEL), lambda i, j, k: (i, 0))
    wide_out = pl.BlockSpec((tm, D_MODEL), lambda i, j, k: (i, j))
    wide_b = pl.BlockSpec((1, D_MODEL), lambda i, j, k: (0, j))
    qkv, qkv_t = matmul(NN, hm, wqkv, _sds((t, QKV_WIDTH), BF16), (nt, 3, 1), a_full,
                        pl.BlockSpec((None, D_MODEL, D_MODEL), lambda i, j, k: (layer, 0, j)), wide_out, None,
                        name=f"proj_qkv_{tag}", bias=bq, bias_spec=wide_b,
                        scale=jnp.asarray(_Q_COLUMN_SCALE), scale_spec=wide_b,
                        out_t_sds=_sds((QKV_WIDTH, t), BF16),
                        out_t_spec=pl.BlockSpec((D_MODEL, tm), lambda i, j, k: (j, i)))
    gates = matmul(NN, hm, wgate, _sds((t, 3 * D_MODEL), F32), (nt, 3, 1), a_full,
                   pl.BlockSpec((None, D_MODEL, D_MODEL), lambda i, j, k: (layer + 1, 0,j)), wide_out,
                   None, name=f"proj_gate_{tag}", bias=bg, bias_spec=wide_b)
    f = matmul(NN, hm, wf, _sds((t, LANES), F32), (nt, 1, 1), a_full,
               pl.BlockSpec((None, D_MODEL, LANES), lambda i, j, k: (layer, 0, 0)),
               pl.BlockSpec((tm, LANES), lambda i, j, k: (i, 0)), None,
               name=f"proj_f_{tag}", bias=bf, bias_spec=pl.BlockSpec((1, LANES), lambda i, j, k: (0, 0)))
    fcol, frow = forget_cumsum(f, f"fcum_{tag}")
    frow = _frow_to_groups(frow)
    on_event("qkv", qkv)
    o_sb, w_sb = sb_fwd(qkv, f"sb_fwd_{tag}")
    on_event("o_sb", o_sb)
    kp = jnp.pad(qkv[:, 10 * LANES:14 * LANES], ((CH_PAD, 0), (0, 0)))
    vp = jnp.pad(qkv[:, 14 * LANES:18 * LANES], ((CH_PAD, 0), (0, 0)))
    o_ch = chunk_fwd(qkv, kp, vp, bias, f"chunk_fwd_{tag}")
    o_fox, lse = fox_fwd(qkv, fcol, frow, f"fox_fwd_{tag}")
    wbr, wout = late_after(o_fox)
    ys = []
    for a, (o, (r0, nr)) in enumerate(zip((o_sb, o_ch, o_fox), BR_ROWS)):
        ys.append(matmul(
            NN, o, wbr, _sds((t, D_MODEL), F32), (nt, 1, nr),
            pl.BlockSpec((tm, 256), lambda i, j, k: (i, k)),
            pl.BlockSpec((None, 256, D_MODEL), functools.partial(lambda i, j, k, r0: (layer, r0 + k, 0), r0=r0)),
            a_full, (tm, D_MODEL), name=f"branch{a}_{tag}"))
    merged = merge_fwd(gates, ys[0], ys[1], ys[2], tm, f"merge_{tag}")
    x_new = matmul(NN, merged, wout, _sds((t, D_MODEL), F32), (nt, 1, 1), a_full,
                   pl.BlockSpec((None, D_MODEL, D_MODEL), lambda i, j, k: (layer, 0, 0)), a_full, None,
                   name=f"wout_{tag}", res=x, res_spec=a_full)
    saved = (hm, qkv, gates, f, fcol, frow, o_sb, o_ch, o_fox, lse, ys, merged, kp, vp, w_sb, qkv_t)
    return x_new, saved, wbr, wout


def _mixer_bwd(dy, x, gain, saved, wqkv, wf, wgate, wbr, wout, bias, layer, tm, tag, on_grads):
    t = x.shape[0]
    nt = t // tm
    hm, qkv, gates, f, fcol, frow, o_sb, o_ch, o_fox, lse, ys, merged, kp, vp, w_sb, qkv_t = saved
    a_full = pl.BlockSpec((tm, D_MODEL), lambda i, j, k: (i, 0))
    red_row = pl.BlockSpec((tm, D_MODEL), lambda i, j, k: (k, 0))
    sq = pl.BlockSpec((D_MODEL, D_MODEL), lambda i, j, k: (0, 0))
    dmerged = matmul(NT, dy, wout, _sds((t, D_MODEL), F32), (nt, 1, 1), a_full,
                     pl.BlockSpec((None, D_MODEL, D_MODEL), lambda i, j, k: (layer, 0, 0)), a_full, None,
                     name=f"dmerged_{tag}")
    all_t = pl.BlockSpec((t, D_MODEL), lambda i, j, k: (0, 0))
    dwout = matmul(TN, merged, dy, _sds((D_MODEL, D_MODEL), BF16), (1, 1, 1), all_t, all_t, sq,
                   None, name=f"dwout_{tag}")
    dgs, dys = merge_bwd(dmerged, gates, ys[0], ys[1], ys[2], tm, f"merge_bwd_{tag}")
    dos, dos_t, dwbrs = [], [], []
    for a, (o, (r0, nr)) in enumerate(zip((o_sb, o_ch, o_fox), BR_ROWS)):
        do, do_t = matmul(
            NT, dys[a], wbr, _sds((t, nr * 256), BF16), (nt, nr, 1), a_full,
            pl.BlockSpec((None, 256, D_MODEL), functools.partial(lambda i, j, k, r0: (layer, r0 + j, 0), r0=r0)),
            pl.BlockSpec((tm, 256), lambda i, j, k: (i, j)), None, name=f"dbranch{a}_{tag}",
            out_t_sds=_sds((nr * 256, t), BF16), out_t_spec=pl.BlockSpec((256, tm), lambda i, j, k: (j, i)))
        dos.append(do)
        dos_t.append(do_t)
        dwbrs.append(matmul(
            TN, o, dys[a], _sds((nr * 256, D_MODEL), BF16), (nr, 1, 1),
            pl.BlockSpec((t, 256), lambda i, j, k: (0, i)), all_t,
            pl.BlockSpec((256, D_MODEL), lambda i, j, k: (i, 0)), None, name=f"dwbr{a}_{tag}"))
    dq_a, dk_a, dv_a = sb_bwd(qkv, qkv_t, w_sb, dos[0], dos_t[0], f"sb_bwd_{tag}")
    dk_a, dv_a = _keys_major(dk_a), _keys_major(dv_a)
    dq_b, dk_b, dv_b, dbias = chunk_bwd(qkv, qkv_t, kp, vp, bias, dos[1], dos_t[1], f"chunk_bwd_{tag}")
    dk_b, dv_b = [x[:, CH_WIN - 1:].transpose(1, 3, 0, 2).reshape(t, W_CH) for x in (dk_b, dv_b)]
    dq_c, dk_c, dv_c, dfrow = fox_bwd(qkv, qkv_t, fcol, frow, o_fox, lse, dos[2], dos_t[2], f"fox_bwd_{tag}")
    dk_c, dv_c = _keys_major(dk_c), _keys_major(dv_c)
    df = forget_cumsum_bwd(_frow_from_groups(dfrow), f, f"fcum_bwd_{tag}")
    dqkv = jnp.concatenate([p.astype(BF16) for p in
                            (dq_a, dk_a, dv_a, dq_b, dk_b, dv_b, dq_c, dk_c, dv_c)], axis=1)
    dgates = jnp.concatenate(dgs, axis=1)
    dtab = rel_bias_scatter(dbias, f"rel_scatter_{tag}")

    all_rows = pl.BlockSpec((t, D_MODEL), lambda i, j, k: (0, 0))
    wide_b = pl.BlockSpec((t, D_MODEL), lambda i, j, k: (0, j))
    wide_o = pl.BlockSpec((D_MODEL, D_MODEL), lambda i, j, k: (0, j))
    wide_cs = pl.BlockSpec((1, D_MODEL), lambda i, j, k: (0, j))
    dwqkv, dbq = matmul(TN, hm, dqkv, _sds((D_MODEL, QKV_WIDTH), BF16), (1, 3, 1), all_rows, wide_b,
                        wide_o, None, name=f"dwqkv_{tag}",
                        colsum_sds=_sds((1, QKV_WIDTH), F32), colsum_spec=wide_cs)
    dwgate, dbg = matmul(TN, hm, dgates, _sds((D_MODEL, 3 * D_MODEL), BF16), (1, 3, 1), all_rows,
                         wide_b, wide_o, None, name=f"dwgate_{tag}",
                         colsum_sds=_sds((1, 3 * D_MODEL), F32), colsum_spec=wide_cs)
    dwf, dbf = matmul(TN, hm, df, _sds((D_MODEL, LANES), BF16), (1, 1, 1), all_rows,
                      pl.BlockSpec((t, LANES), lambda i, j, k: (0, 0)),
                      pl.BlockSpec((D_MODEL, LANES), lambda i, j, k: (0, 0)), None,
                      name=f"dwf_{tag}", colsum_sds=_sds((1, LANES), F32),
                      colsum_spec=pl.BlockSpec((1, LANES), lambda i, j, k: (0, 0)))
    dwbr = jnp.concatenate(dwbrs, axis=0)
    deps = on_grads(dict(dwqkv=dwqkv, dwgate=dwgate, dwf=dwf, dwbr=dwbr, dwout=dwout))
    wide_a = pl.BlockSpec((tm, QKV_WIDTH), lambda i, j, k: (i, 0))
    dhm = matmul(NT, dqkv, wqkv, _sds((t, D_MODEL), F32), (nt, 1, 1), wide_a,
                 pl.BlockSpec((None, D_MODEL, QKV_WIDTH), lambda i, j, k: (layer, 0, 0)), a_full,
                 None, name=f"dhm_qkv_{tag}", deps=deps)
    dhm = matmul(NT, dgates, wgate, _sds((t, D_MODEL), F32), (nt, 1, 1), wide_a,
                 pl.BlockSpec((None, D_MODEL, QKV_WIDTH), lambda i, j, k: (layer + 1, 0, 0)), a_full,
                 None, name=f"dhm_gate_{tag}", res=dhm, res_spec=a_full)
    dhm = matmul(NT, df, wf, _sds((t, D_MODEL), F32), (nt, 1, 1),
                 pl.BlockSpec((tm, LANES), lambda i, j, k: (i, 0)),
                 pl.BlockSpec((None, D_MODEL, LANES), lambda i, j, k: (layer, 0, 0)), a_full, None,
                 name=f"dhm_f_{tag}", res=dhm, res_spec=a_full)
    dx, dgain = rmsnorm_bwd(x, gain, dhm, dy, tm, f"rms_bwd_{tag}")
    return dx, dict(dbq=dbq, dbg=dbg, dbf=dbf, dtab=dtab, dgain=dgain)


def _pack_small(pieces):
    flat = jnp.concatenate([p.reshape(-1).astype(F32) for p in pieces])
    flat = jnp.pad(flat, (0, SMALL_ROWS * LANES - flat.shape[0]))
    return flat.reshape(SMALL_ROWS, LANES)


def _unpack_small(packed, shapes):
    flat = packed.reshape(-1)
    out, pos = [], 0
    for shp in shapes:
        n = int(np.prod(shp))
        out.append(flat[pos:pos + n].reshape(shp))
        pos += n
    return out


def kernel(x, g_ffn1, w_ffn1_in, w_ffn1_out, g_mix, w_in, b_in, rel_bias, w_br_sb, w_br_ch, w_br_fox, w_out, g_ffn2, w_ffn2_in, w_ffn2_out, g_final, loss_target, m_g_ffn1, m_w_ffn1_in, m_w_ffn1_out, m_g_mix, m_w_in, m_b_in, m_rel_bias, m_w_br_sb, m_w_br_ch, m_w_br_fox, m_w_out, m_g_ffn2, m_w_ffn2_in, m_w_ffn2_out, m_g_final, v_g_ffn1, v_w_ffn1_in, v_w_ffn1_out, v_g_mix, v_w_in, v_b_in, v_rel_bias, v_w_br_sb, v_w_br_ch, v_w_br_fox, v_w_out, v_g_ffn2, v_w_ffn2_in, v_w_ffn2_out, v_g_final):
    t = x.shape[1]
    tm = min(512, t)
    xs = x[0]
    target = loss_target[0]
    f_lo, f_hi = QKV_WIDTH, QKV_WIDTH + N_HEADS_FOX

    def ffn_shards(w_in_, w_out_, l):
        return [w_in_[l:l + 1].astype(BF16), w_out_[l:l + 1].astype(BF16)]

    def mixer_shards(l):
        wl = w_in[l]
        return [jnp.stack([wl[:, :QKV_WIDTH], wl[:, f_hi:]]).astype(BF16),
                jnp.pad(wl[:, f_lo:f_hi], ((0, 0), (0, LANES - N_HEADS_FOX)))[None].astype(BF16),
                w_out[l:l + 1].astype(BF16),
                jnp.concatenate([w_br_sb[l], w_br_ch[l], w_br_fox[l]], axis=0)[None].astype(BF16)]

    gathers = {}
    gather_tokens = []

    def start_gather(shards, name):
        handle = gather_start(shards, name, deps=gather_tokens[-1:])
        gather_tokens.append(handle["token"])
        return handle

    def relay(handle, after):
        if "send2" not in handle:
            gather_relay(handle, after)

    relay_on = {("mix", 0, "qkv"): ("mix", 0, 1), ("mix", 0, "o_sb"): ("ffn2", 0, 0),
                ("ffn2", 0, "act"): ("ffn1", 1, 0), ("ffn1", 1, "act"): ("mix", 1, 0),
                ("mix", 1, "qkv"): ("ffn2", 1, 0)}

    def on_event(grp, l):
        def fire(event, array):
            target = relay_on.get((grp, l, event))
            if target is not None:
                relay(gathers[target[:2]][target[2]], array)
        return fire

    for l in range(DEPTH):
        for grp, shards in (("ffn1", ffn_shards(w_ffn1_in, w_ffn1_out, l)), ("mix", mixer_shards(l)),
                            ("ffn2", ffn_shards(w_ffn2_in, w_ffn2_out, l))):
            cut = len(shards) // 2
            if l == 0 and grp != "ffn2":
                gathers[(grp, l)] = (start_gather(shards[:cut], f"gather_{grp}_l{l}_a"),
                                     start_gather(shards[cut:], f"gather_{grp}_l{l}_b"))
            else:
                gathers[(grp, l)] = (start_gather(shards, f"gather_{grp}_l{l}"),)

    def gathered(key, after):
        hs = gathers[key]
        cut = hs[0]["n"]
        relay(hs[0], after)
        first = gather_finish(hs[0], after)
        if len(hs) == 1:
            return first[:cut // 2], lambda later: first[cut // 2:]

        def second(later):
            relay(hs[1], later)
            return gather_finish(hs[1], later)

        return first, second

    def ffn_weights(key, after):
        (wa_,), rest = gathered(key, after)
        return wa_, lambda later: rest(later)[0].reshape(1, 4, FF_BLK, D_MODEL)

    def mixer_weights(key, after):
        (wc_, wf_), rest = gathered(key, after)

        def late(later):
            wout_, wbr_ = rest(later)
            return (wbr_.transpose(0, 2, 1, 3).reshape(1, D_MODEL, D_MODEL), wout_.reshape(1, D_MODEL, D_MODEL))

        return wc_.reshape(2, D_MODEL, QKV_WIDTH), wf_.reshape(1, D_MODEL, LANES), late

    bq = b_in[:, None, :QKV_WIDTH]
    bf = jnp.pad(b_in[:, f_lo:f_hi], ((0, 0), (0, LANES - N_HEADS_FOX)))[:, None, :]
    bg = b_in[:, None, f_hi:]
    tab_t = jnp.pad(rel_bias.transpose(0, 2, 1), ((0, 0), (0, 0), (0, REL_PAD - N_REL)))

    h = xs
    saved = []
    weights = []
    for l in range(DEPTH):
        bias = rel_bias_build(tab_t[l], f"rel_build_l{l}").reshape(N_HEADS_CH, QB, CH_KEYS)
        x0 = h
        wa1, wb1_after = ffn_weights(("ffn1", l), x0)
        x1, s1, wb1 = _ffn_fwd(x0, g_ffn1[l:l + 1], wa1, wb1_after, 0, tm, f"ffn1_l{l}", on_event("ffn1", l),
                               deps=gather_tokens if l == 0 else ())
        wc, wf, late_after = mixer_weights(("mix", l), x1)
        x2, sm, wbr, wout = _mixer_fwd(x1, g_mix[l:l + 1], wc, wf, wc, late_after, bq[l], bf[l], bg[l],
                                       bias, 0, tm, f"mix_l{l}", on_event("mix", l))
        wa2, wb2_after = ffn_weights(("ffn2", l), x2)
        x3, s2, wb2 = _ffn_fwd(x2, g_ffn2[l:l + 1], wa2, wb2_after, 0, tm, f"ffn2_l{l}", on_event("ffn2", l))
        saved.append((x0, x1, x2, s1, sm, s2, bias))
        weights.append(((wa1, wb1), (wc, wf, wout, wbr), (wa2, wb2)))
        h = x3

    dx, dg_final, loss_blk = loss_head(h, g_final[None, :], target, tm, "loss_head")

    g_mix_l = [None] * DEPTH
    dgains = {}
    scatters = {}

    def scatter_ffn(key):
        def on_grads(dwa, dwb):
            scatters[key] = exchange_start(
                "scatter", [dwa[None], dwb.reshape(1, N_DEV, D_FF // N_DEV, D_MODEL)],
                f"scatter_{key[0]}_l{key[1]}")
            return (scatters[key]["token"],)
        return on_grads

    def scatter_mixer(key):
        def on_grads(gm):
            scatters[key] = exchange_start(
                "scatter",
                [gm["dwqkv"].reshape(1, N_DEV, LANES, QKV_WIDTH), gm["dwgate"].reshape(1, N_DEV, LANES, QKV_WIDTH),
                 gm["dwf"].reshape(1, N_DEV, LANES, LANES), gm["dwout"].reshape(1, N_DEV, LANES, D_MODEL),
                 gm["dwbr"].reshape(1, D_MODEL, N_DEV, LANES).transpose(0, 2, 1, 3)],
                f"scatter_{key[0]}_l{key[1]}")
            return (scatters[key]["token"],)
        return on_grads

    for l in reversed(range(DEPTH)):
        x0, x1, x2, s1, sm, s2, bias = saved[l]
        w1, (wc, wf, wout, wbr), w2 = weights[l]
        dx, dgains[("ffn2", l)] = _ffn_bwd(dx, x2, g_ffn2[l:l + 1], s2, *w2, 0, tm, f"ffn2_l{l}",
                                           scatter_ffn(("ffn2", l)))
        dx, g_mix_l[l] = _mixer_bwd(dx, x1, g_mix[l:l + 1], sm, wc, wf, wc, wbr, wout, bias, 0, tm,
                                    f"mix_l{l}", scatter_mixer(("mix", l)))
        dx, dgains[("ffn1", l)] = _ffn_bwd(dx, x0, g_ffn1[l:l + 1], s1, *w1, 0, tm, f"ffn1_l{l}",
                                           scatter_ffn(("ffn1", l)))

    small_shapes = []
    small_pieces = []
    small_w, small_m, small_v = [], [], []

    def add_small(piece, w, m, v):
        small_shapes.append(w.shape)
        small_pieces.append(piece)
        small_w.append(w); small_m.append(m); small_v.append(v)

    dg1 = jnp.concatenate([dgains[("ffn1", l)] for l in range(DEPTH)], axis=0)
    dgm = jnp.concatenate([g_mix_l[l]["dgain"] for l in range(DEPTH)], axis=0)
    dg2 = jnp.concatenate([dgains[("ffn2", l)] for l in range(DEPTH)], axis=0)
    db = jnp.stack([jnp.concatenate([g_mix_l[l]["dbq"][0], g_mix_l[l]["dbf"][0, :N_HEADS_FOX],
                                     g_mix_l[l]["dbg"][0]]) for l in range(DEPTH)])
    drel = jnp.stack([g_mix_l[l]["dtab"][:, :N_REL].T for l in range(DEPTH)])
    add_small(dg1, g_ffn1, m_g_ffn1, v_g_ffn1)
    add_small(dgm, g_mix, m_g_mix, v_g_mix)
    add_small(db, b_in, m_b_in, v_b_in)
    add_small(drel, rel_bias, m_rel_bias, v_rel_bias)
    add_small(dg2, g_ffn2, m_g_ffn2, v_g_ffn2)
    add_small(dg_final[0], g_final, m_g_final, v_g_final)
    loss_piece = loss_blk[0, 0:1]
    small_packed = _pack_small(small_pieces + [loss_piece])

    recv = {}
    last = ("ffn1", 0)
    for l in reversed(range(DEPTH)):
        for grp in ("ffn2", "mix", "ffn1"):
            if (grp, l) != last:
                recv[(grp, l)] = exchange_wait(scatters[(grp, l)], dx, f"scattered_{grp}_l{l}")

    def upd(parts, w, m, v, tr, name, rb0=0):
        _, r, c = w.shape
        nr = r // tr

        def p_spec(layer):
            pinned = (nr - 1) if layer == 0 else 0
            return pl.BlockSpec((N_DEV, None, tr, c),
                                lambda l, i: (0, 0, rb0 + jnp.where(l == layer, i, pinned), 0))

        return adamw(parts, w, m, v, (DEPTH, nr), [p_spec(0), p_spec(1)],
                     pl.BlockSpec((None, tr, c), lambda l, i: (l, i, 0)), name)

    def both(grp, k):
        return [recv[(grp, l)][k] for l in range(DEPTH)]

    out_rows = D_FF // N_DEV // 2
    def upd_transposed(parts, w, m, v, tr, name):
        tp = lambda a: jnp.transpose(a, (0, 2, 1))
        return [tp(o) for o in upd(parts, tp(w), tp(m), tp(v), tr, name)]

    in_rows = FF_BLK // 4
    r_ffn2_in = upd_transposed(both("ffn2", 0), w_ffn2_in, m_w_ffn2_in, v_w_ffn2_in, in_rows, "adamw_ffn2_in")
    r_ffn2_out = upd(both("ffn2", 1), w_ffn2_out, m_w_ffn2_out, v_w_ffn2_out, out_rows, "adamw_ffn2_out")
    r_out = upd(both("mix", 3), w_out, m_w_out, v_w_out, LANES, "adamw_w_out")
    r_br_sb = upd(both("mix", 4), w_br_sb, m_w_br_sb, v_w_br_sb, 256, "adamw_br_sb", rb0=0)
    r_br_ch = upd(both("mix", 4), w_br_ch, m_w_br_ch, v_w_br_ch, 256, "adamw_br_ch", rb0=1)
    r_br_fox = upd(both("mix", 4), w_br_fox, m_w_br_fox, v_w_br_fox, 256, "adamw_br_fox", rb0=3)

    def summed(parts, name):
        _, _, r, c = parts.shape
        return sum_parts(parts, (1,), pl.BlockSpec((N_DEV, None, r, c), lambda s: (0, 0, 0, 0)),
                         pl.BlockSpec((r, c), lambda s: (0, 0)), _sds((r, c), F32), name)

    g_w_in = jnp.stack([
        jnp.concatenate([summed(recv[("mix", l)][0], f"sum_wqkv_l{l}"),
                         summed(recv[("mix", l)][2], f"sum_wf_l{l}")[:, :N_HEADS_FOX],
                         summed(recv[("mix", l)][1], f"sum_wgate_l{l}")], axis=1) for l in range(DEPTH)])
    to_cols = lambda a: jnp.transpose(a, (2, 0, 1))
    n_cols = w_in.shape[2]
    col_blk = n_cols // 4
    win_spec = pl.BlockSpec((col_blk, DEPTH, LANES), lambda i: (i, 0, 0))
    r_in = adamw([to_cols(g_w_in)[None]], to_cols(w_in), to_cols(m_w_in), to_cols(v_w_in), (4,),
                 [pl.BlockSpec((1, col_blk, DEPTH, LANES), lambda i: (0, i, 0, 0))], win_spec, "adamw_w_in")
    r_in = [jnp.transpose(o, (1, 2, 0)) for o in r_in]

    recv[last] = exchange_wait(scatters[last], r_in[1], "scattered_ffn1_l0")
    r_ffn1_in = upd_transposed(both("ffn1", 0), w_ffn1_in, m_w_ffn1_in, v_w_ffn1_in, in_rows, "adamw_ffn1_in")
    r_ffn1_out = upd(both("ffn1", 1), w_ffn1_out, m_w_ffn1_out, v_w_ffn1_out, out_rows, "adamw_ffn1_out")

    small_sum = all_reduce_small(small_packed, "allreduce_small", deps=(r_ffn1_out[1],))
    n_small = sum(int(np.prod(s)) for s in small_shapes)
    loss = small_sum.reshape(-1)[n_small]
    sm_spec = pl.BlockSpec((SMALL_ROWS, LANES), lambda i: (0, 0))
    sm_out = adamw([small_sum[None]], _pack_small(small_w), _pack_small(small_m), _pack_small(small_v),
                   (1,), [pl.BlockSpec((1, SMALL_ROWS, LANES), lambda i: (0, 0, 0))], sm_spec, "adamw_small")
    sm_g, sm_d, sm_m, sm_v = [_unpack_small(o, small_shapes) for o in sm_out]

    def per_kind(k):
        small = (sm_g, sm_d, sm_m, sm_v)[k]
        return [small[0], r_ffn1_in[k], r_ffn1_out[k], small[1], r_in[k], small[2], small[3],
                r_br_sb[k], r_br_ch[k], r_br_fox[k], r_out[k], small[4], r_ffn2_in[k], r_ffn2_out[k],
                small[5]]

    return (loss, dx[None], *per_kind(0), *per_kind(1), *per_kind(2), *per_kind(3))
```

```python
import functools

import numpy as np
import jax
import jax.numpy as jnp
from jax import lax
from jax.experimental import pallas as pl
from jax.experimental.pallas import tpu as pltpu

F32 = jnp.float32
BF16 = jnp.bfloat16

N_DEV = 8
D_MODEL = 1024
DEPTH = 2
HEAD_DIM = 64
W_SB, W_CH, W_FOX = 256, 512, 256
QKV_WIDTH = 3 * (W_SB + W_CH + W_FOX)
N_HEADS_FOX = 4
N_HEADS_CH = 8
D_FF = 2816
FF_BLK = 2 * D_FF // N_DEV
CHUNK = 64
LEFT_CHUNKS = 8
MAX_REL = 128
N_REL = 2 * MAX_REL + 1
REL_PAD = 384
QB = 128
KB = 512
KSUB = KB // QB
CH_WIN = 5
CH_KEYS = CH_WIN * QB
RMS_EPS = 1e-6
NEG = -1e30
SCALE = HEAD_DIM ** -0.5
LANES = 128
VMEM_LIMIT = 56 * 1024 * 1024

ADAM_LR, ADAM_B1, ADAM_B2, ADAM_EPS, ADAM_WD, ADAM_STEP = 0.001, 0.9, 0.999, 1e-08, 0.01, 10

SMALL_ROWS = 192

MESH = pl.DeviceIdType.MESH
ANY = pl.BlockSpec(memory_space=pl.ANY)
HIGHEST = lax.Precision.HIGHEST

NN = (((1,), (0,)), ((), ()))
NT = (((1,), (1,)), ((), ()))
TN = (((0,), (0,)), ((), ()))


def _cparams(n_grid):
    return pltpu.CompilerParams(dimension_semantics=("arbitrary",) * n_grid,
                                vmem_limit_bytes=VMEM_LIMIT)


def _sds(shape, dtype):
    return jax.ShapeDtypeStruct(tuple(shape), dtype)


def _my_index():
    return 4 * lax.axis_index("x") + 2 * lax.axis_index("y") + lax.axis_index("c")


def _peer(mask):
    x, y, c = lax.axis_index("x"), lax.axis_index("y"), lax.axis_index("c")
    px = x ^ ((mask >> 2) & 1)
    py = y ^ ((mask >> 1) & 1)
    pc = c ^ (mask & 1)
    return (px, py, pc), 4 * px + 2 * py + pc


def all_gather(shard, name):
    s, r, c = shard.shape

    def body(in_ref, out_ref, send_sems, recv_sems, local_sem):
        me = _my_index()
        mine = pltpu.make_async_copy(in_ref, out_ref.at[:, me], local_sem)
        mine.start()
        sends = []
        for mask in range(1, N_DEV):
            peer, _ = _peer(mask)
            cp = pltpu.make_async_remote_copy(
                src_ref=in_ref, dst_ref=out_ref.at[:, me],
                send_sem=send_sems.at[mask - 1], recv_sem=recv_sems.at[mask - 1],
                device_id=peer, device_id_type=MESH)
            cp.start()
            sends.append(cp)
        for mask in range(1, N_DEV):
            peer, pidx = _peer(mask)
            pltpu.make_async_remote_copy(
                src_ref=in_ref, dst_ref=out_ref.at[:, pidx],
                send_sem=send_sems.at[mask - 1], recv_sem=recv_sems.at[mask - 1],
                device_id=peer, device_id_type=MESH).wait_recv()
        for cp in sends:
            cp.wait_send()
        mine.wait()

    return pl.pallas_call(
        body, name=name,
        out_shape=_sds((s, N_DEV, r, c), shard.dtype),
        in_specs=[ANY], out_specs=ANY,
        scratch_shapes=[pltpu.SemaphoreType.DMA((N_DEV - 1,)),
                        pltpu.SemaphoreType.DMA((N_DEV - 1,)),
                        pltpu.SemaphoreType.DMA],
    )(shard)


def all_to_all(parts, name):
    s, _, r, c = parts.shape

    def body(in_ref, out_ref, send_sems, recv_sems, local_sem):
        me = _my_index()
        mine = pltpu.make_async_copy(in_ref.at[:, me], out_ref.at[me], local_sem)
        mine.start()
        sends = []
        for mask in range(1, N_DEV):
            peer, pidx = _peer(mask)
            cp = pltpu.make_async_remote_copy(
                src_ref=in_ref.at[:, pidx], dst_ref=out_ref.at[me],
                send_sem=send_sems.at[mask - 1], recv_sem=recv_sems.at[mask - 1],
                device_id=peer, device_id_type=MESH)
            cp.start()
            sends.append(cp)
        for mask in range(1, N_DEV):
            peer, pidx = _peer(mask)
            pltpu.make_async_remote_copy(
                src_ref=in_ref.at[:, me], dst_ref=out_ref.at[pidx],
                send_sem=send_sems.at[mask - 1], recv_sem=recv_sems.at[mask - 1],
                device_id=peer, device_id_type=MESH).wait_recv()
        for cp in sends:
            cp.wait_send()
        mine.wait()

    return pl.pallas_call(
        body, name=name,
        out_shape=_sds((N_DEV, s, r, c), parts.dtype),
        in_specs=[ANY], out_specs=ANY,
        scratch_shapes=[pltpu.SemaphoreType.DMA((N_DEV - 1,)),
                        pltpu.SemaphoreType.DMA((N_DEV - 1,)),
                        pltpu.SemaphoreType.DMA],
    )(parts)


HBM_SPEC = pl.BlockSpec(memory_space=pltpu.HBM)
SEM_SPEC = pl.BlockSpec(memory_space=pltpu.SEMAPHORE)
EFFECT = pltpu.SideEffectType.DATAFLOW_SIDE_EFFECTING


def _exchange_refs(mode, in_ref, land_ref, me, pidx):
    if mode == "gather":
        return in_ref, land_ref.at[:, me], land_ref.at[:, pidx]
    return in_ref.at[:, pidx], land_ref.at[me], land_ref.at[pidx]


def _landing_shape(mode, a):
    if mode == "gather":
        s, r, c = a.shape
        return (s, N_DEV, r, c)
    s, _, r, c = a.shape
    return (N_DEV, s, r, c)


def _own_copy(mode, in_ref, land_ref, me, sem):
    if mode == "gather":
        return pltpu.make_async_copy(in_ref, land_ref.at[:, me], sem)
    return pltpu.make_async_copy(in_ref.at[:, me], land_ref.at[me], sem)


def exchange_start(mode, arrays, name, deps=()):
    n = len(arrays)
    lands0 = [lax.empty(_landing_shape(mode, a), a.dtype) for a in arrays]

    def body(*refs):
        in_refs, land_refs = refs[:n], refs[n:2 * n]
        outs_at = 2 * n + len(deps)
        send_sems, recv_sems, own_sems, token = refs[outs_at], refs[outs_at + 1], refs[outs_at + 2], refs[-1]
        mine = _my_index()
        for k in range(n):
            _own_copy(mode, in_refs[k], land_refs[k], mine, own_sems.at[k]).start()
            for mask in range(1, N_DEV):
                peer, pidx = _peer(mask)
                src, dst, _ = _exchange_refs(mode, in_refs[k], land_refs[k], mine, pidx)
                sem = k * (N_DEV - 1) + mask - 1
                pltpu.make_async_remote_copy(
                    src_ref=src, dst_ref=dst, send_sem=send_sems.at[sem], recv_sem=recv_sems.at[sem],
                    device_id=peer, device_id_type=MESH).start()
        token[...] = jnp.zeros_like(token)

    nsem = n * (N_DEV - 1)
    outs = pl.pallas_call(
        body, name=name,
        out_shape=(pltpu.SemaphoreType.DMA((nsem,)), pltpu.SemaphoreType.DMA((nsem,)),
                   pltpu.SemaphoreType.DMA((n,)),
                   *[pltpu.HBM(a.shape, a.dtype) for a in arrays],
                   *[pltpu.HBM(l.shape, l.dtype) for l in lands0], _sds((8, LANES), F32)),
        in_specs=[HBM_SPEC] * (2 * n) + [ANY] * len(deps),
        out_specs=(SEM_SPEC, SEM_SPEC, SEM_SPEC, *[HBM_SPEC] * (2 * n),
                   pl.BlockSpec(memory_space=pltpu.VMEM)),
        input_output_aliases={k: 3 + k for k in range(2 * n)},
        compiler_params=pltpu.CompilerParams(has_side_effects=EFFECT),
    )(*[pltpu.with_memory_space_constraint(a, pltpu.HBM) for a in arrays],
      *[pltpu.with_memory_space_constraint(l, pltpu.HBM) for l in lands0], *deps)
    return dict(mode=mode, n=n, send=outs[0], recv=outs[1], own=outs[2], ins=outs[3:3 + n],
                lands=outs[3 + n:3 + 2 * n], token=outs[-1])


def exchange_wait(handle, after, name):
    n, mode = handle["n"], handle["mode"]

    def body(*refs):
        in_refs, land_refs = refs[:n], refs[n:2 * n]
        send_sems, recv_sems, own_sems = refs[2 * n], refs[2 * n + 1], refs[2 * n + 2]
        mine = _my_index()
        for k in range(n):
            _own_copy(mode, in_refs[k], land_refs[k], mine, own_sems.at[k]).wait()
            for mask in range(1, N_DEV):
                peer, pidx = _peer(mask)
                src, _, here = _exchange_refs(mode, in_refs[k], land_refs[k], mine, pidx)
                sem = k * (N_DEV - 1) + mask - 1
                cp = pltpu.make_async_remote_copy(
                    src_ref=src, dst_ref=here, send_sem=send_sems.at[sem], recv_sem=recv_sems.at[sem],
                    device_id=peer, device_id_type=MESH)
                cp.wait_send()
                cp.wait_recv()

    thru = (*handle["ins"], *handle["lands"])
    outs = pl.pallas_call(
        body, name=name,
        out_shape=tuple(pltpu.HBM(a.shape, a.dtype) for a in thru),
        in_specs=[HBM_SPEC] * (2 * n) + [SEM_SPEC, SEM_SPEC, SEM_SPEC, ANY],
        out_specs=tuple([HBM_SPEC] * (2 * n)),
        input_output_aliases={k: k for k in range(2 * n)},
        compiler_params=pltpu.CompilerParams(has_side_effects=EFFECT),
    )(*thru, handle["send"], handle["recv"], handle["own"], after)
    return list(outs[n:])


FAR_MASKS = (2, 4, 6)
PHASE1_MASKS = (1,) + FAR_MASKS


def gather_start(arrays, name, deps=()):
    n = len(arrays)
    n1 = len(PHASE1_MASKS)
    lands0 = [lax.empty(_landing_shape("gather", a), a.dtype) for a in arrays]

    def body(*refs):
        in_refs, land_refs = refs[:n], refs[n:2 * n]
        outs_at = 2 * n + len(deps)
        send_sems, recv_sems, own_sems, token = refs[outs_at], refs[outs_at + 1], refs[outs_at + 2], refs[-1]
        mine = _my_index()
        for k in range(n):
            _own_copy("gather", in_refs[k], land_refs[k], mine, own_sems.at[k]).start()
            for j, mask in enumerate(PHASE1_MASKS):
                peer, _ = _peer(mask)
                pltpu.make_async_remote_copy(
                    src_ref=in_refs[k], dst_ref=land_refs[k].at[:, mine],
                    send_sem=send_sems.at[k * n1 + j], recv_sem=recv_sems.at[k * n1 + j],
                    device_id=peer, device_id_type=MESH).start()
        token[...] = jnp.zeros_like(token)

    outs = pl.pallas_call(
        body, name=name,
        out_shape=(pltpu.SemaphoreType.DMA((n * n1,)), pltpu.SemaphoreType.DMA((n * n1,)),
                   pltpu.SemaphoreType.DMA((n,)),
                   *[pltpu.HBM(a.shape, a.dtype) for a in arrays],
                   *[pltpu.HBM(l.shape, l.dtype) for l in lands0], _sds((8, LANES), F32)),
        in_specs=[HBM_SPEC] * (2 * n) + [ANY] * len(deps),
        out_specs=(SEM_SPEC, SEM_SPEC, SEM_SPEC, *[HBM_SPEC] * (2 * n),
                   pl.BlockSpec(memory_space=pltpu.VMEM)),
        input_output_aliases={k: 3 + k for k in range(2 * n)},
        compiler_params=pltpu.CompilerParams(has_side_effects=EFFECT),
    )(*[pltpu.with_memory_space_constraint(a, pltpu.HBM) for a in arrays],
      *[pltpu.with_memory_space_constraint(l, pltpu.HBM) for l in lands0], *deps)
    return dict(n=n, send=outs[0], recv=outs[1], own=outs[2], ins=outs[3:3 + n],
                lands=outs[3 + n:3 + 2 * n], token=outs[-1], name=name)


def gather_relay(handle, after):
    n = handle["n"]
    n1, n2 = len(PHASE1_MASKS), len(FAR_MASKS)

    def body(*refs):
        in_refs, land_refs = refs[:n], refs[n:2 * n]
        send1, recv1 = refs[2 * n], refs[2 * n + 1]
        send2, recv2 = refs[2 * n + 3], refs[2 * n + 4]
        sibling, _ = _peer(1)
        for k in range(n):
            for j, mask in enumerate(FAR_MASKS):
                peer, pidx = _peer(mask)
                landed = land_refs[k].at[:, pidx]
                pltpu.make_async_remote_copy(
                    src_ref=in_refs[k], dst_ref=landed, send_sem=send1.at[k * n1 + 1 + j],
                    recv_sem=recv1.at[k * n1 + 1 + j], device_id=peer, device_id_type=MESH).wait_recv()
                pltpu.make_async_remote_copy(
                    src_ref=landed, dst_ref=landed, send_sem=send2.at[k * n2 + j],
                    recv_sem=recv2.at[k * n2 + j], device_id=sibling, device_id_type=MESH).start()

    thru = (*handle["ins"], *handle["lands"])
    outs = pl.pallas_call(
        body, name=handle["name"] + "_relay",
        out_shape=(pltpu.SemaphoreType.DMA((n * n2,)), pltpu.SemaphoreType.DMA((n * n2,)),
                   *[pltpu.HBM(a.shape, a.dtype) for a in thru]),
        in_specs=[HBM_SPEC] * (2 * n) + [SEM_SPEC, SEM_SPEC, ANY],
        out_specs=(SEM_SPEC, SEM_SPEC, *[HBM_SPEC] * (2 * n)),
        input_output_aliases={k: 2 + k for k in range(2 * n)},
        compiler_params=pltpu.CompilerParams(has_side_effects=EFFECT),
    )(*thru, handle["send"], handle["recv"], after)
    handle.update(send2=outs[0], recv2=outs[1], ins=outs[2:2 + n], lands=outs[2 + n:2 + 2 * n])


def gather_finish(handle, after):
    n = handle["n"]
    n1, n2 = len(PHASE1_MASKS), len(FAR_MASKS)

    def body(*refs):
        in_refs, land_refs = refs[:n], refs[n:2 * n]
        send1, recv1, own_sems, send2, recv2 = refs[2 * n:2 * n + 5]
        mine = _my_index()
        sibling, sib_idx = _peer(1)
        for k in range(n):
            _own_copy("gather", in_refs[k], land_refs[k], mine, own_sems.at[k]).wait()
            for j, mask in enumerate(PHASE1_MASKS):
                peer, pidx = _peer(mask)
                cp = pltpu.make_async_remote_copy(
                    src_ref=in_refs[k], dst_ref=land_refs[k].at[:, pidx], send_sem=send1.at[k * n1 + j],
                    recv_sem=recv1.at[k * n1 + j], device_id=peer, device_id_type=MESH)
                cp.wait_send()
                if mask == 1:
                    cp.wait_recv()
            for j, mask in enumerate(FAR_MASKS):
                _, pidx = _peer(mask)
                _, far_of_sibling = _peer(mask ^ 1)
                cp = pltpu.make_async_remote_copy(
                    src_ref=land_refs[k].at[:, pidx], dst_ref=land_refs[k].at[:, far_of_sibling],
                    send_sem=send2.at[k * n2 + j], recv_sem=recv2.at[k * n2 + j],
                    device_id=sibling, device_id_type=MESH)
                cp.wait_send()
                cp.wait_recv()

    thru = (*handle["ins"], *handle["lands"])
    outs = pl.pallas_call(
        body, name=handle["name"] + "_finish",
        out_shape=tuple(pltpu.HBM(a.shape, a.dtype) for a in thru),
        in_specs=[HBM_SPEC] * (2 * n) + [SEM_SPEC] * 5 + [ANY],
        out_specs=tuple([HBM_SPEC] * (2 * n)),
        input_output_aliases={k: k for k in range(2 * n)},
        compiler_params=pltpu.CompilerParams(has_side_effects=EFFECT),
    )(*thru, handle["send"], handle["recv"], handle["own"], handle["send2"], handle["recv2"], after)
    return list(outs[n:])


def all_reduce_small(packed, name, deps=()):
    rows = packed.shape[0]
    nd = len(deps)

    def body(in_ref, *rest):
        out_ref, slots, send_sems, recv_sems = rest[nd:]
        me = _my_index()
        sends = []
        for mask in range(1, N_DEV):
            peer, _ = _peer(mask)
            cp = pltpu.make_async_remote_copy(
                src_ref=in_ref, dst_ref=slots.at[me],
                send_sem=send_sems.at[mask - 1], recv_sem=recv_sems.at[mask - 1],
                device_id=peer, device_id_type=MESH)
            cp.start()
            sends.append(cp)
        slots[me] = in_ref[...]
        for mask in range(1, N_DEV):
            peer, pidx = _peer(mask)
            pltpu.make_async_remote_copy(
                src_ref=in_ref, dst_ref=slots.at[pidx],
                send_sem=send_sems.at[mask - 1], recv_sem=recv_sems.at[mask - 1],
                device_id=peer, device_id_type=MESH).wait_recv()
        for cp in sends:
            cp.wait_send()
        total = slots[0]
        for p in range(1, N_DEV):
            total = total + slots[p]
        out_ref[...] = total

    return pl.pallas_call(
        body, name=name,
        out_shape=_sds((rows, LANES), F32),
        in_specs=[pl.BlockSpec(memory_space=pltpu.VMEM)] + [ANY] * nd,
        out_specs=pl.BlockSpec(memory_space=pltpu.VMEM),
        scratch_shapes=[pltpu.VMEM((N_DEV, rows, LANES), F32),
                        pltpu.SemaphoreType.DMA((N_DEV - 1,)),
                        pltpu.SemaphoreType.DMA((N_DEV - 1,))],
    )(packed, *deps)


def matmul(dims, a, b, out_sds, grid, a_spec, b_spec, o_spec, acc_shape, *, name, alpha=1.0,
           bias=None, bias_spec=None, scale=None, scale_spec=None, res=None, res_spec=None,
           colsum_sds=None, colsum_spec=None, out_t_sds=None, out_t_spec=None, deps=()):
    nk = grid[2]
    has_bias, has_scale, has_res = bias is not None, scale is not None, res is not None
    has_cs, has_t = colsum_sds is not None, out_t_sds is not None
    if has_cs:
        assert grid[0] == 1 and dims == TN

    def body(*refs):
        a_ref, b_ref = refs[0], refs[1]
        pos = 2
        bias_ref = scale_ref = res_ref = cs_ref = ot_ref = None
        if has_bias:
            bias_ref = refs[pos]; pos += 1
        if has_scale:
            scale_ref = refs[pos]; pos += 1
        if has_res:
            res_ref = refs[pos]; pos += 1
        pos += len(deps)
        o_ref = refs[pos]; pos += 1
        if has_cs:
            cs_ref = refs[pos]; pos += 1
        if has_t:
            ot_ref = refs[pos]; pos += 1
        k = pl.program_id(2)
        bval = b_ref[...]
        part = lax.dot_general(a_ref[...].astype(BF16), bval.astype(BF16), dims,
                               preferred_element_type=F32)

        def finish(total):
            r = total * alpha if alpha != 1.0 else total
            if has_bias:
                r = r + bias_ref[...]
            if has_scale:
                r = r * scale_ref[...]
            if has_res:
                r = r + res_ref[...].astype(F32)
            o_ref[...] = r.astype(o_ref.dtype)
            if has_t:
                ot_ref[...] = r.T.astype(ot_ref.dtype)

        if has_cs:
            csum = jnp.sum(bval.astype(F32), axis=0, keepdims=True)

            @pl.when(k == 0)
            def _():
                cs_ref[...] = csum

            @pl.when(k > 0)
            def _():
                cs_ref[...] += csum

        if nk == 1:
            finish(part)
        else:
            acc_ref = refs[pos]

            @pl.when(k == 0)
            def _():
                acc_ref[...] = part

            @pl.when(k > 0)
            def _():
                acc_ref[...] += part

            @pl.when(k == nk - 1)
            def _():
                finish(acc_ref[...])

    in_specs, args = [a_spec, b_spec], [a, b]
    if has_bias:
        in_specs.append(bias_spec); args.append(bias)
    if has_scale:
        in_specs.append(scale_spec); args.append(scale)
    if has_res:
        in_specs.append(res_spec); args.append(res)
    in_specs += [ANY] * len(deps)
    args += list(deps)
    out_shape, out_specs = [out_sds], [o_spec]
    if has_cs:
        out_shape.append(colsum_sds); out_specs.append(colsum_spec)
    if has_t:
        out_shape.append(out_t_sds); out_specs.append(out_t_spec)
    scratch = [] if nk == 1 else [pltpu.VMEM(acc_shape, F32)]
    outs = pl.pallas_call(
        body, name=name, grid=grid, in_specs=in_specs, out_specs=out_specs, out_shape=out_shape,
        scratch_shapes=scratch, compiler_params=_cparams(3))(*args)
    return outs if (has_cs or has_t) else outs[0]


def _sigmoid(z):
    return 1.0 / (1.0 + jnp.exp(-z))


def _log_sigmoid(z):
    return jnp.minimum(z, 0.0) - jnp.log(1.0 + jnp.exp(-jnp.abs(z)))


def rmsnorm_fwd(x, gain, tm, name, deps=()):
    t, d = x.shape

    def body(x_ref, g_ref, *rest):
        o_ref = rest[-1]
        xf = x_ref[...]
        r = lax.rsqrt(jnp.mean(xf * xf, axis=-1, keepdims=True) + RMS_EPS)
        o_ref[...] = (xf * r * g_ref[...]).astype(o_ref.dtype)

    return pl.pallas_call(
        body, name=name, grid=(t // tm,),
        in_specs=[pl.BlockSpec((tm, d), lambda i: (i, 0)), pl.BlockSpec((1, d), lambda i: (0, 0))]
        + [ANY] * len(deps),
        out_specs=pl.BlockSpec((tm, d), lambda i: (i, 0)),
        out_shape=_sds((t, d), BF16), compiler_params=_cparams(1))(x, gain, *deps)


def rmsnorm_bwd(x, gain, dh, dres, tm, name):
    t, d = x.shape

    def body(x_ref, g_ref, dh_ref, dres_ref, dx_ref, dg_ref):
        i = pl.program_id(0)
        xf = x_ref[...]
        r = lax.rsqrt(jnp.mean(xf * xf, axis=-1, keepdims=True) + RMS_EPS)
        xhat = xf * r
        dh_v = dh_ref[...]
        dxhat = dh_v * g_ref[...]
        dx = r * (dxhat - xhat * jnp.mean(dxhat * xhat, axis=-1, keepdims=True))
        dx_ref[...] = dres_ref[...] + dx
        dg = jnp.sum(dh_v * xhat, axis=0, keepdims=True)

        @pl.when(i == 0)
        def _():
            dg_ref[...] = dg

        @pl.when(i > 0)
        def _():
            dg_ref[...] += dg

    row = pl.BlockSpec((tm, d), lambda i: (i, 0))
    vec = pl.BlockSpec((1, d), lambda i: (0, 0))
    return pl.pallas_call(
        body, name=name, grid=(t // tm,), in_specs=[row, vec, row, row], out_specs=[row, vec],
        out_shape=[_sds((t, d), F32), _sds((1, d), F32)], compiler_params=_cparams(1))(x, gain, dh, dres)


def loss_head(x, gain, target, tm, name):
    t, d = x.shape

    def body(x_ref, g_ref, tgt_ref, dx_ref, dg_ref, loss_ref):
        i = pl.program_id(0)
        xf = x_ref[...]
        g = g_ref[...]
        r = lax.rsqrt(jnp.mean(xf * xf, axis=-1, keepdims=True) + RMS_EPS)
        xhat = xf * r
        err = xhat * g - tgt_ref[...]
        part = 0.5 * jnp.sum(jnp.mean(err * err, axis=-1, keepdims=True))
        dy = err * (1.0 / d)
        dxhat = dy * g
        dx_ref[...] = r * (dxhat - xhat * jnp.mean(dxhat * xhat, axis=-1, keepdims=True))
        dg = jnp.sum(dy * xhat, axis=0, keepdims=True)
        lpart = jnp.full((8, LANES), part, F32)

        @pl.when(i == 0)
        def _():
            dg_ref[...] = dg
            loss_ref[...] = lpart

        @pl.when(i > 0)
        def _():
            dg_ref[...] += dg
            loss_ref[...] += lpart

    row = pl.BlockSpec((tm, d), lambda i: (i, 0))
    vec = pl.BlockSpec((1, d), lambda i: (0, 0))
    return pl.pallas_call(
        body, name=name, grid=(t // tm,), in_specs=[row, vec, row],
        out_specs=[row, vec, pl.BlockSpec((8, LANES), lambda i: (0, 0))],
        out_shape=[_sds((t, d), F32), _sds((1, d), F32), _sds((8, LANES), F32)],
        compiler_params=_cparams(1))(x, gain, target)


def ffn_in_swiglu(hn, wa, s, tm, name):
    t = hn.shape[0]

    def body(h_ref, wg_ref, wu_ref, gu_ref, act_ref):
        h = h_ref[...]
        g = jnp.dot(h, wg_ref[...], preferred_element_type=F32)
        u = jnp.dot(h, wu_ref[...], preferred_element_type=F32)
        sg = _sigmoid(g)
        silu = g * sg
        gu_ref[0] = (0.5 * (sg + silu * (1.0 - sg))).astype(gu_ref.dtype)
        gu_ref[1] = (0.5 * silu).astype(gu_ref.dtype)
        gu_ref[2] = u.astype(gu_ref.dtype)
        act_ref[...] = (silu * u).astype(act_ref.dtype)

    return pl.pallas_call(
        body, name=name, grid=(t // tm, 4),
        in_specs=[pl.BlockSpec((tm, D_MODEL), lambda i, j: (i, 0)),
                  pl.BlockSpec((None, None, D_MODEL, FF_BLK), lambda i, j: (s, j, 0, 0)),
                  pl.BlockSpec((None, None, D_MODEL, FF_BLK), lambda i, j: (s, j + 4, 0, 0))],
        out_specs=[pl.BlockSpec((None, 3, tm, FF_BLK), lambda i, j: (j, 0, i, 0)),
                   pl.BlockSpec((None, tm, FF_BLK), lambda i, j: (j, i, 0))],
        out_shape=[_sds((4, 3, t, FF_BLK), BF16), _sds((4, t, FF_BLK), BF16)],
        compiler_params=_cparams(2))(hn, wa, wa)


def ffn_dact_swiglu(dy, wb, gu, s, tm, name):
    t = dy.shape[0]

    def body(dy_ref, w_ref, gu_ref, o_ref):
        da = lax.dot_general(dy_ref[...].astype(BF16), w_ref[...], NT, preferred_element_type=F32)
        o_ref[0] = (da * gu_ref[2].astype(F32) * gu_ref[0].astype(F32)).astype(o_ref.dtype)
        o_ref[1] = (da * gu_ref[1].astype(F32)).astype(o_ref.dtype)

    return pl.pallas_call(
        body, name=name, grid=(t // tm, 4),
        in_specs=[pl.BlockSpec((tm, D_MODEL), lambda i, j: (i, 0)),
                  pl.BlockSpec((None, None, FF_BLK, D_MODEL), lambda i, j: (s, j, 0, 0)),
                  pl.BlockSpec((None, 3, tm, FF_BLK), lambda i, j: (j, 0, i, 0))],
        out_specs=pl.BlockSpec((None, 2, tm, FF_BLK), lambda i, j: (j, 0, i, 0)),
        out_shape=_sds((4, 2, t, FF_BLK), BF16), compiler_params=_cparams(2))(dy, wb, gu)


def ffn_out_residual(act, wb, x, s, tm, name):
    t = x.shape[0]

    def body(a_ref, w_ref, x_ref, o_ref):
        acc = jnp.dot(a_ref[0], w_ref[0], preferred_element_type=F32)
        for k in range(1, 4):
            acc = acc + jnp.dot(a_ref[k], w_ref[k], preferred_element_type=F32)
        o_ref[...] = x_ref[...] + 0.5 * acc

    row = pl.BlockSpec((tm, D_MODEL), lambda i: (i, 0))
    return pl.pallas_call(
        body, name=name, grid=(t // tm,),
        in_specs=[pl.BlockSpec((4, tm, FF_BLK), lambda i: (0, i, 0)),
                  pl.BlockSpec((None, 4, FF_BLK, D_MODEL), lambda i: (s, 0, 0, 0)), row],
        out_specs=row, out_shape=_sds((t, D_MODEL), F32), compiler_params=_cparams(1))(act, wb, x)


def ffn_dh_norm_bwd(dgu, wa, s, x, gain, dres, tm, name, deps):
    t = dgu.shape[2]
    nd = len(deps)

    def body(g_ref, w_ref, x_ref, gain_ref, dres_ref, *rest):
        dx_ref, dg_ref = rest[nd:]
        i = pl.program_id(0)
        dh = lax.dot_general(g_ref[0, 0], w_ref[0], NT, preferred_element_type=F32)
        for p in range(1, N_DEV):
            dh = dh + lax.dot_general(g_ref[p % 4, p // 4], w_ref[p], NT, preferred_element_type=F32)
        xf = x_ref[...]
        r = lax.rsqrt(jnp.mean(xf * xf, axis=-1, keepdims=True) + RMS_EPS)
        xhat = xf * r
        dxhat = dh * gain_ref[...]
        dx_ref[...] = dres_ref[...] + r * (dxhat - xhat * jnp.mean(dxhat * xhat, axis=-1, keepdims=True))
        dg = jnp.sum(dh * xhat, axis=0, keepdims=True)

        @pl.when(i == 0)
        def _():
            dg_ref[...] = dg

        @pl.when(i > 0)
        def _():
            dg_ref[...] += dg

    row = pl.BlockSpec((tm, D_MODEL), lambda i: (i, 0))
    vec = pl.BlockSpec((1, D_MODEL), lambda i: (0, 0))
    return pl.pallas_call(
        body, name=name, grid=(t // tm,),
        in_specs=[pl.BlockSpec((4, 2, tm, FF_BLK), lambda i: (0, 0, i, 0)),
                  pl.BlockSpec((None, N_DEV, D_MODEL, FF_BLK), lambda i: (s, 0, 0, 0)), row, vec, row]
        + [ANY] * nd,
        out_specs=[row, vec], out_shape=[_sds((t, D_MODEL), F32), _sds((1, D_MODEL), F32)],
        compiler_params=_cparams(1))(dgu, wa, x, gain, dres, *deps)


def merge_fwd(gates, ya, yb, yc, tm, name):
    t, d = ya.shape

    def body(ga_ref, gb_ref, gc_ref, ya_ref, yb_ref, yc_ref, o_ref):
        m = (_sigmoid(ga_ref[...]) * ya_ref[...] + _sigmoid(gb_ref[...]) * yb_ref[...]
             + _sigmoid(gc_ref[...]) * yc_ref[...])
        o_ref[...] = m.astype(o_ref.dtype)

    row = pl.BlockSpec((tm, d), lambda i: (i, 0))
    gspecs = [pl.BlockSpec((tm, d), functools.partial(lambda i, a: (i, a), a=a)) for a in range(3)]
    return pl.pallas_call(
        body, name=name, grid=(t // tm,), in_specs=gspecs + [row, row, row], out_specs=row,
        out_shape=_sds((t, d), BF16), compiler_params=_cparams(1))(gates, gates, gates, ya, yb, yc)


def merge_bwd(dm, gates, ya, yb, yc, tm, name):
    t, d = ya.shape

    def body(dm_ref, g_ref, ya_ref, yb_ref, yc_ref, dg_ref, dya_ref, dyb_ref, dyc_ref):
        dmv = dm_ref[...]
        for a, (y_ref, dy_ref) in enumerate(((ya_ref, dya_ref), (yb_ref, dyb_ref), (yc_ref, dyc_ref))):
            cols = slice(a * d, (a + 1) * d)
            s = _sigmoid(g_ref[:, cols])
            dy_ref[...] = (dmv * s).astype(dy_ref.dtype)
            dg_ref[:, cols] = (dmv * y_ref[...] * s * (1.0 - s)).astype(dg_ref.dtype)

    row = pl.BlockSpec((tm, d), lambda i: (i, 0))
    wide = pl.BlockSpec((tm, 3 * d), lambda i: (i, 0))
    dg, dya, dyb, dyc = pl.pallas_call(
        body, name=name, grid=(t // tm,), in_specs=[row, wide, row, row, row],
        out_specs=[wide, row, row, row],
        out_shape=[_sds((t, 3 * d), BF16)] + [_sds((t, d), BF16)] * 3,
        compiler_params=_cparams(1))(dm, gates, ya, yb, yc)
    return dg, [dya, dyb, dyc]


def _iota2(shape, dim):
    return lax.broadcasted_iota(jnp.int32, shape, dim)


def forget_cumsum(f, name):
    t = f.shape[0]
    nq = t // QB

    def body(f_ref, fcol_ref, frow_ref, carry):
        j = pl.program_id(0)

        @pl.when(j == 0)
        def _():
            carry[...] = jnp.zeros_like(carry)

        logf = _log_sigmoid(f_ref[...])
        tri = (_iota2((QB, QB), 1) <= _iota2((QB, QB), 0)).astype(F32)
        blk = jnp.dot(tri, logf, precision=HIGHEST, preferred_element_type=F32) + carry[...]
        carry[...] += jnp.sum(logf, axis=0, keepdims=True)
        fcol_ref[...] = blk
        frow_ref[...] = blk.T[0:8, :]

    return pl.pallas_call(
        body, name=name, grid=(nq,),
        in_specs=[pl.BlockSpec((QB, LANES), lambda j: (j, 0))],
        out_specs=[pl.BlockSpec((QB, LANES), lambda j: (j, 0)),
                   pl.BlockSpec((None, 8, QB), lambda j: (j, 0, 0))],
        out_shape=[_sds((t, LANES), F32), _sds((nq, 8, QB), F32)],
        scratch_shapes=[pltpu.VMEM((1, LANES), F32)], compiler_params=_cparams(1))(f)


def forget_cumsum_bwd(dfrow, f, name):
    t = f.shape[0]
    nq = t // QB

    def body(dfr_ref, f_ref, df_ref, carry):
        jj = pl.program_id(0)

        @pl.when(jj == 0)
        def _():
            carry[...] = jnp.zeros_like(carry)

        padded = jnp.concatenate([dfr_ref[...], jnp.zeros((QB - 8, QB), F32)], axis=0)
        dfcol = padded.T
        tri = (_iota2((QB, QB), 1) >= _iota2((QB, QB), 0)).astype(F32)
        dlogf = jnp.dot(tri, dfcol, precision=HIGHEST, preferred_element_type=F32) + carry[...]
        carry[...] += jnp.sum(dfcol, axis=0, keepdims=True)
        df_ref[...] = dlogf * _sigmoid(-f_ref[...])

    return pl.pallas_call(
        body, name=name, grid=(nq,),
        in_specs=[pl.BlockSpec((None, 8, QB), lambda jj: (nq - 1 - jj, 0, 0)),
                  pl.BlockSpec((QB, LANES), lambda jj: (nq - 1 - jj, 0))],
        out_specs=pl.BlockSpec((QB, LANES), lambda jj: (nq - 1 - jj, 0)),
        out_shape=_sds((t, LANES), F32),
        scratch_shapes=[pltpu.VMEM((1, LANES), F32)], compiler_params=_cparams(1))(dfrow, f)


REL_DIAG = 768
REL_SHIFT = REL_DIAG - (QB - 1)


def _diag_onehot():
    u = _iota2((REL_PAD, REL_DIAG), 1)
    rel = jnp.clip(CH_KEYS - 1 - u, -MAX_REL, MAX_REL) + MAX_REL
    return (_iota2((REL_PAD, REL_DIAG), 0) == rel).astype(F32)


def rel_bias_build(tab_t, name):
    def body(tab_ref, o_ref):
        diag = jnp.dot(tab_ref[...], _diag_onehot(), precision=HIGHEST, preferred_element_type=F32)
        band = _chunk_band()
        for h in range(N_HEADS_CH):
            rows = jnp.broadcast_to(diag[h:h + 1, :], (QB, REL_DIAG))
            o_ref[h] = pltpu.roll(rows, REL_SHIFT, 1, stride=1, stride_axis=0)[:, :CH_KEYS] + band

    return pl.pallas_call(
        body, name=name, out_shape=_sds((N_HEADS_CH, QB, CH_KEYS), F32),
        in_specs=[pl.BlockSpec(memory_space=pltpu.VMEM)], out_specs=pl.BlockSpec(memory_space=pltpu.VMEM),
    )(tab_t)


def rel_bias_scatter(dbias, name):
    def body(db_ref, o_ref, ddiag):
        flip = (_iota2((QB, QB), 0) + _iota2((QB, QB), 1) == QB - 1).astype(F32)
        for h in range(N_HEADS_CH):
            padded = jnp.concatenate([db_ref[h], jnp.zeros((QB, REL_DIAG - CH_KEYS), F32)], axis=1)
            flipped = jnp.dot(flip, padded, precision=HIGHEST, preferred_element_type=F32)
            unrolled = pltpu.roll(flipped, 0, 1, stride=1, stride_axis=0)
            ddiag[h:h + 1, :] = jnp.sum(unrolled, axis=0, keepdims=True)
        o_ref[...] = lax.dot_general(ddiag[...], _diag_onehot(), NT, precision=HIGHEST,
                                     preferred_element_type=F32)

    return pl.pallas_call(
        body, name=name, out_shape=_sds((N_HEADS_CH, REL_PAD), F32),
        in_specs=[pl.BlockSpec(memory_space=pltpu.VMEM)], out_specs=pl.BlockSpec(memory_space=pltpu.VMEM),
        scratch_shapes=[pltpu.VMEM((N_HEADS_CH, REL_DIAG), F32)],
    )(dbias)


def _hl(h):
    return slice(h * HEAD_DIM, (h + 1) * HEAD_DIM)


def _split_dot(x, tri_bf16):
    hi = x.astype(BF16)
    lo = (x - hi.astype(F32)).astype(BF16)
    return (jnp.dot(hi, tri_bf16, preferred_element_type=F32)
            + jnp.dot(lo, tri_bf16, preferred_element_type=F32))


def _rows(j):
    return pl.ds(pl.multiple_of(j * QB, QB), QB)


def _krows(g):
    return pl.ds(pl.multiple_of(g * KB, KB), KB)


def _log_sigmoid_pair(z):
    sp = jnp.log(1.0 + jnp.exp(-jnp.abs(z)))
    return jnp.minimum(z, 0.0) - sp, -jnp.maximum(z, 0.0) - sp


def _qkv_specs(t, col0, n_pairs):
    q_spec = pl.BlockSpec((QB, LANES), lambda hp, i: (i, col0 + hp))
    k_spec = pl.BlockSpec((t, LANES), lambda hp, i: (0, col0 + n_pairs + hp))
    v_spec = pl.BlockSpec((t, LANES), lambda hp, i: (0, col0 + 2 * n_pairs + hp))
    return q_spec, k_spec, v_spec


def _keys_major(xt):
    pairs, groups, _, _ = xt.shape
    return xt.transpose(1, 3, 0, 2).reshape(groups * KB, pairs * LANES)


def sb_fwd(qkv, name):
    t = qkv.shape[0]
    nq = t // QB

    def body(q_ref, k_ref, v_ref, o_ref, w_ref):
        i = pl.program_id(1)
        groups = i // KSUB + 1
        tri_after = (_iota2((KB, KB), 0) > _iota2((KB, KB), 1)).astype(BF16)
        t_idx = i * QB + _iota2((QB, KB), 0)
        qs = [q_ref[:, _hl(h)] for h in range(2)]

        def step(g, carry, masked):
            strict = (g * KB + _iota2((QB, KB), 1)) < t_idx
            out = []
            for h in range(2):
                tail, acc = carry[2 * h], carry[2 * h + 1]
                k = k_ref[_krows(g), _hl(h)]
                v = v_ref[_krows(g), _hl(h)]
                z = lax.dot_general(qs[h], k, NT, preferred_element_type=F32)
                lb, lf = _log_sigmoid_pair(z)
                if masked:
                    lf = jnp.where(strict, lf, 0.0)
                between = _split_dot(lf, tri_after) + tail
                w = jnp.exp(lb + between)
                if masked:
                    w = jnp.where(strict, w, 0.0)
                w = w.astype(BF16)
                w_ref[h, g] = w
                acc = acc + jnp.dot(w, v, preferred_element_type=F32)
                out += [tail + jnp.sum(lf, axis=1, keepdims=True), acc]
            return tuple(out)

        init = (jnp.zeros((QB, 1), F32), jnp.zeros((QB, HEAD_DIM), F32)) * 2
        res = step(groups - 1, init, True)
        res = lax.fori_loop(0, groups - 1, lambda gg, c: step(groups - 2 - gg, c, False), res)
        for h in range(2):
            o_ref[:, _hl(h)] = res[2 * h + 1].astype(o_ref.dtype)

    q_spec, k_spec, v_spec = _qkv_specs(t, 0, 2)
    return pl.pallas_call(
        body, name=name, grid=(2, nq), in_specs=[q_spec, k_spec, v_spec],
        out_specs=[pl.BlockSpec((QB, LANES), lambda hp, i: (i, hp)),
                   pl.BlockSpec((2, None, t // KB, QB, KB), lambda hp, i: (hp, i, 0, 0, 0))],
        out_shape=[_sds((t, W_SB), BF16), _sds((4, nq, t // KB, QB, KB), BF16)],
        compiler_params=_cparams(2))(qkv, qkv, qkv)


def _hs(h):
    return slice(h * HEAD_DIM, (h + 1) * HEAD_DIM)


def sb_bwd(qkv, qkv_t, w, do, do_t, name):
    t = qkv.shape[0]
    nq = t // QB

    def body(q_ref, k_ref, v_ref, do_ref, qt_ref, dot_ref, w_ref, dq_ref, dkt_ref, dvt_ref):
        i = pl.program_id(1)

        @pl.when(i == 0)
        def _():
            dkt_ref[...] = jnp.zeros_like(dkt_ref)
            dvt_ref[...] = jnp.zeros_like(dvt_ref)

        groups = i // KSUB + 1
        tri_before = (_iota2((KB, KB), 0) < _iota2((KB, KB), 1)).astype(BF16)
        t_idx = i * QB + _iota2((QB, KB), 0)
        qs = [q_ref[:, _hl(h)] for h in range(2)]
        dos = [do_ref[:, _hl(h)] for h in range(2)]
        qts = [qt_ref[_hs(h), :] for h in range(2)]
        dots = [dot_ref[_hs(h), :] for h in range(2)]

        def grads(g, carry, masked):
            strict = (g * KB + _iota2((QB, KB), 1)) < t_idx
            out = []
            for h in range(2):
                head, dq = carry[2 * h], carry[2 * h + 1]
                k = k_ref[_krows(g), _hl(h)]
                v = v_ref[_krows(g), _hl(h)]
                wb = w_ref[h, g]
                z = lax.dot_general(qs[h], k, NT, preferred_element_type=F32)
                beta = _sigmoid(z)
                e = lax.dot_general(dos[h], v, NT, preferred_element_type=F32) * wb.astype(F32)
                before = _split_dot(e, tri_before) + head
                dz = e * (1.0 - beta) - before * beta
                if masked:
                    dz = jnp.where(strict, dz, 0.0)
                dzb = dz.astype(BF16)
                dq = dq + jnp.dot(dzb, k, preferred_element_type=F32)
                dkt_ref[g, _hs(h), :] += jnp.dot(qts[h], dzb, preferred_element_type=F32)
                dvt_ref[g, _hs(h), :] += jnp.dot(dots[h], wb, preferred_element_type=F32)
                out += [head + jnp.sum(e, axis=1, keepdims=True), dq]
            return tuple(out)

        init = (jnp.zeros((QB, 1), F32), jnp.zeros((QB, HEAD_DIM), F32)) * 2
        res = lax.fori_loop(0, groups - 1, lambda g, c: grads(g, c, False), init)
        res = grads(groups - 1, res, True)
        for h in range(2):
            dq_ref[:, _hl(h)] = (res[2 * h + 1] * SCALE).astype(dq_ref.dtype)

    q_spec, k_spec, v_spec = _qkv_specs(t, 0, 2)
    blk = pl.BlockSpec((QB, LANES), lambda hp, i: (i, hp))
    blk_t = pl.BlockSpec((LANES, QB), lambda hp, i: (hp, i))
    acc_t = pl.BlockSpec((None, t // KB, LANES, KB), lambda hp, i: (hp, 0, 0, 0))
    acc_sds = _sds((2, t // KB, LANES, KB), F32)
    return pl.pallas_call(
        body, name=name, grid=(2, nq),
        in_specs=[q_spec, k_spec, v_spec, blk, blk_t, blk_t,
                  pl.BlockSpec((2, None, t // KB, QB, KB), lambda hp, i: (hp, i, 0, 0, 0))],
        out_specs=[blk, acc_t, acc_t],
        out_shape=[_sds((t, W_SB), BF16), acc_sds, acc_sds],
        compiler_params=_cparams(2))(qkv, qkv, qkv, do, qkv_t, do_t, w)


def fox_fwd(qkv, fcol, frow, name):
    t = qkv.shape[0]
    nq = t // QB

    def body(q_ref, k_ref, v_ref, fc_ref, fr_ref, o_ref, lse_ref):
        hp = pl.program_id(0)
        i = pl.program_id(1)
        groups = i // KSUB + 1
        t_idx = i * QB + _iota2((QB, KB), 0)
        lane = _iota2((QB, LANES), 1)
        sub = _iota2((8, KB), 0)
        qs = [q_ref[:, _hl(h)] for h in range(2)]
        f_qs = [jnp.sum(jnp.where(lane == hp * 2 + h, fc_ref[...], 0.0), axis=1, keepdims=True)
                for h in range(2)]

        def step(g, carry, masked):
            causal = (g * KB + _iota2((QB, KB), 1)) <= t_idx
            fr = fr_ref[g]
            out = []
            for h in range(2):
                m, l, acc = carry[3 * h:3 * h + 3]
                k = k_ref[_krows(g), _hl(h)]
                v = v_ref[_krows(g), _hl(h)]
                f_k = jnp.sum(jnp.where(sub == hp * 2 + h, fr, 0.0), axis=0, keepdims=True)
                z = lax.dot_general(qs[h], k, NT, preferred_element_type=F32) + f_qs[h] - f_k
                if masked:
                    z = jnp.where(causal, z, NEG)
                m_new = jnp.maximum(m, jnp.max(z, axis=1, keepdims=True))
                p = jnp.exp(z - m_new)
                corr = jnp.exp(m - m_new)
                l = l * corr + jnp.sum(p, axis=1, keepdims=True)
                acc = acc * corr + jnp.dot(p.astype(BF16), v, preferred_element_type=F32)
                out += [m_new, l, acc]
            return tuple(out)

        init = (jnp.full((QB, 1), NEG, F32), jnp.zeros((QB, 1), F32), jnp.zeros((QB, HEAD_DIM), F32)) * 2
        res = lax.fori_loop(0, groups - 1, lambda g, c: step(g, c, False), init)
        res = step(groups - 1, res, True)
        for h in range(2):
            m, l, acc = res[3 * h:3 * h + 3]
            o_ref[:, _hl(h)] = (acc / l).astype(o_ref.dtype)
            lse_ref[:, _hl(h)] = jnp.broadcast_to(m + jnp.log(l), (QB, HEAD_DIM))

    q_spec, k_spec, v_spec = _qkv_specs(t, 18, 2)
    blk = pl.BlockSpec((QB, LANES), lambda hp, i: (i, hp))
    return pl.pallas_call(
        body, name=name, grid=(2, nq),
        in_specs=[q_spec, k_spec, v_spec, pl.BlockSpec((QB, LANES), lambda hp, i: (i, 0)),
                  pl.BlockSpec((t // KB, 8, KB), lambda hp, i: (0, 0, 0))],
        out_specs=[blk, blk],
        out_shape=[_sds((t, W_FOX), BF16), _sds((t, W_FOX), F32)],
        compiler_params=_cparams(2))(qkv, qkv, qkv, fcol, frow)


def fox_bwd(qkv, qkv_t, fcol, frow, o, lse, do, do_t, name):
    t = qkv.shape[0]
    nq = t // QB

    def body(q_ref, k_ref, v_ref, fc_ref, fr_ref, o_ref, lse_ref, do_ref, qt_ref, dot_ref,
             dq_ref, dk_ref, dv_ref, dfr_ref):
        hp = pl.program_id(0)
        i = pl.program_id(1)
        qts = [qt_ref[_hs(h), :] for h in range(2)]
        dots = [dot_ref[_hs(h), :] for h in range(2)]

        @pl.when(i == 0)
        def _():
            dk_ref[...] = jnp.zeros_like(dk_ref)
            dv_ref[...] = jnp.zeros_like(dv_ref)

        @pl.when((i == 0) & (hp == 0))
        def _():
            dfr_ref[...] = jnp.zeros_like(dfr_ref)

        groups = i // KSUB + 1
        t_idx = i * QB + _iota2((QB, KB), 0)
        lane = _iota2((QB, LANES), 1)
        sub = _iota2((8, KB), 0)
        qs = [q_ref[:, _hl(h)] for h in range(2)]
        dos = [do_ref[:, _hl(h)] for h in range(2)]
        f_qs = [jnp.sum(jnp.where(lane == hp * 2 + h, fc_ref[...], 0.0), axis=1, keepdims=True)
                for h in range(2)]
        lse_qs = [lse_ref[:, h * HEAD_DIM:h * HEAD_DIM + 1] for h in range(2)]
        deltas = [jnp.sum(dos[h].astype(F32) * o_ref[:, _hl(h)].astype(F32), axis=1, keepdims=True)
                  for h in range(2)]

        def step(g, dqs, masked):
            causal = (g * KB + _iota2((QB, KB), 1)) <= t_idx
            fr = fr_ref[g]
            out = []
            dfr = jnp.zeros((8, KB), F32)
            for h in range(2):
                k = k_ref[_krows(g), _hl(h)]
                v = v_ref[_krows(g), _hl(h)]
                f_k = jnp.sum(jnp.where(sub == hp * 2 + h, fr, 0.0), axis=0, keepdims=True)
                z = lax.dot_general(qs[h], k, NT, preferred_element_type=F32) + f_qs[h] - f_k
                p = jnp.exp(z - lse_qs[h])
                if masked:
                    p = jnp.where(causal, p, 0.0)
                dp = lax.dot_general(dos[h], v, NT, preferred_element_type=F32)
                ds = p * (dp - deltas[h])
                dsb = ds.astype(BF16)
                out.append(dqs[h] + jnp.dot(dsb, k, preferred_element_type=F32))
                dk_ref[g, _hs(h), :] += jnp.dot(qts[h], dsb, preferred_element_type=F32)
                dv_ref[g, _hs(h), :] += jnp.dot(dots[h], p.astype(BF16), preferred_element_type=F32)
                colsum = jnp.sum(ds, axis=0, keepdims=True)
                dfr = dfr + jnp.where(sub == hp * 2 + h, -colsum, 0.0)
            dfr_ref[g] += dfr
            return tuple(out)

        res = lax.fori_loop(0, groups - 1, lambda g, c: step(g, c, False),
                            (jnp.zeros((QB, HEAD_DIM), F32),) * 2)
        res = step(groups - 1, res, True)
        for h in range(2):
            dq_ref[:, _hl(h)] = (res[h] * SCALE).astype(dq_ref.dtype)

    q_spec, k_spec, v_spec = _qkv_specs(t, 18, 2)
    blk = pl.BlockSpec((QB, LANES), lambda hp, i: (i, hp))
    frs = pl.BlockSpec((t // KB, 8, KB), lambda hp, i: (0, 0, 0))
    acc_t = pl.BlockSpec((None, t // KB, LANES, KB), lambda hp, i: (hp, 0, 0, 0))
    acc_sds = _sds((2, t // KB, LANES, KB), F32)
    return pl.pallas_call(
        body, name=name, grid=(2, nq),
        in_specs=[q_spec, k_spec, v_spec, pl.BlockSpec((QB, LANES), lambda hp, i: (i, 0)), frs,
                  blk, blk, blk, pl.BlockSpec((LANES, QB), lambda hp, i: (18 + hp, i)),
                  pl.BlockSpec((LANES, QB), lambda hp, i: (hp, i))],
        out_specs=[blk, acc_t, acc_t, frs],
        out_shape=[_sds((t, W_FOX), BF16), acc_sds, acc_sds, _sds((t // KB, 8, KB), F32)],
        compiler_params=_cparams(2))(qkv, qkv, qkv, fcol, frow, o, lse, do, qkv_t, do_t)


def _frow_to_groups(frow):
    n = frow.shape[0] // KSUB
    return frow.reshape(n, KSUB, 8, QB).transpose(0, 2, 1, 3).reshape(n, 8, KB)


def _frow_from_groups(frow):
    n = frow.shape[0]
    return frow.reshape(n, 8, KSUB, QB).transpose(0, 2, 1, 3).reshape(n * KSUB, 8, QB)


def _chunk_band():
    qi = _iota2((QB, CH_KEYS), 0)
    kj = _iota2((QB, CH_KEYS), 1)
    dchunk = (qi >> 6) + LEFT_CHUNKS - (kj >> 6)
    return jnp.where((dchunk >= 0) & (dchunk <= LEFT_CHUNKS), 0.0, NEG)


def _chunk_pad_row(i):
    kj = _iota2((1, CH_KEYS), 1)
    return jnp.where((i - (CH_WIN - 1)) * QB + kj >= 0, 0.0, NEG)


CH_PAD = (CH_WIN - 1) * QB
CH_STEP_HEADS = 4
CH_COLS = CH_STEP_HEADS * HEAD_DIM


def _window(i):
    return pl.ds(pl.multiple_of(i * QB, QB), CH_KEYS)


def _chunk_weights(q, kw, bias, pad_row):
    z = lax.dot_general(q, kw, NT, preferred_element_type=F32) + bias + pad_row
    e = jnp.exp(z - jnp.max(z, axis=1, keepdims=True))
    return e, 1.0 / jnp.sum(e, axis=1, keepdims=True)


def _chunk_specs(t):
    q_spec = pl.BlockSpec((QB, CH_COLS), lambda hp, i: (i, 3 * W_SB // CH_COLS + hp))
    kv_spec = pl.BlockSpec((t + CH_PAD, CH_COLS), lambda hp, i: (0, hp))
    return q_spec, kv_spec


def chunk_fwd(qkv, kp, vp, bias, name):
    t = qkv.shape[0]
    nq = t // QB

    def body(q_ref, k_ref, v_ref, b_ref, o_ref):
        i = pl.program_id(1)
        pad_row = _chunk_pad_row(i)
        for h in range(CH_STEP_HEADS):
            e, inv = _chunk_weights(q_ref[:, _hl(h)], k_ref[_window(i), _hl(h)], b_ref[h], pad_row)
            o = jnp.dot(e.astype(BF16), v_ref[_window(i), _hl(h)], preferred_element_type=F32)
            o_ref[:, _hl(h)] = (o * inv).astype(o_ref.dtype)

    q_spec, kv_spec = _chunk_specs(t)
    return pl.pallas_call(
        body, name=name, grid=(W_CH // CH_COLS, nq),
        in_specs=[q_spec, kv_spec, kv_spec,
                  pl.BlockSpec((CH_STEP_HEADS, QB, CH_KEYS), lambda hp, i: (hp, 0, 0))],
        out_specs=pl.BlockSpec((QB, CH_COLS), lambda hp, i: (i, hp)),
        out_shape=_sds((t, W_CH), BF16), compiler_params=_cparams(2))(qkv, kp, vp, bias)


def chunk_bwd(qkv, qkv_t, kp, vp, bias, do, do_t, name):
    t = qkv.shape[0]
    nq = t // QB

    def body(q_ref, k_ref, v_ref, b_ref, do_ref, qt_ref, dot_ref, dq_ref, dk_ref, dv_ref, db_ref):
        i = pl.program_id(1)

        @pl.when(i == 0)
        def _():
            dk_ref[...] = jnp.zeros_like(dk_ref)
            dv_ref[...] = jnp.zeros_like(dv_ref)
            db_ref[...] = jnp.zeros_like(db_ref)

        pad_row = _chunk_pad_row(i)
        for h in range(CH_STEP_HEADS):
            q = q_ref[:, _hl(h)]
            dov = do_ref[:, _hl(h)]
            kw = k_ref[_window(i), _hl(h)]
            e, inv = _chunk_weights(q, kw, b_ref[h], pad_row)
            p = e * inv
            dp = lax.dot_general(dov, v_ref[_window(i), _hl(h)], NT, preferred_element_type=F32)
            ds = p * (dp - jnp.sum(p * dp, axis=1, keepdims=True))
            db_ref[h] += ds
            dsb = ds.astype(BF16)
            dq_ref[:, _hl(h)] = (jnp.dot(dsb, kw, preferred_element_type=F32) * SCALE).astype(dq_ref.dtype)
            dkt = jnp.dot(qt_ref[_hs(h), :], dsb, preferred_element_type=F32)
            dvt = jnp.dot(dot_ref[_hs(h), :], p.astype(BF16), preferred_element_type=F32)
            for b in range(CH_WIN):
                dk_ref[i + b, _hs(h), :] += dkt[:, b * QB:(b + 1) * QB]
                dv_ref[i + b, _hs(h), :] += dvt[:, b * QB:(b + 1) * QB]

    q_spec, kv_spec = _chunk_specs(t)
    blk = pl.BlockSpec((QB, CH_COLS), lambda hp, i: (i, hp))
    bspec = pl.BlockSpec((CH_STEP_HEADS, QB, CH_KEYS), lambda hp, i: (hp, 0, 0))
    nblk = nq + CH_WIN - 1
    acc_t = pl.BlockSpec((None, nblk, CH_COLS, QB), lambda hp, i: (hp, 0, 0, 0))
    acc_sds = _sds((W_CH // CH_COLS, nblk, CH_COLS, QB), F32)
    return pl.pallas_call(
        body, name=name, grid=(W_CH // CH_COLS, nq),
        in_specs=[q_spec, kv_spec, kv_spec, bspec, blk,
                  pl.BlockSpec((CH_COLS, QB), lambda hp, i: (3 * W_SB // CH_COLS + hp, i)),
                  pl.BlockSpec((CH_COLS, QB), lambda hp, i: (hp, i))],
        out_specs=[blk, acc_t, acc_t, bspec],
        out_shape=[_sds((t, W_CH), BF16), acc_sds, acc_sds, _sds((N_HEADS_CH, QB, CH_KEYS), F32)],
        compiler_params=_cparams(2))(qkv, kp, vp, bias, do, qkv_t, do_t)


def _sum_parts(p_ref):
    total = p_ref[0].astype(F32)
    for p in range(1, p_ref.shape[0]):
        total = total + p_ref[p].astype(F32)
    return total


def sum_parts(parts, grid, p_spec, o_spec, out_sds, name):
    def body(p_ref, o_ref):
        o_ref[...] = _sum_parts(p_ref)

    return pl.pallas_call(body, name=name, grid=grid, in_specs=[p_spec], out_specs=o_spec,
                          out_shape=out_sds, compiler_params=_cparams(len(grid)))(parts)


def adamw(parts, w, m, v, grid, p_specs, w_spec, name):
    c1 = 1.0 / (1.0 - ADAM_B1 ** ADAM_STEP)
    c2 = 1.0 / (1.0 - ADAM_B2 ** ADAM_STEP)
    n = len(parts)

    def body(*refs):
        w_ref, m_ref, v_ref, g_out, d_out, m_out, v_out = refs[n:]
        g = _sum_parts(refs[0])
        for q in range(1, n):
            g = jnp.where(pl.program_id(0) == q, _sum_parts(refs[q]), g)
        m_new = ADAM_B1 * m_ref[...] + (1.0 - ADAM_B1) * g
        v_new = ADAM_B2 * v_ref[...] + (1.0 - ADAM_B2) * (g * g)
        m_hat = m_new * c1
        v_hat = v_new * c2
        g_out[...] = g
        d_out[...] = -ADAM_LR * (m_hat / (jnp.sqrt(v_hat) + ADAM_EPS) + ADAM_WD * w_ref[...])
        m_out[...] = m_new
        v_out[...] = v_new

    out = _sds(w.shape, F32)
    return pl.pallas_call(
        body, name=name, grid=grid, in_specs=[*p_specs, w_spec, w_spec, w_spec],
        out_specs=[w_spec] * 4, out_shape=[out] * 4,
        compiler_params=_cparams(len(grid)))(*parts, w, m, v)


def _ffn_fwd(x, gain, wa, wb_after, s, tm, tag, on_event, deps=()):
    t = x.shape[0]
    hn = rmsnorm_fwd(x, gain, tm, f"rms_{tag}", deps)
    gu, act = ffn_in_swiglu(hn, wa, s, min(2 * tm, t), f"ffn_in_{tag}")
    on_event("act", act)
    wb = wb_after(act)
    y = ffn_out_residual(act, wb, x, s, min(2 * tm, t), f"ffn_out_{tag}")
    return y, (hn, gu, act), wb


def _ffn_bwd(dy, x, gain, saved, wa, wb, s, tm, tag, on_grads):
    t = x.shape[0]
    hn, gu, act = saved
    dgu = ffn_dact_swiglu(dy, wb, gu, s, min(2 * tm, t), f"ffn_dact_{tag}")
    dwb = matmul(TN, act, dy, _sds((4, FF_BLK, D_MODEL), BF16), (4, 1, 1),
                 pl.BlockSpec((None, t, FF_BLK), lambda i, j, k: (i, 0, 0)),
                 pl.BlockSpec((t, D_MODEL), lambda i, j, k: (0, 0)),
                 pl.BlockSpec((None, FF_BLK, D_MODEL), lambda i, j, k: (i, 0, 0)),
                 None, name=f"ffn_dwout_{tag}", alpha=0.5)
    dwa = matmul(TN, dgu, hn, _sds((8, FF_BLK, D_MODEL), BF16), (1, 8, 1),
                 pl.BlockSpec((None, None, t, FF_BLK), lambda i, j, k: (j % 4, j // 4, 0, 0)),
                 pl.BlockSpec((t, D_MODEL), lambda i, j, k: (0, 0)),
                 pl.BlockSpec((None, FF_BLK, D_MODEL), lambda i, j, k: (j, 0, 0)),
                 None, name=f"ffn_dwin_{tag}")
    deps = on_grads(dwa, dwb)
    return ffn_dh_norm_bwd(dgu, wa, s, x, gain, dy, tm, f"ffn_dh_{tag}", deps)


BR_ROWS = ((0, 1), (1, 2), (3, 1))

_Q_COLUMN_SCALE = np.ones((1, QKV_WIDTH), np.float32)
for _lo, _width in ((0, W_SB), (3 * W_SB, W_CH), (3 * (W_SB + W_CH), W_FOX)):
    _Q_COLUMN_SCALE[0, _lo:_lo + _width] = SCALE


def _mixer_fwd(x, gain, wqkv, wf, wgate, late_after, bq, bf, bg, bias, layer, tm, tag, on_event):
    t = x.shape[0]
    nt = t // tm
    hm = rmsnorm_fwd(x, gain, tm, f"rms_{tag}")
    a_full = pl.BlockSpec((tm, D_MODEL), lambda i, j, k: (i, 0))
    wide_out = pl.BlockSpec((tm, D_MODEL), lambda i, j, k: (i, j))
    wide_b = pl.BlockSpec((1, D_MODEL), lambda i, j, k: (0, j))
    qkv, qkv_t = matmul(NN, hm, wqkv, _sds((t, QKV_WIDTH), BF16), (nt, 3, 1), a_full,
                        pl.BlockSpec((None, D_MODEL, D_MODEL), lambda i, j, k: (layer, 0, j)), wide_out, None,
                        name=f"proj_qkv_{tag}", bias=bq, bias_spec=wide_b,
                        scale=jnp.asarray(_Q_COLUMN_SCALE), scale_spec=wide_b,
                        out_t_sds=_sds((QKV_WIDTH, t), BF16),
                        out_t_spec=pl.BlockSpec((D_MODEL, tm), lambda i, j, k: (j, i)))
    gates = matmul(NN, hm, wgate, _sds((t, 3 * D_MODEL), F32), (nt, 3, 1), a_full,
                   pl.BlockSpec((None, D_MODEL, D_MODEL), lambda i, j, k: (layer + 1, 0,j)), wide_out,
                   None, name=f"proj_gate_{tag}", bias=bg, bias_spec=wide_b)
    f = matmul(NN, hm, wf, _sds((t, LANES), F32), (nt, 1, 1), a_full,
               pl.BlockSpec((None, D_MODEL, LANES), lambda i, j, k: (layer, 0, 0)),
               pl.BlockSpec((tm, LANES), lambda i, j, k: (i, 0)), None,
               name=f"proj_f_{tag}", bias=bf, bias_spec=pl.BlockSpec((1, LANES), lambda i, j, k: (0, 0)))
    fcol, frow = forget_cumsum(f, f"fcum_{tag}")
    frow = _frow_to_groups(frow)
    on_event("qkv", qkv)
    o_sb, w_sb = sb_fwd(qkv, f"sb_fwd_{tag}")
    on_event("o_sb", o_sb)
    kp = jnp.pad(qkv[:, 10 * LANES:14 * LANES], ((CH_PAD, 0), (0, 0)))
    vp = jnp.pad(qkv[:, 14 * LANES:18 * LANES], ((CH_PAD, 0), (0, 0)))
    o_ch = chunk_fwd(qkv, kp, vp, bias, f"chunk_fwd_{tag}")
    o_fox, lse = fox_fwd(qkv, fcol, frow, f"fox_fwd_{tag}")
    wbr, wout = late_after(o_fox)
    ys = []
    for a, (o, (r0, nr)) in enumerate(zip((o_sb, o_ch, o_fox), BR_ROWS)):
        ys.append(matmul(
            NN, o, wbr, _sds((t, D_MODEL), F32), (nt, 1, nr),
            pl.BlockSpec((tm, 256), lambda i, j, k: (i, k)),
            pl.BlockSpec((None, 256, D_MODEL), functools.partial(lambda i, j, k, r0: (layer, r0 + k, 0), r0=r0)),
            a_full, (tm, D_MODEL), name=f"branch{a}_{tag}"))
    merged = merge_fwd(gates, ys[0], ys[1], ys[2], tm, f"merge_{tag}")
    x_new = matmul(NN, merged, wout, _sds((t, D_MODEL), F32), (nt, 1, 1), a_full,
                   pl.BlockSpec((None, D_MODEL, D_MODEL), lambda i, j, k: (layer, 0, 0)), a_full, None,
                   name=f"wout_{tag}", res=x, res_spec=a_full)
    saved = (hm, qkv, gates, f, fcol, frow, o_sb, o_ch, o_fox, lse, ys, merged, kp, vp, w_sb, qkv_t)
    return x_new, saved, wbr, wout


def _mixer_bwd(dy, x, gain, saved, wqkv, wf, wgate, wbr, wout, bias, layer, tm, tag, on_grads):
    t = x.shape[0]
    nt = t // tm
    hm, qkv, gates, f, fcol, frow, o_sb, o_ch, o_fox, lse, ys, merged, kp, vp, w_sb, qkv_t = saved
    a_full = pl.BlockSpec((tm, D_MODEL), lambda i, j, k: (i, 0))
    red_row = pl.BlockSpec((tm, D_MODEL), lambda i, j, k: (k, 0))
    sq = pl.BlockSpec((D_MODEL, D_MODEL), lambda i, j, k: (0, 0))
    dmerged = matmul(NT, dy, wout, _sds((t, D_MODEL), F32), (nt, 1, 1), a_full,
                     pl.BlockSpec((None, D_MODEL, D_MODEL), lambda i, j, k: (layer, 0, 0)), a_full, None,
                     name=f"dmerged_{tag}")
    all_t = pl.BlockSpec((t, D_MODEL), lambda i, j, k: (0, 0))
    dwout = matmul(TN, merged, dy, _sds((D_MODEL, D_MODEL), BF16), (1, 1, 1), all_t, all_t, sq,
                   None, name=f"dwout_{tag}")
    dgates, dys = merge_bwd(dmerged, gates, ys[0], ys[1], ys[2], tm // 2, f"merge_bwd_{tag}")
    dos, dos_t, dwbrs = [], [], []
    for a, (o, (r0, nr)) in enumerate(zip((o_sb, o_ch, o_fox), BR_ROWS)):
        do, do_t = matmul(
            NT, dys[a], wbr, _sds((t, nr * 256), BF16), (nt, nr, 1), a_full,
            pl.BlockSpec((None, 256, D_MODEL), functools.partial(lambda i, j, k, r0: (layer, r0 + j, 0), r0=r0)),
            pl.BlockSpec((tm, 256), lambda i, j, k: (i, j)), None, name=f"dbranch{a}_{tag}",
            out_t_sds=_sds((nr * 256, t), BF16), out_t_spec=pl.BlockSpec((256, tm), lambda i, j, k: (j, i)))
        dos.append(do)
        dos_t.append(do_t)
        dwbrs.append(matmul(
            TN, o, dys[a], _sds((nr * 256, D_MODEL), BF16), (nr, 1, 1),
            pl.BlockSpec((t, 256), lambda i, j, k: (0, i)), all_t,
            pl.BlockSpec((256, D_MODEL), lambda i, j, k: (i, 0)), None, name=f"dwbr{a}_{tag}"))
    dq_a, dk_a, dv_a = sb_bwd(qkv, qkv_t, w_sb, dos[0], dos_t[0], f"sb_bwd_{tag}")
    dk_a, dv_a = _keys_major(dk_a), _keys_major(dv_a)
    dq_b, dk_b, dv_b, dbias = chunk_bwd(qkv, qkv_t, kp, vp, bias, dos[1], dos_t[1], f"chunk_bwd_{tag}")
    dk_b, dv_b = [x[:, CH_WIN - 1:].transpose(1, 3, 0, 2).reshape(t, W_CH) for x in (dk_b, dv_b)]
    dq_c, dk_c, dv_c, dfrow = fox_bwd(qkv, qkv_t, fcol, frow, o_fox, lse, dos[2], dos_t[2], f"fox_bwd_{tag}")
    dk_c, dv_c = _keys_major(dk_c), _keys_major(dv_c)
    df = forget_cumsum_bwd(_frow_from_groups(dfrow), f, f"fcum_bwd_{tag}")
    dqkv = jnp.concatenate([p.astype(BF16) for p in
                            (dq_a, dk_a, dv_a, dq_b, dk_b, dv_b, dq_c, dk_c, dv_c)], axis=1)
    dtab = rel_bias_scatter(dbias, f"rel_scatter_{tag}")

    all_rows = pl.BlockSpec((t, D_MODEL), lambda i, j, k: (0, 0))
    wide_b = pl.BlockSpec((t, D_MODEL), lambda i, j, k: (0, j))
    wide_o = pl.BlockSpec((D_MODEL, D_MODEL), lambda i, j, k: (0, j))
    wide_cs = pl.BlockSpec((1, D_MODEL), lambda i, j, k: (0, j))
    dwqkv, dbq = matmul(TN, hm, dqkv, _sds((D_MODEL, QKV_WIDTH), BF16), (1, 3, 1), all_rows, wide_b,
                        wide_o, None, name=f"dwqkv_{tag}",
                        colsum_sds=_sds((1, QKV_WIDTH), F32), colsum_spec=wide_cs)
    dwgate, dbg = matmul(TN, hm, dgates, _sds((D_MODEL, 3 * D_MODEL), BF16), (1, 3, 1), all_rows,
                         wide_b, wide_o, None, name=f"dwgate_{tag}",
                         colsum_sds=_sds((1, 3 * D_MODEL), F32), colsum_spec=wide_cs)
    dwf, dbf = matmul(TN, hm, df, _sds((D_MODEL, LANES), BF16), (1, 1, 1), all_rows,
                      pl.BlockSpec((t, LANES), lambda i, j, k: (0, 0)),
                      pl.BlockSpec((D_MODEL, LANES), lambda i, j, k: (0, 0)), None,
                      name=f"dwf_{tag}", colsum_sds=_sds((1, LANES), F32),
                      colsum_spec=pl.BlockSpec((1, LANES), lambda i, j, k: (0, 0)))
    dwbr = jnp.concatenate(dwbrs, axis=0)
    deps = on_grads(dict(dwqkv=dwqkv, dwgate=dwgate, dwf=dwf, dwbr=dwbr, dwout=dwout))
    wide_a = pl.BlockSpec((tm, QKV_WIDTH), lambda i, j, k: (i, 0))
    dhm = matmul(NT, dqkv, wqkv, _sds((t, D_MODEL), F32), (nt, 1, 1), wide_a,
                 pl.BlockSpec((None, D_MODEL, QKV_WIDTH), lambda i, j, k: (layer, 0, 0)), a_full,
                 None, name=f"dhm_qkv_{tag}", deps=deps)
    dhm = matmul(NT, dgates, wgate, _sds((t, D_MODEL), F32), (nt, 1, 1), wide_a,
                 pl.BlockSpec((None, D_MODEL, QKV_WIDTH), lambda i, j, k: (layer + 1, 0, 0)), a_full,
                 None, name=f"dhm_gate_{tag}", res=dhm, res_spec=a_full)
    dhm = matmul(NT, df, wf, _sds((t, D_MODEL), F32), (nt, 1, 1),
                 pl.BlockSpec((tm, LANES), lambda i, j, k: (i, 0)),
                 pl.BlockSpec((None, D_MODEL, LANES), lambda i, j, k: (layer, 0, 0)), a_full, None,
                 name=f"dhm_f_{tag}", res=dhm, res_spec=a_full)
    dx, dgain = rmsnorm_bwd(x, gain, dhm, dy, tm, f"rms_bwd_{tag}")
    return dx, dict(dbq=dbq, dbg=dbg, dbf=dbf, dtab=dtab, dgain=dgain)


def _pack_small(pieces):
    flat = jnp.concatenate([p.reshape(-1).astype(F32) for p in pieces])
    flat = jnp.pad(flat, (0, SMALL_ROWS * LANES - flat.shape[0]))
    return flat.reshape(SMALL_ROWS, LANES)


def _unpack_small(packed, shapes):
    flat = packed.reshape(-1)
    out, pos = [], 0
    for shp in shapes:
        n = int(np.prod(shp))
        out.append(flat[pos:pos + n].reshape(shp))
        pos += n
    return out


def kernel(x, g_ffn1, w_ffn1_in, w_ffn1_out, g_mix, w_in, b_in, rel_bias, w_br_sb, w_br_ch, w_br_fox, w_out, g_ffn2, w_ffn2_in, w_ffn2_out, g_final, loss_target, m_g_ffn1, m_w_ffn1_in, m_w_ffn1_out, m_g_mix, m_w_in, m_b_in, m_rel_bias, m_w_br_sb, m_w_br_ch, m_w_br_fox, m_w_out, m_g_ffn2, m_w_ffn2_in, m_w_ffn2_out, m_g_final, v_g_ffn1, v_w_ffn1_in, v_w_ffn1_out, v_g_mix, v_w_in, v_b_in, v_rel_bias, v_w_br_sb, v_w_br_ch, v_w_br_fox, v_w_out, v_g_ffn2, v_w_ffn2_in, v_w_ffn2_out, v_g_final):
    t = x.shape[1]
    tm = min(512, t)
    xs = x[0]
    target = loss_target[0]
    f_lo, f_hi = QKV_WIDTH, QKV_WIDTH + N_HEADS_FOX

    def ffn_shards(w_in_, w_out_, l):
        return [w_in_[l:l + 1].astype(BF16), w_out_[l:l + 1].astype(BF16)]

    def mixer_shards(l):
        wl = w_in[l]
        return [jnp.stack([wl[:, :QKV_WIDTH], wl[:, f_hi:]]).astype(BF16),
                jnp.pad(wl[:, f_lo:f_hi], ((0, 0), (0, LANES - N_HEADS_FOX)))[None].astype(BF16),
                w_out[l:l + 1].astype(BF16),
                jnp.concatenate([w_br_sb[l], w_br_ch[l], w_br_fox[l]], axis=0)[None].astype(BF16)]

    gathers = {}
    gather_tokens = []

    def start_gather(shards, name):
        handle = gather_start(shards, name, deps=gather_tokens[-1:])
        gather_tokens.append(handle["token"])
        return handle

    def relay(handle, after):
        if "send2" not in handle:
            gather_relay(handle, after)

    relay_on = {("mix", 0, "qkv"): ("mix", 0, 1), ("mix", 0, "o_sb"): ("ffn2", 0, 0),
                ("ffn2", 0, "act"): ("ffn1", 1, 0), ("ffn1", 1, "act"): ("mix", 1, 0),
                ("mix", 1, "qkv"): ("ffn2", 1, 0)}

    def on_event(grp, l):
        def fire(event, array):
            target = relay_on.get((grp, l, event))
            if target is not None:
                relay(gathers[target[:2]][target[2]], array)
        return fire

    for l in range(DEPTH):
        for grp, shards in (("ffn1", ffn_shards(w_ffn1_in, w_ffn1_out, l)), ("mix", mixer_shards(l)),
                            ("ffn2", ffn_shards(w_ffn2_in, w_ffn2_out, l))):
            cut = len(shards) // 2
            if l == 0 and grp != "ffn2":
                gathers[(grp, l)] = (start_gather(shards[:cut], f"gather_{grp}_l{l}_a"),
                                     start_gather(shards[cut:], f"gather_{grp}_l{l}_b"))
            else:
                gathers[(grp, l)] = (start_gather(shards, f"gather_{grp}_l{l}"),)

    def gathered(key, after):
        hs = gathers[key]
        cut = hs[0]["n"]
        relay(hs[0], after)
        first = gather_finish(hs[0], after)
        if len(hs) == 1:
            return first[:cut // 2], lambda later: first[cut // 2:]

        def second(later):
            relay(hs[1], later)
            return gather_finish(hs[1], later)

        return first, second

    def ffn_weights(key, after):
        (wa_,), rest = gathered(key, after)
        return wa_, lambda later: rest(later)[0].reshape(1, 4, FF_BLK, D_MODEL)

    def mixer_weights(key, after):
        (wc_, wf_), rest = gathered(key, after)

        def late(later):
            wout_, wbr_ = rest(later)
            return (wbr_.transpose(0, 2, 1, 3).reshape(1, D_MODEL, D_MODEL), wout_.reshape(1, D_MODEL, D_MODEL))

        return wc_.reshape(2, D_MODEL, QKV_WIDTH), wf_.reshape(1, D_MODEL, LANES), late

    bq = b_in[:, None, :QKV_WIDTH]
    bf = jnp.pad(b_in[:, f_lo:f_hi], ((0, 0), (0, LANES - N_HEADS_FOX)))[:, None, :]
    bg = b_in[:, None, f_hi:]
    tab_t = jnp.pad(rel_bias.transpose(0, 2, 1), ((0, 0), (0, 0), (0, REL_PAD - N_REL)))

    h = xs
    saved = []
    weights = []
    for l in range(DEPTH):
        bias = rel_bias_build(tab_t[l], f"rel_build_l{l}").reshape(N_HEADS_CH, QB, CH_KEYS)
        x0 = h
        wa1, wb1_after = ffn_weights(("ffn1", l), x0)
        x1, s1, wb1 = _ffn_fwd(x0, g_ffn1[l:l + 1], wa1, wb1_after, 0, tm, f"ffn1_l{l}", on_event("ffn1", l),
                               deps=gather_tokens if l == 0 else ())
        wc, wf, late_after = mixer_weights(("mix", l), x1)
        x2, sm, wbr, wout = _mixer_fwd(x1, g_mix[l:l + 1], wc, wf, wc, late_after, bq[l], bf[l], bg[l],
                                       bias, 0, tm, f"mix_l{l}", on_event("mix", l))
        wa2, wb2_after = ffn_weights(("ffn2", l), x2)
        x3, s2, wb2 = _ffn_fwd(x2, g_ffn2[l:l + 1], wa2, wb2_after, 0, tm, f"ffn2_l{l}", on_event("ffn2", l))
        saved.append((x0, x1, x2, s1, sm, s2, bias))
        weights.append(((wa1, wb1), (wc, wf, wout, wbr), (wa2, wb2)))
        h = x3

    dx, dg_final, loss_blk = loss_head(h, g_final[None, :], target, tm, "loss_head")

    g_mix_l = [None] * DEPTH
    dgains = {}
    scatters = {}

    def scatter_ffn(key):
        def on_grads(dwa, dwb):
            scatters[key] = exchange_start(
                "scatter", [dwa[None], dwb.reshape(1, N_DEV, D_FF // N_DEV, D_MODEL)],
                f"scatter_{key[0]}_l{key[1]}")
            return (scatters[key]["token"],)
        return on_grads

    def scatter_mixer(key):
        def on_grads(gm):
            scatters[key] = exchange_start(
                "scatter",
                [gm["dwqkv"].reshape(1, N_DEV, LANES, QKV_WIDTH), gm["dwgate"].reshape(1, N_DEV, LANES, QKV_WIDTH),
                 gm["dwf"].reshape(1, N_DEV, LANES, LANES), gm["dwout"].reshape(1, N_DEV, LANES, D_MODEL),
                 gm["dwbr"].reshape(1, D_MODEL, N_DEV, LANES).transpose(0, 2, 1, 3)],
                f"scatter_{key[0]}_l{key[1]}")
            return (scatters[key]["token"],)
        return on_grads

    for l in reversed(range(DEPTH)):
        x0, x1, x2, s1, sm, s2, bias = saved[l]
        w1, (wc, wf, wout, wbr), w2 = weights[l]
        dx, dgains[("ffn2", l)] = _ffn_bwd(dx, x2, g_ffn2[l:l + 1], s2, *w2, 0, tm, f"ffn2_l{l}",
                                           scatter_ffn(("ffn2", l)))
        dx, g_mix_l[l] = _mixer_bwd(dx, x1, g_mix[l:l + 1], sm, wc, wf, wc, wbr, wout, bias, 0, tm,
                                    f"mix_l{l}", scatter_mixer(("mix", l)))
        dx, dgains[("ffn1", l)] = _ffn_bwd(dx, x0, g_ffn1[l:l + 1], s1, *w1, 0, tm, f"ffn1_l{l}",
                                           scatter_ffn(("ffn1", l)))

    small_shapes = []
    small_pieces = []
    small_w, small_m, small_v = [], [], []

    def add_small(piece, w, m, v):
        small_shapes.append(w.shape)
        small_pieces.append(piece)
        small_w.append(w); small_m.append(m); small_v.append(v)

    dg1 = jnp.concatenate([dgains[("ffn1", l)] for l in range(DEPTH)], axis=0)
    dgm = jnp.concatenate([g_mix_l[l]["dgain"] for l in range(DEPTH)], axis=0)
    dg2 = jnp.concatenate([dgains[("ffn2", l)] for l in range(DEPTH)], axis=0)
    db = jnp.stack([jnp.concatenate([g_mix_l[l]["dbq"][0], g_mix_l[l]["dbf"][0, :N_HEADS_FOX],
                                     g_mix_l[l]["dbg"][0]]) for l in range(DEPTH)])
    drel = jnp.stack([g_mix_l[l]["dtab"][:, :N_REL].T for l in range(DEPTH)])
    add_small(dg1, g_ffn1, m_g_ffn1, v_g_ffn1)
    add_small(dgm, g_mix, m_g_mix, v_g_mix)
    add_small(db, b_in, m_b_in, v_b_in)
    add_small(drel, rel_bias, m_rel_bias, v_rel_bias)
    add_small(dg2, g_ffn2, m_g_ffn2, v_g_ffn2)
    add_small(dg_final[0], g_final, m_g_final, v_g_final)
    loss_piece = loss_blk[0, 0:1]
    small_packed = _pack_small(small_pieces + [loss_piece])

    recv = {}
    last = ("ffn1", 0)
    for l in reversed(range(DEPTH)):
        for grp in ("ffn2", "mix", "ffn1"):
            if (grp, l) != last:
                recv[(grp, l)] = exchange_wait(scatters[(grp, l)], dx, f"scattered_{grp}_l{l}")

    def upd(parts, w, m, v, tr, name, rb0=0):
        _, r, c = w.shape
        nr = r // tr

        def p_spec(layer):
            pinned = (nr - 1) if layer == 0 else 0
            return pl.BlockSpec((N_DEV, None, tr, c),
                                lambda l, i: (0, 0, rb0 + jnp.where(l == layer, i, pinned), 0))

        return adamw(parts, w, m, v, (DEPTH, nr), [p_spec(0), p_spec(1)],
                     pl.BlockSpec((None, tr, c), lambda l, i: (l, i, 0)), name)

    def both(grp, k):
        return [recv[(grp, l)][k] for l in range(DEPTH)]

    out_rows = D_FF // N_DEV // 2
    def upd_transposed(parts, w, m, v, tr, name):
        tp = lambda a: jnp.transpose(a, (0, 2, 1))
        return [tp(o) for o in upd(parts, tp(w), tp(m), tp(v), tr, name)]

    in_rows = FF_BLK // 4
    r_ffn2_in = upd_transposed(both("ffn2", 0), w_ffn2_in, m_w_ffn2_in, v_w_ffn2_in, in_rows, "adamw_ffn2_in")
    r_ffn2_out = upd(both("ffn2", 1), w_ffn2_out, m_w_ffn2_out, v_w_ffn2_out, out_rows, "adamw_ffn2_out")
    r_out = upd(both("mix", 3), w_out, m_w_out, v_w_out, LANES, "adamw_w_out")
    r_br_sb = upd(both("mix", 4), w_br_sb, m_w_br_sb, v_w_br_sb, 256, "adamw_br_sb", rb0=0)
    r_br_ch = upd(both("mix", 4), w_br_ch, m_w_br_ch, v_w_br_ch, 256, "adamw_br_ch", rb0=1)
    r_br_fox = upd(both("mix", 4), w_br_fox, m_w_br_fox, v_w_br_fox, 256, "adamw_br_fox", rb0=3)

    def summed(parts, name):
        _, _, r, c = parts.shape
        return sum_parts(parts, (1,), pl.BlockSpec((N_DEV, None, r, c), lambda s: (0, 0, 0, 0)),
                         pl.BlockSpec((r, c), lambda s: (0, 0)), _sds((r, c), F32), name)

    g_w_in = jnp.stack([
        jnp.concatenate([summed(recv[("mix", l)][0], f"sum_wqkv_l{l}"),
                         summed(recv[("mix", l)][2], f"sum_wf_l{l}")[:, :N_HEADS_FOX],
                         summed(recv[("mix", l)][1], f"sum_wgate_l{l}")], axis=1) for l in range(DEPTH)])
    to_cols = lambda a: jnp.transpose(a, (2, 0, 1))
    n_cols = w_in.shape[2]
    col_blk = n_cols // 4
    win_spec = pl.BlockSpec((col_blk, DEPTH, LANES), lambda i: (i, 0, 0))
    r_in = adamw([to_cols(g_w_in)[None]], to_cols(w_in), to_cols(m_w_in), to_cols(v_w_in), (4,),
                 [pl.BlockSpec((1, col_blk, DEPTH, LANES), lambda i: (0, i, 0, 0))], win_spec, "adamw_w_in")
    r_in = [jnp.transpose(o, (1, 2, 0)) for o in r_in]

    recv[last] = exchange_wait(scatters[last], r_in[1], "scattered_ffn1_l0")
    r_ffn1_in = upd_transposed(both("ffn1", 0), w_ffn1_in, m_w_ffn1_in, v_w_ffn1_in, in_rows, "adamw_ffn1_in")
    r_ffn1_out = upd(both("ffn1", 1), w_ffn1_out, m_w_ffn1_out, v_w_ffn1_out, out_rows, "adamw_ffn1_out")

    small_sum = all_reduce_small(small_packed, "allreduce_small", deps=(r_ffn1_out[1],))
    n_small = sum(int(np.prod(s)) for s in small_shapes)
    loss = small_sum.reshape(-1)[n_small]
    sm_spec = pl.BlockSpec((SMALL_ROWS, LANES), lambda i: (0, 0))
    sm_out = adamw([small_sum[None]], _pack_small(small_w), _pack_small(small_m), _pack_small(small_v),
                   (1,), [pl.BlockSpec((1, SMALL_ROWS, LANES), lambda i: (0, 0, 0))], sm_spec, "adamw_small")
    sm_g, sm_d, sm_m, sm_v = [_unpack_small(o, small_shapes) for o in sm_out]

    def per_kind(k):
        small = (sm_g, sm_d, sm_m, sm_v)[k]
        return [small[0], r_ffn1_in[k], r_ffn1_out[k], small[1], r_in[k], small[2], small[3],
                r_br_sb[k], r_br_ch[k], r_br_fox[k], r_out[k], small[4], r_ffn2_in[k], r_ffn2_out[k],
                small[5]]

    return (loss, dx[None], *per_kind(0), *per_kind(1), *per_kind(2), *per_kind(3))
```

```python
import functools

import numpy as np
import jax
import jax.numpy as jnp
from jax import lax
from jax.experimental import pallas as pl
from jax.experimental.pallas import tpu as pltpu

F32 = jnp.float32
BF16 = jnp.bfloat16

N_DEV = 8
D_MODEL = 1024
DEPTH = 2
HEAD_DIM = 64
W_SB, W_CH, W_FOX = 256, 512, 256
QKV_WIDTH = 3 * (W_SB + W_CH + W_FOX)
N_HEADS_FOX = 4
N_HEADS_CH = 8
D_FF = 2816
FF_BLK = 2 * D_FF // N_DEV
CHUNK = 64
LEFT_CHUNKS = 8
MAX_REL = 128
N_REL = 2 * MAX_REL + 1
REL_PAD = 384
QB = 128
KB = 512
KSUB = KB // QB
CH_WIN = 5
CH_KEYS = CH_WIN * QB
RMS_EPS = 1e-6
NEG = -1e30
SCALE = HEAD_DIM ** -0.5
LANES = 128
VMEM_LIMIT = 56 * 1024 * 1024

ADAM_LR, ADAM_B1, ADAM_B2, ADAM_EPS, ADAM_WD, ADAM_STEP = 0.001, 0.9, 0.999, 1e-08, 0.01, 10

SMALL_ROWS = 192

MESH = pl.DeviceIdType.MESH
ANY = pl.BlockSpec(memory_space=pl.ANY)
HIGHEST = lax.Precision.HIGHEST

NN = (((1,), (0,)), ((), ()))
NT = (((1,), (1,)), ((), ()))
TN = (((0,), (0,)), ((), ()))


def _cparams(n_grid):
    return pltpu.CompilerParams(dimension_semantics=("arbitrary",) * n_grid,
                                vmem_limit_bytes=VMEM_LIMIT)


def _sds(shape, dtype):
    return jax.ShapeDtypeStruct(tuple(shape), dtype)


def _my_index():
    return 4 * lax.axis_index("x") + 2 * lax.axis_index("y") + lax.axis_index("c")


def _peer(mask):
    x, y, c = lax.axis_index("x"), lax.axis_index("y"), lax.axis_index("c")
    px = x ^ ((mask >> 2) & 1)
    py = y ^ ((mask >> 1) & 1)
    pc = c ^ (mask & 1)
    return (px, py, pc), 4 * px + 2 * py + pc


def all_gather(shard, name):
    s, r, c = shard.shape

    def body(in_ref, out_ref, send_sems, recv_sems, local_sem):
        me = _my_index()
        mine = pltpu.make_async_copy(in_ref, out_ref.at[:, me], local_sem)
        mine.start()
        sends = []
        for mask in range(1, N_DEV):
            peer, _ = _peer(mask)
            cp = pltpu.make_async_remote_copy(
                src_ref=in_ref, dst_ref=out_ref.at[:, me],
                send_sem=send_sems.at[mask - 1], recv_sem=recv_sems.at[mask - 1],
                device_id=peer, device_id_type=MESH)
            cp.start()
            sends.append(cp)
        for mask in range(1, N_DEV):
            peer, pidx = _peer(mask)
            pltpu.make_async_remote_copy(
                src_ref=in_ref, dst_ref=out_ref.at[:, pidx],
                send_sem=send_sems.at[mask - 1], recv_sem=recv_sems.at[mask - 1],
                device_id=peer, device_id_type=MESH).wait_recv()
        for cp in sends:
            cp.wait_send()
        mine.wait()

    return pl.pallas_call(
        body, name=name,
        out_shape=_sds((s, N_DEV, r, c), shard.dtype),
        in_specs=[ANY], out_specs=ANY,
        scratch_shapes=[pltpu.SemaphoreType.DMA((N_DEV - 1,)),
                        pltpu.SemaphoreType.DMA((N_DEV - 1,)),
                        pltpu.SemaphoreType.DMA],
    )(shard)


def all_to_all(parts, name):
    s, _, r, c = parts.shape

    def body(in_ref, out_ref, send_sems, recv_sems, local_sem):
        me = _my_index()
        mine = pltpu.make_async_copy(in_ref.at[:, me], out_ref.at[me], local_sem)
        mine.start()
        sends = []
        for mask in range(1, N_DEV):
            peer, pidx = _peer(mask)
            cp = pltpu.make_async_remote_copy(
                src_ref=in_ref.at[:, pidx], dst_ref=out_ref.at[me],
                send_sem=send_sems.at[mask - 1], recv_sem=recv_sems.at[mask - 1],
                device_id=peer, device_id_type=MESH)
            cp.start()
            sends.append(cp)
        for mask in range(1, N_DEV):
            peer, pidx = _peer(mask)
            pltpu.make_async_remote_copy(
                src_ref=in_ref.at[:, me], dst_ref=out_ref.at[pidx],
                send_sem=send_sems.at[mask - 1], recv_sem=recv_sems.at[mask - 1],
                device_id=peer, device_id_type=MESH).wait_recv()
        for cp in sends:
            cp.wait_send()
        mine.wait()

    return pl.pallas_call(
        body, name=name,
        out_shape=_sds((N_DEV, s, r, c), parts.dtype),
        in_specs=[ANY], out_specs=ANY,
        scratch_shapes=[pltpu.SemaphoreType.DMA((N_DEV - 1,)),
                        pltpu.SemaphoreType.DMA((N_DEV - 1,)),
                        pltpu.SemaphoreType.DMA],
    )(parts)


HBM_SPEC = pl.BlockSpec(memory_space=pltpu.HBM)
SEM_SPEC = pl.BlockSpec(memory_space=pltpu.SEMAPHORE)
EFFECT = pltpu.SideEffectType.DATAFLOW_SIDE_EFFECTING


def _exchange_refs(mode, in_ref, land_ref, me, pidx):
    if mode == "gather":
        return in_ref, land_ref.at[:, me], land_ref.at[:, pidx]
    return in_ref.at[:, pidx], land_ref.at[me], land_ref.at[pidx]


def _landing_shape(mode, a):
    if mode == "gather":
        s, r, c = a.shape
        return (s, N_DEV, r, c)
    s, _, r, c = a.shape
    return (N_DEV, s, r, c)


def _own_copy(mode, in_ref, land_ref, me, sem):
    if mode == "gather":
        return pltpu.make_async_copy(in_ref, land_ref.at[:, me], sem)
    return pltpu.make_async_copy(in_ref.at[:, me], land_ref.at[me], sem)


def exchange_start(mode, arrays, name, deps=()):
    n = len(arrays)
    lands0 = [lax.empty(_landing_shape(mode, a), a.dtype) for a in arrays]

    def body(*refs):
        in_refs, land_refs = refs[:n], refs[n:2 * n]
        outs_at = 2 * n + len(deps)
        send_sems, recv_sems, own_sems, token = refs[outs_at], refs[outs_at + 1], refs[outs_at + 2], refs[-1]
        mine = _my_index()
        for k in range(n):
            _own_copy(mode, in_refs[k], land_refs[k], mine, own_sems.at[k]).start()
            for mask in range(1, N_DEV):
                peer, pidx = _peer(mask)
                src, dst, _ = _exchange_refs(mode, in_refs[k], land_refs[k], mine, pidx)
                sem = k * (N_DEV - 1) + mask - 1
                pltpu.make_async_remote_copy(
                    src_ref=src, dst_ref=dst, send_sem=send_sems.at[sem], recv_sem=recv_sems.at[sem],
                    device_id=peer, device_id_type=MESH).start()
        token[...] = jnp.zeros_like(token)

    nsem = n * (N_DEV - 1)
    outs = pl.pallas_call(
        body, name=name,
        out_shape=(pltpu.SemaphoreType.DMA((nsem,)), pltpu.SemaphoreType.DMA((nsem,)),
                   pltpu.SemaphoreType.DMA((n,)),
                   *[pltpu.HBM(a.shape, a.dtype) for a in arrays],
                   *[pltpu.HBM(l.shape, l.dtype) for l in lands0], _sds((8, LANES), F32)),
        in_specs=[HBM_SPEC] * (2 * n) + [ANY] * len(deps),
        out_specs=(SEM_SPEC, SEM_SPEC, SEM_SPEC, *[HBM_SPEC] * (2 * n),
                   pl.BlockSpec(memory_space=pltpu.VMEM)),
        input_output_aliases={k: 3 + k for k in range(2 * n)},
        compiler_params=pltpu.CompilerParams(has_side_effects=EFFECT),
    )(*[pltpu.with_memory_space_constraint(a, pltpu.HBM) for a in arrays],
      *[pltpu.with_memory_space_constraint(l, pltpu.HBM) for l in lands0], *deps)
    return dict(mode=mode, n=n, send=outs[0], recv=outs[1], own=outs[2], ins=outs[3:3 + n],
                lands=outs[3 + n:3 + 2 * n], token=outs[-1])


def exchange_wait(handle, after, name):
    n, mode = handle["n"], handle["mode"]

    def body(*refs):
        in_refs, land_refs = refs[:n], refs[n:2 * n]
        send_sems, recv_sems, own_sems = refs[2 * n], refs[2 * n + 1], refs[2 * n + 2]
        mine = _my_index()
        for k in range(n):
            _own_copy(mode, in_refs[k], land_refs[k], mine, own_sems.at[k]).wait()
            for mask in range(1, N_DEV):
                peer, pidx = _peer(mask)
                src, _, here = _exchange_refs(mode, in_refs[k], land_refs[k], mine, pidx)
                sem = k * (N_DEV - 1) + mask - 1
                cp = pltpu.make_async_remote_copy(
                    src_ref=src, dst_ref=here, send_sem=send_sems.at[sem], recv_sem=recv_sems.at[sem],
                    device_id=peer, device_id_type=MESH)
                cp.wait_send()
                cp.wait_recv()

    thru = (*handle["ins"], *handle["lands"])
    outs = pl.pallas_call(
        body, name=name,
        out_shape=tuple(pltpu.HBM(a.shape, a.dtype) for a in thru),
        in_specs=[HBM_SPEC] * (2 * n) + [SEM_SPEC, SEM_SPEC, SEM_SPEC, ANY],
        out_specs=tuple([HBM_SPEC] * (2 * n)),
        input_output_aliases={k: k for k in range(2 * n)},
        compiler_params=pltpu.CompilerParams(has_side_effects=EFFECT),
    )(*thru, handle["send"], handle["recv"], handle["own"], after)
    return list(outs[n:])


FAR_MASKS = (2, 4, 6)
PHASE1_MASKS = (1,) + FAR_MASKS


def gather_start(arrays, name, deps=()):
    n = len(arrays)
    n1 = len(PHASE1_MASKS)
    lands0 = [lax.empty(_landing_shape("gather", a), a.dtype) for a in arrays]

    def body(*refs):
        in_refs, land_refs = refs[:n], refs[n:2 * n]
        outs_at = 2 * n + len(deps)
        send_sems, recv_sems, own_sems, token = refs[outs_at], refs[outs_at + 1], refs[outs_at + 2], refs[-1]
        mine = _my_index()
        for k in range(n):
            _own_copy("gather", in_refs[k], land_refs[k], mine, own_sems.at[k]).start()
            for j, mask in enumerate(PHASE1_MASKS):
                peer, _ = _peer(mask)
                pltpu.make_async_remote_copy(
                    src_ref=in_refs[k], dst_ref=land_refs[k].at[:, mine],
                    send_sem=send_sems.at[k * n1 + j], recv_sem=recv_sems.at[k * n1 + j],
                    device_id=peer, device_id_type=MESH).start()
        token[...] = jnp.zeros_like(token)

    outs = pl.pallas_call(
        body, name=name,
        out_shape=(pltpu.SemaphoreType.DMA((n * n1,)), pltpu.SemaphoreType.DMA((n * n1,)),
                   pltpu.SemaphoreType.DMA((n,)),
                   *[pltpu.HBM(a.shape, a.dtype) for a in arrays],
                   *[pltpu.HBM(l.shape, l.dtype) for l in lands0], _sds((8, LANES), F32)),
        in_specs=[HBM_SPEC] * (2 * n) + [ANY] * len(deps),
        out_specs=(SEM_SPEC, SEM_SPEC, SEM_SPEC, *[HBM_SPEC] * (2 * n),
                   pl.BlockSpec(memory_space=pltpu.VMEM)),
        input_output_aliases={k: 3 + k for k in range(2 * n)},
        compiler_params=pltpu.CompilerParams(has_side_effects=EFFECT),
    )(*[pltpu.with_memory_space_constraint(a, pltpu.HBM) for a in arrays],
      *[pltpu.with_memory_space_constraint(l, pltpu.HBM) for l in lands0], *deps)
    return dict(n=n, send=outs[0], recv=outs[1], own=outs[2], ins=outs[3:3 + n],
                lands=outs[3 + n:3 + 2 * n], token=outs[-1], name=name)


def gather_relay(handle, after):
    n = handle["n"]
    n1, n2 = len(PHASE1_MASKS), len(FAR_MASKS)

    def body(*refs):
        in_refs, land_refs = refs[:n], refs[n:2 * n]
        send1, recv1 = refs[2 * n], refs[2 * n + 1]
        send2, recv2 = refs[2 * n + 3], refs[2 * n + 4]
        sibling, _ = _peer(1)
        for k in range(n):
            for j, mask in enumerate(FAR_MASKS):
                peer, pidx = _peer(mask)
                landed = land_refs[k].at[:, pidx]
                pltpu.make_async_remote_copy(
                    src_ref=in_refs[k], dst_ref=landed, send_sem=send1.at[k * n1 + 1 + j],
                    recv_sem=recv1.at[k * n1 + 1 + j], device_id=peer, device_id_type=MESH).wait_recv()
                pltpu.make_async_remote_copy(
                    src_ref=landed, dst_ref=landed, send_sem=send2.at[k * n2 + j],
                    recv_sem=recv2.at[k * n2 + j], device_id=sibling, device_id_type=MESH).start()

    thru = (*handle["ins"], *handle["lands"])
    outs = pl.pallas_call(
        body, name=handle["name"] + "_relay",
        out_shape=(pltpu.SemaphoreType.DMA((n * n2,)), pltpu.SemaphoreType.DMA((n * n2,)),
                   *[pltpu.HBM(a.shape, a.dtype) for a in thru]),
        in_specs=[HBM_SPEC] * (2 * n) + [SEM_SPEC, SEM_SPEC, ANY],
        out_specs=(SEM_SPEC, SEM_SPEC, *[HBM_SPEC] * (2 * n)),
        input_output_aliases={k: 2 + k for k in range(2 * n)},
        compiler_params=pltpu.CompilerParams(has_side_effects=EFFECT),
    )(*thru, handle["send"], handle["recv"], after)
    handle.update(send2=outs[0], recv2=outs[1], ins=outs[2:2 + n], lands=outs[2 + n:2 + 2 * n])


def gather_finish(handle, after):
    n = handle["n"]
    n1, n2 = len(PHASE1_MASKS), len(FAR_MASKS)

    def body(*refs):
        in_refs, land_refs = refs[:n], refs[n:2 * n]
        send1, recv1, own_sems, send2, recv2 = refs[2 * n:2 * n + 5]
        mine = _my_index()
        sibling, sib_idx = _peer(1)
        for k in range(n):
            _own_copy("gather", in_refs[k], land_refs[k], mine, own_sems.at[k]).wait()
            for j, mask in enumerate(PHASE1_MASKS):
                peer, pidx = _peer(mask)
                cp = pltpu.make_async_remote_copy(
                    src_ref=in_refs[k], dst_ref=land_refs[k].at[:, pidx], send_sem=send1.at[k * n1 + j],
                    recv_sem=recv1.at[k * n1 + j], device_id=peer, device_id_type=MESH)
                cp.wait_send()
                if mask == 1:
                    cp.wait_recv()
            for j, mask in enumerate(FAR_MASKS):
                _, pidx = _peer(mask)
                _, far_of_sibling = _peer(mask ^ 1)
                cp = pltpu.make_async_remote_copy(
                    src_ref=land_refs[k].at[:, pidx], dst_ref=land_refs[k].at[:, far_of_sibling],
                    send_sem=send2.at[k * n2 + j], recv_sem=recv2.at[k * n2 + j],
                    device_id=sibling, device_id_type=MESH)
                cp.wait_send()
                cp.wait_recv()

    thru = (*handle["ins"], *handle["lands"])
    outs = pl.pallas_call(
        body, name=handle["name"] + "_finish",
        out_shape=tuple(pltpu.HBM(a.shape, a.dtype) for a in thru),
        in_specs=[HBM_SPEC] * (2 * n) + [SEM_SPEC] * 5 + [ANY],
        out_specs=tuple([HBM_SPEC] * (2 * n)),
        input_output_aliases={k: k for k in range(2 * n)},
        compiler_params=pltpu.CompilerParams(has_side_effects=EFFECT),
    )(*thru, handle["send"], handle["recv"], handle["own"], handle["send2"], handle["recv2"], after)
    return list(outs[n:])


def all_reduce_small(packed, name, deps=()):
    rows = packed.shape[0]
    nd = len(deps)

    def body(in_ref, *rest):
        out_ref, slots, send_sems, recv_sems = rest[nd:]
        me = _my_index()
        sends = []
        for mask in range(1, N_DEV):
            peer, _ = _peer(mask)
            cp = pltpu.make_async_remote_copy(
                src_ref=in_ref, dst_ref=slots.at[me],
                send_sem=send_sems.at[mask - 1], recv_sem=recv_sems.at[mask - 1],
                device_id=peer, device_id_type=MESH)
            cp.start()
            sends.append(cp)
        slots[me] = in_ref[...]
        for mask in range(1, N_DEV):
            peer, pidx = _peer(mask)
            pltpu.make_async_remote_copy(
                src_ref=in_ref, dst_ref=slots.at[pidx],
                send_sem=send_sems.at[mask - 1], recv_sem=recv_sems.at[mask - 1],
                device_id=peer, device_id_type=MESH).wait_recv()
        for cp in sends:
            cp.wait_send()
        total = slots[0]
        for p in range(1, N_DEV):
            total = total + slots[p]
        out_ref[...] = total

    return pl.pallas_call(
        body, name=name,
        out_shape=_sds((rows, LANES), F32),
        in_specs=[pl.BlockSpec(memory_space=pltpu.VMEM)] + [ANY] * nd,
        out_specs=pl.BlockSpec(memory_space=pltpu.VMEM),
        scratch_shapes=[pltpu.VMEM((N_DEV, rows, LANES), F32),
                        pltpu.SemaphoreType.DMA((N_DEV - 1,)),
                        pltpu.SemaphoreType.DMA((N_DEV - 1,))],
    )(packed, *deps)


def matmul(dims, a, b, out_sds, grid, a_spec, b_spec, o_spec, acc_shape, *, name, alpha=1.0,
           bias=None, bias_spec=None, scale=None, scale_spec=None, res=None, res_spec=None,
           colsum_sds=None, colsum_spec=None, out_t_sds=None, out_t_spec=None, deps=()):
    nk = grid[2]
    has_bias, has_scale, has_res = bias is not None, scale is not None, res is not None
    has_cs, has_t = colsum_sds is not None, out_t_sds is not None
    if has_cs:
        assert grid[0] == 1 and dims == TN

    def body(*refs):
        a_ref, b_ref = refs[0], refs[1]
        pos = 2
        bias_ref = scale_ref = res_ref = cs_ref = ot_ref = None
        if has_bias:
            bias_ref = refs[pos]; pos += 1
        if has_scale:
            scale_ref = refs[pos]; pos += 1
        if has_res:
            res_ref = refs[pos]; pos += 1
        pos += len(deps)
        o_ref = refs[pos]; pos += 1
        if has_cs:
            cs_ref = refs[pos]; pos += 1
        if has_t:
            ot_ref = refs[pos]; pos += 1
        k = pl.program_id(2)
        bval = b_ref[...]
        part = lax.dot_general(a_ref[...].astype(BF16), bval.astype(BF16), dims,
                               preferred_element_type=F32)

        def finish(total):
            r = total * alpha if alpha != 1.0 else total
            if has_bias:
                r = r + bias_ref[...]
            if has_scale:
                r = r * scale_ref[...]
            if has_res:
                r = r + res_ref[...].astype(F32)
            o_ref[...] = r.astype(o_ref.dtype)
            if has_t:
                ot_ref[...] = r.T.astype(ot_ref.dtype)

        if has_cs:
            csum = jnp.sum(bval.astype(F32), axis=0, keepdims=True)

            @pl.when(k == 0)
            def _():
                cs_ref[...] = csum

            @pl.when(k > 0)
            def _():
                cs_ref[...] += csum

        if nk == 1:
            finish(part)
        else:
            acc_ref = refs[pos]

            @pl.when(k == 0)
            def _():
                acc_ref[...] = part

            @pl.when(k > 0)
            def _():
                acc_ref[...] += part

            @pl.when(k == nk - 1)
            def _():
                finish(acc_ref[...])

    in_specs, args = [a_spec, b_spec], [a, b]
    if has_bias:
        in_specs.append(bias_spec); args.append(bias)
    if has_scale:
        in_specs.append(scale_spec); args.append(scale)
    if has_res:
        in_specs.append(res_spec); args.append(res)
    in_specs += [ANY] * len(deps)
    args += list(deps)
    out_shape, out_specs = [out_sds], [o_spec]
    if has_cs:
        out_shape.append(colsum_sds); out_specs.append(colsum_spec)
    if has_t:
        out_shape.append(out_t_sds); out_specs.append(out_t_spec)
    scratch = [] if nk == 1 else [pltpu.VMEM(acc_shape, F32)]
    outs = pl.pallas_call(
        body, name=name, grid=grid, in_specs=in_specs, out_specs=out_specs, out_shape=out_shape,
        scratch_shapes=scratch, compiler_params=_cparams(3))(*args)
    return outs if (has_cs or has_t) else outs[0]


def _sigmoid(z):
    return 1.0 / (1.0 + jnp.exp(-z))


def _log_sigmoid(z):
    return jnp.minimum(z, 0.0) - jnp.log(1.0 + jnp.exp(-jnp.abs(z)))


def rmsnorm_fwd(x, gain, tm, name, deps=()):
    t, d = x.shape

    def body(x_ref, g_ref, *rest):
        o_ref = rest[-1]
        xf = x_ref[...]
        r = lax.rsqrt(jnp.mean(xf * xf, axis=-1, keepdims=True) + RMS_EPS)
        o_ref[...] = (xf * r * g_ref[...]).astype(o_ref.dtype)

    return pl.pallas_call(
        body, name=name, grid=(t // tm,),
        in_specs=[pl.BlockSpec((tm, d), lambda i: (i, 0)), pl.BlockSpec((1, d), lambda i: (0, 0))]
        + [ANY] * len(deps),
        out_specs=pl.BlockSpec((tm, d), lambda i: (i, 0)),
        out_shape=_sds((t, d), BF16), compiler_params=_cparams(1))(x, gain, *deps)


def rmsnorm_bwd(x, gain, dh, dres, tm, name):
    t, d = x.shape

    def body(x_ref, g_ref, dh_ref, dres_ref, dx_ref, dg_ref):
        i = pl.program_id(0)
        xf = x_ref[...]
        r = lax.rsqrt(jnp.mean(xf * xf, axis=-1, keepdims=True) + RMS_EPS)
        xhat = xf * r
        dh_v = dh_ref[...]
        dxhat = dh_v * g_ref[...]
        dx = r * (dxhat - xhat * jnp.mean(dxhat * xhat, axis=-1, keepdims=True))
        dx_ref[...] = dres_ref[...] + dx
        dg = jnp.sum(dh_v * xhat, axis=0, keepdims=True)

        @pl.when(i == 0)
        def _():
            dg_ref[...] = dg

        @pl.when(i > 0)
        def _():
            dg_ref[...] += dg

    row = pl.BlockSpec((tm, d), lambda i: (i, 0))
    vec = pl.BlockSpec((1, d), lambda i: (0, 0))
    return pl.pallas_call(
        body, name=name, grid=(t // tm,), in_specs=[row, vec, row, row], out_specs=[row, vec],
        out_shape=[_sds((t, d), F32), _sds((1, d), F32)], compiler_params=_cparams(1))(x, gain, dh, dres)


def loss_head(x, gain, target, tm, name):
    t, d = x.shape

    def body(x_ref, g_ref, tgt_ref, dx_ref, dg_ref, loss_ref):
        i = pl.program_id(0)
        xf = x_ref[...]
        g = g_ref[...]
        r = lax.rsqrt(jnp.mean(xf * xf, axis=-1, keepdims=True) + RMS_EPS)
        xhat = xf * r
        err = xhat * g - tgt_ref[...]
        part = 0.5 * jnp.sum(jnp.mean(err * err, axis=-1, keepdims=True))
        dy = err * (1.0 / d)
        dxhat = dy * g
        dx_ref[...] = r * (dxhat - xhat * jnp.mean(dxhat * xhat, axis=-1, keepdims=True))
        dg = jnp.sum(dy * xhat, axis=0, keepdims=True)
        lpart = jnp.full((8, LANES), part, F32)

        @pl.when(i == 0)
        def _():
            dg_ref[...] = dg
            loss_ref[...] = lpart

        @pl.when(i > 0)
        def _():
            dg_ref[...] += dg
            loss_ref[...] += lpart

    row = pl.BlockSpec((tm, d), lambda i: (i, 0))
    vec = pl.BlockSpec((1, d), lambda i: (0, 0))
    return pl.pallas_call(
        body, name=name, grid=(t // tm,), in_specs=[row, vec, row],
        out_specs=[row, vec, pl.BlockSpec((8, LANES), lambda i: (0, 0))],
        out_shape=[_sds((t, d), F32), _sds((1, d), F32), _sds((8, LANES), F32)],
        compiler_params=_cparams(1))(x, gain, target)


def ffn_in_swiglu(hn, wa, s, tm, name):
    t = hn.shape[0]
    halves = 2 if tm % 512 == 0 else 1
    rows = tm // halves

    def body(h_ref, wg_ref, wu_ref, gu_ref, act_ref):
        for c in range(halves):
            rs = slice(c * rows, (c + 1) * rows)
            h = h_ref[rs, :]
            g = jnp.dot(h, wg_ref[...], preferred_element_type=F32)
            u = jnp.dot(h, wu_ref[...], preferred_element_type=F32)
            gu_ref[0, rs, :] = g.astype(gu_ref.dtype)
            gu_ref[1, rs, :] = u.astype(gu_ref.dtype)
            act_ref[rs, :] = (g * _sigmoid(g) * u).astype(act_ref.dtype)

    return pl.pallas_call(
        body, name=name, grid=(t // tm, 4),
        in_specs=[pl.BlockSpec((tm, D_MODEL), lambda i, j: (i, 0)),
                  pl.BlockSpec((None, None, D_MODEL, FF_BLK), lambda i, j: (s, j, 0, 0)),
                  pl.BlockSpec((None, None, D_MODEL, FF_BLK), lambda i, j: (s, j + 4, 0, 0))],
        out_specs=[pl.BlockSpec((None, 2, tm, FF_BLK), lambda i, j: (j, 0, i, 0)),
                   pl.BlockSpec((None, tm, FF_BLK), lambda i, j: (j, i, 0))],
        out_shape=[_sds((4, 2, t, FF_BLK), BF16), _sds((4, t, FF_BLK), BF16)],
        compiler_params=_cparams(2))(hn, wa, wa)


def ffn_dact_swiglu(dy, wb, gu, s, tm, name):
    t = dy.shape[0]

    def body(dy_ref, w_ref, gu_ref, o_ref):
        da = 0.5 * lax.dot_general(dy_ref[...].astype(BF16), w_ref[...], NT, preferred_element_type=F32)
        g = gu_ref[0].astype(F32)
        u = gu_ref[1].astype(F32)
        sg = _sigmoid(g)
        o_ref[0] = (da * u * (sg * (1.0 + g * (1.0 - sg)))).astype(o_ref.dtype)
        o_ref[1] = (da * g * sg).astype(o_ref.dtype)

    blk = pl.BlockSpec((None, 2, tm, FF_BLK), lambda i, j: (j, 0, i, 0))
    return pl.pallas_call(
        body, name=name, grid=(t // tm, 4),
        in_specs=[pl.BlockSpec((tm, D_MODEL), lambda i, j: (i, 0)),
                  pl.BlockSpec((None, None, FF_BLK, D_MODEL), lambda i, j: (s, j, 0, 0)), blk],
        out_specs=blk, out_shape=_sds((4, 2, t, FF_BLK), BF16),
        compiler_params=_cparams(2))(dy, wb, gu)


def ffn_out_residual(act, wb, x, s, tm, name):
    t = x.shape[0]

    def body(a_ref, w_ref, x_ref, o_ref):
        acc = jnp.dot(a_ref[0], w_ref[0], preferred_element_type=F32)
        for k in range(1, 4):
            acc = acc + jnp.dot(a_ref[k], w_ref[k], preferred_element_type=F32)
        o_ref[...] = x_ref[...] + 0.5 * acc

    row = pl.BlockSpec((tm, D_MODEL), lambda i: (i, 0))
    return pl.pallas_call(
        body, name=name, grid=(t // tm,),
        in_specs=[pl.BlockSpec((4, tm, FF_BLK), lambda i: (0, i, 0)),
                  pl.BlockSpec((None, 4, FF_BLK, D_MODEL), lambda i: (s, 0, 0, 0)), row],
        out_specs=row, out_shape=_sds((t, D_MODEL), F32), compiler_params=_cparams(1))(act, wb, x)


def ffn_dh_norm_bwd(dgu, wa, s, x, gain, dres, tm, name, deps):
    t = dgu.shape[2]
    nd = len(deps)

    def body(g_ref, w_ref, x_ref, gain_ref, dres_ref, *rest):
        dx_ref, dg_ref = rest[nd:]
        i = pl.program_id(0)
        dh = lax.dot_general(g_ref[0, 0], w_ref[0], NT, preferred_element_type=F32)
        for p in range(1, N_DEV):
            dh = dh + lax.dot_general(g_ref[p % 4, p // 4], w_ref[p], NT, preferred_element_type=F32)
        xf = x_ref[...]
        r = lax.rsqrt(jnp.mean(xf * xf, axis=-1, keepdims=True) + RMS_EPS)
        xhat = xf * r
        dxhat = dh * gain_ref[...]
        dx_ref[...] = dres_ref[...] + r * (dxhat - xhat * jnp.mean(dxhat * xhat, axis=-1, keepdims=True))
        dg = jnp.sum(dh * xhat, axis=0, keepdims=True)

        @pl.when(i == 0)
        def _():
            dg_ref[...] = dg

        @pl.when(i > 0)
        def _():
            dg_ref[...] += dg

    row = pl.BlockSpec((tm, D_MODEL), lambda i: (i, 0))
    vec = pl.BlockSpec((1, D_MODEL), lambda i: (0, 0))
    return pl.pallas_call(
        body, name=name, grid=(t // tm,),
        in_specs=[pl.BlockSpec((4, 2, tm, FF_BLK), lambda i: (0, 0, i, 0)),
                  pl.BlockSpec((None, N_DEV, D_MODEL, FF_BLK), lambda i: (s, 0, 0, 0)), row, vec, row]
        + [ANY] * nd,
        out_specs=[row, vec], out_shape=[_sds((t, D_MODEL), F32), _sds((1, D_MODEL), F32)],
        compiler_params=_cparams(1))(dgu, wa, x, gain, dres, *deps)


def merge_fwd(gates, ya, yb, yc, tm, name):
    t, d = ya.shape

    def body(ga_ref, gb_ref, gc_ref, ya_ref, yb_ref, yc_ref, o_ref):
        m = (_sigmoid(ga_ref[...]) * ya_ref[...] + _sigmoid(gb_ref[...]) * yb_ref[...]
             + _sigmoid(gc_ref[...]) * yc_ref[...])
        o_ref[...] = m.astype(o_ref.dtype)

    row = pl.BlockSpec((tm, d), lambda i: (i, 0))
    gspecs = [pl.BlockSpec((tm, d), functools.partial(lambda i, a: (i, a), a=a)) for a in range(3)]
    return pl.pallas_call(
        body, name=name, grid=(t // tm,), in_specs=gspecs + [row, row, row], out_specs=row,
        out_shape=_sds((t, d), BF16), compiler_params=_cparams(1))(gates, gates, gates, ya, yb, yc)


def merge_bwd(dm, gates, ya, yb, yc, tm, name):
    t, d = ya.shape

    def body(dm_ref, g_ref, ya_ref, yb_ref, yc_ref, dg_ref, dya_ref, dyb_ref, dyc_ref):
        dmv = dm_ref[...]
        for a, (y_ref, dy_ref) in enumerate(((ya_ref, dya_ref), (yb_ref, dyb_ref), (yc_ref, dyc_ref))):
            cols = slice(a * d, (a + 1) * d)
            s = _sigmoid(g_ref[:, cols])
            dy_ref[...] = (dmv * s).astype(dy_ref.dtype)
            dg_ref[:, cols] = (dmv * y_ref[...] * s * (1.0 - s)).astype(dg_ref.dtype)

    row = pl.BlockSpec((tm, d), lambda i: (i, 0))
    wide = pl.BlockSpec((tm, 3 * d), lambda i: (i, 0))
    dg, dya, dyb, dyc = pl.pallas_call(
        body, name=name, grid=(t // tm,), in_specs=[row, wide, row, row, row],
        out_specs=[wide, row, row, row],
        out_shape=[_sds((t, 3 * d), BF16)] + [_sds((t, d), BF16)] * 3,
        compiler_params=_cparams(1))(dm, gates, ya, yb, yc)
    return dg, [dya, dyb, dyc]


def _iota2(shape, dim):
    return lax.broadcasted_iota(jnp.int32, shape, dim)


def forget_cumsum(f, name):
    t = f.shape[0]
    nq = t // QB

    def body(f_ref, fcol_ref, frow_ref, carry):
        j = pl.program_id(0)

        @pl.when(j == 0)
        def _():
            carry[...] = jnp.zeros_like(carry)

        logf = _log_sigmoid(f_ref[...])
        tri = (_iota2((QB, QB), 1) <= _iota2((QB, QB), 0)).astype(F32)
        blk = jnp.dot(tri, logf, precision=HIGHEST, preferred_element_type=F32) + carry[...]
        carry[...] += jnp.sum(logf, axis=0, keepdims=True)
        fcol_ref[...] = blk
        frow_ref[...] = blk.T[0:8, :]

    return pl.pallas_call(
        body, name=name, grid=(nq,),
        in_specs=[pl.BlockSpec((QB, LANES), lambda j: (j, 0))],
        out_specs=[pl.BlockSpec((QB, LANES), lambda j: (j, 0)),
                   pl.BlockSpec((None, 8, QB), lambda j: (j, 0, 0))],
        out_shape=[_sds((t, LANES), F32), _sds((nq, 8, QB), F32)],
        scratch_shapes=[pltpu.VMEM((1, LANES), F32)], compiler_params=_cparams(1))(f)


def forget_cumsum_bwd(dfrow, f, name):
    t = f.shape[0]
    nq = t // QB

    def body(dfr_ref, f_ref, df_ref, carry):
        jj = pl.program_id(0)

        @pl.when(jj == 0)
        def _():
            carry[...] = jnp.zeros_like(carry)

        padded = jnp.concatenate([dfr_ref[...], jnp.zeros((QB - 8, QB), F32)], axis=0)
        dfcol = padded.T
        tri = (_iota2((QB, QB), 1) >= _iota2((QB, QB), 0)).astype(F32)
        dlogf = jnp.dot(tri, dfcol, precision=HIGHEST, preferred_element_type=F32) + carry[...]
        carry[...] += jnp.sum(dfcol, axis=0, keepdims=True)
        df_ref[...] = dlogf * _sigmoid(-f_ref[...])

    return pl.pallas_call(
        body, name=name, grid=(nq,),
        in_specs=[pl.BlockSpec((None, 8, QB), lambda jj: (nq - 1 - jj, 0, 0)),
                  pl.BlockSpec((QB, LANES), lambda jj: (nq - 1 - jj, 0))],
        out_specs=pl.BlockSpec((QB, LANES), lambda jj: (nq - 1 - jj, 0)),
        out_shape=_sds((t, LANES), F32),
        scratch_shapes=[pltpu.VMEM((1, LANES), F32)], compiler_params=_cparams(1))(dfrow, f)


REL_DIAG = 768
REL_SHIFT = REL_DIAG - (QB - 1)


def _diag_onehot():
    u = _iota2((REL_PAD, REL_DIAG), 1)
    rel = jnp.clip(CH_KEYS - 1 - u, -MAX_REL, MAX_REL) + MAX_REL
    return (_iota2((REL_PAD, REL_DIAG), 0) == rel).astype(F32)


def rel_bias_build(tab_t, name):
    def body(tab_ref, o_ref):
        diag = jnp.dot(tab_ref[...], _diag_onehot(), precision=HIGHEST, preferred_element_type=F32)
        band = _chunk_band()
        for h in range(N_HEADS_CH):
            rows = jnp.broadcast_to(diag[h:h + 1, :], (QB, REL_DIAG))
            o_ref[h] = pltpu.roll(rows, REL_SHIFT, 1, stride=1, stride_axis=0)[:, :CH_KEYS] + band

    return pl.pallas_call(
        body, name=name, out_shape=_sds((N_HEADS_CH, QB, CH_KEYS), F32),
        in_specs=[pl.BlockSpec(memory_space=pltpu.VMEM)], out_specs=pl.BlockSpec(memory_space=pltpu.VMEM),
    )(tab_t)


def rel_bias_scatter(dbias, name):
    def body(db_ref, o_ref, ddiag):
        flip = (_iota2((QB, QB), 0) + _iota2((QB, QB), 1) == QB - 1).astype(F32)
        for h in range(N_HEADS_CH):
            padded = jnp.concatenate([db_ref[h], jnp.zeros((QB, REL_DIAG - CH_KEYS), F32)], axis=1)
            flipped = jnp.dot(flip, padded, precision=HIGHEST, preferred_element_type=F32)
            unrolled = pltpu.roll(flipped, 0, 1, stride=1, stride_axis=0)
            ddiag[h:h + 1, :] = jnp.sum(unrolled, axis=0, keepdims=True)
        o_ref[...] = lax.dot_general(ddiag[...], _diag_onehot(), NT, precision=HIGHEST,
                                     preferred_element_type=F32)

    return pl.pallas_call(
        body, name=name, out_shape=_sds((N_HEADS_CH, REL_PAD), F32),
        in_specs=[pl.BlockSpec(memory_space=pltpu.VMEM)], out_specs=pl.BlockSpec(memory_space=pltpu.VMEM),
        scratch_shapes=[pltpu.VMEM((N_HEADS_CH, REL_DIAG), F32)],
    )(dbias)


def _hl(h):
    return slice(h * HEAD_DIM, (h + 1) * HEAD_DIM)


def _split_dot(x, tri_bf16):
    hi = x.astype(BF16)
    lo = (x - hi.astype(F32)).astype(BF16)
    return (jnp.dot(hi, tri_bf16, preferred_element_type=F32)
            + jnp.dot(lo, tri_bf16, preferred_element_type=F32))


def _rows(j):
    return pl.ds(pl.multiple_of(j * QB, QB), QB)


def _krows(g):
    return pl.ds(pl.multiple_of(g * KB, KB), KB)


def _log_sigmoid_pair(z):
    sp = jnp.log(1.0 + jnp.exp(-jnp.abs(z)))
    return jnp.minimum(z, 0.0) - sp, -jnp.maximum(z, 0.0) - sp


def _qkv_specs(t, col0, n_pairs):
    q_spec = pl.BlockSpec((QB, LANES), lambda hp, i: (i, col0 + hp))
    k_spec = pl.BlockSpec((t, LANES), lambda hp, i: (0, col0 + n_pairs + hp))
    v_spec = pl.BlockSpec((t, LANES), lambda hp, i: (0, col0 + 2 * n_pairs + hp))
    return q_spec, k_spec, v_spec


def _keys_major(xt):
    pairs, groups, _, _ = xt.shape
    return xt.transpose(1, 3, 0, 2).reshape(groups * KB, pairs * LANES)


def sb_fwd(qkv, name):
    t = qkv.shape[0]
    nq = t // QB

    def body(q_ref, k_ref, v_ref, o_ref, w_ref):
        i = pl.program_id(1)
        groups = i // KSUB + 1
        tri_after = (_iota2((KB, KB), 0) > _iota2((KB, KB), 1)).astype(BF16)
        t_idx = i * QB + _iota2((QB, KB), 0)
        qs = [q_ref[:, _hl(h)] for h in range(2)]

        def step(g, carry, masked):
            strict = (g * KB + _iota2((QB, KB), 1)) < t_idx
            out = []
            for h in range(2):
                tail, acc = carry[2 * h], carry[2 * h + 1]
                k = k_ref[_krows(g), _hl(h)]
                v = v_ref[_krows(g), _hl(h)]
                z = lax.dot_general(qs[h], k, NT, preferred_element_type=F32)
                lb, lf = _log_sigmoid_pair(z)
                if masked:
                    lf = jnp.where(strict, lf, 0.0)
                between = _split_dot(lf, tri_after) + tail
                w = jnp.exp(lb + between)
                if masked:
                    w = jnp.where(strict, w, 0.0)
                w = w.astype(BF16)
                w_ref[h, g] = w
                acc = acc + jnp.dot(w, v, preferred_element_type=F32)
                out += [tail + jnp.sum(lf, axis=1, keepdims=True), acc]
            return tuple(out)

        init = (jnp.zeros((QB, 1), F32), jnp.zeros((QB, HEAD_DIM), F32)) * 2
        res = step(groups - 1, init, True)
        res = lax.fori_loop(0, groups - 1, lambda gg, c: step(groups - 2 - gg, c, False), res)
        for h in range(2):
            o_ref[:, _hl(h)] = res[2 * h + 1].astype(o_ref.dtype)

    q_spec, k_spec, v_spec = _qkv_specs(t, 0, 2)
    return pl.pallas_call(
        body, name=name, grid=(2, nq), in_specs=[q_spec, k_spec, v_spec],
        out_specs=[pl.BlockSpec((QB, LANES), lambda hp, i: (i, hp)),
                   pl.BlockSpec((2, None, t // KB, QB, KB), lambda hp, i: (hp, i, 0, 0, 0))],
        out_shape=[_sds((t, W_SB), BF16), _sds((4, nq, t // KB, QB, KB), BF16)],
        compiler_params=_cparams(2))(qkv, qkv, qkv)


def _hs(h):
    return slice(h * HEAD_DIM, (h + 1) * HEAD_DIM)


def sb_bwd(qkv, qkv_t, w, do, do_t, name):
    t = qkv.shape[0]
    nq = t // QB

    def body(q_ref, k_ref, v_ref, do_ref, qt_ref, dot_ref, w_ref, dq_ref, dkt_ref, dvt_ref):
        i = pl.program_id(1)

        @pl.when(i == 0)
        def _():
            dkt_ref[...] = jnp.zeros_like(dkt_ref)
            dvt_ref[...] = jnp.zeros_like(dvt_ref)

        groups = i // KSUB + 1
        tri_before = (_iota2((KB, KB), 0) < _iota2((KB, KB), 1)).astype(BF16)
        t_idx = i * QB + _iota2((QB, KB), 0)
        qs = [q_ref[:, _hl(h)] for h in range(2)]
        dos = [do_ref[:, _hl(h)] for h in range(2)]
        qts = [qt_ref[_hs(h), :] for h in range(2)]
        dots = [dot_ref[_hs(h), :] for h in range(2)]

        def grads(g, carry, masked):
            strict = (g * KB + _iota2((QB, KB), 1)) < t_idx
            out = []
            for h in range(2):
                head, dq = carry[2 * h], carry[2 * h + 1]
                k = k_ref[_krows(g), _hl(h)]
                v = v_ref[_krows(g), _hl(h)]
                wb = w_ref[h, g]
                z = lax.dot_general(qs[h], k, NT, preferred_element_type=F32)
                beta = _sigmoid(z)
                e = lax.dot_general(dos[h], v, NT, preferred_element_type=F32) * wb.astype(F32)
                before = _split_dot(e, tri_before) + head
                dz = e * (1.0 - beta) - before * beta
                if masked:
                    dz = jnp.where(strict, dz, 0.0)
                dzb = dz.astype(BF16)
                dq = dq + jnp.dot(dzb, k, preferred_element_type=F32)
                dkt_ref[g, _hs(h), :] += jnp.dot(qts[h], dzb, preferred_element_type=F32)
                dvt_ref[g, _hs(h), :] += jnp.dot(dots[h], wb, preferred_element_type=F32)
                out += [head + jnp.sum(e, axis=1, keepdims=True), dq]
            return tuple(out)

        init = (jnp.zeros((QB, 1), F32), jnp.zeros((QB, HEAD_DIM), F32)) * 2
        res = lax.fori_loop(0, groups - 1, lambda g, c: grads(g, c, False), init)
        res = grads(groups - 1, res, True)
        for h in range(2):
            dq_ref[:, _hl(h)] = (res[2 * h + 1] * SCALE).astype(dq_ref.dtype)

    q_spec, k_spec, v_spec = _qkv_specs(t, 0, 2)
    blk = pl.BlockSpec((QB, LANES), lambda hp, i: (i, hp))
    blk_t = pl.BlockSpec((LANES, QB), lambda hp, i: (hp, i))
    acc_t = pl.BlockSpec((None, t // KB, LANES, KB), lambda hp, i: (hp, 0, 0, 0))
    acc_sds = _sds((2, t // KB, LANES, KB), F32)
    return pl.pallas_call(
        body, name=name, grid=(2, nq),
        in_specs=[q_spec, k_spec, v_spec, blk, blk_t, blk_t,
                  pl.BlockSpec((2, None, t // KB, QB, KB), lambda hp, i: (hp, i, 0, 0, 0))],
        out_specs=[blk, acc_t, acc_t],
        out_shape=[_sds((t, W_SB), BF16), acc_sds, acc_sds],
        compiler_params=_cparams(2))(qkv, qkv, qkv, do, qkv_t, do_t, w)


def fox_fwd(qkv, fcol, frow, name):
    t = qkv.shape[0]
    nq = t // QB

    def body(q_ref, k_ref, v_ref, fc_ref, fr_ref, o_ref, lse_ref):
        hp = pl.program_id(0)
        i = pl.program_id(1)
        groups = i // KSUB + 1
        t_idx = i * QB + _iota2((QB, KB), 0)
        lane = _iota2((QB, LANES), 1)
        sub = _iota2((8, KB), 0)
        qs = [q_ref[:, _hl(h)] for h in range(2)]
        f_qs = [jnp.sum(jnp.where(lane == hp * 2 + h, fc_ref[...], 0.0), axis=1, keepdims=True)
                for h in range(2)]

        def step(g, carry, masked):
            causal = (g * KB + _iota2((QB, KB), 1)) <= t_idx
            fr = fr_ref[g]
            out = []
            for h in range(2):
                m, l, acc = carry[3 * h:3 * h + 3]
                k = k_ref[_krows(g), _hl(h)]
                v = v_ref[_krows(g), _hl(h)]
                f_k = jnp.sum(jnp.where(sub == hp * 2 + h, fr, 0.0), axis=0, keepdims=True)
                z = lax.dot_general(qs[h], k, NT, preferred_element_type=F32) + f_qs[h] - f_k
                if masked:
                    z = jnp.where(causal, z, NEG)
                m_new = jnp.maximum(m, jnp.max(z, axis=1, keepdims=True))
                p = jnp.exp(z - m_new)
                corr = jnp.exp(m - m_new)
                l = l * corr + jnp.sum(p, axis=1, keepdims=True)
                acc = acc * corr + jnp.dot(p.astype(BF16), v, preferred_element_type=F32)
                out += [m_new, l, acc]
            return tuple(out)

        init = (jnp.full((QB, 1), NEG, F32), jnp.zeros((QB, 1), F32), jnp.zeros((QB, HEAD_DIM), F32)) * 2
        res = lax.fori_loop(0, groups - 1, lambda g, c: step(g, c, False), init)
        res = step(groups - 1, res, True)
        for h in range(2):
            m, l, acc = res[3 * h:3 * h + 3]
            o_ref[:, _hl(h)] = (acc / l).astype(o_ref.dtype)
            lse_ref[:, _hl(h)] = jnp.broadcast_to(m + jnp.log(l), (QB, HEAD_DIM))

    q_spec, k_spec, v_spec = _qkv_specs(t, 18, 2)
    blk = pl.BlockSpec((QB, LANES), lambda hp, i: (i, hp))
    return pl.pallas_call(
        body, name=name, grid=(2, nq),
        in_specs=[q_spec, k_spec, v_spec, pl.BlockSpec((QB, LANES), lambda hp, i: (i, 0)),
                  pl.BlockSpec((t // KB, 8, KB), lambda hp, i: (0, 0, 0))],
        out_specs=[blk, blk],
        out_shape=[_sds((t, W_FOX), BF16), _sds((t, W_FOX), F32)],
        compiler_params=_cparams(2))(qkv, qkv, qkv, fcol, frow)


def fox_bwd(qkv, qkv_t, fcol, frow, o, lse, do, do_t, name):
    t = qkv.shape[0]
    nq = t // QB

    def body(q_ref, k_ref, v_ref, fc_ref, fr_ref, o_ref, lse_ref, do_ref, qt_ref, dot_ref,
             dq_ref, dk_ref, dv_ref, dfr_ref):
        hp = pl.program_id(0)
        i = pl.program_id(1)
        qts = [qt_ref[_hs(h), :] for h in range(2)]
        dots = [dot_ref[_hs(h), :] for h in range(2)]

        @pl.when(i == 0)
        def _():
            dk_ref[...] = jnp.zeros_like(dk_ref)
            dv_ref[...] = jnp.zeros_like(dv_ref)

        @pl.when((i == 0) & (hp == 0))
        def _():
            dfr_ref[...] = jnp.zeros_like(dfr_ref)

        groups = i // KSUB + 1
        t_idx = i * QB + _iota2((QB, KB), 0)
        lane = _iota2((QB, LANES), 1)
        sub = _iota2((8, KB), 0)
        qs = [q_ref[:, _hl(h)] for h in range(2)]
        dos = [do_ref[:, _hl(h)] for h in range(2)]
        f_qs = [jnp.sum(jnp.where(lane == hp * 2 + h, fc_ref[...], 0.0), axis=1, keepdims=True)
                for h in range(2)]
        lse_qs = [lse_ref[:, h * HEAD_DIM:h * HEAD_DIM + 1] for h in range(2)]
        deltas = [jnp.sum(dos[h].astype(F32) * o_ref[:, _hl(h)].astype(F32), axis=1, keepdims=True)
                  for h in range(2)]

        def step(g, dqs, masked):
            causal = (g * KB + _iota2((QB, KB), 1)) <= t_idx
            fr = fr_ref[g]
            out = []
            dfr = jnp.zeros((8, KB), F32)
            for h in range(2):
                k = k_ref[_krows(g), _hl(h)]
                v = v_ref[_krows(g), _hl(h)]
                f_k = jnp.sum(jnp.where(sub == hp * 2 + h, fr, 0.0), axis=0, keepdims=True)
                z = lax.dot_general(qs[h], k, NT, preferred_element_type=F32) + f_qs[h] - f_k
                p = jnp.exp(z - lse_qs[h])
                if masked:
                    p = jnp.where(causal, p, 0.0)
                dp = lax.dot_general(dos[h], v, NT, preferred_element_type=F32)
                ds = p * (dp - deltas[h])
                dsb = ds.astype(BF16)
                out.append(dqs[h] + jnp.dot(dsb, k, preferred_element_type=F32))
                dk_ref[g, _hs(h), :] += jnp.dot(qts[h], dsb, preferred_element_type=F32)
                dv_ref[g, _hs(h), :] += jnp.dot(dots[h], p.astype(BF16), preferred_element_type=F32)
                colsum = jnp.sum(ds, axis=0, keepdims=True)
                dfr = dfr + jnp.where(sub == hp * 2 + h, -colsum, 0.0)
            dfr_ref[g] += dfr
            return tuple(out)

        res = lax.fori_loop(0, groups - 1, lambda g, c: step(g, c, False),
                            (jnp.zeros((QB, HEAD_DIM), F32),) * 2)
        res = step(groups - 1, res, True)
        for h in range(2):
            dq_ref[:, _hl(h)] = (res[h] * SCALE).astype(dq_ref.dtype)

    q_spec, k_spec, v_spec = _qkv_specs(t, 18, 2)
    blk = pl.BlockSpec((QB, LANES), lambda hp, i: (i, hp))
    frs = pl.BlockSpec((t // KB, 8, KB), lambda hp, i: (0, 0, 0))
    acc_t = pl.BlockSpec((None, t // KB, LANES, KB), lambda hp, i: (hp, 0, 0, 0))
    acc_sds = _sds((2, t // KB, LANES, KB), F32)
    return pl.pallas_call(
        body, name=name, grid=(2, nq),
        in_specs=[q_spec, k_spec, v_spec, pl.BlockSpec((QB, LANES), lambda hp, i: (i, 0)), frs,
                  blk, blk, blk, pl.BlockSpec((LANES, QB), lambda hp, i: (18 + hp, i)),
                  pl.BlockSpec((LANES, QB), lambda hp, i: (hp, i))],
        out_specs=[blk, acc_t, acc_t, frs],
        out_shape=[_sds((t, W_FOX), BF16), acc_sds, acc_sds, _sds((t // KB, 8, KB), F32)],
        compiler_params=_cparams(2))(qkv, qkv, qkv, fcol, frow, o, lse, do, qkv_t, do_t)


def _frow_to_groups(frow):
    n = frow.shape[0] // KSUB
    return frow.reshape(n, KSUB, 8, QB).transpose(0, 2, 1, 3).reshape(n, 8, KB)


def _frow_from_groups(frow):
    n = frow.shape[0]
    return frow.reshape(n, 8, KSUB, QB).transpose(0, 2, 1, 3).reshape(n * KSUB, 8, QB)


def _chunk_band():
    qi = _iota2((QB, CH_KEYS), 0)
    kj = _iota2((QB, CH_KEYS), 1)
    dchunk = (qi >> 6) + LEFT_CHUNKS - (kj >> 6)
    return jnp.where((dchunk >= 0) & (dchunk <= LEFT_CHUNKS), 0.0, NEG)


def _chunk_pad_row(i):
    kj = _iota2((1, CH_KEYS), 1)
    return jnp.where((i - (CH_WIN - 1)) * QB + kj >= 0, 0.0, NEG)


CH_PAD = (CH_WIN - 1) * QB
CH_STEP_HEADS = 4
CH_COLS = CH_STEP_HEADS * HEAD_DIM


def _window(i):
    return pl.ds(pl.multiple_of(i * QB, QB), CH_KEYS)


def _chunk_weights(q, kw, bias, pad_row):
    z = lax.dot_general(q, kw, NT, preferred_element_type=F32) + bias + pad_row
    e = jnp.exp(z - jnp.max(z, axis=1, keepdims=True))
    return e, 1.0 / jnp.sum(e, axis=1, keepdims=True)


def _chunk_specs(t):
    q_spec = pl.BlockSpec((QB, CH_COLS), lambda hp, i: (i, 3 * W_SB // CH_COLS + hp))
    kv_spec = pl.BlockSpec((t + CH_PAD, CH_COLS), lambda hp, i: (0, hp))
    return q_spec, kv_spec


def chunk_fwd(qkv, kp, vp, bias, name):
    t = qkv.shape[0]
    nq = t // QB

    def body(q_ref, k_ref, v_ref, b_ref, o_ref):
        i = pl.program_id(1)
        pad_row = _chunk_pad_row(i)
        for h in range(CH_STEP_HEADS):
            e, inv = _chunk_weights(q_ref[:, _hl(h)], k_ref[_window(i), _hl(h)], b_ref[h], pad_row)
            o = jnp.dot(e.astype(BF16), v_ref[_window(i), _hl(h)], preferred_element_type=F32)
            o_ref[:, _hl(h)] = (o * inv).astype(o_ref.dtype)

    q_spec, kv_spec = _chunk_specs(t)
    return pl.pallas_call(
        body, name=name, grid=(W_CH // CH_COLS, nq),
        in_specs=[q_spec, kv_spec, kv_spec,
                  pl.BlockSpec((CH_STEP_HEADS, QB, CH_KEYS), lambda hp, i: (hp, 0, 0))],
        out_specs=pl.BlockSpec((QB, CH_COLS), lambda hp, i: (i, hp)),
        out_shape=_sds((t, W_CH), BF16), compiler_params=_cparams(2))(qkv, kp, vp, bias)


def chunk_bwd(qkv, qkv_t, kp, vp, bias, do, do_t, name):
    t = qkv.shape[0]
    nq = t // QB

    def body(q_ref, k_ref, v_ref, b_ref, do_ref, qt_ref, dot_ref, dq_ref, dk_ref, dv_ref, db_ref):
        i = pl.program_id(1)

        @pl.when(i == 0)
        def _():
            dk_ref[...] = jnp.zeros_like(dk_ref)
            dv_ref[...] = jnp.zeros_like(dv_ref)
            db_ref[...] = jnp.zeros_like(db_ref)

        pad_row = _chunk_pad_row(i)
        for h in range(CH_STEP_HEADS):
            q = q_ref[:, _hl(h)]
            dov = do_ref[:, _hl(h)]
            kw = k_ref[_window(i), _hl(h)]
            e, inv = _chunk_weights(q, kw, b_ref[h], pad_row)
            p = e * inv
            dp = lax.dot_general(dov, v_ref[_window(i), _hl(h)], NT, preferred_element_type=F32)
            ds = p * (dp - jnp.sum(p * dp, axis=1, keepdims=True))
            db_ref[h] += ds
            dsb = ds.astype(BF16)
            dq_ref[:, _hl(h)] = (jnp.dot(dsb, kw, preferred_element_type=F32) * SCALE).astype(dq_ref.dtype)
            dkt = jnp.dot(qt_ref[_hs(h), :], dsb, preferred_element_type=F32)
            dvt = jnp.dot(dot_ref[_hs(h), :], p.astype(BF16), preferred_element_type=F32)
            for b in range(CH_WIN):
                dk_ref[i + b, _hs(h), :] += dkt[:, b * QB:(b + 1) * QB]
                dv_ref[i + b, _hs(h), :] += dvt[:, b * QB:(b + 1) * QB]

    q_spec, kv_spec = _chunk_specs(t)
    blk = pl.BlockSpec((QB, CH_COLS), lambda hp, i: (i, hp))
    bspec = pl.BlockSpec((CH_STEP_HEADS, QB, CH_KEYS), lambda hp, i: (hp, 0, 0))
    nblk = nq + CH_WIN - 1
    acc_t = pl.BlockSpec((None, nblk, CH_COLS, QB), lambda hp, i: (hp, 0, 0, 0))
    acc_sds = _sds((W_CH // CH_COLS, nblk, CH_COLS, QB), F32)
    return pl.pallas_call(
        body, name=name, grid=(W_CH // CH_COLS, nq),
        in_specs=[q_spec, kv_spec, kv_spec, bspec, blk,
                  pl.BlockSpec((CH_COLS, QB), lambda hp, i: (3 * W_SB // CH_COLS + hp, i)),
                  pl.BlockSpec((CH_COLS, QB), lambda hp, i: (hp, i))],
        out_specs=[blk, acc_t, acc_t, bspec],
        out_shape=[_sds((t, W_CH), BF16), acc_sds, acc_sds, _sds((N_HEADS_CH, QB, CH_KEYS), F32)],
        compiler_params=_cparams(2))(qkv, kp, vp, bias, do, qkv_t, do_t)


def _sum_parts(p_ref):
    total = p_ref[0].astype(F32)
    for p in range(1, p_ref.shape[0]):
        total = total + p_ref[p].astype(F32)
    return total


def sum_parts(parts, grid, p_spec, o_spec, out_sds, name):
    def body(p_ref, o_ref):
        o_ref[...] = _sum_parts(p_ref)

    return pl.pallas_call(body, name=name, grid=grid, in_specs=[p_spec], out_specs=o_spec,
                          out_shape=out_sds, compiler_params=_cparams(len(grid)))(parts)


def adamw(parts, w, m, v, grid, p_specs, w_spec, name):
    c1 = 1.0 / (1.0 - ADAM_B1 ** ADAM_STEP)
    c2 = 1.0 / (1.0 - ADAM_B2 ** ADAM_STEP)
    n = len(parts)

    def body(*refs):
        w_ref, m_ref, v_ref, g_out, d_out, m_out, v_out = refs[n:]
        g = _sum_parts(refs[0])
        for q in range(1, n):
            g = jnp.where(pl.program_id(0) == q, _sum_parts(refs[q]), g)
        m_new = ADAM_B1 * m_ref[...] + (1.0 - ADAM_B1) * g
        v_new = ADAM_B2 * v_ref[...] + (1.0 - ADAM_B2) * (g * g)
        m_hat = m_new * c1
        v_hat = v_new * c2
        g_out[...] = g
        d_out[...] = -ADAM_LR * (m_hat / (jnp.sqrt(v_hat) + ADAM_EPS) + ADAM_WD * w_ref[...])
        m_out[...] = m_new
        v_out[...] = v_new

    out = _sds(w.shape, F32)
    return pl.pallas_call(
        body, name=name, grid=grid, in_specs=[*p_specs, w_spec, w_spec, w_spec],
        out_specs=[w_spec] * 4, out_shape=[out] * 4,
        compiler_params=_cparams(len(grid)))(*parts, w, m, v)


def _ffn_fwd(x, gain, wa, wb_after, s, tm, tag, on_event, deps=()):
    t = x.shape[0]
    hn = rmsnorm_fwd(x, gain, tm, f"rms_{tag}", deps)
    gu, act = ffn_in_swiglu(hn, wa, s, min(2 * tm, t), f"ffn_in_{tag}")
    on_event("act", act)
    wb = wb_after(act)
    y = ffn_out_residual(act, wb, x, s, min(2 * tm, t), f"ffn_out_{tag}")
    return y, (hn, gu, act), wb


def _ffn_bwd(dy, x, gain, saved, wa, wb, s, tm, tag, on_grads):
    t = x.shape[0]
    hn, gu, act = saved
    dgu = ffn_dact_swiglu(dy, wb, gu, s, min(2 * tm, t), f"ffn_dact_{tag}")
    dwb = matmul(TN, act, dy, _sds((4, FF_BLK, D_MODEL), BF16), (4, 1, 1),
                 pl.BlockSpec((None, t, FF_BLK), lambda i, j, k: (i, 0, 0)),
                 pl.BlockSpec((t, D_MODEL), lambda i, j, k: (0, 0)),
                 pl.BlockSpec((None, FF_BLK, D_MODEL), lambda i, j, k: (i, 0, 0)),
                 None, name=f"ffn_dwout_{tag}", alpha=0.5)
    dwa = matmul(TN, dgu, hn, _sds((8, FF_BLK, D_MODEL), BF16), (1, 8, 1),
                 pl.BlockSpec((None, None, t, FF_BLK), lambda i, j, k: (j % 4, j // 4, 0, 0)),
                 pl.BlockSpec((t, D_MODEL), lambda i, j, k: (0, 0)),
                 pl.BlockSpec((None, FF_BLK, D_MODEL), lambda i, j, k: (j, 0, 0)),
                 None, name=f"ffn_dwin_{tag}")
    deps = on_grads(dwa, dwb)
    return ffn_dh_norm_bwd(dgu, wa, s, x, gain, dy, tm, f"ffn_dh_{tag}", deps)


BR_ROWS = ((0, 1), (1, 2), (3, 1))

_Q_COLUMN_SCALE = np.ones((1, QKV_WIDTH), np.float32)
for _lo, _width in ((0, W_SB), (3 * W_SB, W_CH), (3 * (W_SB + W_CH), W_FOX)):
    _Q_COLUMN_SCALE[0, _lo:_lo + _width] = SCALE


def _mixer_fwd(x, gain, wqkv, wf, wgate, late_after, bq, bf, bg, bias, layer, tm, tag, on_event):
    t = x.shape[0]
    nt = t // tm
    hm = rmsnorm_fwd(x, gain, tm, f"rms_{tag}")
    a_full = pl.BlockSpec((tm, D_MODEL), lambda i, j, k: (i, 0))
    wide_out = pl.BlockSpec((tm, D_MODEL), lambda i, j, k: (i, j))
    wide_b = pl.BlockSpec((1, D_MODEL), lambda i, j, k: (0, j))
    qkv, qkv_t = matmul(NN, hm, wqkv, _sds((t, QKV_WIDTH), BF16), (nt, 3, 1), a_full,
                        pl.BlockSpec((None, D_MODEL, D_MODEL), lambda i, j, k: (layer, 0, j)), wide_out, None,
                        name=f"proj_qkv_{tag}", bias=bq, bias_spec=wide_b,
                        scale=jnp.asarray(_Q_COLUMN_SCALE), scale_spec=wide_b,
                        out_t_sds=_sds((QKV_WIDTH, t), BF16),
                        out_t_spec=pl.BlockSpec((D_MODEL, tm), lambda i, j, k: (j, i)))
    gates = matmul(NN, hm, wgate, _sds((t, 3 * D_MODEL), F32), (nt, 3, 1), a_full,
                   pl.BlockSpec((None, D_MODEL, D_MODEL), lambda i, j, k: (layer + 1, 0,j)), wide_out,
                   None, name=f"proj_gate_{tag}", bias=bg, bias_spec=wide_b)
    f = matmul(NN, hm, wf, _sds((t, LANES), F32), (nt, 1, 1), a_full,
               pl.BlockSpec((None, D_MODEL, LANES), lambda i, j, k: (layer, 0, 0)),
               pl.BlockSpec((tm, LANES), lambda i, j, k: (i, 0)), None,
               name=f"proj_f_{tag}", bias=bf, bias_spec=pl.BlockSpec((1, LANES), lambda i, j, k: (0, 0)))
    fcol, frow = forget_cumsum(f, f"fcum_{tag}")
    frow = _frow_to_groups(frow)
    on_event("qkv", qkv)
    o_sb, w_sb = sb_fwd(qkv, f"sb_fwd_{tag}")
    on_event("o_sb", o_sb)
    kp = jnp.pad(qkv[:, 10 * LANES:14 * LANES], ((CH_PAD, 0), (0, 0)))
    vp = jnp.pad(qkv[:, 14 * LANES:18 * LANES], ((CH_PAD, 0), (0, 0)))
    o_ch = chunk_fwd(qkv, kp, vp, bias, f"chunk_fwd_{tag}")
    o_fox, lse = fox_fwd(qkv, fcol, frow, f"fox_fwd_{tag}")
    wbr, wout = late_after(o_fox)
    ys = []
    for a, (o, (r0, nr)) in enumerate(zip((o_sb, o_ch, o_fox), BR_ROWS)):
        ys.append(matmul(
            NN, o, wbr, _sds((t, D_MODEL), F32), (nt, 1, nr),
            pl.BlockSpec((tm, 256), lambda i, j, k: (i, k)),
            pl.BlockSpec((None, 256, D_MODEL), functools.partial(lambda i, j, k, r0: (layer, r0 + k, 0), r0=r0)),
            a_full, (tm, D_MODEL), name=f"branch{a}_{tag}"))
    merged = merge_fwd(gates, ys[0], ys[1], ys[2], tm, f"merge_{tag}")
    x_new = matmul(NN, merged, wout, _sds((t, D_MODEL), F32), (nt, 1, 1), a_full,
                   pl.BlockSpec((None, D_MODEL, D_MODEL), lambda i, j, k: (layer, 0, 0)), a_full, None,
                   name=f"wout_{tag}", res=x, res_spec=a_full)
    saved = (hm, qkv, gates, f, fcol, frow, o_sb, o_ch, o_fox, lse, ys, merged, kp, vp, w_sb, qkv_t)
    return x_new, saved, wbr, wout


def _mixer_bwd(dy, x, gain, saved, wqkv, wf, wgate, wbr, wout, bias, layer, tm, tag, on_grads):
    t = x.shape[0]
    nt = t // tm
    hm, qkv, gates, f, fcol, frow, o_sb, o_ch, o_fox, lse, ys, merged, kp, vp, w_sb, qkv_t = saved
    a_full = pl.BlockSpec((tm, D_MODEL), lambda i, j, k: (i, 0))
    red_row = pl.BlockSpec((tm, D_MODEL), lambda i, j, k: (k, 0))
    sq = pl.BlockSpec((D_MODEL, D_MODEL), lambda i, j, k: (0, 0))
    dmerged = matmul(NT, dy, wout, _sds((t, D_MODEL), F32), (nt, 1, 1), a_full,
                     pl.BlockSpec((None, D_MODEL, D_MODEL), lambda i, j, k: (layer, 0, 0)), a_full, None,
                     name=f"dmerged_{tag}")
    all_t = pl.BlockSpec((t, D_MODEL), lambda i, j, k: (0, 0))
    dwout = matmul(TN, merged, dy, _sds((D_MODEL, D_MODEL), BF16), (1, 1, 1), all_t, all_t, sq,
                   None, name=f"dwout_{tag}")
    dgates, dys = merge_bwd(dmerged, gates, ys[0], ys[1], ys[2], tm // 2, f"merge_bwd_{tag}")
    dos, dos_t, dwbrs = [], [], []
    for a, (o, (r0, nr)) in enumerate(zip((o_sb, o_ch, o_fox), BR_ROWS)):
        do, do_t = matmul(
            NT, dys[a], wbr, _sds((t, nr * 256), BF16), (nt, nr, 1), a_full,
            pl.BlockSpec((None, 256, D_MODEL), functools.partial(lambda i, j, k, r0: (layer, r0 + j, 0), r0=r0)),
            pl.BlockSpec((tm, 256), lambda i, j, k: (i, j)), None, name=f"dbranch{a}_{tag}",
            out_t_sds=_sds((nr * 256, t), BF16), out_t_spec=pl.BlockSpec((256, tm), lambda i, j, k: (j, i)))
        dos.append(do)
        dos_t.append(do_t)
        dwbrs.append(matmul(
            TN, o, dys[a], _sds((nr * 256, D_MODEL), BF16), (nr, 1, 1),
            pl.BlockSpec((t, 256), lambda i, j, k: (0, i)), all_t,
            pl.BlockSpec((256, D_MODEL), lambda i, j, k: (i, 0)), None, name=f"dwbr{a}_{tag}"))
    dq_a, dk_a, dv_a = sb_bwd(qkv, qkv_t, w_sb, dos[0], dos_t[0], f"sb_bwd_{tag}")
    dk_a, dv_a = _keys_major(dk_a), _keys_major(dv_a)
    dq_b, dk_b, dv_b, dbias = chunk_bwd(qkv, qkv_t, kp, vp, bias, dos[1], dos_t[1], f"chunk_bwd_{tag}")
    dk_b, dv_b = [x[:, CH_WIN - 1:].transpose(1, 3, 0, 2).reshape(t, W_CH) for x in (dk_b, dv_b)]
    dq_c, dk_c, dv_c, dfrow = fox_bwd(qkv, qkv_t, fcol, frow, o_fox, lse, dos[2], dos_t[2], f"fox_bwd_{tag}")
    dk_c, dv_c = _keys_major(dk_c), _keys_major(dv_c)
    df = forget_cumsum_bwd(_frow_from_groups(dfrow), f, f"fcum_bwd_{tag}")
    dqkv = jnp.concatenate([p.astype(BF16) for p in
                            (dq_a, dk_a, dv_a, dq_b, dk_b, dv_b, dq_c, dk_c, dv_c)], axis=1)
    dtab = rel_bias_scatter(dbias, f"rel_scatter_{tag}")

    all_rows = pl.BlockSpec((t, D_MODEL), lambda i, j, k: (0, 0))
    wide_b = pl.BlockSpec((t, D_MODEL), lambda i, j, k: (0, j))
    wide_o = pl.BlockSpec((D_MODEL, D_MODEL), lambda i, j, k: (0, j))
    wide_cs = pl.BlockSpec((1, D_MODEL), lambda i, j, k: (0, j))
    dwqkv, dbq = matmul(TN, hm, dqkv, _sds((D_MODEL, QKV_WIDTH), BF16), (1, 3, 1), all_rows, wide_b,
                        wide_o, None, name=f"dwqkv_{tag}",
                        colsum_sds=_sds((1, QKV_WIDTH), F32), colsum_spec=wide_cs)
    dwgate, dbg = matmul(TN, hm, dgates, _sds((D_MODEL, 3 * D_MODEL), BF16), (1, 3, 1), all_rows,
                         wide_b, wide_o, None, name=f"dwgate_{tag}",
                         colsum_sds=_sds((1, 3 * D_MODEL), F32), colsum_spec=wide_cs)
    dwf, dbf = matmul(TN, hm, df, _sds((D_MODEL, LANES), BF16), (1, 1, 1), all_rows,
                      pl.BlockSpec((t, LANES), lambda i, j, k: (0, 0)),
                      pl.BlockSpec((D_MODEL, LANES), lambda i, j, k: (0, 0)), None,
                      name=f"dwf_{tag}", colsum_sds=_sds((1, LANES), F32),
                      colsum_spec=pl.BlockSpec((1, LANES), lambda i, j, k: (0, 0)))
    dwbr = jnp.concatenate(dwbrs, axis=0)
    deps = on_grads(dict(dwqkv=dwqkv, dwgate=dwgate, dwf=dwf, dwbr=dwbr, dwout=dwout))
    wide_a = pl.BlockSpec((tm, QKV_WIDTH), lambda i, j, k: (i, 0))
    dhm = matmul(NT, dqkv, wqkv, _sds((t, D_MODEL), F32), (nt, 1, 1), wide_a,
                 pl.BlockSpec((None, D_MODEL, QKV_WIDTH), lambda i, j, k: (layer, 0, 0)), a_full,
                 None, name=f"dhm_qkv_{tag}", deps=deps)
    dhm = matmul(NT, dgates, wgate, _sds((t, D_MODEL), F32), (nt, 1, 1), wide_a,
                 pl.BlockSpec((None, D_MODEL, QKV_WIDTH), lambda i, j, k: (layer + 1, 0, 0)), a_full,
                 None, name=f"dhm_gate_{tag}", res=dhm, res_spec=a_full)
    dhm = matmul(NT, df, wf, _sds((t, D_MODEL), F32), (nt, 1, 1),
                 pl.BlockSpec((tm, LANES), lambda i, j, k: (i, 0)),
                 pl.BlockSpec((None, D_MODEL, LANES), lambda i, j, k: (layer, 0, 0)), a_full, None,
                 name=f"dhm_f_{tag}", res=dhm, res_spec=a_full)
    dx, dgain = rmsnorm_bwd(x, gain, dhm, dy, tm, f"rms_bwd_{tag}")
    return dx, dict(dbq=dbq, dbg=dbg, dbf=dbf, dtab=dtab, dgain=dgain)


def _pack_small(pieces):
    flat = jnp.concatenate([p.reshape(-1).astype(F32) for p in pieces])
    flat = jnp.pad(flat, (0, SMALL_ROWS * LANES - flat.shape[0]))
    return flat.reshape(SMALL_ROWS, LANES)


def _unpack_small(packed, shapes):
    flat = packed.reshape(-1)
    out, pos = [], 0
    for shp in shapes:
        n = int(np.prod(shp))
        out.append(flat[pos:pos + n].reshape(shp))
        pos += n
    return out


def kernel(x, g_ffn1, w_ffn1_in, w_ffn1_out, g_mix, w_in, b_in, rel_bias, w_br_sb, w_br_ch, w_br_fox, w_out, g_ffn2, w_ffn2_in, w_ffn2_out, g_final, loss_target, m_g_ffn1, m_w_ffn1_in, m_w_ffn1_out, m_g_mix, m_w_in, m_b_in, m_rel_bias, m_w_br_sb, m_w_br_ch, m_w_br_fox, m_w_out, m_g_ffn2, m_w_ffn2_in, m_w_ffn2_out, m_g_final, v_g_ffn1, v_w_ffn1_in, v_w_ffn1_out, v_g_mix, v_w_in, v_b_in, v_rel_bias, v_w_br_sb, v_w_br_ch, v_w_br_fox, v_w_out, v_g_ffn2, v_w_ffn2_in, v_w_ffn2_out, v_g_final):
    t = x.shape[1]
    tm = min(512, t)
    xs = x[0]
    target = loss_target[0]
    f_lo, f_hi = QKV_WIDTH, QKV_WIDTH + N_HEADS_FOX

    def ffn_shards(w_in_, w_out_, l):
        return [w_in_[l:l + 1].astype(BF16), w_out_[l:l + 1].astype(BF16)]

    def mixer_shards(l):
        wl = w_in[l]
        return [jnp.stack([wl[:, :QKV_WIDTH], wl[:, f_hi:]]).astype(BF16),
                jnp.pad(wl[:, f_lo:f_hi], ((0, 0), (0, LANES - N_HEADS_FOX)))[None].astype(BF16),
                w_out[l:l + 1].astype(BF16),
                jnp.concatenate([w_br_sb[l], w_br_ch[l], w_br_fox[l]], axis=0)[None].astype(BF16)]

    gathers = {}
    gather_tokens = []

    def start_gather(shards, name):
        handle = gather_start(shards, name, deps=gather_tokens[-1:])
        gather_tokens.append(handle["token"])
        return handle

    def relay(handle, after):
        if "send2" not in handle:
            gather_relay(handle, after)

    relay_on = {("mix", 0, "qkv"): ("mix", 0, 1), ("mix", 0, "o_sb"): ("ffn2", 0, 0),
                ("ffn2", 0, "act"): ("ffn1", 1, 0), ("ffn1", 1, "act"): ("mix", 1, 0),
                ("mix", 1, "qkv"): ("ffn2", 1, 0)}

    def on_event(grp, l):
        def fire(event, array):
            target = relay_on.get((grp, l, event))
            if target is not None:
                relay(gathers[target[:2]][target[2]], array)
        return fire

    for l in range(DEPTH):
        for grp, shards in (("ffn1", ffn_shards(w_ffn1_in, w_ffn1_out, l)), ("mix", mixer_shards(l)),
                            ("ffn2", ffn_shards(w_ffn2_in, w_ffn2_out, l))):
            cut = len(shards) // 2
            if l == 0 and grp != "ffn2":
                gathers[(grp, l)] = (start_gather(shards[:cut], f"gather_{grp}_l{l}_a"),
                                     start_gather(shards[cut:], f"gather_{grp}_l{l}_b"))
            else:
                gathers[(grp, l)] = (start_gather(shards, f"gather_{grp}_l{l}"),)

    def gathered(key, after):
        hs = gathers[key]
        cut = hs[0]["n"]
        relay(hs[0], after)
        first = gather_finish(hs[0], after)
        if len(hs) == 1:
            return first[:cut // 2], lambda later: first[cut // 2:]

        def second(later):
            relay(hs[1], later)
            return gather_finish(hs[1], later)

        return first, second

    def ffn_weights(key, after):
        (wa_,), rest = gathered(key, after)
        return wa_, lambda later: rest(later)[0].reshape(1, 4, FF_BLK, D_MODEL)

    def mixer_weights(key, after):
        (wc_, wf_), rest = gathered(key, after)

        def late(later):
            wout_, wbr_ = rest(later)
            return (wbr_.transpose(0, 2, 1, 3).reshape(1, D_MODEL, D_MODEL), wout_.reshape(1, D_MODEL, D_MODEL))

        return wc_.reshape(2, D_MODEL, QKV_WIDTH), wf_.reshape(1, D_MODEL, LANES), late

    bq = b_in[:, None, :QKV_WIDTH]
    bf = jnp.pad(b_in[:, f_lo:f_hi], ((0, 0), (0, LANES - N_HEADS_FOX)))[:, None, :]
    bg = b_in[:, None, f_hi:]
    tab_t = jnp.pad(rel_bias.transpose(0, 2, 1), ((0, 0), (0, 0), (0, REL_PAD - N_REL)))

    h = xs
    saved = []
    weights = []
    for l in range(DEPTH):
        bias = rel_bias_build(tab_t[l], f"rel_build_l{l}").reshape(N_HEADS_CH, QB, CH_KEYS)
        x0 = h
        wa1, wb1_after = ffn_weights(("ffn1", l), x0)
        x1, s1, wb1 = _ffn_fwd(x0, g_ffn1[l:l + 1], wa1, wb1_after, 0, tm, f"ffn1_l{l}", on_event("ffn1", l),
                               deps=gather_tokens if l == 0 else ())
        wc, wf, late_after = mixer_weights(("mix", l), x1)
        x2, sm, wbr, wout = _mixer_fwd(x1, g_mix[l:l + 1], wc, wf, wc, late_after, bq[l], bf[l], bg[l],
                                       bias, 0, tm, f"mix_l{l}", on_event("mix", l))
        wa2, wb2_after = ffn_weights(("ffn2", l), x2)
        x3, s2, wb2 = _ffn_fwd(x2, g_ffn2[l:l + 1], wa2, wb2_after, 0, tm, f"ffn2_l{l}", on_event("ffn2", l))
        saved.append((x0, x1, x2, s1, sm, s2, bias))
        weights.append(((wa1, wb1), (wc, wf, wout, wbr), (wa2, wb2)))
        h = x3

    dx, dg_final, loss_blk = loss_head(h, g_final[None, :], target, tm, "loss_head")

    g_mix_l = [None] * DEPTH
    dgains = {}
    scatters = {}

    def scatter_ffn(key):
        def on_grads(dwa, dwb):
            scatters[key] = exchange_start(
                "scatter", [dwa[None], dwb.reshape(1, N_DEV, D_FF // N_DEV, D_MODEL)],
                f"scatter_{key[0]}_l{key[1]}")
            return (scatters[key]["token"],)
        return on_grads

    def scatter_mixer(key):
        def on_grads(gm):
            scatters[key] = exchange_start(
                "scatter",
                [gm["dwqkv"].reshape(1, N_DEV, LANES, QKV_WIDTH), gm["dwgate"].reshape(1, N_DEV, LANES, QKV_WIDTH),
                 gm["dwf"].reshape(1, N_DEV, LANES, LANES), gm["dwout"].reshape(1, N_DEV, LANES, D_MODEL),
                 gm["dwbr"].reshape(1, D_MODEL, N_DEV, LANES).transpose(0, 2, 1, 3)],
                f"scatter_{key[0]}_l{key[1]}")
            return (scatters[key]["token"],)
        return on_grads

    for l in reversed(range(DEPTH)):
        x0, x1, x2, s1, sm, s2, bias = saved[l]
        w1, (wc, wf, wout, wbr), w2 = weights[l]
        dx, dgains[("ffn2", l)] = _ffn_bwd(dx, x2, g_ffn2[l:l + 1], s2, *w2, 0, tm, f"ffn2_l{l}",
                                           scatter_ffn(("ffn2", l)))
        dx, g_mix_l[l] = _mixer_bwd(dx, x1, g_mix[l:l + 1], sm, wc, wf, wc, wbr, wout, bias, 0, tm,
                                    f"mix_l{l}", scatter_mixer(("mix", l)))
        dx, dgains[("ffn1", l)] = _ffn_bwd(dx, x0, g_ffn1[l:l + 1], s1, *w1, 0, tm, f"ffn1_l{l}",
                                           scatter_ffn(("ffn1", l)))

    small_shapes = []
    small_pieces = []
    small_w, small_m, small_v = [], [], []

    def add_small(piece, w, m, v):
        small_shapes.append(w.shape)
        small_pieces.append(piece)
        small_w.append(w); small_m.append(m); small_v.append(v)

    dg1 = jnp.concatenate([dgains[("ffn1", l)] for l in range(DEPTH)], axis=0)
    dgm = jnp.concatenate([g_mix_l[l]["dgain"] for l in range(DEPTH)], axis=0)
    dg2 = jnp.concatenate([dgains[("ffn2", l)] for l in range(DEPTH)], axis=0)
    db = jnp.stack([jnp.concatenate([g_mix_l[l]["dbq"][0], g_mix_l[l]["dbf"][0, :N_HEADS_FOX],
                                     g_mix_l[l]["dbg"][0]]) for l in range(DEPTH)])
    drel = jnp.stack([g_mix_l[l]["dtab"][:, :N_REL].T for l in range(DEPTH)])
    add_small(dg1, g_ffn1, m_g_ffn1, v_g_ffn1)
    add_small(dgm, g_mix, m_g_mix, v_g_mix)
    add_small(db, b_in, m_b_in, v_b_in)
    add_small(drel, rel_bias, m_rel_bias, v_rel_bias)
    add_small(dg2, g_ffn2, m_g_ffn2, v_g_ffn2)
    add_small(dg_final[0], g_final, m_g_final, v_g_final)
    loss_piece = loss_blk[0, 0:1]
    small_packed = _pack_small(small_pieces + [loss_piece])

    recv = {}
    last = ("ffn1", 0)
    for l in reversed(range(DEPTH)):
        for grp in ("ffn2", "mix", "ffn1"):
            if (grp, l) != last:
                recv[(grp, l)] = exchange_wait(scatters[(grp, l)], dx, f"scattered_{grp}_l{l}")

    def upd(parts, w, m, v, tr, name, rb0=0):
        _, r, c = w.shape
        nr = r // tr

        def p_spec(layer):
            pinned = (nr - 1) if layer == 0 else 0
            return pl.BlockSpec((N_DEV, None, tr, c),
                                lambda l, i: (0, 0, rb0 + jnp.where(l == layer, i, pinned), 0))

        return adamw(parts, w, m, v, (DEPTH, nr), [p_spec(0), p_spec(1)],
                     pl.BlockSpec((None, tr, c), lambda l, i: (l, i, 0)), name)

    def both(grp, k):
        return [recv[(grp, l)][k] for l in range(DEPTH)]

    out_rows = D_FF // N_DEV // 2
    def upd_transposed(parts, w, m, v, tr, name):
        tp = lambda a: jnp.transpose(a, (0, 2, 1))
        return [tp(o) for o in upd(parts, tp(w), tp(m), tp(v), tr, name)]

    in_rows = FF_BLK // 4
    r_ffn2_in = upd_transposed(both("ffn2", 0), w_ffn2_in, m_w_ffn2_in, v_w_ffn2_in, in_rows, "adamw_ffn2_in")
    r_ffn2_out = upd(both("ffn2", 1), w_ffn2_out, m_w_ffn2_out, v_w_ffn2_out, out_rows, "adamw_ffn2_out")
    r_out = upd(both("mix", 3), w_out, m_w_out, v_w_out, LANES, "adamw_w_out")
    r_br_sb = upd(both("mix", 4), w_br_sb, m_w_br_sb, v_w_br_sb, 256, "adamw_br_sb", rb0=0)
    r_br_ch = upd(both("mix", 4), w_br_ch, m_w_br_ch, v_w_br_ch, 256, "adamw_br_ch", rb0=1)
    r_br_fox = upd(both("mix", 4), w_br_fox, m_w_br_fox, v_w_br_fox, 256, "adamw_br_fox", rb0=3)

    def summed(parts, name):
        _, _, r, c = parts.shape
        return sum_parts(parts, (1,), pl.BlockSpec((N_DEV, None, r, c), lambda s: (0, 0, 0, 0)),
                         pl.BlockSpec((r, c), lambda s: (0, 0)), _sds((r, c), F32), name)

    g_w_in = jnp.stack([
        jnp.concatenate([summed(recv[("mix", l)][0], f"sum_wqkv_l{l}"),
                         summed(recv[("mix", l)][2], f"sum_wf_l{l}")[:, :N_HEADS_FOX],
                         summed(recv[("mix", l)][1], f"sum_wgate_l{l}")], axis=1) for l in range(DEPTH)])
    to_cols = lambda a: jnp.transpose(a, (2, 0, 1))
    n_cols = w_in.shape[2]
    col_blk = n_cols // 4
    win_spec = pl.BlockSpec((col_blk, DEPTH, LANES), lambda i: (i, 0, 0))
    r_in = adamw([to_cols(g_w_in)[None]], to_cols(w_in), to_cols(m_w_in), to_cols(v_w_in), (4,),
                 [pl.BlockSpec((1, col_blk, DEPTH, LANES), lambda i: (0, i, 0, 0))], win_spec, "adamw_w_in")
    r_in = [jnp.transpose(o, (1, 2, 0)) for o in r_in]

    recv[last] = exchange_wait(scatters[last], r_in[1], "scattered_ffn1_l0")
    r_ffn1_in = upd_transposed(both("ffn1", 0), w_ffn1_in, m_w_ffn1_in, v_w_ffn1_in, in_rows, "adamw_ffn1_in")
    r_ffn1_out = upd(both("ffn1", 1), w_ffn1_out, m_w_ffn1_out, v_w_ffn1_out, out_rows, "adamw_ffn1_out")

    small_sum = all_reduce_small(small_packed, "allreduce_small", deps=(r_ffn1_out[1],))
    n_small = sum(int(np.prod(s)) for s in small_shapes)
    loss = small_sum.reshape(-1)[n_small]
    sm_spec = pl.BlockSpec((SMALL_ROWS, LANES), lambda i: (0, 0))
    sm_out = adamw([small_sum[None]], _pack_small(small_w), _pack_small(small_m), _pack_small(small_v),
                   (1,), [pl.BlockSpec((1, SMALL_ROWS, LANES), lambda i: (0, 0, 0))], sm_spec, "adamw_small")
    sm_g, sm_d, sm_m, sm_v = [_unpack_small(o, small_shapes) for o in sm_out]

    def per_kind(k):
        small = (sm_g, sm_d, sm_m, sm_v)[k]
        return [small[0], r_ffn1_in[k], r_ffn1_out[k], small[1], r_in[k], small[2], small[3],
                r_br_sb[k], r_br_ch[k], r_br_fox[k], r_out[k], small[4], r_ffn2_in[k], r_ffn2_out[k],
                small[5]]

    return (loss, dx[None], *per_kind(0), *per_kind(1), *per_kind(2), *per_kind(3))
```

```python
import functools

import numpy as np
import jax
import jax.numpy as jnp
from jax import lax
from jax.experimental import pallas as pl
from jax.experimental.pallas import tpu as pltpu

F32 = jnp.float32
BF16 = jnp.bfloat16

N_DEV = 8
D_MODEL = 1024
DEPTH = 2
HEAD_DIM = 64
W_SB, W_CH, W_FOX = 256, 512, 256
QKV_WIDTH = 3 * (W_SB + W_CH + W_FOX)
N_HEADS_FOX = 4
N_HEADS_CH = 8
D_FF = 2816
FF_BLK = 2 * D_FF // N_DEV
CHUNK = 64
LEFT_CHUNKS = 8
MAX_REL = 128
N_REL = 2 * MAX_REL + 1
REL_PAD = 384
QB = 128
KB = 512
KSUB = KB // QB
CH_WIN = 5
CH_KEYS = CH_WIN * QB
RMS_EPS = 1e-6
NEG = -1e30
SCALE = HEAD_DIM ** -0.5
LANES = 128
VMEM_LIMIT = 56 * 1024 * 1024

ADAM_LR, ADAM_B1, ADAM_B2, ADAM_EPS, ADAM_WD, ADAM_STEP = 0.001, 0.9, 0.999, 1e-08, 0.01, 10

SMALL_ROWS = 192

MESH = pl.DeviceIdType.MESH
ANY = pl.BlockSpec(memory_space=pl.ANY)
HIGHEST = lax.Precision.HIGHEST

NN = (((1,), (0,)), ((), ()))
NT = (((1,), (1,)), ((), ()))
TN = (((0,), (0,)), ((), ()))


def _cparams(n_grid):
    return pltpu.CompilerParams(dimension_semantics=("arbitrary",) * n_grid,
                                vmem_limit_bytes=VMEM_LIMIT)


def _sds(shape, dtype):
    return jax.ShapeDtypeStruct(tuple(shape), dtype)


def _my_index():
    return 4 * lax.axis_index("x") + 2 * lax.axis_index("y") + lax.axis_index("c")


def _peer(mask):
    x, y, c = lax.axis_index("x"), lax.axis_index("y"), lax.axis_index("c")
    px = x ^ ((mask >> 2) & 1)
    py = y ^ ((mask >> 1) & 1)
    pc = c ^ (mask & 1)
    return (px, py, pc), 4 * px + 2 * py + pc


def all_gather(shard, name):
    s, r, c = shard.shape

    def body(in_ref, out_ref, send_sems, recv_sems, local_sem):
        me = _my_index()
        mine = pltpu.make_async_copy(in_ref, out_ref.at[:, me], local_sem)
        mine.start()
        sends = []
        for mask in range(1, N_DEV):
            peer, _ = _peer(mask)
            cp = pltpu.make_async_remote_copy(
                src_ref=in_ref, dst_ref=out_ref.at[:, me],
                send_sem=send_sems.at[mask - 1], recv_sem=recv_sems.at[mask - 1],
                device_id=peer, device_id_type=MESH)
            cp.start()
            sends.append(cp)
        for mask in range(1, N_DEV):
            peer, pidx = _peer(mask)
            pltpu.make_async_remote_copy(
                src_ref=in_ref, dst_ref=out_ref.at[:, pidx],
                send_sem=send_sems.at[mask - 1], recv_sem=recv_sems.at[mask - 1],
                device_id=peer, device_id_type=MESH).wait_recv()
        for cp in sends:
            cp.wait_send()
        mine.wait()

    return pl.pallas_call(
        body, name=name,
        out_shape=_sds((s, N_DEV, r, c), shard.dtype),
        in_specs=[ANY], out_specs=ANY,
        scratch_shapes=[pltpu.SemaphoreType.DMA((N_DEV - 1,)),
                        pltpu.SemaphoreType.DMA((N_DEV - 1,)),
                        pltpu.SemaphoreType.DMA],
    )(shard)


def all_to_all(parts, name):
    s, _, r, c = parts.shape

    def body(in_ref, out_ref, send_sems, recv_sems, local_sem):
        me = _my_index()
        mine = pltpu.make_async_copy(in_ref.at[:, me], out_ref.at[me], local_sem)
        mine.start()
        sends = []
        for mask in range(1, N_DEV):
            peer, pidx = _peer(mask)
            cp = pltpu.make_async_remote_copy(
                src_ref=in_ref.at[:, pidx], dst_ref=out_ref.at[me],
                send_sem=send_sems.at[mask - 1], recv_sem=recv_sems.at[mask - 1],
                device_id=peer, device_id_type=MESH)
            cp.start()
            sends.append(cp)
        for mask in range(1, N_DEV):
            peer, pidx = _peer(mask)
            pltpu.make_async_remote_copy(
                src_ref=in_ref.at[:, me], dst_ref=out_ref.at[pidx],
                send_sem=send_sems.at[mask - 1], recv_sem=recv_sems.at[mask - 1],
                device_id=peer, device_id_type=MESH).wait_recv()
        for cp in sends:
            cp.wait_send()
        mine.wait()

    return pl.pallas_call(
        body, name=name,
        out_shape=_sds((N_DEV, s, r, c), parts.dtype),
        in_specs=[ANY], out_specs=ANY,
        scratch_shapes=[pltpu.SemaphoreType.DMA((N_DEV - 1,)),
                        pltpu.SemaphoreType.DMA((N_DEV - 1,)),
                        pltpu.SemaphoreType.DMA],
    )(parts)


HBM_SPEC = pl.BlockSpec(memory_space=pltpu.HBM)
SEM_SPEC = pl.BlockSpec(memory_space=pltpu.SEMAPHORE)
EFFECT = pltpu.SideEffectType.DATAFLOW_SIDE_EFFECTING


def _exchange_refs(mode, in_ref, land_ref, me, pidx):
    if mode == "gather":
        return in_ref, land_ref.at[:, me], land_ref.at[:, pidx]
    return in_ref.at[:, pidx], land_ref.at[me], land_ref.at[pidx]


def _landing_shape(mode, a):
    if mode == "gather":
        s, r, c = a.shape
        return (s, N_DEV, r, c)
    s, _, r, c = a.shape
    return (N_DEV, s, r, c)


def _own_copy(mode, in_ref, land_ref, me, sem):
    if mode == "gather":
        return pltpu.make_async_copy(in_ref, land_ref.at[:, me], sem)
    return pltpu.make_async_copy(in_ref.at[:, me], land_ref.at[me], sem)


def exchange_start(mode, arrays, name, deps=()):
    n = len(arrays)
    lands0 = [lax.empty(_landing_shape(mode, a), a.dtype) for a in arrays]

    def body(*refs):
        in_refs, land_refs = refs[:n], refs[n:2 * n]
        outs_at = 2 * n + len(deps)
        send_sems, recv_sems, own_sems, token = refs[outs_at], refs[outs_at + 1], refs[outs_at + 2], refs[-1]
        mine = _my_index()
        for k in range(n):
            _own_copy(mode, in_refs[k], land_refs[k], mine, own_sems.at[k]).start()
            for mask in range(1, N_DEV):
                peer, pidx = _peer(mask)
                src, dst, _ = _exchange_refs(mode, in_refs[k], land_refs[k], mine, pidx)
                sem = k * (N_DEV - 1) + mask - 1
                pltpu.make_async_remote_copy(
                    src_ref=src, dst_ref=dst, send_sem=send_sems.at[sem], recv_sem=recv_sems.at[sem],
                    device_id=peer, device_id_type=MESH).start()
        token[...] = jnp.zeros_like(token)

    nsem = n * (N_DEV - 1)
    outs = pl.pallas_call(
        body, name=name,
        out_shape=(pltpu.SemaphoreType.DMA((nsem,)), pltpu.SemaphoreType.DMA((nsem,)),
                   pltpu.SemaphoreType.DMA((n,)),
                   *[pltpu.HBM(a.shape, a.dtype) for a in arrays],
                   *[pltpu.HBM(l.shape, l.dtype) for l in lands0], _sds((8, LANES), F32)),
        in_specs=[HBM_SPEC] * (2 * n) + [ANY] * len(deps),
        out_specs=(SEM_SPEC, SEM_SPEC, SEM_SPEC, *[HBM_SPEC] * (2 * n),
                   pl.BlockSpec(memory_space=pltpu.VMEM)),
        input_output_aliases={k: 3 + k for k in range(2 * n)},
        compiler_params=pltpu.CompilerParams(has_side_effects=EFFECT),
    )(*[pltpu.with_memory_space_constraint(a, pltpu.HBM) for a in arrays],
      *[pltpu.with_memory_space_constraint(l, pltpu.HBM) for l in lands0], *deps)
    return dict(mode=mode, n=n, send=outs[0], recv=outs[1], own=outs[2], ins=outs[3:3 + n],
                lands=outs[3 + n:3 + 2 * n], token=outs[-1])


def exchange_wait(handle, after, name):
    n, mode = handle["n"], handle["mode"]

    def body(*refs):
        in_refs, land_refs = refs[:n], refs[n:2 * n]
        send_sems, recv_sems, own_sems = refs[2 * n], refs[2 * n + 1], refs[2 * n + 2]
        mine = _my_index()
        for k in range(n):
            _own_copy(mode, in_refs[k], land_refs[k], mine, own_sems.at[k]).wait()
            for mask in range(1, N_DEV):
                peer, pidx = _peer(mask)
                src, _, here = _exchange_refs(mode, in_refs[k], land_refs[k], mine, pidx)
                sem = k * (N_DEV - 1) + mask - 1
                cp = pltpu.make_async_remote_copy(
                    src_ref=src, dst_ref=here, send_sem=send_sems.at[sem], recv_sem=recv_sems.at[sem],
                    device_id=peer, device_id_type=MESH)
                cp.wait_send()
                cp.wait_recv()

    thru = (*handle["ins"], *handle["lands"])
    outs = pl.pallas_call(
        body, name=name,
        out_shape=tuple(pltpu.HBM(a.shape, a.dtype) for a in thru),
        in_specs=[HBM_SPEC] * (2 * n) + [SEM_SPEC, SEM_SPEC, SEM_SPEC, ANY],
        out_specs=tuple([HBM_SPEC] * (2 * n)),
        input_output_aliases={k: k for k in range(2 * n)},
        compiler_params=pltpu.CompilerParams(has_side_effects=EFFECT),
    )(*thru, handle["send"], handle["recv"], handle["own"], after)
    return list(outs[n:])


FAR_MASKS = (2, 4, 6)
PHASE1_MASKS = (1,) + FAR_MASKS


def gather_start(arrays, name, deps=()):
    n = len(arrays)
    n1 = len(PHASE1_MASKS)
    lands0 = [lax.empty(_landing_shape("gather", a), a.dtype) for a in arrays]

    def body(*refs):
        in_refs, land_refs = refs[:n], refs[n:2 * n]
        outs_at = 2 * n + len(deps)
        send_sems, recv_sems, own_sems, token = refs[outs_at], refs[outs_at + 1], refs[outs_at + 2], refs[-1]
        mine = _my_index()
        for k in range(n):
            _own_copy("gather", in_refs[k], land_refs[k], mine, own_sems.at[k]).start()
            for j, mask in enumerate(PHASE1_MASKS):
                peer, _ = _peer(mask)
                pltpu.make_async_remote_copy(
                    src_ref=in_refs[k], dst_ref=land_refs[k].at[:, mine],
                    send_sem=send_sems.at[k * n1 + j], recv_sem=recv_sems.at[k * n1 + j],
                    device_id=peer, device_id_type=MESH).start()
        token[...] = jnp.zeros_like(token)

    outs = pl.pallas_call(
        body, name=name,
        out_shape=(pltpu.SemaphoreType.DMA((n * n1,)), pltpu.SemaphoreType.DMA((n * n1,)),
                   pltpu.SemaphoreType.DMA((n,)),
                   *[pltpu.HBM(a.shape, a.dtype) for a in arrays],
                   *[pltpu.HBM(l.shape, l.dtype) for l in lands0], _sds((8, LANES), F32)),
        in_specs=[HBM_SPEC] * (2 * n) + [ANY] * len(deps),
        out_specs=(SEM_SPEC, SEM_SPEC, SEM_SPEC, *[HBM_SPEC] * (2 * n),
                   pl.BlockSpec(memory_space=pltpu.VMEM)),
        input_output_aliases={k: 3 + k for k in range(2 * n)},
        compiler_params=pltpu.CompilerParams(has_side_effects=EFFECT),
    )(*[pltpu.with_memory_space_constraint(a, pltpu.HBM) for a in arrays],
      *[pltpu.with_memory_space_constraint(l, pltpu.HBM) for l in lands0], *deps)
    return dict(n=n, send=outs[0], recv=outs[1], own=outs[2], ins=outs[3:3 + n],
                lands=outs[3 + n:3 + 2 * n], token=outs[-1], name=name)


def gather_relay(handle, after):
    n = handle["n"]
    n1, n2 = len(PHASE1_MASKS), len(FAR_MASKS)

    def body(*refs):
        in_refs, land_refs = refs[:n], refs[n:2 * n]
        send1, recv1 = refs[2 * n], refs[2 * n + 1]
        send2, recv2, token = refs[2 * n + 3], refs[2 * n + 4], refs[-1]
        token[...] = jnp.zeros_like(token)
        sibling, _ = _peer(1)
        for k in range(n):
            for j, mask in enumerate(FAR_MASKS):
                peer, pidx = _peer(mask)
                landed = land_refs[k].at[:, pidx]
                pltpu.make_async_remote_copy(
                    src_ref=in_refs[k], dst_ref=landed, send_sem=send1.at[k * n1 + 1 + j],
                    recv_sem=recv1.at[k * n1 + 1 + j], device_id=peer, device_id_type=MESH).wait_recv()
                pltpu.make_async_remote_copy(
                    src_ref=landed, dst_ref=landed, send_sem=send2.at[k * n2 + j],
                    recv_sem=recv2.at[k * n2 + j], device_id=sibling, device_id_type=MESH).start()

    thru = (*handle["ins"], *handle["lands"])
    outs = pl.pallas_call(
        body, name=handle["name"] + "_relay",
        out_shape=(pltpu.SemaphoreType.DMA((n * n2,)), pltpu.SemaphoreType.DMA((n * n2,)),
                   *[pltpu.HBM(a.shape, a.dtype) for a in thru], _sds((8, LANES), F32)),
        in_specs=[HBM_SPEC] * (2 * n) + [SEM_SPEC, SEM_SPEC, ANY],
        out_specs=(SEM_SPEC, SEM_SPEC, *[HBM_SPEC] * (2 * n), pl.BlockSpec(memory_space=pltpu.VMEM)),
        input_output_aliases={k: 2 + k for k in range(2 * n)},
        compiler_params=pltpu.CompilerParams(has_side_effects=EFFECT),
    )(*thru, handle["send"], handle["recv"], after)
    handle.update(send2=outs[0], recv2=outs[1], ins=outs[2:2 + n], lands=outs[2 + n:2 + 2 * n],
                  relay_token=outs[-1])


def gather_finish(handle, after):
    n = handle["n"]
    n1, n2 = len(PHASE1_MASKS), len(FAR_MASKS)

    def body(*refs):
        in_refs, land_refs = refs[:n], refs[n:2 * n]
        send1, recv1, own_sems, send2, recv2 = refs[2 * n:2 * n + 5]
        mine = _my_index()
        sibling, sib_idx = _peer(1)
        for k in range(n):
            _own_copy("gather", in_refs[k], land_refs[k], mine, own_sems.at[k]).wait()
            for j, mask in enumerate(PHASE1_MASKS):
                peer, pidx = _peer(mask)
                cp = pltpu.make_async_remote_copy(
                    src_ref=in_refs[k], dst_ref=land_refs[k].at[:, pidx], send_sem=send1.at[k * n1 + j],
                    recv_sem=recv1.at[k * n1 + j], device_id=peer, device_id_type=MESH)
                cp.wait_send()
                if mask == 1:
                    cp.wait_recv()
            for j, mask in enumerate(FAR_MASKS):
                _, pidx = _peer(mask)
                _, far_of_sibling = _peer(mask ^ 1)
                cp = pltpu.make_async_remote_copy(
                    src_ref=land_refs[k].at[:, pidx], dst_ref=land_refs[k].at[:, far_of_sibling],
                    send_sem=send2.at[k * n2 + j], recv_sem=recv2.at[k * n2 + j],
                    device_id=sibling, device_id_type=MESH)
                cp.wait_send()
                cp.wait_recv()

    thru = (*handle["ins"], *handle["lands"])
    outs = pl.pallas_call(
        body, name=handle["name"] + "_finish",
        out_shape=tuple(pltpu.HBM(a.shape, a.dtype) for a in thru),
        in_specs=[HBM_SPEC] * (2 * n) + [SEM_SPEC] * 5 + [ANY],
        out_specs=tuple([HBM_SPEC] * (2 * n)),
        input_output_aliases={k: k for k in range(2 * n)},
        compiler_params=pltpu.CompilerParams(has_side_effects=EFFECT),
    )(*thru, handle["send"], handle["recv"], handle["own"], handle["send2"], handle["recv2"], after)
    return list(outs[n:])


def all_reduce_small(packed, name, deps=()):
    rows = packed.shape[0]
    nd = len(deps)

    def body(in_ref, *rest):
        out_ref, slots, send_sems, recv_sems = rest[nd:]
        me = _my_index()
        sends = []
        for mask in range(1, N_DEV):
            peer, _ = _peer(mask)
            cp = pltpu.make_async_remote_copy(
                src_ref=in_ref, dst_ref=slots.at[me],
                send_sem=send_sems.at[mask - 1], recv_sem=recv_sems.at[mask - 1],
                device_id=peer, device_id_type=MESH)
            cp.start()
            sends.append(cp)
        slots[me] = in_ref[...]
        for mask in range(1, N_DEV):
            peer, pidx = _peer(mask)
            pltpu.make_async_remote_copy(
                src_ref=in_ref, dst_ref=slots.at[pidx],
                send_sem=send_sems.at[mask - 1], recv_sem=recv_sems.at[mask - 1],
                device_id=peer, device_id_type=MESH).wait_recv()
        for cp in sends:
            cp.wait_send()
        total = slots[0]
        for p in range(1, N_DEV):
            total = total + slots[p]
        out_ref[...] = total

    return pl.pallas_call(
        body, name=name,
        out_shape=_sds((rows, LANES), F32),
        in_specs=[pl.BlockSpec(memory_space=pltpu.VMEM)] + [ANY] * nd,
        out_specs=pl.BlockSpec(memory_space=pltpu.VMEM),
        scratch_shapes=[pltpu.VMEM((N_DEV, rows, LANES), F32),
                        pltpu.SemaphoreType.DMA((N_DEV - 1,)),
                        pltpu.SemaphoreType.DMA((N_DEV - 1,))],
    )(packed, *deps)


def matmul(dims, a, b, out_sds, grid, a_spec, b_spec, o_spec, acc_shape, *, name, alpha=1.0,
           bias=None, bias_spec=None, scale=None, scale_spec=None, res=None, res_spec=None,
           colsum_sds=None, colsum_spec=None, out_t_sds=None, out_t_spec=None, deps=()):
    nk = grid[2]
    has_bias, has_scale, has_res = bias is not None, scale is not None, res is not None
    has_cs, has_t = colsum_sds is not None, out_t_sds is not None
    if has_cs:
        assert grid[0] == 1 and dims == TN

    def body(*refs):
        a_ref, b_ref = refs[0], refs[1]
        pos = 2
        bias_ref = scale_ref = res_ref = cs_ref = ot_ref = None
        if has_bias:
            bias_ref = refs[pos]; pos += 1
        if has_scale:
            scale_ref = refs[pos]; pos += 1
        if has_res:
            res_ref = refs[pos]; pos += 1
        pos += len(deps)
        o_ref = refs[pos]; pos += 1
        if has_cs:
            cs_ref = refs[pos]; pos += 1
        if has_t:
            ot_ref = refs[pos]; pos += 1
        k = pl.program_id(2)
        bval = b_ref[...]
        part = lax.dot_general(a_ref[...].astype(BF16), bval.astype(BF16), dims,
                               preferred_element_type=F32)

        def finish(total):
            r = total * alpha if alpha != 1.0 else total
            if has_bias:
                r = r + bias_ref[...]
            if has_scale:
                r = r * scale_ref[...]
            if has_res:
                r = r + res_ref[...].astype(F32)
            o_ref[...] = r.astype(o_ref.dtype)
            if has_t:
                ot_ref[...] = r.T.astype(ot_ref.dtype)

        if has_cs:
            csum = jnp.sum(bval.astype(F32), axis=0, keepdims=True)

            @pl.when(k == 0)
            def _():
                cs_ref[...] = csum

            @pl.when(k > 0)
            def _():
                cs_ref[...] += csum

        if nk == 1:
            finish(part)
        else:
            acc_ref = refs[pos]

            @pl.when(k == 0)
            def _():
                acc_ref[...] = part

            @pl.when(k > 0)
            def _():
                acc_ref[...] += part

            @pl.when(k == nk - 1)
            def _():
                finish(acc_ref[...])

    in_specs, args = [a_spec, b_spec], [a, b]
    if has_bias:
        in_specs.append(bias_spec); args.append(bias)
    if has_scale:
        in_specs.append(scale_spec); args.append(scale)
    if has_res:
        in_specs.append(res_spec); args.append(res)
    in_specs += [ANY] * len(deps)
    args += list(deps)
    out_shape, out_specs = [out_sds], [o_spec]
    if has_cs:
        out_shape.append(colsum_sds); out_specs.append(colsum_spec)
    if has_t:
        out_shape.append(out_t_sds); out_specs.append(out_t_spec)
    scratch = [] if nk == 1 else [pltpu.VMEM(acc_shape, F32)]
    outs = pl.pallas_call(
        body, name=name, grid=grid, in_specs=in_specs, out_specs=out_specs, out_shape=out_shape,
        scratch_shapes=scratch, compiler_params=_cparams(3))(*args)
    return outs if (has_cs or has_t) else outs[0]


def _sigmoid(z):
    return 1.0 / (1.0 + jnp.exp(-z))


def _log_sigmoid(z):
    return jnp.minimum(z, 0.0) - jnp.log(1.0 + jnp.exp(-jnp.abs(z)))


def rmsnorm_fwd(x, gain, tm, name, deps=()):
    t, d = x.shape

    def body(x_ref, g_ref, *rest):
        o_ref = rest[-1]
        xf = x_ref[...]
        r = lax.rsqrt(jnp.mean(xf * xf, axis=-1, keepdims=True) + RMS_EPS)
        o_ref[...] = (xf * r * g_ref[...]).astype(o_ref.dtype)

    return pl.pallas_call(
        body, name=name, grid=(t // tm,),
        in_specs=[pl.BlockSpec((tm, d), lambda i: (i, 0)), pl.BlockSpec((1, d), lambda i: (0, 0))]
        + [ANY] * len(deps),
        out_specs=pl.BlockSpec((tm, d), lambda i: (i, 0)),
        out_shape=_sds((t, d), BF16), compiler_params=_cparams(1))(x, gain, *deps)


def rmsnorm_bwd(x, gain, dh, dres, tm, name):
    t, d = x.shape

    def body(x_ref, g_ref, dh_ref, dres_ref, dx_ref, dg_ref):
        i = pl.program_id(0)
        xf = x_ref[...]
        r = lax.rsqrt(jnp.mean(xf * xf, axis=-1, keepdims=True) + RMS_EPS)
        xhat = xf * r
        dh_v = dh_ref[...]
        dxhat = dh_v * g_ref[...]
        dx = r * (dxhat - xhat * jnp.mean(dxhat * xhat, axis=-1, keepdims=True))
        dx_ref[...] = dres_ref[...] + dx
        dg = jnp.sum(dh_v * xhat, axis=0, keepdims=True)

        @pl.when(i == 0)
        def _():
            dg_ref[...] = dg

        @pl.when(i > 0)
        def _():
            dg_ref[...] += dg

    row = pl.BlockSpec((tm, d), lambda i: (i, 0))
    vec = pl.BlockSpec((1, d), lambda i: (0, 0))
    return pl.pallas_call(
        body, name=name, grid=(t // tm,), in_specs=[row, vec, row, row], out_specs=[row, vec],
        out_shape=[_sds((t, d), F32), _sds((1, d), F32)], compiler_params=_cparams(1))(x, gain, dh, dres)


def loss_head(x, gain, target, tm, name):
    t, d = x.shape

    def body(x_ref, g_ref, tgt_ref, dx_ref, dg_ref, loss_ref):
        i = pl.program_id(0)
        xf = x_ref[...]
        g = g_ref[...]
        r = lax.rsqrt(jnp.mean(xf * xf, axis=-1, keepdims=True) + RMS_EPS)
        xhat = xf * r
        err = xhat * g - tgt_ref[...]
        part = 0.5 * jnp.sum(jnp.mean(err * err, axis=-1, keepdims=True))
        dy = err * (1.0 / d)
        dxhat = dy * g
        dx_ref[...] = r * (dxhat - xhat * jnp.mean(dxhat * xhat, axis=-1, keepdims=True))
        dg = jnp.sum(dy * xhat, axis=0, keepdims=True)
        lpart = jnp.full((8, LANES), part, F32)

        @pl.when(i == 0)
        def _():
            dg_ref[...] = dg
            loss_ref[...] = lpart

        @pl.when(i > 0)
        def _():
            dg_ref[...] += dg
            loss_ref[...] += lpart

    row = pl.BlockSpec((tm, d), lambda i: (i, 0))
    vec = pl.BlockSpec((1, d), lambda i: (0, 0))
    return pl.pallas_call(
        body, name=name, grid=(t // tm,), in_specs=[row, vec, row],
        out_specs=[row, vec, pl.BlockSpec((8, LANES), lambda i: (0, 0))],
        out_shape=[_sds((t, d), F32), _sds((1, d), F32), _sds((8, LANES), F32)],
        compiler_params=_cparams(1))(x, gain, target)


def ffn_in_swiglu(hn, wa, s, tm, name):
    t = hn.shape[0]
    halves = 2 if tm % 512 == 0 else 1
    rows = tm // halves

    def body(h_ref, wg_ref, wu_ref, gu_ref, act_ref):
        for c in range(halves):
            rs = slice(c * rows, (c + 1) * rows)
            h = h_ref[rs, :]
            g = jnp.dot(h, wg_ref[...], preferred_element_type=F32)
            u = jnp.dot(h, wu_ref[...], preferred_element_type=F32)
            gu_ref[0, rs, :] = g.astype(gu_ref.dtype)
            gu_ref[1, rs, :] = u.astype(gu_ref.dtype)
            act_ref[rs, :] = (g * _sigmoid(g) * u).astype(act_ref.dtype)

    return pl.pallas_call(
        body, name=name, grid=(t // tm, 4),
        in_specs=[pl.BlockSpec((tm, D_MODEL), lambda i, j: (i, 0)),
                  pl.BlockSpec((None, None, D_MODEL, FF_BLK), lambda i, j: (s, j, 0, 0)),
                  pl.BlockSpec((None, None, D_MODEL, FF_BLK), lambda i, j: (s, j + 4, 0, 0))],
        out_specs=[pl.BlockSpec((None, 2, tm, FF_BLK), lambda i, j: (j, 0, i, 0)),
                   pl.BlockSpec((None, tm, FF_BLK), lambda i, j: (j, i, 0))],
        out_shape=[_sds((4, 2, t, FF_BLK), BF16), _sds((4, t, FF_BLK), BF16)],
        compiler_params=_cparams(2))(hn, wa, wa)


def ffn_dact_swiglu(dy, wb, gu, s, tm, name):
    t = dy.shape[0]

    def body(dy_ref, w_ref, gu_ref, o_ref):
        da = 0.5 * lax.dot_general(dy_ref[...].astype(BF16), w_ref[...], NT, preferred_element_type=F32)
        g = gu_ref[0].astype(F32)
        u = gu_ref[1].astype(F32)
        sg = _sigmoid(g)
        o_ref[0] = (da * u * (sg * (1.0 + g * (1.0 - sg)))).astype(o_ref.dtype)
        o_ref[1] = (da * g * sg).astype(o_ref.dtype)

    blk = pl.BlockSpec((None, 2, tm, FF_BLK), lambda i, j: (j, 0, i, 0))
    return pl.pallas_call(
        body, name=name, grid=(t // tm, 4),
        in_specs=[pl.BlockSpec((tm, D_MODEL), lambda i, j: (i, 0)),
                  pl.BlockSpec((None, None, FF_BLK, D_MODEL), lambda i, j: (s, j, 0, 0)), blk],
        out_specs=blk, out_shape=_sds((4, 2, t, FF_BLK), BF16),
        compiler_params=_cparams(2))(dy, wb, gu)


def ffn_out_residual(act, wb, x, s, tm, name, deps=()):
    t = x.shape[0]

    def body(a_ref, w_ref, x_ref, *rest):
        o_ref = rest[-1]
        acc = jnp.dot(a_ref[0], w_ref[0], preferred_element_type=F32)
        for k in range(1, 4):
            acc = acc + jnp.dot(a_ref[k], w_ref[k], preferred_element_type=F32)
        o_ref[...] = x_ref[...] + 0.5 * acc

    row = pl.BlockSpec((tm, D_MODEL), lambda i: (i, 0))
    return pl.pallas_call(
        body, name=name, grid=(t // tm,),
        in_specs=[pl.BlockSpec((4, tm, FF_BLK), lambda i: (0, i, 0)),
                  pl.BlockSpec((None, 4, FF_BLK, D_MODEL), lambda i: (s, 0, 0, 0)), row] + [ANY] * len(deps),
        out_specs=row, out_shape=_sds((t, D_MODEL), F32), compiler_params=_cparams(1))(act, wb, x, *deps)


def ffn_dh_norm_bwd(dgu, wa, s, x, gain, dres, tm, name, deps):
    t = dgu.shape[2]
    nd = len(deps)

    def body(g_ref, w_ref, x_ref, gain_ref, dres_ref, *rest):
        dx_ref, dg_ref = rest[nd:]
        i = pl.program_id(0)
        dh = lax.dot_general(g_ref[0, 0], w_ref[0], NT, preferred_element_type=F32)
        for p in range(1, N_DEV):
            dh = dh + lax.dot_general(g_ref[p % 4, p // 4], w_ref[p], NT, preferred_element_type=F32)
        xf = x_ref[...]
        r = lax.rsqrt(jnp.mean(xf * xf, axis=-1, keepdims=True) + RMS_EPS)
        xhat = xf * r
        dxhat = dh * gain_ref[...]
        dx_ref[...] = dres_ref[...] + r * (dxhat - xhat * jnp.mean(dxhat * xhat, axis=-1, keepdims=True))
        dg = jnp.sum(dh * xhat, axis=0, keepdims=True)

        @pl.when(i == 0)
        def _():
            dg_ref[...] = dg

        @pl.when(i > 0)
        def _():
            dg_ref[...] += dg

    row = pl.BlockSpec((tm, D_MODEL), lambda i: (i, 0))
    vec = pl.BlockSpec((1, D_MODEL), lambda i: (0, 0))
    return pl.pallas_call(
        body, name=name, grid=(t // tm,),
        in_specs=[pl.BlockSpec((4, 2, tm, FF_BLK), lambda i: (0, 0, i, 0)),
                  pl.BlockSpec((None, N_DEV, D_MODEL, FF_BLK), lambda i: (s, 0, 0, 0)), row, vec, row]
        + [ANY] * nd,
        out_specs=[row, vec], out_shape=[_sds((t, D_MODEL), F32), _sds((1, D_MODEL), F32)],
        compiler_params=_cparams(1))(dgu, wa, x, gain, dres, *deps)


def merge_fwd(gates, ya, yb, yc, tm, name):
    t, d = ya.shape

    def body(ga_ref, gb_ref, gc_ref, ya_ref, yb_ref, yc_ref, o_ref):
        m = (_sigmoid(ga_ref[...]) * ya_ref[...] + _sigmoid(gb_ref[...]) * yb_ref[...]
             + _sigmoid(gc_ref[...]) * yc_ref[...])
        o_ref[...] = m.astype(o_ref.dtype)

    row = pl.BlockSpec((tm, d), lambda i: (i, 0))
    gspecs = [pl.BlockSpec((tm, d), functools.partial(lambda i, a: (i, a), a=a)) for a in range(3)]
    return pl.pallas_call(
        body, name=name, grid=(t // tm,), in_specs=gspecs + [row, row, row], out_specs=row,
        out_shape=_sds((t, d), BF16), compiler_params=_cparams(1))(gates, gates, gates, ya, yb, yc)


def merge_bwd(dm, gates, ya, yb, yc, tm, name):
    t, d = ya.shape

    def body(dm_ref, g_ref, ya_ref, yb_ref, yc_ref, dg_ref, dya_ref, dyb_ref, dyc_ref):
        dmv = dm_ref[...]
        for a, (y_ref, dy_ref) in enumerate(((ya_ref, dya_ref), (yb_ref, dyb_ref), (yc_ref, dyc_ref))):
            cols = slice(a * d, (a + 1) * d)
            s = _sigmoid(g_ref[:, cols])
            dy_ref[...] = (dmv * s).astype(dy_ref.dtype)
            dg_ref[:, cols] = (dmv * y_ref[...] * s * (1.0 - s)).astype(dg_ref.dtype)

    row = pl.BlockSpec((tm, d), lambda i: (i, 0))
    wide = pl.BlockSpec((tm, 3 * d), lambda i: (i, 0))
    dg, dya, dyb, dyc = pl.pallas_call(
        body, name=name, grid=(t // tm,), in_specs=[row, wide, row, row, row],
        out_specs=[wide, row, row, row],
        out_shape=[_sds((t, 3 * d), BF16)] + [_sds((t, d), BF16)] * 3,
        compiler_params=_cparams(1))(dm, gates, ya, yb, yc)
    return dg, [dya, dyb, dyc]


def _iota2(shape, dim):
    return lax.broadcasted_iota(jnp.int32, shape, dim)


def forget_cumsum(f, name):
    t = f.shape[0]
    nq = t // QB

    def body(f_ref, fcol_ref, frow_ref, carry):
        j = pl.program_id(0)

        @pl.when(j == 0)
        def _():
            carry[...] = jnp.zeros_like(carry)

        logf = _log_sigmoid(f_ref[...])
        tri = (_iota2((QB, QB), 1) <= _iota2((QB, QB), 0)).astype(F32)
        blk = jnp.dot(tri, logf, precision=HIGHEST, preferred_element_type=F32) + carry[...]
        carry[...] += jnp.sum(logf, axis=0, keepdims=True)
        fcol_ref[...] = blk
        frow_ref[...] = blk.T[0:8, :]

    return pl.pallas_call(
        body, name=name, grid=(nq,),
        in_specs=[pl.BlockSpec((QB, LANES), lambda j: (j, 0))],
        out_specs=[pl.BlockSpec((QB, LANES), lambda j: (j, 0)),
                   pl.BlockSpec((None, 8, QB), lambda j: (j, 0, 0))],
        out_shape=[_sds((t, LANES), F32), _sds((nq, 8, QB), F32)],
        scratch_shapes=[pltpu.VMEM((1, LANES), F32)], compiler_params=_cparams(1))(f)


def forget_cumsum_bwd(dfrow, f, name):
    t = f.shape[0]
    nq = t // QB

    def body(dfr_ref, f_ref, df_ref, carry):
        jj = pl.program_id(0)

        @pl.when(jj == 0)
        def _():
            carry[...] = jnp.zeros_like(carry)

        padded = jnp.concatenate([dfr_ref[...], jnp.zeros((QB - 8, QB), F32)], axis=0)
        dfcol = padded.T
        tri = (_iota2((QB, QB), 1) >= _iota2((QB, QB), 0)).astype(F32)
        dlogf = jnp.dot(tri, dfcol, precision=HIGHEST, preferred_element_type=F32) + carry[...]
        carry[...] += jnp.sum(dfcol, axis=0, keepdims=True)
        df_ref[...] = dlogf * _sigmoid(-f_ref[...])

    return pl.pallas_call(
        body, name=name, grid=(nq,),
        in_specs=[pl.BlockSpec((None, 8, QB), lambda jj: (nq - 1 - jj, 0, 0)),
                  pl.BlockSpec((QB, LANES), lambda jj: (nq - 1 - jj, 0))],
        out_specs=pl.BlockSpec((QB, LANES), lambda jj: (nq - 1 - jj, 0)),
        out_shape=_sds((t, LANES), F32),
        scratch_shapes=[pltpu.VMEM((1, LANES), F32)], compiler_params=_cparams(1))(dfrow, f)


REL_DIAG = 768
REL_SHIFT = REL_DIAG - (QB - 1)


def _diag_onehot():
    u = _iota2((REL_PAD, REL_DIAG), 1)
    rel = jnp.clip(CH_KEYS - 1 - u, -MAX_REL, MAX_REL) + MAX_REL
    return (_iota2((REL_PAD, REL_DIAG), 0) == rel).astype(F32)


def rel_bias_build(tab_t, name):
    def body(tab_ref, o_ref):
        diag = jnp.dot(tab_ref[...], _diag_onehot(), precision=HIGHEST, preferred_element_type=F32)
        band = _chunk_band()
        for h in range(N_HEADS_CH):
            rows = jnp.broadcast_to(diag[h:h + 1, :], (QB, REL_DIAG))
            o_ref[h] = pltpu.roll(rows, REL_SHIFT, 1, stride=1, stride_axis=0)[:, :CH_KEYS] + band

    return pl.pallas_call(
        body, name=name, out_shape=_sds((N_HEADS_CH, QB, CH_KEYS), F32),
        in_specs=[pl.BlockSpec(memory_space=pltpu.VMEM)], out_specs=pl.BlockSpec(memory_space=pltpu.VMEM),
    )(tab_t)


def rel_bias_scatter(dbias, name):
    def body(db_ref, o_ref, ddiag):
        flip = (_iota2((QB, QB), 0) + _iota2((QB, QB), 1) == QB - 1).astype(F32)
        for h in range(N_HEADS_CH):
            padded = jnp.concatenate([db_ref[h], jnp.zeros((QB, REL_DIAG - CH_KEYS), F32)], axis=1)
            flipped = jnp.dot(flip, padded, precision=HIGHEST, preferred_element_type=F32)
            unrolled = pltpu.roll(flipped, 0, 1, stride=1, stride_axis=0)
            ddiag[h:h + 1, :] = jnp.sum(unrolled, axis=0, keepdims=True)
        o_ref[...] = lax.dot_general(ddiag[...], _diag_onehot(), NT, precision=HIGHEST,
                                     preferred_element_type=F32)

    return pl.pallas_call(
        body, name=name, out_shape=_sds((N_HEADS_CH, REL_PAD), F32),
        in_specs=[pl.BlockSpec(memory_space=pltpu.VMEM)], out_specs=pl.BlockSpec(memory_space=pltpu.VMEM),
        scratch_shapes=[pltpu.VMEM((N_HEADS_CH, REL_DIAG), F32)],
    )(dbias)


def _hl(h):
    return slice(h * HEAD_DIM, (h + 1) * HEAD_DIM)


def _split_dot(x, tri_bf16):
    hi = x.astype(BF16)
    lo = (x - hi.astype(F32)).astype(BF16)
    return (jnp.dot(hi, tri_bf16, preferred_element_type=F32)
            + jnp.dot(lo, tri_bf16, preferred_element_type=F32))


def _rows(j):
    return pl.ds(pl.multiple_of(j * QB, QB), QB)


def _krows(g):
    return pl.ds(pl.multiple_of(g * KB, KB), KB)


def _log_sigmoid_pair(z):
    sp = jnp.log(1.0 + jnp.exp(-jnp.abs(z)))
    return jnp.minimum(z, 0.0) - sp, -jnp.maximum(z, 0.0) - sp


def _qkv_specs(t, col0, n_pairs):
    q_spec = pl.BlockSpec((QB, LANES), lambda hp, i: (i, col0 + hp))
    k_spec = pl.BlockSpec((t, LANES), lambda hp, i: (0, col0 + n_pairs + hp))
    v_spec = pl.BlockSpec((t, LANES), lambda hp, i: (0, col0 + 2 * n_pairs + hp))
    return q_spec, k_spec, v_spec


def _keys_major(xt):
    pairs, groups, _, _ = xt.shape
    return xt.transpose(1, 3, 0, 2).reshape(groups * KB, pairs * LANES)


def sb_fwd(qkv, name):
    t = qkv.shape[0]
    nq = t // QB

    def body(q_ref, k_ref, v_ref, o_ref, w_ref):
        i = pl.program_id(1)
        groups = i // KSUB + 1
        tri_after = (_iota2((KB, KB), 0) > _iota2((KB, KB), 1)).astype(BF16)
        t_idx = i * QB + _iota2((QB, KB), 0)
        qs = [q_ref[:, _hl(h)] for h in range(2)]

        def step(g, carry, masked):
            strict = (g * KB + _iota2((QB, KB), 1)) < t_idx
            out = []
            for h in range(2):
                tail, acc = carry[2 * h], carry[2 * h + 1]
                k = k_ref[_krows(g), _hl(h)]
                v = v_ref[_krows(g), _hl(h)]
                z = lax.dot_general(qs[h], k, NT, preferred_element_type=F32)
                lb, lf = _log_sigmoid_pair(z)
                if masked:
                    lf = jnp.where(strict, lf, 0.0)
                between = _split_dot(lf, tri_after) + tail
                w = jnp.exp(lb + between)
                if masked:
                    w = jnp.where(strict, w, 0.0)
                w = w.astype(BF16)
                w_ref[h, g] = w
                acc = acc + jnp.dot(w, v, preferred_element_type=F32)
                out += [tail + jnp.sum(lf, axis=1, keepdims=True), acc]
            return tuple(out)

        init = (jnp.zeros((QB, 1), F32), jnp.zeros((QB, HEAD_DIM), F32)) * 2
        res = step(groups - 1, init, True)
        res = lax.fori_loop(0, groups - 1, lambda gg, c: step(groups - 2 - gg, c, False), res)
        for h in range(2):
            o_ref[:, _hl(h)] = res[2 * h + 1].astype(o_ref.dtype)

    q_spec, k_spec, v_spec = _qkv_specs(t, 0, 2)
    return pl.pallas_call(
        body, name=name, grid=(2, nq), in_specs=[q_spec, k_spec, v_spec],
        out_specs=[pl.BlockSpec((QB, LANES), lambda hp, i: (i, hp)),
                   pl.BlockSpec((2, None, t // KB, QB, KB), lambda hp, i: (hp, i, 0, 0, 0))],
        out_shape=[_sds((t, W_SB), BF16), _sds((4, nq, t // KB, QB, KB), BF16)],
        compiler_params=_cparams(2))(qkv, qkv, qkv)


def _hs(h):
    return slice(h * HEAD_DIM, (h + 1) * HEAD_DIM)


def sb_bwd(qkv, qkv_t, w, do, do_t, name):
    t = qkv.shape[0]
    nq = t // QB

    def body(q_ref, k_ref, v_ref, do_ref, qt_ref, dot_ref, w_ref, dq_ref, dkt_ref, dvt_ref):
        i = pl.program_id(1)

        @pl.when(i == 0)
        def _():
            dkt_ref[...] = jnp.zeros_like(dkt_ref)
            dvt_ref[...] = jnp.zeros_like(dvt_ref)

        groups = i // KSUB + 1
        tri_before = (_iota2((KB, KB), 0) < _iota2((KB, KB), 1)).astype(BF16)
        t_idx = i * QB + _iota2((QB, KB), 0)
        qs = [q_ref[:, _hl(h)] for h in range(2)]
        dos = [do_ref[:, _hl(h)] for h in range(2)]
        qts = [qt_ref[_hs(h), :] for h in range(2)]
        dots = [dot_ref[_hs(h), :] for h in range(2)]

        def grads(g, carry, masked):
            strict = (g * KB + _iota2((QB, KB), 1)) < t_idx
            out = []
            for h in range(2):
                head, dq = carry[2 * h], carry[2 * h + 1]
                k = k_ref[_krows(g), _hl(h)]
                v = v_ref[_krows(g), _hl(h)]
                wb = w_ref[h, g]
                z = lax.dot_general(qs[h], k, NT, preferred_element_type=F32)
                beta = _sigmoid(z)
                e = lax.dot_general(dos[h], v, NT, preferred_element_type=F32) * wb.astype(F32)
                before = _split_dot(e, tri_before) + head
                dz = e * (1.0 - beta) - before * beta
                if masked:
                    dz = jnp.where(strict, dz, 0.0)
                dzb = dz.astype(BF16)
                dq = dq + jnp.dot(dzb, k, preferred_element_type=F32)
                dkt_ref[g, _hs(h), :] += jnp.dot(qts[h], dzb, preferred_element_type=F32)
                dvt_ref[g, _hs(h), :] += jnp.dot(dots[h], wb, preferred_element_type=F32)
                out += [head + jnp.sum(e, axis=1, keepdims=True), dq]
            return tuple(out)

        init = (jnp.zeros((QB, 1), F32), jnp.zeros((QB, HEAD_DIM), F32)) * 2
        res = lax.fori_loop(0, groups - 1, lambda g, c: grads(g, c, False), init)
        res = grads(groups - 1, res, True)
        for h in range(2):
            dq_ref[:, _hl(h)] = (res[2 * h + 1] * SCALE).astype(dq_ref.dtype)

    q_spec, k_spec, v_spec = _qkv_specs(t, 0, 2)
    blk = pl.BlockSpec((QB, LANES), lambda hp, i: (i, hp))
    blk_t = pl.BlockSpec((LANES, QB), lambda hp, i: (hp, i))
    acc_t = pl.BlockSpec((None, t // KB, LANES, KB), lambda hp, i: (hp, 0, 0, 0))
    acc_sds = _sds((2, t // KB, LANES, KB), F32)
    return pl.pallas_call(
        body, name=name, grid=(2, nq),
        in_specs=[q_spec, k_spec, v_spec, blk, blk_t, blk_t,
                  pl.BlockSpec((2, None, t // KB, QB, KB), lambda hp, i: (hp, i, 0, 0, 0))],
        out_specs=[blk, acc_t, acc_t],
        out_shape=[_sds((t, W_SB), BF16), acc_sds, acc_sds],
        compiler_params=_cparams(2))(qkv, qkv, qkv, do, qkv_t, do_t, w)


def fox_fwd(qkv, fcol, frow, name):
    t = qkv.shape[0]
    nq = t // QB

    def body(q_ref, k_ref, v_ref, fc_ref, fr_ref, o_ref, lse_ref):
        hp = pl.program_id(0)
        i = pl.program_id(1)
        groups = i // KSUB + 1
        t_idx = i * QB + _iota2((QB, KB), 0)
        lane = _iota2((QB, LANES), 1)
        sub = _iota2((8, KB), 0)
        qs = [q_ref[:, _hl(h)] for h in range(2)]
        f_qs = [jnp.sum(jnp.where(lane == hp * 2 + h, fc_ref[...], 0.0), axis=1, keepdims=True)
                for h in range(2)]

        def step(g, carry, masked):
            causal = (g * KB + _iota2((QB, KB), 1)) <= t_idx
            fr = fr_ref[g]
            out = []
            for h in range(2):
                m, l, acc = carry[3 * h:3 * h + 3]
                k = k_ref[_krows(g), _hl(h)]
                v = v_ref[_krows(g), _hl(h)]
                f_k = jnp.sum(jnp.where(sub == hp * 2 + h, fr, 0.0), axis=0, keepdims=True)
                z = lax.dot_general(qs[h], k, NT, preferred_element_type=F32) + f_qs[h] - f_k
                if masked:
                    z = jnp.where(causal, z, NEG)
                m_new = jnp.maximum(m, jnp.max(z, axis=1, keepdims=True))
                p = jnp.exp(z - m_new)
                corr = jnp.exp(m - m_new)
                l = l * corr + jnp.sum(p, axis=1, keepdims=True)
                acc = acc * corr + jnp.dot(p.astype(BF16), v, preferred_element_type=F32)
                out += [m_new, l, acc]
            return tuple(out)

        init = (jnp.full((QB, 1), NEG, F32), jnp.zeros((QB, 1), F32), jnp.zeros((QB, HEAD_DIM), F32)) * 2
        res = lax.fori_loop(0, groups - 1, lambda g, c: step(g, c, False), init)
        res = step(groups - 1, res, True)
        for h in range(2):
            m, l, acc = res[3 * h:3 * h + 3]
            o_ref[:, _hl(h)] = (acc / l).astype(o_ref.dtype)
            lse_ref[:, _hl(h)] = jnp.broadcast_to(m + jnp.log(l), (QB, HEAD_DIM))

    q_spec, k_spec, v_spec = _qkv_specs(t, 18, 2)
    blk = pl.BlockSpec((QB, LANES), lambda hp, i: (i, hp))
    return pl.pallas_call(
        body, name=name, grid=(2, nq),
        in_specs=[q_spec, k_spec, v_spec, pl.BlockSpec((QB, LANES), lambda hp, i: (i, 0)),
                  pl.BlockSpec((t // KB, 8, KB), lambda hp, i: (0, 0, 0))],
        out_specs=[blk, blk],
        out_shape=[_sds((t, W_FOX), BF16), _sds((t, W_FOX), F32)],
        compiler_params=_cparams(2))(qkv, qkv, qkv, fcol, frow)


def fox_bwd(qkv, qkv_t, fcol, frow, o, lse, do, do_t, name):
    t = qkv.shape[0]
    nq = t // QB

    def body(q_ref, k_ref, v_ref, fc_ref, fr_ref, o_ref, lse_ref, do_ref, qt_ref, dot_ref,
             dq_ref, dk_ref, dv_ref, dfr_ref):
        hp = pl.program_id(0)
        i = pl.program_id(1)
        qts = [qt_ref[_hs(h), :] for h in range(2)]
        dots = [dot_ref[_hs(h), :] for h in range(2)]

        @pl.when(i == 0)
        def _():
            dk_ref[...] = jnp.zeros_like(dk_ref)
            dv_ref[...] = jnp.zeros_like(dv_ref)

        @pl.when((i == 0) & (hp == 0))
        def _():
            dfr_ref[...] = jnp.zeros_like(dfr_ref)

        groups = i // KSUB + 1
        t_idx = i * QB + _iota2((QB, KB), 0)
        lane = _iota2((QB, LANES), 1)
        sub = _iota2((8, KB), 0)
        qs = [q_ref[:, _hl(h)] for h in range(2)]
        dos = [do_ref[:, _hl(h)] for h in range(2)]
        f_qs = [jnp.sum(jnp.where(lane == hp * 2 + h, fc_ref[...], 0.0), axis=1, keepdims=True)
                for h in range(2)]
        lse_qs = [lse_ref[:, h * HEAD_DIM:h * HEAD_DIM + 1] for h in range(2)]
        deltas = [jnp.sum(dos[h].astype(F32) * o_ref[:, _hl(h)].astype(F32), axis=1, keepdims=True)
                  for h in range(2)]

        def step(g, dqs, masked):
            causal = (g * KB + _iota2((QB, KB), 1)) <= t_idx
            fr = fr_ref[g]
            out = []
            dfr = jnp.zeros((8, KB), F32)
            for h in range(2):
                k = k_ref[_krows(g), _hl(h)]
                v = v_ref[_krows(g), _hl(h)]
                f_k = jnp.sum(jnp.where(sub == hp * 2 + h, fr, 0.0), axis=0, keepdims=True)
                z = lax.dot_general(qs[h], k, NT, preferred_element_type=F32) + f_qs[h] - f_k
                p = jnp.exp(z - lse_qs[h])
                if masked:
                    p = jnp.where(causal, p, 0.0)
                dp = lax.dot_general(dos[h], v, NT, preferred_element_type=F32)
                ds = p * (dp - deltas[h])
                dsb = ds.astype(BF16)
                out.append(dqs[h] + jnp.dot(dsb, k, preferred_element_type=F32))
                dk_ref[g, _hs(h), :] += jnp.dot(qts[h], dsb, preferred_element_type=F32)
                dv_ref[g, _hs(h), :] += jnp.dot(dots[h], p.astype(BF16), preferred_element_type=F32)
                colsum = jnp.sum(ds, axis=0, keepdims=True)
                dfr = dfr + jnp.where(sub == hp * 2 + h, -colsum, 0.0)
            dfr_ref[g] += dfr
            return tuple(out)

        res = lax.fori_loop(0, groups - 1, lambda g, c: step(g, c, False),
                            (jnp.zeros((QB, HEAD_DIM), F32),) * 2)
        res = step(groups - 1, res, True)
        for h in range(2):
            dq_ref[:, _hl(h)] = (res[h] * SCALE).astype(dq_ref.dtype)

    q_spec, k_spec, v_spec = _qkv_specs(t, 18, 2)
    blk = pl.BlockSpec((QB, LANES), lambda hp, i: (i, hp))
    frs = pl.BlockSpec((t // KB, 8, KB), lambda hp, i: (0, 0, 0))
    acc_t = pl.BlockSpec((None, t // KB, LANES, KB), lambda hp, i: (hp, 0, 0, 0))
    acc_sds = _sds((2, t // KB, LANES, KB), F32)
    return pl.pallas_call(
        body, name=name, grid=(2, nq),
        in_specs=[q_spec, k_spec, v_spec, pl.BlockSpec((QB, LANES), lambda hp, i: (i, 0)), frs,
                  blk, blk, blk, pl.BlockSpec((LANES, QB), lambda hp, i: (18 + hp, i)),
                  pl.BlockSpec((LANES, QB), lambda hp, i: (hp, i))],
        out_specs=[blk, acc_t, acc_t, frs],
        out_shape=[_sds((t, W_FOX), BF16), acc_sds, acc_sds, _sds((t // KB, 8, KB), F32)],
        compiler_params=_cparams(2))(qkv, qkv, qkv, fcol, frow, o, lse, do, qkv_t, do_t)


def _frow_to_groups(frow):
    n = frow.shape[0] // KSUB
    return frow.reshape(n, KSUB, 8, QB).transpose(0, 2, 1, 3).reshape(n, 8, KB)


def _frow_from_groups(frow):
    n = frow.shape[0]
    return frow.reshape(n, 8, KSUB, QB).transpose(0, 2, 1, 3).reshape(n * KSUB, 8, QB)


def _chunk_band():
    qi = _iota2((QB, CH_KEYS), 0)
    kj = _iota2((QB, CH_KEYS), 1)
    dchunk = (qi >> 6) + LEFT_CHUNKS - (kj >> 6)
    return jnp.where((dchunk >= 0) & (dchunk <= LEFT_CHUNKS), 0.0, NEG)


def _chunk_pad_row(i):
    kj = _iota2((1, CH_KEYS), 1)
    return jnp.where((i - (CH_WIN - 1)) * QB + kj >= 0, 0.0, NEG)


CH_PAD = (CH_WIN - 1) * QB
CH_STEP_HEADS = 4
CH_COLS = CH_STEP_HEADS * HEAD_DIM


def _window(i):
    return pl.ds(pl.multiple_of(i * QB, QB), CH_KEYS)


def _chunk_weights(q, kw, bias, pad_row):
    z = lax.dot_general(q, kw, NT, preferred_element_type=F32) + bias + pad_row
    e = jnp.exp(z - jnp.max(z, axis=1, keepdims=True))
    return e, 1.0 / jnp.sum(e, axis=1, keepdims=True)


def _chunk_specs(t):
    q_spec = pl.BlockSpec((QB, CH_COLS), lambda hp, i: (i, 3 * W_SB // CH_COLS + hp))
    kv_spec = pl.BlockSpec((t + CH_PAD, CH_COLS), lambda hp, i: (0, hp))
    return q_spec, kv_spec


def chunk_fwd(qkv, kp, vp, bias, name):
    t = qkv.shape[0]
    nq = t // QB

    def body(q_ref, k_ref, v_ref, b_ref, o_ref):
        i = pl.program_id(1)
        pad_row = _chunk_pad_row(i)
        for h in range(CH_STEP_HEADS):
            e, inv = _chunk_weights(q_ref[:, _hl(h)], k_ref[_window(i), _hl(h)], b_ref[h], pad_row)
            o = jnp.dot(e.astype(BF16), v_ref[_window(i), _hl(h)], preferred_element_type=F32)
            o_ref[:, _hl(h)] = (o * inv).astype(o_ref.dtype)

    q_spec, kv_spec = _chunk_specs(t)
    return pl.pallas_call(
        body, name=name, grid=(W_CH // CH_COLS, nq),
        in_specs=[q_spec, kv_spec, kv_spec,
                  pl.BlockSpec((CH_STEP_HEADS, QB, CH_KEYS), lambda hp, i: (hp, 0, 0))],
        out_specs=pl.BlockSpec((QB, CH_COLS), lambda hp, i: (i, hp)),
        out_shape=_sds((t, W_CH), BF16), compiler_params=_cparams(2))(qkv, kp, vp, bias)


def chunk_bwd(qkv, qkv_t, kp, vp, bias, do, do_t, name):
    t = qkv.shape[0]
    nq = t // QB

    def body(q_ref, k_ref, v_ref, b_ref, do_ref, qt_ref, dot_ref, dq_ref, dk_ref, dv_ref, db_ref):
        i = pl.program_id(1)

        @pl.when(i == 0)
        def _():
            dk_ref[...] = jnp.zeros_like(dk_ref)
            dv_ref[...] = jnp.zeros_like(dv_ref)
            db_ref[...] = jnp.zeros_like(db_ref)

        pad_row = _chunk_pad_row(i)
        for h in range(CH_STEP_HEADS):
            q = q_ref[:, _hl(h)]
            dov = do_ref[:, _hl(h)]
            kw = k_ref[_window(i), _hl(h)]
            e, inv = _chunk_weights(q, kw, b_ref[h], pad_row)
            p = e * inv
            dp = lax.dot_general(dov, v_ref[_window(i), _hl(h)], NT, preferred_element_type=F32)
            ds = p * (dp - jnp.sum(p * dp, axis=1, keepdims=True))
            db_ref[h] += ds
            dsb = ds.astype(BF16)
            dq_ref[:, _hl(h)] = (jnp.dot(dsb, kw, preferred_element_type=F32) * SCALE).astype(dq_ref.dtype)
            dkt = jnp.dot(qt_ref[_hs(h), :], dsb, preferred_element_type=F32)
            dvt = jnp.dot(dot_ref[_hs(h), :], p.astype(BF16), preferred_element_type=F32)
            for b in range(CH_WIN):
                dk_ref[i + b, _hs(h), :] += dkt[:, b * QB:(b + 1) * QB]
                dv_ref[i + b, _hs(h), :] += dvt[:, b * QB:(b + 1) * QB]

    q_spec, kv_spec = _chunk_specs(t)
    blk = pl.BlockSpec((QB, CH_COLS), lambda hp, i: (i, hp))
    bspec = pl.BlockSpec((CH_STEP_HEADS, QB, CH_KEYS), lambda hp, i: (hp, 0, 0))
    nblk = nq + CH_WIN - 1
    acc_t = pl.BlockSpec((None, nblk, CH_COLS, QB), lambda hp, i: (hp, 0, 0, 0))
    acc_sds = _sds((W_CH // CH_COLS, nblk, CH_COLS, QB), F32)
    return pl.pallas_call(
        body, name=name, grid=(W_CH // CH_COLS, nq),
        in_specs=[q_spec, kv_spec, kv_spec, bspec, blk,
                  pl.BlockSpec((CH_COLS, QB), lambda hp, i: (3 * W_SB // CH_COLS + hp, i)),
                  pl.BlockSpec((CH_COLS, QB), lambda hp, i: (hp, i))],
        out_specs=[blk, acc_t, acc_t, bspec],
        out_shape=[_sds((t, W_CH), BF16), acc_sds, acc_sds, _sds((N_HEADS_CH, QB, CH_KEYS), F32)],
        compiler_params=_cparams(2))(qkv, kp, vp, bias, do, qkv_t, do_t)


def _sum_parts(p_ref):
    total = p_ref[0].astype(F32)
    for p in range(1, p_ref.shape[0]):
        total = total + p_ref[p].astype(F32)
    return total


def sum_parts(parts, grid, p_spec, o_spec, out_sds, name):
    def body(p_ref, o_ref):
        o_ref[...] = _sum_parts(p_ref)

    return pl.pallas_call(body, name=name, grid=grid, in_specs=[p_spec], out_specs=o_spec,
                          out_shape=out_sds, compiler_params=_cparams(len(grid)))(parts)


def adamw(parts, w, m, v, grid, p_specs, w_spec, name):
    c1 = 1.0 / (1.0 - ADAM_B1 ** ADAM_STEP)
    c2 = 1.0 / (1.0 - ADAM_B2 ** ADAM_STEP)
    n = len(parts)

    def body(*refs):
        w_ref, m_ref, v_ref, g_out, d_out, m_out, v_out = refs[n:]
        g = _sum_parts(refs[0])
        for q in range(1, n):
            g = jnp.where(pl.program_id(0) == q, _sum_parts(refs[q]), g)
        m_new = ADAM_B1 * m_ref[...] + (1.0 - ADAM_B1) * g
        v_new = ADAM_B2 * v_ref[...] + (1.0 - ADAM_B2) * (g * g)
        m_hat = m_new * c1
        v_hat = v_new * c2
        g_out[...] = g
        d_out[...] = -ADAM_LR * (m_hat / (jnp.sqrt(v_hat) + ADAM_EPS) + ADAM_WD * w_ref[...])
        m_out[...] = m_new
        v_out[...] = v_new

    out = _sds(w.shape, F32)
    return pl.pallas_call(
        body, name=name, grid=grid, in_specs=[*p_specs, w_spec, w_spec, w_spec],
        out_specs=[w_spec] * 4, out_shape=[out] * 4,
        compiler_params=_cparams(len(grid)))(*parts, w, m, v)


def _ffn_fwd(x, gain, wa, wb_after, s, tm, tag, on_event, deps=()):
    t = x.shape[0]
    hn = rmsnorm_fwd(x, gain, tm, f"rms_{tag}", deps)
    gu, act = ffn_in_swiglu(hn, wa, s, min(2 * tm, t), f"ffn_in_{tag}")
    relayed = on_event("act", act)
    wb = wb_after(act)
    y = ffn_out_residual(act, wb, x, s, min(2 * tm, t), f"ffn_out_{tag}", relayed)
    return y, (hn, gu, act), wb


def _ffn_bwd(dy, x, gain, saved, wa, wb, s, tm, tag, on_grads):
    t = x.shape[0]
    hn, gu, act = saved
    dgu = ffn_dact_swiglu(dy, wb, gu, s, min(2 * tm, t), f"ffn_dact_{tag}")
    dwb = matmul(TN, act, dy, _sds((4, FF_BLK, D_MODEL), BF16), (4, 1, 1),
                 pl.BlockSpec((None, t, FF_BLK), lambda i, j, k: (i, 0, 0)),
                 pl.BlockSpec((t, D_MODEL), lambda i, j, k: (0, 0)),
                 pl.BlockSpec((None, FF_BLK, D_MODEL), lambda i, j, k: (i, 0, 0)),
                 None, name=f"ffn_dwout_{tag}", alpha=0.5)
    dwa = matmul(TN, dgu, hn, _sds((8, FF_BLK, D_MODEL), BF16), (1, 8, 1),
                 pl.BlockSpec((None, None, t, FF_BLK), lambda i, j, k: (j % 4, j // 4, 0, 0)),
                 pl.BlockSpec((t, D_MODEL), lambda i, j, k: (0, 0)),
                 pl.BlockSpec((None, FF_BLK, D_MODEL), lambda i, j, k: (j, 0, 0)),
                 None, name=f"ffn_dwin_{tag}")
    deps = on_grads(dwa, dwb)
    return ffn_dh_norm_bwd(dgu, wa, s, x, gain, dy, tm, f"ffn_dh_{tag}", deps)


BR_ROWS = ((0, 1), (1, 2), (3, 1))

_Q_COLUMN_SCALE = np.ones((1, QKV_WIDTH), np.float32)
for _lo, _width in ((0, W_SB), (3 * W_SB, W_CH), (3 * (W_SB + W_CH), W_FOX)):
    _Q_COLUMN_SCALE[0, _lo:_lo + _width] = SCALE


def _mixer_fwd(x, gain, wqkv, wf, wgate, late_after, bq, bf, bg, bias, layer, tm, tag, on_event):
    t = x.shape[0]
    nt = t // tm
    hm = rmsnorm_fwd(x, gain, tm, f"rms_{tag}")
    a_full = pl.BlockSpec((tm, D_MODEL), lambda i, j, k: (i, 0))
    wide_out = pl.BlockSpec((tm, D_MODEL), lambda i, j, k: (i, j))
    wide_b = pl.BlockSpec((1, D_MODEL), lambda i, j, k: (0, j))
    qkv, qkv_t = matmul(NN, hm, wqkv, _sds((t, QKV_WIDTH), BF16), (nt, 3, 1), a_full,
                        pl.BlockSpec((None, D_MODEL, D_MODEL), lambda i, j, k: (layer, 0, j)), wide_out, None,
                        name=f"proj_qkv_{tag}", bias=bq, bias_spec=wide_b,
                        scale=jnp.asarray(_Q_COLUMN_SCALE), scale_spec=wide_b,
                        out_t_sds=_sds((QKV_WIDTH, t), BF16),
                        out_t_spec=pl.BlockSpec((D_MODEL, tm), lambda i, j, k: (j, i)))
    relayed = on_event("qkv", qkv)
    gates = matmul(NN, hm, wgate, _sds((t, 3 * D_MODEL), F32), (nt, 3, 1), a_full,
                   pl.BlockSpec((None, D_MODEL, D_MODEL), lambda i, j, k: (layer + 1, 0,j)), wide_out,
                   None, name=f"proj_gate_{tag}", bias=bg, bias_spec=wide_b, deps=relayed)
    f = matmul(NN, hm, wf, _sds((t, LANES), F32), (nt, 1, 1), a_full,
               pl.BlockSpec((None, D_MODEL, LANES), lambda i, j, k: (layer, 0, 0)),
               pl.BlockSpec((tm, LANES), lambda i, j, k: (i, 0)), None,
               name=f"proj_f_{tag}", bias=bf, bias_spec=pl.BlockSpec((1, LANES), lambda i, j, k: (0, 0)))
    fcol, frow = forget_cumsum(f, f"fcum_{tag}")
    frow = _frow_to_groups(frow)
    o_sb, w_sb = sb_fwd(qkv, f"sb_fwd_{tag}")
    relayed = on_event("o_sb", o_sb)
    kp = jnp.pad(qkv[:, 10 * LANES:14 * LANES], ((CH_PAD, 0), (0, 0)))
    vp = jnp.pad(qkv[:, 14 * LANES:18 * LANES], ((CH_PAD, 0), (0, 0)))
    o_ch = chunk_fwd(qkv, kp, vp, bias, f"chunk_fwd_{tag}")
    o_fox, lse = fox_fwd(qkv, fcol, frow, f"fox_fwd_{tag}")
    wbr, wout = late_after(o_fox)
    ys = []
    for a, (o, (r0, nr)) in enumerate(zip((o_sb, o_ch, o_fox), BR_ROWS)):
        ys.append(matmul(
            NN, o, wbr, _sds((t, D_MODEL), F32), (nt, 1, nr),
            pl.BlockSpec((tm, 256), lambda i, j, k: (i, k)),
            pl.BlockSpec((None, 256, D_MODEL), functools.partial(lambda i, j, k, r0: (layer, r0 + k, 0), r0=r0)),
            a_full, (tm, D_MODEL), name=f"branch{a}_{tag}", deps=relayed))
    merged = merge_fwd(gates, ys[0], ys[1], ys[2], tm, f"merge_{tag}")
    x_new = matmul(NN, merged, wout, _sds((t, D_MODEL), F32), (nt, 1, 1), a_full,
                   pl.BlockSpec((None, D_MODEL, D_MODEL), lambda i, j, k: (layer, 0, 0)), a_full, None,
                   name=f"wout_{tag}", res=x, res_spec=a_full)
    saved = (hm, qkv, gates, f, fcol, frow, o_sb, o_ch, o_fox, lse, ys, merged, kp, vp, w_sb, qkv_t)
    return x_new, saved, wbr, wout


def _mixer_bwd(dy, x, gain, saved, wqkv, wf, wgate, wbr, wout, bias, layer, tm, tag, on_grads):
    t = x.shape[0]
    nt = t // tm
    hm, qkv, gates, f, fcol, frow, o_sb, o_ch, o_fox, lse, ys, merged, kp, vp, w_sb, qkv_t = saved
    a_full = pl.BlockSpec((tm, D_MODEL), lambda i, j, k: (i, 0))
    red_row = pl.BlockSpec((tm, D_MODEL), lambda i, j, k: (k, 0))
    sq = pl.BlockSpec((D_MODEL, D_MODEL), lambda i, j, k: (0, 0))
    dmerged = matmul(NT, dy, wout, _sds((t, D_MODEL), F32), (nt, 1, 1), a_full,
                     pl.BlockSpec((None, D_MODEL, D_MODEL), lambda i, j, k: (layer, 0, 0)), a_full, None,
                     name=f"dmerged_{tag}")
    all_t = pl.BlockSpec((t, D_MODEL), lambda i, j, k: (0, 0))
    dwout = matmul(TN, merged, dy, _sds((D_MODEL, D_MODEL), BF16), (1, 1, 1), all_t, all_t, sq,
                   None, name=f"dwout_{tag}")
    dgates, dys = merge_bwd(dmerged, gates, ys[0], ys[1], ys[2], tm // 2, f"merge_bwd_{tag}")
    dos, dos_t, dwbrs = [], [], []
    for a, (o, (r0, nr)) in enumerate(zip((o_sb, o_ch, o_fox), BR_ROWS)):
        do, do_t = matmul(
            NT, dys[a], wbr, _sds((t, nr * 256), BF16), (nt, nr, 1), a_full,
            pl.BlockSpec((None, 256, D_MODEL), functools.partial(lambda i, j, k, r0: (layer, r0 + j, 0), r0=r0)),
            pl.BlockSpec((tm, 256), lambda i, j, k: (i, j)), None, name=f"dbranch{a}_{tag}",
            out_t_sds=_sds((nr * 256, t), BF16), out_t_spec=pl.BlockSpec((256, tm), lambda i, j, k: (j, i)))
        dos.append(do)
        dos_t.append(do_t)
        dwbrs.append(matmul(
            TN, o, dys[a], _sds((nr * 256, D_MODEL), BF16), (nr, 1, 1),
            pl.BlockSpec((t, 256), lambda i, j, k: (0, i)), all_t,
            pl.BlockSpec((256, D_MODEL), lambda i, j, k: (i, 0)), None, name=f"dwbr{a}_{tag}"))
    dq_a, dk_a, dv_a = sb_bwd(qkv, qkv_t, w_sb, dos[0], dos_t[0], f"sb_bwd_{tag}")
    dk_a, dv_a = _keys_major(dk_a), _keys_major(dv_a)
    dq_b, dk_b, dv_b, dbias = chunk_bwd(qkv, qkv_t, kp, vp, bias, dos[1], dos_t[1], f"chunk_bwd_{tag}")
    dk_b, dv_b = [x[:, CH_WIN - 1:].transpose(1, 3, 0, 2).reshape(t, W_CH) for x in (dk_b, dv_b)]
    dq_c, dk_c, dv_c, dfrow = fox_bwd(qkv, qkv_t, fcol, frow, o_fox, lse, dos[2], dos_t[2], f"fox_bwd_{tag}")
    dk_c, dv_c = _keys_major(dk_c), _keys_major(dv_c)
    df = forget_cumsum_bwd(_frow_from_groups(dfrow), f, f"fcum_bwd_{tag}")
    dqkv = jnp.concatenate([p.astype(BF16) for p in
                            (dq_a, dk_a, dv_a, dq_b, dk_b, dv_b, dq_c, dk_c, dv_c)], axis=1)
    dtab = rel_bias_scatter(dbias, f"rel_scatter_{tag}")

    all_rows = pl.BlockSpec((t, D_MODEL), lambda i, j, k: (0, 0))
    wide_b = pl.BlockSpec((t, D_MODEL), lambda i, j, k: (0, j))
    wide_o = pl.BlockSpec((D_MODEL, D_MODEL), lambda i, j, k: (0, j))
    wide_cs = pl.BlockSpec((1, D_MODEL), lambda i, j, k: (0, j))
    dwqkv, dbq = matmul(TN, hm, dqkv, _sds((D_MODEL, QKV_WIDTH), BF16), (1, 3, 1), all_rows, wide_b,
                        wide_o, None, name=f"dwqkv_{tag}",
                        colsum_sds=_sds((1, QKV_WIDTH), F32), colsum_spec=wide_cs)
    dwgate, dbg = matmul(TN, hm, dgates, _sds((D_MODEL, 3 * D_MODEL), BF16), (1, 3, 1), all_rows,
                         wide_b, wide_o, None, name=f"dwgate_{tag}",
                         colsum_sds=_sds((1, 3 * D_MODEL), F32), colsum_spec=wide_cs)
    dwf, dbf = matmul(TN, hm, df, _sds((D_MODEL, LANES), BF16), (1, 1, 1), all_rows,
                      pl.BlockSpec((t, LANES), lambda i, j, k: (0, 0)),
                      pl.BlockSpec((D_MODEL, LANES), lambda i, j, k: (0, 0)), None,
                      name=f"dwf_{tag}", colsum_sds=_sds((1, LANES), F32),
                      colsum_spec=pl.BlockSpec((1, LANES), lambda i, j, k: (0, 0)))
    dwbr = jnp.concatenate(dwbrs, axis=0)
    deps = on_grads(dict(dwqkv=dwqkv, dwgate=dwgate, dwf=dwf, dwbr=dwbr, dwout=dwout))
    wide_a = pl.BlockSpec((tm, QKV_WIDTH), lambda i, j, k: (i, 0))
    dhm = matmul(NT, dqkv, wqkv, _sds((t, D_MODEL), F32), (nt, 1, 1), wide_a,
                 pl.BlockSpec((None, D_MODEL, QKV_WIDTH), lambda i, j, k: (layer, 0, 0)), a_full,
                 None, name=f"dhm_qkv_{tag}", deps=deps)
    dhm = matmul(NT, dgates, wgate, _sds((t, D_MODEL), F32), (nt, 1, 1), wide_a,
                 pl.BlockSpec((None, D_MODEL, QKV_WIDTH), lambda i, j, k: (layer + 1, 0, 0)), a_full,
                 None, name=f"dhm_gate_{tag}", res=dhm, res_spec=a_full)
    dhm = matmul(NT, df, wf, _sds((t, D_MODEL), F32), (nt, 1, 1),
                 pl.BlockSpec((tm, LANES), lambda i, j, k: (i, 0)),
                 pl.BlockSpec((None, D_MODEL, LANES), lambda i, j, k: (layer, 0, 0)), a_full, None,
                 name=f"dhm_f_{tag}", res=dhm, res_spec=a_full)
    dx, dgain = rmsnorm_bwd(x, gain, dhm, dy, tm, f"rms_bwd_{tag}")
    return dx, dict(dbq=dbq, dbg=dbg, dbf=dbf, dtab=dtab, dgain=dgain)


def _pack_small(pieces):
    flat = jnp.concatenate([p.reshape(-1).astype(F32) for p in pieces])
    flat = jnp.pad(flat, (0, SMALL_ROWS * LANES - flat.shape[0]))
    return flat.reshape(SMALL_ROWS, LANES)


def _unpack_small(packed, shapes):
    flat = packed.reshape(-1)
    out, pos = [], 0
    for shp in shapes:
        n = int(np.prod(shp))
        out.append(flat[pos:pos + n].reshape(shp))
        pos += n
    return out


def kernel(x, g_ffn1, w_ffn1_in, w_ffn1_out, g_mix, w_in, b_in, rel_bias, w_br_sb, w_br_ch, w_br_fox, w_out, g_ffn2, w_ffn2_in, w_ffn2_out, g_final, loss_target, m_g_ffn1, m_w_ffn1_in, m_w_ffn1_out, m_g_mix, m_w_in, m_b_in, m_rel_bias, m_w_br_sb, m_w_br_ch, m_w_br_fox, m_w_out, m_g_ffn2, m_w_ffn2_in, m_w_ffn2_out, m_g_final, v_g_ffn1, v_w_ffn1_in, v_w_ffn1_out, v_g_mix, v_w_in, v_b_in, v_rel_bias, v_w_br_sb, v_w_br_ch, v_w_br_fox, v_w_out, v_g_ffn2, v_w_ffn2_in, v_w_ffn2_out, v_g_final):
    t = x.shape[1]
    tm = min(512, t)
    xs = x[0]
    target = loss_target[0]
    f_lo, f_hi = QKV_WIDTH, QKV_WIDTH + N_HEADS_FOX

    def ffn_shards(w_in_, w_out_, l):
        return [w_in_[l:l + 1].astype(BF16), w_out_[l:l + 1].astype(BF16)]

    def mixer_shards(l):
        wl = w_in[l]
        return [jnp.stack([wl[:, :QKV_WIDTH], wl[:, f_hi:]]).astype(BF16),
                jnp.pad(wl[:, f_lo:f_hi], ((0, 0), (0, LANES - N_HEADS_FOX)))[None].astype(BF16),
                w_out[l:l + 1].astype(BF16),
                jnp.concatenate([w_br_sb[l], w_br_ch[l], w_br_fox[l]], axis=0)[None].astype(BF16)]

    gathers = {}
    gather_tokens = []

    def start_gather(shards, name):
        handle = gather_start(shards, name, deps=gather_tokens[-1:])
        gather_tokens.append(handle["token"])
        return handle

    def relay(handle, after):
        if "send2" not in handle:
            gather_relay(handle, after)

    relay_on = {("mix", 0, "qkv"): ("mix", 0, 1), ("mix", 0, "o_sb"): ("ffn2", 0, 0),
                ("ffn2", 0, "act"): ("ffn1", 1, 0), ("ffn1", 1, "act"): ("mix", 1, 0),
                ("mix", 1, "qkv"): ("ffn2", 1, 0)}

    def on_event(grp, l):
        def fire(event, array):
            target = relay_on.get((grp, l, event))
            if target is None:
                return ()
            handle = gathers[target[:2]][target[2]]
            relay(handle, array)
            return (handle["relay_token"],)
        return fire

    for l in range(DEPTH):
        for grp, shards in (("ffn1", ffn_shards(w_ffn1_in, w_ffn1_out, l)), ("mix", mixer_shards(l)),
                            ("ffn2", ffn_shards(w_ffn2_in, w_ffn2_out, l))):
            cut = len(shards) // 2
            if l == 0 and grp != "ffn2":
                gathers[(grp, l)] = (start_gather(shards[:cut], f"gather_{grp}_l{l}_a"),
                                     start_gather(shards[cut:], f"gather_{grp}_l{l}_b"))
            else:
                gathers[(grp, l)] = (start_gather(shards, f"gather_{grp}_l{l}"),)

    def gathered(key, after):
        hs = gathers[key]
        cut = hs[0]["n"]
        relay(hs[0], after)
        first = gather_finish(hs[0], after)
        if len(hs) == 1:
            return first[:cut // 2], lambda later: first[cut // 2:]

        def second(later):
            relay(hs[1], later)
            return gather_finish(hs[1], later)

        return first, second

    def ffn_weights(key, after):
        (wa_,), rest = gathered(key, after)
        return wa_, lambda later: rest(later)[0].reshape(1, 4, FF_BLK, D_MODEL)

    def mixer_weights(key, after):
        (wc_, wf_), rest = gathered(key, after)

        def late(later):
            wout_, wbr_ = rest(later)
            return (wbr_.transpose(0, 2, 1, 3).reshape(1, D_MODEL, D_MODEL), wout_.reshape(1, D_MODEL, D_MODEL))

        return wc_.reshape(2, D_MODEL, QKV_WIDTH), wf_.reshape(1, D_MODEL, LANES), late

    bq = b_in[:, None, :QKV_WIDTH]
    bf = jnp.pad(b_in[:, f_lo:f_hi], ((0, 0), (0, LANES - N_HEADS_FOX)))[:, None, :]
    bg = b_in[:, None, f_hi:]
    tab_t = jnp.pad(rel_bias.transpose(0, 2, 1), ((0, 0), (0, 0), (0, REL_PAD - N_REL)))

    h = xs
    saved = []
    weights = []
    for l in range(DEPTH):
        bias = rel_bias_build(tab_t[l], f"rel_build_l{l}").reshape(N_HEADS_CH, QB, CH_KEYS)
        x0 = h
        wa1, wb1_after = ffn_weights(("ffn1", l), x0)
        x1, s1, wb1 = _ffn_fwd(x0, g_ffn1[l:l + 1], wa1, wb1_after, 0, tm, f"ffn1_l{l}", on_event("ffn1", l),
                               deps=gather_tokens if l == 0 else ())
        wc, wf, late_after = mixer_weights(("mix", l), x1)
        x2, sm, wbr, wout = _mixer_fwd(x1, g_mix[l:l + 1], wc, wf, wc, late_after, bq[l], bf[l], bg[l],
                                       bias, 0, tm, f"mix_l{l}", on_event("mix", l))
        wa2, wb2_after = ffn_weights(("ffn2", l), x2)
        x3, s2, wb2 = _ffn_fwd(x2, g_ffn2[l:l + 1], wa2, wb2_after, 0, tm, f"ffn2_l{l}", on_event("ffn2", l))
        saved.append((x0, x1, x2, s1, sm, s2, bias))
        weights.append(((wa1, wb1), (wc, wf, wout, wbr), (wa2, wb2)))
        h = x3

    dx, dg_final, loss_blk = loss_head(h, g_final[None, :], target, tm, "loss_head")

    g_mix_l = [None] * DEPTH
    dgains = {}
    scatters = {}

    def scatter_ffn(key):
        def on_grads(dwa, dwb):
            scatters[key] = exchange_start(
                "scatter", [dwa[None], dwb.reshape(1, N_DEV, D_FF // N_DEV, D_MODEL)],
                f"scatter_{key[0]}_l{key[1]}")
            return (scatters[key]["token"],)
        return on_grads

    def scatter_mixer(key):
        def on_grads(gm):
            scatters[key] = exchange_start(
                "scatter",
                [gm["dwqkv"].reshape(1, N_DEV, LANES, QKV_WIDTH), gm["dwgate"].reshape(1, N_DEV, LANES, QKV_WIDTH),
                 gm["dwf"].reshape(1, N_DEV, LANES, LANES), gm["dwout"].reshape(1, N_DEV, LANES, D_MODEL),
                 gm["dwbr"].reshape(1, D_MODEL, N_DEV, LANES).transpose(0, 2, 1, 3)],
                f"scatter_{key[0]}_l{key[1]}")
            return (scatters[key]["token"],)
        return on_grads

    for l in reversed(range(DEPTH)):
        x0, x1, x2, s1, sm, s2, bias = saved[l]
        w1, (wc, wf, wout, wbr), w2 = weights[l]
        dx, dgains[("ffn2", l)] = _ffn_bwd(dx, x2, g_ffn2[l:l + 1], s2, *w2, 0, tm, f"ffn2_l{l}",
                                           scatter_ffn(("ffn2", l)))
        dx, g_mix_l[l] = _mixer_bwd(dx, x1, g_mix[l:l + 1], sm, wc, wf, wc, wbr, wout, bias, 0, tm,
                                    f"mix_l{l}", scatter_mixer(("mix", l)))
        dx, dgains[("ffn1", l)] = _ffn_bwd(dx, x0, g_ffn1[l:l + 1], s1, *w1, 0, tm, f"ffn1_l{l}",
                                           scatter_ffn(("ffn1", l)))

    small_shapes = []
    small_pieces = []
    small_w, small_m, small_v = [], [], []

    def add_small(piece, w, m, v):
        small_shapes.append(w.shape)
        small_pieces.append(piece)
        small_w.append(w); small_m.append(m); small_v.append(v)

    dg1 = jnp.concatenate([dgains[("ffn1", l)] for l in range(DEPTH)], axis=0)
    dgm = jnp.concatenate([g_mix_l[l]["dgain"] for l in range(DEPTH)], axis=0)
    dg2 = jnp.concatenate([dgains[("ffn2", l)] for l in range(DEPTH)], axis=0)
    db = jnp.stack([jnp.concatenate([g_mix_l[l]["dbq"][0], g_mix_l[l]["dbf"][0, :N_HEADS_FOX],
                                     g_mix_l[l]["dbg"][0]]) for l in range(DEPTH)])
    drel = jnp.stack([g_mix_l[l]["dtab"][:, :N_REL].T for l in range(DEPTH)])
    add_small(dg1, g_ffn1, m_g_ffn1, v_g_ffn1)
    add_small(dgm, g_mix, m_g_mix, v_g_mix)
    add_small(db, b_in, m_b_in, v_b_in)
    add_small(drel, rel_bias, m_rel_bias, v_rel_bias)
    add_small(dg2, g_ffn2, m_g_ffn2, v_g_ffn2)
    add_small(dg_final[0], g_final, m_g_final, v_g_final)
    loss_piece = loss_blk[0, 0:1]
    small_packed = _pack_small(small_pieces + [loss_piece])

    recv = {}
    last = ("ffn1", 0)
    for l in reversed(range(DEPTH)):
        for grp in ("ffn2", "mix", "ffn1"):
            if (grp, l) != last:
                recv[(grp, l)] = exchange_wait(scatters[(grp, l)], dx, f"scattered_{grp}_l{l}")

    def upd(parts, w, m, v, tr, name, rb0=0):
        _, r, c = w.shape
        nr = r // tr

        def p_spec(layer):
            pinned = (nr - 1) if layer == 0 else 0
            return pl.BlockSpec((N_DEV, None, tr, c),
                                lambda l, i: (0, 0, rb0 + jnp.where(l == layer, i, pinned), 0))

        return adamw(parts, w, m, v, (DEPTH, nr), [p_spec(0), p_spec(1)],
                     pl.BlockSpec((None, tr, c), lambda l, i: (l, i, 0)), name)

    def both(grp, k):
        return [recv[(grp, l)][k] for l in range(DEPTH)]

    out_rows = D_FF // N_DEV // 2
    def upd_transposed(parts, w, m, v, tr, name):
        tp = lambda a: jnp.transpose(a, (0, 2, 1))
        return [tp(o) for o in upd(parts, tp(w), tp(m), tp(v), tr, name)]

    in_rows = FF_BLK // 4
    r_ffn2_in = upd_transposed(both("ffn2", 0), w_ffn2_in, m_w_ffn2_in, v_w_ffn2_in, in_rows, "adamw_ffn2_in")
    r_ffn2_out = upd(both("ffn2", 1), w_ffn2_out, m_w_ffn2_out, v_w_ffn2_out, out_rows, "adamw_ffn2_out")
    r_out = upd(both("mix", 3), w_out, m_w_out, v_w_out, LANES, "adamw_w_out")
    r_br_sb = upd(both("mix", 4), w_br_sb, m_w_br_sb, v_w_br_sb, 256, "adamw_br_sb", rb0=0)
    r_br_ch = upd(both("mix", 4), w_br_ch, m_w_br_ch, v_w_br_ch, 256, "adamw_br_ch", rb0=1)
    r_br_fox = upd(both("mix", 4), w_br_fox, m_w_br_fox, v_w_br_fox, 256, "adamw_br_fox", rb0=3)

    def summed(parts, name):
        _, _, r, c = parts.shape
        return sum_parts(parts, (1,), pl.BlockSpec((N_DEV, None, r, c), lambda s: (0, 0, 0, 0)),
                         pl.BlockSpec((r, c), lambda s: (0, 0)), _sds((r, c), F32), name)

    g_w_in = jnp.stack([
        jnp.concatenate([summed(recv[("mix", l)][0], f"sum_wqkv_l{l}"),
                         summed(recv[("mix", l)][2], f"sum_wf_l{l}")[:, :N_HEADS_FOX],
                         summed(recv[("mix", l)][1], f"sum_wgate_l{l}")], axis=1) for l in range(DEPTH)])
    to_cols = lambda a: jnp.transpose(a, (2, 0, 1))
    n_cols = w_in.shape[2]
    col_blk = n_cols // 4
    win_spec = pl.BlockSpec((col_blk, DEPTH, LANES), lambda i: (i, 0, 0))
    r_in = adamw([to_cols(g_w_in)[None]], to_cols(w_in), to_cols(m_w_in), to_cols(v_w_in), (4,),
                 [pl.BlockSpec((1, col_blk, DEPTH, LANES), lambda i: (0, i, 0, 0))], win_spec, "adamw_w_in")
    r_in = [jnp.transpose(o, (1, 2, 0)) for o in r_in]

    recv[last] = exchange_wait(scatters[last], r_in[1], "scattered_ffn1_l0")
    r_ffn1_in = upd_transposed(both("ffn1", 0), w_ffn1_in, m_w_ffn1_in, v_w_ffn1_in, in_rows, "adamw_ffn1_in")
    r_ffn1_out = upd(both("ffn1", 1), w_ffn1_out, m_w_ffn1_out, v_w_ffn1_out, out_rows, "adamw_ffn1_out")

    small_sum = all_reduce_small(small_packed, "allreduce_small", deps=(r_ffn1_out[1],))
    n_small = sum(int(np.prod(s)) for s in small_shapes)
    loss = small_sum.reshape(-1)[n_small]
    sm_spec = pl.BlockSpec((SMALL_ROWS, LANES), lambda i: (0, 0))
    sm_out = adamw([small_sum[None]], _pack_small(small_w), _pack_small(small_m), _pack_small(small_v),
                   (1,), [pl.BlockSpec((1, SMALL_ROWS, LANES), lambda i: (0, 0, 0))], sm_spec, "adamw_small")
    sm_g, sm_d, sm_m, sm_v = [_unpack_small(o, small_shapes) for o in sm_out]

    def per_kind(k):
        small = (sm_g, sm_d, sm_m, sm_v)[k]
        return [small[0], r_ffn1_in[k], r_ffn1_out[k], small[1], r_in[k], small[2], small[3],
                r_br_sb[k], r_br_ch[k], r_br_fox[k], r_out[k], small[4], r_ffn2_in[k], r_ffn2_out[k],
                small[5]]

    return (loss, dx[None], *per_kind(0), *per_kind(1), *per_kind(2), *per_kind(3))
```

```python
import functools

import numpy as np
import jax
import jax.numpy as jnp
from jax import lax
from jax.experimental import pallas as pl
from jax.experimental.pallas import tpu as pltpu

F32 = jnp.float32
BF16 = jnp.bfloat16

N_DEV = 8
D_MODEL = 1024
DEPTH = 2
HEAD_DIM = 64
W_SB, W_CH, W_FOX = 256, 512, 256
QKV_WIDTH = 3 * (W_SB + W_CH + W_FOX)
N_HEADS_FOX = 4
N_HEADS_CH = 8
D_FF = 2816
FF_BLK = 2 * D_FF // N_DEV
CHUNK = 64
LEFT_CHUNKS = 8
MAX_REL = 128
N_REL = 2 * MAX_REL + 1
REL_PAD = 384
QB = 128
KB = 512
KSUB = KB // QB
CH_WIN = 5
CH_KEYS = CH_WIN * QB
RMS_EPS = 1e-6
NEG = -1e30
SCALE = HEAD_DIM ** -0.5
LANES = 128
VMEM_LIMIT = 56 * 1024 * 1024

ADAM_LR, ADAM_B1, ADAM_B2, ADAM_EPS, ADAM_WD, ADAM_STEP = 0.001, 0.9, 0.999, 1e-08, 0.01, 10

SMALL_ROWS = 192

MESH = pl.DeviceIdType.MESH
ANY = pl.BlockSpec(memory_space=pl.ANY)
HIGHEST = lax.Precision.HIGHEST

NN = (((1,), (0,)), ((), ()))
NT = (((1,), (1,)), ((), ()))
TN = (((0,), (0,)), ((), ()))


def _cparams(n_grid):
    return pltpu.CompilerParams(dimension_semantics=("arbitrary",) * n_grid,
                                vmem_limit_bytes=VMEM_LIMIT)


def _sds(shape, dtype):
    return jax.ShapeDtypeStruct(tuple(shape), dtype)


def _my_index():
    return 4 * lax.axis_index("x") + 2 * lax.axis_index("y") + lax.axis_index("c")


def _peer(mask):
    x, y, c = lax.axis_index("x"), lax.axis_index("y"), lax.axis_index("c")
    px = x ^ ((mask >> 2) & 1)
    py = y ^ ((mask >> 1) & 1)
    pc = c ^ (mask & 1)
    return (px, py, pc), 4 * px + 2 * py + pc


def all_gather(shard, name):
    s, r, c = shard.shape

    def body(in_ref, out_ref, send_sems, recv_sems, local_sem):
        me = _my_index()
        mine = pltpu.make_async_copy(in_ref, out_ref.at[:, me], local_sem)
        mine.start()
        sends = []
        for mask in range(1, N_DEV):
            peer, _ = _peer(mask)
            cp = pltpu.make_async_remote_copy(
                src_ref=in_ref, dst_ref=out_ref.at[:, me],
                send_sem=send_sems.at[mask - 1], recv_sem=recv_sems.at[mask - 1],
                device_id=peer, device_id_type=MESH)
            cp.start()
            sends.append(cp)
        for mask in range(1, N_DEV):
            peer, pidx = _peer(mask)
            pltpu.make_async_remote_copy(
                src_ref=in_ref, dst_ref=out_ref.at[:, pidx],
                send_sem=send_sems.at[mask - 1], recv_sem=recv_sems.at[mask - 1],
                device_id=peer, device_id_type=MESH).wait_recv()
        for cp in sends:
            cp.wait_send()
        mine.wait()

    return pl.pallas_call(
        body, name=name,
        out_shape=_sds((s, N_DEV, r, c), shard.dtype),
        in_specs=[ANY], out_specs=ANY,
        scratch_shapes=[pltpu.SemaphoreType.DMA((N_DEV - 1,)),
                        pltpu.SemaphoreType.DMA((N_DEV - 1,)),
                        pltpu.SemaphoreType.DMA],
    )(shard)


def all_to_all(parts, name):
    s, _, r, c = parts.shape

    def body(in_ref, out_ref, send_sems, recv_sems, local_sem):
        me = _my_index()
        mine = pltpu.make_async_copy(in_ref.at[:, me], out_ref.at[me], local_sem)
        mine.start()
        sends = []
        for mask in range(1, N_DEV):
            peer, pidx = _peer(mask)
            cp = pltpu.make_async_remote_copy(
                src_ref=in_ref.at[:, pidx], dst_ref=out_ref.at[me],
                send_sem=send_sems.at[mask - 1], recv_sem=recv_sems.at[mask - 1],
                device_id=peer, device_id_type=MESH)
            cp.start()
            sends.append(cp)
        for mask in range(1, N_DEV):
            peer, pidx = _peer(mask)
            pltpu.make_async_remote_copy(
                src_ref=in_ref.at[:, me], dst_ref=out_ref.at[pidx],
                send_sem=send_sems.at[mask - 1], recv_sem=recv_sems.at[mask - 1],
                device_id=peer, device_id_type=MESH).wait_recv()
        for cp in sends:
            cp.wait_send()
        mine.wait()

    return pl.pallas_call(
        body, name=name,
        out_shape=_sds((N_DEV, s, r, c), parts.dtype),
        in_specs=[ANY], out_specs=ANY,
        scratch_shapes=[pltpu.SemaphoreType.DMA((N_DEV - 1,)),
                        pltpu.SemaphoreType.DMA((N_DEV - 1,)),
                        pltpu.SemaphoreType.DMA],
    )(parts)


HBM_SPEC = pl.BlockSpec(memory_space=pltpu.HBM)
SEM_SPEC = pl.BlockSpec(memory_space=pltpu.SEMAPHORE)
EFFECT = pltpu.SideEffectType.DATAFLOW_SIDE_EFFECTING


def _exchange_refs(mode, in_ref, land_ref, me, pidx):
    if mode == "gather":
        return in_ref, land_ref.at[:, me], land_ref.at[:, pidx]
    return in_ref.at[:, pidx], land_ref.at[me], land_ref.at[pidx]


def _landing_shape(mode, a):
    if mode == "gather":
        s, r, c = a.shape
        return (s, N_DEV, r, c)
    s, _, r, c = a.shape
    return (N_DEV, s, r, c)


def _own_copy(mode, in_ref, land_ref, me, sem):
    if mode == "gather":
        return pltpu.make_async_copy(in_ref, land_ref.at[:, me], sem)
    return pltpu.make_async_copy(in_ref.at[:, me], land_ref.at[me], sem)


def exchange_start(mode, arrays, name, deps=()):
    n = len(arrays)
    lands0 = [lax.empty(_landing_shape(mode, a), a.dtype) for a in arrays]

    def body(*refs):
        in_refs, land_refs = refs[:n], refs[n:2 * n]
        outs_at = 2 * n + len(deps)
        send_sems, recv_sems, own_sems, token = refs[outs_at], refs[outs_at + 1], refs[outs_at + 2], refs[-1]
        mine = _my_index()
        for k in range(n):
            _own_copy(mode, in_refs[k], land_refs[k], mine, own_sems.at[k]).start()
            for mask in range(1, N_DEV):
                peer, pidx = _peer(mask)
                src, dst, _ = _exchange_refs(mode, in_refs[k], land_refs[k], mine, pidx)
                sem = k * (N_DEV - 1) + mask - 1
                pltpu.make_async_remote_copy(
                    src_ref=src, dst_ref=dst, send_sem=send_sems.at[sem], recv_sem=recv_sems.at[sem],
                    device_id=peer, device_id_type=MESH).start()
        token[...] = jnp.zeros_like(token)

    nsem = n * (N_DEV - 1)
    outs = pl.pallas_call(
        body, name=name,
        out_shape=(pltpu.SemaphoreType.DMA((nsem,)), pltpu.SemaphoreType.DMA((nsem,)),
                   pltpu.SemaphoreType.DMA((n,)),
                   *[pltpu.HBM(a.shape, a.dtype) for a in arrays],
                   *[pltpu.HBM(l.shape, l.dtype) for l in lands0], _sds((8, LANES), F32)),
        in_specs=[HBM_SPEC] * (2 * n) + [ANY] * len(deps),
        out_specs=(SEM_SPEC, SEM_SPEC, SEM_SPEC, *[HBM_SPEC] * (2 * n),
                   pl.BlockSpec(memory_space=pltpu.VMEM)),
        input_output_aliases={k: 3 + k for k in range(2 * n)},
        compiler_params=pltpu.CompilerParams(has_side_effects=EFFECT),
    )(*[pltpu.with_memory_space_constraint(a, pltpu.HBM) for a in arrays],
      *[pltpu.with_memory_space_constraint(l, pltpu.HBM) for l in lands0], *deps)
    return dict(mode=mode, n=n, send=outs[0], recv=outs[1], own=outs[2], ins=outs[3:3 + n],
                lands=outs[3 + n:3 + 2 * n], token=outs[-1])


def exchange_wait(handle, after, name):
    n, mode = handle["n"], handle["mode"]

    def body(*refs):
        in_refs, land_refs = refs[:n], refs[n:2 * n]
        send_sems, recv_sems, own_sems = refs[2 * n], refs[2 * n + 1], refs[2 * n + 2]
        mine = _my_index()
        for k in range(n):
            _own_copy(mode, in_refs[k], land_refs[k], mine, own_sems.at[k]).wait()
            for mask in range(1, N_DEV):
                peer, pidx = _peer(mask)
                src, _, here = _exchange_refs(mode, in_refs[k], land_refs[k], mine, pidx)
                sem = k * (N_DEV - 1) + mask - 1
                cp = pltpu.make_async_remote_copy(
                    src_ref=src, dst_ref=here, send_sem=send_sems.at[sem], recv_sem=recv_sems.at[sem],
                    device_id=peer, device_id_type=MESH)
                cp.wait_send()
                cp.wait_recv()

    thru = (*handle["ins"], *handle["lands"])
    outs = pl.pallas_call(
        body, name=name,
        out_shape=tuple(pltpu.HBM(a.shape, a.dtype) for a in thru),
        in_specs=[HBM_SPEC] * (2 * n) + [SEM_SPEC, SEM_SPEC, SEM_SPEC, ANY],
        out_specs=tuple([HBM_SPEC] * (2 * n)),
        input_output_aliases={k: k for k in range(2 * n)},
        compiler_params=pltpu.CompilerParams(has_side_effects=EFFECT),
    )(*thru, handle["send"], handle["recv"], handle["own"], after)
    return list(outs[n:])


FAR_MASKS = (2, 4, 6)
PHASE1_MASKS = (1,) + FAR_MASKS


def gather_start(arrays, name, deps=()):
    n = len(arrays)
    n1 = len(PHASE1_MASKS)
    lands0 = [lax.empty(_landing_shape("gather", a), a.dtype) for a in arrays]

    def body(*refs):
        in_refs, land_refs = refs[:n], refs[n:2 * n]
        outs_at = 2 * n + len(deps)
        send_sems, recv_sems, own_sems, token = refs[outs_at], refs[outs_at + 1], refs[outs_at + 2], refs[-1]
        mine = _my_index()
        for k in range(n):
            _own_copy("gather", in_refs[k], land_refs[k], mine, own_sems.at[k]).start()
            for j, mask in enumerate(PHASE1_MASKS):
                peer, _ = _peer(mask)
                pltpu.make_async_remote_copy(
                    src_ref=in_refs[k], dst_ref=land_refs[k].at[:, mine],
                    send_sem=send_sems.at[k * n1 + j], recv_sem=recv_sems.at[k * n1 + j],
                    device_id=peer, device_id_type=MESH).start()
        token[...] = jnp.zeros_like(token)

    outs = pl.pallas_call(
        body, name=name,
        out_shape=(pltpu.SemaphoreType.DMA((n * n1,)), pltpu.SemaphoreType.DMA((n * n1,)),
                   pltpu.SemaphoreType.DMA((n,)),
                   *[pltpu.HBM(a.shape, a.dtype) for a in arrays],
                   *[pltpu.HBM(l.shape, l.dtype) for l in lands0], _sds((8, LANES), F32)),
        in_specs=[HBM_SPEC] * (2 * n) + [ANY] * len(deps),
        out_specs=(SEM_SPEC, SEM_SPEC, SEM_SPEC, *[HBM_SPEC] * (2 * n),
                   pl.BlockSpec(memory_space=pltpu.VMEM)),
        input_output_aliases={k: 3 + k for k in range(2 * n)},
        compiler_params=pltpu.CompilerParams(has_side_effects=EFFECT),
    )(*[pltpu.with_memory_space_constraint(a, pltpu.HBM) for a in arrays],
      *[pltpu.with_memory_space_constraint(l, pltpu.HBM) for l in lands0], *deps)
    return dict(n=n, send=outs[0], recv=outs[1], own=outs[2], ins=outs[3:3 + n],
                lands=outs[3 + n:3 + 2 * n], token=outs[-1], name=name)


def gather_relay(handle, after):
    n = handle["n"]
    n1, n2 = len(PHASE1_MASKS), len(FAR_MASKS)

    def body(*refs):
        in_refs, land_refs = refs[:n], refs[n:2 * n]
        send1, recv1 = refs[2 * n], refs[2 * n + 1]
        send2, recv2, token = refs[2 * n + 3], refs[2 * n + 4], refs[-1]
        token[...] = jnp.zeros_like(token)
        sibling, _ = _peer(1)
        for k in range(n):
            for j, mask in enumerate(FAR_MASKS):
                peer, pidx = _peer(mask)
                landed = land_refs[k].at[:, pidx]
                pltpu.make_async_remote_copy(
                    src_ref=in_refs[k], dst_ref=landed, send_sem=send1.at[k * n1 + 1 + j],
                    recv_sem=recv1.at[k * n1 + 1 + j], device_id=peer, device_id_type=MESH).wait_recv()
                pltpu.make_async_remote_copy(
                    src_ref=landed, dst_ref=landed, send_sem=send2.at[k * n2 + j],
                    recv_sem=recv2.at[k * n2 + j], device_id=sibling, device_id_type=MESH).start()

    thru = (*handle["ins"], *handle["lands"])
    outs = pl.pallas_call(
        body, name=handle["name"] + "_relay",
        out_shape=(pltpu.SemaphoreType.DMA((n * n2,)), pltpu.SemaphoreType.DMA((n * n2,)),
                   *[pltpu.HBM(a.shape, a.dtype) for a in thru], _sds((8, LANES), F32)),
        in_specs=[HBM_SPEC] * (2 * n) + [SEM_SPEC, SEM_SPEC, ANY],
        out_specs=(SEM_SPEC, SEM_SPEC, *[HBM_SPEC] * (2 * n), pl.BlockSpec(memory_space=pltpu.VMEM)),
        input_output_aliases={k: 2 + k for k in range(2 * n)},
        compiler_params=pltpu.CompilerParams(has_side_effects=EFFECT),
    )(*thru, handle["send"], handle["recv"], after)
    handle.update(send2=outs[0], recv2=outs[1], ins=outs[2:2 + n], lands=outs[2 + n:2 + 2 * n],
                  relay_token=outs[-1])


def gather_finish(handle, after):
    n = handle["n"]
    n1, n2 = len(PHASE1_MASKS), len(FAR_MASKS)

    def body(*refs):
        in_refs, land_refs = refs[:n], refs[n:2 * n]
        send1, recv1, own_sems, send2, recv2 = refs[2 * n:2 * n + 5]
        mine = _my_index()
        sibling, sib_idx = _peer(1)
        for k in range(n):
            _own_copy("gather", in_refs[k], land_refs[k], mine, own_sems.at[k]).wait()
            for j, mask in enumerate(PHASE1_MASKS):
                peer, pidx = _peer(mask)
                cp = pltpu.make_async_remote_copy(
                    src_ref=in_refs[k], dst_ref=land_refs[k].at[:, pidx], send_sem=send1.at[k * n1 + j],
                    recv_sem=recv1.at[k * n1 + j], device_id=peer, device_id_type=MESH)
                cp.wait_send()
                if mask == 1:
                    cp.wait_recv()
            for j, mask in enumerate(FAR_MASKS):
                _, pidx = _peer(mask)
                _, far_of_sibling = _peer(mask ^ 1)
                cp = pltpu.make_async_remote_copy(
                    src_ref=land_refs[k].at[:, pidx], dst_ref=land_refs[k].at[:, far_of_sibling],
                    send_sem=send2.at[k * n2 + j], recv_sem=recv2.at[k * n2 + j],
                    device_id=sibling, device_id_type=MESH)
                cp.wait_send()
                cp.wait_recv()

    thru = (*handle["ins"], *handle["lands"])
    outs = pl.pallas_call(
        body, name=handle["name"] + "_finish",
        out_shape=tuple(pltpu.HBM(a.shape, a.dtype) for a in thru),
        in_specs=[HBM_SPEC] * (2 * n) + [SEM_SPEC] * 5 + [ANY],
        out_specs=tuple([HBM_SPEC] * (2 * n)),
        input_output_aliases={k: k for k in range(2 * n)},
        compiler_params=pltpu.CompilerParams(has_side_effects=EFFECT),
    )(*thru, handle["send"], handle["recv"], handle["own"], handle["send2"], handle["recv2"], after)
    return list(outs[n:])


def all_reduce_small(packed, name, deps=()):
    rows = packed.shape[0]
    nd = len(deps)

    def body(in_ref, *rest):
        out_ref, slots, send_sems, recv_sems = rest[nd:]
        me = _my_index()
        sends = []
        for mask in range(1, N_DEV):
            peer, _ = _peer(mask)
            cp = pltpu.make_async_remote_copy(
                src_ref=in_ref, dst_ref=slots.at[me],
                send_sem=send_sems.at[mask - 1], recv_sem=recv_sems.at[mask - 1],
                device_id=peer, device_id_type=MESH)
            cp.start()
            sends.append(cp)
        slots[me] = in_ref[...]
        for mask in range(1, N_DEV):
            peer, pidx = _peer(mask)
            pltpu.make_async_remote_copy(
                src_ref=in_ref, dst_ref=slots.at[pidx],
                send_sem=send_sems.at[mask - 1], recv_sem=recv_sems.at[mask - 1],
                device_id=peer, device_id_type=MESH).wait_recv()
        for cp in sends:
            cp.wait_send()
        total = slots[0]
        for p in range(1, N_DEV):
            total = total + slots[p]
        out_ref[...] = total

    return pl.pallas_call(
        body, name=name,
        out_shape=_sds((rows, LANES), F32),
        in_specs=[pl.BlockSpec(memory_space=pltpu.VMEM)] + [ANY] * nd,
        out_specs=pl.BlockSpec(memory_space=pltpu.VMEM),
        scratch_shapes=[pltpu.VMEM((N_DEV, rows, LANES), F32),
                        pltpu.SemaphoreType.DMA((N_DEV - 1,)),
                        pltpu.SemaphoreType.DMA((N_DEV - 1,))],
    )(packed, *deps)


def matmul(dims, a, b, out_sds, grid, a_spec, b_spec, o_spec, acc_shape, *, name, alpha=1.0,
           bias=None, bias_spec=None, scale=None, scale_spec=None, res=None, res_spec=None,
           colsum_sds=None, colsum_spec=None, out_t_sds=None, out_t_spec=None, deps=()):
    nk = grid[2]
    has_bias, has_scale, has_res = bias is not None, scale is not None, res is not None
    has_cs, has_t = colsum_sds is not None, out_t_sds is not None
    if has_cs:
        assert grid[0] == 1 and dims == TN

    def body(*refs):
        a_ref, b_ref = refs[0], refs[1]
        pos = 2
        bias_ref = scale_ref = res_ref = cs_ref = ot_ref = None
        if has_bias:
            bias_ref = refs[pos]; pos += 1
        if has_scale:
            scale_ref = refs[pos]; pos += 1
        if has_res:
            res_ref = refs[pos]; pos += 1
        pos += len(deps)
        o_ref = refs[pos]; pos += 1
        if has_cs:
            cs_ref = refs[pos]; pos += 1
        if has_t:
            ot_ref = refs[pos]; pos += 1
        k = pl.program_id(2)
        bval = b_ref[...]
        part = lax.dot_general(a_ref[...].astype(BF16), bval.astype(BF16), dims,
                               preferred_element_type=F32)

        def finish(total):
            r = total * alpha if alpha != 1.0 else total
            if has_bias:
                r = r + bias_ref[...]
            if has_scale:
                r = r * scale_ref[...]
            if has_res:
                r = r + res_ref[...].astype(F32)
            o_ref[...] = r.astype(o_ref.dtype)
            if has_t:
                ot_ref[...] = r.T.astype(ot_ref.dtype)

        if has_cs:
            csum = jnp.sum(bval.astype(F32), axis=0, keepdims=True)

            @pl.when(k == 0)
            def _():
                cs_ref[...] = csum

            @pl.when(k > 0)
            def _():
                cs_ref[...] += csum

        if nk == 1:
            finish(part)
        else:
            acc_ref = refs[pos]

            @pl.when(k == 0)
            def _():
                acc_ref[...] = part

            @pl.when(k > 0)
            def _():
                acc_ref[...] += part

            @pl.when(k == nk - 1)
            def _():
                finish(acc_ref[...])

    in_specs, args = [a_spec, b_spec], [a, b]
    if has_bias:
        in_specs.append(bias_spec); args.append(bias)
    if has_scale:
        in_specs.append(scale_spec); args.append(scale)
    if has_res:
        in_specs.append(res_spec); args.append(res)
    in_specs += [ANY] * len(deps)
    args += list(deps)
    out_shape, out_specs = [out_sds], [o_spec]
    if has_cs:
        out_shape.append(colsum_sds); out_specs.append(colsum_spec)
    if has_t:
        out_shape.append(out_t_sds); out_specs.append(out_t_spec)
    scratch = [] if nk == 1 else [pltpu.VMEM(acc_shape, F32)]
    outs = pl.pallas_call(
        body, name=name, grid=grid, in_specs=in_specs, out_specs=out_specs, out_shape=out_shape,
        scratch_shapes=scratch, compiler_params=_cparams(3))(*args)
    return outs if (has_cs or has_t) else outs[0]


def _sigmoid(z):
    return 1.0 / (1.0 + jnp.exp(-z))


def _log_sigmoid(z):
    return jnp.minimum(z, 0.0) - jnp.log(1.0 + jnp.exp(-jnp.abs(z)))


def rmsnorm_fwd(x, gain, tm, name, deps=()):
    t, d = x.shape

    def body(x_ref, g_ref, *rest):
        o_ref = rest[-1]
        xf = x_ref[...]
        r = lax.rsqrt(jnp.mean(xf * xf, axis=-1, keepdims=True) + RMS_EPS)
        o_ref[...] = (xf * r * g_ref[...]).astype(o_ref.dtype)

    return pl.pallas_call(
        body, name=name, grid=(t // tm,),
        in_specs=[pl.BlockSpec((tm, d), lambda i: (i, 0)), pl.BlockSpec((1, d), lambda i: (0, 0))]
        + [ANY] * len(deps),
        out_specs=pl.BlockSpec((tm, d), lambda i: (i, 0)),
        out_shape=_sds((t, d), BF16), compiler_params=_cparams(1))(x, gain, *deps)


def rmsnorm_bwd(x, gain, dh, dres, tm, name):
    t, d = x.shape

    def body(x_ref, g_ref, dh_ref, dres_ref, dx_ref, dg_ref):
        i = pl.program_id(0)
        xf = x_ref[...]
        r = lax.rsqrt(jnp.mean(xf * xf, axis=-1, keepdims=True) + RMS_EPS)
        xhat = xf * r
        dh_v = dh_ref[...]
        dxhat = dh_v * g_ref[...]
        dx = r * (dxhat - xhat * jnp.mean(dxhat * xhat, axis=-1, keepdims=True))
        dx_ref[...] = dres_ref[...] + dx
        dg = jnp.sum(dh_v * xhat, axis=0, keepdims=True)

        @pl.when(i == 0)
        def _():
            dg_ref[...] = dg

        @pl.when(i > 0)
        def _():
            dg_ref[...] += dg

    row = pl.BlockSpec((tm, d), lambda i: (i, 0))
    vec = pl.BlockSpec((1, d), lambda i: (0, 0))
    return pl.pallas_call(
        body, name=name, grid=(t // tm,), in_specs=[row, vec, row, row], out_specs=[row, vec],
        out_shape=[_sds((t, d), F32), _sds((1, d), F32)], compiler_params=_cparams(1))(x, gain, dh, dres)


def loss_head(x, gain, target, tm, name):
    t, d = x.shape

    def body(x_ref, g_ref, tgt_ref, dx_ref, dg_ref, loss_ref):
        i = pl.program_id(0)
        xf = x_ref[...]
        g = g_ref[...]
        r = lax.rsqrt(jnp.mean(xf * xf, axis=-1, keepdims=True) + RMS_EPS)
        xhat = xf * r
        err = xhat * g - tgt_ref[...]
        part = 0.5 * jnp.sum(jnp.mean(err * err, axis=-1, keepdims=True))
        dy = err * (1.0 / d)
        dxhat = dy * g
        dx_ref[...] = r * (dxhat - xhat * jnp.mean(dxhat * xhat, axis=-1, keepdims=True))
        dg = jnp.sum(dy * xhat, axis=0, keepdims=True)
        lpart = jnp.full((8, LANES), part, F32)

        @pl.when(i == 0)
        def _():
            dg_ref[...] = dg
            loss_ref[...] = lpart

        @pl.when(i > 0)
        def _():
            dg_ref[...] += dg
            loss_ref[...] += lpart

    row = pl.BlockSpec((tm, d), lambda i: (i, 0))
    vec = pl.BlockSpec((1, d), lambda i: (0, 0))
    return pl.pallas_call(
        body, name=name, grid=(t // tm,), in_specs=[row, vec, row],
        out_specs=[row, vec, pl.BlockSpec((8, LANES), lambda i: (0, 0))],
        out_shape=[_sds((t, d), F32), _sds((1, d), F32), _sds((8, LANES), F32)],
        compiler_params=_cparams(1))(x, gain, target)


def ffn_in_swiglu(hn, wa, s, tm, name):
    t = hn.shape[0]
    halves = 2 if tm % 512 == 0 else 1
    rows = tm // halves

    def body(h_ref, wg_ref, wu_ref, gu_ref, act_ref):
        for c in range(halves):
            rs = slice(c * rows, (c + 1) * rows)
            h = h_ref[rs, :]
            g = jnp.dot(h, wg_ref[...], preferred_element_type=F32)
            u = jnp.dot(h, wu_ref[...], preferred_element_type=F32)
            gu_ref[0, rs, :] = g.astype(gu_ref.dtype)
            gu_ref[1, rs, :] = u.astype(gu_ref.dtype)
            act_ref[rs, :] = (g * _sigmoid(g) * u).astype(act_ref.dtype)

    return pl.pallas_call(
        body, name=name, grid=(t // tm, 4),
        in_specs=[pl.BlockSpec((tm, D_MODEL), lambda i, j: (i, 0)),
                  pl.BlockSpec((None, None, D_MODEL, FF_BLK), lambda i, j: (s, j, 0, 0)),
                  pl.BlockSpec((None, None, D_MODEL, FF_BLK), lambda i, j: (s, j + 4, 0, 0))],
        out_specs=[pl.BlockSpec((None, 2, tm, FF_BLK), lambda i, j: (j, 0, i, 0)),
                   pl.BlockSpec((None, tm, FF_BLK), lambda i, j: (j, i, 0))],
        out_shape=[_sds((4, 2, t, FF_BLK), BF16), _sds((4, t, FF_BLK), BF16)],
        compiler_params=_cparams(2))(hn, wa, wa)


def ffn_dact_swiglu(dy, wb, gu, s, tm, name):
    t = dy.shape[0]

    def body(dy_ref, w_ref, gu_ref, o_ref):
        da = 0.5 * lax.dot_general(dy_ref[...].astype(BF16), w_ref[...], NT, preferred_element_type=F32)
        g = gu_ref[0].astype(F32)
        u = gu_ref[1].astype(F32)
        sg = _sigmoid(g)
        o_ref[0] = (da * u * (sg * (1.0 + g * (1.0 - sg)))).astype(o_ref.dtype)
        o_ref[1] = (da * g * sg).astype(o_ref.dtype)

    blk = pl.BlockSpec((None, 2, tm, FF_BLK), lambda i, j: (j, 0, i, 0))
    return pl.pallas_call(
        body, name=name, grid=(t // tm, 4),
        in_specs=[pl.BlockSpec((tm, D_MODEL), lambda i, j: (i, 0)),
                  pl.BlockSpec((None, None, FF_BLK, D_MODEL), lambda i, j: (s, j, 0, 0)), blk],
        out_specs=blk, out_shape=_sds((4, 2, t, FF_BLK), BF16),
        compiler_params=_cparams(2))(dy, wb, gu)


def ffn_out_residual(act, wb, x, s, tm, name, deps=()):
    t = x.shape[0]

    def body(a_ref, w_ref, x_ref, *rest):
        o_ref = rest[-1]
        acc = jnp.dot(a_ref[0], w_ref[0], preferred_element_type=F32)
        for k in range(1, 4):
            acc = acc + jnp.dot(a_ref[k], w_ref[k], preferred_element_type=F32)
        o_ref[...] = x_ref[...] + 0.5 * acc

    row = pl.BlockSpec((tm, D_MODEL), lambda i: (i, 0))
    return pl.pallas_call(
        body, name=name, grid=(t // tm,),
        in_specs=[pl.BlockSpec((4, tm, FF_BLK), lambda i: (0, i, 0)),
                  pl.BlockSpec((None, 4, FF_BLK, D_MODEL), lambda i: (s, 0, 0, 0)), row] + [ANY] * len(deps),
        out_specs=row, out_shape=_sds((t, D_MODEL), F32), compiler_params=_cparams(1))(act, wb, x, *deps)


def ffn_dh_norm_bwd(dgu, wa, s, x, gain, dres, tm, name, deps):
    t = dgu.shape[2]
    nd = len(deps)

    def body(g_ref, w_ref, x_ref, gain_ref, dres_ref, *rest):
        dx_ref, dg_ref = rest[nd:]
        i = pl.program_id(0)
        dh = lax.dot_general(g_ref[0, 0], w_ref[0], NT, preferred_element_type=F32)
        for p in range(1, N_DEV):
            dh = dh + lax.dot_general(g_ref[p % 4, p // 4], w_ref[p], NT, preferred_element_type=F32)
        xf = x_ref[...]
        r = lax.rsqrt(jnp.mean(xf * xf, axis=-1, keepdims=True) + RMS_EPS)
        xhat = xf * r
        dxhat = dh * gain_ref[...]
        dx_ref[...] = dres_ref[...] + r * (dxhat - xhat * jnp.mean(dxhat * xhat, axis=-1, keepdims=True))
        dg = jnp.sum(dh * xhat, axis=0, keepdims=True)

        @pl.when(i == 0)
        def _():
            dg_ref[...] = dg

        @pl.when(i > 0)
        def _():
            dg_ref[...] += dg

    row = pl.BlockSpec((tm, D_MODEL), lambda i: (i, 0))
    vec = pl.BlockSpec((1, D_MODEL), lambda i: (0, 0))
    return pl.pallas_call(
        body, name=name, grid=(t // tm,),
        in_specs=[pl.BlockSpec((4, 2, tm, FF_BLK), lambda i: (0, 0, i, 0)),
                  pl.BlockSpec((None, N_DEV, D_MODEL, FF_BLK), lambda i: (s, 0, 0, 0)), row, vec, row]
        + [ANY] * nd,
        out_specs=[row, vec], out_shape=[_sds((t, D_MODEL), F32), _sds((1, D_MODEL), F32)],
        compiler_params=_cparams(1))(dgu, wa, x, gain, dres, *deps)


def merge_fwd(gates, ya, yb, yc, tm, name):
    t, d = ya.shape

    def body(ga_ref, gb_ref, gc_ref, ya_ref, yb_ref, yc_ref, o_ref):
        m = (_sigmoid(ga_ref[...]) * ya_ref[...] + _sigmoid(gb_ref[...]) * yb_ref[...]
             + _sigmoid(gc_ref[...]) * yc_ref[...])
        o_ref[...] = m.astype(o_ref.dtype)

    row = pl.BlockSpec((tm, d), lambda i: (i, 0))
    gspecs = [pl.BlockSpec((tm, d), functools.partial(lambda i, a: (i, a), a=a)) for a in range(3)]
    return pl.pallas_call(
        body, name=name, grid=(t // tm,), in_specs=gspecs + [row, row, row], out_specs=row,
        out_shape=_sds((t, d), BF16), compiler_params=_cparams(1))(gates, gates, gates, ya, yb, yc)


def branch_merge(os_, wbr, layer, gates, tm, name, deps=()):
    t = os_[0].shape[0]
    d = D_MODEL
    nd = len(deps)
    widths = [o.shape[1] for o in os_]
    starts = [sum(widths[:a]) for a in range(3)]

    def body(oa_ref, ob_ref, oc_ref, w_ref, g_ref, *rest):
        ya_ref, yb_ref, yc_ref, m_ref = rest[nd:]
        merged = None
        for a, (o_ref, y_ref) in enumerate(((oa_ref, ya_ref), (ob_ref, yb_ref), (oc_ref, yc_ref))):
            y = jnp.dot(o_ref[...], w_ref[starts[a]:starts[a] + widths[a], :], preferred_element_type=F32)
            y_ref[...] = y
            term = _sigmoid(g_ref[:, a * d:(a + 1) * d]) * y
            merged = term if merged is None else merged + term
        m_ref[...] = merged.astype(m_ref.dtype)

    row = pl.BlockSpec((tm, d), lambda i: (i, 0))
    ya, yb, yc, merged = pl.pallas_call(
        body, name=name, grid=(t // tm,),
        in_specs=[pl.BlockSpec((tm, w), lambda i: (i, 0)) for w in widths]
        + [pl.BlockSpec((None, d, d), lambda i: (layer, 0, 0)), pl.BlockSpec((tm, 3 * d), lambda i: (i, 0))]
        + [ANY] * nd,
        out_specs=[row, row, row, row],
        out_shape=[_sds((t, d), F32)] * 3 + [_sds((t, d), BF16)],
        compiler_params=_cparams(1))(*os_, wbr, gates, *deps)
    return [ya, yb, yc], merged


def dmerged_merge_bwd(dy, wout, layer, gates, ys, tm, name):
    t, d = dy.shape

    def body(dy_ref, w_ref, g_ref, ya_ref, yb_ref, yc_ref, dg_ref, dya_ref, dyb_ref, dyc_ref):
        dmv = lax.dot_general(dy_ref[...].astype(BF16), w_ref[...], NT, preferred_element_type=F32)
        for a, (y_ref, dy_out) in enumerate(((ya_ref, dya_ref), (yb_ref, dyb_ref), (yc_ref, dyc_ref))):
            cols = slice(a * d, (a + 1) * d)
            s = _sigmoid(g_ref[:, cols])
            dy_out[...] = (dmv * s).astype(dy_out.dtype)
            dg_ref[:, cols] = (dmv * y_ref[...] * s * (1.0 - s)).astype(dg_ref.dtype)

    row = pl.BlockSpec((tm, d), lambda i: (i, 0))
    wide = pl.BlockSpec((tm, 3 * d), lambda i: (i, 0))
    dg, dya, dyb, dyc = pl.pallas_call(
        body, name=name, grid=(t // tm,),
        in_specs=[row, pl.BlockSpec((None, d, d), lambda i: (layer, 0, 0)), wide, row, row, row],
        out_specs=[wide, row, row, row],
        out_shape=[_sds((t, 3 * d), BF16)] + [_sds((t, d), BF16)] * 3,
        compiler_params=_cparams(1))(dy, wout, gates, *ys)
    return dg, [dya, dyb, dyc]


def merge_bwd(dm, gates, ya, yb, yc, tm, name):
    t, d = ya.shape

    def body(dm_ref, g_ref, ya_ref, yb_ref, yc_ref, dg_ref, dya_ref, dyb_ref, dyc_ref):
        dmv = dm_ref[...]
        for a, (y_ref, dy_ref) in enumerate(((ya_ref, dya_ref), (yb_ref, dyb_ref), (yc_ref, dyc_ref))):
            cols = slice(a * d, (a + 1) * d)
            s = _sigmoid(g_ref[:, cols])
            dy_ref[...] = (dmv * s).astype(dy_ref.dtype)
            dg_ref[:, cols] = (dmv * y_ref[...] * s * (1.0 - s)).astype(dg_ref.dtype)

    row = pl.BlockSpec((tm, d), lambda i: (i, 0))
    wide = pl.BlockSpec((tm, 3 * d), lambda i: (i, 0))
    dg, dya, dyb, dyc = pl.pallas_call(
        body, name=name, grid=(t // tm,), in_specs=[row, wide, row, row, row],
        out_specs=[wide, row, row, row],
        out_shape=[_sds((t, 3 * d), BF16)] + [_sds((t, d), BF16)] * 3,
        compiler_params=_cparams(1))(dm, gates, ya, yb, yc)
    return dg, [dya, dyb, dyc]


def _iota2(shape, dim):
    return lax.broadcasted_iota(jnp.int32, shape, dim)


def forget_cumsum(f, name):
    t = f.shape[0]
    nq = t // QB

    def body(f_ref, fcol_ref, frow_ref, carry):
        j = pl.program_id(0)

        @pl.when(j == 0)
        def _():
            carry[...] = jnp.zeros_like(carry)

        logf = _log_sigmoid(f_ref[...])
        tri = (_iota2((QB, QB), 1) <= _iota2((QB, QB), 0)).astype(F32)
        blk = jnp.dot(tri, logf, precision=HIGHEST, preferred_element_type=F32) + carry[...]
        carry[...] += jnp.sum(logf, axis=0, keepdims=True)
        fcol_ref[...] = blk
        frow_ref[...] = blk.T[0:8, :]

    return pl.pallas_call(
        body, name=name, grid=(nq,),
        in_specs=[pl.BlockSpec((QB, LANES), lambda j: (j, 0))],
        out_specs=[pl.BlockSpec((QB, LANES), lambda j: (j, 0)),
                   pl.BlockSpec((None, 8, QB), lambda j: (j, 0, 0))],
        out_shape=[_sds((t, LANES), F32), _sds((nq, 8, QB), F32)],
        scratch_shapes=[pltpu.VMEM((1, LANES), F32)], compiler_params=_cparams(1))(f)


def forget_cumsum_bwd(dfrow, f, name):
    t = f.shape[0]
    nq = t // QB

    def body(dfr_ref, f_ref, df_ref, carry):
        jj = pl.program_id(0)

        @pl.when(jj == 0)
        def _():
            carry[...] = jnp.zeros_like(carry)

        padded = jnp.concatenate([dfr_ref[...], jnp.zeros((QB - 8, QB), F32)], axis=0)
        dfcol = padded.T
        tri = (_iota2((QB, QB), 1) >= _iota2((QB, QB), 0)).astype(F32)
        dlogf = jnp.dot(tri, dfcol, precision=HIGHEST, preferred_element_type=F32) + carry[...]
        carry[...] += jnp.sum(dfcol, axis=0, keepdims=True)
        df_ref[...] = dlogf * _sigmoid(-f_ref[...])

    return pl.pallas_call(
        body, name=name, grid=(nq,),
        in_specs=[pl.BlockSpec((None, 8, QB), lambda jj: (nq - 1 - jj, 0, 0)),
                  pl.BlockSpec((QB, LANES), lambda jj: (nq - 1 - jj, 0))],
        out_specs=pl.BlockSpec((QB, LANES), lambda jj: (nq - 1 - jj, 0)),
        out_shape=_sds((t, LANES), F32),
        scratch_shapes=[pltpu.VMEM((1, LANES), F32)], compiler_params=_cparams(1))(dfrow, f)


REL_DIAG = 768
REL_SHIFT = REL_DIAG - (QB - 1)


def _diag_onehot():
    u = _iota2((REL_PAD, REL_DIAG), 1)
    rel = jnp.clip(CH_KEYS - 1 - u, -MAX_REL, MAX_REL) + MAX_REL
    return (_iota2((REL_PAD, REL_DIAG), 0) == rel).astype(F32)


def rel_bias_build(tab_t, name):
    def body(tab_ref, o_ref):
        diag = jnp.dot(tab_ref[...], _diag_onehot(), precision=HIGHEST, preferred_element_type=F32)
        band = _chunk_band()
        for h in range(N_HEADS_CH):
            rows = jnp.broadcast_to(diag[h:h + 1, :], (QB, REL_DIAG))
            o_ref[h] = pltpu.roll(rows, REL_SHIFT, 1, stride=1, stride_axis=0)[:, :CH_KEYS] + band

    return pl.pallas_call(
        body, name=name, out_shape=_sds((N_HEADS_CH, QB, CH_KEYS), F32),
        in_specs=[pl.BlockSpec(memory_space=pltpu.VMEM)], out_specs=pl.BlockSpec(memory_space=pltpu.VMEM),
    )(tab_t)


def rel_bias_scatter(dbias, name):
    def body(db_ref, o_ref, ddiag):
        flip = (_iota2((QB, QB), 0) + _iota2((QB, QB), 1) == QB - 1).astype(F32)
        for h in range(N_HEADS_CH):
            padded = jnp.concatenate([db_ref[h], jnp.zeros((QB, REL_DIAG - CH_KEYS), F32)], axis=1)
            flipped = jnp.dot(flip, padded, precision=HIGHEST, preferred_element_type=F32)
            unrolled = pltpu.roll(flipped, 0, 1, stride=1, stride_axis=0)
            ddiag[h:h + 1, :] = jnp.sum(unrolled, axis=0, keepdims=True)
        o_ref[...] = lax.dot_general(ddiag[...], _diag_onehot(), NT, precision=HIGHEST,
                                     preferred_element_type=F32)

    return pl.pallas_call(
        body, name=name, out_shape=_sds((N_HEADS_CH, REL_PAD), F32),
        in_specs=[pl.BlockSpec(memory_space=pltpu.VMEM)], out_specs=pl.BlockSpec(memory_space=pltpu.VMEM),
        scratch_shapes=[pltpu.VMEM((N_HEADS_CH, REL_DIAG), F32)],
    )(dbias)


def _hl(h):
    return slice(h * HEAD_DIM, (h + 1) * HEAD_DIM)


def _split_dot(x, tri_bf16):
    hi = x.astype(BF16)
    lo = (x - hi.astype(F32)).astype(BF16)
    return (jnp.dot(hi, tri_bf16, preferred_element_type=F32)
            + jnp.dot(lo, tri_bf16, preferred_element_type=F32))


def _rows(j):
    return pl.ds(pl.multiple_of(j * QB, QB), QB)


def _krows(g):
    return pl.ds(pl.multiple_of(g * KB, KB), KB)


def _log_sigmoid_pair(z):
    sp = jnp.log(1.0 + jnp.exp(-jnp.abs(z)))
    return jnp.minimum(z, 0.0) - sp, -jnp.maximum(z, 0.0) - sp


def _qkv_specs(t, col0, n_pairs):
    q_spec = pl.BlockSpec((QB, LANES), lambda hp, i: (i, col0 + hp))
    k_spec = pl.BlockSpec((t, LANES), lambda hp, i: (0, col0 + n_pairs + hp))
    v_spec = pl.BlockSpec((t, LANES), lambda hp, i: (0, col0 + 2 * n_pairs + hp))
    return q_spec, k_spec, v_spec


def _keys_major(xt):
    pairs, groups, _, _ = xt.shape
    return xt.transpose(1, 3, 0, 2).reshape(groups * KB, pairs * LANES)


def sb_fwd(qkv, name):
    t = qkv.shape[0]
    nq = t // QB

    def body(q_ref, k_ref, v_ref, o_ref, w_ref):
        i = pl.program_id(1)
        groups = i // KSUB + 1
        tri_after = (_iota2((KB, KB), 0) > _iota2((KB, KB), 1)).astype(BF16)
        t_idx = i * QB + _iota2((QB, KB), 0)
        qs = [q_ref[:, _hl(h)] for h in range(2)]

        def step(g, carry, masked):
            strict = (g * KB + _iota2((QB, KB), 1)) < t_idx
            out = []
            for h in range(2):
                tail, acc = carry[2 * h], carry[2 * h + 1]
                k = k_ref[_krows(g), _hl(h)]
                v = v_ref[_krows(g), _hl(h)]
                z = lax.dot_general(qs[h], k, NT, preferred_element_type=F32)
                lb, lf = _log_sigmoid_pair(z)
                if masked:
                    lf = jnp.where(strict, lf, 0.0)
                between = _split_dot(lf, tri_after) + tail
                w = jnp.exp(lb + between)
                if masked:
                    w = jnp.where(strict, w, 0.0)
                w = w.astype(BF16)
                w_ref[h, g] = w
                acc = acc + jnp.dot(w, v, preferred_element_type=F32)
                out += [tail + jnp.sum(lf, axis=1, keepdims=True), acc]
            return tuple(out)

        init = (jnp.zeros((QB, 1), F32), jnp.zeros((QB, HEAD_DIM), F32)) * 2
        res = step(groups - 1, init, True)
        res = lax.fori_loop(0, groups - 1, lambda gg, c: step(groups - 2 - gg, c, False), res)
        for h in range(2):
            o_ref[:, _hl(h)] = res[2 * h + 1].astype(o_ref.dtype)

    q_spec, k_spec, v_spec = _qkv_specs(t, 0, 2)
    return pl.pallas_call(
        body, name=name, grid=(2, nq), in_specs=[q_spec, k_spec, v_spec],
        out_specs=[pl.BlockSpec((QB, LANES), lambda hp, i: (i, hp)),
                   pl.BlockSpec((2, None, t // KB, QB, KB), lambda hp, i: (hp, i, 0, 0, 0))],
        out_shape=[_sds((t, W_SB), BF16), _sds((4, nq, t // KB, QB, KB), BF16)],
        compiler_params=_cparams(2))(qkv, qkv, qkv)


def _hs(h):
    return slice(h * HEAD_DIM, (h + 1) * HEAD_DIM)


def sb_bwd(qkv, qkv_t, w, do, do_t, name):
    t = qkv.shape[0]
    nq = t // QB

    def body(q_ref, k_ref, v_ref, do_ref, qt_ref, dot_ref, w_ref, dq_ref, dkt_ref, dvt_ref):
        i = pl.program_id(1)

        @pl.when(i == 0)
        def _():
            dkt_ref[...] = jnp.zeros_like(dkt_ref)
            dvt_ref[...] = jnp.zeros_like(dvt_ref)

        groups = i // KSUB + 1
        tri_before = (_iota2((KB, KB), 0) < _iota2((KB, KB), 1)).astype(BF16)
        t_idx = i * QB + _iota2((QB, KB), 0)
        qs = [q_ref[:, _hl(h)] for h in range(2)]
        dos = [do_ref[:, _hl(h)] for h in range(2)]
        qts = [qt_ref[_hs(h), :] for h in range(2)]
        dots = [dot_ref[_hs(h), :] for h in range(2)]

        def grads(g, carry, masked):
            strict = (g * KB + _iota2((QB, KB), 1)) < t_idx
            out = []
            for h in range(2):
                head, dq = carry[2 * h], carry[2 * h + 1]
                k = k_ref[_krows(g), _hl(h)]
                v = v_ref[_krows(g), _hl(h)]
                wb = w_ref[h, g]
                z = lax.dot_general(qs[h], k, NT, preferred_element_type=F32)
                beta = _sigmoid(z)
                e = lax.dot_general(dos[h], v, NT, preferred_element_type=F32) * wb.astype(F32)
                before = _split_dot(e, tri_before) + head
                dz = e * (1.0 - beta) - before * beta
                if masked:
                    dz = jnp.where(strict, dz, 0.0)
                dzb = dz.astype(BF16)
                dq = dq + jnp.dot(dzb, k, preferred_element_type=F32)
                dkt_ref[g, _hs(h), :] += jnp.dot(qts[h], dzb, preferred_element_type=F32)
                dvt_ref[g, _hs(h), :] += jnp.dot(dots[h], wb, preferred_element_type=F32)
                out += [head + jnp.sum(e, axis=1, keepdims=True), dq]
            return tuple(out)

        init = (jnp.zeros((QB, 1), F32), jnp.zeros((QB, HEAD_DIM), F32)) * 2
        res = lax.fori_loop(0, groups - 1, lambda g, c: grads(g, c, False), init)
        res = grads(groups - 1, res, True)
        for h in range(2):
            dq_ref[:, _hl(h)] = (res[2 * h + 1] * SCALE).astype(dq_ref.dtype)

    q_spec, k_spec, v_spec = _qkv_specs(t, 0, 2)
    blk = pl.BlockSpec((QB, LANES), lambda hp, i: (i, hp))
    blk_t = pl.BlockSpec((LANES, QB), lambda hp, i: (hp, i))
    acc_t = pl.BlockSpec((None, t // KB, LANES, KB), lambda hp, i: (hp, 0, 0, 0))
    acc_sds = _sds((2, t // KB, LANES, KB), F32)
    return pl.pallas_call(
        body, name=name, grid=(2, nq),
        in_specs=[q_spec, k_spec, v_spec, blk, blk_t, blk_t,
                  pl.BlockSpec((2, None, t // KB, QB, KB), lambda hp, i: (hp, i, 0, 0, 0))],
        out_specs=[blk, acc_t, acc_t],
        out_shape=[_sds((t, W_SB), BF16), acc_sds, acc_sds],
        compiler_params=_cparams(2))(qkv, qkv, qkv, do, qkv_t, do_t, w)


def fox_fwd(qkv, fcol, frow, name):
    t = qkv.shape[0]
    nq = t // QB

    def body(q_ref, k_ref, v_ref, fc_ref, fr_ref, o_ref, lse_ref):
        hp = pl.program_id(0)
        i = pl.program_id(1)
        groups = i // KSUB + 1
        t_idx = i * QB + _iota2((QB, KB), 0)
        lane = _iota2((QB, LANES), 1)
        sub = _iota2((8, KB), 0)
        qs = [q_ref[:, _hl(h)] for h in range(2)]
        f_qs = [jnp.sum(jnp.where(lane == hp * 2 + h, fc_ref[...], 0.0), axis=1, keepdims=True)
                for h in range(2)]

        def step(g, carry, masked):
            causal = (g * KB + _iota2((QB, KB), 1)) <= t_idx
            fr = fr_ref[g]
            out = []
            for h in range(2):
                m, l, acc = carry[3 * h:3 * h + 3]
                k = k_ref[_krows(g), _hl(h)]
                v = v_ref[_krows(g), _hl(h)]
                f_k = jnp.sum(jnp.where(sub == hp * 2 + h, fr, 0.0), axis=0, keepdims=True)
                z = lax.dot_general(qs[h], k, NT, preferred_element_type=F32) + f_qs[h] - f_k
                if masked:
                    z = jnp.where(causal, z, NEG)
                m_new = jnp.maximum(m, jnp.max(z, axis=1, keepdims=True))
                p = jnp.exp(z - m_new)
                corr = jnp.exp(m - m_new)
                l = l * corr + jnp.sum(p, axis=1, keepdims=True)
                acc = acc * corr + jnp.dot(p.astype(BF16), v, preferred_element_type=F32)
                out += [m_new, l, acc]
            return tuple(out)

        init = (jnp.full((QB, 1), NEG, F32), jnp.zeros((QB, 1), F32), jnp.zeros((QB, HEAD_DIM), F32)) * 2
        res = lax.fori_loop(0, groups - 1, lambda g, c: step(g, c, False), init)
        res = step(groups - 1, res, True)
        for h in range(2):
            m, l, acc = res[3 * h:3 * h + 3]
            o_ref[:, _hl(h)] = (acc / l).astype(o_ref.dtype)
            lse_ref[:, _hl(h)] = jnp.broadcast_to(m + jnp.log(l), (QB, HEAD_DIM))

    q_spec, k_spec, v_spec = _qkv_specs(t, 18, 2)
    blk = pl.BlockSpec((QB, LANES), lambda hp, i: (i, hp))
    return pl.pallas_call(
        body, name=name, grid=(2, nq),
        in_specs=[q_spec, k_spec, v_spec, pl.BlockSpec((QB, LANES), lambda hp, i: (i, 0)),
                  pl.BlockSpec((t // KB, 8, KB), lambda hp, i: (0, 0, 0))],
        out_specs=[blk, blk],
        out_shape=[_sds((t, W_FOX), BF16), _sds((t, W_FOX), F32)],
        compiler_params=_cparams(2))(qkv, qkv, qkv, fcol, frow)


def fox_bwd(qkv, qkv_t, fcol, frow, o, lse, do, do_t, name):
    t = qkv.shape[0]
    nq = t // QB

    def body(q_ref, k_ref, v_ref, fc_ref, fr_ref, o_ref, lse_ref, do_ref, qt_ref, dot_ref,
             dq_ref, dk_ref, dv_ref, dfr_ref):
        hp = pl.program_id(0)
        i = pl.program_id(1)
        qts = [qt_ref[_hs(h), :] for h in range(2)]
        dots = [dot_ref[_hs(h), :] for h in range(2)]

        @pl.when(i == 0)
        def _():
            dk_ref[...] = jnp.zeros_like(dk_ref)
            dv_ref[...] = jnp.zeros_like(dv_ref)

        @pl.when((i == 0) & (hp == 0))
        def _():
            dfr_ref[...] = jnp.zeros_like(dfr_ref)

        groups = i // KSUB + 1
        t_idx = i * QB + _iota2((QB, KB), 0)
        lane = _iota2((QB, LANES), 1)
        sub = _iota2((8, KB), 0)
        qs = [q_ref[:, _hl(h)] for h in range(2)]
        dos = [do_ref[:, _hl(h)] for h in range(2)]
        f_qs = [jnp.sum(jnp.where(lane == hp * 2 + h, fc_ref[...], 0.0), axis=1, keepdims=True)
                for h in range(2)]
        lse_qs = [lse_ref[:, h * HEAD_DIM:h * HEAD_DIM + 1] for h in range(2)]
        deltas = [jnp.sum(dos[h].astype(F32) * o_ref[:, _hl(h)].astype(F32), axis=1, keepdims=True)
                  for h in range(2)]

        def step(g, dqs, masked):
            causal = (g * KB + _iota2((QB, KB), 1)) <= t_idx
            fr = fr_ref[g]
            out = []
            dfr = jnp.zeros((8, KB), F32)
            for h in range(2):
                k = k_ref[_krows(g), _hl(h)]
                v = v_ref[_krows(g), _hl(h)]
                f_k = jnp.sum(jnp.where(sub == hp * 2 + h, fr, 0.0), axis=0, keepdims=True)
                z = lax.dot_general(qs[h], k, NT, preferred_element_type=F32) + f_qs[h] - f_k
                p = jnp.exp(z - lse_qs[h])
                if masked:
                    p = jnp.where(causal, p, 0.0)
                dp = lax.dot_general(dos[h], v, NT, preferred_element_type=F32)
                ds = p * (dp - deltas[h])
                dsb = ds.astype(BF16)
                out.append(dqs[h] + jnp.dot(dsb, k, preferred_element_type=F32))
                dk_ref[g, _hs(h), :] += jnp.dot(qts[h], dsb, preferred_element_type=F32)
                dv_ref[g, _hs(h), :] += jnp.dot(dots[h], p.astype(BF16), preferred_element_type=F32)
                colsum = jnp.sum(ds, axis=0, keepdims=True)
                dfr = dfr + jnp.where(sub == hp * 2 + h, -colsum, 0.0)
            dfr_ref[g] += dfr
            return tuple(out)

        res = lax.fori_loop(0, groups - 1, lambda g, c: step(g, c, False),
                            (jnp.zeros((QB, HEAD_DIM), F32),) * 2)
        res = step(groups - 1, res, True)
        for h in range(2):
            dq_ref[:, _hl(h)] = (res[h] * SCALE).astype(dq_ref.dtype)

    q_spec, k_spec, v_spec = _qkv_specs(t, 18, 2)
    blk = pl.BlockSpec((QB, LANES), lambda hp, i: (i, hp))
    frs = pl.BlockSpec((t // KB, 8, KB), lambda hp, i: (0, 0, 0))
    acc_t = pl.BlockSpec((None, t // KB, LANES, KB), lambda hp, i: (hp, 0, 0, 0))
    acc_sds = _sds((2, t // KB, LANES, KB), F32)
    return pl.pallas_call(
        body, name=name, grid=(2, nq),
        in_specs=[q_spec, k_spec, v_spec, pl.BlockSpec((QB, LANES), lambda hp, i: (i, 0)), frs,
                  blk, blk, blk, pl.BlockSpec((LANES, QB), lambda hp, i: (18 + hp, i)),
                  pl.BlockSpec((LANES, QB), lambda hp, i: (hp, i))],
        out_specs=[blk, acc_t, acc_t, frs],
        out_shape=[_sds((t, W_FOX), BF16), acc_sds, acc_sds, _sds((t // KB, 8, KB), F32)],
        compiler_params=_cparams(2))(qkv, qkv, qkv, fcol, frow, o, lse, do, qkv_t, do_t)


def _frow_to_groups(frow):
    n = frow.shape[0] // KSUB
    return frow.reshape(n, KSUB, 8, QB).transpose(0, 2, 1, 3).reshape(n, 8, KB)


def _frow_from_groups(frow):
    n = frow.shape[0]
    return frow.reshape(n, 8, KSUB, QB).transpose(0, 2, 1, 3).reshape(n * KSUB, 8, QB)


def _chunk_band():
    qi = _iota2((QB, CH_KEYS), 0)
    kj = _iota2((QB, CH_KEYS), 1)
    dchunk = (qi >> 6) + LEFT_CHUNKS - (kj >> 6)
    return jnp.where((dchunk >= 0) & (dchunk <= LEFT_CHUNKS), 0.0, NEG)


def _chunk_pad_row(i):
    kj = _iota2((1, CH_KEYS), 1)
    return jnp.where((i - (CH_WIN - 1)) * QB + kj >= 0, 0.0, NEG)


CH_PAD = (CH_WIN - 1) * QB
CH_STEP_HEADS = 4
CH_COLS = CH_STEP_HEADS * HEAD_DIM


def _window(i):
    return pl.ds(pl.multiple_of(i * QB, QB), CH_KEYS)


def _chunk_weights(q, kw, bias, pad_row):
    z = lax.dot_general(q, kw, NT, preferred_element_type=F32) + bias + pad_row
    e = jnp.exp(z - jnp.max(z, axis=1, keepdims=True))
    return e, 1.0 / jnp.sum(e, axis=1, keepdims=True)


def _chunk_specs(t):
    q_spec = pl.BlockSpec((QB, CH_COLS), lambda hp, i: (i, 3 * W_SB // CH_COLS + hp))
    kv_spec = pl.BlockSpec((t + CH_PAD, CH_COLS), lambda hp, i: (0, hp))
    return q_spec, kv_spec


def chunk_fwd(qkv, kp, vp, bias, name):
    t = qkv.shape[0]
    nq = t // QB

    def body(q_ref, k_ref, v_ref, b_ref, o_ref):
        i = pl.program_id(1)
        pad_row = _chunk_pad_row(i)
        for h in range(CH_STEP_HEADS):
            e, inv = _chunk_weights(q_ref[:, _hl(h)], k_ref[_window(i), _hl(h)], b_ref[h], pad_row)
            o = jnp.dot(e.astype(BF16), v_ref[_window(i), _hl(h)], preferred_element_type=F32)
            o_ref[:, _hl(h)] = (o * inv).astype(o_ref.dtype)

    q_spec, kv_spec = _chunk_specs(t)
    return pl.pallas_call(
        body, name=name, grid=(W_CH // CH_COLS, nq),
        in_specs=[q_spec, kv_spec, kv_spec,
                  pl.BlockSpec((CH_STEP_HEADS, QB, CH_KEYS), lambda hp, i: (hp, 0, 0))],
        out_specs=pl.BlockSpec((QB, CH_COLS), lambda hp, i: (i, hp)),
        out_shape=_sds((t, W_CH), BF16), compiler_params=_cparams(2))(qkv, kp, vp, bias)


def chunk_bwd(qkv, qkv_t, kp, vp, bias, do, do_t, name):
    t = qkv.shape[0]
    nq = t // QB

    def body(q_ref, k_ref, v_ref, b_ref, do_ref, qt_ref, dot_ref, dq_ref, dk_ref, dv_ref, db_ref):
        i = pl.program_id(1)

        @pl.when(i == 0)
        def _():
            dk_ref[...] = jnp.zeros_like(dk_ref)
            dv_ref[...] = jnp.zeros_like(dv_ref)
            db_ref[...] = jnp.zeros_like(db_ref)

        pad_row = _chunk_pad_row(i)
        for h in range(CH_STEP_HEADS):
            q = q_ref[:, _hl(h)]
            dov = do_ref[:, _hl(h)]
            kw = k_ref[_window(i), _hl(h)]
            e, inv = _chunk_weights(q, kw, b_ref[h], pad_row)
            p = e * inv
            dp = lax.dot_general(dov, v_ref[_window(i), _hl(h)], NT, preferred_element_type=F32)
            ds = p * (dp - jnp.sum(p * dp, axis=1, keepdims=True))
            db_ref[h] += ds
            dsb = ds.astype(BF16)
            dq_ref[:, _hl(h)] = (jnp.dot(dsb, kw, preferred_element_type=F32) * SCALE).astype(dq_ref.dtype)
            dkt = jnp.dot(qt_ref[_hs(h), :], dsb, preferred_element_type=F32)
            dvt = jnp.dot(dot_ref[_hs(h), :], p.astype(BF16), preferred_element_type=F32)
            for b in range(CH_WIN):
                dk_ref[i + b, _hs(h), :] += dkt[:, b * QB:(b + 1) * QB]
                dv_ref[i + b, _hs(h), :] += dvt[:, b * QB:(b + 1) * QB]

    q_spec, kv_spec = _chunk_specs(t)
    blk = pl.BlockSpec((QB, CH_COLS), lambda hp, i: (i, hp))
    bspec = pl.BlockSpec((CH_STEP_HEADS, QB, CH_KEYS), lambda hp, i: (hp, 0, 0))
    nblk = nq + CH_WIN - 1
    acc_t = pl.BlockSpec((None, nblk, CH_COLS, QB), lambda hp, i: (hp, 0, 0, 0))
    acc_sds = _sds((W_CH // CH_COLS, nblk, CH_COLS, QB), F32)
    return pl.pallas_call(
        body, name=name, grid=(W_CH // CH_COLS, nq),
        in_specs=[q_spec, kv_spec, kv_spec, bspec, blk,
                  pl.BlockSpec((CH_COLS, QB), lambda hp, i: (3 * W_SB // CH_COLS + hp, i)),
                  pl.BlockSpec((CH_COLS, QB), lambda hp, i: (hp, i))],
        out_specs=[blk, acc_t, acc_t, bspec],
        out_shape=[_sds((t, W_CH), BF16), acc_sds, acc_sds, _sds((N_HEADS_CH, QB, CH_KEYS), F32)],
        compiler_params=_cparams(2))(qkv, kp, vp, bias, do, qkv_t, do_t)


def _sum_parts(p_ref):
    total = p_ref[0].astype(F32)
    for p in range(1, p_ref.shape[0]):
        total = total + p_ref[p].astype(F32)
    return total


def sum_parts(parts, grid, p_spec, o_spec, out_sds, name):
    def body(p_ref, o_ref):
        o_ref[...] = _sum_parts(p_ref)

    return pl.pallas_call(body, name=name, grid=grid, in_specs=[p_spec], out_specs=o_spec,
                          out_shape=out_sds, compiler_params=_cparams(len(grid)))(parts)


def adamw(parts, w, m, v, grid, p_specs, w_spec, name):
    c1 = 1.0 / (1.0 - ADAM_B1 ** ADAM_STEP)
    c2 = 1.0 / (1.0 - ADAM_B2 ** ADAM_STEP)
    n = len(parts)

    def body(*refs):
        w_ref, m_ref, v_ref, g_out, d_out, m_out, v_out = refs[n:]
        g = _sum_parts(refs[0])
        for q in range(1, n):
            g = jnp.where(pl.program_id(0) == q, _sum_parts(refs[q]), g)
        m_new = ADAM_B1 * m_ref[...] + (1.0 - ADAM_B1) * g
        v_new = ADAM_B2 * v_ref[...] + (1.0 - ADAM_B2) * (g * g)
        m_hat = m_new * c1
        v_hat = v_new * c2
        g_out[...] = g
        d_out[...] = -ADAM_LR * (m_hat / (jnp.sqrt(v_hat) + ADAM_EPS) + ADAM_WD * w_ref[...])
        m_out[...] = m_new
        v_out[...] = v_new

    out = _sds(w.shape, F32)
    return pl.pallas_call(
        body, name=name, grid=grid, in_specs=[*p_specs, w_spec, w_spec, w_spec],
        out_specs=[w_spec] * 4, out_shape=[out] * 4,
        compiler_params=_cparams(len(grid)))(*parts, w, m, v)


def _ffn_fwd(x, gain, wa, wb_after, s, tm, tag, on_event, deps=()):
    t = x.shape[0]
    hn = rmsnorm_fwd(x, gain, tm, f"rms_{tag}", deps)
    gu, act = ffn_in_swiglu(hn, wa, s, min(2 * tm, t), f"ffn_in_{tag}")
    relayed = on_event("act", act)
    wb = wb_after(act)
    y = ffn_out_residual(act, wb, x, s, min(2 * tm, t), f"ffn_out_{tag}", relayed)
    return y, (hn, gu, act), wb


def _ffn_bwd(dy, x, gain, saved, wa, wb, s, tm, tag, on_grads):
    t = x.shape[0]
    hn, gu, act = saved
    dgu = ffn_dact_swiglu(dy, wb, gu, s, min(2 * tm, t), f"ffn_dact_{tag}")
    dwb = matmul(TN, act, dy, _sds((4, FF_BLK, D_MODEL), BF16), (4, 1, 1),
                 pl.BlockSpec((None, t, FF_BLK), lambda i, j, k: (i, 0, 0)),
                 pl.BlockSpec((t, D_MODEL), lambda i, j, k: (0, 0)),
                 pl.BlockSpec((None, FF_BLK, D_MODEL), lambda i, j, k: (i, 0, 0)),
                 None, name=f"ffn_dwout_{tag}", alpha=0.5)
    dwa = matmul(TN, dgu, hn, _sds((8, FF_BLK, D_MODEL), BF16), (1, 8, 1),
                 pl.BlockSpec((None, None, t, FF_BLK), lambda i, j, k: (j % 4, j // 4, 0, 0)),
                 pl.BlockSpec((t, D_MODEL), lambda i, j, k: (0, 0)),
                 pl.BlockSpec((None, FF_BLK, D_MODEL), lambda i, j, k: (j, 0, 0)),
                 None, name=f"ffn_dwin_{tag}")
    deps = on_grads(dwa, dwb)
    return ffn_dh_norm_bwd(dgu, wa, s, x, gain, dy, tm, f"ffn_dh_{tag}", deps)


BR_ROWS = ((0, 1), (1, 2), (3, 1))

_Q_COLUMN_SCALE = np.ones((1, QKV_WIDTH), np.float32)
for _lo, _width in ((0, W_SB), (3 * W_SB, W_CH), (3 * (W_SB + W_CH), W_FOX)):
    _Q_COLUMN_SCALE[0, _lo:_lo + _width] = SCALE


def _mixer_fwd(x, gain, wqkv, wf, wgate, late_after, bq, bf, bg, bias, layer, tm, tag, on_event):
    t = x.shape[0]
    nt = t // tm
    hm = rmsnorm_fwd(x, gain, tm, f"rms_{tag}")
    a_full = pl.BlockSpec((tm, D_MODEL), lambda i, j, k: (i, 0))
    wide_out = pl.BlockSpec((tm, D_MODEL), lambda i, j, k: (i, j))
    wide_b = pl.BlockSpec((1, D_MODEL), lambda i, j, k: (0, j))
    qkv, qkv_t = matmul(NN, hm, wqkv, _sds((t, QKV_WIDTH), BF16), (nt, 3, 1), a_full,
                        pl.BlockSpec((None, D_MODEL, D_MODEL), lambda i, j, k: (layer, 0, j)), wide_out, None,
                        name=f"proj_qkv_{tag}", bias=bq, bias_spec=wide_b,
                        scale=jnp.asarray(_Q_COLUMN_SCALE), scale_spec=wide_b,
                        out_t_sds=_sds((QKV_WIDTH, t), BF16),
                        out_t_spec=pl.BlockSpec((D_MODEL, tm), lambda i, j, k: (j, i)))
    relayed = on_event("qkv", qkv)
    gates = matmul(NN, hm, wgate, _sds((t, 3 * D_MODEL), F32), (nt, 3, 1), a_full,
                   pl.BlockSpec((None, D_MODEL, D_MODEL), lambda i, j, k: (layer + 1, 0,j)), wide_out,
                   None, name=f"proj_gate_{tag}", bias=bg, bias_spec=wide_b, deps=relayed)
    f = matmul(NN, hm, wf, _sds((t, LANES), F32), (nt, 1, 1), a_full,
               pl.BlockSpec((None, D_MODEL, LANES), lambda i, j, k: (layer, 0, 0)),
               pl.BlockSpec((tm, LANES), lambda i, j, k: (i, 0)), None,
               name=f"proj_f_{tag}", bias=bf, bias_spec=pl.BlockSpec((1, LANES), lambda i, j, k: (0, 0)))
    fcol, frow = forget_cumsum(f, f"fcum_{tag}")
    frow = _frow_to_groups(frow)
    o_sb, w_sb = sb_fwd(qkv, f"sb_fwd_{tag}")
    relayed = on_event("o_sb", o_sb)
    kp = jnp.pad(qkv[:, 10 * LANES:14 * LANES], ((CH_PAD, 0), (0, 0)))
    vp = jnp.pad(qkv[:, 14 * LANES:18 * LANES], ((CH_PAD, 0), (0, 0)))
    o_ch = chunk_fwd(qkv, kp, vp, bias, f"chunk_fwd_{tag}")
    o_fox, lse = fox_fwd(qkv, fcol, frow, f"fox_fwd_{tag}")
    wbr, wout = late_after(o_fox)
    ys, merged = branch_merge((o_sb, o_ch, o_fox), wbr, layer, gates, tm, f"branch_merge_{tag}", relayed)
    x_new = matmul(NN, merged, wout, _sds((t, D_MODEL), F32), (nt, 1, 1), a_full,
                   pl.BlockSpec((None, D_MODEL, D_MODEL), lambda i, j, k: (layer, 0, 0)), a_full, None,
                   name=f"wout_{tag}", res=x, res_spec=a_full)
    saved = (hm, qkv, gates, f, fcol, frow, o_sb, o_ch, o_fox, lse, ys, merged, kp, vp, w_sb, qkv_t)
    return x_new, saved, wbr, wout


def _mixer_bwd(dy, x, gain, saved, wqkv, wf, wgate, wbr, wout, bias, layer, tm, tag, on_grads):
    t = x.shape[0]
    nt = t // tm
    hm, qkv, gates, f, fcol, frow, o_sb, o_ch, o_fox, lse, ys, merged, kp, vp, w_sb, qkv_t = saved
    a_full = pl.BlockSpec((tm, D_MODEL), lambda i, j, k: (i, 0))
    red_row = pl.BlockSpec((tm, D_MODEL), lambda i, j, k: (k, 0))
    sq = pl.BlockSpec((D_MODEL, D_MODEL), lambda i, j, k: (0, 0))
    dgates, dys = dmerged_merge_bwd(dy, wout, layer, gates, ys, tm // 2, f"dmerged_{tag}")
    all_t = pl.BlockSpec((t, D_MODEL), lambda i, j, k: (0, 0))
    dwout = matmul(TN, merged, dy, _sds((D_MODEL, D_MODEL), BF16), (1, 1, 1), all_t, all_t, sq,
                   None, name=f"dwout_{tag}")
    dos, dos_t, dwbrs = [], [], []
    for a, (o, (r0, nr)) in enumerate(zip((o_sb, o_ch, o_fox), BR_ROWS)):
        do, do_t = matmul(
            NT, dys[a], wbr, _sds((t, nr * 256), BF16), (nt, nr, 1), a_full,
            pl.BlockSpec((None, 256, D_MODEL), functools.partial(lambda i, j, k, r0: (layer, r0 + j, 0), r0=r0)),
            pl.BlockSpec((tm, 256), lambda i, j, k: (i, j)), None, name=f"dbranch{a}_{tag}",
            out_t_sds=_sds((nr * 256, t), BF16), out_t_spec=pl.BlockSpec((256, tm), lambda i, j, k: (j, i)))
        dos.append(do)
        dos_t.append(do_t)
        dwbrs.append(matmul(
            TN, o, dys[a], _sds((nr * 256, D_MODEL), BF16), (nr, 1, 1),
            pl.BlockSpec((t, 256), lambda i, j, k: (0, i)), all_t,
            pl.BlockSpec((256, D_MODEL), lambda i, j, k: (i, 0)), None, name=f"dwbr{a}_{tag}"))
    dq_a, dk_a, dv_a = sb_bwd(qkv, qkv_t, w_sb, dos[0], dos_t[0], f"sb_bwd_{tag}")
    dk_a, dv_a = _keys_major(dk_a), _keys_major(dv_a)
    dq_b, dk_b, dv_b, dbias = chunk_bwd(qkv, qkv_t, kp, vp, bias, dos[1], dos_t[1], f"chunk_bwd_{tag}")
    dk_b, dv_b = [x[:, CH_WIN - 1:].transpose(1, 3, 0, 2).reshape(t, W_CH) for x in (dk_b, dv_b)]
    dq_c, dk_c, dv_c, dfrow = fox_bwd(qkv, qkv_t, fcol, frow, o_fox, lse, dos[2], dos_t[2], f"fox_bwd_{tag}")
    dk_c, dv_c = _keys_major(dk_c), _keys_major(dv_c)
    df = forget_cumsum_bwd(_frow_from_groups(dfrow), f, f"fcum_bwd_{tag}")
    dqkv = jnp.concatenate([p.astype(BF16) for p in
                            (dq_a, dk_a, dv_a, dq_b, dk_b, dv_b, dq_c, dk_c, dv_c)], axis=1)
    dtab = rel_bias_scatter(dbias, f"rel_scatter_{tag}")

    all_rows = pl.BlockSpec((t, D_MODEL), lambda i, j, k: (0, 0))
    wide_b = pl.BlockSpec((t, D_MODEL), lambda i, j, k: (0, j))
    wide_o = pl.BlockSpec((D_MODEL, D_MODEL), lambda i, j, k: (0, j))
    wide_cs = pl.BlockSpec((1, D_MODEL), lambda i, j, k: (0, j))
    dwqkv, dbq = matmul(TN, hm, dqkv, _sds((D_MODEL, QKV_WIDTH), BF16), (1, 3, 1), all_rows, wide_b,
                        wide_o, None, name=f"dwqkv_{tag}",
                        colsum_sds=_sds((1, QKV_WIDTH), F32), colsum_spec=wide_cs)
    dwgate, dbg = matmul(TN, hm, dgates, _sds((D_MODEL, 3 * D_MODEL), BF16), (1, 3, 1), all_rows,
                         wide_b, wide_o, None, name=f"dwgate_{tag}",
                         colsum_sds=_sds((1, 3 * D_MODEL), F32), colsum_spec=wide_cs)
    dwf, dbf = matmul(TN, hm, df, _sds((D_MODEL, LANES), BF16), (1, 1, 1), all_rows,
                      pl.BlockSpec((t, LANES), lambda i, j, k: (0, 0)),
                      pl.BlockSpec((D_MODEL, LANES), lambda i, j, k: (0, 0)), None,
                      name=f"dwf_{tag}", colsum_sds=_sds((1, LANES), F32),
                      colsum_spec=pl.BlockSpec((1, LANES), lambda i, j, k: (0, 0)))
    dwbr = jnp.concatenate(dwbrs, axis=0)
    deps = on_grads(dict(dwqkv=dwqkv, dwgate=dwgate, dwf=dwf, dwbr=dwbr, dwout=dwout))
    wide_a = pl.BlockSpec((tm, QKV_WIDTH), lambda i, j, k: (i, 0))
    dhm = matmul(NT, dqkv, wqkv, _sds((t, D_MODEL), F32), (nt, 1, 1), wide_a,
                 pl.BlockSpec((None, D_MODEL, QKV_WIDTH), lambda i, j, k: (layer, 0, 0)), a_full,
                 None, name=f"dhm_qkv_{tag}", deps=deps)
    dhm = matmul(NT, dgates, wgate, _sds((t, D_MODEL), F32), (nt, 1, 1), wide_a,
                 pl.BlockSpec((None, D_MODEL, QKV_WIDTH), lambda i, j, k: (layer + 1, 0, 0)), a_full,
                 None, name=f"dhm_gate_{tag}", res=dhm, res_spec=a_full)
    dhm = matmul(NT, df, wf, _sds((t, D_MODEL), F32), (nt, 1, 1),
                 pl.BlockSpec((tm, LANES), lambda i, j, k: (i, 0)),
                 pl.BlockSpec((None, D_MODEL, LANES), lambda i, j, k: (layer, 0, 0)), a_full, None,
                 name=f"dhm_f_{tag}", res=dhm, res_spec=a_full)
    dx, dgain = rmsnorm_bwd(x, gain, dhm, dy, tm, f"rms_bwd_{tag}")
    return dx, dict(dbq=dbq, dbg=dbg, dbf=dbf, dtab=dtab, dgain=dgain)


def _pack_small(pieces):
    flat = jnp.concatenate([p.reshape(-1).astype(F32) for p in pieces])
    flat = jnp.pad(flat, (0, SMALL_ROWS * LANES - flat.shape[0]))
    return flat.reshape(SMALL_ROWS, LANES)


def _unpack_small(packed, shapes):
    flat = packed.reshape(-1)
    out, pos = [], 0
    for shp in shapes:
        n = int(np.prod(shp))
        out.append(flat[pos:pos + n].reshape(shp))
        pos += n
    return out


def kernel(x, g_ffn1, w_ffn1_in, w_ffn1_out, g_mix, w_in, b_in, rel_bias, w_br_sb, w_br_ch, w_br_fox, w_out, g_ffn2, w_ffn2_in, w_ffn2_out, g_final, loss_target, m_g_ffn1, m_w_ffn1_in, m_w_ffn1_out, m_g_mix, m_w_in, m_b_in, m_rel_bias, m_w_br_sb, m_w_br_ch, m_w_br_fox, m_w_out, m_g_ffn2, m_w_ffn2_in, m_w_ffn2_out, m_g_final, v_g_ffn1, v_w_ffn1_in, v_w_ffn1_out, v_g_mix, v_w_in, v_b_in, v_rel_bias, v_w_br_sb, v_w_br_ch, v_w_br_fox, v_w_out, v_g_ffn2, v_w_ffn2_in, v_w_ffn2_out, v_g_final):
    t = x.shape[1]
    tm = min(512, t)
    xs = x[0]
    target = loss_target[0]
    f_lo, f_hi = QKV_WIDTH, QKV_WIDTH + N_HEADS_FOX

    def ffn_shards(w_in_, w_out_, l):
        return [w_in_[l:l + 1].astype(BF16), w_out_[l:l + 1].astype(BF16)]

    def mixer_shards(l):
        wl = w_in[l]
        return [jnp.stack([wl[:, :QKV_WIDTH], wl[:, f_hi:]]).astype(BF16),
                jnp.pad(wl[:, f_lo:f_hi], ((0, 0), (0, LANES - N_HEADS_FOX)))[None].astype(BF16),
                w_out[l:l + 1].astype(BF16),
                jnp.concatenate([w_br_sb[l], w_br_ch[l], w_br_fox[l]], axis=0)[None].astype(BF16)]

    gathers = {}
    gather_tokens = []

    def start_gather(shards, name):
        handle = gather_start(shards, name, deps=gather_tokens[-1:])
        gather_tokens.append(handle["token"])
        return handle

    def relay(handle, after):
        if "send2" not in handle:
            gather_relay(handle, after)

    relay_on = {("mix", 0, "qkv"): ("mix", 0, 1), ("mix", 0, "o_sb"): ("ffn2", 0, 0),
                ("ffn2", 0, "act"): ("ffn1", 1, 0), ("ffn1", 1, "act"): ("mix", 1, 0),
                ("mix", 1, "qkv"): ("ffn2", 1, 0)}

    def on_event(grp, l):
        def fire(event, array):
            target = relay_on.get((grp, l, event))
            if target is None:
                return ()
            handle = gathers[target[:2]][target[2]]
            relay(handle, array)
            return (handle["relay_token"],)
        return fire

    for l in range(DEPTH):
        for grp, shards in (("ffn1", ffn_shards(w_ffn1_in, w_ffn1_out, l)), ("mix", mixer_shards(l)),
                            ("ffn2", ffn_shards(w_ffn2_in, w_ffn2_out, l))):
            cut = len(shards) // 2
            if l == 0 and grp != "ffn2":
                gathers[(grp, l)] = (start_gather(shards[:cut], f"gather_{grp}_l{l}_a"),
                                     start_gather(shards[cut:], f"gather_{grp}_l{l}_b"))
            else:
                gathers[(grp, l)] = (start_gather(shards, f"gather_{grp}_l{l}"),)

    def gathered(key, after):
        hs = gathers[key]
        cut = hs[0]["n"]
        relay(hs[0], after)
        first = gather_finish(hs[0], after)
        if len(hs) == 1:
            return first[:cut // 2], lambda later: first[cut // 2:]

        def second(later):
            relay(hs[1], later)
            return gather_finish(hs[1], later)

        return first, second

    def ffn_weights(key, after):
        (wa_,), rest = gathered(key, after)
        return wa_, lambda later: rest(later)[0].reshape(1, 4, FF_BLK, D_MODEL)

    def mixer_weights(key, after):
        (wc_, wf_), rest = gathered(key, after)

        def late(later):
            wout_, wbr_ = rest(later)
            return (wbr_.transpose(0, 2, 1, 3).reshape(1, D_MODEL, D_MODEL), wout_.reshape(1, D_MODEL, D_MODEL))

        return wc_.reshape(2, D_MODEL, QKV_WIDTH), wf_.reshape(1, D_MODEL, LANES), late

    bq = b_in[:, None, :QKV_WIDTH]
    bf = jnp.pad(b_in[:, f_lo:f_hi], ((0, 0), (0, LANES - N_HEADS_FOX)))[:, None, :]
    bg = b_in[:, None, f_hi:]
    tab_t = jnp.pad(rel_bias.transpose(0, 2, 1), ((0, 0), (0, 0), (0, REL_PAD - N_REL)))

    h = xs
    saved = []
    weights = []
    for l in range(DEPTH):
        bias = rel_bias_build(tab_t[l], f"rel_build_l{l}").reshape(N_HEADS_CH, QB, CH_KEYS)
        x0 = h
        wa1, wb1_after = ffn_weights(("ffn1", l), x0)
        x1, s1, wb1 = _ffn_fwd(x0, g_ffn1[l:l + 1], wa1, wb1_after, 0, tm, f"ffn1_l{l}", on_event("ffn1", l),
                               deps=gather_tokens if l == 0 else ())
        wc, wf, late_after = mixer_weights(("mix", l), x1)
        x2, sm, wbr, wout = _mixer_fwd(x1, g_mix[l:l + 1], wc, wf, wc, late_after, bq[l], bf[l], bg[l],
                                       bias, 0, tm, f"mix_l{l}", on_event("mix", l))
        wa2, wb2_after = ffn_weights(("ffn2", l), x2)
        x3, s2, wb2 = _ffn_fwd(x2, g_ffn2[l:l + 1], wa2, wb2_after, 0, tm, f"ffn2_l{l}", on_event("ffn2", l))
        saved.append((x0, x1, x2, s1, sm, s2, bias))
        weights.append(((wa1, wb1), (wc, wf, wout, wbr), (wa2, wb2)))
        h = x3

    dx, dg_final, loss_blk = loss_head(h, g_final[None, :], target, tm, "loss_head")

    g_mix_l = [None] * DEPTH
    dgains = {}
    scatters = {}

    def scatter_ffn(key):
        def on_grads(dwa, dwb):
            scatters[key] = exchange_start(
                "scatter", [dwa[None], dwb.reshape(1, N_DEV, D_FF // N_DEV, D_MODEL)],
                f"scatter_{key[0]}_l{key[1]}")
            return (scatters[key]["token"],)
        return on_grads

    def scatter_mixer(key):
        def on_grads(gm):
            scatters[key] = exchange_start(
                "scatter",
                [gm["dwqkv"].reshape(1, N_DEV, LANES, QKV_WIDTH), gm["dwgate"].reshape(1, N_DEV, LANES, QKV_WIDTH),
                 gm["dwf"].reshape(1, N_DEV, LANES, LANES), gm["dwout"].reshape(1, N_DEV, LANES, D_MODEL),
                 gm["dwbr"].reshape(1, D_MODEL, N_DEV, LANES).transpose(0, 2, 1, 3)],
                f"scatter_{key[0]}_l{key[1]}")
            return (scatters[key]["token"],)
        return on_grads

    for l in reversed(range(DEPTH)):
        x0, x1, x2, s1, sm, s2, bias = saved[l]
        w1, (wc, wf, wout, wbr), w2 = weights[l]
        dx, dgains[("ffn2", l)] = _ffn_bwd(dx, x2, g_ffn2[l:l + 1], s2, *w2, 0, tm, f"ffn2_l{l}",
                                           scatter_ffn(("ffn2", l)))
        dx, g_mix_l[l] = _mixer_bwd(dx, x1, g_mix[l:l + 1], sm, wc, wf, wc, wbr, wout, bias, 0, tm,
                                    f"mix_l{l}", scatter_mixer(("mix", l)))
        dx, dgains[("ffn1", l)] = _ffn_bwd(dx, x0, g_ffn1[l:l + 1], s1, *w1, 0, tm, f"ffn1_l{l}",
                                           scatter_ffn(("ffn1", l)))

    small_shapes = []
    small_pieces = []
    small_w, small_m, small_v = [], [], []

    def add_small(piece, w, m, v):
        small_shapes.append(w.shape)
        small_pieces.append(piece)
        small_w.append(w); small_m.append(m); small_v.append(v)

    dg1 = jnp.concatenate([dgains[("ffn1", l)] for l in range(DEPTH)], axis=0)
    dgm = jnp.concatenate([g_mix_l[l]["dgain"] for l in range(DEPTH)], axis=0)
    dg2 = jnp.concatenate([dgains[("ffn2", l)] for l in range(DEPTH)], axis=0)
    db = jnp.stack([jnp.concatenate([g_mix_l[l]["dbq"][0], g_mix_l[l]["dbf"][0, :N_HEADS_FOX],
                                     g_mix_l[l]["dbg"][0]]) for l in range(DEPTH)])
    drel = jnp.stack([g_mix_l[l]["dtab"][:, :N_REL].T for l in range(DEPTH)])
    add_small(dg1, g_ffn1, m_g_ffn1, v_g_ffn1)
    add_small(dgm, g_mix, m_g_mix, v_g_mix)
    add_small(db, b_in, m_b_in, v_b_in)
    add_small(drel, rel_bias, m_rel_bias, v_rel_bias)
    add_small(dg2, g_ffn2, m_g_ffn2, v_g_ffn2)
    add_small(dg_final[0], g_final, m_g_final, v_g_final)
    loss_piece = loss_blk[0, 0:1]
    small_packed = _pack_small(small_pieces + [loss_piece])

    recv = {}
    last = ("ffn1", 0)
    for l in reversed(range(DEPTH)):
        for grp in ("ffn2", "mix", "ffn1"):
            if (grp, l) != last:
                recv[(grp, l)] = exchange_wait(scatters[(grp, l)], dx, f"scattered_{grp}_l{l}")

    def upd(parts, w, m, v, tr, name, rb0=0):
        _, r, c = w.shape
        nr = r // tr

        def p_spec(layer):
            pinned = (nr - 1) if layer == 0 else 0
            return pl.BlockSpec((N_DEV, None, tr, c),
                                lambda l, i: (0, 0, rb0 + jnp.where(l == layer, i, pinned), 0))

        return adamw(parts, w, m, v, (DEPTH, nr), [p_spec(0), p_spec(1)],
                     pl.BlockSpec((None, tr, c), lambda l, i: (l, i, 0)), name)

    def both(grp, k):
        return [recv[(grp, l)][k] for l in range(DEPTH)]

    out_rows = D_FF // N_DEV // 2
    def upd_transposed(parts, w, m, v, tr, name):
        tp = lambda a: jnp.transpose(a, (0, 2, 1))
        return [tp(o) for o in upd(parts, tp(w), tp(m), tp(v), tr, name)]

    in_rows = FF_BLK // 4
    r_ffn2_in = upd_transposed(both("ffn2", 0), w_ffn2_in, m_w_ffn2_in, v_w_ffn2_in, in_rows, "adamw_ffn2_in")
    r_ffn2_out = upd(both("ffn2", 1), w_ffn2_out, m_w_ffn2_out, v_w_ffn2_out, out_rows, "adamw_ffn2_out")
    r_out = upd(both("mix", 3), w_out, m_w_out, v_w_out, LANES, "adamw_w_out")
    r_br_sb = upd(both("mix", 4), w_br_sb, m_w_br_sb, v_w_br_sb, 256, "adamw_br_sb", rb0=0)
    r_br_ch = upd(both("mix", 4), w_br_ch, m_w_br_ch, v_w_br_ch, 256, "adamw_br_ch", rb0=1)
    r_br_fox = upd(both("mix", 4), w_br_fox, m_w_br_fox, v_w_br_fox, 256, "adamw_br_fox", rb0=3)

    def summed(parts, name):
        _, _, r, c = parts.shape
        return sum_parts(parts, (1,), pl.BlockSpec((N_DEV, None, r, c), lambda s: (0, 0, 0, 0)),
                         pl.BlockSpec((r, c), lambda s: (0, 0)), _sds((r, c), F32), name)

    g_w_in = jnp.stack([
        jnp.concatenate([summed(recv[("mix", l)][0], f"sum_wqkv_l{l}"),
                         summed(recv[("mix", l)][2], f"sum_wf_l{l}")[:, :N_HEADS_FOX],
                         summed(recv[("mix", l)][1], f"sum_wgate_l{l}")], axis=1) for l in range(DEPTH)])
    to_cols = lambda a: jnp.transpose(a, (2, 0, 1))
    n_cols = w_in.shape[2]
    col_blk = n_cols // 4
    win_spec = pl.BlockSpec((col_blk, DEPTH, LANES), lambda i: (i, 0, 0))
    r_in = adamw([to_cols(g_w_in)[None]], to_cols(w_in), to_cols(m_w_in), to_cols(v_w_in), (4,),
                 [pl.BlockSpec((1, col_blk, DEPTH, LANES), lambda i: (0, i, 0, 0))], win_spec, "adamw_w_in")
    r_in = [jnp.transpose(o, (1, 2, 0)) for o in r_in]

    recv[last] = exchange_wait(scatters[last], r_in[1], "scattered_ffn1_l0")
    r_ffn1_in = upd_transposed(both("ffn1", 0), w_ffn1_in, m_w_ffn1_in, v_w_ffn1_in, in_rows, "adamw_ffn1_in")
    r_ffn1_out = upd(both("ffn1", 1), w_ffn1_out, m_w_ffn1_out, v_w_ffn1_out, out_rows, "adamw_ffn1_out")

    small_sum = all_reduce_small(small_packed, "allreduce_small", deps=(r_ffn1_out[1],))
    n_small = sum(int(np.prod(s)) for s in small_shapes)
    loss = small_sum.reshape(-1)[n_small]
    sm_spec = pl.BlockSpec((SMALL_ROWS, LANES), lambda i: (0, 0))
    sm_out = adamw([small_sum[None]], _pack_small(small_w), _pack_small(small_m), _pack_small(small_v),
                   (1,), [pl.BlockSpec((1, SMALL_ROWS, LANES), lambda i: (0, 0, 0))], sm_spec, "adamw_small")
    sm_g, sm_d, sm_m, sm_v = [_unpack_small(o, small_shapes) for o in sm_out]

    def per_kind(k):
        small = (sm_g, sm_d, sm_m, sm_v)[k]
        return [small[0], r_ffn1_in[k], r_ffn1_out[k], small[1], r_in[k], small[2], small[3],
                r_br_sb[k], r_br_ch[k], r_br_fox[k], r_out[k], small[4], r_ffn2_in[k], r_ffn2_out[k],
                small[5]]

    return (loss, dx[None], *per_kind(0), *per_kind(1), *per_kind(2), *per_kind(3))
```

```python
import functools

import numpy as np
import jax
import jax.numpy as jnp
from jax import lax
from jax.experimental import pallas as pl
from jax.experimental.pallas import tpu as pltpu

F32 = jnp.float32
BF16 = jnp.bfloat16

N_DEV = 8
D_MODEL = 1024
DEPTH = 2
HEAD_DIM = 64
W_SB, W_CH, W_FOX = 256, 512, 256
QKV_WIDTH = 3 * (W_SB + W_CH + W_FOX)
N_HEADS_FOX = 4
N_HEADS_CH = 8
D_FF = 2816
FF_BLK = 2 * D_FF // N_DEV
CHUNK = 64
LEFT_CHUNKS = 8
MAX_REL = 128
N_REL = 2 * MAX_REL + 1
REL_PAD = 384
QB = 128
KB = 512
KSUB = KB // QB
CH_WIN = 5
CH_KEYS = CH_WIN * QB
RMS_EPS = 1e-6
NEG = -1e30
SCALE = HEAD_DIM ** -0.5
LANES = 128
VMEM_LIMIT = 56 * 1024 * 1024

ADAM_LR, ADAM_B1, ADAM_B2, ADAM_EPS, ADAM_WD, ADAM_STEP = 0.001, 0.9, 0.999, 1e-08, 0.01, 10

SMALL_ROWS = 192

MESH = pl.DeviceIdType.MESH
ANY = pl.BlockSpec(memory_space=pl.ANY)
HIGHEST = lax.Precision.HIGHEST

NN = (((1,), (0,)), ((), ()))
NT = (((1,), (1,)), ((), ()))
TN = (((0,), (0,)), ((), ()))


def _cparams(n_grid):
    return pltpu.CompilerParams(dimension_semantics=("arbitrary",) * n_grid,
                                vmem_limit_bytes=VMEM_LIMIT)


def _sds(shape, dtype):
    return jax.ShapeDtypeStruct(tuple(shape), dtype)


def _my_index():
    return 4 * lax.axis_index("x") + 2 * lax.axis_index("y") + lax.axis_index("c")


def _peer(mask):
    x, y, c = lax.axis_index("x"), lax.axis_index("y"), lax.axis_index("c")
    px = x ^ ((mask >> 2) & 1)
    py = y ^ ((mask >> 1) & 1)
    pc = c ^ (mask & 1)
    return (px, py, pc), 4 * px + 2 * py + pc


def all_gather(shard, name):
    s, r, c = shard.shape

    def body(in_ref, out_ref, send_sems, recv_sems, local_sem):
        me = _my_index()
        mine = pltpu.make_async_copy(in_ref, out_ref.at[:, me], local_sem)
        mine.start()
        sends = []
        for mask in range(1, N_DEV):
            peer, _ = _peer(mask)
            cp = pltpu.make_async_remote_copy(
                src_ref=in_ref, dst_ref=out_ref.at[:, me],
                send_sem=send_sems.at[mask - 1], recv_sem=recv_sems.at[mask - 1],
                device_id=peer, device_id_type=MESH)
            cp.start()
            sends.append(cp)
        for mask in range(1, N_DEV):
            peer, pidx = _peer(mask)
            pltpu.make_async_remote_copy(
                src_ref=in_ref, dst_ref=out_ref.at[:, pidx],
                send_sem=send_sems.at[mask - 1], recv_sem=recv_sems.at[mask - 1],
                device_id=peer, device_id_type=MESH).wait_recv()
        for cp in sends:
            cp.wait_send()
        mine.wait()

    return pl.pallas_call(
        body, name=name,
        out_shape=_sds((s, N_DEV, r, c), shard.dtype),
        in_specs=[ANY], out_specs=ANY,
        scratch_shapes=[pltpu.SemaphoreType.DMA((N_DEV - 1,)),
                        pltpu.SemaphoreType.DMA((N_DEV - 1,)),
                        pltpu.SemaphoreType.DMA],
    )(shard)


def all_to_all(parts, name):
    s, _, r, c = parts.shape

    def body(in_ref, out_ref, send_sems, recv_sems, local_sem):
        me = _my_index()
        mine = pltpu.make_async_copy(in_ref.at[:, me], out_ref.at[me], local_sem)
        mine.start()
        sends = []
        for mask in range(1, N_DEV):
            peer, pidx = _peer(mask)
            cp = pltpu.make_async_remote_copy(
                src_ref=in_ref.at[:, pidx], dst_ref=out_ref.at[me],
                send_sem=send_sems.at[mask - 1], recv_sem=recv_sems.at[mask - 1],
                device_id=peer, device_id_type=MESH)
            cp.start()
            sends.append(cp)
        for mask in range(1, N_DEV):
            peer, pidx = _peer(mask)
            pltpu.make_async_remote_copy(
                src_ref=in_ref.at[:, me], dst_ref=out_ref.at[pidx],
                send_sem=send_sems.at[mask - 1], recv_sem=recv_sems.at[mask - 1],
                device_id=peer, device_id_type=MESH).wait_recv()
        for cp in sends:
            cp.wait_send()
        mine.wait()

    return pl.pallas_call(
        body, name=name,
        out_shape=_sds((N_DEV, s, r, c), parts.dtype),
        in_specs=[ANY], out_specs=ANY,
        scratch_shapes=[pltpu.SemaphoreType.DMA((N_DEV - 1,)),
                        pltpu.SemaphoreType.DMA((N_DEV - 1,)),
                        pltpu.SemaphoreType.DMA],
    )(parts)


HBM_SPEC = pl.BlockSpec(memory_space=pltpu.HBM)
SEM_SPEC = pl.BlockSpec(memory_space=pltpu.SEMAPHORE)
EFFECT = pltpu.SideEffectType.DATAFLOW_SIDE_EFFECTING


def _exchange_refs(mode, in_ref, land_ref, me, pidx):
    if mode == "gather":
        return in_ref, land_ref.at[:, me], land_ref.at[:, pidx]
    return in_ref.at[:, pidx], land_ref.at[me], land_ref.at[pidx]


def _landing_shape(mode, a):
    if mode == "gather":
        s, r, c = a.shape
        return (s, N_DEV, r, c)
    s, _, r, c = a.shape
    return (N_DEV, s, r, c)


def _own_copy(mode, in_ref, land_ref, me, sem):
    if mode == "gather":
        return pltpu.make_async_copy(in_ref, land_ref.at[:, me], sem)
    return pltpu.make_async_copy(in_ref.at[:, me], land_ref.at[me], sem)


def exchange_start(mode, arrays, name, deps=()):
    n = len(arrays)
    lands0 = [lax.empty(_landing_shape(mode, a), a.dtype) for a in arrays]

    def body(*refs):
        in_refs, land_refs = refs[:n], refs[n:2 * n]
        outs_at = 2 * n + len(deps)
        send_sems, recv_sems, own_sems, token = refs[outs_at], refs[outs_at + 1], refs[outs_at + 2], refs[-1]
        mine = _my_index()
        for k in range(n):
            _own_copy(mode, in_refs[k], land_refs[k], mine, own_sems.at[k]).start()
            for mask in range(1, N_DEV):
                peer, pidx = _peer(mask)
                src, dst, _ = _exchange_refs(mode, in_refs[k], land_refs[k], mine, pidx)
                sem = k * (N_DEV - 1) + mask - 1
                pltpu.make_async_remote_copy(
                    src_ref=src, dst_ref=dst, send_sem=send_sems.at[sem], recv_sem=recv_sems.at[sem],
                    device_id=peer, device_id_type=MESH).start()
        token[...] = jnp.zeros_like(token)

    nsem = n * (N_DEV - 1)
    outs = pl.pallas_call(
        body, name=name,
        out_shape=(pltpu.SemaphoreType.DMA((nsem,)), pltpu.SemaphoreType.DMA((nsem,)),
                   pltpu.SemaphoreType.DMA((n,)),
                   *[pltpu.HBM(a.shape, a.dtype) for a in arrays],
                   *[pltpu.HBM(l.shape, l.dtype) for l in lands0], _sds((8, LANES), F32)),
        in_specs=[HBM_SPEC] * (2 * n) + [ANY] * len(deps),
        out_specs=(SEM_SPEC, SEM_SPEC, SEM_SPEC, *[HBM_SPEC] * (2 * n),
                   pl.BlockSpec(memory_space=pltpu.VMEM)),
        input_output_aliases={k: 3 + k for k in range(2 * n)},
        compiler_params=pltpu.CompilerParams(has_side_effects=EFFECT),
    )(*[pltpu.with_memory_space_constraint(a, pltpu.HBM) for a in arrays],
      *[pltpu.with_memory_space_constraint(l, pltpu.HBM) for l in lands0], *deps)
    return dict(mode=mode, n=n, send=outs[0], recv=outs[1], own=outs[2], ins=outs[3:3 + n],
                lands=outs[3 + n:3 + 2 * n], token=outs[-1])


def exchange_wait(handle, after, name):
    n, mode = handle["n"], handle["mode"]

    def body(*refs):
        in_refs, land_refs = refs[:n], refs[n:2 * n]
        send_sems, recv_sems, own_sems = refs[2 * n], refs[2 * n + 1], refs[2 * n + 2]
        mine = _my_index()
        for k in range(n):
            _own_copy(mode, in_refs[k], land_refs[k], mine, own_sems.at[k]).wait()
            for mask in range(1, N_DEV):
                peer, pidx = _peer(mask)
                src, _, here = _exchange_refs(mode, in_refs[k], land_refs[k], mine, pidx)
                sem = k * (N_DEV - 1) + mask - 1
                cp = pltpu.make_async_remote_copy(
                    src_ref=src, dst_ref=here, send_sem=send_sems.at[sem], recv_sem=recv_sems.at[sem],
                    device_id=peer, device_id_type=MESH)
                cp.wait_send()
                cp.wait_recv()

    thru = (*handle["ins"], *handle["lands"])
    outs = pl.pallas_call(
        body, name=name,
        out_shape=tuple(pltpu.HBM(a.shape, a.dtype) for a in thru),
        in_specs=[HBM_SPEC] * (2 * n) + [SEM_SPEC, SEM_SPEC, SEM_SPEC, ANY],
        out_specs=tuple([HBM_SPEC] * (2 * n)),
        input_output_aliases={k: k for k in range(2 * n)},
        compiler_params=pltpu.CompilerParams(has_side_effects=EFFECT),
    )(*thru, handle["send"], handle["recv"], handle["own"], after)
    return list(outs[n:])


FAR_MASKS = (2, 4, 6)
PHASE1_MASKS = (1,) + FAR_MASKS


def gather_start(arrays, name, deps=()):
    n = len(arrays)
    n1 = len(PHASE1_MASKS)
    lands0 = [lax.empty(_landing_shape("gather", a), a.dtype) for a in arrays]

    def body(*refs):
        in_refs, land_refs = refs[:n], refs[n:2 * n]
        outs_at = 2 * n + len(deps)
        send_sems, recv_sems, own_sems, token = refs[outs_at], refs[outs_at + 1], refs[outs_at + 2], refs[-1]
        mine = _my_index()
        for k in range(n):
            _own_copy("gather", in_refs[k], land_refs[k], mine, own_sems.at[k]).start()
            for j, mask in enumerate(PHASE1_MASKS):
                peer, _ = _peer(mask)
                pltpu.make_async_remote_copy(
                    src_ref=in_refs[k], dst_ref=land_refs[k].at[:, mine],
                    send_sem=send_sems.at[k * n1 + j], recv_sem=recv_sems.at[k * n1 + j],
                    device_id=peer, device_id_type=MESH).start()
        token[...] = jnp.zeros_like(token)

    outs = pl.pallas_call(
        body, name=name,
        out_shape=(pltpu.SemaphoreType.DMA((n * n1,)), pltpu.SemaphoreType.DMA((n * n1,)),
                   pltpu.SemaphoreType.DMA((n,)),
                   *[pltpu.HBM(a.shape, a.dtype) for a in arrays],
                   *[pltpu.HBM(l.shape, l.dtype) for l in lands0], _sds((8, LANES), F32)),
        in_specs=[HBM_SPEC] * (2 * n) + [ANY] * len(deps),
        out_specs=(SEM_SPEC, SEM_SPEC, SEM_SPEC, *[HBM_SPEC] * (2 * n),
                   pl.BlockSpec(memory_space=pltpu.VMEM)),
        input_output_aliases={k: 3 + k for k in range(2 * n)},
        compiler_params=pltpu.CompilerParams(has_side_effects=EFFECT),
    )(*[pltpu.with_memory_space_constraint(a, pltpu.HBM) for a in arrays],
      *[pltpu.with_memory_space_constraint(l, pltpu.HBM) for l in lands0], *deps)
    return dict(n=n, send=outs[0], recv=outs[1], own=outs[2], ins=outs[3:3 + n],
                lands=outs[3 + n:3 + 2 * n], token=outs[-1], name=name)


def gather_relay(handle, after):
    n = handle["n"]
    n1, n2 = len(PHASE1_MASKS), len(FAR_MASKS)

    def body(*refs):
        in_refs, land_refs = refs[:n], refs[n:2 * n]
        send1, recv1 = refs[2 * n], refs[2 * n + 1]
        send2, recv2, token = refs[2 * n + 3], refs[2 * n + 4], refs[-1]
        token[...] = jnp.zeros_like(token)
        sibling, _ = _peer(1)
        for k in range(n):
            for j, mask in enumerate(FAR_MASKS):
                peer, pidx = _peer(mask)
                landed = land_refs[k].at[:, pidx]
                pltpu.make_async_remote_copy(
                    src_ref=in_refs[k], dst_ref=landed, send_sem=send1.at[k * n1 + 1 + j],
                    recv_sem=recv1.at[k * n1 + 1 + j], device_id=peer, device_id_type=MESH).wait_recv()
                pltpu.make_async_remote_copy(
                    src_ref=landed, dst_ref=landed, send_sem=send2.at[k * n2 + j],
                    recv_sem=recv2.at[k * n2 + j], device_id=sibling, device_id_type=MESH).start()

    thru = (*handle["ins"], *handle["lands"])
    outs = pl.pallas_call(
        body, name=handle["name"] + "_relay",
        out_shape=(pltpu.SemaphoreType.DMA((n * n2,)), pltpu.SemaphoreType.DMA((n * n2,)),
                   *[pltpu.HBM(a.shape, a.dtype) for a in thru], _sds((8, LANES), F32)),
        in_specs=[HBM_SPEC] * (2 * n) + [SEM_SPEC, SEM_SPEC, ANY],
        out_specs=(SEM_SPEC, SEM_SPEC, *[HBM_SPEC] * (2 * n), pl.BlockSpec(memory_space=pltpu.VMEM)),
        input_output_aliases={k: 2 + k for k in range(2 * n)},
        compiler_params=pltpu.CompilerParams(has_side_effects=EFFECT),
    )(*thru, handle["send"], handle["recv"], after)
    handle.update(send2=outs[0], recv2=outs[1], ins=outs[2:2 + n], lands=outs[2 + n:2 + 2 * n],
                  relay_token=outs[-1])


def gather_finish(handle, after):
    n = handle["n"]
    n1, n2 = len(PHASE1_MASKS), len(FAR_MASKS)

    def body(*refs):
        in_refs, land_refs = refs[:n], refs[n:2 * n]
        send1, recv1, own_sems, send2, recv2 = refs[2 * n:2 * n + 5]
        mine = _my_index()
        sibling, sib_idx = _peer(1)
        for k in range(n):
            _own_copy("gather", in_refs[k], land_refs[k], mine, own_sems.at[k]).wait()
            for j, mask in enumerate(PHASE1_MASKS):
                peer, pidx = _peer(mask)
                cp = pltpu.make_async_remote_copy(
                    src_ref=in_refs[k], dst_ref=land_refs[k].at[:, pidx], send_sem=send1.at[k * n1 + j],
                    recv_sem=recv1.at[k * n1 + j], device_id=peer, device_id_type=MESH)
                cp.wait_send()
                if mask == 1:
                    cp.wait_recv()
            for j, mask in enumerate(FAR_MASKS):
                _, pidx = _peer(mask)
                _, far_of_sibling = _peer(mask ^ 1)
                cp = pltpu.make_async_remote_copy(
                    src_ref=land_refs[k].at[:, pidx], dst_ref=land_refs[k].at[:, far_of_sibling],
                    send_sem=send2.at[k * n2 + j], recv_sem=recv2.at[k * n2 + j],
                    device_id=sibling, device_id_type=MESH)
                cp.wait_send()
                cp.wait_recv()

    thru = (*handle["ins"], *handle["lands"])
    outs = pl.pallas_call(
        body, name=handle["name"] + "_finish",
        out_shape=tuple(pltpu.HBM(a.shape, a.dtype) for a in thru),
        in_specs=[HBM_SPEC] * (2 * n) + [SEM_SPEC] * 5 + [ANY],
        out_specs=tuple([HBM_SPEC] * (2 * n)),
        input_output_aliases={k: k for k in range(2 * n)},
        compiler_params=pltpu.CompilerParams(has_side_effects=EFFECT),
    )(*thru, handle["send"], handle["recv"], handle["own"], handle["send2"], handle["recv2"], after)
    return list(outs[n:])


def all_reduce_small(packed, name, deps=()):
    rows = packed.shape[0]
    nd = len(deps)

    def body(in_ref, *rest):
        out_ref, slots, send_sems, recv_sems = rest[nd:]
        me = _my_index()
        sends = []
        for mask in range(1, N_DEV):
            peer, _ = _peer(mask)
            cp = pltpu.make_async_remote_copy(
                src_ref=in_ref, dst_ref=slots.at[me],
                send_sem=send_sems.at[mask - 1], recv_sem=recv_sems.at[mask - 1],
                device_id=peer, device_id_type=MESH)
            cp.start()
            sends.append(cp)
        slots[me] = in_ref[...]
        for mask in range(1, N_DEV):
            peer, pidx = _peer(mask)
            pltpu.make_async_remote_copy(
                src_ref=in_ref, dst_ref=slots.at[pidx],
                send_sem=send_sems.at[mask - 1], recv_sem=recv_sems.at[mask - 1],
                device_id=peer, device_id_type=MESH).wait_recv()
        for cp in sends:
            cp.wait_send()
        total = slots[0]
        for p in range(1, N_DEV):
            total = total + slots[p]
        out_ref[...] = total

    return pl.pallas_call(
        body, name=name,
        out_shape=_sds((rows, LANES), F32),
        in_specs=[pl.BlockSpec(memory_space=pltpu.VMEM)] + [ANY] * nd,
        out_specs=pl.BlockSpec(memory_space=pltpu.VMEM),
        scratch_shapes=[pltpu.VMEM((N_DEV, rows, LANES), F32),
                        pltpu.SemaphoreType.DMA((N_DEV - 1,)),
                        pltpu.SemaphoreType.DMA((N_DEV - 1,))],
    )(packed, *deps)


def matmul(dims, a, b, out_sds, grid, a_spec, b_spec, o_spec, acc_shape, *, name, alpha=1.0,
           bias=None, bias_spec=None, scale=None, scale_spec=None, res=None, res_spec=None,
           colsum_sds=None, colsum_spec=None, out_t_sds=None, out_t_spec=None, deps=()):
    nk = grid[2]
    has_bias, has_scale, has_res = bias is not None, scale is not None, res is not None
    has_cs, has_t = colsum_sds is not None, out_t_sds is not None
    if has_cs:
        assert grid[0] == 1 and dims == TN

    def body(*refs):
        a_ref, b_ref = refs[0], refs[1]
        pos = 2
        bias_ref = scale_ref = res_ref = cs_ref = ot_ref = None
        if has_bias:
            bias_ref = refs[pos]; pos += 1
        if has_scale:
            scale_ref = refs[pos]; pos += 1
        if has_res:
            res_ref = refs[pos]; pos += 1
        pos += len(deps)
        o_ref = refs[pos]; pos += 1
        if has_cs:
            cs_ref = refs[pos]; pos += 1
        if has_t:
            ot_ref = refs[pos]; pos += 1
        k = pl.program_id(2)
        bval = b_ref[...]
        part = lax.dot_general(a_ref[...].astype(BF16), bval.astype(BF16), dims,
                               preferred_element_type=F32)

        def finish(total):
            r = total * alpha if alpha != 1.0 else total
            if has_bias:
                r = r + bias_ref[...]
            if has_scale:
                r = r * scale_ref[...]
            if has_res:
                r = r + res_ref[...].astype(F32)
            o_ref[...] = r.astype(o_ref.dtype)
            if has_t:
                ot_ref[...] = r.T.astype(ot_ref.dtype)

        if has_cs:
            csum = jnp.sum(bval.astype(F32), axis=0, keepdims=True)

            @pl.when(k == 0)
            def _():
                cs_ref[...] = csum

            @pl.when(k > 0)
            def _():
                cs_ref[...] += csum

        if nk == 1:
            finish(part)
        else:
            acc_ref = refs[pos]

            @pl.when(k == 0)
            def _():
                acc_ref[...] = part

            @pl.when(k > 0)
            def _():
                acc_ref[...] += part

            @pl.when(k == nk - 1)
            def _():
                finish(acc_ref[...])

    in_specs, args = [a_spec, b_spec], [a, b]
    if has_bias:
        in_specs.append(bias_spec); args.append(bias)
    if has_scale:
        in_specs.append(scale_spec); args.append(scale)
    if has_res:
        in_specs.append(res_spec); args.append(res)
    in_specs += [ANY] * len(deps)
    args += list(deps)
    out_shape, out_specs = [out_sds], [o_spec]
    if has_cs:
        out_shape.append(colsum_sds); out_specs.append(colsum_spec)
    if has_t:
        out_shape.append(out_t_sds); out_specs.append(out_t_spec)
    scratch = [] if nk == 1 else [pltpu.VMEM(acc_shape, F32)]
    outs = pl.pallas_call(
        body, name=name, grid=grid, in_specs=in_specs, out_specs=out_specs, out_shape=out_shape,
        scratch_shapes=scratch, compiler_params=_cparams(3))(*args)
    return outs if (has_cs or has_t) else outs[0]


def _sigmoid(z):
    return 1.0 / (1.0 + jnp.exp(-z))


def _log_sigmoid(z):
    return jnp.minimum(z, 0.0) - jnp.log(1.0 + jnp.exp(-jnp.abs(z)))


def rmsnorm_fwd(x, gain, tm, name, deps=()):
    t, d = x.shape

    def body(x_ref, g_ref, *rest):
        o_ref = rest[-1]
        xf = x_ref[...]
        r = lax.rsqrt(jnp.mean(xf * xf, axis=-1, keepdims=True) + RMS_EPS)
        o_ref[...] = (xf * r * g_ref[...]).astype(o_ref.dtype)

    return pl.pallas_call(
        body, name=name, grid=(t // tm,),
        in_specs=[pl.BlockSpec((tm, d), lambda i: (i, 0)), pl.BlockSpec((1, d), lambda i: (0, 0))]
        + [ANY] * len(deps),
        out_specs=pl.BlockSpec((tm, d), lambda i: (i, 0)),
        out_shape=_sds((t, d), BF16), compiler_params=_cparams(1))(x, gain, *deps)


def rmsnorm_bwd(x, gain, dh, dres, tm, name):
    t, d = x.shape

    def body(x_ref, g_ref, dh_ref, dres_ref, dx_ref, dg_ref):
        i = pl.program_id(0)
        xf = x_ref[...]
        r = lax.rsqrt(jnp.mean(xf * xf, axis=-1, keepdims=True) + RMS_EPS)
        xhat = xf * r
        dh_v = dh_ref[...]
        dxhat = dh_v * g_ref[...]
        dx = r * (dxhat - xhat * jnp.mean(dxhat * xhat, axis=-1, keepdims=True))
        dx_ref[...] = dres_ref[...] + dx
        dg = jnp.sum(dh_v * xhat, axis=0, keepdims=True)

        @pl.when(i == 0)
        def _():
            dg_ref[...] = dg

        @pl.when(i > 0)
        def _():
            dg_ref[...] += dg

    row = pl.BlockSpec((tm, d), lambda i: (i, 0))
    vec = pl.BlockSpec((1, d), lambda i: (0, 0))
    return pl.pallas_call(
        body, name=name, grid=(t // tm,), in_specs=[row, vec, row, row], out_specs=[row, vec],
        out_shape=[_sds((t, d), F32), _sds((1, d), F32)], compiler_params=_cparams(1))(x, gain, dh, dres)


def loss_head(x, gain, target, tm, name):
    t, d = x.shape

    def body(x_ref, g_ref, tgt_ref, dx_ref, dg_ref, loss_ref):
        i = pl.program_id(0)
        xf = x_ref[...]
        g = g_ref[...]
        r = lax.rsqrt(jnp.mean(xf * xf, axis=-1, keepdims=True) + RMS_EPS)
        xhat = xf * r
        err = xhat * g - tgt_ref[...]
        part = 0.5 * jnp.sum(jnp.mean(err * err, axis=-1, keepdims=True))
        dy = err * (1.0 / d)
        dxhat = dy * g
        dx_ref[...] = r * (dxhat - xhat * jnp.mean(dxhat * xhat, axis=-1, keepdims=True))
        dg = jnp.sum(dy * xhat, axis=0, keepdims=True)
        lpart = jnp.full((8, LANES), part, F32)

        @pl.when(i == 0)
        def _():
            dg_ref[...] = dg
            loss_ref[...] = lpart

        @pl.when(i > 0)
        def _():
            dg_ref[...] += dg
            loss_ref[...] += lpart

    row = pl.BlockSpec((tm, d), lambda i: (i, 0))
    vec = pl.BlockSpec((1, d), lambda i: (0, 0))
    return pl.pallas_call(
        body, name=name, grid=(t // tm,), in_specs=[row, vec, row],
        out_specs=[row, vec, pl.BlockSpec((8, LANES), lambda i: (0, 0))],
        out_shape=[_sds((t, d), F32), _sds((1, d), F32), _sds((8, LANES), F32)],
        compiler_params=_cparams(1))(x, gain, target)


def ffn_in_swiglu(hn, wa, s, tm, name):
    t = hn.shape[0]
    halves = 2 if tm % 512 == 0 else 1
    rows = tm // halves

    def body(h_ref, wg_ref, wu_ref, gu_ref, act_ref):
        for c in range(halves):
            rs = slice(c * rows, (c + 1) * rows)
            h = h_ref[rs, :]
            g = jnp.dot(h, wg_ref[...], preferred_element_type=F32)
            u = jnp.dot(h, wu_ref[...], preferred_element_type=F32)
            gu_ref[0, rs, :] = g.astype(gu_ref.dtype)
            gu_ref[1, rs, :] = u.astype(gu_ref.dtype)
            act_ref[rs, :] = (g * _sigmoid(g) * u).astype(act_ref.dtype)

    return pl.pallas_call(
        body, name=name, grid=(t // tm, 4),
        in_specs=[pl.BlockSpec((tm, D_MODEL), lambda i, j: (i, 0)),
                  pl.BlockSpec((None, None, D_MODEL, FF_BLK), lambda i, j: (s, j, 0, 0)),
                  pl.BlockSpec((None, None, D_MODEL, FF_BLK), lambda i, j: (s, j + 4, 0, 0))],
        out_specs=[pl.BlockSpec((None, 2, tm, FF_BLK), lambda i, j: (j, 0, i, 0)),
                   pl.BlockSpec((None, tm, FF_BLK), lambda i, j: (j, i, 0))],
        out_shape=[_sds((4, 2, t, FF_BLK), BF16), _sds((4, t, FF_BLK), BF16)],
        compiler_params=_cparams(2))(hn, wa, wa)


def ffn_dact_swiglu(dy, wb, gu, s, tm, name):
    t = dy.shape[0]

    def body(dy_ref, w_ref, gu_ref, o_ref):
        da = 0.5 * lax.dot_general(dy_ref[...].astype(BF16), w_ref[...], NT, preferred_element_type=F32)
        g = gu_ref[0].astype(F32)
        u = gu_ref[1].astype(F32)
        sg = _sigmoid(g)
        o_ref[0] = (da * u * (sg * (1.0 + g * (1.0 - sg)))).astype(o_ref.dtype)
        o_ref[1] = (da * g * sg).astype(o_ref.dtype)

    blk = pl.BlockSpec((None, 2, tm, FF_BLK), lambda i, j: (j, 0, i, 0))
    return pl.pallas_call(
        body, name=name, grid=(t // tm, 4),
        in_specs=[pl.BlockSpec((tm, D_MODEL), lambda i, j: (i, 0)),
                  pl.BlockSpec((None, None, FF_BLK, D_MODEL), lambda i, j: (s, j, 0, 0)), blk],
        out_specs=blk, out_shape=_sds((4, 2, t, FF_BLK), BF16),
        compiler_params=_cparams(2))(dy, wb, gu)


def ffn_out_residual(act, wb, x, s, tm, name, deps=()):
    t = x.shape[0]

    def body(a_ref, w_ref, x_ref, *rest):
        o_ref = rest[-1]
        acc = jnp.dot(a_ref[0], w_ref[0], preferred_element_type=F32)
        for k in range(1, 4):
            acc = acc + jnp.dot(a_ref[k], w_ref[k], preferred_element_type=F32)
        o_ref[...] = x_ref[...] + 0.5 * acc

    row = pl.BlockSpec((tm, D_MODEL), lambda i: (i, 0))
    return pl.pallas_call(
        body, name=name, grid=(t // tm,),
        in_specs=[pl.BlockSpec((4, tm, FF_BLK), lambda i: (0, i, 0)),
                  pl.BlockSpec((None, 4, FF_BLK, D_MODEL), lambda i: (s, 0, 0, 0)), row] + [ANY] * len(deps),
        out_specs=row, out_shape=_sds((t, D_MODEL), F32), compiler_params=_cparams(1))(act, wb, x, *deps)


def ffn_dh_norm_bwd(dgu, wa, s, x, gain, dres, tm, name, deps):
    t = dgu.shape[2]
    nd = len(deps)

    def body(g_ref, w_ref, x_ref, gain_ref, dres_ref, *rest):
        dx_ref, dg_ref = rest[nd:]
        i = pl.program_id(0)
        dh = lax.dot_general(g_ref[0, 0], w_ref[0], NT, preferred_element_type=F32)
        for p in range(1, N_DEV):
            dh = dh + lax.dot_general(g_ref[p % 4, p // 4], w_ref[p], NT, preferred_element_type=F32)
        xf = x_ref[...]
        r = lax.rsqrt(jnp.mean(xf * xf, axis=-1, keepdims=True) + RMS_EPS)
        xhat = xf * r
        dxhat = dh * gain_ref[...]
        dx_ref[...] = dres_ref[...] + r * (dxhat - xhat * jnp.mean(dxhat * xhat, axis=-1, keepdims=True))
        dg = jnp.sum(dh * xhat, axis=0, keepdims=True)

        @pl.when(i == 0)
        def _():
            dg_ref[...] = dg

        @pl.when(i > 0)
        def _():
            dg_ref[...] += dg

    row = pl.BlockSpec((tm, D_MODEL), lambda i: (i, 0))
    vec = pl.BlockSpec((1, D_MODEL), lambda i: (0, 0))
    return pl.pallas_call(
        body, name=name, grid=(t // tm,),
        in_specs=[pl.BlockSpec((4, 2, tm, FF_BLK), lambda i: (0, 0, i, 0)),
                  pl.BlockSpec((None, N_DEV, D_MODEL, FF_BLK), lambda i: (s, 0, 0, 0)), row, vec, row]
        + [ANY] * nd,
        out_specs=[row, vec], out_shape=[_sds((t, D_MODEL), F32), _sds((1, D_MODEL), F32)],
        compiler_params=_cparams(1))(dgu, wa, x, gain, dres, *deps)


def merge_fwd(gates, ya, yb, yc, tm, name):
    t, d = ya.shape

    def body(ga_ref, gb_ref, gc_ref, ya_ref, yb_ref, yc_ref, o_ref):
        m = (_sigmoid(ga_ref[...]) * ya_ref[...] + _sigmoid(gb_ref[...]) * yb_ref[...]
             + _sigmoid(gc_ref[...]) * yc_ref[...])
        o_ref[...] = m.astype(o_ref.dtype)

    row = pl.BlockSpec((tm, d), lambda i: (i, 0))
    gspecs = [pl.BlockSpec((tm, d), functools.partial(lambda i, a: (i, a), a=a)) for a in range(3)]
    return pl.pallas_call(
        body, name=name, grid=(t // tm,), in_specs=gspecs + [row, row, row], out_specs=row,
        out_shape=_sds((t, d), BF16), compiler_params=_cparams(1))(gates, gates, gates, ya, yb, yc)


def branch_merge(os_, wbr, layer, gates, tm, name, deps=()):
    t = os_[0].shape[0]
    d = D_MODEL
    nd = len(deps)
    widths = [o.shape[1] for o in os_]
    starts = [sum(widths[:a]) for a in range(3)]

    def body(oa_ref, ob_ref, oc_ref, w_ref, g_ref, *rest):
        ya_ref, yb_ref, yc_ref, m_ref = rest[nd:]
        merged = None
        for a, (o_ref, y_ref) in enumerate(((oa_ref, ya_ref), (ob_ref, yb_ref), (oc_ref, yc_ref))):
            y = jnp.dot(o_ref[...], w_ref[starts[a]:starts[a] + widths[a], :], preferred_element_type=F32)
            y_ref[...] = y
            term = _sigmoid(g_ref[:, a * d:(a + 1) * d]) * y
            merged = term if merged is None else merged + term
        m_ref[...] = merged.astype(m_ref.dtype)

    row = pl.BlockSpec((tm, d), lambda i: (i, 0))
    ya, yb, yc, merged = pl.pallas_call(
        body, name=name, grid=(t // tm,),
        in_specs=[pl.BlockSpec((tm, w), lambda i: (i, 0)) for w in widths]
        + [pl.BlockSpec((None, d, d), lambda i: (layer, 0, 0)), pl.BlockSpec((tm, 3 * d), lambda i: (i, 0))]
        + [ANY] * nd,
        out_specs=[row, row, row, row],
        out_shape=[_sds((t, d), F32)] * 3 + [_sds((t, d), BF16)],
        compiler_params=_cparams(1))(*os_, wbr, gates, *deps)
    return [ya, yb, yc], merged


def dmerged_merge_bwd(dy, wout, layer, gates, ys, tm, name):
    t, d = dy.shape

    def body(dy_ref, w_ref, g_ref, ya_ref, yb_ref, yc_ref, dg_ref, dya_ref, dyb_ref, dyc_ref):
        dmv = lax.dot_general(dy_ref[...].astype(BF16), w_ref[...], NT, preferred_element_type=F32)
        for a, (y_ref, dy_out) in enumerate(((ya_ref, dya_ref), (yb_ref, dyb_ref), (yc_ref, dyc_ref))):
            cols = slice(a * d, (a + 1) * d)
            s = _sigmoid(g_ref[:, cols])
            dy_out[...] = (dmv * s).astype(dy_out.dtype)
            dg_ref[:, cols] = (dmv * y_ref[...] * s * (1.0 - s)).astype(dg_ref.dtype)

    row = pl.BlockSpec((tm, d), lambda i: (i, 0))
    wide = pl.BlockSpec((tm, 3 * d), lambda i: (i, 0))
    dg, dya, dyb, dyc = pl.pallas_call(
        body, name=name, grid=(t // tm,),
        in_specs=[row, pl.BlockSpec((None, d, d), lambda i: (layer, 0, 0)), wide, row, row, row],
        out_specs=[wide, row, row, row],
        out_shape=[_sds((t, 3 * d), BF16)] + [_sds((t, d), BF16)] * 3,
        compiler_params=_cparams(1))(dy, wout, gates, *ys)
    return dg, [dya, dyb, dyc]


def branch_bwd(dys, os_, wbr, layer, tm, name):
    t = dys[0].shape[0]
    d = D_MODEL
    nt = t // tm
    widths = [o.shape[1] for o in os_]
    starts = [sum(widths[:a]) for a in range(3)]

    def body(dya_ref, dyb_ref, dyc_ref, oa_ref, ob_ref, oc_ref, w_ref, do_ref, dot_ref, dw_ref, acc_ref):
        i = pl.program_id(0)
        for a, (dy_ref, o_ref) in enumerate(((dya_ref, oa_ref), (dyb_ref, ob_ref), (dyc_ref, oc_ref))):
            rows = slice(starts[a], starts[a] + widths[a])
            dyv = dy_ref[...]
            do = lax.dot_general(dyv, w_ref[rows, :], NT, preferred_element_type=F32)
            do_ref[:, rows] = do.astype(do_ref.dtype)
            dot_ref[rows, :] = do.T.astype(dot_ref.dtype)
            dw = lax.dot_general(o_ref[...], dyv, TN, preferred_element_type=F32)

            @pl.when(i == 0)
            def _():
                acc_ref[rows, :] = dw

            @pl.when(i > 0)
            def _():
                acc_ref[rows, :] += dw

        @pl.when(i == nt - 1)
        def _():
            dw_ref[...] = acc_ref[...].astype(dw_ref.dtype)

    row = pl.BlockSpec((tm, d), lambda i: (i, 0))
    return pl.pallas_call(
        body, name=name, grid=(nt,),
        in_specs=[row, row, row] + [pl.BlockSpec((tm, w), lambda i: (i, 0)) for w in widths]
        + [pl.BlockSpec((None, d, d), lambda i: (layer, 0, 0))],
        out_specs=[row, pl.BlockSpec((d, tm), lambda i: (0, i)), pl.BlockSpec((d, d), lambda i: (0, 0))],
        out_shape=[_sds((t, d), BF16), _sds((d, t), BF16), _sds((d, d), BF16)],
        scratch_shapes=[pltpu.VMEM((d, d), F32)], compiler_params=_cparams(1))(*dys, *os_, wbr)


def mixer_dh_norm_bwd(dh_part, dgates, wc, gate_idx, df, wf, layer, x, gain, dres, tm, name):
    t, d = x.shape

    def body(dhp_ref, dg_ref, wg_ref, df_ref, wf_ref, x_ref, gain_ref, dres_ref, dx_ref, dgain_ref):
        i = pl.program_id(0)
        dh = (dhp_ref[...]
              + lax.dot_general(dg_ref[...], wg_ref[...], NT, preferred_element_type=F32)
              + lax.dot_general(df_ref[...].astype(BF16), wf_ref[...], NT, preferred_element_type=F32))
        xf = x_ref[...]
        r = lax.rsqrt(jnp.mean(xf * xf, axis=-1, keepdims=True) + RMS_EPS)
        xhat = xf * r
        dxhat = dh * gain_ref[...]
        dx_ref[...] = dres_ref[...] + r * (dxhat - xhat * jnp.mean(dxhat * xhat, axis=-1, keepdims=True))
        dg = jnp.sum(dh * xhat, axis=0, keepdims=True)

        @pl.when(i == 0)
        def _():
            dgain_ref[...] = dg

        @pl.when(i > 0)
        def _():
            dgain_ref[...] += dg

    row = pl.BlockSpec((tm, d), lambda i: (i, 0))
    vec = pl.BlockSpec((1, d), lambda i: (0, 0))
    return pl.pallas_call(
        body, name=name, grid=(t // tm,),
        in_specs=[row, pl.BlockSpec((tm, QKV_WIDTH), lambda i: (i, 0)),
                  pl.BlockSpec((None, d, QKV_WIDTH), lambda i: (gate_idx, 0, 0)),
                  pl.BlockSpec((tm, LANES), lambda i: (i, 0)),
                  pl.BlockSpec((None, d, LANES), lambda i: (layer, 0, 0)), row, vec, row],
        out_specs=[row, vec], out_shape=[_sds((t, d), F32), _sds((1, d), F32)],
        compiler_params=_cparams(1))(dh_part, dgates, wc, df, wf, x, gain, dres)


def merge_bwd(dm, gates, ya, yb, yc, tm, name):
    t, d = ya.shape

    def body(dm_ref, g_ref, ya_ref, yb_ref, yc_ref, dg_ref, dya_ref, dyb_ref, dyc_ref):
        dmv = dm_ref[...]
        for a, (y_ref, dy_ref) in enumerate(((ya_ref, dya_ref), (yb_ref, dyb_ref), (yc_ref, dyc_ref))):
            cols = slice(a * d, (a + 1) * d)
            s = _sigmoid(g_ref[:, cols])
            dy_ref[...] = (dmv * s).astype(dy_ref.dtype)
            dg_ref[:, cols] = (dmv * y_ref[...] * s * (1.0 - s)).astype(dg_ref.dtype)

    row = pl.BlockSpec((tm, d), lambda i: (i, 0))
    wide = pl.BlockSpec((tm, 3 * d), lambda i: (i, 0))
    dg, dya, dyb, dyc = pl.pallas_call(
        body, name=name, grid=(t // tm,), in_specs=[row, wide, row, row, row],
        out_specs=[wide, row, row, row],
        out_shape=[_sds((t, 3 * d), BF16)] + [_sds((t, d), BF16)] * 3,
        compiler_params=_cparams(1))(dm, gates, ya, yb, yc)
    return dg, [dya, dyb, dyc]


def _iota2(shape, dim):
    return lax.broadcasted_iota(jnp.int32, shape, dim)


def forget_cumsum(f, name):
    t = f.shape[0]
    nq = t // QB

    def body(f_ref, fcol_ref, frow_ref, carry):
        j = pl.program_id(0)

        @pl.when(j == 0)
        def _():
            carry[...] = jnp.zeros_like(carry)

        logf = _log_sigmoid(f_ref[...])
        tri = (_iota2((QB, QB), 1) <= _iota2((QB, QB), 0)).astype(F32)
        blk = jnp.dot(tri, logf, precision=HIGHEST, preferred_element_type=F32) + carry[...]
        carry[...] += jnp.sum(logf, axis=0, keepdims=True)
        fcol_ref[...] = blk
        frow_ref[...] = blk.T[0:8, :]

    return pl.pallas_call(
        body, name=name, grid=(nq,),
        in_specs=[pl.BlockSpec((QB, LANES), lambda j: (j, 0))],
        out_specs=[pl.BlockSpec((QB, LANES), lambda j: (j, 0)),
                   pl.BlockSpec((None, 8, QB), lambda j: (j, 0, 0))],
        out_shape=[_sds((t, LANES), F32), _sds((nq, 8, QB), F32)],
        scratch_shapes=[pltpu.VMEM((1, LANES), F32)], compiler_params=_cparams(1))(f)


def forget_cumsum_bwd(dfrow, f, name):
    t = f.shape[0]
    nq = t // QB

    def body(dfr_ref, f_ref, df_ref, carry):
        jj = pl.program_id(0)

        @pl.when(jj == 0)
        def _():
            carry[...] = jnp.zeros_like(carry)

        padded = jnp.concatenate([dfr_ref[...], jnp.zeros((QB - 8, QB), F32)], axis=0)
        dfcol = padded.T
        tri = (_iota2((QB, QB), 1) >= _iota2((QB, QB), 0)).astype(F32)
        dlogf = jnp.dot(tri, dfcol, precision=HIGHEST, preferred_element_type=F32) + carry[...]
        carry[...] += jnp.sum(dfcol, axis=0, keepdims=True)
        df_ref[...] = dlogf * _sigmoid(-f_ref[...])

    return pl.pallas_call(
        body, name=name, grid=(nq,),
        in_specs=[pl.BlockSpec((None, 8, QB), lambda jj: (nq - 1 - jj, 0, 0)),
                  pl.BlockSpec((QB, LANES), lambda jj: (nq - 1 - jj, 0))],
        out_specs=pl.BlockSpec((QB, LANES), lambda jj: (nq - 1 - jj, 0)),
        out_shape=_sds((t, LANES), F32),
        scratch_shapes=[pltpu.VMEM((1, LANES), F32)], compiler_params=_cparams(1))(dfrow, f)


REL_DIAG = 768
REL_SHIFT = REL_DIAG - (QB - 1)


def _diag_onehot():
    u = _iota2((REL_PAD, REL_DIAG), 1)
    rel = jnp.clip(CH_KEYS - 1 - u, -MAX_REL, MAX_REL) + MAX_REL
    return (_iota2((REL_PAD, REL_DIAG), 0) == rel).astype(F32)


def rel_bias_build(tab_t, name):
    def body(tab_ref, o_ref):
        diag = jnp.dot(tab_ref[...], _diag_onehot(), precision=HIGHEST, preferred_element_type=F32)
        band = _chunk_band()
        for h in range(N_HEADS_CH):
            rows = jnp.broadcast_to(diag[h:h + 1, :], (QB, REL_DIAG))
            o_ref[h] = pltpu.roll(rows, REL_SHIFT, 1, stride=1, stride_axis=0)[:, :CH_KEYS] + band

    return pl.pallas_call(
        body, name=name, out_shape=_sds((N_HEADS_CH, QB, CH_KEYS), F32),
        in_specs=[pl.BlockSpec(memory_space=pltpu.VMEM)], out_specs=pl.BlockSpec(memory_space=pltpu.VMEM),
    )(tab_t)


def rel_bias_scatter(dbias, name):
    def body(db_ref, o_ref, ddiag):
        flip = (_iota2((QB, QB), 0) + _iota2((QB, QB), 1) == QB - 1).astype(F32)
        for h in range(N_HEADS_CH):
            padded = jnp.concatenate([db_ref[h], jnp.zeros((QB, REL_DIAG - CH_KEYS), F32)], axis=1)
            flipped = jnp.dot(flip, padded, precision=HIGHEST, preferred_element_type=F32)
            unrolled = pltpu.roll(flipped, 0, 1, stride=1, stride_axis=0)
            ddiag[h:h + 1, :] = jnp.sum(unrolled, axis=0, keepdims=True)
        o_ref[...] = lax.dot_general(ddiag[...], _diag_onehot(), NT, precision=HIGHEST,
                                     preferred_element_type=F32)

    return pl.pallas_call(
        body, name=name, out_shape=_sds((N_HEADS_CH, REL_PAD), F32),
        in_specs=[pl.BlockSpec(memory_space=pltpu.VMEM)], out_specs=pl.BlockSpec(memory_space=pltpu.VMEM),
        scratch_shapes=[pltpu.VMEM((N_HEADS_CH, REL_DIAG), F32)],
    )(dbias)


def _hl(h):
    return slice(h * HEAD_DIM, (h + 1) * HEAD_DIM)


def _split_dot(x, tri_bf16):
    hi = x.astype(BF16)
    lo = (x - hi.astype(F32)).astype(BF16)
    return (jnp.dot(hi, tri_bf16, preferred_element_type=F32)
            + jnp.dot(lo, tri_bf16, preferred_element_type=F32))


def _rows(j):
    return pl.ds(pl.multiple_of(j * QB, QB), QB)


def _krows(g):
    return pl.ds(pl.multiple_of(g * KB, KB), KB)


def _log_sigmoid_pair(z):
    sp = jnp.log(1.0 + jnp.exp(-jnp.abs(z)))
    return jnp.minimum(z, 0.0) - sp, -jnp.maximum(z, 0.0) - sp


def _qkv_specs(t, col0, n_pairs):
    q_spec = pl.BlockSpec((QB, LANES), lambda hp, i: (i, col0 + hp))
    k_spec = pl.BlockSpec((t, LANES), lambda hp, i: (0, col0 + n_pairs + hp))
    v_spec = pl.BlockSpec((t, LANES), lambda hp, i: (0, col0 + 2 * n_pairs + hp))
    return q_spec, k_spec, v_spec


def _keys_major(xt):
    pairs, groups, _, _ = xt.shape
    return xt.transpose(1, 3, 0, 2).reshape(groups * KB, pairs * LANES)


def sb_fwd(qkv, name):
    t = qkv.shape[0]
    nq = t // QB

    def body(q_ref, k_ref, v_ref, o_ref, w_ref):
        i = pl.program_id(1)
        groups = i // KSUB + 1
        tri_after = (_iota2((KB, KB), 0) > _iota2((KB, KB), 1)).astype(BF16)
        t_idx = i * QB + _iota2((QB, KB), 0)
        qs = [q_ref[:, _hl(h)] for h in range(2)]

        def step(g, carry, masked):
            strict = (g * KB + _iota2((QB, KB), 1)) < t_idx
            out = []
            for h in range(2):
                tail, acc = carry[2 * h], carry[2 * h + 1]
                k = k_ref[_krows(g), _hl(h)]
                v = v_ref[_krows(g), _hl(h)]
                z = lax.dot_general(qs[h], k, NT, preferred_element_type=F32)
                lb, lf = _log_sigmoid_pair(z)
                if masked:
                    lf = jnp.where(strict, lf, 0.0)
                between = _split_dot(lf, tri_after) + tail
                w = jnp.exp(lb + between)
                if masked:
                    w = jnp.where(strict, w, 0.0)
                w = w.astype(BF16)
                w_ref[h, g] = w
                acc = acc + jnp.dot(w, v, preferred_element_type=F32)
                out += [tail + jnp.sum(lf, axis=1, keepdims=True), acc]
            return tuple(out)

        init = (jnp.zeros((QB, 1), F32), jnp.zeros((QB, HEAD_DIM), F32)) * 2
        res = step(groups - 1, init, True)
        res = lax.fori_loop(0, groups - 1, lambda gg, c: step(groups - 2 - gg, c, False), res)
        for h in range(2):
            o_ref[:, _hl(h)] = res[2 * h + 1].astype(o_ref.dtype)

    q_spec, k_spec, v_spec = _qkv_specs(t, 0, 2)
    return pl.pallas_call(
        body, name=name, grid=(2, nq), in_specs=[q_spec, k_spec, v_spec],
        out_specs=[pl.BlockSpec((QB, LANES), lambda hp, i: (i, hp)),
                   pl.BlockSpec((2, None, t // KB, QB, KB), lambda hp, i: (hp, i, 0, 0, 0))],
        out_shape=[_sds((t, W_SB), BF16), _sds((4, nq, t // KB, QB, KB), BF16)],
        compiler_params=_cparams(2))(qkv, qkv, qkv)


def _hs(h):
    return slice(h * HEAD_DIM, (h + 1) * HEAD_DIM)


def sb_bwd(qkv, qkv_t, w, do, do_t, name):
    t = qkv.shape[0]
    nq = t // QB

    def body(q_ref, k_ref, v_ref, do_ref, qt_ref, dot_ref, w_ref, dq_ref, dkt_ref, dvt_ref):
        i = pl.program_id(1)

        @pl.when(i == 0)
        def _():
            dkt_ref[...] = jnp.zeros_like(dkt_ref)
            dvt_ref[...] = jnp.zeros_like(dvt_ref)

        groups = i // KSUB + 1
        tri_before = (_iota2((KB, KB), 0) < _iota2((KB, KB), 1)).astype(BF16)
        t_idx = i * QB + _iota2((QB, KB), 0)
        qs = [q_ref[:, _hl(h)] for h in range(2)]
        dos = [do_ref[:, _hl(h)] for h in range(2)]
        qts = [qt_ref[_hs(h), :] for h in range(2)]
        dots = [dot_ref[_hs(h), :] for h in range(2)]

        def grads(g, carry, masked):
            strict = (g * KB + _iota2((QB, KB), 1)) < t_idx
            out = []
            for h in range(2):
                head, dq = carry[2 * h], carry[2 * h + 1]
                k = k_ref[_krows(g), _hl(h)]
                v = v_ref[_krows(g), _hl(h)]
                wb = w_ref[h, g]
                z = lax.dot_general(qs[h], k, NT, preferred_element_type=F32)
                beta = _sigmoid(z)
                e = lax.dot_general(dos[h], v, NT, preferred_element_type=F32) * wb.astype(F32)
                before = _split_dot(e, tri_before) + head
                dz = e * (1.0 - beta) - before * beta
                if masked:
                    dz = jnp.where(strict, dz, 0.0)
                dzb = dz.astype(BF16)
                dq = dq + jnp.dot(dzb, k, preferred_element_type=F32)
                dkt_ref[g, _hs(h), :] += jnp.dot(qts[h], dzb, preferred_element_type=F32)
                dvt_ref[g, _hs(h), :] += jnp.dot(dots[h], wb, preferred_element_type=F32)
                out += [head + jnp.sum(e, axis=1, keepdims=True), dq]
            return tuple(out)

        init = (jnp.zeros((QB, 1), F32), jnp.zeros((QB, HEAD_DIM), F32)) * 2
        res = lax.fori_loop(0, groups - 1, lambda g, c: grads(g, c, False), init)
        res = grads(groups - 1, res, True)
        for h in range(2):
            dq_ref[:, _hl(h)] = (res[2 * h + 1] * SCALE).astype(dq_ref.dtype)

    q_spec, k_spec, v_spec = _qkv_specs(t, 0, 2)
    blk = pl.BlockSpec((QB, LANES), lambda hp, i: (i, hp))
    blk_t = pl.BlockSpec((LANES, QB), lambda hp, i: (hp, i))
    acc_t = pl.BlockSpec((None, t // KB, LANES, KB), lambda hp, i: (hp, 0, 0, 0))
    acc_sds = _sds((2, t // KB, LANES, KB), F32)
    return pl.pallas_call(
        body, name=name, grid=(2, nq),
        in_specs=[q_spec, k_spec, v_spec, blk, blk_t, blk_t,
                  pl.BlockSpec((2, None, t // KB, QB, KB), lambda hp, i: (hp, i, 0, 0, 0))],
        out_specs=[blk, acc_t, acc_t],
        out_shape=[_sds((t, W_SB), BF16), acc_sds, acc_sds],
        compiler_params=_cparams(2))(qkv, qkv, qkv, do, qkv_t, do_t, w)


def fox_fwd(qkv, fcol, frow, name):
    t = qkv.shape[0]
    nq = t // QB

    def body(q_ref, k_ref, v_ref, fc_ref, fr_ref, o_ref, lse_ref):
        hp = pl.program_id(0)
        i = pl.program_id(1)
        groups = i // KSUB + 1
        t_idx = i * QB + _iota2((QB, KB), 0)
        lane = _iota2((QB, LANES), 1)
        sub = _iota2((8, KB), 0)
        qs = [q_ref[:, _hl(h)] for h in range(2)]
        f_qs = [jnp.sum(jnp.where(lane == hp * 2 + h, fc_ref[...], 0.0), axis=1, keepdims=True)
                for h in range(2)]

        def step(g, carry, masked):
            causal = (g * KB + _iota2((QB, KB), 1)) <= t_idx
            fr = fr_ref[g]
            out = []
            for h in range(2):
                m, l, acc = carry[3 * h:3 * h + 3]
                k = k_ref[_krows(g), _hl(h)]
                v = v_ref[_krows(g), _hl(h)]
                f_k = jnp.sum(jnp.where(sub == hp * 2 + h, fr, 0.0), axis=0, keepdims=True)
                z = lax.dot_general(qs[h], k, NT, preferred_element_type=F32) + f_qs[h] - f_k
                if masked:
                    z = jnp.where(causal, z, NEG)
                m_new = jnp.maximum(m, jnp.max(z, axis=1, keepdims=True))
                p = jnp.exp(z - m_new)
                corr = jnp.exp(m - m_new)
                l = l * corr + jnp.sum(p, axis=1, keepdims=True)
                acc = acc * corr + jnp.dot(p.astype(BF16), v, preferred_element_type=F32)
                out += [m_new, l, acc]
            return tuple(out)

        init = (jnp.full((QB, 1), NEG, F32), jnp.zeros((QB, 1), F32), jnp.zeros((QB, HEAD_DIM), F32)) * 2
        res = lax.fori_loop(0, groups - 1, lambda g, c: step(g, c, False), init)
        res = step(groups - 1, res, True)
        for h in range(2):
            m, l, acc = res[3 * h:3 * h + 3]
            o_ref[:, _hl(h)] = (acc / l).astype(o_ref.dtype)
            lse_ref[:, _hl(h)] = jnp.broadcast_to(m + jnp.log(l), (QB, HEAD_DIM))

    q_spec, k_spec, v_spec = _qkv_specs(t, 18, 2)
    blk = pl.BlockSpec((QB, LANES), lambda hp, i: (i, hp))
    return pl.pallas_call(
        body, name=name, grid=(2, nq),
        in_specs=[q_spec, k_spec, v_spec, pl.BlockSpec((QB, LANES), lambda hp, i: (i, 0)),
                  pl.BlockSpec((t // KB, 8, KB), lambda hp, i: (0, 0, 0))],
        out_specs=[blk, blk],
        out_shape=[_sds((t, W_FOX), BF16), _sds((t, W_FOX), F32)],
        compiler_params=_cparams(2))(qkv, qkv, qkv, fcol, frow)


def fox_bwd(qkv, qkv_t, fcol, frow, o, lse, do, do_t, name, do_col=0):
    t = qkv.shape[0]
    nq = t // QB

    def body(q_ref, k_ref, v_ref, fc_ref, fr_ref, o_ref, lse_ref, do_ref, qt_ref, dot_ref,
             dq_ref, dk_ref, dv_ref, dfr_ref):
        hp = pl.program_id(0)
        i = pl.program_id(1)
        qts = [qt_ref[_hs(h), :] for h in range(2)]
        dots = [dot_ref[_hs(h), :] for h in range(2)]

        @pl.when(i == 0)
        def _():
            dk_ref[...] = jnp.zeros_like(dk_ref)
            dv_ref[...] = jnp.zeros_like(dv_ref)

        @pl.when((i == 0) & (hp == 0))
        def _():
            dfr_ref[...] = jnp.zeros_like(dfr_ref)

        groups = i // KSUB + 1
        t_idx = i * QB + _iota2((QB, KB), 0)
        lane = _iota2((QB, LANES), 1)
        sub = _iota2((8, KB), 0)
        qs = [q_ref[:, _hl(h)] for h in range(2)]
        dos = [do_ref[:, _hl(h)] for h in range(2)]
        f_qs = [jnp.sum(jnp.where(lane == hp * 2 + h, fc_ref[...], 0.0), axis=1, keepdims=True)
                for h in range(2)]
        lse_qs = [lse_ref[:, h * HEAD_DIM:h * HEAD_DIM + 1] for h in range(2)]
        deltas = [jnp.sum(dos[h].astype(F32) * o_ref[:, _hl(h)].astype(F32), axis=1, keepdims=True)
                  for h in range(2)]

        def step(g, dqs, masked):
            causal = (g * KB + _iota2((QB, KB), 1)) <= t_idx
            fr = fr_ref[g]
            out = []
            dfr = jnp.zeros((8, KB), F32)
            for h in range(2):
                k = k_ref[_krows(g), _hl(h)]
                v = v_ref[_krows(g), _hl(h)]
                f_k = jnp.sum(jnp.where(sub == hp * 2 + h, fr, 0.0), axis=0, keepdims=True)
                z = lax.dot_general(qs[h], k, NT, preferred_element_type=F32) + f_qs[h] - f_k
                p = jnp.exp(z - lse_qs[h])
                if masked:
                    p = jnp.where(causal, p, 0.0)
                dp = lax.dot_general(dos[h], v, NT, preferred_element_type=F32)
                ds = p * (dp - deltas[h])
                dsb = ds.astype(BF16)
                out.append(dqs[h] + jnp.dot(dsb, k, preferred_element_type=F32))
                dk_ref[g, _hs(h), :] += jnp.dot(qts[h], dsb, preferred_element_type=F32)
                dv_ref[g, _hs(h), :] += jnp.dot(dots[h], p.astype(BF16), preferred_element_type=F32)
                colsum = jnp.sum(ds, axis=0, keepdims=True)
                dfr = dfr + jnp.where(sub == hp * 2 + h, -colsum, 0.0)
            dfr_ref[g] += dfr
            return tuple(out)

        res = lax.fori_loop(0, groups - 1, lambda g, c: step(g, c, False),
                            (jnp.zeros((QB, HEAD_DIM), F32),) * 2)
        res = step(groups - 1, res, True)
        for h in range(2):
            dq_ref[:, _hl(h)] = (res[h] * SCALE).astype(dq_ref.dtype)

    q_spec, k_spec, v_spec = _qkv_specs(t, 18, 2)
    blk = pl.BlockSpec((QB, LANES), lambda hp, i: (i, hp))
    frs = pl.BlockSpec((t // KB, 8, KB), lambda hp, i: (0, 0, 0))
    acc_t = pl.BlockSpec((None, t // KB, LANES, KB), lambda hp, i: (hp, 0, 0, 0))
    acc_sds = _sds((2, t // KB, LANES, KB), F32)
    return pl.pallas_call(
        body, name=name, grid=(2, nq),
        in_specs=[q_spec, k_spec, v_spec, pl.BlockSpec((QB, LANES), lambda hp, i: (i, 0)), frs,
                  blk, blk, pl.BlockSpec((QB, LANES), lambda hp, i: (i, do_col + hp)),
                  pl.BlockSpec((LANES, QB), lambda hp, i: (18 + hp, i)),
                  pl.BlockSpec((LANES, QB), lambda hp, i: (do_col + hp, i))],
        out_specs=[blk, acc_t, acc_t, frs],
        out_shape=[_sds((t, W_FOX), BF16), acc_sds, acc_sds, _sds((t // KB, 8, KB), F32)],
        compiler_params=_cparams(2))(qkv, qkv, qkv, fcol, frow, o, lse, do, qkv_t, do_t)


def _frow_to_groups(frow):
    n = frow.shape[0] // KSUB
    return frow.reshape(n, KSUB, 8, QB).transpose(0, 2, 1, 3).reshape(n, 8, KB)


def _frow_from_groups(frow):
    n = frow.shape[0]
    return frow.reshape(n, 8, KSUB, QB).transpose(0, 2, 1, 3).reshape(n * KSUB, 8, QB)


def _chunk_band():
    qi = _iota2((QB, CH_KEYS), 0)
    kj = _iota2((QB, CH_KEYS), 1)
    dchunk = (qi >> 6) + LEFT_CHUNKS - (kj >> 6)
    return jnp.where((dchunk >= 0) & (dchunk <= LEFT_CHUNKS), 0.0, NEG)


def _chunk_pad_row(i):
    kj = _iota2((1, CH_KEYS), 1)
    return jnp.where((i - (CH_WIN - 1)) * QB + kj >= 0, 0.0, NEG)


CH_PAD = (CH_WIN - 1) * QB
CH_STEP_HEADS = 4
CH_COLS = CH_STEP_HEADS * HEAD_DIM


def _window(i):
    return pl.ds(pl.multiple_of(i * QB, QB), CH_KEYS)


def _chunk_weights(q, kw, bias, pad_row):
    z = lax.dot_general(q, kw, NT, preferred_element_type=F32) + bias + pad_row
    e = jnp.exp(z - jnp.max(z, axis=1, keepdims=True))
    return e, 1.0 / jnp.sum(e, axis=1, keepdims=True)


def _chunk_specs(t):
    q_spec = pl.BlockSpec((QB, CH_COLS), lambda hp, i: (i, 3 * W_SB // CH_COLS + hp))
    kv_spec = pl.BlockSpec((t + CH_PAD, CH_COLS), lambda hp, i: (0, hp))
    return q_spec, kv_spec


def chunk_fwd(qkv, kp, vp, bias, name):
    t = qkv.shape[0]
    nq = t // QB

    def body(q_ref, k_ref, v_ref, b_ref, o_ref):
        i = pl.program_id(1)
        pad_row = _chunk_pad_row(i)
        for h in range(CH_STEP_HEADS):
            e, inv = _chunk_weights(q_ref[:, _hl(h)], k_ref[_window(i), _hl(h)], b_ref[h], pad_row)
            o = jnp.dot(e.astype(BF16), v_ref[_window(i), _hl(h)], preferred_element_type=F32)
            o_ref[:, _hl(h)] = (o * inv).astype(o_ref.dtype)

    q_spec, kv_spec = _chunk_specs(t)
    return pl.pallas_call(
        body, name=name, grid=(W_CH // CH_COLS, nq),
        in_specs=[q_spec, kv_spec, kv_spec,
                  pl.BlockSpec((CH_STEP_HEADS, QB, CH_KEYS), lambda hp, i: (hp, 0, 0))],
        out_specs=pl.BlockSpec((QB, CH_COLS), lambda hp, i: (i, hp)),
        out_shape=_sds((t, W_CH), BF16), compiler_params=_cparams(2))(qkv, kp, vp, bias)


def chunk_bwd(qkv, qkv_t, kp, vp, bias, do, do_t, name, do_col=0):
    t = qkv.shape[0]
    nq = t // QB

    def body(q_ref, k_ref, v_ref, b_ref, do_ref, qt_ref, dot_ref, dq_ref, dk_ref, dv_ref, db_ref):
        i = pl.program_id(1)

        @pl.when(i == 0)
        def _():
            dk_ref[...] = jnp.zeros_like(dk_ref)
            dv_ref[...] = jnp.zeros_like(dv_ref)
            db_ref[...] = jnp.zeros_like(db_ref)

        pad_row = _chunk_pad_row(i)
        for h in range(CH_STEP_HEADS):
            q = q_ref[:, _hl(h)]
            dov = do_ref[:, _hl(h)]
            kw = k_ref[_window(i), _hl(h)]
            e, inv = _chunk_weights(q, kw, b_ref[h], pad_row)
            p = e * inv
            dp = lax.dot_general(dov, v_ref[_window(i), _hl(h)], NT, preferred_element_type=F32)
            ds = p * (dp - jnp.sum(p * dp, axis=1, keepdims=True))
            db_ref[h] += ds
            dsb = ds.astype(BF16)
            dq_ref[:, _hl(h)] = (jnp.dot(dsb, kw, preferred_element_type=F32) * SCALE).astype(dq_ref.dtype)
            dkt = jnp.dot(qt_ref[_hs(h), :], dsb, preferred_element_type=F32)
            dvt = jnp.dot(dot_ref[_hs(h), :], p.astype(BF16), preferred_element_type=F32)
            for b in range(CH_WIN):
                dk_ref[i + b, _hs(h), :] += dkt[:, b * QB:(b + 1) * QB]
                dv_ref[i + b, _hs(h), :] += dvt[:, b * QB:(b + 1) * QB]

    q_spec, kv_spec = _chunk_specs(t)
    blk = pl.BlockSpec((QB, CH_COLS), lambda hp, i: (i, hp))
    bspec = pl.BlockSpec((CH_STEP_HEADS, QB, CH_KEYS), lambda hp, i: (hp, 0, 0))
    nblk = nq + CH_WIN - 1
    acc_t = pl.BlockSpec((None, nblk, CH_COLS, QB), lambda hp, i: (hp, 0, 0, 0))
    acc_sds = _sds((W_CH // CH_COLS, nblk, CH_COLS, QB), F32)
    return pl.pallas_call(
        body, name=name, grid=(W_CH // CH_COLS, nq),
        in_specs=[q_spec, kv_spec, kv_spec, bspec,
                  pl.BlockSpec((QB, CH_COLS), lambda hp, i: (i, do_col + hp)),
                  pl.BlockSpec((CH_COLS, QB), lambda hp, i: (3 * W_SB // CH_COLS + hp, i)),
                  pl.BlockSpec((CH_COLS, QB), lambda hp, i: (do_col + hp, i))],
        out_specs=[blk, acc_t, acc_t, bspec],
        out_shape=[_sds((t, W_CH), BF16), acc_sds, acc_sds, _sds((N_HEADS_CH, QB, CH_KEYS), F32)],
        compiler_params=_cparams(2))(qkv, kp, vp, bias, do, qkv_t, do_t)


def _sum_parts(p_ref):
    total = p_ref[0].astype(F32)
    for p in range(1, p_ref.shape[0]):
        total = total + p_ref[p].astype(F32)
    return total


def sum_parts(parts, grid, p_spec, o_spec, out_sds, name):
    def body(p_ref, o_ref):
        o_ref[...] = _sum_parts(p_ref)

    return pl.pallas_call(body, name=name, grid=grid, in_specs=[p_spec], out_specs=o_spec,
                          out_shape=out_sds, compiler_params=_cparams(len(grid)))(parts)


def adamw(parts, w, m, v, grid, p_specs, w_spec, name):
    c1 = 1.0 / (1.0 - ADAM_B1 ** ADAM_STEP)
    c2 = 1.0 / (1.0 - ADAM_B2 ** ADAM_STEP)
    n = len(parts)

    def body(*refs):
        w_ref, m_ref, v_ref, g_out, d_out, m_out, v_out = refs[n:]
        g = _sum_parts(refs[0])
        for q in range(1, n):
            g = jnp.where(pl.program_id(0) == q, _sum_parts(refs[q]), g)
        m_new = ADAM_B1 * m_ref[...] + (1.0 - ADAM_B1) * g
        v_new = ADAM_B2 * v_ref[...] + (1.0 - ADAM_B2) * (g * g)
        m_hat = m_new * c1
        v_hat = v_new * c2
        g_out[...] = g
        d_out[...] = -ADAM_LR * (m_hat / (jnp.sqrt(v_hat) + ADAM_EPS) + ADAM_WD * w_ref[...])
        m_out[...] = m_new
        v_out[...] = v_new

    out = _sds(w.shape, F32)
    return pl.pallas_call(
        body, name=name, grid=grid, in_specs=[*p_specs, w_spec, w_spec, w_spec],
        out_specs=[w_spec] * 4, out_shape=[out] * 4,
        compiler_params=_cparams(len(grid)))(*parts, w, m, v)


def _ffn_fwd(x, gain, wa, wb_after, s, tm, tag, on_event, deps=()):
    t = x.shape[0]
    hn = rmsnorm_fwd(x, gain, tm, f"rms_{tag}", deps)
    gu, act = ffn_in_swiglu(hn, wa, s, min(2 * tm, t), f"ffn_in_{tag}")
    relayed = on_event("act", act)
    wb = wb_after(act)
    y = ffn_out_residual(act, wb, x, s, min(2 * tm, t), f"ffn_out_{tag}", relayed)
    return y, (hn, gu, act), wb


def _ffn_bwd(dy, x, gain, saved, wa, wb, s, tm, tag, on_grads):
    t = x.shape[0]
    hn, gu, act = saved
    dgu = ffn_dact_swiglu(dy, wb, gu, s, min(2 * tm, t), f"ffn_dact_{tag}")
    dwb = matmul(TN, act, dy, _sds((4, FF_BLK, D_MODEL), BF16), (4, 1, 1),
                 pl.BlockSpec((None, t, FF_BLK), lambda i, j, k: (i, 0, 0)),
                 pl.BlockSpec((t, D_MODEL), lambda i, j, k: (0, 0)),
                 pl.BlockSpec((None, FF_BLK, D_MODEL), lambda i, j, k: (i, 0, 0)),
                 None, name=f"ffn_dwout_{tag}", alpha=0.5)
    dwa = matmul(TN, dgu, hn, _sds((8, FF_BLK, D_MODEL), BF16), (1, 8, 1),
                 pl.BlockSpec((None, None, t, FF_BLK), lambda i, j, k: (j % 4, j // 4, 0, 0)),
                 pl.BlockSpec((t, D_MODEL), lambda i, j, k: (0, 0)),
                 pl.BlockSpec((None, FF_BLK, D_MODEL), lambda i, j, k: (j, 0, 0)),
                 None, name=f"ffn_dwin_{tag}")
    deps = on_grads(dwa, dwb)
    return ffn_dh_norm_bwd(dgu, wa, s, x, gain, dy, tm, f"ffn_dh_{tag}", deps)


BR_ROWS = ((0, 1), (1, 2), (3, 1))

_Q_COLUMN_SCALE = np.ones((1, QKV_WIDTH), np.float32)
for _lo, _width in ((0, W_SB), (3 * W_SB, W_CH), (3 * (W_SB + W_CH), W_FOX)):
    _Q_COLUMN_SCALE[0, _lo:_lo + _width] = SCALE


def _mixer_fwd(x, gain, wqkv, wf, wgate, late_after, bq, bf, bg, bias, layer, tm, tag, on_event):
    t = x.shape[0]
    nt = t // tm
    hm = rmsnorm_fwd(x, gain, tm, f"rms_{tag}")
    a_full = pl.BlockSpec((tm, D_MODEL), lambda i, j, k: (i, 0))
    wide_out = pl.BlockSpec((tm, D_MODEL), lambda i, j, k: (i, j))
    wide_b = pl.BlockSpec((1, D_MODEL), lambda i, j, k: (0, j))
    qkv, qkv_t = matmul(NN, hm, wqkv, _sds((t, QKV_WIDTH), BF16), (nt, 3, 1), a_full,
                        pl.BlockSpec((None, D_MODEL, D_MODEL), lambda i, j, k: (layer, 0, j)), wide_out, None,
                        name=f"proj_qkv_{tag}", bias=bq, bias_spec=wide_b,
                        scale=jnp.asarray(_Q_COLUMN_SCALE), scale_spec=wide_b,
                        out_t_sds=_sds((QKV_WIDTH, t), BF16),
                        out_t_spec=pl.BlockSpec((D_MODEL, tm), lambda i, j, k: (j, i)))
    relayed = on_event("qkv", qkv)
    gates = matmul(NN, hm, wgate, _sds((t, 3 * D_MODEL), F32), (nt, 3, 1), a_full,
                   pl.BlockSpec((None, D_MODEL, D_MODEL), lambda i, j, k: (layer + 1, 0,j)), wide_out,
                   None, name=f"proj_gate_{tag}", bias=bg, bias_spec=wide_b, deps=relayed)
    f = matmul(NN, hm, wf, _sds((t, LANES), F32), (nt, 1, 1), a_full,
               pl.BlockSpec((None, D_MODEL, LANES), lambda i, j, k: (layer, 0, 0)),
               pl.BlockSpec((tm, LANES), lambda i, j, k: (i, 0)), None,
               name=f"proj_f_{tag}", bias=bf, bias_spec=pl.BlockSpec((1, LANES), lambda i, j, k: (0, 0)))
    fcol, frow = forget_cumsum(f, f"fcum_{tag}")
    frow = _frow_to_groups(frow)
    o_sb, w_sb = sb_fwd(qkv, f"sb_fwd_{tag}")
    relayed = on_event("o_sb", o_sb)
    kp = jnp.pad(qkv[:, 10 * LANES:14 * LANES], ((CH_PAD, 0), (0, 0)))
    vp = jnp.pad(qkv[:, 14 * LANES:18 * LANES], ((CH_PAD, 0), (0, 0)))
    o_ch = chunk_fwd(qkv, kp, vp, bias, f"chunk_fwd_{tag}")
    o_fox, lse = fox_fwd(qkv, fcol, frow, f"fox_fwd_{tag}")
    wbr, wout = late_after(o_fox)
    ys, merged = branch_merge((o_sb, o_ch, o_fox), wbr, layer, gates, tm, f"branch_merge_{tag}", relayed)
    x_new = matmul(NN, merged, wout, _sds((t, D_MODEL), F32), (nt, 1, 1), a_full,
                   pl.BlockSpec((None, D_MODEL, D_MODEL), lambda i, j, k: (layer, 0, 0)), a_full, None,
                   name=f"wout_{tag}", res=x, res_spec=a_full)
    saved = (hm, qkv, gates, f, fcol, frow, o_sb, o_ch, o_fox, lse, ys, merged, kp, vp, w_sb, qkv_t)
    return x_new, saved, wbr, wout


def _mixer_bwd(dy, x, gain, saved, wqkv, wf, wgate, wbr, wout, bias, layer, tm, tag, on_grads):
    t = x.shape[0]
    nt = t // tm
    hm, qkv, gates, f, fcol, frow, o_sb, o_ch, o_fox, lse, ys, merged, kp, vp, w_sb, qkv_t = saved
    a_full = pl.BlockSpec((tm, D_MODEL), lambda i, j, k: (i, 0))
    red_row = pl.BlockSpec((tm, D_MODEL), lambda i, j, k: (k, 0))
    sq = pl.BlockSpec((D_MODEL, D_MODEL), lambda i, j, k: (0, 0))
    dgates, dys = dmerged_merge_bwd(dy, wout, layer, gates, ys, tm // 2, f"dmerged_{tag}")
    all_t = pl.BlockSpec((t, D_MODEL), lambda i, j, k: (0, 0))
    dwout = matmul(TN, merged, dy, _sds((D_MODEL, D_MODEL), BF16), (1, 1, 1), all_t, all_t, sq,
                   None, name=f"dwout_{tag}")
    do, do_t, dwbr = branch_bwd(dys, (o_sb, o_ch, o_fox), wbr, layer, tm, f"dbranch_{tag}")
    dq_a, dk_a, dv_a = sb_bwd(qkv, qkv_t, w_sb, do, do_t, f"sb_bwd_{tag}")
    dk_a, dv_a = _keys_major(dk_a), _keys_major(dv_a)
    dq_b, dk_b, dv_b, dbias = chunk_bwd(qkv, qkv_t, kp, vp, bias, do, do_t, f"chunk_bwd_{tag}",
                                        do_col=W_SB // CH_COLS)
    dk_b, dv_b = [x[:, CH_WIN - 1:].transpose(1, 3, 0, 2).reshape(t, W_CH) for x in (dk_b, dv_b)]
    dq_c, dk_c, dv_c, dfrow = fox_bwd(qkv, qkv_t, fcol, frow, o_fox, lse, do, do_t, f"fox_bwd_{tag}",
                                      do_col=(W_SB + W_CH) // LANES)
    dk_c, dv_c = _keys_major(dk_c), _keys_major(dv_c)
    df = forget_cumsum_bwd(_frow_from_groups(dfrow), f, f"fcum_bwd_{tag}")
    dqkv = jnp.concatenate([p.astype(BF16) for p in
                            (dq_a, dk_a, dv_a, dq_b, dk_b, dv_b, dq_c, dk_c, dv_c)], axis=1)
    dtab = rel_bias_scatter(dbias, f"rel_scatter_{tag}")

    all_rows = pl.BlockSpec((t, D_MODEL), lambda i, j, k: (0, 0))
    wide_b = pl.BlockSpec((t, D_MODEL), lambda i, j, k: (0, j))
    wide_o = pl.BlockSpec((D_MODEL, D_MODEL), lambda i, j, k: (0, j))
    wide_cs = pl.BlockSpec((1, D_MODEL), lambda i, j, k: (0, j))
    dwqkv, dbq = matmul(TN, hm, dqkv, _sds((D_MODEL, QKV_WIDTH), BF16), (1, 3, 1), all_rows, wide_b,
                        wide_o, None, name=f"dwqkv_{tag}",
                        colsum_sds=_sds((1, QKV_WIDTH), F32), colsum_spec=wide_cs)
    dwgate, dbg = matmul(TN, hm, dgates, _sds((D_MODEL, 3 * D_MODEL), BF16), (1, 3, 1), all_rows,
                         wide_b, wide_o, None, name=f"dwgate_{tag}",
                         colsum_sds=_sds((1, 3 * D_MODEL), F32), colsum_spec=wide_cs)
    dwf, dbf = matmul(TN, hm, df, _sds((D_MODEL, LANES), BF16), (1, 1, 1), all_rows,
                      pl.BlockSpec((t, LANES), lambda i, j, k: (0, 0)),
                      pl.BlockSpec((D_MODEL, LANES), lambda i, j, k: (0, 0)), None,
                      name=f"dwf_{tag}", colsum_sds=_sds((1, LANES), F32),
                      colsum_spec=pl.BlockSpec((1, LANES), lambda i, j, k: (0, 0)))
    deps = on_grads(dict(dwqkv=dwqkv, dwgate=dwgate, dwf=dwf, dwbr=dwbr, dwout=dwout))
    wide_a = pl.BlockSpec((tm, QKV_WIDTH), lambda i, j, k: (i, 0))
    dhm = matmul(NT, dqkv, wqkv, _sds((t, D_MODEL), F32), (nt, 1, 1), wide_a,
                 pl.BlockSpec((None, D_MODEL, QKV_WIDTH), lambda i, j, k: (layer, 0, 0)), a_full,
                 None, name=f"dhm_qkv_{tag}", deps=deps)
    dx, dgain = mixer_dh_norm_bwd(dhm, dgates, wgate, layer + 1, df, wf, layer, x, gain, dy, tm,
                                  f"dhm_gate_{tag}")
    return dx, dict(dbq=dbq, dbg=dbg, dbf=dbf, dtab=dtab, dgain=dgain)


def _pack_small(pieces):
    flat = jnp.concatenate([p.reshape(-1).astype(F32) for p in pieces])
    flat = jnp.pad(flat, (0, SMALL_ROWS * LANES - flat.shape[0]))
    return flat.reshape(SMALL_ROWS, LANES)


def _unpack_small(packed, shapes):
    flat = packed.reshape(-1)
    out, pos = [], 0
    for shp in shapes:
        n = int(np.prod(shp))
        out.append(flat[pos:pos + n].reshape(shp))
        pos += n
    return out


def kernel(x, g_ffn1, w_ffn1_in, w_ffn1_out, g_mix, w_in, b_in, rel_bias, w_br_sb, w_br_ch, w_br_fox, w_out, g_ffn2, w_ffn2_in, w_ffn2_out, g_final, loss_target, m_g_ffn1, m_w_ffn1_in, m_w_ffn1_out, m_g_mix, m_w_in, m_b_in, m_rel_bias, m_w_br_sb, m_w_br_ch, m_w_br_fox, m_w_out, m_g_ffn2, m_w_ffn2_in, m_w_ffn2_out, m_g_final, v_g_ffn1, v_w_ffn1_in, v_w_ffn1_out, v_g_mix, v_w_in, v_b_in, v_rel_bias, v_w_br_sb, v_w_br_ch, v_w_br_fox, v_w_out, v_g_ffn2, v_w_ffn2_in, v_w_ffn2_out, v_g_final):
    t = x.shape[1]
    tm = min(512, t)
    xs = x[0]
    target = loss_target[0]
    f_lo, f_hi = QKV_WIDTH, QKV_WIDTH + N_HEADS_FOX

    def ffn_shards(w_in_, w_out_, l):
        return [w_in_[l:l + 1].astype(BF16), w_out_[l:l + 1].astype(BF16)]

    def mixer_shards(l):
        wl = w_in[l]
        return [jnp.stack([wl[:, :QKV_WIDTH], wl[:, f_hi:]]).astype(BF16),
                jnp.pad(wl[:, f_lo:f_hi], ((0, 0), (0, LANES - N_HEADS_FOX)))[None].astype(BF16),
                w_out[l:l + 1].astype(BF16),
                jnp.concatenate([w_br_sb[l], w_br_ch[l], w_br_fox[l]], axis=0)[None].astype(BF16)]

    gathers = {}
    gather_tokens = []

    def start_gather(shards, name):
        handle = gather_start(shards, name, deps=gather_tokens[-1:])
        gather_tokens.append(handle["token"])
        return handle

    def relay(handle, after):
        if "send2" not in handle:
            gather_relay(handle, after)

    relay_on = {("mix", 0, "qkv"): ("mix", 0, 1), ("mix", 0, "o_sb"): ("ffn2", 0, 0),
                ("ffn2", 0, "act"): ("ffn1", 1, 0), ("ffn1", 1, "act"): ("mix", 1, 0),
                ("mix", 1, "qkv"): ("ffn2", 1, 0)}

    def on_event(grp, l):
        def fire(event, array):
            target = relay_on.get((grp, l, event))
            if target is None:
                return ()
            handle = gathers[target[:2]][target[2]]
            relay(handle, array)
            return (handle["relay_token"],)
        return fire

    for l in range(DEPTH):
        for grp, shards in (("ffn1", ffn_shards(w_ffn1_in, w_ffn1_out, l)), ("mix", mixer_shards(l)),
                            ("ffn2", ffn_shards(w_ffn2_in, w_ffn2_out, l))):
            cut = len(shards) // 2
            if l == 0 and grp != "ffn2":
                gathers[(grp, l)] = (start_gather(shards[:cut], f"gather_{grp}_l{l}_a"),
                                     start_gather(shards[cut:], f"gather_{grp}_l{l}_b"))
            else:
                gathers[(grp, l)] = (start_gather(shards, f"gather_{grp}_l{l}"),)

    def gathered(key, after):
        hs = gathers[key]
        cut = hs[0]["n"]
        relay(hs[0], after)
        first = gather_finish(hs[0], after)
        if len(hs) == 1:
            return first[:cut // 2], lambda later: first[cut // 2:]

        def second(later):
            relay(hs[1], later)
            return gather_finish(hs[1], later)

        return first, second

    def ffn_weights(key, after):
        (wa_,), rest = gathered(key, after)
        return wa_, lambda later: rest(later)[0].reshape(1, 4, FF_BLK, D_MODEL)

    def mixer_weights(key, after):
        (wc_, wf_), rest = gathered(key, after)

        def late(later):
            wout_, wbr_ = rest(later)
            return (wbr_.transpose(0, 2, 1, 3).reshape(1, D_MODEL, D_MODEL), wout_.reshape(1, D_MODEL, D_MODEL))

        return wc_.reshape(2, D_MODEL, QKV_WIDTH), wf_.reshape(1, D_MODEL, LANES), late

    bq = b_in[:, None, :QKV_WIDTH]
    bf = jnp.pad(b_in[:, f_lo:f_hi], ((0, 0), (0, LANES - N_HEADS_FOX)))[:, None, :]
    bg = b_in[:, None, f_hi:]
    tab_t = jnp.pad(rel_bias.transpose(0, 2, 1), ((0, 0), (0, 0), (0, REL_PAD - N_REL)))

    h = xs
    saved = []
    weights = []
    for l in range(DEPTH):
        bias = rel_bias_build(tab_t[l], f"rel_build_l{l}").reshape(N_HEADS_CH, QB, CH_KEYS)
        x0 = h
        wa1, wb1_after = ffn_weights(("ffn1", l), x0)
        x1, s1, wb1 = _ffn_fwd(x0, g_ffn1[l:l + 1], wa1, wb1_after, 0, tm, f"ffn1_l{l}", on_event("ffn1", l),
                               deps=gather_tokens if l == 0 else ())
        wc, wf, late_after = mixer_weights(("mix", l), x1)
        x2, sm, wbr, wout = _mixer_fwd(x1, g_mix[l:l + 1], wc, wf, wc, late_after, bq[l], bf[l], bg[l],
                                       bias, 0, tm, f"mix_l{l}", on_event("mix", l))
        wa2, wb2_after = ffn_weights(("ffn2", l), x2)
        x3, s2, wb2 = _ffn_fwd(x2, g_ffn2[l:l + 1], wa2, wb2_after, 0, tm, f"ffn2_l{l}", on_event("ffn2", l))
        saved.append((x0, x1, x2, s1, sm, s2, bias))
        weights.append(((wa1, wb1), (wc, wf, wout, wbr), (wa2, wb2)))
        h = x3

    dx, dg_final, loss_blk = loss_head(h, g_final[None, :], target, tm, "loss_head")

    g_mix_l = [None] * DEPTH
    dgains = {}
    scatters = {}

    def scatter_ffn(key):
        def on_grads(dwa, dwb):
            scatters[key] = exchange_start(
                "scatter", [dwa[None], dwb.reshape(1, N_DEV, D_FF // N_DEV, D_MODEL)],
                f"scatter_{key[0]}_l{key[1]}")
            return (scatters[key]["token"],)
        return on_grads

    def scatter_mixer(key):
        def on_grads(gm):
            scatters[key] = exchange_start(
                "scatter",
                [gm["dwqkv"].reshape(1, N_DEV, LANES, QKV_WIDTH), gm["dwgate"].reshape(1, N_DEV, LANES, QKV_WIDTH),
                 gm["dwf"].reshape(1, N_DEV, LANES, LANES), gm["dwout"].reshape(1, N_DEV, LANES, D_MODEL),
                 gm["dwbr"].reshape(1, D_MODEL, N_DEV, LANES).transpose(0, 2, 1, 3)],
                f"scatter_{key[0]}_l{key[1]}")
            return (scatters[key]["token"],)
        return on_grads

    for l in reversed(range(DEPTH)):
        x0, x1, x2, s1, sm, s2, bias = saved[l]
        w1, (wc, wf, wout, wbr), w2 = weights[l]
        dx, dgains[("ffn2", l)] = _ffn_bwd(dx, x2, g_ffn2[l:l + 1], s2, *w2, 0, tm, f"ffn2_l{l}",
                                           scatter_ffn(("ffn2", l)))
        dx, g_mix_l[l] = _mixer_bwd(dx, x1, g_mix[l:l + 1], sm, wc, wf, wc, wbr, wout, bias, 0, tm,
                                    f"mix_l{l}", scatter_mixer(("mix", l)))
        dx, dgains[("ffn1", l)] = _ffn_bwd(dx, x0, g_ffn1[l:l + 1], s1, *w1, 0, tm, f"ffn1_l{l}",
                                           scatter_ffn(("ffn1", l)))

    small_shapes = []
    small_pieces = []
    small_w, small_m, small_v = [], [], []

    def add_small(piece, w, m, v):
        small_shapes.append(w.shape)
        small_pieces.append(piece)
        small_w.append(w); small_m.append(m); small_v.append(v)

    dg1 = jnp.concatenate([dgains[("ffn1", l)] for l in range(DEPTH)], axis=0)
    dgm = jnp.concatenate([g_mix_l[l]["dgain"] for l in range(DEPTH)], axis=0)
    dg2 = jnp.concatenate([dgains[("ffn2", l)] for l in range(DEPTH)], axis=0)
    db = jnp.stack([jnp.concatenate([g_mix_l[l]["dbq"][0], g_mix_l[l]["dbf"][0, :N_HEADS_FOX],
                                     g_mix_l[l]["dbg"][0]]) for l in range(DEPTH)])
    drel = jnp.stack([g_mix_l[l]["dtab"][:, :N_REL].T for l in range(DEPTH)])
    add_small(dg1, g_ffn1, m_g_ffn1, v_g_ffn1)
    add_small(dgm, g_mix, m_g_mix, v_g_mix)
    add_small(db, b_in, m_b_in, v_b_in)
    add_small(drel, rel_bias, m_rel_bias, v_rel_bias)
    add_small(dg2, g_ffn2, m_g_ffn2, v_g_ffn2)
    add_small(dg_final[0], g_final, m_g_final, v_g_final)
    loss_piece = loss_blk[0, 0:1]
    small_packed = _pack_small(small_pieces + [loss_piece])

    recv = {}
    last = ("ffn1", 0)
    for l in reversed(range(DEPTH)):
        for grp in ("ffn2", "mix", "ffn1"):
            if (grp, l) != last:
                recv[(grp, l)] = exchange_wait(scatters[(grp, l)], dx, f"scattered_{grp}_l{l}")

    def upd(parts, w, m, v, tr, name, rb0=0):
        _, r, c = w.shape
        nr = r // tr

        def p_spec(layer):
            pinned = (nr - 1) if layer == 0 else 0
            return pl.BlockSpec((N_DEV, None, tr, c),
                                lambda l, i: (0, 0, rb0 + jnp.where(l == layer, i, pinned), 0))

        return adamw(parts, w, m, v, (DEPTH, nr), [p_spec(0), p_spec(1)],
                     pl.BlockSpec((None, tr, c), lambda l, i: (l, i, 0)), name)

    def both(grp, k):
        return [recv[(grp, l)][k] for l in range(DEPTH)]

    out_rows = D_FF // N_DEV // 2
    def upd_transposed(parts, w, m, v, tr, name):
        tp = lambda a: jnp.transpose(a, (0, 2, 1))
        return [tp(o) for o in upd(parts, tp(w), tp(m), tp(v), tr, name)]

    in_rows = FF_BLK // 4
    r_ffn2_in = upd_transposed(both("ffn2", 0), w_ffn2_in, m_w_ffn2_in, v_w_ffn2_in, in_rows, "adamw_ffn2_in")
    r_ffn2_out = upd(both("ffn2", 1), w_ffn2_out, m_w_ffn2_out, v_w_ffn2_out, out_rows, "adamw_ffn2_out")
    r_out = upd(both("mix", 3), w_out, m_w_out, v_w_out, LANES, "adamw_w_out")
    r_br_sb = upd(both("mix", 4), w_br_sb, m_w_br_sb, v_w_br_sb, 256, "adamw_br_sb", rb0=0)
    r_br_ch = upd(both("mix", 4), w_br_ch, m_w_br_ch, v_w_br_ch, 256, "adamw_br_ch", rb0=1)
    r_br_fox = upd(both("mix", 4), w_br_fox, m_w_br_fox, v_w_br_fox, 256, "adamw_br_fox", rb0=3)

    def summed(parts, name):
        _, _, r, c = parts.shape
        return sum_parts(parts, (1,), pl.BlockSpec((N_DEV, None, r, c), lambda s: (0, 0, 0, 0)),
                         pl.BlockSpec((r, c), lambda s: (0, 0)), _sds((r, c), F32), name)

    g_w_in = jnp.stack([
        jnp.concatenate([summed(recv[("mix", l)][0], f"sum_wqkv_l{l}"),
                         summed(recv[("mix", l)][2], f"sum_wf_l{l}")[:, :N_HEADS_FOX],
                         summed(recv[("mix", l)][1], f"sum_wgate_l{l}")], axis=1) for l in range(DEPTH)])
    to_cols = lambda a: jnp.transpose(a, (2, 0, 1))
    n_cols = w_in.shape[2]
    col_blk = n_cols // 4
    win_spec = pl.BlockSpec((col_blk, DEPTH, LANES), lambda i: (i, 0, 0))
    r_in = adamw([to_cols(g_w_in)[None]], to_cols(w_in), to_cols(m_w_in), to_cols(v_w_in), (4,),
                 [pl.BlockSpec((1, col_blk, DEPTH, LANES), lambda i: (0, i, 0, 0))], win_spec, "adamw_w_in")
    r_in = [jnp.transpose(o, (1, 2, 0)) for o in r_in]

    recv[last] = exchange_wait(scatters[last], r_in[1], "scattered_ffn1_l0")
    r_ffn1_in = upd_transposed(both("ffn1", 0), w_ffn1_in, m_w_ffn1_in, v_w_ffn1_in, in_rows, "adamw_ffn1_in")
    r_ffn1_out = upd(both("ffn1", 1), w_ffn1_out, m_w_ffn1_out, v_w_ffn1_out, out_rows, "adamw_ffn1_out")

    small_sum = all_reduce_small(small_packed, "allreduce_small", deps=(r_ffn1_out[1],))
    n_small = sum(int(np.prod(s)) for s in small_shapes)
    loss = small_sum.reshape(-1)[n_small]
    sm_spec = pl.BlockSpec((SMALL_ROWS, LANES), lambda i: (0, 0))
    sm_out = adamw([small_sum[None]], _pack_small(small_w), _pack_small(small_m), _pack_small(small_v),
                   (1,), [pl.BlockSpec((1, SMALL_ROWS, LANES), lambda i: (0, 0, 0))], sm_spec, "adamw_small")
    sm_g, sm_d, sm_m, sm_v = [_unpack_small(o, small_shapes) for o in sm_out]

    def per_kind(k):
        small = (sm_g, sm_d, sm_m, sm_v)[k]
        return [small[0], r_ffn1_in[k], r_ffn1_out[k], small[1], r_in[k], small[2], small[3],
                r_br_sb[k], r_br_ch[k], r_br_fox[k], r_out[k], small[4], r_ffn2_in[k], r_ffn2_out[k],
                small[5]]

    return (loss, dx[None], *per_kind(0), *per_kind(1), *per_kind(2), *per_kind(3))
```

```python
import functools

import numpy as np
import jax
import jax.numpy as jnp
from jax import lax
from jax.experimental import pallas as pl
from jax.experimental.pallas import tpu as pltpu

F32 = jnp.float32
BF16 = jnp.bfloat16

N_DEV = 8
D_MODEL = 1024
DEPTH = 2
HEAD_DIM = 64
W_SB, W_CH, W_FOX = 256, 512, 256
QKV_WIDTH = 3 * (W_SB + W_CH + W_FOX)
N_HEADS_FOX = 4
N_HEADS_CH = 8
D_FF = 2816
FF_BLK = 2 * D_FF // N_DEV
CHUNK = 64
LEFT_CHUNKS = 8
MAX_REL = 128
N_REL = 2 * MAX_REL + 1
REL_PAD = 384
QB = 128
KB = 512
KSUB = KB // QB
CH_WIN = 5
CH_KEYS = CH_WIN * QB
RMS_EPS = 1e-6
NEG = -1e30
SCALE = HEAD_DIM ** -0.5
LANES = 128
VMEM_LIMIT = 56 * 1024 * 1024

ADAM_LR, ADAM_B1, ADAM_B2, ADAM_EPS, ADAM_WD, ADAM_STEP = 0.001, 0.9, 0.999, 1e-08, 0.01, 10

SMALL_ROWS = 192

MESH = pl.DeviceIdType.MESH
ANY = pl.BlockSpec(memory_space=pl.ANY)
HIGHEST = lax.Precision.HIGHEST

NN = (((1,), (0,)), ((), ()))
NT = (((1,), (1,)), ((), ()))
TN = (((0,), (0,)), ((), ()))


def _cparams(n_grid):
    return pltpu.CompilerParams(dimension_semantics=("arbitrary",) * n_grid,
                                vmem_limit_bytes=VMEM_LIMIT)


def _sds(shape, dtype):
    return jax.ShapeDtypeStruct(tuple(shape), dtype)


def _my_index():
    return 4 * lax.axis_index("x") + 2 * lax.axis_index("y") + lax.axis_index("c")


def _peer(mask):
    x, y, c = lax.axis_index("x"), lax.axis_index("y"), lax.axis_index("c")
    px = x ^ ((mask >> 2) & 1)
    py = y ^ ((mask >> 1) & 1)
    pc = c ^ (mask & 1)
    return (px, py, pc), 4 * px + 2 * py + pc


def all_gather(shard, name):
    s, r, c = shard.shape

    def body(in_ref, out_ref, send_sems, recv_sems, local_sem):
        me = _my_index()
        mine = pltpu.make_async_copy(in_ref, out_ref.at[:, me], local_sem)
        mine.start()
        sends = []
        for mask in range(1, N_DEV):
            peer, _ = _peer(mask)
            cp = pltpu.make_async_remote_copy(
                src_ref=in_ref, dst_ref=out_ref.at[:, me],
                send_sem=send_sems.at[mask - 1], recv_sem=recv_sems.at[mask - 1],
                device_id=peer, device_id_type=MESH)
            cp.start()
            sends.append(cp)
        for mask in range(1, N_DEV):
            peer, pidx = _peer(mask)
            pltpu.make_async_remote_copy(
                src_ref=in_ref, dst_ref=out_ref.at[:, pidx],
                send_sem=send_sems.at[mask - 1], recv_sem=recv_sems.at[mask - 1],
                device_id=peer, device_id_type=MESH).wait_recv()
        for cp in sends:
            cp.wait_send()
        mine.wait()

    return pl.pallas_call(
        body, name=name,
        out_shape=_sds((s, N_DEV, r, c), shard.dtype),
        in_specs=[ANY], out_specs=ANY,
        scratch_shapes=[pltpu.SemaphoreType.DMA((N_DEV - 1,)),
                        pltpu.SemaphoreType.DMA((N_DEV - 1,)),
                        pltpu.SemaphoreType.DMA],
    )(shard)


def all_to_all(parts, name):
    s, _, r, c = parts.shape

    def body(in_ref, out_ref, send_sems, recv_sems, local_sem):
        me = _my_index()
        mine = pltpu.make_async_copy(in_ref.at[:, me], out_ref.at[me], local_sem)
        mine.start()
        sends = []
        for mask in range(1, N_DEV):
            peer, pidx = _peer(mask)
            cp = pltpu.make_async_remote_copy(
                src_ref=in_ref.at[:, pidx], dst_ref=out_ref.at[me],
                send_sem=send_sems.at[mask - 1], recv_sem=recv_sems.at[mask - 1],
                device_id=peer, device_id_type=MESH)
            cp.start()
            sends.append(cp)
        for mask in range(1, N_DEV):
            peer, pidx = _peer(mask)
            pltpu.make_async_remote_copy(
                src_ref=in_ref.at[:, me], dst_ref=out_ref.at[pidx],
                send_sem=send_sems.at[mask - 1], recv_sem=recv_sems.at[mask - 1],
                device_id=peer, device_id_type=MESH).wait_recv()
        for cp in sends:
            cp.wait_send()
        mine.wait()

    return pl.pallas_call(
        body, name=name,
        out_shape=_sds((N_DEV, s, r, c), parts.dtype),
        in_specs=[ANY], out_specs=ANY,
        scratch_shapes=[pltpu.SemaphoreType.DMA((N_DEV - 1,)),
                        pltpu.SemaphoreType.DMA((N_DEV - 1,)),
                        pltpu.SemaphoreType.DMA],
    )(parts)


HBM_SPEC = pl.BlockSpec(memory_space=pltpu.HBM)
SEM_SPEC = pl.BlockSpec(memory_space=pltpu.SEMAPHORE)
EFFECT = pltpu.SideEffectType.DATAFLOW_SIDE_EFFECTING


def _exchange_refs(mode, in_ref, land_ref, me, pidx):
    if mode == "gather":
        return in_ref, land_ref.at[:, me], land_ref.at[:, pidx]
    return in_ref.at[:, pidx], land_ref.at[me], land_ref.at[pidx]


def _landing_shape(mode, a):
    if mode == "gather":
        s, r, c = a.shape
        return (s, N_DEV, r, c)
    s, _, r, c = a.shape
    return (N_DEV, s, r, c)


def _own_copy(mode, in_ref, land_ref, me, sem):
    if mode == "gather":
        return pltpu.make_async_copy(in_ref, land_ref.at[:, me], sem)
    return pltpu.make_async_copy(in_ref.at[:, me], land_ref.at[me], sem)


def exchange_start(mode, arrays, name, deps=()):
    n = len(arrays)
    lands0 = [lax.empty(_landing_shape(mode, a), a.dtype) for a in arrays]

    def body(*refs):
        in_refs, land_refs = refs[:n], refs[n:2 * n]
        outs_at = 2 * n + len(deps)
        send_sems, recv_sems, own_sems, token = refs[outs_at], refs[outs_at + 1], refs[outs_at + 2], refs[-1]
        mine = _my_index()
        for k in range(n):
            _own_copy(mode, in_refs[k], land_refs[k], mine, own_sems.at[k]).start()
            for mask in range(1, N_DEV):
                peer, pidx = _peer(mask)
                src, dst, _ = _exchange_refs(mode, in_refs[k], land_refs[k], mine, pidx)
                sem = k * (N_DEV - 1) + mask - 1
                pltpu.make_async_remote_copy(
                    src_ref=src, dst_ref=dst, send_sem=send_sems.at[sem], recv_sem=recv_sems.at[sem],
                    device_id=peer, device_id_type=MESH).start()
        token[...] = jnp.zeros_like(token)

    nsem = n * (N_DEV - 1)
    outs = pl.pallas_call(
        body, name=name,
        out_shape=(pltpu.SemaphoreType.DMA((nsem,)), pltpu.SemaphoreType.DMA((nsem,)),
                   pltpu.SemaphoreType.DMA((n,)),
                   *[pltpu.HBM(a.shape, a.dtype) for a in arrays],
                   *[pltpu.HBM(l.shape, l.dtype) for l in lands0], _sds((8, LANES), F32)),
        in_specs=[HBM_SPEC] * (2 * n) + [ANY] * len(deps),
        out_specs=(SEM_SPEC, SEM_SPEC, SEM_SPEC, *[HBM_SPEC] * (2 * n),
                   pl.BlockSpec(memory_space=pltpu.VMEM)),
        input_output_aliases={k: 3 + k for k in range(2 * n)},
        compiler_params=pltpu.CompilerParams(has_side_effects=EFFECT),
    )(*[pltpu.with_memory_space_constraint(a, pltpu.HBM) for a in arrays],
      *[pltpu.with_memory_space_constraint(l, pltpu.HBM) for l in lands0], *deps)
    return dict(mode=mode, n=n, send=outs[0], recv=outs[1], own=outs[2], ins=outs[3:3 + n],
                lands=outs[3 + n:3 + 2 * n], token=outs[-1])


def exchange_wait(handle, after, name):
    n, mode = handle["n"], handle["mode"]

    def body(*refs):
        in_refs, land_refs = refs[:n], refs[n:2 * n]
        send_sems, recv_sems, own_sems = refs[2 * n], refs[2 * n + 1], refs[2 * n + 2]
        mine = _my_index()
        for k in range(n):
            _own_copy(mode, in_refs[k], land_refs[k], mine, own_sems.at[k]).wait()
            for mask in range(1, N_DEV):
                peer, pidx = _peer(mask)
                src, _, here = _exchange_refs(mode, in_refs[k], land_refs[k], mine, pidx)
                sem = k * (N_DEV - 1) + mask - 1
                cp = pltpu.make_async_remote_copy(
                    src_ref=src, dst_ref=here, send_sem=send_sems.at[sem], recv_sem=recv_sems.at[sem],
                    device_id=peer, device_id_type=MESH)
                cp.wait_send()
                cp.wait_recv()

    thru = (*handle["ins"], *handle["lands"])
    outs = pl.pallas_call(
        body, name=name,
        out_shape=tuple(pltpu.HBM(a.shape, a.dtype) for a in thru),
        in_specs=[HBM_SPEC] * (2 * n) + [SEM_SPEC, SEM_SPEC, SEM_SPEC, ANY],
        out_specs=tuple([HBM_SPEC] * (2 * n)),
        input_output_aliases={k: k for k in range(2 * n)},
        compiler_params=pltpu.CompilerParams(has_side_effects=EFFECT),
    )(*thru, handle["send"], handle["recv"], handle["own"], after)
    return list(outs[n:])


FAR_MASKS = (2, 4, 6)
PHASE1_MASKS = (1,) + FAR_MASKS


def gather_start(arrays, name, deps=()):
    n = len(arrays)
    n1 = len(PHASE1_MASKS)
    lands0 = [lax.empty(_landing_shape("gather", a), a.dtype) for a in arrays]

    def body(*refs):
        in_refs, land_refs = refs[:n], refs[n:2 * n]
        outs_at = 2 * n + len(deps)
        send_sems, recv_sems, own_sems, token = refs[outs_at], refs[outs_at + 1], refs[outs_at + 2], refs[-1]
        mine = _my_index()
        for k in range(n):
            _own_copy("gather", in_refs[k], land_refs[k], mine, own_sems.at[k]).start()
            for j, mask in enumerate(PHASE1_MASKS):
                peer, _ = _peer(mask)
                pltpu.make_async_remote_copy(
                    src_ref=in_refs[k], dst_ref=land_refs[k].at[:, mine],
                    send_sem=send_sems.at[k * n1 + j], recv_sem=recv_sems.at[k * n1 + j],
                    device_id=peer, device_id_type=MESH).start()
        token[...] = jnp.zeros_like(token)

    outs = pl.pallas_call(
        body, name=name,
        out_shape=(pltpu.SemaphoreType.DMA((n * n1,)), pltpu.SemaphoreType.DMA((n * n1,)),
                   pltpu.SemaphoreType.DMA((n,)),
                   *[pltpu.HBM(a.shape, a.dtype) for a in arrays],
                   *[pltpu.HBM(l.shape, l.dtype) for l in lands0], _sds((8, LANES), F32)),
        in_specs=[HBM_SPEC] * (2 * n) + [ANY] * len(deps),
        out_specs=(SEM_SPEC, SEM_SPEC, SEM_SPEC, *[HBM_SPEC] * (2 * n),
                   pl.BlockSpec(memory_space=pltpu.VMEM)),
        input_output_aliases={k: 3 + k for k in range(2 * n)},
        compiler_params=pltpu.CompilerParams(has_side_effects=EFFECT),
    )(*[pltpu.with_memory_space_constraint(a, pltpu.HBM) for a in arrays],
      *[pltpu.with_memory_space_constraint(l, pltpu.HBM) for l in lands0], *deps)
    return dict(n=n, send=outs[0], recv=outs[1], own=outs[2], ins=outs[3:3 + n],
                lands=outs[3 + n:3 + 2 * n], token=outs[-1], name=name)


def gather_relay(handle, after):
    n = handle["n"]
    n1, n2 = len(PHASE1_MASKS), len(FAR_MASKS)

    def body(*refs):
        in_refs, land_refs = refs[:n], refs[n:2 * n]
        send1, recv1 = refs[2 * n], refs[2 * n + 1]
        send2, recv2, token = refs[2 * n + 3], refs[2 * n + 4], refs[-1]
        token[...] = jnp.zeros_like(token)
        sibling, _ = _peer(1)
        for k in range(n):
            for j, mask in enumerate(FAR_MASKS):
                peer, pidx = _peer(mask)
                landed = land_refs[k].at[:, pidx]
                pltpu.make_async_remote_copy(
                    src_ref=in_refs[k], dst_ref=landed, send_sem=send1.at[k * n1 + 1 + j],
                    recv_sem=recv1.at[k * n1 + 1 + j], device_id=peer, device_id_type=MESH).wait_recv()
                pltpu.make_async_remote_copy(
                    src_ref=landed, dst_ref=landed, send_sem=send2.at[k * n2 + j],
                    recv_sem=recv2.at[k * n2 + j], device_id=sibling, device_id_type=MESH).start()

    thru = (*handle["ins"], *handle["lands"])
    outs = pl.pallas_call(
        body, name=handle["name"] + "_relay",
        out_shape=(pltpu.SemaphoreType.DMA((n * n2,)), pltpu.SemaphoreType.DMA((n * n2,)),
                   *[pltpu.HBM(a.shape, a.dtype) for a in thru], _sds((8, LANES), F32)),
        in_specs=[HBM_SPEC] * (2 * n) + [SEM_SPEC, SEM_SPEC, ANY],
        out_specs=(SEM_SPEC, SEM_SPEC, *[HBM_SPEC] * (2 * n), pl.BlockSpec(memory_space=pltpu.VMEM)),
        input_output_aliases={k: 2 + k for k in range(2 * n)},
        compiler_params=pltpu.CompilerParams(has_side_effects=EFFECT),
    )(*thru, handle["send"], handle["recv"], after)
    handle.update(send2=outs[0], recv2=outs[1], ins=outs[2:2 + n], lands=outs[2 + n:2 + 2 * n],
                  relay_token=outs[-1])


def gather_finish(handle, after):
    n = handle["n"]
    n1, n2 = len(PHASE1_MASKS), len(FAR_MASKS)

    def body(*refs):
        in_refs, land_refs = refs[:n], refs[n:2 * n]
        send1, recv1, own_sems, send2, recv2 = refs[2 * n:2 * n + 5]
        mine = _my_index()
        sibling, sib_idx = _peer(1)
        for k in range(n):
            _own_copy("gather", in_refs[k], land_refs[k], mine, own_sems.at[k]).wait()
            for j, mask in enumerate(PHASE1_MASKS):
                peer, pidx = _peer(mask)
                cp = pltpu.make_async_remote_copy(
                    src_ref=in_refs[k], dst_ref=land_refs[k].at[:, pidx], send_sem=send1.at[k * n1 + j],
                    recv_sem=recv1.at[k * n1 + j], device_id=peer, device_id_type=MESH)
                cp.wait_send()
                if mask == 1:
                    cp.wait_recv()
            for j, mask in enumerate(FAR_MASKS):
                _, pidx = _peer(mask)
                _, far_of_sibling = _peer(mask ^ 1)
                cp = pltpu.make_async_remote_copy(
                    src_ref=land_refs[k].at[:, pidx], dst_ref=land_refs[k].at[:, far_of_sibling],
                    send_sem=send2.at[k * n2 + j], recv_sem=recv2.at[k * n2 + j],
                    device_id=sibling, device_id_type=MESH)
                cp.wait_send()
                cp.wait_recv()

    thru = (*handle["ins"], *handle["lands"])
    outs = pl.pallas_call(
        body, name=handle["name"] + "_finish",
        out_shape=tuple(pltpu.HBM(a.shape, a.dtype) for a in thru),
        in_specs=[HBM_SPEC] * (2 * n) + [SEM_SPEC] * 5 + [ANY],
        out_specs=tuple([HBM_SPEC] * (2 * n)),
        input_output_aliases={k: k for k in range(2 * n)},
        compiler_params=pltpu.CompilerParams(has_side_effects=EFFECT),
    )(*thru, handle["send"], handle["recv"], handle["own"], handle["send2"], handle["recv2"], after)
    return list(outs[n:])


def all_reduce_small(packed, name, deps=()):
    rows = packed.shape[0]
    nd = len(deps)

    def body(in_ref, *rest):
        out_ref, slots, send_sems, recv_sems = rest[nd:]
        me = _my_index()
        sends = []
        for mask in range(1, N_DEV):
            peer, _ = _peer(mask)
            cp = pltpu.make_async_remote_copy(
                src_ref=in_ref, dst_ref=slots.at[me],
                send_sem=send_sems.at[mask - 1], recv_sem=recv_sems.at[mask - 1],
                device_id=peer, device_id_type=MESH)
            cp.start()
            sends.append(cp)
        slots[me] = in_ref[...]
        for mask in range(1, N_DEV):
            peer, pidx = _peer(mask)
            pltpu.make_async_remote_copy(
                src_ref=in_ref, dst_ref=slots.at[pidx],
                send_sem=send_sems.at[mask - 1], recv_sem=recv_sems.at[mask - 1],
                device_id=peer, device_id_type=MESH).wait_recv()
        for cp in sends:
            cp.wait_send()
        total = slots[0]
        for p in range(1, N_DEV):
            total = total + slots[p]
        out_ref[...] = total

    return pl.pallas_call(
        body, name=name,
        out_shape=_sds((rows, LANES), F32),
        in_specs=[pl.BlockSpec(memory_space=pltpu.VMEM)] + [ANY] * nd,
        out_specs=pl.BlockSpec(memory_space=pltpu.VMEM),
        scratch_shapes=[pltpu.VMEM((N_DEV, rows, LANES), F32),
                        pltpu.SemaphoreType.DMA((N_DEV - 1,)),
                        pltpu.SemaphoreType.DMA((N_DEV - 1,))],
    )(packed, *deps)


def matmul(dims, a, b, out_sds, grid, a_spec, b_spec, o_spec, acc_shape, *, name, alpha=1.0,
           bias=None, bias_spec=None, scale=None, scale_spec=None, res=None, res_spec=None,
           colsum_sds=None, colsum_spec=None, out_t_sds=None, out_t_spec=None, deps=()):
    nk = grid[2]
    has_bias, has_scale, has_res = bias is not None, scale is not None, res is not None
    has_cs, has_t = colsum_sds is not None, out_t_sds is not None
    if has_cs:
        assert grid[0] == 1 and dims == TN

    def body(*refs):
        a_ref, b_ref = refs[0], refs[1]
        pos = 2
        bias_ref = scale_ref = res_ref = cs_ref = ot_ref = None
        if has_bias:
            bias_ref = refs[pos]; pos += 1
        if has_scale:
            scale_ref = refs[pos]; pos += 1
        if has_res:
            res_ref = refs[pos]; pos += 1
        pos += len(deps)
        o_ref = refs[pos]; pos += 1
        if has_cs:
            cs_ref = refs[pos]; pos += 1
        if has_t:
            ot_ref = refs[pos]; pos += 1
        k = pl.program_id(2)
        bval = b_ref[...]
        part = lax.dot_general(a_ref[...].astype(BF16), bval.astype(BF16), dims,
                               preferred_element_type=F32)

        def finish(total):
            r = total * alpha if alpha != 1.0 else total
            if has_bias:
                r = r + bias_ref[...]
            if has_scale:
                r = r * scale_ref[...]
            if has_res:
                r = r + res_ref[...].astype(F32)
            o_ref[...] = r.astype(o_ref.dtype)
            if has_t:
                ot_ref[...] = r.T.astype(ot_ref.dtype)

        if has_cs:
            csum = jnp.sum(bval.astype(F32), axis=0, keepdims=True)

            @pl.when(k == 0)
            def _():
                cs_ref[...] = csum

            @pl.when(k > 0)
            def _():
                cs_ref[...] += csum

        if nk == 1:
            finish(part)
        else:
            acc_ref = refs[pos]

            @pl.when(k == 0)
            def _():
                acc_ref[...] = part

            @pl.when(k > 0)
            def _():
                acc_ref[...] += part

            @pl.when(k == nk - 1)
            def _():
                finish(acc_ref[...])

    in_specs, args = [a_spec, b_spec], [a, b]
    if has_bias:
        in_specs.append(bias_spec); args.append(bias)
    if has_scale:
        in_specs.append(scale_spec); args.append(scale)
    if has_res:
        in_specs.append(res_spec); args.append(res)
    in_specs += [ANY] * len(deps)
    args += list(deps)
    out_shape, out_specs = [out_sds], [o_spec]
    if has_cs:
        out_shape.append(colsum_sds); out_specs.append(colsum_spec)
    if has_t:
        out_shape.append(out_t_sds); out_specs.append(out_t_spec)
    scratch = [] if nk == 1 else [pltpu.VMEM(acc_shape, F32)]
    outs = pl.pallas_call(
        body, name=name, grid=grid, in_specs=in_specs, out_specs=out_specs, out_shape=out_shape,
        scratch_shapes=scratch, compiler_params=_cparams(3))(*args)
    return outs if (has_cs or has_t) else outs[0]


def _sigmoid(z):
    return 1.0 / (1.0 + jnp.exp(-z))


def _log_sigmoid(z):
    return jnp.minimum(z, 0.0) - jnp.log(1.0 + jnp.exp(-jnp.abs(z)))


def rmsnorm_fwd(x, gain, tm, name, deps=()):
    t, d = x.shape

    def body(x_ref, g_ref, *rest):
        o_ref = rest[-1]
        xf = x_ref[...]
        r = lax.rsqrt(jnp.mean(xf * xf, axis=-1, keepdims=True) + RMS_EPS)
        o_ref[...] = (xf * r * g_ref[...]).astype(o_ref.dtype)

    return pl.pallas_call(
        body, name=name, grid=(t // tm,),
        in_specs=[pl.BlockSpec((tm, d), lambda i: (i, 0)), pl.BlockSpec((1, d), lambda i: (0, 0))]
        + [ANY] * len(deps),
        out_specs=pl.BlockSpec((tm, d), lambda i: (i, 0)),
        out_shape=_sds((t, d), BF16), compiler_params=_cparams(1))(x, gain, *deps)


def rmsnorm_bwd(x, gain, dh, dres, tm, name):
    t, d = x.shape

    def body(x_ref, g_ref, dh_ref, dres_ref, dx_ref, dg_ref):
        i = pl.program_id(0)
        xf = x_ref[...]
        r = lax.rsqrt(jnp.mean(xf * xf, axis=-1, keepdims=True) + RMS_EPS)
        xhat = xf * r
        dh_v = dh_ref[...]
        dxhat = dh_v * g_ref[...]
        dx = r * (dxhat - xhat * jnp.mean(dxhat * xhat, axis=-1, keepdims=True))
        dx_ref[...] = dres_ref[...] + dx
        dg = jnp.sum(dh_v * xhat, axis=0, keepdims=True)

        @pl.when(i == 0)
        def _():
            dg_ref[...] = dg

        @pl.when(i > 0)
        def _():
            dg_ref[...] += dg

    row = pl.BlockSpec((tm, d), lambda i: (i, 0))
    vec = pl.BlockSpec((1, d), lambda i: (0, 0))
    return pl.pallas_call(
        body, name=name, grid=(t // tm,), in_specs=[row, vec, row, row], out_specs=[row, vec],
        out_shape=[_sds((t, d), F32), _sds((1, d), F32)], compiler_params=_cparams(1))(x, gain, dh, dres)


def loss_head(x, gain, target, tm, name):
    t, d = x.shape

    def body(x_ref, g_ref, tgt_ref, dx_ref, dg_ref, loss_ref):
        i = pl.program_id(0)
        xf = x_ref[...]
        g = g_ref[...]
        r = lax.rsqrt(jnp.mean(xf * xf, axis=-1, keepdims=True) + RMS_EPS)
        xhat = xf * r
        err = xhat * g - tgt_ref[...]
        part = 0.5 * jnp.sum(jnp.mean(err * err, axis=-1, keepdims=True))
        dy = err * (1.0 / d)
        dxhat = dy * g
        dx_ref[...] = r * (dxhat - xhat * jnp.mean(dxhat * xhat, axis=-1, keepdims=True))
        dg = jnp.sum(dy * xhat, axis=0, keepdims=True)
        lpart = jnp.full((8, LANES), part, F32)

        @pl.when(i == 0)
        def _():
            dg_ref[...] = dg
            loss_ref[...] = lpart

        @pl.when(i > 0)
        def _():
            dg_ref[...] += dg
            loss_ref[...] += lpart

    row = pl.BlockSpec((tm, d), lambda i: (i, 0))
    vec = pl.BlockSpec((1, d), lambda i: (0, 0))
    return pl.pallas_call(
        body, name=name, grid=(t // tm,), in_specs=[row, vec, row],
        out_specs=[row, vec, pl.BlockSpec((8, LANES), lambda i: (0, 0))],
        out_shape=[_sds((t, d), F32), _sds((1, d), F32), _sds((8, LANES), F32)],
        compiler_params=_cparams(1))(x, gain, target)


def ffn_in_swiglu(hn, wa, s, tm, name):
    t = hn.shape[0]
    halves = 2 if tm % 512 == 0 else 1
    rows = tm // halves

    def body(h_ref, wg_ref, wu_ref, gu_ref, act_ref):
        for c in range(halves):
            rs = slice(c * rows, (c + 1) * rows)
            h = h_ref[rs, :]
            g = jnp.dot(h, wg_ref[...], preferred_element_type=F32)
            u = jnp.dot(h, wu_ref[...], preferred_element_type=F32)
            gu_ref[0, rs, :] = g.astype(gu_ref.dtype)
            gu_ref[1, rs, :] = u.astype(gu_ref.dtype)
            act_ref[rs, :] = (g * _sigmoid(g) * u).astype(act_ref.dtype)

    return pl.pallas_call(
        body, name=name, grid=(t // tm, 4),
        in_specs=[pl.BlockSpec((tm, D_MODEL), lambda i, j: (i, 0)),
                  pl.BlockSpec((None, None, D_MODEL, FF_BLK), lambda i, j: (s, j, 0, 0)),
                  pl.BlockSpec((None, None, D_MODEL, FF_BLK), lambda i, j: (s, j + 4, 0, 0))],
        out_specs=[pl.BlockSpec((None, 2, tm, FF_BLK), lambda i, j: (j, 0, i, 0)),
                   pl.BlockSpec((None, tm, FF_BLK), lambda i, j: (j, i, 0))],
        out_shape=[_sds((4, 2, t, FF_BLK), BF16), _sds((4, t, FF_BLK), BF16)],
        compiler_params=_cparams(2))(hn, wa, wa)


def ffn_dact_swiglu(dy, wb, gu, s, tm, name):
    t = dy.shape[0]

    def body(dy_ref, w_ref, gu_ref, o_ref):
        da = 0.5 * lax.dot_general(dy_ref[...].astype(BF16), w_ref[...], NT, preferred_element_type=F32)
        g = gu_ref[0].astype(F32)
        u = gu_ref[1].astype(F32)
        sg = _sigmoid(g)
        o_ref[0] = (da * u * (sg * (1.0 + g * (1.0 - sg)))).astype(o_ref.dtype)
        o_ref[1] = (da * g * sg).astype(o_ref.dtype)

    blk = pl.BlockSpec((None, 2, tm, FF_BLK), lambda i, j: (j, 0, i, 0))
    return pl.pallas_call(
        body, name=name, grid=(t // tm, 4),
        in_specs=[pl.BlockSpec((tm, D_MODEL), lambda i, j: (i, 0)),
                  pl.BlockSpec((None, None, FF_BLK, D_MODEL), lambda i, j: (s, j, 0, 0)), blk],
        out_specs=blk, out_shape=_sds((4, 2, t, FF_BLK), BF16),
        compiler_params=_cparams(2))(dy, wb, gu)


def ffn_out_residual(act, wb, x, s, tm, name, deps=()):
    t = x.shape[0]

    def body(a_ref, w_ref, x_ref, *rest):
        o_ref = rest[-1]
        acc = jnp.dot(a_ref[0], w_ref[0], preferred_element_type=F32)
        for k in range(1, 4):
            acc = acc + jnp.dot(a_ref[k], w_ref[k], preferred_element_type=F32)
        o_ref[...] = x_ref[...] + 0.5 * acc

    row = pl.BlockSpec((tm, D_MODEL), lambda i: (i, 0))
    return pl.pallas_call(
        body, name=name, grid=(t // tm,),
        in_specs=[pl.BlockSpec((4, tm, FF_BLK), lambda i: (0, i, 0)),
                  pl.BlockSpec((None, 4, FF_BLK, D_MODEL), lambda i: (s, 0, 0, 0)), row] + [ANY] * len(deps),
        out_specs=row, out_shape=_sds((t, D_MODEL), F32), compiler_params=_cparams(1))(act, wb, x, *deps)


def ffn_dh_norm_bwd(dgu, wa, s, x, gain, dres, tm, name, deps):
    t = dgu.shape[2]
    nd = len(deps)

    def body(g_ref, w_ref, x_ref, gain_ref, dres_ref, *rest):
        dx_ref, dg_ref = rest[nd:]
        i = pl.program_id(0)
        dh = lax.dot_general(g_ref[0, 0], w_ref[0], NT, preferred_element_type=F32)
        for p in range(1, N_DEV):
            dh = dh + lax.dot_general(g_ref[p % 4, p // 4], w_ref[p], NT, preferred_element_type=F32)
        xf = x_ref[...]
        r = lax.rsqrt(jnp.mean(xf * xf, axis=-1, keepdims=True) + RMS_EPS)
        xhat = xf * r
        dxhat = dh * gain_ref[...]
        dx_ref[...] = dres_ref[...] + r * (dxhat - xhat * jnp.mean(dxhat * xhat, axis=-1, keepdims=True))
        dg = jnp.sum(dh * xhat, axis=0, keepdims=True)

        @pl.when(i == 0)
        def _():
            dg_ref[...] = dg

        @pl.when(i > 0)
        def _():
            dg_ref[...] += dg

    row = pl.BlockSpec((tm, D_MODEL), lambda i: (i, 0))
    vec = pl.BlockSpec((1, D_MODEL), lambda i: (0, 0))
    return pl.pallas_call(
        body, name=name, grid=(t // tm,),
        in_specs=[pl.BlockSpec((4, 2, tm, FF_BLK), lambda i: (0, 0, i, 0)),
                  pl.BlockSpec((None, N_DEV, D_MODEL, FF_BLK), lambda i: (s, 0, 0, 0)), row, vec, row]
        + [ANY] * nd,
        out_specs=[row, vec], out_shape=[_sds((t, D_MODEL), F32), _sds((1, D_MODEL), F32)],
        compiler_params=_cparams(1))(dgu, wa, x, gain, dres, *deps)


def merge_fwd(gates, ya, yb, yc, tm, name):
    t, d = ya.shape

    def body(ga_ref, gb_ref, gc_ref, ya_ref, yb_ref, yc_ref, o_ref):
        m = (_sigmoid(ga_ref[...]) * ya_ref[...] + _sigmoid(gb_ref[...]) * yb_ref[...]
             + _sigmoid(gc_ref[...]) * yc_ref[...])
        o_ref[...] = m.astype(o_ref.dtype)

    row = pl.BlockSpec((tm, d), lambda i: (i, 0))
    gspecs = [pl.BlockSpec((tm, d), functools.partial(lambda i, a: (i, a), a=a)) for a in range(3)]
    return pl.pallas_call(
        body, name=name, grid=(t // tm,), in_specs=gspecs + [row, row, row], out_specs=row,
        out_shape=_sds((t, d), BF16), compiler_params=_cparams(1))(gates, gates, gates, ya, yb, yc)


def branch_merge(os_, wbr, layer, gates, tm, name, deps=()):
    t = os_[0].shape[0]
    d = D_MODEL
    nd = len(deps)
    widths = [o.shape[1] for o in os_]
    starts = [sum(widths[:a]) for a in range(3)]

    def body(oa_ref, ob_ref, oc_ref, w_ref, g_ref, *rest):
        ya_ref, yb_ref, yc_ref, m_ref = rest[nd:]
        merged = None
        for a, (o_ref, y_ref) in enumerate(((oa_ref, ya_ref), (ob_ref, yb_ref), (oc_ref, yc_ref))):
            y = jnp.dot(o_ref[...], w_ref[starts[a]:starts[a] + widths[a], :], preferred_element_type=F32)
            y_ref[...] = y
            term = _sigmoid(g_ref[:, a * d:(a + 1) * d]) * y
            merged = term if merged is None else merged + term
        m_ref[...] = merged.astype(m_ref.dtype)

    row = pl.BlockSpec((tm, d), lambda i: (i, 0))
    ya, yb, yc, merged = pl.pallas_call(
        body, name=name, grid=(t // tm,),
        in_specs=[pl.BlockSpec((tm, w), lambda i: (i, 0)) for w in widths]
        + [pl.BlockSpec((None, d, d), lambda i: (layer, 0, 0)), pl.BlockSpec((tm, 3 * d), lambda i: (i, 0))]
        + [ANY] * nd,
        out_specs=[row, row, row, row],
        out_shape=[_sds((t, d), F32)] * 3 + [_sds((t, d), BF16)],
        compiler_params=_cparams(1))(*os_, wbr, gates, *deps)
    return [ya, yb, yc], merged


def dmerged_merge_bwd(dy, wout, layer, gates, ys, tm, name):
    t, d = dy.shape

    def body(dy_ref, w_ref, g_ref, ya_ref, yb_ref, yc_ref, dg_ref, dya_ref, dyb_ref, dyc_ref):
        dmv = lax.dot_general(dy_ref[...].astype(BF16), w_ref[...], NT, preferred_element_type=F32)
        for a, (y_ref, dy_out) in enumerate(((ya_ref, dya_ref), (yb_ref, dyb_ref), (yc_ref, dyc_ref))):
            cols = slice(a * d, (a + 1) * d)
            s = _sigmoid(g_ref[:, cols])
            dy_out[...] = (dmv * s).astype(dy_out.dtype)
            dg_ref[:, cols] = (dmv * y_ref[...] * s * (1.0 - s)).astype(dg_ref.dtype)

    row = pl.BlockSpec((tm, d), lambda i: (i, 0))
    wide = pl.BlockSpec((tm, 3 * d), lambda i: (i, 0))
    dg, dya, dyb, dyc = pl.pallas_call(
        body, name=name, grid=(t // tm,),
        in_specs=[row, pl.BlockSpec((None, d, d), lambda i: (layer, 0, 0)), wide, row, row, row],
        out_specs=[wide, row, row, row],
        out_shape=[_sds((t, 3 * d), BF16)] + [_sds((t, d), BF16)] * 3,
        compiler_params=_cparams(1))(dy, wout, gates, *ys)
    return dg, [dya, dyb, dyc]


def branch_bwd(dys, os_, wbr, layer, tm, name):
    t = dys[0].shape[0]
    d = D_MODEL
    nt = t // tm
    widths = [o.shape[1] for o in os_]
    starts = [sum(widths[:a]) for a in range(3)]

    def body(dya_ref, dyb_ref, dyc_ref, oa_ref, ob_ref, oc_ref, w_ref, do_ref, dot_ref, dw_ref, acc_ref):
        i = pl.program_id(0)
        for a, (dy_ref, o_ref) in enumerate(((dya_ref, oa_ref), (dyb_ref, ob_ref), (dyc_ref, oc_ref))):
            rows = slice(starts[a], starts[a] + widths[a])
            dyv = dy_ref[...]
            do = lax.dot_general(dyv, w_ref[rows, :], NT, preferred_element_type=F32)
            do_ref[:, rows] = do.astype(do_ref.dtype)
            dot_ref[rows, :] = do.T.astype(dot_ref.dtype)
            dw = lax.dot_general(o_ref[...], dyv, TN, preferred_element_type=F32)

            @pl.when(i == 0)
            def _():
                acc_ref[rows, :] = dw

            @pl.when(i > 0)
            def _():
                acc_ref[rows, :] += dw

        @pl.when(i == nt - 1)
        def _():
            dw_ref[...] = acc_ref[...].astype(dw_ref.dtype)

    row = pl.BlockSpec((tm, d), lambda i: (i, 0))
    return pl.pallas_call(
        body, name=name, grid=(nt,),
        in_specs=[row, row, row] + [pl.BlockSpec((tm, w), lambda i: (i, 0)) for w in widths]
        + [pl.BlockSpec((None, d, d), lambda i: (layer, 0, 0))],
        out_specs=[row, pl.BlockSpec((d, tm), lambda i: (0, i)), pl.BlockSpec((d, d), lambda i: (0, 0))],
        out_shape=[_sds((t, d), BF16), _sds((d, t), BF16), _sds((d, d), BF16)],
        scratch_shapes=[pltpu.VMEM((d, d), F32)], compiler_params=_cparams(1))(*dys, *os_, wbr)


def mixer_dh_norm_bwd(dh_part, dgates, wc, gate_idx, df, wf, layer, x, gain, dres, tm, name):
    t, d = x.shape

    def body(dhp_ref, dg_ref, wg_ref, df_ref, wf_ref, x_ref, gain_ref, dres_ref, dx_ref, dgain_ref):
        i = pl.program_id(0)
        dh = (dhp_ref[...]
              + lax.dot_general(dg_ref[...], wg_ref[...], NT, preferred_element_type=F32)
              + lax.dot_general(df_ref[...].astype(BF16), wf_ref[...], NT, preferred_element_type=F32))
        xf = x_ref[...]
        r = lax.rsqrt(jnp.mean(xf * xf, axis=-1, keepdims=True) + RMS_EPS)
        xhat = xf * r
        dxhat = dh * gain_ref[...]
        dx_ref[...] = dres_ref[...] + r * (dxhat - xhat * jnp.mean(dxhat * xhat, axis=-1, keepdims=True))
        dg = jnp.sum(dh * xhat, axis=0, keepdims=True)

        @pl.when(i == 0)
        def _():
            dgain_ref[...] = dg

        @pl.when(i > 0)
        def _():
            dgain_ref[...] += dg

    row = pl.BlockSpec((tm, d), lambda i: (i, 0))
    vec = pl.BlockSpec((1, d), lambda i: (0, 0))
    return pl.pallas_call(
        body, name=name, grid=(t // tm,),
        in_specs=[row, pl.BlockSpec((tm, QKV_WIDTH), lambda i: (i, 0)),
                  pl.BlockSpec((None, d, QKV_WIDTH), lambda i: (gate_idx, 0, 0)),
                  pl.BlockSpec((tm, LANES), lambda i: (i, 0)),
                  pl.BlockSpec((None, d, LANES), lambda i: (layer, 0, 0)), row, vec, row],
        out_specs=[row, vec], out_shape=[_sds((t, d), F32), _sds((1, d), F32)],
        compiler_params=_cparams(1))(dh_part, dgates, wc, df, wf, x, gain, dres)


def merge_bwd(dm, gates, ya, yb, yc, tm, name):
    t, d = ya.shape

    def body(dm_ref, g_ref, ya_ref, yb_ref, yc_ref, dg_ref, dya_ref, dyb_ref, dyc_ref):
        dmv = dm_ref[...]
        for a, (y_ref, dy_ref) in enumerate(((ya_ref, dya_ref), (yb_ref, dyb_ref), (yc_ref, dyc_ref))):
            cols = slice(a * d, (a + 1) * d)
            s = _sigmoid(g_ref[:, cols])
            dy_ref[...] = (dmv * s).astype(dy_ref.dtype)
            dg_ref[:, cols] = (dmv * y_ref[...] * s * (1.0 - s)).astype(dg_ref.dtype)

    row = pl.BlockSpec((tm, d), lambda i: (i, 0))
    wide = pl.BlockSpec((tm, 3 * d), lambda i: (i, 0))
    dg, dya, dyb, dyc = pl.pallas_call(
        body, name=name, grid=(t // tm,), in_specs=[row, wide, row, row, row],
        out_specs=[wide, row, row, row],
        out_shape=[_sds((t, 3 * d), BF16)] + [_sds((t, d), BF16)] * 3,
        compiler_params=_cparams(1))(dm, gates, ya, yb, yc)
    return dg, [dya, dyb, dyc]


def _iota2(shape, dim):
    return lax.broadcasted_iota(jnp.int32, shape, dim)


def forget_cumsum(f, name):
    t = f.shape[0]
    nq = t // QB

    def body(f_ref, fcol_ref, frow_ref, carry):
        j = pl.program_id(0)

        @pl.when(j == 0)
        def _():
            carry[...] = jnp.zeros_like(carry)

        logf = _log_sigmoid(f_ref[...])
        tri = (_iota2((QB, QB), 1) <= _iota2((QB, QB), 0)).astype(F32)
        blk = jnp.dot(tri, logf, precision=HIGHEST, preferred_element_type=F32) + carry[...]
        carry[...] += jnp.sum(logf, axis=0, keepdims=True)
        fcol_ref[...] = blk
        frow_ref[...] = blk.T[0:8, :]

    return pl.pallas_call(
        body, name=name, grid=(nq,),
        in_specs=[pl.BlockSpec((QB, LANES), lambda j: (j, 0))],
        out_specs=[pl.BlockSpec((QB, LANES), lambda j: (j, 0)),
                   pl.BlockSpec((None, 8, QB), lambda j: (j, 0, 0))],
        out_shape=[_sds((t, LANES), F32), _sds((nq, 8, QB), F32)],
        scratch_shapes=[pltpu.VMEM((1, LANES), F32)], compiler_params=_cparams(1))(f)


def proj_gates_forget(hm, wc, gate_idx, wf, layer, bg, bf, tm, name, deps=()):
    t, d = hm.shape
    sub = tm // QB
    nd = len(deps)

    def body(h_ref, wg_ref, bg_ref, wf_ref, bf_ref, *rest):
        g_ref, f_ref, fcol_ref, frow_ref, carry = rest[nd:]
        i, j = pl.program_id(0), pl.program_id(1)
        h = h_ref[...]
        g_ref[...] = jnp.dot(h, wg_ref[...], preferred_element_type=F32) + bg_ref[...]

        @pl.when((i == 0) & (j == 0))
        def _():
            carry[...] = jnp.zeros_like(carry)

        @pl.when(j == 0)
        def _():
            f = jnp.dot(h, wf_ref[...], preferred_element_type=F32) + bf_ref[...]
            f_ref[...] = f
            logf = _log_sigmoid(f)
            tri = (_iota2((QB, QB), 1) <= _iota2((QB, QB), 0)).astype(F32)
            for s in range(sub):
                rows = slice(s * QB, (s + 1) * QB)
                part = logf[rows, :]
                blk = jnp.dot(tri, part, precision=HIGHEST, preferred_element_type=F32) + carry[...]
                carry[...] += jnp.sum(part, axis=0, keepdims=True)
                fcol_ref[rows, :] = blk
                frow_ref[s] = blk.T[0:8, :]

    narrow = pl.BlockSpec((tm, LANES), lambda i, j: (i, 0))
    return pl.pallas_call(
        body, name=name, grid=(t // tm, 3),
        in_specs=[pl.BlockSpec((tm, d), lambda i, j: (i, 0)),
                  pl.BlockSpec((None, d, d), lambda i, j: (gate_idx, 0, j)),
                  pl.BlockSpec((1, d), lambda i, j: (0, j)),
                  pl.BlockSpec((None, d, LANES), lambda i, j: (layer, 0, 0)),
                  pl.BlockSpec((1, LANES), lambda i, j: (0, 0))] + [ANY] * nd,
        out_specs=[pl.BlockSpec((tm, d), lambda i, j: (i, j)), narrow, narrow,
                   pl.BlockSpec((sub, 8, QB), lambda i, j: (i, 0, 0))],
        out_shape=[_sds((t, 3 * d), F32), _sds((t, LANES), F32), _sds((t, LANES), F32),
                   _sds((t // QB, 8, QB), F32)],
        scratch_shapes=[pltpu.VMEM((1, LANES), F32)], compiler_params=_cparams(2))(hm, wc, bg, wf, bf, *deps)


def forget_cumsum_bwd(dfrow, f, name):
    t = f.shape[0]
    nq = t // QB

    def body(dfr_ref, f_ref, df_ref, carry):
        jj = pl.program_id(0)

        @pl.when(jj == 0)
        def _():
            carry[...] = jnp.zeros_like(carry)

        padded = jnp.concatenate([dfr_ref[...], jnp.zeros((QB - 8, QB), F32)], axis=0)
        dfcol = padded.T
        tri = (_iota2((QB, QB), 1) >= _iota2((QB, QB), 0)).astype(F32)
        dlogf = jnp.dot(tri, dfcol, precision=HIGHEST, preferred_element_type=F32) + carry[...]
        carry[...] += jnp.sum(dfcol, axis=0, keepdims=True)
        df_ref[...] = dlogf * _sigmoid(-f_ref[...])

    return pl.pallas_call(
        body, name=name, grid=(nq,),
        in_specs=[pl.BlockSpec((None, 8, QB), lambda jj: (nq - 1 - jj, 0, 0)),
                  pl.BlockSpec((QB, LANES), lambda jj: (nq - 1 - jj, 0))],
        out_specs=pl.BlockSpec((QB, LANES), lambda jj: (nq - 1 - jj, 0)),
        out_shape=_sds((t, LANES), F32),
        scratch_shapes=[pltpu.VMEM((1, LANES), F32)], compiler_params=_cparams(1))(dfrow, f)


REL_DIAG = 768
REL_SHIFT = REL_DIAG - (QB - 1)


def _diag_onehot():
    u = _iota2((REL_PAD, REL_DIAG), 1)
    rel = jnp.clip(CH_KEYS - 1 - u, -MAX_REL, MAX_REL) + MAX_REL
    return (_iota2((REL_PAD, REL_DIAG), 0) == rel).astype(F32)


def rel_bias_build(tab_t, name):
    def body(tab_ref, o_ref):
        diag = jnp.dot(tab_ref[...], _diag_onehot(), precision=HIGHEST, preferred_element_type=F32)
        band = _chunk_band()
        for h in range(N_HEADS_CH):
            rows = jnp.broadcast_to(diag[h:h + 1, :], (QB, REL_DIAG))
            o_ref[h] = pltpu.roll(rows, REL_SHIFT, 1, stride=1, stride_axis=0)[:, :CH_KEYS] + band

    return pl.pallas_call(
        body, name=name, out_shape=_sds((N_HEADS_CH, QB, CH_KEYS), F32),
        in_specs=[pl.BlockSpec(memory_space=pltpu.VMEM)], out_specs=pl.BlockSpec(memory_space=pltpu.VMEM),
    )(tab_t)


def rel_bias_scatter(dbias, name):
    def body(db_ref, o_ref, ddiag):
        flip = (_iota2((QB, QB), 0) + _iota2((QB, QB), 1) == QB - 1).astype(F32)
        for h in range(N_HEADS_CH):
            padded = jnp.concatenate([db_ref[h], jnp.zeros((QB, REL_DIAG - CH_KEYS), F32)], axis=1)
            flipped = jnp.dot(flip, padded, precision=HIGHEST, preferred_element_type=F32)
            unrolled = pltpu.roll(flipped, 0, 1, stride=1, stride_axis=0)
            ddiag[h:h + 1, :] = jnp.sum(unrolled, axis=0, keepdims=True)
        o_ref[...] = lax.dot_general(ddiag[...], _diag_onehot(), NT, precision=HIGHEST,
                                     preferred_element_type=F32)

    return pl.pallas_call(
        body, name=name, out_shape=_sds((N_HEADS_CH, REL_PAD), F32),
        in_specs=[pl.BlockSpec(memory_space=pltpu.VMEM)], out_specs=pl.BlockSpec(memory_space=pltpu.VMEM),
        scratch_shapes=[pltpu.VMEM((N_HEADS_CH, REL_DIAG), F32)],
    )(dbias)


def _hl(h):
    return slice(h * HEAD_DIM, (h + 1) * HEAD_DIM)


def _split_dot(x, tri_bf16):
    hi = x.astype(BF16)
    lo = (x - hi.astype(F32)).astype(BF16)
    return (jnp.dot(hi, tri_bf16, preferred_element_type=F32)
            + jnp.dot(lo, tri_bf16, preferred_element_type=F32))


def _rows(j):
    return pl.ds(pl.multiple_of(j * QB, QB), QB)


def _krows(g):
    return pl.ds(pl.multiple_of(g * KB, KB), KB)


def _log_sigmoid_pair(z):
    sp = jnp.log(1.0 + jnp.exp(-jnp.abs(z)))
    return jnp.minimum(z, 0.0) - sp, -jnp.maximum(z, 0.0) - sp


def _qkv_specs(t, col0, n_pairs):
    q_spec = pl.BlockSpec((QB, LANES), lambda hp, i: (i, col0 + hp))
    k_spec = pl.BlockSpec((t, LANES), lambda hp, i: (0, col0 + n_pairs + hp))
    v_spec = pl.BlockSpec((t, LANES), lambda hp, i: (0, col0 + 2 * n_pairs + hp))
    return q_spec, k_spec, v_spec


def _keys_major(xt):
    pairs, groups, _, _ = xt.shape
    return xt.transpose(1, 3, 0, 2).reshape(groups * KB, pairs * LANES)


def sb_fwd(qkv, name):
    t = qkv.shape[0]
    nq = t // QB

    def body(q_ref, k_ref, v_ref, o_ref, w_ref):
        i = pl.program_id(1)
        groups = i // KSUB + 1
        tri_after = (_iota2((KB, KB), 0) > _iota2((KB, KB), 1)).astype(BF16)
        t_idx = i * QB + _iota2((QB, KB), 0)
        qs = [q_ref[:, _hl(h)] for h in range(2)]

        def step(g, carry, masked):
            strict = (g * KB + _iota2((QB, KB), 1)) < t_idx
            out = []
            for h in range(2):
                tail, acc = carry[2 * h], carry[2 * h + 1]
                k = k_ref[_krows(g), _hl(h)]
                v = v_ref[_krows(g), _hl(h)]
                z = lax.dot_general(qs[h], k, NT, preferred_element_type=F32)
                lb, lf = _log_sigmoid_pair(z)
                if masked:
                    lf = jnp.where(strict, lf, 0.0)
                between = _split_dot(lf, tri_after) + tail
                w = jnp.exp(lb + between)
                if masked:
                    w = jnp.where(strict, w, 0.0)
                w = w.astype(BF16)
                w_ref[h, g] = w
                acc = acc + jnp.dot(w, v, preferred_element_type=F32)
                out += [tail + jnp.sum(lf, axis=1, keepdims=True), acc]
            return tuple(out)

        init = (jnp.zeros((QB, 1), F32), jnp.zeros((QB, HEAD_DIM), F32)) * 2
        res = step(groups - 1, init, True)
        res = lax.fori_loop(0, groups - 1, lambda gg, c: step(groups - 2 - gg, c, False), res)
        for h in range(2):
            o_ref[:, _hl(h)] = res[2 * h + 1].astype(o_ref.dtype)

    q_spec, k_spec, v_spec = _qkv_specs(t, 0, 2)
    return pl.pallas_call(
        body, name=name, grid=(2, nq), in_specs=[q_spec, k_spec, v_spec],
        out_specs=[pl.BlockSpec((QB, LANES), lambda hp, i: (i, hp)),
                   pl.BlockSpec((2, None, t // KB, QB, KB), lambda hp, i: (hp, i, 0, 0, 0))],
        out_shape=[_sds((t, W_SB), BF16), _sds((4, nq, t // KB, QB, KB), BF16)],
        compiler_params=_cparams(2))(qkv, qkv, qkv)


def _hs(h):
    return slice(h * HEAD_DIM, (h + 1) * HEAD_DIM)


def sb_bwd(qkv, qkv_t, w, do, do_t, name):
    t = qkv.shape[0]
    nq = t // QB

    def body(q_ref, k_ref, v_ref, do_ref, qt_ref, dot_ref, w_ref, dq_ref, dkt_ref, dvt_ref):
        i = pl.program_id(1)

        @pl.when(i == 0)
        def _():
            dkt_ref[...] = jnp.zeros_like(dkt_ref)
            dvt_ref[...] = jnp.zeros_like(dvt_ref)

        groups = i // KSUB + 1
        tri_before = (_iota2((KB, KB), 0) < _iota2((KB, KB), 1)).astype(BF16)
        t_idx = i * QB + _iota2((QB, KB), 0)
        qs = [q_ref[:, _hl(h)] for h in range(2)]
        dos = [do_ref[:, _hl(h)] for h in range(2)]
        qts = [qt_ref[_hs(h), :] for h in range(2)]
        dots = [dot_ref[_hs(h), :] for h in range(2)]

        def grads(g, carry, masked):
            strict = (g * KB + _iota2((QB, KB), 1)) < t_idx
            out = []
            for h in range(2):
                head, dq = carry[2 * h], carry[2 * h + 1]
                k = k_ref[_krows(g), _hl(h)]
                v = v_ref[_krows(g), _hl(h)]
                wb = w_ref[h, g]
                z = lax.dot_general(qs[h], k, NT, preferred_element_type=F32)
                beta = _sigmoid(z)
                e = lax.dot_general(dos[h], v, NT, preferred_element_type=F32) * wb.astype(F32)
                before = _split_dot(e, tri_before) + head
                dz = e * (1.0 - beta) - before * beta
                if masked:
                    dz = jnp.where(strict, dz, 0.0)
                dzb = dz.astype(BF16)
                dq = dq + jnp.dot(dzb, k, preferred_element_type=F32)
                dkt_ref[g, _hs(h), :] += jnp.dot(qts[h], dzb, preferred_element_type=F32)
                dvt_ref[g, _hs(h), :] += jnp.dot(dots[h], wb, preferred_element_type=F32)
                out += [head + jnp.sum(e, axis=1, keepdims=True), dq]
            return tuple(out)

        init = (jnp.zeros((QB, 1), F32), jnp.zeros((QB, HEAD_DIM), F32)) * 2
        res = lax.fori_loop(0, groups - 1, lambda g, c: grads(g, c, False), init)
        res = grads(groups - 1, res, True)
        for h in range(2):
            dq_ref[:, _hl(h)] = (res[2 * h + 1] * SCALE).astype(dq_ref.dtype)

    q_spec, k_spec, v_spec = _qkv_specs(t, 0, 2)
    blk = pl.BlockSpec((QB, LANES), lambda hp, i: (i, hp))
    blk_t = pl.BlockSpec((LANES, QB), lambda hp, i: (hp, i))
    acc_t = pl.BlockSpec((None, t // KB, LANES, KB), lambda hp, i: (hp, 0, 0, 0))
    acc_sds = _sds((2, t // KB, LANES, KB), F32)
    return pl.pallas_call(
        body, name=name, grid=(2, nq),
        in_specs=[q_spec, k_spec, v_spec, blk, blk_t, blk_t,
                  pl.BlockSpec((2, None, t // KB, QB, KB), lambda hp, i: (hp, i, 0, 0, 0))],
        out_specs=[blk, acc_t, acc_t],
        out_shape=[_sds((t, W_SB), BF16), acc_sds, acc_sds],
        compiler_params=_cparams(2))(qkv, qkv, qkv, do, qkv_t, do_t, w)


def fox_fwd(qkv, fcol, frow, name):
    t = qkv.shape[0]
    nq = t // QB

    def body(q_ref, k_ref, v_ref, fc_ref, fr_ref, o_ref, lse_ref):
        hp = pl.program_id(0)
        i = pl.program_id(1)
        groups = i // KSUB + 1
        t_idx = i * QB + _iota2((QB, KB), 0)
        lane = _iota2((QB, LANES), 1)
        sub = _iota2((8, KB), 0)
        qs = [q_ref[:, _hl(h)] for h in range(2)]
        f_qs = [jnp.sum(jnp.where(lane == hp * 2 + h, fc_ref[...], 0.0), axis=1, keepdims=True)
                for h in range(2)]

        def step(g, carry, masked):
            causal = (g * KB + _iota2((QB, KB), 1)) <= t_idx
            fr = fr_ref[g]
            out = []
            for h in range(2):
                m, l, acc = carry[3 * h:3 * h + 3]
                k = k_ref[_krows(g), _hl(h)]
                v = v_ref[_krows(g), _hl(h)]
                f_k = jnp.sum(jnp.where(sub == hp * 2 + h, fr, 0.0), axis=0, keepdims=True)
                z = lax.dot_general(qs[h], k, NT, preferred_element_type=F32) + f_qs[h] - f_k
                if masked:
                    z = jnp.where(causal, z, NEG)
                m_new = jnp.maximum(m, jnp.max(z, axis=1, keepdims=True))
                p = jnp.exp(z - m_new)
                corr = jnp.exp(m - m_new)
                l = l * corr + jnp.sum(p, axis=1, keepdims=True)
                acc = acc * corr + jnp.dot(p.astype(BF16), v, preferred_element_type=F32)
                out += [m_new, l, acc]
            return tuple(out)

        init = (jnp.full((QB, 1), NEG, F32), jnp.zeros((QB, 1), F32), jnp.zeros((QB, HEAD_DIM), F32)) * 2
        res = lax.fori_loop(0, groups - 1, lambda g, c: step(g, c, False), init)
        res = step(groups - 1, res, True)
        for h in range(2):
            m, l, acc = res[3 * h:3 * h + 3]
            o_ref[:, _hl(h)] = (acc / l).astype(o_ref.dtype)
            lse_ref[:, _hl(h)] = jnp.broadcast_to(m + jnp.log(l), (QB, HEAD_DIM))

    q_spec, k_spec, v_spec = _qkv_specs(t, 18, 2)
    blk = pl.BlockSpec((QB, LANES), lambda hp, i: (i, hp))
    return pl.pallas_call(
        body, name=name, grid=(2, nq),
        in_specs=[q_spec, k_spec, v_spec, pl.BlockSpec((QB, LANES), lambda hp, i: (i, 0)),
                  pl.BlockSpec((t // KB, 8, KB), lambda hp, i: (0, 0, 0))],
        out_specs=[blk, blk],
        out_shape=[_sds((t, W_FOX), BF16), _sds((t, W_FOX), F32)],
        compiler_params=_cparams(2))(qkv, qkv, qkv, fcol, frow)


def fox_bwd(qkv, qkv_t, fcol, frow, o, lse, do, do_t, name, do_col=0):
    t = qkv.shape[0]
    nq = t // QB

    def body(q_ref, k_ref, v_ref, fc_ref, fr_ref, o_ref, lse_ref, do_ref, qt_ref, dot_ref,
             dq_ref, dk_ref, dv_ref, dfr_ref):
        hp = pl.program_id(0)
        i = pl.program_id(1)
        qts = [qt_ref[_hs(h), :] for h in range(2)]
        dots = [dot_ref[_hs(h), :] for h in range(2)]

        @pl.when(i == 0)
        def _():
            dk_ref[...] = jnp.zeros_like(dk_ref)
            dv_ref[...] = jnp.zeros_like(dv_ref)

        @pl.when((i == 0) & (hp == 0))
        def _():
            dfr_ref[...] = jnp.zeros_like(dfr_ref)

        groups = i // KSUB + 1
        t_idx = i * QB + _iota2((QB, KB), 0)
        lane = _iota2((QB, LANES), 1)
        sub = _iota2((8, KB), 0)
        qs = [q_ref[:, _hl(h)] for h in range(2)]
        dos = [do_ref[:, _hl(h)] for h in range(2)]
        f_qs = [jnp.sum(jnp.where(lane == hp * 2 + h, fc_ref[...], 0.0), axis=1, keepdims=True)
                for h in range(2)]
        lse_qs = [lse_ref[:, h * HEAD_DIM:h * HEAD_DIM + 1] for h in range(2)]
        deltas = [jnp.sum(dos[h].astype(F32) * o_ref[:, _hl(h)].astype(F32), axis=1, keepdims=True)
                  for h in range(2)]

        def step(g, dqs, masked):
            causal = (g * KB + _iota2((QB, KB), 1)) <= t_idx
            fr = fr_ref[g]
            out = []
            dfr = jnp.zeros((8, KB), F32)
            for h in range(2):
                k = k_ref[_krows(g), _hl(h)]
                v = v_ref[_krows(g), _hl(h)]
                f_k = jnp.sum(jnp.where(sub == hp * 2 + h, fr, 0.0), axis=0, keepdims=True)
                z = lax.dot_general(qs[h], k, NT, preferred_element_type=F32) + f_qs[h] - f_k
                p = jnp.exp(z - lse_qs[h])
                if masked:
                    p = jnp.where(causal, p, 0.0)
                dp = lax.dot_general(dos[h], v, NT, preferred_element_type=F32)
                ds = p * (dp - deltas[h])
                dsb = ds.astype(BF16)
                out.append(dqs[h] + jnp.dot(dsb, k, preferred_element_type=F32))
                dk_ref[g, _hs(h), :] += jnp.dot(qts[h], dsb, preferred_element_type=F32)
                dv_ref[g, _hs(h), :] += jnp.dot(dots[h], p.astype(BF16), preferred_element_type=F32)
                colsum = jnp.sum(ds, axis=0, keepdims=True)
                dfr = dfr + jnp.where(sub == hp * 2 + h, -colsum, 0.0)
            dfr_ref[g] += dfr
            return tuple(out)

        res = lax.fori_loop(0, groups - 1, lambda g, c: step(g, c, False),
                            (jnp.zeros((QB, HEAD_DIM), F32),) * 2)
        res = step(groups - 1, res, True)
        for h in range(2):
            dq_ref[:, _hl(h)] = (res[h] * SCALE).astype(dq_ref.dtype)

    q_spec, k_spec, v_spec = _qkv_specs(t, 18, 2)
    blk = pl.BlockSpec((QB, LANES), lambda hp, i: (i, hp))
    frs = pl.BlockSpec((t // KB, 8, KB), lambda hp, i: (0, 0, 0))
    acc_t = pl.BlockSpec((None, t // KB, LANES, KB), lambda hp, i: (hp, 0, 0, 0))
    acc_sds = _sds((2, t // KB, LANES, KB), F32)
    return pl.pallas_call(
        body, name=name, grid=(2, nq),
        in_specs=[q_spec, k_spec, v_spec, pl.BlockSpec((QB, LANES), lambda hp, i: (i, 0)), frs,
                  blk, blk, pl.BlockSpec((QB, LANES), lambda hp, i: (i, do_col + hp)),
                  pl.BlockSpec((LANES, QB), lambda hp, i: (18 + hp, i)),
                  pl.BlockSpec((LANES, QB), lambda hp, i: (do_col + hp, i))],
        out_specs=[blk, acc_t, acc_t, frs],
        out_shape=[_sds((t, W_FOX), BF16), acc_sds, acc_sds, _sds((t // KB, 8, KB), F32)],
        compiler_params=_cparams(2))(qkv, qkv, qkv, fcol, frow, o, lse, do, qkv_t, do_t)


def _frow_to_groups(frow):
    n = frow.shape[0] // KSUB
    return frow.reshape(n, KSUB, 8, QB).transpose(0, 2, 1, 3).reshape(n, 8, KB)


def _frow_from_groups(frow):
    n = frow.shape[0]
    return frow.reshape(n, 8, KSUB, QB).transpose(0, 2, 1, 3).reshape(n * KSUB, 8, QB)


def _chunk_band():
    qi = _iota2((QB, CH_KEYS), 0)
    kj = _iota2((QB, CH_KEYS), 1)
    dchunk = (qi >> 6) + LEFT_CHUNKS - (kj >> 6)
    return jnp.where((dchunk >= 0) & (dchunk <= LEFT_CHUNKS), 0.0, NEG)


def _chunk_pad_row(i):
    kj = _iota2((1, CH_KEYS), 1)
    return jnp.where((i - (CH_WIN - 1)) * QB + kj >= 0, 0.0, NEG)


CH_PAD = (CH_WIN - 1) * QB
CH_STEP_HEADS = 4
CH_COLS = CH_STEP_HEADS * HEAD_DIM


def _window(i):
    return pl.ds(pl.multiple_of(i * QB, QB), CH_KEYS)


def _chunk_weights(q, kw, bias, pad_row):
    z = lax.dot_general(q, kw, NT, preferred_element_type=F32) + bias + pad_row
    e = jnp.exp(z - jnp.max(z, axis=1, keepdims=True))
    return e, 1.0 / jnp.sum(e, axis=1, keepdims=True)


def _chunk_specs(t):
    q_spec = pl.BlockSpec((QB, CH_COLS), lambda hp, i: (i, 3 * W_SB // CH_COLS + hp))
    kv_spec = pl.BlockSpec((t + CH_PAD, CH_COLS), lambda hp, i: (0, hp))
    return q_spec, kv_spec


def chunk_fwd(qkv, kp, vp, bias, name):
    t = qkv.shape[0]
    nq = t // QB

    def body(q_ref, k_ref, v_ref, b_ref, o_ref):
        i = pl.program_id(1)
        pad_row = _chunk_pad_row(i)
        for h in range(CH_STEP_HEADS):
            e, inv = _chunk_weights(q_ref[:, _hl(h)], k_ref[_window(i), _hl(h)], b_ref[h], pad_row)
            o = jnp.dot(e.astype(BF16), v_ref[_window(i), _hl(h)], preferred_element_type=F32)
            o_ref[:, _hl(h)] = (o * inv).astype(o_ref.dtype)

    q_spec, kv_spec = _chunk_specs(t)
    return pl.pallas_call(
        body, name=name, grid=(W_CH // CH_COLS, nq),
        in_specs=[q_spec, kv_spec, kv_spec,
                  pl.BlockSpec((CH_STEP_HEADS, QB, CH_KEYS), lambda hp, i: (hp, 0, 0))],
        out_specs=pl.BlockSpec((QB, CH_COLS), lambda hp, i: (i, hp)),
        out_shape=_sds((t, W_CH), BF16), compiler_params=_cparams(2))(qkv, kp, vp, bias)


def chunk_bwd(qkv, qkv_t, kp, vp, bias, do, do_t, name, do_col=0):
    t = qkv.shape[0]
    nq = t // QB

    def body(q_ref, k_ref, v_ref, b_ref, do_ref, qt_ref, dot_ref, dq_ref, dk_ref, dv_ref, db_ref):
        i = pl.program_id(1)

        @pl.when(i == 0)
        def _():
            dk_ref[...] = jnp.zeros_like(dk_ref)
            dv_ref[...] = jnp.zeros_like(dv_ref)
            db_ref[...] = jnp.zeros_like(db_ref)

        pad_row = _chunk_pad_row(i)
        for h in range(CH_STEP_HEADS):
            q = q_ref[:, _hl(h)]
            dov = do_ref[:, _hl(h)]
            kw = k_ref[_window(i), _hl(h)]
            e, inv = _chunk_weights(q, kw, b_ref[h], pad_row)
            p = e * inv
            dp = lax.dot_general(dov, v_ref[_window(i), _hl(h)], NT, preferred_element_type=F32)
            ds = p * (dp - jnp.sum(p * dp, axis=1, keepdims=True))
            db_ref[h] += ds
            dsb = ds.astype(BF16)
            dq_ref[:, _hl(h)] = (jnp.dot(dsb, kw, preferred_element_type=F32) * SCALE).astype(dq_ref.dtype)
            dkt = jnp.dot(qt_ref[_hs(h), :], dsb, preferred_element_type=F32)
            dvt = jnp.dot(dot_ref[_hs(h), :], p.astype(BF16), preferred_element_type=F32)
            for b in range(CH_WIN):
                dk_ref[i + b, _hs(h), :] += dkt[:, b * QB:(b + 1) * QB]
                dv_ref[i + b, _hs(h), :] += dvt[:, b * QB:(b + 1) * QB]

    q_spec, kv_spec = _chunk_specs(t)
    blk = pl.BlockSpec((QB, CH_COLS), lambda hp, i: (i, hp))
    bspec = pl.BlockSpec((CH_STEP_HEADS, QB, CH_KEYS), lambda hp, i: (hp, 0, 0))
    nblk = nq + CH_WIN - 1
    acc_t = pl.BlockSpec((None, nblk, CH_COLS, QB), lambda hp, i: (hp, 0, 0, 0))
    acc_sds = _sds((W_CH // CH_COLS, nblk, CH_COLS, QB), F32)
    return pl.pallas_call(
        body, name=name, grid=(W_CH // CH_COLS, nq),
        in_specs=[q_spec, kv_spec, kv_spec, bspec,
                  pl.BlockSpec((QB, CH_COLS), lambda hp, i: (i, do_col + hp)),
                  pl.BlockSpec((CH_COLS, QB), lambda hp, i: (3 * W_SB // CH_COLS + hp, i)),
                  pl.BlockSpec((CH_COLS, QB), lambda hp, i: (do_col + hp, i))],
        out_specs=[blk, acc_t, acc_t, bspec],
        out_shape=[_sds((t, W_CH), BF16), acc_sds, acc_sds, _sds((N_HEADS_CH, QB, CH_KEYS), F32)],
        compiler_params=_cparams(2))(qkv, kp, vp, bias, do, qkv_t, do_t)


def _sum_parts(p_ref):
    total = p_ref[0].astype(F32)
    for p in range(1, p_ref.shape[0]):
        total = total + p_ref[p].astype(F32)
    return total


def sum_parts(parts, grid, p_spec, o_spec, out_sds, name):
    def body(p_ref, o_ref):
        o_ref[...] = _sum_parts(p_ref)

    return pl.pallas_call(body, name=name, grid=grid, in_specs=[p_spec], out_specs=o_spec,
                          out_shape=out_sds, compiler_params=_cparams(len(grid)))(parts)


def adamw(parts, w, m, v, grid, p_specs, w_spec, name):
    c1 = 1.0 / (1.0 - ADAM_B1 ** ADAM_STEP)
    c2 = 1.0 / (1.0 - ADAM_B2 ** ADAM_STEP)
    n = len(parts)

    def body(*refs):
        w_ref, m_ref, v_ref, g_out, d_out, m_out, v_out = refs[n:]
        g = _sum_parts(refs[0])
        for q in range(1, n):
            g = jnp.where(pl.program_id(0) == q, _sum_parts(refs[q]), g)
        m_new = ADAM_B1 * m_ref[...] + (1.0 - ADAM_B1) * g
        v_new = ADAM_B2 * v_ref[...] + (1.0 - ADAM_B2) * (g * g)
        m_hat = m_new * c1
        v_hat = v_new * c2
        g_out[...] = g
        d_out[...] = -ADAM_LR * (m_hat / (jnp.sqrt(v_hat) + ADAM_EPS) + ADAM_WD * w_ref[...])
        m_out[...] = m_new
        v_out[...] = v_new

    out = _sds(w.shape, F32)
    return pl.pallas_call(
        body, name=name, grid=grid, in_specs=[*p_specs, w_spec, w_spec, w_spec],
        out_specs=[w_spec] * 4, out_shape=[out] * 4,
        compiler_params=_cparams(len(grid)))(*parts, w, m, v)


def _ffn_fwd(x, gain, wa, wb_after, s, tm, tag, on_event, deps=()):
    t = x.shape[0]
    hn = rmsnorm_fwd(x, gain, tm, f"rms_{tag}", deps)
    gu, act = ffn_in_swiglu(hn, wa, s, min(2 * tm, t), f"ffn_in_{tag}")
    relayed = on_event("act", act)
    wb = wb_after(act)
    y = ffn_out_residual(act, wb, x, s, min(2 * tm, t), f"ffn_out_{tag}", relayed)
    return y, (hn, gu, act), wb


def _ffn_bwd(dy, x, gain, saved, wa, wb, s, tm, tag, on_grads):
    t = x.shape[0]
    hn, gu, act = saved
    dgu = ffn_dact_swiglu(dy, wb, gu, s, min(2 * tm, t), f"ffn_dact_{tag}")
    dwb = matmul(TN, act, dy, _sds((4, FF_BLK, D_MODEL), BF16), (4, 1, 1),
                 pl.BlockSpec((None, t, FF_BLK), lambda i, j, k: (i, 0, 0)),
                 pl.BlockSpec((t, D_MODEL), lambda i, j, k: (0, 0)),
                 pl.BlockSpec((None, FF_BLK, D_MODEL), lambda i, j, k: (i, 0, 0)),
                 None, name=f"ffn_dwout_{tag}", alpha=0.5)
    dwa = matmul(TN, dgu, hn, _sds((8, FF_BLK, D_MODEL), BF16), (1, 8, 1),
                 pl.BlockSpec((None, None, t, FF_BLK), lambda i, j, k: (j % 4, j // 4, 0, 0)),
                 pl.BlockSpec((t, D_MODEL), lambda i, j, k: (0, 0)),
                 pl.BlockSpec((None, FF_BLK, D_MODEL), lambda i, j, k: (j, 0, 0)),
                 None, name=f"ffn_dwin_{tag}")
    deps = on_grads(dwa, dwb)
    return ffn_dh_norm_bwd(dgu, wa, s, x, gain, dy, tm, f"ffn_dh_{tag}", deps)


BR_ROWS = ((0, 1), (1, 2), (3, 1))

_Q_COLUMN_SCALE = np.ones((1, QKV_WIDTH), np.float32)
for _lo, _width in ((0, W_SB), (3 * W_SB, W_CH), (3 * (W_SB + W_CH), W_FOX)):
    _Q_COLUMN_SCALE[0, _lo:_lo + _width] = SCALE


def _mixer_fwd(x, gain, wqkv, wf, wgate, late_after, bq, bf, bg, bias, layer, tm, tag, on_event):
    t = x.shape[0]
    nt = t // tm
    hm = rmsnorm_fwd(x, gain, tm, f"rms_{tag}")
    a_full = pl.BlockSpec((tm, D_MODEL), lambda i, j, k: (i, 0))
    wide_out = pl.BlockSpec((tm, D_MODEL), lambda i, j, k: (i, j))
    wide_b = pl.BlockSpec((1, D_MODEL), lambda i, j, k: (0, j))
    qkv, qkv_t = matmul(NN, hm, wqkv, _sds((t, QKV_WIDTH), BF16), (nt, 3, 1), a_full,
                        pl.BlockSpec((None, D_MODEL, D_MODEL), lambda i, j, k: (layer, 0, j)), wide_out, None,
                        name=f"proj_qkv_{tag}", bias=bq, bias_spec=wide_b,
                        scale=jnp.asarray(_Q_COLUMN_SCALE), scale_spec=wide_b,
                        out_t_sds=_sds((QKV_WIDTH, t), BF16),
                        out_t_spec=pl.BlockSpec((D_MODEL, tm), lambda i, j, k: (j, i)))
    relayed = on_event("qkv", qkv)
    gates, f, fcol, frow = proj_gates_forget(hm, wgate, layer + 1, wf, layer, bg, bf, tm,
                                             f"proj_gate_{tag}", relayed)
    frow = _frow_to_groups(frow)
    o_sb, w_sb = sb_fwd(qkv, f"sb_fwd_{tag}")
    relayed = on_event("o_sb", o_sb)
    kp = jnp.pad(qkv[:, 10 * LANES:14 * LANES], ((CH_PAD, 0), (0, 0)))
    vp = jnp.pad(qkv[:, 14 * LANES:18 * LANES], ((CH_PAD, 0), (0, 0)))
    o_ch = chunk_fwd(qkv, kp, vp, bias, f"chunk_fwd_{tag}")
    o_fox, lse = fox_fwd(qkv, fcol, frow, f"fox_fwd_{tag}")
    wbr, wout = late_after(o_fox)
    ys, merged = branch_merge((o_sb, o_ch, o_fox), wbr, layer, gates, tm, f"branch_merge_{tag}", relayed)
    x_new = matmul(NN, merged, wout, _sds((t, D_MODEL), F32), (nt, 1, 1), a_full,
                   pl.BlockSpec((None, D_MODEL, D_MODEL), lambda i, j, k: (layer, 0, 0)), a_full, None,
                   name=f"wout_{tag}", res=x, res_spec=a_full)
    saved = (hm, qkv, gates, f, fcol, frow, o_sb, o_ch, o_fox, lse, ys, merged, kp, vp, w_sb, qkv_t)
    return x_new, saved, wbr, wout


def _mixer_bwd(dy, x, gain, saved, wqkv, wf, wgate, wbr, wout, bias, layer, tm, tag, on_grads):
    t = x.shape[0]
    nt = t // tm
    hm, qkv, gates, f, fcol, frow, o_sb, o_ch, o_fox, lse, ys, merged, kp, vp, w_sb, qkv_t = saved
    a_full = pl.BlockSpec((tm, D_MODEL), lambda i, j, k: (i, 0))
    red_row = pl.BlockSpec((tm, D_MODEL), lambda i, j, k: (k, 0))
    sq = pl.BlockSpec((D_MODEL, D_MODEL), lambda i, j, k: (0, 0))
    dgates, dys = dmerged_merge_bwd(dy, wout, layer, gates, ys, tm // 2, f"dmerged_{tag}")
    all_t = pl.BlockSpec((t, D_MODEL), lambda i, j, k: (0, 0))
    dwout = matmul(TN, merged, dy, _sds((D_MODEL, D_MODEL), BF16), (1, 1, 1), all_t, all_t, sq,
                   None, name=f"dwout_{tag}")
    do, do_t, dwbr = branch_bwd(dys, (o_sb, o_ch, o_fox), wbr, layer, tm, f"dbranch_{tag}")
    dq_a, dk_a, dv_a = sb_bwd(qkv, qkv_t, w_sb, do, do_t, f"sb_bwd_{tag}")
    dk_a, dv_a = _keys_major(dk_a), _keys_major(dv_a)
    dq_b, dk_b, dv_b, dbias = chunk_bwd(qkv, qkv_t, kp, vp, bias, do, do_t, f"chunk_bwd_{tag}",
                                        do_col=W_SB // CH_COLS)
    dk_b, dv_b = [x[:, CH_WIN - 1:].transpose(1, 3, 0, 2).reshape(t, W_CH) for x in (dk_b, dv_b)]
    dq_c, dk_c, dv_c, dfrow = fox_bwd(qkv, qkv_t, fcol, frow, o_fox, lse, do, do_t, f"fox_bwd_{tag}",
                                      do_col=(W_SB + W_CH) // LANES)
    dk_c, dv_c = _keys_major(dk_c), _keys_major(dv_c)
    df = forget_cumsum_bwd(_frow_from_groups(dfrow), f, f"fcum_bwd_{tag}")
    dqkv = jnp.concatenate([p.astype(BF16) for p in
                            (dq_a, dk_a, dv_a, dq_b, dk_b, dv_b, dq_c, dk_c, dv_c)], axis=1)
    dtab = rel_bias_scatter(dbias, f"rel_scatter_{tag}")

    all_rows = pl.BlockSpec((t, D_MODEL), lambda i, j, k: (0, 0))
    wide_b = pl.BlockSpec((t, D_MODEL), lambda i, j, k: (0, j))
    wide_o = pl.BlockSpec((D_MODEL, D_MODEL), lambda i, j, k: (0, j))
    wide_cs = pl.BlockSpec((1, D_MODEL), lambda i, j, k: (0, j))
    dwqkv, dbq = matmul(TN, hm, dqkv, _sds((D_MODEL, QKV_WIDTH), BF16), (1, 3, 1), all_rows, wide_b,
                        wide_o, None, name=f"dwqkv_{tag}",
                        colsum_sds=_sds((1, QKV_WIDTH), F32), colsum_spec=wide_cs)
    dwgate, dbg = matmul(TN, hm, dgates, _sds((D_MODEL, 3 * D_MODEL), BF16), (1, 3, 1), all_rows,
                         wide_b, wide_o, None, name=f"dwgate_{tag}",
                         colsum_sds=_sds((1, 3 * D_MODEL), F32), colsum_spec=wide_cs)
    dwf, dbf = matmul(TN, hm, df, _sds((D_MODEL, LANES), BF16), (1, 1, 1), all_rows,
                      pl.BlockSpec((t, LANES), lambda i, j, k: (0, 0)),
                      pl.BlockSpec((D_MODEL, LANES), lambda i, j, k: (0, 0)), None,
                      name=f"dwf_{tag}", colsum_sds=_sds((1, LANES), F32),
                      colsum_spec=pl.BlockSpec((1, LANES), lambda i, j, k: (0, 0)))
    deps = on_grads(dict(dwqkv=dwqkv, dwgate=dwgate, dwf=dwf, dwbr=dwbr, dwout=dwout))
    wide_a = pl.BlockSpec((tm, QKV_WIDTH), lambda i, j, k: (i, 0))
    dhm = matmul(NT, dqkv, wqkv, _sds((t, D_MODEL), F32), (nt, 1, 1), wide_a,
                 pl.BlockSpec((None, D_MODEL, QKV_WIDTH), lambda i, j, k: (layer, 0, 0)), a_full,
                 None, name=f"dhm_qkv_{tag}", deps=deps)
    dx, dgain = mixer_dh_norm_bwd(dhm, dgates, wgate, layer + 1, df, wf, layer, x, gain, dy, tm,
                                  f"dhm_gate_{tag}")
    return dx, dict(dbq=dbq, dbg=dbg, dbf=dbf, dtab=dtab, dgain=dgain)


def _pack_small(pieces):
    flat = jnp.concatenate([p.reshape(-1).astype(F32) for p in pieces])
    flat = jnp.pad(flat, (0, SMALL_ROWS * LANES - flat.shape[0]))
    return flat.reshape(SMALL_ROWS, LANES)


def _unpack_small(packed, shapes):
    flat = packed.reshape(-1)
    out, pos = [], 0
    for shp in shapes:
        n = int(np.prod(shp))
        out.append(flat[pos:pos + n].reshape(shp))
        pos += n
    return out


def kernel(x, g_ffn1, w_ffn1_in, w_ffn1_out, g_mix, w_in, b_in, rel_bias, w_br_sb, w_br_ch, w_br_fox, w_out, g_ffn2, w_ffn2_in, w_ffn2_out, g_final, loss_target, m_g_ffn1, m_w_ffn1_in, m_w_ffn1_out, m_g_mix, m_w_in, m_b_in, m_rel_bias, m_w_br_sb, m_w_br_ch, m_w_br_fox, m_w_out, m_g_ffn2, m_w_ffn2_in, m_w_ffn2_out, m_g_final, v_g_ffn1, v_w_ffn1_in, v_w_ffn1_out, v_g_mix, v_w_in, v_b_in, v_rel_bias, v_w_br_sb, v_w_br_ch, v_w_br_fox, v_w_out, v_g_ffn2, v_w_ffn2_in, v_w_ffn2_out, v_g_final):
    t = x.shape[1]
    tm = min(512, t)
    xs = x[0]
    target = loss_target[0]
    f_lo, f_hi = QKV_WIDTH, QKV_WIDTH + N_HEADS_FOX

    def ffn_shards(w_in_, w_out_, l):
        return [w_in_[l:l + 1].astype(BF16), w_out_[l:l + 1].astype(BF16)]

    def mixer_shards(l):
        wl = w_in[l]
        return [jnp.stack([wl[:, :QKV_WIDTH], wl[:, f_hi:]]).astype(BF16),
                jnp.pad(wl[:, f_lo:f_hi], ((0, 0), (0, LANES - N_HEADS_FOX)))[None].astype(BF16),
                w_out[l:l + 1].astype(BF16),
                jnp.concatenate([w_br_sb[l], w_br_ch[l], w_br_fox[l]], axis=0)[None].astype(BF16)]

    gathers = {}
    gather_tokens = []

    def start_gather(shards, name):
        handle = gather_start(shards, name, deps=gather_tokens[-1:])
        gather_tokens.append(handle["token"])
        return handle

    def relay(handle, after):
        if "send2" not in handle:
            gather_relay(handle, after)

    relay_on = {("mix", 0, "qkv"): ("mix", 0, 1), ("mix", 0, "o_sb"): ("ffn2", 0, 0),
                ("ffn2", 0, "act"): ("ffn1", 1, 0), ("ffn1", 1, "act"): ("mix", 1, 0),
                ("mix", 1, "qkv"): ("ffn2", 1, 0)}

    def on_event(grp, l):
        def fire(event, array):
            target = relay_on.get((grp, l, event))
            if target is None:
                return ()
            handle = gathers[target[:2]][target[2]]
            relay(handle, array)
            return (handle["relay_token"],)
        return fire

    for l in range(DEPTH):
        for grp, shards in (("ffn1", ffn_shards(w_ffn1_in, w_ffn1_out, l)), ("mix", mixer_shards(l)),
                            ("ffn2", ffn_shards(w_ffn2_in, w_ffn2_out, l))):
            cut = len(shards) // 2
            if l == 0 and grp != "ffn2":
                gathers[(grp, l)] = (start_gather(shards[:cut], f"gather_{grp}_l{l}_a"),
                                     start_gather(shards[cut:], f"gather_{grp}_l{l}_b"))
            else:
                gathers[(grp, l)] = (start_gather(shards, f"gather_{grp}_l{l}"),)

    def gathered(key, after):
        hs = gathers[key]
        cut = hs[0]["n"]
        relay(hs[0], after)
        first = gather_finish(hs[0], after)
        if len(hs) == 1:
            return first[:cut // 2], lambda later: first[cut // 2:]

        def second(later):
            relay(hs[1], later)
            return gather_finish(hs[1], later)

        return first, second

    def ffn_weights(key, after):
        (wa_,), rest = gathered(key, after)
        return wa_, lambda later: rest(later)[0].reshape(1, 4, FF_BLK, D_MODEL)

    def mixer_weights(key, after):
        (wc_, wf_), rest = gathered(key, after)

        def late(later):
            wout_, wbr_ = rest(later)
            return (wbr_.transpose(0, 2, 1, 3).reshape(1, D_MODEL, D_MODEL), wout_.reshape(1, D_MODEL, D_MODEL))

        return wc_.reshape(2, D_MODEL, QKV_WIDTH), wf_.reshape(1, D_MODEL, LANES), late

    bq = b_in[:, None, :QKV_WIDTH]
    bf = jnp.pad(b_in[:, f_lo:f_hi], ((0, 0), (0, LANES - N_HEADS_FOX)))[:, None, :]
    bg = b_in[:, None, f_hi:]
    tab_t = jnp.pad(rel_bias.transpose(0, 2, 1), ((0, 0), (0, 0), (0, REL_PAD - N_REL)))

    h = xs
    saved = []
    weights = []
    for l in range(DEPTH):
        bias = rel_bias_build(tab_t[l], f"rel_build_l{l}").reshape(N_HEADS_CH, QB, CH_KEYS)
        x0 = h
        wa1, wb1_after = ffn_weights(("ffn1", l), x0)
        x1, s1, wb1 = _ffn_fwd(x0, g_ffn1[l:l + 1], wa1, wb1_after, 0, tm, f"ffn1_l{l}", on_event("ffn1", l),
                               deps=gather_tokens if l == 0 else ())
        wc, wf, late_after = mixer_weights(("mix", l), x1)
        x2, sm, wbr, wout = _mixer_fwd(x1, g_mix[l:l + 1], wc, wf, wc, late_after, bq[l], bf[l], bg[l],
                                       bias, 0, tm, f"mix_l{l}", on_event("mix", l))
        wa2, wb2_after = ffn_weights(("ffn2", l), x2)
        x3, s2, wb2 = _ffn_fwd(x2, g_ffn2[l:l + 1], wa2, wb2_after, 0, tm, f"ffn2_l{l}", on_event("ffn2", l))
        saved.append((x0, x1, x2, s1, sm, s2, bias))
        weights.append(((wa1, wb1), (wc, wf, wout, wbr), (wa2, wb2)))
        h = x3

    dx, dg_final, loss_blk = loss_head(h, g_final[None, :], target, tm, "loss_head")

    g_mix_l = [None] * DEPTH
    dgains = {}
    scatters = {}

    def scatter_ffn(key):
        def on_grads(dwa, dwb):
            scatters[key] = exchange_start(
                "scatter", [dwa[None], dwb.reshape(1, N_DEV, D_FF // N_DEV, D_MODEL)],
                f"scatter_{key[0]}_l{key[1]}")
            return (scatters[key]["token"],)
        return on_grads

    def scatter_mixer(key):
        def on_grads(gm):
            scatters[key] = exchange_start(
                "scatter",
                [gm["dwqkv"].reshape(1, N_DEV, LANES, QKV_WIDTH), gm["dwgate"].reshape(1, N_DEV, LANES, QKV_WIDTH),
                 gm["dwf"].reshape(1, N_DEV, LANES, LANES), gm["dwout"].reshape(1, N_DEV, LANES, D_MODEL),
                 gm["dwbr"].reshape(1, D_MODEL, N_DEV, LANES).transpose(0, 2, 1, 3)],
                f"scatter_{key[0]}_l{key[1]}")
            return (scatters[key]["token"],)
        return on_grads

    for l in reversed(range(DEPTH)):
        x0, x1, x2, s1, sm, s2, bias = saved[l]
        w1, (wc, wf, wout, wbr), w2 = weights[l]
        dx, dgains[("ffn2", l)] = _ffn_bwd(dx, x2, g_ffn2[l:l + 1], s2, *w2, 0, tm, f"ffn2_l{l}",
                                           scatter_ffn(("ffn2", l)))
        dx, g_mix_l[l] = _mixer_bwd(dx, x1, g_mix[l:l + 1], sm, wc, wf, wc, wbr, wout, bias, 0, tm,
                                    f"mix_l{l}", scatter_mixer(("mix", l)))
        dx, dgains[("ffn1", l)] = _ffn_bwd(dx, x0, g_ffn1[l:l + 1], s1, *w1, 0, tm, f"ffn1_l{l}",
                                           scatter_ffn(("ffn1", l)))

    small_shapes = []
    small_pieces = []
    small_w, small_m, small_v = [], [], []

    def add_small(piece, w, m, v):
        small_shapes.append(w.shape)
        small_pieces.append(piece)
        small_w.append(w); small_m.append(m); small_v.append(v)

    dg1 = jnp.concatenate([dgains[("ffn1", l)] for l in range(DEPTH)], axis=0)
    dgm = jnp.concatenate([g_mix_l[l]["dgain"] for l in range(DEPTH)], axis=0)
    dg2 = jnp.concatenate([dgains[("ffn2", l)] for l in range(DEPTH)], axis=0)
    db = jnp.stack([jnp.concatenate([g_mix_l[l]["dbq"][0], g_mix_l[l]["dbf"][0, :N_HEADS_FOX],
                                     g_mix_l[l]["dbg"][0]]) for l in range(DEPTH)])
    drel = jnp.stack([g_mix_l[l]["dtab"][:, :N_REL].T for l in range(DEPTH)])
    add_small(dg1, g_ffn1, m_g_ffn1, v_g_ffn1)
    add_small(dgm, g_mix, m_g_mix, v_g_mix)
    add_small(db, b_in, m_b_in, v_b_in)
    add_small(drel, rel_bias, m_rel_bias, v_rel_bias)
    add_small(dg2, g_ffn2, m_g_ffn2, v_g_ffn2)
    add_small(dg_final[0], g_final, m_g_final, v_g_final)
    loss_piece = loss_blk[0, 0:1]
    small_packed = _pack_small(small_pieces + [loss_piece])

    recv = {}
    last = ("ffn1", 0)
    for l in reversed(range(DEPTH)):
        for grp in ("ffn2", "mix", "ffn1"):
            if (grp, l) != last:
                recv[(grp, l)] = exchange_wait(scatters[(grp, l)], dx, f"scattered_{grp}_l{l}")

    def upd(parts, w, m, v, tr, name, rb0=0):
        _, r, c = w.shape
        nr = r // tr

        def p_spec(layer):
            pinned = (nr - 1) if layer == 0 else 0
            return pl.BlockSpec((N_DEV, None, tr, c),
                                lambda l, i: (0, 0, rb0 + jnp.where(l == layer, i, pinned), 0))

        return adamw(parts, w, m, v, (DEPTH, nr), [p_spec(0), p_spec(1)],
                     pl.BlockSpec((None, tr, c), lambda l, i: (l, i, 0)), name)

    def both(grp, k):
        return [recv[(grp, l)][k] for l in range(DEPTH)]

    out_rows = D_FF // N_DEV // 2
    def upd_transposed(parts, w, m, v, tr, name):
        tp = lambda a: jnp.transpose(a, (0, 2, 1))
        return [tp(o) for o in upd(parts, tp(w), tp(m), tp(v), tr, name)]

    in_rows = FF_BLK // 4
    r_ffn2_in = upd_transposed(both("ffn2", 0), w_ffn2_in, m_w_ffn2_in, v_w_ffn2_in, in_rows, "adamw_ffn2_in")
    r_ffn2_out = upd(both("ffn2", 1), w_ffn2_out, m_w_ffn2_out, v_w_ffn2_out, out_rows, "adamw_ffn2_out")
    r_out = upd(both("mix", 3), w_out, m_w_out, v_w_out, LANES, "adamw_w_out")
    r_br_sb = upd(both("mix", 4), w_br_sb, m_w_br_sb, v_w_br_sb, 256, "adamw_br_sb", rb0=0)
    r_br_ch = upd(both("mix", 4), w_br_ch, m_w_br_ch, v_w_br_ch, 256, "adamw_br_ch", rb0=1)
    r_br_fox = upd(both("mix", 4), w_br_fox, m_w_br_fox, v_w_br_fox, 256, "adamw_br_fox", rb0=3)

    def summed(parts, name):
        _, _, r, c = parts.shape
        return sum_parts(parts, (1,), pl.BlockSpec((N_DEV, None, r, c), lambda s: (0, 0, 0, 0)),
                         pl.BlockSpec((r, c), lambda s: (0, 0)), _sds((r, c), F32), name)

    g_w_in = jnp.stack([
        jnp.concatenate([summed(recv[("mix", l)][0], f"sum_wqkv_l{l}"),
                         summed(recv[("mix", l)][2], f"sum_wf_l{l}")[:, :N_HEADS_FOX],
                         summed(recv[("mix", l)][1], f"sum_wgate_l{l}")], axis=1) for l in range(DEPTH)])
    to_cols = lambda a: jnp.transpose(a, (2, 0, 1))
    n_cols = w_in.shape[2]
    col_blk = n_cols // 4
    win_spec = pl.BlockSpec((col_blk, DEPTH, LANES), lambda i: (i, 0, 0))
    r_in = adamw([to_cols(g_w_in)[None]], to_cols(w_in), to_cols(m_w_in), to_cols(v_w_in), (4,),
                 [pl.BlockSpec((1, col_blk, DEPTH, LANES), lambda i: (0, i, 0, 0))], win_spec, "adamw_w_in")
    r_in = [jnp.transpose(o, (1, 2, 0)) for o in r_in]

    recv[last] = exchange_wait(scatters[last], r_in[1], "scattered_ffn1_l0")
    r_ffn1_in = upd_transposed(both("ffn1", 0), w_ffn1_in, m_w_ffn1_in, v_w_ffn1_in, in_rows, "adamw_ffn1_in")
    r_ffn1_out = upd(both("ffn1", 1), w_ffn1_out, m_w_ffn1_out, v_w_ffn1_out, out_rows, "adamw_ffn1_out")

    small_sum = all_reduce_small(small_packed, "allreduce_small", deps=(r_ffn1_out[1],))
    n_small = sum(int(np.prod(s)) for s in small_shapes)
    loss = small_sum.reshape(-1)[n_small]
    sm_spec = pl.BlockSpec((SMALL_ROWS, LANES), lambda i: (0, 0))
    sm_out = adamw([small_sum[None]], _pack_small(small_w), _pack_small(small_m), _pack_small(small_v),
                   (1,), [pl.BlockSpec((1, SMALL_ROWS, LANES), lambda i: (0, 0, 0))], sm_spec, "adamw_small")
    sm_g, sm_d, sm_m, sm_v = [_unpack_small(o, small_shapes) for o in sm_out]

    def per_kind(k):
        small = (sm_g, sm_d, sm_m, sm_v)[k]
        return [small[0], r_ffn1_in[k], r_ffn1_out[k], small[1], r_in[k], small[2], small[3],
                r_br_sb[k], r_br_ch[k], r_br_fox[k], r_out[k], small[4], r_ffn2_in[k], r_ffn2_out[k],
                small[5]]

    return (loss, dx[None], *per_kind(0), *per_kind(1), *per_kind(2), *per_kind(3))
```

```python
import numpy as np
import jax
import jax.numpy as jnp
from jax import lax
from jax.experimental import pallas as pl
from jax.experimental.pallas import tpu as pltpu

F32 = jnp.float32
BF16 = jnp.bfloat16

N_DEV = 8
D_MODEL = 1024
DEPTH = 2
HEAD_DIM = 64
W_SB, W_CH, W_FOX = 256, 512, 256
QKV_WIDTH = 3 * (W_SB + W_CH + W_FOX)
N_HEADS_FOX = 4
N_HEADS_CH = 8
D_FF = 2816
FF_BLK = 2 * D_FF // N_DEV
CHUNK = 64
LEFT_CHUNKS = 8
MAX_REL = 128
N_REL = 2 * MAX_REL + 1
REL_PAD = 384
QB = 128
KB = 512
KSUB = KB // QB
CH_WIN = 5
CH_KEYS = CH_WIN * QB
RMS_EPS = 1e-6
NEG = -1e30
SCALE = HEAD_DIM ** -0.5
LANES = 128
VMEM_LIMIT = 56 * 1024 * 1024

ADAM_LR, ADAM_B1, ADAM_B2, ADAM_EPS, ADAM_WD, ADAM_STEP = 0.001, 0.9, 0.999, 1e-08, 0.01, 10

SMALL_ROWS = 192

MESH = pl.DeviceIdType.MESH
ANY = pl.BlockSpec(memory_space=pl.ANY)
HIGHEST = lax.Precision.HIGHEST

NN = (((1,), (0,)), ((), ()))
NT = (((1,), (1,)), ((), ()))
TN = (((0,), (0,)), ((), ()))


def _cparams(n_grid):
    return pltpu.CompilerParams(dimension_semantics=("arbitrary",) * n_grid,
                                vmem_limit_bytes=VMEM_LIMIT)


def _sds(shape, dtype):
    return jax.ShapeDtypeStruct(tuple(shape), dtype)


def _my_index():
    return 4 * lax.axis_index("x") + 2 * lax.axis_index("y") + lax.axis_index("c")


def _peer(mask):
    x, y, c = lax.axis_index("x"), lax.axis_index("y"), lax.axis_index("c")
    px = x ^ ((mask >> 2) & 1)
    py = y ^ ((mask >> 1) & 1)
    pc = c ^ (mask & 1)
    return (px, py, pc), 4 * px + 2 * py + pc


HBM_SPEC = pl.BlockSpec(memory_space=pltpu.HBM)
SEM_SPEC = pl.BlockSpec(memory_space=pltpu.SEMAPHORE)
EFFECT = pltpu.SideEffectType.DATAFLOW_SIDE_EFFECTING


def _exchange_refs(mode, in_ref, land_ref, me, pidx):
    if mode == "gather":
        return in_ref, land_ref.at[:, me], land_ref.at[:, pidx]
    return in_ref.at[:, pidx], land_ref.at[me], land_ref.at[pidx]


def _landing_shape(mode, a):
    if mode == "gather":
        s, r, c = a.shape
        return (s, N_DEV, r, c)
    s, _, r, c = a.shape
    return (N_DEV, s, r, c)


def _own_copy(mode, in_ref, land_ref, me, sem):
    if mode == "gather":
        return pltpu.make_async_copy(in_ref, land_ref.at[:, me], sem)
    return pltpu.make_async_copy(in_ref.at[:, me], land_ref.at[me], sem)


def exchange_start(mode, arrays, name, deps=()):
    n = len(arrays)
    lands0 = [lax.empty(_landing_shape(mode, a), a.dtype) for a in arrays]

    def body(*refs):
        in_refs, land_refs = refs[:n], refs[n:2 * n]
        outs_at = 2 * n + len(deps)
        send_sems, recv_sems, own_sems, token = refs[outs_at], refs[outs_at + 1], refs[outs_at + 2], refs[-1]
        mine = _my_index()
        for k in range(n):
            _own_copy(mode, in_refs[k], land_refs[k], mine, own_sems.at[k]).start()
            for mask in range(1, N_DEV):
                peer, pidx = _peer(mask)
                src, dst, _ = _exchange_refs(mode, in_refs[k], land_refs[k], mine, pidx)
                sem = k * (N_DEV - 1) + mask - 1
                pltpu.make_async_remote_copy(
                    src_ref=src, dst_ref=dst, send_sem=send_sems.at[sem], recv_sem=recv_sems.at[sem],
                    device_id=peer, device_id_type=MESH).start()
        token[...] = jnp.zeros_like(token)

    nsem = n * (N_DEV - 1)
    outs = pl.pallas_call(
        body, name=name,
        out_shape=(pltpu.SemaphoreType.DMA((nsem,)), pltpu.SemaphoreType.DMA((nsem,)),
                   pltpu.SemaphoreType.DMA((n,)),
                   *[pltpu.HBM(a.shape, a.dtype) for a in arrays],
                   *[pltpu.HBM(l.shape, l.dtype) for l in lands0], _sds((8, LANES), F32)),
        in_specs=[HBM_SPEC] * (2 * n) + [ANY] * len(deps),
        out_specs=(SEM_SPEC, SEM_SPEC, SEM_SPEC, *[HBM_SPEC] * (2 * n),
                   pl.BlockSpec(memory_space=pltpu.VMEM)),
        input_output_aliases={k: 3 + k for k in range(2 * n)},
        compiler_params=pltpu.CompilerParams(has_side_effects=EFFECT),
    )(*[pltpu.with_memory_space_constraint(a, pltpu.HBM) for a in arrays],
      *[pltpu.with_memory_space_constraint(l, pltpu.HBM) for l in lands0], *deps)
    return dict(mode=mode, n=n, send=outs[0], recv=outs[1], own=outs[2], ins=outs[3:3 + n],
                lands=outs[3 + n:3 + 2 * n], token=outs[-1])


def exchange_wait(handle, after, name):
    n, mode = handle["n"], handle["mode"]

    def body(*refs):
        in_refs, land_refs = refs[:n], refs[n:2 * n]
        send_sems, recv_sems, own_sems = refs[2 * n], refs[2 * n + 1], refs[2 * n + 2]
        mine = _my_index()
        for k in range(n):
            _own_copy(mode, in_refs[k], land_refs[k], mine, own_sems.at[k]).wait()
            for mask in range(1, N_DEV):
                peer, pidx = _peer(mask)
                src, _, here = _exchange_refs(mode, in_refs[k], land_refs[k], mine, pidx)
                sem = k * (N_DEV - 1) + mask - 1
                cp = pltpu.make_async_remote_copy(
                    src_ref=src, dst_ref=here, send_sem=send_sems.at[sem], recv_sem=recv_sems.at[sem],
                    device_id=peer, device_id_type=MESH)
                cp.wait_send()
                cp.wait_recv()

    thru = (*handle["ins"], *handle["lands"])
    outs = pl.pallas_call(
        body, name=name,
        out_shape=tuple(pltpu.HBM(a.shape, a.dtype) for a in thru),
        in_specs=[HBM_SPEC] * (2 * n) + [SEM_SPEC, SEM_SPEC, SEM_SPEC, ANY],
        out_specs=tuple([HBM_SPEC] * (2 * n)),
        input_output_aliases={k: k for k in range(2 * n)},
        compiler_params=pltpu.CompilerParams(has_side_effects=EFFECT),
    )(*thru, handle["send"], handle["recv"], handle["own"], after)
    return list(outs[n:])


FAR_MASKS = (2, 4, 6)
PHASE1_MASKS = (1,) + FAR_MASKS


def gather_start(arrays, name, deps=()):
    n = len(arrays)
    n1 = len(PHASE1_MASKS)
    lands0 = [lax.empty(_landing_shape("gather", a), a.dtype) for a in arrays]

    def body(*refs):
        in_refs, land_refs = refs[:n], refs[n:2 * n]
        outs_at = 2 * n + len(deps)
        send_sems, recv_sems, own_sems, token = refs[outs_at], refs[outs_at + 1], refs[outs_at + 2], refs[-1]
        mine = _my_index()
        for k in range(n):
            _own_copy("gather", in_refs[k], land_refs[k], mine, own_sems.at[k]).start()
            for j, mask in enumerate(PHASE1_MASKS):
                peer, _ = _peer(mask)
                pltpu.make_async_remote_copy(
                    src_ref=in_refs[k], dst_ref=land_refs[k].at[:, mine],
                    send_sem=send_sems.at[k * n1 + j], recv_sem=recv_sems.at[k * n1 + j],
                    device_id=peer, device_id_type=MESH).start()
        token[...] = jnp.zeros_like(token)

    outs = pl.pallas_call(
        body, name=name,
        out_shape=(pltpu.SemaphoreType.DMA((n * n1,)), pltpu.SemaphoreType.DMA((n * n1,)),
                   pltpu.SemaphoreType.DMA((n,)),
                   *[pltpu.HBM(a.shape, a.dtype) for a in arrays],
                   *[pltpu.HBM(l.shape, l.dtype) for l in lands0], _sds((8, LANES), F32)),
        in_specs=[HBM_SPEC] * (2 * n) + [ANY] * len(deps),
        out_specs=(SEM_SPEC, SEM_SPEC, SEM_SPEC, *[HBM_SPEC] * (2 * n),
                   pl.BlockSpec(memory_space=pltpu.VMEM)),
        input_output_aliases={k: 3 + k for k in range(2 * n)},
        compiler_params=pltpu.CompilerParams(has_side_effects=EFFECT),
    )(*[pltpu.with_memory_space_constraint(a, pltpu.HBM) for a in arrays],
      *[pltpu.with_memory_space_constraint(l, pltpu.HBM) for l in lands0], *deps)
    return dict(n=n, send=outs[0], recv=outs[1], own=outs[2], ins=outs[3:3 + n],
                lands=outs[3 + n:3 + 2 * n], token=outs[-1], name=name)


def gather_relay(handle, after):
    n = handle["n"]
    n1, n2 = len(PHASE1_MASKS), len(FAR_MASKS)

    def body(*refs):
        in_refs, land_refs = refs[:n], refs[n:2 * n]
        send1, recv1 = refs[2 * n], refs[2 * n + 1]
        send2, recv2, token = refs[2 * n + 3], refs[2 * n + 4], refs[-1]
        token[...] = jnp.zeros_like(token)
        sibling, _ = _peer(1)
        for k in range(n):
            for j, mask in enumerate(FAR_MASKS):
                peer, pidx = _peer(mask)
                landed = land_refs[k].at[:, pidx]
                pltpu.make_async_remote_copy(
                    src_ref=in_refs[k], dst_ref=landed, send_sem=send1.at[k * n1 + 1 + j],
                    recv_sem=recv1.at[k * n1 + 1 + j], device_id=peer, device_id_type=MESH).wait_recv()
                pltpu.make_async_remote_copy(
                    src_ref=landed, dst_ref=landed, send_sem=send2.at[k * n2 + j],
                    recv_sem=recv2.at[k * n2 + j], device_id=sibling, device_id_type=MESH).start()

    thru = (*handle["ins"], *handle["lands"])
    outs = pl.pallas_call(
        body, name=handle["name"] + "_relay",
        out_shape=(pltpu.SemaphoreType.DMA((n * n2,)), pltpu.SemaphoreType.DMA((n * n2,)),
                   *[pltpu.HBM(a.shape, a.dtype) for a in thru], _sds((8, LANES), F32)),
        in_specs=[HBM_SPEC] * (2 * n) + [SEM_SPEC, SEM_SPEC, ANY],
        out_specs=(SEM_SPEC, SEM_SPEC, *[HBM_SPEC] * (2 * n), pl.BlockSpec(memory_space=pltpu.VMEM)),
        input_output_aliases={k: 2 + k for k in range(2 * n)},
        compiler_params=pltpu.CompilerParams(has_side_effects=EFFECT),
    )(*thru, handle["send"], handle["recv"], after)
    handle.update(send2=outs[0], recv2=outs[1], ins=outs[2:2 + n], lands=outs[2 + n:2 + 2 * n],
                  relay_token=outs[-1])


def gather_finish(handle, after):
    n = handle["n"]
    n1, n2 = len(PHASE1_MASKS), len(FAR_MASKS)

    def body(*refs):
        in_refs, land_refs = refs[:n], refs[n:2 * n]
        send1, recv1, own_sems, send2, recv2 = refs[2 * n:2 * n + 5]
        mine = _my_index()
        sibling, sib_idx = _peer(1)
        for k in range(n):
            _own_copy("gather", in_refs[k], land_refs[k], mine, own_sems.at[k]).wait()
            for j, mask in enumerate(PHASE1_MASKS):
                peer, pidx = _peer(mask)
                cp = pltpu.make_async_remote_copy(
                    src_ref=in_refs[k], dst_ref=land_refs[k].at[:, pidx], send_sem=send1.at[k * n1 + j],
                    recv_sem=recv1.at[k * n1 + j], device_id=peer, device_id_type=MESH)
                cp.wait_send()
                if mask == 1:
                    cp.wait_recv()
            for j, mask in enumerate(FAR_MASKS):
                _, pidx = _peer(mask)
                _, far_of_sibling = _peer(mask ^ 1)
                cp = pltpu.make_async_remote_copy(
                    src_ref=land_refs[k].at[:, pidx], dst_ref=land_refs[k].at[:, far_of_sibling],
                    send_sem=send2.at[k * n2 + j], recv_sem=recv2.at[k * n2 + j],
                    device_id=sibling, device_id_type=MESH)
                cp.wait_send()
                cp.wait_recv()

    thru = (*handle["ins"], *handle["lands"])
    outs = pl.pallas_call(
        body, name=handle["name"] + "_finish",
        out_shape=tuple(pltpu.HBM(a.shape, a.dtype) for a in thru),
        in_specs=[HBM_SPEC] * (2 * n) + [SEM_SPEC] * 5 + [ANY],
        out_specs=tuple([HBM_SPEC] * (2 * n)),
        input_output_aliases={k: k for k in range(2 * n)},
        compiler_params=pltpu.CompilerParams(has_side_effects=EFFECT),
    )(*thru, handle["send"], handle["recv"], handle["own"], handle["send2"], handle["recv2"], after)
    return list(outs[n:])


def all_reduce_small(packed, name, deps=()):
    rows = packed.shape[0]
    nd = len(deps)

    def body(in_ref, *rest):
        out_ref, slots, send_sems, recv_sems = rest[nd:]
        me = _my_index()
        sends = []
        for mask in range(1, N_DEV):
            peer, _ = _peer(mask)
            cp = pltpu.make_async_remote_copy(
                src_ref=in_ref, dst_ref=slots.at[me],
                send_sem=send_sems.at[mask - 1], recv_sem=recv_sems.at[mask - 1],
                device_id=peer, device_id_type=MESH)
            cp.start()
            sends.append(cp)
        slots[me] = in_ref[...]
        for mask in range(1, N_DEV):
            peer, pidx = _peer(mask)
            pltpu.make_async_remote_copy(
                src_ref=in_ref, dst_ref=slots.at[pidx],
                send_sem=send_sems.at[mask - 1], recv_sem=recv_sems.at[mask - 1],
                device_id=peer, device_id_type=MESH).wait_recv()
        for cp in sends:
            cp.wait_send()
        total = slots[0]
        for p in range(1, N_DEV):
            total = total + slots[p]
        out_ref[...] = total

    return pl.pallas_call(
        body, name=name,
        out_shape=_sds((rows, LANES), F32),
        in_specs=[pl.BlockSpec(memory_space=pltpu.VMEM)] + [ANY] * nd,
        out_specs=pl.BlockSpec(memory_space=pltpu.VMEM),
        scratch_shapes=[pltpu.VMEM((N_DEV, rows, LANES), F32),
                        pltpu.SemaphoreType.DMA((N_DEV - 1,)),
                        pltpu.SemaphoreType.DMA((N_DEV - 1,))],
    )(packed, *deps)


def matmul(dims, a, b, out_sds, grid, a_spec, b_spec, o_spec, acc_shape, *, name, alpha=1.0,
           bias=None, bias_spec=None, scale=None, scale_spec=None, res=None, res_spec=None,
           colsum_sds=None, colsum_spec=None, out_t_sds=None, out_t_spec=None, deps=()):
    nk = grid[2]
    has_bias, has_scale, has_res = bias is not None, scale is not None, res is not None
    has_cs, has_t = colsum_sds is not None, out_t_sds is not None
    if has_cs:
        assert grid[0] == 1 and dims == TN

    def body(*refs):
        a_ref, b_ref = refs[0], refs[1]
        pos = 2
        bias_ref = scale_ref = res_ref = cs_ref = ot_ref = None
        if has_bias:
            bias_ref = refs[pos]; pos += 1
        if has_scale:
            scale_ref = refs[pos]; pos += 1
        if has_res:
            res_ref = refs[pos]; pos += 1
        pos += len(deps)
        o_ref = refs[pos]; pos += 1
        if has_cs:
            cs_ref = refs[pos]; pos += 1
        if has_t:
            ot_ref = refs[pos]; pos += 1
        k = pl.program_id(2)
        bval = b_ref[...]
        part = lax.dot_general(a_ref[...].astype(BF16), bval.astype(BF16), dims,
                               preferred_element_type=F32)

        def finish(total):
            r = total * alpha if alpha != 1.0 else total
            if has_bias:
                r = r + bias_ref[...]
            if has_scale:
                r = r * scale_ref[...]
            if has_res:
                r = r + res_ref[...].astype(F32)
            o_ref[...] = r.astype(o_ref.dtype)
            if has_t:
                ot_ref[...] = r.T.astype(ot_ref.dtype)

        if has_cs:
            csum = jnp.sum(bval.astype(F32), axis=0, keepdims=True)

            @pl.when(k == 0)
            def _():
                cs_ref[...] = csum

            @pl.when(k > 0)
            def _():
                cs_ref[...] += csum

        if nk == 1:
            finish(part)
        else:
            acc_ref = refs[pos]

            @pl.when(k == 0)
            def _():
                acc_ref[...] = part

            @pl.when(k > 0)
            def _():
                acc_ref[...] += part

            @pl.when(k == nk - 1)
            def _():
                finish(acc_ref[...])

    in_specs, args = [a_spec, b_spec], [a, b]
    if has_bias:
        in_specs.append(bias_spec); args.append(bias)
    if has_scale:
        in_specs.append(scale_spec); args.append(scale)
    if has_res:
        in_specs.append(res_spec); args.append(res)
    in_specs += [ANY] * len(deps)
    args += list(deps)
    out_shape, out_specs = [out_sds], [o_spec]
    if has_cs:
        out_shape.append(colsum_sds); out_specs.append(colsum_spec)
    if has_t:
        out_shape.append(out_t_sds); out_specs.append(out_t_spec)
    scratch = [] if nk == 1 else [pltpu.VMEM(acc_shape, F32)]
    outs = pl.pallas_call(
        body, name=name, grid=grid, in_specs=in_specs, out_specs=out_specs, out_shape=out_shape,
        scratch_shapes=scratch, compiler_params=_cparams(3))(*args)
    return outs if (has_cs or has_t) else outs[0]


def _sigmoid(z):
    return 1.0 / (1.0 + jnp.exp(-z))


def _log_sigmoid(z):
    return jnp.minimum(z, 0.0) - jnp.log(1.0 + jnp.exp(-jnp.abs(z)))


def rmsnorm_fwd(x, gain, tm, name, deps=()):
    t, d = x.shape

    def body(x_ref, g_ref, *rest):
        o_ref = rest[-1]
        xf = x_ref[...]
        r = lax.rsqrt(jnp.mean(xf * xf, axis=-1, keepdims=True) + RMS_EPS)
        o_ref[...] = (xf * r * g_ref[...]).astype(o_ref.dtype)

    return pl.pallas_call(
        body, name=name, grid=(t // tm,),
        in_specs=[pl.BlockSpec((tm, d), lambda i: (i, 0)), pl.BlockSpec((1, d), lambda i: (0, 0))]
        + [ANY] * len(deps),
        out_specs=pl.BlockSpec((tm, d), lambda i: (i, 0)),
        out_shape=_sds((t, d), BF16), compiler_params=_cparams(1))(x, gain, *deps)


def loss_head(x, gain, target, tm, name):
    t, d = x.shape

    def body(x_ref, g_ref, tgt_ref, dx_ref, dg_ref, loss_ref):
        i = pl.program_id(0)
        xf = x_ref[...]
        g = g_ref[...]
        r = lax.rsqrt(jnp.mean(xf * xf, axis=-1, keepdims=True) + RMS_EPS)
        xhat = xf * r
        err = xhat * g - tgt_ref[...]
        part = 0.5 * jnp.sum(jnp.mean(err * err, axis=-1, keepdims=True))
        dy = err * (1.0 / d)
        dxhat = dy * g
        dx_ref[...] = r * (dxhat - xhat * jnp.mean(dxhat * xhat, axis=-1, keepdims=True))
        dg = jnp.sum(dy * xhat, axis=0, keepdims=True)
        lpart = jnp.full((8, LANES), part, F32)

        @pl.when(i == 0)
        def _():
            dg_ref[...] = dg
            loss_ref[...] = lpart

        @pl.when(i > 0)
        def _():
            dg_ref[...] += dg
            loss_ref[...] += lpart

    row = pl.BlockSpec((tm, d), lambda i: (i, 0))
    vec = pl.BlockSpec((1, d), lambda i: (0, 0))
    return pl.pallas_call(
        body, name=name, grid=(t // tm,), in_specs=[row, vec, row],
        out_specs=[row, vec, pl.BlockSpec((8, LANES), lambda i: (0, 0))],
        out_shape=[_sds((t, d), F32), _sds((1, d), F32), _sds((8, LANES), F32)],
        compiler_params=_cparams(1))(x, gain, target)


def ffn_in_swiglu(hn, wa, s, tm, name):
    t = hn.shape[0]
    halves = 2 if tm % 512 == 0 else 1
    rows = tm // halves

    def body(h_ref, wg_ref, wu_ref, gu_ref, act_ref):
        for c in range(halves):
            rs = slice(c * rows, (c + 1) * rows)
            h = h_ref[rs, :]
            g = jnp.dot(h, wg_ref[...], preferred_element_type=F32)
            u = jnp.dot(h, wu_ref[...], preferred_element_type=F32)
            gu_ref[0, rs, :] = g.astype(gu_ref.dtype)
            gu_ref[1, rs, :] = u.astype(gu_ref.dtype)
            act_ref[rs, :] = (g * _sigmoid(g) * u).astype(act_ref.dtype)

    return pl.pallas_call(
        body, name=name, grid=(t // tm, 4),
        in_specs=[pl.BlockSpec((tm, D_MODEL), lambda i, j: (i, 0)),
                  pl.BlockSpec((None, None, D_MODEL, FF_BLK), lambda i, j: (s, j, 0, 0)),
                  pl.BlockSpec((None, None, D_MODEL, FF_BLK), lambda i, j: (s, j + 4, 0, 0))],
        out_specs=[pl.BlockSpec((None, 2, tm, FF_BLK), lambda i, j: (j, 0, i, 0)),
                   pl.BlockSpec((None, tm, FF_BLK), lambda i, j: (j, i, 0))],
        out_shape=[_sds((4, 2, t, FF_BLK), BF16), _sds((4, t, FF_BLK), BF16)],
        compiler_params=_cparams(2))(hn, wa, wa)


def ffn_dact_swiglu(dy, wb, gu, s, tm, name):
    t = dy.shape[0]

    def body(dy_ref, w_ref, gu_ref, o_ref):
        da = 0.5 * lax.dot_general(dy_ref[...].astype(BF16), w_ref[...], NT, preferred_element_type=F32)
        g = gu_ref[0].astype(F32)
        u = gu_ref[1].astype(F32)
        sg = _sigmoid(g)
        o_ref[0] = (da * u * (sg * (1.0 + g * (1.0 - sg)))).astype(o_ref.dtype)
        o_ref[1] = (da * g * sg).astype(o_ref.dtype)

    blk = pl.BlockSpec((None, 2, tm, FF_BLK), lambda i, j: (j, 0, i, 0))
    return pl.pallas_call(
        body, name=name, grid=(t // tm, 4),
        in_specs=[pl.BlockSpec((tm, D_MODEL), lambda i, j: (i, 0)),
                  pl.BlockSpec((None, None, FF_BLK, D_MODEL), lambda i, j: (s, j, 0, 0)), blk],
        out_specs=blk, out_shape=_sds((4, 2, t, FF_BLK), BF16),
        compiler_params=_cparams(2))(dy, wb, gu)


def ffn_out_residual(act, wb, x, s, tm, name, deps=()):
    t = x.shape[0]

    def body(a_ref, w_ref, x_ref, *rest):
        o_ref = rest[-1]
        acc = jnp.dot(a_ref[0], w_ref[0], preferred_element_type=F32)
        for k in range(1, 4):
            acc = acc + jnp.dot(a_ref[k], w_ref[k], preferred_element_type=F32)
        o_ref[...] = x_ref[...] + 0.5 * acc

    row = pl.BlockSpec((tm, D_MODEL), lambda i: (i, 0))
    return pl.pallas_call(
        body, name=name, grid=(t // tm,),
        in_specs=[pl.BlockSpec((4, tm, FF_BLK), lambda i: (0, i, 0)),
                  pl.BlockSpec((None, 4, FF_BLK, D_MODEL), lambda i: (s, 0, 0, 0)), row] + [ANY] * len(deps),
        out_specs=row, out_shape=_sds((t, D_MODEL), F32), compiler_params=_cparams(1))(act, wb, x, *deps)


def ffn_dh_norm_bwd(dgu, wa, s, x, gain, dres, tm, name, deps):
    t = dgu.shape[2]
    nd = len(deps)

    def body(g_ref, w_ref, x_ref, gain_ref, dres_ref, *rest):
        dx_ref, dg_ref = rest[nd:]
        i = pl.program_id(0)
        dh = lax.dot_general(g_ref[0, 0], w_ref[0], NT, preferred_element_type=F32)
        for p in range(1, N_DEV):
            dh = dh + lax.dot_general(g_ref[p % 4, p // 4], w_ref[p], NT, preferred_element_type=F32)
        xf = x_ref[...]
        r = lax.rsqrt(jnp.mean(xf * xf, axis=-1, keepdims=True) + RMS_EPS)
        xhat = xf * r
        dxhat = dh * gain_ref[...]
        dx_ref[...] = dres_ref[...] + r * (dxhat - xhat * jnp.mean(dxhat * xhat, axis=-1, keepdims=True))
        dg = jnp.sum(dh * xhat, axis=0, keepdims=True)

        @pl.when(i == 0)
        def _():
            dg_ref[...] = dg

        @pl.when(i > 0)
        def _():
            dg_ref[...] += dg

    row = pl.BlockSpec((tm, D_MODEL), lambda i: (i, 0))
    vec = pl.BlockSpec((1, D_MODEL), lambda i: (0, 0))
    return pl.pallas_call(
        body, name=name, grid=(t // tm,),
        in_specs=[pl.BlockSpec((4, 2, tm, FF_BLK), lambda i: (0, 0, i, 0)),
                  pl.BlockSpec((None, N_DEV, D_MODEL, FF_BLK), lambda i: (s, 0, 0, 0)), row, vec, row]
        + [ANY] * nd,
        out_specs=[row, vec], out_shape=[_sds((t, D_MODEL), F32), _sds((1, D_MODEL), F32)],
        compiler_params=_cparams(1))(dgu, wa, x, gain, dres, *deps)


def branch_merge(os_, wbr, layer, gates, tm, name, deps=()):
    t = os_[0].shape[0]
    d = D_MODEL
    nd = len(deps)
    widths = [o.shape[1] for o in os_]
    starts = [sum(widths[:a]) for a in range(3)]

    def body(oa_ref, ob_ref, oc_ref, w_ref, g_ref, *rest):
        ya_ref, yb_ref, yc_ref, m_ref = rest[nd:]
        merged = None
        for a, (o_ref, y_ref) in enumerate(((oa_ref, ya_ref), (ob_ref, yb_ref), (oc_ref, yc_ref))):
            y = jnp.dot(o_ref[...], w_ref[starts[a]:starts[a] + widths[a], :], preferred_element_type=F32)
            y_ref[...] = y
            term = _sigmoid(g_ref[:, a * d:(a + 1) * d]) * y
            merged = term if merged is None else merged + term
        m_ref[...] = merged.astype(m_ref.dtype)

    row = pl.BlockSpec((tm, d), lambda i: (i, 0))
    ya, yb, yc, merged = pl.pallas_call(
        body, name=name, grid=(t // tm,),
        in_specs=[pl.BlockSpec((tm, w), lambda i: (i, 0)) for w in widths]
        + [pl.BlockSpec((None, d, d), lambda i: (layer, 0, 0)), pl.BlockSpec((tm, 3 * d), lambda i: (i, 0))]
        + [ANY] * nd,
        out_specs=[row, row, row, row],
        out_shape=[_sds((t, d), F32)] * 3 + [_sds((t, d), BF16)],
        compiler_params=_cparams(1))(*os_, wbr, gates, *deps)
    return [ya, yb, yc], merged


def dmerged_merge_bwd(dy, wout, layer, gates, ys, tm, name):
    t, d = dy.shape

    def body(dy_ref, w_ref, g_ref, ya_ref, yb_ref, yc_ref, dg_ref, dya_ref, dyb_ref, dyc_ref):
        dmv = lax.dot_general(dy_ref[...].astype(BF16), w_ref[...], NT, preferred_element_type=F32)
        for a, (y_ref, dy_out) in enumerate(((ya_ref, dya_ref), (yb_ref, dyb_ref), (yc_ref, dyc_ref))):
            cols = slice(a * d, (a + 1) * d)
            s = _sigmoid(g_ref[:, cols])
            dy_out[...] = (dmv * s).astype(dy_out.dtype)
            dg_ref[:, cols] = (dmv * y_ref[...] * s * (1.0 - s)).astype(dg_ref.dtype)

    row = pl.BlockSpec((tm, d), lambda i: (i, 0))
    wide = pl.BlockSpec((tm, 3 * d), lambda i: (i, 0))
    dg, dya, dyb, dyc = pl.pallas_call(
        body, name=name, grid=(t // tm,),
        in_specs=[row, pl.BlockSpec((None, d, d), lambda i: (layer, 0, 0)), wide, row, row, row],
        out_specs=[wide, row, row, row],
        out_shape=[_sds((t, 3 * d), BF16)] + [_sds((t, d), BF16)] * 3,
        compiler_params=_cparams(1))(dy, wout, gates, *ys)
    return dg, [dya, dyb, dyc]


def branch_bwd(dys, os_, wbr, layer, tm, name):
    t = dys[0].shape[0]
    d = D_MODEL
    nt = t // tm
    widths = [o.shape[1] for o in os_]
    starts = [sum(widths[:a]) for a in range(3)]

    def body(dya_ref, dyb_ref, dyc_ref, oa_ref, ob_ref, oc_ref, w_ref, do_ref, dot_ref, dw_ref, acc_ref):
        i = pl.program_id(0)
        for a, (dy_ref, o_ref) in enumerate(((dya_ref, oa_ref), (dyb_ref, ob_ref), (dyc_ref, oc_ref))):
            rows = slice(starts[a], starts[a] + widths[a])
            dyv = dy_ref[...]
            do = lax.dot_general(dyv, w_ref[rows, :], NT, preferred_element_type=F32)
            do_ref[:, rows] = do.astype(do_ref.dtype)
            dot_ref[rows, :] = do.T.astype(dot_ref.dtype)
            dw = lax.dot_general(o_ref[...], dyv, TN, preferred_element_type=F32)

            @pl.when(i == 0)
            def _():
                acc_ref[rows, :] = dw

            @pl.when(i > 0)
            def _():
                acc_ref[rows, :] += dw

        @pl.when(i == nt - 1)
        def _():
            dw_ref[...] = acc_ref[...].astype(dw_ref.dtype)

    row = pl.BlockSpec((tm, d), lambda i: (i, 0))
    return pl.pallas_call(
        body, name=name, grid=(nt,),
        in_specs=[row, row, row] + [pl.BlockSpec((tm, w), lambda i: (i, 0)) for w in widths]
        + [pl.BlockSpec((None, d, d), lambda i: (layer, 0, 0))],
        out_specs=[row, pl.BlockSpec((d, tm), lambda i: (0, i)), pl.BlockSpec((d, d), lambda i: (0, 0))],
        out_shape=[_sds((t, d), BF16), _sds((d, t), BF16), _sds((d, d), BF16)],
        scratch_shapes=[pltpu.VMEM((d, d), F32)], compiler_params=_cparams(1))(*dys, *os_, wbr)


def mixer_dh_norm_bwd(dh_part, dgates, wc, gate_idx, df, wf, layer, x, gain, dres, tm, name):
    t, d = x.shape

    def body(dhp_ref, dg_ref, wg_ref, df_ref, wf_ref, x_ref, gain_ref, dres_ref, dx_ref, dgain_ref):
        i = pl.program_id(0)
        dh = (dhp_ref[...]
              + lax.dot_general(dg_ref[...], wg_ref[...], NT, preferred_element_type=F32)
              + lax.dot_general(df_ref[...].astype(BF16), wf_ref[...], NT, preferred_element_type=F32))
        xf = x_ref[...]
        r = lax.rsqrt(jnp.mean(xf * xf, axis=-1, keepdims=True) + RMS_EPS)
        xhat = xf * r
        dxhat = dh * gain_ref[...]
        dx_ref[...] = dres_ref[...] + r * (dxhat - xhat * jnp.mean(dxhat * xhat, axis=-1, keepdims=True))
        dg = jnp.sum(dh * xhat, axis=0, keepdims=True)

        @pl.when(i == 0)
        def _():
            dgain_ref[...] = dg

        @pl.when(i > 0)
        def _():
            dgain_ref[...] += dg

    row = pl.BlockSpec((tm, d), lambda i: (i, 0))
    vec = pl.BlockSpec((1, d), lambda i: (0, 0))
    return pl.pallas_call(
        body, name=name, grid=(t // tm,),
        in_specs=[row, pl.BlockSpec((tm, QKV_WIDTH), lambda i: (i, 0)),
                  pl.BlockSpec((None, d, QKV_WIDTH), lambda i: (gate_idx, 0, 0)),
                  pl.BlockSpec((tm, LANES), lambda i: (i, 0)),
                  pl.BlockSpec((None, d, LANES), lambda i: (layer, 0, 0)), row, vec, row],
        out_specs=[row, vec], out_shape=[_sds((t, d), F32), _sds((1, d), F32)],
        compiler_params=_cparams(1))(dh_part, dgates, wc, df, wf, x, gain, dres)


def _iota2(shape, dim):
    return lax.broadcasted_iota(jnp.int32, shape, dim)


def proj_gates_forget(hm, wc, gate_idx, wf, layer, bg, bf, tm, name, deps=()):
    t, d = hm.shape
    sub = tm // QB
    nd = len(deps)

    def body(h_ref, wg_ref, bg_ref, wf_ref, bf_ref, *rest):
        g_ref, f_ref, fcol_ref, frow_ref, carry = rest[nd:]
        i, j = pl.program_id(0), pl.program_id(1)
        h = h_ref[...]
        g_ref[...] = jnp.dot(h, wg_ref[...], preferred_element_type=F32) + bg_ref[...]

        @pl.when((i == 0) & (j == 0))
        def _():
            carry[...] = jnp.zeros_like(carry)

        @pl.when(j == 0)
        def _():
            f = jnp.dot(h, wf_ref[...], preferred_element_type=F32) + bf_ref[...]
            f_ref[...] = f
            logf = _log_sigmoid(f)
            tri = (_iota2((QB, QB), 1) <= _iota2((QB, QB), 0)).astype(F32)
            for s in range(sub):
                rows = slice(s * QB, (s + 1) * QB)
                part = logf[rows, :]
                blk = jnp.dot(tri, part, precision=HIGHEST, preferred_element_type=F32) + carry[...]
                carry[...] += jnp.sum(part, axis=0, keepdims=True)
                fcol_ref[rows, :] = blk
                frow_ref[s] = blk.T[0:8, :]

    narrow = pl.BlockSpec((tm, LANES), lambda i, j: (i, 0))
    return pl.pallas_call(
        body, name=name, grid=(t // tm, 3),
        in_specs=[pl.BlockSpec((tm, d), lambda i, j: (i, 0)),
                  pl.BlockSpec((None, d, d), lambda i, j: (gate_idx, 0, j)),
                  pl.BlockSpec((1, d), lambda i, j: (0, j)),
                  pl.BlockSpec((None, d, LANES), lambda i, j: (layer, 0, 0)),
                  pl.BlockSpec((1, LANES), lambda i, j: (0, 0))] + [ANY] * nd,
        out_specs=[pl.BlockSpec((tm, d), lambda i, j: (i, j)), narrow, narrow,
                   pl.BlockSpec((sub, 8, QB), lambda i, j: (i, 0, 0))],
        out_shape=[_sds((t, 3 * d), F32), _sds((t, LANES), F32), _sds((t, LANES), F32),
                   _sds((t // QB, 8, QB), F32)],
        scratch_shapes=[pltpu.VMEM((1, LANES), F32)], compiler_params=_cparams(2))(hm, wc, bg, wf, bf, *deps)


def forget_cumsum_bwd(dfrow, f, name):
    t = f.shape[0]
    nq = t // QB

    def body(dfr_ref, f_ref, df_ref, carry):
        jj = pl.program_id(0)

        @pl.when(jj == 0)
        def _():
            carry[...] = jnp.zeros_like(carry)

        padded = jnp.concatenate([dfr_ref[...], jnp.zeros((QB - 8, QB), F32)], axis=0)
        dfcol = padded.T
        tri = (_iota2((QB, QB), 1) >= _iota2((QB, QB), 0)).astype(F32)
        dlogf = jnp.dot(tri, dfcol, precision=HIGHEST, preferred_element_type=F32) + carry[...]
        carry[...] += jnp.sum(dfcol, axis=0, keepdims=True)
        df_ref[...] = dlogf * _sigmoid(-f_ref[...])

    return pl.pallas_call(
        body, name=name, grid=(nq,),
        in_specs=[pl.BlockSpec((None, 8, QB), lambda jj: (nq - 1 - jj, 0, 0)),
                  pl.BlockSpec((QB, LANES), lambda jj: (nq - 1 - jj, 0))],
        out_specs=pl.BlockSpec((QB, LANES), lambda jj: (nq - 1 - jj, 0)),
        out_shape=_sds((t, LANES), F32),
        scratch_shapes=[pltpu.VMEM((1, LANES), F32)], compiler_params=_cparams(1))(dfrow, f)


REL_DIAG = 768
REL_SHIFT = REL_DIAG - (QB - 1)


def _diag_onehot():
    u = _iota2((REL_PAD, REL_DIAG), 1)
    rel = jnp.clip(CH_KEYS - 1 - u, -MAX_REL, MAX_REL) + MAX_REL
    return (_iota2((REL_PAD, REL_DIAG), 0) == rel).astype(F32)


def rel_bias_build(tab_t, name):
    def body(tab_ref, o_ref):
        diag = jnp.dot(tab_ref[...], _diag_onehot(), precision=HIGHEST, preferred_element_type=F32)
        band = _chunk_band()
        for h in range(N_HEADS_CH):
            rows = jnp.broadcast_to(diag[h:h + 1, :], (QB, REL_DIAG))
            o_ref[h] = pltpu.roll(rows, REL_SHIFT, 1, stride=1, stride_axis=0)[:, :CH_KEYS] + band

    return pl.pallas_call(
        body, name=name, out_shape=_sds((N_HEADS_CH, QB, CH_KEYS), F32),
        in_specs=[pl.BlockSpec(memory_space=pltpu.VMEM)], out_specs=pl.BlockSpec(memory_space=pltpu.VMEM),
    )(tab_t)


def rel_bias_scatter(dbias, name):
    def body(db_ref, o_ref, ddiag):
        flip = (_iota2((QB, QB), 0) + _iota2((QB, QB), 1) == QB - 1).astype(F32)
        for h in range(N_HEADS_CH):
            padded = jnp.concatenate([db_ref[h], jnp.zeros((QB, REL_DIAG - CH_KEYS), F32)], axis=1)
            flipped = jnp.dot(flip, padded, precision=HIGHEST, preferred_element_type=F32)
            unrolled = pltpu.roll(flipped, 0, 1, stride=1, stride_axis=0)
            ddiag[h:h + 1, :] = jnp.sum(unrolled, axis=0, keepdims=True)
        o_ref[...] = lax.dot_general(ddiag[...], _diag_onehot(), NT, precision=HIGHEST,
                                     preferred_element_type=F32)

    return pl.pallas_call(
        body, name=name, out_shape=_sds((N_HEADS_CH, REL_PAD), F32),
        in_specs=[pl.BlockSpec(memory_space=pltpu.VMEM)], out_specs=pl.BlockSpec(memory_space=pltpu.VMEM),
        scratch_shapes=[pltpu.VMEM((N_HEADS_CH, REL_DIAG), F32)],
    )(dbias)


def _hl(h):
    return slice(h * HEAD_DIM, (h + 1) * HEAD_DIM)


def _split_dot(x, tri_bf16):
    hi = x.astype(BF16)
    lo = (x - hi.astype(F32)).astype(BF16)
    return (jnp.dot(hi, tri_bf16, preferred_element_type=F32)
            + jnp.dot(lo, tri_bf16, preferred_element_type=F32))


def _krows(g):
    return pl.ds(pl.multiple_of(g * KB, KB), KB)


def _log_sigmoid_pair(z):
    sp = jnp.log(1.0 + jnp.exp(-jnp.abs(z)))
    return jnp.minimum(z, 0.0) - sp, -jnp.maximum(z, 0.0) - sp


def _qkv_specs(t, col0, n_pairs):
    q_spec = pl.BlockSpec((QB, LANES), lambda hp, i: (i, col0 + hp))
    k_spec = pl.BlockSpec((t, LANES), lambda hp, i: (0, col0 + n_pairs + hp))
    v_spec = pl.BlockSpec((t, LANES), lambda hp, i: (0, col0 + 2 * n_pairs + hp))
    return q_spec, k_spec, v_spec


def _keys_major(xt):
    pairs, groups, _, _ = xt.shape
    return xt.transpose(1, 3, 0, 2).reshape(groups * KB, pairs * LANES)


def sb_fwd(qkv, name):
    t = qkv.shape[0]
    nq = t // QB

    def body(q_ref, k_ref, v_ref, o_ref, w_ref):
        i = pl.program_id(1)
        groups = i // KSUB + 1
        tri_after = (_iota2((KB, KB), 0) > _iota2((KB, KB), 1)).astype(BF16)
        t_idx = i * QB + _iota2((QB, KB), 0)
        qs = [q_ref[:, _hl(h)] for h in range(2)]

        def step(g, carry, masked):
            strict = (g * KB + _iota2((QB, KB), 1)) < t_idx
            out = []
            for h in range(2):
                tail, acc = carry[2 * h], carry[2 * h + 1]
                k = k_ref[_krows(g), _hl(h)]
                v = v_ref[_krows(g), _hl(h)]
                z = lax.dot_general(qs[h], k, NT, preferred_element_type=F32)
                lb, lf = _log_sigmoid_pair(z)
                if masked:
                    lf = jnp.where(strict, lf, 0.0)
                between = _split_dot(lf, tri_after) + tail
                w = jnp.exp(lb + between)
                if masked:
                    w = jnp.where(strict, w, 0.0)
                w = w.astype(BF16)
                w_ref[h, g] = w
                acc = acc + jnp.dot(w, v, preferred_element_type=F32)
                out += [tail + jnp.sum(lf, axis=1, keepdims=True), acc]
            return tuple(out)

        init = (jnp.zeros((QB, 1), F32), jnp.zeros((QB, HEAD_DIM), F32)) * 2
        res = step(groups - 1, init, True)
        res = lax.fori_loop(0, groups - 1, lambda gg, c: step(groups - 2 - gg, c, False), res)
        for h in range(2):
            o_ref[:, _hl(h)] = res[2 * h + 1].astype(o_ref.dtype)

    q_spec, k_spec, v_spec = _qkv_specs(t, 0, 2)
    return pl.pallas_call(
        body, name=name, grid=(2, nq), in_specs=[q_spec, k_spec, v_spec],
        out_specs=[pl.BlockSpec((QB, LANES), lambda hp, i: (i, hp)),
                   pl.BlockSpec((2, None, t // KB, QB, KB), lambda hp, i: (hp, i, 0, 0, 0))],
        out_shape=[_sds((t, W_SB), BF16), _sds((4, nq, t // KB, QB, KB), BF16)],
        compiler_params=_cparams(2))(qkv, qkv, qkv)


def _hs(h):
    return slice(h * HEAD_DIM, (h + 1) * HEAD_DIM)


def sb_bwd(qkv, qkv_t, w, do, do_t, name):
    t = qkv.shape[0]
    nq = t // QB

    def body(q_ref, k_ref, v_ref, do_ref, qt_ref, dot_ref, w_ref, dq_ref, dkt_ref, dvt_ref):
        i = pl.program_id(1)

        @pl.when(i == 0)
        def _():
            dkt_ref[...] = jnp.zeros_like(dkt_ref)
            dvt_ref[...] = jnp.zeros_like(dvt_ref)

        groups = i // KSUB + 1
        tri_before = (_iota2((KB, KB), 0) < _iota2((KB, KB), 1)).astype(BF16)
        t_idx = i * QB + _iota2((QB, KB), 0)
        qs = [q_ref[:, _hl(h)] for h in range(2)]
        dos = [do_ref[:, _hl(h)] for h in range(2)]
        qts = [qt_ref[_hs(h), :] for h in range(2)]
        dots = [dot_ref[_hs(h), :] for h in range(2)]

        def grads(g, carry, masked):
            strict = (g * KB + _iota2((QB, KB), 1)) < t_idx
            out = []
            for h in range(2):
                head, dq = carry[2 * h], carry[2 * h + 1]
                k = k_ref[_krows(g), _hl(h)]
                v = v_ref[_krows(g), _hl(h)]
                wb = w_ref[h, g]
                z = lax.dot_general(qs[h], k, NT, preferred_element_type=F32)
                beta = _sigmoid(z)
                e = lax.dot_general(dos[h], v, NT, preferred_element_type=F32) * wb.astype(F32)
                before = _split_dot(e, tri_before) + head
                dz = e * (1.0 - beta) - before * beta
                if masked:
                    dz = jnp.where(strict, dz, 0.0)
                dzb = dz.astype(BF16)
                dq = dq + jnp.dot(dzb, k, preferred_element_type=F32)
                dkt_ref[g, _hs(h), :] += jnp.dot(qts[h], dzb, preferred_element_type=F32)
                dvt_ref[g, _hs(h), :] += jnp.dot(dots[h], wb, preferred_element_type=F32)
                out += [head + jnp.sum(e, axis=1, keepdims=True), dq]
            return tuple(out)

        init = (jnp.zeros((QB, 1), F32), jnp.zeros((QB, HEAD_DIM), F32)) * 2
        res = lax.fori_loop(0, groups - 1, lambda g, c: grads(g, c, False), init)
        res = grads(groups - 1, res, True)
        for h in range(2):
            dq_ref[:, _hl(h)] = (res[2 * h + 1] * SCALE).astype(dq_ref.dtype)

    q_spec, k_spec, v_spec = _qkv_specs(t, 0, 2)
    blk = pl.BlockSpec((QB, LANES), lambda hp, i: (i, hp))
    blk_t = pl.BlockSpec((LANES, QB), lambda hp, i: (hp, i))
    acc_t = pl.BlockSpec((None, t // KB, LANES, KB), lambda hp, i: (hp, 0, 0, 0))
    acc_sds = _sds((2, t // KB, LANES, KB), F32)
    return pl.pallas_call(
        body, name=name, grid=(2, nq),
        in_specs=[q_spec, k_spec, v_spec, blk, blk_t, blk_t,
                  pl.BlockSpec((2, None, t // KB, QB, KB), lambda hp, i: (hp, i, 0, 0, 0))],
        out_specs=[blk, acc_t, acc_t],
        out_shape=[_sds((t, W_SB), BF16), acc_sds, acc_sds],
        compiler_params=_cparams(2))(qkv, qkv, qkv, do, qkv_t, do_t, w)


def fox_fwd(qkv, fcol, frow, name):
    t = qkv.shape[0]
    nq = t // QB

    def body(q_ref, k_ref, v_ref, fc_ref, fr_ref, o_ref, lse_ref):
        hp = pl.program_id(0)
        i = pl.program_id(1)
        groups = i // KSUB + 1
        t_idx = i * QB + _iota2((QB, KB), 0)
        lane = _iota2((QB, LANES), 1)
        sub = _iota2((8, KB), 0)
        qs = [q_ref[:, _hl(h)] for h in range(2)]
        f_qs = [jnp.sum(jnp.where(lane == hp * 2 + h, fc_ref[...], 0.0), axis=1, keepdims=True)
                for h in range(2)]

        def step(g, carry, masked):
            causal = (g * KB + _iota2((QB, KB), 1)) <= t_idx
            fr = fr_ref[g]
            out = []
            for h in range(2):
                m, l, acc = carry[3 * h:3 * h + 3]
                k = k_ref[_krows(g), _hl(h)]
                v = v_ref[_krows(g), _hl(h)]
                f_k = jnp.sum(jnp.where(sub == hp * 2 + h, fr, 0.0), axis=0, keepdims=True)
                z = lax.dot_general(qs[h], k, NT, preferred_element_type=F32) + f_qs[h] - f_k
                if masked:
                    z = jnp.where(causal, z, NEG)
                m_new = jnp.maximum(m, jnp.max(z, axis=1, keepdims=True))
                p = jnp.exp(z - m_new)
                corr = jnp.exp(m - m_new)
                l = l * corr + jnp.sum(p, axis=1, keepdims=True)
                acc = acc * corr + jnp.dot(p.astype(BF16), v, preferred_element_type=F32)
                out += [m_new, l, acc]
            return tuple(out)

        init = (jnp.full((QB, 1), NEG, F32), jnp.zeros((QB, 1), F32), jnp.zeros((QB, HEAD_DIM), F32)) * 2
        res = lax.fori_loop(0, groups - 1, lambda g, c: step(g, c, False), init)
        res = step(groups - 1, res, True)
        for h in range(2):
            m, l, acc = res[3 * h:3 * h + 3]
            o_ref[:, _hl(h)] = (acc / l).astype(o_ref.dtype)
            lse_ref[:, _hl(h)] = jnp.broadcast_to(m + jnp.log(l), (QB, HEAD_DIM))

    q_spec, k_spec, v_spec = _qkv_specs(t, 18, 2)
    blk = pl.BlockSpec((QB, LANES), lambda hp, i: (i, hp))
    return pl.pallas_call(
        body, name=name, grid=(2, nq),
        in_specs=[q_spec, k_spec, v_spec, pl.BlockSpec((QB, LANES), lambda hp, i: (i, 0)),
                  pl.BlockSpec((t // KB, 8, KB), lambda hp, i: (0, 0, 0))],
        out_specs=[blk, blk],
        out_shape=[_sds((t, W_FOX), BF16), _sds((t, W_FOX), F32)],
        compiler_params=_cparams(2))(qkv, qkv, qkv, fcol, frow)


def fox_bwd(qkv, qkv_t, fcol, frow, o, lse, do, do_t, name, do_col=0):
    t = qkv.shape[0]
    nq = t // QB

    def body(q_ref, k_ref, v_ref, fc_ref, fr_ref, o_ref, lse_ref, do_ref, qt_ref, dot_ref,
             dq_ref, dk_ref, dv_ref, dfr_ref):
        hp = pl.program_id(0)
        i = pl.program_id(1)
        qts = [qt_ref[_hs(h), :] for h in range(2)]
        dots = [dot_ref[_hs(h), :] for h in range(2)]

        @pl.when(i == 0)
        def _():
            dk_ref[...] = jnp.zeros_like(dk_ref)
            dv_ref[...] = jnp.zeros_like(dv_ref)

        @pl.when((i == 0) & (hp == 0))
        def _():
            dfr_ref[...] = jnp.zeros_like(dfr_ref)

        groups = i // KSUB + 1
        t_idx = i * QB + _iota2((QB, KB), 0)
        lane = _iota2((QB, LANES), 1)
        sub = _iota2((8, KB), 0)
        qs = [q_ref[:, _hl(h)] for h in range(2)]
        dos = [do_ref[:, _hl(h)] for h in range(2)]
        f_qs = [jnp.sum(jnp.where(lane == hp * 2 + h, fc_ref[...], 0.0), axis=1, keepdims=True)
                for h in range(2)]
        lse_qs = [lse_ref[:, h * HEAD_DIM:h * HEAD_DIM + 1] for h in range(2)]
        deltas = [jnp.sum(dos[h].astype(F32) * o_ref[:, _hl(h)].astype(F32), axis=1, keepdims=True)
                  for h in range(2)]

        def step(g, dqs, masked):
            causal = (g * KB + _iota2((QB, KB), 1)) <= t_idx
            fr = fr_ref[g]
            out = []
            dfr = jnp.zeros((8, KB), F32)
            for h in range(2):
                k = k_ref[_krows(g), _hl(h)]
                v = v_ref[_krows(g), _hl(h)]
                f_k = jnp.sum(jnp.where(sub == hp * 2 + h, fr, 0.0), axis=0, keepdims=True)
                z = lax.dot_general(qs[h], k, NT, preferred_element_type=F32) + f_qs[h] - f_k
                p = jnp.exp(z - lse_qs[h])
                if masked:
                    p = jnp.where(causal, p, 0.0)
                dp = lax.dot_general(dos[h], v, NT, preferred_element_type=F32)
                ds = p * (dp - deltas[h])
                dsb = ds.astype(BF16)
                out.append(dqs[h] + jnp.dot(dsb, k, preferred_element_type=F32))
                dk_ref[g, _hs(h), :] += jnp.dot(qts[h], dsb, preferred_element_type=F32)
                dv_ref[g, _hs(h), :] += jnp.dot(dots[h], p.astype(BF16), preferred_element_type=F32)
                colsum = jnp.sum(ds, axis=0, keepdims=True)
                dfr = dfr + jnp.where(sub == hp * 2 + h, -colsum, 0.0)
            dfr_ref[g] += dfr
            return tuple(out)

        res = lax.fori_loop(0, groups - 1, lambda g, c: step(g, c, False),
                            (jnp.zeros((QB, HEAD_DIM), F32),) * 2)
        res = step(groups - 1, res, True)
        for h in range(2):
            dq_ref[:, _hl(h)] = (res[h] * SCALE).astype(dq_ref.dtype)

    q_spec, k_spec, v_spec = _qkv_specs(t, 18, 2)
    blk = pl.BlockSpec((QB, LANES), lambda hp, i: (i, hp))
    frs = pl.BlockSpec((t // KB, 8, KB), lambda hp, i: (0, 0, 0))
    acc_t = pl.BlockSpec((None, t // KB, LANES, KB), lambda hp, i: (hp, 0, 0, 0))
    acc_sds = _sds((2, t // KB, LANES, KB), F32)
    return pl.pallas_call(
        body, name=name, grid=(2, nq),
        in_specs=[q_spec, k_spec, v_spec, pl.BlockSpec((QB, LANES), lambda hp, i: (i, 0)), frs,
                  blk, blk, pl.BlockSpec((QB, LANES), lambda hp, i: (i, do_col + hp)),
                  pl.BlockSpec((LANES, QB), lambda hp, i: (18 + hp, i)),
                  pl.BlockSpec((LANES, QB), lambda hp, i: (do_col + hp, i))],
        out_specs=[blk, acc_t, acc_t, frs],
        out_shape=[_sds((t, W_FOX), BF16), acc_sds, acc_sds, _sds((t // KB, 8, KB), F32)],
        compiler_params=_cparams(2))(qkv, qkv, qkv, fcol, frow, o, lse, do, qkv_t, do_t)


def _frow_to_groups(frow):
    n = frow.shape[0] // KSUB
    return frow.reshape(n, KSUB, 8, QB).transpose(0, 2, 1, 3).reshape(n, 8, KB)


def _frow_from_groups(frow):
    n = frow.shape[0]
    return frow.reshape(n, 8, KSUB, QB).transpose(0, 2, 1, 3).reshape(n * KSUB, 8, QB)


def _chunk_band():
    qi = _iota2((QB, CH_KEYS), 0)
    kj = _iota2((QB, CH_KEYS), 1)
    dchunk = (qi >> 6) + LEFT_CHUNKS - (kj >> 6)
    return jnp.where((dchunk >= 0) & (dchunk <= LEFT_CHUNKS), 0.0, NEG)


def _chunk_pad_row(i):
    kj = _iota2((1, CH_KEYS), 1)
    return jnp.where((i - (CH_WIN - 1)) * QB + kj >= 0, 0.0, NEG)


CH_PAD = (CH_WIN - 1) * QB
CH_STEP_HEADS = 4
CH_COLS = CH_STEP_HEADS * HEAD_DIM


def _window(i):
    return pl.ds(pl.multiple_of(i * QB, QB), CH_KEYS)


def _chunk_weights(q, kw, bias, pad_row):
    z = lax.dot_general(q, kw, NT, preferred_element_type=F32) + bias + pad_row
    e = jnp.exp(z - jnp.max(z, axis=1, keepdims=True))
    return e, 1.0 / jnp.sum(e, axis=1, keepdims=True)


def _chunk_specs(t):
    q_spec = pl.BlockSpec((QB, CH_COLS), lambda hp, i: (i, 3 * W_SB // CH_COLS + hp))
    kv_spec = pl.BlockSpec((t + CH_PAD, CH_COLS), lambda hp, i: (0, hp))
    return q_spec, kv_spec


def chunk_fwd(qkv, kp, vp, bias, name):
    t = qkv.shape[0]
    nq = t // QB

    def body(q_ref, k_ref, v_ref, b_ref, o_ref):
        i = pl.program_id(1)
        pad_row = _chunk_pad_row(i)
        for h in range(CH_STEP_HEADS):
            e, inv = _chunk_weights(q_ref[:, _hl(h)], k_ref[_window(i), _hl(h)], b_ref[h], pad_row)
            o = jnp.dot(e.astype(BF16), v_ref[_window(i), _hl(h)], preferred_element_type=F32)
            o_ref[:, _hl(h)] = (o * inv).astype(o_ref.dtype)

    q_spec, kv_spec = _chunk_specs(t)
    return pl.pallas_call(
        body, name=name, grid=(W_CH // CH_COLS, nq),
        in_specs=[q_spec, kv_spec, kv_spec,
                  pl.BlockSpec((CH_STEP_HEADS, QB, CH_KEYS), lambda hp, i: (hp, 0, 0))],
        out_specs=pl.BlockSpec((QB, CH_COLS), lambda hp, i: (i, hp)),
        out_shape=_sds((t, W_CH), BF16), compiler_params=_cparams(2))(qkv, kp, vp, bias)


def chunk_bwd(qkv, qkv_t, kp, vp, bias, do, do_t, name, do_col=0):
    t = qkv.shape[0]
    nq = t // QB

    def body(q_ref, k_ref, v_ref, b_ref, do_ref, qt_ref, dot_ref, dq_ref, dk_ref, dv_ref, db_ref):
        i = pl.program_id(1)

        @pl.when(i == 0)
        def _():
            dk_ref[...] = jnp.zeros_like(dk_ref)
            dv_ref[...] = jnp.zeros_like(dv_ref)
            db_ref[...] = jnp.zeros_like(db_ref)

        pad_row = _chunk_pad_row(i)
        for h in range(CH_STEP_HEADS):
            q = q_ref[:, _hl(h)]
            dov = do_ref[:, _hl(h)]
            kw = k_ref[_window(i), _hl(h)]
            e, inv = _chunk_weights(q, kw, b_ref[h], pad_row)
            p = e * inv
            dp = lax.dot_general(dov, v_ref[_window(i), _hl(h)], NT, preferred_element_type=F32)
            ds = p * (dp - jnp.sum(p * dp, axis=1, keepdims=True))
            db_ref[h] += ds
            dsb = ds.astype(BF16)
            dq_ref[:, _hl(h)] = (jnp.dot(dsb, kw, preferred_element_type=F32) * SCALE).astype(dq_ref.dtype)
            dkt = jnp.dot(qt_ref[_hs(h), :], dsb, preferred_element_type=F32)
            dvt = jnp.dot(dot_ref[_hs(h), :], p.astype(BF16), preferred_element_type=F32)
            for b in range(CH_WIN):
                dk_ref[i + b, _hs(h), :] += dkt[:, b * QB:(b + 1) * QB]
                dv_ref[i + b, _hs(h), :] += dvt[:, b * QB:(b + 1) * QB]

    q_spec, kv_spec = _chunk_specs(t)
    blk = pl.BlockSpec((QB, CH_COLS), lambda hp, i: (i, hp))
    bspec = pl.BlockSpec((CH_STEP_HEADS, QB, CH_KEYS), lambda hp, i: (hp, 0, 0))
    nblk = nq + CH_WIN - 1
    acc_t = pl.BlockSpec((None, nblk, CH_COLS, QB), lambda hp, i: (hp, 0, 0, 0))
    acc_sds = _sds((W_CH // CH_COLS, nblk, CH_COLS, QB), F32)
    return pl.pallas_call(
        body, name=name, grid=(W_CH // CH_COLS, nq),
        in_specs=[q_spec, kv_spec, kv_spec, bspec,
                  pl.BlockSpec((QB, CH_COLS), lambda hp, i: (i, do_col + hp)),
                  pl.BlockSpec((CH_COLS, QB), lambda hp, i: (3 * W_SB // CH_COLS + hp, i)),
                  pl.BlockSpec((CH_COLS, QB), lambda hp, i: (do_col + hp, i))],
        out_specs=[blk, acc_t, acc_t, bspec],
        out_shape=[_sds((t, W_CH), BF16), acc_sds, acc_sds, _sds((N_HEADS_CH, QB, CH_KEYS), F32)],
        compiler_params=_cparams(2))(qkv, kp, vp, bias, do, qkv_t, do_t)


def _sum_parts(p_ref):
    total = p_ref[0].astype(F32)
    for p in range(1, p_ref.shape[0]):
        total = total + p_ref[p].astype(F32)
    return total


def sum_parts_multi(parts_list, name):
    n = len(parts_list)

    def body(*refs):
        for p_ref, o_ref in zip(refs[:n], refs[n:]):
            o_ref[...] = _sum_parts(p_ref)

    shapes = [p.shape[2:] for p in parts_list]
    return pl.pallas_call(
        body, name=name, grid=(1,),
        in_specs=[pl.BlockSpec((N_DEV, None, r, c), lambda s: (0, 0, 0, 0)) for r, c in shapes],
        out_specs=[pl.BlockSpec((r, c), lambda s: (0, 0)) for r, c in shapes],
        out_shape=[_sds((r, c), F32) for r, c in shapes], compiler_params=_cparams(1))(*parts_list)


def adamw(parts, w, m, v, grid, p_specs, w_spec, name):
    c1 = 1.0 / (1.0 - ADAM_B1 ** ADAM_STEP)
    c2 = 1.0 / (1.0 - ADAM_B2 ** ADAM_STEP)
    n = len(parts)

    def body(*refs):
        w_ref, m_ref, v_ref, g_out, d_out, m_out, v_out = refs[n:]
        g = _sum_parts(refs[0])
        for q in range(1, n):
            g = jnp.where(pl.program_id(0) == q, _sum_parts(refs[q]), g)
        m_new = ADAM_B1 * m_ref[...] + (1.0 - ADAM_B1) * g
        v_new = ADAM_B2 * v_ref[...] + (1.0 - ADAM_B2) * (g * g)
        m_hat = m_new * c1
        v_hat = v_new * c2
        g_out[...] = g
        d_out[...] = -ADAM_LR * (m_hat / (jnp.sqrt(v_hat) + ADAM_EPS) + ADAM_WD * w_ref[...])
        m_out[...] = m_new
        v_out[...] = v_new

    out = _sds(w.shape, F32)
    return pl.pallas_call(
        body, name=name, grid=grid, in_specs=[*p_specs, w_spec, w_spec, w_spec],
        out_specs=[w_spec] * 4, out_shape=[out] * 4,
        compiler_params=_cparams(len(grid)))(*parts, w, m, v)


def _ffn_fwd(x, gain, wa, wb_after, s, tm, tag, on_event, deps=()):
    t = x.shape[0]
    hn = rmsnorm_fwd(x, gain, tm, f"rms_{tag}", deps)
    gu, act = ffn_in_swiglu(hn, wa, s, min(2 * tm, t), f"ffn_in_{tag}")
    relayed = on_event("act", act)
    wb = wb_after(act)
    y = ffn_out_residual(act, wb, x, s, min(2 * tm, t), f"ffn_out_{tag}", relayed)
    return y, (hn, gu, act), wb


def _ffn_bwd(dy, x, gain, saved, wa, wb, s, tm, tag, on_grads):
    t = x.shape[0]
    hn, gu, act = saved
    dgu = ffn_dact_swiglu(dy, wb, gu, s, min(2 * tm, t), f"ffn_dact_{tag}")
    dwb = matmul(TN, act, dy, _sds((4, FF_BLK, D_MODEL), BF16), (4, 1, 1),
                 pl.BlockSpec((None, t, FF_BLK), lambda i, j, k: (i, 0, 0)),
                 pl.BlockSpec((t, D_MODEL), lambda i, j, k: (0, 0)),
                 pl.BlockSpec((None, FF_BLK, D_MODEL), lambda i, j, k: (i, 0, 0)),
                 None, name=f"ffn_dwout_{tag}", alpha=0.5)
    dwa = matmul(TN, dgu, hn, _sds((8, FF_BLK, D_MODEL), BF16), (1, 8, 1),
                 pl.BlockSpec((None, None, t, FF_BLK), lambda i, j, k: (j % 4, j // 4, 0, 0)),
                 pl.BlockSpec((t, D_MODEL), lambda i, j, k: (0, 0)),
                 pl.BlockSpec((None, FF_BLK, D_MODEL), lambda i, j, k: (j, 0, 0)),
                 None, name=f"ffn_dwin_{tag}")
    deps = on_grads(dwa, dwb)
    return ffn_dh_norm_bwd(dgu, wa, s, x, gain, dy, tm, f"ffn_dh_{tag}", deps)


_Q_COLUMN_SCALE = np.ones((1, QKV_WIDTH), np.float32)
for _lo, _width in ((0, W_SB), (3 * W_SB, W_CH), (3 * (W_SB + W_CH), W_FOX)):
    _Q_COLUMN_SCALE[0, _lo:_lo + _width] = SCALE


def _mixer_fwd(x, gain, wqkv, wf, wgate, late_after, bq, bf, bg, bias, layer, tm, tag, on_event):
    t = x.shape[0]
    nt = t // tm
    hm = rmsnorm_fwd(x, gain, tm, f"rms_{tag}")
    a_full = pl.BlockSpec((tm, D_MODEL), lambda i, j, k: (i, 0))
    wide_out = pl.BlockSpec((tm, D_MODEL), lambda i, j, k: (i, j))
    wide_b = pl.BlockSpec((1, D_MODEL), lambda i, j, k: (0, j))
    qkv, qkv_t = matmul(NN, hm, wqkv, _sds((t, QKV_WIDTH), BF16), (nt, 3, 1), a_full,
                        pl.BlockSpec((None, D_MODEL, D_MODEL), lambda i, j, k: (layer, 0, j)), wide_out, None,
                        name=f"proj_qkv_{tag}", bias=bq, bias_spec=wide_b,
                        scale=jnp.asarray(_Q_COLUMN_SCALE), scale_spec=wide_b,
                        out_t_sds=_sds((QKV_WIDTH, t), BF16),
                        out_t_spec=pl.BlockSpec((D_MODEL, tm), lambda i, j, k: (j, i)))
    relayed = on_event("qkv", qkv)
    gates, f, fcol, frow = proj_gates_forget(hm, wgate, layer + 1, wf, layer, bg, bf, tm,
                                             f"proj_gate_{tag}", relayed)
    frow = _frow_to_groups(frow)
    o_sb, w_sb = sb_fwd(qkv, f"sb_fwd_{tag}")
    relayed = on_event("o_sb", o_sb)
    kp = jnp.pad(qkv[:, 10 * LANES:14 * LANES], ((CH_PAD, 0), (0, 0)))
    vp = jnp.pad(qkv[:, 14 * LANES:18 * LANES], ((CH_PAD, 0), (0, 0)))
    o_ch = chunk_fwd(qkv, kp, vp, bias, f"chunk_fwd_{tag}")
    o_fox, lse = fox_fwd(qkv, fcol, frow, f"fox_fwd_{tag}")
    wbr, wout = late_after(o_fox)
    ys, merged = branch_merge((o_sb, o_ch, o_fox), wbr, layer, gates, tm, f"branch_merge_{tag}", relayed)
    x_new = matmul(NN, merged, wout, _sds((t, D_MODEL), F32), (nt, 1, 1), a_full,
                   pl.BlockSpec((None, D_MODEL, D_MODEL), lambda i, j, k: (layer, 0, 0)), a_full, None,
                   name=f"wout_{tag}", res=x, res_spec=a_full)
    saved = (hm, qkv, gates, f, fcol, frow, o_sb, o_ch, o_fox, lse, ys, merged, kp, vp, w_sb, qkv_t)
    return x_new, saved, wbr, wout


def _mixer_bwd(dy, x, gain, saved, wqkv, wf, wgate, wbr, wout, bias, layer, tm, tag, on_grads):
    t = x.shape[0]
    nt = t // tm
    hm, qkv, gates, f, fcol, frow, o_sb, o_ch, o_fox, lse, ys, merged, kp, vp, w_sb, qkv_t = saved
    a_full = pl.BlockSpec((tm, D_MODEL), lambda i, j, k: (i, 0))
    red_row = pl.BlockSpec((tm, D_MODEL), lambda i, j, k: (k, 0))
    sq = pl.BlockSpec((D_MODEL, D_MODEL), lambda i, j, k: (0, 0))
    dgates, dys = dmerged_merge_bwd(dy, wout, layer, gates, ys, tm // 2, f"dmerged_{tag}")
    all_t = pl.BlockSpec((t, D_MODEL), lambda i, j, k: (0, 0))
    dwout = matmul(TN, merged, dy, _sds((D_MODEL, D_MODEL), BF16), (1, 1, 1), all_t, all_t, sq,
                   None, name=f"dwout_{tag}")
    do, do_t, dwbr = branch_bwd(dys, (o_sb, o_ch, o_fox), wbr, layer, tm, f"dbranch_{tag}")
    dq_a, dk_a, dv_a = sb_bwd(qkv, qkv_t, w_sb, do, do_t, f"sb_bwd_{tag}")
    dk_a, dv_a = _keys_major(dk_a), _keys_major(dv_a)
    dq_b, dk_b, dv_b, dbias = chunk_bwd(qkv, qkv_t, kp, vp, bias, do, do_t, f"chunk_bwd_{tag}",
                                        do_col=W_SB // CH_COLS)
    dk_b, dv_b = [x[:, CH_WIN - 1:].transpose(1, 3, 0, 2).reshape(t, W_CH) for x in (dk_b, dv_b)]
    dq_c, dk_c, dv_c, dfrow = fox_bwd(qkv, qkv_t, fcol, frow, o_fox, lse, do, do_t, f"fox_bwd_{tag}",
                                      do_col=(W_SB + W_CH) // LANES)
    dk_c, dv_c = _keys_major(dk_c), _keys_major(dv_c)
    df = forget_cumsum_bwd(_frow_from_groups(dfrow), f, f"fcum_bwd_{tag}")
    dqkv = jnp.concatenate([p.astype(BF16) for p in
                            (dq_a, dk_a, dv_a, dq_b, dk_b, dv_b, dq_c, dk_c, dv_c)], axis=1)
    dtab = rel_bias_scatter(dbias, f"rel_scatter_{tag}")

    all_rows = pl.BlockSpec((t, D_MODEL), lambda i, j, k: (0, 0))
    wide_b = pl.BlockSpec((t, D_MODEL), lambda i, j, k: (0, j))
    wide_o = pl.BlockSpec((D_MODEL, D_MODEL), lambda i, j, k: (0, j))
    wide_cs = pl.BlockSpec((1, D_MODEL), lambda i, j, k: (0, j))
    dwqkv, dbq = matmul(TN, hm, dqkv, _sds((D_MODEL, QKV_WIDTH), BF16), (1, 3, 1), all_rows, wide_b,
                        wide_o, None, name=f"dwqkv_{tag}",
                        colsum_sds=_sds((1, QKV_WIDTH), F32), colsum_spec=wide_cs)
    dwgate, dbg = matmul(TN, hm, dgates, _sds((D_MODEL, 3 * D_MODEL), BF16), (1, 3, 1), all_rows,
                         wide_b, wide_o, None, name=f"dwgate_{tag}",
                         colsum_sds=_sds((1, 3 * D_MODEL), F32), colsum_spec=wide_cs)
    dwf, dbf = matmul(TN, hm, df, _sds((D_MODEL, LANES), BF16), (1, 1, 1), all_rows,
                      pl.BlockSpec((t, LANES), lambda i, j, k: (0, 0)),
                      pl.BlockSpec((D_MODEL, LANES), lambda i, j, k: (0, 0)), None,
                      name=f"dwf_{tag}", colsum_sds=_sds((1, LANES), F32),
                      colsum_spec=pl.BlockSpec((1, LANES), lambda i, j, k: (0, 0)))
    deps = on_grads(dict(dwqkv=dwqkv, dwgate=dwgate, dwf=dwf, dwbr=dwbr, dwout=dwout))
    wide_a = pl.BlockSpec((tm, QKV_WIDTH), lambda i, j, k: (i, 0))
    dhm = matmul(NT, dqkv, wqkv, _sds((t, D_MODEL), F32), (nt, 1, 1), wide_a,
                 pl.BlockSpec((None, D_MODEL, QKV_WIDTH), lambda i, j, k: (layer, 0, 0)), a_full,
                 None, name=f"dhm_qkv_{tag}", deps=deps)
    dx, dgain = mixer_dh_norm_bwd(dhm, dgates, wgate, layer + 1, df, wf, layer, x, gain, dy, tm,
                                  f"dhm_gate_{tag}")
    return dx, dict(dbq=dbq, dbg=dbg, dbf=dbf, dtab=dtab, dgain=dgain)


def _pack_small(pieces):
    flat = jnp.concatenate([p.reshape(-1).astype(F32) for p in pieces])
    flat = jnp.pad(flat, (0, SMALL_ROWS * LANES - flat.shape[0]))
    return flat.reshape(SMALL_ROWS, LANES)


def _unpack_small(packed, shapes):
    flat = packed.reshape(-1)
    out, pos = [], 0
    for shp in shapes:
        n = int(np.prod(shp))
        out.append(flat[pos:pos + n].reshape(shp))
        pos += n
    return out


def kernel(x, g_ffn1, w_ffn1_in, w_ffn1_out, g_mix, w_in, b_in, rel_bias, w_br_sb, w_br_ch, w_br_fox, w_out, g_ffn2, w_ffn2_in, w_ffn2_out, g_final, loss_target, m_g_ffn1, m_w_ffn1_in, m_w_ffn1_out, m_g_mix, m_w_in, m_b_in, m_rel_bias, m_w_br_sb, m_w_br_ch, m_w_br_fox, m_w_out, m_g_ffn2, m_w_ffn2_in, m_w_ffn2_out, m_g_final, v_g_ffn1, v_w_ffn1_in, v_w_ffn1_out, v_g_mix, v_w_in, v_b_in, v_rel_bias, v_w_br_sb, v_w_br_ch, v_w_br_fox, v_w_out, v_g_ffn2, v_w_ffn2_in, v_w_ffn2_out, v_g_final):
    t = x.shape[1]
    tm = min(512, t)
    xs = x[0]
    target = loss_target[0]
    f_lo, f_hi = QKV_WIDTH, QKV_WIDTH + N_HEADS_FOX

    def ffn_shards(w_in_, w_out_, l):
        return [w_in_[l:l + 1].astype(BF16), w_out_[l:l + 1].astype(BF16)]

    def mixer_shards(l):
        wl = w_in[l]
        return [jnp.stack([wl[:, :QKV_WIDTH], wl[:, f_hi:]]).astype(BF16),
                jnp.pad(wl[:, f_lo:f_hi], ((0, 0), (0, LANES - N_HEADS_FOX)))[None].astype(BF16),
                w_out[l:l + 1].astype(BF16),
                jnp.concatenate([w_br_sb[l], w_br_ch[l], w_br_fox[l]], axis=0)[None].astype(BF16)]

    gathers = {}
    gather_tokens = []

    def start_gather(shards, name):
        handle = gather_start(shards, name, deps=gather_tokens[-1:])
        gather_tokens.append(handle["token"])
        return handle

    def relay(handle, after):
        if "send2" not in handle:
            gather_relay(handle, after)

    relay_on = {("mix", 0, "qkv"): ("mix", 0, 1), ("mix", 0, "o_sb"): ("ffn2", 0, 0),
                ("ffn2", 0, "act"): ("ffn1", 1, 0), ("ffn1", 1, "act"): ("mix", 1, 0),
                ("mix", 1, "qkv"): ("ffn2", 1, 0)}

    def on_event(grp, l):
        def fire(event, array):
            target = relay_on.get((grp, l, event))
            if target is None:
                return ()
            handle = gathers[target[:2]][target[2]]
            relay(handle, array)
            return (handle["relay_token"],)
        return fire

    for l in range(DEPTH):
        for grp, shards in (("ffn1", ffn_shards(w_ffn1_in, w_ffn1_out, l)), ("mix", mixer_shards(l)),
                            ("ffn2", ffn_shards(w_ffn2_in, w_ffn2_out, l))):
            cut = len(shards) // 2
            if l == 0 and grp != "ffn2":
                gathers[(grp, l)] = (start_gather(shards[:cut], f"gather_{grp}_l{l}_a"),
                                     start_gather(shards[cut:], f"gather_{grp}_l{l}_b"))
            else:
                gathers[(grp, l)] = (start_gather(shards, f"gather_{grp}_l{l}"),)

    def gathered(key, after):
        hs = gathers[key]
        cut = hs[0]["n"]
        relay(hs[0], after)
        first = gather_finish(hs[0], after)
        if len(hs) == 1:
            return first[:cut // 2], lambda later: first[cut // 2:]

        def second(later):
            relay(hs[1], later)
            return gather_finish(hs[1], later)

        return first, second

    def ffn_weights(key, after):
        (wa_,), rest = gathered(key, after)
        return wa_, lambda later: rest(later)[0].reshape(1, 4, FF_BLK, D_MODEL)

    def mixer_weights(key, after):
        (wc_, wf_), rest = gathered(key, after)

        def late(later):
            wout_, wbr_ = rest(later)
            return (wbr_.transpose(0, 2, 1, 3).reshape(1, D_MODEL, D_MODEL), wout_.reshape(1, D_MODEL, D_MODEL))

        return wc_.reshape(2, D_MODEL, QKV_WIDTH), wf_.reshape(1, D_MODEL, LANES), late

    bq = b_in[:, None, :QKV_WIDTH]
    bf = jnp.pad(b_in[:, f_lo:f_hi], ((0, 0), (0, LANES - N_HEADS_FOX)))[:, None, :]
    bg = b_in[:, None, f_hi:]
    tab_t = jnp.pad(rel_bias.transpose(0, 2, 1), ((0, 0), (0, 0), (0, REL_PAD - N_REL)))

    h = xs
    saved = []
    weights = []
    for l in range(DEPTH):
        bias = rel_bias_build(tab_t[l], f"rel_build_l{l}").reshape(N_HEADS_CH, QB, CH_KEYS)
        x0 = h
        wa1, wb1_after = ffn_weights(("ffn1", l), x0)
        x1, s1, wb1 = _ffn_fwd(x0, g_ffn1[l:l + 1], wa1, wb1_after, 0, tm, f"ffn1_l{l}", on_event("ffn1", l),
                               deps=gather_tokens if l == 0 else ())
        wc, wf, late_after = mixer_weights(("mix", l), x1)
        x2, sm, wbr, wout = _mixer_fwd(x1, g_mix[l:l + 1], wc, wf, wc, late_after, bq[l], bf[l], bg[l],
                                       bias, 0, tm, f"mix_l{l}", on_event("mix", l))
        wa2, wb2_after = ffn_weights(("ffn2", l), x2)
        x3, s2, wb2 = _ffn_fwd(x2, g_ffn2[l:l + 1], wa2, wb2_after, 0, tm, f"ffn2_l{l}", on_event("ffn2", l))
        saved.append((x0, x1, x2, s1, sm, s2, bias))
        weights.append(((wa1, wb1), (wc, wf, wout, wbr), (wa2, wb2)))
        h = x3

    dx, dg_final, loss_blk = loss_head(h, g_final[None, :], target, tm, "loss_head")

    g_mix_l = [None] * DEPTH
    dgains = {}
    scatters = {}

    def scatter_ffn(key):
        def on_grads(dwa, dwb):
            scatters[key] = exchange_start(
                "scatter", [dwa[None], dwb.reshape(1, N_DEV, D_FF // N_DEV, D_MODEL)],
                f"scatter_{key[0]}_l{key[1]}")
            return (scatters[key]["token"],)
        return on_grads

    def scatter_mixer(key):
        def on_grads(gm):
            scatters[key] = exchange_start(
                "scatter",
                [gm["dwqkv"].reshape(1, N_DEV, LANES, QKV_WIDTH), gm["dwgate"].reshape(1, N_DEV, LANES, QKV_WIDTH),
                 gm["dwf"].reshape(1, N_DEV, LANES, LANES), gm["dwout"].reshape(1, N_DEV, LANES, D_MODEL),
                 gm["dwbr"].reshape(1, D_MODEL, N_DEV, LANES).transpose(0, 2, 1, 3)],
                f"scatter_{key[0]}_l{key[1]}")
            return (scatters[key]["token"],)
        return on_grads

    for l in reversed(range(DEPTH)):
        x0, x1, x2, s1, sm, s2, bias = saved[l]
        w1, (wc, wf, wout, wbr), w2 = weights[l]
        dx, dgains[("ffn2", l)] = _ffn_bwd(dx, x2, g_ffn2[l:l + 1], s2, *w2, 0, tm, f"ffn2_l{l}",
                                           scatter_ffn(("ffn2", l)))
        dx, g_mix_l[l] = _mixer_bwd(dx, x1, g_mix[l:l + 1], sm, wc, wf, wc, wbr, wout, bias, 0, tm,
                                    f"mix_l{l}", scatter_mixer(("mix", l)))
        dx, dgains[("ffn1", l)] = _ffn_bwd(dx, x0, g_ffn1[l:l + 1], s1, *w1, 0, tm, f"ffn1_l{l}",
                                           scatter_ffn(("ffn1", l)))

    small_shapes = []
    small_pieces = []
    small_w, small_m, small_v = [], [], []

    def add_small(piece, w, m, v):
        small_shapes.append(w.shape)
        small_pieces.append(piece)
        small_w.append(w); small_m.append(m); small_v.append(v)

    dg1 = jnp.concatenate([dgains[("ffn1", l)] for l in range(DEPTH)], axis=0)
    dgm = jnp.concatenate([g_mix_l[l]["dgain"] for l in range(DEPTH)], axis=0)
    dg2 = jnp.concatenate([dgains[("ffn2", l)] for l in range(DEPTH)], axis=0)
    db = jnp.stack([jnp.concatenate([g_mix_l[l]["dbq"][0], g_mix_l[l]["dbf"][0, :N_HEADS_FOX],
                                     g_mix_l[l]["dbg"][0]]) for l in range(DEPTH)])
    drel = jnp.stack([g_mix_l[l]["dtab"][:, :N_REL].T for l in range(DEPTH)])
    add_small(dg1, g_ffn1, m_g_ffn1, v_g_ffn1)
    add_small(dgm, g_mix, m_g_mix, v_g_mix)
    add_small(db, b_in, m_b_in, v_b_in)
    add_small(drel, rel_bias, m_rel_bias, v_rel_bias)
    add_small(dg2, g_ffn2, m_g_ffn2, v_g_ffn2)
    add_small(dg_final[0], g_final, m_g_final, v_g_final)
    loss_piece = loss_blk[0, 0:1]
    small_packed = _pack_small(small_pieces + [loss_piece])

    recv = {}
    last = ("ffn1", 0)
    for l in reversed(range(DEPTH)):
        for grp in ("ffn2", "mix", "ffn1"):
            if (grp, l) != last:
                recv[(grp, l)] = exchange_wait(scatters[(grp, l)], dx, f"scattered_{grp}_l{l}")

    def upd(parts, w, m, v, tr, name, rb0=0):
        _, r, c = w.shape
        nr = r // tr

        def p_spec(layer):
            pinned = (nr - 1) if layer == 0 else 0
            return pl.BlockSpec((N_DEV, None, tr, c),
                                lambda l, i: (0, 0, rb0 + jnp.where(l == layer, i, pinned), 0))

        return adamw(parts, w, m, v, (DEPTH, nr), [p_spec(0), p_spec(1)],
                     pl.BlockSpec((None, tr, c), lambda l, i: (l, i, 0)), name)

    def both(grp, k):
        return [recv[(grp, l)][k] for l in range(DEPTH)]

    out_rows = D_FF // N_DEV // 2
    def upd_transposed(parts, w, m, v, tr, name):
        tp = lambda a: jnp.transpose(a, (0, 2, 1))
        return [tp(o) for o in upd(parts, tp(w), tp(m), tp(v), tr, name)]

    in_rows = FF_BLK // 4
    r_ffn2_in = upd_transposed(both("ffn2", 0), w_ffn2_in, m_w_ffn2_in, v_w_ffn2_in, in_rows, "adamw_ffn2_in")
    r_ffn2_out = upd(both("ffn2", 1), w_ffn2_out, m_w_ffn2_out, v_w_ffn2_out, out_rows, "adamw_ffn2_out")
    r_out = upd(both("mix", 3), w_out, m_w_out, v_w_out, LANES, "adamw_w_out")
    r_br_sb = upd(both("mix", 4), w_br_sb, m_w_br_sb, v_w_br_sb, 256, "adamw_br_sb", rb0=0)
    r_br_ch = upd(both("mix", 4), w_br_ch, m_w_br_ch, v_w_br_ch, 256, "adamw_br_ch", rb0=1)
    r_br_fox = upd(both("mix", 4), w_br_fox, m_w_br_fox, v_w_br_fox, 256, "adamw_br_fox", rb0=3)

    def w_in_grad(l):
        pieces = [recv[("mix", l)][k] for k in (0, 2, 1)]
        gq, gf, gg = sum_parts_multi(pieces, f"sum_w_in_l{l}")
        return jnp.concatenate([gq, gf[:, :N_HEADS_FOX], gg], axis=1)

    g_w_in = jnp.stack([w_in_grad(l) for l in range(DEPTH)])
    to_cols = lambda a: jnp.transpose(a, (2, 0, 1))
    n_cols = w_in.shape[2]
    col_blk = n_cols // 4
    win_spec = pl.BlockSpec((col_blk, DEPTH, LANES), lambda i: (i, 0, 0))
    r_in = adamw([to_cols(g_w_in)[None]], to_cols(w_in), to_cols(m_w_in), to_cols(v_w_in), (4,),
                 [pl.BlockSpec((1, col_blk, DEPTH, LANES), lambda i: (0, i, 0, 0))], win_spec, "adamw_w_in")
    r_in = [jnp.transpose(o, (1, 2, 0)) for o in r_in]

    recv[last] = exchange_wait(scatters[last], r_in[1], "scattered_ffn1_l0")
    r_ffn1_in = upd_transposed(both("ffn1", 0), w_ffn1_in, m_w_ffn1_in, v_w_ffn1_in, in_rows, "adamw_ffn1_in")
    r_ffn1_out = upd(both("ffn1", 1), w_ffn1_out, m_w_ffn1_out, v_w_ffn1_out, out_rows, "adamw_ffn1_out")

    small_sum = all_reduce_small(small_packed, "allreduce_small", deps=(r_ffn1_out[1],))
    n_small = sum(int(np.prod(s)) for s in small_shapes)
    loss = small_sum.reshape(-1)[n_small]
    sm_spec = pl.BlockSpec((SMALL_ROWS, LANES), lambda i: (0, 0))
    sm_out = adamw([small_sum[None]], _pack_small(small_w), _pack_small(small_m), _pack_small(small_v),
                   (1,), [pl.BlockSpec((1, SMALL_ROWS, LANES), lambda i: (0, 0, 0))], sm_spec, "adamw_small")
    sm_g, sm_d, sm_m, sm_v = [_unpack_small(o, small_shapes) for o in sm_out]

    def per_kind(k):
        small = (sm_g, sm_d, sm_m, sm_v)[k]
        return [small[0], r_ffn1_in[k], r_ffn1_out[k], small[1], r_in[k], small[2], small[3],
                r_br_sb[k], r_br_ch[k], r_br_fox[k], r_out[k], small[4], r_ffn2_in[k], r_ffn2_out[k],
                small[5]]

    return (loss, dx[None], *per_kind(0), *per_kind(1), *per_kind(2), *per_kind(3))
```

```python
import numpy as np
import jax
import jax.numpy as jnp
from jax import lax
from jax.experimental import pallas as pl
from jax.experimental.pallas import tpu as pltpu

F32 = jnp.float32
BF16 = jnp.bfloat16

N_DEV = 8
D_MODEL = 1024
DEPTH = 2
HEAD_DIM = 64
W_SB, W_CH, W_FOX = 256, 512, 256
QKV_WIDTH = 3 * (W_SB + W_CH + W_FOX)
N_HEADS_FOX = 4
N_HEADS_CH = 8
D_FF = 2816
FF_BLK = 2 * D_FF // N_DEV
CHUNK = 64
LEFT_CHUNKS = 8
MAX_REL = 128
N_REL = 2 * MAX_REL + 1
REL_PAD = 384
QB = 128
KB = 512
KSUB = KB // QB
CH_WIN = 5
CH_KEYS = CH_WIN * QB
RMS_EPS = 1e-6
NEG = -1e30
SCALE = HEAD_DIM ** -0.5
LANES = 128
VMEM_LIMIT = 56 * 1024 * 1024

ADAM_LR, ADAM_B1, ADAM_B2, ADAM_EPS, ADAM_WD, ADAM_STEP = 0.001, 0.9, 0.999, 1e-08, 0.01, 10

SMALL_ROWS = 192

MESH = pl.DeviceIdType.MESH
ANY = pl.BlockSpec(memory_space=pl.ANY)
HIGHEST = lax.Precision.HIGHEST

NN = (((1,), (0,)), ((), ()))
NT = (((1,), (1,)), ((), ()))
TN = (((0,), (0,)), ((), ()))


def _cparams(n_grid):
    return pltpu.CompilerParams(dimension_semantics=("arbitrary",) * n_grid,
                                vmem_limit_bytes=VMEM_LIMIT)


def _sds(shape, dtype):
    return jax.ShapeDtypeStruct(tuple(shape), dtype)


def _my_index():
    return 4 * lax.axis_index("x") + 2 * lax.axis_index("y") + lax.axis_index("c")


def _peer(mask):
    x, y, c = lax.axis_index("x"), lax.axis_index("y"), lax.axis_index("c")
    px = x ^ ((mask >> 2) & 1)
    py = y ^ ((mask >> 1) & 1)
    pc = c ^ (mask & 1)
    return (px, py, pc), 4 * px + 2 * py + pc


HBM_SPEC = pl.BlockSpec(memory_space=pltpu.HBM)
SEM_SPEC = pl.BlockSpec(memory_space=pltpu.SEMAPHORE)
EFFECT = pltpu.SideEffectType.DATAFLOW_SIDE_EFFECTING


def _exchange_refs(mode, in_ref, land_ref, me, pidx):
    if mode == "gather":
        return in_ref, land_ref.at[:, me], land_ref.at[:, pidx]
    return in_ref.at[:, pidx], land_ref.at[me], land_ref.at[pidx]


def _landing_shape(mode, a):
    if mode == "gather":
        s, r, c = a.shape
        return (s, N_DEV, r, c)
    s, _, r, c = a.shape
    return (N_DEV, s, r, c)


def _own_copy(mode, in_ref, land_ref, me, sem):
    if mode == "gather":
        return pltpu.make_async_copy(in_ref, land_ref.at[:, me], sem)
    return pltpu.make_async_copy(in_ref.at[:, me], land_ref.at[me], sem)


def exchange_start(mode, arrays, name, deps=()):
    n = len(arrays)
    lands0 = [lax.empty(_landing_shape(mode, a), a.dtype) for a in arrays]

    def body(*refs):
        in_refs, land_refs = refs[:n], refs[n:2 * n]
        outs_at = 2 * n + len(deps)
        send_sems, recv_sems, own_sems, token = refs[outs_at], refs[outs_at + 1], refs[outs_at + 2], refs[-1]
        mine = _my_index()
        for k in range(n):
            _own_copy(mode, in_refs[k], land_refs[k], mine, own_sems.at[k]).start()
            for mask in range(1, N_DEV):
                peer, pidx = _peer(mask)
                src, dst, _ = _exchange_refs(mode, in_refs[k], land_refs[k], mine, pidx)
                sem = k * (N_DEV - 1) + mask - 1
                pltpu.make_async_remote_copy(
                    src_ref=src, dst_ref=dst, send_sem=send_sems.at[sem], recv_sem=recv_sems.at[sem],
                    device_id=peer, device_id_type=MESH).start()
        token[...] = jnp.zeros_like(token)

    nsem = n * (N_DEV - 1)
    outs = pl.pallas_call(
        body, name=name,
        out_shape=(pltpu.SemaphoreType.DMA((nsem,)), pltpu.SemaphoreType.DMA((nsem,)),
                   pltpu.SemaphoreType.DMA((n,)),
                   *[pltpu.HBM(a.shape, a.dtype) for a in arrays],
                   *[pltpu.HBM(l.shape, l.dtype) for l in lands0], _sds((8, LANES), F32)),
        in_specs=[HBM_SPEC] * (2 * n) + [ANY] * len(deps),
        out_specs=(SEM_SPEC, SEM_SPEC, SEM_SPEC, *[HBM_SPEC] * (2 * n),
                   pl.BlockSpec(memory_space=pltpu.VMEM)),
        input_output_aliases={k: 3 + k for k in range(2 * n)},
        compiler_params=pltpu.CompilerParams(has_side_effects=EFFECT),
    )(*[pltpu.with_memory_space_constraint(a, pltpu.HBM) for a in arrays],
      *[pltpu.with_memory_space_constraint(l, pltpu.HBM) for l in lands0], *deps)
    return dict(mode=mode, n=n, send=outs[0], recv=outs[1], own=outs[2], ins=outs[3:3 + n],
                lands=outs[3 + n:3 + 2 * n], token=outs[-1])


def exchange_wait(handle, after, name):
    n, mode = handle["n"], handle["mode"]

    def body(*refs):
        in_refs, land_refs = refs[:n], refs[n:2 * n]
        send_sems, recv_sems, own_sems = refs[2 * n], refs[2 * n + 1], refs[2 * n + 2]
        mine = _my_index()
        for k in range(n):
            _own_copy(mode, in_refs[k], land_refs[k], mine, own_sems.at[k]).wait()
            for mask in range(1, N_DEV):
                peer, pidx = _peer(mask)
                src, _, here = _exchange_refs(mode, in_refs[k], land_refs[k], mine, pidx)
                sem = k * (N_DEV - 1) + mask - 1
                cp = pltpu.make_async_remote_copy(
                    src_ref=src, dst_ref=here, send_sem=send_sems.at[sem], recv_sem=recv_sems.at[sem],
                    device_id=peer, device_id_type=MESH)
                cp.wait_send()
                cp.wait_recv()

    thru = (*handle["ins"], *handle["lands"])
    outs = pl.pallas_call(
        body, name=name,
        out_shape=tuple(pltpu.HBM(a.shape, a.dtype) for a in thru),
        in_specs=[HBM_SPEC] * (2 * n) + [SEM_SPEC, SEM_SPEC, SEM_SPEC, ANY],
        out_specs=tuple([HBM_SPEC] * (2 * n)),
        input_output_aliases={k: k for k in range(2 * n)},
        compiler_params=pltpu.CompilerParams(has_side_effects=EFFECT),
    )(*thru, handle["send"], handle["recv"], handle["own"], after)
    return list(outs[n:])


FAR_MASKS = (2, 4, 6)
PHASE1_MASKS = (1,) + FAR_MASKS


def gather_start(arrays, name, deps=()):
    n = len(arrays)
    n1 = len(PHASE1_MASKS)
    lands0 = [lax.empty(_landing_shape("gather", a), a.dtype) for a in arrays]

    def body(*refs):
        in_refs, land_refs = refs[:n], refs[n:2 * n]
        outs_at = 2 * n + len(deps)
        send_sems, recv_sems, own_sems, token = refs[outs_at], refs[outs_at + 1], refs[outs_at + 2], refs[-1]
        mine = _my_index()
        for k in range(n):
            _own_copy("gather", in_refs[k], land_refs[k], mine, own_sems.at[k]).start()
            for j, mask in enumerate(PHASE1_MASKS):
                peer, _ = _peer(mask)
                pltpu.make_async_remote_copy(
                    src_ref=in_refs[k], dst_ref=land_refs[k].at[:, mine],
                    send_sem=send_sems.at[k * n1 + j], recv_sem=recv_sems.at[k * n1 + j],
                    device_id=peer, device_id_type=MESH).start()
        token[...] = jnp.zeros_like(token)

    outs = pl.pallas_call(
        body, name=name,
        out_shape=(pltpu.SemaphoreType.DMA((n * n1,)), pltpu.SemaphoreType.DMA((n * n1,)),
                   pltpu.SemaphoreType.DMA((n,)),
                   *[pltpu.HBM(a.shape, a.dtype) for a in arrays],
                   *[pltpu.HBM(l.shape, l.dtype) for l in lands0], _sds((8, LANES), F32)),
        in_specs=[HBM_SPEC] * (2 * n) + [ANY] * len(deps),
        out_specs=(SEM_SPEC, SEM_SPEC, SEM_SPEC, *[HBM_SPEC] * (2 * n),
                   pl.BlockSpec(memory_space=pltpu.VMEM)),
        input_output_aliases={k: 3 + k for k in range(2 * n)},
        compiler_params=pltpu.CompilerParams(has_side_effects=EFFECT),
    )(*[pltpu.with_memory_space_constraint(a, pltpu.HBM) for a in arrays],
      *[pltpu.with_memory_space_constraint(l, pltpu.HBM) for l in lands0], *deps)
    return dict(n=n, send=outs[0], recv=outs[1], own=outs[2], ins=outs[3:3 + n],
                lands=outs[3 + n:3 + 2 * n], token=outs[-1], name=name)


def gather_relay(handle, after):
    n = handle["n"]
    n1, n2 = len(PHASE1_MASKS), len(FAR_MASKS)

    def body(*refs):
        in_refs, land_refs = refs[:n], refs[n:2 * n]
        send1, recv1 = refs[2 * n], refs[2 * n + 1]
        send2, recv2, token = refs[2 * n + 3], refs[2 * n + 4], refs[-1]
        token[...] = jnp.zeros_like(token)
        sibling, _ = _peer(1)
        for k in range(n):
            for j, mask in enumerate(FAR_MASKS):
                peer, pidx = _peer(mask)
                landed = land_refs[k].at[:, pidx]
                pltpu.make_async_remote_copy(
                    src_ref=in_refs[k], dst_ref=landed, send_sem=send1.at[k * n1 + 1 + j],
                    recv_sem=recv1.at[k * n1 + 1 + j], device_id=peer, device_id_type=MESH).wait_recv()
                pltpu.make_async_remote_copy(
                    src_ref=landed, dst_ref=landed, send_sem=send2.at[k * n2 + j],
                    recv_sem=recv2.at[k * n2 + j], device_id=sibling, device_id_type=MESH).start()

    thru = (*handle["ins"], *handle["lands"])
    outs = pl.pallas_call(
        body, name=handle["name"] + "_relay",
        out_shape=(pltpu.SemaphoreType.DMA((n * n2,)), pltpu.SemaphoreType.DMA((n * n2,)),
                   *[pltpu.HBM(a.shape, a.dtype) for a in thru], _sds((8, LANES), F32)),
        in_specs=[HBM_SPEC] * (2 * n) + [SEM_SPEC, SEM_SPEC, ANY],
        out_specs=(SEM_SPEC, SEM_SPEC, *[HBM_SPEC] * (2 * n), pl.BlockSpec(memory_space=pltpu.VMEM)),
        input_output_aliases={k: 2 + k for k in range(2 * n)},
        compiler_params=pltpu.CompilerParams(has_side_effects=EFFECT),
    )(*thru, handle["send"], handle["recv"], after)
    handle.update(send2=outs[0], recv2=outs[1], ins=outs[2:2 + n], lands=outs[2 + n:2 + 2 * n],
                  relay_token=outs[-1])


def gather_finish(handle, after):
    n = handle["n"]
    n1, n2 = len(PHASE1_MASKS), len(FAR_MASKS)

    def body(*refs):
        in_refs, land_refs = refs[:n], refs[n:2 * n]
        send1, recv1, own_sems, send2, recv2 = refs[2 * n:2 * n + 5]
        mine = _my_index()
        sibling, sib_idx = _peer(1)
        for k in range(n):
            _own_copy("gather", in_refs[k], land_refs[k], mine, own_sems.at[k]).wait()
            for j, mask in enumerate(PHASE1_MASKS):
                peer, pidx = _peer(mask)
                cp = pltpu.make_async_remote_copy(
                    src_ref=in_refs[k], dst_ref=land_refs[k].at[:, pidx], send_sem=send1.at[k * n1 + j],
                    recv_sem=recv1.at[k * n1 + j], device_id=peer, device_id_type=MESH)
                cp.wait_send()
                if mask == 1:
                    cp.wait_recv()
            for j, mask in enumerate(FAR_MASKS):
                _, pidx = _peer(mask)
                _, far_of_sibling = _peer(mask ^ 1)
                cp = pltpu.make_async_remote_copy(
                    src_ref=land_refs[k].at[:, pidx], dst_ref=land_refs[k].at[:, far_of_sibling],
                    send_sem=send2.at[k * n2 + j], recv_sem=recv2.at[k * n2 + j],
                    device_id=sibling, device_id_type=MESH)
                cp.wait_send()
                cp.wait_recv()

    thru = (*handle["ins"], *handle["lands"])
    outs = pl.pallas_call(
        body, name=handle["name"] + "_finish",
        out_shape=tuple(pltpu.HBM(a.shape, a.dtype) for a in thru),
        in_specs=[HBM_SPEC] * (2 * n) + [SEM_SPEC] * 5 + [ANY],
        out_specs=tuple([HBM_SPEC] * (2 * n)),
        input_output_aliases={k: k for k in range(2 * n)},
        compiler_params=pltpu.CompilerParams(has_side_effects=EFFECT),
    )(*thru, handle["send"], handle["recv"], handle["own"], handle["send2"], handle["recv2"], after)
    return list(outs[n:])


def all_reduce_small(packed, name, deps=()):
    rows = packed.shape[0]
    nd = len(deps)

    def body(in_ref, *rest):
        out_ref, slots, send_sems, recv_sems = rest[nd:]
        me = _my_index()
        sends = []
        for mask in range(1, N_DEV):
            peer, _ = _peer(mask)
            cp = pltpu.make_async_remote_copy(
                src_ref=in_ref, dst_ref=slots.at[me],
                send_sem=send_sems.at[mask - 1], recv_sem=recv_sems.at[mask - 1],
                device_id=peer, device_id_type=MESH)
            cp.start()
            sends.append(cp)
        slots[me] = in_ref[...]
        for mask in range(1, N_DEV):
            peer, pidx = _peer(mask)
            pltpu.make_async_remote_copy(
                src_ref=in_ref, dst_ref=slots.at[pidx],
                send_sem=send_sems.at[mask - 1], recv_sem=recv_sems.at[mask - 1],
                device_id=peer, device_id_type=MESH).wait_recv()
        for cp in sends:
            cp.wait_send()
        total = slots[0]
        for p in range(1, N_DEV):
            total = total + slots[p]
        out_ref[...] = total

    return pl.pallas_call(
        body, name=name,
        out_shape=_sds((rows, LANES), F32),
        in_specs=[pl.BlockSpec(memory_space=pltpu.VMEM)] + [ANY] * nd,
        out_specs=pl.BlockSpec(memory_space=pltpu.VMEM),
        scratch_shapes=[pltpu.VMEM((N_DEV, rows, LANES), F32),
                        pltpu.SemaphoreType.DMA((N_DEV - 1,)),
                        pltpu.SemaphoreType.DMA((N_DEV - 1,))],
    )(packed, *deps)


def matmul(dims, a, b, out_sds, grid, a_spec, b_spec, o_spec, acc_shape, *, name, alpha=1.0,
           bias=None, bias_spec=None, scale=None, scale_spec=None, res=None, res_spec=None,
           colsum_sds=None, colsum_spec=None, out_t_sds=None, out_t_spec=None, deps=()):
    nk = grid[2]
    has_bias, has_scale, has_res = bias is not None, scale is not None, res is not None
    has_cs, has_t = colsum_sds is not None, out_t_sds is not None
    if has_cs:
        assert grid[0] == 1 and dims == TN

    def body(*refs):
        a_ref, b_ref = refs[0], refs[1]
        pos = 2
        bias_ref = scale_ref = res_ref = cs_ref = ot_ref = None
        if has_bias:
            bias_ref = refs[pos]; pos += 1
        if has_scale:
            scale_ref = refs[pos]; pos += 1
        if has_res:
            res_ref = refs[pos]; pos += 1
        pos += len(deps)
        o_ref = refs[pos]; pos += 1
        if has_cs:
            cs_ref = refs[pos]; pos += 1
        if has_t:
            ot_ref = refs[pos]; pos += 1
        k = pl.program_id(2)
        bval = b_ref[...]
        part = lax.dot_general(a_ref[...].astype(BF16), bval.astype(BF16), dims,
                               preferred_element_type=F32)

        def finish(total):
            r = total * alpha if alpha != 1.0 else total
            if has_bias:
                r = r + bias_ref[...]
            if has_scale:
                r = r * scale_ref[...]
            if has_res:
                r = r + res_ref[...].astype(F32)
            o_ref[...] = r.astype(o_ref.dtype)
            if has_t:
                ot_ref[...] = r.T.astype(ot_ref.dtype)

        if has_cs:
            csum = jnp.sum(bval.astype(F32), axis=0, keepdims=True)

            @pl.when(k == 0)
            def _():
                cs_ref[...] = csum

            @pl.when(k > 0)
            def _():
                cs_ref[...] += csum

        if nk == 1:
            finish(part)
        else:
            acc_ref = refs[pos]

            @pl.when(k == 0)
            def _():
                acc_ref[...] = part

            @pl.when(k > 0)
            def _():
                acc_ref[...] += part

            @pl.when(k == nk - 1)
            def _():
                finish(acc_ref[...])

    in_specs, args = [a_spec, b_spec], [a, b]
    if has_bias:
        in_specs.append(bias_spec); args.append(bias)
    if has_scale:
        in_specs.append(scale_spec); args.append(scale)
    if has_res:
        in_specs.append(res_spec); args.append(res)
    in_specs += [ANY] * len(deps)
    args += list(deps)
    out_shape, out_specs = [out_sds], [o_spec]
    if has_cs:
        out_shape.append(colsum_sds); out_specs.append(colsum_spec)
    if has_t:
        out_shape.append(out_t_sds); out_specs.append(out_t_spec)
    scratch = [] if nk == 1 else [pltpu.VMEM(acc_shape, F32)]
    outs = pl.pallas_call(
        body, name=name, grid=grid, in_specs=in_specs, out_specs=out_specs, out_shape=out_shape,
        scratch_shapes=scratch, compiler_params=_cparams(3))(*args)
    return outs if (has_cs or has_t) else outs[0]


def _sigmoid(z):
    return 1.0 / (1.0 + jnp.exp(-z))


def _log_sigmoid(z):
    return jnp.minimum(z, 0.0) - jnp.log(1.0 + jnp.exp(-jnp.abs(z)))


def rmsnorm_fwd(x, gain, tm, name, deps=()):
    t, d = x.shape

    def body(x_ref, g_ref, *rest):
        o_ref = rest[-1]
        xf = x_ref[...]
        r = lax.rsqrt(jnp.mean(xf * xf, axis=-1, keepdims=True) + RMS_EPS)
        o_ref[...] = (xf * r * g_ref[...]).astype(o_ref.dtype)

    return pl.pallas_call(
        body, name=name, grid=(t // tm,),
        in_specs=[pl.BlockSpec((tm, d), lambda i: (i, 0)), pl.BlockSpec((1, d), lambda i: (0, 0))]
        + [ANY] * len(deps),
        out_specs=pl.BlockSpec((tm, d), lambda i: (i, 0)),
        out_shape=_sds((t, d), BF16), compiler_params=_cparams(1))(x, gain, *deps)


def loss_head(x, gain, target, tm, name):
    t, d = x.shape

    def body(x_ref, g_ref, tgt_ref, dx_ref, dg_ref, loss_ref):
        i = pl.program_id(0)
        xf = x_ref[...]
        g = g_ref[...]
        r = lax.rsqrt(jnp.mean(xf * xf, axis=-1, keepdims=True) + RMS_EPS)
        xhat = xf * r
        err = xhat * g - tgt_ref[...]
        part = 0.5 * jnp.sum(jnp.mean(err * err, axis=-1, keepdims=True))
        dy = err * (1.0 / d)
        dxhat = dy * g
        dx_ref[...] = r * (dxhat - xhat * jnp.mean(dxhat * xhat, axis=-1, keepdims=True))
        dg = jnp.sum(dy * xhat, axis=0, keepdims=True)
        lpart = jnp.full((8, LANES), part, F32)

        @pl.when(i == 0)
        def _():
            dg_ref[...] = dg
            loss_ref[...] = lpart

        @pl.when(i > 0)
        def _():
            dg_ref[...] += dg
            loss_ref[...] += lpart

    row = pl.BlockSpec((tm, d), lambda i: (i, 0))
    vec = pl.BlockSpec((1, d), lambda i: (0, 0))
    return pl.pallas_call(
        body, name=name, grid=(t // tm,), in_specs=[row, vec, row],
        out_specs=[row, vec, pl.BlockSpec((8, LANES), lambda i: (0, 0))],
        out_shape=[_sds((t, d), F32), _sds((1, d), F32), _sds((8, LANES), F32)],
        compiler_params=_cparams(1))(x, gain, target)


def ffn_in_swiglu(hn, wa, s, tm, name):
    t = hn.shape[0]
    halves = 2 if tm % 512 == 0 else 1
    rows = tm // halves

    def body(h_ref, wg_ref, wu_ref, gu_ref, act_ref):
        for c in range(halves):
            rs = slice(c * rows, (c + 1) * rows)
            h = h_ref[rs, :]
            g = jnp.dot(h, wg_ref[...], preferred_element_type=F32)
            u = jnp.dot(h, wu_ref[...], preferred_element_type=F32)
            gu_ref[0, rs, :] = g.astype(gu_ref.dtype)
            gu_ref[1, rs, :] = u.astype(gu_ref.dtype)
            act_ref[rs, :] = (g * _sigmoid(g) * u).astype(act_ref.dtype)

    return pl.pallas_call(
        body, name=name, grid=(t // tm, 4),
        in_specs=[pl.BlockSpec((tm, D_MODEL), lambda i, j: (i, 0)),
                  pl.BlockSpec((None, None, D_MODEL, FF_BLK), lambda i, j: (s, j, 0, 0)),
                  pl.BlockSpec((None, None, D_MODEL, FF_BLK), lambda i, j: (s, j + 4, 0, 0))],
        out_specs=[pl.BlockSpec((None, 2, tm, FF_BLK), lambda i, j: (j, 0, i, 0)),
                   pl.BlockSpec((None, tm, FF_BLK), lambda i, j: (j, i, 0))],
        out_shape=[_sds((4, 2, t, FF_BLK), BF16), _sds((4, t, FF_BLK), BF16)],
        compiler_params=_cparams(2))(hn, wa, wa)


def ffn_dact_swiglu(dy, wb, gu, s, tm, name):
    t = dy.shape[0]

    def body(dy_ref, w_ref, gu_ref, o_ref):
        da = 0.5 * lax.dot_general(dy_ref[...].astype(BF16), w_ref[...], NT, preferred_element_type=F32)
        g = gu_ref[0].astype(F32)
        u = gu_ref[1].astype(F32)
        sg = _sigmoid(g)
        o_ref[0] = (da * u * (sg * (1.0 + g * (1.0 - sg)))).astype(o_ref.dtype)
        o_ref[1] = (da * g * sg).astype(o_ref.dtype)

    blk = pl.BlockSpec((None, 2, tm, FF_BLK), lambda i, j: (j, 0, i, 0))
    return pl.pallas_call(
        body, name=name, grid=(t // tm, 4),
        in_specs=[pl.BlockSpec((tm, D_MODEL), lambda i, j: (i, 0)),
                  pl.BlockSpec((None, None, FF_BLK, D_MODEL), lambda i, j: (s, j, 0, 0)), blk],
        out_specs=blk, out_shape=_sds((4, 2, t, FF_BLK), BF16),
        compiler_params=_cparams(2))(dy, wb, gu)


def ffn_out_residual(act, wb, x, s, tm, name, deps=()):
    t = x.shape[0]

    def body(a_ref, w_ref, x_ref, *rest):
        o_ref = rest[-1]
        acc = jnp.dot(a_ref[0], w_ref[0], preferred_element_type=F32)
        for k in range(1, 4):
            acc = acc + jnp.dot(a_ref[k], w_ref[k], preferred_element_type=F32)
        o_ref[...] = x_ref[...] + 0.5 * acc

    row = pl.BlockSpec((tm, D_MODEL), lambda i: (i, 0))
    return pl.pallas_call(
        body, name=name, grid=(t // tm,),
        in_specs=[pl.BlockSpec((4, tm, FF_BLK), lambda i: (0, i, 0)),
                  pl.BlockSpec((None, 4, FF_BLK, D_MODEL), lambda i: (s, 0, 0, 0)), row] + [ANY] * len(deps),
        out_specs=row, out_shape=_sds((t, D_MODEL), F32), compiler_params=_cparams(1))(act, wb, x, *deps)


def ffn_dh_norm_bwd(dgu, wa, s, x, gain, dres, tm, name, deps):
    t = dgu.shape[2]
    nd = len(deps)

    def body(g_ref, w_ref, x_ref, gain_ref, dres_ref, *rest):
        dx_ref, dg_ref = rest[nd:]
        i = pl.program_id(0)
        dh = lax.dot_general(g_ref[0, 0], w_ref[0], NT, preferred_element_type=F32)
        for p in range(1, N_DEV):
            dh = dh + lax.dot_general(g_ref[p % 4, p // 4], w_ref[p], NT, preferred_element_type=F32)
        xf = x_ref[...]
        r = lax.rsqrt(jnp.mean(xf * xf, axis=-1, keepdims=True) + RMS_EPS)
        xhat = xf * r
        dxhat = dh * gain_ref[...]
        dx_ref[...] = dres_ref[...] + r * (dxhat - xhat * jnp.mean(dxhat * xhat, axis=-1, keepdims=True))
        dg = jnp.sum(dh * xhat, axis=0, keepdims=True)

        @pl.when(i == 0)
        def _():
            dg_ref[...] = dg

        @pl.when(i > 0)
        def _():
            dg_ref[...] += dg

    row = pl.BlockSpec((tm, D_MODEL), lambda i: (i, 0))
    vec = pl.BlockSpec((1, D_MODEL), lambda i: (0, 0))
    return pl.pallas_call(
        body, name=name, grid=(t // tm,),
        in_specs=[pl.BlockSpec((4, 2, tm, FF_BLK), lambda i: (0, 0, i, 0)),
                  pl.BlockSpec((None, N_DEV, D_MODEL, FF_BLK), lambda i: (s, 0, 0, 0)), row, vec, row]
        + [ANY] * nd,
        out_specs=[row, vec], out_shape=[_sds((t, D_MODEL), F32), _sds((1, D_MODEL), F32)],
        compiler_params=_cparams(1))(dgu, wa, x, gain, dres, *deps)


def branch_merge(os_, wbr, layer, gates, tm, name, deps=()):
    t = os_[0].shape[0]
    d = D_MODEL
    nd = len(deps)
    widths = [o.shape[1] for o in os_]
    starts = [sum(widths[:a]) for a in range(3)]

    def body(oa_ref, ob_ref, oc_ref, w_ref, g_ref, *rest):
        ya_ref, yb_ref, yc_ref, m_ref = rest[nd:]
        merged = None
        for a, (o_ref, y_ref) in enumerate(((oa_ref, ya_ref), (ob_ref, yb_ref), (oc_ref, yc_ref))):
            y = jnp.dot(o_ref[...], w_ref[starts[a]:starts[a] + widths[a], :], preferred_element_type=F32)
            y_ref[...] = y
            term = _sigmoid(g_ref[:, a * d:(a + 1) * d]) * y
            merged = term if merged is None else merged + term
        m_ref[...] = merged.astype(m_ref.dtype)

    row = pl.BlockSpec((tm, d), lambda i: (i, 0))
    ya, yb, yc, merged = pl.pallas_call(
        body, name=name, grid=(t // tm,),
        in_specs=[pl.BlockSpec((tm, w), lambda i: (i, 0)) for w in widths]
        + [pl.BlockSpec((None, d, d), lambda i: (layer, 0, 0)), pl.BlockSpec((tm, 3 * d), lambda i: (i, 0))]
        + [ANY] * nd,
        out_specs=[row, row, row, row],
        out_shape=[_sds((t, d), F32)] * 3 + [_sds((t, d), BF16)],
        compiler_params=_cparams(1))(*os_, wbr, gates, *deps)
    return [ya, yb, yc], merged


def dmerged_merge_bwd(dy, wout, layer, gates, ys, tm, name):
    t, d = dy.shape

    def body(dy_ref, w_ref, g_ref, ya_ref, yb_ref, yc_ref, dg_ref, dya_ref, dyb_ref, dyc_ref):
        dmv = lax.dot_general(dy_ref[...].astype(BF16), w_ref[...], NT, preferred_element_type=F32)
        for a, (y_ref, dy_out) in enumerate(((ya_ref, dya_ref), (yb_ref, dyb_ref), (yc_ref, dyc_ref))):
            cols = slice(a * d, (a + 1) * d)
            s = _sigmoid(g_ref[:, cols])
            dy_out[...] = (dmv * s).astype(dy_out.dtype)
            dg_ref[:, cols] = (dmv * y_ref[...] * s * (1.0 - s)).astype(dg_ref.dtype)

    row = pl.BlockSpec((tm, d), lambda i: (i, 0))
    wide = pl.BlockSpec((tm, 3 * d), lambda i: (i, 0))
    dg, dya, dyb, dyc = pl.pallas_call(
        body, name=name, grid=(t // tm,),
        in_specs=[row, pl.BlockSpec((None, d, d), lambda i: (layer, 0, 0)), wide, row, row, row],
        out_specs=[wide, row, row, row],
        out_shape=[_sds((t, 3 * d), BF16)] + [_sds((t, d), BF16)] * 3,
        compiler_params=_cparams(1))(dy, wout, gates, *ys)
    return dg, [dya, dyb, dyc]


def branch_bwd(dys, os_, wbr, layer, tm, name):
    t = dys[0].shape[0]
    d = D_MODEL
    nt = t // tm
    widths = [o.shape[1] for o in os_]
    starts = [sum(widths[:a]) for a in range(3)]

    def body(dya_ref, dyb_ref, dyc_ref, oa_ref, ob_ref, oc_ref, w_ref, do_ref, dot_ref, dw_ref, acc_ref):
        i = pl.program_id(0)
        for a, (dy_ref, o_ref) in enumerate(((dya_ref, oa_ref), (dyb_ref, ob_ref), (dyc_ref, oc_ref))):
            rows = slice(starts[a], starts[a] + widths[a])
            dyv = dy_ref[...]
            do = lax.dot_general(dyv, w_ref[rows, :], NT, preferred_element_type=F32)
            do_ref[:, rows] = do.astype(do_ref.dtype)
            dot_ref[rows, :] = do.T.astype(dot_ref.dtype)
            dw = lax.dot_general(o_ref[...], dyv, TN, preferred_element_type=F32)

            @pl.when(i == 0)
            def _():
                acc_ref[rows, :] = dw

            @pl.when(i > 0)
            def _():
                acc_ref[rows, :] += dw

        @pl.when(i == nt - 1)
        def _():
            dw_ref[...] = acc_ref[...].astype(dw_ref.dtype)

    row = pl.BlockSpec((tm, d), lambda i: (i, 0))
    return pl.pallas_call(
        body, name=name, grid=(nt,),
        in_specs=[row, row, row] + [pl.BlockSpec((tm, w), lambda i: (i, 0)) for w in widths]
        + [pl.BlockSpec((None, d, d), lambda i: (layer, 0, 0))],
        out_specs=[row, pl.BlockSpec((d, tm), lambda i: (0, i)), pl.BlockSpec((d, d), lambda i: (0, 0))],
        out_shape=[_sds((t, d), BF16), _sds((d, t), BF16), _sds((d, d), BF16)],
        scratch_shapes=[pltpu.VMEM((d, d), F32)], compiler_params=_cparams(1))(*dys, *os_, wbr)


def mixer_dh_norm_bwd(dh_part, dgates, wc, gate_idx, df, wf, layer, x, gain, dres, tm, name):
    t, d = x.shape

    def body(dhp_ref, dg_ref, wg_ref, df_ref, wf_ref, x_ref, gain_ref, dres_ref, dx_ref, dgain_ref):
        i = pl.program_id(0)
        dh = (dhp_ref[...]
              + lax.dot_general(dg_ref[...], wg_ref[...], NT, preferred_element_type=F32)
              + lax.dot_general(df_ref[...].astype(BF16), wf_ref[...], NT, preferred_element_type=F32))
        xf = x_ref[...]
        r = lax.rsqrt(jnp.mean(xf * xf, axis=-1, keepdims=True) + RMS_EPS)
        xhat = xf * r
        dxhat = dh * gain_ref[...]
        dx_ref[...] = dres_ref[...] + r * (dxhat - xhat * jnp.mean(dxhat * xhat, axis=-1, keepdims=True))
        dg = jnp.sum(dh * xhat, axis=0, keepdims=True)

        @pl.when(i == 0)
        def _():
            dgain_ref[...] = dg

        @pl.when(i > 0)
        def _():
            dgain_ref[...] += dg

    row = pl.BlockSpec((tm, d), lambda i: (i, 0))
    vec = pl.BlockSpec((1, d), lambda i: (0, 0))
    return pl.pallas_call(
        body, name=name, grid=(t // tm,),
        in_specs=[row, pl.BlockSpec((tm, QKV_WIDTH), lambda i: (i, 0)),
                  pl.BlockSpec((None, d, QKV_WIDTH), lambda i: (gate_idx, 0, 0)),
                  pl.BlockSpec((tm, LANES), lambda i: (i, 0)),
                  pl.BlockSpec((None, d, LANES), lambda i: (layer, 0, 0)), row, vec, row],
        out_specs=[row, vec], out_shape=[_sds((t, d), F32), _sds((1, d), F32)],
        compiler_params=_cparams(1))(dh_part, dgates, wc, df, wf, x, gain, dres)


def _iota2(shape, dim):
    return lax.broadcasted_iota(jnp.int32, shape, dim)


def proj_gates_forget(hm, wc, gate_idx, wf, layer, bg, bf, tm, name, deps=()):
    t, d = hm.shape
    sub = tm // QB
    nd = len(deps)

    def body(h_ref, wg_ref, bg_ref, wf_ref, bf_ref, *rest):
        g_ref, f_ref, fcol_ref, frow_ref, carry = rest[nd:]
        i, j = pl.program_id(0), pl.program_id(1)
        h = h_ref[...]
        g_ref[...] = jnp.dot(h, wg_ref[...], preferred_element_type=F32) + bg_ref[...]

        @pl.when((i == 0) & (j == 0))
        def _():
            carry[...] = jnp.zeros_like(carry)

        @pl.when(j == 0)
        def _():
            f = jnp.dot(h, wf_ref[...], preferred_element_type=F32) + bf_ref[...]
            f_ref[...] = f
            logf = _log_sigmoid(f)
            tri = (_iota2((QB, QB), 1) <= _iota2((QB, QB), 0)).astype(F32)
            for s in range(sub):
                rows = slice(s * QB, (s + 1) * QB)
                part = logf[rows, :]
                blk = jnp.dot(tri, part, precision=HIGHEST, preferred_element_type=F32) + carry[...]
                carry[...] += jnp.sum(part, axis=0, keepdims=True)
                fcol_ref[rows, :] = blk
                frow_ref[s] = blk.T[0:8, :]

    narrow = pl.BlockSpec((tm, LANES), lambda i, j: (i, 0))
    return pl.pallas_call(
        body, name=name, grid=(t // tm, 3),
        in_specs=[pl.BlockSpec((tm, d), lambda i, j: (i, 0)),
                  pl.BlockSpec((None, d, d), lambda i, j: (gate_idx, 0, j)),
                  pl.BlockSpec((1, d), lambda i, j: (0, j)),
                  pl.BlockSpec((None, d, LANES), lambda i, j: (layer, 0, 0)),
                  pl.BlockSpec((1, LANES), lambda i, j: (0, 0))] + [ANY] * nd,
        out_specs=[pl.BlockSpec((tm, d), lambda i, j: (i, j)), narrow, narrow,
                   pl.BlockSpec((sub, 8, QB), lambda i, j: (i, 0, 0))],
        out_shape=[_sds((t, 3 * d), F32), _sds((t, LANES), F32), _sds((t, LANES), F32),
                   _sds((t // QB, 8, QB), F32)],
        scratch_shapes=[pltpu.VMEM((1, LANES), F32)], compiler_params=_cparams(2))(hm, wc, bg, wf, bf, *deps)


def forget_cumsum_bwd(dfrow, f, name):
    t = f.shape[0]
    nq = t // QB

    def body(dfr_ref, f_ref, df_ref, carry):
        jj = pl.program_id(0)

        @pl.when(jj == 0)
        def _():
            carry[...] = jnp.zeros_like(carry)

        padded = jnp.concatenate([dfr_ref[...], jnp.zeros((QB - 8, QB), F32)], axis=0)
        dfcol = padded.T
        tri = (_iota2((QB, QB), 1) >= _iota2((QB, QB), 0)).astype(F32)
        dlogf = jnp.dot(tri, dfcol, precision=HIGHEST, preferred_element_type=F32) + carry[...]
        carry[...] += jnp.sum(dfcol, axis=0, keepdims=True)
        df_ref[...] = dlogf * _sigmoid(-f_ref[...])

    return pl.pallas_call(
        body, name=name, grid=(nq,),
        in_specs=[pl.BlockSpec((None, 8, QB), lambda jj: (nq - 1 - jj, 0, 0)),
                  pl.BlockSpec((QB, LANES), lambda jj: (nq - 1 - jj, 0))],
        out_specs=pl.BlockSpec((QB, LANES), lambda jj: (nq - 1 - jj, 0)),
        out_shape=_sds((t, LANES), F32),
        scratch_shapes=[pltpu.VMEM((1, LANES), F32)], compiler_params=_cparams(1))(dfrow, f)


REL_DIAG = 768
REL_SHIFT = REL_DIAG - (QB - 1)


def _diag_onehot():
    u = _iota2((REL_PAD, REL_DIAG), 1)
    rel = jnp.clip(CH_KEYS - 1 - u, -MAX_REL, MAX_REL) + MAX_REL
    return (_iota2((REL_PAD, REL_DIAG), 0) == rel).astype(F32)


def rel_bias_build(tab_t, name):
    def body(tab_ref, o_ref):
        diag = jnp.dot(tab_ref[...], _diag_onehot(), precision=HIGHEST, preferred_element_type=F32)
        band = _chunk_band()
        for h in range(N_HEADS_CH):
            rows = jnp.broadcast_to(diag[h:h + 1, :], (QB, REL_DIAG))
            o_ref[h] = pltpu.roll(rows, REL_SHIFT, 1, stride=1, stride_axis=0)[:, :CH_KEYS] + band

    return pl.pallas_call(
        body, name=name, out_shape=_sds((N_HEADS_CH, QB, CH_KEYS), F32),
        in_specs=[pl.BlockSpec(memory_space=pltpu.VMEM)], out_specs=pl.BlockSpec(memory_space=pltpu.VMEM),
    )(tab_t)


def rel_bias_scatter(dbias, name):
    def body(db_ref, o_ref, ddiag):
        flip = (_iota2((QB, QB), 0) + _iota2((QB, QB), 1) == QB - 1).astype(F32)
        for h in range(N_HEADS_CH):
            padded = jnp.concatenate([db_ref[h], jnp.zeros((QB, REL_DIAG - CH_KEYS), F32)], axis=1)
            flipped = jnp.dot(flip, padded, precision=HIGHEST, preferred_element_type=F32)
            unrolled = pltpu.roll(flipped, 0, 1, stride=1, stride_axis=0)
            ddiag[h:h + 1, :] = jnp.sum(unrolled, axis=0, keepdims=True)
        o_ref[...] = lax.dot_general(ddiag[...], _diag_onehot(), NT, precision=HIGHEST,
                                     preferred_element_type=F32)

    return pl.pallas_call(
        body, name=name, out_shape=_sds((N_HEADS_CH, REL_PAD), F32),
        in_specs=[pl.BlockSpec(memory_space=pltpu.VMEM)], out_specs=pl.BlockSpec(memory_space=pltpu.VMEM),
        scratch_shapes=[pltpu.VMEM((N_HEADS_CH, REL_DIAG), F32)],
    )(dbias)


def _hl(h):
    return slice(h * HEAD_DIM, (h + 1) * HEAD_DIM)


def _split_dot(x, tri_bf16):
    hi = x.astype(BF16)
    lo = (x - hi.astype(F32)).astype(BF16)
    return (jnp.dot(hi, tri_bf16, preferred_element_type=F32)
            + jnp.dot(lo, tri_bf16, preferred_element_type=F32))


def _krows(g):
    return pl.ds(pl.multiple_of(g * KB, KB), KB)


def _log_sigmoid_pair(z):
    sp = jnp.log(1.0 + jnp.exp(-jnp.abs(z)))
    return jnp.minimum(z, 0.0) - sp, -jnp.maximum(z, 0.0) - sp


def _qkv_specs(t, col0, n_pairs):
    q_spec = pl.BlockSpec((QB, LANES), lambda hp, i: (i, col0 + hp))
    k_spec = pl.BlockSpec((t, LANES), lambda hp, i: (0, col0 + n_pairs + hp))
    v_spec = pl.BlockSpec((t, LANES), lambda hp, i: (0, col0 + 2 * n_pairs + hp))
    return q_spec, k_spec, v_spec


def _keys_major(xt):
    n, groups, w, _ = xt.shape
    return xt.transpose(1, 3, 0, 2).reshape(groups * KB, n * w)


def sb_fwd(qkv, name):
    t = qkv.shape[0]
    nq = t // QB

    def body(q_ref, k_ref, v_ref, o_ref, w_ref):
        i = pl.program_id(1)
        groups = i // KSUB + 1
        tri_after = (_iota2((KB, KB), 0) > _iota2((KB, KB), 1)).astype(BF16)
        t_idx = i * QB + _iota2((QB, KB), 0)
        qs = [q_ref[:, _hl(h)] for h in range(2)]

        def step(g, carry, masked):
            strict = (g * KB + _iota2((QB, KB), 1)) < t_idx
            out = []
            for h in range(2):
                tail, acc = carry[2 * h], carry[2 * h + 1]
                k = k_ref[_krows(g), _hl(h)]
                v = v_ref[_krows(g), _hl(h)]
                z = lax.dot_general(qs[h], k, NT, preferred_element_type=F32)
                lb, lf = _log_sigmoid_pair(z)
                if masked:
                    lf = jnp.where(strict, lf, 0.0)
                between = _split_dot(lf, tri_after) + tail
                w = jnp.exp(lb + between)
                if masked:
                    w = jnp.where(strict, w, 0.0)
                w = w.astype(BF16)
                w_ref[h, g] = w
                acc = acc + jnp.dot(w, v, preferred_element_type=F32)
                out += [tail + jnp.sum(lf, axis=1, keepdims=True), acc]
            return tuple(out)

        init = (jnp.zeros((QB, 1), F32), jnp.zeros((QB, HEAD_DIM), F32)) * 2
        res = step(groups - 1, init, True)
        res = lax.fori_loop(0, groups - 1, lambda gg, c: step(groups - 2 - gg, c, False), res)
        for h in range(2):
            o_ref[:, _hl(h)] = res[2 * h + 1].astype(o_ref.dtype)

    q_spec, k_spec, v_spec = _qkv_specs(t, 0, 2)
    return pl.pallas_call(
        body, name=name, grid=(2, nq), in_specs=[q_spec, k_spec, v_spec],
        out_specs=[pl.BlockSpec((QB, LANES), lambda hp, i: (i, hp)),
                   pl.BlockSpec((2, None, t // KB, QB, KB), lambda hp, i: (hp, i, 0, 0, 0))],
        out_shape=[_sds((t, W_SB), BF16), _sds((4, nq, t // KB, QB, KB), BF16)],
        compiler_params=_cparams(2))(qkv, qkv, qkv)


def _hs(h):
    return slice(h * HEAD_DIM, (h + 1) * HEAD_DIM)


def sb_bwd(qkv, qkv_t, w, do, do_t, name):
    t = qkv.shape[0]
    nq = t // QB

    def body(q_ref, k_ref, v_ref, do_ref, qt_ref, dot_ref, w_ref, dq_ref, dkt_ref, dvt_ref):
        i = pl.program_id(1)

        @pl.when(i == 0)
        def _():
            dkt_ref[...] = jnp.zeros_like(dkt_ref)
            dvt_ref[...] = jnp.zeros_like(dvt_ref)

        groups = i // KSUB + 1
        tri_before = (_iota2((KB, KB), 0) < _iota2((KB, KB), 1)).astype(BF16)
        t_idx = i * QB + _iota2((QB, KB), 0)
        qs = [q_ref[:, _hl(h)] for h in range(2)]
        dos = [do_ref[:, _hl(h)] for h in range(2)]
        qts = [qt_ref[_hs(h), :] for h in range(2)]
        dots = [dot_ref[_hs(h), :] for h in range(2)]

        def grads(g, carry, masked):
            strict = (g * KB + _iota2((QB, KB), 1)) < t_idx
            out = []
            for h in range(2):
                head, dq = carry[2 * h], carry[2 * h + 1]
                k = k_ref[_krows(g), _hl(h)]
                v = v_ref[_krows(g), _hl(h)]
                wb = w_ref[h, g]
                z = lax.dot_general(qs[h], k, NT, preferred_element_type=F32)
                beta = _sigmoid(z)
                e = lax.dot_general(dos[h], v, NT, preferred_element_type=F32) * wb.astype(F32)
                before = _split_dot(e, tri_before) + head
                dz = e * (1.0 - beta) - before * beta
                if masked:
                    dz = jnp.where(strict, dz, 0.0)
                dzb = dz.astype(BF16)
                dq = dq + jnp.dot(dzb, k, preferred_element_type=F32)
                dkt_ref[g, _hs(h), :] += jnp.dot(qts[h], dzb, preferred_element_type=F32)
                dvt_ref[g, _hs(h), :] += jnp.dot(dots[h], wb, preferred_element_type=F32)
                out += [head + jnp.sum(e, axis=1, keepdims=True), dq]
            return tuple(out)

        init = (jnp.zeros((QB, 1), F32), jnp.zeros((QB, HEAD_DIM), F32)) * 2
        res = lax.fori_loop(0, groups - 1, lambda g, c: grads(g, c, False), init)
        res = grads(groups - 1, res, True)
        for h in range(2):
            dq_ref[:, _hl(h)] = (res[2 * h + 1] * SCALE).astype(dq_ref.dtype)

    q_spec, k_spec, v_spec = _qkv_specs(t, 0, 2)
    blk = pl.BlockSpec((QB, LANES), lambda hp, i: (i, hp))
    blk_t = pl.BlockSpec((LANES, QB), lambda hp, i: (hp, i))
    acc_t = pl.BlockSpec((None, t // KB, LANES, KB), lambda hp, i: (hp, 0, 0, 0))
    acc_sds = _sds((2, t // KB, LANES, KB), F32)
    return pl.pallas_call(
        body, name=name, grid=(2, nq),
        in_specs=[q_spec, k_spec, v_spec, blk, blk_t, blk_t,
                  pl.BlockSpec((2, None, t // KB, QB, KB), lambda hp, i: (hp, i, 0, 0, 0))],
        out_specs=[blk, acc_t, acc_t],
        out_shape=[_sds((t, W_SB), BF16), acc_sds, acc_sds],
        compiler_params=_cparams(2))(qkv, qkv, qkv, do, qkv_t, do_t, w)


FOX_STEP_HEADS = 4
FOX_COLS = FOX_STEP_HEADS * HEAD_DIM


def _fox_specs(t):
    first = 3 * (W_SB + W_CH) // FOX_COLS
    n = W_FOX // FOX_COLS
    q_spec = pl.BlockSpec((QB, FOX_COLS), lambda hp, i: (i, first + hp))
    k_spec = pl.BlockSpec((t, FOX_COLS), lambda hp, i: (0, first + n + hp))
    v_spec = pl.BlockSpec((t, FOX_COLS), lambda hp, i: (0, first + 2 * n + hp))
    return q_spec, k_spec, v_spec


def fox_fwd(qkv, fcol, frow, name):
    t = qkv.shape[0]
    nq = t // QB

    def body(q_ref, k_ref, v_ref, fc_ref, fr_ref, o_ref, lse_ref):
        hp = pl.program_id(0)
        i = pl.program_id(1)
        groups = i // KSUB + 1
        t_idx = i * QB + _iota2((QB, KB), 0)
        lane = _iota2((QB, LANES), 1)
        sub = _iota2((8, KB), 0)
        qs = [q_ref[:, _hl(h)] for h in range(FOX_STEP_HEADS)]
        f_qs = [jnp.sum(jnp.where(lane == hp * FOX_STEP_HEADS + h, fc_ref[...], 0.0), axis=1, keepdims=True)
                for h in range(FOX_STEP_HEADS)]

        def step(g, carry, masked):
            causal = (g * KB + _iota2((QB, KB), 1)) <= t_idx
            fr = fr_ref[g]
            out = []
            for h in range(FOX_STEP_HEADS):
                m, l, acc = carry[3 * h:3 * h + 3]
                k = k_ref[_krows(g), _hl(h)]
                v = v_ref[_krows(g), _hl(h)]
                f_k = jnp.sum(jnp.where(sub == hp * FOX_STEP_HEADS + h, fr, 0.0), axis=0, keepdims=True)
                z = lax.dot_general(qs[h], k, NT, preferred_element_type=F32) + f_qs[h] - f_k
                if masked:
                    z = jnp.where(causal, z, NEG)
                m_new = jnp.maximum(m, jnp.max(z, axis=1, keepdims=True))
                p = jnp.exp(z - m_new)
                corr = jnp.exp(m - m_new)
                l = l * corr + jnp.sum(p, axis=1, keepdims=True)
                acc = acc * corr + jnp.dot(p.astype(BF16), v, preferred_element_type=F32)
                out += [m_new, l, acc]
            return tuple(out)

        init = (jnp.full((QB, 1), NEG, F32), jnp.zeros((QB, 1), F32),
                jnp.zeros((QB, HEAD_DIM), F32)) * FOX_STEP_HEADS
        res = lax.fori_loop(0, groups - 1, lambda g, c: step(g, c, False), init)
        res = step(groups - 1, res, True)
        for h in range(FOX_STEP_HEADS):
            m, l, acc = res[3 * h:3 * h + 3]
            o_ref[:, _hl(h)] = (acc / l).astype(o_ref.dtype)
            lse_ref[:, _hl(h)] = jnp.broadcast_to(m + jnp.log(l), (QB, HEAD_DIM))

    q_spec, k_spec, v_spec = _fox_specs(t)
    blk = pl.BlockSpec((QB, FOX_COLS), lambda hp, i: (i, hp))
    return pl.pallas_call(
        body, name=name, grid=(W_FOX // FOX_COLS, nq),
        in_specs=[q_spec, k_spec, v_spec, pl.BlockSpec((QB, LANES), lambda hp, i: (i, 0)),
                  pl.BlockSpec((t // KB, 8, KB), lambda hp, i: (0, 0, 0))],
        out_specs=[blk, blk],
        out_shape=[_sds((t, W_FOX), BF16), _sds((t, W_FOX), F32)],
        compiler_params=_cparams(2))(qkv, qkv, qkv, fcol, frow)


def fox_bwd(qkv, qkv_t, fcol, frow, o, lse, do, do_t, name, do_col=0):
    t = qkv.shape[0]
    nq = t // QB

    def body(q_ref, k_ref, v_ref, fc_ref, fr_ref, o_ref, lse_ref, do_ref, qt_ref, dot_ref,
             dq_ref, dk_ref, dv_ref, dfr_ref):
        hp = pl.program_id(0)
        i = pl.program_id(1)
        qts = [qt_ref[_hs(h), :] for h in range(FOX_STEP_HEADS)]
        dots = [dot_ref[_hs(h), :] for h in range(FOX_STEP_HEADS)]

        @pl.when(i == 0)
        def _():
            dk_ref[...] = jnp.zeros_like(dk_ref)
            dv_ref[...] = jnp.zeros_like(dv_ref)

        @pl.when((i == 0) & (hp == 0))
        def _():
            dfr_ref[...] = jnp.zeros_like(dfr_ref)

        groups = i // KSUB + 1
        t_idx = i * QB + _iota2((QB, KB), 0)
        lane = _iota2((QB, LANES), 1)
        sub = _iota2((8, KB), 0)
        qs = [q_ref[:, _hl(h)] for h in range(FOX_STEP_HEADS)]
        dos = [do_ref[:, _hl(h)] for h in range(FOX_STEP_HEADS)]
        f_qs = [jnp.sum(jnp.where(lane == hp * FOX_STEP_HEADS + h, fc_ref[...], 0.0), axis=1, keepdims=True)
                for h in range(FOX_STEP_HEADS)]
        lse_qs = [lse_ref[:, h * HEAD_DIM:h * HEAD_DIM + 1] for h in range(FOX_STEP_HEADS)]
        deltas = [jnp.sum(dos[h].astype(F32) * o_ref[:, _hl(h)].astype(F32), axis=1, keepdims=True)
                  for h in range(FOX_STEP_HEADS)]

        def step(g, dqs, masked):
            causal = (g * KB + _iota2((QB, KB), 1)) <= t_idx
            fr = fr_ref[g]
            out = []
            dfr = jnp.zeros((8, KB), F32)
            for h in range(FOX_STEP_HEADS):
                k = k_ref[_krows(g), _hl(h)]
                v = v_ref[_krows(g), _hl(h)]
                f_k = jnp.sum(jnp.where(sub == hp * FOX_STEP_HEADS + h, fr, 0.0), axis=0, keepdims=True)
                z = lax.dot_general(qs[h], k, NT, preferred_element_type=F32) + f_qs[h] - f_k
                p = jnp.exp(z - lse_qs[h])
                if masked:
                    p = jnp.where(causal, p, 0.0)
                dp = lax.dot_general(dos[h], v, NT, preferred_element_type=F32)
                ds = p * (dp - deltas[h])
                dsb = ds.astype(BF16)
                out.append(dqs[h] + jnp.dot(dsb, k, preferred_element_type=F32))
                dk_ref[g, _hs(h), :] += jnp.dot(qts[h], dsb, preferred_element_type=F32)
                dv_ref[g, _hs(h), :] += jnp.dot(dots[h], p.astype(BF16), preferred_element_type=F32)
                colsum = jnp.sum(ds, axis=0, keepdims=True)
                dfr = dfr + jnp.where(sub == hp * FOX_STEP_HEADS + h, -colsum, 0.0)
            dfr_ref[g] += dfr
            return tuple(out)

        res = lax.fori_loop(0, groups - 1, lambda g, c: step(g, c, False),
                            (jnp.zeros((QB, HEAD_DIM), F32),) * FOX_STEP_HEADS)
        res = step(groups - 1, res, True)
        for h in range(FOX_STEP_HEADS):
            dq_ref[:, _hl(h)] = (res[h] * SCALE).astype(dq_ref.dtype)

    q_spec, k_spec, v_spec = _fox_specs(t)
    blk = pl.BlockSpec((QB, FOX_COLS), lambda hp, i: (i, hp))
    frs = pl.BlockSpec((t // KB, 8, KB), lambda hp, i: (0, 0, 0))
    acc_t = pl.BlockSpec((None, t // KB, FOX_COLS, KB), lambda hp, i: (hp, 0, 0, 0))
    acc_sds = _sds((W_FOX // FOX_COLS, t // KB, FOX_COLS, KB), F32)
    return pl.pallas_call(
        body, name=name, grid=(W_FOX // FOX_COLS, nq),
        in_specs=[q_spec, k_spec, v_spec, pl.BlockSpec((QB, LANES), lambda hp, i: (i, 0)), frs,
                  blk, blk, pl.BlockSpec((QB, FOX_COLS), lambda hp, i: (i, do_col + hp)),
                  pl.BlockSpec((FOX_COLS, QB), lambda hp, i: (3 * (W_SB + W_CH) // FOX_COLS + hp, i)),
                  pl.BlockSpec((FOX_COLS, QB), lambda hp, i: (do_col + hp, i))],
        out_specs=[blk, acc_t, acc_t, frs],
        out_shape=[_sds((t, W_FOX), BF16), acc_sds, acc_sds, _sds((t // KB, 8, KB), F32)],
        compiler_params=_cparams(2))(qkv, qkv, qkv, fcol, frow, o, lse, do, qkv_t, do_t)


def _frow_to_groups(frow):
    n = frow.shape[0] // KSUB
    return frow.reshape(n, KSUB, 8, QB).transpose(0, 2, 1, 3).reshape(n, 8, KB)


def _frow_from_groups(frow):
    n = frow.shape[0]
    return frow.reshape(n, 8, KSUB, QB).transpose(0, 2, 1, 3).reshape(n * KSUB, 8, QB)


def _chunk_band():
    qi = _iota2((QB, CH_KEYS), 0)
    kj = _iota2((QB, CH_KEYS), 1)
    dchunk = (qi >> 6) + LEFT_CHUNKS - (kj >> 6)
    return jnp.where((dchunk >= 0) & (dchunk <= LEFT_CHUNKS), 0.0, NEG)


def _chunk_pad_row(i):
    kj = _iota2((1, CH_KEYS), 1)
    return jnp.where((i - (CH_WIN - 1)) * QB + kj >= 0, 0.0, NEG)


CH_PAD = (CH_WIN - 1) * QB
CH_STEP_HEADS = 4
CH_COLS = CH_STEP_HEADS * HEAD_DIM


def _window(i):
    return pl.ds(pl.multiple_of(i * QB, QB), CH_KEYS)


def _chunk_weights(q, kw, bias, pad_row):
    z = lax.dot_general(q, kw, NT, preferred_element_type=F32) + bias + pad_row
    e = jnp.exp(z - jnp.max(z, axis=1, keepdims=True))
    return e, 1.0 / jnp.sum(e, axis=1, keepdims=True)


def _chunk_specs(t):
    q_spec = pl.BlockSpec((QB, CH_COLS), lambda hp, i: (i, 3 * W_SB // CH_COLS + hp))
    kv_spec = pl.BlockSpec((t + CH_PAD, CH_COLS), lambda hp, i: (0, hp))
    return q_spec, kv_spec


def chunk_fwd(qkv, kp, vp, bias, name):
    t = qkv.shape[0]
    nq = t // QB

    def body(q_ref, k_ref, v_ref, b_ref, o_ref):
        i = pl.program_id(1)
        pad_row = _chunk_pad_row(i)
        for h in range(CH_STEP_HEADS):
            e, inv = _chunk_weights(q_ref[:, _hl(h)], k_ref[_window(i), _hl(h)], b_ref[h], pad_row)
            o = jnp.dot(e.astype(BF16), v_ref[_window(i), _hl(h)], preferred_element_type=F32)
            o_ref[:, _hl(h)] = (o * inv).astype(o_ref.dtype)

    q_spec, kv_spec = _chunk_specs(t)
    return pl.pallas_call(
        body, name=name, grid=(W_CH // CH_COLS, nq),
        in_specs=[q_spec, kv_spec, kv_spec,
                  pl.BlockSpec((CH_STEP_HEADS, QB, CH_KEYS), lambda hp, i: (hp, 0, 0))],
        out_specs=pl.BlockSpec((QB, CH_COLS), lambda hp, i: (i, hp)),
        out_shape=_sds((t, W_CH), BF16), compiler_params=_cparams(2))(qkv, kp, vp, bias)


def chunk_bwd(qkv, qkv_t, kp, vp, bias, do, do_t, name, do_col=0):
    t = qkv.shape[0]
    nq = t // QB

    def body(q_ref, k_ref, v_ref, b_ref, do_ref, qt_ref, dot_ref, dq_ref, dk_ref, dv_ref, db_ref):
        i = pl.program_id(1)

        @pl.when(i == 0)
        def _():
            dk_ref[...] = jnp.zeros_like(dk_ref)
            dv_ref[...] = jnp.zeros_like(dv_ref)
            db_ref[...] = jnp.zeros_like(db_ref)

        pad_row = _chunk_pad_row(i)
        for h in range(CH_STEP_HEADS):
            q = q_ref[:, _hl(h)]
            dov = do_ref[:, _hl(h)]
            kw = k_ref[_window(i), _hl(h)]
            e, inv = _chunk_weights(q, kw, b_ref[h], pad_row)
            p = e * inv
            dp = lax.dot_general(dov, v_ref[_window(i), _hl(h)], NT, preferred_element_type=F32)
            ds = p * (dp - jnp.sum(p * dp, axis=1, keepdims=True))
            db_ref[h] += ds
            dsb = ds.astype(BF16)
            dq_ref[:, _hl(h)] = (jnp.dot(dsb, kw, preferred_element_type=F32) * SCALE).astype(dq_ref.dtype)
            dkt = jnp.dot(qt_ref[_hs(h), :], dsb, preferred_element_type=F32)
            dvt = jnp.dot(dot_ref[_hs(h), :], p.astype(BF16), preferred_element_type=F32)
            for b in range(CH_WIN):
                dk_ref[i + b, _hs(h), :] += dkt[:, b * QB:(b + 1) * QB]
                dv_ref[i + b, _hs(h), :] += dvt[:, b * QB:(b + 1) * QB]

    q_spec, kv_spec = _chunk_specs(t)
    blk = pl.BlockSpec((QB, CH_COLS), lambda hp, i: (i, hp))
    bspec = pl.BlockSpec((CH_STEP_HEADS, QB, CH_KEYS), lambda hp, i: (hp, 0, 0))
    nblk = nq + CH_WIN - 1
    acc_t = pl.BlockSpec((None, nblk, CH_COLS, QB), lambda hp, i: (hp, 0, 0, 0))
    acc_sds = _sds((W_CH // CH_COLS, nblk, CH_COLS, QB), F32)
    return pl.pallas_call(
        body, name=name, grid=(W_CH // CH_COLS, nq),
        in_specs=[q_spec, kv_spec, kv_spec, bspec,
                  pl.BlockSpec((QB, CH_COLS), lambda hp, i: (i, do_col + hp)),
                  pl.BlockSpec((CH_COLS, QB), lambda hp, i: (3 * W_SB // CH_COLS + hp, i)),
                  pl.BlockSpec((CH_COLS, QB), lambda hp, i: (do_col + hp, i))],
        out_specs=[blk, acc_t, acc_t, bspec],
        out_shape=[_sds((t, W_CH), BF16), acc_sds, acc_sds, _sds((N_HEADS_CH, QB, CH_KEYS), F32)],
        compiler_params=_cparams(2))(qkv, kp, vp, bias, do, qkv_t, do_t)


def _sum_parts(p_ref):
    total = p_ref[0].astype(F32)
    for p in range(1, p_ref.shape[0]):
        total = total + p_ref[p].astype(F32)
    return total


def sum_parts_multi(parts_list, name):
    n = len(parts_list)

    def body(*refs):
        for p_ref, o_ref in zip(refs[:n], refs[n:]):
            o_ref[...] = _sum_parts(p_ref)

    shapes = [p.shape[2:] for p in parts_list]
    return pl.pallas_call(
        body, name=name, grid=(1,),
        in_specs=[pl.BlockSpec((N_DEV, None, r, c), lambda s: (0, 0, 0, 0)) for r, c in shapes],
        out_specs=[pl.BlockSpec((r, c), lambda s: (0, 0)) for r, c in shapes],
        out_shape=[_sds((r, c), F32) for r, c in shapes], compiler_params=_cparams(1))(*parts_list)


def adamw(parts, w, m, v, grid, p_specs, w_spec, name):
    c1 = 1.0 / (1.0 - ADAM_B1 ** ADAM_STEP)
    c2 = 1.0 / (1.0 - ADAM_B2 ** ADAM_STEP)
    n = len(parts)

    def body(*refs):
        w_ref, m_ref, v_ref, g_out, d_out, m_out, v_out = refs[n:]
        g = _sum_parts(refs[0])
        for q in range(1, n):
            g = jnp.where(pl.program_id(0) == q, _sum_parts(refs[q]), g)
        m_new = ADAM_B1 * m_ref[...] + (1.0 - ADAM_B1) * g
        v_new = ADAM_B2 * v_ref[...] + (1.0 - ADAM_B2) * (g * g)
        m_hat = m_new * c1
        v_hat = v_new * c2
        g_out[...] = g
        d_out[...] = -ADAM_LR * (m_hat / (jnp.sqrt(v_hat) + ADAM_EPS) + ADAM_WD * w_ref[...])
        m_out[...] = m_new
        v_out[...] = v_new

    out = _sds(w.shape, F32)
    return pl.pallas_call(
        body, name=name, grid=grid, in_specs=[*p_specs, w_spec, w_spec, w_spec],
        out_specs=[w_spec] * 4, out_shape=[out] * 4,
        compiler_params=_cparams(len(grid)))(*parts, w, m, v)


def _ffn_fwd(x, gain, wa, wb_after, s, tm, tag, on_event, deps=()):
    t = x.shape[0]
    hn = rmsnorm_fwd(x, gain, tm, f"rms_{tag}", deps)
    gu, act = ffn_in_swiglu(hn, wa, s, min(2 * tm, t), f"ffn_in_{tag}")
    relayed = on_event("act", act)
    wb = wb_after(act)
    y = ffn_out_residual(act, wb, x, s, min(2 * tm, t), f"ffn_out_{tag}", relayed)
    return y, (hn, gu, act), wb


def _ffn_bwd(dy, x, gain, saved, wa, wb, s, tm, tag, on_grads):
    t = x.shape[0]
    hn, gu, act = saved
    dgu = ffn_dact_swiglu(dy, wb, gu, s, min(2 * tm, t), f"ffn_dact_{tag}")
    dwb = matmul(TN, act, dy, _sds((4, FF_BLK, D_MODEL), BF16), (4, 1, 1),
                 pl.BlockSpec((None, t, FF_BLK), lambda i, j, k: (i, 0, 0)),
                 pl.BlockSpec((t, D_MODEL), lambda i, j, k: (0, 0)),
                 pl.BlockSpec((None, FF_BLK, D_MODEL), lambda i, j, k: (i, 0, 0)),
                 None, name=f"ffn_dwout_{tag}", alpha=0.5)
    dwa = matmul(TN, dgu, hn, _sds((8, FF_BLK, D_MODEL), BF16), (1, 8, 1),
                 pl.BlockSpec((None, None, t, FF_BLK), lambda i, j, k: (j % 4, j // 4, 0, 0)),
                 pl.BlockSpec((t, D_MODEL), lambda i, j, k: (0, 0)),
                 pl.BlockSpec((None, FF_BLK, D_MODEL), lambda i, j, k: (j, 0, 0)),
                 None, name=f"ffn_dwin_{tag}")
    deps = on_grads(dwa, dwb)
    return ffn_dh_norm_bwd(dgu, wa, s, x, gain, dy, tm, f"ffn_dh_{tag}", deps)


_Q_COLUMN_SCALE = np.ones((1, QKV_WIDTH), np.float32)
for _lo, _width in ((0, W_SB), (3 * W_SB, W_CH), (3 * (W_SB + W_CH), W_FOX)):
    _Q_COLUMN_SCALE[0, _lo:_lo + _width] = SCALE


def _mixer_fwd(x, gain, wqkv, wf, wgate, late_after, bq, bf, bg, bias, layer, tm, tag, on_event):
    t = x.shape[0]
    nt = t // tm
    hm = rmsnorm_fwd(x, gain, tm, f"rms_{tag}")
    a_full = pl.BlockSpec((tm, D_MODEL), lambda i, j, k: (i, 0))
    wide_out = pl.BlockSpec((tm, D_MODEL), lambda i, j, k: (i, j))
    wide_b = pl.BlockSpec((1, D_MODEL), lambda i, j, k: (0, j))
    qkv, qkv_t = matmul(NN, hm, wqkv, _sds((t, QKV_WIDTH), BF16), (nt, 3, 1), a_full,
                        pl.BlockSpec((None, D_MODEL, D_MODEL), lambda i, j, k: (layer, 0, j)), wide_out, None,
                        name=f"proj_qkv_{tag}", bias=bq, bias_spec=wide_b,
                        scale=jnp.asarray(_Q_COLUMN_SCALE), scale_spec=wide_b,
                        out_t_sds=_sds((QKV_WIDTH, t), BF16),
                        out_t_spec=pl.BlockSpec((D_MODEL, tm), lambda i, j, k: (j, i)))
    relayed = on_event("qkv", qkv)
    gates, f, fcol, frow = proj_gates_forget(hm, wgate, layer + 1, wf, layer, bg, bf, tm,
                                             f"proj_gate_{tag}", relayed)
    frow = _frow_to_groups(frow)
    o_sb, w_sb = sb_fwd(qkv, f"sb_fwd_{tag}")
    relayed = on_event("o_sb", o_sb)
    kp = jnp.pad(qkv[:, 10 * LANES:14 * LANES], ((CH_PAD, 0), (0, 0)))
    vp = jnp.pad(qkv[:, 14 * LANES:18 * LANES], ((CH_PAD, 0), (0, 0)))
    o_ch = chunk_fwd(qkv, kp, vp, bias, f"chunk_fwd_{tag}")
    o_fox, lse = fox_fwd(qkv, fcol, frow, f"fox_fwd_{tag}")
    wbr, wout = late_after(o_fox)
    ys, merged = branch_merge((o_sb, o_ch, o_fox), wbr, layer, gates, tm, f"branch_merge_{tag}", relayed)
    x_new = matmul(NN, merged, wout, _sds((t, D_MODEL), F32), (nt, 1, 1), a_full,
                   pl.BlockSpec((None, D_MODEL, D_MODEL), lambda i, j, k: (layer, 0, 0)), a_full, None,
                   name=f"wout_{tag}", res=x, res_spec=a_full)
    saved = (hm, qkv, gates, f, fcol, frow, o_sb, o_ch, o_fox, lse, ys, merged, kp, vp, w_sb, qkv_t)
    return x_new, saved, wbr, wout


def _mixer_bwd(dy, x, gain, saved, wqkv, wf, wgate, wbr, wout, bias, layer, tm, tag, on_grads):
    t = x.shape[0]
    nt = t // tm
    hm, qkv, gates, f, fcol, frow, o_sb, o_ch, o_fox, lse, ys, merged, kp, vp, w_sb, qkv_t = saved
    a_full = pl.BlockSpec((tm, D_MODEL), lambda i, j, k: (i, 0))
    red_row = pl.BlockSpec((tm, D_MODEL), lambda i, j, k: (k, 0))
    sq = pl.BlockSpec((D_MODEL, D_MODEL), lambda i, j, k: (0, 0))
    dgates, dys = dmerged_merge_bwd(dy, wout, layer, gates, ys, tm // 2, f"dmerged_{tag}")
    all_t = pl.BlockSpec((t, D_MODEL), lambda i, j, k: (0, 0))
    dwout = matmul(TN, merged, dy, _sds((D_MODEL, D_MODEL), BF16), (1, 1, 1), all_t, all_t, sq,
                   None, name=f"dwout_{tag}")
    do, do_t, dwbr = branch_bwd(dys, (o_sb, o_ch, o_fox), wbr, layer, tm, f"dbranch_{tag}")
    dq_a, dk_a, dv_a = sb_bwd(qkv, qkv_t, w_sb, do, do_t, f"sb_bwd_{tag}")
    dk_a, dv_a = _keys_major(dk_a), _keys_major(dv_a)
    dq_b, dk_b, dv_b, dbias = chunk_bwd(qkv, qkv_t, kp, vp, bias, do, do_t, f"chunk_bwd_{tag}",
                                        do_col=W_SB // CH_COLS)
    dk_b, dv_b = [x[:, CH_WIN - 1:].transpose(1, 3, 0, 2).reshape(t, W_CH) for x in (dk_b, dv_b)]
    dq_c, dk_c, dv_c, dfrow = fox_bwd(qkv, qkv_t, fcol, frow, o_fox, lse, do, do_t, f"fox_bwd_{tag}",
                                      do_col=(W_SB + W_CH) // FOX_COLS)
    dk_c, dv_c = _keys_major(dk_c), _keys_major(dv_c)
    df = forget_cumsum_bwd(_frow_from_groups(dfrow), f, f"fcum_bwd_{tag}")
    dqkv = jnp.concatenate([p.astype(BF16) for p in
                            (dq_a, dk_a, dv_a, dq_b, dk_b, dv_b, dq_c, dk_c, dv_c)], axis=1)
    dtab = rel_bias_scatter(dbias, f"rel_scatter_{tag}")

    all_rows = pl.BlockSpec((t, D_MODEL), lambda i, j, k: (0, 0))
    wide_b = pl.BlockSpec((t, D_MODEL), lambda i, j, k: (0, j))
    wide_o = pl.BlockSpec((D_MODEL, D_MODEL), lambda i, j, k: (0, j))
    wide_cs = pl.BlockSpec((1, D_MODEL), lambda i, j, k: (0, j))
    dwqkv, dbq = matmul(TN, hm, dqkv, _sds((D_MODEL, QKV_WIDTH), BF16), (1, 3, 1), all_rows, wide_b,
                        wide_o, None, name=f"dwqkv_{tag}",
                        colsum_sds=_sds((1, QKV_WIDTH), F32), colsum_spec=wide_cs)
    dwgate, dbg = matmul(TN, hm, dgates, _sds((D_MODEL, 3 * D_MODEL), BF16), (1, 3, 1), all_rows,
                         wide_b, wide_o, None, name=f"dwgate_{tag}",
                         colsum_sds=_sds((1, 3 * D_MODEL), F32), colsum_spec=wide_cs)
    dwf, dbf = matmul(TN, hm, df, _sds((D_MODEL, LANES), BF16), (1, 1, 1), all_rows,
                      pl.BlockSpec((t, LANES), lambda i, j, k: (0, 0)),
                      pl.BlockSpec((D_MODEL, LANES), lambda i, j, k: (0, 0)), None,
                      name=f"dwf_{tag}", colsum_sds=_sds((1, LANES), F32),
                      colsum_spec=pl.BlockSpec((1, LANES), lambda i, j, k: (0, 0)))
    deps = on_grads(dict(dwqkv=dwqkv, dwgate=dwgate, dwf=dwf, dwbr=dwbr, dwout=dwout))
    wide_a = pl.BlockSpec((tm, QKV_WIDTH), lambda i, j, k: (i, 0))
    dhm = matmul(NT, dqkv, wqkv, _sds((t, D_MODEL), F32), (nt, 1, 1), wide_a,
                 pl.BlockSpec((None, D_MODEL, QKV_WIDTH), lambda i, j, k: (layer, 0, 0)), a_full,
                 None, name=f"dhm_qkv_{tag}", deps=deps)
    dx, dgain = mixer_dh_norm_bwd(dhm, dgates, wgate, layer + 1, df, wf, layer, x, gain, dy, tm,
                                  f"dhm_gate_{tag}")
    return dx, dict(dbq=dbq, dbg=dbg, dbf=dbf, dtab=dtab, dgain=dgain)


def _pack_small(pieces):
    flat = jnp.concatenate([p.reshape(-1).astype(F32) for p in pieces])
    flat = jnp.pad(flat, (0, SMALL_ROWS * LANES - flat.shape[0]))
    return flat.reshape(SMALL_ROWS, LANES)


def _unpack_small(packed, shapes):
    flat = packed.reshape(-1)
    out, pos = [], 0
    for shp in shapes:
        n = int(np.prod(shp))
        out.append(flat[pos:pos + n].reshape(shp))
        pos += n
    return out


def kernel(x, g_ffn1, w_ffn1_in, w_ffn1_out, g_mix, w_in, b_in, rel_bias, w_br_sb, w_br_ch, w_br_fox, w_out, g_ffn2, w_ffn2_in, w_ffn2_out, g_final, loss_target, m_g_ffn1, m_w_ffn1_in, m_w_ffn1_out, m_g_mix, m_w_in, m_b_in, m_rel_bias, m_w_br_sb, m_w_br_ch, m_w_br_fox, m_w_out, m_g_ffn2, m_w_ffn2_in, m_w_ffn2_out, m_g_final, v_g_ffn1, v_w_ffn1_in, v_w_ffn1_out, v_g_mix, v_w_in, v_b_in, v_rel_bias, v_w_br_sb, v_w_br_ch, v_w_br_fox, v_w_out, v_g_ffn2, v_w_ffn2_in, v_w_ffn2_out, v_g_final):
    t = x.shape[1]
    tm = min(512, t)
    xs = x[0]
    target = loss_target[0]
    f_lo, f_hi = QKV_WIDTH, QKV_WIDTH + N_HEADS_FOX

    def ffn_shards(w_in_, w_out_, l):
        return [w_in_[l:l + 1].astype(BF16), w_out_[l:l + 1].astype(BF16)]

    def mixer_shards(l):
        wl = w_in[l]
        return [jnp.stack([wl[:, :QKV_WIDTH], wl[:, f_hi:]]).astype(BF16),
                jnp.pad(wl[:, f_lo:f_hi], ((0, 0), (0, LANES - N_HEADS_FOX)))[None].astype(BF16),
                w_out[l:l + 1].astype(BF16),
                jnp.concatenate([w_br_sb[l], w_br_ch[l], w_br_fox[l]], axis=0)[None].astype(BF16)]

    gathers = {}
    gather_tokens = []

    def start_gather(shards, name):
        handle = gather_start(shards, name, deps=gather_tokens[-1:])
        gather_tokens.append(handle["token"])
        return handle

    def relay(handle, after):
        if "send2" not in handle:
            gather_relay(handle, after)

    relay_on = {("mix", 0, "qkv"): ("mix", 0, 1), ("mix", 0, "o_sb"): ("ffn2", 0, 0),
                ("ffn2", 0, "act"): ("ffn1", 1, 0), ("ffn1", 1, "act"): ("mix", 1, 0),
                ("mix", 1, "qkv"): ("ffn2", 1, 0)}

    def on_event(grp, l):
        def fire(event, array):
            target = relay_on.get((grp, l, event))
            if target is None:
                return ()
            handle = gathers[target[:2]][target[2]]
            relay(handle, array)
            return (handle["relay_token"],)
        return fire

    for l in range(DEPTH):
        for grp, shards in (("ffn1", ffn_shards(w_ffn1_in, w_ffn1_out, l)), ("mix", mixer_shards(l)),
                            ("ffn2", ffn_shards(w_ffn2_in, w_ffn2_out, l))):
            cut = len(shards) // 2
            if l == 0 and grp != "ffn2":
                gathers[(grp, l)] = (start_gather(shards[:cut], f"gather_{grp}_l{l}_a"),
                                     start_gather(shards[cut:], f"gather_{grp}_l{l}_b"))
            else:
                gathers[(grp, l)] = (start_gather(shards, f"gather_{grp}_l{l}"),)

    def gathered(key, after):
        hs = gathers[key]
        cut = hs[0]["n"]
        relay(hs[0], after)
        first = gather_finish(hs[0], after)
        if len(hs) == 1:
            return first[:cut // 2], lambda later: first[cut // 2:]

        def second(later):
            relay(hs[1], later)
            return gather_finish(hs[1], later)

        return first, second

    def ffn_weights(key, after):
        (wa_,), rest = gathered(key, after)
        return wa_, lambda later: rest(later)[0].reshape(1, 4, FF_BLK, D_MODEL)

    def mixer_weights(key, after):
        (wc_, wf_), rest = gathered(key, after)

        def late(later):
            wout_, wbr_ = rest(later)
            return (wbr_.transpose(0, 2, 1, 3).reshape(1, D_MODEL, D_MODEL), wout_.reshape(1, D_MODEL, D_MODEL))

        return wc_.reshape(2, D_MODEL, QKV_WIDTH), wf_.reshape(1, D_MODEL, LANES), late

    bq = b_in[:, None, :QKV_WIDTH]
    bf = jnp.pad(b_in[:, f_lo:f_hi], ((0, 0), (0, LANES - N_HEADS_FOX)))[:, None, :]
    bg = b_in[:, None, f_hi:]
    tab_t = jnp.pad(rel_bias.transpose(0, 2, 1), ((0, 0), (0, 0), (0, REL_PAD - N_REL)))

    h = xs
    saved = []
    weights = []
    for l in range(DEPTH):
        bias = rel_bias_build(tab_t[l], f"rel_build_l{l}").reshape(N_HEADS_CH, QB, CH_KEYS)
        x0 = h
        wa1, wb1_after = ffn_weights(("ffn1", l), x0)
        x1, s1, wb1 = _ffn_fwd(x0, g_ffn1[l:l + 1], wa1, wb1_after, 0, tm, f"ffn1_l{l}", on_event("ffn1", l),
                               deps=gather_tokens if l == 0 else ())
        wc, wf, late_after = mixer_weights(("mix", l), x1)
        x2, sm, wbr, wout = _mixer_fwd(x1, g_mix[l:l + 1], wc, wf, wc, late_after, bq[l], bf[l], bg[l],
                                       bias, 0, tm, f"mix_l{l}", on_event("mix", l))
        wa2, wb2_after = ffn_weights(("ffn2", l), x2)
        x3, s2, wb2 = _ffn_fwd(x2, g_ffn2[l:l + 1], wa2, wb2_after, 0, tm, f"ffn2_l{l}", on_event("ffn2", l))
        saved.append((x0, x1, x2, s1, sm, s2, bias))
        weights.append(((wa1, wb1), (wc, wf, wout, wbr), (wa2, wb2)))
        h = x3

    dx, dg_final, loss_blk = loss_head(h, g_final[None, :], target, tm, "loss_head")

    g_mix_l = [None] * DEPTH
    dgains = {}
    scatters = {}

    def scatter_ffn(key):
        def on_grads(dwa, dwb):
            scatters[key] = exchange_start(
                "scatter", [dwa[None], dwb.reshape(1, N_DEV, D_FF // N_DEV, D_MODEL)],
                f"scatter_{key[0]}_l{key[1]}")
            return (scatters[key]["token"],)
        return on_grads

    def scatter_mixer(key):
        def on_grads(gm):
            scatters[key] = exchange_start(
                "scatter",
                [gm["dwqkv"].reshape(1, N_DEV, LANES, QKV_WIDTH), gm["dwgate"].reshape(1, N_DEV, LANES, QKV_WIDTH),
                 gm["dwf"].reshape(1, N_DEV, LANES, LANES), gm["dwout"].reshape(1, N_DEV, LANES, D_MODEL),
                 gm["dwbr"].reshape(1, D_MODEL, N_DEV, LANES).transpose(0, 2, 1, 3)],
                f"scatter_{key[0]}_l{key[1]}")
            return (scatters[key]["token"],)
        return on_grads

    for l in reversed(range(DEPTH)):
        x0, x1, x2, s1, sm, s2, bias = saved[l]
        w1, (wc, wf, wout, wbr), w2 = weights[l]
        dx, dgains[("ffn2", l)] = _ffn_bwd(dx, x2, g_ffn2[l:l + 1], s2, *w2, 0, tm, f"ffn2_l{l}",
                                           scatter_ffn(("ffn2", l)))
        dx, g_mix_l[l] = _mixer_bwd(dx, x1, g_mix[l:l + 1], sm, wc, wf, wc, wbr, wout, bias, 0, tm,
                                    f"mix_l{l}", scatter_mixer(("mix", l)))
        dx, dgains[("ffn1", l)] = _ffn_bwd(dx, x0, g_ffn1[l:l + 1], s1, *w1, 0, tm, f"ffn1_l{l}",
                                           scatter_ffn(("ffn1", l)))

    small_shapes = []
    small_pieces = []
    small_w, small_m, small_v = [], [], []

    def add_small(piece, w, m, v):
        small_shapes.append(w.shape)
        small_pieces.append(piece)
        small_w.append(w); small_m.append(m); small_v.append(v)

    dg1 = jnp.concatenate([dgains[("ffn1", l)] for l in range(DEPTH)], axis=0)
    dgm = jnp.concatenate([g_mix_l[l]["dgain"] for l in range(DEPTH)], axis=0)
    dg2 = jnp.concatenate([dgains[("ffn2", l)] for l in range(DEPTH)], axis=0)
    db = jnp.stack([jnp.concatenate([g_mix_l[l]["dbq"][0], g_mix_l[l]["dbf"][0, :N_HEADS_FOX],
                                     g_mix_l[l]["dbg"][0]]) for l in range(DEPTH)])
    drel = jnp.stack([g_mix_l[l]["dtab"][:, :N_REL].T for l in range(DEPTH)])
    add_small(dg1, g_ffn1, m_g_ffn1, v_g_ffn1)
    add_small(dgm, g_mix, m_g_mix, v_g_mix)
    add_small(db, b_in, m_b_in, v_b_in)
    add_small(drel, rel_bias, m_rel_bias, v_rel_bias)
    add_small(dg2, g_ffn2, m_g_ffn2, v_g_ffn2)
    add_small(dg_final[0], g_final, m_g_final, v_g_final)
    loss_piece = loss_blk[0, 0:1]
    small_packed = _pack_small(small_pieces + [loss_piece])

    recv = {}
    last = ("ffn1", 0)
    for l in reversed(range(DEPTH)):
        for grp in ("ffn2", "mix", "ffn1"):
            if (grp, l) != last:
                recv[(grp, l)] = exchange_wait(scatters[(grp, l)], dx, f"scattered_{grp}_l{l}")

    def upd(parts, w, m, v, tr, name, rb0=0):
        _, r, c = w.shape
        nr = r // tr

        def p_spec(layer):
            pinned = (nr - 1) if layer == 0 else 0
            return pl.BlockSpec((N_DEV, None, tr, c),
                                lambda l, i: (0, 0, rb0 + jnp.where(l == layer, i, pinned), 0))

        return adamw(parts, w, m, v, (DEPTH, nr), [p_spec(0), p_spec(1)],
                     pl.BlockSpec((None, tr, c), lambda l, i: (l, i, 0)), name)

    def both(grp, k):
        return [recv[(grp, l)][k] for l in range(DEPTH)]

    out_rows = D_FF // N_DEV // 2
    def upd_transposed(parts, w, m, v, tr, name):
        tp = lambda a: jnp.transpose(a, (0, 2, 1))
        return [tp(o) for o in upd(parts, tp(w), tp(m), tp(v), tr, name)]

    in_rows = FF_BLK // 4
    r_ffn2_in = upd_transposed(both("ffn2", 0), w_ffn2_in, m_w_ffn2_in, v_w_ffn2_in, in_rows, "adamw_ffn2_in")
    r_ffn2_out = upd(both("ffn2", 1), w_ffn2_out, m_w_ffn2_out, v_w_ffn2_out, out_rows, "adamw_ffn2_out")
    r_out = upd(both("mix", 3), w_out, m_w_out, v_w_out, LANES, "adamw_w_out")
    r_br_sb = upd(both("mix", 4), w_br_sb, m_w_br_sb, v_w_br_sb, 256, "adamw_br_sb", rb0=0)
    r_br_ch = upd(both("mix", 4), w_br_ch, m_w_br_ch, v_w_br_ch, 256, "adamw_br_ch", rb0=1)
    r_br_fox = upd(both("mix", 4), w_br_fox, m_w_br_fox, v_w_br_fox, 256, "adamw_br_fox", rb0=3)

    def w_in_grad(l):
        pieces = [recv[("mix", l)][k] for k in (0, 2, 1)]
        gq, gf, gg = sum_parts_multi(pieces, f"sum_w_in_l{l}")
        return jnp.concatenate([gq, gf[:, :N_HEADS_FOX], gg], axis=1)

    g_w_in = jnp.stack([w_in_grad(l) for l in range(DEPTH)])
    to_cols = lambda a: jnp.transpose(a, (2, 0, 1))
    n_cols = w_in.shape[2]
    col_blk = n_cols // 4
    win_spec = pl.BlockSpec((col_blk, DEPTH, LANES), lambda i: (i, 0, 0))
    r_in = adamw([to_cols(g_w_in)[None]], to_cols(w_in), to_cols(m_w_in), to_cols(v_w_in), (4,),
                 [pl.BlockSpec((1, col_blk, DEPTH, LANES), lambda i: (0, i, 0, 0))], win_spec, "adamw_w_in")
    r_in = [jnp.transpose(o, (1, 2, 0)) for o in r_in]

    recv[last] = exchange_wait(scatters[last], r_in[1], "scattered_ffn1_l0")
    r_ffn1_in = upd_transposed(both("ffn1", 0), w_ffn1_in, m_w_ffn1_in, v_w_ffn1_in, in_rows, "adamw_ffn1_in")
    r_ffn1_out = upd(both("ffn1", 1), w_ffn1_out, m_w_ffn1_out, v_w_ffn1_out, out_rows, "adamw_ffn1_out")

    small_sum = all_reduce_small(small_packed, "allreduce_small", deps=(r_ffn1_out[1],))
    n_small = sum(int(np.prod(s)) for s in small_shapes)
    loss = small_sum.reshape(-1)[n_small]
    sm_spec = pl.BlockSpec((SMALL_ROWS, LANES), lambda i: (0, 0))
    sm_out = adamw([small_sum[None]], _pack_small(small_w), _pack_small(small_m), _pack_small(small_v),
                   (1,), [pl.BlockSpec((1, SMALL_ROWS, LANES), lambda i: (0, 0, 0))], sm_spec, "adamw_small")
    sm_g, sm_d, sm_m, sm_v = [_unpack_small(o, small_shapes) for o in sm_out]

    def per_kind(k):
        small = (sm_g, sm_d, sm_m, sm_v)[k]
        return [small[0], r_ffn1_in[k], r_ffn1_out[k], small[1], r_in[k], small[2], small[3],
                r_br_sb[k], r_br_ch[k], r_br_fox[k], r_out[k], small[4], r_ffn2_in[k], r_ffn2_out[k],
                small[5]]

    return (loss, dx[None], *per_kind(0), *per_kind(1), *per_kind(2), *per_kind(3))
```

```python
import numpy as np
import jax
import jax.numpy as jnp
from jax import lax
from jax.experimental import pallas as pl
from jax.experimental.pallas import tpu as pltpu

F32 = jnp.float32
BF16 = jnp.bfloat16

N_DEV = 8
D_MODEL = 1024
DEPTH = 2
HEAD_DIM = 64
W_SB, W_CH, W_FOX = 256, 512, 256
QKV_WIDTH = 3 * (W_SB + W_CH + W_FOX)
N_HEADS_FOX = 4
N_HEADS_CH = 8
D_FF = 2816
FF_BLK = 2 * D_FF // N_DEV
CHUNK = 64
LEFT_CHUNKS = 8
MAX_REL = 128
N_REL = 2 * MAX_REL + 1
REL_PAD = 384
QB = 128
KB = 512
KSUB = KB // QB
CH_WIN = 5
CH_KEYS = CH_WIN * QB
RMS_EPS = 1e-6
NEG = -1e30
SCALE = HEAD_DIM ** -0.5
LANES = 128
VMEM_LIMIT = 56 * 1024 * 1024

ADAM_LR, ADAM_B1, ADAM_B2, ADAM_EPS, ADAM_WD, ADAM_STEP = 0.001, 0.9, 0.999, 1e-08, 0.01, 10

SMALL_ROWS = 192

MESH = pl.DeviceIdType.MESH
ANY = pl.BlockSpec(memory_space=pl.ANY)
HIGHEST = lax.Precision.HIGHEST

NN = (((1,), (0,)), ((), ()))
NT = (((1,), (1,)), ((), ()))
TN = (((0,), (0,)), ((), ()))


def _cparams(n_grid):
    return pltpu.CompilerParams(dimension_semantics=("arbitrary",) * n_grid,
                                vmem_limit_bytes=VMEM_LIMIT)


def _sds(shape, dtype):
    return jax.ShapeDtypeStruct(tuple(shape), dtype)


def _my_index():
    return 4 * lax.axis_index("x") + 2 * lax.axis_index("y") + lax.axis_index("c")


def _peer(mask):
    x, y, c = lax.axis_index("x"), lax.axis_index("y"), lax.axis_index("c")
    px = x ^ ((mask >> 2) & 1)
    py = y ^ ((mask >> 1) & 1)
    pc = c ^ (mask & 1)
    return (px, py, pc), 4 * px + 2 * py + pc


HBM_SPEC = pl.BlockSpec(memory_space=pltpu.HBM)
SEM_SPEC = pl.BlockSpec(memory_space=pltpu.SEMAPHORE)
EFFECT = pltpu.SideEffectType.DATAFLOW_SIDE_EFFECTING


def _exchange_refs(mode, in_ref, land_ref, me, pidx):
    if mode == "gather":
        return in_ref, land_ref.at[:, me], land_ref.at[:, pidx]
    return in_ref.at[:, pidx], land_ref.at[me], land_ref.at[pidx]


def _landing_shape(mode, a):
    if mode == "gather":
        s, r, c = a.shape
        return (s, N_DEV, r, c)
    s, _, r, c = a.shape
    return (N_DEV, s, r, c)


def _own_copy(mode, in_ref, land_ref, me, sem):
    if mode == "gather":
        return pltpu.make_async_copy(in_ref, land_ref.at[:, me], sem)
    return pltpu.make_async_copy(in_ref.at[:, me], land_ref.at[me], sem)


def exchange_start(mode, arrays, name, deps=()):
    n = len(arrays)
    lands0 = [lax.empty(_landing_shape(mode, a), a.dtype) for a in arrays]

    def body(*refs):
        in_refs, land_refs = refs[:n], refs[n:2 * n]
        outs_at = 2 * n + len(deps)
        send_sems, recv_sems, own_sems, token = refs[outs_at], refs[outs_at + 1], refs[outs_at + 2], refs[-1]
        mine = _my_index()
        for k in range(n):
            _own_copy(mode, in_refs[k], land_refs[k], mine, own_sems.at[k]).start()
            for mask in range(1, N_DEV):
                peer, pidx = _peer(mask)
                src, dst, _ = _exchange_refs(mode, in_refs[k], land_refs[k], mine, pidx)
                sem = k * (N_DEV - 1) + mask - 1
                pltpu.make_async_remote_copy(
                    src_ref=src, dst_ref=dst, send_sem=send_sems.at[sem], recv_sem=recv_sems.at[sem],
                    device_id=peer, device_id_type=MESH).start()
        token[...] = jnp.zeros_like(token)

    nsem = n * (N_DEV - 1)
    outs = pl.pallas_call(
        body, name=name,
        out_shape=(pltpu.SemaphoreType.DMA((nsem,)), pltpu.SemaphoreType.DMA((nsem,)),
                   pltpu.SemaphoreType.DMA((n,)),
                   *[pltpu.HBM(a.shape, a.dtype) for a in arrays],
                   *[pltpu.HBM(l.shape, l.dtype) for l in lands0], _sds((8, LANES), F32)),
        in_specs=[HBM_SPEC] * (2 * n) + [ANY] * len(deps),
        out_specs=(SEM_SPEC, SEM_SPEC, SEM_SPEC, *[HBM_SPEC] * (2 * n),
                   pl.BlockSpec(memory_space=pltpu.VMEM)),
        input_output_aliases={k: 3 + k for k in range(2 * n)},
        compiler_params=pltpu.CompilerParams(has_side_effects=EFFECT),
    )(*[pltpu.with_memory_space_constraint(a, pltpu.HBM) for a in arrays],
      *[pltpu.with_memory_space_constraint(l, pltpu.HBM) for l in lands0], *deps)
    return dict(mode=mode, n=n, send=outs[0], recv=outs[1], own=outs[2], ins=outs[3:3 + n],
                lands=outs[3 + n:3 + 2 * n], token=outs[-1])


def exchange_wait(handle, after, name):
    n, mode = handle["n"], handle["mode"]

    def body(*refs):
        in_refs, land_refs = refs[:n], refs[n:2 * n]
        send_sems, recv_sems, own_sems = refs[2 * n], refs[2 * n + 1], refs[2 * n + 2]
        mine = _my_index()
        for k in range(n):
            _own_copy(mode, in_refs[k], land_refs[k], mine, own_sems.at[k]).wait()
            for mask in range(1, N_DEV):
                peer, pidx = _peer(mask)
                src, _, here = _exchange_refs(mode, in_refs[k], land_refs[k], mine, pidx)
                sem = k * (N_DEV - 1) + mask - 1
                cp = pltpu.make_async_remote_copy(
                    src_ref=src, dst_ref=here, send_sem=send_sems.at[sem], recv_sem=recv_sems.at[sem],
                    device_id=peer, device_id_type=MESH)
                cp.wait_send()
                cp.wait_recv()

    thru = (*handle["ins"], *handle["lands"])
    outs = pl.pallas_call(
        body, name=name,
        out_shape=tuple(pltpu.HBM(a.shape, a.dtype) for a in thru),
        in_specs=[HBM_SPEC] * (2 * n) + [SEM_SPEC, SEM_SPEC, SEM_SPEC, ANY],
        out_specs=tuple([HBM_SPEC] * (2 * n)),
        input_output_aliases={k: k for k in range(2 * n)},
        compiler_params=pltpu.CompilerParams(has_side_effects=EFFECT),
    )(*thru, handle["send"], handle["recv"], handle["own"], after)
    return list(outs[n:])


FAR_MASKS = (2, 4, 6)
PHASE1_MASKS = (1,) + FAR_MASKS


def gather_start(arrays, name, deps=()):
    n = len(arrays)
    n1 = len(PHASE1_MASKS)
    lands0 = [lax.empty(_landing_shape("gather", a), a.dtype) for a in arrays]

    def body(*refs):
        in_refs, land_refs = refs[:n], refs[n:2 * n]
        outs_at = 2 * n + len(deps)
        send_sems, recv_sems, own_sems, token = refs[outs_at], refs[outs_at + 1], refs[outs_at + 2], refs[-1]
        mine = _my_index()
        for k in range(n):
            _own_copy("gather", in_refs[k], land_refs[k], mine, own_sems.at[k]).start()
            for j, mask in enumerate(PHASE1_MASKS):
                peer, _ = _peer(mask)
                pltpu.make_async_remote_copy(
                    src_ref=in_refs[k], dst_ref=land_refs[k].at[:, mine],
                    send_sem=send_sems.at[k * n1 + j], recv_sem=recv_sems.at[k * n1 + j],
                    device_id=peer, device_id_type=MESH).start()
        token[...] = jnp.zeros_like(token)

    outs = pl.pallas_call(
        body, name=name,
        out_shape=(pltpu.SemaphoreType.DMA((n * n1,)), pltpu.SemaphoreType.DMA((n * n1,)),
                   pltpu.SemaphoreType.DMA((n,)),
                   *[pltpu.HBM(a.shape, a.dtype) for a in arrays],
                   *[pltpu.HBM(l.shape, l.dtype) for l in lands0], _sds((8, LANES), F32)),
        in_specs=[HBM_SPEC] * (2 * n) + [ANY] * len(deps),
        out_specs=(SEM_SPEC, SEM_SPEC, SEM_SPEC, *[HBM_SPEC] * (2 * n),
                   pl.BlockSpec(memory_space=pltpu.VMEM)),
        input_output_aliases={k: 3 + k for k in range(2 * n)},
        compiler_params=pltpu.CompilerParams(has_side_effects=EFFECT),
    )(*[pltpu.with_memory_space_constraint(a, pltpu.HBM) for a in arrays],
      *[pltpu.with_memory_space_constraint(l, pltpu.HBM) for l in lands0], *deps)
    return dict(n=n, send=outs[0], recv=outs[1], own=outs[2], ins=outs[3:3 + n],
                lands=outs[3 + n:3 + 2 * n], token=outs[-1], name=name)


def gather_relay(handle, after):
    n = handle["n"]
    n1, n2 = len(PHASE1_MASKS), len(FAR_MASKS)

    def body(*refs):
        in_refs, land_refs = refs[:n], refs[n:2 * n]
        send1, recv1 = refs[2 * n], refs[2 * n + 1]
        send2, recv2, token = refs[2 * n + 3], refs[2 * n + 4], refs[-1]
        token[...] = jnp.zeros_like(token)
        sibling, _ = _peer(1)
        for k in range(n):
            for j, mask in enumerate(FAR_MASKS):
                peer, pidx = _peer(mask)
                landed = land_refs[k].at[:, pidx]
                pltpu.make_async_remote_copy(
                    src_ref=in_refs[k], dst_ref=landed, send_sem=send1.at[k * n1 + 1 + j],
                    recv_sem=recv1.at[k * n1 + 1 + j], device_id=peer, device_id_type=MESH).wait_recv()
                pltpu.make_async_remote_copy(
                    src_ref=landed, dst_ref=landed, send_sem=send2.at[k * n2 + j],
                    recv_sem=recv2.at[k * n2 + j], device_id=sibling, device_id_type=MESH).start()

    thru = (*handle["ins"], *handle["lands"])
    outs = pl.pallas_call(
        body, name=handle["name"] + "_relay",
        out_shape=(pltpu.SemaphoreType.DMA((n * n2,)), pltpu.SemaphoreType.DMA((n * n2,)),
                   *[pltpu.HBM(a.shape, a.dtype) for a in thru], _sds((8, LANES), F32)),
        in_specs=[HBM_SPEC] * (2 * n) + [SEM_SPEC, SEM_SPEC, ANY],
        out_specs=(SEM_SPEC, SEM_SPEC, *[HBM_SPEC] * (2 * n), pl.BlockSpec(memory_space=pltpu.VMEM)),
        input_output_aliases={k: 2 + k for k in range(2 * n)},
        compiler_params=pltpu.CompilerParams(has_side_effects=EFFECT),
    )(*thru, handle["send"], handle["recv"], after)
    handle.update(send2=outs[0], recv2=outs[1], ins=outs[2:2 + n], lands=outs[2 + n:2 + 2 * n],
                  relay_token=outs[-1])


def gather_finish(handle, after):
    n = handle["n"]
    n1, n2 = len(PHASE1_MASKS), len(FAR_MASKS)

    def body(*refs):
        in_refs, land_refs = refs[:n], refs[n:2 * n]
        send1, recv1, own_sems, send2, recv2 = refs[2 * n:2 * n + 5]
        mine = _my_index()
        sibling, sib_idx = _peer(1)
        for k in range(n):
            _own_copy("gather", in_refs[k], land_refs[k], mine, own_sems.at[k]).wait()
            for j, mask in enumerate(PHASE1_MASKS):
                peer, pidx = _peer(mask)
                cp = pltpu.make_async_remote_copy(
                    src_ref=in_refs[k], dst_ref=land_refs[k].at[:, pidx], send_sem=send1.at[k * n1 + j],
                    recv_sem=recv1.at[k * n1 + j], device_id=peer, device_id_type=MESH)
                cp.wait_send()
                if mask == 1:
                    cp.wait_recv()
            for j, mask in enumerate(FAR_MASKS):
                _, pidx = _peer(mask)
                _, far_of_sibling = _peer(mask ^ 1)
                cp = pltpu.make_async_remote_copy(
                    src_ref=land_refs[k].at[:, pidx], dst_ref=land_refs[k].at[:, far_of_sibling],
                    send_sem=send2.at[k * n2 + j], recv_sem=recv2.at[k * n2 + j],
                    device_id=sibling, device_id_type=MESH)
                cp.wait_send()
                cp.wait_recv()

    thru = (*handle["ins"], *handle["lands"])
    outs = pl.pallas_call(
        body, name=handle["name"] + "_finish",
        out_shape=tuple(pltpu.HBM(a.shape, a.dtype) for a in thru),
        in_specs=[HBM_SPEC] * (2 * n) + [SEM_SPEC] * 5 + [ANY],
        out_specs=tuple([HBM_SPEC] * (2 * n)),
        input_output_aliases={k: k for k in range(2 * n)},
        compiler_params=pltpu.CompilerParams(has_side_effects=EFFECT),
    )(*thru, handle["send"], handle["recv"], handle["own"], handle["send2"], handle["recv2"], after)
    return list(outs[n:])


def all_reduce_small(packed, name, deps=()):
    rows = packed.shape[0]
    nd = len(deps)

    def body(in_ref, *rest):
        out_ref, slots, send_sems, recv_sems = rest[nd:]
        me = _my_index()
        sends = []
        for mask in range(1, N_DEV):
            peer, _ = _peer(mask)
            cp = pltpu.make_async_remote_copy(
                src_ref=in_ref, dst_ref=slots.at[me],
                send_sem=send_sems.at[mask - 1], recv_sem=recv_sems.at[mask - 1],
                device_id=peer, device_id_type=MESH)
            cp.start()
            sends.append(cp)
        slots[me] = in_ref[...]
        for mask in range(1, N_DEV):
            peer, pidx = _peer(mask)
            pltpu.make_async_remote_copy(
                src_ref=in_ref, dst_ref=slots.at[pidx],
                send_sem=send_sems.at[mask - 1], recv_sem=recv_sems.at[mask - 1],
                device_id=peer, device_id_type=MESH).wait_recv()
        for cp in sends:
            cp.wait_send()
        total = slots[0]
        for p in range(1, N_DEV):
            total = total + slots[p]
        out_ref[...] = total

    return pl.pallas_call(
        body, name=name,
        out_shape=_sds((rows, LANES), F32),
        in_specs=[pl.BlockSpec(memory_space=pltpu.VMEM)] + [ANY] * nd,
        out_specs=pl.BlockSpec(memory_space=pltpu.VMEM),
        scratch_shapes=[pltpu.VMEM((N_DEV, rows, LANES), F32),
                        pltpu.SemaphoreType.DMA((N_DEV - 1,)),
                        pltpu.SemaphoreType.DMA((N_DEV - 1,))],
    )(packed, *deps)


def matmul(dims, a, b, out_sds, grid, a_spec, b_spec, o_spec, acc_shape, *, name, alpha=1.0,
           bias=None, bias_spec=None, scale=None, scale_spec=None, res=None, res_spec=None,
           colsum_sds=None, colsum_spec=None, out_t_sds=None, out_t_spec=None, deps=()):
    nk = grid[2]
    has_bias, has_scale, has_res = bias is not None, scale is not None, res is not None
    has_cs, has_t = colsum_sds is not None, out_t_sds is not None
    if has_cs:
        assert grid[0] == 1 and dims == TN

    def body(*refs):
        a_ref, b_ref = refs[0], refs[1]
        pos = 2
        bias_ref = scale_ref = res_ref = cs_ref = ot_ref = None
        if has_bias:
            bias_ref = refs[pos]; pos += 1
        if has_scale:
            scale_ref = refs[pos]; pos += 1
        if has_res:
            res_ref = refs[pos]; pos += 1
        pos += len(deps)
        o_ref = refs[pos]; pos += 1
        if has_cs:
            cs_ref = refs[pos]; pos += 1
        if has_t:
            ot_ref = refs[pos]; pos += 1
        k = pl.program_id(2)
        bval = b_ref[...]
        part = lax.dot_general(a_ref[...].astype(BF16), bval.astype(BF16), dims,
                               preferred_element_type=F32)

        def finish(total):
            r = total * alpha if alpha != 1.0 else total
            if has_bias:
                r = r + bias_ref[...]
            if has_scale:
                r = r * scale_ref[...]
            if has_res:
                r = r + res_ref[...].astype(F32)
            o_ref[...] = r.astype(o_ref.dtype)
            if has_t:
                ot_ref[...] = r.T.astype(ot_ref.dtype)

        if has_cs:
            csum = jnp.sum(bval.astype(F32), axis=0, keepdims=True)

            @pl.when(k == 0)
            def _():
                cs_ref[...] = csum

            @pl.when(k > 0)
            def _():
                cs_ref[...] += csum

        if nk == 1:
            finish(part)
        else:
            acc_ref = refs[pos]

            @pl.when(k == 0)
            def _():
                acc_ref[...] = part

            @pl.when(k > 0)
            def _():
                acc_ref[...] += part

            @pl.when(k == nk - 1)
            def _():
                finish(acc_ref[...])

    in_specs, args = [a_spec, b_spec], [a, b]
    if has_bias:
        in_specs.append(bias_spec); args.append(bias)
    if has_scale:
        in_specs.append(scale_spec); args.append(scale)
    if has_res:
        in_specs.append(res_spec); args.append(res)
    in_specs += [ANY] * len(deps)
    args += list(deps)
    out_shape, out_specs = [out_sds], [o_spec]
    if has_cs:
        out_shape.append(colsum_sds); out_specs.append(colsum_spec)
    if has_t:
        out_shape.append(out_t_sds); out_specs.append(out_t_spec)
    scratch = [] if nk == 1 else [pltpu.VMEM(acc_shape, F32)]
    outs = pl.pallas_call(
        body, name=name, grid=grid, in_specs=in_specs, out_specs=out_specs, out_shape=out_shape,
        scratch_shapes=scratch, compiler_params=_cparams(3))(*args)
    return outs if (has_cs or has_t) else outs[0]


def _sigmoid(z):
    return 1.0 / (1.0 + jnp.exp(-z))


def _log_sigmoid(z):
    return jnp.minimum(z, 0.0) - jnp.log(1.0 + jnp.exp(-jnp.abs(z)))


def rmsnorm_fwd(x, gain, tm, name, deps=()):
    t, d = x.shape

    def body(x_ref, g_ref, *rest):
        o_ref = rest[-1]
        xf = x_ref[...]
        r = lax.rsqrt(jnp.mean(xf * xf, axis=-1, keepdims=True) + RMS_EPS)
        o_ref[...] = (xf * r * g_ref[...]).astype(o_ref.dtype)

    return pl.pallas_call(
        body, name=name, grid=(t // tm,),
        in_specs=[pl.BlockSpec((tm, d), lambda i: (i, 0)), pl.BlockSpec((1, d), lambda i: (0, 0))]
        + [ANY] * len(deps),
        out_specs=pl.BlockSpec((tm, d), lambda i: (i, 0)),
        out_shape=_sds((t, d), BF16), compiler_params=_cparams(1))(x, gain, *deps)


def loss_head(x, gain, target, tm, name):
    t, d = x.shape

    def body(x_ref, g_ref, tgt_ref, dx_ref, dg_ref, loss_ref):
        i = pl.program_id(0)
        xf = x_ref[...]
        g = g_ref[...]
        r = lax.rsqrt(jnp.mean(xf * xf, axis=-1, keepdims=True) + RMS_EPS)
        xhat = xf * r
        err = xhat * g - tgt_ref[...]
        part = 0.5 * jnp.sum(jnp.mean(err * err, axis=-1, keepdims=True))
        dy = err * (1.0 / d)
        dxhat = dy * g
        dx_ref[...] = r * (dxhat - xhat * jnp.mean(dxhat * xhat, axis=-1, keepdims=True))
        dg = jnp.sum(dy * xhat, axis=0, keepdims=True)
        lpart = jnp.full((8, LANES), part, F32)

        @pl.when(i == 0)
        def _():
            dg_ref[...] = dg
            loss_ref[...] = lpart

        @pl.when(i > 0)
        def _():
            dg_ref[...] += dg
            loss_ref[...] += lpart

    row = pl.BlockSpec((tm, d), lambda i: (i, 0))
    vec = pl.BlockSpec((1, d), lambda i: (0, 0))
    return pl.pallas_call(
        body, name=name, grid=(t // tm,), in_specs=[row, vec, row],
        out_specs=[row, vec, pl.BlockSpec((8, LANES), lambda i: (0, 0))],
        out_shape=[_sds((t, d), F32), _sds((1, d), F32), _sds((8, LANES), F32)],
        compiler_params=_cparams(1))(x, gain, target)


def ffn_in_swiglu(hn, wa, s, tm, name):
    t = hn.shape[0]
    halves = 2 if tm % 512 == 0 else 1
    rows = tm // halves

    def body(h_ref, wg_ref, wu_ref, gu_ref, act_ref):
        for c in range(halves):
            rs = slice(c * rows, (c + 1) * rows)
            h = h_ref[rs, :]
            g = jnp.dot(h, wg_ref[...], preferred_element_type=F32)
            u = jnp.dot(h, wu_ref[...], preferred_element_type=F32)
            gu_ref[0, rs, :] = g.astype(gu_ref.dtype)
            gu_ref[1, rs, :] = u.astype(gu_ref.dtype)
            act_ref[rs, :] = (g * _sigmoid(g) * u).astype(act_ref.dtype)

    return pl.pallas_call(
        body, name=name, grid=(t // tm, 4),
        in_specs=[pl.BlockSpec((tm, D_MODEL), lambda i, j: (i, 0)),
                  pl.BlockSpec((None, None, D_MODEL, FF_BLK), lambda i, j: (s, j, 0, 0)),
                  pl.BlockSpec((None, None, D_MODEL, FF_BLK), lambda i, j: (s, j + 4, 0, 0))],
        out_specs=[pl.BlockSpec((None, 2, tm, FF_BLK), lambda i, j: (j, 0, i, 0)),
                   pl.BlockSpec((None, tm, FF_BLK), lambda i, j: (j, i, 0))],
        out_shape=[_sds((4, 2, t, FF_BLK), BF16), _sds((4, t, FF_BLK), BF16)],
        compiler_params=_cparams(2))(hn, wa, wa)


def ffn_dact_swiglu(dy, wb, gu, s, tm, name):
    t = dy.shape[0]

    def body(dy_ref, w_ref, gu_ref, o_ref):
        da = 0.5 * lax.dot_general(dy_ref[...].astype(BF16), w_ref[...], NT, preferred_element_type=F32)
        g = gu_ref[0].astype(F32)
        u = gu_ref[1].astype(F32)
        sg = _sigmoid(g)
        o_ref[0] = (da * u * (sg * (1.0 + g * (1.0 - sg)))).astype(o_ref.dtype)
        o_ref[1] = (da * g * sg).astype(o_ref.dtype)

    blk = pl.BlockSpec((None, 2, tm, FF_BLK), lambda i, j: (j, 0, i, 0))
    return pl.pallas_call(
        body, name=name, grid=(t // tm, 4),
        in_specs=[pl.BlockSpec((tm, D_MODEL), lambda i, j: (i, 0)),
                  pl.BlockSpec((None, None, FF_BLK, D_MODEL), lambda i, j: (s, j, 0, 0)), blk],
        out_specs=blk, out_shape=_sds((4, 2, t, FF_BLK), BF16),
        compiler_params=_cparams(2))(dy, wb, gu)


def ffn_out_residual(act, wb, x, s, tm, name, deps=()):
    t = x.shape[0]

    def body(a_ref, w_ref, x_ref, *rest):
        o_ref = rest[-1]
        acc = jnp.dot(a_ref[0], w_ref[0], preferred_element_type=F32)
        for k in range(1, 4):
            acc = acc + jnp.dot(a_ref[k], w_ref[k], preferred_element_type=F32)
        o_ref[...] = x_ref[...] + 0.5 * acc

    row = pl.BlockSpec((tm, D_MODEL), lambda i: (i, 0))
    return pl.pallas_call(
        body, name=name, grid=(t // tm,),
        in_specs=[pl.BlockSpec((4, tm, FF_BLK), lambda i: (0, i, 0)),
                  pl.BlockSpec((None, 4, FF_BLK, D_MODEL), lambda i: (s, 0, 0, 0)), row] + [ANY] * len(deps),
        out_specs=row, out_shape=_sds((t, D_MODEL), F32), compiler_params=_cparams(1))(act, wb, x, *deps)


def ffn_dh_norm_bwd(dgu, wa, s, x, gain, dres, tm, name, deps):
    t = dgu.shape[2]
    nd = len(deps)

    def body(g_ref, w_ref, x_ref, gain_ref, dres_ref, *rest):
        dx_ref, dg_ref = rest[nd:]
        i = pl.program_id(0)
        dh = lax.dot_general(g_ref[0, 0], w_ref[0], NT, preferred_element_type=F32)
        for p in range(1, N_DEV):
            dh = dh + lax.dot_general(g_ref[p % 4, p // 4], w_ref[p], NT, preferred_element_type=F32)
        xf = x_ref[...]
        r = lax.rsqrt(jnp.mean(xf * xf, axis=-1, keepdims=True) + RMS_EPS)
        xhat = xf * r
        dxhat = dh * gain_ref[...]
        dx_ref[...] = dres_ref[...] + r * (dxhat - xhat * jnp.mean(dxhat * xhat, axis=-1, keepdims=True))
        dg = jnp.sum(dh * xhat, axis=0, keepdims=True)

        @pl.when(i == 0)
        def _():
            dg_ref[...] = dg

        @pl.when(i > 0)
        def _():
            dg_ref[...] += dg

    row = pl.BlockSpec((tm, D_MODEL), lambda i: (i, 0))
    vec = pl.BlockSpec((1, D_MODEL), lambda i: (0, 0))
    return pl.pallas_call(
        body, name=name, grid=(t // tm,),
        in_specs=[pl.BlockSpec((4, 2, tm, FF_BLK), lambda i: (0, 0, i, 0)),
                  pl.BlockSpec((None, N_DEV, D_MODEL, FF_BLK), lambda i: (s, 0, 0, 0)), row, vec, row]
        + [ANY] * nd,
        out_specs=[row, vec], out_shape=[_sds((t, D_MODEL), F32), _sds((1, D_MODEL), F32)],
        compiler_params=_cparams(1))(dgu, wa, x, gain, dres, *deps)


def branch_merge(os_, wbr, layer, gates, tm, name, deps=()):
    t = os_[0].shape[0]
    d = D_MODEL
    nd = len(deps)
    widths = [o.shape[1] for o in os_]
    starts = [sum(widths[:a]) for a in range(3)]

    def body(oa_ref, ob_ref, oc_ref, w_ref, g_ref, *rest):
        ya_ref, yb_ref, yc_ref, m_ref = rest[nd:]
        merged = None
        for a, (o_ref, y_ref) in enumerate(((oa_ref, ya_ref), (ob_ref, yb_ref), (oc_ref, yc_ref))):
            y = jnp.dot(o_ref[...], w_ref[starts[a]:starts[a] + widths[a], :], preferred_element_type=F32)
            y_ref[...] = y
            term = _sigmoid(g_ref[:, a * d:(a + 1) * d]) * y
            merged = term if merged is None else merged + term
        m_ref[...] = merged.astype(m_ref.dtype)

    row = pl.BlockSpec((tm, d), lambda i: (i, 0))
    ya, yb, yc, merged = pl.pallas_call(
        body, name=name, grid=(t // tm,),
        in_specs=[pl.BlockSpec((tm, w), lambda i: (i, 0)) for w in widths]
        + [pl.BlockSpec((None, d, d), lambda i: (layer, 0, 0)), pl.BlockSpec((tm, 3 * d), lambda i: (i, 0))]
        + [ANY] * nd,
        out_specs=[row, row, row, row],
        out_shape=[_sds((t, d), F32)] * 3 + [_sds((t, d), BF16)],
        compiler_params=_cparams(1))(*os_, wbr, gates, *deps)
    return [ya, yb, yc], merged


def dmerged_merge_bwd(dy, wout, layer, gates, ys, tm, name):
    t, d = dy.shape

    def body(dy_ref, w_ref, g_ref, ya_ref, yb_ref, yc_ref, dg_ref, dya_ref, dyb_ref, dyc_ref):
        dmv = lax.dot_general(dy_ref[...].astype(BF16), w_ref[...], NT, preferred_element_type=F32)
        for a, (y_ref, dy_out) in enumerate(((ya_ref, dya_ref), (yb_ref, dyb_ref), (yc_ref, dyc_ref))):
            cols = slice(a * d, (a + 1) * d)
            s = _sigmoid(g_ref[:, cols])
            dy_out[...] = (dmv * s).astype(dy_out.dtype)
            dg_ref[:, cols] = (dmv * y_ref[...] * s * (1.0 - s)).astype(dg_ref.dtype)

    row = pl.BlockSpec((tm, d), lambda i: (i, 0))
    wide = pl.BlockSpec((tm, 3 * d), lambda i: (i, 0))
    dg, dya, dyb, dyc = pl.pallas_call(
        body, name=name, grid=(t // tm,),
        in_specs=[row, pl.BlockSpec((None, d, d), lambda i: (layer, 0, 0)), wide, row, row, row],
        out_specs=[wide, row, row, row],
        out_shape=[_sds((t, 3 * d), BF16)] + [_sds((t, d), BF16)] * 3,
        compiler_params=_cparams(1))(dy, wout, gates, *ys)
    return dg, [dya, dyb, dyc]


def branch_bwd(dys, os_, wbr, layer, tm, name):
    t = dys[0].shape[0]
    d = D_MODEL
    nt = t // tm
    widths = [o.shape[1] for o in os_]
    starts = [sum(widths[:a]) for a in range(3)]

    def body(dya_ref, dyb_ref, dyc_ref, oa_ref, ob_ref, oc_ref, w_ref, do_ref, dot_ref, dw_ref, acc_ref):
        i = pl.program_id(0)
        for a, (dy_ref, o_ref) in enumerate(((dya_ref, oa_ref), (dyb_ref, ob_ref), (dyc_ref, oc_ref))):
            rows = slice(starts[a], starts[a] + widths[a])
            dyv = dy_ref[...]
            do = lax.dot_general(dyv, w_ref[rows, :], NT, preferred_element_type=F32)
            do_ref[:, rows] = do.astype(do_ref.dtype)
            dot_ref[rows, :] = do.T.astype(dot_ref.dtype)
            dw = lax.dot_general(o_ref[...], dyv, TN, preferred_element_type=F32)

            @pl.when(i == 0)
            def _():
                acc_ref[rows, :] = dw

            @pl.when(i > 0)
            def _():
                acc_ref[rows, :] += dw

        @pl.when(i == nt - 1)
        def _():
            dw_ref[...] = acc_ref[...].astype(dw_ref.dtype)

    row = pl.BlockSpec((tm, d), lambda i: (i, 0))
    return pl.pallas_call(
        body, name=name, grid=(nt,),
        in_specs=[row, row, row] + [pl.BlockSpec((tm, w), lambda i: (i, 0)) for w in widths]
        + [pl.BlockSpec((None, d, d), lambda i: (layer, 0, 0))],
        out_specs=[row, pl.BlockSpec((d, tm), lambda i: (0, i)), pl.BlockSpec((d, d), lambda i: (0, 0))],
        out_shape=[_sds((t, d), BF16), _sds((d, t), BF16), _sds((d, d), BF16)],
        scratch_shapes=[pltpu.VMEM((d, d), F32)], compiler_params=_cparams(1))(*dys, *os_, wbr)


def mixer_dh_norm_bwd(dh_part, dgates, wc, gate_idx, df, wf, layer, x, gain, dres, tm, name):
    t, d = x.shape

    def body(dhp_ref, dg_ref, wg_ref, df_ref, wf_ref, x_ref, gain_ref, dres_ref, dx_ref, dgain_ref):
        i = pl.program_id(0)
        dh = (dhp_ref[...]
              + lax.dot_general(dg_ref[...], wg_ref[...], NT, preferred_element_type=F32)
              + lax.dot_general(df_ref[...].astype(BF16), wf_ref[...], NT, preferred_element_type=F32))
        xf = x_ref[...]
        r = lax.rsqrt(jnp.mean(xf * xf, axis=-1, keepdims=True) + RMS_EPS)
        xhat = xf * r
        dxhat = dh * gain_ref[...]
        dx_ref[...] = dres_ref[...] + r * (dxhat - xhat * jnp.mean(dxhat * xhat, axis=-1, keepdims=True))
        dg = jnp.sum(dh * xhat, axis=0, keepdims=True)

        @pl.when(i == 0)
        def _():
            dgain_ref[...] = dg

        @pl.when(i > 0)
        def _():
            dgain_ref[...] += dg

    row = pl.BlockSpec((tm, d), lambda i: (i, 0))
    vec = pl.BlockSpec((1, d), lambda i: (0, 0))
    return pl.pallas_call(
        body, name=name, grid=(t // tm,),
        in_specs=[row, pl.BlockSpec((tm, QKV_WIDTH), lambda i: (i, 0)),
                  pl.BlockSpec((None, d, QKV_WIDTH), lambda i: (gate_idx, 0, 0)),
                  pl.BlockSpec((tm, LANES), lambda i: (i, 0)),
                  pl.BlockSpec((None, d, LANES), lambda i: (layer, 0, 0)), row, vec, row],
        out_specs=[row, vec], out_shape=[_sds((t, d), F32), _sds((1, d), F32)],
        compiler_params=_cparams(1))(dh_part, dgates, wc, df, wf, x, gain, dres)


def _iota2(shape, dim):
    return lax.broadcasted_iota(jnp.int32, shape, dim)


def proj_gates_forget(hm, wc, gate_idx, wf, layer, bg, bf, tm, name, deps=()):
    t, d = hm.shape
    sub = tm // QB
    nd = len(deps)

    def body(h_ref, wg_ref, bg_ref, wf_ref, bf_ref, *rest):
        g_ref, f_ref, fcol_ref, frow_ref, carry = rest[nd:]
        i, j = pl.program_id(0), pl.program_id(1)
        h = h_ref[...]
        g_ref[...] = jnp.dot(h, wg_ref[...], preferred_element_type=F32) + bg_ref[...]

        @pl.when((i == 0) & (j == 0))
        def _():
            carry[...] = jnp.zeros_like(carry)

        @pl.when(j == 0)
        def _():
            f = jnp.dot(h, wf_ref[...], preferred_element_type=F32) + bf_ref[...]
            f_ref[...] = f
            logf = _log_sigmoid(f)
            tri = (_iota2((QB, QB), 1) <= _iota2((QB, QB), 0)).astype(F32)
            for s in range(sub):
                rows = slice(s * QB, (s + 1) * QB)
                part = logf[rows, :]
                blk = jnp.dot(tri, part, precision=HIGHEST, preferred_element_type=F32) + carry[...]
                carry[...] += jnp.sum(part, axis=0, keepdims=True)
                fcol_ref[rows, :] = blk
                frow_ref[s] = blk.T[0:8, :]

    narrow = pl.BlockSpec((tm, LANES), lambda i, j: (i, 0))
    return pl.pallas_call(
        body, name=name, grid=(t // tm, 3),
        in_specs=[pl.BlockSpec((tm, d), lambda i, j: (i, 0)),
                  pl.BlockSpec((None, d, d), lambda i, j: (gate_idx, 0, j)),
                  pl.BlockSpec((1, d), lambda i, j: (0, j)),
                  pl.BlockSpec((None, d, LANES), lambda i, j: (layer, 0, 0)),
                  pl.BlockSpec((1, LANES), lambda i, j: (0, 0))] + [ANY] * nd,
        out_specs=[pl.BlockSpec((tm, d), lambda i, j: (i, j)), narrow, narrow,
                   pl.BlockSpec((sub, 8, QB), lambda i, j: (i, 0, 0))],
        out_shape=[_sds((t, 3 * d), F32), _sds((t, LANES), F32), _sds((t, LANES), F32),
                   _sds((t // QB, 8, QB), F32)],
        scratch_shapes=[pltpu.VMEM((1, LANES), F32)], compiler_params=_cparams(2))(hm, wc, bg, wf, bf, *deps)


def forget_cumsum_bwd(dfrow, f, name):
    t = f.shape[0]
    nq = t // QB

    def body(dfr_ref, f_ref, df_ref, carry):
        jj = pl.program_id(0)

        @pl.when(jj == 0)
        def _():
            carry[...] = jnp.zeros_like(carry)

        padded = jnp.concatenate([dfr_ref[...], jnp.zeros((QB - 8, QB), F32)], axis=0)
        dfcol = padded.T
        tri = (_iota2((QB, QB), 1) >= _iota2((QB, QB), 0)).astype(F32)
        dlogf = jnp.dot(tri, dfcol, precision=HIGHEST, preferred_element_type=F32) + carry[...]
        carry[...] += jnp.sum(dfcol, axis=0, keepdims=True)
        df_ref[...] = dlogf * _sigmoid(-f_ref[...])

    return pl.pallas_call(
        body, name=name, grid=(nq,),
        in_specs=[pl.BlockSpec((None, 8, QB), lambda jj: (nq - 1 - jj, 0, 0)),
                  pl.BlockSpec((QB, LANES), lambda jj: (nq - 1 - jj, 0))],
        out_specs=pl.BlockSpec((QB, LANES), lambda jj: (nq - 1 - jj, 0)),
        out_shape=_sds((t, LANES), F32),
        scratch_shapes=[pltpu.VMEM((1, LANES), F32)], compiler_params=_cparams(1))(dfrow, f)


REL_DIAG = 768
REL_SHIFT = REL_DIAG - (QB - 1)


def _diag_onehot():
    u = _iota2((REL_PAD, REL_DIAG), 1)
    rel = jnp.clip(CH_KEYS - 1 - u, -MAX_REL, MAX_REL) + MAX_REL
    return (_iota2((REL_PAD, REL_DIAG), 0) == rel).astype(F32)


def rel_bias_build(tab_t, name):
    def body(tab_ref, o_ref):
        diag = jnp.dot(tab_ref[...], _diag_onehot(), precision=HIGHEST, preferred_element_type=F32)
        band = _chunk_band()
        for h in range(N_HEADS_CH):
            rows = jnp.broadcast_to(diag[h:h + 1, :], (QB, REL_DIAG))
            o_ref[h] = pltpu.roll(rows, REL_SHIFT, 1, stride=1, stride_axis=0)[:, :CH_KEYS] + band

    return pl.pallas_call(
        body, name=name, out_shape=_sds((N_HEADS_CH, QB, CH_KEYS), F32),
        in_specs=[pl.BlockSpec(memory_space=pltpu.VMEM)], out_specs=pl.BlockSpec(memory_space=pltpu.VMEM),
    )(tab_t)


def rel_bias_scatter(dbias, name):
    def body(db_ref, o_ref, ddiag):
        flip = (_iota2((QB, QB), 0) + _iota2((QB, QB), 1) == QB - 1).astype(F32)
        for h in range(N_HEADS_CH):
            padded = jnp.concatenate([db_ref[h], jnp.zeros((QB, REL_DIAG - CH_KEYS), F32)], axis=1)
            flipped = jnp.dot(flip, padded, precision=HIGHEST, preferred_element_type=F32)
            unrolled = pltpu.roll(flipped, 0, 1, stride=1, stride_axis=0)
            ddiag[h:h + 1, :] = jnp.sum(unrolled, axis=0, keepdims=True)
        o_ref[...] = lax.dot_general(ddiag[...], _diag_onehot(), NT, precision=HIGHEST,
                                     preferred_element_type=F32)

    return pl.pallas_call(
        body, name=name, out_shape=_sds((N_HEADS_CH, REL_PAD), F32),
        in_specs=[pl.BlockSpec(memory_space=pltpu.VMEM)], out_specs=pl.BlockSpec(memory_space=pltpu.VMEM),
        scratch_shapes=[pltpu.VMEM((N_HEADS_CH, REL_DIAG), F32)],
    )(dbias)


def _hl(h):
    return slice(h * HEAD_DIM, (h + 1) * HEAD_DIM)


def _split_dot(x, tri_bf16):
    hi = x.astype(BF16)
    lo = (x - hi.astype(F32)).astype(BF16)
    return (jnp.dot(hi, tri_bf16, preferred_element_type=F32)
            + jnp.dot(lo, tri_bf16, preferred_element_type=F32))


def _krows(g):
    return pl.ds(pl.multiple_of(g * KB, KB), KB)


def _log_sigmoid_pair(z):
    sp = jnp.log(1.0 + jnp.exp(-jnp.abs(z)))
    return jnp.minimum(z, 0.0) - sp, -jnp.maximum(z, 0.0) - sp


SB_STEP_HEADS = 4
SB_COLS = SB_STEP_HEADS * HEAD_DIM


def _sb_specs(t):
    n = W_SB // SB_COLS
    q_spec = pl.BlockSpec((QB, SB_COLS), lambda hp, i: (i, hp))
    k_spec = pl.BlockSpec((t, SB_COLS), lambda hp, i: (0, n + hp))
    v_spec = pl.BlockSpec((t, SB_COLS), lambda hp, i: (0, 2 * n + hp))
    return q_spec, k_spec, v_spec


def _keys_major(xt):
    n, groups, w, _ = xt.shape
    return xt.transpose(1, 3, 0, 2).reshape(groups * KB, n * w)


def sb_fwd(qkv, name):
    t = qkv.shape[0]
    nq = t // QB

    def body(q_ref, k_ref, v_ref, o_ref, w_ref):
        i = pl.program_id(1)
        groups = i // KSUB + 1
        tri_after = (_iota2((KB, KB), 0) > _iota2((KB, KB), 1)).astype(BF16)
        t_idx = i * QB + _iota2((QB, KB), 0)
        qs = [q_ref[:, _hl(h)] for h in range(SB_STEP_HEADS)]

        def step(g, carry, masked):
            strict = (g * KB + _iota2((QB, KB), 1)) < t_idx
            out = []
            for h in range(SB_STEP_HEADS):
                tail, acc = carry[2 * h], carry[2 * h + 1]
                k = k_ref[_krows(g), _hl(h)]
                v = v_ref[_krows(g), _hl(h)]
                z = lax.dot_general(qs[h], k, NT, preferred_element_type=F32)
                lb, lf = _log_sigmoid_pair(z)
                if masked:
                    lf = jnp.where(strict, lf, 0.0)
                between = _split_dot(lf, tri_after) + tail
                w = jnp.exp(lb + between)
                if masked:
                    w = jnp.where(strict, w, 0.0)
                w = w.astype(BF16)
                w_ref[h, g] = w
                acc = acc + jnp.dot(w, v, preferred_element_type=F32)
                out += [tail + jnp.sum(lf, axis=1, keepdims=True), acc]
            return tuple(out)

        init = (jnp.zeros((QB, 1), F32), jnp.zeros((QB, HEAD_DIM), F32)) * SB_STEP_HEADS
        res = step(groups - 1, init, True)
        res = lax.fori_loop(0, groups - 1, lambda gg, c: step(groups - 2 - gg, c, False), res)
        for h in range(SB_STEP_HEADS):
            o_ref[:, _hl(h)] = res[2 * h + 1].astype(o_ref.dtype)

    q_spec, k_spec, v_spec = _sb_specs(t)
    return pl.pallas_call(
        body, name=name, grid=(W_SB // SB_COLS, nq), in_specs=[q_spec, k_spec, v_spec],
        out_specs=[pl.BlockSpec((QB, SB_COLS), lambda hp, i: (i, hp)),
                   pl.BlockSpec((SB_STEP_HEADS, None, t // KB, QB, KB), lambda hp, i: (hp, i, 0, 0, 0))],
        out_shape=[_sds((t, W_SB), BF16), _sds((4, nq, t // KB, QB, KB), BF16)],
        compiler_params=_cparams(2))(qkv, qkv, qkv)


def _hs(h):
    return slice(h * HEAD_DIM, (h + 1) * HEAD_DIM)


def sb_bwd(qkv, qkv_t, w, do, do_t, name):
    t = qkv.shape[0]
    nq = t // QB

    def body(q_ref, k_ref, v_ref, do_ref, qt_ref, dot_ref, w_ref, dq_ref, dkt_ref, dvt_ref):
        i = pl.program_id(1)

        @pl.when(i == 0)
        def _():
            dkt_ref[...] = jnp.zeros_like(dkt_ref)
            dvt_ref[...] = jnp.zeros_like(dvt_ref)

        groups = i // KSUB + 1
        tri_before = (_iota2((KB, KB), 0) < _iota2((KB, KB), 1)).astype(BF16)
        t_idx = i * QB + _iota2((QB, KB), 0)
        qs = [q_ref[:, _hl(h)] for h in range(SB_STEP_HEADS)]
        dos = [do_ref[:, _hl(h)] for h in range(SB_STEP_HEADS)]
        qts = [qt_ref[_hs(h), :] for h in range(SB_STEP_HEADS)]
        dots = [dot_ref[_hs(h), :] for h in range(SB_STEP_HEADS)]

        def grads(g, carry, masked):
            strict = (g * KB + _iota2((QB, KB), 1)) < t_idx
            out = []
            for h in range(SB_STEP_HEADS):
                head, dq = carry[2 * h], carry[2 * h + 1]
                k = k_ref[_krows(g), _hl(h)]
                v = v_ref[_krows(g), _hl(h)]
                wb = w_ref[h, g]
                z = lax.dot_general(qs[h], k, NT, preferred_element_type=F32)
                beta = _sigmoid(z)
                e = lax.dot_general(dos[h], v, NT, preferred_element_type=F32) * wb.astype(F32)
                before = _split_dot(e, tri_before) + head
                dz = e * (1.0 - beta) - before * beta
                if masked:
                    dz = jnp.where(strict, dz, 0.0)
                dzb = dz.astype(BF16)
                dq = dq + jnp.dot(dzb, k, preferred_element_type=F32)
                dkt_ref[g, _hs(h), :] += jnp.dot(qts[h], dzb, preferred_element_type=F32)
                dvt_ref[g, _hs(h), :] += jnp.dot(dots[h], wb, preferred_element_type=F32)
                out += [head + jnp.sum(e, axis=1, keepdims=True), dq]
            return tuple(out)

        init = (jnp.zeros((QB, 1), F32), jnp.zeros((QB, HEAD_DIM), F32)) * SB_STEP_HEADS
        res = lax.fori_loop(0, groups - 1, lambda g, c: grads(g, c, False), init)
        res = grads(groups - 1, res, True)
        for h in range(SB_STEP_HEADS):
            dq_ref[:, _hl(h)] = (res[2 * h + 1] * SCALE).astype(dq_ref.dtype)

    q_spec, k_spec, v_spec = _sb_specs(t)
    blk = pl.BlockSpec((QB, SB_COLS), lambda hp, i: (i, hp))
    blk_t = pl.BlockSpec((SB_COLS, QB), lambda hp, i: (hp, i))
    acc_t = pl.BlockSpec((None, t // KB, SB_COLS, KB), lambda hp, i: (hp, 0, 0, 0))
    acc_sds = _sds((W_SB // SB_COLS, t // KB, SB_COLS, KB), F32)
    return pl.pallas_call(
        body, name=name, grid=(W_SB // SB_COLS, nq),
        in_specs=[q_spec, k_spec, v_spec, blk, blk_t, blk_t,
                  pl.BlockSpec((SB_STEP_HEADS, None, t // KB, QB, KB), lambda hp, i: (hp, i, 0, 0, 0))],
        out_specs=[blk, acc_t, acc_t],
        out_shape=[_sds((t, W_SB), BF16), acc_sds, acc_sds],
        compiler_params=_cparams(2))(qkv, qkv, qkv, do, qkv_t, do_t, w)


FOX_STEP_HEADS = 4
FOX_COLS = FOX_STEP_HEADS * HEAD_DIM


def _fox_specs(t):
    first = 3 * (W_SB + W_CH) // FOX_COLS
    n = W_FOX // FOX_COLS
    q_spec = pl.BlockSpec((QB, FOX_COLS), lambda hp, i: (i, first + hp))
    k_spec = pl.BlockSpec((t, FOX_COLS), lambda hp, i: (0, first + n + hp))
    v_spec = pl.BlockSpec((t, FOX_COLS), lambda hp, i: (0, first + 2 * n + hp))
    return q_spec, k_spec, v_spec


def fox_fwd(qkv, fcol, frow, name):
    t = qkv.shape[0]
    nq = t // QB

    def body(q_ref, k_ref, v_ref, fc_ref, fr_ref, o_ref, lse_ref):
        hp = pl.program_id(0)
        i = pl.program_id(1)
        groups = i // KSUB + 1
        t_idx = i * QB + _iota2((QB, KB), 0)
        lane = _iota2((QB, LANES), 1)
        sub = _iota2((8, KB), 0)
        qs = [q_ref[:, _hl(h)] for h in range(FOX_STEP_HEADS)]
        f_qs = [jnp.sum(jnp.where(lane == hp * FOX_STEP_HEADS + h, fc_ref[...], 0.0), axis=1, keepdims=True)
                for h in range(FOX_STEP_HEADS)]

        def step(g, carry, masked):
            causal = (g * KB + _iota2((QB, KB), 1)) <= t_idx
            fr = fr_ref[g]
            out = []
            for h in range(FOX_STEP_HEADS):
                m, l, acc = carry[3 * h:3 * h + 3]
                k = k_ref[_krows(g), _hl(h)]
                v = v_ref[_krows(g), _hl(h)]
                f_k = jnp.sum(jnp.where(sub == hp * FOX_STEP_HEADS + h, fr, 0.0), axis=0, keepdims=True)
                z = lax.dot_general(qs[h], k, NT, preferred_element_type=F32) + f_qs[h] - f_k
                if masked:
                    z = jnp.where(causal, z, NEG)
                m_new = jnp.maximum(m, jnp.max(z, axis=1, keepdims=True))
                p = jnp.exp(z - m_new)
                corr = jnp.exp(m - m_new)
                l = l * corr + jnp.sum(p, axis=1, keepdims=True)
                acc = acc * corr + jnp.dot(p.astype(BF16), v, preferred_element_type=F32)
                out += [m_new, l, acc]
            return tuple(out)

        init = (jnp.full((QB, 1), NEG, F32), jnp.zeros((QB, 1), F32),
                jnp.zeros((QB, HEAD_DIM), F32)) * FOX_STEP_HEADS
        res = lax.fori_loop(0, groups - 1, lambda g, c: step(g, c, False), init)
        res = step(groups - 1, res, True)
        for h in range(FOX_STEP_HEADS):
            m, l, acc = res[3 * h:3 * h + 3]
            o_ref[:, _hl(h)] = (acc / l).astype(o_ref.dtype)
            lse_ref[:, _hl(h)] = jnp.broadcast_to(m + jnp.log(l), (QB, HEAD_DIM))

    q_spec, k_spec, v_spec = _fox_specs(t)
    blk = pl.BlockSpec((QB, FOX_COLS), lambda hp, i: (i, hp))
    return pl.pallas_call(
        body, name=name, grid=(W_FOX // FOX_COLS, nq),
        in_specs=[q_spec, k_spec, v_spec, pl.BlockSpec((QB, LANES), lambda hp, i: (i, 0)),
                  pl.BlockSpec((t // KB, 8, KB), lambda hp, i: (0, 0, 0))],
        out_specs=[blk, blk],
        out_shape=[_sds((t, W_FOX), BF16), _sds((t, W_FOX), F32)],
        compiler_params=_cparams(2))(qkv, qkv, qkv, fcol, frow)


def fox_bwd(qkv, qkv_t, fcol, frow, o, lse, do, do_t, name, do_col=0):
    t = qkv.shape[0]
    nq = t // QB

    def body(q_ref, k_ref, v_ref, fc_ref, fr_ref, o_ref, lse_ref, do_ref, qt_ref, dot_ref,
             dq_ref, dk_ref, dv_ref, dfr_ref):
        hp = pl.program_id(0)
        i = pl.program_id(1)
        qts = [qt_ref[_hs(h), :] for h in range(FOX_STEP_HEADS)]
        dots = [dot_ref[_hs(h), :] for h in range(FOX_STEP_HEADS)]

        @pl.when(i == 0)
        def _():
            dk_ref[...] = jnp.zeros_like(dk_ref)
            dv_ref[...] = jnp.zeros_like(dv_ref)

        @pl.when((i == 0) & (hp == 0))
        def _():
            dfr_ref[...] = jnp.zeros_like(dfr_ref)

        groups = i // KSUB + 1
        t_idx = i * QB + _iota2((QB, KB), 0)
        lane = _iota2((QB, LANES), 1)
        sub = _iota2((8, KB), 0)
        qs = [q_ref[:, _hl(h)] for h in range(FOX_STEP_HEADS)]
        dos = [do_ref[:, _hl(h)] for h in range(FOX_STEP_HEADS)]
        f_qs = [jnp.sum(jnp.where(lane == hp * FOX_STEP_HEADS + h, fc_ref[...], 0.0), axis=1, keepdims=True)
                for h in range(FOX_STEP_HEADS)]
        lse_qs = [lse_ref[:, h * HEAD_DIM:h * HEAD_DIM + 1] for h in range(FOX_STEP_HEADS)]
        deltas = [jnp.sum(dos[h].astype(F32) * o_ref[:, _hl(h)].astype(F32), axis=1, keepdims=True)
                  for h in range(FOX_STEP_HEADS)]

        def step(g, dqs, masked):
            causal = (g * KB + _iota2((QB, KB), 1)) <= t_idx
            fr = fr_ref[g]
            out = []
            dfr = jnp.zeros((8, KB), F32)
            for h in range(FOX_STEP_HEADS):
                k = k_ref[_krows(g), _hl(h)]
                v = v_ref[_krows(g), _hl(h)]
                f_k = jnp.sum(jnp.where(sub == hp * FOX_STEP_HEADS + h, fr, 0.0), axis=0, keepdims=True)
                z = lax.dot_general(qs[h], k, NT, preferred_element_type=F32) + f_qs[h] - f_k
                p = jnp.exp(z - lse_qs[h])
                if masked:
                    p = jnp.where(causal, p, 0.0)
                dp = lax.dot_general(dos[h], v, NT, preferred_element_type=F32)
                ds = p * (dp - deltas[h])
                dsb = ds.astype(BF16)
                out.append(dqs[h] + jnp.dot(dsb, k, preferred_element_type=F32))
                dk_ref[g, _hs(h), :] += jnp.dot(qts[h], dsb, preferred_element_type=F32)
                dv_ref[g, _hs(h), :] += jnp.dot(dots[h], p.astype(BF16), preferred_element_type=F32)
                colsum = jnp.sum(ds, axis=0, keepdims=True)
                dfr = dfr + jnp.where(sub == hp * FOX_STEP_HEADS + h, -colsum, 0.0)
            dfr_ref[g] += dfr
            return tuple(out)

        res = lax.fori_loop(0, groups - 1, lambda g, c: step(g, c, False),
                            (jnp.zeros((QB, HEAD_DIM), F32),) * FOX_STEP_HEADS)
        res = step(groups - 1, res, True)
        for h in range(FOX_STEP_HEADS):
            dq_ref[:, _hl(h)] = (res[h] * SCALE).astype(dq_ref.dtype)

    q_spec, k_spec, v_spec = _fox_specs(t)
    blk = pl.BlockSpec((QB, FOX_COLS), lambda hp, i: (i, hp))
    frs = pl.BlockSpec((t // KB, 8, KB), lambda hp, i: (0, 0, 0))
    acc_t = pl.BlockSpec((None, t // KB, FOX_COLS, KB), lambda hp, i: (hp, 0, 0, 0))
    acc_sds = _sds((W_FOX // FOX_COLS, t // KB, FOX_COLS, KB), F32)
    return pl.pallas_call(
        body, name=name, grid=(W_FOX // FOX_COLS, nq),
        in_specs=[q_spec, k_spec, v_spec, pl.BlockSpec((QB, LANES), lambda hp, i: (i, 0)), frs,
                  blk, blk, pl.BlockSpec((QB, FOX_COLS), lambda hp, i: (i, do_col + hp)),
                  pl.BlockSpec((FOX_COLS, QB), lambda hp, i: (3 * (W_SB + W_CH) // FOX_COLS + hp, i)),
                  pl.BlockSpec((FOX_COLS, QB), lambda hp, i: (do_col + hp, i))],
        out_specs=[blk, acc_t, acc_t, frs],
        out_shape=[_sds((t, W_FOX), BF16), acc_sds, acc_sds, _sds((t // KB, 8, KB), F32)],
        compiler_params=_cparams(2))(qkv, qkv, qkv, fcol, frow, o, lse, do, qkv_t, do_t)


def _frow_to_groups(frow):
    n = frow.shape[0] // KSUB
    return frow.reshape(n, KSUB, 8, QB).transpose(0, 2, 1, 3).reshape(n, 8, KB)


def _frow_from_groups(frow):
    n = frow.shape[0]
    return frow.reshape(n, 8, KSUB, QB).transpose(0, 2, 1, 3).reshape(n * KSUB, 8, QB)


def _chunk_band():
    qi = _iota2((QB, CH_KEYS), 0)
    kj = _iota2((QB, CH_KEYS), 1)
    dchunk = (qi >> 6) + LEFT_CHUNKS - (kj >> 6)
    return jnp.where((dchunk >= 0) & (dchunk <= LEFT_CHUNKS), 0.0, NEG)


def _chunk_pad_row(i):
    kj = _iota2((1, CH_KEYS), 1)
    return jnp.where((i - (CH_WIN - 1)) * QB + kj >= 0, 0.0, NEG)


CH_PAD = (CH_WIN - 1) * QB
CH_STEP_HEADS = 4
CH_COLS = CH_STEP_HEADS * HEAD_DIM


def _window(i):
    return pl.ds(pl.multiple_of(i * QB, QB), CH_KEYS)


def _chunk_weights(q, kw, bias, pad_row):
    z = lax.dot_general(q, kw, NT, preferred_element_type=F32) + bias + pad_row
    e = jnp.exp(z - jnp.max(z, axis=1, keepdims=True))
    return e, 1.0 / jnp.sum(e, axis=1, keepdims=True)


def _chunk_specs(t):
    q_spec = pl.BlockSpec((QB, CH_COLS), lambda hp, i: (i, 3 * W_SB // CH_COLS + hp))
    kv_spec = pl.BlockSpec((t + CH_PAD, CH_COLS), lambda hp, i: (0, hp))
    return q_spec, kv_spec


def chunk_fwd(qkv, kp, vp, bias, name):
    t = qkv.shape[0]
    nq = t // QB

    def body(q_ref, k_ref, v_ref, b_ref, o_ref):
        i = pl.program_id(1)
        pad_row = _chunk_pad_row(i)
        for h in range(CH_STEP_HEADS):
            e, inv = _chunk_weights(q_ref[:, _hl(h)], k_ref[_window(i), _hl(h)], b_ref[h], pad_row)
            o = jnp.dot(e.astype(BF16), v_ref[_window(i), _hl(h)], preferred_element_type=F32)
            o_ref[:, _hl(h)] = (o * inv).astype(o_ref.dtype)

    q_spec, kv_spec = _chunk_specs(t)
    return pl.pallas_call(
        body, name=name, grid=(W_CH // CH_COLS, nq),
        in_specs=[q_spec, kv_spec, kv_spec,
                  pl.BlockSpec((CH_STEP_HEADS, QB, CH_KEYS), lambda hp, i: (hp, 0, 0))],
        out_specs=pl.BlockSpec((QB, CH_COLS), lambda hp, i: (i, hp)),
        out_shape=_sds((t, W_CH), BF16), compiler_params=_cparams(2))(qkv, kp, vp, bias)


def chunk_bwd(qkv, qkv_t, kp, vp, bias, do, do_t, name, do_col=0):
    t = qkv.shape[0]
    nq = t // QB

    def body(q_ref, k_ref, v_ref, b_ref, do_ref, qt_ref, dot_ref, dq_ref, dk_ref, dv_ref, db_ref):
        i = pl.program_id(1)

        @pl.when(i == 0)
        def _():
            dk_ref[...] = jnp.zeros_like(dk_ref)
            dv_ref[...] = jnp.zeros_like(dv_ref)
            db_ref[...] = jnp.zeros_like(db_ref)

        pad_row = _chunk_pad_row(i)
        for h in range(CH_STEP_HEADS):
            q = q_ref[:, _hl(h)]
            dov = do_ref[:, _hl(h)]
            kw = k_ref[_window(i), _hl(h)]
            e, inv = _chunk_weights(q, kw, b_ref[h], pad_row)
            p = e * inv
            dp = lax.dot_general(dov, v_ref[_window(i), _hl(h)], NT, preferred_element_type=F32)
            ds = p * (dp - jnp.sum(p * dp, axis=1, keepdims=True))
            db_ref[h] += ds
            dsb = ds.astype(BF16)
            dq_ref[:, _hl(h)] = (jnp.dot(dsb, kw, preferred_element_type=F32) * SCALE).astype(dq_ref.dtype)
            dkt = jnp.dot(qt_ref[_hs(h), :], dsb, preferred_element_type=F32)
            dvt = jnp.dot(dot_ref[_hs(h), :], p.astype(BF16), preferred_element_type=F32)
            for b in range(CH_WIN):
                dk_ref[i + b, _hs(h), :] += dkt[:, b * QB:(b + 1) * QB]
                dv_ref[i + b, _hs(h), :] += dvt[:, b * QB:(b + 1) * QB]

    q_spec, kv_spec = _chunk_specs(t)
    blk = pl.BlockSpec((QB, CH_COLS), lambda hp, i: (i, hp))
    bspec = pl.BlockSpec((CH_STEP_HEADS, QB, CH_KEYS), lambda hp, i: (hp, 0, 0))
    nblk = nq + CH_WIN - 1
    acc_t = pl.BlockSpec((None, nblk, CH_COLS, QB), lambda hp, i: (hp, 0, 0, 0))
    acc_sds = _sds((W_CH // CH_COLS, nblk, CH_COLS, QB), F32)
    return pl.pallas_call(
        body, name=name, grid=(W_CH // CH_COLS, nq),
        in_specs=[q_spec, kv_spec, kv_spec, bspec,
                  pl.BlockSpec((QB, CH_COLS), lambda hp, i: (i, do_col + hp)),
                  pl.BlockSpec((CH_COLS, QB), lambda hp, i: (3 * W_SB // CH_COLS + hp, i)),
                  pl.BlockSpec((CH_COLS, QB), lambda hp, i: (do_col + hp, i))],
        out_specs=[blk, acc_t, acc_t, bspec],
        out_shape=[_sds((t, W_CH), BF16), acc_sds, acc_sds, _sds((N_HEADS_CH, QB, CH_KEYS), F32)],
        compiler_params=_cparams(2))(qkv, kp, vp, bias, do, qkv_t, do_t)


def _sum_parts(p_ref):
    total = p_ref[0].astype(F32)
    for p in range(1, p_ref.shape[0]):
        total = total + p_ref[p].astype(F32)
    return total


def sum_parts_multi(parts_list, name):
    n = len(parts_list)

    def body(*refs):
        for p_ref, o_ref in zip(refs[:n], refs[n:]):
            o_ref[...] = _sum_parts(p_ref)

    shapes = [p.shape[2:] for p in parts_list]
    return pl.pallas_call(
        body, name=name, grid=(1,),
        in_specs=[pl.BlockSpec((N_DEV, None, r, c), lambda s: (0, 0, 0, 0)) for r, c in shapes],
        out_specs=[pl.BlockSpec((r, c), lambda s: (0, 0)) for r, c in shapes],
        out_shape=[_sds((r, c), F32) for r, c in shapes], compiler_params=_cparams(1))(*parts_list)


def adamw(parts, w, m, v, grid, p_specs, w_spec, name):
    c1 = 1.0 / (1.0 - ADAM_B1 ** ADAM_STEP)
    c2 = 1.0 / (1.0 - ADAM_B2 ** ADAM_STEP)
    n = len(parts)

    def body(*refs):
        w_ref, m_ref, v_ref, g_out, d_out, m_out, v_out = refs[n:]
        g = _sum_parts(refs[0])
        for q in range(1, n):
            g = jnp.where(pl.program_id(0) == q, _sum_parts(refs[q]), g)
        m_new = ADAM_B1 * m_ref[...] + (1.0 - ADAM_B1) * g
        v_new = ADAM_B2 * v_ref[...] + (1.0 - ADAM_B2) * (g * g)
        m_hat = m_new * c1
        v_hat = v_new * c2
        g_out[...] = g
        d_out[...] = -ADAM_LR * (m_hat / (jnp.sqrt(v_hat) + ADAM_EPS) + ADAM_WD * w_ref[...])
        m_out[...] = m_new
        v_out[...] = v_new

    out = _sds(w.shape, F32)
    return pl.pallas_call(
        body, name=name, grid=grid, in_specs=[*p_specs, w_spec, w_spec, w_spec],
        out_specs=[w_spec] * 4, out_shape=[out] * 4,
        compiler_params=_cparams(len(grid)))(*parts, w, m, v)


def _ffn_fwd(x, gain, wa, wb_after, s, tm, tag, on_event, deps=()):
    t = x.shape[0]
    hn = rmsnorm_fwd(x, gain, tm, f"rms_{tag}", deps)
    gu, act = ffn_in_swiglu(hn, wa, s, min(2 * tm, t), f"ffn_in_{tag}")
    relayed = on_event("act", act)
    wb = wb_after(act)
    y = ffn_out_residual(act, wb, x, s, min(2 * tm, t), f"ffn_out_{tag}", relayed)
    return y, (hn, gu, act), wb


def _ffn_bwd(dy, x, gain, saved, wa, wb, s, tm, tag, on_grads):
    t = x.shape[0]
    hn, gu, act = saved
    dgu = ffn_dact_swiglu(dy, wb, gu, s, min(2 * tm, t), f"ffn_dact_{tag}")
    dwb = matmul(TN, act, dy, _sds((4, FF_BLK, D_MODEL), BF16), (4, 1, 1),
                 pl.BlockSpec((None, t, FF_BLK), lambda i, j, k: (i, 0, 0)),
                 pl.BlockSpec((t, D_MODEL), lambda i, j, k: (0, 0)),
                 pl.BlockSpec((None, FF_BLK, D_MODEL), lambda i, j, k: (i, 0, 0)),
                 None, name=f"ffn_dwout_{tag}", alpha=0.5)
    dwa = matmul(TN, dgu, hn, _sds((8, FF_BLK, D_MODEL), BF16), (1, 8, 1),
                 pl.BlockSpec((None, None, t, FF_BLK), lambda i, j, k: (j % 4, j // 4, 0, 0)),
                 pl.BlockSpec((t, D_MODEL), lambda i, j, k: (0, 0)),
                 pl.BlockSpec((None, FF_BLK, D_MODEL), lambda i, j, k: (j, 0, 0)),
                 None, name=f"ffn_dwin_{tag}")
    deps = on_grads(dwa, dwb)
    return ffn_dh_norm_bwd(dgu, wa, s, x, gain, dy, tm, f"ffn_dh_{tag}", deps)


_Q_COLUMN_SCALE = np.ones((1, QKV_WIDTH), np.float32)
for _lo, _width in ((0, W_SB), (3 * W_SB, W_CH), (3 * (W_SB + W_CH), W_FOX)):
    _Q_COLUMN_SCALE[0, _lo:_lo + _width] = SCALE


def _mixer_fwd(x, gain, wqkv, wf, wgate, late_after, bq, bf, bg, bias, layer, tm, tag, on_event):
    t = x.shape[0]
    nt = t // tm
    hm = rmsnorm_fwd(x, gain, tm, f"rms_{tag}")
    a_full = pl.BlockSpec((tm, D_MODEL), lambda i, j, k: (i, 0))
    wide_out = pl.BlockSpec((tm, D_MODEL), lambda i, j, k: (i, j))
    wide_b = pl.BlockSpec((1, D_MODEL), lambda i, j, k: (0, j))
    qkv, qkv_t = matmul(NN, hm, wqkv, _sds((t, QKV_WIDTH), BF16), (nt, 3, 1), a_full,
                        pl.BlockSpec((None, D_MODEL, D_MODEL), lambda i, j, k: (layer, 0, j)), wide_out, None,
                        name=f"proj_qkv_{tag}", bias=bq, bias_spec=wide_b,
                        scale=jnp.asarray(_Q_COLUMN_SCALE), scale_spec=wide_b,
                        out_t_sds=_sds((QKV_WIDTH, t), BF16),
                        out_t_spec=pl.BlockSpec((D_MODEL, tm), lambda i, j, k: (j, i)))
    relayed = on_event("qkv", qkv)
    gates, f, fcol, frow = proj_gates_forget(hm, wgate, layer + 1, wf, layer, bg, bf, tm,
                                             f"proj_gate_{tag}", relayed)
    frow = _frow_to_groups(frow)
    o_sb, w_sb = sb_fwd(qkv, f"sb_fwd_{tag}")
    relayed = on_event("o_sb", o_sb)
    kp = jnp.pad(qkv[:, 10 * LANES:14 * LANES], ((CH_PAD, 0), (0, 0)))
    vp = jnp.pad(qkv[:, 14 * LANES:18 * LANES], ((CH_PAD, 0), (0, 0)))
    o_ch = chunk_fwd(qkv, kp, vp, bias, f"chunk_fwd_{tag}")
    o_fox, lse = fox_fwd(qkv, fcol, frow, f"fox_fwd_{tag}")
    wbr, wout = late_after(o_fox)
    ys, merged = branch_merge((o_sb, o_ch, o_fox), wbr, layer, gates, tm, f"branch_merge_{tag}", relayed)
    x_new = matmul(NN, merged, wout, _sds((t, D_MODEL), F32), (nt, 1, 1), a_full,
                   pl.BlockSpec((None, D_MODEL, D_MODEL), lambda i, j, k: (layer, 0, 0)), a_full, None,
                   name=f"wout_{tag}", res=x, res_spec=a_full)
    saved = (hm, qkv, gates, f, fcol, frow, o_sb, o_ch, o_fox, lse, ys, merged, kp, vp, w_sb, qkv_t)
    return x_new, saved, wbr, wout


def _mixer_bwd(dy, x, gain, saved, wqkv, wf, wgate, wbr, wout, bias, layer, tm, tag, on_grads):
    t = x.shape[0]
    nt = t // tm
    hm, qkv, gates, f, fcol, frow, o_sb, o_ch, o_fox, lse, ys, merged, kp, vp, w_sb, qkv_t = saved
    a_full = pl.BlockSpec((tm, D_MODEL), lambda i, j, k: (i, 0))
    red_row = pl.BlockSpec((tm, D_MODEL), lambda i, j, k: (k, 0))
    sq = pl.BlockSpec((D_MODEL, D_MODEL), lambda i, j, k: (0, 0))
    dgates, dys = dmerged_merge_bwd(dy, wout, layer, gates, ys, tm // 2, f"dmerged_{tag}")
    all_t = pl.BlockSpec((t, D_MODEL), lambda i, j, k: (0, 0))
    dwout = matmul(TN, merged, dy, _sds((D_MODEL, D_MODEL), BF16), (1, 1, 1), all_t, all_t, sq,
                   None, name=f"dwout_{tag}")
    do, do_t, dwbr = branch_bwd(dys, (o_sb, o_ch, o_fox), wbr, layer, tm, f"dbranch_{tag}")
    dq_a, dk_a, dv_a = sb_bwd(qkv, qkv_t, w_sb, do, do_t, f"sb_bwd_{tag}")
    dk_a, dv_a = _keys_major(dk_a), _keys_major(dv_a)
    dq_b, dk_b, dv_b, dbias = chunk_bwd(qkv, qkv_t, kp, vp, bias, do, do_t, f"chunk_bwd_{tag}",
                                        do_col=W_SB // CH_COLS)
    dk_b, dv_b = [x[:, CH_WIN - 1:].transpose(1, 3, 0, 2).reshape(t, W_CH) for x in (dk_b, dv_b)]
    dq_c, dk_c, dv_c, dfrow = fox_bwd(qkv, qkv_t, fcol, frow, o_fox, lse, do, do_t, f"fox_bwd_{tag}",
                                      do_col=(W_SB + W_CH) // FOX_COLS)
    dk_c, dv_c = _keys_major(dk_c), _keys_major(dv_c)
    df = forget_cumsum_bwd(_frow_from_groups(dfrow), f, f"fcum_bwd_{tag}")
    dqkv = jnp.concatenate([p.astype(BF16) for p in
                            (dq_a, dk_a, dv_a, dq_b, dk_b, dv_b, dq_c, dk_c, dv_c)], axis=1)
    dtab = rel_bias_scatter(dbias, f"rel_scatter_{tag}")

    all_rows = pl.BlockSpec((t, D_MODEL), lambda i, j, k: (0, 0))
    wide_b = pl.BlockSpec((t, D_MODEL), lambda i, j, k: (0, j))
    wide_o = pl.BlockSpec((D_MODEL, D_MODEL), lambda i, j, k: (0, j))
    wide_cs = pl.BlockSpec((1, D_MODEL), lambda i, j, k: (0, j))
    dwqkv, dbq = matmul(TN, hm, dqkv, _sds((D_MODEL, QKV_WIDTH), BF16), (1, 3, 1), all_rows, wide_b,
                        wide_o, None, name=f"dwqkv_{tag}",
                        colsum_sds=_sds((1, QKV_WIDTH), F32), colsum_spec=wide_cs)
    dwgate, dbg = matmul(TN, hm, dgates, _sds((D_MODEL, 3 * D_MODEL), BF16), (1, 3, 1), all_rows,
                         wide_b, wide_o, None, name=f"dwgate_{tag}",
                         colsum_sds=_sds((1, 3 * D_MODEL), F32), colsum_spec=wide_cs)
    dwf, dbf = matmul(TN, hm, df, _sds((D_MODEL, LANES), BF16), (1, 1, 1), all_rows,
                      pl.BlockSpec((t, LANES), lambda i, j, k: (0, 0)),
                      pl.BlockSpec((D_MODEL, LANES), lambda i, j, k: (0, 0)), None,
                      name=f"dwf_{tag}", colsum_sds=_sds((1, LANES), F32),
                      colsum_spec=pl.BlockSpec((1, LANES), lambda i, j, k: (0, 0)))
    deps = on_grads(dict(dwqkv=dwqkv, dwgate=dwgate, dwf=dwf, dwbr=dwbr, dwout=dwout))
    wide_a = pl.BlockSpec((tm, QKV_WIDTH), lambda i, j, k: (i, 0))
    dhm = matmul(NT, dqkv, wqkv, _sds((t, D_MODEL), F32), (nt, 1, 1), wide_a,
                 pl.BlockSpec((None, D_MODEL, QKV_WIDTH), lambda i, j, k: (layer, 0, 0)), a_full,
                 None, name=f"dhm_qkv_{tag}", deps=deps)
    dx, dgain = mixer_dh_norm_bwd(dhm, dgates, wgate, layer + 1, df, wf, layer, x, gain, dy, tm,
                                  f"dhm_gate_{tag}")
    return dx, dict(dbq=dbq, dbg=dbg, dbf=dbf, dtab=dtab, dgain=dgain)


def _pack_small(pieces):
    flat = jnp.concatenate([p.reshape(-1).astype(F32) for p in pieces])
    flat = jnp.pad(flat, (0, SMALL_ROWS * LANES - flat.shape[0]))
    return flat.reshape(SMALL_ROWS, LANES)


def _unpack_small(packed, shapes):
    flat = packed.reshape(-1)
    out, pos = [], 0
    for shp in shapes:
        n = int(np.prod(shp))
        out.append(flat[pos:pos + n].reshape(shp))
        pos += n
    return out


def kernel(x, g_ffn1, w_ffn1_in, w_ffn1_out, g_mix, w_in, b_in, rel_bias, w_br_sb, w_br_ch, w_br_fox, w_out, g_ffn2, w_ffn2_in, w_ffn2_out, g_final, loss_target, m_g_ffn1, m_w_ffn1_in, m_w_ffn1_out, m_g_mix, m_w_in, m_b_in, m_rel_bias, m_w_br_sb, m_w_br_ch, m_w_br_fox, m_w_out, m_g_ffn2, m_w_ffn2_in, m_w_ffn2_out, m_g_final, v_g_ffn1, v_w_ffn1_in, v_w_ffn1_out, v_g_mix, v_w_in, v_b_in, v_rel_bias, v_w_br_sb, v_w_br_ch, v_w_br_fox, v_w_out, v_g_ffn2, v_w_ffn2_in, v_w_ffn2_out, v_g_final):
    t = x.shape[1]
    tm = min(512, t)
    xs = x[0]
    target = loss_target[0]
    f_lo, f_hi = QKV_WIDTH, QKV_WIDTH + N_HEADS_FOX

    def ffn_shards(w_in_, w_out_, l):
        return [w_in_[l:l + 1].astype(BF16), w_out_[l:l + 1].astype(BF16)]

    def mixer_shards(l):
        wl = w_in[l]
        return [jnp.stack([wl[:, :QKV_WIDTH], wl[:, f_hi:]]).astype(BF16),
                jnp.pad(wl[:, f_lo:f_hi], ((0, 0), (0, LANES - N_HEADS_FOX)))[None].astype(BF16),
                w_out[l:l + 1].astype(BF16),
                jnp.concatenate([w_br_sb[l], w_br_ch[l], w_br_fox[l]], axis=0)[None].astype(BF16)]

    gathers = {}
    gather_tokens = []

    def start_gather(shards, name):
        handle = gather_start(shards, name, deps=gather_tokens[-1:])
        gather_tokens.append(handle["token"])
        return handle

    def relay(handle, after):
        if "send2" not in handle:
            gather_relay(handle, after)

    relay_on = {("mix", 0, "qkv"): ("mix", 0, 1), ("mix", 0, "o_sb"): ("ffn2", 0, 0),
                ("ffn2", 0, "act"): ("ffn1", 1, 0), ("ffn1", 1, "act"): ("mix", 1, 0),
                ("mix", 1, "qkv"): ("ffn2", 1, 0)}

    def on_event(grp, l):
        def fire(event, array):
            target = relay_on.get((grp, l, event))
            if target is None:
                return ()
            handle = gathers[target[:2]][target[2]]
            relay(handle, array)
            return (handle["relay_token"],)
        return fire

    for l in range(DEPTH):
        for grp, shards in (("ffn1", ffn_shards(w_ffn1_in, w_ffn1_out, l)), ("mix", mixer_shards(l)),
                            ("ffn2", ffn_shards(w_ffn2_in, w_ffn2_out, l))):
            cut = len(shards) // 2
            if l == 0 and grp != "ffn2":
                gathers[(grp, l)] = (start_gather(shards[:cut], f"gather_{grp}_l{l}_a"),
                                     start_gather(shards[cut:], f"gather_{grp}_l{l}_b"))
            else:
                gathers[(grp, l)] = (start_gather(shards, f"gather_{grp}_l{l}"),)

    def gathered(key, after):
        hs = gathers[key]
        cut = hs[0]["n"]
        relay(hs[0], after)
        first = gather_finish(hs[0], after)
        if len(hs) == 1:
            return first[:cut // 2], lambda later: first[cut // 2:]

        def second(later):
            relay(hs[1], later)
            return gather_finish(hs[1], later)

        return first, second

    def ffn_weights(key, after):
        (wa_,), rest = gathered(key, after)
        return wa_, lambda later: rest(later)[0].reshape(1, 4, FF_BLK, D_MODEL)

    def mixer_weights(key, after):
        (wc_, wf_), rest = gathered(key, after)

        def late(later):
            wout_, wbr_ = rest(later)
            return (wbr_.transpose(0, 2, 1, 3).reshape(1, D_MODEL, D_MODEL), wout_.reshape(1, D_MODEL, D_MODEL))

        return wc_.reshape(2, D_MODEL, QKV_WIDTH), wf_.reshape(1, D_MODEL, LANES), late

    bq = b_in[:, None, :QKV_WIDTH]
    bf = jnp.pad(b_in[:, f_lo:f_hi], ((0, 0), (0, LANES - N_HEADS_FOX)))[:, None, :]
    bg = b_in[:, None, f_hi:]
    tab_t = jnp.pad(rel_bias.transpose(0, 2, 1), ((0, 0), (0, 0), (0, REL_PAD - N_REL)))

    h = xs
    saved = []
    weights = []
    for l in range(DEPTH):
        bias = rel_bias_build(tab_t[l], f"rel_build_l{l}").reshape(N_HEADS_CH, QB, CH_KEYS)
        x0 = h
        wa1, wb1_after = ffn_weights(("ffn1", l), x0)
        x1, s1, wb1 = _ffn_fwd(x0, g_ffn1[l:l + 1], wa1, wb1_after, 0, tm, f"ffn1_l{l}", on_event("ffn1", l),
                               deps=gather_tokens if l == 0 else ())
        wc, wf, late_after = mixer_weights(("mix", l), x1)
        x2, sm, wbr, wout = _mixer_fwd(x1, g_mix[l:l + 1], wc, wf, wc, late_after, bq[l], bf[l], bg[l],
                                       bias, 0, tm, f"mix_l{l}", on_event("mix", l))
        wa2, wb2_after = ffn_weights(("ffn2", l), x2)
        x3, s2, wb2 = _ffn_fwd(x2, g_ffn2[l:l + 1], wa2, wb2_after, 0, tm, f"ffn2_l{l}", on_event("ffn2", l))
        saved.append((x0, x1, x2, s1, sm, s2, bias))
        weights.append(((wa1, wb1), (wc, wf, wout, wbr), (wa2, wb2)))
        h = x3

    dx, dg_final, loss_blk = loss_head(h, g_final[None, :], target, tm, "loss_head")

    g_mix_l = [None] * DEPTH
    dgains = {}
    scatters = {}

    def scatter_ffn(key):
        def on_grads(dwa, dwb):
            scatters[key] = exchange_start(
                "scatter", [dwa[None], dwb.reshape(1, N_DEV, D_FF // N_DEV, D_MODEL)],
                f"scatter_{key[0]}_l{key[1]}")
            return (scatters[key]["token"],)
        return on_grads

    def scatter_mixer(key):
        def on_grads(gm):
            scatters[key] = exchange_start(
                "scatter",
                [gm["dwqkv"].reshape(1, N_DEV, LANES, QKV_WIDTH), gm["dwgate"].reshape(1, N_DEV, LANES, QKV_WIDTH),
                 gm["dwf"].reshape(1, N_DEV, LANES, LANES), gm["dwout"].reshape(1, N_DEV, LANES, D_MODEL),
                 gm["dwbr"].reshape(1, D_MODEL, N_DEV, LANES).transpose(0, 2, 1, 3)],
                f"scatter_{key[0]}_l{key[1]}")
            return (scatters[key]["token"],)
        return on_grads

    for l in reversed(range(DEPTH)):
        x0, x1, x2, s1, sm, s2, bias = saved[l]
        w1, (wc, wf, wout, wbr), w2 = weights[l]
        dx, dgains[("ffn2", l)] = _ffn_bwd(dx, x2, g_ffn2[l:l + 1], s2, *w2, 0, tm, f"ffn2_l{l}",
                                           scatter_ffn(("ffn2", l)))
        dx, g_mix_l[l] = _mixer_bwd(dx, x1, g_mix[l:l + 1], sm, wc, wf, wc, wbr, wout, bias, 0, tm,
                                    f"mix_l{l}", scatter_mixer(("mix", l)))
        dx, dgains[("ffn1", l)] = _ffn_bwd(dx, x0, g_ffn1[l:l + 1], s1, *w1, 0, tm, f"ffn1_l{l}",
                                           scatter_ffn(("ffn1", l)))

    small_shapes = []
    small_pieces = []
    small_w, small_m, small_v = [], [], []

    def add_small(piece, w, m, v):
        small_shapes.append(w.shape)
        small_pieces.append(piece)
        small_w.append(w); small_m.append(m); small_v.append(v)

    dg1 = jnp.concatenate([dgains[("ffn1", l)] for l in range(DEPTH)], axis=0)
    dgm = jnp.concatenate([g_mix_l[l]["dgain"] for l in range(DEPTH)], axis=0)
    dg2 = jnp.concatenate([dgains[("ffn2", l)] for l in range(DEPTH)], axis=0)
    db = jnp.stack([jnp.concatenate([g_mix_l[l]["dbq"][0], g_mix_l[l]["dbf"][0, :N_HEADS_FOX],
                                     g_mix_l[l]["dbg"][0]]) for l in range(DEPTH)])
    drel = jnp.stack([g_mix_l[l]["dtab"][:, :N_REL].T for l in range(DEPTH)])
    add_small(dg1, g_ffn1, m_g_ffn1, v_g_ffn1)
    add_small(dgm, g_mix, m_g_mix, v_g_mix)
    add_small(db, b_in, m_b_in, v_b_in)
    add_small(drel, rel_bias, m_rel_bias, v_rel_bias)
    add_small(dg2, g_ffn2, m_g_ffn2, v_g_ffn2)
    add_small(dg_final[0], g_final, m_g_final, v_g_final)
    loss_piece = loss_blk[0, 0:1]
    small_packed = _pack_small(small_pieces + [loss_piece])

    recv = {}
    last = ("ffn1", 0)
    for l in reversed(range(DEPTH)):
        for grp in ("ffn2", "mix", "ffn1"):
            if (grp, l) != last:
                recv[(grp, l)] = exchange_wait(scatters[(grp, l)], dx, f"scattered_{grp}_l{l}")

    def upd(parts, w, m, v, tr, name, rb0=0):
        _, r, c = w.shape
        nr = r // tr

        def p_spec(layer):
            pinned = (nr - 1) if layer == 0 else 0
            return pl.BlockSpec((N_DEV, None, tr, c),
                                lambda l, i: (0, 0, rb0 + jnp.where(l == layer, i, pinned), 0))

        return adamw(parts, w, m, v, (DEPTH, nr), [p_spec(0), p_spec(1)],
                     pl.BlockSpec((None, tr, c), lambda l, i: (l, i, 0)), name)

    def both(grp, k):
        return [recv[(grp, l)][k] for l in range(DEPTH)]

    out_rows = D_FF // N_DEV // 2
    def upd_transposed(parts, w, m, v, tr, name):
        tp = lambda a: jnp.transpose(a, (0, 2, 1))
        return [tp(o) for o in upd(parts, tp(w), tp(m), tp(v), tr, name)]

    in_rows = FF_BLK // 4
    r_ffn2_in = upd_transposed(both("ffn2", 0), w_ffn2_in, m_w_ffn2_in, v_w_ffn2_in, in_rows, "adamw_ffn2_in")
    r_ffn2_out = upd(both("ffn2", 1), w_ffn2_out, m_w_ffn2_out, v_w_ffn2_out, out_rows, "adamw_ffn2_out")
    r_out = upd(both("mix", 3), w_out, m_w_out, v_w_out, LANES, "adamw_w_out")
    r_br_sb = upd(both("mix", 4), w_br_sb, m_w_br_sb, v_w_br_sb, 256, "adamw_br_sb", rb0=0)
    r_br_ch = upd(both("mix", 4), w_br_ch, m_w_br_ch, v_w_br_ch, 256, "adamw_br_ch", rb0=1)
    r_br_fox = upd(both("mix", 4), w_br_fox, m_w_br_fox, v_w_br_fox, 256, "adamw_br_fox", rb0=3)

    def w_in_grad(l):
        pieces = [recv[("mix", l)][k] for k in (0, 2, 1)]
        gq, gf, gg = sum_parts_multi(pieces, f"sum_w_in_l{l}")
        return jnp.concatenate([gq, gf[:, :N_HEADS_FOX], gg], axis=1)

    g_w_in = jnp.stack([w_in_grad(l) for l in range(DEPTH)])
    to_cols = lambda a: jnp.transpose(a, (2, 0, 1))
    n_cols = w_in.shape[2]
    col_blk = n_cols // 4
    win_spec = pl.BlockSpec((col_blk, DEPTH, LANES), lambda i: (i, 0, 0))
    r_in = adamw([to_cols(g_w_in)[None]], to_cols(w_in), to_cols(m_w_in), to_cols(v_w_in), (4,),
                 [pl.BlockSpec((1, col_blk, DEPTH, LANES), lambda i: (0, i, 0, 0))], win_spec, "adamw_w_in")
    r_in = [jnp.transpose(o, (1, 2, 0)) for o in r_in]

    recv[last] = exchange_wait(scatters[last], r_in[1], "scattered_ffn1_l0")
    r_ffn1_in = upd_transposed(both("ffn1", 0), w_ffn1_in, m_w_ffn1_in, v_w_ffn1_in, in_rows, "adamw_ffn1_in")
    r_ffn1_out = upd(both("ffn1", 1), w_ffn1_out, m_w_ffn1_out, v_w_ffn1_out, out_rows, "adamw_ffn1_out")

    small_sum = all_reduce_small(small_packed, "allreduce_small", deps=(r_ffn1_out[1],))
    n_small = sum(int(np.prod(s)) for s in small_shapes)
    loss = small_sum.reshape(-1)[n_small]
    sm_spec = pl.BlockSpec((SMALL_ROWS, LANES), lambda i: (0, 0))
    sm_out = adamw([small_sum[None]], _pack_small(small_w), _pack_small(small_m), _pack_small(small_v),
                   (1,), [pl.BlockSpec((1, SMALL_ROWS, LANES), lambda i: (0, 0, 0))], sm_spec, "adamw_small")
    sm_g, sm_d, sm_m, sm_v = [_unpack_small(o, small_shapes) for o in sm_out]

    def per_kind(k):
        small = (sm_g, sm_d, sm_m, sm_v)[k]
        return [small[0], r_ffn1_in[k], r_ffn1_out[k], small[1], r_in[k], small[2], small[3],
                r_br_sb[k], r_br_ch[k], r_br_fox[k], r_out[k], small[4], r_ffn2_in[k], r_ffn2_out[k],
                small[5]]

    return (loss, dx[None], *per_kind(0), *per_kind(1), *per_kind(2), *per_kind(3))
```

```python
import numpy as np
import jax
import jax.numpy as jnp
from jax import lax
from jax.experimental import pallas as pl
from jax.experimental.pallas import tpu as pltpu

F32 = jnp.float32
BF16 = jnp.bfloat16

N_DEV = 8
D_MODEL = 1024
DEPTH = 2
HEAD_DIM = 64
W_SB, W_CH, W_FOX = 256, 512, 256
QKV_WIDTH = 3 * (W_SB + W_CH + W_FOX)
N_HEADS_FOX = 4
N_HEADS_CH = 8
D_FF = 2816
FF_BLK = 2 * D_FF // N_DEV
CHUNK = 64
LEFT_CHUNKS = 8
MAX_REL = 128
N_REL = 2 * MAX_REL + 1
REL_PAD = 384
QB = 128
KB = 512
KSUB = KB // QB
CH_WIN = 5
CH_KEYS = CH_WIN * QB
RMS_EPS = 1e-6
NEG = -1e30
SCALE = HEAD_DIM ** -0.5
LANES = 128
VMEM_LIMIT = 56 * 1024 * 1024

ADAM_LR, ADAM_B1, ADAM_B2, ADAM_EPS, ADAM_WD, ADAM_STEP = 0.001, 0.9, 0.999, 1e-08, 0.01, 10

SMALL_ROWS = 192

MESH = pl.DeviceIdType.MESH
ANY = pl.BlockSpec(memory_space=pl.ANY)
HIGHEST = lax.Precision.HIGHEST

NN = (((1,), (0,)), ((), ()))
NT = (((1,), (1,)), ((), ()))
TN = (((0,), (0,)), ((), ()))


def _cparams(n_grid):
    return pltpu.CompilerParams(dimension_semantics=("arbitrary",) * n_grid,
                                vmem_limit_bytes=VMEM_LIMIT)


def _sds(shape, dtype):
    return jax.ShapeDtypeStruct(tuple(shape), dtype)


def _my_index():
    return 4 * lax.axis_index("x") + 2 * lax.axis_index("y") + lax.axis_index("c")


def _peer(mask):
    x, y, c = lax.axis_index("x"), lax.axis_index("y"), lax.axis_index("c")
    px = x ^ ((mask >> 2) & 1)
    py = y ^ ((mask >> 1) & 1)
    pc = c ^ (mask & 1)
    return (px, py, pc), 4 * px + 2 * py + pc


HBM_SPEC = pl.BlockSpec(memory_space=pltpu.HBM)
SEM_SPEC = pl.BlockSpec(memory_space=pltpu.SEMAPHORE)
EFFECT = pltpu.SideEffectType.DATAFLOW_SIDE_EFFECTING


def _exchange_refs(mode, in_ref, land_ref, me, pidx):
    if mode == "gather":
        return in_ref, land_ref.at[:, me], land_ref.at[:, pidx]
    return in_ref.at[:, pidx], land_ref.at[me], land_ref.at[pidx]


def _landing_shape(mode, a):
    if mode == "gather":
        s, r, c = a.shape
        return (s, N_DEV, r, c)
    s, _, r, c = a.shape
    return (N_DEV, s, r, c)


def _own_copy(mode, in_ref, land_ref, me, sem):
    if mode == "gather":
        return pltpu.make_async_copy(in_ref, land_ref.at[:, me], sem)
    return pltpu.make_async_copy(in_ref.at[:, me], land_ref.at[me], sem)


def exchange_start(mode, arrays, name, deps=()):
    n = len(arrays)
    lands0 = [lax.empty(_landing_shape(mode, a), a.dtype) for a in arrays]

    def body(*refs):
        in_refs, land_refs = refs[:n], refs[n:2 * n]
        outs_at = 2 * n + len(deps)
        send_sems, recv_sems, own_sems, token = refs[outs_at], refs[outs_at + 1], refs[outs_at + 2], refs[-1]
        mine = _my_index()
        for k in range(n):
            _own_copy(mode, in_refs[k], land_refs[k], mine, own_sems.at[k]).start()
            for mask in range(1, N_DEV):
                peer, pidx = _peer(mask)
                src, dst, _ = _exchange_refs(mode, in_refs[k], land_refs[k], mine, pidx)
                sem = k * (N_DEV - 1) + mask - 1
                pltpu.make_async_remote_copy(
                    src_ref=src, dst_ref=dst, send_sem=send_sems.at[sem], recv_sem=recv_sems.at[sem],
                    device_id=peer, device_id_type=MESH).start()
        token[...] = jnp.zeros_like(token)

    nsem = n * (N_DEV - 1)
    outs = pl.pallas_call(
        body, name=name,
        out_shape=(pltpu.SemaphoreType.DMA((nsem,)), pltpu.SemaphoreType.DMA((nsem,)),
                   pltpu.SemaphoreType.DMA((n,)),
                   *[pltpu.HBM(a.shape, a.dtype) for a in arrays],
                   *[pltpu.HBM(l.shape, l.dtype) for l in lands0], _sds((8, LANES), F32)),
        in_specs=[HBM_SPEC] * (2 * n) + [ANY] * len(deps),
        out_specs=(SEM_SPEC, SEM_SPEC, SEM_SPEC, *[HBM_SPEC] * (2 * n),
                   pl.BlockSpec(memory_space=pltpu.VMEM)),
        input_output_aliases={k: 3 + k for k in range(2 * n)},
        compiler_params=pltpu.CompilerParams(has_side_effects=EFFECT),
    )(*[pltpu.with_memory_space_constraint(a, pltpu.HBM) for a in arrays],
      *[pltpu.with_memory_space_constraint(l, pltpu.HBM) for l in lands0], *deps)
    return dict(mode=mode, n=n, send=outs[0], recv=outs[1], own=outs[2], ins=outs[3:3 + n],
                lands=outs[3 + n:3 + 2 * n], token=outs[-1])


def exchange_wait(handle, after, name):
    n, mode = handle["n"], handle["mode"]

    def body(*refs):
        in_refs, land_refs = refs[:n], refs[n:2 * n]
        send_sems, recv_sems, own_sems = refs[2 * n], refs[2 * n + 1], refs[2 * n + 2]
        mine = _my_index()
        for k in range(n):
            _own_copy(mode, in_refs[k], land_refs[k], mine, own_sems.at[k]).wait()
            for mask in range(1, N_DEV):
                peer, pidx = _peer(mask)
                src, _, here = _exchange_refs(mode, in_refs[k], land_refs[k], mine, pidx)
                sem = k * (N_DEV - 1) + mask - 1
                cp = pltpu.make_async_remote_copy(
                    src_ref=src, dst_ref=here, send_sem=send_sems.at[sem], recv_sem=recv_sems.at[sem],
                    device_id=peer, device_id_type=MESH)
                cp.wait_send()
                cp.wait_recv()

    thru = (*handle["ins"], *handle["lands"])
    outs = pl.pallas_call(
        body, name=name,
        out_shape=tuple(pltpu.HBM(a.shape, a.dtype) for a in thru),
        in_specs=[HBM_SPEC] * (2 * n) + [SEM_SPEC, SEM_SPEC, SEM_SPEC, ANY],
        out_specs=tuple([HBM_SPEC] * (2 * n)),
        input_output_aliases={k: k for k in range(2 * n)},
        compiler_params=pltpu.CompilerParams(has_side_effects=EFFECT),
    )(*thru, handle["send"], handle["recv"], handle["own"], after)
    return list(outs[n:])


FAR_MASKS = (2, 4, 6)
PHASE1_MASKS = (1,) + FAR_MASKS


def gather_start(arrays, name, deps=()):
    n = len(arrays)
    n1 = len(PHASE1_MASKS)
    lands0 = [lax.empty(_landing_shape("gather", a), a.dtype) for a in arrays]

    def body(*refs):
        in_refs, land_refs = refs[:n], refs[n:2 * n]
        outs_at = 2 * n + len(deps)
        send_sems, recv_sems, own_sems, token = refs[outs_at], refs[outs_at + 1], refs[outs_at + 2], refs[-1]
        mine = _my_index()
        for k in range(n):
            _own_copy("gather", in_refs[k], land_refs[k], mine, own_sems.at[k]).start()
            for j, mask in enumerate(PHASE1_MASKS):
                peer, _ = _peer(mask)
                pltpu.make_async_remote_copy(
                    src_ref=in_refs[k], dst_ref=land_refs[k].at[:, mine],
                    send_sem=send_sems.at[k * n1 + j], recv_sem=recv_sems.at[k * n1 + j],
                    device_id=peer, device_id_type=MESH).start()
        token[...] = jnp.zeros_like(token)

    outs = pl.pallas_call(
        body, name=name,
        out_shape=(pltpu.SemaphoreType.DMA((n * n1,)), pltpu.SemaphoreType.DMA((n * n1,)),
                   pltpu.SemaphoreType.DMA((n,)),
                   *[pltpu.HBM(a.shape, a.dtype) for a in arrays],
                   *[pltpu.HBM(l.shape, l.dtype) for l in lands0], _sds((8, LANES), F32)),
        in_specs=[HBM_SPEC] * (2 * n) + [ANY] * len(deps),
        out_specs=(SEM_SPEC, SEM_SPEC, SEM_SPEC, *[HBM_SPEC] * (2 * n),
                   pl.BlockSpec(memory_space=pltpu.VMEM)),
        input_output_aliases={k: 3 + k for k in range(2 * n)},
        compiler_params=pltpu.CompilerParams(has_side_effects=EFFECT),
    )(*[pltpu.with_memory_space_constraint(a, pltpu.HBM) for a in arrays],
      *[pltpu.with_memory_space_constraint(l, pltpu.HBM) for l in lands0], *deps)
    return dict(n=n, send=outs[0], recv=outs[1], own=outs[2], ins=outs[3:3 + n],
                lands=outs[3 + n:3 + 2 * n], token=outs[-1], name=name)


def gather_relay(handle, after):
    n = handle["n"]
    n1, n2 = len(PHASE1_MASKS), len(FAR_MASKS)

    def body(*refs):
        in_refs, land_refs = refs[:n], refs[n:2 * n]
        send1, recv1 = refs[2 * n], refs[2 * n + 1]
        send2, recv2, token = refs[2 * n + 3], refs[2 * n + 4], refs[-1]
        token[...] = jnp.zeros_like(token)
        sibling, _ = _peer(1)
        for k in range(n):
            for j, mask in enumerate(FAR_MASKS):
                peer, pidx = _peer(mask)
                landed = land_refs[k].at[:, pidx]
                pltpu.make_async_remote_copy(
                    src_ref=in_refs[k], dst_ref=landed, send_sem=send1.at[k * n1 + 1 + j],
                    recv_sem=recv1.at[k * n1 + 1 + j], device_id=peer, device_id_type=MESH).wait_recv()
                pltpu.make_async_remote_copy(
                    src_ref=landed, dst_ref=landed, send_sem=send2.at[k * n2 + j],
                    recv_sem=recv2.at[k * n2 + j], device_id=sibling, device_id_type=MESH).start()

    thru = (*handle["ins"], *handle["lands"])
    outs = pl.pallas_call(
        body, name=handle["name"] + "_relay",
        out_shape=(pltpu.SemaphoreType.DMA((n * n2,)), pltpu.SemaphoreType.DMA((n * n2,)),
                   *[pltpu.HBM(a.shape, a.dtype) for a in thru], _sds((8, LANES), F32)),
        in_specs=[HBM_SPEC] * (2 * n) + [SEM_SPEC, SEM_SPEC, ANY],
        out_specs=(SEM_SPEC, SEM_SPEC, *[HBM_SPEC] * (2 * n), pl.BlockSpec(memory_space=pltpu.VMEM)),
        input_output_aliases={k: 2 + k for k in range(2 * n)},
        compiler_params=pltpu.CompilerParams(has_side_effects=EFFECT),
    )(*thru, handle["send"], handle["recv"], after)
    handle.update(send2=outs[0], recv2=outs[1], ins=outs[2:2 + n], lands=outs[2 + n:2 + 2 * n],
                  relay_token=outs[-1])


def gather_finish(handle, after):
    n = handle["n"]
    n1, n2 = len(PHASE1_MASKS), len(FAR_MASKS)

    def body(*refs):
        in_refs, land_refs = refs[:n], refs[n:2 * n]
        send1, recv1, own_sems, send2, recv2 = refs[2 * n:2 * n + 5]
        mine = _my_index()
        sibling, sib_idx = _peer(1)
        for k in range(n):
            _own_copy("gather", in_refs[k], land_refs[k], mine, own_sems.at[k]).wait()
            for j, mask in enumerate(PHASE1_MASKS):
                peer, pidx = _peer(mask)
                cp = pltpu.make_async_remote_copy(
                    src_ref=in_refs[k], dst_ref=land_refs[k].at[:, pidx], send_sem=send1.at[k * n1 + j],
                    recv_sem=recv1.at[k * n1 + j], device_id=peer, device_id_type=MESH)
                cp.wait_send()
                if mask == 1:
                    cp.wait_recv()
            for j, mask in enumerate(FAR_MASKS):
                _, pidx = _peer(mask)
                _, far_of_sibling = _peer(mask ^ 1)
                cp = pltpu.make_async_remote_copy(
                    src_ref=land_refs[k].at[:, pidx], dst_ref=land_refs[k].at[:, far_of_sibling],
                    send_sem=send2.at[k * n2 + j], recv_sem=recv2.at[k * n2 + j],
                    device_id=sibling, device_id_type=MESH)
                cp.wait_send()
                cp.wait_recv()

    thru = (*handle["ins"], *handle["lands"])
    outs = pl.pallas_call(
        body, name=handle["name"] + "_finish",
        out_shape=tuple(pltpu.HBM(a.shape, a.dtype) for a in thru),
        in_specs=[HBM_SPEC] * (2 * n) + [SEM_SPEC] * 5 + [ANY],
        out_specs=tuple([HBM_SPEC] * (2 * n)),
        input_output_aliases={k: k for k in range(2 * n)},
        compiler_params=pltpu.CompilerParams(has_side_effects=EFFECT),
    )(*thru, handle["send"], handle["recv"], handle["own"], handle["send2"], handle["recv2"], after)
    return list(outs[n:])


def all_reduce_small(packed, name, deps=()):
    rows = packed.shape[0]
    nd = len(deps)

    def body(in_ref, *rest):
        out_ref, slots, send_sems, recv_sems = rest[nd:]
        me = _my_index()
        sends = []
        for mask in range(1, N_DEV):
            peer, _ = _peer(mask)
            cp = pltpu.make_async_remote_copy(
                src_ref=in_ref, dst_ref=slots.at[me],
                send_sem=send_sems.at[mask - 1], recv_sem=recv_sems.at[mask - 1],
                device_id=peer, device_id_type=MESH)
            cp.start()
            sends.append(cp)
        slots[me] = in_ref[...]
        for mask in range(1, N_DEV):
            peer, pidx = _peer(mask)
            pltpu.make_async_remote_copy(
                src_ref=in_ref, dst_ref=slots.at[pidx],
                send_sem=send_sems.at[mask - 1], recv_sem=recv_sems.at[mask - 1],
                device_id=peer, device_id_type=MESH).wait_recv()
        for cp in sends:
            cp.wait_send()
        total = slots[0]
        for p in range(1, N_DEV):
            total = total + slots[p]
        out_ref[...] = total

    return pl.pallas_call(
        body, name=name,
        out_shape=_sds((rows, LANES), F32),
        in_specs=[pl.BlockSpec(memory_space=pltpu.VMEM)] + [ANY] * nd,
        out_specs=pl.BlockSpec(memory_space=pltpu.VMEM),
        scratch_shapes=[pltpu.VMEM((N_DEV, rows, LANES), F32),
                        pltpu.SemaphoreType.DMA((N_DEV - 1,)),
                        pltpu.SemaphoreType.DMA((N_DEV - 1,))],
    )(packed, *deps)


def matmul(dims, a, b, out_sds, grid, a_spec, b_spec, o_spec, acc_shape, *, name, alpha=1.0,
           bias=None, bias_spec=None, scale=None, scale_spec=None, res=None, res_spec=None,
           colsum_sds=None, colsum_spec=None, out_t_sds=None, out_t_spec=None, deps=()):
    nk = grid[2]
    has_bias, has_scale, has_res = bias is not None, scale is not None, res is not None
    has_cs, has_t = colsum_sds is not None, out_t_sds is not None
    if has_cs:
        assert grid[0] == 1 and dims == TN

    def body(*refs):
        a_ref, b_ref = refs[0], refs[1]
        pos = 2
        bias_ref = scale_ref = res_ref = cs_ref = ot_ref = None
        if has_bias:
            bias_ref = refs[pos]; pos += 1
        if has_scale:
            scale_ref = refs[pos]; pos += 1
        if has_res:
            res_ref = refs[pos]; pos += 1
        pos += len(deps)
        o_ref = refs[pos]; pos += 1
        if has_cs:
            cs_ref = refs[pos]; pos += 1
        if has_t:
            ot_ref = refs[pos]; pos += 1
        k = pl.program_id(2)
        bval = b_ref[...]
        part = lax.dot_general(a_ref[...].astype(BF16), bval.astype(BF16), dims,
                               preferred_element_type=F32)

        def finish(total):
            r = total * alpha if alpha != 1.0 else total
            if has_bias:
                r = r + bias_ref[...]
            if has_scale:
                r = r * scale_ref[...]
            if has_res:
                r = r + res_ref[...].astype(F32)
            o_ref[...] = r.astype(o_ref.dtype)
            if has_t:
                ot_ref[...] = r.T.astype(ot_ref.dtype)

        if has_cs:
            csum = jnp.sum(bval.astype(F32), axis=0, keepdims=True)

            @pl.when(k == 0)
            def _():
                cs_ref[...] = csum

            @pl.when(k > 0)
            def _():
                cs_ref[...] += csum

        if nk == 1:
            finish(part)
        else:
            acc_ref = refs[pos]

            @pl.when(k == 0)
            def _():
                acc_ref[...] = part

            @pl.when(k > 0)
            def _():
                acc_ref[...] += part

            @pl.when(k == nk - 1)
            def _():
                finish(acc_ref[...])

    in_specs, args = [a_spec, b_spec], [a, b]
    if has_bias:
        in_specs.append(bias_spec); args.append(bias)
    if has_scale:
        in_specs.append(scale_spec); args.append(scale)
    if has_res:
        in_specs.append(res_spec); args.append(res)
    in_specs += [ANY] * len(deps)
    args += list(deps)
    out_shape, out_specs = [out_sds], [o_spec]
    if has_cs:
        out_shape.append(colsum_sds); out_specs.append(colsum_spec)
    if has_t:
        out_shape.append(out_t_sds); out_specs.append(out_t_spec)
    scratch = [] if nk == 1 else [pltpu.VMEM(acc_shape, F32)]
    outs = pl.pallas_call(
        body, name=name, grid=grid, in_specs=in_specs, out_specs=out_specs, out_shape=out_shape,
        scratch_shapes=scratch, compiler_params=_cparams(3))(*args)
    return outs if (has_cs or has_t) else outs[0]


def _sigmoid(z):
    return 1.0 / (1.0 + jnp.exp(-z))


def _log_sigmoid(z):
    return jnp.minimum(z, 0.0) - jnp.log(1.0 + jnp.exp(-jnp.abs(z)))


def rmsnorm_fwd(x, gain, tm, name, deps=()):
    t, d = x.shape

    def body(x_ref, g_ref, *rest):
        o_ref = rest[-1]
        xf = x_ref[...]
        r = lax.rsqrt(jnp.mean(xf * xf, axis=-1, keepdims=True) + RMS_EPS)
        o_ref[...] = (xf * r * g_ref[...]).astype(o_ref.dtype)

    return pl.pallas_call(
        body, name=name, grid=(t // tm,),
        in_specs=[pl.BlockSpec((tm, d), lambda i: (i, 0)), pl.BlockSpec((1, d), lambda i: (0, 0))]
        + [ANY] * len(deps),
        out_specs=pl.BlockSpec((tm, d), lambda i: (i, 0)),
        out_shape=_sds((t, d), BF16), compiler_params=_cparams(1))(x, gain, *deps)


def loss_head(x, gain, target, tm, name):
    t, d = x.shape

    def body(x_ref, g_ref, tgt_ref, dx_ref, dg_ref, loss_ref):
        i = pl.program_id(0)
        xf = x_ref[...]
        g = g_ref[...]
        r = lax.rsqrt(jnp.mean(xf * xf, axis=-1, keepdims=True) + RMS_EPS)
        xhat = xf * r
        err = xhat * g - tgt_ref[...]
        part = 0.5 * jnp.sum(jnp.mean(err * err, axis=-1, keepdims=True))
        dy = err * (1.0 / d)
        dxhat = dy * g
        dx_ref[...] = r * (dxhat - xhat * jnp.mean(dxhat * xhat, axis=-1, keepdims=True))
        dg = jnp.sum(dy * xhat, axis=0, keepdims=True)
        lpart = jnp.full((8, LANES), part, F32)

        @pl.when(i == 0)
        def _():
            dg_ref[...] = dg
            loss_ref[...] = lpart

        @pl.when(i > 0)
        def _():
            dg_ref[...] += dg
            loss_ref[...] += lpart

    row = pl.BlockSpec((tm, d), lambda i: (i, 0))
    vec = pl.BlockSpec((1, d), lambda i: (0, 0))
    return pl.pallas_call(
        body, name=name, grid=(t // tm,), in_specs=[row, vec, row],
        out_specs=[row, vec, pl.BlockSpec((8, LANES), lambda i: (0, 0))],
        out_shape=[_sds((t, d), F32), _sds((1, d), F32), _sds((8, LANES), F32)],
        compiler_params=_cparams(1))(x, gain, target)


def ffn_in_swiglu(hn, wa, s, tm, name):
    t = hn.shape[0]
    halves = 2 if tm % 512 == 0 else 1
    rows = tm // halves

    def body(h_ref, wg_ref, wu_ref, gu_ref, act_ref):
        for c in range(halves):
            rs = slice(c * rows, (c + 1) * rows)
            h = h_ref[rs, :]
            g = jnp.dot(h, wg_ref[...], preferred_element_type=F32)
            u = jnp.dot(h, wu_ref[...], preferred_element_type=F32)
            gu_ref[0, rs, :] = g.astype(gu_ref.dtype)
            gu_ref[1, rs, :] = u.astype(gu_ref.dtype)
            act_ref[rs, :] = (g * _sigmoid(g) * u).astype(act_ref.dtype)

    return pl.pallas_call(
        body, name=name, grid=(t // tm, 4),
        in_specs=[pl.BlockSpec((tm, D_MODEL), lambda i, j: (i, 0)),
                  pl.BlockSpec((None, None, D_MODEL, FF_BLK), lambda i, j: (s, j, 0, 0)),
                  pl.BlockSpec((None, None, D_MODEL, FF_BLK), lambda i, j: (s, j + 4, 0, 0))],
        out_specs=[pl.BlockSpec((None, 2, tm, FF_BLK), lambda i, j: (j, 0, i, 0)),
                   pl.BlockSpec((None, tm, FF_BLK), lambda i, j: (j, i, 0))],
        out_shape=[_sds((4, 2, t, FF_BLK), BF16), _sds((4, t, FF_BLK), BF16)],
        compiler_params=_cparams(2))(hn, wa, wa)


def ffn_dact_swiglu(dy, wb, gu, s, tm, name):
    t = dy.shape[0]

    def body(dy_ref, w_ref, gu_ref, o_ref):
        da = 0.5 * lax.dot_general(dy_ref[...].astype(BF16), w_ref[...], NT, preferred_element_type=F32)
        g = gu_ref[0].astype(F32)
        u = gu_ref[1].astype(F32)
        sg = _sigmoid(g)
        o_ref[0] = (da * u * (sg * (1.0 + g * (1.0 - sg)))).astype(o_ref.dtype)
        o_ref[1] = (da * g * sg).astype(o_ref.dtype)

    blk = pl.BlockSpec((None, 2, tm, FF_BLK), lambda i, j: (j, 0, i, 0))
    return pl.pallas_call(
        body, name=name, grid=(t // tm, 4),
        in_specs=[pl.BlockSpec((tm, D_MODEL), lambda i, j: (i, 0)),
                  pl.BlockSpec((None, None, FF_BLK, D_MODEL), lambda i, j: (s, j, 0, 0)), blk],
        out_specs=blk, out_shape=_sds((4, 2, t, FF_BLK), BF16),
        compiler_params=_cparams(2))(dy, wb, gu)


def ffn_out_residual(act, wb, x, s, tm, name, deps=()):
    t = x.shape[0]

    def body(a_ref, w_ref, x_ref, *rest):
        o_ref = rest[-1]
        acc = jnp.dot(a_ref[0], w_ref[0], preferred_element_type=F32)
        for k in range(1, 4):
            acc = acc + jnp.dot(a_ref[k], w_ref[k], preferred_element_type=F32)
        o_ref[...] = x_ref[...] + 0.5 * acc

    row = pl.BlockSpec((tm, D_MODEL), lambda i: (i, 0))
    return pl.pallas_call(
        body, name=name, grid=(t // tm,),
        in_specs=[pl.BlockSpec((4, tm, FF_BLK), lambda i: (0, i, 0)),
                  pl.BlockSpec((None, 4, FF_BLK, D_MODEL), lambda i: (s, 0, 0, 0)), row] + [ANY] * len(deps),
        out_specs=row, out_shape=_sds((t, D_MODEL), F32), compiler_params=_cparams(1))(act, wb, x, *deps)


def ffn_dh_norm_bwd(dgu, wa, s, x, gain, dres, tm, name, deps):
    t = dgu.shape[2]
    nd = len(deps)

    def body(g_ref, w_ref, x_ref, gain_ref, dres_ref, *rest):
        dx_ref, dg_ref = rest[nd:]
        i = pl.program_id(0)
        dh = lax.dot_general(g_ref[0, 0], w_ref[0], NT, preferred_element_type=F32)
        for p in range(1, N_DEV):
            dh = dh + lax.dot_general(g_ref[p % 4, p // 4], w_ref[p], NT, preferred_element_type=F32)
        xf = x_ref[...]
        r = lax.rsqrt(jnp.mean(xf * xf, axis=-1, keepdims=True) + RMS_EPS)
        xhat = xf * r
        dxhat = dh * gain_ref[...]
        dx_ref[...] = dres_ref[...] + r * (dxhat - xhat * jnp.mean(dxhat * xhat, axis=-1, keepdims=True))
        dg = jnp.sum(dh * xhat, axis=0, keepdims=True)

        @pl.when(i == 0)
        def _():
            dg_ref[...] = dg

        @pl.when(i > 0)
        def _():
            dg_ref[...] += dg

    row = pl.BlockSpec((tm, D_MODEL), lambda i: (i, 0))
    vec = pl.BlockSpec((1, D_MODEL), lambda i: (0, 0))
    return pl.pallas_call(
        body, name=name, grid=(t // tm,),
        in_specs=[pl.BlockSpec((4, 2, tm, FF_BLK), lambda i: (0, 0, i, 0)),
                  pl.BlockSpec((None, N_DEV, D_MODEL, FF_BLK), lambda i: (s, 0, 0, 0)), row, vec, row]
        + [ANY] * nd,
        out_specs=[row, vec], out_shape=[_sds((t, D_MODEL), F32), _sds((1, D_MODEL), F32)],
        compiler_params=_cparams(1))(dgu, wa, x, gain, dres, *deps)


def branch_merge(os_, wbr, layer, gates, tm, name, deps=()):
    t = os_[0].shape[0]
    d = D_MODEL
    nd = len(deps)
    widths = [o.shape[1] for o in os_]
    starts = [sum(widths[:a]) for a in range(3)]

    def body(oa_ref, ob_ref, oc_ref, w_ref, g_ref, *rest):
        ya_ref, yb_ref, yc_ref, m_ref = rest[nd:]
        merged = None
        for a, (o_ref, y_ref) in enumerate(((oa_ref, ya_ref), (ob_ref, yb_ref), (oc_ref, yc_ref))):
            y = jnp.dot(o_ref[...], w_ref[starts[a]:starts[a] + widths[a], :], preferred_element_type=F32)
            y_ref[...] = y
            term = _sigmoid(g_ref[:, a * d:(a + 1) * d]) * y
            merged = term if merged is None else merged + term
        m_ref[...] = merged.astype(m_ref.dtype)

    row = pl.BlockSpec((tm, d), lambda i: (i, 0))
    ya, yb, yc, merged = pl.pallas_call(
        body, name=name, grid=(t // tm,),
        in_specs=[pl.BlockSpec((tm, w), lambda i: (i, 0)) for w in widths]
        + [pl.BlockSpec((None, d, d), lambda i: (layer, 0, 0)), pl.BlockSpec((tm, 3 * d), lambda i: (i, 0))]
        + [ANY] * nd,
        out_specs=[row, row, row, row],
        out_shape=[_sds((t, d), F32)] * 3 + [_sds((t, d), BF16)],
        compiler_params=_cparams(1))(*os_, wbr, gates, *deps)
    return [ya, yb, yc], merged


def dmerged_merge_bwd(dy, wout, layer, gates, ys, tm, name):
    t, d = dy.shape

    def body(dy_ref, w_ref, g_ref, ya_ref, yb_ref, yc_ref, dg_ref, dya_ref, dyb_ref, dyc_ref):
        dmv = lax.dot_general(dy_ref[...].astype(BF16), w_ref[...], NT, preferred_element_type=F32)
        for a, (y_ref, dy_out) in enumerate(((ya_ref, dya_ref), (yb_ref, dyb_ref), (yc_ref, dyc_ref))):
            cols = slice(a * d, (a + 1) * d)
            s = _sigmoid(g_ref[:, cols])
            dy_out[...] = (dmv * s).astype(dy_out.dtype)
            dg_ref[:, cols] = (dmv * y_ref[...] * s * (1.0 - s)).astype(dg_ref.dtype)

    row = pl.BlockSpec((tm, d), lambda i: (i, 0))
    wide = pl.BlockSpec((tm, 3 * d), lambda i: (i, 0))
    dg, dya, dyb, dyc = pl.pallas_call(
        body, name=name, grid=(t // tm,),
        in_specs=[row, pl.BlockSpec((None, d, d), lambda i: (layer, 0, 0)), wide, row, row, row],
        out_specs=[wide, row, row, row],
        out_shape=[_sds((t, 3 * d), BF16)] + [_sds((t, d), BF16)] * 3,
        compiler_params=_cparams(1))(dy, wout, gates, *ys)
    return dg, [dya, dyb, dyc]


def branch_bwd(dys, os_, wbr, layer, tm, name):
    t = dys[0].shape[0]
    d = D_MODEL
    nt = t // tm
    widths = [o.shape[1] for o in os_]
    starts = [sum(widths[:a]) for a in range(3)]

    def body(dya_ref, dyb_ref, dyc_ref, oa_ref, ob_ref, oc_ref, w_ref, do_ref, dot_ref, dw_ref, acc_ref):
        i = pl.program_id(0)
        for a, (dy_ref, o_ref) in enumerate(((dya_ref, oa_ref), (dyb_ref, ob_ref), (dyc_ref, oc_ref))):
            rows = slice(starts[a], starts[a] + widths[a])
            dyv = dy_ref[...]
            do = lax.dot_general(dyv, w_ref[rows, :], NT, preferred_element_type=F32)
            do_ref[:, rows] = do.astype(do_ref.dtype)
            dot_ref[rows, :] = do.T.astype(dot_ref.dtype)
            dw = lax.dot_general(o_ref[...], dyv, TN, preferred_element_type=F32)

            @pl.when(i == 0)
            def _():
                acc_ref[rows, :] = dw

            @pl.when(i > 0)
            def _():
                acc_ref[rows, :] += dw

        @pl.when(i == nt - 1)
        def _():
            dw_ref[...] = acc_ref[...].astype(dw_ref.dtype)

    row = pl.BlockSpec((tm, d), lambda i: (i, 0))
    return pl.pallas_call(
        body, name=name, grid=(nt,),
        in_specs=[row, row, row] + [pl.BlockSpec((tm, w), lambda i: (i, 0)) for w in widths]
        + [pl.BlockSpec((None, d, d), lambda i: (layer, 0, 0))],
        out_specs=[row, pl.BlockSpec((d, tm), lambda i: (0, i)), pl.BlockSpec((d, d), lambda i: (0, 0))],
        out_shape=[_sds((t, d), BF16), _sds((d, t), BF16), _sds((d, d), BF16)],
        scratch_shapes=[pltpu.VMEM((d, d), F32)], compiler_params=_cparams(1))(*dys, *os_, wbr)


def mixer_dh_norm_bwd(dh_part, dgates, wc, gate_idx, df, wf, layer, x, gain, dres, tm, name):
    t, d = x.shape

    def body(dhp_ref, dg_ref, wg_ref, df_ref, wf_ref, x_ref, gain_ref, dres_ref, dx_ref, dgain_ref):
        i = pl.program_id(0)
        dh = (dhp_ref[...]
              + lax.dot_general(dg_ref[...], wg_ref[...], NT, preferred_element_type=F32)
              + lax.dot_general(df_ref[...].astype(BF16), wf_ref[...], NT, preferred_element_type=F32))
        xf = x_ref[...]
        r = lax.rsqrt(jnp.mean(xf * xf, axis=-1, keepdims=True) + RMS_EPS)
        xhat = xf * r
        dxhat = dh * gain_ref[...]
        dx_ref[...] = dres_ref[...] + r * (dxhat - xhat * jnp.mean(dxhat * xhat, axis=-1, keepdims=True))
        dg = jnp.sum(dh * xhat, axis=0, keepdims=True)

        @pl.when(i == 0)
        def _():
            dgain_ref[...] = dg

        @pl.when(i > 0)
        def _():
            dgain_ref[...] += dg

    row = pl.BlockSpec((tm, d), lambda i: (i, 0))
    vec = pl.BlockSpec((1, d), lambda i: (0, 0))
    return pl.pallas_call(
        body, name=name, grid=(t // tm,),
        in_specs=[row, pl.BlockSpec((tm, QKV_WIDTH), lambda i: (i, 0)),
                  pl.BlockSpec((None, d, QKV_WIDTH), lambda i: (gate_idx, 0, 0)),
                  pl.BlockSpec((tm, LANES), lambda i: (i, 0)),
                  pl.BlockSpec((None, d, LANES), lambda i: (layer, 0, 0)), row, vec, row],
        out_specs=[row, vec], out_shape=[_sds((t, d), F32), _sds((1, d), F32)],
        compiler_params=_cparams(1))(dh_part, dgates, wc, df, wf, x, gain, dres)


def _iota2(shape, dim):
    return lax.broadcasted_iota(jnp.int32, shape, dim)


def proj_gates_forget(hm, wc, gate_idx, wf, layer, bg, bf, tm, name, deps=()):
    t, d = hm.shape
    sub = tm // QB
    nd = len(deps)

    def body(h_ref, wg_ref, bg_ref, wf_ref, bf_ref, *rest):
        g_ref, f_ref, fcol_ref, frow_ref, carry = rest[nd:]
        i, j = pl.program_id(0), pl.program_id(1)
        h = h_ref[...]
        g_ref[...] = jnp.dot(h, wg_ref[...], preferred_element_type=F32) + bg_ref[...]

        @pl.when((i == 0) & (j == 0))
        def _():
            carry[...] = jnp.zeros_like(carry)

        @pl.when(j == 0)
        def _():
            f = jnp.dot(h, wf_ref[...], preferred_element_type=F32) + bf_ref[...]
            f_ref[...] = f
            logf = _log_sigmoid(f)
            tri = (_iota2((QB, QB), 1) <= _iota2((QB, QB), 0)).astype(F32)
            for s in range(sub):
                rows = slice(s * QB, (s + 1) * QB)
                part = logf[rows, :]
                blk = jnp.dot(tri, part, precision=HIGHEST, preferred_element_type=F32) + carry[...]
                carry[...] += jnp.sum(part, axis=0, keepdims=True)
                fcol_ref[rows, :] = blk
                frow_ref[s] = blk.T[0:8, :]

    narrow = pl.BlockSpec((tm, LANES), lambda i, j: (i, 0))
    return pl.pallas_call(
        body, name=name, grid=(t // tm, 3),
        in_specs=[pl.BlockSpec((tm, d), lambda i, j: (i, 0)),
                  pl.BlockSpec((None, d, d), lambda i, j: (gate_idx, 0, j)),
                  pl.BlockSpec((1, d), lambda i, j: (0, j)),
                  pl.BlockSpec((None, d, LANES), lambda i, j: (layer, 0, 0)),
                  pl.BlockSpec((1, LANES), lambda i, j: (0, 0))] + [ANY] * nd,
        out_specs=[pl.BlockSpec((tm, d), lambda i, j: (i, j)), narrow, narrow,
                   pl.BlockSpec((sub, 8, QB), lambda i, j: (i, 0, 0))],
        out_shape=[_sds((t, 3 * d), F32), _sds((t, LANES), F32), _sds((t, LANES), F32),
                   _sds((t // QB, 8, QB), F32)],
        scratch_shapes=[pltpu.VMEM((1, LANES), F32)], compiler_params=_cparams(2))(hm, wc, bg, wf, bf, *deps)


def forget_cumsum_bwd(dfrow, f, name):
    t = f.shape[0]
    nq = t // QB

    def body(dfr_ref, f_ref, df_ref, carry):
        jj = pl.program_id(0)

        @pl.when(jj == 0)
        def _():
            carry[...] = jnp.zeros_like(carry)

        padded = jnp.concatenate([dfr_ref[...], jnp.zeros((QB - 8, QB), F32)], axis=0)
        dfcol = padded.T
        tri = (_iota2((QB, QB), 1) >= _iota2((QB, QB), 0)).astype(F32)
        dlogf = jnp.dot(tri, dfcol, precision=HIGHEST, preferred_element_type=F32) + carry[...]
        carry[...] += jnp.sum(dfcol, axis=0, keepdims=True)
        df_ref[...] = dlogf * _sigmoid(-f_ref[...])

    return pl.pallas_call(
        body, name=name, grid=(nq,),
        in_specs=[pl.BlockSpec((None, 8, QB), lambda jj: (nq - 1 - jj, 0, 0)),
                  pl.BlockSpec((QB, LANES), lambda jj: (nq - 1 - jj, 0))],
        out_specs=pl.BlockSpec((QB, LANES), lambda jj: (nq - 1 - jj, 0)),
        out_shape=_sds((t, LANES), F32),
        scratch_shapes=[pltpu.VMEM((1, LANES), F32)], compiler_params=_cparams(1))(dfrow, f)


REL_DIAG = 768
REL_SHIFT = REL_DIAG - (QB - 1)


def _diag_onehot():
    u = _iota2((REL_PAD, REL_DIAG), 1)
    rel = jnp.clip(CH_KEYS - 1 - u, -MAX_REL, MAX_REL) + MAX_REL
    return (_iota2((REL_PAD, REL_DIAG), 0) == rel).astype(F32)


def rel_bias_build(tab_t, name):
    def body(tab_ref, o_ref):
        diag = jnp.dot(tab_ref[...], _diag_onehot(), precision=HIGHEST, preferred_element_type=F32)
        band = _chunk_band()
        for h in range(N_HEADS_CH):
            rows = jnp.broadcast_to(diag[h:h + 1, :], (QB, REL_DIAG))
            o_ref[h] = pltpu.roll(rows, REL_SHIFT, 1, stride=1, stride_axis=0)[:, :CH_KEYS] + band

    return pl.pallas_call(
        body, name=name, out_shape=_sds((N_HEADS_CH, QB, CH_KEYS), F32),
        in_specs=[pl.BlockSpec(memory_space=pltpu.VMEM)], out_specs=pl.BlockSpec(memory_space=pltpu.VMEM),
    )(tab_t)


def rel_bias_scatter(dbias, name):
    def body(db_ref, o_ref, ddiag):
        flip = (_iota2((QB, QB), 0) + _iota2((QB, QB), 1) == QB - 1).astype(F32)
        for h in range(N_HEADS_CH):
            padded = jnp.concatenate([db_ref[h], jnp.zeros((QB, REL_DIAG - CH_KEYS), F32)], axis=1)
            flipped = jnp.dot(flip, padded, precision=HIGHEST, preferred_element_type=F32)
            unrolled = pltpu.roll(flipped, 0, 1, stride=1, stride_axis=0)
            ddiag[h:h + 1, :] = jnp.sum(unrolled, axis=0, keepdims=True)
        o_ref[...] = lax.dot_general(ddiag[...], _diag_onehot(), NT, precision=HIGHEST,
                                     preferred_element_type=F32)

    return pl.pallas_call(
        body, name=name, out_shape=_sds((N_HEADS_CH, REL_PAD), F32),
        in_specs=[pl.BlockSpec(memory_space=pltpu.VMEM)], out_specs=pl.BlockSpec(memory_space=pltpu.VMEM),
        scratch_shapes=[pltpu.VMEM((N_HEADS_CH, REL_DIAG), F32)],
    )(dbias)


def _hl(h):
    return slice(h * HEAD_DIM, (h + 1) * HEAD_DIM)


def _split_dot(x, tri_bf16):
    hi = x.astype(BF16)
    lo = (x - hi.astype(F32)).astype(BF16)
    return (jnp.dot(hi, tri_bf16, preferred_element_type=F32)
            + jnp.dot(lo, tri_bf16, preferred_element_type=F32))


def _krows(g):
    return pl.ds(pl.multiple_of(g * KB, KB), KB)


def _log_sigmoid_pair(z):
    sp = jnp.log(1.0 + jnp.exp(-jnp.abs(z)))
    return jnp.minimum(z, 0.0) - sp, -jnp.maximum(z, 0.0) - sp


SB_STEP_HEADS = 4
SB_COLS = SB_STEP_HEADS * HEAD_DIM


def _sb_specs(t):
    n = W_SB // SB_COLS
    q_spec = pl.BlockSpec((QB, SB_COLS), lambda hp, i: (i, hp))
    k_spec = pl.BlockSpec((t, SB_COLS), lambda hp, i: (0, n + hp))
    v_spec = pl.BlockSpec((t, SB_COLS), lambda hp, i: (0, 2 * n + hp))
    return q_spec, k_spec, v_spec


def _keys_major(xt):
    n, groups, w, _ = xt.shape
    return xt.transpose(1, 3, 0, 2).reshape(groups * KB, n * w)


def sb_fwd(qkv, name):
    t = qkv.shape[0]
    nq = t // QB

    def body(q_ref, k_ref, v_ref, o_ref, w_ref):
        i = pl.program_id(1)
        groups = i // KSUB + 1
        tri_after = (_iota2((KB, KB), 0) > _iota2((KB, KB), 1)).astype(BF16)
        t_idx = i * QB + _iota2((QB, KB), 0)
        qs = [q_ref[:, _hl(h)] for h in range(SB_STEP_HEADS)]

        def step(g, carry, masked):
            strict = (g * KB + _iota2((QB, KB), 1)) < t_idx
            out = []
            for h in range(SB_STEP_HEADS):
                tail, acc = carry[2 * h], carry[2 * h + 1]
                k = k_ref[_krows(g), _hl(h)]
                v = v_ref[_krows(g), _hl(h)]
                z = lax.dot_general(qs[h], k, NT, preferred_element_type=F32)
                lb, lf = _log_sigmoid_pair(z)
                if masked:
                    lf = jnp.where(strict, lf, 0.0)
                between = _split_dot(lf, tri_after) + tail
                w = jnp.exp(lb + between)
                if masked:
                    w = jnp.where(strict, w, 0.0)
                w = w.astype(BF16)
                w_ref[h, g] = w
                acc = acc + jnp.dot(w, v, preferred_element_type=F32)
                out += [tail + jnp.sum(lf, axis=1, keepdims=True), acc]
            return tuple(out)

        init = (jnp.zeros((QB, 1), F32), jnp.zeros((QB, HEAD_DIM), F32)) * SB_STEP_HEADS
        res = step(groups - 1, init, True)
        res = lax.fori_loop(0, groups - 1, lambda gg, c: step(groups - 2 - gg, c, False), res)
        for h in range(SB_STEP_HEADS):
            o_ref[:, _hl(h)] = res[2 * h + 1].astype(o_ref.dtype)

    q_spec, k_spec, v_spec = _sb_specs(t)
    return pl.pallas_call(
        body, name=name, grid=(W_SB // SB_COLS, nq), in_specs=[q_spec, k_spec, v_spec],
        out_specs=[pl.BlockSpec((QB, SB_COLS), lambda hp, i: (i, hp)),
                   pl.BlockSpec((SB_STEP_HEADS, None, t // KB, QB, KB), lambda hp, i: (hp, i, 0, 0, 0))],
        out_shape=[_sds((t, W_SB), BF16), _sds((4, nq, t // KB, QB, KB), BF16)],
        compiler_params=_cparams(2))(qkv, qkv, qkv)


def _hs(h):
    return slice(h * HEAD_DIM, (h + 1) * HEAD_DIM)


def sb_bwd(qkv, qkv_t, w, do, do_t, name):
    t = qkv.shape[0]
    nq = t // QB

    def body(q_ref, k_ref, v_ref, do_ref, qt_ref, dot_ref, w_ref, dq_ref, dkt_ref, dvt_ref):
        i = pl.program_id(1)

        @pl.when(i == 0)
        def _():
            dkt_ref[...] = jnp.zeros_like(dkt_ref)
            dvt_ref[...] = jnp.zeros_like(dvt_ref)

        groups = i // KSUB + 1
        tri_before = (_iota2((KB, KB), 0) < _iota2((KB, KB), 1)).astype(BF16)
        t_idx = i * QB + _iota2((QB, KB), 0)
        qs = [q_ref[:, _hl(h)] for h in range(SB_STEP_HEADS)]
        dos = [do_ref[:, _hl(h)] for h in range(SB_STEP_HEADS)]
        qts = [qt_ref[_hs(h), :] for h in range(SB_STEP_HEADS)]
        dots = [dot_ref[_hs(h), :] for h in range(SB_STEP_HEADS)]

        def grads(g, carry, masked):
            strict = (g * KB + _iota2((QB, KB), 1)) < t_idx
            out = []
            for h in range(SB_STEP_HEADS):
                head, dq = carry[2 * h], carry[2 * h + 1]
                k = k_ref[_krows(g), _hl(h)]
                v = v_ref[_krows(g), _hl(h)]
                wb = w_ref[h, g]
                z = lax.dot_general(qs[h], k, NT, preferred_element_type=F32)
                beta = _sigmoid(z)
                e = lax.dot_general(dos[h], v, NT, preferred_element_type=F32) * wb.astype(F32)
                before = _split_dot(e, tri_before) + head
                dz = e * (1.0 - beta) - before * beta
                if masked:
                    dz = jnp.where(strict, dz, 0.0)
                dzb = dz.astype(BF16)
                dq = dq + jnp.dot(dzb, k, preferred_element_type=F32)
                dkt_ref[g, _hs(h), :] += jnp.dot(qts[h], dzb, preferred_element_type=F32)
                dvt_ref[g, _hs(h), :] += jnp.dot(dots[h], wb, preferred_element_type=F32)
                out += [head + jnp.sum(e, axis=1, keepdims=True), dq]
            return tuple(out)

        init = (jnp.zeros((QB, 1), F32), jnp.zeros((QB, HEAD_DIM), F32)) * SB_STEP_HEADS
        res = lax.fori_loop(0, groups - 1, lambda g, c: grads(g, c, False), init)
        res = grads(groups - 1, res, True)
        for h in range(SB_STEP_HEADS):
            dq_ref[:, _hl(h)] = (res[2 * h + 1] * SCALE).astype(dq_ref.dtype)

    q_spec, k_spec, v_spec = _sb_specs(t)
    blk = pl.BlockSpec((QB, SB_COLS), lambda hp, i: (i, hp))
    blk_t = pl.BlockSpec((SB_COLS, QB), lambda hp, i: (hp, i))
    acc_t = pl.BlockSpec((None, t // KB, SB_COLS, KB), lambda hp, i: (hp, 0, 0, 0))
    acc_sds = _sds((W_SB // SB_COLS, t // KB, SB_COLS, KB), F32)
    return pl.pallas_call(
        body, name=name, grid=(W_SB // SB_COLS, nq),
        in_specs=[q_spec, k_spec, v_spec, blk, blk_t, blk_t,
                  pl.BlockSpec((SB_STEP_HEADS, None, t // KB, QB, KB), lambda hp, i: (hp, i, 0, 0, 0))],
        out_specs=[blk, acc_t, acc_t],
        out_shape=[_sds((t, W_SB), BF16), acc_sds, acc_sds],
        compiler_params=_cparams(2))(qkv, qkv, qkv, do, qkv_t, do_t, w)


FOX_STEP_HEADS = 4
FOX_COLS = FOX_STEP_HEADS * HEAD_DIM


def _fox_specs(t):
    first = 3 * (W_SB + W_CH) // FOX_COLS
    n = W_FOX // FOX_COLS
    q_spec = pl.BlockSpec((QB, FOX_COLS), lambda hp, i: (i, first + hp))
    k_spec = pl.BlockSpec((t, FOX_COLS), lambda hp, i: (0, first + n + hp))
    v_spec = pl.BlockSpec((t, FOX_COLS), lambda hp, i: (0, first + 2 * n + hp))
    return q_spec, k_spec, v_spec


def fox_fwd(qkv, fcol, frow, name):
    t = qkv.shape[0]
    nq = t // QB

    def body(q_ref, k_ref, v_ref, fc_ref, fr_ref, o_ref, lse_ref):
        hp = pl.program_id(0)
        i = pl.program_id(1)
        groups = i // KSUB + 1
        t_idx = i * QB + _iota2((QB, KB), 0)
        lane = _iota2((QB, LANES), 1)
        sub = _iota2((8, KB), 0)
        qs = [q_ref[:, _hl(h)] for h in range(FOX_STEP_HEADS)]
        f_qs = [jnp.sum(jnp.where(lane == hp * FOX_STEP_HEADS + h, fc_ref[...], 0.0), axis=1, keepdims=True)
                for h in range(FOX_STEP_HEADS)]

        def step(g, carry, masked):
            causal = (g * KB + _iota2((QB, KB), 1)) <= t_idx
            fr = fr_ref[g]
            out = []
            for h in range(FOX_STEP_HEADS):
                m, l, acc = carry[3 * h:3 * h + 3]
                k = k_ref[_krows(g), _hl(h)]
                v = v_ref[_krows(g), _hl(h)]
                f_k = jnp.sum(jnp.where(sub == hp * FOX_STEP_HEADS + h, fr, 0.0), axis=0, keepdims=True)
                z = lax.dot_general(qs[h], k, NT, preferred_element_type=F32) + f_qs[h] - f_k
                if masked:
                    z = jnp.where(causal, z, NEG)
                m_new = jnp.maximum(m, jnp.max(z, axis=1, keepdims=True))
                p = jnp.exp(z - m_new)
                corr = jnp.exp(m - m_new)
                l = l * corr + jnp.sum(p, axis=1, keepdims=True)
                acc = acc * corr + jnp.dot(p.astype(BF16), v, preferred_element_type=F32)
                out += [m_new, l, acc]
            return tuple(out)

        init = (jnp.full((QB, 1), NEG, F32), jnp.zeros((QB, 1), F32),
                jnp.zeros((QB, HEAD_DIM), F32)) * FOX_STEP_HEADS
        res = lax.fori_loop(0, groups - 1, lambda g, c: step(g, c, False), init)
        res = step(groups - 1, res, True)
        for h in range(FOX_STEP_HEADS):
            m, l, acc = res[3 * h:3 * h + 3]
            o_ref[:, _hl(h)] = (acc / l).astype(o_ref.dtype)
            lse_ref[:, _hl(h)] = jnp.broadcast_to(m + jnp.log(l), (QB, HEAD_DIM))

    q_spec, k_spec, v_spec = _fox_specs(t)
    blk = pl.BlockSpec((QB, FOX_COLS), lambda hp, i: (i, hp))
    return pl.pallas_call(
        body, name=name, grid=(W_FOX // FOX_COLS, nq),
        in_specs=[q_spec, k_spec, v_spec, pl.BlockSpec((QB, LANES), lambda hp, i: (i, 0)),
                  pl.BlockSpec((t // KB, 8, KB), lambda hp, i: (0, 0, 0))],
        out_specs=[blk, blk],
        out_shape=[_sds((t, W_FOX), BF16), _sds((t, W_FOX), F32)],
        compiler_params=_cparams(2))(qkv, qkv, qkv, fcol, frow)


def fox_bwd(qkv, qkv_t, fcol, frow, o, lse, do, do_t, name, do_col=0):
    t = qkv.shape[0]
    nq = t // QB

    def body(q_ref, k_ref, v_ref, fc_ref, fr_ref, o_ref, lse_ref, do_ref, qt_ref, dot_ref,
             dq_ref, dk_ref, dv_ref, dfr_ref):
        hp = pl.program_id(0)
        i = pl.program_id(1)
        qts = [qt_ref[_hs(h), :] for h in range(FOX_STEP_HEADS)]
        dots = [dot_ref[_hs(h), :] for h in range(FOX_STEP_HEADS)]

        @pl.when(i == 0)
        def _():
            dk_ref[...] = jnp.zeros_like(dk_ref)
            dv_ref[...] = jnp.zeros_like(dv_ref)

        @pl.when((i == 0) & (hp == 0))
        def _():
            dfr_ref[...] = jnp.zeros_like(dfr_ref)

        groups = i // KSUB + 1
        t_idx = i * QB + _iota2((QB, KB), 0)
        lane = _iota2((QB, LANES), 1)
        sub = _iota2((8, KB), 0)
        qs = [q_ref[:, _hl(h)] for h in range(FOX_STEP_HEADS)]
        dos = [do_ref[:, _hl(h)] for h in range(FOX_STEP_HEADS)]
        f_qs = [jnp.sum(jnp.where(lane == hp * FOX_STEP_HEADS + h, fc_ref[...], 0.0), axis=1, keepdims=True)
                for h in range(FOX_STEP_HEADS)]
        lse_qs = [lse_ref[:, h * HEAD_DIM:h * HEAD_DIM + 1] for h in range(FOX_STEP_HEADS)]
        deltas = [jnp.sum(dos[h].astype(F32) * o_ref[:, _hl(h)].astype(F32), axis=1, keepdims=True)
                  for h in range(FOX_STEP_HEADS)]

        def step(g, dqs, masked):
            causal = (g * KB + _iota2((QB, KB), 1)) <= t_idx
            fr = fr_ref[g]
            out = []
            dfr = jnp.zeros((8, KB), F32)
            for h in range(FOX_STEP_HEADS):
                k = k_ref[_krows(g), _hl(h)]
                v = v_ref[_krows(g), _hl(h)]
                f_k = jnp.sum(jnp.where(sub == hp * FOX_STEP_HEADS + h, fr, 0.0), axis=0, keepdims=True)
                z = lax.dot_general(qs[h], k, NT, preferred_element_type=F32) + f_qs[h] - f_k
                p = jnp.exp(z - lse_qs[h])
                if masked:
                    p = jnp.where(causal, p, 0.0)
                dp = lax.dot_general(dos[h], v, NT, preferred_element_type=F32)
                ds = p * (dp - deltas[h])
                dsb = ds.astype(BF16)
                out.append(dqs[h] + jnp.dot(dsb, k, preferred_element_type=F32))
                dk_ref[g, _hs(h), :] += jnp.dot(qts[h], dsb, preferred_element_type=F32)
                dv_ref[g, _hs(h), :] += jnp.dot(dots[h], p.astype(BF16), preferred_element_type=F32)
                colsum = jnp.sum(ds, axis=0, keepdims=True)
                dfr = dfr + jnp.where(sub == hp * FOX_STEP_HEADS + h, -colsum, 0.0)
            dfr_ref[g] += dfr
            return tuple(out)

        res = lax.fori_loop(0, groups - 1, lambda g, c: step(g, c, False),
                            (jnp.zeros((QB, HEAD_DIM), F32),) * FOX_STEP_HEADS)
        res = step(groups - 1, res, True)
        for h in range(FOX_STEP_HEADS):
            dq_ref[:, _hl(h)] = (res[h] * SCALE).astype(dq_ref.dtype)

    q_spec, k_spec, v_spec = _fox_specs(t)
    blk = pl.BlockSpec((QB, FOX_COLS), lambda hp, i: (i, hp))
    frs = pl.BlockSpec((t // KB, 8, KB), lambda hp, i: (0, 0, 0))
    acc_t = pl.BlockSpec((None, t // KB, FOX_COLS, KB), lambda hp, i: (hp, 0, 0, 0))
    acc_sds = _sds((W_FOX // FOX_COLS, t // KB, FOX_COLS, KB), F32)
    return pl.pallas_call(
        body, name=name, grid=(W_FOX // FOX_COLS, nq),
        in_specs=[q_spec, k_spec, v_spec, pl.BlockSpec((QB, LANES), lambda hp, i: (i, 0)), frs,
                  blk, blk, pl.BlockSpec((QB, FOX_COLS), lambda hp, i: (i, do_col + hp)),
                  pl.BlockSpec((FOX_COLS, QB), lambda hp, i: (3 * (W_SB + W_CH) // FOX_COLS + hp, i)),
                  pl.BlockSpec((FOX_COLS, QB), lambda hp, i: (do_col + hp, i))],
        out_specs=[blk, acc_t, acc_t, frs],
        out_shape=[_sds((t, W_FOX), BF16), acc_sds, acc_sds, _sds((t // KB, 8, KB), F32)],
        compiler_params=_cparams(2))(qkv, qkv, qkv, fcol, frow, o, lse, do, qkv_t, do_t)


def _frow_to_groups(frow):
    n = frow.shape[0] // KSUB
    return frow.reshape(n, KSUB, 8, QB).transpose(0, 2, 1, 3).reshape(n, 8, KB)


def _frow_from_groups(frow):
    n = frow.shape[0]
    return frow.reshape(n, 8, KSUB, QB).transpose(0, 2, 1, 3).reshape(n * KSUB, 8, QB)


def _chunk_band():
    qi = _iota2((QB, CH_KEYS), 0)
    kj = _iota2((QB, CH_KEYS), 1)
    dchunk = (qi >> 6) + LEFT_CHUNKS - (kj >> 6)
    return jnp.where((dchunk >= 0) & (dchunk <= LEFT_CHUNKS), 0.0, NEG)


def _chunk_pad_row(i):
    kj = _iota2((1, CH_KEYS), 1)
    return jnp.where((i - (CH_WIN - 1)) * QB + kj >= 0, 0.0, NEG)


CH_PAD = (CH_WIN - 1) * QB
CH_STEP_HEADS = 4
CH_COLS = CH_STEP_HEADS * HEAD_DIM


def _window(i):
    return pl.ds(pl.multiple_of(i * QB, QB), CH_KEYS)


def _chunk_weights(q, kw, bias, pad_row):
    z = lax.dot_general(q, kw, NT, preferred_element_type=F32) + bias + pad_row
    e = jnp.exp(z - jnp.max(z, axis=1, keepdims=True))
    return e, 1.0 / jnp.sum(e, axis=1, keepdims=True)


def _chunk_specs(t):
    q_spec = pl.BlockSpec((QB, CH_COLS), lambda hp, i: (i, 3 * W_SB // CH_COLS + hp))
    kv_spec = pl.BlockSpec((t + CH_PAD, CH_COLS), lambda hp, i: (0, hp))
    return q_spec, kv_spec


def chunk_fwd(qkv, kp, vp, bias, name):
    t = qkv.shape[0]
    nq = t // QB

    def body(q_ref, k_ref, v_ref, b_ref, o_ref):
        i = pl.program_id(1)
        pad_row = _chunk_pad_row(i)
        for h in range(CH_STEP_HEADS):
            e, inv = _chunk_weights(q_ref[:, _hl(h)], k_ref[_window(i), _hl(h)], b_ref[h], pad_row)
            o = jnp.dot(e.astype(BF16), v_ref[_window(i), _hl(h)], preferred_element_type=F32)
            o_ref[:, _hl(h)] = (o * inv).astype(o_ref.dtype)

    q_spec, kv_spec = _chunk_specs(t)
    return pl.pallas_call(
        body, name=name, grid=(W_CH // CH_COLS, nq),
        in_specs=[q_spec, kv_spec, kv_spec,
                  pl.BlockSpec((CH_STEP_HEADS, QB, CH_KEYS), lambda hp, i: (hp, 0, 0))],
        out_specs=pl.BlockSpec((QB, CH_COLS), lambda hp, i: (i, hp)),
        out_shape=_sds((t, W_CH), BF16), compiler_params=_cparams(2))(qkv, kp, vp, bias)


def chunk_bwd(qkv, qkv_t, kp, vp, bias, do, do_t, name, do_col=0):
    t = qkv.shape[0]
    nq = t // QB

    def body(q_ref, k_ref, v_ref, b_ref, do_ref, qt_ref, dot_ref, dq_ref, dk_ref, dv_ref, db_ref):
        i = pl.program_id(1)

        @pl.when(i == 0)
        def _():
            dk_ref[...] = jnp.zeros_like(dk_ref)
            dv_ref[...] = jnp.zeros_like(dv_ref)
            db_ref[...] = jnp.zeros_like(db_ref)

        pad_row = _chunk_pad_row(i)
        for h in range(CH_STEP_HEADS):
            q = q_ref[:, _hl(h)]
            dov = do_ref[:, _hl(h)]
            kw = k_ref[_window(i), _hl(h)]
            e, inv = _chunk_weights(q, kw, b_ref[h], pad_row)
            p = e * inv
            dp = lax.dot_general(dov, v_ref[_window(i), _hl(h)], NT, preferred_element_type=F32)
            ds = p * (dp - jnp.sum(p * dp, axis=1, keepdims=True))
            db_ref[h] += ds
            dsb = ds.astype(BF16)
            dq_ref[:, _hl(h)] = (jnp.dot(dsb, kw, preferred_element_type=F32) * SCALE).astype(dq_ref.dtype)
            dkt = jnp.dot(qt_ref[_hs(h), :], dsb, preferred_element_type=F32)
            dvt = jnp.dot(dot_ref[_hs(h), :], p.astype(BF16), preferred_element_type=F32)
            for b in range(CH_WIN):
                dk_ref[i + b, _hs(h), :] += dkt[:, b * QB:(b + 1) * QB]
                dv_ref[i + b, _hs(h), :] += dvt[:, b * QB:(b + 1) * QB]

    q_spec, kv_spec = _chunk_specs(t)
    blk = pl.BlockSpec((QB, CH_COLS), lambda hp, i: (i, hp))
    bspec = pl.BlockSpec((CH_STEP_HEADS, QB, CH_KEYS), lambda hp, i: (hp, 0, 0))
    nblk = nq + CH_WIN - 1
    acc_t = pl.BlockSpec((None, nblk, CH_COLS, QB), lambda hp, i: (hp, 0, 0, 0))
    acc_sds = _sds((W_CH // CH_COLS, nblk, CH_COLS, QB), F32)
    return pl.pallas_call(
        body, name=name, grid=(W_CH // CH_COLS, nq),
        in_specs=[q_spec, kv_spec, kv_spec, bspec,
                  pl.BlockSpec((QB, CH_COLS), lambda hp, i: (i, do_col + hp)),
                  pl.BlockSpec((CH_COLS, QB), lambda hp, i: (3 * W_SB // CH_COLS + hp, i)),
                  pl.BlockSpec((CH_COLS, QB), lambda hp, i: (do_col + hp, i))],
        out_specs=[blk, acc_t, acc_t, bspec],
        out_shape=[_sds((t, W_CH), BF16), acc_sds, acc_sds, _sds((N_HEADS_CH, QB, CH_KEYS), F32)],
        compiler_params=_cparams(2))(qkv, kp, vp, bias, do, qkv_t, do_t)


def _sum_parts(p_ref):
    total = p_ref[0].astype(F32)
    for p in range(1, p_ref.shape[0]):
        total = total + p_ref[p].astype(F32)
    return total


def sum_parts_multi(parts_list, name):
    n = len(parts_list)

    def body(*refs):
        for p_ref, o_ref in zip(refs[:n], refs[n:]):
            o_ref[...] = _sum_parts(p_ref)

    shapes = [p.shape[2:] for p in parts_list]
    return pl.pallas_call(
        body, name=name, grid=(1,),
        in_specs=[pl.BlockSpec((N_DEV, None, r, c), lambda s: (0, 0, 0, 0)) for r, c in shapes],
        out_specs=[pl.BlockSpec((r, c), lambda s: (0, 0)) for r, c in shapes],
        out_shape=[_sds((r, c), F32) for r, c in shapes], compiler_params=_cparams(1))(*parts_list)


def adamw(parts, w, m, v, grid, p_specs, w_spec, name):
    c1 = 1.0 / (1.0 - ADAM_B1 ** ADAM_STEP)
    c2 = 1.0 / (1.0 - ADAM_B2 ** ADAM_STEP)
    n = len(parts)

    def body(*refs):
        w_ref, m_ref, v_ref, g_out, d_out, m_out, v_out = refs[n:]
        g = _sum_parts(refs[0])
        for q in range(1, n):
            g = jnp.where(pl.program_id(0) == q, _sum_parts(refs[q]), g)
        m_new = ADAM_B1 * m_ref[...] + (1.0 - ADAM_B1) * g
        v_new = ADAM_B2 * v_ref[...] + (1.0 - ADAM_B2) * (g * g)
        m_hat = m_new * c1
        v_hat = v_new * c2
        g_out[...] = g
        d_out[...] = -ADAM_LR * (m_hat / (jnp.sqrt(v_hat) + ADAM_EPS) + ADAM_WD * w_ref[...])
        m_out[...] = m_new
        v_out[...] = v_new

    out = _sds(w.shape, F32)
    return pl.pallas_call(
        body, name=name, grid=grid, in_specs=[*p_specs, w_spec, w_spec, w_spec],
        out_specs=[w_spec] * 4, out_shape=[out] * 4,
        compiler_params=_cparams(len(grid)))(*parts, w, m, v)


def _ffn_fwd(x, gain, wa, wb_after, s, tm, tag, on_event, deps=()):
    t = x.shape[0]
    hn = rmsnorm_fwd(x, gain, tm, f"rms_{tag}", deps)
    gu, act = ffn_in_swiglu(hn, wa, s, min(2 * tm, t), f"ffn_in_{tag}")
    relayed = on_event("act", act)
    wb = wb_after(act)
    y = ffn_out_residual(act, wb, x, s, min(2 * tm, t), f"ffn_out_{tag}", relayed)
    return y, (hn, gu, act), wb


def _ffn_bwd(dy, x, gain, saved, wa, wb, s, tm, tag, on_grads):
    t = x.shape[0]
    hn, gu, act = saved
    dgu = ffn_dact_swiglu(dy, wb, gu, s, min(2 * tm, t), f"ffn_dact_{tag}")
    dwb = matmul(TN, act, dy, _sds((4, FF_BLK, D_MODEL), BF16), (4, 1, 1),
                 pl.BlockSpec((None, t, FF_BLK), lambda i, j, k: (i, 0, 0)),
                 pl.BlockSpec((t, D_MODEL), lambda i, j, k: (0, 0)),
                 pl.BlockSpec((None, FF_BLK, D_MODEL), lambda i, j, k: (i, 0, 0)),
                 None, name=f"ffn_dwout_{tag}", alpha=0.5)
    dwa = matmul(TN, dgu, hn, _sds((8, FF_BLK, D_MODEL), BF16), (1, 8, 1),
                 pl.BlockSpec((None, None, t, FF_BLK), lambda i, j, k: (j % 4, j // 4, 0, 0)),
                 pl.BlockSpec((t, D_MODEL), lambda i, j, k: (0, 0)),
                 pl.BlockSpec((None, FF_BLK, D_MODEL), lambda i, j, k: (j, 0, 0)),
                 None, name=f"ffn_dwin_{tag}")
    deps = on_grads(dwa, dwb)
    return ffn_dh_norm_bwd(dgu, wa, s, x, gain, dy, tm, f"ffn_dh_{tag}", deps)


_Q_COLUMN_SCALE = np.ones((1, QKV_WIDTH), np.float32)
for _lo, _width in ((0, W_SB), (3 * W_SB, W_CH), (3 * (W_SB + W_CH), W_FOX)):
    _Q_COLUMN_SCALE[0, _lo:_lo + _width] = SCALE


def _mixer_fwd(x, gain, wqkv, wf, wgate, late_after, bq, bf, bg, bias, layer, tm, tag, on_event):
    t = x.shape[0]
    nt = t // tm
    hm = rmsnorm_fwd(x, gain, tm, f"rms_{tag}")
    a_full = pl.BlockSpec((tm, D_MODEL), lambda i, j, k: (i, 0))
    wide_out = pl.BlockSpec((tm, D_MODEL), lambda i, j, k: (i, j))
    wide_b = pl.BlockSpec((1, D_MODEL), lambda i, j, k: (0, j))
    tq = min(2 * tm, t)
    qkv, qkv_t = matmul(NN, hm, wqkv, _sds((t, QKV_WIDTH), BF16), (t // tq, 3, 1),
                        pl.BlockSpec((tq, D_MODEL), lambda i, j, k: (i, 0)),
                        pl.BlockSpec((None, D_MODEL, D_MODEL), lambda i, j, k: (layer, 0, j)),
                        pl.BlockSpec((tq, D_MODEL), lambda i, j, k: (i, j)), None,
                        name=f"proj_qkv_{tag}", bias=bq, bias_spec=wide_b,
                        scale=jnp.asarray(_Q_COLUMN_SCALE), scale_spec=wide_b,
                        out_t_sds=_sds((QKV_WIDTH, t), BF16),
                        out_t_spec=pl.BlockSpec((D_MODEL, tq), lambda i, j, k: (j, i)))
    relayed = on_event("qkv", qkv)
    gates, f, fcol, frow = proj_gates_forget(hm, wgate, layer + 1, wf, layer, bg, bf, tm,
                                             f"proj_gate_{tag}", relayed)
    frow = _frow_to_groups(frow)
    o_sb, w_sb = sb_fwd(qkv, f"sb_fwd_{tag}")
    relayed = on_event("o_sb", o_sb)
    kp = jnp.pad(qkv[:, 10 * LANES:14 * LANES], ((CH_PAD, 0), (0, 0)))
    vp = jnp.pad(qkv[:, 14 * LANES:18 * LANES], ((CH_PAD, 0), (0, 0)))
    o_ch = chunk_fwd(qkv, kp, vp, bias, f"chunk_fwd_{tag}")
    o_fox, lse = fox_fwd(qkv, fcol, frow, f"fox_fwd_{tag}")
    wbr, wout = late_after(o_fox)
    ys, merged = branch_merge((o_sb, o_ch, o_fox), wbr, layer, gates, tm, f"branch_merge_{tag}", relayed)
    x_new = matmul(NN, merged, wout, _sds((t, D_MODEL), F32), (nt, 1, 1), a_full,
                   pl.BlockSpec((None, D_MODEL, D_MODEL), lambda i, j, k: (layer, 0, 0)), a_full, None,
                   name=f"wout_{tag}", res=x, res_spec=a_full)
    saved = (hm, qkv, gates, f, fcol, frow, o_sb, o_ch, o_fox, lse, ys, merged, kp, vp, w_sb, qkv_t)
    return x_new, saved, wbr, wout


def _mixer_bwd(dy, x, gain, saved, wqkv, wf, wgate, wbr, wout, bias, layer, tm, tag, on_grads):
    t = x.shape[0]
    nt = t // tm
    hm, qkv, gates, f, fcol, frow, o_sb, o_ch, o_fox, lse, ys, merged, kp, vp, w_sb, qkv_t = saved
    a_full = pl.BlockSpec((tm, D_MODEL), lambda i, j, k: (i, 0))
    red_row = pl.BlockSpec((tm, D_MODEL), lambda i, j, k: (k, 0))
    sq = pl.BlockSpec((D_MODEL, D_MODEL), lambda i, j, k: (0, 0))
    dgates, dys = dmerged_merge_bwd(dy, wout, layer, gates, ys, tm // 2, f"dmerged_{tag}")
    all_t = pl.BlockSpec((t, D_MODEL), lambda i, j, k: (0, 0))
    dwout = matmul(TN, merged, dy, _sds((D_MODEL, D_MODEL), BF16), (1, 1, 1), all_t, all_t, sq,
                   None, name=f"dwout_{tag}")
    do, do_t, dwbr = branch_bwd(dys, (o_sb, o_ch, o_fox), wbr, layer, tm, f"dbranch_{tag}")
    dq_a, dk_a, dv_a = sb_bwd(qkv, qkv_t, w_sb, do, do_t, f"sb_bwd_{tag}")
    dk_a, dv_a = _keys_major(dk_a), _keys_major(dv_a)
    dq_b, dk_b, dv_b, dbias = chunk_bwd(qkv, qkv_t, kp, vp, bias, do, do_t, f"chunk_bwd_{tag}",
                                        do_col=W_SB // CH_COLS)
    dk_b, dv_b = [x[:, CH_WIN - 1:].transpose(1, 3, 0, 2).reshape(t, W_CH) for x in (dk_b, dv_b)]
    dq_c, dk_c, dv_c, dfrow = fox_bwd(qkv, qkv_t, fcol, frow, o_fox, lse, do, do_t, f"fox_bwd_{tag}",
                                      do_col=(W_SB + W_CH) // FOX_COLS)
    dk_c, dv_c = _keys_major(dk_c), _keys_major(dv_c)
    df = forget_cumsum_bwd(_frow_from_groups(dfrow), f, f"fcum_bwd_{tag}")
    dqkv = jnp.concatenate([p.astype(BF16) for p in
                            (dq_a, dk_a, dv_a, dq_b, dk_b, dv_b, dq_c, dk_c, dv_c)], axis=1)
    dtab = rel_bias_scatter(dbias, f"rel_scatter_{tag}")

    all_rows = pl.BlockSpec((t, D_MODEL), lambda i, j, k: (0, 0))
    wide_b = pl.BlockSpec((t, D_MODEL), lambda i, j, k: (0, j))
    wide_o = pl.BlockSpec((D_MODEL, D_MODEL), lambda i, j, k: (0, j))
    wide_cs = pl.BlockSpec((1, D_MODEL), lambda i, j, k: (0, j))
    dwqkv, dbq = matmul(TN, hm, dqkv, _sds((D_MODEL, QKV_WIDTH), BF16), (1, 3, 1), all_rows, wide_b,
                        wide_o, None, name=f"dwqkv_{tag}",
                        colsum_sds=_sds((1, QKV_WIDTH), F32), colsum_spec=wide_cs)
    dwgate, dbg = matmul(TN, hm, dgates, _sds((D_MODEL, 3 * D_MODEL), BF16), (1, 3, 1), all_rows,
                         wide_b, wide_o, None, name=f"dwgate_{tag}",
                         colsum_sds=_sds((1, 3 * D_MODEL), F32), colsum_spec=wide_cs)
    dwf, dbf = matmul(TN, hm, df, _sds((D_MODEL, LANES), BF16), (1, 1, 1), all_rows,
                      pl.BlockSpec((t, LANES), lambda i, j, k: (0, 0)),
                      pl.BlockSpec((D_MODEL, LANES), lambda i, j, k: (0, 0)), None,
                      name=f"dwf_{tag}", colsum_sds=_sds((1, LANES), F32),
                      colsum_spec=pl.BlockSpec((1, LANES), lambda i, j, k: (0, 0)))
    deps = on_grads(dict(dwqkv=dwqkv, dwgate=dwgate, dwf=dwf, dwbr=dwbr, dwout=dwout))
    wide_a = pl.BlockSpec((tm, QKV_WIDTH), lambda i, j, k: (i, 0))
    dhm = matmul(NT, dqkv, wqkv, _sds((t, D_MODEL), F32), (nt, 1, 1), wide_a,
                 pl.BlockSpec((None, D_MODEL, QKV_WIDTH), lambda i, j, k: (layer, 0, 0)), a_full,
                 None, name=f"dhm_qkv_{tag}", deps=deps)
    dx, dgain = mixer_dh_norm_bwd(dhm, dgates, wgate, layer + 1, df, wf, layer, x, gain, dy, tm,
                                  f"dhm_gate_{tag}")
    return dx, dict(dbq=dbq, dbg=dbg, dbf=dbf, dtab=dtab, dgain=dgain)


def _pack_small(pieces):
    flat = jnp.concatenate([p.reshape(-1).astype(F32) for p in pieces])
    flat = jnp.pad(flat, (0, SMALL_ROWS * LANES - flat.shape[0]))
    return flat.reshape(SMALL_ROWS, LANES)


def _unpack_small(packed, shapes):
    flat = packed.reshape(-1)
    out, pos = [], 0
    for shp in shapes:
        n = int(np.prod(shp))
        out.append(flat[pos:pos + n].reshape(shp))
        pos += n
    return out


def kernel(x, g_ffn1, w_ffn1_in, w_ffn1_out, g_mix, w_in, b_in, rel_bias, w_br_sb, w_br_ch, w_br_fox, w_out, g_ffn2, w_ffn2_in, w_ffn2_out, g_final, loss_target, m_g_ffn1, m_w_ffn1_in, m_w_ffn1_out, m_g_mix, m_w_in, m_b_in, m_rel_bias, m_w_br_sb, m_w_br_ch, m_w_br_fox, m_w_out, m_g_ffn2, m_w_ffn2_in, m_w_ffn2_out, m_g_final, v_g_ffn1, v_w_ffn1_in, v_w_ffn1_out, v_g_mix, v_w_in, v_b_in, v_rel_bias, v_w_br_sb, v_w_br_ch, v_w_br_fox, v_w_out, v_g_ffn2, v_w_ffn2_in, v_w_ffn2_out, v_g_final):
    t = x.shape[1]
    tm = min(512, t)
    xs = x[0]
    target = loss_target[0]
    f_lo, f_hi = QKV_WIDTH, QKV_WIDTH + N_HEADS_FOX

    def ffn_shards(w_in_, w_out_, l):
        return [w_in_[l:l + 1].astype(BF16), w_out_[l:l + 1].astype(BF16)]

    def mixer_shards(l):
        wl = w_in[l]
        return [jnp.stack([wl[:, :QKV_WIDTH], wl[:, f_hi:]]).astype(BF16),
                jnp.pad(wl[:, f_lo:f_hi], ((0, 0), (0, LANES - N_HEADS_FOX)))[None].astype(BF16),
                w_out[l:l + 1].astype(BF16),
                jnp.concatenate([w_br_sb[l], w_br_ch[l], w_br_fox[l]], axis=0)[None].astype(BF16)]

    gathers = {}
    gather_tokens = []

    def start_gather(shards, name):
        handle = gather_start(shards, name, deps=gather_tokens[-1:])
        gather_tokens.append(handle["token"])
        return handle

    def relay(handle, after):
        if "send2" not in handle:
            gather_relay(handle, after)

    relay_on = {("mix", 0, "qkv"): ("mix", 0, 1), ("mix", 0, "o_sb"): ("ffn2", 0, 0),
                ("ffn2", 0, "act"): ("ffn1", 1, 0), ("ffn1", 1, "act"): ("mix", 1, 0),
                ("mix", 1, "qkv"): ("ffn2", 1, 0)}

    def on_event(grp, l):
        def fire(event, array):
            target = relay_on.get((grp, l, event))
            if target is None:
                return ()
            handle = gathers[target[:2]][target[2]]
            relay(handle, array)
            return (handle["relay_token"],)
        return fire

    for l in range(DEPTH):
        for grp, shards in (("ffn1", ffn_shards(w_ffn1_in, w_ffn1_out, l)), ("mix", mixer_shards(l)),
                            ("ffn2", ffn_shards(w_ffn2_in, w_ffn2_out, l))):
            cut = len(shards) // 2
            if l == 0 and grp != "ffn2":
                gathers[(grp, l)] = (start_gather(shards[:cut], f"gather_{grp}_l{l}_a"),
                                     start_gather(shards[cut:], f"gather_{grp}_l{l}_b"))
            else:
                gathers[(grp, l)] = (start_gather(shards, f"gather_{grp}_l{l}"),)

    def gathered(key, after):
        hs = gathers[key]
        cut = hs[0]["n"]
        relay(hs[0], after)
        first = gather_finish(hs[0], after)
        if len(hs) == 1:
            return first[:cut // 2], lambda later: first[cut // 2:]

        def second(later):
            relay(hs[1], later)
            return gather_finish(hs[1], later)

        return first, second

    def ffn_weights(key, after):
        (wa_,), rest = gathered(key, after)
        return wa_, lambda later: rest(later)[0].reshape(1, 4, FF_BLK, D_MODEL)

    def mixer_weights(key, after):
        (wc_, wf_), rest = gathered(key, after)

        def late(later):
            wout_, wbr_ = rest(later)
            return (wbr_.transpose(0, 2, 1, 3).reshape(1, D_MODEL, D_MODEL), wout_.reshape(1, D_MODEL, D_MODEL))

        return wc_.reshape(2, D_MODEL, QKV_WIDTH), wf_.reshape(1, D_MODEL, LANES), late

    bq = b_in[:, None, :QKV_WIDTH]
    bf = jnp.pad(b_in[:, f_lo:f_hi], ((0, 0), (0, LANES - N_HEADS_FOX)))[:, None, :]
    bg = b_in[:, None, f_hi:]
    tab_t = jnp.pad(rel_bias.transpose(0, 2, 1), ((0, 0), (0, 0), (0, REL_PAD - N_REL)))

    h = xs
    saved = []
    weights = []
    for l in range(DEPTH):
        bias = rel_bias_build(tab_t[l], f"rel_build_l{l}").reshape(N_HEADS_CH, QB, CH_KEYS)
        x0 = h
        wa1, wb1_after = ffn_weights(("ffn1", l), x0)
        x1, s1, wb1 = _ffn_fwd(x0, g_ffn1[l:l + 1], wa1, wb1_after, 0, tm, f"ffn1_l{l}", on_event("ffn1", l),
                               deps=gather_tokens if l == 0 else ())
        wc, wf, late_after = mixer_weights(("mix", l), x1)
        x2, sm, wbr, wout = _mixer_fwd(x1, g_mix[l:l + 1], wc, wf, wc, late_after, bq[l], bf[l], bg[l],
                                       bias, 0, tm, f"mix_l{l}", on_event("mix", l))
        wa2, wb2_after = ffn_weights(("ffn2", l), x2)
        x3, s2, wb2 = _ffn_fwd(x2, g_ffn2[l:l + 1], wa2, wb2_after, 0, tm, f"ffn2_l{l}", on_event("ffn2", l))
        saved.append((x0, x1, x2, s1, sm, s2, bias))
        weights.append(((wa1, wb1), (wc, wf, wout, wbr), (wa2, wb2)))
        h = x3

    dx, dg_final, loss_blk = loss_head(h, g_final[None, :], target, tm, "loss_head")

    g_mix_l = [None] * DEPTH
    dgains = {}
    scatters = {}

    def scatter_ffn(key):
        def on_grads(dwa, dwb):
            scatters[key] = exchange_start(
                "scatter", [dwa[None], dwb.reshape(1, N_DEV, D_FF // N_DEV, D_MODEL)],
                f"scatter_{key[0]}_l{key[1]}")
            return (scatters[key]["token"],)
        return on_grads

    def scatter_mixer(key):
        def on_grads(gm):
            scatters[key] = exchange_start(
                "scatter",
                [gm["dwqkv"].reshape(1, N_DEV, LANES, QKV_WIDTH), gm["dwgate"].reshape(1, N_DEV, LANES, QKV_WIDTH),
                 gm["dwf"].reshape(1, N_DEV, LANES, LANES), gm["dwout"].reshape(1, N_DEV, LANES, D_MODEL),
                 gm["dwbr"].reshape(1, D_MODEL, N_DEV, LANES).transpose(0, 2, 1, 3)],
                f"scatter_{key[0]}_l{key[1]}")
            return (scatters[key]["token"],)
        return on_grads

    for l in reversed(range(DEPTH)):
        x0, x1, x2, s1, sm, s2, bias = saved[l]
        w1, (wc, wf, wout, wbr), w2 = weights[l]
        dx, dgains[("ffn2", l)] = _ffn_bwd(dx, x2, g_ffn2[l:l + 1], s2, *w2, 0, tm, f"ffn2_l{l}",
                                           scatter_ffn(("ffn2", l)))
        dx, g_mix_l[l] = _mixer_bwd(dx, x1, g_mix[l:l + 1], sm, wc, wf, wc, wbr, wout, bias, 0, tm,
                                    f"mix_l{l}", scatter_mixer(("mix", l)))
        dx, dgains[("ffn1", l)] = _ffn_bwd(dx, x0, g_ffn1[l:l + 1], s1, *w1, 0, tm, f"ffn1_l{l}",
                                           scatter_ffn(("ffn1", l)))

    small_shapes = []
    small_pieces = []
    small_w, small_m, small_v = [], [], []

    def add_small(piece, w, m, v):
        small_shapes.append(w.shape)
        small_pieces.append(piece)
        small_w.append(w); small_m.append(m); small_v.append(v)

    dg1 = jnp.concatenate([dgains[("ffn1", l)] for l in range(DEPTH)], axis=0)
    dgm = jnp.concatenate([g_mix_l[l]["dgain"] for l in range(DEPTH)], axis=0)
    dg2 = jnp.concatenate([dgains[("ffn2", l)] for l in range(DEPTH)], axis=0)
    db = jnp.stack([jnp.concatenate([g_mix_l[l]["dbq"][0], g_mix_l[l]["dbf"][0, :N_HEADS_FOX],
                                     g_mix_l[l]["dbg"][0]]) for l in range(DEPTH)])
    drel = jnp.stack([g_mix_l[l]["dtab"][:, :N_REL].T for l in range(DEPTH)])
    add_small(dg1, g_ffn1, m_g_ffn1, v_g_ffn1)
    add_small(dgm, g_mix, m_g_mix, v_g_mix)
    add_small(db, b_in, m_b_in, v_b_in)
    add_small(drel, rel_bias, m_rel_bias, v_rel_bias)
    add_small(dg2, g_ffn2, m_g_ffn2, v_g_ffn2)
    add_small(dg_final[0], g_final, m_g_final, v_g_final)
    loss_piece = loss_blk[0, 0:1]
    small_packed = _pack_small(small_pieces + [loss_piece])

    recv = {}
    last = ("ffn1", 0)
    for l in reversed(range(DEPTH)):
        for grp in ("ffn2", "mix", "ffn1"):
            if (grp, l) != last:
                recv[(grp, l)] = exchange_wait(scatters[(grp, l)], dx, f"scattered_{grp}_l{l}")

    def upd(parts, w, m, v, tr, name, rb0=0):
        _, r, c = w.shape
        nr = r // tr

        def p_spec(layer):
            pinned = (nr - 1) if layer == 0 else 0
            return pl.BlockSpec((N_DEV, None, tr, c),
                                lambda l, i: (0, 0, rb0 + jnp.where(l == layer, i, pinned), 0))

        return adamw(parts, w, m, v, (DEPTH, nr), [p_spec(0), p_spec(1)],
                     pl.BlockSpec((None, tr, c), lambda l, i: (l, i, 0)), name)

    def both(grp, k):
        return [recv[(grp, l)][k] for l in range(DEPTH)]

    out_rows = D_FF // N_DEV // 2
    def upd_transposed(parts, w, m, v, tr, name):
        tp = lambda a: jnp.transpose(a, (0, 2, 1))
        return [tp(o) for o in upd(parts, tp(w), tp(m), tp(v), tr, name)]

    in_rows = FF_BLK // 4
    r_ffn2_in = upd_transposed(both("ffn2", 0), w_ffn2_in, m_w_ffn2_in, v_w_ffn2_in, in_rows, "adamw_ffn2_in")
    r_ffn2_out = upd(both("ffn2", 1), w_ffn2_out, m_w_ffn2_out, v_w_ffn2_out, out_rows, "adamw_ffn2_out")
    r_out = upd(both("mix", 3), w_out, m_w_out, v_w_out, LANES, "adamw_w_out")
    r_br_sb = upd(both("mix", 4), w_br_sb, m_w_br_sb, v_w_br_sb, 256, "adamw_br_sb", rb0=0)
    r_br_ch = upd(both("mix", 4), w_br_ch, m_w_br_ch, v_w_br_ch, 256, "adamw_br_ch", rb0=1)
    r_br_fox = upd(both("mix", 4), w_br_fox, m_w_br_fox, v_w_br_fox, 256, "adamw_br_fox", rb0=3)

    def w_in_grad(l):
        pieces = [recv[("mix", l)][k] for k in (0, 2, 1)]
        gq, gf, gg = sum_parts_multi(pieces, f"sum_w_in_l{l}")
        return jnp.concatenate([gq, gf[:, :N_HEADS_FOX], gg], axis=1)

    g_w_in = jnp.stack([w_in_grad(l) for l in range(DEPTH)])
    to_cols = lambda a: jnp.transpose(a, (2, 0, 1))
    n_cols = w_in.shape[2]
    col_blk = n_cols // 4
    win_spec = pl.BlockSpec((col_blk, DEPTH, LANES), lambda i: (i, 0, 0))
    r_in = adamw([to_cols(g_w_in)[None]], to_cols(w_in), to_cols(m_w_in), to_cols(v_w_in), (4,),
                 [pl.BlockSpec((1, col_blk, DEPTH, LANES), lambda i: (0, i, 0, 0))], win_spec, "adamw_w_in")
    r_in = [jnp.transpose(o, (1, 2, 0)) for o in r_in]

    recv[last] = exchange_wait(scatters[last], r_in[1], "scattered_ffn1_l0")
    r_ffn1_in = upd_transposed(both("ffn1", 0), w_ffn1_in, m_w_ffn1_in, v_w_ffn1_in, in_rows, "adamw_ffn1_in")
    r_ffn1_out = upd(both("ffn1", 1), w_ffn1_out, m_w_ffn1_out, v_w_ffn1_out, out_rows, "adamw_ffn1_out")

    small_sum = all_reduce_small(small_packed, "allreduce_small", deps=(r_ffn1_out[1],))
    n_small = sum(int(np.prod(s)) for s in small_shapes)
    loss = small_sum.reshape(-1)[n_small]
    sm_spec = pl.BlockSpec((SMALL_ROWS, LANES), lambda i: (0, 0))
    sm_out = adamw([small_sum[None]], _pack_small(small_w), _pack_small(small_m), _pack_small(small_v),
                   (1,), [pl.BlockSpec((1, SMALL_ROWS, LANES), lambda i: (0, 0, 0))], sm_spec, "adamw_small")
    sm_g, sm_d, sm_m, sm_v = [_unpack_small(o, small_shapes) for o in sm_out]

    def per_kind(k):
        small = (sm_g, sm_d, sm_m, sm_v)[k]
        return [small[0], r_ffn1_in[k], r_ffn1_out[k], small[1], r_in[k], small[2], small[3],
                r_br_sb[k], r_br_ch[k], r_br_fox[k], r_out[k], small[4], r_ffn2_in[k], r_ffn2_out[k],
                small[5]]

    return (loss, dx[None], *per_kind(0), *per_kind(1), *per_kind(2), *per_kind(3))
```

```python
import numpy as np
import jax
import jax.numpy as jnp
from jax import lax
from jax.experimental import pallas as pl
from jax.experimental.pallas import tpu as pltpu

F32 = jnp.float32
BF16 = jnp.bfloat16

N_DEV = 8
D_MODEL = 1024
DEPTH = 2
HEAD_DIM = 64
W_SB, W_CH, W_FOX = 256, 512, 256
QKV_WIDTH = 3 * (W_SB + W_CH + W_FOX)
N_HEADS_FOX = 4
N_HEADS_CH = 8
D_FF = 2816
FF_BLK = 2 * D_FF // N_DEV
CHUNK = 64
LEFT_CHUNKS = 8
MAX_REL = 128
N_REL = 2 * MAX_REL + 1
REL_PAD = 384
QB = 128
KB = 512
KSUB = KB // QB
CH_WIN = 5
CH_KEYS = CH_WIN * QB
RMS_EPS = 1e-6
NEG = -1e30
SCALE = HEAD_DIM ** -0.5
LANES = 128
VMEM_LIMIT = 56 * 1024 * 1024

ADAM_LR, ADAM_B1, ADAM_B2, ADAM_EPS, ADAM_WD, ADAM_STEP = 0.001, 0.9, 0.999, 1e-08, 0.01, 10

SMALL_ROWS = 192

MESH = pl.DeviceIdType.MESH
ANY = pl.BlockSpec(memory_space=pl.ANY)
HIGHEST = lax.Precision.HIGHEST

NN = (((1,), (0,)), ((), ()))
NT = (((1,), (1,)), ((), ()))
TN = (((0,), (0,)), ((), ()))


def _cparams(n_grid):
    return pltpu.CompilerParams(dimension_semantics=("arbitrary",) * n_grid,
                                vmem_limit_bytes=VMEM_LIMIT)


def _sds(shape, dtype):
    return jax.ShapeDtypeStruct(tuple(shape), dtype)


def _my_index():
    return 4 * lax.axis_index("x") + 2 * lax.axis_index("y") + lax.axis_index("c")


def _peer(mask):
    x, y, c = lax.axis_index("x"), lax.axis_index("y"), lax.axis_index("c")
    px = x ^ ((mask >> 2) & 1)
    py = y ^ ((mask >> 1) & 1)
    pc = c ^ (mask & 1)
    return (px, py, pc), 4 * px + 2 * py + pc


HBM_SPEC = pl.BlockSpec(memory_space=pltpu.HBM)
SEM_SPEC = pl.BlockSpec(memory_space=pltpu.SEMAPHORE)
EFFECT = pltpu.SideEffectType.DATAFLOW_SIDE_EFFECTING


def _exchange_refs(mode, in_ref, land_ref, me, pidx):
    if mode == "gather":
        return in_ref, land_ref.at[:, me], land_ref.at[:, pidx]
    return in_ref.at[:, pidx], land_ref.at[me], land_ref.at[pidx]


def _landing_shape(mode, a):
    if mode == "gather":
        s, r, c = a.shape
        return (s, N_DEV, r, c)
    s, _, r, c = a.shape
    return (N_DEV, s, r, c)


def _own_copy(mode, in_ref, land_ref, me, sem):
    if mode == "gather":
        return pltpu.make_async_copy(in_ref, land_ref.at[:, me], sem)
    return pltpu.make_async_copy(in_ref.at[:, me], land_ref.at[me], sem)


def exchange_start(mode, arrays, name, deps=()):
    n = len(arrays)
    lands0 = [lax.empty(_landing_shape(mode, a), a.dtype) for a in arrays]

    def body(*refs):
        in_refs, land_refs = refs[:n], refs[n:2 * n]
        outs_at = 2 * n + len(deps)
        send_sems, recv_sems, own_sems, token = refs[outs_at], refs[outs_at + 1], refs[outs_at + 2], refs[-1]
        mine = _my_index()
        for k in range(n):
            _own_copy(mode, in_refs[k], land_refs[k], mine, own_sems.at[k]).start()
            for mask in range(1, N_DEV):
                peer, pidx = _peer(mask)
                src, dst, _ = _exchange_refs(mode, in_refs[k], land_refs[k], mine, pidx)
                sem = k * (N_DEV - 1) + mask - 1
                pltpu.make_async_remote_copy(
                    src_ref=src, dst_ref=dst, send_sem=send_sems.at[sem], recv_sem=recv_sems.at[sem],
                    device_id=peer, device_id_type=MESH).start()
        token[...] = jnp.zeros_like(token)

    nsem = n * (N_DEV - 1)
    outs = pl.pallas_call(
        body, name=name,
        out_shape=(pltpu.SemaphoreType.DMA((nsem,)), pltpu.SemaphoreType.DMA((nsem,)),
                   pltpu.SemaphoreType.DMA((n,)),
                   *[pltpu.HBM(a.shape, a.dtype) for a in arrays],
                   *[pltpu.HBM(l.shape, l.dtype) for l in lands0], _sds((8, LANES), F32)),
        in_specs=[HBM_SPEC] * (2 * n) + [ANY] * len(deps),
        out_specs=(SEM_SPEC, SEM_SPEC, SEM_SPEC, *[HBM_SPEC] * (2 * n),
                   pl.BlockSpec(memory_space=pltpu.VMEM)),
        input_output_aliases={k: 3 + k for k in range(2 * n)},
        compiler_params=pltpu.CompilerParams(has_side_effects=EFFECT),
    )(*[pltpu.with_memory_space_constraint(a, pltpu.HBM) for a in arrays],
      *[pltpu.with_memory_space_constraint(l, pltpu.HBM) for l in lands0], *deps)
    return dict(mode=mode, n=n, send=outs[0], recv=outs[1], own=outs[2], ins=outs[3:3 + n],
                lands=outs[3 + n:3 + 2 * n], token=outs[-1])


def exchange_wait(handle, after, name):
    n, mode = handle["n"], handle["mode"]

    def body(*refs):
        in_refs, land_refs = refs[:n], refs[n:2 * n]
        send_sems, recv_sems, own_sems = refs[2 * n], refs[2 * n + 1], refs[2 * n + 2]
        mine = _my_index()
        for k in range(n):
            _own_copy(mode, in_refs[k], land_refs[k], mine, own_sems.at[k]).wait()
            for mask in range(1, N_DEV):
                peer, pidx = _peer(mask)
                src, _, here = _exchange_refs(mode, in_refs[k], land_refs[k], mine, pidx)
                sem = k * (N_DEV - 1) + mask - 1
                cp = pltpu.make_async_remote_copy(
                    src_ref=src, dst_ref=here, send_sem=send_sems.at[sem], recv_sem=recv_sems.at[sem],
                    device_id=peer, device_id_type=MESH)
                cp.wait_send()
                cp.wait_recv()

    thru = (*handle["ins"], *handle["lands"])
    outs = pl.pallas_call(
        body, name=name,
        out_shape=tuple(pltpu.HBM(a.shape, a.dtype) for a in thru),
        in_specs=[HBM_SPEC] * (2 * n) + [SEM_SPEC, SEM_SPEC, SEM_SPEC, ANY],
        out_specs=tuple([HBM_SPEC] * (2 * n)),
        input_output_aliases={k: k for k in range(2 * n)},
        compiler_params=pltpu.CompilerParams(has_side_effects=EFFECT),
    )(*thru, handle["send"], handle["recv"], handle["own"], after)
    return list(outs[n:])


FAR_MASKS = (2, 4, 6)
PHASE1_MASKS = (1,) + FAR_MASKS


def gather_start(arrays, name, deps=()):
    n = len(arrays)
    n1 = len(PHASE1_MASKS)
    lands0 = [lax.empty(_landing_shape("gather", a), a.dtype) for a in arrays]

    def body(*refs):
        in_refs, land_refs = refs[:n], refs[n:2 * n]
        outs_at = 2 * n + len(deps)
        send_sems, recv_sems, own_sems, token = refs[outs_at], refs[outs_at + 1], refs[outs_at + 2], refs[-1]
        mine = _my_index()
        for k in range(n):
            _own_copy("gather", in_refs[k], land_refs[k], mine, own_sems.at[k]).start()
            for j, mask in enumerate(PHASE1_MASKS):
                peer, _ = _peer(mask)
                pltpu.make_async_remote_copy(
                    src_ref=in_refs[k], dst_ref=land_refs[k].at[:, mine],
                    send_sem=send_sems.at[k * n1 + j], recv_sem=recv_sems.at[k * n1 + j],
                    device_id=peer, device_id_type=MESH).start()
        token[...] = jnp.zeros_like(token)

    outs = pl.pallas_call(
        body, name=name,
        out_shape=(pltpu.SemaphoreType.DMA((n * n1,)), pltpu.SemaphoreType.DMA((n * n1,)),
                   pltpu.SemaphoreType.DMA((n,)),
                   *[pltpu.HBM(a.shape, a.dtype) for a in arrays],
                   *[pltpu.HBM(l.shape, l.dtype) for l in lands0], _sds((8, LANES), F32)),
        in_specs=[HBM_SPEC] * (2 * n) + [ANY] * len(deps),
        out_specs=(SEM_SPEC, SEM_SPEC, SEM_SPEC, *[HBM_SPEC] * (2 * n),
                   pl.BlockSpec(memory_space=pltpu.VMEM)),
        input_output_aliases={k: 3 + k for k in range(2 * n)},
        compiler_params=pltpu.CompilerParams(has_side_effects=EFFECT),
    )(*[pltpu.with_memory_space_constraint(a, pltpu.HBM) for a in arrays],
      *[pltpu.with_memory_space_constraint(l, pltpu.HBM) for l in lands0], *deps)
    return dict(n=n, send=outs[0], recv=outs[1], own=outs[2], ins=outs[3:3 + n],
                lands=outs[3 + n:3 + 2 * n], token=outs[-1], name=name)


def gather_relay(handle, after):
    n = handle["n"]
    n1, n2 = len(PHASE1_MASKS), len(FAR_MASKS)

    def body(*refs):
        in_refs, land_refs = refs[:n], refs[n:2 * n]
        send1, recv1 = refs[2 * n], refs[2 * n + 1]
        send2, recv2, token = refs[2 * n + 3], refs[2 * n + 4], refs[-1]
        token[...] = jnp.zeros_like(token)
        sibling, _ = _peer(1)
        for k in range(n):
            for j, mask in enumerate(FAR_MASKS):
                peer, pidx = _peer(mask)
                landed = land_refs[k].at[:, pidx]
                pltpu.make_async_remote_copy(
                    src_ref=in_refs[k], dst_ref=landed, send_sem=send1.at[k * n1 + 1 + j],
                    recv_sem=recv1.at[k * n1 + 1 + j], device_id=peer, device_id_type=MESH).wait_recv()
                pltpu.make_async_remote_copy(
                    src_ref=landed, dst_ref=landed, send_sem=send2.at[k * n2 + j],
                    recv_sem=recv2.at[k * n2 + j], device_id=sibling, device_id_type=MESH).start()

    thru = (*handle["ins"], *handle["lands"])
    outs = pl.pallas_call(
        body, name=handle["name"] + "_relay",
        out_shape=(pltpu.SemaphoreType.DMA((n * n2,)), pltpu.SemaphoreType.DMA((n * n2,)),
                   *[pltpu.HBM(a.shape, a.dtype) for a in thru], _sds((8, LANES), F32)),
        in_specs=[HBM_SPEC] * (2 * n) + [SEM_SPEC, SEM_SPEC, ANY],
        out_specs=(SEM_SPEC, SEM_SPEC, *[HBM_SPEC] * (2 * n), pl.BlockSpec(memory_space=pltpu.VMEM)),
        input_output_aliases={k: 2 + k for k in range(2 * n)},
        compiler_params=pltpu.CompilerParams(has_side_effects=EFFECT),
    )(*thru, handle["send"], handle["recv"], after)
    handle.update(send2=outs[0], recv2=outs[1], ins=outs[2:2 + n], lands=outs[2 + n:2 + 2 * n],
                  relay_token=outs[-1])


def gather_finish(handle, after):
    n = handle["n"]
    n1, n2 = len(PHASE1_MASKS), len(FAR_MASKS)

    def body(*refs):
        in_refs, land_refs = refs[:n], refs[n:2 * n]
        send1, recv1, own_sems, send2, recv2 = refs[2 * n:2 * n + 5]
        mine = _my_index()
        sibling, sib_idx = _peer(1)
        for k in range(n):
            _own_copy("gather", in_refs[k], land_refs[k], mine, own_sems.at[k]).wait()
            for j, mask in enumerate(PHASE1_MASKS):
                peer, pidx = _peer(mask)
                cp = pltpu.make_async_remote_copy(
                    src_ref=in_refs[k], dst_ref=land_refs[k].at[:, pidx], send_sem=send1.at[k * n1 + j],
                    recv_sem=recv1.at[k * n1 + j], device_id=peer, device_id_type=MESH)
                cp.wait_send()
                if mask == 1:
                    cp.wait_recv()
            for j, mask in enumerate(FAR_MASKS):
                _, pidx = _peer(mask)
                _, far_of_sibling = _peer(mask ^ 1)
                cp = pltpu.make_async_remote_copy(
                    src_ref=land_refs[k].at[:, pidx], dst_ref=land_refs[k].at[:, far_of_sibling],
                    send_sem=send2.at[k * n2 + j], recv_sem=recv2.at[k * n2 + j],
                    device_id=sibling, device_id_type=MESH)
                cp.wait_send()
                cp.wait_recv()

    thru = (*handle["ins"], *handle["lands"])
    outs = pl.pallas_call(
        body, name=handle["name"] + "_finish",
        out_shape=tuple(pltpu.HBM(a.shape, a.dtype) for a in thru),
        in_specs=[HBM_SPEC] * (2 * n) + [SEM_SPEC] * 5 + [ANY],
        out_specs=tuple([HBM_SPEC] * (2 * n)),
        input_output_aliases={k: k for k in range(2 * n)},
        compiler_params=pltpu.CompilerParams(has_side_effects=EFFECT),
    )(*thru, handle["send"], handle["recv"], handle["own"], handle["send2"], handle["recv2"], after)
    return list(outs[n:])


def all_reduce_small(packed, name, deps=()):
    rows = packed.shape[0]
    nd = len(deps)

    def body(in_ref, *rest):
        out_ref, slots, send_sems, recv_sems = rest[nd:]
        me = _my_index()
        sends = []
        for mask in range(1, N_DEV):
            peer, _ = _peer(mask)
            cp = pltpu.make_async_remote_copy(
                src_ref=in_ref, dst_ref=slots.at[me],
                send_sem=send_sems.at[mask - 1], recv_sem=recv_sems.at[mask - 1],
                device_id=peer, device_id_type=MESH)
            cp.start()
            sends.append(cp)
        slots[me] = in_ref[...]
        for mask in range(1, N_DEV):
            peer, pidx = _peer(mask)
            pltpu.make_async_remote_copy(
                src_ref=in_ref, dst_ref=slots.at[pidx],
                send_sem=send_sems.at[mask - 1], recv_sem=recv_sems.at[mask - 1],
                device_id=peer, device_id_type=MESH).wait_recv()
        for cp in sends:
            cp.wait_send()
        total = slots[0]
        for p in range(1, N_DEV):
            total = total + slots[p]
        out_ref[...] = total

    return pl.pallas_call(
        body, name=name,
        out_shape=_sds((rows, LANES), F32),
        in_specs=[pl.BlockSpec(memory_space=pltpu.VMEM)] + [ANY] * nd,
        out_specs=pl.BlockSpec(memory_space=pltpu.VMEM),
        scratch_shapes=[pltpu.VMEM((N_DEV, rows, LANES), F32),
                        pltpu.SemaphoreType.DMA((N_DEV - 1,)),
                        pltpu.SemaphoreType.DMA((N_DEV - 1,))],
    )(packed, *deps)


def matmul(dims, a, b, out_sds, grid, a_spec, b_spec, o_spec, acc_shape, *, name, alpha=1.0,
           bias=None, bias_spec=None, scale=None, scale_spec=None, res=None, res_spec=None,
           colsum_sds=None, colsum_spec=None, out_t_sds=None, out_t_spec=None, deps=()):
    nk = grid[2]
    has_bias, has_scale, has_res = bias is not None, scale is not None, res is not None
    has_cs, has_t = colsum_sds is not None, out_t_sds is not None
    if has_cs:
        assert grid[0] == 1 and dims == TN

    def body(*refs):
        a_ref, b_ref = refs[0], refs[1]
        pos = 2
        bias_ref = scale_ref = res_ref = cs_ref = ot_ref = None
        if has_bias:
            bias_ref = refs[pos]; pos += 1
        if has_scale:
            scale_ref = refs[pos]; pos += 1
        if has_res:
            res_ref = refs[pos]; pos += 1
        pos += len(deps)
        o_ref = refs[pos]; pos += 1
        if has_cs:
            cs_ref = refs[pos]; pos += 1
        if has_t:
            ot_ref = refs[pos]; pos += 1
        k = pl.program_id(2)
        bval = b_ref[...]
        part = lax.dot_general(a_ref[...].astype(BF16), bval.astype(BF16), dims,
                               preferred_element_type=F32)

        def finish(total):
            r = total * alpha if alpha != 1.0 else total
            if has_bias:
                r = r + bias_ref[...]
            if has_scale:
                r = r * scale_ref[...]
            if has_res:
                r = r + res_ref[...].astype(F32)
            o_ref[...] = r.astype(o_ref.dtype)
            if has_t:
                ot_ref[...] = r.T.astype(ot_ref.dtype)

        if has_cs:
            csum = jnp.sum(bval.astype(F32), axis=0, keepdims=True)

            @pl.when(k == 0)
            def _():
                cs_ref[...] = csum

            @pl.when(k > 0)
            def _():
                cs_ref[...] += csum

        if nk == 1:
            finish(part)
        else:
            acc_ref = refs[pos]

            @pl.when(k == 0)
            def _():
                acc_ref[...] = part

            @pl.when(k > 0)
            def _():
                acc_ref[...] += part

            @pl.when(k == nk - 1)
            def _():
                finish(acc_ref[...])

    in_specs, args = [a_spec, b_spec], [a, b]
    if has_bias:
        in_specs.append(bias_spec); args.append(bias)
    if has_scale:
        in_specs.append(scale_spec); args.append(scale)
    if has_res:
        in_specs.append(res_spec); args.append(res)
    in_specs += [ANY] * len(deps)
    args += list(deps)
    out_shape, out_specs = [out_sds], [o_spec]
    if has_cs:
        out_shape.append(colsum_sds); out_specs.append(colsum_spec)
    if has_t:
        out_shape.append(out_t_sds); out_specs.append(out_t_spec)
    scratch = [] if nk == 1 else [pltpu.VMEM(acc_shape, F32)]
    outs = pl.pallas_call(
        body, name=name, grid=grid, in_specs=in_specs, out_specs=out_specs, out_shape=out_shape,
        scratch_shapes=scratch, compiler_params=_cparams(3))(*args)
    return outs if (has_cs or has_t) else outs[0]


def _sigmoid(z):
    return 1.0 / (1.0 + jnp.exp(-z))


def _log_sigmoid(z):
    return jnp.minimum(z, 0.0) - jnp.log(1.0 + jnp.exp(-jnp.abs(z)))


def rmsnorm_fwd(x, gain, tm, name, deps=()):
    t, d = x.shape

    def body(x_ref, g_ref, *rest):
        o_ref = rest[-1]
        xf = x_ref[...]
        r = lax.rsqrt(jnp.mean(xf * xf, axis=-1, keepdims=True) + RMS_EPS)
        o_ref[...] = (xf * r * g_ref[...]).astype(o_ref.dtype)

    return pl.pallas_call(
        body, name=name, grid=(t // tm,),
        in_specs=[pl.BlockSpec((tm, d), lambda i: (i, 0)), pl.BlockSpec((1, d), lambda i: (0, 0))]
        + [ANY] * len(deps),
        out_specs=pl.BlockSpec((tm, d), lambda i: (i, 0)),
        out_shape=_sds((t, d), BF16), compiler_params=_cparams(1))(x, gain, *deps)


def loss_head(x, gain, target, tm, name):
    t, d = x.shape

    def body(x_ref, g_ref, tgt_ref, dx_ref, dg_ref, loss_ref):
        i = pl.program_id(0)
        xf = x_ref[...]
        g = g_ref[...]
        r = lax.rsqrt(jnp.mean(xf * xf, axis=-1, keepdims=True) + RMS_EPS)
        xhat = xf * r
        err = xhat * g - tgt_ref[...]
        part = 0.5 * jnp.sum(jnp.mean(err * err, axis=-1, keepdims=True))
        dy = err * (1.0 / d)
        dxhat = dy * g
        dx_ref[...] = r * (dxhat - xhat * jnp.mean(dxhat * xhat, axis=-1, keepdims=True))
        dg = jnp.sum(dy * xhat, axis=0, keepdims=True)
        lpart = jnp.full((8, LANES), part, F32)

        @pl.when(i == 0)
        def _():
            dg_ref[...] = dg
            loss_ref[...] = lpart

        @pl.when(i > 0)
        def _():
            dg_ref[...] += dg
            loss_ref[...] += lpart

    row = pl.BlockSpec((tm, d), lambda i: (i, 0))
    vec = pl.BlockSpec((1, d), lambda i: (0, 0))
    return pl.pallas_call(
        body, name=name, grid=(t // tm,), in_specs=[row, vec, row],
        out_specs=[row, vec, pl.BlockSpec((8, LANES), lambda i: (0, 0))],
        out_shape=[_sds((t, d), F32), _sds((1, d), F32), _sds((8, LANES), F32)],
        compiler_params=_cparams(1))(x, gain, target)


def ffn_in_swiglu(hn, wa, s, tm, name):
    t = hn.shape[0]
    halves = 2 if tm % 512 == 0 else 1
    rows = tm // halves

    def body(h_ref, wg_ref, wu_ref, gu_ref, act_ref):
        for c in range(halves):
            rs = slice(c * rows, (c + 1) * rows)
            h = h_ref[rs, :]
            g = jnp.dot(h, wg_ref[...], preferred_element_type=F32)
            u = jnp.dot(h, wu_ref[...], preferred_element_type=F32)
            gu_ref[0, rs, :] = g.astype(gu_ref.dtype)
            gu_ref[1, rs, :] = u.astype(gu_ref.dtype)
            act_ref[rs, :] = (g * _sigmoid(g) * u).astype(act_ref.dtype)

    return pl.pallas_call(
        body, name=name, grid=(t // tm, 4),
        in_specs=[pl.BlockSpec((tm, D_MODEL), lambda i, j: (i, 0)),
                  pl.BlockSpec((None, None, D_MODEL, FF_BLK), lambda i, j: (s, j, 0, 0)),
                  pl.BlockSpec((None, None, D_MODEL, FF_BLK), lambda i, j: (s, j + 4, 0, 0))],
        out_specs=[pl.BlockSpec((None, 2, tm, FF_BLK), lambda i, j: (j, 0, i, 0)),
                   pl.BlockSpec((None, tm, FF_BLK), lambda i, j: (j, i, 0))],
        out_shape=[_sds((4, 2, t, FF_BLK), BF16), _sds((4, t, FF_BLK), BF16)],
        compiler_params=_cparams(2))(hn, wa, wa)


def ffn_dact_swiglu(dy, wb, gu, s, tm, name):
    t = dy.shape[0]

    def body(dy_ref, w_ref, gu_ref, o_ref):
        da = 0.5 * lax.dot_general(dy_ref[...].astype(BF16), w_ref[...], NT, preferred_element_type=F32)
        g = gu_ref[0].astype(F32)
        u = gu_ref[1].astype(F32)
        sg = _sigmoid(g)
        o_ref[0] = (da * u * (sg * (1.0 + g * (1.0 - sg)))).astype(o_ref.dtype)
        o_ref[1] = (da * g * sg).astype(o_ref.dtype)

    blk = pl.BlockSpec((None, 2, tm, FF_BLK), lambda i, j: (j, 0, i, 0))
    return pl.pallas_call(
        body, name=name, grid=(t // tm, 4),
        in_specs=[pl.BlockSpec((tm, D_MODEL), lambda i, j: (i, 0)),
                  pl.BlockSpec((None, None, FF_BLK, D_MODEL), lambda i, j: (s, j, 0, 0)), blk],
        out_specs=blk, out_shape=_sds((4, 2, t, FF_BLK), BF16),
        compiler_params=_cparams(2))(dy, wb, gu)


def ffn_out_residual(act, wb, x, s, tm, name, deps=()):
    t = x.shape[0]

    def body(a_ref, w_ref, x_ref, *rest):
        o_ref = rest[-1]
        acc = jnp.dot(a_ref[0], w_ref[0], preferred_element_type=F32)
        for k in range(1, 4):
            acc = acc + jnp.dot(a_ref[k], w_ref[k], preferred_element_type=F32)
        o_ref[...] = x_ref[...] + 0.5 * acc

    row = pl.BlockSpec((tm, D_MODEL), lambda i: (i, 0))
    return pl.pallas_call(
        body, name=name, grid=(t // tm,),
        in_specs=[pl.BlockSpec((4, tm, FF_BLK), lambda i: (0, i, 0)),
                  pl.BlockSpec((None, 4, FF_BLK, D_MODEL), lambda i: (s, 0, 0, 0)), row] + [ANY] * len(deps),
        out_specs=row, out_shape=_sds((t, D_MODEL), F32), compiler_params=_cparams(1))(act, wb, x, *deps)


def ffn_dh_norm_bwd(dgu, wa, s, x, gain, dres, tm, name, deps):
    t = dgu.shape[2]
    nd = len(deps)

    def body(g_ref, w_ref, x_ref, gain_ref, dres_ref, *rest):
        dx_ref, dg_ref = rest[nd:]
        i = pl.program_id(0)
        dh = lax.dot_general(g_ref[0, 0], w_ref[0], NT, preferred_element_type=F32)
        for p in range(1, N_DEV):
            dh = dh + lax.dot_general(g_ref[p % 4, p // 4], w_ref[p], NT, preferred_element_type=F32)
        xf = x_ref[...]
        r = lax.rsqrt(jnp.mean(xf * xf, axis=-1, keepdims=True) + RMS_EPS)
        xhat = xf * r
        dxhat = dh * gain_ref[...]
        dx_ref[...] = dres_ref[...] + r * (dxhat - xhat * jnp.mean(dxhat * xhat, axis=-1, keepdims=True))
        dg = jnp.sum(dh * xhat, axis=0, keepdims=True)

        @pl.when(i == 0)
        def _():
            dg_ref[...] = dg

        @pl.when(i > 0)
        def _():
            dg_ref[...] += dg

    row = pl.BlockSpec((tm, D_MODEL), lambda i: (i, 0))
    vec = pl.BlockSpec((1, D_MODEL), lambda i: (0, 0))
    return pl.pallas_call(
        body, name=name, grid=(t // tm,),
        in_specs=[pl.BlockSpec((4, 2, tm, FF_BLK), lambda i: (0, 0, i, 0)),
                  pl.BlockSpec((None, N_DEV, D_MODEL, FF_BLK), lambda i: (s, 0, 0, 0)), row, vec, row]
        + [ANY] * nd,
        out_specs=[row, vec], out_shape=[_sds((t, D_MODEL), F32), _sds((1, D_MODEL), F32)],
        compiler_params=_cparams(1))(dgu, wa, x, gain, dres, *deps)


def branch_merge(os_, wbr, layer, gates, tm, name, deps=()):
    t = os_[0].shape[0]
    d = D_MODEL
    nd = len(deps)
    widths = [o.shape[1] for o in os_]
    starts = [sum(widths[:a]) for a in range(3)]

    def body(oa_ref, ob_ref, oc_ref, w_ref, g_ref, *rest):
        ya_ref, yb_ref, yc_ref, m_ref = rest[nd:]
        merged = None
        for a, (o_ref, y_ref) in enumerate(((oa_ref, ya_ref), (ob_ref, yb_ref), (oc_ref, yc_ref))):
            y = jnp.dot(o_ref[...], w_ref[starts[a]:starts[a] + widths[a], :], preferred_element_type=F32)
            y_ref[...] = y
            term = _sigmoid(g_ref[:, a * d:(a + 1) * d]) * y
            merged = term if merged is None else merged + term
        m_ref[...] = merged.astype(m_ref.dtype)

    row = pl.BlockSpec((tm, d), lambda i: (i, 0))
    ya, yb, yc, merged = pl.pallas_call(
        body, name=name, grid=(t // tm,),
        in_specs=[pl.BlockSpec((tm, w), lambda i: (i, 0)) for w in widths]
        + [pl.BlockSpec((None, d, d), lambda i: (layer, 0, 0)), pl.BlockSpec((tm, 3 * d), lambda i: (i, 0))]
        + [ANY] * nd,
        out_specs=[row, row, row, row],
        out_shape=[_sds((t, d), F32)] * 3 + [_sds((t, d), BF16)],
        compiler_params=_cparams(1))(*os_, wbr, gates, *deps)
    return [ya, yb, yc], merged


def dmerged_merge_bwd(dy, wout, layer, gates, ys, tm, name):
    t, d = dy.shape

    def body(dy_ref, w_ref, g_ref, ya_ref, yb_ref, yc_ref, dg_ref, dya_ref, dyb_ref, dyc_ref):
        dmv = lax.dot_general(dy_ref[...].astype(BF16), w_ref[...], NT, preferred_element_type=F32)
        for a, (y_ref, dy_out) in enumerate(((ya_ref, dya_ref), (yb_ref, dyb_ref), (yc_ref, dyc_ref))):
            cols = slice(a * d, (a + 1) * d)
            s = _sigmoid(g_ref[:, cols])
            dy_out[...] = (dmv * s).astype(dy_out.dtype)
            dg_ref[:, cols] = (dmv * y_ref[...] * s * (1.0 - s)).astype(dg_ref.dtype)

    row = pl.BlockSpec((tm, d), lambda i: (i, 0))
    wide = pl.BlockSpec((tm, 3 * d), lambda i: (i, 0))
    dg, dya, dyb, dyc = pl.pallas_call(
        body, name=name, grid=(t // tm,),
        in_specs=[row, pl.BlockSpec((None, d, d), lambda i: (layer, 0, 0)), wide, row, row, row],
        out_specs=[wide, row, row, row],
        out_shape=[_sds((t, 3 * d), BF16)] + [_sds((t, d), BF16)] * 3,
        compiler_params=_cparams(1))(dy, wout, gates, *ys)
    return dg, [dya, dyb, dyc]


def branch_bwd(dys, os_, wbr, layer, tm, name):
    t = dys[0].shape[0]
    d = D_MODEL
    nt = t // tm
    widths = [o.shape[1] for o in os_]
    starts = [sum(widths[:a]) for a in range(3)]

    def body(dya_ref, dyb_ref, dyc_ref, oa_ref, ob_ref, oc_ref, w_ref, do_ref, dot_ref, dw_ref, acc_ref):
        i = pl.program_id(0)
        for a, (dy_ref, o_ref) in enumerate(((dya_ref, oa_ref), (dyb_ref, ob_ref), (dyc_ref, oc_ref))):
            rows = slice(starts[a], starts[a] + widths[a])
            dyv = dy_ref[...]
            do = lax.dot_general(dyv, w_ref[rows, :], NT, preferred_element_type=F32)
            do_ref[:, rows] = do.astype(do_ref.dtype)
            dot_ref[rows, :] = do.T.astype(dot_ref.dtype)
            dw = lax.dot_general(o_ref[...], dyv, TN, preferred_element_type=F32)

            @pl.when(i == 0)
            def _():
                acc_ref[rows, :] = dw

            @pl.when(i > 0)
            def _():
                acc_ref[rows, :] += dw

        @pl.when(i == nt - 1)
        def _():
            dw_ref[...] = acc_ref[...].astype(dw_ref.dtype)

    row = pl.BlockSpec((tm, d), lambda i: (i, 0))
    return pl.pallas_call(
        body, name=name, grid=(nt,),
        in_specs=[row, row, row] + [pl.BlockSpec((tm, w), lambda i: (i, 0)) for w in widths]
        + [pl.BlockSpec((None, d, d), lambda i: (layer, 0, 0))],
        out_specs=[row, pl.BlockSpec((d, tm), lambda i: (0, i)), pl.BlockSpec((d, d), lambda i: (0, 0))],
        out_shape=[_sds((t, d), BF16), _sds((d, t), BF16), _sds((d, d), BF16)],
        scratch_shapes=[pltpu.VMEM((d, d), F32)], compiler_params=_cparams(1))(*dys, *os_, wbr)


def mixer_dh_norm_bwd(dh_part, dgates, wc, gate_idx, df, wf, layer, x, gain, dres, tm, name):
    t, d = x.shape

    def body(dhp_ref, dg_ref, wg_ref, df_ref, wf_ref, x_ref, gain_ref, dres_ref, dx_ref, dgain_ref):
        i = pl.program_id(0)
        dh = (dhp_ref[...]
              + lax.dot_general(dg_ref[...], wg_ref[...], NT, preferred_element_type=F32)
              + lax.dot_general(df_ref[...].astype(BF16), wf_ref[...], NT, preferred_element_type=F32))
        xf = x_ref[...]
        r = lax.rsqrt(jnp.mean(xf * xf, axis=-1, keepdims=True) + RMS_EPS)
        xhat = xf * r
        dxhat = dh * gain_ref[...]
        dx_ref[...] = dres_ref[...] + r * (dxhat - xhat * jnp.mean(dxhat * xhat, axis=-1, keepdims=True))
        dg = jnp.sum(dh * xhat, axis=0, keepdims=True)

        @pl.when(i == 0)
        def _():
            dgain_ref[...] = dg

        @pl.when(i > 0)
        def _():
            dgain_ref[...] += dg

    row = pl.BlockSpec((tm, d), lambda i: (i, 0))
    vec = pl.BlockSpec((1, d), lambda i: (0, 0))
    return pl.pallas_call(
        body, name=name, grid=(t // tm,),
        in_specs=[row, pl.BlockSpec((tm, QKV_WIDTH), lambda i: (i, 0)),
                  pl.BlockSpec((None, d, QKV_WIDTH), lambda i: (gate_idx, 0, 0)),
                  pl.BlockSpec((tm, LANES), lambda i: (i, 0)),
                  pl.BlockSpec((None, d, LANES), lambda i: (layer, 0, 0)), row, vec, row],
        out_specs=[row, vec], out_shape=[_sds((t, d), F32), _sds((1, d), F32)],
        compiler_params=_cparams(1))(dh_part, dgates, wc, df, wf, x, gain, dres)


def _iota2(shape, dim):
    return lax.broadcasted_iota(jnp.int32, shape, dim)


def proj_gates_forget(hm, wc, gate_idx, wf, layer, bg, bf, tm, name, deps=()):
    t, d = hm.shape
    sub = tm // QB
    nd = len(deps)

    def body(h_ref, wg_ref, bg_ref, wf_ref, bf_ref, *rest):
        g_ref, f_ref, fcol_ref, frow_ref, carry = rest[nd:]
        i, j = pl.program_id(0), pl.program_id(1)
        h = h_ref[...]
        g_ref[...] = jnp.dot(h, wg_ref[...], preferred_element_type=F32) + bg_ref[...]

        @pl.when((i == 0) & (j == 0))
        def _():
            carry[...] = jnp.zeros_like(carry)

        @pl.when(j == 0)
        def _():
            f = jnp.dot(h, wf_ref[...], preferred_element_type=F32) + bf_ref[...]
            f_ref[...] = f
            logf = _log_sigmoid(f)
            tri = (_iota2((QB, QB), 1) <= _iota2((QB, QB), 0)).astype(F32)
            for s in range(sub):
                rows = slice(s * QB, (s + 1) * QB)
                part = logf[rows, :]
                blk = jnp.dot(tri, part, precision=HIGHEST, preferred_element_type=F32) + carry[...]
                carry[...] += jnp.sum(part, axis=0, keepdims=True)
                fcol_ref[rows, :] = blk
                frow_ref[s] = blk.T[0:8, :]

    narrow = pl.BlockSpec((tm, LANES), lambda i, j: (i, 0))
    return pl.pallas_call(
        body, name=name, grid=(t // tm, 3),
        in_specs=[pl.BlockSpec((tm, d), lambda i, j: (i, 0)),
                  pl.BlockSpec((None, d, d), lambda i, j: (gate_idx, 0, j)),
                  pl.BlockSpec((1, d), lambda i, j: (0, j)),
                  pl.BlockSpec((None, d, LANES), lambda i, j: (layer, 0, 0)),
                  pl.BlockSpec((1, LANES), lambda i, j: (0, 0))] + [ANY] * nd,
        out_specs=[pl.BlockSpec((tm, d), lambda i, j: (i, j)), narrow, narrow,
                   pl.BlockSpec((sub, 8, QB), lambda i, j: (i, 0, 0))],
        out_shape=[_sds((t, 3 * d), F32), _sds((t, LANES), F32), _sds((t, LANES), F32),
                   _sds((t // QB, 8, QB), F32)],
        scratch_shapes=[pltpu.VMEM((1, LANES), F32)], compiler_params=_cparams(2))(hm, wc, bg, wf, bf, *deps)


def forget_cumsum_bwd(dfrow, f, name):
    t = f.shape[0]
    nq = t // QB

    def body(dfr_ref, f_ref, df_ref, carry):
        jj = pl.program_id(0)

        @pl.when(jj == 0)
        def _():
            carry[...] = jnp.zeros_like(carry)

        padded = jnp.concatenate([dfr_ref[...], jnp.zeros((QB - 8, QB), F32)], axis=0)
        dfcol = padded.T
        tri = (_iota2((QB, QB), 1) >= _iota2((QB, QB), 0)).astype(F32)
        dlogf = jnp.dot(tri, dfcol, precision=HIGHEST, preferred_element_type=F32) + carry[...]
        carry[...] += jnp.sum(dfcol, axis=0, keepdims=True)
        df_ref[...] = dlogf * _sigmoid(-f_ref[...])

    return pl.pallas_call(
        body, name=name, grid=(nq,),
        in_specs=[pl.BlockSpec((None, 8, QB), lambda jj: (nq - 1 - jj, 0, 0)),
                  pl.BlockSpec((QB, LANES), lambda jj: (nq - 1 - jj, 0))],
        out_specs=pl.BlockSpec((QB, LANES), lambda jj: (nq - 1 - jj, 0)),
        out_shape=_sds((t, LANES), F32),
        scratch_shapes=[pltpu.VMEM((1, LANES), F32)], compiler_params=_cparams(1))(dfrow, f)


REL_DIAG = 768
REL_SHIFT = REL_DIAG - (QB - 1)


def _diag_onehot():
    u = _iota2((REL_PAD, REL_DIAG), 1)
    rel = jnp.clip(CH_KEYS - 1 - u, -MAX_REL, MAX_REL) + MAX_REL
    return (_iota2((REL_PAD, REL_DIAG), 0) == rel).astype(F32)


def rel_bias_build(tab_t, name):
    def body(tab_ref, o_ref):
        diag = jnp.dot(tab_ref[...], _diag_onehot(), precision=HIGHEST, preferred_element_type=F32)
        band = _chunk_band()
        for h in range(N_HEADS_CH):
            rows = jnp.broadcast_to(diag[h:h + 1, :], (QB, REL_DIAG))
            o_ref[h] = pltpu.roll(rows, REL_SHIFT, 1, stride=1, stride_axis=0)[:, :CH_KEYS] + band

    return pl.pallas_call(
        body, name=name, out_shape=_sds((N_HEADS_CH, QB, CH_KEYS), F32),
        in_specs=[pl.BlockSpec(memory_space=pltpu.VMEM)], out_specs=pl.BlockSpec(memory_space=pltpu.VMEM),
    )(tab_t)


def rel_bias_scatter(dbias, name):
    def body(db_ref, o_ref, ddiag):
        flip = (_iota2((QB, QB), 0) + _iota2((QB, QB), 1) == QB - 1).astype(F32)
        for h in range(N_HEADS_CH):
            padded = jnp.concatenate([db_ref[h], jnp.zeros((QB, REL_DIAG - CH_KEYS), F32)], axis=1)
            flipped = jnp.dot(flip, padded, precision=HIGHEST, preferred_element_type=F32)
            unrolled = pltpu.roll(flipped, 0, 1, stride=1, stride_axis=0)
            ddiag[h:h + 1, :] = jnp.sum(unrolled, axis=0, keepdims=True)
        o_ref[...] = lax.dot_general(ddiag[...], _diag_onehot(), NT, precision=HIGHEST,
                                     preferred_element_type=F32)

    return pl.pallas_call(
        body, name=name, out_shape=_sds((N_HEADS_CH, REL_PAD), F32),
        in_specs=[pl.BlockSpec(memory_space=pltpu.VMEM)], out_specs=pl.BlockSpec(memory_space=pltpu.VMEM),
        scratch_shapes=[pltpu.VMEM((N_HEADS_CH, REL_DIAG), F32)],
    )(dbias)


def _hl(h):
    return slice(h * HEAD_DIM, (h + 1) * HEAD_DIM)


def _split_dot(x, tri_bf16):
    hi = x.astype(BF16)
    lo = (x - hi.astype(F32)).astype(BF16)
    return (jnp.dot(hi, tri_bf16, preferred_element_type=F32)
            + jnp.dot(lo, tri_bf16, preferred_element_type=F32))


def _krows(g):
    return pl.ds(pl.multiple_of(g * KB, KB), KB)


def _log_sigmoid_pair(z):
    sp = jnp.log(1.0 + jnp.exp(-jnp.abs(z)))
    return jnp.minimum(z, 0.0) - sp, -jnp.maximum(z, 0.0) - sp


SB_STEP_HEADS = 4
SB_COLS = SB_STEP_HEADS * HEAD_DIM


def _sb_specs(t):
    n = W_SB // SB_COLS
    q_spec = pl.BlockSpec((QB, SB_COLS), lambda hp, i: (i, hp))
    k_spec = pl.BlockSpec((t, SB_COLS), lambda hp, i: (0, n + hp))
    v_spec = pl.BlockSpec((t, SB_COLS), lambda hp, i: (0, 2 * n + hp))
    return q_spec, k_spec, v_spec


def _keys_major(xt):
    n, groups, w, _ = xt.shape
    return xt.transpose(1, 3, 0, 2).reshape(groups * KB, n * w)


def sb_fwd(qkv, name):
    t = qkv.shape[0]
    nq = t // QB

    def body(q_ref, k_ref, v_ref, o_ref, w_ref):
        i = pl.program_id(1)
        groups = i // KSUB + 1
        tri_after = (_iota2((KB, KB), 0) > _iota2((KB, KB), 1)).astype(BF16)
        t_idx = i * QB + _iota2((QB, KB), 0)
        qs = [q_ref[:, _hl(h)] for h in range(SB_STEP_HEADS)]

        def step(g, carry, masked):
            strict = (g * KB + _iota2((QB, KB), 1)) < t_idx
            out = []
            for h in range(SB_STEP_HEADS):
                tail, acc = carry[2 * h], carry[2 * h + 1]
                k = k_ref[_krows(g), _hl(h)]
                v = v_ref[_krows(g), _hl(h)]
                z = lax.dot_general(qs[h], k, NT, preferred_element_type=F32)
                lb, lf = _log_sigmoid_pair(z)
                if masked:
                    lf = jnp.where(strict, lf, 0.0)
                between = _split_dot(lf, tri_after) + tail
                w = jnp.exp(lb + between)
                if masked:
                    w = jnp.where(strict, w, 0.0)
                w = w.astype(BF16)
                w_ref[h, g] = w
                acc = acc + jnp.dot(w, v, preferred_element_type=F32)
                out += [tail + jnp.sum(lf, axis=1, keepdims=True), acc]
            return tuple(out)

        init = (jnp.zeros((QB, 1), F32), jnp.zeros((QB, HEAD_DIM), F32)) * SB_STEP_HEADS
        res = step(groups - 1, init, True)
        res = lax.fori_loop(0, groups - 1, lambda gg, c: step(groups - 2 - gg, c, False), res)
        for h in range(SB_STEP_HEADS):
            o_ref[:, _hl(h)] = res[2 * h + 1].astype(o_ref.dtype)

    q_spec, k_spec, v_spec = _sb_specs(t)
    return pl.pallas_call(
        body, name=name, grid=(W_SB // SB_COLS, nq), in_specs=[q_spec, k_spec, v_spec],
        out_specs=[pl.BlockSpec((QB, SB_COLS), lambda hp, i: (i, hp)),
                   pl.BlockSpec((SB_STEP_HEADS, None, t // KB, QB, KB), lambda hp, i: (hp, i, 0, 0, 0))],
        out_shape=[_sds((t, W_SB), BF16), _sds((4, nq, t // KB, QB, KB), BF16)],
        compiler_params=_cparams(2))(qkv, qkv, qkv)


def _hs(h):
    return slice(h * HEAD_DIM, (h + 1) * HEAD_DIM)


def sb_bwd(qkv, qkv_t, w, do, do_t, name):
    t = qkv.shape[0]
    nq = t // QB

    def body(q_ref, k_ref, v_ref, do_ref, qt_ref, dot_ref, w_ref, dq_ref, dkt_ref, dvt_ref):
        i = pl.program_id(1)

        @pl.when(i == 0)
        def _():
            dkt_ref[...] = jnp.zeros_like(dkt_ref)
            dvt_ref[...] = jnp.zeros_like(dvt_ref)

        groups = i // KSUB + 1
        tri_before = (_iota2((KB, KB), 0) < _iota2((KB, KB), 1)).astype(BF16)
        t_idx = i * QB + _iota2((QB, KB), 0)
        qs = [q_ref[:, _hl(h)] for h in range(SB_STEP_HEADS)]
        dos = [do_ref[:, _hl(h)] for h in range(SB_STEP_HEADS)]
        qts = [qt_ref[_hs(h), :] for h in range(SB_STEP_HEADS)]
        dots = [dot_ref[_hs(h), :] for h in range(SB_STEP_HEADS)]

        def grads(g, carry, masked):
            strict = (g * KB + _iota2((QB, KB), 1)) < t_idx
            out = []
            for h in range(SB_STEP_HEADS):
                head, dq = carry[2 * h], carry[2 * h + 1]
                k = k_ref[_krows(g), _hl(h)]
                v = v_ref[_krows(g), _hl(h)]
                wb = w_ref[h, g]
                z = lax.dot_general(qs[h], k, NT, preferred_element_type=F32)
                beta = _sigmoid(z)
                e = lax.dot_general(dos[h], v, NT, preferred_element_type=F32) * wb.astype(F32)
                before = _split_dot(e, tri_before) + head
                dz = e * (1.0 - beta) - before * beta
                if masked:
                    dz = jnp.where(strict, dz, 0.0)
                dzb = dz.astype(BF16)
                dq = dq + jnp.dot(dzb, k, preferred_element_type=F32)
                dkt_ref[g, _hs(h), :] += jnp.dot(qts[h], dzb, preferred_element_type=F32)
                dvt_ref[g, _hs(h), :] += jnp.dot(dots[h], wb, preferred_element_type=F32)
                out += [head + jnp.sum(e, axis=1, keepdims=True), dq]
            return tuple(out)

        init = (jnp.zeros((QB, 1), F32), jnp.zeros((QB, HEAD_DIM), F32)) * SB_STEP_HEADS
        res = lax.fori_loop(0, groups - 1, lambda g, c: grads(g, c, False), init)
        res = grads(groups - 1, res, True)
        for h in range(SB_STEP_HEADS):
            dq_ref[:, _hl(h)] = (res[2 * h + 1] * SCALE).astype(dq_ref.dtype)

    q_spec, k_spec, v_spec = _sb_specs(t)
    blk = pl.BlockSpec((QB, SB_COLS), lambda hp, i: (i, hp))
    blk_t = pl.BlockSpec((SB_COLS, QB), lambda hp, i: (hp, i))
    acc_t = pl.BlockSpec((None, t // KB, SB_COLS, KB), lambda hp, i: (hp, 0, 0, 0))
    acc_sds = _sds((W_SB // SB_COLS, t // KB, SB_COLS, KB), F32)
    return pl.pallas_call(
        body, name=name, grid=(W_SB // SB_COLS, nq),
        in_specs=[q_spec, k_spec, v_spec, blk, blk_t, blk_t,
                  pl.BlockSpec((SB_STEP_HEADS, None, t // KB, QB, KB), lambda hp, i: (hp, i, 0, 0, 0))],
        out_specs=[blk, acc_t, acc_t],
        out_shape=[_sds((t, W_SB), BF16), acc_sds, acc_sds],
        compiler_params=_cparams(2))(qkv, qkv, qkv, do, qkv_t, do_t, w)


FOX_STEP_HEADS = 4
FOX_COLS = FOX_STEP_HEADS * HEAD_DIM


def _fox_specs(t):
    first = 3 * (W_SB + W_CH) // FOX_COLS
    n = W_FOX // FOX_COLS
    q_spec = pl.BlockSpec((QB, FOX_COLS), lambda hp, i: (i, first + hp))
    k_spec = pl.BlockSpec((t, FOX_COLS), lambda hp, i: (0, first + n + hp))
    v_spec = pl.BlockSpec((t, FOX_COLS), lambda hp, i: (0, first + 2 * n + hp))
    return q_spec, k_spec, v_spec


def fox_fwd(qkv, fcol, frow, name):
    t = qkv.shape[0]
    nq = t // QB

    def body(q_ref, k_ref, v_ref, fc_ref, fr_ref, o_ref, lse_ref):
        hp = pl.program_id(0)
        i = pl.program_id(1)
        groups = i // KSUB + 1
        t_idx = i * QB + _iota2((QB, KB), 0)
        lane = _iota2((QB, LANES), 1)
        sub = _iota2((8, KB), 0)
        qs = [q_ref[:, _hl(h)] for h in range(FOX_STEP_HEADS)]
        f_qs = [jnp.sum(jnp.where(lane == hp * FOX_STEP_HEADS + h, fc_ref[...], 0.0), axis=1, keepdims=True)
                for h in range(FOX_STEP_HEADS)]

        def step(g, carry, masked):
            causal = (g * KB + _iota2((QB, KB), 1)) <= t_idx
            fr = fr_ref[g]
            out = []
            for h in range(FOX_STEP_HEADS):
                m, l, acc = carry[3 * h:3 * h + 3]
                k = k_ref[_krows(g), _hl(h)]
                v = v_ref[_krows(g), _hl(h)]
                f_k = jnp.sum(jnp.where(sub == hp * FOX_STEP_HEADS + h, fr, 0.0), axis=0, keepdims=True)
                z = lax.dot_general(qs[h], k, NT, preferred_element_type=F32) + f_qs[h] - f_k
                if masked:
                    z = jnp.where(causal, z, NEG)
                m_new = jnp.maximum(m, jnp.max(z, axis=1, keepdims=True))
                p = jnp.exp(z - m_new)
                corr = jnp.exp(m - m_new)
                l = l * corr + jnp.sum(p, axis=1, keepdims=True)
                acc = acc * corr + jnp.dot(p.astype(BF16), v, preferred_element_type=F32)
                out += [m_new, l, acc]
            return tuple(out)

        init = (jnp.full((QB, 1), NEG, F32), jnp.zeros((QB, 1), F32),
                jnp.zeros((QB, HEAD_DIM), F32)) * FOX_STEP_HEADS
        res = lax.fori_loop(0, groups - 1, lambda g, c: step(g, c, False), init)
        res = step(groups - 1, res, True)
        for h in range(FOX_STEP_HEADS):
            m, l, acc = res[3 * h:3 * h + 3]
            o_ref[:, _hl(h)] = (acc / l).astype(o_ref.dtype)
            lse_ref[:, _hl(h)] = jnp.broadcast_to(m + jnp.log(l), (QB, HEAD_DIM))

    q_spec, k_spec, v_spec = _fox_specs(t)
    blk = pl.BlockSpec((QB, FOX_COLS), lambda hp, i: (i, hp))
    return pl.pallas_call(
        body, name=name, grid=(W_FOX // FOX_COLS, nq),
        in_specs=[q_spec, k_spec, v_spec, pl.BlockSpec((QB, LANES), lambda hp, i: (i, 0)),
                  pl.BlockSpec((t // KB, 8, KB), lambda hp, i: (0, 0, 0))],
        out_specs=[blk, blk],
        out_shape=[_sds((t, W_FOX), BF16), _sds((t, W_FOX), F32)],
        compiler_params=_cparams(2))(qkv, qkv, qkv, fcol, frow)


def fox_bwd(qkv, qkv_t, fcol, frow, o, lse, do, do_t, name, do_col=0):
    t = qkv.shape[0]
    nq = t // QB

    def body(q_ref, k_ref, v_ref, fc_ref, fr_ref, o_ref, lse_ref, do_ref, qt_ref, dot_ref,
             dq_ref, dk_ref, dv_ref, dfr_ref):
        hp = pl.program_id(0)
        i = pl.program_id(1)
        qts = [qt_ref[_hs(h), :] for h in range(FOX_STEP_HEADS)]
        dots = [dot_ref[_hs(h), :] for h in range(FOX_STEP_HEADS)]

        @pl.when(i == 0)
        def _():
            dk_ref[...] = jnp.zeros_like(dk_ref)
            dv_ref[...] = jnp.zeros_like(dv_ref)

        @pl.when((i == 0) & (hp == 0))
        def _():
            dfr_ref[...] = jnp.zeros_like(dfr_ref)

        groups = i // KSUB + 1
        t_idx = i * QB + _iota2((QB, KB), 0)
        lane = _iota2((QB, LANES), 1)
        sub = _iota2((8, KB), 0)
        qs = [q_ref[:, _hl(h)] for h in range(FOX_STEP_HEADS)]
        dos = [do_ref[:, _hl(h)] for h in range(FOX_STEP_HEADS)]
        f_qs = [jnp.sum(jnp.where(lane == hp * FOX_STEP_HEADS + h, fc_ref[...], 0.0), axis=1, keepdims=True)
                for h in range(FOX_STEP_HEADS)]
        lse_qs = [lse_ref[:, h * HEAD_DIM:h * HEAD_DIM + 1] for h in range(FOX_STEP_HEADS)]
        deltas = [jnp.sum(dos[h].astype(F32) * o_ref[:, _hl(h)].astype(F32), axis=1, keepdims=True)
                  for h in range(FOX_STEP_HEADS)]

        def step(g, dqs, masked):
            causal = (g * KB + _iota2((QB, KB), 1)) <= t_idx
            fr = fr_ref[g]
            out = []
            dfr = jnp.zeros((8, KB), F32)
            for h in range(FOX_STEP_HEADS):
                k = k_ref[_krows(g), _hl(h)]
                v = v_ref[_krows(g), _hl(h)]
                f_k = jnp.sum(jnp.where(sub == hp * FOX_STEP_HEADS + h, fr, 0.0), axis=0, keepdims=True)
                z = lax.dot_general(qs[h], k, NT, preferred_element_type=F32) + f_qs[h] - f_k
                p = jnp.exp(z - lse_qs[h])
                if masked:
                    p = jnp.where(causal, p, 0.0)
                dp = lax.dot_general(dos[h], v, NT, preferred_element_type=F32)
                ds = p * (dp - deltas[h])
                dsb = ds.astype(BF16)
                out.append(dqs[h] + jnp.dot(dsb, k, preferred_element_type=F32))
                dk_ref[g, _hs(h), :] += jnp.dot(qts[h], dsb, preferred_element_type=F32)
                dv_ref[g, _hs(h), :] += jnp.dot(dots[h], p.astype(BF16), preferred_element_type=F32)
                colsum = jnp.sum(ds, axis=0, keepdims=True)
                dfr = dfr + jnp.where(sub == hp * FOX_STEP_HEADS + h, -colsum, 0.0)
            dfr_ref[g] += dfr
            return tuple(out)

        res = lax.fori_loop(0, groups - 1, lambda g, c: step(g, c, False),
                            (jnp.zeros((QB, HEAD_DIM), F32),) * FOX_STEP_HEADS)
        res = step(groups - 1, res, True)
        for h in range(FOX_STEP_HEADS):
            dq_ref[:, _hl(h)] = (res[h] * SCALE).astype(dq_ref.dtype)

    q_spec, k_spec, v_spec = _fox_specs(t)
    blk = pl.BlockSpec((QB, FOX_COLS), lambda hp, i: (i, hp))
    frs = pl.BlockSpec((t // KB, 8, KB), lambda hp, i: (0, 0, 0))
    acc_t = pl.BlockSpec((None, t // KB, FOX_COLS, KB), lambda hp, i: (hp, 0, 0, 0))
    acc_sds = _sds((W_FOX // FOX_COLS, t // KB, FOX_COLS, KB), F32)
    return pl.pallas_call(
        body, name=name, grid=(W_FOX // FOX_COLS, nq),
        in_specs=[q_spec, k_spec, v_spec, pl.BlockSpec((QB, LANES), lambda hp, i: (i, 0)), frs,
                  blk, blk, pl.BlockSpec((QB, FOX_COLS), lambda hp, i: (i, do_col + hp)),
                  pl.BlockSpec((FOX_COLS, QB), lambda hp, i: (3 * (W_SB + W_CH) // FOX_COLS + hp, i)),
                  pl.BlockSpec((FOX_COLS, QB), lambda hp, i: (do_col + hp, i))],
        out_specs=[blk, acc_t, acc_t, frs],
        out_shape=[_sds((t, W_FOX), BF16), acc_sds, acc_sds, _sds((t // KB, 8, KB), F32)],
        compiler_params=_cparams(2))(qkv, qkv, qkv, fcol, frow, o, lse, do, qkv_t, do_t)


def _frow_to_groups(frow):
    n = frow.shape[0] // KSUB
    return frow.reshape(n, KSUB, 8, QB).transpose(0, 2, 1, 3).reshape(n, 8, KB)


def _frow_from_groups(frow):
    n = frow.shape[0]
    return frow.reshape(n, 8, KSUB, QB).transpose(0, 2, 1, 3).reshape(n * KSUB, 8, QB)


def _chunk_band():
    qi = _iota2((QB, CH_KEYS), 0)
    kj = _iota2((QB, CH_KEYS), 1)
    dchunk = (qi >> 6) + LEFT_CHUNKS - (kj >> 6)
    return jnp.where((dchunk >= 0) & (dchunk <= LEFT_CHUNKS), 0.0, NEG)


def _chunk_pad_row(i):
    kj = _iota2((1, CH_KEYS), 1)
    return jnp.where((i - (CH_WIN - 1)) * QB + kj >= 0, 0.0, NEG)


CH_PAD = (CH_WIN - 1) * QB
CH_STEP_HEADS = 4
CH_COLS = CH_STEP_HEADS * HEAD_DIM


def _window(i):
    return pl.ds(pl.multiple_of(i * QB, QB), CH_KEYS)


def _chunk_weights(q, kw, bias, pad_row):
    z = lax.dot_general(q, kw, NT, preferred_element_type=F32) + bias + pad_row
    e = jnp.exp(z - jnp.max(z, axis=1, keepdims=True))
    return e, 1.0 / jnp.sum(e, axis=1, keepdims=True)


def _chunk_specs(t):
    q_spec = pl.BlockSpec((QB, CH_COLS), lambda hp, i: (i, 3 * W_SB // CH_COLS + hp))
    kv_spec = pl.BlockSpec((t + CH_PAD, CH_COLS), lambda hp, i: (0, hp))
    return q_spec, kv_spec


def chunk_fwd(qkv, kp, vp, bias, name):
    t = qkv.shape[0]
    nq = t // QB

    def body(q_ref, k_ref, v_ref, b_ref, o_ref):
        i = pl.program_id(1)
        pad_row = _chunk_pad_row(i)
        for h in range(CH_STEP_HEADS):
            e, inv = _chunk_weights(q_ref[:, _hl(h)], k_ref[_window(i), _hl(h)], b_ref[h], pad_row)
            o = jnp.dot(e.astype(BF16), v_ref[_window(i), _hl(h)], preferred_element_type=F32)
            o_ref[:, _hl(h)] = (o * inv).astype(o_ref.dtype)

    q_spec, kv_spec = _chunk_specs(t)
    return pl.pallas_call(
        body, name=name, grid=(W_CH // CH_COLS, nq),
        in_specs=[q_spec, kv_spec, kv_spec,
                  pl.BlockSpec((CH_STEP_HEADS, QB, CH_KEYS), lambda hp, i: (hp, 0, 0))],
        out_specs=pl.BlockSpec((QB, CH_COLS), lambda hp, i: (i, hp)),
        out_shape=_sds((t, W_CH), BF16), compiler_params=_cparams(2))(qkv, kp, vp, bias)


def chunk_bwd(qkv, qkv_t, kp, vp, bias, do, do_t, name, do_col=0):
    t = qkv.shape[0]
    nq = t // QB

    def body(q_ref, k_ref, v_ref, b_ref, do_ref, qt_ref, dot_ref, dq_ref, dk_ref, dv_ref, db_ref):
        i = pl.program_id(1)

        @pl.when(i == 0)
        def _():
            dk_ref[...] = jnp.zeros_like(dk_ref)
            dv_ref[...] = jnp.zeros_like(dv_ref)
            db_ref[...] = jnp.zeros_like(db_ref)

        pad_row = _chunk_pad_row(i)
        for h in range(CH_STEP_HEADS):
            q = q_ref[:, _hl(h)]
            dov = do_ref[:, _hl(h)]
            kw = k_ref[_window(i), _hl(h)]
            e, inv = _chunk_weights(q, kw, b_ref[h], pad_row)
            p = e * inv
            dp = lax.dot_general(dov, v_ref[_window(i), _hl(h)], NT, preferred_element_type=F32)
            ds = p * (dp - jnp.sum(p * dp, axis=1, keepdims=True))
            db_ref[h] += ds
            dsb = ds.astype(BF16)
            dq_ref[:, _hl(h)] = (jnp.dot(dsb, kw, preferred_element_type=F32) * SCALE).astype(dq_ref.dtype)
            dkt = jnp.dot(qt_ref[_hs(h), :], dsb, preferred_element_type=F32)
            dvt = jnp.dot(dot_ref[_hs(h), :], p.astype(BF16), preferred_element_type=F32)
            for b in range(CH_WIN):
                dk_ref[i + b, _hs(h), :] += dkt[:, b * QB:(b + 1) * QB]
                dv_ref[i + b, _hs(h), :] += dvt[:, b * QB:(b + 1) * QB]

    q_spec, kv_spec = _chunk_specs(t)
    blk = pl.BlockSpec((QB, CH_COLS), lambda hp, i: (i, hp))
    bspec = pl.BlockSpec((CH_STEP_HEADS, QB, CH_KEYS), lambda hp, i: (hp, 0, 0))
    nblk = nq + CH_WIN - 1
    acc_t = pl.BlockSpec((None, nblk, CH_COLS, QB), lambda hp, i: (hp, 0, 0, 0))
    acc_sds = _sds((W_CH // CH_COLS, nblk, CH_COLS, QB), F32)
    return pl.pallas_call(
        body, name=name, grid=(W_CH // CH_COLS, nq),
        in_specs=[q_spec, kv_spec, kv_spec, bspec,
                  pl.BlockSpec((QB, CH_COLS), lambda hp, i: (i, do_col + hp)),
                  pl.BlockSpec((CH_COLS, QB), lambda hp, i: (3 * W_SB // CH_COLS + hp, i)),
                  pl.BlockSpec((CH_COLS, QB), lambda hp, i: (do_col + hp, i))],
        out_specs=[blk, acc_t, acc_t, bspec],
        out_shape=[_sds((t, W_CH), BF16), acc_sds, acc_sds, _sds((N_HEADS_CH, QB, CH_KEYS), F32)],
        compiler_params=_cparams(2))(qkv, kp, vp, bias, do, qkv_t, do_t)


def _sum_parts(p_ref):
    total = p_ref[0].astype(F32)
    for p in range(1, p_ref.shape[0]):
        total = total + p_ref[p].astype(F32)
    return total


def sum_parts_multi(parts_list, name):
    n = len(parts_list)

    def body(*refs):
        for p_ref, o_ref in zip(refs[:n], refs[n:]):
            o_ref[...] = _sum_parts(p_ref)

    shapes = [p.shape[2:] for p in parts_list]
    return pl.pallas_call(
        body, name=name, grid=(1,),
        in_specs=[pl.BlockSpec((N_DEV, None, r, c), lambda s: (0, 0, 0, 0)) for r, c in shapes],
        out_specs=[pl.BlockSpec((r, c), lambda s: (0, 0)) for r, c in shapes],
        out_shape=[_sds((r, c), F32) for r, c in shapes], compiler_params=_cparams(1))(*parts_list)


def adamw(parts, w, m, v, grid, p_specs, w_spec, name):
    c1 = 1.0 / (1.0 - ADAM_B1 ** ADAM_STEP)
    c2 = 1.0 / (1.0 - ADAM_B2 ** ADAM_STEP)
    n = len(parts)

    def body(*refs):
        w_ref, m_ref, v_ref, g_out, d_out, m_out, v_out = refs[n:]
        g = _sum_parts(refs[0])
        for q in range(1, n):
            g = jnp.where(pl.program_id(0) == q, _sum_parts(refs[q]), g)
        m_new = ADAM_B1 * m_ref[...] + (1.0 - ADAM_B1) * g
        v_new = ADAM_B2 * v_ref[...] + (1.0 - ADAM_B2) * (g * g)
        m_hat = m_new * c1
        v_hat = v_new * c2
        g_out[...] = g
        d_out[...] = -ADAM_LR * (m_hat / (jnp.sqrt(v_hat) + ADAM_EPS) + ADAM_WD * w_ref[...])
        m_out[...] = m_new
        v_out[...] = v_new

    out = _sds(w.shape, F32)
    return pl.pallas_call(
        body, name=name, grid=grid, in_specs=[*p_specs, w_spec, w_spec, w_spec],
        out_specs=[w_spec] * 4, out_shape=[out] * 4,
        compiler_params=_cparams(len(grid)))(*parts, w, m, v)


def _ffn_fwd(x, gain, wa, wb_after, s, tm, tag, on_event, deps=()):
    t = x.shape[0]
    hn = rmsnorm_fwd(x, gain, tm, f"rms_{tag}", deps)
    gu, act = ffn_in_swiglu(hn, wa, s, min(2 * tm, t), f"ffn_in_{tag}")
    relayed = on_event("act", act)
    wb = wb_after(act)
    y = ffn_out_residual(act, wb, x, s, min(2 * tm, t), f"ffn_out_{tag}", relayed)
    return y, (hn, gu, act), wb


def _ffn_bwd(dy, x, gain, saved, wa, wb, s, tm, tag, on_grads):
    t = x.shape[0]
    hn, gu, act = saved
    dgu = ffn_dact_swiglu(dy, wb, gu, s, min(2 * tm, t), f"ffn_dact_{tag}")
    dwb = matmul(TN, act, dy, _sds((4, FF_BLK, D_MODEL), BF16), (4, 1, 1),
                 pl.BlockSpec((None, t, FF_BLK), lambda i, j, k: (i, 0, 0)),
                 pl.BlockSpec((t, D_MODEL), lambda i, j, k: (0, 0)),
                 pl.BlockSpec((None, FF_BLK, D_MODEL), lambda i, j, k: (i, 0, 0)),
                 None, name=f"ffn_dwout_{tag}", alpha=0.5)
    dwa = matmul(TN, dgu, hn, _sds((8, FF_BLK, D_MODEL), BF16), (1, 8, 1),
                 pl.BlockSpec((None, None, t, FF_BLK), lambda i, j, k: (j % 4, j // 4, 0, 0)),
                 pl.BlockSpec((t, D_MODEL), lambda i, j, k: (0, 0)),
                 pl.BlockSpec((None, FF_BLK, D_MODEL), lambda i, j, k: (j, 0, 0)),
                 None, name=f"ffn_dwin_{tag}")
    deps = on_grads(dwa, dwb)
    return ffn_dh_norm_bwd(dgu, wa, s, x, gain, dy, tm, f"ffn_dh_{tag}", deps)


_Q_COLUMN_SCALE = np.ones((1, QKV_WIDTH), np.float32)
for _lo, _width in ((0, W_SB), (3 * W_SB, W_CH), (3 * (W_SB + W_CH), W_FOX)):
    _Q_COLUMN_SCALE[0, _lo:_lo + _width] = SCALE


def _mixer_fwd(x, gain, wqkv, wf, wgate, late_after, bq, bf, bg, bias, layer, tm, tag, on_event):
    t = x.shape[0]
    nt = t // tm
    hm = rmsnorm_fwd(x, gain, tm, f"rms_{tag}")
    a_full = pl.BlockSpec((tm, D_MODEL), lambda i, j, k: (i, 0))
    wide_out = pl.BlockSpec((tm, D_MODEL), lambda i, j, k: (i, j))
    wide_b = pl.BlockSpec((1, D_MODEL), lambda i, j, k: (0, j))
    tq = min(2 * tm, t)
    qkv, qkv_t = matmul(NN, hm, wqkv, _sds((t, QKV_WIDTH), BF16), (t // tq, 3, 1),
                        pl.BlockSpec((tq, D_MODEL), lambda i, j, k: (i, 0)),
                        pl.BlockSpec((None, D_MODEL, D_MODEL), lambda i, j, k: (layer, 0, j)),
                        pl.BlockSpec((tq, D_MODEL), lambda i, j, k: (i, j)), None,
                        name=f"proj_qkv_{tag}", bias=bq, bias_spec=wide_b,
                        scale=jnp.asarray(_Q_COLUMN_SCALE), scale_spec=wide_b,
                        out_t_sds=_sds((QKV_WIDTH, t), BF16),
                        out_t_spec=pl.BlockSpec((D_MODEL, tq), lambda i, j, k: (j, i)))
    relayed = on_event("qkv", qkv)
    gates, f, fcol, frow = proj_gates_forget(hm, wgate, layer + 1, wf, layer, bg, bf, tq,
                                             f"proj_gate_{tag}", relayed)
    frow = _frow_to_groups(frow)
    o_sb, w_sb = sb_fwd(qkv, f"sb_fwd_{tag}")
    relayed = on_event("o_sb", o_sb)
    kp = jnp.pad(qkv[:, 10 * LANES:14 * LANES], ((CH_PAD, 0), (0, 0)))
    vp = jnp.pad(qkv[:, 14 * LANES:18 * LANES], ((CH_PAD, 0), (0, 0)))
    o_ch = chunk_fwd(qkv, kp, vp, bias, f"chunk_fwd_{tag}")
    o_fox, lse = fox_fwd(qkv, fcol, frow, f"fox_fwd_{tag}")
    wbr, wout = late_after(o_fox)
    ys, merged = branch_merge((o_sb, o_ch, o_fox), wbr, layer, gates, tm, f"branch_merge_{tag}", relayed)
    x_new = matmul(NN, merged, wout, _sds((t, D_MODEL), F32), (nt, 1, 1), a_full,
                   pl.BlockSpec((None, D_MODEL, D_MODEL), lambda i, j, k: (layer, 0, 0)), a_full, None,
                   name=f"wout_{tag}", res=x, res_spec=a_full)
    saved = (hm, qkv, gates, f, fcol, frow, o_sb, o_ch, o_fox, lse, ys, merged, kp, vp, w_sb, qkv_t)
    return x_new, saved, wbr, wout


def _mixer_bwd(dy, x, gain, saved, wqkv, wf, wgate, wbr, wout, bias, layer, tm, tag, on_grads):
    t = x.shape[0]
    nt = t // tm
    hm, qkv, gates, f, fcol, frow, o_sb, o_ch, o_fox, lse, ys, merged, kp, vp, w_sb, qkv_t = saved
    a_full = pl.BlockSpec((tm, D_MODEL), lambda i, j, k: (i, 0))
    red_row = pl.BlockSpec((tm, D_MODEL), lambda i, j, k: (k, 0))
    sq = pl.BlockSpec((D_MODEL, D_MODEL), lambda i, j, k: (0, 0))
    dgates, dys = dmerged_merge_bwd(dy, wout, layer, gates, ys, tm // 2, f"dmerged_{tag}")
    all_t = pl.BlockSpec((t, D_MODEL), lambda i, j, k: (0, 0))
    dwout = matmul(TN, merged, dy, _sds((D_MODEL, D_MODEL), BF16), (1, 1, 1), all_t, all_t, sq,
                   None, name=f"dwout_{tag}")
    do, do_t, dwbr = branch_bwd(dys, (o_sb, o_ch, o_fox), wbr, layer, tm, f"dbranch_{tag}")
    dq_a, dk_a, dv_a = sb_bwd(qkv, qkv_t, w_sb, do, do_t, f"sb_bwd_{tag}")
    dk_a, dv_a = _keys_major(dk_a), _keys_major(dv_a)
    dq_b, dk_b, dv_b, dbias = chunk_bwd(qkv, qkv_t, kp, vp, bias, do, do_t, f"chunk_bwd_{tag}",
                                        do_col=W_SB // CH_COLS)
    dk_b, dv_b = [x[:, CH_WIN - 1:].transpose(1, 3, 0, 2).reshape(t, W_CH) for x in (dk_b, dv_b)]
    dq_c, dk_c, dv_c, dfrow = fox_bwd(qkv, qkv_t, fcol, frow, o_fox, lse, do, do_t, f"fox_bwd_{tag}",
                                      do_col=(W_SB + W_CH) // FOX_COLS)
    dk_c, dv_c = _keys_major(dk_c), _keys_major(dv_c)
    df = forget_cumsum_bwd(_frow_from_groups(dfrow), f, f"fcum_bwd_{tag}")
    dqkv = jnp.concatenate([p.astype(BF16) for p in
                            (dq_a, dk_a, dv_a, dq_b, dk_b, dv_b, dq_c, dk_c, dv_c)], axis=1)
    dtab = rel_bias_scatter(dbias, f"rel_scatter_{tag}")

    all_rows = pl.BlockSpec((t, D_MODEL), lambda i, j, k: (0, 0))
    wide_b = pl.BlockSpec((t, D_MODEL), lambda i, j, k: (0, j))
    wide_o = pl.BlockSpec((D_MODEL, D_MODEL), lambda i, j, k: (0, j))
    wide_cs = pl.BlockSpec((1, D_MODEL), lambda i, j, k: (0, j))
    dwqkv, dbq = matmul(TN, hm, dqkv, _sds((D_MODEL, QKV_WIDTH), BF16), (1, 3, 1), all_rows, wide_b,
                        wide_o, None, name=f"dwqkv_{tag}",
                        colsum_sds=_sds((1, QKV_WIDTH), F32), colsum_spec=wide_cs)
    dwgate, dbg = matmul(TN, hm, dgates, _sds((D_MODEL, 3 * D_MODEL), BF16), (1, 3, 1), all_rows,
                         wide_b, wide_o, None, name=f"dwgate_{tag}",
                         colsum_sds=_sds((1, 3 * D_MODEL), F32), colsum_spec=wide_cs)
    dwf, dbf = matmul(TN, hm, df, _sds((D_MODEL, LANES), BF16), (1, 1, 1), all_rows,
                      pl.BlockSpec((t, LANES), lambda i, j, k: (0, 0)),
                      pl.BlockSpec((D_MODEL, LANES), lambda i, j, k: (0, 0)), None,
                      name=f"dwf_{tag}", colsum_sds=_sds((1, LANES), F32),
                      colsum_spec=pl.BlockSpec((1, LANES), lambda i, j, k: (0, 0)))
    deps = on_grads(dict(dwqkv=dwqkv, dwgate=dwgate, dwf=dwf, dwbr=dwbr, dwout=dwout))
    wide_a = pl.BlockSpec((tm, QKV_WIDTH), lambda i, j, k: (i, 0))
    dhm = matmul(NT, dqkv, wqkv, _sds((t, D_MODEL), F32), (nt, 1, 1), wide_a,
                 pl.BlockSpec((None, D_MODEL, QKV_WIDTH), lambda i, j, k: (layer, 0, 0)), a_full,
                 None, name=f"dhm_qkv_{tag}", deps=deps)
    dx, dgain = mixer_dh_norm_bwd(dhm, dgates, wgate, layer + 1, df, wf, layer, x, gain, dy, tm,
                                  f"dhm_gate_{tag}")
    return dx, dict(dbq=dbq, dbg=dbg, dbf=dbf, dtab=dtab, dgain=dgain)


def _pack_small(pieces):
    flat = jnp.concatenate([p.reshape(-1).astype(F32) for p in pieces])
    flat = jnp.pad(flat, (0, SMALL_ROWS * LANES - flat.shape[0]))
    return flat.reshape(SMALL_ROWS, LANES)


def _unpack_small(packed, shapes):
    flat = packed.reshape(-1)
    out, pos = [], 0
    for shp in shapes:
        n = int(np.prod(shp))
        out.append(flat[pos:pos + n].reshape(shp))
        pos += n
    return out


def kernel(x, g_ffn1, w_ffn1_in, w_ffn1_out, g_mix, w_in, b_in, rel_bias, w_br_sb, w_br_ch, w_br_fox, w_out, g_ffn2, w_ffn2_in, w_ffn2_out, g_final, loss_target, m_g_ffn1, m_w_ffn1_in, m_w_ffn1_out, m_g_mix, m_w_in, m_b_in, m_rel_bias, m_w_br_sb, m_w_br_ch, m_w_br_fox, m_w_out, m_g_ffn2, m_w_ffn2_in, m_w_ffn2_out, m_g_final, v_g_ffn1, v_w_ffn1_in, v_w_ffn1_out, v_g_mix, v_w_in, v_b_in, v_rel_bias, v_w_br_sb, v_w_br_ch, v_w_br_fox, v_w_out, v_g_ffn2, v_w_ffn2_in, v_w_ffn2_out, v_g_final):
    t = x.shape[1]
    tm = min(512, t)
    xs = x[0]
    target = loss_target[0]
    f_lo, f_hi = QKV_WIDTH, QKV_WIDTH + N_HEADS_FOX

    def ffn_shards(w_in_, w_out_, l):
        return [w_in_[l:l + 1].astype(BF16), w_out_[l:l + 1].astype(BF16)]

    def mixer_shards(l):
        wl = w_in[l]
        return [jnp.stack([wl[:, :QKV_WIDTH], wl[:, f_hi:]]).astype(BF16),
                jnp.pad(wl[:, f_lo:f_hi], ((0, 0), (0, LANES - N_HEADS_FOX)))[None].astype(BF16),
                w_out[l:l + 1].astype(BF16),
                jnp.concatenate([w_br_sb[l], w_br_ch[l], w_br_fox[l]], axis=0)[None].astype(BF16)]

    gathers = {}
    gather_tokens = []

    def start_gather(shards, name):
        handle = gather_start(shards, name, deps=gather_tokens[-1:])
        gather_tokens.append(handle["token"])
        return handle

    def relay(handle, after):
        if "send2" not in handle:
            gather_relay(handle, after)

    relay_on = {("mix", 0, "qkv"): ("mix", 0, 1), ("mix", 0, "o_sb"): ("ffn2", 0, 0),
                ("ffn2", 0, "act"): ("ffn1", 1, 0), ("ffn1", 1, "act"): ("mix", 1, 0),
                ("mix", 1, "qkv"): ("ffn2", 1, 0)}

    def on_event(grp, l):
        def fire(event, array):
            target = relay_on.get((grp, l, event))
            if target is None:
                return ()
            handle = gathers[target[:2]][target[2]]
            relay(handle, array)
            return (handle["relay_token"],)
        return fire

    for l in range(DEPTH):
        for grp, shards in (("ffn1", ffn_shards(w_ffn1_in, w_ffn1_out, l)), ("mix", mixer_shards(l)),
                            ("ffn2", ffn_shards(w_ffn2_in, w_ffn2_out, l))):
            cut = len(shards) // 2
            if l == 0 and grp != "ffn2":
                gathers[(grp, l)] = (start_gather(shards[:cut], f"gather_{grp}_l{l}_a"),
                                     start_gather(shards[cut:], f"gather_{grp}_l{l}_b"))
            else:
                gathers[(grp, l)] = (start_gather(shards, f"gather_{grp}_l{l}"),)

    def gathered(key, after):
        hs = gathers[key]
        cut = hs[0]["n"]
        relay(hs[0], after)
        first = gather_finish(hs[0], after)
        if len(hs) == 1:
            return first[:cut // 2], lambda later: first[cut // 2:]

        def second(later):
            relay(hs[1], later)
            return gather_finish(hs[1], later)

        return first, second

    def ffn_weights(key, after):
        (wa_,), rest = gathered(key, after)
        return wa_, lambda later: rest(later)[0].reshape(1, 4, FF_BLK, D_MODEL)

    def mixer_weights(key, after):
        (wc_, wf_), rest = gathered(key, after)

        def late(later):
            wout_, wbr_ = rest(later)
            return (wbr_.transpose(0, 2, 1, 3).reshape(1, D_MODEL, D_MODEL), wout_.reshape(1, D_MODEL, D_MODEL))

        return wc_.reshape(2, D_MODEL, QKV_WIDTH), wf_.reshape(1, D_MODEL, LANES), late

    bq = b_in[:, None, :QKV_WIDTH]
    bf = jnp.pad(b_in[:, f_lo:f_hi], ((0, 0), (0, LANES - N_HEADS_FOX)))[:, None, :]
    bg = b_in[:, None, f_hi:]
    tab_t = jnp.pad(rel_bias.transpose(0, 2, 1), ((0, 0), (0, 0), (0, REL_PAD - N_REL)))

    h = xs
    saved = []
    weights = []
    for l in range(DEPTH):
        bias = rel_bias_build(tab_t[l], f"rel_build_l{l}").reshape(N_HEADS_CH, QB, CH_KEYS)
        x0 = h
        wa1, wb1_after = ffn_weights(("ffn1", l), x0)
        x1, s1, wb1 = _ffn_fwd(x0, g_ffn1[l:l + 1], wa1, wb1_after, 0, tm, f"ffn1_l{l}", on_event("ffn1", l),
                               deps=gather_tokens if l == 0 else ())
        wc, wf, late_after = mixer_weights(("mix", l), x1)
        x2, sm, wbr, wout = _mixer_fwd(x1, g_mix[l:l + 1], wc, wf, wc, late_after, bq[l], bf[l], bg[l],
                                       bias, 0, tm, f"mix_l{l}", on_event("mix", l))
        wa2, wb2_after = ffn_weights(("ffn2", l), x2)
        x3, s2, wb2 = _ffn_fwd(x2, g_ffn2[l:l + 1], wa2, wb2_after, 0, tm, f"ffn2_l{l}", on_event("ffn2", l))
        saved.append((x0, x1, x2, s1, sm, s2, bias))
        weights.append(((wa1, wb1), (wc, wf, wout, wbr), (wa2, wb2)))
        h = x3

    dx, dg_final, loss_blk = loss_head(h, g_final[None, :], target, tm, "loss_head")

    g_mix_l = [None] * DEPTH
    dgains = {}
    scatters = {}

    def scatter_ffn(key):
        def on_grads(dwa, dwb):
            scatters[key] = exchange_start(
                "scatter", [dwa[None], dwb.reshape(1, N_DEV, D_FF // N_DEV, D_MODEL)],
                f"scatter_{key[0]}_l{key[1]}")
            return (scatters[key]["token"],)
        return on_grads

    def scatter_mixer(key):
        def on_grads(gm):
            scatters[key] = exchange_start(
                "scatter",
                [gm["dwqkv"].reshape(1, N_DEV, LANES, QKV_WIDTH), gm["dwgate"].reshape(1, N_DEV, LANES, QKV_WIDTH),
                 gm["dwf"].reshape(1, N_DEV, LANES, LANES), gm["dwout"].reshape(1, N_DEV, LANES, D_MODEL),
                 gm["dwbr"].reshape(1, D_MODEL, N_DEV, LANES).transpose(0, 2, 1, 3)],
                f"scatter_{key[0]}_l{key[1]}")
            return (scatters[key]["token"],)
        return on_grads

    for l in reversed(range(DEPTH)):
        x0, x1, x2, s1, sm, s2, bias = saved[l]
        w1, (wc, wf, wout, wbr), w2 = weights[l]
        dx, dgains[("ffn2", l)] = _ffn_bwd(dx, x2, g_ffn2[l:l + 1], s2, *w2, 0, tm, f"ffn2_l{l}",
                                           scatter_ffn(("ffn2", l)))
        dx, g_mix_l[l] = _mixer_bwd(dx, x1, g_mix[l:l + 1], sm, wc, wf, wc, wbr, wout, bias, 0, tm,
                                    f"mix_l{l}", scatter_mixer(("mix", l)))
        dx, dgains[("ffn1", l)] = _ffn_bwd(dx, x0, g_ffn1[l:l + 1], s1, *w1, 0, tm, f"ffn1_l{l}",
                                           scatter_ffn(("ffn1", l)))

    small_shapes = []
    small_pieces = []
    small_w, small_m, small_v = [], [], []

    def add_small(piece, w, m, v):
        small_shapes.append(w.shape)
        small_pieces.append(piece)
        small_w.append(w); small_m.append(m); small_v.append(v)

    dg1 = jnp.concatenate([dgains[("ffn1", l)] for l in range(DEPTH)], axis=0)
    dgm = jnp.concatenate([g_mix_l[l]["dgain"] for l in range(DEPTH)], axis=0)
    dg2 = jnp.concatenate([dgains[("ffn2", l)] for l in range(DEPTH)], axis=0)
    db = jnp.stack([jnp.concatenate([g_mix_l[l]["dbq"][0], g_mix_l[l]["dbf"][0, :N_HEADS_FOX],
                                     g_mix_l[l]["dbg"][0]]) for l in range(DEPTH)])
    drel = jnp.stack([g_mix_l[l]["dtab"][:, :N_REL].T for l in range(DEPTH)])
    add_small(dg1, g_ffn1, m_g_ffn1, v_g_ffn1)
    add_small(dgm, g_mix, m_g_mix, v_g_mix)
    add_small(db, b_in, m_b_in, v_b_in)
    add_small(drel, rel_bias, m_rel_bias, v_rel_bias)
    add_small(dg2, g_ffn2, m_g_ffn2, v_g_ffn2)
    add_small(dg_final[0], g_final, m_g_final, v_g_final)
    loss_piece = loss_blk[0, 0:1]
    small_packed = _pack_small(small_pieces + [loss_piece])

    recv = {}
    last = ("ffn1", 0)
    for l in reversed(range(DEPTH)):
        for grp in ("ffn2", "mix", "ffn1"):
            if (grp, l) != last:
                recv[(grp, l)] = exchange_wait(scatters[(grp, l)], dx, f"scattered_{grp}_l{l}")

    def upd(parts, w, m, v, tr, name, rb0=0):
        _, r, c = w.shape
        nr = r // tr

        def p_spec(layer):
            pinned = (nr - 1) if layer == 0 else 0
            return pl.BlockSpec((N_DEV, None, tr, c),
                                lambda l, i: (0, 0, rb0 + jnp.where(l == layer, i, pinned), 0))

        return adamw(parts, w, m, v, (DEPTH, nr), [p_spec(0), p_spec(1)],
                     pl.BlockSpec((None, tr, c), lambda l, i: (l, i, 0)), name)

    def both(grp, k):
        return [recv[(grp, l)][k] for l in range(DEPTH)]

    out_rows = D_FF // N_DEV // 2
    def upd_transposed(parts, w, m, v, tr, name):
        tp = lambda a: jnp.transpose(a, (0, 2, 1))
        return [tp(o) for o in upd(parts, tp(w), tp(m), tp(v), tr, name)]

    in_rows = FF_BLK // 4
    r_ffn2_in = upd_transposed(both("ffn2", 0), w_ffn2_in, m_w_ffn2_in, v_w_ffn2_in, in_rows, "adamw_ffn2_in")
    r_ffn2_out = upd(both("ffn2", 1), w_ffn2_out, m_w_ffn2_out, v_w_ffn2_out, out_rows, "adamw_ffn2_out")
    r_out = upd(both("mix", 3), w_out, m_w_out, v_w_out, LANES, "adamw_w_out")
    r_br_sb = upd(both("mix", 4), w_br_sb, m_w_br_sb, v_w_br_sb, 256, "adamw_br_sb", rb0=0)
    r_br_ch = upd(both("mix", 4), w_br_ch, m_w_br_ch, v_w_br_ch, 256, "adamw_br_ch", rb0=1)
    r_br_fox = upd(both("mix", 4), w_br_fox, m_w_br_fox, v_w_br_fox, 256, "adamw_br_fox", rb0=3)

    def w_in_grad(l):
        pieces = [recv[("mix", l)][k] for k in (0, 2, 1)]
        gq, gf, gg = sum_parts_multi(pieces, f"sum_w_in_l{l}")
        return jnp.concatenate([gq, gf[:, :N_HEADS_FOX], gg], axis=1)

    g_w_in = jnp.stack([w_in_grad(l) for l in range(DEPTH)])
    to_cols = lambda a: jnp.transpose(a, (2, 0, 1))
    n_cols = w_in.shape[2]
    col_blk = n_cols // 4
    win_spec = pl.BlockSpec((col_blk, DEPTH, LANES), lambda i: (i, 0, 0))
    r_in = adamw([to_cols(g_w_in)[None]], to_cols(w_in), to_cols(m_w_in), to_cols(v_w_in), (4,),
                 [pl.BlockSpec((1, col_blk, DEPTH, LANES), lambda i: (0, i, 0, 0))], win_spec, "adamw_w_in")
    r_in = [jnp.transpose(o, (1, 2, 0)) for o in r_in]

    recv[last] = exchange_wait(scatters[last], r_in[1], "scattered_ffn1_l0")
    r_ffn1_in = upd_transposed(both("ffn1", 0), w_ffn1_in, m_w_ffn1_in, v_w_ffn1_in, in_rows, "adamw_ffn1_in")
    r_ffn1_out = upd(both("ffn1", 1), w_ffn1_out, m_w_ffn1_out, v_w_ffn1_out, out_rows, "adamw_ffn1_out")

    small_sum = all_reduce_small(small_packed, "allreduce_small", deps=(r_ffn1_out[1],))
    n_small = sum(int(np.prod(s)) for s in small_shapes)
    loss = small_sum.reshape(-1)[n_small]
    sm_spec = pl.BlockSpec((SMALL_ROWS, LANES), lambda i: (0, 0))
    sm_out = adamw([small_sum[None]], _pack_small(small_w), _pack_small(small_m), _pack_small(small_v),
                   (1,), [pl.BlockSpec((1, SMALL_ROWS, LANES), lambda i: (0, 0, 0))], sm_spec, "adamw_small")
    sm_g, sm_d, sm_m, sm_v = [_unpack_small(o, small_shapes) for o in sm_out]

    def per_kind(k):
        small = (sm_g, sm_d, sm_m, sm_v)[k]
        return [small[0], r_ffn1_in[k], r_ffn1_out[k], small[1], r_in[k], small[2], small[3],
                r_br_sb[k], r_br_ch[k], r_br_fox[k], r_out[k], small[4], r_ffn2_in[k], r_ffn2_out[k],
                small[5]]

    return (loss, dx[None], *per_kind(0), *per_kind(1), *per_kind(2), *per_kind(3))
```
